```python
import math
import jax, jax.numpy as jnp
from jax import lax
import numpy as np

D_MODEL = 1024
BATCH = 8
SEQ = 2048
DEPTH = 4

CHUNK = 64
N_MIXERS = 3
ALPHA = (2.0 * DEPTH) ** 0.25
BETA = (8.0 * DEPTH) ** -0.25
LN_EPS = 1e-5
RMS_EPS = 1e-6
SC_WIDTH = 3
MLA_HEADS = 8
QK_NOPE = 128
QK_ROPE = 64
V_HEAD = 128
Q_LORA = 3 * D_MODEL // 8
KV_LORA = D_MODEL // 4
ROPE_THETA = 10000.0
Q_BLOCK = 128
CONF_WIDTH = 31
D_FF = 4 * D_MODEL
N_A = len(range(0, DEPTH, N_MIXERS))
N_B = len(range(1, DEPTH, N_MIXERS))
N_C = len(range(2, DEPTH, N_MIXERS))

kernel_name = "hybrid_chunk_causal_deepnorm_trunk"


def _layer_norm(x, g, b):
    xf = x.astype(jnp.float32)
    mu = jnp.mean(xf, axis=-1, keepdims=True)
    var = jnp.mean(jnp.square(xf - mu), axis=-1, keepdims=True)
    y = (xf - mu) * lax.rsqrt(var + LN_EPS) * g.astype(jnp.float32) + b.astype(jnp.float32)
    return y.astype(x.dtype)


def _rms_norm(x, g):
    xf = x.astype(jnp.float32)
    y = xf * lax.rsqrt(jnp.mean(jnp.square(xf), axis=-1, keepdims=True) + RMS_EPS) * g.astype(jnp.float32)
    return y.astype(x.dtype)


def _causal_dwconv(x, w):
    k_width, c = w.shape
    return lax.conv_general_dilated(
        x, w[:, None, :].astype(x.dtype), window_strides=(1,), padding=[(k_width - 1, 0)],
        dimension_numbers=("NWC", "WIO", "NWC"), feature_group_count=c)


def _rope(x, cos, sin):
    x1, x2 = jnp.split(x, 2, axis=-1)
    c = cos[None, :, None, :].astype(x.dtype)
    s = sin[None, :, None, :].astype(x.dtype)
    return jnp.concatenate([x1 * c - x2 * s, x1 * s + x2 * c], axis=-1)


def _short_conv_mixer(x, w_in, conv_w, w_out):
    b_gate, c_gate, h = jnp.split(x @ w_in, 3, axis=-1)
    return (b_gate * _causal_dwconv(c_gate * h, conv_w)) @ w_out


def _mla_mixer(x, w_dq, g_q, w_uq, w_dkv, g_kv, w_uk, w_uv, w_o):
    bsz, seq, _ = x.shape
    pos = jnp.arange(seq, dtype=jnp.float32)
    inv_freq = ROPE_THETA ** (-jnp.arange(0, QK_ROPE, 2, dtype=jnp.float32) / QK_ROPE)
    ang = pos[:, None] * inv_freq[None, :]
    cos, sin = jnp.cos(ang), jnp.sin(ang)
    cq = _rms_norm(x @ w_dq, g_q)
    q = (cq @ w_uq).reshape(bsz, seq, MLA_HEADS, QK_NOPE + QK_ROPE)
    q_nope, q_pe = q[..., :QK_NOPE], _rope(q[..., QK_NOPE:], cos, sin)
    ckv_full = x @ w_dkv
    ckv = _rms_norm(ckv_full[..., :KV_LORA], g_kv)
    k_pe = _rope(ckv_full[..., None, KV_LORA:], cos, sin)[:, :, 0, :]
    k_nope = jnp.einsum("bsc,chd->bshd", ckv, w_uk)
    v = jnp.einsum("bsc,chd->bshd", ckv, w_uv)
    scale = (QK_NOPE + QK_ROPE) ** -0.5
    n_blk = seq // Q_BLOCK
    qn_b = q_nope.reshape(bsz, n_blk, Q_BLOCK, MLA_HEADS, QK_NOPE).transpose(1, 0, 2, 3, 4)
    qp_b = q_pe.reshape(bsz, n_blk, Q_BLOCK, MLA_HEADS, QK_ROPE).transpose(1, 0, 2, 3, 4)
    k_chunk = jnp.arange(seq) // CHUNK

    def block(args):
        qn, qp, blk = args
        s = (jnp.einsum("bqhd,bkhd->bhqk", qn, k_nope)
             + jnp.einsum("bqhd,bkd->bhqk", qp, k_pe)).astype(jnp.float32) * scale
        q_chunk = (blk * Q_BLOCK + jnp.arange(Q_BLOCK)) // CHUNK
        allowed = k_chunk[None, :] <= q_chunk[:, None]
        p = jax.nn.softmax(jnp.where(allowed[None, None], s, -jnp.inf), axis=-1).astype(v.dtype)
        return jnp.einsum("bhqk,bkhd->bqhd", p, v)

    o = lax.map(block, (qn_b, qp_b, jnp.arange(n_blk)))
    o = o.transpose(1, 0, 2, 3, 4).reshape(bsz, seq, MLA_HEADS * V_HEAD)
    return o @ w_o


def _conformer_conv_mixer(x, w_pw1, b_pw1, dw_w, dw_b, norm_g, norm_b, w_pw2, b_pw2):
    a, gate = jnp.split(x @ w_pw1 + b_pw1, 2, axis=-1)
    h = a * jax.nn.sigmoid(gate)
    h = _causal_dwconv(h, dw_w) + dw_b
    h = jax.nn.silu(_layer_norm(h, norm_g, norm_b))
    return h @ w_pw2 + b_pw2


def _sq_relu_mlp(x, w1, w2):
    return jnp.square(jax.nn.relu(x @ w1)) @ w2


def _normal(k, shape, fan_in, scale=1.0):
    return jax.random.normal(k, shape, jnp.float32) * (scale * fan_in ** -0.5)


def _fwd_setup_inputs(seed: int = 0) -> dict:
    key = jax.random.key(seed)
    ks = iter(jax.random.split(key, 40))
    D = D_MODEL
    gain = lambda shape: 1.0 + 0.02 * jax.random.normal(next(ks), shape, jnp.float32)
    bias = lambda shape: 0.02 * jax.random.normal(next(ks), shape, jnp.float32)
    return {
        "x": jax.random.normal(next(ks), (BATCH, SEQ, D), jnp.float32),
        "sc_w_in": _normal(next(ks), (N_A, D, 3 * D), D),
        "sc_conv_w": _normal(next(ks), (N_A, SC_WIDTH, D), SC_WIDTH),
        "sc_w_out": _normal(next(ks), (N_A, D, D), D, BETA),
        "mla_w_dq": _normal(next(ks), (N_B, D, Q_LORA), D),
        "mla_g_q": gain((N_B, Q_LORA)),
        "mla_w_uq": _normal(next(ks), (N_B, Q_LORA, MLA_HEADS * (QK_NOPE + QK_ROPE)), Q_LORA),
        "mla_w_dkv": _normal(next(ks), (N_B, D, KV_LORA + QK_ROPE), D),
        "mla_g_kv": gain((N_B, KV_LORA)),
        "mla_w_uk": _normal(next(ks), (N_B, KV_LORA, MLA_HEADS, QK_NOPE), KV_LORA),
        "mla_w_uv": _normal(next(ks), (N_B, KV_LORA, MLA_HEADS, V_HEAD), KV_LORA, BETA),
        "mla_w_o": _normal(next(ks), (N_B, MLA_HEADS * V_HEAD, D), MLA_HEADS * V_HEAD, BETA),
        "cf_w_pw1": _normal(next(ks), (N_C, D, 2 * D), D),
        "cf_b_pw1": bias((N_C, 2 * D)),
        "cf_dw_w": _normal(next(ks), (N_C, CONF_WIDTH, D), CONF_WIDTH),
        "cf_dw_b": bias((N_C, D)),
        "cf_norm_g": gain((N_C, D)),
        "cf_norm_b": bias((N_C, D)),
        "cf_w_pw2": _normal(next(ks), (N_C, D, D), D, BETA),
        "cf_b_pw2": bias((N_C, D)),
        "ff_w1": _normal(next(ks), (DEPTH, D, D_FF), D, BETA),
        "ff_w2": _normal(next(ks), (DEPTH, D_FF, D), D_FF, BETA),
        "ln_mix_g": gain((DEPTH, D)),
        "ln_mix_b": bias((DEPTH, D)),
        "ln_ff_g": gain((DEPTH, D)),
        "ln_ff_b": bias((DEPTH, D)),
    }


def _fwd_reference(x, sc_w_in, sc_conv_w, sc_w_out,
              mla_w_dq, mla_g_q, mla_w_uq, mla_w_dkv, mla_g_kv, mla_w_uk, mla_w_uv, mla_w_o,
              cf_w_pw1, cf_b_pw1, cf_dw_w, cf_dw_b, cf_norm_g, cf_norm_b, cf_w_pw2, cf_b_pw2,
              ff_w1, ff_w2, ln_mix_g, ln_mix_b, ln_ff_g, ln_ff_b):
    for i in range(DEPTH):
        m, j = i % N_MIXERS, i // N_MIXERS
        if m == 0:
            y = _short_conv_mixer(x, sc_w_in[j], sc_conv_w[j], sc_w_out[j])
        elif m == 1:
            y = _mla_mixer(x, mla_w_dq[j], mla_g_q[j], mla_w_uq[j], mla_w_dkv[j], mla_g_kv[j],
                           mla_w_uk[j], mla_w_uv[j], mla_w_o[j])
        else:
            y = _conformer_conv_mixer(x, cf_w_pw1[j], cf_b_pw1[j], cf_dw_w[j], cf_dw_b[j],
                                      cf_norm_g[j], cf_norm_b[j], cf_w_pw2[j], cf_b_pw2[j])
        x = _layer_norm(ALPHA * x + y, ln_mix_g[i], ln_mix_b[i])
        x = _layer_norm(ALPHA * x + _sq_relu_mlp(x, ff_w1[i], ff_w2[i]), ln_ff_g[i], ln_ff_b[i])
    return x


import jax as _jax
import jax.numpy as _jnp

TWIN_FORMAT = 'train_step'
FWD_PARAMS = ['x', 'sc_w_in', 'sc_conv_w', 'sc_w_out', 'mla_w_dq', 'mla_g_q', 'mla_w_uq', 'mla_w_dkv', 'mla_g_kv', 'mla_w_uk', 'mla_w_uv', 'mla_w_o', 'cf_w_pw1', 'cf_b_pw1', 'cf_dw_w', 'cf_dw_b', 'cf_norm_g', 'cf_norm_b', 'cf_w_pw2', 'cf_b_pw2', 'ff_w1', 'ff_w2', 'ln_mix_g', 'ln_mix_b', 'ln_ff_g', 'ln_ff_b']
TWIN_WEIGHTS = ['sc_w_in', 'sc_conv_w', 'sc_w_out', 'mla_w_dq', 'mla_g_q', 'mla_w_uq', 'mla_w_dkv', 'mla_g_kv', 'mla_w_uk', 'mla_w_uv', 'mla_w_o', 'cf_w_pw1', 'cf_b_pw1', 'cf_dw_w', 'cf_dw_b', 'cf_norm_g', 'cf_norm_b', 'cf_w_pw2', 'cf_b_pw2', 'ff_w1', 'ff_w2', 'ln_mix_g', 'ln_mix_b', 'ln_ff_g', 'ln_ff_b']
TWIN_DIFF_INPUT = 'x'
TWIN_INPUTS = ['x', 'sc_w_in', 'sc_conv_w', 'sc_w_out', 'mla_w_dq', 'mla_g_q', 'mla_w_uq', 'mla_w_dkv', 'mla_g_kv', 'mla_w_uk', 'mla_w_uv', 'mla_w_o', 'cf_w_pw1', 'cf_b_pw1', 'cf_dw_w', 'cf_dw_b', 'cf_norm_g', 'cf_norm_b', 'cf_w_pw2', 'cf_b_pw2', 'ff_w1', 'ff_w2', 'ln_mix_g', 'ln_mix_b', 'ln_ff_g', 'ln_ff_b', 'loss_target', 'm_sc_w_in', 'm_sc_conv_w', 'm_sc_w_out', 'm_mla_w_dq', 'm_mla_g_q', 'm_mla_w_uq', 'm_mla_w_dkv', 'm_mla_g_kv', 'm_mla_w_uk', 'm_mla_w_uv', 'm_mla_w_o', 'm_cf_w_pw1', 'm_cf_b_pw1', 'm_cf_dw_w', 'm_cf_dw_b', 'm_cf_norm_g', 'm_cf_norm_b', 'm_cf_w_pw2', 'm_cf_b_pw2', 'm_ff_w1', 'm_ff_w2', 'm_ln_mix_g', 'm_ln_mix_b', 'm_ln_ff_g', 'm_ln_ff_b', 'v_sc_w_in', 'v_sc_conv_w', 'v_sc_w_out', 'v_mla_w_dq', 'v_mla_g_q', 'v_mla_w_uq', 'v_mla_w_dkv', 'v_mla_g_kv', 'v_mla_w_uk', 'v_mla_w_uv', 'v_mla_w_o', 'v_cf_w_pw1', 'v_cf_b_pw1', 'v_cf_dw_w', 'v_cf_dw_b', 'v_cf_norm_g', 'v_cf_norm_b', 'v_cf_w_pw2', 'v_cf_b_pw2', 'v_ff_w1', 'v_ff_w2', 'v_ln_mix_g', 'v_ln_mix_b', 'v_ln_ff_g', 'v_ln_ff_b']
TWIN_OUTPUTS = ['loss', 'grad_x', 'grad_sc_w_in', 'grad_sc_conv_w', 'grad_sc_w_out', 'grad_mla_w_dq', 'grad_mla_g_q', 'grad_mla_w_uq', 'grad_mla_w_dkv', 'grad_mla_g_kv', 'grad_mla_w_uk', 'grad_mla_w_uv', 'grad_mla_w_o', 'grad_cf_w_pw1', 'grad_cf_b_pw1', 'grad_cf_dw_w', 'grad_cf_dw_b', 'grad_cf_norm_g', 'grad_cf_norm_b', 'grad_cf_w_pw2', 'grad_cf_b_pw2', 'grad_ff_w1', 'grad_ff_w2', 'grad_ln_mix_g', 'grad_ln_mix_b', 'grad_ln_ff_g', 'grad_ln_ff_b', 'delta_sc_w_in', 'delta_sc_conv_w', 'delta_sc_w_out', 'delta_mla_w_dq', 'delta_mla_g_q', 'delta_mla_w_uq', 'delta_mla_w_dkv', 'delta_mla_g_kv', 'delta_mla_w_uk', 'delta_mla_w_uv', 'delta_mla_w_o', 'delta_cf_w_pw1', 'delta_cf_b_pw1', 'delta_cf_dw_w', 'delta_cf_dw_b', 'delta_cf_norm_g', 'delta_cf_norm_b', 'delta_cf_w_pw2', 'delta_cf_b_pw2', 'delta_ff_w1', 'delta_ff_w2', 'delta_ln_mix_g', 'delta_ln_mix_b', 'delta_ln_ff_g', 'delta_ln_ff_b', 'new_m_sc_w_in', 'new_m_sc_conv_w', 'new_m_sc_w_out', 'new_m_mla_w_dq', 'new_m_mla_g_q', 'new_m_mla_w_uq', 'new_m_mla_w_dkv', 'new_m_mla_g_kv', 'new_m_mla_w_uk', 'new_m_mla_w_uv', 'new_m_mla_w_o', 'new_m_cf_w_pw1', 'new_m_cf_b_pw1', 'new_m_cf_dw_w', 'new_m_cf_dw_b', 'new_m_cf_norm_g', 'new_m_cf_norm_b', 'new_m_cf_w_pw2', 'new_m_cf_b_pw2', 'new_m_ff_w1', 'new_m_ff_w2', 'new_m_ln_mix_g', 'new_m_ln_mix_b', 'new_m_ln_ff_g', 'new_m_ln_ff_b', 'new_v_sc_w_in', 'new_v_sc_conv_w', 'new_v_sc_w_out', 'new_v_mla_w_dq', 'new_v_mla_g_q', 'new_v_mla_w_uq', 'new_v_mla_w_dkv', 'new_v_mla_g_kv', 'new_v_mla_w_uk', 'new_v_mla_w_uv', 'new_v_mla_w_o', 'new_v_cf_w_pw1', 'new_v_cf_b_pw1', 'new_v_cf_dw_w', 'new_v_cf_dw_b', 'new_v_cf_norm_g', 'new_v_cf_norm_b', 'new_v_cf_w_pw2', 'new_v_cf_b_pw2', 'new_v_ff_w1', 'new_v_ff_w2', 'new_v_ln_mix_g', 'new_v_ln_mix_b', 'new_v_ln_ff_g', 'new_v_ln_ff_b']
TWIN_LEAF_KINDS = {'loss': 'loss', 'grad_x': 'grad_x', 'grad_sc_w_in': 'grad_w', 'grad_sc_conv_w': 'grad_w', 'grad_sc_w_out': 'grad_w', 'grad_mla_w_dq': 'grad_w', 'grad_mla_g_q': 'grad_w', 'grad_mla_w_uq': 'grad_w', 'grad_mla_w_dkv': 'grad_w', 'grad_mla_g_kv': 'grad_w', 'grad_mla_w_uk': 'grad_w', 'grad_mla_w_uv': 'grad_w', 'grad_mla_w_o': 'grad_w', 'grad_cf_w_pw1': 'grad_w', 'grad_cf_b_pw1': 'grad_w', 'grad_cf_dw_w': 'grad_w', 'grad_cf_dw_b': 'grad_w', 'grad_cf_norm_g': 'grad_w', 'grad_cf_norm_b': 'grad_w', 'grad_cf_w_pw2': 'grad_w', 'grad_cf_b_pw2': 'grad_w', 'grad_ff_w1': 'grad_w', 'grad_ff_w2': 'grad_w', 'grad_ln_mix_g': 'grad_w', 'grad_ln_mix_b': 'grad_w', 'grad_ln_ff_g': 'grad_w', 'grad_ln_ff_b': 'grad_w', 'delta_sc_w_in': 'delta_w', 'delta_sc_conv_w': 'delta_w', 'delta_sc_w_out': 'delta_w', 'delta_mla_w_dq': 'delta_w', 'delta_mla_g_q': 'delta_w', 'delta_mla_w_uq': 'delta_w', 'delta_mla_w_dkv': 'delta_w', 'delta_mla_g_kv': 'delta_w', 'delta_mla_w_uk': 'delta_w', 'delta_mla_w_uv': 'delta_w', 'delta_mla_w_o': 'delta_w', 'delta_cf_w_pw1': 'delta_w', 'delta_cf_b_pw1': 'delta_w', 'delta_cf_dw_w': 'delta_w', 'delta_cf_dw_b': 'delta_w', 'delta_cf_norm_g': 'delta_w', 'delta_cf_norm_b': 'delta_w', 'delta_cf_w_pw2': 'delta_w', 'delta_cf_b_pw2': 'delta_w', 'delta_ff_w1': 'delta_w', 'delta_ff_w2': 'delta_w', 'delta_ln_mix_g': 'delta_w', 'delta_ln_mix_b': 'delta_w', 'delta_ln_ff_g': 'delta_w', 'delta_ln_ff_b': 'delta_w', 'new_m_sc_w_in': 'new_m', 'new_m_sc_conv_w': 'new_m', 'new_m_sc_w_out': 'new_m', 'new_m_mla_w_dq': 'new_m', 'new_m_mla_g_q': 'new_m', 'new_m_mla_w_uq': 'new_m', 'new_m_mla_w_dkv': 'new_m', 'new_m_mla_g_kv': 'new_m', 'new_m_mla_w_uk': 'new_m', 'new_m_mla_w_uv': 'new_m', 'new_m_mla_w_o': 'new_m', 'new_m_cf_w_pw1': 'new_m', 'new_m_cf_b_pw1': 'new_m', 'new_m_cf_dw_w': 'new_m', 'new_m_cf_dw_b': 'new_m', 'new_m_cf_norm_g': 'new_m', 'new_m_cf_norm_b': 'new_m', 'new_m_cf_w_pw2': 'new_m', 'new_m_cf_b_pw2': 'new_m', 'new_m_ff_w1': 'new_m', 'new_m_ff_w2': 'new_m', 'new_m_ln_mix_g': 'new_m', 'new_m_ln_mix_b': 'new_m', 'new_m_ln_ff_g': 'new_m', 'new_m_ln_ff_b': 'new_m', 'new_v_sc_w_in': 'new_v', 'new_v_sc_conv_w': 'new_v', 'new_v_sc_w_out': 'new_v', 'new_v_mla_w_dq': 'new_v', 'new_v_mla_g_q': 'new_v', 'new_v_mla_w_uq': 'new_v', 'new_v_mla_w_dkv': 'new_v', 'new_v_mla_g_kv': 'new_v', 'new_v_mla_w_uk': 'new_v', 'new_v_mla_w_uv': 'new_v', 'new_v_mla_w_o': 'new_v', 'new_v_cf_w_pw1': 'new_v', 'new_v_cf_b_pw1': 'new_v', 'new_v_cf_dw_w': 'new_v', 'new_v_cf_dw_b': 'new_v', 'new_v_cf_norm_g': 'new_v', 'new_v_cf_norm_b': 'new_v', 'new_v_cf_w_pw2': 'new_v', 'new_v_cf_b_pw2': 'new_v', 'new_v_ff_w1': 'new_v', 'new_v_ff_w2': 'new_v', 'new_v_ln_mix_g': 'new_v', 'new_v_ln_mix_b': 'new_v', 'new_v_ln_ff_g': 'new_v', 'new_v_ln_ff_b': 'new_v'}


def _forward(args):
    return _fwd_reference(*[args[k] for k in FWD_PARAMS])


def _output_shape():
    out = _jax.eval_shape(lambda: _forward(_fwd_setup_inputs(0)))
    return out.shape, out.dtype

N_MICROBATCH = 1
ADAM_LR = 0.001
ADAM_B1 = 0.9
ADAM_B2 = 0.999
ADAM_EPS = 1e-08
ADAM_WD = 0.01
ADAM_STEP = 10
PER_EXAMPLE_BATCH_AXIS = {'x': 0, 'loss_target': 0}
SHARED_INPUTS = []
_WEIGHT_DTYPES = {'sc_w_in': _jnp.float32, 'sc_conv_w': _jnp.float32, 'sc_w_out': _jnp.float32, 'mla_w_dq': _jnp.float32, 'mla_g_q': _jnp.float32, 'mla_w_uq': _jnp.float32, 'mla_w_dkv': _jnp.float32, 'mla_g_kv': _jnp.float32, 'mla_w_uk': _jnp.float32, 'mla_w_uv': _jnp.float32, 'mla_w_o': _jnp.float32, 'cf_w_pw1': _jnp.float32, 'cf_b_pw1': _jnp.float32, 'cf_dw_w': _jnp.float32, 'cf_dw_b': _jnp.float32, 'cf_norm_g': _jnp.float32, 'cf_norm_b': _jnp.float32, 'cf_w_pw2': _jnp.float32, 'cf_b_pw2': _jnp.float32, 'ff_w1': _jnp.float32, 'ff_w2': _jnp.float32, 'ln_mix_g': _jnp.float32, 'ln_mix_b': _jnp.float32, 'ln_ff_g': _jnp.float32, 'ln_ff_b': _jnp.float32}
MOMENT_SCALE = {'sc_w_in': 3.159703e-02, 'sc_conv_w': 3.123266e-02, 'sc_w_out': 7.531747e-02, 'mla_w_dq': 3.359845e-03, 'mla_g_q': 3.206365e-03, 'mla_w_uq': 1.680839e-03, 'mla_w_dkv': 5.228956e-03, 'mla_g_kv': 6.253675e-03, 'mla_w_uk': 1.732986e-03, 'mla_w_uv': 5.499622e-03, 'mla_w_o': 5.478780e-03, 'cf_w_pw1': 1.545410e-02, 'cf_b_pw1': 2.366507e-02, 'cf_dw_w': 2.044323e-02, 'cf_dw_b': 5.705410e-02, 'cf_norm_g': 2.885778e-02, 'cf_norm_b': 3.365180e-02, 'cf_w_pw2': 5.418534e-02, 'cf_b_pw2': 1.649252e-01, 'ff_w1': 9.744360e-03, 'ff_w2': 2.064133e-02, 'ln_mix_g': 5.723602e-01, 'ln_mix_b': 2.816116e-01, 'ln_ff_g': 8.049765e+00, 'ln_ff_b': 6.788066e-01}


def _to_microbatches(a, axis):
    t = _jnp.moveaxis(a, axis, 0)
    t = t.reshape((N_MICROBATCH, t.shape[0] // N_MICROBATCH) + t.shape[1:])
    return _jnp.moveaxis(t, 1, axis + 1)


def setup_inputs(seed: int = 0) -> dict:
    inp = _fwd_setup_inputs(seed)
    key = _jax.random.fold_in(_jax.random.key(seed), 7919)
    shape, _ = _output_shape()
    out = dict(inp)
    out["loss_target"] = _jax.random.normal(_jax.random.fold_in(key, 0), shape, _jnp.float32)
    for i, name in enumerate(TWIN_WEIGHTS):
        w = inp[name].astype(_jnp.float32)
        if MOMENT_SCALE is None:
            s = _jnp.sqrt(_jnp.mean(_jnp.square(w)) + 1e-30)
        else:
            s = MOMENT_SCALE[name]
        km, kv = _jax.random.split(_jax.random.fold_in(key, i + 1))
        out[name] = w
        out["m_" + name] = s * _jax.random.normal(km, w.shape, _jnp.float32)
        out["v_" + name] = (s * s) * _jax.random.uniform(kv, w.shape, _jnp.float32, 0.5, 1.5)
    if N_MICROBATCH > 1:
        for name, axis in PER_EXAMPLE_BATCH_AXIS.items():
            out[name] = _to_microbatches(out[name], axis)
    return {'x': out['x'], 'sc_w_in': out['sc_w_in'], 'sc_conv_w': out['sc_conv_w'], 'sc_w_out': out['sc_w_out'], 'mla_w_dq': out['mla_w_dq'], 'mla_g_q': out['mla_g_q'], 'mla_w_uq': out['mla_w_uq'], 'mla_w_dkv': out['mla_w_dkv'], 'mla_g_kv': out['mla_g_kv'], 'mla_w_uk': out['mla_w_uk'], 'mla_w_uv': out['mla_w_uv'], 'mla_w_o': out['mla_w_o'], 'cf_w_pw1': out['cf_w_pw1'], 'cf_b_pw1': out['cf_b_pw1'], 'cf_dw_w': out['cf_dw_w'], 'cf_dw_b': out['cf_dw_b'], 'cf_norm_g': out['cf_norm_g'], 'cf_norm_b': out['cf_norm_b'], 'cf_w_pw2': out['cf_w_pw2'], 'cf_b_pw2': out['cf_b_pw2'], 'ff_w1': out['ff_w1'], 'ff_w2': out['ff_w2'], 'ln_mix_g': out['ln_mix_g'], 'ln_mix_b': out['ln_mix_b'], 'ln_ff_g': out['ln_ff_g'], 'ln_ff_b': out['ln_ff_b'], 'loss_target': out['loss_target'], 'm_sc_w_in': out['m_sc_w_in'], 'm_sc_conv_w': out['m_sc_conv_w'], 'm_sc_w_out': out['m_sc_w_out'], 'm_mla_w_dq': out['m_mla_w_dq'], 'm_mla_g_q': out['m_mla_g_q'], 'm_mla_w_uq': out['m_mla_w_uq'], 'm_mla_w_dkv': out['m_mla_w_dkv'], 'm_mla_g_kv': out['m_mla_g_kv'], 'm_mla_w_uk': out['m_mla_w_uk'], 'm_mla_w_uv': out['m_mla_w_uv'], 'm_mla_w_o': out['m_mla_w_o'], 'm_cf_w_pw1': out['m_cf_w_pw1'], 'm_cf_b_pw1': out['m_cf_b_pw1'], 'm_cf_dw_w': out['m_cf_dw_w'], 'm_cf_dw_b': out['m_cf_dw_b'], 'm_cf_norm_g': out['m_cf_norm_g'], 'm_cf_norm_b': out['m_cf_norm_b'], 'm_cf_w_pw2': out['m_cf_w_pw2'], 'm_cf_b_pw2': out['m_cf_b_pw2'], 'm_ff_w1': out['m_ff_w1'], 'm_ff_w2': out['m_ff_w2'], 'm_ln_mix_g': out['m_ln_mix_g'], 'm_ln_mix_b': out['m_ln_mix_b'], 'm_ln_ff_g': out['m_ln_ff_g'], 'm_ln_ff_b': out['m_ln_ff_b'], 'v_sc_w_in': out['v_sc_w_in'], 'v_sc_conv_w': out['v_sc_conv_w'], 'v_sc_w_out': out['v_sc_w_out'], 'v_mla_w_dq': out['v_mla_w_dq'], 'v_mla_g_q': out['v_mla_g_q'], 'v_mla_w_uq': out['v_mla_w_uq'], 'v_mla_w_dkv': out['v_mla_w_dkv'], 'v_mla_g_kv': out['v_mla_g_kv'], 'v_mla_w_uk': out['v_mla_w_uk'], 'v_mla_w_uv': out['v_mla_w_uv'], 'v_mla_w_o': out['v_mla_w_o'], 'v_cf_w_pw1': out['v_cf_w_pw1'], 'v_cf_b_pw1': out['v_cf_b_pw1'], 'v_cf_dw_w': out['v_cf_dw_w'], 'v_cf_dw_b': out['v_cf_dw_b'], 'v_cf_norm_g': out['v_cf_norm_g'], 'v_cf_norm_b': out['v_cf_norm_b'], 'v_cf_w_pw2': out['v_cf_w_pw2'], 'v_cf_b_pw2': out['v_cf_b_pw2'], 'v_ff_w1': out['v_ff_w1'], 'v_ff_w2': out['v_ff_w2'], 'v_ln_mix_g': out['v_ln_mix_g'], 'v_ln_mix_b': out['v_ln_mix_b'], 'v_ln_ff_g': out['v_ln_ff_g'], 'v_ln_ff_b': out['v_ln_ff_b']}


def _loss(weights, diff, rest, loss_target):
    with _jax.named_scope("forward"):
        args = {**rest, TWIN_DIFF_INPUT: diff, **{k: w.astype(_WEIGHT_DTYPES[k]) for k, w in weights.items()}}
        y = _forward(args)
    with _jax.named_scope("loss_head"):
        err = _jnp.square(y.astype(_jnp.float32) - loss_target)
        return 0.5 * _jnp.sum(_jnp.mean(err, axis=-1)) if err.ndim else 0.5 * err


def _adamw(w, g, m, v):
    m = ADAM_B1 * m + (1.0 - ADAM_B1) * g
    v = ADAM_B2 * v + (1.0 - ADAM_B2) * _jnp.square(g)
    m_hat = m / (1.0 - ADAM_B1 ** ADAM_STEP)
    v_hat = v / (1.0 - ADAM_B2 ** ADAM_STEP)
    delta = -ADAM_LR * (m_hat / (_jnp.sqrt(v_hat) + ADAM_EPS) + ADAM_WD * w)
    return delta, m, v


def reference(x, sc_w_in, sc_conv_w, sc_w_out, mla_w_dq, mla_g_q, mla_w_uq, mla_w_dkv, mla_g_kv, mla_w_uk, mla_w_uv, mla_w_o, cf_w_pw1, cf_b_pw1, cf_dw_w, cf_dw_b, cf_norm_g, cf_norm_b, cf_w_pw2, cf_b_pw2, ff_w1, ff_w2, ln_mix_g, ln_mix_b, ln_ff_g, ln_ff_b, loss_target, m_sc_w_in, m_sc_conv_w, m_sc_w_out, m_mla_w_dq, m_mla_g_q, m_mla_w_uq, m_mla_w_dkv, m_mla_g_kv, m_mla_w_uk, m_mla_w_uv, m_mla_w_o, m_cf_w_pw1, m_cf_b_pw1, m_cf_dw_w, m_cf_dw_b, m_cf_norm_g, m_cf_norm_b, m_cf_w_pw2, m_cf_b_pw2, m_ff_w1, m_ff_w2, m_ln_mix_g, m_ln_mix_b, m_ln_ff_g, m_ln_ff_b, v_sc_w_in, v_sc_conv_w, v_sc_w_out, v_mla_w_dq, v_mla_g_q, v_mla_w_uq, v_mla_w_dkv, v_mla_g_kv, v_mla_w_uk, v_mla_w_uv, v_mla_w_o, v_cf_w_pw1, v_cf_b_pw1, v_cf_dw_w, v_cf_dw_b, v_cf_norm_g, v_cf_norm_b, v_cf_w_pw2, v_cf_b_pw2, v_ff_w1, v_ff_w2, v_ln_mix_g, v_ln_mix_b, v_ln_ff_g, v_ln_ff_b):
    given = dict(x=x, sc_w_in=sc_w_in, sc_conv_w=sc_conv_w, sc_w_out=sc_w_out, mla_w_dq=mla_w_dq, mla_g_q=mla_g_q, mla_w_uq=mla_w_uq, mla_w_dkv=mla_w_dkv, mla_g_kv=mla_g_kv, mla_w_uk=mla_w_uk, mla_w_uv=mla_w_uv, mla_w_o=mla_w_o, cf_w_pw1=cf_w_pw1, cf_b_pw1=cf_b_pw1, cf_dw_w=cf_dw_w, cf_dw_b=cf_dw_b, cf_norm_g=cf_norm_g, cf_norm_b=cf_norm_b, cf_w_pw2=cf_w_pw2, cf_b_pw2=cf_b_pw2, ff_w1=ff_w1, ff_w2=ff_w2, ln_mix_g=ln_mix_g, ln_mix_b=ln_mix_b, ln_ff_g=ln_ff_g, ln_ff_b=ln_ff_b, loss_target=loss_target, m_sc_w_in=m_sc_w_in, m_sc_conv_w=m_sc_conv_w, m_sc_w_out=m_sc_w_out, m_mla_w_dq=m_mla_w_dq, m_mla_g_q=m_mla_g_q, m_mla_w_uq=m_mla_w_uq, m_mla_w_dkv=m_mla_w_dkv, m_mla_g_kv=m_mla_g_kv, m_mla_w_uk=m_mla_w_uk, m_mla_w_uv=m_mla_w_uv, m_mla_w_o=m_mla_w_o, m_cf_w_pw1=m_cf_w_pw1, m_cf_b_pw1=m_cf_b_pw1, m_cf_dw_w=m_cf_dw_w, m_cf_dw_b=m_cf_dw_b, m_cf_norm_g=m_cf_norm_g, m_cf_norm_b=m_cf_norm_b, m_cf_w_pw2=m_cf_w_pw2, m_cf_b_pw2=m_cf_b_pw2, m_ff_w1=m_ff_w1, m_ff_w2=m_ff_w2, m_ln_mix_g=m_ln_mix_g, m_ln_mix_b=m_ln_mix_b, m_ln_ff_g=m_ln_ff_g, m_ln_ff_b=m_ln_ff_b, v_sc_w_in=v_sc_w_in, v_sc_conv_w=v_sc_conv_w, v_sc_w_out=v_sc_w_out, v_mla_w_dq=v_mla_w_dq, v_mla_g_q=v_mla_g_q, v_mla_w_uq=v_mla_w_uq, v_mla_w_dkv=v_mla_w_dkv, v_mla_g_kv=v_mla_g_kv, v_mla_w_uk=v_mla_w_uk, v_mla_w_uv=v_mla_w_uv, v_mla_w_o=v_mla_w_o, v_cf_w_pw1=v_cf_w_pw1, v_cf_b_pw1=v_cf_b_pw1, v_cf_dw_w=v_cf_dw_w, v_cf_dw_b=v_cf_dw_b, v_cf_norm_g=v_cf_norm_g, v_cf_norm_b=v_cf_norm_b, v_cf_w_pw2=v_cf_w_pw2, v_cf_b_pw2=v_cf_b_pw2, v_ff_w1=v_ff_w1, v_ff_w2=v_ff_w2, v_ln_mix_g=v_ln_mix_g, v_ln_mix_b=v_ln_mix_b, v_ln_ff_g=v_ln_ff_g, v_ln_ff_b=v_ln_ff_b)
    weights = {n: given[n] for n in TWIN_WEIGHTS}
    shared = {n: given[n] for n in SHARED_INPUTS}
    per_example = {n: given[n] for n in ['x']}
    grad_fn = _jax.value_and_grad(_loss, argnums=(0, 1))

    def one_microbatch(ex, loss_target):
        ex = dict(ex)
        diff = ex.pop(TWIN_DIFF_INPUT)
        return grad_fn(weights, diff, {**shared, **ex}, loss_target)

    if N_MICROBATCH == 1:
        loss, (grad_w, grad_x) = one_microbatch(per_example, given["loss_target"])
    else:
        def body(carry, xs):
            loss_sum, grad_sum = carry
            l_k, (gw_k, gx_k) = one_microbatch(xs[0], xs[1])
            with _jax.named_scope("update"):
                return (loss_sum + l_k, _jax.tree.map(_jnp.add, grad_sum, gw_k)), gx_k

        init = (_jnp.zeros((), _jnp.float32), _jax.tree.map(_jnp.zeros_like, weights))
        (loss, grad_w), grad_x = _jax.lax.scan(body, init, (per_example, given["loss_target"]))
    with _jax.named_scope("update"):
        delta_w, new_m, new_v = {}, {}, {}
        for n in TWIN_WEIGHTS:
            delta_w[n], new_m[n], new_v[n] = _adamw(weights[n], grad_w[n], given["m_" + n], given["v_" + n])
    return (loss, grad_x, *[grad_w[n] for n in TWIN_WEIGHTS], *[delta_w[n] for n in TWIN_WEIGHTS],
            *[new_m[n] for n in TWIN_WEIGHTS], *[new_v[n] for n in TWIN_WEIGHTS])
```

```python
import functools

import jax
import jax.numpy as jnp
from jax import lax
from jax.experimental import pallas as pl
from jax.experimental.pallas import tpu as pltpu

F32 = jnp.float32
MXU_DTYPE = jnp.bfloat16

N_DEV = 8
D_MODEL = 1024
DEPTH = 4
N_MIXERS = 3
CHUNK = 64
ALPHA = (2.0 * DEPTH) ** 0.25
LN_EPS = 1e-5
RMS_EPS = 1e-6
MLA_HEADS = 8
QK_NOPE = 128
QK_ROPE = 64
V_HEAD = 128
Q_LORA = 384
KV_LORA = 256
ROPE_THETA = 10000.0
ADAM_LR = 0.001
ADAM_B1 = 0.9
ADAM_B2 = 0.999
ADAM_EPS = 1e-08
ADAM_WD = 0.01
ADAM_STEP = 10

LANES = 128
SUBLANES = 8
VMEM_LIMIT = 56 * 1024 * 1024
CONV_PAD = 32
MASK_VALUE = -1e30

MESH_AXES = ("x", "y", "c")


def _params(semantics):
    return pltpu.CompilerParams(dimension_semantics=semantics, vmem_limit_bytes=VMEM_LIMIT)


def _tile(n, cap):
    if n <= cap:
        return n
    t = cap - cap % LANES
    while t >= LANES:
        if n % t == 0:
            return t
        t -= LANES
    raise ValueError(f"no tile for {n} under {cap}")


def _row_tile(rows, cap):
    if rows <= cap:
        return rows
    t = cap - cap % SUBLANES
    while t >= SUBLANES:
        if rows % t == 0:
            return t
        t -= SUBLANES
    raise ValueError(f"no row tile for {rows} under {cap}")


def _mm(a, b, *, name, ta=False, tb=False, out_dtypes=(F32,), epilogue=None, tile_extras=(), col_extras=(),
        tm_cap=1024, tn_cap=1024, tk_cap=1024):
    m, k = (a.shape[1], a.shape[0]) if ta else a.shape
    n, kb = (b.shape[0], b.shape[1]) if tb else (b.shape[1], b.shape[0])
    assert k == kb, (a.shape, b.shape, ta, tb)
    tm, tn, tk = _tile(m, tm_cap), _tile(n, tn_cap), _tile(k, tk_cap)
    nk = k // tk
    n_te, n_ce, n_out = len(tile_extras), len(col_extras), len(out_dtypes)
    dims = (((0 if ta else 1,), (1 if tb else 0,)), ((), ()))

    def body(*refs):
        a_ref, b_ref = refs[0], refs[1]
        te_refs = refs[2:2 + n_te]
        ce_refs = refs[2 + n_te:2 + n_te + n_ce]
        out_refs = refs[2 + n_te + n_ce:2 + n_te + n_ce + n_out]

        def finish(acc):
            if epilogue is None:
                outs = (acc,)
            else:
                outs = epilogue(acc, *[r[...] for r in te_refs], *[r[...] for r in ce_refs])
            for o_ref, o in zip(out_refs, outs):
                o_ref[...] = o.astype(o_ref.dtype)

        part = lax.dot_general(a_ref[...].astype(MXU_DTYPE), b_ref[...].astype(MXU_DTYPE), dims,
                               preferred_element_type=F32)
        if nk == 1:
            finish(part)
        else:
            acc_ref = refs[-1]
            kk = pl.program_id(2)

            @pl.when(kk == 0)
            def _():
                acc_ref[...] = part

            @pl.when(jnp.logical_and(kk > 0, kk < nk - 1))
            def _():
                acc_ref[...] += part

            @pl.when(kk == nk - 1)
            def _():
                finish(acc_ref[...] + part)

    a_spec = pl.BlockSpec((tk, tm), lambda i, j, kk: (kk, i)) if ta else pl.BlockSpec((tm, tk), lambda i, j, kk: (i, kk))
    b_spec = pl.BlockSpec((tn, tk), lambda i, j, kk: (j, kk)) if tb else pl.BlockSpec((tk, tn), lambda i, j, kk: (kk, j))
    tile_spec = pl.BlockSpec((tm, tn), lambda i, j, kk: (i, j))
    col_spec = pl.BlockSpec((1, tn), lambda i, j, kk: (0, j))
    outs = pl.pallas_call(
        body,
        name=name,
        grid=(m // tm, n // tn, nk),
        in_specs=[a_spec, b_spec] + [tile_spec] * n_te + [col_spec] * n_ce,
        out_specs=[tile_spec] * n_out,
        out_shape=[jax.ShapeDtypeStruct((m, n), dt) for dt in out_dtypes],
        scratch_shapes=[pltpu.VMEM((tm, tn), F32)] if nk > 1 else [],
        compiler_params=_params(("parallel", "parallel", "arbitrary")),
    )(a, b, *tile_extras, *col_extras)
    return outs[0] if n_out == 1 else tuple(outs)


def _rowwise(fn, rows, bcast, out_rows, out_accs, *, name, tb_cap=256):
    t = rows[0].shape[0]
    tb = _row_tile(t, tb_cap)
    n_r, n_b, n_o, n_a = len(rows), len(bcast), len(out_rows), len(out_accs)

    def body(*refs):
        r_refs = refs[:n_r]
        b_refs = refs[n_r:n_r + n_b]
        o_refs = refs[n_r + n_b:n_r + n_b + n_o]
        a_refs = refs[n_r + n_b + n_o:]
        outs, accs = fn([r[...] for r in r_refs], [r[...] for r in b_refs])
        for o_ref, o in zip(o_refs, outs):
            o_ref[...] = o.astype(o_ref.dtype)
        if n_a:
            i = pl.program_id(0)

            @pl.when(i == 0)
            def _():
                for a_ref, acc in zip(a_refs, accs):
                    a_ref[...] = acc

            @pl.when(i > 0)
            def _():
                for a_ref, acc in zip(a_refs, accs):
                    a_ref[...] += acc

    in_specs = [pl.BlockSpec((tb, r.shape[1]), lambda i: (i, 0)) for r in rows]
    in_specs += [pl.BlockSpec(b.shape, lambda i: (0, 0)) for b in bcast]
    out_specs = [pl.BlockSpec((tb, c), lambda i: (i, 0)) for c, _ in out_rows]
    out_specs += [pl.BlockSpec(s, lambda i: (0, 0)) for s in out_accs]
    out_shape = [jax.ShapeDtypeStruct((t, c), dt) for c, dt in out_rows]
    out_shape += [jax.ShapeDtypeStruct(s, F32) for s in out_accs]
    outs = pl.pallas_call(
        body,
        name=name,
        grid=(t // tb,),
        in_specs=in_specs,
        out_specs=out_specs,
        out_shape=out_shape,
        compiler_params=_params(("arbitrary",)),
    )(*rows, *bcast)
    return tuple(outs)


def _colsum(v):
    return jnp.sum(v, axis=0, keepdims=True)


def _sigmoid(v):
    return 1.0 / (1.0 + jnp.exp(-v))


def _ln_fwd(x_res, y_sub, g, b, *, name):
    d = x_res.shape[1]

    def fn(rows, bc):
        z = ALPHA * rows[0] + rows[1]
        mu = jnp.mean(z, axis=-1, keepdims=True)
        zc = z - mu
        var = jnp.mean(zc * zc, axis=-1, keepdims=True)
        rstd = lax.rsqrt(var + LN_EPS)
        xhat = zc * rstd
        out = xhat * bc[0] + bc[1]
        return (out, out, xhat, rstd), ()

    return _rowwise(fn, [x_res, y_sub], [g, b], [(d, F32), (d, MXU_DTYPE), (d, F32), (1, F32)], [], name=name)


def _ln_bwd(terms, xhat, rstd, g, *, name):
    d = xhat.shape[1]
    coefs = [c for c, _ in terms]

    def fn(rows, bc):
        xh, rs = rows[0], rows[1]
        dout = None
        for c, v in zip(coefs, rows[2:]):
            tv = v if c == 1.0 else c * v
            dout = tv if dout is None else dout + tv
        dxh = dout * bc[0]
        m1 = jnp.mean(dxh, axis=-1, keepdims=True)
        m2 = jnp.mean(dxh * xh, axis=-1, keepdims=True)
        dz = rs * (dxh - m1 - xh * m2)
        return (dz, dz), (_colsum(dout * xh), _colsum(dout), _colsum(dz))

    return _rowwise(fn, [xhat, rstd] + [v for _, v in terms], [g], [(d, F32), (d, MXU_DTYPE)],
                    [(1, d), (1, d), (1, d)], name=name)


def _rms_fwd(x, g, *, name):
    c = x.shape[1]

    def fn(rows, bc):
        v = rows[0]
        r = lax.rsqrt(jnp.mean(v * v, axis=-1, keepdims=True) + RMS_EPS)
        return (v * r * bc[0],), ()

    return _rowwise(fn, [x], [g], [(c, MXU_DTYPE)], [], name=name)[0]


def _rms_bwd(x, g, dy, *, name):
    c = x.shape[1]

    def fn(rows, bc):
        v, dyv = rows
        r = lax.rsqrt(jnp.mean(v * v, axis=-1, keepdims=True) + RMS_EPS)
        xh = v * r
        dxh = dyv * bc[0]
        dx = r * (dxh - xh * jnp.mean(dxh * xh, axis=-1, keepdims=True))
        return (dx,), (_colsum(dyv * xh),)

    return _rowwise(fn, [x, dy], [g], [(c, F32)], [(1, c)], name=name)


def _conv_chunk(t):
    return _row_tile(t, 256)


def _conv_fwd(x, w, bias, *, name):
    t, c = x.shape
    kw = w.shape[0]
    ch = _conv_chunk(t)

    def body(x_ref, w_ref, b_ref, o_ref, pad_ref):
        pad_ref[0:CONV_PAD, :] = jnp.zeros((CONV_PAD, LANES), F32)
        pad_ref[CONV_PAD:CONV_PAD + t, :] = x_ref[...]
        for t0 in range(0, t, ch):
            acc = jnp.broadcast_to(b_ref[...], (ch, LANES))
            for k in range(kw):
                s = kw - 1 - k
                acc = acc + w_ref[k:k + 1, :] * pad_ref[CONV_PAD - s + t0:CONV_PAD - s + t0 + ch, :]
            o_ref[t0:t0 + ch, :] = acc

    return pl.pallas_call(
        body,
        name=name,
        grid=(c // LANES,),
        in_specs=[pl.BlockSpec((t, LANES), lambda j: (0, j)), pl.BlockSpec((kw, LANES), lambda j: (0, j)),
                  pl.BlockSpec((1, LANES), lambda j: (0, j))],
        out_specs=pl.BlockSpec((t, LANES), lambda j: (0, j)),
        out_shape=jax.ShapeDtypeStruct((t, c), F32),
        scratch_shapes=[pltpu.VMEM((CONV_PAD + t, LANES), F32)],
        compiler_params=_params(("parallel",)),
    )(x, w, bias)


def _conv_bwd(dy, x, w, *, name):
    t, c = x.shape
    kw = w.shape[0]
    ch = _conv_chunk(t)

    def body(dy_ref, x_ref, w_ref, dx_ref, dw_ref, xpad_ref, dpad_ref):
        xpad_ref[0:CONV_PAD, :] = jnp.zeros((CONV_PAD, LANES), F32)
        xpad_ref[CONV_PAD:CONV_PAD + t, :] = x_ref[...]
        dpad_ref[0:t, :] = dy_ref[...]
        dpad_ref[t:t + CONV_PAD, :] = jnp.zeros((CONV_PAD, LANES), F32)
        for t0 in range(0, t, ch):
            acc = jnp.zeros((ch, LANES), F32)
            for k in range(kw):
                s = kw - 1 - k
                acc = acc + w_ref[k:k + 1, :] * dpad_ref[t0 + s:t0 + s + ch, :]
            dx_ref[t0:t0 + ch, :] = acc
        for k in range(kw):
            s = kw - 1 - k
            acc = jnp.zeros((ch, LANES), F32)
            for t0 in range(0, t, ch):
                acc = acc + dy_ref[t0:t0 + ch, :] * xpad_ref[CONV_PAD - s + t0:CONV_PAD - s + t0 + ch, :]
            dw_ref[k:k + 1, :] = _colsum(acc)

    blk = pl.BlockSpec((t, LANES), lambda j: (0, j))
    wblk = pl.BlockSpec((kw, LANES), lambda j: (0, j))
    return pl.pallas_call(
        body,
        name=name,
        grid=(c // LANES,),
        in_specs=[blk, blk, wblk],
        out_specs=[blk, wblk],
        out_shape=[jax.ShapeDtypeStruct((t, c), F32), jax.ShapeDtypeStruct((kw, c), F32)],
        scratch_shapes=[pltpu.VMEM((CONV_PAD + t, LANES), F32), pltpu.VMEM((CONV_PAD + t, LANES), F32)],
        compiler_params=_params(("parallel",)),
    )(dy, x, w)


ATTN_SCALE = (QK_NOPE + QK_ROPE) ** -0.5
_NT = (((1,), (1,)), ((), ()))
_TN = (((0,), (0,)), ((), ()))
_NN = (((1,), (0,)), ((), ()))


def _dot(a, b, dims):
    return lax.dot_general(a, b, dims, preferred_element_type=F32)


def _scores(qn_ref, qp_ref, kn_ref, kp_ref, qi, tq, t):
    s = _dot(qn_ref[...], kn_ref[...], _NT) + _dot(qp_ref[...], kp_ref[...], _NT)
    s = s * ATTN_SCALE
    rows = qi * tq + lax.broadcasted_iota(jnp.int32, (tq, t), 0)
    cols = lax.broadcasted_iota(jnp.int32, (tq, t), 1)
    shift = CHUNK.bit_length() - 1
    allowed = jnp.right_shift(cols, shift) <= jnp.right_shift(rows, shift)
    return jnp.where(allowed, s, MASK_VALUE)


def _attn_fwd(qn, qp, kn, kp, v, *, name):
    h, t, _ = qn.shape
    tq = _row_tile(t, 256)

    def body(qn_ref, qp_ref, kn_ref, kp_ref, v_ref, o_ref, lse_ref):
        s = _scores(qn_ref, qp_ref, kn_ref, kp_ref, pl.program_id(1), tq, t)
        m = jnp.max(s, axis=-1, keepdims=True)
        p = jnp.exp(s - m)
        l = jnp.sum(p, axis=-1, keepdims=True)
        o = _dot((p / l).astype(MXU_DTYPE), v_ref[...], _NN)
        o_ref[...] = o.astype(o_ref.dtype)
        lse_ref[...] = m + jnp.log(l)

    qblk = lambda w: pl.BlockSpec((None, tq, w), lambda hh, qi: (hh, qi, 0))
    kblk = lambda w: pl.BlockSpec((None, t, w), lambda hh, qi: (hh, 0, 0))
    return pl.pallas_call(
        body,
        name=name,
        grid=(h, t // tq),
        in_specs=[qblk(QK_NOPE), qblk(QK_ROPE), kblk(QK_NOPE), pl.BlockSpec((t, QK_ROPE), lambda hh, qi: (0, 0)),
                  kblk(V_HEAD)],
        out_specs=[qblk(V_HEAD), qblk(1)],
        out_shape=[jax.ShapeDtypeStruct((h, t, V_HEAD), MXU_DTYPE), jax.ShapeDtypeStruct((h, t, 1), F32)],
        compiler_params=_params(("parallel", "parallel")),
    )(qn, qp, kn, kp, v)


def _attn_bwd(qn, qp, kn, kp, v, do, lse, *, name):
    h, t, _ = qn.shape
    tq = _row_tile(t, 256)

    def body(qn_ref, qp_ref, kn_ref, kp_ref, v_ref, do_ref, lse_ref, dqn_ref, dqp_ref, dkn_ref, dkp_ref, dv_ref):
        hh, qi = pl.program_id(0), pl.program_id(1)
        s = _scores(qn_ref, qp_ref, kn_ref, kp_ref, qi, tq, t)
        p = jnp.exp(s - lse_ref[...])
        dov = do_ref[...]
        dp = _dot(dov, v_ref[...], _NT)
        ds = p * (dp - jnp.sum(dp * p, axis=-1, keepdims=True)) * ATTN_SCALE
        ds_m = ds.astype(MXU_DTYPE)
        p_m = p.astype(MXU_DTYPE)
        dqn_ref[...] = _dot(ds_m, kn_ref[...], _NN)
        dqp_ref[...] = _dot(ds_m, kp_ref[...], _NN)
        dkn_part = _dot(ds_m, qn_ref[...], _TN)
        dkp_part = _dot(ds_m, qp_ref[...], _TN)
        dv_part = _dot(p_m, dov, _TN)

        @pl.when(qi == 0)
        def _():
            dkn_ref[...] = dkn_part
            dv_ref[...] = dv_part

        @pl.when(qi > 0)
        def _():
            dkn_ref[...] += dkn_part
            dv_ref[...] += dv_part

        first = jnp.logical_and(hh == 0, qi == 0)

        @pl.when(first)
        def _():
            dkp_ref[...] = dkp_part

        @pl.when(jnp.logical_not(first))
        def _():
            dkp_ref[...] += dkp_part

    qblk = lambda w: pl.BlockSpec((None, tq, w), lambda hh, qi: (hh, qi, 0))
    kblk = lambda w: pl.BlockSpec((None, t, w), lambda hh, qi: (hh, 0, 0))
    kpblk = pl.BlockSpec((t, QK_ROPE), lambda hh, qi: (0, 0))
    return pl.pallas_call(
        body,
        name=name,
        grid=(h, t // tq),
        in_specs=[qblk(QK_NOPE), qblk(QK_ROPE), kblk(QK_NOPE), kpblk, kblk(V_HEAD), qblk(V_HEAD), qblk(1)],
        out_specs=[qblk(QK_NOPE), qblk(QK_ROPE), kblk(QK_NOPE), kpblk, kblk(V_HEAD)],
        out_shape=[jax.ShapeDtypeStruct((h, t, QK_NOPE), F32), jax.ShapeDtypeStruct((h, t, QK_ROPE), F32),
                   jax.ShapeDtypeStruct((h, t, QK_NOPE), F32), jax.ShapeDtypeStruct((t, QK_ROPE), F32),
                   jax.ShapeDtypeStruct((h, t, V_HEAD), F32)],
        compiler_params=_params(("arbitrary", "arbitrary")),
    )(qn, qp, kn, kp, v, do, lse)


def _rope(x1, x2, cos, sin, *, name, inverse=False):
    w = x1.shape[1]
    sign = -1.0 if inverse else 1.0

    def fn(rows, bc):
        a, b, c, s = rows
        s = sign * s
        return (a * c - b * s, a * s + b * c), ()

    return _rowwise(fn, [x1, x2, cos, sin], [], [(w, F32), (w, F32)], [], name=name)


def _mlp_fwd(x_m, w1, w2, tag):
    def epi(acc):
        r = jnp.maximum(acc, 0.0)
        return acc, r * r

    hid, act = _mm(x_m, w1, name=f"{tag}_up", out_dtypes=(F32, MXU_DTYPE), epilogue=epi)
    y = _mm(act, w2, name=f"{tag}_down")
    return y, (x_m, hid, act)


def _mlp_bwd(dy_m, saved, w1, w2, tag):
    x_m, hid, act = saved
    dw2 = _mm(act, dy_m, ta=True, name=f"{tag}_dw2", out_dtypes=(MXU_DTYPE,))

    def epi(acc, hv):
        return (acc * (2.0 * jnp.maximum(hv, 0.0)),)

    dh = _mm(dy_m, w2, tb=True, name=f"{tag}_dact", out_dtypes=(MXU_DTYPE,), epilogue=epi, tile_extras=(hid,))
    dw1 = _mm(x_m, dh, ta=True, name=f"{tag}_dw1", out_dtypes=(MXU_DTYPE,))
    dx = _mm(dh, w1, tb=True, name=f"{tag}_dx")
    return [dx], dw1, dw2


def _sc_fwd(x_m, w_in, conv_w, w_out, tag):
    d = D_MODEL
    u = _mm(x_m, w_in, name=f"{tag}_in")

    def gate_fn(rows, bc):
        uv = rows[0]
        return (uv[:, d:2 * d] * uv[:, 2 * d:3 * d],), ()

    p = _rowwise(gate_fn, [u], [], [(d, F32)], [], name=f"{tag}_ch")[0]
    q = _conv_fwd(p, conv_w, jnp.zeros((1, d), F32), name=f"{tag}_conv")

    def out_fn(rows, bc):
        return (rows[0][:, 0:d] * rows[1],), ()

    r = _rowwise(out_fn, [u, q], [], [(d, MXU_DTYPE)], [], name=f"{tag}_bq")[0]
    y = _mm(r, w_out, name=f"{tag}_out")
    return y, (x_m, u, p, q, r)


def _sc_bwd(dy_m, saved, w_in, conv_w, w_out, tag):
    d = D_MODEL
    x_m, u, p, q, r = saved
    dw_out = _mm(r, dy_m, ta=True, name=f"{tag}_dwout", out_dtypes=(MXU_DTYPE,))
    dr = _mm(dy_m, w_out, tb=True, name=f"{tag}_dr")

    def dq_fn(rows, bc):
        return (rows[0] * rows[1][:, 0:d],), ()

    dq = _rowwise(dq_fn, [dr, u], [], [(d, F32)], [], name=f"{tag}_dq")[0]
    dp, dconv = _conv_bwd(dq, p, conv_w, name=f"{tag}_dconv")

    def du_fn(rows, bc):
        drv, qv, dpv, uv = rows
        return (jnp.concatenate([drv * qv, dpv * uv[:, 2 * d:3 * d], dpv * uv[:, d:2 * d]], axis=1),), ()

    du = _rowwise(du_fn, [dr, q, dp, u], [], [(3 * d, MXU_DTYPE)], [], name=f"{tag}_du")[0]
    dw_in = _mm(x_m, du, ta=True, name=f"{tag}_dwin", out_dtypes=(MXU_DTYPE,))
    dx = _mm(du, w_in, tb=True, name=f"{tag}_dx")
    return [dx], {"sc_w_in": dw_in, "sc_conv_w": dconv, "sc_w_out": dw_out}


def _cf_fwd(x_m, w, tag):
    d = D_MODEL

    def bias_epi(acc, bias):
        return (acc + bias,)

    u = _mm(x_m, w["cf_w_pw1"], name=f"{tag}_pw1", epilogue=bias_epi, col_extras=(w["cf_b_pw1"],))

    def glu_fn(rows, bc):
        uv = rows[0]
        return (uv[:, 0:d] * _sigmoid(uv[:, d:2 * d]),), ()

    h1 = _rowwise(glu_fn, [u], [], [(d, F32)], [], name=f"{tag}_glu")[0]
    h2 = _conv_fwd(h1, w["cf_dw_w"], w["cf_dw_b"], name=f"{tag}_conv")

    def norm_fn(rows, bc):
        hv = rows[0]
        mu = jnp.mean(hv, axis=-1, keepdims=True)
        hc = hv - mu
        rstd = lax.rsqrt(jnp.mean(hc * hc, axis=-1, keepdims=True) + LN_EPS)
        h3 = hc * rstd * bc[0] + bc[1]
        return (h3 * _sigmoid(h3),), ()

    h4 = _rowwise(norm_fn, [h2], [w["cf_norm_g"], w["cf_norm_b"]], [(d, MXU_DTYPE)], [], name=f"{tag}_norm")[0]
    y = _mm(h4, w["cf_w_pw2"], name=f"{tag}_pw2", epilogue=bias_epi, col_extras=(w["cf_b_pw2"],))
    return y, (x_m, u, h1, h2, h4)


def _cf_bwd(dy_m, dy_colsum, saved, w, tag):
    d = D_MODEL
    x_m, u, h1, h2, h4 = saved
    dw_pw2 = _mm(h4, dy_m, ta=True, name=f"{tag}_dwpw2", out_dtypes=(MXU_DTYPE,))
    dh4 = _mm(dy_m, w["cf_w_pw2"], tb=True, name=f"{tag}_dh4")

    def dnorm_fn(rows, bc):
        dh4v, hv = rows
        g, b = bc
        mu = jnp.mean(hv, axis=-1, keepdims=True)
        hc = hv - mu
        rstd = lax.rsqrt(jnp.mean(hc * hc, axis=-1, keepdims=True) + LN_EPS)
        xh = hc * rstd
        h3 = xh * g + b
        sg = _sigmoid(h3)
        dh3 = dh4v * (sg * (1.0 + h3 * (1.0 - sg)))
        dxh = dh3 * g
        dh2 = rstd * (dxh - jnp.mean(dxh, axis=-1, keepdims=True) - xh * jnp.mean(dxh * xh, axis=-1, keepdims=True))
        return (dh2,), (_colsum(dh3 * xh), _colsum(dh3), _colsum(dh2))

    dh2, dnorm_g, dnorm_b, ddw_b = _rowwise(dnorm_fn, [dh4, h2], [w["cf_norm_g"], w["cf_norm_b"]], [(d, F32)],
                                            [(1, d), (1, d), (1, d)], name=f"{tag}_dnorm")
    dh1, ddw_w = _conv_bwd(dh2, h1, w["cf_dw_w"], name=f"{tag}_dconv")

    def dglu_fn(rows, bc):
        dh1v, uv = rows
        a, gate = uv[:, 0:d], uv[:, d:2 * d]
        sg = _sigmoid(gate)
        du = jnp.concatenate([dh1v * sg, dh1v * a * sg * (1.0 - sg)], axis=1)
        return (du,), (_colsum(du),)

    du, db_pw1 = _rowwise(dglu_fn, [dh1, u], [], [(2 * d, MXU_DTYPE)], [(1, 2 * d)], name=f"{tag}_dglu")
    dw_pw1 = _mm(x_m, du, ta=True, name=f"{tag}_dwpw1", out_dtypes=(MXU_DTYPE,))
    dx = _mm(du, w["cf_w_pw1"], tb=True, name=f"{tag}_dx")
    grads = {"cf_w_pw1": dw_pw1, "cf_b_pw1": db_pw1, "cf_dw_w": ddw_w, "cf_dw_b": ddw_b, "cf_norm_g": dnorm_g,
             "cf_norm_b": dnorm_b, "cf_w_pw2": dw_pw2, "cf_b_pw2": dy_colsum}
    return [dx], grads


def _rope_tables(t):
    pos = jnp.arange(t, dtype=F32)
    inv_freq = ROPE_THETA ** (-jnp.arange(0, QK_ROPE, 2, dtype=F32) / QK_ROPE)
    ang = pos[:, None] * inv_freq[None, :]
    return jnp.cos(ang), jnp.sin(ang)


def _heads(a, width):
    return a.reshape(a.shape[0], MLA_HEADS, width).transpose(1, 0, 2)


def _unheads(a):
    return a.transpose(1, 0, 2).reshape(a.shape[1], -1)


def _mla_fwd(x_m, w, tag):
    t = x_m.shape[0]
    half = QK_ROPE // 2
    cos, sin = _rope_tables(t)
    cos_h, sin_h = jnp.tile(cos, (1, MLA_HEADS)), jnp.tile(sin, (1, MLA_HEADS))
    cq_pre = _mm(x_m, w["mla_w_dq"], name=f"{tag}_dq")
    cq = _rms_fwd(cq_pre, w["mla_g_q"], name=f"{tag}_qnorm")
    q = _mm(cq, w["mla_w_uq"], name=f"{tag}_uq").reshape(t, MLA_HEADS, QK_NOPE + QK_ROPE)
    q1 = q[:, :, QK_NOPE:QK_NOPE + half].reshape(t, MLA_HEADS * half)
    q2 = q[:, :, QK_NOPE + half:].reshape(t, MLA_HEADS * half)
    r1, r2 = _rope(q1, q2, cos_h, sin_h, name=f"{tag}_qrope")
    qp = jnp.concatenate([r1.reshape(t, MLA_HEADS, half), r2.reshape(t, MLA_HEADS, half)], axis=-1)
    qn_h = q[:, :, :QK_NOPE].transpose(1, 0, 2).astype(MXU_DTYPE)
    qp_h = qp.transpose(1, 0, 2).astype(MXU_DTYPE)
    ckv_full = _mm(x_m, w["mla_w_dkv"], name=f"{tag}_dkv")
    ckv_pre = ckv_full[:, :KV_LORA]
    ckv = _rms_fwd(ckv_pre, w["mla_g_kv"], name=f"{tag}_kvnorm")
    k1, k2 = _rope(ckv_full[:, KV_LORA:KV_LORA + half], ckv_full[:, KV_LORA + half:], cos, sin, name=f"{tag}_krope")
    kp = jnp.concatenate([k1, k2], axis=-1).astype(MXU_DTYPE)
    kn_h = _heads(_mm(ckv, w["mla_w_uk"], name=f"{tag}_uk", out_dtypes=(MXU_DTYPE,)), QK_NOPE)
    v_h = _heads(_mm(ckv, w["mla_w_uv"], name=f"{tag}_uv", out_dtypes=(MXU_DTYPE,)), V_HEAD)
    o_h, lse = _attn_fwd(qn_h, qp_h, kn_h, kp, v_h, name=f"{tag}_attn")
    o_flat = _unheads(o_h)
    y = _mm(o_flat, w["mla_w_o"], name=f"{tag}_o")
    return y, (x_m, cq_pre, cq, ckv_pre, ckv, qn_h, qp_h, kn_h, kp, v_h, lse, o_flat, (cos, sin, cos_h, sin_h))


def _mla_bwd(dy_m, saved, w, tag):
    x_m, cq_pre, cq, ckv_pre, ckv, qn_h, qp_h, kn_h, kp, v_h, lse, o_flat, (cos, sin, cos_h, sin_h) = saved
    t = x_m.shape[0]
    half = QK_ROPE // 2
    dw_o = _mm(o_flat, dy_m, ta=True, name=f"{tag}_dwo", out_dtypes=(MXU_DTYPE,))
    do_h = _heads(_mm(dy_m, w["mla_w_o"], tb=True, name=f"{tag}_do", out_dtypes=(MXU_DTYPE,)), V_HEAD)
    dqn_h, dqp_h, dkn_h, dkp, dv_h = _attn_bwd(qn_h, qp_h, kn_h, kp, v_h, do_h, lse, name=f"{tag}_dattn")
    dqp = dqp_h.transpose(1, 0, 2)
    d1, d2 = _rope(dqp[:, :, :half].reshape(t, -1), dqp[:, :, half:].reshape(t, -1), cos_h, sin_h,
                   name=f"{tag}_dqrope", inverse=True)
    dq = jnp.concatenate([dqn_h.transpose(1, 0, 2), d1.reshape(t, MLA_HEADS, half), d2.reshape(t, MLA_HEADS, half)],
                         axis=-1).reshape(t, -1).astype(MXU_DTYPE)
    dw_uq = _mm(cq, dq, ta=True, name=f"{tag}_dwuq", out_dtypes=(MXU_DTYPE,))
    dcq = _mm(dq, w["mla_w_uq"], tb=True, name=f"{tag}_dcq")
    dcq_pre, dg_q = _rms_bwd(cq_pre, w["mla_g_q"], dcq, name=f"{tag}_dqnorm")
    dcq_pre_m = dcq_pre.astype(MXU_DTYPE)
    dw_dq = _mm(x_m, dcq_pre_m, ta=True, name=f"{tag}_dwdq", out_dtypes=(MXU_DTYPE,))
    dx_q = _mm(dcq_pre_m, w["mla_w_dq"], tb=True, name=f"{tag}_dxq")
    dkn = _unheads(dkn_h).astype(MXU_DTYPE)
    dv = _unheads(dv_h).astype(MXU_DTYPE)
    dw_uk = _mm(ckv, dkn, ta=True, name=f"{tag}_dwuk", out_dtypes=(MXU_DTYPE,))
    dw_uv = _mm(ckv, dv, ta=True, name=f"{tag}_dwuv", out_dtypes=(MXU_DTYPE,))
    dckv_k = _mm(dkn, w["mla_w_uk"], tb=True, name=f"{tag}_dckvk")
    dckv = _mm(dv, w["mla_w_uv"], tb=True, name=f"{tag}_dckv", epilogue=lambda acc, other: (acc + other,),
               tile_extras=(dckv_k,))
    dckv_pre, dg_kv = _rms_bwd(ckv_pre, w["mla_g_kv"], dckv, name=f"{tag}_dkvnorm")
    dk1, dk2 = _rope(dkp[:, :half], dkp[:, half:], cos, sin, name=f"{tag}_dkrope", inverse=True)
    dckv_full = jnp.concatenate([dckv_pre, dk1, dk2], axis=-1).astype(MXU_DTYPE)
    dw_dkv = _mm(x_m, dckv_full, ta=True, name=f"{tag}_dwdkv", out_dtypes=(MXU_DTYPE,))
    dx_kv = _mm(dckv_full, w["mla_w_dkv"], tb=True, name=f"{tag}_dxkv")
    grads = {"mla_w_dq": dw_dq, "mla_g_q": dg_q, "mla_w_uq": dw_uq, "mla_w_dkv": dw_dkv, "mla_g_kv": dg_kv,
             "mla_w_uk": dw_uk, "mla_w_uv": dw_uv, "mla_w_o": dw_o}
    return [dx_q, dx_kv], grads


def _layer_weights(w, prefix, j):
    out = {}
    for k, v in w.items():
        if k.startswith(prefix):
            out[k] = v[j] if v.ndim == 3 else v[j].reshape(1, -1)
    return out


def _local_step(x, target, w):
    d = D_MODEL
    cast = lambda rows, bc: ((rows[0],), ())
    x_m = _rowwise(cast, [x], [], [(d, MXU_DTYPE)], [], name="x_cast")[0]
    cur, cur_m = x, x_m
    saved = []
    for i in range(DEPTH):
        mixer, j = i % N_MIXERS, i // N_MIXERS
        if mixer == 0:
            y, s_mix = _sc_fwd(cur_m, w["sc_w_in"][j], w["sc_conv_w"][j], w["sc_w_out"][j], f"l{i}_sc")
        elif mixer == 1:
            y, s_mix = _mla_fwd(cur_m, _layer_weights(w, "mla_", j), f"l{i}_mla")
        else:
            y, s_mix = _cf_fwd(cur_m, _layer_weights(w, "cf_", j), f"l{i}_cf")
        cur, cur_m, xh_mix, rs_mix = _ln_fwd(cur, y, w["ln_mix_g"][i:i + 1], w["ln_mix_b"][i:i + 1], name=f"l{i}_lnmix")
        y, s_ff = _mlp_fwd(cur_m, w["ff_w1"][i], w["ff_w2"][i], f"l{i}_ff")
        cur, cur_m, xh_ff, rs_ff = _ln_fwd(cur, y, w["ln_ff_g"][i:i + 1], w["ln_ff_b"][i:i + 1], name=f"l{i}_lnff")
        saved.append((s_mix, xh_mix, rs_mix, s_ff, xh_ff, rs_ff))

    def loss_fn(rows, bc):
        err = rows[0] - rows[1]
        return (err * (1.0 / d),), (_colsum(err * err),)

    dy, sq = _rowwise(loss_fn, [cur, target], [], [(d, F32)], [(1, d)], name="loss")
    loss = (0.5 / d) * jnp.sum(sq)

    names = ["sc_w_in", "sc_conv_w", "sc_w_out", "mla_w_dq", "mla_g_q", "mla_w_uq", "mla_w_dkv", "mla_g_kv", "mla_w_uk",
             "mla_w_uv", "mla_w_o", "cf_w_pw1", "cf_b_pw1", "cf_dw_w", "cf_dw_b", "cf_norm_g", "cf_norm_b", "cf_w_pw2",
             "cf_b_pw2", "ff_w1", "ff_w2", "ln_mix_g", "ln_mix_b", "ln_ff_g", "ln_ff_b"]
    grads = {n: [None] * w[n].shape[0] for n in names}
    terms = [(1.0, dy)]
    for i in reversed(range(DEPTH)):
        mixer, j = i % N_MIXERS, i // N_MIXERS
        s_mix, xh_mix, rs_mix, s_ff, xh_ff, rs_ff = saved[i]
        dz, dz_m, dg, db, _ = _ln_bwd(terms, xh_ff, rs_ff, w["ln_ff_g"][i:i + 1], name=f"l{i}_dlnff")
        grads["ln_ff_g"][i], grads["ln_ff_b"][i] = dg, db
        dxs, dw1, dw2 = _mlp_bwd(dz_m, s_ff, w["ff_w1"][i], w["ff_w2"][i], f"l{i}_ff")
        grads["ff_w1"][i], grads["ff_w2"][i] = dw1, dw2
        terms = [(ALPHA, dz)] + [(1.0, v) for v in dxs]
        dz, dz_m, dg, db, dz_sum = _ln_bwd(terms, xh_mix, rs_mix, w["ln_mix_g"][i:i + 1], name=f"l{i}_dlnmix")
        grads["ln_mix_g"][i], grads["ln_mix_b"][i] = dg, db
        if mixer == 0:
            dxs, g_mix = _sc_bwd(dz_m, s_mix, w["sc_w_in"][j], w["sc_conv_w"][j], w["sc_w_out"][j], f"l{i}_sc")
        elif mixer == 1:
            dxs, g_mix = _mla_bwd(dz_m, s_mix, _layer_weights(w, "mla_", j), f"l{i}_mla")
        else:
            dxs, g_mix = _cf_bwd(dz_m, dz_sum, s_mix, _layer_weights(w, "cf_", j), f"l{i}_cf")
        for n, g in g_mix.items():
            grads[n][j] = g
        terms = [(ALPHA, dz)] + [(1.0, v) for v in dxs]

    coefs = [c for c, _ in terms]

    def sum_fn(rows, bc):
        acc = None
        for c, v in zip(coefs, rows):
            tv = v if c == 1.0 else c * v
            acc = tv if acc is None else acc + tv
        return (acc,), ()

    grad_x = _rowwise(sum_fn, [v for _, v in terms], [], [(d, F32)], [], name="grad_x")[0]
    return loss, grad_x, grads


MESH_ID = pl.DeviceIdType.MESH
ANY = pl.BlockSpec(memory_space=pl.ANY)


def _block_id(px, py, pc):
    return 4 * px + 2 * py + pc


def _all_gather(shard, *, name):
    r, c = shard.shape

    def body(x_ref, out_ref, send_sems, recv_sems, local_sem):
        x, y, core = lax.axis_index("x"), lax.axis_index("y"), lax.axis_index("c")
        me, sibling = (x, y, core), (x, y, 1 - core)
        chips = [(1 - x, y), (x, 1 - y), (1 - x, 1 - y)]

        def rows(pos):
            return out_ref.at[_block_id(*pos)]

        def copy(k, block, to, src=None):
            return pltpu.make_async_remote_copy(
                src_ref=rows(block) if src is None else src, dst_ref=rows(block), send_sem=send_sems.at[k],
                recv_sem=recv_sems.at[k], device_id=to, device_id_type=MESH_ID)

        mine = pltpu.make_async_copy(x_ref, rows(me), local_sem)
        mine.start()
        first = [copy(0, me, sibling, src=x_ref)]
        first += [copy(1 + j, me, (*chip, core), src=x_ref) for j, chip in enumerate(chips)]
        for cp in first:
            cp.start()
        passed = [copy(4 + j, (*chip, core), sibling) for j, chip in enumerate(chips)]
        for j, chip in enumerate(chips):
            copy(1 + j, (*chip, core), me).wait_recv()
            passed[j].start()
        copy(0, sibling, me).wait_recv()
        for j, chip in enumerate(chips):
            copy(4 + j, (*chip, 1 - core), me).wait_recv()
        for cp in first + passed:
            cp.wait_send()
        mine.wait()

    return pl.pallas_call(
        body,
        name=name,
        out_shape=jax.ShapeDtypeStruct((N_DEV, r, c), shard.dtype),
        in_specs=[ANY],
        out_specs=ANY,
        scratch_shapes=[pltpu.SemaphoreType.DMA((7,)), pltpu.SemaphoreType.DMA((7,)), pltpu.SemaphoreType.DMA],
    )(shard)


def _swap_with_sibling(parts, *, name):
    _, r, c = parts.shape

    def body(p_ref, got_ref, send_sems, recv_sems):
        x, y, core = lax.axis_index("x"), lax.axis_index("y"), lax.axis_index("c")
        chips = [(x, y), (1 - x, y), (x, 1 - y), (1 - x, 1 - y)]
        copies = []
        for k, (px, py) in enumerate(chips):
            copies.append(pltpu.make_async_remote_copy(
                src_ref=p_ref.at[_block_id(px, py, 1 - core)], dst_ref=got_ref.at[k], send_sem=send_sems.at[k],
                recv_sem=recv_sems.at[k], device_id=(x, y, 1 - core), device_id_type=MESH_ID))
        for cp in copies:
            cp.start()
        for cp in copies:
            cp.wait()

    return pl.pallas_call(
        body,
        name=name,
        out_shape=jax.ShapeDtypeStruct((4, r, c), parts.dtype),
        in_specs=[ANY],
        out_specs=ANY,
        scratch_shapes=[pltpu.SemaphoreType.DMA((4,)), pltpu.SemaphoreType.DMA((4,))],
    )(parts)


def _swap_between_chips(parts, *, name):
    _, r, c = parts.shape

    def body(p_ref, got_ref, send_sems, recv_sems):
        x, y, core = lax.axis_index("x"), lax.axis_index("y"), lax.axis_index("c")
        chips = [(1 - x, y), (x, 1 - y), (1 - x, 1 - y)]
        copies = []
        for j, (px, py) in enumerate(chips):
            copies.append(pltpu.make_async_remote_copy(
                src_ref=p_ref.at[j], dst_ref=got_ref.at[j], send_sem=send_sems.at[j], recv_sem=recv_sems.at[j],
                device_id=(px, py, core), device_id_type=MESH_ID))
        for cp in copies:
            cp.start()
        for cp in copies:
            cp.wait()

    return pl.pallas_call(
        body,
        name=name,
        out_shape=jax.ShapeDtypeStruct((3, r, c), parts.dtype),
        in_specs=[ANY],
        out_specs=ANY,
        scratch_shapes=[pltpu.SemaphoreType.DMA((3,)), pltpu.SemaphoreType.DMA((3,))],
    )(parts)


def _chip_partials(parts, got, *, name):
    _, r, c = parts.shape
    tb = _row_tile(r, 512)
    x, y, core = lax.axis_index("x"), lax.axis_index("y"), lax.axis_index("c")
    chips = [(x, y), (1 - x, y), (x, 1 - y), (1 - x, 1 - y)]
    ids = jnp.stack([_block_id(px, py, core) for px, py in chips]).astype(jnp.int32)

    def body(ids_ref, p0, p1, p2, p3, g_ref, own_ref, out_ref):
        own_ref[...] = p0[...].astype(F32) + g_ref[0].astype(F32)
        for j, p in enumerate((p1, p2, p3)):
            out_ref[j] = (p[...].astype(F32) + g_ref[j + 1].astype(F32)).astype(out_ref.dtype)

    def part_spec(k):
        return pl.BlockSpec((None, tb, c), lambda i, ids_ref: (ids_ref[k], i, 0))

    grid_spec = pltpu.PrefetchScalarGridSpec(
        num_scalar_prefetch=1,
        grid=(r // tb,),
        in_specs=[part_spec(k) for k in range(4)] + [pl.BlockSpec((4, tb, c), lambda i, ids_ref: (0, i, 0))],
        out_specs=[pl.BlockSpec((tb, c), lambda i, ids_ref: (i, 0)), pl.BlockSpec((3, tb, c), lambda i, ids_ref: (0, i, 0))],
    )
    return pl.pallas_call(
        body,
        name=name,
        grid_spec=grid_spec,
        out_shape=[jax.ShapeDtypeStruct((r, c), F32), jax.ShapeDtypeStruct((3, r, c), parts.dtype)],
        compiler_params=_params(("parallel",)),
    )(ids, parts, parts, parts, parts, got)


def _sum_blocks(first, blocks, *, name):
    n, r, c = blocks.shape
    tb = _row_tile(r, 512)

    def body(*refs):
        b_ref, o_ref = refs[-2], refs[-1]
        acc = refs[0][...] if first is not None else b_ref[0].astype(F32)
        for j in range(0 if first is not None else 1, n):
            acc = acc + b_ref[j].astype(F32)
        o_ref[...] = acc

    row_spec = pl.BlockSpec((tb, c), lambda i: (i, 0))
    return pl.pallas_call(
        body,
        name=name,
        grid=(r // tb,),
        in_specs=([row_spec] if first is not None else []) + [pl.BlockSpec((n, tb, c), lambda i: (0, i, 0))],
        out_specs=row_spec,
        out_shape=jax.ShapeDtypeStruct((r, c), F32),
        compiler_params=_params(("parallel",)),
    )(*([first] if first is not None else []), blocks)


def _reduce_scatter(parts, *, tag):
    got = _swap_with_sibling(parts, name=f"{tag}_sibling")
    own, others = _chip_partials(parts, got, name=f"{tag}_chipsum")
    arrived = _swap_between_chips(others, name=f"{tag}_chips")
    return _sum_blocks(own, arrived, name=f"{tag}_final")


PARAMS = {
    "sc_w_in": ((2, 1024, 3072), 2), "sc_conv_w": ((2, 3, 1024), 2), "sc_w_out": ((2, 1024, 1024), 1),
    "mla_w_dq": ((1, 1024, 384), 1), "mla_g_q": ((1, 384), None), "mla_w_uq": ((1, 384, 1536), 2),
    "mla_w_dkv": ((1, 1024, 320), 1), "mla_g_kv": ((1, 256), None), "mla_w_uk": ((1, 256, 8, 128), 1),
    "mla_w_uv": ((1, 256, 8, 128), 1), "mla_w_o": ((1, 1024, 1024), 1), "cf_w_pw1": ((1, 1024, 2048), 2),
    "cf_b_pw1": ((1, 2048), 1), "cf_dw_w": ((1, 31, 1024), 2), "cf_dw_b": ((1, 1024), 1), "cf_norm_g": ((1, 1024), 1),
    "cf_norm_b": ((1, 1024), 1), "cf_w_pw2": ((1, 1024, 1024), 1), "cf_b_pw2": ((1, 1024), 1),
    "ff_w1": ((4, 1024, 4096), 2), "ff_w2": ((4, 4096, 1024), 1), "ln_mix_g": ((4, 1024), None),
    "ln_mix_b": ((4, 1024), None), "ln_ff_g": ((4, 1024), None), "ln_ff_b": ((4, 1024), None),
}
NAMES = list(PARAMS)
BIG = ["sc_w_in", "sc_w_out", "mla_w_dq", "mla_w_uq", "mla_w_dkv", "mla_w_uk", "mla_w_uv", "mla_w_o", "cf_w_pw1",
       "cf_w_pw2", "ff_w1", "ff_w2"]
SMALL = [n for n in NAMES if n not in BIG]
SMALL_SHARDED = [n for n in SMALL if PARAMS[n][1] is not None]
BIG_WIDTH = 1024
SMALL_WIDTH = LANES
BF16_ROWS = 16


def _shard_shape(name):
    shape, ax = PARAMS[name]
    if ax is None:
        return shape
    return tuple(s // N_DEV if i == ax else s for i, s in enumerate(shape))


def _rows_of(shape, width, align):
    n = 1
    for s in shape:
        n *= s
    rows = -(-n // width)
    return -(-rows // align) * align


def _pack_rows(arrays, width, align, dtype, lead=0):
    segs = []
    for a in arrays:
        batch = a.shape[:lead]
        flat = a.astype(dtype).reshape(*batch, -1)
        rows = _rows_of(a.shape[lead:], width, align)
        flat = jnp.pad(flat, [(0, 0)] * lead + [(0, rows * width - flat.shape[-1])])
        segs.append(flat.reshape(*batch, rows, width))
    return jnp.concatenate(segs, axis=lead)


def _unpack_rows(buf, shapes, width, align):
    out, off = [], 0
    lead = buf.shape[:-2]
    for shape in shapes:
        n = 1
        for s in shape:
            n *= s
        rows = _rows_of(shape, width, align)
        seg = buf[..., off:off + rows, :].reshape(*lead, rows * width)[..., :n]
        out.append(seg.reshape(*lead, *shape))
        off += rows
    return out


def _merge_shards(stacked, ax):
    moved = jnp.moveaxis(stacked, 0, ax)
    shape = moved.shape
    return moved.reshape(*shape[:ax], shape[ax] * shape[ax + 1], *shape[ax + 2:])


def _split_shards(full, ax):
    shape = full.shape
    split = full.reshape(*shape[:ax], N_DEV, shape[ax] // N_DEV, *shape[ax + 1:])
    return jnp.moveaxis(split, ax, 0)


def _adamw(w, g, m, v, *, name):
    shape = w.shape
    c = shape[-1]
    c1 = 1.0 - ADAM_B1 ** ADAM_STEP
    c2 = 1.0 - ADAM_B2 ** ADAM_STEP

    def fn(rows, bc):
        wv, gv, mv, vv = rows
        m_new = ADAM_B1 * mv + (1.0 - ADAM_B1) * gv
        v_new = ADAM_B2 * vv + (1.0 - ADAM_B2) * (gv * gv)
        m_hat = m_new / c1
        v_hat = v_new / c2
        delta = -ADAM_LR * (m_hat / (jnp.sqrt(v_hat) + ADAM_EPS) + ADAM_WD * wv)
        return (delta, m_new, v_new), ()

    flat = [a.reshape(-1, c) for a in (w, g, m, v)]
    outs = _rowwise(fn, flat, [], [(c, F32)] * 3, [], name=name, tb_cap=512)
    return tuple(o.reshape(shape) for o in outs)


def _train_step(x, target, weights, m_state, v_state):
    me = _block_id(lax.axis_index("x"), lax.axis_index("y"), lax.axis_index("c"))

    big_shapes = [_shard_shape(n) for n in BIG]
    small_shapes = [_shard_shape(n) for n in SMALL_SHARDED]
    big_all = _all_gather(_pack_rows([weights[n] for n in BIG], BIG_WIDTH, BF16_ROWS, MXU_DTYPE), name="gather_big")
    small_all = _all_gather(_pack_rows([weights[n] for n in SMALL_SHARDED], SMALL_WIDTH, SUBLANES, F32),
                            name="gather_small")
    full = {}
    for n, stacked in zip(BIG, _unpack_rows(big_all, big_shapes, BIG_WIDTH, BF16_ROWS)):
        full[n] = _merge_shards(stacked, PARAMS[n][1])
    for n, stacked in zip(SMALL_SHARDED, _unpack_rows(small_all, small_shapes, SMALL_WIDTH, SUBLANES)):
        full[n] = _merge_shards(stacked, PARAMS[n][1])
    for n in SMALL:
        if PARAMS[n][1] is None:
            full[n] = weights[n]
    for n in ("mla_w_uk", "mla_w_uv"):
        full[n] = full[n].reshape(1, KV_LORA, MLA_HEADS * QK_NOPE)

    loss_local, grad_x, grads = _local_step(x[0], target[0], full)
    loss = lax.psum(loss_local, MESH_AXES)
    full_grads = {n: jnp.stack(grads[n]).reshape(PARAMS[n][0]) for n in NAMES}

    big_parts = _pack_rows([_split_shards(full_grads[n], PARAMS[n][1]) for n in BIG], BIG_WIDTH, BF16_ROWS, MXU_DTYPE,
                           lead=1)
    big_sum = _reduce_scatter(big_parts, tag="reduce_big")
    small_mine = _pack_rows([full_grads[n] for n in SMALL], SMALL_WIDTH, SUBLANES, F32)
    small_sum = _sum_blocks(None, _all_gather(small_mine, name="gather_small_grads"), name="sum_small_grads")
    g_shard = {}
    for n, g in zip(BIG, _unpack_rows(big_sum, big_shapes, BIG_WIDTH, BF16_ROWS)):
        g_shard[n] = g
    for n, g in zip(SMALL, _unpack_rows(small_sum, [PARAMS[n][0] for n in SMALL], SMALL_WIDTH, SUBLANES)):
        ax = PARAMS[n][1]
        g_shard[n] = g if ax is None else lax.dynamic_index_in_dim(_split_shards(g, ax), me, axis=0, keepdims=False)

    delta, new_m, new_v = {}, {}, {}
    for n in NAMES:
        delta[n], new_m[n], new_v[n] = _adamw(weights[n], g_shard[n], m_state[n], v_state[n], name=f"adamw_{n}")
    return (loss, grad_x[None], *[g_shard[n] for n in NAMES], *[delta[n] for n in NAMES],
            *[new_m[n] for n in NAMES], *[new_v[n] for n in NAMES])


def kernel(x, sc_w_in, sc_conv_w, sc_w_out, mla_w_dq, mla_g_q, mla_w_uq, mla_w_dkv, mla_g_kv, mla_w_uk, mla_w_uv, mla_w_o, cf_w_pw1, cf_b_pw1, cf_dw_w, cf_dw_b, cf_norm_g, cf_norm_b, cf_w_pw2, cf_b_pw2, ff_w1, ff_w2, ln_mix_g, ln_mix_b, ln_ff_g, ln_ff_b, loss_target, m_sc_w_in, m_sc_conv_w, m_sc_w_out, m_mla_w_dq, m_mla_g_q, m_mla_w_uq, m_mla_w_dkv, m_mla_g_kv, m_mla_w_uk, m_mla_w_uv, m_mla_w_o, m_cf_w_pw1, m_cf_b_pw1, m_cf_dw_w, m_cf_dw_b, m_cf_norm_g, m_cf_norm_b, m_cf_w_pw2, m_cf_b_pw2, m_ff_w1, m_ff_w2, m_ln_mix_g, m_ln_mix_b, m_ln_ff_g, m_ln_ff_b, v_sc_w_in, v_sc_conv_w, v_sc_w_out, v_mla_w_dq, v_mla_g_q, v_mla_w_uq, v_mla_w_dkv, v_mla_g_kv, v_mla_w_uk, v_mla_w_uv, v_mla_w_o, v_cf_w_pw1, v_cf_b_pw1, v_cf_dw_w, v_cf_dw_b, v_cf_norm_g, v_cf_norm_b, v_cf_w_pw2, v_cf_b_pw2, v_ff_w1, v_ff_w2, v_ln_mix_g, v_ln_mix_b, v_ln_ff_g, v_ln_ff_b):
    w_list = (sc_w_in, sc_conv_w, sc_w_out, mla_w_dq, mla_g_q, mla_w_uq, mla_w_dkv, mla_g_kv, mla_w_uk, mla_w_uv, mla_w_o,
              cf_w_pw1, cf_b_pw1, cf_dw_w, cf_dw_b, cf_norm_g, cf_norm_b, cf_w_pw2, cf_b_pw2, ff_w1, ff_w2, ln_mix_g,
              ln_mix_b, ln_ff_g, ln_ff_b)
    m_list = (m_sc_w_in, m_sc_conv_w, m_sc_w_out, m_mla_w_dq, m_mla_g_q, m_mla_w_uq, m_mla_w_dkv, m_mla_g_kv, m_mla_w_uk,
              m_mla_w_uv, m_mla_w_o, m_cf_w_pw1, m_cf_b_pw1, m_cf_dw_w, m_cf_dw_b, m_cf_norm_g, m_cf_norm_b, m_cf_w_pw2,
              m_cf_b_pw2, m_ff_w1, m_ff_w2, m_ln_mix_g, m_ln_mix_b, m_ln_ff_g, m_ln_ff_b)
    v_list = (v_sc_w_in, v_sc_conv_w, v_sc_w_out, v_mla_w_dq, v_mla_g_q, v_mla_w_uq, v_mla_w_dkv, v_mla_g_kv, v_mla_w_uk,
              v_mla_w_uv, v_mla_w_o, v_cf_w_pw1, v_cf_b_pw1, v_cf_dw_w, v_cf_dw_b, v_cf_norm_g, v_cf_norm_b, v_cf_w_pw2,
              v_cf_b_pw2, v_ff_w1, v_ff_w2, v_ln_mix_g, v_ln_mix_b, v_ln_ff_g, v_ln_ff_b)
    return _train_step(x, loss_target, dict(zip(NAMES, w_list)), dict(zip(NAMES, m_list)), dict(zip(NAMES, v_list)))
```

```python
import functools

import jax
import jax.numpy as jnp
from jax import lax
from jax.experimental import pallas as pl
from jax.experimental.pallas import tpu as pltpu

F32 = jnp.float32
MXU_DTYPE = jnp.bfloat16

N_DEV = 8
D_MODEL = 1024
DEPTH = 4
N_MIXERS = 3
CHUNK = 64
ALPHA = (2.0 * DEPTH) ** 0.25
LN_EPS = 1e-5
RMS_EPS = 1e-6
MLA_HEADS = 8
QK_NOPE = 128
QK_ROPE = 64
V_HEAD = 128
Q_LORA = 384
KV_LORA = 256
ROPE_THETA = 10000.0
ADAM_LR = 0.001
ADAM_B1 = 0.9
ADAM_B2 = 0.999
ADAM_EPS = 1e-08
ADAM_WD = 0.01
ADAM_STEP = 10

LANES = 128
SUBLANES = 8
VMEM_LIMIT = 56 * 1024 * 1024
CONV_PAD = 32
MASK_VALUE = -1e30

MESH_AXES = ("x", "y", "c")


def _params(semantics):
    return pltpu.CompilerParams(dimension_semantics=semantics, vmem_limit_bytes=VMEM_LIMIT)


def _tile(n, cap):
    if n <= cap:
        return n
    t = cap - cap % LANES
    while t >= LANES:
        if n % t == 0:
            return t
        t -= LANES
    raise ValueError(f"no tile for {n} under {cap}")


def _row_tile(rows, cap):
    if rows <= cap:
        return rows
    t = cap - cap % SUBLANES
    while t >= SUBLANES:
        if rows % t == 0:
            return t
        t -= SUBLANES
    raise ValueError(f"no row tile for {rows} under {cap}")


def _mm(a, b, *, name, ta=False, tb=False, b_dev=False, out_dev=False, out_dtypes=(F32,), epilogue=None,
        tile_extras=(), col_extras=(), tm_cap=1024, tn_cap=1024, tk_cap=1024):
    m, k = (a.shape[1], a.shape[0]) if ta else a.shape
    if b_dev:
        nd, b_rows, ns = b.shape
        n, kb = (b_rows, nd * ns) if tb else (nd * ns, b_rows)
    else:
        n, kb = (b.shape[0], b.shape[1]) if tb else (b.shape[1], b.shape[0])
    assert k == kb, (a.shape, b.shape, ta, tb)
    tm, tn, tk = _tile(m, tm_cap), _tile(n, tn_cap), _tile(k, tk_cap)
    if b_dev and tb:
        tk = ns
    elif b_dev:
        tn = ns
    if out_dev:
        assert not tile_extras and not col_extras and n % N_DEV == 0
        tn = n // N_DEV
    nk = k // tk
    n_te, n_ce, n_out = len(tile_extras), len(col_extras), len(out_dtypes)
    dims = (((0 if ta else 1,), (1 if tb else 0,)), ((), ()))

    def body(*refs):
        a_ref, b_ref = refs[0], refs[1]
        te_refs = refs[2:2 + n_te]
        ce_refs = refs[2 + n_te:2 + n_te + n_ce]
        out_refs = refs[2 + n_te + n_ce:2 + n_te + n_ce + n_out]

        def finish(acc):
            if epilogue is None:
                outs = (acc,)
            else:
                outs = epilogue(acc, *[r[...] for r in te_refs], *[r[...] for r in ce_refs])
            for o_ref, o in zip(out_refs, outs):
                o_ref[...] = o.astype(o_ref.dtype)

        part = lax.dot_general(a_ref[...].astype(MXU_DTYPE), b_ref[...].astype(MXU_DTYPE), dims,
                               preferred_element_type=F32)
        if nk == 1:
            finish(part)
        else:
            acc_ref = refs[-1]
            kk = pl.program_id(2)

            @pl.when(kk == 0)
            def _():
                acc_ref[...] = part

            @pl.when(jnp.logical_and(kk > 0, kk < nk - 1))
            def _():
                acc_ref[...] += part

            @pl.when(kk == nk - 1)
            def _():
                finish(acc_ref[...] + part)

    a_spec = pl.BlockSpec((tk, tm), lambda i, j, kk: (kk, i)) if ta else pl.BlockSpec((tm, tk), lambda i, j, kk: (i, kk))
    if b_dev and tb:
        b_spec = pl.BlockSpec((None, tn, tk), lambda i, j, kk: (kk, j, 0))
    elif b_dev:
        b_spec = pl.BlockSpec((None, tk, tn), lambda i, j, kk: (j, kk, 0))
    elif tb:
        b_spec = pl.BlockSpec((tn, tk), lambda i, j, kk: (j, kk))
    else:
        b_spec = pl.BlockSpec((tk, tn), lambda i, j, kk: (kk, j))
    tile_spec = pl.BlockSpec((tm, tn), lambda i, j, kk: (i, j))
    col_spec = pl.BlockSpec((1, tn), lambda i, j, kk: (0, j))
    if out_dev:
        out_spec = pl.BlockSpec((None, tm, tn), lambda i, j, kk: (j, i, 0))
        out_shape = [jax.ShapeDtypeStruct((N_DEV, m, tn), dt) for dt in out_dtypes]
    else:
        out_spec = tile_spec
        out_shape = [jax.ShapeDtypeStruct((m, n), dt) for dt in out_dtypes]
    outs = pl.pallas_call(
        body,
        name=name,
        grid=(m // tm, n // tn, nk),
        in_specs=[a_spec, b_spec] + [tile_spec] * n_te + [col_spec] * n_ce,
        out_specs=[out_spec] * n_out,
        out_shape=out_shape,
        scratch_shapes=[pltpu.VMEM((tm, tn), F32)] if nk > 1 else [],
        compiler_params=_params(("parallel", "parallel", "arbitrary")),
    )(a, b, *tile_extras, *col_extras)
    return outs[0] if n_out == 1 else tuple(outs)


def _rowwise(fn, rows, bcast, out_rows, out_accs, *, name, tb_cap=256):
    t = rows[0].shape[0]
    tb = _row_tile(t, tb_cap)
    n_r, n_b, n_o, n_a = len(rows), len(bcast), len(out_rows), len(out_accs)

    def body(*refs):
        r_refs = refs[:n_r]
        b_refs = refs[n_r:n_r + n_b]
        o_refs = refs[n_r + n_b:n_r + n_b + n_o]
        a_refs = refs[n_r + n_b + n_o:]
        outs, accs = fn([r[...] for r in r_refs], [r[...] for r in b_refs])
        for o_ref, o in zip(o_refs, outs):
            o_ref[...] = o.astype(o_ref.dtype)
        if n_a:
            i = pl.program_id(0)

            @pl.when(i == 0)
            def _():
                for a_ref, acc in zip(a_refs, accs):
                    a_ref[...] = acc

            @pl.when(i > 0)
            def _():
                for a_ref, acc in zip(a_refs, accs):
                    a_ref[...] += acc

    in_specs = [pl.BlockSpec((tb, r.shape[1]), lambda i: (i, 0)) for r in rows]
    in_specs += [pl.BlockSpec(b.shape, lambda i: (0, 0)) for b in bcast]
    out_specs = [pl.BlockSpec((tb, c), lambda i: (i, 0)) for c, _ in out_rows]
    out_specs += [pl.BlockSpec(s, lambda i: (0, 0)) for s in out_accs]
    out_shape = [jax.ShapeDtypeStruct((t, c), dt) for c, dt in out_rows]
    out_shape += [jax.ShapeDtypeStruct(s, F32) for s in out_accs]
    outs = pl.pallas_call(
        body,
        name=name,
        grid=(t // tb,),
        in_specs=in_specs,
        out_specs=out_specs,
        out_shape=out_shape,
        compiler_params=_params(("arbitrary",)),
    )(*rows, *bcast)
    return tuple(outs)


def _colsum(v):
    return jnp.sum(v, axis=0, keepdims=True)


def _sigmoid(v):
    return 1.0 / (1.0 + jnp.exp(-v))


def _ln_fwd(x_res, y_sub, g, b, *, name):
    d = x_res.shape[1]

    def fn(rows, bc):
        z = ALPHA * rows[0] + rows[1]
        mu = jnp.mean(z, axis=-1, keepdims=True)
        zc = z - mu
        var = jnp.mean(zc * zc, axis=-1, keepdims=True)
        rstd = lax.rsqrt(var + LN_EPS)
        xhat = zc * rstd
        out = xhat * bc[0] + bc[1]
        return (out, out, xhat, rstd), ()

    return _rowwise(fn, [x_res, y_sub], [g, b], [(d, F32), (d, MXU_DTYPE), (d, F32), (1, F32)], [], name=name)


def _ln_bwd(terms, xhat, rstd, g, *, name):
    d = xhat.shape[1]
    coefs = [c for c, _ in terms]

    def fn(rows, bc):
        xh, rs = rows[0], rows[1]
        dout = None
        for c, v in zip(coefs, rows[2:]):
            tv = v if c == 1.0 else c * v
            dout = tv if dout is None else dout + tv
        dxh = dout * bc[0]
        m1 = jnp.mean(dxh, axis=-1, keepdims=True)
        m2 = jnp.mean(dxh * xh, axis=-1, keepdims=True)
        dz = rs * (dxh - m1 - xh * m2)
        return (dz, dz), (_colsum(dout * xh), _colsum(dout), _colsum(dz))

    return _rowwise(fn, [xhat, rstd] + [v for _, v in terms], [g], [(d, F32), (d, MXU_DTYPE)],
                    [(1, d), (1, d), (1, d)], name=name)


def _rms_fwd(x, g, *, name):
    c = x.shape[1]

    def fn(rows, bc):
        v = rows[0]
        r = lax.rsqrt(jnp.mean(v * v, axis=-1, keepdims=True) + RMS_EPS)
        return (v * r * bc[0],), ()

    return _rowwise(fn, [x], [g], [(c, MXU_DTYPE)], [], name=name)[0]


def _rms_bwd(x, g, dy, *, name):
    c = x.shape[1]

    def fn(rows, bc):
        v, dyv = rows
        r = lax.rsqrt(jnp.mean(v * v, axis=-1, keepdims=True) + RMS_EPS)
        xh = v * r
        dxh = dyv * bc[0]
        dx = r * (dxh - xh * jnp.mean(dxh * xh, axis=-1, keepdims=True))
        return (dx,), (_colsum(dyv * xh),)

    return _rowwise(fn, [x, dy], [g], [(c, F32)], [(1, c)], name=name)


def _conv_chunk(t):
    return _row_tile(t, 256)


def _conv_fwd(x, w, bias, *, name):
    t, c = x.shape
    kw = w.shape[0]
    ch = _conv_chunk(t)

    def body(x_ref, w_ref, b_ref, o_ref, pad_ref):
        pad_ref[0:CONV_PAD, :] = jnp.zeros((CONV_PAD, LANES), F32)
        pad_ref[CONV_PAD:CONV_PAD + t, :] = x_ref[...]
        for t0 in range(0, t, ch):
            acc = jnp.broadcast_to(b_ref[...], (ch, LANES))
            for k in range(kw):
                s = kw - 1 - k
                acc = acc + w_ref[k:k + 1, :] * pad_ref[CONV_PAD - s + t0:CONV_PAD - s + t0 + ch, :]
            o_ref[t0:t0 + ch, :] = acc

    return pl.pallas_call(
        body,
        name=name,
        grid=(c // LANES,),
        in_specs=[pl.BlockSpec((t, LANES), lambda j: (0, j)), pl.BlockSpec((kw, LANES), lambda j: (0, j)),
                  pl.BlockSpec((1, LANES), lambda j: (0, j))],
        out_specs=pl.BlockSpec((t, LANES), lambda j: (0, j)),
        out_shape=jax.ShapeDtypeStruct((t, c), F32),
        scratch_shapes=[pltpu.VMEM((CONV_PAD + t, LANES), F32)],
        compiler_params=_params(("parallel",)),
    )(x, w, bias)


def _conv_bwd(dy, x, w, *, name):
    t, c = x.shape
    kw = w.shape[0]
    ch = _conv_chunk(t)

    def body(dy_ref, x_ref, w_ref, dx_ref, dw_ref, xpad_ref, dpad_ref):
        xpad_ref[0:CONV_PAD, :] = jnp.zeros((CONV_PAD, LANES), F32)
        xpad_ref[CONV_PAD:CONV_PAD + t, :] = x_ref[...]
        dpad_ref[0:t, :] = dy_ref[...]
        dpad_ref[t:t + CONV_PAD, :] = jnp.zeros((CONV_PAD, LANES), F32)
        for t0 in range(0, t, ch):
            acc = jnp.zeros((ch, LANES), F32)
            for k in range(kw):
                s = kw - 1 - k
                acc = acc + w_ref[k:k + 1, :] * dpad_ref[t0 + s:t0 + s + ch, :]
            dx_ref[t0:t0 + ch, :] = acc
        for k in range(kw):
            s = kw - 1 - k
            acc = jnp.zeros((ch, LANES), F32)
            for t0 in range(0, t, ch):
                acc = acc + dy_ref[t0:t0 + ch, :] * xpad_ref[CONV_PAD - s + t0:CONV_PAD - s + t0 + ch, :]
            dw_ref[k:k + 1, :] = _colsum(acc)

    blk = pl.BlockSpec((t, LANES), lambda j: (0, j))
    wblk = pl.BlockSpec((kw, LANES), lambda j: (0, j))
    return pl.pallas_call(
        body,
        name=name,
        grid=(c // LANES,),
        in_specs=[blk, blk, wblk],
        out_specs=[blk, wblk],
        out_shape=[jax.ShapeDtypeStruct((t, c), F32), jax.ShapeDtypeStruct((kw, c), F32)],
        scratch_shapes=[pltpu.VMEM((CONV_PAD + t, LANES), F32), pltpu.VMEM((CONV_PAD + t, LANES), F32)],
        compiler_params=_params(("parallel",)),
    )(dy, x, w)


ATTN_SCALE = (QK_NOPE + QK_ROPE) ** -0.5
_NT = (((1,), (1,)), ((), ()))
_TN = (((0,), (0,)), ((), ()))
_NN = (((1,), (0,)), ((), ()))


def _dot(a, b, dims):
    return lax.dot_general(a, b, dims, preferred_element_type=F32)


def _scores(qn_ref, qp_ref, kn_ref, kp_ref, qi, tq, t):
    s = _dot(qn_ref[...], kn_ref[...], _NT) + _dot(qp_ref[...], kp_ref[...], _NT)
    s = s * ATTN_SCALE
    rows = qi * tq + lax.broadcasted_iota(jnp.int32, (tq, t), 0)
    cols = lax.broadcasted_iota(jnp.int32, (tq, t), 1)
    shift = CHUNK.bit_length() - 1
    allowed = jnp.right_shift(cols, shift) <= jnp.right_shift(rows, shift)
    return jnp.where(allowed, s, MASK_VALUE)


def _attn_fwd(qn, qp, kn, kp, v, *, name):
    h, t, _ = qn.shape
    tq = _row_tile(t, 256)

    def body(qn_ref, qp_ref, kn_ref, kp_ref, v_ref, o_ref, lse_ref):
        s = _scores(qn_ref, qp_ref, kn_ref, kp_ref, pl.program_id(1), tq, t)
        m = jnp.max(s, axis=-1, keepdims=True)
        p = jnp.exp(s - m)
        l = jnp.sum(p, axis=-1, keepdims=True)
        o = _dot((p / l).astype(MXU_DTYPE), v_ref[...], _NN)
        o_ref[...] = o.astype(o_ref.dtype)
        lse_ref[...] = m + jnp.log(l)

    qblk = lambda w: pl.BlockSpec((None, tq, w), lambda hh, qi: (hh, qi, 0))
    kblk = lambda w: pl.BlockSpec((None, t, w), lambda hh, qi: (hh, 0, 0))
    return pl.pallas_call(
        body,
        name=name,
        grid=(h, t // tq),
        in_specs=[qblk(QK_NOPE), qblk(QK_ROPE), kblk(QK_NOPE), pl.BlockSpec((t, QK_ROPE), lambda hh, qi: (0, 0)),
                  kblk(V_HEAD)],
        out_specs=[qblk(V_HEAD), qblk(1)],
        out_shape=[jax.ShapeDtypeStruct((h, t, V_HEAD), MXU_DTYPE), jax.ShapeDtypeStruct((h, t, 1), F32)],
        compiler_params=_params(("parallel", "parallel")),
    )(qn, qp, kn, kp, v)


def _attn_bwd(qn, qp, kn, kp, v, do, lse, *, name):
    h, t, _ = qn.shape
    tq = _row_tile(t, 256)

    def body(qn_ref, qp_ref, kn_ref, kp_ref, v_ref, do_ref, lse_ref, dqn_ref, dqp_ref, dkn_ref, dkp_ref, dv_ref):
        hh, qi = pl.program_id(0), pl.program_id(1)
        s = _scores(qn_ref, qp_ref, kn_ref, kp_ref, qi, tq, t)
        p = jnp.exp(s - lse_ref[...])
        dov = do_ref[...]
        dp = _dot(dov, v_ref[...], _NT)
        ds = p * (dp - jnp.sum(dp * p, axis=-1, keepdims=True)) * ATTN_SCALE
        ds_m = ds.astype(MXU_DTYPE)
        p_m = p.astype(MXU_DTYPE)
        dqn_ref[...] = _dot(ds_m, kn_ref[...], _NN)
        dqp_ref[...] = _dot(ds_m, kp_ref[...], _NN)
        dkn_part = _dot(ds_m, qn_ref[...], _TN)
        dkp_part = _dot(ds_m, qp_ref[...], _TN)
        dv_part = _dot(p_m, dov, _TN)

        @pl.when(qi == 0)
        def _():
            dkn_ref[...] = dkn_part
            dv_ref[...] = dv_part

        @pl.when(qi > 0)
        def _():
            dkn_ref[...] += dkn_part
            dv_ref[...] += dv_part

        first = jnp.logical_and(hh == 0, qi == 0)

        @pl.when(first)
        def _():
            dkp_ref[...] = dkp_part

        @pl.when(jnp.logical_not(first))
        def _():
            dkp_ref[...] += dkp_part

    qblk = lambda w: pl.BlockSpec((None, tq, w), lambda hh, qi: (hh, qi, 0))
    kblk = lambda w: pl.BlockSpec((None, t, w), lambda hh, qi: (hh, 0, 0))
    kpblk = pl.BlockSpec((t, QK_ROPE), lambda hh, qi: (0, 0))
    return pl.pallas_call(
        body,
        name=name,
        grid=(h, t // tq),
        in_specs=[qblk(QK_NOPE), qblk(QK_ROPE), kblk(QK_NOPE), kpblk, kblk(V_HEAD), qblk(V_HEAD), qblk(1)],
        out_specs=[qblk(QK_NOPE), qblk(QK_ROPE), kblk(QK_NOPE), kpblk, kblk(V_HEAD)],
        out_shape=[jax.ShapeDtypeStruct((h, t, QK_NOPE), F32), jax.ShapeDtypeStruct((h, t, QK_ROPE), F32),
                   jax.ShapeDtypeStruct((h, t, QK_NOPE), F32), jax.ShapeDtypeStruct((t, QK_ROPE), F32),
                   jax.ShapeDtypeStruct((h, t, V_HEAD), F32)],
        compiler_params=_params(("arbitrary", "arbitrary")),
    )(qn, qp, kn, kp, v, do, lse)


def _rope(x1, x2, cos, sin, *, name, inverse=False):
    w = x1.shape[1]
    sign = -1.0 if inverse else 1.0

    def fn(rows, bc):
        a, b, c, s = rows
        s = sign * s
        return (a * c - b * s, a * s + b * c), ()

    return _rowwise(fn, [x1, x2, cos, sin], [], [(w, F32), (w, F32)], [], name=name)


def _mlp_fwd(x_m, w1, w2, tag):
    def epi(acc):
        r = jnp.maximum(acc, 0.0)
        return acc, r * r

    hid, act = _mm(x_m, w1, b_dev=True, name=f"{tag}_up", out_dtypes=(F32, MXU_DTYPE), epilogue=epi)
    y = _mm(act, w2, name=f"{tag}_down")
    return y, (x_m, hid, act)


def _mlp_bwd(dy_m, saved, w1, w2, tag):
    x_m, hid, act = saved
    dw2 = _mm(act, dy_m, ta=True, name=f"{tag}_dw2", out_dtypes=(MXU_DTYPE,))

    def epi(acc, hv):
        return (acc * (2.0 * jnp.maximum(hv, 0.0)),)

    dh = _mm(dy_m, w2, tb=True, name=f"{tag}_dact", out_dtypes=(MXU_DTYPE,), epilogue=epi, tile_extras=(hid,))
    dw1 = _mm(x_m, dh, ta=True, out_dev=True, name=f"{tag}_dw1", out_dtypes=(MXU_DTYPE,))
    dx = _mm(dh, w1, tb=True, b_dev=True, name=f"{tag}_dx")
    return [dx], {"ff_w1": dw1, "ff_w2": dw2}


def _sc_fwd(x_m, w_in, conv_w, w_out, tag):
    d = D_MODEL
    u = _mm(x_m, w_in, b_dev=True, name=f"{tag}_in")

    def gate_fn(rows, bc):
        uv = rows[0]
        return (uv[:, d:2 * d] * uv[:, 2 * d:3 * d],), ()

    p = _rowwise(gate_fn, [u], [], [(d, F32)], [], name=f"{tag}_ch")[0]
    q = _conv_fwd(p, conv_w, jnp.zeros((1, d), F32), name=f"{tag}_conv")

    def out_fn(rows, bc):
        return (rows[0][:, 0:d] * rows[1],), ()

    r = _rowwise(out_fn, [u, q], [], [(d, MXU_DTYPE)], [], name=f"{tag}_bq")[0]
    y = _mm(r, w_out, name=f"{tag}_out")
    return y, (x_m, u, p, q, r)


def _sc_bwd(dy_m, saved, w_in, conv_w, w_out, tag):
    d = D_MODEL
    x_m, u, p, q, r = saved
    dw_out = _mm(r, dy_m, ta=True, name=f"{tag}_dwout", out_dtypes=(MXU_DTYPE,))
    dr = _mm(dy_m, w_out, tb=True, name=f"{tag}_dr")

    def dq_fn(rows, bc):
        return (rows[0] * rows[1][:, 0:d],), ()

    dq = _rowwise(dq_fn, [dr, u], [], [(d, F32)], [], name=f"{tag}_dq")[0]
    dp, dconv = _conv_bwd(dq, p, conv_w, name=f"{tag}_dconv")

    def du_fn(rows, bc):
        drv, qv, dpv, uv = rows
        return (jnp.concatenate([drv * qv, dpv * uv[:, 2 * d:3 * d], dpv * uv[:, d:2 * d]], axis=1),), ()

    du = _rowwise(du_fn, [dr, q, dp, u], [], [(3 * d, MXU_DTYPE)], [], name=f"{tag}_du")[0]
    dw_in = _mm(x_m, du, ta=True, out_dev=True, name=f"{tag}_dwin", out_dtypes=(MXU_DTYPE,))
    dx = _mm(du, w_in, tb=True, b_dev=True, name=f"{tag}_dx")
    return [dx], {"sc_w_in": dw_in, "sc_conv_w": dconv, "sc_w_out": dw_out}


def _cf_fwd(x_m, w, tag):
    d = D_MODEL

    def bias_epi(acc, bias):
        return (acc + bias,)

    u = _mm(x_m, w["cf_w_pw1"], b_dev=True, name=f"{tag}_pw1", epilogue=bias_epi, col_extras=(w["cf_b_pw1"],))

    def glu_fn(rows, bc):
        uv = rows[0]
        return (uv[:, 0:d] * _sigmoid(uv[:, d:2 * d]),), ()

    h1 = _rowwise(glu_fn, [u], [], [(d, F32)], [], name=f"{tag}_glu")[0]
    h2 = _conv_fwd(h1, w["cf_dw_w"], w["cf_dw_b"], name=f"{tag}_conv")

    def norm_fn(rows, bc):
        hv = rows[0]
        mu = jnp.mean(hv, axis=-1, keepdims=True)
        hc = hv - mu
        rstd = lax.rsqrt(jnp.mean(hc * hc, axis=-1, keepdims=True) + LN_EPS)
        h3 = hc * rstd * bc[0] + bc[1]
        return (h3 * _sigmoid(h3),), ()

    h4 = _rowwise(norm_fn, [h2], [w["cf_norm_g"], w["cf_norm_b"]], [(d, MXU_DTYPE)], [], name=f"{tag}_norm")[0]
    y = _mm(h4, w["cf_w_pw2"], name=f"{tag}_pw2", epilogue=bias_epi, col_extras=(w["cf_b_pw2"],))
    return y, (x_m, u, h1, h2, h4)


def _cf_bwd(dy_m, dy_colsum, saved, w, tag):
    d = D_MODEL
    x_m, u, h1, h2, h4 = saved
    dw_pw2 = _mm(h4, dy_m, ta=True, name=f"{tag}_dwpw2", out_dtypes=(MXU_DTYPE,))
    dh4 = _mm(dy_m, w["cf_w_pw2"], tb=True, name=f"{tag}_dh4")

    def dnorm_fn(rows, bc):
        dh4v, hv = rows
        g, b = bc
        mu = jnp.mean(hv, axis=-1, keepdims=True)
        hc = hv - mu
        rstd = lax.rsqrt(jnp.mean(hc * hc, axis=-1, keepdims=True) + LN_EPS)
        xh = hc * rstd
        h3 = xh * g + b
        sg = _sigmoid(h3)
        dh3 = dh4v * (sg * (1.0 + h3 * (1.0 - sg)))
        dxh = dh3 * g
        dh2 = rstd * (dxh - jnp.mean(dxh, axis=-1, keepdims=True) - xh * jnp.mean(dxh * xh, axis=-1, keepdims=True))
        return (dh2,), (_colsum(dh3 * xh), _colsum(dh3), _colsum(dh2))

    dh2, dnorm_g, dnorm_b, ddw_b = _rowwise(dnorm_fn, [dh4, h2], [w["cf_norm_g"], w["cf_norm_b"]], [(d, F32)],
                                            [(1, d), (1, d), (1, d)], name=f"{tag}_dnorm")
    dh1, ddw_w = _conv_bwd(dh2, h1, w["cf_dw_w"], name=f"{tag}_dconv")

    def dglu_fn(rows, bc):
        dh1v, uv = rows
        a, gate = uv[:, 0:d], uv[:, d:2 * d]
        sg = _sigmoid(gate)
        du = jnp.concatenate([dh1v * sg, dh1v * a * sg * (1.0 - sg)], axis=1)
        return (du,), (_colsum(du),)

    du, db_pw1 = _rowwise(dglu_fn, [dh1, u], [], [(2 * d, MXU_DTYPE)], [(1, 2 * d)], name=f"{tag}_dglu")
    dw_pw1 = _mm(x_m, du, ta=True, out_dev=True, name=f"{tag}_dwpw1", out_dtypes=(MXU_DTYPE,))
    dx = _mm(du, w["cf_w_pw1"], tb=True, b_dev=True, name=f"{tag}_dx")
    grads = {"cf_w_pw1": dw_pw1, "cf_b_pw1": db_pw1, "cf_dw_w": ddw_w, "cf_dw_b": ddw_b, "cf_norm_g": dnorm_g,
             "cf_norm_b": dnorm_b, "cf_w_pw2": dw_pw2, "cf_b_pw2": dy_colsum}
    return [dx], grads


def _rope_tables(t):
    pos = jnp.arange(t, dtype=F32)
    inv_freq = ROPE_THETA ** (-jnp.arange(0, QK_ROPE, 2, dtype=F32) / QK_ROPE)
    ang = pos[:, None] * inv_freq[None, :]
    return jnp.cos(ang), jnp.sin(ang)


def _heads(a, width):
    return a.reshape(a.shape[0], MLA_HEADS, width).transpose(1, 0, 2)


def _unheads(a):
    return a.transpose(1, 0, 2).reshape(a.shape[1], -1)


def _mla_fwd(x_m, w, tag):
    t = x_m.shape[0]
    half = QK_ROPE // 2
    cos, sin = _rope_tables(t)
    cos_h, sin_h = jnp.tile(cos, (1, MLA_HEADS)), jnp.tile(sin, (1, MLA_HEADS))
    cq_pre = _mm(x_m, w["mla_w_dq"], name=f"{tag}_dq")
    cq = _rms_fwd(cq_pre, w["mla_g_q"], name=f"{tag}_qnorm")
    q = _mm(cq, w["mla_w_uq"], name=f"{tag}_uq").reshape(t, MLA_HEADS, QK_NOPE + QK_ROPE)
    q1 = q[:, :, QK_NOPE:QK_NOPE + half].reshape(t, MLA_HEADS * half)
    q2 = q[:, :, QK_NOPE + half:].reshape(t, MLA_HEADS * half)
    r1, r2 = _rope(q1, q2, cos_h, sin_h, name=f"{tag}_qrope")
    qp = jnp.concatenate([r1.reshape(t, MLA_HEADS, half), r2.reshape(t, MLA_HEADS, half)], axis=-1)
    qn_h = q[:, :, :QK_NOPE].transpose(1, 0, 2).astype(MXU_DTYPE)
    qp_h = qp.transpose(1, 0, 2).astype(MXU_DTYPE)
    ckv_full = _mm(x_m, w["mla_w_dkv"], name=f"{tag}_dkv")
    ckv_pre = ckv_full[:, :KV_LORA]
    ckv = _rms_fwd(ckv_pre, w["mla_g_kv"], name=f"{tag}_kvnorm")
    k1, k2 = _rope(ckv_full[:, KV_LORA:KV_LORA + half], ckv_full[:, KV_LORA + half:], cos, sin, name=f"{tag}_krope")
    kp = jnp.concatenate([k1, k2], axis=-1).astype(MXU_DTYPE)
    kn_h = _heads(_mm(ckv, w["mla_w_uk"], name=f"{tag}_uk", out_dtypes=(MXU_DTYPE,)), QK_NOPE)
    v_h = _heads(_mm(ckv, w["mla_w_uv"], name=f"{tag}_uv", out_dtypes=(MXU_DTYPE,)), V_HEAD)
    o_h, lse = _attn_fwd(qn_h, qp_h, kn_h, kp, v_h, name=f"{tag}_attn")
    o_flat = _unheads(o_h)
    y = _mm(o_flat, w["mla_w_o"], name=f"{tag}_o")
    return y, (x_m, cq_pre, cq, ckv_pre, ckv, qn_h, qp_h, kn_h, kp, v_h, lse, o_flat, (cos, sin, cos_h, sin_h))


def _mla_bwd(dy_m, saved, w, tag):
    x_m, cq_pre, cq, ckv_pre, ckv, qn_h, qp_h, kn_h, kp, v_h, lse, o_flat, (cos, sin, cos_h, sin_h) = saved
    t = x_m.shape[0]
    half = QK_ROPE // 2
    dw_o = _mm(o_flat, dy_m, ta=True, name=f"{tag}_dwo", out_dtypes=(MXU_DTYPE,))
    do_h = _heads(_mm(dy_m, w["mla_w_o"], tb=True, name=f"{tag}_do", out_dtypes=(MXU_DTYPE,)), V_HEAD)
    dqn_h, dqp_h, dkn_h, dkp, dv_h = _attn_bwd(qn_h, qp_h, kn_h, kp, v_h, do_h, lse, name=f"{tag}_dattn")
    dqp = dqp_h.transpose(1, 0, 2)
    d1, d2 = _rope(dqp[:, :, :half].reshape(t, -1), dqp[:, :, half:].reshape(t, -1), cos_h, sin_h,
                   name=f"{tag}_dqrope", inverse=True)
    dq = jnp.concatenate([dqn_h.transpose(1, 0, 2), d1.reshape(t, MLA_HEADS, half), d2.reshape(t, MLA_HEADS, half)],
                         axis=-1).reshape(t, -1).astype(MXU_DTYPE)
    dw_uq = _mm(cq, dq, ta=True, name=f"{tag}_dwuq", out_dtypes=(MXU_DTYPE,))
    dcq = _mm(dq, w["mla_w_uq"], tb=True, name=f"{tag}_dcq")
    dcq_pre, dg_q = _rms_bwd(cq_pre, w["mla_g_q"], dcq, name=f"{tag}_dqnorm")
    dcq_pre_m = dcq_pre.astype(MXU_DTYPE)
    dw_dq = _mm(x_m, dcq_pre_m, ta=True, name=f"{tag}_dwdq", out_dtypes=(MXU_DTYPE,))
    dx_q = _mm(dcq_pre_m, w["mla_w_dq"], tb=True, name=f"{tag}_dxq")
    dkn = _unheads(dkn_h).astype(MXU_DTYPE)
    dv = _unheads(dv_h).astype(MXU_DTYPE)
    dw_uk = _mm(ckv, dkn, ta=True, name=f"{tag}_dwuk", out_dtypes=(MXU_DTYPE,))
    dw_uv = _mm(ckv, dv, ta=True, name=f"{tag}_dwuv", out_dtypes=(MXU_DTYPE,))
    dckv_k = _mm(dkn, w["mla_w_uk"], tb=True, name=f"{tag}_dckvk")
    dckv = _mm(dv, w["mla_w_uv"], tb=True, name=f"{tag}_dckv", epilogue=lambda acc, other: (acc + other,),
               tile_extras=(dckv_k,))
    dckv_pre, dg_kv = _rms_bwd(ckv_pre, w["mla_g_kv"], dckv, name=f"{tag}_dkvnorm")
    dk1, dk2 = _rope(dkp[:, :half], dkp[:, half:], cos, sin, name=f"{tag}_dkrope", inverse=True)
    dckv_full = jnp.concatenate([dckv_pre, dk1, dk2], axis=-1).astype(MXU_DTYPE)
    dw_dkv = _mm(x_m, dckv_full, ta=True, name=f"{tag}_dwdkv", out_dtypes=(MXU_DTYPE,))
    dx_kv = _mm(dckv_full, w["mla_w_dkv"], tb=True, name=f"{tag}_dxkv")
    grads = {"mla_w_dq": dw_dq, "mla_g_q": dg_q, "mla_w_uq": dw_uq, "mla_w_dkv": dw_dkv, "mla_g_kv": dg_kv,
             "mla_w_uk": dw_uk, "mla_w_uv": dw_uv, "mla_w_o": dw_o}
    return [dx_q, dx_kv], grads


def _cast_input(x):
    cast = lambda rows, bc: ((rows[0],), ())
    return _rowwise(cast, [x], [], [(D_MODEL, MXU_DTYPE)], [], name="x_cast")[0]


def _layer_fwd(i, cur, cur_m, wl):
    mixer = i % N_MIXERS
    if mixer == 0:
        y, s_mix = _sc_fwd(cur_m, wl["sc_w_in"], wl["sc_conv_w"], wl["sc_w_out"], f"l{i}_sc")
    elif mixer == 1:
        y, s_mix = _mla_fwd(cur_m, wl, f"l{i}_mla")
    else:
        y, s_mix = _cf_fwd(cur_m, wl, f"l{i}_cf")
    cur, cur_m, xh_mix, rs_mix = _ln_fwd(cur, y, wl["ln_mix_g"], wl["ln_mix_b"], name=f"l{i}_lnmix")
    y, s_ff = _mlp_fwd(cur_m, wl["ff_w1"], wl["ff_w2"], f"l{i}_ff")
    cur, cur_m, xh_ff, rs_ff = _ln_fwd(cur, y, wl["ln_ff_g"], wl["ln_ff_b"], name=f"l{i}_lnff")
    return cur, cur_m, (s_mix, xh_mix, rs_mix, s_ff, xh_ff, rs_ff)


def _loss_and_grad(out, target):
    d = D_MODEL

    def loss_fn(rows, bc):
        err = rows[0] - rows[1]
        return (err * (1.0 / d),), (_colsum(err * err),)

    dy, sq = _rowwise(loss_fn, [out, target], [], [(d, F32)], [(1, d)], name="loss")
    return (0.5 / d) * jnp.sum(sq), dy


def _layer_bwd(i, terms, saved, wl):
    mixer = i % N_MIXERS
    s_mix, xh_mix, rs_mix, s_ff, xh_ff, rs_ff = saved
    grads = {}
    dz, dz_m, grads["ln_ff_g"], grads["ln_ff_b"], _ = _ln_bwd(terms, xh_ff, rs_ff, wl["ln_ff_g"], name=f"l{i}_dlnff")
    dxs, g_ff = _mlp_bwd(dz_m, s_ff, wl["ff_w1"], wl["ff_w2"], f"l{i}_ff")
    grads.update(g_ff)
    terms = [(ALPHA, dz)] + [(1.0, v) for v in dxs]
    dz, dz_m, grads["ln_mix_g"], grads["ln_mix_b"], dz_sum = _ln_bwd(terms, xh_mix, rs_mix, wl["ln_mix_g"],
                                                                     name=f"l{i}_dlnmix")
    if mixer == 0:
        dxs, g_mix = _sc_bwd(dz_m, s_mix, wl["sc_w_in"], wl["sc_conv_w"], wl["sc_w_out"], f"l{i}_sc")
    elif mixer == 1:
        dxs, g_mix = _mla_bwd(dz_m, s_mix, wl, f"l{i}_mla")
    else:
        dxs, g_mix = _cf_bwd(dz_m, dz_sum, s_mix, wl, f"l{i}_cf")
    grads.update(g_mix)
    return [(ALPHA, dz)] + [(1.0, v) for v in dxs], grads


def _sum_terms(terms, *, name):
    coefs = [c for c, _ in terms]

    def sum_fn(rows, bc):
        acc = None
        for c, v in zip(coefs, rows):
            tv = v if c == 1.0 else c * v
            acc = tv if acc is None else acc + tv
        return (acc,), ()

    return _rowwise(sum_fn, [v for _, v in terms], [], [(D_MODEL, F32)], [], name=name)[0]


def _local_step(x, target, layers):
    cur, cur_m = x, _cast_input(x)
    saved = []
    for i in range(DEPTH):
        cur, cur_m, s = _layer_fwd(i, cur, cur_m, layers[i])
        saved.append(s)
    loss, dy = _loss_and_grad(cur, target)
    terms = [(1.0, dy)]
    grads = [None] * DEPTH
    for i in reversed(range(DEPTH)):
        terms, grads[i] = _layer_bwd(i, terms, saved[i], layers[i])
    return loss, _sum_terms(terms, name="grad_x"), grads


MESH_ID = pl.DeviceIdType.MESH
ANY = pl.BlockSpec(memory_space=pl.ANY)


def _block_id(px, py, pc):
    return 4 * px + 2 * py + pc


COPIES_PER_GATHER = 7


def _all_gather(shards, *, name):
    n = len(shards)

    def body(*refs):
        x_refs, out_refs = refs[:n], refs[n:2 * n]
        send_sems, recv_sems, local_sems = refs[2 * n:]
        x, y, core = lax.axis_index("x"), lax.axis_index("y"), lax.axis_index("c")
        me, sibling = (x, y, core), (x, y, 1 - core)
        chips = [(1 - x, y), (x, 1 - y), (1 - x, 1 - y)]

        def copy(t, k, block, to, from_input=False):
            dst = out_refs[t].at[_block_id(*block)]
            return pltpu.make_async_remote_copy(
                src_ref=x_refs[t] if from_input else dst, dst_ref=dst,
                send_sem=send_sems.at[t * COPIES_PER_GATHER + k], recv_sem=recv_sems.at[t * COPIES_PER_GATHER + k],
                device_id=to, device_id_type=MESH_ID)

        started = []
        mine = [pltpu.make_async_copy(x_refs[t], out_refs[t].at[_block_id(*me)], local_sems.at[t]) for t in range(n)]
        for t in range(n):
            mine[t].start()
            first = [copy(t, 0, me, sibling, from_input=True)]
            first += [copy(t, 1 + j, me, (*chip, core), from_input=True) for j, chip in enumerate(chips)]
            for cp in first:
                cp.start()
            started += first
        for t in range(n):
            for j, chip in enumerate(chips):
                copy(t, 1 + j, (*chip, core), me).wait_recv()
                passed = copy(t, 4 + j, (*chip, core), sibling)
                passed.start()
                started.append(passed)
        for t in range(n):
            copy(t, 0, sibling, me).wait_recv()
            for j, chip in enumerate(chips):
                copy(t, 4 + j, (*chip, 1 - core), me).wait_recv()
        for cp in started:
            cp.wait_send()
        for cp in mine:
            cp.wait()

    n_sems = n * COPIES_PER_GATHER
    outs = pl.pallas_call(
        body,
        name=name,
        out_shape=[jax.ShapeDtypeStruct((N_DEV, *s.shape), s.dtype) for s in shards],
        in_specs=[ANY] * n,
        out_specs=[ANY] * n,
        scratch_shapes=[pltpu.SemaphoreType.DMA((n_sems,)), pltpu.SemaphoreType.DMA((n_sems,)),
                        pltpu.SemaphoreType.DMA((n,))],
    )(*shards)
    return list(outs)


def _swap_with_sibling(parts, *, name):
    n = len(parts)

    def body(*refs):
        p_refs, got_refs = refs[:n], refs[n:2 * n]
        send_sems, recv_sems = refs[2 * n:]
        x, y, core = lax.axis_index("x"), lax.axis_index("y"), lax.axis_index("c")
        chips = [(x, y), (1 - x, y), (x, 1 - y), (1 - x, 1 - y)]
        copies = []
        for t in range(n):
            for k, (px, py) in enumerate(chips):
                copies.append(pltpu.make_async_remote_copy(
                    src_ref=p_refs[t].at[_block_id(px, py, 1 - core)], dst_ref=got_refs[t].at[k],
                    send_sem=send_sems.at[4 * t + k], recv_sem=recv_sems.at[4 * t + k], device_id=(x, y, 1 - core),
                    device_id_type=MESH_ID))
        for cp in copies:
            cp.start()
        for cp in copies:
            cp.wait()

    outs = pl.pallas_call(
        body,
        name=name,
        out_shape=[jax.ShapeDtypeStruct((4, *p.shape[1:]), p.dtype) for p in parts],
        in_specs=[ANY] * n,
        out_specs=[ANY] * n,
        scratch_shapes=[pltpu.SemaphoreType.DMA((4 * n,)), pltpu.SemaphoreType.DMA((4 * n,))],
    )(*parts)
    return list(outs)


def _swap_between_chips(parts, *, name):
    n = len(parts)

    def body(*refs):
        p_refs, got_refs = refs[:n], refs[n:2 * n]
        send_sems, recv_sems = refs[2 * n:]
        x, y, core = lax.axis_index("x"), lax.axis_index("y"), lax.axis_index("c")
        chips = [(1 - x, y), (x, 1 - y), (1 - x, 1 - y)]
        copies = []
        for t in range(n):
            for j, (px, py) in enumerate(chips):
                copies.append(pltpu.make_async_remote_copy(
                    src_ref=p_refs[t].at[j], dst_ref=got_refs[t].at[j], send_sem=send_sems.at[3 * t + j],
                    recv_sem=recv_sems.at[3 * t + j], device_id=(px, py, core), device_id_type=MESH_ID))
        for cp in copies:
            cp.start()
        for cp in copies:
            cp.wait()

    outs = pl.pallas_call(
        body,
        name=name,
        out_shape=[jax.ShapeDtypeStruct(p.shape, p.dtype) for p in parts],
        in_specs=[ANY] * n,
        out_specs=[ANY] * n,
        scratch_shapes=[pltpu.SemaphoreType.DMA((3 * n,)), pltpu.SemaphoreType.DMA((3 * n,))],
    )(*parts)
    return list(outs)


def _chip_partials(parts, got, *, name):
    _, r, c = parts.shape
    tb = _row_tile(r, 512)
    x, y, core = lax.axis_index("x"), lax.axis_index("y"), lax.axis_index("c")
    chips = [(x, y), (1 - x, y), (x, 1 - y), (1 - x, 1 - y)]
    ids = jnp.stack([_block_id(px, py, core) for px, py in chips]).astype(jnp.int32)

    def body(ids_ref, p0, p1, p2, p3, g_ref, own_ref, out_ref):
        own_ref[...] = p0[...].astype(F32) + g_ref[0].astype(F32)
        for j, p in enumerate((p1, p2, p3)):
            out_ref[j] = (p[...].astype(F32) + g_ref[j + 1].astype(F32)).astype(out_ref.dtype)

    def part_spec(k):
        return pl.BlockSpec((None, tb, c), lambda i, ids_ref: (ids_ref[k], i, 0))

    grid_spec = pltpu.PrefetchScalarGridSpec(
        num_scalar_prefetch=1,
        grid=(r // tb,),
        in_specs=[part_spec(k) for k in range(4)] + [pl.BlockSpec((4, tb, c), lambda i, ids_ref: (0, i, 0))],
        out_specs=[pl.BlockSpec((tb, c), lambda i, ids_ref: (i, 0)), pl.BlockSpec((3, tb, c), lambda i, ids_ref: (0, i, 0))],
    )
    return pl.pallas_call(
        body,
        name=name,
        grid_spec=grid_spec,
        out_shape=[jax.ShapeDtypeStruct((r, c), F32), jax.ShapeDtypeStruct((3, r, c), parts.dtype)],
        compiler_params=_params(("parallel",)),
    )(ids, parts, parts, parts, parts, got)


def _sum_blocks(first, blocks, *, name):
    n, r, c = blocks.shape
    tb = _row_tile(r, 512)

    def body(*refs):
        b_ref, o_ref = refs[-2], refs[-1]
        acc = refs[0][...] if first is not None else b_ref[0].astype(F32)
        for j in range(0 if first is not None else 1, n):
            acc = acc + b_ref[j].astype(F32)
        o_ref[...] = acc

    row_spec = pl.BlockSpec((tb, c), lambda i: (i, 0))
    return pl.pallas_call(
        body,
        name=name,
        grid=(r // tb,),
        in_specs=([row_spec] if first is not None else []) + [pl.BlockSpec((n, tb, c), lambda i: (0, i, 0))],
        out_specs=row_spec,
        out_shape=jax.ShapeDtypeStruct((r, c), F32),
        compiler_params=_params(("parallel",)),
    )(*([first] if first is not None else []), blocks)


def _reduce_scatter(parts, *, tag):
    got = _swap_with_sibling(parts, name=f"{tag}_sibling")
    sums = [_chip_partials(p, g, name=f"{tag}_chipsum{t}") for t, (p, g) in enumerate(zip(parts, got))]
    arrived = _swap_between_chips([others for _, others in sums], name=f"{tag}_chips")
    return [(own, arr) for (own, _), arr in zip(sums, arrived)]


PARAMS = {
    "sc_w_in": ((2, 1024, 3072), 2), "sc_conv_w": ((2, 3, 1024), 2), "sc_w_out": ((2, 1024, 1024), 1),
    "mla_w_dq": ((1, 1024, 384), 1), "mla_g_q": ((1, 384), None), "mla_w_uq": ((1, 384, 1536), 2),
    "mla_w_dkv": ((1, 1024, 320), 1), "mla_g_kv": ((1, 256), None), "mla_w_uk": ((1, 256, 8, 128), 1),
    "mla_w_uv": ((1, 256, 8, 128), 1), "mla_w_o": ((1, 1024, 1024), 1), "cf_w_pw1": ((1, 1024, 2048), 2),
    "cf_b_pw1": ((1, 2048), 1), "cf_dw_w": ((1, 31, 1024), 2), "cf_dw_b": ((1, 1024), 1), "cf_norm_g": ((1, 1024), 1),
    "cf_norm_b": ((1, 1024), 1), "cf_w_pw2": ((1, 1024, 1024), 1), "cf_b_pw2": ((1, 1024), 1),
    "ff_w1": ((4, 1024, 4096), 2), "ff_w2": ((4, 4096, 1024), 1), "ln_mix_g": ((4, 1024), None),
    "ln_mix_b": ((4, 1024), None), "ln_ff_g": ((4, 1024), None), "ln_ff_b": ((4, 1024), None),
}
NAMES = list(PARAMS)
BIG = ["sc_w_in", "sc_w_out", "mla_w_dq", "mla_w_uq", "mla_w_dkv", "mla_w_uk", "mla_w_uv", "mla_w_o", "cf_w_pw1",
       "cf_w_pw2", "ff_w1", "ff_w2"]
SMALL = [n for n in NAMES if n not in BIG]
SMALL_SHARDED = [n for n in SMALL if PARAMS[n][1] is not None]
SMALL_WIDTH = LANES
DEV_BLOCKED = ("sc_w_in", "cf_w_pw1", "ff_w1")
MIXER_BIG = {0: ["sc_w_in", "sc_w_out"], 1: ["mla_w_dq", "mla_w_uq", "mla_w_dkv", "mla_w_uk", "mla_w_uv", "mla_w_o"],
             2: ["cf_w_pw1", "cf_w_pw2"]}


def _layer_big(i):
    return [(n, i // N_MIXERS) for n in MIXER_BIG[i % N_MIXERS]] + [("ff_w1", i), ("ff_w2", i)]


def _shard_2d(name, a):
    return a.reshape(a.shape[0], -1)


def _as_operand(name, g):
    if name in DEV_BLOCKED:
        return g
    if name == "mla_w_uq":
        return g.transpose(1, 0, 2).reshape(g.shape[1], -1)
    return g.reshape(-1, g.shape[2])


def _as_parts(name, grad):
    if name in DEV_BLOCKED:
        return grad
    if name == "mla_w_uq":
        return grad.reshape(grad.shape[0], N_DEV, -1).transpose(1, 0, 2)
    return grad.reshape(N_DEV, -1, grad.shape[1])


def _shard_shape(name):
    shape, ax = PARAMS[name]
    if ax is None:
        return shape
    return tuple(s // N_DEV if i == ax else s for i, s in enumerate(shape))


def _rows_of(shape, width, align):
    n = 1
    for s in shape:
        n *= s
    rows = -(-n // width)
    return -(-rows // align) * align


def _pack_rows(arrays, width, align, dtype, lead=0):
    segs = []
    for a in arrays:
        batch = a.shape[:lead]
        flat = a.astype(dtype).reshape(*batch, -1)
        rows = _rows_of(a.shape[lead:], width, align)
        flat = jnp.pad(flat, [(0, 0)] * lead + [(0, rows * width - flat.shape[-1])])
        segs.append(flat.reshape(*batch, rows, width))
    return jnp.concatenate(segs, axis=lead)


def _unpack_rows(buf, shapes, width, align):
    out, off = [], 0
    lead = buf.shape[:-2]
    for shape in shapes:
        n = 1
        for s in shape:
            n *= s
        rows = _rows_of(shape, width, align)
        seg = buf[..., off:off + rows, :].reshape(*lead, rows * width)[..., :n]
        out.append(seg.reshape(*lead, *shape))
        off += rows
    return out


def _merge_shards(stacked, ax):
    moved = jnp.moveaxis(stacked, 0, ax)
    shape = moved.shape
    return moved.reshape(*shape[:ax], shape[ax] * shape[ax + 1], *shape[ax + 2:])


def _split_shards(full, ax):
    shape = full.shape
    split = full.reshape(*shape[:ax], N_DEV, shape[ax] // N_DEV, *shape[ax + 1:])
    return jnp.moveaxis(split, ax, 0)


def _adam_math(wv, gv, mv, vv):
    m_new = ADAM_B1 * mv + (1.0 - ADAM_B1) * gv
    v_new = ADAM_B2 * vv + (1.0 - ADAM_B2) * (gv * gv)
    m_hat = m_new / (1.0 - ADAM_B1 ** ADAM_STEP)
    v_hat = v_new / (1.0 - ADAM_B2 ** ADAM_STEP)
    delta = -ADAM_LR * (m_hat / (jnp.sqrt(v_hat) + ADAM_EPS) + ADAM_WD * wv)
    return delta, m_new, v_new


def _adamw(w, g, m, v, *, name):
    shape = w.shape
    c = shape[-1]

    def fn(rows, bc):
        return _adam_math(*rows), ()

    flat = [a.reshape(-1, c) for a in (w, g, m, v)]
    outs = _rowwise(fn, flat, [], [(c, F32)] * 3, [], name=name, tb_cap=512)
    return tuple(o.reshape(shape) for o in outs)


def _adamw_reduced(w, m, v, layer_grads, *, name):
    n_l, r, c = w.shape
    tb = _row_tile(r, 256)
    nb = r // tb

    def body(*refs):
        w_ref, m_ref, v_ref = refs[:3]
        g_refs = refs[3:3 + 2 * n_l]
        g_out, d_out, m_out, v_out = refs[3 + 2 * n_l:]
        layer = pl.program_id(0)
        for l in range(n_l):
            @pl.when(layer == l)
            def _(l=l):
                own, arr = g_refs[2 * l], g_refs[2 * l + 1]
                g = own[...] + arr[0].astype(F32)
                g = g + arr[1].astype(F32)
                g = g + arr[2].astype(F32)
                delta, m_new, v_new = _adam_math(w_ref[...], g, m_ref[...], v_ref[...])
                g_out[...] = g
                d_out[...] = delta
                m_out[...] = m_new
                v_out[...] = v_new

    stacked = pl.BlockSpec((tb, c), lambda l, i: (l * nb + i, 0))
    in_specs = [stacked] * 3
    operands = [a.reshape(n_l * r, c) for a in (w, m, v)]
    for l, (own, arr) in enumerate(layer_grads):
        in_specs.append(pl.BlockSpec((tb, c), lambda ll, i, l=l: (jnp.where(ll == l, i, 0), 0)))
        in_specs.append(pl.BlockSpec((3, tb, c), lambda ll, i, l=l: (0, jnp.where(ll == l, i, 0), 0)))
        operands += [own, arr]
    outs = pl.pallas_call(
        body,
        name=name,
        grid=(n_l, nb),
        in_specs=in_specs,
        out_specs=[stacked] * 4,
        out_shape=[jax.ShapeDtypeStruct((n_l * r, c), F32)] * 4,
        compiler_params=_params(("arbitrary", "arbitrary")),
    )(*operands)
    return tuple(o.reshape(n_l, r, c) for o in outs)


def _train_step(x, target, weights, m_state, v_state):
    me = _block_id(lax.axis_index("x"), lax.axis_index("y"), lax.axis_index("c"))

    small_shapes = [_shard_shape(n) for n in SMALL_SHARDED]
    small_all = _all_gather([_pack_rows([weights[n] for n in SMALL_SHARDED], SMALL_WIDTH, SUBLANES, F32)],
                            name="gather_small")[0]
    small_full = {n: weights[n] for n in SMALL if PARAMS[n][1] is None}
    for n, stacked in zip(SMALL_SHARDED, _unpack_rows(small_all, small_shapes, SMALL_WIDTH, SUBLANES)):
        small_full[n] = _merge_shards(stacked, PARAMS[n][1])

    def layer_weights(i):
        names = _layer_big(i)
        shards = [_shard_2d(n, weights[n][j]).astype(MXU_DTYPE) for n, j in names]
        wl = {n: _as_operand(n, g) for (n, _), g in zip(names, _all_gather(shards, name=f"gather_l{i}"))}
        for n in SMALL:
            if n.startswith("ln_"):
                wl[n] = small_full[n][i:i + 1]
            elif n.startswith(("sc_", "mla_", "cf_")[i % N_MIXERS]):
                v = small_full[n][i // N_MIXERS]
                wl[n] = v if v.ndim == 2 else v.reshape(1, -1)
        return wl

    layers = [layer_weights(i) for i in range(DEPTH)]

    cur, cur_m = x[0], _cast_input(x[0])
    saved = []
    for i in range(DEPTH):
        cur, cur_m, s = _layer_fwd(i, cur, cur_m, layers[i])
        saved.append(s)
    loss_local, dy = _loss_and_grad(cur, target[0])
    loss = lax.psum(loss_local, MESH_AXES)
    terms = [(1.0, dy)]
    reduced = {n: [None] * PARAMS[n][0][0] for n in BIG}
    small_grads = {n: [None] * PARAMS[n][0][0] for n in SMALL}
    for i in reversed(range(DEPTH)):
        terms, grads = _layer_bwd(i, terms, saved[i], layers[i])
        names = _layer_big(i)
        for (n, j), r in zip(names, _reduce_scatter([_as_parts(n, grads[n]) for n, _ in names], tag=f"reduce_l{i}")):
            reduced[n][j] = r
        for n in SMALL:
            if n in grads:
                small_grads[n][i if n.startswith("ln_") else i // N_MIXERS] = grads[n]
    grad_x = _sum_terms(terms, name="grad_x")

    small_mine = _pack_rows([jnp.stack(small_grads[n]).reshape(PARAMS[n][0]) for n in SMALL], SMALL_WIDTH, SUBLANES, F32)
    small_sum = _sum_blocks(None, _all_gather([small_mine], name="gather_small_grads")[0], name="sum_small_grads")

    grad, delta, new_m, new_v = {}, {}, {}, {}
    for n, g in zip(SMALL, _unpack_rows(small_sum, [PARAMS[n][0] for n in SMALL], SMALL_WIDTH, SUBLANES)):
        ax = PARAMS[n][1]
        grad[n] = g if ax is None else lax.dynamic_index_in_dim(_split_shards(g, ax), me, axis=0, keepdims=False)
        delta[n], new_m[n], new_v[n] = _adamw(weights[n], grad[n], m_state[n], v_state[n], name=f"adamw_{n}")
    for n in BIG:
        shape = weights[n].shape
        view = lambda a: a.reshape(shape[0], shape[1], -1)
        outs = _adamw_reduced(view(weights[n]), view(m_state[n]), view(v_state[n]), reduced[n], name=f"adamw_{n}")
        grad[n], delta[n], new_m[n], new_v[n] = [o.reshape(shape) for o in outs]
    return (loss, grad_x[None], *[grad[n] for n in NAMES], *[delta[n] for n in NAMES],
            *[new_m[n] for n in NAMES], *[new_v[n] for n in NAMES])


def kernel(x, sc_w_in, sc_conv_w, sc_w_out, mla_w_dq, mla_g_q, mla_w_uq, mla_w_dkv, mla_g_kv, mla_w_uk, mla_w_uv, mla_w_o, cf_w_pw1, cf_b_pw1, cf_dw_w, cf_dw_b, cf_norm_g, cf_norm_b, cf_w_pw2, cf_b_pw2, ff_w1, ff_w2, ln_mix_g, ln_mix_b, ln_ff_g, ln_ff_b, loss_target, m_sc_w_in, m_sc_conv_w, m_sc_w_out, m_mla_w_dq, m_mla_g_q, m_mla_w_uq, m_mla_w_dkv, m_mla_g_kv, m_mla_w_uk, m_mla_w_uv, m_mla_w_o, m_cf_w_pw1, m_cf_b_pw1, m_cf_dw_w, m_cf_dw_b, m_cf_norm_g, m_cf_norm_b, m_cf_w_pw2, m_cf_b_pw2, m_ff_w1, m_ff_w2, m_ln_mix_g, m_ln_mix_b, m_ln_ff_g, m_ln_ff_b, v_sc_w_in, v_sc_conv_w, v_sc_w_out, v_mla_w_dq, v_mla_g_q, v_mla_w_uq, v_mla_w_dkv, v_mla_g_kv, v_mla_w_uk, v_mla_w_uv, v_mla_w_o, v_cf_w_pw1, v_cf_b_pw1, v_cf_dw_w, v_cf_dw_b, v_cf_norm_g, v_cf_norm_b, v_cf_w_pw2, v_cf_b_pw2, v_ff_w1, v_ff_w2, v_ln_mix_g, v_ln_mix_b, v_ln_ff_g, v_ln_ff_b):
    w_list = (sc_w_in, sc_conv_w, sc_w_out, mla_w_dq, mla_g_q, mla_w_uq, mla_w_dkv, mla_g_kv, mla_w_uk, mla_w_uv, mla_w_o,
              cf_w_pw1, cf_b_pw1, cf_dw_w, cf_dw_b, cf_norm_g, cf_norm_b, cf_w_pw2, cf_b_pw2, ff_w1, ff_w2, ln_mix_g,
              ln_mix_b, ln_ff_g, ln_ff_b)
    m_list = (m_sc_w_in, m_sc_conv_w, m_sc_w_out, m_mla_w_dq, m_mla_g_q, m_mla_w_uq, m_mla_w_dkv, m_mla_g_kv, m_mla_w_uk,
              m_mla_w_uv, m_mla_w_o, m_cf_w_pw1, m_cf_b_pw1, m_cf_dw_w, m_cf_dw_b, m_cf_norm_g, m_cf_norm_b, m_cf_w_pw2,
              m_cf_b_pw2, m_ff_w1, m_ff_w2, m_ln_mix_g, m_ln_mix_b, m_ln_ff_g, m_ln_ff_b)
    v_list = (v_sc_w_in, v_sc_conv_w, v_sc_w_out, v_mla_w_dq, v_mla_g_q, v_mla_w_uq, v_mla_w_dkv, v_mla_g_kv, v_mla_w_uk,
              v_mla_w_uv, v_mla_w_o, v_cf_w_pw1, v_cf_b_pw1, v_cf_dw_w, v_cf_dw_b, v_cf_norm_g, v_cf_norm_b, v_cf_w_pw2,
              v_cf_b_pw2, v_ff_w1, v_ff_w2, v_ln_mix_g, v_ln_mix_b, v_ln_ff_g, v_ln_ff_b)
    return _train_step(x, loss_target, dict(zip(NAMES, w_list)), dict(zip(NAMES, m_list)), dict(zip(NAMES, v_list)))
```

```python
import functools

import jax
import jax.numpy as jnp
from jax import lax
from jax.experimental import pallas as pl
from jax.experimental.pallas import tpu as pltpu

F32 = jnp.float32
MXU_DTYPE = jnp.bfloat16

N_DEV = 8
D_MODEL = 1024
DEPTH = 4
N_MIXERS = 3
CHUNK = 64
ALPHA = (2.0 * DEPTH) ** 0.25
LN_EPS = 1e-5
RMS_EPS = 1e-6
MLA_HEADS = 8
QK_NOPE = 128
QK_ROPE = 64
V_HEAD = 128
Q_LORA = 384
KV_LORA = 256
ROPE_THETA = 10000.0
ADAM_LR = 0.001
ADAM_B1 = 0.9
ADAM_B2 = 0.999
ADAM_EPS = 1e-08
ADAM_WD = 0.01
ADAM_STEP = 10

LANES = 128
SUBLANES = 8
VMEM_LIMIT = 56 * 1024 * 1024
CONV_PAD = 32
MASK_VALUE = -1e30

MESH_AXES = ("x", "y", "c")


def _params(semantics):
    return pltpu.CompilerParams(dimension_semantics=semantics, vmem_limit_bytes=VMEM_LIMIT)


def _tile(n, cap):
    if n <= cap:
        return n
    t = cap - cap % LANES
    while t >= LANES:
        if n % t == 0:
            return t
        t -= LANES
    raise ValueError(f"no tile for {n} under {cap}")


def _row_tile(rows, cap):
    if rows <= cap:
        return rows
    t = cap - cap % SUBLANES
    while t >= SUBLANES:
        if rows % t == 0:
            return t
        t -= SUBLANES
    raise ValueError(f"no row tile for {rows} under {cap}")


def _mm(a, b, *, name, ta=False, tb=False, b_dev=False, out_dev=False, out_dtypes=(F32,), epilogue=None,
        tile_extras=(), col_extras=(), tm_cap=1024, tn_cap=1024, tk_cap=1024):
    m, k = (a.shape[1], a.shape[0]) if ta else a.shape
    if b_dev:
        nd, b_rows, ns = b.shape
        n, kb = (b_rows, nd * ns) if tb else (nd * ns, b_rows)
    else:
        n, kb = (b.shape[0], b.shape[1]) if tb else (b.shape[1], b.shape[0])
    assert k == kb, (a.shape, b.shape, ta, tb)
    tm, tn, tk = _tile(m, tm_cap), _tile(n, tn_cap), _tile(k, tk_cap)
    if b_dev and tb:
        tk = ns
    elif b_dev:
        tn = ns
    if out_dev:
        assert not tile_extras and not col_extras and n % N_DEV == 0
        tn = n // N_DEV
    nk = k // tk
    n_te, n_ce, n_out = len(tile_extras), len(col_extras), len(out_dtypes)
    dims = (((0 if ta else 1,), (1 if tb else 0,)), ((), ()))

    def body(*refs):
        a_ref, b_ref = refs[0], refs[1]
        te_refs = refs[2:2 + n_te]
        ce_refs = refs[2 + n_te:2 + n_te + n_ce]
        out_refs = refs[2 + n_te + n_ce:2 + n_te + n_ce + n_out]

        def finish(acc):
            if epilogue is None:
                outs = (acc,)
            else:
                outs = epilogue(acc, *[r[...] for r in te_refs], *[r[...] for r in ce_refs])
            for o_ref, o in zip(out_refs, outs):
                o_ref[...] = o.astype(o_ref.dtype)

        part = lax.dot_general(a_ref[...].astype(MXU_DTYPE), b_ref[...].astype(MXU_DTYPE), dims,
                               preferred_element_type=F32)
        if nk == 1:
            finish(part)
        else:
            acc_ref = refs[-1]
            kk = pl.program_id(2)

            @pl.when(kk == 0)
            def _():
                acc_ref[...] = part

            @pl.when(jnp.logical_and(kk > 0, kk < nk - 1))
            def _():
                acc_ref[...] += part

            @pl.when(kk == nk - 1)
            def _():
                finish(acc_ref[...] + part)

    a_spec = pl.BlockSpec((tk, tm), lambda i, j, kk: (kk, i)) if ta else pl.BlockSpec((tm, tk), lambda i, j, kk: (i, kk))
    if b_dev and tb:
        b_spec = pl.BlockSpec((None, tn, tk), lambda i, j, kk: (kk, j, 0))
    elif b_dev:
        b_spec = pl.BlockSpec((None, tk, tn), lambda i, j, kk: (j, kk, 0))
    elif tb:
        b_spec = pl.BlockSpec((tn, tk), lambda i, j, kk: (j, kk))
    else:
        b_spec = pl.BlockSpec((tk, tn), lambda i, j, kk: (kk, j))
    tile_spec = pl.BlockSpec((tm, tn), lambda i, j, kk: (i, j))
    col_spec = pl.BlockSpec((1, tn), lambda i, j, kk: (0, j))
    if out_dev:
        out_spec = pl.BlockSpec((None, tm, tn), lambda i, j, kk: (j, i, 0))
        out_shape = [jax.ShapeDtypeStruct((N_DEV, m, tn), dt) for dt in out_dtypes]
    else:
        out_spec = tile_spec
        out_shape = [jax.ShapeDtypeStruct((m, n), dt) for dt in out_dtypes]
    outs = pl.pallas_call(
        body,
        name=name,
        grid=(m // tm, n // tn, nk),
        in_specs=[a_spec, b_spec] + [tile_spec] * n_te + [col_spec] * n_ce,
        out_specs=[out_spec] * n_out,
        out_shape=out_shape,
        scratch_shapes=[pltpu.VMEM((tm, tn), F32)] if nk > 1 else [],
        compiler_params=_params(("parallel", "parallel", "arbitrary")),
    )(a, b, *tile_extras, *col_extras)
    return outs[0] if n_out == 1 else tuple(outs)


def _rowwise(fn, rows, bcast, out_rows, out_accs, *, name, tb_cap=256):
    t = rows[0].shape[0]
    tb = _row_tile(t, tb_cap)
    n_r, n_b, n_o, n_a = len(rows), len(bcast), len(out_rows), len(out_accs)

    def body(*refs):
        r_refs = refs[:n_r]
        b_refs = refs[n_r:n_r + n_b]
        o_refs = refs[n_r + n_b:n_r + n_b + n_o]
        a_refs = refs[n_r + n_b + n_o:]
        outs, accs = fn([r[...] for r in r_refs], [r[...] for r in b_refs])
        for o_ref, o in zip(o_refs, outs):
            o_ref[...] = o.astype(o_ref.dtype)
        if n_a:
            i = pl.program_id(0)

            @pl.when(i == 0)
            def _():
                for a_ref, acc in zip(a_refs, accs):
                    a_ref[...] = acc

            @pl.when(i > 0)
            def _():
                for a_ref, acc in zip(a_refs, accs):
                    a_ref[...] += acc

    in_specs = [pl.BlockSpec((tb, r.shape[1]), lambda i: (i, 0)) for r in rows]
    in_specs += [pl.BlockSpec(b.shape, lambda i: (0, 0)) for b in bcast]
    out_specs = [pl.BlockSpec((tb, c), lambda i: (i, 0)) for c, _ in out_rows]
    out_specs += [pl.BlockSpec(s, lambda i: (0, 0)) for s in out_accs]
    out_shape = [jax.ShapeDtypeStruct((t, c), dt) for c, dt in out_rows]
    out_shape += [jax.ShapeDtypeStruct(s, F32) for s in out_accs]
    outs = pl.pallas_call(
        body,
        name=name,
        grid=(t // tb,),
        in_specs=in_specs,
        out_specs=out_specs,
        out_shape=out_shape,
        compiler_params=_params(("arbitrary",)),
    )(*rows, *bcast)
    return tuple(outs)


def _colsum(v):
    return jnp.sum(v, axis=0, keepdims=True)


def _sigmoid(v):
    return 1.0 / (1.0 + jnp.exp(-v))


def _ln_fwd(x_res, y_sub, g, b, *, name):
    d = x_res.shape[1]

    def fn(rows, bc):
        z = ALPHA * rows[0] + rows[1]
        mu = jnp.mean(z, axis=-1, keepdims=True)
        zc = z - mu
        var = jnp.mean(zc * zc, axis=-1, keepdims=True)
        rstd = lax.rsqrt(var + LN_EPS)
        xhat = zc * rstd
        out = xhat * bc[0] + bc[1]
        return (out, out, xhat, rstd), ()

    return _rowwise(fn, [x_res, y_sub], [g, b], [(d, F32), (d, MXU_DTYPE), (d, F32), (1, F32)], [], name=name)


def _ln_bwd(terms, xhat, rstd, g, *, name):
    d = xhat.shape[1]
    coefs = [c for c, _ in terms]

    def fn(rows, bc):
        xh, rs = rows[0], rows[1]
        dout = None
        for c, v in zip(coefs, rows[2:]):
            tv = v if c == 1.0 else c * v
            dout = tv if dout is None else dout + tv
        dxh = dout * bc[0]
        m1 = jnp.mean(dxh, axis=-1, keepdims=True)
        m2 = jnp.mean(dxh * xh, axis=-1, keepdims=True)
        dz = rs * (dxh - m1 - xh * m2)
        return (dz, dz), (_colsum(dout * xh), _colsum(dout), _colsum(dz))

    return _rowwise(fn, [xhat, rstd] + [v for _, v in terms], [g], [(d, F32), (d, MXU_DTYPE)],
                    [(1, d), (1, d), (1, d)], name=name)


def _rms_fwd(x, g, *, name):
    c = x.shape[1]

    def fn(rows, bc):
        v = rows[0]
        r = lax.rsqrt(jnp.mean(v * v, axis=-1, keepdims=True) + RMS_EPS)
        return (v * r * bc[0],), ()

    return _rowwise(fn, [x], [g], [(c, MXU_DTYPE)], [], name=name)[0]


def _rms_bwd(x, g, dy, *, name):
    c = x.shape[1]

    def fn(rows, bc):
        v, dyv = rows
        r = lax.rsqrt(jnp.mean(v * v, axis=-1, keepdims=True) + RMS_EPS)
        xh = v * r
        dxh = dyv * bc[0]
        dx = r * (dxh - xh * jnp.mean(dxh * xh, axis=-1, keepdims=True))
        return (dx,), (_colsum(dyv * xh),)

    return _rowwise(fn, [x, dy], [g], [(c, F32)], [(1, c)], name=name)


def _conv_chunk(t):
    return _row_tile(t, 256)


def _conv_fwd(x, w, bias, *, name):
    t, c = x.shape
    kw = w.shape[0]
    ch = _conv_chunk(t)

    def body(x_ref, w_ref, b_ref, o_ref, pad_ref):
        pad_ref[0:CONV_PAD, :] = jnp.zeros((CONV_PAD, LANES), F32)
        pad_ref[CONV_PAD:CONV_PAD + t, :] = x_ref[...]
        for t0 in range(0, t, ch):
            acc = jnp.broadcast_to(b_ref[...], (ch, LANES))
            for k in range(kw):
                s = kw - 1 - k
                acc = acc + w_ref[k:k + 1, :] * pad_ref[CONV_PAD - s + t0:CONV_PAD - s + t0 + ch, :]
            o_ref[t0:t0 + ch, :] = acc

    return pl.pallas_call(
        body,
        name=name,
        grid=(c // LANES,),
        in_specs=[pl.BlockSpec((t, LANES), lambda j: (0, j)), pl.BlockSpec((kw, LANES), lambda j: (0, j)),
                  pl.BlockSpec((1, LANES), lambda j: (0, j))],
        out_specs=pl.BlockSpec((t, LANES), lambda j: (0, j)),
        out_shape=jax.ShapeDtypeStruct((t, c), F32),
        scratch_shapes=[pltpu.VMEM((CONV_PAD + t, LANES), F32)],
        compiler_params=_params(("parallel",)),
    )(x, w, bias)


def _conv_bwd(dy, x, w, *, name):
    t, c = x.shape
    kw = w.shape[0]
    ch = _conv_chunk(t)

    def body(dy_ref, x_ref, w_ref, dx_ref, dw_ref, xpad_ref, dpad_ref):
        xpad_ref[0:CONV_PAD, :] = jnp.zeros((CONV_PAD, LANES), F32)
        xpad_ref[CONV_PAD:CONV_PAD + t, :] = x_ref[...]
        dpad_ref[0:t, :] = dy_ref[...]
        dpad_ref[t:t + CONV_PAD, :] = jnp.zeros((CONV_PAD, LANES), F32)
        for t0 in range(0, t, ch):
            acc = jnp.zeros((ch, LANES), F32)
            for k in range(kw):
                s = kw - 1 - k
                acc = acc + w_ref[k:k + 1, :] * dpad_ref[t0 + s:t0 + s + ch, :]
            dx_ref[t0:t0 + ch, :] = acc
        for k in range(kw):
            s = kw - 1 - k
            acc = jnp.zeros((ch, LANES), F32)
            for t0 in range(0, t, ch):
                acc = acc + dy_ref[t0:t0 + ch, :] * xpad_ref[CONV_PAD - s + t0:CONV_PAD - s + t0 + ch, :]
            dw_ref[k:k + 1, :] = _colsum(acc)

    blk = pl.BlockSpec((t, LANES), lambda j: (0, j))
    wblk = pl.BlockSpec((kw, LANES), lambda j: (0, j))
    return pl.pallas_call(
        body,
        name=name,
        grid=(c // LANES,),
        in_specs=[blk, blk, wblk],
        out_specs=[blk, wblk],
        out_shape=[jax.ShapeDtypeStruct((t, c), F32), jax.ShapeDtypeStruct((kw, c), F32)],
        scratch_shapes=[pltpu.VMEM((CONV_PAD + t, LANES), F32), pltpu.VMEM((CONV_PAD + t, LANES), F32)],
        compiler_params=_params(("parallel",)),
    )(dy, x, w)


ATTN_SCALE = (QK_NOPE + QK_ROPE) ** -0.5
_NT = (((1,), (1,)), ((), ()))
_TN = (((0,), (0,)), ((), ()))
_NN = (((1,), (0,)), ((), ()))


def _dot(a, b, dims):
    return lax.dot_general(a, b, dims, preferred_element_type=F32)


def _scores(qn_ref, qp_ref, kn_ref, kp_ref, qi, tq, t):
    s = _dot(qn_ref[...], kn_ref[...], _NT) + _dot(qp_ref[...], kp_ref[...], _NT)
    s = s * ATTN_SCALE
    rows = qi * tq + lax.broadcasted_iota(jnp.int32, (tq, t), 0)
    cols = lax.broadcasted_iota(jnp.int32, (tq, t), 1)
    shift = CHUNK.bit_length() - 1
    allowed = jnp.right_shift(cols, shift) <= jnp.right_shift(rows, shift)
    return jnp.where(allowed, s, MASK_VALUE)


def _attn_fwd(qn, qp, kn, kp, v, *, name):
    h, t, _ = qn.shape
    tq = _row_tile(t, 256)

    def body(qn_ref, qp_ref, kn_ref, kp_ref, v_ref, o_ref, lse_ref):
        s = _scores(qn_ref, qp_ref, kn_ref, kp_ref, pl.program_id(1), tq, t)
        m = jnp.max(s, axis=-1, keepdims=True)
        p = jnp.exp(s - m)
        l = jnp.sum(p, axis=-1, keepdims=True)
        o = _dot((p / l).astype(MXU_DTYPE), v_ref[...], _NN)
        o_ref[...] = o.astype(o_ref.dtype)
        lse_ref[...] = m + jnp.log(l)

    qblk = lambda w: pl.BlockSpec((None, tq, w), lambda hh, qi: (hh, qi, 0))
    kblk = lambda w: pl.BlockSpec((None, t, w), lambda hh, qi: (hh, 0, 0))
    return pl.pallas_call(
        body,
        name=name,
        grid=(h, t // tq),
        in_specs=[qblk(QK_NOPE), qblk(QK_ROPE), kblk(QK_NOPE), pl.BlockSpec((t, QK_ROPE), lambda hh, qi: (0, 0)),
                  kblk(V_HEAD)],
        out_specs=[qblk(V_HEAD), qblk(1)],
        out_shape=[jax.ShapeDtypeStruct((h, t, V_HEAD), MXU_DTYPE), jax.ShapeDtypeStruct((h, t, 1), F32)],
        compiler_params=_params(("parallel", "parallel")),
    )(qn, qp, kn, kp, v)


def _attn_bwd(qn, qp, kn, kp, v, do, lse, *, name):
    h, t, _ = qn.shape
    tq = _row_tile(t, 256)

    def body(qn_ref, qp_ref, kn_ref, kp_ref, v_ref, do_ref, lse_ref, dqn_ref, dqp_ref, dkn_ref, dkp_ref, dv_ref):
        hh, qi = pl.program_id(0), pl.program_id(1)
        s = _scores(qn_ref, qp_ref, kn_ref, kp_ref, qi, tq, t)
        p = jnp.exp(s - lse_ref[...])
        dov = do_ref[...]
        dp = _dot(dov, v_ref[...], _NT)
        ds = p * (dp - jnp.sum(dp * p, axis=-1, keepdims=True)) * ATTN_SCALE
        ds_m = ds.astype(MXU_DTYPE)
        p_m = p.astype(MXU_DTYPE)
        dqn_ref[...] = _dot(ds_m, kn_ref[...], _NN)
        dqp_ref[...] = _dot(ds_m, kp_ref[...], _NN)
        dkn_part = _dot(ds_m, qn_ref[...], _TN)
        dkp_part = _dot(ds_m, qp_ref[...], _TN)
        dv_part = _dot(p_m, dov, _TN)

        @pl.when(qi == 0)
        def _():
            dkn_ref[...] = dkn_part
            dv_ref[...] = dv_part

        @pl.when(qi > 0)
        def _():
            dkn_ref[...] += dkn_part
            dv_ref[...] += dv_part

        first = jnp.logical_and(hh == 0, qi == 0)

        @pl.when(first)
        def _():
            dkp_ref[...] = dkp_part

        @pl.when(jnp.logical_not(first))
        def _():
            dkp_ref[...] += dkp_part

    qblk = lambda w: pl.BlockSpec((None, tq, w), lambda hh, qi: (hh, qi, 0))
    kblk = lambda w: pl.BlockSpec((None, t, w), lambda hh, qi: (hh, 0, 0))
    kpblk = pl.BlockSpec((t, QK_ROPE), lambda hh, qi: (0, 0))
    return pl.pallas_call(
        body,
        name=name,
        grid=(h, t // tq),
        in_specs=[qblk(QK_NOPE), qblk(QK_ROPE), kblk(QK_NOPE), kpblk, kblk(V_HEAD), qblk(V_HEAD), qblk(1)],
        out_specs=[qblk(QK_NOPE), qblk(QK_ROPE), kblk(QK_NOPE), kpblk, kblk(V_HEAD)],
        out_shape=[jax.ShapeDtypeStruct((h, t, QK_NOPE), F32), jax.ShapeDtypeStruct((h, t, QK_ROPE), F32),
                   jax.ShapeDtypeStruct((h, t, QK_NOPE), F32), jax.ShapeDtypeStruct((t, QK_ROPE), F32),
                   jax.ShapeDtypeStruct((h, t, V_HEAD), F32)],
        compiler_params=_params(("arbitrary", "arbitrary")),
    )(qn, qp, kn, kp, v, do, lse)


def _rope(x1, x2, cos, sin, *, name, inverse=False):
    w = x1.shape[1]
    sign = -1.0 if inverse else 1.0

    def fn(rows, bc):
        a, b, c, s = rows
        s = sign * s
        return (a * c - b * s, a * s + b * c), ()

    return _rowwise(fn, [x1, x2, cos, sin], [], [(w, F32), (w, F32)], [], name=name)


def _mlp_fwd(x_m, w1, w2, tag):
    def epi(acc):
        r = jnp.maximum(acc, 0.0)
        return acc, r * r

    hid, act = _mm(x_m, w1, b_dev=True, name=f"{tag}_up", out_dtypes=(F32, MXU_DTYPE), epilogue=epi)
    y = _mm(act, w2, name=f"{tag}_down")
    return y, (x_m, hid, act)


def _mlp_bwd(dy_m, saved, w1, w2, tag):
    x_m, hid, act = saved
    dw2 = _mm(act, dy_m, ta=True, name=f"{tag}_dw2", out_dtypes=(MXU_DTYPE,))

    def epi(acc, hv):
        return (acc * (2.0 * jnp.maximum(hv, 0.0)),)

    dh = _mm(dy_m, w2, tb=True, name=f"{tag}_dact", out_dtypes=(MXU_DTYPE,), epilogue=epi, tile_extras=(hid,))
    dw1 = _mm(x_m, dh, ta=True, out_dev=True, name=f"{tag}_dw1", out_dtypes=(MXU_DTYPE,))
    dx = _mm(dh, w1, tb=True, b_dev=True, name=f"{tag}_dx")
    return [dx], {"ff_w1": dw1, "ff_w2": dw2}


def _sc_fwd(x_m, w_in, conv_w, w_out, tag):
    d = D_MODEL
    u = _mm(x_m, w_in, b_dev=True, name=f"{tag}_in")

    def gate_fn(rows, bc):
        uv = rows[0]
        return (uv[:, d:2 * d] * uv[:, 2 * d:3 * d],), ()

    p = _rowwise(gate_fn, [u], [], [(d, F32)], [], name=f"{tag}_ch")[0]
    q = _conv_fwd(p, conv_w, jnp.zeros((1, d), F32), name=f"{tag}_conv")

    def out_fn(rows, bc):
        return (rows[0][:, 0:d] * rows[1],), ()

    r = _rowwise(out_fn, [u, q], [], [(d, MXU_DTYPE)], [], name=f"{tag}_bq")[0]
    y = _mm(r, w_out, name=f"{tag}_out")
    return y, (x_m, u, p, q, r)


def _sc_bwd(dy_m, saved, w_in, conv_w, w_out, tag):
    d = D_MODEL
    x_m, u, p, q, r = saved
    dw_out = _mm(r, dy_m, ta=True, name=f"{tag}_dwout", out_dtypes=(MXU_DTYPE,))
    dr = _mm(dy_m, w_out, tb=True, name=f"{tag}_dr")

    def dq_fn(rows, bc):
        return (rows[0] * rows[1][:, 0:d],), ()

    dq = _rowwise(dq_fn, [dr, u], [], [(d, F32)], [], name=f"{tag}_dq")[0]
    dp, dconv = _conv_bwd(dq, p, conv_w, name=f"{tag}_dconv")

    def du_fn(rows, bc):
        drv, qv, dpv, uv = rows
        return (jnp.concatenate([drv * qv, dpv * uv[:, 2 * d:3 * d], dpv * uv[:, d:2 * d]], axis=1),), ()

    du = _rowwise(du_fn, [dr, q, dp, u], [], [(3 * d, MXU_DTYPE)], [], name=f"{tag}_du")[0]
    dw_in = _mm(x_m, du, ta=True, out_dev=True, name=f"{tag}_dwin", out_dtypes=(MXU_DTYPE,))
    dx = _mm(du, w_in, tb=True, b_dev=True, name=f"{tag}_dx")
    return [dx], {"sc_w_in": dw_in, "sc_conv_w": dconv, "sc_w_out": dw_out}


def _cf_fwd(x_m, w, tag):
    d = D_MODEL

    def bias_epi(acc, bias):
        return (acc + bias,)

    u = _mm(x_m, w["cf_w_pw1"], b_dev=True, name=f"{tag}_pw1", epilogue=bias_epi, col_extras=(w["cf_b_pw1"],))

    def glu_fn(rows, bc):
        uv = rows[0]
        return (uv[:, 0:d] * _sigmoid(uv[:, d:2 * d]),), ()

    h1 = _rowwise(glu_fn, [u], [], [(d, F32)], [], name=f"{tag}_glu")[0]
    h2 = _conv_fwd(h1, w["cf_dw_w"], w["cf_dw_b"], name=f"{tag}_conv")

    def norm_fn(rows, bc):
        hv = rows[0]
        mu = jnp.mean(hv, axis=-1, keepdims=True)
        hc = hv - mu
        rstd = lax.rsqrt(jnp.mean(hc * hc, axis=-1, keepdims=True) + LN_EPS)
        h3 = hc * rstd * bc[0] + bc[1]
        return (h3 * _sigmoid(h3),), ()

    h4 = _rowwise(norm_fn, [h2], [w["cf_norm_g"], w["cf_norm_b"]], [(d, MXU_DTYPE)], [], name=f"{tag}_norm")[0]
    y = _mm(h4, w["cf_w_pw2"], name=f"{tag}_pw2", epilogue=bias_epi, col_extras=(w["cf_b_pw2"],))
    return y, (x_m, u, h1, h2, h4)


def _cf_bwd(dy_m, dy_colsum, saved, w, tag):
    d = D_MODEL
    x_m, u, h1, h2, h4 = saved
    dw_pw2 = _mm(h4, dy_m, ta=True, name=f"{tag}_dwpw2", out_dtypes=(MXU_DTYPE,))
    dh4 = _mm(dy_m, w["cf_w_pw2"], tb=True, name=f"{tag}_dh4")

    def dnorm_fn(rows, bc):
        dh4v, hv = rows
        g, b = bc
        mu = jnp.mean(hv, axis=-1, keepdims=True)
        hc = hv - mu
        rstd = lax.rsqrt(jnp.mean(hc * hc, axis=-1, keepdims=True) + LN_EPS)
        xh = hc * rstd
        h3 = xh * g + b
        sg = _sigmoid(h3)
        dh3 = dh4v * (sg * (1.0 + h3 * (1.0 - sg)))
        dxh = dh3 * g
        dh2 = rstd * (dxh - jnp.mean(dxh, axis=-1, keepdims=True) - xh * jnp.mean(dxh * xh, axis=-1, keepdims=True))
        return (dh2,), (_colsum(dh3 * xh), _colsum(dh3), _colsum(dh2))

    dh2, dnorm_g, dnorm_b, ddw_b = _rowwise(dnorm_fn, [dh4, h2], [w["cf_norm_g"], w["cf_norm_b"]], [(d, F32)],
                                            [(1, d), (1, d), (1, d)], name=f"{tag}_dnorm")
    dh1, ddw_w = _conv_bwd(dh2, h1, w["cf_dw_w"], name=f"{tag}_dconv")

    def dglu_fn(rows, bc):
        dh1v, uv = rows
        a, gate = uv[:, 0:d], uv[:, d:2 * d]
        sg = _sigmoid(gate)
        du = jnp.concatenate([dh1v * sg, dh1v * a * sg * (1.0 - sg)], axis=1)
        return (du,), (_colsum(du),)

    du, db_pw1 = _rowwise(dglu_fn, [dh1, u], [], [(2 * d, MXU_DTYPE)], [(1, 2 * d)], name=f"{tag}_dglu")
    dw_pw1 = _mm(x_m, du, ta=True, out_dev=True, name=f"{tag}_dwpw1", out_dtypes=(MXU_DTYPE,))
    dx = _mm(du, w["cf_w_pw1"], tb=True, b_dev=True, name=f"{tag}_dx")
    grads = {"cf_w_pw1": dw_pw1, "cf_b_pw1": db_pw1, "cf_dw_w": ddw_w, "cf_dw_b": ddw_b, "cf_norm_g": dnorm_g,
             "cf_norm_b": dnorm_b, "cf_w_pw2": dw_pw2, "cf_b_pw2": dy_colsum}
    return [dx], grads


def _rope_tables(t):
    pos = jnp.arange(t, dtype=F32)
    inv_freq = ROPE_THETA ** (-jnp.arange(0, QK_ROPE, 2, dtype=F32) / QK_ROPE)
    ang = pos[:, None] * inv_freq[None, :]
    return jnp.cos(ang), jnp.sin(ang)


def _heads(a, width):
    return a.reshape(a.shape[0], MLA_HEADS, width).transpose(1, 0, 2)


def _unheads(a):
    return a.transpose(1, 0, 2).reshape(a.shape[1], -1)


def _mla_fwd(x_m, w, tag):
    t = x_m.shape[0]
    half = QK_ROPE // 2
    cos, sin = _rope_tables(t)
    cos_h, sin_h = jnp.tile(cos, (1, MLA_HEADS)), jnp.tile(sin, (1, MLA_HEADS))
    cq_pre = _mm(x_m, w["mla_w_dq"], name=f"{tag}_dq")
    cq = _rms_fwd(cq_pre, w["mla_g_q"], name=f"{tag}_qnorm")
    q = _mm(cq, w["mla_w_uq"], name=f"{tag}_uq").reshape(t, MLA_HEADS, QK_NOPE + QK_ROPE)
    q1 = q[:, :, QK_NOPE:QK_NOPE + half].reshape(t, MLA_HEADS * half)
    q2 = q[:, :, QK_NOPE + half:].reshape(t, MLA_HEADS * half)
    r1, r2 = _rope(q1, q2, cos_h, sin_h, name=f"{tag}_qrope")
    qp = jnp.concatenate([r1.reshape(t, MLA_HEADS, half), r2.reshape(t, MLA_HEADS, half)], axis=-1)
    qn_h = q[:, :, :QK_NOPE].transpose(1, 0, 2).astype(MXU_DTYPE)
    qp_h = qp.transpose(1, 0, 2).astype(MXU_DTYPE)
    ckv_full = _mm(x_m, w["mla_w_dkv"], name=f"{tag}_dkv")
    ckv_pre = ckv_full[:, :KV_LORA]
    ckv = _rms_fwd(ckv_pre, w["mla_g_kv"], name=f"{tag}_kvnorm")
    k1, k2 = _rope(ckv_full[:, KV_LORA:KV_LORA + half], ckv_full[:, KV_LORA + half:], cos, sin, name=f"{tag}_krope")
    kp = jnp.concatenate([k1, k2], axis=-1).astype(MXU_DTYPE)
    kn_h = _heads(_mm(ckv, w["mla_w_uk"], name=f"{tag}_uk", out_dtypes=(MXU_DTYPE,)), QK_NOPE)
    v_h = _heads(_mm(ckv, w["mla_w_uv"], name=f"{tag}_uv", out_dtypes=(MXU_DTYPE,)), V_HEAD)
    o_h, lse = _attn_fwd(qn_h, qp_h, kn_h, kp, v_h, name=f"{tag}_attn")
    o_flat = _unheads(o_h)
    y = _mm(o_flat, w["mla_w_o"], name=f"{tag}_o")
    return y, (x_m, cq_pre, cq, ckv_pre, ckv, qn_h, qp_h, kn_h, kp, v_h, lse, o_flat, (cos, sin, cos_h, sin_h))


def _mla_bwd(dy_m, saved, w, tag):
    x_m, cq_pre, cq, ckv_pre, ckv, qn_h, qp_h, kn_h, kp, v_h, lse, o_flat, (cos, sin, cos_h, sin_h) = saved
    t = x_m.shape[0]
    half = QK_ROPE // 2
    dw_o = _mm(o_flat, dy_m, ta=True, name=f"{tag}_dwo", out_dtypes=(MXU_DTYPE,))
    do_h = _heads(_mm(dy_m, w["mla_w_o"], tb=True, name=f"{tag}_do", out_dtypes=(MXU_DTYPE,)), V_HEAD)
    dqn_h, dqp_h, dkn_h, dkp, dv_h = _attn_bwd(qn_h, qp_h, kn_h, kp, v_h, do_h, lse, name=f"{tag}_dattn")
    dqp = dqp_h.transpose(1, 0, 2)
    d1, d2 = _rope(dqp[:, :, :half].reshape(t, -1), dqp[:, :, half:].reshape(t, -1), cos_h, sin_h,
                   name=f"{tag}_dqrope", inverse=True)
    dq = jnp.concatenate([dqn_h.transpose(1, 0, 2), d1.reshape(t, MLA_HEADS, half), d2.reshape(t, MLA_HEADS, half)],
                         axis=-1).reshape(t, -1).astype(MXU_DTYPE)
    dw_uq = _mm(cq, dq, ta=True, name=f"{tag}_dwuq", out_dtypes=(MXU_DTYPE,))
    dcq = _mm(dq, w["mla_w_uq"], tb=True, name=f"{tag}_dcq")
    dcq_pre, dg_q = _rms_bwd(cq_pre, w["mla_g_q"], dcq, name=f"{tag}_dqnorm")
    dcq_pre_m = dcq_pre.astype(MXU_DTYPE)
    dw_dq = _mm(x_m, dcq_pre_m, ta=True, name=f"{tag}_dwdq", out_dtypes=(MXU_DTYPE,))
    dx_q = _mm(dcq_pre_m, w["mla_w_dq"], tb=True, name=f"{tag}_dxq")
    dkn = _unheads(dkn_h).astype(MXU_DTYPE)
    dv = _unheads(dv_h).astype(MXU_DTYPE)
    dw_uk = _mm(ckv, dkn, ta=True, name=f"{tag}_dwuk", out_dtypes=(MXU_DTYPE,))
    dw_uv = _mm(ckv, dv, ta=True, name=f"{tag}_dwuv", out_dtypes=(MXU_DTYPE,))
    dckv_k = _mm(dkn, w["mla_w_uk"], tb=True, name=f"{tag}_dckvk")
    dckv = _mm(dv, w["mla_w_uv"], tb=True, name=f"{tag}_dckv", epilogue=lambda acc, other: (acc + other,),
               tile_extras=(dckv_k,))
    dckv_pre, dg_kv = _rms_bwd(ckv_pre, w["mla_g_kv"], dckv, name=f"{tag}_dkvnorm")
    dk1, dk2 = _rope(dkp[:, :half], dkp[:, half:], cos, sin, name=f"{tag}_dkrope", inverse=True)
    dckv_full = jnp.concatenate([dckv_pre, dk1, dk2], axis=-1).astype(MXU_DTYPE)
    dw_dkv = _mm(x_m, dckv_full, ta=True, name=f"{tag}_dwdkv", out_dtypes=(MXU_DTYPE,))
    dx_kv = _mm(dckv_full, w["mla_w_dkv"], tb=True, name=f"{tag}_dxkv")
    grads = {"mla_w_dq": dw_dq, "mla_g_q": dg_q, "mla_w_uq": dw_uq, "mla_w_dkv": dw_dkv, "mla_g_kv": dg_kv,
             "mla_w_uk": dw_uk, "mla_w_uv": dw_uv, "mla_w_o": dw_o}
    return [dx_q, dx_kv], grads


def _cast_input(x):
    cast = lambda rows, bc: ((rows[0],), ())
    return _rowwise(cast, [x], [], [(D_MODEL, MXU_DTYPE)], [], name="x_cast")[0]


def _hooked(value, hook):
    if hook is None:
        return value
    token = hook(value)
    return value if token is None else lax.optimization_barrier((value, token))[0]


def _mix_fwd(i, cur, cur_m, wl, hook=None):
    mixer = i % N_MIXERS
    if mixer == 0:
        y, s_mix = _sc_fwd(cur_m, wl["sc_w_in"], wl["sc_conv_w"], wl["sc_w_out"], f"l{i}_sc")
    elif mixer == 1:
        y, s_mix = _mla_fwd(cur_m, wl, f"l{i}_mla")
    else:
        y, s_mix = _cf_fwd(cur_m, wl, f"l{i}_cf")
    y = _hooked(y, hook)
    cur, cur_m, xh_mix, rs_mix = _ln_fwd(cur, y, wl["ln_mix_g"], wl["ln_mix_b"], name=f"l{i}_lnmix")
    return cur, cur_m, (s_mix, xh_mix, rs_mix)


def _ff_fwd(i, cur, cur_m, wl, hook=None):
    y, s_ff = _mlp_fwd(cur_m, wl["ff_w1"], wl["ff_w2"], f"l{i}_ff")
    y = _hooked(y, hook)
    cur, cur_m, xh_ff, rs_ff = _ln_fwd(cur, y, wl["ln_ff_g"], wl["ln_ff_b"], name=f"l{i}_lnff")
    return cur, cur_m, (s_ff, xh_ff, rs_ff)


def _layer_fwd(i, cur, cur_m, wl):
    cur, cur_m, s_mix = _mix_fwd(i, cur, cur_m, wl)
    cur, cur_m, s_ff = _ff_fwd(i, cur, cur_m, wl)
    return cur, cur_m, (*s_mix, *s_ff)


def _loss_and_grad(out, target):
    d = D_MODEL

    def loss_fn(rows, bc):
        err = rows[0] - rows[1]
        return (err * (1.0 / d),), (_colsum(err * err),)

    dy, sq = _rowwise(loss_fn, [out, target], [], [(d, F32)], [(1, d)], name="loss")
    return (0.5 / d) * jnp.sum(sq), dy


def _ff_bwd(i, terms, saved, wl, hook=None):
    s_ff, xh_ff, rs_ff = saved
    grads = {}
    dz, dz_m, grads["ln_ff_g"], grads["ln_ff_b"], _ = _ln_bwd(terms, xh_ff, rs_ff, wl["ln_ff_g"], name=f"l{i}_dlnff")
    dz_m = _hooked(dz_m, hook)
    dxs, g_ff = _mlp_bwd(dz_m, s_ff, wl["ff_w1"], wl["ff_w2"], f"l{i}_ff")
    grads.update(g_ff)
    return [(ALPHA, dz)] + [(1.0, v) for v in dxs], grads


def _mix_bwd(i, terms, saved, wl, hook=None):
    mixer = i % N_MIXERS
    s_mix, xh_mix, rs_mix = saved
    grads = {}
    dz, dz_m, grads["ln_mix_g"], grads["ln_mix_b"], dz_sum = _ln_bwd(terms, xh_mix, rs_mix, wl["ln_mix_g"],
                                                                     name=f"l{i}_dlnmix")
    dz_m = _hooked(dz_m, hook)
    if mixer == 0:
        dxs, g_mix = _sc_bwd(dz_m, s_mix, wl["sc_w_in"], wl["sc_conv_w"], wl["sc_w_out"], f"l{i}_sc")
    elif mixer == 1:
        dxs, g_mix = _mla_bwd(dz_m, s_mix, wl, f"l{i}_mla")
    else:
        dxs, g_mix = _cf_bwd(dz_m, dz_sum, s_mix, wl, f"l{i}_cf")
    grads.update(g_mix)
    return [(ALPHA, dz)] + [(1.0, v) for v in dxs], grads


def _layer_bwd(i, terms, saved, wl):
    terms, g_ff = _ff_bwd(i, terms, saved[3:], wl)
    terms, g_mix = _mix_bwd(i, terms, saved[:3], wl)
    return terms, {**g_ff, **g_mix}


def _sum_terms(terms, *, name):
    coefs = [c for c, _ in terms]

    def sum_fn(rows, bc):
        acc = None
        for c, v in zip(coefs, rows):
            tv = v if c == 1.0 else c * v
            acc = tv if acc is None else acc + tv
        return (acc,), ()

    return _rowwise(sum_fn, [v for _, v in terms], [], [(D_MODEL, F32)], [], name=name)[0]


def _local_step(x, target, layers):
    cur, cur_m = x, _cast_input(x)
    saved = []
    for i in range(DEPTH):
        cur, cur_m, s = _layer_fwd(i, cur, cur_m, layers[i])
        saved.append(s)
    loss, dy = _loss_and_grad(cur, target)
    terms = [(1.0, dy)]
    grads = [None] * DEPTH
    for i in reversed(range(DEPTH)):
        terms, grads[i] = _layer_bwd(i, terms, saved[i], layers[i])
    return loss, _sum_terms(terms, name="grad_x"), grads


MESH_ID = pl.DeviceIdType.MESH
ANY = pl.BlockSpec(memory_space=pl.ANY)


def _block_id(px, py, pc):
    return 4 * px + 2 * py + pc


COPIES_PER_GATHER = 7


def _all_gather(shards, *, name):
    n = len(shards)

    def body(*refs):
        x_refs, out_refs = refs[:n], refs[n:2 * n]
        send_sems, recv_sems, local_sems = refs[2 * n:]
        x, y, core = lax.axis_index("x"), lax.axis_index("y"), lax.axis_index("c")
        me, sibling = (x, y, core), (x, y, 1 - core)
        chips = [(1 - x, y), (x, 1 - y), (1 - x, 1 - y)]

        def copy(t, k, block, to, from_input=False):
            dst = out_refs[t].at[_block_id(*block)]
            return pltpu.make_async_remote_copy(
                src_ref=x_refs[t] if from_input else dst, dst_ref=dst,
                send_sem=send_sems.at[t * COPIES_PER_GATHER + k], recv_sem=recv_sems.at[t * COPIES_PER_GATHER + k],
                device_id=to, device_id_type=MESH_ID)

        started = []
        mine = [pltpu.make_async_copy(x_refs[t], out_refs[t].at[_block_id(*me)], local_sems.at[t]) for t in range(n)]
        for t in range(n):
            mine[t].start()
            first = [copy(t, 0, me, sibling, from_input=True)]
            first += [copy(t, 1 + j, me, (*chip, core), from_input=True) for j, chip in enumerate(chips)]
            for cp in first:
                cp.start()
            started += first
        for t in range(n):
            for j, chip in enumerate(chips):
                copy(t, 1 + j, (*chip, core), me).wait_recv()
                passed = copy(t, 4 + j, (*chip, core), sibling)
                passed.start()
                started.append(passed)
        for t in range(n):
            copy(t, 0, sibling, me).wait_recv()
            for j, chip in enumerate(chips):
                copy(t, 4 + j, (*chip, 1 - core), me).wait_recv()
        for cp in started:
            cp.wait_send()
        for cp in mine:
            cp.wait()

    n_sems = n * COPIES_PER_GATHER
    outs = pl.pallas_call(
        body,
        name=name,
        out_shape=[jax.ShapeDtypeStruct((N_DEV, *s.shape), s.dtype) for s in shards],
        in_specs=[ANY] * n,
        out_specs=[ANY] * n,
        scratch_shapes=[pltpu.SemaphoreType.DMA((n_sems,)), pltpu.SemaphoreType.DMA((n_sems,)),
                        pltpu.SemaphoreType.DMA((n,))],
    )(*shards)
    return list(outs)


def _swap_with_sibling(parts, *, name):
    n = len(parts)

    def body(*refs):
        p_refs, got_refs = refs[:n], refs[n:2 * n]
        send_sems, recv_sems = refs[2 * n:]
        x, y, core = lax.axis_index("x"), lax.axis_index("y"), lax.axis_index("c")
        chips = [(x, y), (1 - x, y), (x, 1 - y), (1 - x, 1 - y)]
        copies = []
        for t in range(n):
            for k, (px, py) in enumerate(chips):
                copies.append(pltpu.make_async_remote_copy(
                    src_ref=p_refs[t].at[_block_id(px, py, 1 - core)], dst_ref=got_refs[t].at[k],
                    send_sem=send_sems.at[4 * t + k], recv_sem=recv_sems.at[4 * t + k], device_id=(x, y, 1 - core),
                    device_id_type=MESH_ID))
        for cp in copies:
            cp.start()
        for cp in copies:
            cp.wait()

    outs = pl.pallas_call(
        body,
        name=name,
        out_shape=[jax.ShapeDtypeStruct((4, *p.shape[1:]), p.dtype) for p in parts],
        in_specs=[ANY] * n,
        out_specs=[ANY] * n,
        scratch_shapes=[pltpu.SemaphoreType.DMA((4 * n,)), pltpu.SemaphoreType.DMA((4 * n,))],
    )(*parts)
    return list(outs)


def _swap_between_chips(parts, *, name):
    n = len(parts)

    def body(*refs):
        p_refs, got_refs = refs[:n], refs[n:2 * n]
        send_sems, recv_sems = refs[2 * n:]
        x, y, core = lax.axis_index("x"), lax.axis_index("y"), lax.axis_index("c")
        chips = [(1 - x, y), (x, 1 - y), (1 - x, 1 - y)]
        copies = []
        for t in range(n):
            for j, (px, py) in enumerate(chips):
                copies.append(pltpu.make_async_remote_copy(
                    src_ref=p_refs[t].at[j], dst_ref=got_refs[t].at[j], send_sem=send_sems.at[3 * t + j],
                    recv_sem=recv_sems.at[3 * t + j], device_id=(px, py, core), device_id_type=MESH_ID))
        for cp in copies:
            cp.start()
        for cp in copies:
            cp.wait()

    outs = pl.pallas_call(
        body,
        name=name,
        out_shape=[jax.ShapeDtypeStruct(p.shape, p.dtype) for p in parts],
        in_specs=[ANY] * n,
        out_specs=[ANY] * n,
        scratch_shapes=[pltpu.SemaphoreType.DMA((3 * n,)), pltpu.SemaphoreType.DMA((3 * n,))],
    )(*parts)
    return list(outs)


def _chip_partials(parts, got, *, name):
    _, r, c = parts.shape
    tb = _row_tile(r, 512)
    x, y, core = lax.axis_index("x"), lax.axis_index("y"), lax.axis_index("c")
    chips = [(x, y), (1 - x, y), (x, 1 - y), (1 - x, 1 - y)]
    ids = jnp.stack([_block_id(px, py, core) for px, py in chips]).astype(jnp.int32)

    def body(ids_ref, p0, p1, p2, p3, g_ref, own_ref, out_ref):
        own_ref[...] = p0[...].astype(F32) + g_ref[0].astype(F32)
        for j, p in enumerate((p1, p2, p3)):
            out_ref[j] = (p[...].astype(F32) + g_ref[j + 1].astype(F32)).astype(out_ref.dtype)

    def part_spec(k):
        return pl.BlockSpec((None, tb, c), lambda i, ids_ref: (ids_ref[k], i, 0))

    grid_spec = pltpu.PrefetchScalarGridSpec(
        num_scalar_prefetch=1,
        grid=(r // tb,),
        in_specs=[part_spec(k) for k in range(4)] + [pl.BlockSpec((4, tb, c), lambda i, ids_ref: (0, i, 0))],
        out_specs=[pl.BlockSpec((tb, c), lambda i, ids_ref: (i, 0)), pl.BlockSpec((3, tb, c), lambda i, ids_ref: (0, i, 0))],
    )
    return pl.pallas_call(
        body,
        name=name,
        grid_spec=grid_spec,
        out_shape=[jax.ShapeDtypeStruct((r, c), F32), jax.ShapeDtypeStruct((3, r, c), parts.dtype)],
        compiler_params=_params(("parallel",)),
    )(ids, parts, parts, parts, parts, got)


def _sum_blocks(first, blocks, *, name):
    n, r, c = blocks.shape
    tb = _row_tile(r, 512)

    def body(*refs):
        b_ref, o_ref = refs[-2], refs[-1]
        acc = refs[0][...] if first is not None else b_ref[0].astype(F32)
        for j in range(0 if first is not None else 1, n):
            acc = acc + b_ref[j].astype(F32)
        o_ref[...] = acc

    row_spec = pl.BlockSpec((tb, c), lambda i: (i, 0))
    return pl.pallas_call(
        body,
        name=name,
        grid=(r // tb,),
        in_specs=([row_spec] if first is not None else []) + [pl.BlockSpec((n, tb, c), lambda i: (0, i, 0))],
        out_specs=row_spec,
        out_shape=jax.ShapeDtypeStruct((r, c), F32),
        compiler_params=_params(("parallel",)),
    )(*([first] if first is not None else []), blocks)


def _reduce_scatter(parts, *, tag):
    got = _swap_with_sibling(parts, name=f"{tag}_sibling")
    sums = [_chip_partials(p, g, name=f"{tag}_chipsum{t}") for t, (p, g) in enumerate(zip(parts, got))]
    arrived = _swap_between_chips([others for _, others in sums], name=f"{tag}_chips")
    return [(own, arr) for (own, _), arr in zip(sums, arrived)]


HBM = pl.BlockSpec(memory_space=pltpu.HBM)
SEM = pl.BlockSpec(memory_space=pltpu.SEMAPHORE)
DATAFLOW = pltpu.SideEffectType.DATAFLOW_SIDE_EFFECTING


def _tie(value, *before):
    return lax.optimization_barrier((value, *before))[0]


def _split_call(name, arrays, sems, n_new, body, after=None):
    n_a, n_s = len(arrays), len(sems)
    n_after = 0 if after is None else 1

    def kernel_body(*refs):
        new = refs[n_a + n_s + n_after:n_a + n_s + n_after + 2] if n_new else (None, None)
        body(refs[:n_a], refs[n_a:n_a + n_s], *new)
        refs[-1][...] = jnp.zeros_like(refs[-1])

    n_sem_out = 2 if n_new else 0
    out_shape = [pltpu.SemaphoreType.DMA((n_new,))] * n_sem_out
    out_shape += [pltpu.HBM(a.shape, a.dtype) for a in arrays]
    out_shape += [jax.ShapeDtypeStruct((SUBLANES, LANES), F32)]
    outs = pl.pallas_call(
        kernel_body,
        name=name,
        out_shape=out_shape,
        in_specs=[HBM] * n_a + [SEM] * n_s + [ANY] * n_after,
        out_specs=[SEM] * n_sem_out + [HBM] * n_a + [pl.BlockSpec(memory_space=pltpu.VMEM)],
        input_output_aliases={i: n_sem_out + i for i in range(n_a)},
        compiler_params=pltpu.CompilerParams(has_side_effects=DATAFLOW),
    )(*[pltpu.with_memory_space_constraint(a, pltpu.HBM) for a in arrays], *sems, *([after] if n_after else []))
    return tuple(outs[:n_sem_out]), list(outs[n_sem_out:n_sem_out + n_a]), outs[-1]


def _remote(src, dst, send, recv, k, to):
    return pltpu.make_async_remote_copy(src_ref=src, dst_ref=dst, send_sem=send.at[k], recv_sem=recv.at[k],
                                        device_id=to, device_id_type=MESH_ID)


def _gather_start(shards, *, name):
    n = len(shards)
    me = _block_id(lax.axis_index("x"), lax.axis_index("y"), lax.axis_index("c"))
    lands = [lax.dynamic_update_index_in_dim(lax.empty((N_DEV, *s.shape), s.dtype), s, me, 0) for s in shards]

    def body(refs, _, send, recv):
        x, y, core = lax.axis_index("x"), lax.axis_index("y"), lax.axis_index("c")
        targets = [(x, y, 1 - core), (1 - x, y, core), (x, 1 - y, core), (1 - x, 1 - y, core)]
        for t in range(n):
            for k, to in enumerate(targets):
                _remote(refs[t], refs[n + t].at[_block_id(x, y, core)], send, recv, 4 * t + k, to).start()

    sems, thru, token = _split_call(name, shards + lands, [], 4 * n, body)
    return {"n": n, "sems1": sems, "arrays": thru, "token": token}


def _gather_pass_on(st, *, name, after):
    n = st["n"]

    def body(refs, sems, send, recv):
        send1, recv1 = sems
        x, y, core = lax.axis_index("x"), lax.axis_index("y"), lax.axis_index("c")
        chips = [(1 - x, y), (x, 1 - y), (1 - x, 1 - y)]
        for t in range(n):
            for j, (px, py) in enumerate(chips):
                block = refs[n + t].at[_block_id(px, py, core)]
                _remote(refs[t], block, send1, recv1, 4 * t + 1 + j, (x, y, 1 - core)).wait_recv()
                _remote(block, block, send, recv, 3 * t + j, (x, y, 1 - core)).start()

    sems2, thru, token = _split_call(name, st["arrays"], list(st["sems1"]), 3 * n, body, after=after)
    return {**st, "sems2": sems2, "arrays": thru, "token": token}


def _gather_finish(st, *, name, after):
    n = st["n"]

    def body(refs, sems, *_):
        send1, recv1, send2, recv2 = sems
        x, y, core = lax.axis_index("x"), lax.axis_index("y"), lax.axis_index("c")
        sibling = (x, y, 1 - core)
        chips = [(1 - x, y), (x, 1 - y), (1 - x, 1 - y)]
        for t in range(n):
            land = refs[n + t]
            for k in range(4):
                _remote(refs[t], land.at[_block_id(x, y, core)], send1, recv1, 4 * t + k, sibling).wait_send()
            _remote(refs[t], land.at[_block_id(x, y, 1 - core)], send1, recv1, 4 * t, sibling).wait_recv()
            for j, (px, py) in enumerate(chips):
                _remote(land.at[_block_id(px, py, core)], land.at[_block_id(px, py, core)], send2, recv2, 3 * t + j,
                        sibling).wait_send()
                _remote(refs[t], land.at[_block_id(px, py, 1 - core)], send2, recv2, 3 * t + j, sibling).wait_recv()

    _, thru, _ = _split_call(name, st["arrays"], [*st["sems1"], *st["sems2"]], 0, body, after=after)
    return thru[n:]


def _swap_start(sources, lands, plan, per_array, *, name):
    n = len(sources)

    def body(refs, _, send, recv):
        x, y, core = lax.axis_index("x"), lax.axis_index("y"), lax.axis_index("c")
        for t in range(n):
            for k in range(per_array):
                block, to = plan(x, y, core, k)
                _remote(refs[t].at[block], refs[n + t].at[k], send, recv, per_array * t + k, to).start()

    sems, thru, token = _split_call(name, sources + lands, [], per_array * n, body)
    return {"n": n, "sems": sems, "arrays": thru, "token": token, "plan": plan, "per_array": per_array}


def _swap_finish(st, *, name, after):
    n, plan, per_array = st["n"], st["plan"], st["per_array"]

    def body(refs, sems, *_):
        send, recv = sems
        x, y, core = lax.axis_index("x"), lax.axis_index("y"), lax.axis_index("c")
        for t in range(n):
            for k in range(per_array):
                block, to = plan(x, y, core, k)
                cp = _remote(refs[t].at[block], refs[n + t].at[k], send, recv, per_array * t + k, to)
                cp.wait_send()
                cp.wait_recv()

    _, thru, _ = _split_call(name, st["arrays"], list(st["sems"]), 0, body, after=after)
    return thru[:n], thru[n:]


def _to_sibling(x, y, core, k):
    px, py = [(x, y), (1 - x, y), (x, 1 - y), (1 - x, 1 - y)][k]
    return _block_id(px, py, 1 - core), (x, y, 1 - core)


def _to_chip(x, y, core, k):
    px, py = [(1 - x, y), (x, 1 - y), (1 - x, 1 - y)][k]
    return k, (px, py, core)


PARAMS = {
    "sc_w_in": ((2, 1024, 3072), 2), "sc_conv_w": ((2, 3, 1024), 2), "sc_w_out": ((2, 1024, 1024), 1),
    "mla_w_dq": ((1, 1024, 384), 1), "mla_g_q": ((1, 384), None), "mla_w_uq": ((1, 384, 1536), 2),
    "mla_w_dkv": ((1, 1024, 320), 1), "mla_g_kv": ((1, 256), None), "mla_w_uk": ((1, 256, 8, 128), 1),
    "mla_w_uv": ((1, 256, 8, 128), 1), "mla_w_o": ((1, 1024, 1024), 1), "cf_w_pw1": ((1, 1024, 2048), 2),
    "cf_b_pw1": ((1, 2048), 1), "cf_dw_w": ((1, 31, 1024), 2), "cf_dw_b": ((1, 1024), 1), "cf_norm_g": ((1, 1024), 1),
    "cf_norm_b": ((1, 1024), 1), "cf_w_pw2": ((1, 1024, 1024), 1), "cf_b_pw2": ((1, 1024), 1),
    "ff_w1": ((4, 1024, 4096), 2), "ff_w2": ((4, 4096, 1024), 1), "ln_mix_g": ((4, 1024), None),
    "ln_mix_b": ((4, 1024), None), "ln_ff_g": ((4, 1024), None), "ln_ff_b": ((4, 1024), None),
}
NAMES = list(PARAMS)
BIG = ["sc_w_in", "sc_w_out", "mla_w_dq", "mla_w_uq", "mla_w_dkv", "mla_w_uk", "mla_w_uv", "mla_w_o", "cf_w_pw1",
       "cf_w_pw2", "ff_w1", "ff_w2"]
SMALL = [n for n in NAMES if n not in BIG]
SMALL_SHARDED = [n for n in SMALL if PARAMS[n][1] is not None]
SMALL_WIDTH = LANES
DEV_BLOCKED = ("sc_w_in", "cf_w_pw1", "ff_w1")
MIXER_BIG = {0: ["sc_w_in", "sc_w_out"], 1: ["mla_w_dq", "mla_w_uq", "mla_w_dkv", "mla_w_uk", "mla_w_uv", "mla_w_o"],
             2: ["cf_w_pw1", "cf_w_pw2"]}


def _layer_big(i):
    return [(n, i // N_MIXERS) for n in MIXER_BIG[i % N_MIXERS]] + [("ff_w1", i), ("ff_w2", i)]


def _shard_2d(name, a):
    return a.reshape(a.shape[0], -1)


def _as_operand(name, g):
    if name in DEV_BLOCKED:
        return g
    if name == "mla_w_uq":
        return g.transpose(1, 0, 2).reshape(g.shape[1], -1)
    return g.reshape(-1, g.shape[2])


def _as_parts(name, grad):
    if name in DEV_BLOCKED:
        return grad
    if name == "mla_w_uq":
        return grad.reshape(grad.shape[0], N_DEV, -1).transpose(1, 0, 2)
    return grad.reshape(N_DEV, -1, grad.shape[1])


def _shard_shape(name):
    shape, ax = PARAMS[name]
    if ax is None:
        return shape
    return tuple(s // N_DEV if i == ax else s for i, s in enumerate(shape))


def _rows_of(shape, width, align):
    n = 1
    for s in shape:
        n *= s
    rows = -(-n // width)
    return -(-rows // align) * align


def _pack_rows(arrays, width, align, dtype, lead=0):
    segs = []
    for a in arrays:
        batch = a.shape[:lead]
        flat = a.astype(dtype).reshape(*batch, -1)
        rows = _rows_of(a.shape[lead:], width, align)
        flat = jnp.pad(flat, [(0, 0)] * lead + [(0, rows * width - flat.shape[-1])])
        segs.append(flat.reshape(*batch, rows, width))
    return jnp.concatenate(segs, axis=lead)


def _unpack_rows(buf, shapes, width, align):
    out, off = [], 0
    lead = buf.shape[:-2]
    for shape in shapes:
        n = 1
        for s in shape:
            n *= s
        rows = _rows_of(shape, width, align)
        seg = buf[..., off:off + rows, :].reshape(*lead, rows * width)[..., :n]
        out.append(seg.reshape(*lead, *shape))
        off += rows
    return out


def _merge_shards(stacked, ax):
    moved = jnp.moveaxis(stacked, 0, ax)
    shape = moved.shape
    return moved.reshape(*shape[:ax], shape[ax] * shape[ax + 1], *shape[ax + 2:])


def _split_shards(full, ax):
    shape = full.shape
    split = full.reshape(*shape[:ax], N_DEV, shape[ax] // N_DEV, *shape[ax + 1:])
    return jnp.moveaxis(split, ax, 0)


def _adam_math(wv, gv, mv, vv):
    m_new = ADAM_B1 * mv + (1.0 - ADAM_B1) * gv
    v_new = ADAM_B2 * vv + (1.0 - ADAM_B2) * (gv * gv)
    m_hat = m_new / (1.0 - ADAM_B1 ** ADAM_STEP)
    v_hat = v_new / (1.0 - ADAM_B2 ** ADAM_STEP)
    delta = -ADAM_LR * (m_hat / (jnp.sqrt(v_hat) + ADAM_EPS) + ADAM_WD * wv)
    return delta, m_new, v_new


def _adamw(w, g, m, v, *, name):
    shape = w.shape
    c = shape[-1]

    def fn(rows, bc):
        return _adam_math(*rows), ()

    flat = [a.reshape(-1, c) for a in (w, g, m, v)]
    outs = _rowwise(fn, flat, [], [(c, F32)] * 3, [], name=name, tb_cap=512)
    return tuple(o.reshape(shape) for o in outs)


def _adamw_reduced(w, m, v, layer_grads, *, name):
    n_l, r, c = w.shape
    tb = _row_tile(r, 256)
    nb = r // tb

    def body(*refs):
        w_ref, m_ref, v_ref = refs[:3]
        g_refs = refs[3:3 + 2 * n_l]
        g_out, d_out, m_out, v_out = refs[3 + 2 * n_l:]
        layer = pl.program_id(0)
        for l in range(n_l):
            @pl.when(layer == l)
            def _(l=l):
                own, arr = g_refs[2 * l], g_refs[2 * l + 1]
                g = own[...] + arr[0].astype(F32)
                g = g + arr[1].astype(F32)
                g = g + arr[2].astype(F32)
                delta, m_new, v_new = _adam_math(w_ref[...], g, m_ref[...], v_ref[...])
                g_out[...] = g
                d_out[...] = delta
                m_out[...] = m_new
                v_out[...] = v_new

    stacked = pl.BlockSpec((tb, c), lambda l, i: (l * nb + i, 0))
    in_specs = [stacked] * 3
    operands = [a.reshape(n_l * r, c) for a in (w, m, v)]
    for l, (own, arr) in enumerate(layer_grads):
        in_specs.append(pl.BlockSpec((tb, c), lambda ll, i, l=l: (jnp.where(ll == l, i, 0), 0)))
        in_specs.append(pl.BlockSpec((3, tb, c), lambda ll, i, l=l: (0, jnp.where(ll == l, i, 0), 0)))
        operands += [own, arr]
    outs = pl.pallas_call(
        body,
        name=name,
        grid=(n_l, nb),
        in_specs=in_specs,
        out_specs=[stacked] * 4,
        out_shape=[jax.ShapeDtypeStruct((n_l * r, c), F32)] * 4,
        compiler_params=_params(("arbitrary", "arbitrary")),
    )(*operands)
    return tuple(o.reshape(n_l, r, c) for o in outs)


def _train_step(x, target, weights, m_state, v_state):
    me = _block_id(lax.axis_index("x"), lax.axis_index("y"), lax.axis_index("c"))

    small_shapes = [_shard_shape(n) for n in SMALL_SHARDED]
    small_all = _all_gather([_pack_rows([weights[n] for n in SMALL_SHARDED], SMALL_WIDTH, SUBLANES, F32)],
                            name="gather_small")[0]
    small_full = {n: weights[n] for n in SMALL if PARAMS[n][1] is None}
    for n, stacked in zip(SMALL_SHARDED, _unpack_rows(small_all, small_shapes, SMALL_WIDTH, SUBLANES)):
        small_full[n] = _merge_shards(stacked, PARAMS[n][1])

    def small_weights(i):
        wl = {}
        for n in SMALL:
            if n.startswith("ln_"):
                wl[n] = small_full[n][i:i + 1]
            elif n.startswith(("sc_", "mla_", "cf_")[i % N_MIXERS]):
                v = small_full[n][i // N_MIXERS]
                wl[n] = v if v.ndim == 2 else v.reshape(1, -1)
        return wl

    layers = [small_weights(i) for i in range(DEPTH)]
    groups = [(kind, i) for i in range(DEPTH) for kind in ("mix", "ff")]

    def group_names(g):
        kind, i = g
        if kind == "mix":
            return [(n, i // N_MIXERS) for n in MIXER_BIG[i % N_MIXERS]]
        return [("ff_w1", i), ("ff_w2", i)]

    def shards_of(g, behind=None):
        shards = [_shard_2d(n, weights[n][j]).astype(MXU_DTYPE) for n, j in group_names(g)]
        return shards if behind is None else [_tie(s, behind) for s in shards]

    def install(g, gathered):
        for (n, _), a in zip(group_names(g), gathered):
            layers[g[1]][n] = _as_operand(n, a)

    install(groups[0], _all_gather(shards_of(groups[0]), name="gather_g0"))
    gather = {1: _gather_start(shards_of(groups[1]), name="gather_g1_start")}
    cur, cur_m = x[0], _cast_input(x[0])
    cur_m = _tie(cur_m, gather[1]["token"])
    saved = {}
    for k, g in enumerate(groups):
        kind, i = g
        if k + 2 < len(groups):
            gather[k + 2] = _gather_start(shards_of(groups[k + 2], behind=cur), name=f"gather_g{k + 2}_start")
            cur_m = _tie(cur_m, gather[k + 2]["token"])

        def pass_on(y, k=k):
            if k + 1 == len(groups):
                return None
            gather[k + 1] = _gather_pass_on(gather[k + 1], name=f"gather_g{k + 1}_pass", after=y)
            return gather[k + 1]["token"]

        phase = _mix_fwd if kind == "mix" else _ff_fwd
        cur, cur_m, saved[g] = phase(i, cur, cur_m, layers[i], hook=pass_on)
        if k + 1 < len(groups):
            install(groups[k + 1], _gather_finish(gather[k + 1], name=f"gather_g{k + 1}_finish", after=cur))

    loss_local, dy = _loss_and_grad(cur, target[0])
    loss = lax.psum(loss_local, MESH_AXES)

    reduced = {n: [None] * PARAMS[n][0][0] for n in BIG}
    small_grads = {n: [None] * PARAMS[n][0][0] for n in SMALL}
    reduce = {}

    def to_chips(p, after):
        st = reduce[p]
        parts, got = _swap_finish(st["sibling"], name=f"reduce_p{p}_sibling_finish", after=after)
        sums = [_chip_partials(a, b, name=f"reduce_p{p}_chipsum{t}") for t, (a, b) in enumerate(zip(parts, got))]
        st["own"] = [own for own, _ in sums]
        others = [o for _, o in sums]
        st["chips"] = _swap_start(others, [lax.empty(o.shape, o.dtype) for o in others], _to_chip, 3,
                                  name=f"reduce_p{p}_chips_start")
        return st["chips"]["token"]

    def arrived(p, after):
        st = reduce[p]
        _, got = _swap_finish(st["chips"], name=f"reduce_p{p}_chips_finish", after=after)
        for (n, j), own, arr in zip(st["names"], st["own"], got):
            reduced[n][j] = (own, arr)

    terms = [(1.0, dy)]
    back = list(reversed(groups))
    for p, g in enumerate(back):
        kind, i = g

        def after_first_kernel(dz, p=p):
            return to_chips(p - 1, dz) if p >= 1 else None

        phase = _mix_bwd if kind == "mix" else _ff_bwd
        terms, grads = phase(i, terms, saved[g], layers[i], hook=after_first_kernel)
        for n in SMALL:
            if n in grads:
                small_grads[n][i if n.startswith("ln_") else i // N_MIXERS] = grads[n]
        if p >= 2:
            arrived(p - 2, terms[-1][1])
        names = group_names(g)
        parts = [_as_parts(n, grads[n]) for n, _ in names]
        reduce[p] = {"names": names, "sibling": _swap_start(
            parts, [lax.empty((4, *a.shape[1:]), a.dtype) for a in parts], _to_sibling, 4,
            name=f"reduce_p{p}_sibling_start")}
        terms = [(terms[0][0], _tie(terms[0][1], reduce[p]["sibling"]["token"]))] + terms[1:]
    grad_x = _sum_terms(terms, name="grad_x")
    last = len(back) - 1
    arrived(last - 1, grad_x)
    tail = to_chips(last, grad_x)
    arrived(last, tail)

    small_mine = _pack_rows([jnp.stack(small_grads[n]).reshape(PARAMS[n][0]) for n in SMALL], SMALL_WIDTH, SUBLANES, F32)
    small_sum = _sum_blocks(None, _all_gather([small_mine], name="gather_small_grads")[0], name="sum_small_grads")

    grad, delta, new_m, new_v = {}, {}, {}, {}
    for n, g in zip(SMALL, _unpack_rows(small_sum, [PARAMS[n][0] for n in SMALL], SMALL_WIDTH, SUBLANES)):
        ax = PARAMS[n][1]
        grad[n] = g if ax is None else lax.dynamic_index_in_dim(_split_shards(g, ax), me, axis=0, keepdims=False)
        delta[n], new_m[n], new_v[n] = _adamw(weights[n], grad[n], m_state[n], v_state[n], name=f"adamw_{n}")
    for n in BIG:
        shape = weights[n].shape
        view = lambda a: a.reshape(shape[0], shape[1], -1)
        outs = _adamw_reduced(view(weights[n]), view(m_state[n]), view(v_state[n]), reduced[n], name=f"adamw_{n}")
        grad[n], delta[n], new_m[n], new_v[n] = [o.reshape(shape) for o in outs]
    return (loss, grad_x[None], *[grad[n] for n in NAMES], *[delta[n] for n in NAMES],
            *[new_m[n] for n in NAMES], *[new_v[n] for n in NAMES])


def kernel(x, sc_w_in, sc_conv_w, sc_w_out, mla_w_dq, mla_g_q, mla_w_uq, mla_w_dkv, mla_g_kv, mla_w_uk, mla_w_uv, mla_w_o, cf_w_pw1, cf_b_pw1, cf_dw_w, cf_dw_b, cf_norm_g, cf_norm_b, cf_w_pw2, cf_b_pw2, ff_w1, ff_w2, ln_mix_g, ln_mix_b, ln_ff_g, ln_ff_b, loss_target, m_sc_w_in, m_sc_conv_w, m_sc_w_out, m_mla_w_dq, m_mla_g_q, m_mla_w_uq, m_mla_w_dkv, m_mla_g_kv, m_mla_w_uk, m_mla_w_uv, m_mla_w_o, m_cf_w_pw1, m_cf_b_pw1, m_cf_dw_w, m_cf_dw_b, m_cf_norm_g, m_cf_norm_b, m_cf_w_pw2, m_cf_b_pw2, m_ff_w1, m_ff_w2, m_ln_mix_g, m_ln_mix_b, m_ln_ff_g, m_ln_ff_b, v_sc_w_in, v_sc_conv_w, v_sc_w_out, v_mla_w_dq, v_mla_g_q, v_mla_w_uq, v_mla_w_dkv, v_mla_g_kv, v_mla_w_uk, v_mla_w_uv, v_mla_w_o, v_cf_w_pw1, v_cf_b_pw1, v_cf_dw_w, v_cf_dw_b, v_cf_norm_g, v_cf_norm_b, v_cf_w_pw2, v_cf_b_pw2, v_ff_w1, v_ff_w2, v_ln_mix_g, v_ln_mix_b, v_ln_ff_g, v_ln_ff_b):
    w_list = (sc_w_in, sc_conv_w, sc_w_out, mla_w_dq, mla_g_q, mla_w_uq, mla_w_dkv, mla_g_kv, mla_w_uk, mla_w_uv, mla_w_o,
              cf_w_pw1, cf_b_pw1, cf_dw_w, cf_dw_b, cf_norm_g, cf_norm_b, cf_w_pw2, cf_b_pw2, ff_w1, ff_w2, ln_mix_g,
              ln_mix_b, ln_ff_g, ln_ff_b)
    m_list = (m_sc_w_in, m_sc_conv_w, m_sc_w_out, m_mla_w_dq, m_mla_g_q, m_mla_w_uq, m_mla_w_dkv, m_mla_g_kv, m_mla_w_uk,
              m_mla_w_uv, m_mla_w_o, m_cf_w_pw1, m_cf_b_pw1, m_cf_dw_w, m_cf_dw_b, m_cf_norm_g, m_cf_norm_b, m_cf_w_pw2,
              m_cf_b_pw2, m_ff_w1, m_ff_w2, m_ln_mix_g, m_ln_mix_b, m_ln_ff_g, m_ln_ff_b)
    v_list = (v_sc_w_in, v_sc_conv_w, v_sc_w_out, v_mla_w_dq, v_mla_g_q, v_mla_w_uq, v_mla_w_dkv, v_mla_g_kv, v_mla_w_uk,
              v_mla_w_uv, v_mla_w_o, v_cf_w_pw1, v_cf_b_pw1, v_cf_dw_w, v_cf_dw_b, v_cf_norm_g, v_cf_norm_b, v_cf_w_pw2,
              v_cf_b_pw2, v_ff_w1, v_ff_w2, v_ln_mix_g, v_ln_mix_b, v_ln_ff_g, v_ln_ff_b)
    return _train_step(x, loss_target, dict(zip(NAMES, w_list)), dict(zip(NAMES, m_list)), dict(zip(NAMES, v_list)))
```

```python
import functools

import jax
import jax.numpy as jnp
from jax import lax
from jax.experimental import pallas as pl
from jax.experimental.pallas import tpu as pltpu

F32 = jnp.float32
MXU_DTYPE = jnp.bfloat16

N_DEV = 8
D_MODEL = 1024
DEPTH = 4
N_MIXERS = 3
CHUNK = 64
ALPHA = (2.0 * DEPTH) ** 0.25
LN_EPS = 1e-5
RMS_EPS = 1e-6
MLA_HEADS = 8
QK_NOPE = 128
QK_ROPE = 64
V_HEAD = 128
Q_LORA = 384
KV_LORA = 256
ROPE_THETA = 10000.0
ADAM_LR = 0.001
ADAM_B1 = 0.9
ADAM_B2 = 0.999
ADAM_EPS = 1e-08
ADAM_WD = 0.01
ADAM_STEP = 10

LANES = 128
SUBLANES = 8
VMEM_LIMIT = 56 * 1024 * 1024
CONV_PAD = 32
MASK_VALUE = -1e30

MESH_AXES = ("x", "y", "c")


def _params(semantics):
    return pltpu.CompilerParams(dimension_semantics=semantics, vmem_limit_bytes=VMEM_LIMIT)


def _tile(n, cap):
    if n <= cap:
        return n
    t = cap - cap % LANES
    while t >= LANES:
        if n % t == 0:
            return t
        t -= LANES
    raise ValueError(f"no tile for {n} under {cap}")


def _row_tile(rows, cap):
    if rows <= cap:
        return rows
    t = cap - cap % SUBLANES
    while t >= SUBLANES:
        if rows % t == 0:
            return t
        t -= SUBLANES
    raise ValueError(f"no row tile for {rows} under {cap}")


def _mm(a, b, *, name, ta=False, tb=False, b_dev=False, out_dev=False, out_dtypes=(F32,), epilogue=None,
        tile_extras=(), col_extras=(), after=(), tm_cap=1024, tn_cap=1024, tk_cap=1024):
    m, k = (a.shape[1], a.shape[0]) if ta else a.shape
    if b_dev:
        nd, b_rows, ns = b.shape
        n, kb = (b_rows, nd * ns) if tb else (nd * ns, b_rows)
    else:
        n, kb = (b.shape[0], b.shape[1]) if tb else (b.shape[1], b.shape[0])
    assert k == kb, (a.shape, b.shape, ta, tb)
    tm, tn, tk = _tile(m, tm_cap), _tile(n, tn_cap), _tile(k, tk_cap)
    if b_dev and tb:
        tk = ns
    elif b_dev:
        tn = ns
    if out_dev:
        assert not tile_extras and not col_extras and n % N_DEV == 0
        tn = n // N_DEV
    nk = k // tk
    n_te, n_ce, n_out = len(tile_extras), len(col_extras), len(out_dtypes)
    dims = (((0 if ta else 1,), (1 if tb else 0,)), ((), ()))

    def body(*refs):
        a_ref, b_ref = refs[0], refs[1]
        te_refs = refs[2:2 + n_te]
        ce_refs = refs[2 + n_te:2 + n_te + n_ce]
        first_out = 2 + n_te + n_ce + len(after)
        out_refs = refs[first_out:first_out + n_out]

        def finish(acc):
            if epilogue is None:
                outs = (acc,)
            else:
                outs = epilogue(acc, *[r[...] for r in te_refs], *[r[...] for r in ce_refs])
            for o_ref, o in zip(out_refs, outs):
                o_ref[...] = o.astype(o_ref.dtype)

        part = lax.dot_general(a_ref[...].astype(MXU_DTYPE), b_ref[...].astype(MXU_DTYPE), dims,
                               preferred_element_type=F32)
        if nk == 1:
            finish(part)
        else:
            acc_ref = refs[-1]
            kk = pl.program_id(2)

            @pl.when(kk == 0)
            def _():
                acc_ref[...] = part

            @pl.when(jnp.logical_and(kk > 0, kk < nk - 1))
            def _():
                acc_ref[...] += part

            @pl.when(kk == nk - 1)
            def _():
                finish(acc_ref[...] + part)

    a_spec = pl.BlockSpec((tk, tm), lambda i, j, kk: (kk, i)) if ta else pl.BlockSpec((tm, tk), lambda i, j, kk: (i, kk))
    if b_dev and tb:
        b_spec = pl.BlockSpec((None, tn, tk), lambda i, j, kk: (kk, j, 0))
    elif b_dev:
        b_spec = pl.BlockSpec((None, tk, tn), lambda i, j, kk: (j, kk, 0))
    elif tb:
        b_spec = pl.BlockSpec((tn, tk), lambda i, j, kk: (j, kk))
    else:
        b_spec = pl.BlockSpec((tk, tn), lambda i, j, kk: (kk, j))
    tile_spec = pl.BlockSpec((tm, tn), lambda i, j, kk: (i, j))
    col_spec = pl.BlockSpec((1, tn), lambda i, j, kk: (0, j))
    if out_dev:
        out_spec = pl.BlockSpec((None, tm, tn), lambda i, j, kk: (j, i, 0))
        out_shape = [jax.ShapeDtypeStruct((N_DEV, m, tn), dt) for dt in out_dtypes]
    else:
        out_spec = tile_spec
        out_shape = [jax.ShapeDtypeStruct((m, n), dt) for dt in out_dtypes]
    outs = pl.pallas_call(
        body,
        name=name,
        grid=(m // tm, n // tn, nk),
        in_specs=[a_spec, b_spec] + [tile_spec] * n_te + [col_spec] * n_ce + [ANY] * len(after),
        out_specs=[out_spec] * n_out,
        out_shape=out_shape,
        scratch_shapes=[pltpu.VMEM((tm, tn), F32)] if nk > 1 else [],
        compiler_params=_params(("parallel", "parallel", "arbitrary")),
    )(a, b, *tile_extras, *col_extras, *after)
    return outs[0] if n_out == 1 else tuple(outs)


def _rowwise(fn, rows, bcast, out_rows, out_accs, *, name, tb_cap=256, after=()):
    t = rows[0].shape[0]
    tb = _row_tile(t, tb_cap)
    n_r, n_b, n_o, n_a = len(rows), len(bcast), len(out_rows), len(out_accs)

    def body(*refs):
        r_refs = refs[:n_r]
        b_refs = refs[n_r:n_r + n_b]
        first_out = n_r + n_b + len(after)
        o_refs = refs[first_out:first_out + n_o]
        a_refs = refs[first_out + n_o:]
        outs, accs = fn([r[...] for r in r_refs], [r[...] for r in b_refs])
        for o_ref, o in zip(o_refs, outs):
            o_ref[...] = o.astype(o_ref.dtype)
        if n_a:
            i = pl.program_id(0)

            @pl.when(i == 0)
            def _():
                for a_ref, acc in zip(a_refs, accs):
                    a_ref[...] = acc

            @pl.when(i > 0)
            def _():
                for a_ref, acc in zip(a_refs, accs):
                    a_ref[...] += acc

    in_specs = [pl.BlockSpec((tb, r.shape[1]), lambda i: (i, 0)) for r in rows]
    in_specs += [pl.BlockSpec(b.shape, lambda i: (0, 0)) for b in bcast]
    in_specs += [ANY] * len(after)
    out_specs = [pl.BlockSpec((tb, c), lambda i: (i, 0)) for c, _ in out_rows]
    out_specs += [pl.BlockSpec(s, lambda i: (0, 0)) for s in out_accs]
    out_shape = [jax.ShapeDtypeStruct((t, c), dt) for c, dt in out_rows]
    out_shape += [jax.ShapeDtypeStruct(s, F32) for s in out_accs]
    outs = pl.pallas_call(
        body,
        name=name,
        grid=(t // tb,),
        in_specs=in_specs,
        out_specs=out_specs,
        out_shape=out_shape,
        compiler_params=_params(("arbitrary",)),
    )(*rows, *bcast, *after)
    return tuple(outs)


def _colsum(v):
    return jnp.sum(v, axis=0, keepdims=True)


def _sigmoid(v):
    return 1.0 / (1.0 + jnp.exp(-v))


def _ln_fwd(x_res, y_sub, g, b, *, name, after=()):
    d = x_res.shape[1]

    def fn(rows, bc):
        z = ALPHA * rows[0] + rows[1]
        mu = jnp.mean(z, axis=-1, keepdims=True)
        zc = z - mu
        var = jnp.mean(zc * zc, axis=-1, keepdims=True)
        rstd = lax.rsqrt(var + LN_EPS)
        xhat = zc * rstd
        out = xhat * bc[0] + bc[1]
        return (out, out, xhat, rstd), ()

    return _rowwise(fn, [x_res, y_sub], [g, b], [(d, F32), (d, MXU_DTYPE), (d, F32), (1, F32)], [], name=name,
                    after=after)


def _ln_bwd(terms, xhat, rstd, g, *, name, after=()):
    d = xhat.shape[1]
    coefs = [c for c, _ in terms]

    def fn(rows, bc):
        xh, rs = rows[0], rows[1]
        dout = None
        for c, v in zip(coefs, rows[2:]):
            tv = v if c == 1.0 else c * v
            dout = tv if dout is None else dout + tv
        dxh = dout * bc[0]
        m1 = jnp.mean(dxh, axis=-1, keepdims=True)
        m2 = jnp.mean(dxh * xh, axis=-1, keepdims=True)
        dz = rs * (dxh - m1 - xh * m2)
        return (dz, dz), (_colsum(dout * xh), _colsum(dout), _colsum(dz))

    return _rowwise(fn, [xhat, rstd] + [v for _, v in terms], [g], [(d, F32), (d, MXU_DTYPE)],
                    [(1, d), (1, d), (1, d)], name=name, after=after)


def _rms_fwd(x, g, *, name):
    c = x.shape[1]

    def fn(rows, bc):
        v = rows[0]
        r = lax.rsqrt(jnp.mean(v * v, axis=-1, keepdims=True) + RMS_EPS)
        return (v * r * bc[0],), ()

    return _rowwise(fn, [x], [g], [(c, MXU_DTYPE)], [], name=name)[0]


def _rms_bwd(x, g, dy, *, name):
    c = x.shape[1]

    def fn(rows, bc):
        v, dyv = rows
        r = lax.rsqrt(jnp.mean(v * v, axis=-1, keepdims=True) + RMS_EPS)
        xh = v * r
        dxh = dyv * bc[0]
        dx = r * (dxh - xh * jnp.mean(dxh * xh, axis=-1, keepdims=True))
        return (dx,), (_colsum(dyv * xh),)

    return _rowwise(fn, [x, dy], [g], [(c, F32)], [(1, c)], name=name)


def _conv_chunk(t):
    return _row_tile(t, 256)


def _conv_fwd(x, w, bias, *, name):
    t, c = x.shape
    kw = w.shape[0]
    ch = _conv_chunk(t)

    def body(x_ref, w_ref, b_ref, o_ref, pad_ref):
        pad_ref[0:CONV_PAD, :] = jnp.zeros((CONV_PAD, LANES), F32)
        pad_ref[CONV_PAD:CONV_PAD + t, :] = x_ref[...]
        for t0 in range(0, t, ch):
            acc = jnp.broadcast_to(b_ref[...], (ch, LANES))
            for k in range(kw):
                s = kw - 1 - k
                acc = acc + w_ref[k:k + 1, :] * pad_ref[CONV_PAD - s + t0:CONV_PAD - s + t0 + ch, :]
            o_ref[t0:t0 + ch, :] = acc

    return pl.pallas_call(
        body,
        name=name,
        grid=(c // LANES,),
        in_specs=[pl.BlockSpec((t, LANES), lambda j: (0, j)), pl.BlockSpec((kw, LANES), lambda j: (0, j)),
                  pl.BlockSpec((1, LANES), lambda j: (0, j))],
        out_specs=pl.BlockSpec((t, LANES), lambda j: (0, j)),
        out_shape=jax.ShapeDtypeStruct((t, c), F32),
        scratch_shapes=[pltpu.VMEM((CONV_PAD + t, LANES), F32)],
        compiler_params=_params(("parallel",)),
    )(x, w, bias)


def _conv_bwd(dy, x, w, *, name):
    t, c = x.shape
    kw = w.shape[0]
    ch = _conv_chunk(t)

    def body(dy_ref, x_ref, w_ref, dx_ref, dw_ref, xpad_ref, dpad_ref):
        xpad_ref[0:CONV_PAD, :] = jnp.zeros((CONV_PAD, LANES), F32)
        xpad_ref[CONV_PAD:CONV_PAD + t, :] = x_ref[...]
        dpad_ref[0:t, :] = dy_ref[...]
        dpad_ref[t:t + CONV_PAD, :] = jnp.zeros((CONV_PAD, LANES), F32)
        for t0 in range(0, t, ch):
            acc = jnp.zeros((ch, LANES), F32)
            for k in range(kw):
                s = kw - 1 - k
                acc = acc + w_ref[k:k + 1, :] * dpad_ref[t0 + s:t0 + s + ch, :]
            dx_ref[t0:t0 + ch, :] = acc
        for k in range(kw):
            s = kw - 1 - k
            acc = jnp.zeros((ch, LANES), F32)
            for t0 in range(0, t, ch):
                acc = acc + dy_ref[t0:t0 + ch, :] * xpad_ref[CONV_PAD - s + t0:CONV_PAD - s + t0 + ch, :]
            dw_ref[k:k + 1, :] = _colsum(acc)

    blk = pl.BlockSpec((t, LANES), lambda j: (0, j))
    wblk = pl.BlockSpec((kw, LANES), lambda j: (0, j))
    return pl.pallas_call(
        body,
        name=name,
        grid=(c // LANES,),
        in_specs=[blk, blk, wblk],
        out_specs=[blk, wblk],
        out_shape=[jax.ShapeDtypeStruct((t, c), F32), jax.ShapeDtypeStruct((kw, c), F32)],
        scratch_shapes=[pltpu.VMEM((CONV_PAD + t, LANES), F32), pltpu.VMEM((CONV_PAD + t, LANES), F32)],
        compiler_params=_params(("parallel",)),
    )(dy, x, w)


ATTN_SCALE = (QK_NOPE + QK_ROPE) ** -0.5
_NT = (((1,), (1,)), ((), ()))
_TN = (((0,), (0,)), ((), ()))
_NN = (((1,), (0,)), ((), ()))


def _dot(a, b, dims):
    return lax.dot_general(a, b, dims, preferred_element_type=F32)


def _scores(qn_ref, qp_ref, kn_ref, kp_ref, qi, tq, t):
    s = _dot(qn_ref[...], kn_ref[...], _NT) + _dot(qp_ref[...], kp_ref[...], _NT)
    s = s * ATTN_SCALE
    rows = qi * tq + lax.broadcasted_iota(jnp.int32, (tq, t), 0)
    cols = lax.broadcasted_iota(jnp.int32, (tq, t), 1)
    shift = CHUNK.bit_length() - 1
    allowed = jnp.right_shift(cols, shift) <= jnp.right_shift(rows, shift)
    return jnp.where(allowed, s, MASK_VALUE)


def _attn_fwd(qn, qp, kn, kp, v, *, name):
    h, t, _ = qn.shape
    tq = _row_tile(t, 256)

    def body(qn_ref, qp_ref, kn_ref, kp_ref, v_ref, o_ref, lse_ref):
        s = _scores(qn_ref, qp_ref, kn_ref, kp_ref, pl.program_id(1), tq, t)
        m = jnp.max(s, axis=-1, keepdims=True)
        p = jnp.exp(s - m)
        l = jnp.sum(p, axis=-1, keepdims=True)
        o = _dot((p / l).astype(MXU_DTYPE), v_ref[...], _NN)
        o_ref[...] = o.astype(o_ref.dtype)
        lse_ref[...] = m + jnp.log(l)

    qblk = lambda w: pl.BlockSpec((None, tq, w), lambda hh, qi: (hh, qi, 0))
    kblk = lambda w: pl.BlockSpec((None, t, w), lambda hh, qi: (hh, 0, 0))
    return pl.pallas_call(
        body,
        name=name,
        grid=(h, t // tq),
        in_specs=[qblk(QK_NOPE), qblk(QK_ROPE), kblk(QK_NOPE), pl.BlockSpec((t, QK_ROPE), lambda hh, qi: (0, 0)),
                  kblk(V_HEAD)],
        out_specs=[qblk(V_HEAD), qblk(1)],
        out_shape=[jax.ShapeDtypeStruct((h, t, V_HEAD), MXU_DTYPE), jax.ShapeDtypeStruct((h, t, 1), F32)],
        compiler_params=_params(("parallel", "parallel")),
    )(qn, qp, kn, kp, v)


def _attn_bwd(qn, qp, kn, kp, v, do, lse, *, name):
    h, t, _ = qn.shape
    tq = _row_tile(t, 256)

    def body(qn_ref, qp_ref, kn_ref, kp_ref, v_ref, do_ref, lse_ref, dqn_ref, dqp_ref, dkn_ref, dkp_ref, dv_ref):
        hh, qi = pl.program_id(0), pl.program_id(1)
        s = _scores(qn_ref, qp_ref, kn_ref, kp_ref, qi, tq, t)
        p = jnp.exp(s - lse_ref[...])
        dov = do_ref[...]
        dp = _dot(dov, v_ref[...], _NT)
        ds = p * (dp - jnp.sum(dp * p, axis=-1, keepdims=True)) * ATTN_SCALE
        ds_m = ds.astype(MXU_DTYPE)
        p_m = p.astype(MXU_DTYPE)
        dqn_ref[...] = _dot(ds_m, kn_ref[...], _NN)
        dqp_ref[...] = _dot(ds_m, kp_ref[...], _NN)
        dkn_part = _dot(ds_m, qn_ref[...], _TN)
        dkp_part = _dot(ds_m, qp_ref[...], _TN)
        dv_part = _dot(p_m, dov, _TN)

        @pl.when(qi == 0)
        def _():
            dkn_ref[...] = dkn_part
            dv_ref[...] = dv_part

        @pl.when(qi > 0)
        def _():
            dkn_ref[...] += dkn_part
            dv_ref[...] += dv_part

        first = jnp.logical_and(hh == 0, qi == 0)

        @pl.when(first)
        def _():
            dkp_ref[...] = dkp_part

        @pl.when(jnp.logical_not(first))
        def _():
            dkp_ref[...] += dkp_part

    qblk = lambda w: pl.BlockSpec((None, tq, w), lambda hh, qi: (hh, qi, 0))
    kblk = lambda w: pl.BlockSpec((None, t, w), lambda hh, qi: (hh, 0, 0))
    kpblk = pl.BlockSpec((t, QK_ROPE), lambda hh, qi: (0, 0))
    return pl.pallas_call(
        body,
        name=name,
        grid=(h, t // tq),
        in_specs=[qblk(QK_NOPE), qblk(QK_ROPE), kblk(QK_NOPE), kpblk, kblk(V_HEAD), qblk(V_HEAD), qblk(1)],
        out_specs=[qblk(QK_NOPE), qblk(QK_ROPE), kblk(QK_NOPE), kpblk, kblk(V_HEAD)],
        out_shape=[jax.ShapeDtypeStruct((h, t, QK_NOPE), F32), jax.ShapeDtypeStruct((h, t, QK_ROPE), F32),
                   jax.ShapeDtypeStruct((h, t, QK_NOPE), F32), jax.ShapeDtypeStruct((t, QK_ROPE), F32),
                   jax.ShapeDtypeStruct((h, t, V_HEAD), F32)],
        compiler_params=_params(("arbitrary", "arbitrary")),
    )(qn, qp, kn, kp, v, do, lse)


def _rope(x1, x2, cos, sin, *, name, inverse=False):
    w = x1.shape[1]
    sign = -1.0 if inverse else 1.0

    def fn(rows, bc):
        a, b, c, s = rows
        s = sign * s
        return (a * c - b * s, a * s + b * c), ()

    return _rowwise(fn, [x1, x2, cos, sin], [], [(w, F32), (w, F32)], [], name=name)


def _mlp_fwd(x_m, w1, w2, tag, after=()):
    def epi(acc):
        r = jnp.maximum(acc, 0.0)
        return acc, r * r

    hid, act = _mm(x_m, w1, b_dev=True, name=f"{tag}_up", out_dtypes=(F32, MXU_DTYPE), epilogue=epi, after=after)
    y = _mm(act, w2, name=f"{tag}_down")
    return y, (x_m, hid, act)


def _mlp_bwd(dy_m, saved, w1, w2, tag, after=()):
    x_m, hid, act = saved
    dw2 = _mm(act, dy_m, ta=True, name=f"{tag}_dw2", out_dtypes=(MXU_DTYPE,))

    def epi(acc, hv):
        return (acc * (2.0 * jnp.maximum(hv, 0.0)),)

    dh = _mm(dy_m, w2, tb=True, name=f"{tag}_dact", out_dtypes=(MXU_DTYPE,), epilogue=epi, tile_extras=(hid,),
             after=after)
    dw1 = _mm(x_m, dh, ta=True, out_dev=True, name=f"{tag}_dw1", out_dtypes=(MXU_DTYPE,))
    dx = _mm(dh, w1, tb=True, b_dev=True, name=f"{tag}_dx")
    return [dx], {"ff_w1": dw1, "ff_w2": dw2}


def _sc_fwd(x_m, w_in, conv_w, w_out, tag, after=()):
    d = D_MODEL
    u = _mm(x_m, w_in, b_dev=True, name=f"{tag}_in", after=after)

    def gate_fn(rows, bc):
        uv = rows[0]
        return (uv[:, d:2 * d] * uv[:, 2 * d:3 * d],), ()

    p = _rowwise(gate_fn, [u], [], [(d, F32)], [], name=f"{tag}_ch")[0]
    q = _conv_fwd(p, conv_w, jnp.zeros((1, d), F32), name=f"{tag}_conv")

    def out_fn(rows, bc):
        return (rows[0][:, 0:d] * rows[1],), ()

    r = _rowwise(out_fn, [u, q], [], [(d, MXU_DTYPE)], [], name=f"{tag}_bq")[0]
    y = _mm(r, w_out, name=f"{tag}_out")
    return y, (x_m, u, p, q, r)


def _sc_bwd(dy_m, saved, w_in, conv_w, w_out, tag, after=()):
    d = D_MODEL
    x_m, u, p, q, r = saved
    dw_out = _mm(r, dy_m, ta=True, name=f"{tag}_dwout", out_dtypes=(MXU_DTYPE,))
    dr = _mm(dy_m, w_out, tb=True, name=f"{tag}_dr", after=after)

    def dq_fn(rows, bc):
        return (rows[0] * rows[1][:, 0:d],), ()

    dq = _rowwise(dq_fn, [dr, u], [], [(d, F32)], [], name=f"{tag}_dq")[0]
    dp, dconv = _conv_bwd(dq, p, conv_w, name=f"{tag}_dconv")

    def du_fn(rows, bc):
        drv, qv, dpv, uv = rows
        return (jnp.concatenate([drv * qv, dpv * uv[:, 2 * d:3 * d], dpv * uv[:, d:2 * d]], axis=1),), ()

    du = _rowwise(du_fn, [dr, q, dp, u], [], [(3 * d, MXU_DTYPE)], [], name=f"{tag}_du")[0]
    dw_in = _mm(x_m, du, ta=True, out_dev=True, name=f"{tag}_dwin", out_dtypes=(MXU_DTYPE,))
    dx = _mm(du, w_in, tb=True, b_dev=True, name=f"{tag}_dx")
    return [dx], {"sc_w_in": dw_in, "sc_conv_w": dconv, "sc_w_out": dw_out}


def _cf_fwd(x_m, w, tag, after=()):
    d = D_MODEL

    def bias_epi(acc, bias):
        return (acc + bias,)

    u = _mm(x_m, w["cf_w_pw1"], b_dev=True, name=f"{tag}_pw1", epilogue=bias_epi, col_extras=(w["cf_b_pw1"],),
            after=after)

    def glu_fn(rows, bc):
        uv = rows[0]
        return (uv[:, 0:d] * _sigmoid(uv[:, d:2 * d]),), ()

    h1 = _rowwise(glu_fn, [u], [], [(d, F32)], [], name=f"{tag}_glu")[0]
    h2 = _conv_fwd(h1, w["cf_dw_w"], w["cf_dw_b"], name=f"{tag}_conv")

    def norm_fn(rows, bc):
        hv = rows[0]
        mu = jnp.mean(hv, axis=-1, keepdims=True)
        hc = hv - mu
        rstd = lax.rsqrt(jnp.mean(hc * hc, axis=-1, keepdims=True) + LN_EPS)
        h3 = hc * rstd * bc[0] + bc[1]
        return (h3 * _sigmoid(h3),), ()

    h4 = _rowwise(norm_fn, [h2], [w["cf_norm_g"], w["cf_norm_b"]], [(d, MXU_DTYPE)], [], name=f"{tag}_norm")[0]
    y = _mm(h4, w["cf_w_pw2"], name=f"{tag}_pw2", epilogue=bias_epi, col_extras=(w["cf_b_pw2"],))
    return y, (x_m, u, h1, h2, h4)


def _cf_bwd(dy_m, dy_colsum, saved, w, tag, after=()):
    d = D_MODEL
    x_m, u, h1, h2, h4 = saved
    dw_pw2 = _mm(h4, dy_m, ta=True, name=f"{tag}_dwpw2", out_dtypes=(MXU_DTYPE,))
    dh4 = _mm(dy_m, w["cf_w_pw2"], tb=True, name=f"{tag}_dh4", after=after)

    def dnorm_fn(rows, bc):
        dh4v, hv = rows
        g, b = bc
        mu = jnp.mean(hv, axis=-1, keepdims=True)
        hc = hv - mu
        rstd = lax.rsqrt(jnp.mean(hc * hc, axis=-1, keepdims=True) + LN_EPS)
        xh = hc * rstd
        h3 = xh * g + b
        sg = _sigmoid(h3)
        dh3 = dh4v * (sg * (1.0 + h3 * (1.0 - sg)))
        dxh = dh3 * g
        dh2 = rstd * (dxh - jnp.mean(dxh, axis=-1, keepdims=True) - xh * jnp.mean(dxh * xh, axis=-1, keepdims=True))
        return (dh2,), (_colsum(dh3 * xh), _colsum(dh3), _colsum(dh2))

    dh2, dnorm_g, dnorm_b, ddw_b = _rowwise(dnorm_fn, [dh4, h2], [w["cf_norm_g"], w["cf_norm_b"]], [(d, F32)],
                                            [(1, d), (1, d), (1, d)], name=f"{tag}_dnorm")
    dh1, ddw_w = _conv_bwd(dh2, h1, w["cf_dw_w"], name=f"{tag}_dconv")

    def dglu_fn(rows, bc):
        dh1v, uv = rows
        a, gate = uv[:, 0:d], uv[:, d:2 * d]
        sg = _sigmoid(gate)
        du = jnp.concatenate([dh1v * sg, dh1v * a * sg * (1.0 - sg)], axis=1)
        return (du,), (_colsum(du),)

    du, db_pw1 = _rowwise(dglu_fn, [dh1, u], [], [(2 * d, MXU_DTYPE)], [(1, 2 * d)], name=f"{tag}_dglu")
    dw_pw1 = _mm(x_m, du, ta=True, out_dev=True, name=f"{tag}_dwpw1", out_dtypes=(MXU_DTYPE,))
    dx = _mm(du, w["cf_w_pw1"], tb=True, b_dev=True, name=f"{tag}_dx")
    grads = {"cf_w_pw1": dw_pw1, "cf_b_pw1": db_pw1, "cf_dw_w": ddw_w, "cf_dw_b": ddw_b, "cf_norm_g": dnorm_g,
             "cf_norm_b": dnorm_b, "cf_w_pw2": dw_pw2, "cf_b_pw2": dy_colsum}
    return [dx], grads


def _rope_tables(t):
    pos = jnp.arange(t, dtype=F32)
    inv_freq = ROPE_THETA ** (-jnp.arange(0, QK_ROPE, 2, dtype=F32) / QK_ROPE)
    ang = pos[:, None] * inv_freq[None, :]
    return jnp.cos(ang), jnp.sin(ang)


def _heads(a, width):
    return a.reshape(a.shape[0], MLA_HEADS, width).transpose(1, 0, 2)


def _unheads(a):
    return a.transpose(1, 0, 2).reshape(a.shape[1], -1)


def _mla_fwd(x_m, w, tag, after=()):
    t = x_m.shape[0]
    half = QK_ROPE // 2
    cos, sin = _rope_tables(t)
    cos_h, sin_h = jnp.tile(cos, (1, MLA_HEADS)), jnp.tile(sin, (1, MLA_HEADS))
    cq_pre = _mm(x_m, w["mla_w_dq"], name=f"{tag}_dq", after=after)
    cq = _rms_fwd(cq_pre, w["mla_g_q"], name=f"{tag}_qnorm")
    q = _mm(cq, w["mla_w_uq"], name=f"{tag}_uq").reshape(t, MLA_HEADS, QK_NOPE + QK_ROPE)
    q1 = q[:, :, QK_NOPE:QK_NOPE + half].reshape(t, MLA_HEADS * half)
    q2 = q[:, :, QK_NOPE + half:].reshape(t, MLA_HEADS * half)
    r1, r2 = _rope(q1, q2, cos_h, sin_h, name=f"{tag}_qrope")
    qp = jnp.concatenate([r1.reshape(t, MLA_HEADS, half), r2.reshape(t, MLA_HEADS, half)], axis=-1)
    qn_h = q[:, :, :QK_NOPE].transpose(1, 0, 2).astype(MXU_DTYPE)
    qp_h = qp.transpose(1, 0, 2).astype(MXU_DTYPE)
    ckv_full = _mm(x_m, w["mla_w_dkv"], name=f"{tag}_dkv")
    ckv_pre = ckv_full[:, :KV_LORA]
    ckv = _rms_fwd(ckv_pre, w["mla_g_kv"], name=f"{tag}_kvnorm")
    k1, k2 = _rope(ckv_full[:, KV_LORA:KV_LORA + half], ckv_full[:, KV_LORA + half:], cos, sin, name=f"{tag}_krope")
    kp = jnp.concatenate([k1, k2], axis=-1).astype(MXU_DTYPE)
    kn_h = _heads(_mm(ckv, w["mla_w_uk"], name=f"{tag}_uk", out_dtypes=(MXU_DTYPE,)), QK_NOPE)
    v_h = _heads(_mm(ckv, w["mla_w_uv"], name=f"{tag}_uv", out_dtypes=(MXU_DTYPE,)), V_HEAD)
    o_h, lse = _attn_fwd(qn_h, qp_h, kn_h, kp, v_h, name=f"{tag}_attn")
    o_flat = _unheads(o_h)
    y = _mm(o_flat, w["mla_w_o"], name=f"{tag}_o")
    return y, (x_m, cq_pre, cq, ckv_pre, ckv, qn_h, qp_h, kn_h, kp, v_h, lse, o_flat, (cos, sin, cos_h, sin_h))


def _mla_bwd(dy_m, saved, w, tag, after=()):
    x_m, cq_pre, cq, ckv_pre, ckv, qn_h, qp_h, kn_h, kp, v_h, lse, o_flat, (cos, sin, cos_h, sin_h) = saved
    t = x_m.shape[0]
    half = QK_ROPE // 2
    dw_o = _mm(o_flat, dy_m, ta=True, name=f"{tag}_dwo", out_dtypes=(MXU_DTYPE,))
    do_h = _heads(_mm(dy_m, w["mla_w_o"], tb=True, name=f"{tag}_do", out_dtypes=(MXU_DTYPE,), after=after), V_HEAD)
    dqn_h, dqp_h, dkn_h, dkp, dv_h = _attn_bwd(qn_h, qp_h, kn_h, kp, v_h, do_h, lse, name=f"{tag}_dattn")
    dqp = dqp_h.transpose(1, 0, 2)
    d1, d2 = _rope(dqp[:, :, :half].reshape(t, -1), dqp[:, :, half:].reshape(t, -1), cos_h, sin_h,
                   name=f"{tag}_dqrope", inverse=True)
    dq = jnp.concatenate([dqn_h.transpose(1, 0, 2), d1.reshape(t, MLA_HEADS, half), d2.reshape(t, MLA_HEADS, half)],
                         axis=-1).reshape(t, -1).astype(MXU_DTYPE)
    dw_uq = _mm(cq, dq, ta=True, name=f"{tag}_dwuq", out_dtypes=(MXU_DTYPE,))
    dcq = _mm(dq, w["mla_w_uq"], tb=True, name=f"{tag}_dcq")
    dcq_pre, dg_q = _rms_bwd(cq_pre, w["mla_g_q"], dcq, name=f"{tag}_dqnorm")
    dcq_pre_m = dcq_pre.astype(MXU_DTYPE)
    dw_dq = _mm(x_m, dcq_pre_m, ta=True, name=f"{tag}_dwdq", out_dtypes=(MXU_DTYPE,))
    dx_q = _mm(dcq_pre_m, w["mla_w_dq"], tb=True, name=f"{tag}_dxq")
    dkn = _unheads(dkn_h).astype(MXU_DTYPE)
    dv = _unheads(dv_h).astype(MXU_DTYPE)
    dw_uk = _mm(ckv, dkn, ta=True, name=f"{tag}_dwuk", out_dtypes=(MXU_DTYPE,))
    dw_uv = _mm(ckv, dv, ta=True, name=f"{tag}_dwuv", out_dtypes=(MXU_DTYPE,))
    dckv_k = _mm(dkn, w["mla_w_uk"], tb=True, name=f"{tag}_dckvk")
    dckv = _mm(dv, w["mla_w_uv"], tb=True, name=f"{tag}_dckv", epilogue=lambda acc, other: (acc + other,),
               tile_extras=(dckv_k,))
    dckv_pre, dg_kv = _rms_bwd(ckv_pre, w["mla_g_kv"], dckv, name=f"{tag}_dkvnorm")
    dk1, dk2 = _rope(dkp[:, :half], dkp[:, half:], cos, sin, name=f"{tag}_dkrope", inverse=True)
    dckv_full = jnp.concatenate([dckv_pre, dk1, dk2], axis=-1).astype(MXU_DTYPE)
    dw_dkv = _mm(x_m, dckv_full, ta=True, name=f"{tag}_dwdkv", out_dtypes=(MXU_DTYPE,))
    dx_kv = _mm(dckv_full, w["mla_w_dkv"], tb=True, name=f"{tag}_dxkv")
    grads = {"mla_w_dq": dw_dq, "mla_g_q": dg_q, "mla_w_uq": dw_uq, "mla_w_dkv": dw_dkv, "mla_g_kv": dg_kv,
             "mla_w_uk": dw_uk, "mla_w_uv": dw_uv, "mla_w_o": dw_o}
    return [dx_q, dx_kv], grads


def _cast_input(x):
    cast = lambda rows, bc: ((rows[0],), ())
    return _rowwise(cast, [x], [], [(D_MODEL, MXU_DTYPE)], [], name="x_cast")[0]


def _hooked(value, hook):
    token = None if hook is None else hook(value)
    return () if token is None else (token,)


def _mix_fwd(i, cur, cur_m, wl, hook=None, after=()):
    mixer = i % N_MIXERS
    if mixer == 0:
        y, s_mix = _sc_fwd(cur_m, wl["sc_w_in"], wl["sc_conv_w"], wl["sc_w_out"], f"l{i}_sc", after=after)
    elif mixer == 1:
        y, s_mix = _mla_fwd(cur_m, wl, f"l{i}_mla", after=after)
    else:
        y, s_mix = _cf_fwd(cur_m, wl, f"l{i}_cf", after=after)
    cur, cur_m, xh_mix, rs_mix = _ln_fwd(cur, y, wl["ln_mix_g"], wl["ln_mix_b"], name=f"l{i}_lnmix",
                                         after=_hooked(y, hook))
    return cur, cur_m, (s_mix, xh_mix, rs_mix)


def _ff_fwd(i, cur, cur_m, wl, hook=None, after=()):
    y, s_ff = _mlp_fwd(cur_m, wl["ff_w1"], wl["ff_w2"], f"l{i}_ff", after=after)
    cur, cur_m, xh_ff, rs_ff = _ln_fwd(cur, y, wl["ln_ff_g"], wl["ln_ff_b"], name=f"l{i}_lnff", after=_hooked(y, hook))
    return cur, cur_m, (s_ff, xh_ff, rs_ff)


def _layer_fwd(i, cur, cur_m, wl):
    cur, cur_m, s_mix = _mix_fwd(i, cur, cur_m, wl)
    cur, cur_m, s_ff = _ff_fwd(i, cur, cur_m, wl)
    return cur, cur_m, (*s_mix, *s_ff)


def _loss_and_grad(out, target):
    d = D_MODEL

    def loss_fn(rows, bc):
        err = rows[0] - rows[1]
        return (err * (1.0 / d),), (_colsum(err * err),)

    dy, sq = _rowwise(loss_fn, [out, target], [], [(d, F32)], [(1, d)], name="loss")
    return (0.5 / d) * jnp.sum(sq), dy


def _ff_bwd(i, terms, saved, wl, hook=None, after=()):
    s_ff, xh_ff, rs_ff = saved
    grads = {}
    dz, dz_m, grads["ln_ff_g"], grads["ln_ff_b"], _ = _ln_bwd(terms, xh_ff, rs_ff, wl["ln_ff_g"], name=f"l{i}_dlnff",
                                                              after=after)
    dxs, g_ff = _mlp_bwd(dz_m, s_ff, wl["ff_w1"], wl["ff_w2"], f"l{i}_ff", after=_hooked(dz, hook))
    grads.update(g_ff)
    return [(ALPHA, dz)] + [(1.0, v) for v in dxs], grads


def _mix_bwd(i, terms, saved, wl, hook=None, after=()):
    mixer = i % N_MIXERS
    s_mix, xh_mix, rs_mix = saved
    grads = {}
    dz, dz_m, grads["ln_mix_g"], grads["ln_mix_b"], dz_sum = _ln_bwd(terms, xh_mix, rs_mix, wl["ln_mix_g"],
                                                                     name=f"l{i}_dlnmix", after=after)
    behind = _hooked(dz, hook)
    if mixer == 0:
        dxs, g_mix = _sc_bwd(dz_m, s_mix, wl["sc_w_in"], wl["sc_conv_w"], wl["sc_w_out"], f"l{i}_sc", after=behind)
    elif mixer == 1:
        dxs, g_mix = _mla_bwd(dz_m, s_mix, wl, f"l{i}_mla", after=behind)
    else:
        dxs, g_mix = _cf_bwd(dz_m, dz_sum, s_mix, wl, f"l{i}_cf", after=behind)
    grads.update(g_mix)
    return [(ALPHA, dz)] + [(1.0, v) for v in dxs], grads


def _layer_bwd(i, terms, saved, wl):
    terms, g_ff = _ff_bwd(i, terms, saved[3:], wl)
    terms, g_mix = _mix_bwd(i, terms, saved[:3], wl)
    return terms, {**g_ff, **g_mix}


def _sum_terms(terms, *, name):
    coefs = [c for c, _ in terms]

    def sum_fn(rows, bc):
        acc = None
        for c, v in zip(coefs, rows):
            tv = v if c == 1.0 else c * v
            acc = tv if acc is None else acc + tv
        return (acc,), ()

    return _rowwise(sum_fn, [v for _, v in terms], [], [(D_MODEL, F32)], [], name=name)[0]


def _local_step(x, target, layers):
    cur, cur_m = x, _cast_input(x)
    saved = []
    for i in range(DEPTH):
        cur, cur_m, s = _layer_fwd(i, cur, cur_m, layers[i])
        saved.append(s)
    loss, dy = _loss_and_grad(cur, target)
    terms = [(1.0, dy)]
    grads = [None] * DEPTH
    for i in reversed(range(DEPTH)):
        terms, grads[i] = _layer_bwd(i, terms, saved[i], layers[i])
    return loss, _sum_terms(terms, name="grad_x"), grads


MESH_ID = pl.DeviceIdType.MESH
ANY = pl.BlockSpec(memory_space=pl.ANY)


def _block_id(px, py, pc):
    return 4 * px + 2 * py + pc


COPIES_PER_GATHER = 7


def _all_gather(shards, *, name):
    n = len(shards)

    def body(*refs):
        x_refs, out_refs = refs[:n], refs[n:2 * n]
        send_sems, recv_sems, local_sems = refs[2 * n:]
        x, y, core = lax.axis_index("x"), lax.axis_index("y"), lax.axis_index("c")
        me, sibling = (x, y, core), (x, y, 1 - core)
        chips = [(1 - x, y), (x, 1 - y), (1 - x, 1 - y)]

        def copy(t, k, block, to, from_input=False):
            dst = out_refs[t].at[_block_id(*block)]
            return pltpu.make_async_remote_copy(
                src_ref=x_refs[t] if from_input else dst, dst_ref=dst,
                send_sem=send_sems.at[t * COPIES_PER_GATHER + k], recv_sem=recv_sems.at[t * COPIES_PER_GATHER + k],
                device_id=to, device_id_type=MESH_ID)

        started = []
        mine = [pltpu.make_async_copy(x_refs[t], out_refs[t].at[_block_id(*me)], local_sems.at[t]) for t in range(n)]
        for t in range(n):
            mine[t].start()
            first = [copy(t, 0, me, sibling, from_input=True)]
            first += [copy(t, 1 + j, me, (*chip, core), from_input=True) for j, chip in enumerate(chips)]
            for cp in first:
                cp.start()
            started += first
        for t in range(n):
            for j, chip in enumerate(chips):
                copy(t, 1 + j, (*chip, core), me).wait_recv()
                passed = copy(t, 4 + j, (*chip, core), sibling)
                passed.start()
                started.append(passed)
        for t in range(n):
            copy(t, 0, sibling, me).wait_recv()
            for j, chip in enumerate(chips):
                copy(t, 4 + j, (*chip, 1 - core), me).wait_recv()
        for cp in started:
            cp.wait_send()
        for cp in mine:
            cp.wait()

    n_sems = n * COPIES_PER_GATHER
    outs = pl.pallas_call(
        body,
        name=name,
        out_shape=[jax.ShapeDtypeStruct((N_DEV, *s.shape), s.dtype) for s in shards],
        in_specs=[ANY] * n,
        out_specs=[ANY] * n,
        scratch_shapes=[pltpu.SemaphoreType.DMA((n_sems,)), pltpu.SemaphoreType.DMA((n_sems,)),
                        pltpu.SemaphoreType.DMA((n,))],
    )(*shards)
    return list(outs)


def _swap_with_sibling(parts, *, name):
    n = len(parts)

    def body(*refs):
        p_refs, got_refs = refs[:n], refs[n:2 * n]
        send_sems, recv_sems = refs[2 * n:]
        x, y, core = lax.axis_index("x"), lax.axis_index("y"), lax.axis_index("c")
        chips = [(x, y), (1 - x, y), (x, 1 - y), (1 - x, 1 - y)]
        copies = []
        for t in range(n):
            for k, (px, py) in enumerate(chips):
                copies.append(pltpu.make_async_remote_copy(
                    src_ref=p_refs[t].at[_block_id(px, py, 1 - core)], dst_ref=got_refs[t].at[k],
                    send_sem=send_sems.at[4 * t + k], recv_sem=recv_sems.at[4 * t + k], device_id=(x, y, 1 - core),
                    device_id_type=MESH_ID))
        for cp in copies:
            cp.start()
        for cp in copies:
            cp.wait()

    outs = pl.pallas_call(
        body,
        name=name,
        out_shape=[jax.ShapeDtypeStruct((4, *p.shape[1:]), p.dtype) for p in parts],
        in_specs=[ANY] * n,
        out_specs=[ANY] * n,
        scratch_shapes=[pltpu.SemaphoreType.DMA((4 * n,)), pltpu.SemaphoreType.DMA((4 * n,))],
    )(*parts)
    return list(outs)


def _swap_between_chips(parts, *, name):
    n = len(parts)

    def body(*refs):
        p_refs, got_refs = refs[:n], refs[n:2 * n]
        send_sems, recv_sems = refs[2 * n:]
        x, y, core = lax.axis_index("x"), lax.axis_index("y"), lax.axis_index("c")
        chips = [(1 - x, y), (x, 1 - y), (1 - x, 1 - y)]
        copies = []
        for t in range(n):
            for j, (px, py) in enumerate(chips):
                copies.append(pltpu.make_async_remote_copy(
                    src_ref=p_refs[t].at[j], dst_ref=got_refs[t].at[j], send_sem=send_sems.at[3 * t + j],
                    recv_sem=recv_sems.at[3 * t + j], device_id=(px, py, core), device_id_type=MESH_ID))
        for cp in copies:
            cp.start()
        for cp in copies:
            cp.wait()

    outs = pl.pallas_call(
        body,
        name=name,
        out_shape=[jax.ShapeDtypeStruct(p.shape, p.dtype) for p in parts],
        in_specs=[ANY] * n,
        out_specs=[ANY] * n,
        scratch_shapes=[pltpu.SemaphoreType.DMA((3 * n,)), pltpu.SemaphoreType.DMA((3 * n,))],
    )(*parts)
    return list(outs)


def _chip_partials(parts, got, *, name):
    _, r, c = parts.shape
    tb = _row_tile(r, 512)
    x, y, core = lax.axis_index("x"), lax.axis_index("y"), lax.axis_index("c")
    chips = [(x, y), (1 - x, y), (x, 1 - y), (1 - x, 1 - y)]
    ids = jnp.stack([_block_id(px, py, core) for px, py in chips]).astype(jnp.int32)

    def body(ids_ref, p0, p1, p2, p3, g_ref, own_ref, out_ref):
        own_ref[...] = p0[...].astype(F32) + g_ref[0].astype(F32)
        for j, p in enumerate((p1, p2, p3)):
            out_ref[j] = (p[...].astype(F32) + g_ref[j + 1].astype(F32)).astype(out_ref.dtype)

    def part_spec(k):
        return pl.BlockSpec((None, tb, c), lambda i, ids_ref: (ids_ref[k], i, 0))

    grid_spec = pltpu.PrefetchScalarGridSpec(
        num_scalar_prefetch=1,
        grid=(r // tb,),
        in_specs=[part_spec(k) for k in range(4)] + [pl.BlockSpec((4, tb, c), lambda i, ids_ref: (0, i, 0))],
        out_specs=[pl.BlockSpec((tb, c), lambda i, ids_ref: (i, 0)), pl.BlockSpec((3, tb, c), lambda i, ids_ref: (0, i, 0))],
    )
    return pl.pallas_call(
        body,
        name=name,
        grid_spec=grid_spec,
        out_shape=[jax.ShapeDtypeStruct((r, c), F32), jax.ShapeDtypeStruct((3, r, c), parts.dtype)],
        compiler_params=_params(("parallel",)),
    )(ids, parts, parts, parts, parts, got)


def _sum_blocks(first, blocks, *, name):
    n, r, c = blocks.shape
    tb = _row_tile(r, 512)

    def body(*refs):
        b_ref, o_ref = refs[-2], refs[-1]
        acc = refs[0][...] if first is not None else b_ref[0].astype(F32)
        for j in range(0 if first is not None else 1, n):
            acc = acc + b_ref[j].astype(F32)
        o_ref[...] = acc

    row_spec = pl.BlockSpec((tb, c), lambda i: (i, 0))
    return pl.pallas_call(
        body,
        name=name,
        grid=(r // tb,),
        in_specs=([row_spec] if first is not None else []) + [pl.BlockSpec((n, tb, c), lambda i: (0, i, 0))],
        out_specs=row_spec,
        out_shape=jax.ShapeDtypeStruct((r, c), F32),
        compiler_params=_params(("parallel",)),
    )(*([first] if first is not None else []), blocks)


def _reduce_scatter(parts, *, tag):
    got = _swap_with_sibling(parts, name=f"{tag}_sibling")
    sums = [_chip_partials(p, g, name=f"{tag}_chipsum{t}") for t, (p, g) in enumerate(zip(parts, got))]
    arrived = _swap_between_chips([others for _, others in sums], name=f"{tag}_chips")
    return [(own, arr) for (own, _), arr in zip(sums, arrived)]


HBM = pl.BlockSpec(memory_space=pltpu.HBM)
SEM = pl.BlockSpec(memory_space=pltpu.SEMAPHORE)
DATAFLOW = pltpu.SideEffectType.DATAFLOW_SIDE_EFFECTING


def _tie(value, *before):
    return lax.optimization_barrier((value, *before))[0]


def _split_call(name, arrays, sems, n_new, body, after=None):
    n_a, n_s = len(arrays), len(sems)
    n_after = 0 if after is None else 1

    def kernel_body(*refs):
        new = refs[n_a + n_s + n_after:n_a + n_s + n_after + 2] if n_new else (None, None)
        body(refs[:n_a], refs[n_a:n_a + n_s], *new)
        refs[-1][...] = jnp.zeros_like(refs[-1])

    n_sem_out = 2 if n_new else 0
    out_shape = [pltpu.SemaphoreType.DMA((n_new,))] * n_sem_out
    out_shape += [pltpu.HBM(a.shape, a.dtype) for a in arrays]
    out_shape += [jax.ShapeDtypeStruct((SUBLANES, LANES), F32)]
    outs = pl.pallas_call(
        kernel_body,
        name=name,
        out_shape=out_shape,
        in_specs=[HBM] * n_a + [SEM] * n_s + [ANY] * n_after,
        out_specs=[SEM] * n_sem_out + [HBM] * n_a + [pl.BlockSpec(memory_space=pltpu.VMEM)],
        input_output_aliases={i: n_sem_out + i for i in range(n_a)},
        compiler_params=pltpu.CompilerParams(has_side_effects=DATAFLOW),
    )(*[pltpu.with_memory_space_constraint(a, pltpu.HBM) for a in arrays], *sems, *([after] if n_after else []))
    return tuple(outs[:n_sem_out]), list(outs[n_sem_out:n_sem_out + n_a]), outs[-1]


def _remote(src, dst, send, recv, k, to):
    return pltpu.make_async_remote_copy(src_ref=src, dst_ref=dst, send_sem=send.at[k], recv_sem=recv.at[k],
                                        device_id=to, device_id_type=MESH_ID)


def _gather_start(shards, *, name):
    n = len(shards)
    me = _block_id(lax.axis_index("x"), lax.axis_index("y"), lax.axis_index("c"))
    lands = [lax.dynamic_update_index_in_dim(lax.empty((N_DEV, *s.shape), s.dtype), s, me, 0) for s in shards]

    def body(refs, _, send, recv):
        x, y, core = lax.axis_index("x"), lax.axis_index("y"), lax.axis_index("c")
        targets = [(x, y, 1 - core), (1 - x, y, core), (x, 1 - y, core), (1 - x, 1 - y, core)]
        for t in range(n):
            for k, to in enumerate(targets):
                _remote(refs[t], refs[n + t].at[_block_id(x, y, core)], send, recv, 4 * t + k, to).start()

    sems, thru, token = _split_call(name, shards + lands, [], 4 * n, body)
    return {"n": n, "sems1": sems, "arrays": thru, "token": token}


def _gather_pass_on(st, *, name, after):
    n = st["n"]

    def body(refs, sems, send, recv):
        send1, recv1 = sems
        x, y, core = lax.axis_index("x"), lax.axis_index("y"), lax.axis_index("c")
        chips = [(1 - x, y), (x, 1 - y), (1 - x, 1 - y)]
        for t in range(n):
            for j, (px, py) in enumerate(chips):
                block = refs[n + t].at[_block_id(px, py, core)]
                _remote(refs[t], block, send1, recv1, 4 * t + 1 + j, (x, y, 1 - core)).wait_recv()
                _remote(block, block, send, recv, 3 * t + j, (x, y, 1 - core)).start()

    sems2, thru, token = _split_call(name, st["arrays"], list(st["sems1"]), 3 * n, body, after=after)
    return {**st, "sems2": sems2, "arrays": thru, "token": token}


def _gather_finish(st, *, name, after):
    n = st["n"]

    def body(refs, sems, *_):
        send1, recv1, send2, recv2 = sems
        x, y, core = lax.axis_index("x"), lax.axis_index("y"), lax.axis_index("c")
        sibling = (x, y, 1 - core)
        chips = [(1 - x, y), (x, 1 - y), (1 - x, 1 - y)]
        for t in range(n):
            land = refs[n + t]
            for k in range(4):
                _remote(refs[t], land.at[_block_id(x, y, core)], send1, recv1, 4 * t + k, sibling).wait_send()
            _remote(refs[t], land.at[_block_id(x, y, 1 - core)], send1, recv1, 4 * t, sibling).wait_recv()
            for j, (px, py) in enumerate(chips):
                _remote(land.at[_block_id(px, py, core)], land.at[_block_id(px, py, core)], send2, recv2, 3 * t + j,
                        sibling).wait_send()
                _remote(refs[t], land.at[_block_id(px, py, 1 - core)], send2, recv2, 3 * t + j, sibling).wait_recv()

    _, thru, _ = _split_call(name, st["arrays"], [*st["sems1"], *st["sems2"]], 0, body, after=after)
    return thru[n:]


def _swap_start(sources, lands, plan, per_array, *, name):
    n = len(sources)

    def body(refs, _, send, recv):
        x, y, core = lax.axis_index("x"), lax.axis_index("y"), lax.axis_index("c")
        for t in range(n):
            for k in range(per_array):
                block, to = plan(x, y, core, k)
                _remote(refs[t].at[block], refs[n + t].at[k], send, recv, per_array * t + k, to).start()

    sems, thru, token = _split_call(name, sources + lands, [], per_array * n, body)
    return {"n": n, "sems": sems, "arrays": thru, "token": token, "plan": plan, "per_array": per_array}


def _swap_finish(st, *, name, after):
    n, plan, per_array = st["n"], st["plan"], st["per_array"]

    def body(refs, sems, *_):
        send, recv = sems
        x, y, core = lax.axis_index("x"), lax.axis_index("y"), lax.axis_index("c")
        for t in range(n):
            for k in range(per_array):
                block, to = plan(x, y, core, k)
                cp = _remote(refs[t].at[block], refs[n + t].at[k], send, recv, per_array * t + k, to)
                cp.wait_send()
                cp.wait_recv()

    _, thru, _ = _split_call(name, st["arrays"], list(st["sems"]), 0, body, after=after)
    return thru[:n], thru[n:]


def _to_sibling(x, y, core, k):
    px, py = [(x, y), (1 - x, y), (x, 1 - y), (1 - x, 1 - y)][k]
    return _block_id(px, py, 1 - core), (x, y, 1 - core)


def _to_chip(x, y, core, k):
    px, py = [(1 - x, y), (x, 1 - y), (1 - x, 1 - y)][k]
    return k, (px, py, core)


PARAMS = {
    "sc_w_in": ((2, 1024, 3072), 2), "sc_conv_w": ((2, 3, 1024), 2), "sc_w_out": ((2, 1024, 1024), 1),
    "mla_w_dq": ((1, 1024, 384), 1), "mla_g_q": ((1, 384), None), "mla_w_uq": ((1, 384, 1536), 2),
    "mla_w_dkv": ((1, 1024, 320), 1), "mla_g_kv": ((1, 256), None), "mla_w_uk": ((1, 256, 8, 128), 1),
    "mla_w_uv": ((1, 256, 8, 128), 1), "mla_w_o": ((1, 1024, 1024), 1), "cf_w_pw1": ((1, 1024, 2048), 2),
    "cf_b_pw1": ((1, 2048), 1), "cf_dw_w": ((1, 31, 1024), 2), "cf_dw_b": ((1, 1024), 1), "cf_norm_g": ((1, 1024), 1),
    "cf_norm_b": ((1, 1024), 1), "cf_w_pw2": ((1, 1024, 1024), 1), "cf_b_pw2": ((1, 1024), 1),
    "ff_w1": ((4, 1024, 4096), 2), "ff_w2": ((4, 4096, 1024), 1), "ln_mix_g": ((4, 1024), None),
    "ln_mix_b": ((4, 1024), None), "ln_ff_g": ((4, 1024), None), "ln_ff_b": ((4, 1024), None),
}
NAMES = list(PARAMS)
BIG = ["sc_w_in", "sc_w_out", "mla_w_dq", "mla_w_uq", "mla_w_dkv", "mla_w_uk", "mla_w_uv", "mla_w_o", "cf_w_pw1",
       "cf_w_pw2", "ff_w1", "ff_w2"]
SMALL = [n for n in NAMES if n not in BIG]
SMALL_SHARDED = [n for n in SMALL if PARAMS[n][1] is not None]
SMALL_WIDTH = LANES
DEV_BLOCKED = ("sc_w_in", "cf_w_pw1", "ff_w1")
MIXER_BIG = {0: ["sc_w_in", "sc_w_out"], 1: ["mla_w_dq", "mla_w_uq", "mla_w_dkv", "mla_w_uk", "mla_w_uv", "mla_w_o"],
             2: ["cf_w_pw1", "cf_w_pw2"]}


def _layer_big(i):
    return [(n, i // N_MIXERS) for n in MIXER_BIG[i % N_MIXERS]] + [("ff_w1", i), ("ff_w2", i)]


def _shard_2d(name, a):
    return a.reshape(a.shape[0], -1)


def _as_operand(name, g):
    if name in DEV_BLOCKED:
        return g
    if name == "mla_w_uq":
        return g.transpose(1, 0, 2).reshape(g.shape[1], -1)
    return g.reshape(-1, g.shape[2])


def _as_parts(name, grad):
    if name in DEV_BLOCKED:
        return grad
    if name == "mla_w_uq":
        return grad.reshape(grad.shape[0], N_DEV, -1).transpose(1, 0, 2)
    return grad.reshape(N_DEV, -1, grad.shape[1])


def _shard_shape(name):
    shape, ax = PARAMS[name]
    if ax is None:
        return shape
    return tuple(s // N_DEV if i == ax else s for i, s in enumerate(shape))


def _rows_of(shape, width, align):
    n = 1
    for s in shape:
        n *= s
    rows = -(-n // width)
    return -(-rows // align) * align


def _pack_rows(arrays, width, align, dtype, lead=0):
    segs = []
    for a in arrays:
        batch = a.shape[:lead]
        flat = a.astype(dtype).reshape(*batch, -1)
        rows = _rows_of(a.shape[lead:], width, align)
        flat = jnp.pad(flat, [(0, 0)] * lead + [(0, rows * width - flat.shape[-1])])
        segs.append(flat.reshape(*batch, rows, width))
    return jnp.concatenate(segs, axis=lead)


def _unpack_rows(buf, shapes, width, align):
    out, off = [], 0
    lead = buf.shape[:-2]
    for shape in shapes:
        n = 1
        for s in shape:
            n *= s
        rows = _rows_of(shape, width, align)
        seg = buf[..., off:off + rows, :].reshape(*lead, rows * width)[..., :n]
        out.append(seg.reshape(*lead, *shape))
        off += rows
    return out


def _merge_shards(stacked, ax):
    moved = jnp.moveaxis(stacked, 0, ax)
    shape = moved.shape
    return moved.reshape(*shape[:ax], shape[ax] * shape[ax + 1], *shape[ax + 2:])


def _split_shards(full, ax):
    shape = full.shape
    split = full.reshape(*shape[:ax], N_DEV, shape[ax] // N_DEV, *shape[ax + 1:])
    return jnp.moveaxis(split, ax, 0)


def _adam_math(wv, gv, mv, vv):
    m_new = ADAM_B1 * mv + (1.0 - ADAM_B1) * gv
    v_new = ADAM_B2 * vv + (1.0 - ADAM_B2) * (gv * gv)
    m_hat = m_new / (1.0 - ADAM_B1 ** ADAM_STEP)
    v_hat = v_new / (1.0 - ADAM_B2 ** ADAM_STEP)
    delta = -ADAM_LR * (m_hat / (jnp.sqrt(v_hat) + ADAM_EPS) + ADAM_WD * wv)
    return delta, m_new, v_new


def _adamw(w, g, m, v, *, name):
    shape = w.shape
    c = shape[-1]

    def fn(rows, bc):
        return _adam_math(*rows), ()

    flat = [a.reshape(-1, c) for a in (w, g, m, v)]
    outs = _rowwise(fn, flat, [], [(c, F32)] * 3, [], name=name, tb_cap=512)
    return tuple(o.reshape(shape) for o in outs)


def _adamw_reduced(w, m, v, layer_grads, *, name):
    n_l, r, c = w.shape
    tb = _row_tile(r, 256)
    nb = r // tb

    def body(*refs):
        w_ref, m_ref, v_ref = refs[:3]
        g_refs = refs[3:3 + 2 * n_l]
        g_out, d_out, m_out, v_out = refs[3 + 2 * n_l:]
        layer = pl.program_id(0)
        for l in range(n_l):
            @pl.when(layer == l)
            def _(l=l):
                own, arr = g_refs[2 * l], g_refs[2 * l + 1]
                g = own[...] + arr[0].astype(F32)
                g = g + arr[1].astype(F32)
                g = g + arr[2].astype(F32)
                delta, m_new, v_new = _adam_math(w_ref[...], g, m_ref[...], v_ref[...])
                g_out[...] = g
                d_out[...] = delta
                m_out[...] = m_new
                v_out[...] = v_new

    stacked = pl.BlockSpec((tb, c), lambda l, i: (l * nb + i, 0))
    in_specs = [stacked] * 3
    operands = [a.reshape(n_l * r, c) for a in (w, m, v)]
    for l, (own, arr) in enumerate(layer_grads):
        in_specs.append(pl.BlockSpec((tb, c), lambda ll, i, l=l: (jnp.where(ll == l, i, 0), 0)))
        in_specs.append(pl.BlockSpec((3, tb, c), lambda ll, i, l=l: (0, jnp.where(ll == l, i, 0), 0)))
        operands += [own, arr]
    outs = pl.pallas_call(
        body,
        name=name,
        grid=(n_l, nb),
        in_specs=in_specs,
        out_specs=[stacked] * 4,
        out_shape=[jax.ShapeDtypeStruct((n_l * r, c), F32)] * 4,
        compiler_params=_params(("arbitrary", "arbitrary")),
    )(*operands)
    return tuple(o.reshape(n_l, r, c) for o in outs)


def _train_step(x, target, weights, m_state, v_state):
    me = _block_id(lax.axis_index("x"), lax.axis_index("y"), lax.axis_index("c"))

    small_shapes = [_shard_shape(n) for n in SMALL_SHARDED]
    small_all = _all_gather([_pack_rows([weights[n] for n in SMALL_SHARDED], SMALL_WIDTH, SUBLANES, F32)],
                            name="gather_small")[0]
    small_full = {n: weights[n] for n in SMALL if PARAMS[n][1] is None}
    for n, stacked in zip(SMALL_SHARDED, _unpack_rows(small_all, small_shapes, SMALL_WIDTH, SUBLANES)):
        small_full[n] = _merge_shards(stacked, PARAMS[n][1])

    def small_weights(i):
        wl = {}
        for n in SMALL:
            if n.startswith("ln_"):
                wl[n] = small_full[n][i:i + 1]
            elif n.startswith(("sc_", "mla_", "cf_")[i % N_MIXERS]):
                v = small_full[n][i // N_MIXERS]
                wl[n] = v if v.ndim == 2 else v.reshape(1, -1)
        return wl

    layers = [small_weights(i) for i in range(DEPTH)]
    groups = [(kind, i) for i in range(DEPTH) for kind in ("mix", "ff")]

    def group_names(g):
        kind, i = g
        if kind == "mix":
            return [(n, i // N_MIXERS) for n in MIXER_BIG[i % N_MIXERS]]
        return [("ff_w1", i), ("ff_w2", i)]

    def shards_of(g):
        return [_shard_2d(n, weights[n][j]).astype(MXU_DTYPE) for n, j in group_names(g)]

    def install(g, gathered):
        for (n, _), a in zip(group_names(g), gathered):
            layers[g[1]][n] = _as_operand(n, a)

    install(groups[0], _all_gather(shards_of(groups[0]), name="gather_g0"))
    gather = {1: _gather_start(shards_of(groups[1]), name="gather_g1_start")}
    cur, cur_m = x[0], _cast_input(x[0])
    saved = {}
    for k, g in enumerate(groups):
        kind, i = g
        tokens = [gather[1]["token"]] if k == 0 else []
        if k + 2 < len(groups):
            gather[k + 2] = _gather_start(shards_of(groups[k + 2]), name=f"gather_g{k + 2}_start")
            tokens.append(gather[k + 2]["token"])

        def pass_on(y, k=k):
            if k + 1 == len(groups):
                return None
            gather[k + 1] = _gather_pass_on(gather[k + 1], name=f"gather_g{k + 1}_pass", after=y)
            return gather[k + 1]["token"]

        phase = _mix_fwd if kind == "mix" else _ff_fwd
        cur, cur_m, saved[g] = phase(i, cur, cur_m, layers[i], hook=pass_on, after=tokens)
        if k + 1 < len(groups):
            install(groups[k + 1], _gather_finish(gather[k + 1], name=f"gather_g{k + 1}_finish", after=cur))

    loss_local, dy = _loss_and_grad(cur, target[0])
    loss = lax.psum(loss_local, MESH_AXES)

    reduced = {n: [None] * PARAMS[n][0][0] for n in BIG}
    small_grads = {n: [None] * PARAMS[n][0][0] for n in SMALL}
    reduce = {}

    def to_chips(p, after):
        st = reduce[p]
        parts, got = _swap_finish(st["sibling"], name=f"reduce_p{p}_sibling_finish", after=after)
        sums = [_chip_partials(a, b, name=f"reduce_p{p}_chipsum{t}") for t, (a, b) in enumerate(zip(parts, got))]
        st["own"] = [own for own, _ in sums]
        others = [o for _, o in sums]
        st["chips"] = _swap_start(others, [lax.empty(o.shape, o.dtype) for o in others], _to_chip, 3,
                                  name=f"reduce_p{p}_chips_start")
        return st["chips"]["token"]

    def arrived(p, after):
        st = reduce[p]
        _, got = _swap_finish(st["chips"], name=f"reduce_p{p}_chips_finish", after=after)
        for (n, j), own, arr in zip(st["names"], st["own"], got):
            reduced[n][j] = (own, arr)

    terms = [(1.0, dy)]
    back = list(reversed(groups))
    for p, g in enumerate(back):
        kind, i = g

        def after_first_kernel(dz, p=p):
            return to_chips(p - 1, dz) if p >= 1 else None

        phase = _mix_bwd if kind == "mix" else _ff_bwd
        tokens = [reduce[p - 1]["sibling"]["token"]] if p >= 1 else []
        terms, grads = phase(i, terms, saved[g], layers[i], hook=after_first_kernel, after=tokens)
        for n in SMALL:
            if n in grads:
                small_grads[n][i if n.startswith("ln_") else i // N_MIXERS] = grads[n]
        if p >= 2:
            arrived(p - 2, terms[-1][1])
        names = group_names(g)
        parts = [_as_parts(n, grads[n]) for n, _ in names]
        reduce[p] = {"names": names, "sibling": _swap_start(
            parts, [lax.empty((4, *a.shape[1:]), a.dtype) for a in parts], _to_sibling, 4,
            name=f"reduce_p{p}_sibling_start")}
    grad_x = _sum_terms(terms, name="grad_x")
    last = len(back) - 1
    arrived(last - 1, grad_x)
    tail = to_chips(last, grad_x)
    arrived(last, tail)

    small_mine = _pack_rows([jnp.stack(small_grads[n]).reshape(PARAMS[n][0]) for n in SMALL], SMALL_WIDTH, SUBLANES, F32)
    small_sum = _sum_blocks(None, _all_gather([small_mine], name="gather_small_grads")[0], name="sum_small_grads")

    grad, delta, new_m, new_v = {}, {}, {}, {}
    for n, g in zip(SMALL, _unpack_rows(small_sum, [PARAMS[n][0] for n in SMALL], SMALL_WIDTH, SUBLANES)):
        ax = PARAMS[n][1]
        grad[n] = g if ax is None else lax.dynamic_index_in_dim(_split_shards(g, ax), me, axis=0, keepdims=False)
        delta[n], new_m[n], new_v[n] = _adamw(weights[n], grad[n], m_state[n], v_state[n], name=f"adamw_{n}")
    for n in BIG:
        shape = weights[n].shape
        view = lambda a: a.reshape(shape[0], shape[1], -1)
        outs = _adamw_reduced(view(weights[n]), view(m_state[n]), view(v_state[n]), reduced[n], name=f"adamw_{n}")
        grad[n], delta[n], new_m[n], new_v[n] = [o.reshape(shape) for o in outs]
    return (loss, grad_x[None], *[grad[n] for n in NAMES], *[delta[n] for n in NAMES],
            *[new_m[n] for n in NAMES], *[new_v[n] for n in NAMES])


def kernel(x, sc_w_in, sc_conv_w, sc_w_out, mla_w_dq, mla_g_q, mla_w_uq, mla_w_dkv, mla_g_kv, mla_w_uk, mla_w_uv, mla_w_o, cf_w_pw1, cf_b_pw1, cf_dw_w, cf_dw_b, cf_norm_g, cf_norm_b, cf_w_pw2, cf_b_pw2, ff_w1, ff_w2, ln_mix_g, ln_mix_b, ln_ff_g, ln_ff_b, loss_target, m_sc_w_in, m_sc_conv_w, m_sc_w_out, m_mla_w_dq, m_mla_g_q, m_mla_w_uq, m_mla_w_dkv, m_mla_g_kv, m_mla_w_uk, m_mla_w_uv, m_mla_w_o, m_cf_w_pw1, m_cf_b_pw1, m_cf_dw_w, m_cf_dw_b, m_cf_norm_g, m_cf_norm_b, m_cf_w_pw2, m_cf_b_pw2, m_ff_w1, m_ff_w2, m_ln_mix_g, m_ln_mix_b, m_ln_ff_g, m_ln_ff_b, v_sc_w_in, v_sc_conv_w, v_sc_w_out, v_mla_w_dq, v_mla_g_q, v_mla_w_uq, v_mla_w_dkv, v_mla_g_kv, v_mla_w_uk, v_mla_w_uv, v_mla_w_o, v_cf_w_pw1, v_cf_b_pw1, v_cf_dw_w, v_cf_dw_b, v_cf_norm_g, v_cf_norm_b, v_cf_w_pw2, v_cf_b_pw2, v_ff_w1, v_ff_w2, v_ln_mix_g, v_ln_mix_b, v_ln_ff_g, v_ln_ff_b):
    w_list = (sc_w_in, sc_conv_w, sc_w_out, mla_w_dq, mla_g_q, mla_w_uq, mla_w_dkv, mla_g_kv, mla_w_uk, mla_w_uv, mla_w_o,
              cf_w_pw1, cf_b_pw1, cf_dw_w, cf_dw_b, cf_norm_g, cf_norm_b, cf_w_pw2, cf_b_pw2, ff_w1, ff_w2, ln_mix_g,
              ln_mix_b, ln_ff_g, ln_ff_b)
    m_list = (m_sc_w_in, m_sc_conv_w, m_sc_w_out, m_mla_w_dq, m_mla_g_q, m_mla_w_uq, m_mla_w_dkv, m_mla_g_kv, m_mla_w_uk,
              m_mla_w_uv, m_mla_w_o, m_cf_w_pw1, m_cf_b_pw1, m_cf_dw_w, m_cf_dw_b, m_cf_norm_g, m_cf_norm_b, m_cf_w_pw2,
              m_cf_b_pw2, m_ff_w1, m_ff_w2, m_ln_mix_g, m_ln_mix_b, m_ln_ff_g, m_ln_ff_b)
    v_list = (v_sc_w_in, v_sc_conv_w, v_sc_w_out, v_mla_w_dq, v_mla_g_q, v_mla_w_uq, v_mla_w_dkv, v_mla_g_kv, v_mla_w_uk,
              v_mla_w_uv, v_mla_w_o, v_cf_w_pw1, v_cf_b_pw1, v_cf_dw_w, v_cf_dw_b, v_cf_norm_g, v_cf_norm_b, v_cf_w_pw2,
              v_cf_b_pw2, v_ff_w1, v_ff_w2, v_ln_mix_g, v_ln_mix_b, v_ln_ff_g, v_ln_ff_b)
    return _train_step(x, loss_target, dict(zip(NAMES, w_list)), dict(zip(NAMES, m_list)), dict(zip(NAMES, v_list)))
```

```python
import functools

import jax
import jax.numpy as jnp
from jax import lax
from jax.experimental import pallas as pl
from jax.experimental.pallas import tpu as pltpu

F32 = jnp.float32
MXU_DTYPE = jnp.bfloat16

N_DEV = 8
D_MODEL = 1024
DEPTH = 4
N_MIXERS = 3
CHUNK = 64
ALPHA = (2.0 * DEPTH) ** 0.25
LN_EPS = 1e-5
RMS_EPS = 1e-6
MLA_HEADS = 8
QK_NOPE = 128
QK_ROPE = 64
V_HEAD = 128
Q_LORA = 384
KV_LORA = 256
ROPE_THETA = 10000.0
ADAM_LR = 0.001
ADAM_B1 = 0.9
ADAM_B2 = 0.999
ADAM_EPS = 1e-08
ADAM_WD = 0.01
ADAM_STEP = 10

LANES = 128
SUBLANES = 8
VMEM_LIMIT = 56 * 1024 * 1024
MM_VMEM_BUDGET = 40 * 1024 * 1024
CONV_PAD = 32
MASK_VALUE = -1e30

MESH_AXES = ("x", "y", "c")


def _params(semantics):
    return pltpu.CompilerParams(dimension_semantics=semantics, vmem_limit_bytes=VMEM_LIMIT)


def _tile(n, cap):
    if n <= cap:
        return n
    t = cap - cap % LANES
    while t >= LANES:
        if n % t == 0:
            return t
        t -= LANES
    raise ValueError(f"no tile for {n} under {cap}")


def _row_tile(rows, cap):
    if rows <= cap:
        return rows
    t = cap - cap % SUBLANES
    while t >= SUBLANES:
        if rows % t == 0:
            return t
        t -= SUBLANES
    raise ValueError(f"no row tile for {rows} under {cap}")


def _mm(a, b, *, name, ta=False, tb=False, b_dev=False, out_dev=False, out_dtypes=(F32,), epilogue=None,
        tile_extras=(), col_extras=(), after=(), tm_cap=1024, tn_cap=1024, tk_cap=4096):
    m, k = (a.shape[1], a.shape[0]) if ta else a.shape
    if b_dev:
        nd, b_rows, ns = b.shape
        n, kb = (b_rows, nd * ns) if tb else (nd * ns, b_rows)
    else:
        n, kb = (b.shape[0], b.shape[1]) if tb else (b.shape[1], b.shape[0])
    assert k == kb, (a.shape, b.shape, ta, tb)
    tk = ns if (b_dev and tb) else _tile(k, tk_cap)
    nk = k // tk
    fixed_tn = ns if (b_dev and not tb) else (n // N_DEV if out_dev else None)
    if out_dev:
        assert not tile_extras and not col_extras and n % N_DEV == 0
    out_bytes = sum(jnp.dtype(dt).itemsize for dt in out_dtypes) + sum(e.dtype.itemsize for e in tile_extras)
    tm, tn = None, None
    for cand_m, cand_n in ((tm_cap, tn_cap), (tm_cap, tn_cap // 2), (tm_cap // 2, tn_cap), (tm_cap // 2, tn_cap // 2),
                           (tm_cap // 4, tn_cap // 2), (tm_cap // 4, tn_cap // 4)):
        tm, tn = _tile(m, cand_m), (fixed_tn if fixed_tn is not None else _tile(n, cand_n))
        need = 2 * (tm * tk * a.dtype.itemsize + tk * tn * b.dtype.itemsize) + 2 * tm * tn * out_bytes
        need += tm * tn * 4 * (2 if nk > 1 else 1)
        if need <= MM_VMEM_BUDGET:
            break
    n_te, n_ce, n_out = len(tile_extras), len(col_extras), len(out_dtypes)
    dims = (((0 if ta else 1,), (1 if tb else 0,)), ((), ()))

    def body(*refs):
        a_ref, b_ref = refs[0], refs[1]
        te_refs = refs[2:2 + n_te]
        ce_refs = refs[2 + n_te:2 + n_te + n_ce]
        first_out = 2 + n_te + n_ce + len(after)
        out_refs = refs[first_out:first_out + n_out]

        def finish(acc):
            if epilogue is None:
                outs = (acc,)
            else:
                outs = epilogue(acc, *[r[...] for r in te_refs], *[r[...] for r in ce_refs])
            for o_ref, o in zip(out_refs, outs):
                o_ref[...] = o.astype(o_ref.dtype)

        def product():
            return lax.dot_general(a_ref[...].astype(MXU_DTYPE), b_ref[...].astype(MXU_DTYPE), dims,
                                   preferred_element_type=F32)

        if nk == 1:
            finish(product())
        else:
            acc_ref = refs[-1]
            kk = pl.program_id(2)

            @pl.when(kk == 0)
            def _():
                acc_ref[...] = jnp.zeros_like(acc_ref)

            acc_ref[...] += product()

            @pl.when(kk == nk - 1)
            def _():
                finish(acc_ref[...])

    a_spec = pl.BlockSpec((tk, tm), lambda i, j, kk: (kk, i)) if ta else pl.BlockSpec((tm, tk), lambda i, j, kk: (i, kk))
    if b_dev and tb:
        b_spec = pl.BlockSpec((None, tn, tk), lambda i, j, kk: (kk, j, 0))
    elif b_dev:
        b_spec = pl.BlockSpec((None, tk, tn), lambda i, j, kk: (j, kk, 0))
    elif tb:
        b_spec = pl.BlockSpec((tn, tk), lambda i, j, kk: (j, kk))
    else:
        b_spec = pl.BlockSpec((tk, tn), lambda i, j, kk: (kk, j))
    tile_spec = pl.BlockSpec((tm, tn), lambda i, j, kk: (i, j))
    col_spec = pl.BlockSpec((1, tn), lambda i, j, kk: (0, j))
    if out_dev:
        out_spec = pl.BlockSpec((None, tm, tn), lambda i, j, kk: (j, i, 0))
        out_shape = [jax.ShapeDtypeStruct((N_DEV, m, tn), dt) for dt in out_dtypes]
    else:
        out_spec = tile_spec
        out_shape = [jax.ShapeDtypeStruct((m, n), dt) for dt in out_dtypes]
    outs = pl.pallas_call(
        body,
        name=name,
        grid=(m // tm, n // tn, nk),
        in_specs=[a_spec, b_spec] + [tile_spec] * n_te + [col_spec] * n_ce + [ANY] * len(after),
        out_specs=[out_spec] * n_out,
        out_shape=out_shape,
        scratch_shapes=[pltpu.VMEM((tm, tn), F32)] if nk > 1 else [],
        compiler_params=_params(("parallel", "parallel", "arbitrary")),
    )(a, b, *tile_extras, *col_extras, *after)
    return outs[0] if n_out == 1 else tuple(outs)


def _rowwise(fn, rows, bcast, out_rows, out_accs, *, name, tb_cap=256, after=()):
    t = rows[0].shape[0]
    tb = _row_tile(t, tb_cap)
    n_r, n_b, n_o, n_a = len(rows), len(bcast), len(out_rows), len(out_accs)

    def body(*refs):
        r_refs = refs[:n_r]
        b_refs = refs[n_r:n_r + n_b]
        first_out = n_r + n_b + len(after)
        o_refs = refs[first_out:first_out + n_o]
        a_refs = refs[first_out + n_o:]
        outs, accs = fn([r[...] for r in r_refs], [r[...] for r in b_refs])
        for o_ref, o in zip(o_refs, outs):
            o_ref[...] = o.astype(o_ref.dtype)
        if n_a:
            i = pl.program_id(0)

            @pl.when(i == 0)
            def _():
                for a_ref, acc in zip(a_refs, accs):
                    a_ref[...] = acc

            @pl.when(i > 0)
            def _():
                for a_ref, acc in zip(a_refs, accs):
                    a_ref[...] += acc

    in_specs = [pl.BlockSpec((tb, r.shape[1]), lambda i: (i, 0)) for r in rows]
    in_specs += [pl.BlockSpec(b.shape, lambda i: (0, 0)) for b in bcast]
    in_specs += [ANY] * len(after)
    out_specs = [pl.BlockSpec((tb, c), lambda i: (i, 0)) for c, _ in out_rows]
    out_specs += [pl.BlockSpec(s, lambda i: (0, 0)) for s in out_accs]
    out_shape = [jax.ShapeDtypeStruct((t, c), dt) for c, dt in out_rows]
    out_shape += [jax.ShapeDtypeStruct(s, F32) for s in out_accs]
    outs = pl.pallas_call(
        body,
        name=name,
        grid=(t // tb,),
        in_specs=in_specs,
        out_specs=out_specs,
        out_shape=out_shape,
        compiler_params=_params(("arbitrary",)),
    )(*rows, *bcast, *after)
    return tuple(outs)


def _colsum(v):
    return jnp.sum(v, axis=0, keepdims=True)


def _sigmoid(v):
    return 1.0 / (1.0 + jnp.exp(-v))


def _ln_fwd(x_res, y_sub, g, b, *, name, after=()):
    d = x_res.shape[1]

    def fn(rows, bc):
        z = ALPHA * rows[0] + rows[1]
        mu = jnp.mean(z, axis=-1, keepdims=True)
        zc = z - mu
        var = jnp.mean(zc * zc, axis=-1, keepdims=True)
        rstd = lax.rsqrt(var + LN_EPS)
        xhat = zc * rstd
        out = xhat * bc[0] + bc[1]
        return (out, out, xhat, rstd), ()

    return _rowwise(fn, [x_res, y_sub], [g, b], [(d, F32), (d, MXU_DTYPE), (d, F32), (1, F32)], [], name=name,
                    after=after)


def _ln_bwd(terms, xhat, rstd, g, *, name, after=()):
    d = xhat.shape[1]
    coefs = [c for c, _ in terms]

    def fn(rows, bc):
        xh, rs = rows[0], rows[1]
        dout = None
        for c, v in zip(coefs, rows[2:]):
            tv = v if c == 1.0 else c * v
            dout = tv if dout is None else dout + tv
        dxh = dout * bc[0]
        m1 = jnp.mean(dxh, axis=-1, keepdims=True)
        m2 = jnp.mean(dxh * xh, axis=-1, keepdims=True)
        dz = rs * (dxh - m1 - xh * m2)
        return (dz, dz), (_colsum(dout * xh), _colsum(dout), _colsum(dz))

    return _rowwise(fn, [xhat, rstd] + [v for _, v in terms], [g], [(d, F32), (d, MXU_DTYPE)],
                    [(1, d), (1, d), (1, d)], name=name, after=after)


def _rms_fwd(x, g, *, name):
    c = x.shape[1]

    def fn(rows, bc):
        v = rows[0]
        r = lax.rsqrt(jnp.mean(v * v, axis=-1, keepdims=True) + RMS_EPS)
        return (v * r * bc[0],), ()

    return _rowwise(fn, [x], [g], [(c, MXU_DTYPE)], [], name=name)[0]


def _rms_bwd(x, g, dy, *, name):
    c = x.shape[1]

    def fn(rows, bc):
        v, dyv = rows
        r = lax.rsqrt(jnp.mean(v * v, axis=-1, keepdims=True) + RMS_EPS)
        xh = v * r
        dxh = dyv * bc[0]
        dx = r * (dxh - xh * jnp.mean(dxh * xh, axis=-1, keepdims=True))
        return (dx,), (_colsum(dyv * xh),)

    return _rowwise(fn, [x, dy], [g], [(c, F32)], [(1, c)], name=name)


def _conv_chunk(t):
    return _row_tile(t, 256)


def _conv_fwd(x, w, bias, *, name):
    t, c = x.shape
    kw = w.shape[0]
    ch = _conv_chunk(t)

    def body(x_ref, w_ref, b_ref, o_ref, pad_ref):
        pad_ref[0:CONV_PAD, :] = jnp.zeros((CONV_PAD, LANES), F32)
        pad_ref[CONV_PAD:CONV_PAD + t, :] = x_ref[...]
        for t0 in range(0, t, ch):
            acc = jnp.broadcast_to(b_ref[...], (ch, LANES))
            for k in range(kw):
                s = kw - 1 - k
                acc = acc + w_ref[k:k + 1, :] * pad_ref[CONV_PAD - s + t0:CONV_PAD - s + t0 + ch, :]
            o_ref[t0:t0 + ch, :] = acc

    return pl.pallas_call(
        body,
        name=name,
        grid=(c // LANES,),
        in_specs=[pl.BlockSpec((t, LANES), lambda j: (0, j)), pl.BlockSpec((kw, LANES), lambda j: (0, j)),
                  pl.BlockSpec((1, LANES), lambda j: (0, j))],
        out_specs=pl.BlockSpec((t, LANES), lambda j: (0, j)),
        out_shape=jax.ShapeDtypeStruct((t, c), F32),
        scratch_shapes=[pltpu.VMEM((CONV_PAD + t, LANES), F32)],
        compiler_params=_params(("parallel",)),
    )(x, w, bias)


def _conv_bwd(dy, x, w, *, name):
    t, c = x.shape
    kw = w.shape[0]
    ch = _conv_chunk(t)

    def body(dy_ref, x_ref, w_ref, dx_ref, dw_ref, xpad_ref, dpad_ref):
        xpad_ref[0:CONV_PAD, :] = jnp.zeros((CONV_PAD, LANES), F32)
        xpad_ref[CONV_PAD:CONV_PAD + t, :] = x_ref[...]
        dpad_ref[0:t, :] = dy_ref[...]
        dpad_ref[t:t + CONV_PAD, :] = jnp.zeros((CONV_PAD, LANES), F32)
        for t0 in range(0, t, ch):
            acc = jnp.zeros((ch, LANES), F32)
            for k in range(kw):
                s = kw - 1 - k
                acc = acc + w_ref[k:k + 1, :] * dpad_ref[t0 + s:t0 + s + ch, :]
            dx_ref[t0:t0 + ch, :] = acc
        for k in range(kw):
            s = kw - 1 - k
            acc = jnp.zeros((ch, LANES), F32)
            for t0 in range(0, t, ch):
                acc = acc + dy_ref[t0:t0 + ch, :] * xpad_ref[CONV_PAD - s + t0:CONV_PAD - s + t0 + ch, :]
            dw_ref[k:k + 1, :] = _colsum(acc)

    blk = pl.BlockSpec((t, LANES), lambda j: (0, j))
    wblk = pl.BlockSpec((kw, LANES), lambda j: (0, j))
    return pl.pallas_call(
        body,
        name=name,
        grid=(c // LANES,),
        in_specs=[blk, blk, wblk],
        out_specs=[blk, wblk],
        out_shape=[jax.ShapeDtypeStruct((t, c), F32), jax.ShapeDtypeStruct((kw, c), F32)],
        scratch_shapes=[pltpu.VMEM((CONV_PAD + t, LANES), F32), pltpu.VMEM((CONV_PAD + t, LANES), F32)],
        compiler_params=_params(("parallel",)),
    )(dy, x, w)


ATTN_SCALE = (QK_NOPE + QK_ROPE) ** -0.5
_NT = (((1,), (1,)), ((), ()))
_TN = (((0,), (0,)), ((), ()))
_NN = (((1,), (0,)), ((), ()))


def _dot(a, b, dims):
    return lax.dot_general(a, b, dims, preferred_element_type=F32)


def _chunk_scores(qn, qp, kn_ref, kp_ref, off, tq, diagonal):
    s = _dot(qn, kn_ref[pl.ds(off, tq), :], _NT) + _dot(qp, kp_ref[pl.ds(off, tq), :], _NT)
    s = s * ATTN_SCALE
    if diagonal:
        shift = CHUNK.bit_length() - 1
        rows = lax.broadcasted_iota(jnp.int32, (tq, tq), 0)
        cols = lax.broadcasted_iota(jnp.int32, (tq, tq), 1)
        s = jnp.where(jnp.right_shift(cols, shift) <= jnp.right_shift(rows, shift), s, MASK_VALUE)
    return s


def _attn_fwd(qn, qp, kn, kp, v, *, name):
    h, t, _ = qn.shape
    tq = _row_tile(t, 256)
    assert tq % CHUNK == 0

    def body(qn_ref, qp_ref, kn_ref, kp_ref, v_ref, o_ref, of_ref, lse_ref):
        qi = pl.program_id(1)
        qn, qp = qn_ref[...], qp_ref[...]

        def step(off, carry, diagonal):
            m, l, acc = carry
            s = _chunk_scores(qn, qp, kn_ref, kp_ref, off, tq, diagonal)
            m_new = jnp.maximum(m, jnp.max(s, axis=-1, keepdims=True))
            alpha = jnp.exp(m - m_new)
            p = jnp.exp(s - m_new)
            l = alpha * l + jnp.sum(p, axis=-1, keepdims=True)
            acc = alpha * acc + _dot(p.astype(MXU_DTYPE), v_ref[pl.ds(off, tq), :], _NN)
            return m_new, l, acc

        init = (jnp.full((tq, 1), MASK_VALUE, F32), jnp.zeros((tq, 1), F32), jnp.zeros((tq, V_HEAD), F32))
        carry = lax.fori_loop(0, qi, lambda j, c: step(pl.multiple_of(j * tq, tq), c, False), init)
        m, l, acc = step(pl.multiple_of(qi * tq, tq), carry, True)
        o = acc / l
        o_ref[...] = o.astype(o_ref.dtype)
        of_ref[...] = o
        lse_ref[...] = m + jnp.log(l)

    qblk = lambda w: pl.BlockSpec((None, tq, w), lambda hh, qi: (hh, qi, 0))
    kblk = lambda w: pl.BlockSpec((None, t, w), lambda hh, qi: (hh, 0, 0))
    return pl.pallas_call(
        body,
        name=name,
        grid=(h, t // tq),
        in_specs=[qblk(QK_NOPE), qblk(QK_ROPE), kblk(QK_NOPE), pl.BlockSpec((t, QK_ROPE), lambda hh, qi: (0, 0)),
                  kblk(V_HEAD)],
        out_specs=[qblk(V_HEAD), qblk(V_HEAD), qblk(1)],
        out_shape=[jax.ShapeDtypeStruct((h, t, V_HEAD), MXU_DTYPE), jax.ShapeDtypeStruct((h, t, V_HEAD), F32),
                   jax.ShapeDtypeStruct((h, t, 1), F32)],
        compiler_params=_params(("parallel", "parallel")),
    )(qn, qp, kn, kp, v)


def _attn_bwd(qn, qp, kn, kp, v, do, o, lse, *, name):
    h, t, _ = qn.shape
    tq = _row_tile(t, 256)

    def body(qn_ref, qp_ref, kn_ref, kp_ref, v_ref, do_ref, o_ref, lse_ref, dqn_ref, dqp_ref, dkn_ref, dkp_ref,
             dv_ref):
        hh, qi = pl.program_id(0), pl.program_id(1)

        @pl.when(qi == 0)
        def _():
            dkn_ref[...] = jnp.zeros_like(dkn_ref)
            dv_ref[...] = jnp.zeros_like(dv_ref)

        @pl.when(jnp.logical_and(hh == 0, qi == 0))
        def _():
            dkp_ref[...] = jnp.zeros_like(dkp_ref)

        qn, qp, dov, lse = qn_ref[...], qp_ref[...], do_ref[...], lse_ref[...]
        delta = jnp.sum(dov.astype(F32) * o_ref[...], axis=-1, keepdims=True)

        def step(off, carry, diagonal):
            dqn, dqp = carry
            s = _chunk_scores(qn, qp, kn_ref, kp_ref, off, tq, diagonal)
            p = jnp.exp(s - lse)
            dp = _dot(dov, v_ref[pl.ds(off, tq), :], _NT)
            ds_m = (p * (dp - delta) * ATTN_SCALE).astype(MXU_DTYPE)
            dqn = dqn + _dot(ds_m, kn_ref[pl.ds(off, tq), :], _NN)
            dqp = dqp + _dot(ds_m, kp_ref[pl.ds(off, tq), :], _NN)
            dkn_ref[pl.ds(off, tq), :] += _dot(ds_m, qn, _TN)
            dkp_ref[pl.ds(off, tq), :] += _dot(ds_m, qp, _TN)
            dv_ref[pl.ds(off, tq), :] += _dot(p.astype(MXU_DTYPE), dov, _TN)
            return dqn, dqp

        init = (jnp.zeros((tq, QK_NOPE), F32), jnp.zeros((tq, QK_ROPE), F32))
        carry = lax.fori_loop(0, qi, lambda j, c: step(pl.multiple_of(j * tq, tq), c, False), init)
        dqn, dqp = step(pl.multiple_of(qi * tq, tq), carry, True)
        dqn_ref[...] = dqn
        dqp_ref[...] = dqp

    qblk = lambda w: pl.BlockSpec((None, tq, w), lambda hh, qi: (hh, qi, 0))
    kblk = lambda w: pl.BlockSpec((None, t, w), lambda hh, qi: (hh, 0, 0))
    kpblk = pl.BlockSpec((t, QK_ROPE), lambda hh, qi: (0, 0))
    return pl.pallas_call(
        body,
        name=name,
        grid=(h, t // tq),
        in_specs=[qblk(QK_NOPE), qblk(QK_ROPE), kblk(QK_NOPE), kpblk, kblk(V_HEAD), qblk(V_HEAD), qblk(V_HEAD),
                  qblk(1)],
        out_specs=[qblk(QK_NOPE), qblk(QK_ROPE), kblk(QK_NOPE), kpblk, kblk(V_HEAD)],
        out_shape=[jax.ShapeDtypeStruct((h, t, QK_NOPE), F32), jax.ShapeDtypeStruct((h, t, QK_ROPE), F32),
                   jax.ShapeDtypeStruct((h, t, QK_NOPE), F32), jax.ShapeDtypeStruct((t, QK_ROPE), F32),
                   jax.ShapeDtypeStruct((h, t, V_HEAD), F32)],
        compiler_params=_params(("arbitrary", "arbitrary")),
    )(qn, qp, kn, kp, v, do, o, lse)


def _rope(x1, x2, cos, sin, *, name, inverse=False):
    w = x1.shape[1]
    sign = -1.0 if inverse else 1.0

    def fn(rows, bc):
        a, b, c, s = rows
        s = sign * s
        return (a * c - b * s, a * s + b * c), ()

    return _rowwise(fn, [x1, x2, cos, sin], [], [(w, F32), (w, F32)], [], name=name)


def _mlp_fwd(x_m, w1, w2, tag, after=()):
    def epi(acc):
        r = jnp.maximum(acc, 0.0)
        return (r * r,)

    act = _mm(x_m, w1, b_dev=True, name=f"{tag}_up", out_dtypes=(MXU_DTYPE,), epilogue=epi, after=after)
    y = _mm(act, w2, name=f"{tag}_down")
    return y, (x_m, act)


def _mlp_bwd(dy_m, saved, w1, w2, tag, after=()):
    x_m, act = saved
    dw2 = _mm(act, dy_m, ta=True, name=f"{tag}_dw2", out_dtypes=(MXU_DTYPE,))

    def epi(acc, av):
        return (acc * (2.0 * jnp.sqrt(av.astype(F32))),)

    dh = _mm(dy_m, w2, tb=True, name=f"{tag}_dact", out_dtypes=(MXU_DTYPE,), epilogue=epi, tile_extras=(act,),
             after=after)
    dw1 = _mm(x_m, dh, ta=True, out_dev=True, name=f"{tag}_dw1", out_dtypes=(MXU_DTYPE,))
    dx = _mm(dh, w1, tb=True, b_dev=True, name=f"{tag}_dx")
    return [dx], {"ff_w1": dw1, "ff_w2": dw2}


def _sc_fwd(x_m, w_in, conv_w, w_out, tag, after=()):
    d = D_MODEL
    u = _mm(x_m, w_in, b_dev=True, name=f"{tag}_in", after=after)

    def gate_fn(rows, bc):
        uv = rows[0]
        return (uv[:, d:2 * d] * uv[:, 2 * d:3 * d],), ()

    p = _rowwise(gate_fn, [u], [], [(d, F32)], [], name=f"{tag}_ch")[0]
    q = _conv_fwd(p, conv_w, jnp.zeros((1, d), F32), name=f"{tag}_conv")

    def out_fn(rows, bc):
        return (rows[0][:, 0:d] * rows[1],), ()

    r = _rowwise(out_fn, [u, q], [], [(d, MXU_DTYPE)], [], name=f"{tag}_bq")[0]
    y = _mm(r, w_out, name=f"{tag}_out")
    return y, (x_m, u, p, q, r)


def _sc_bwd(dy_m, saved, w_in, conv_w, w_out, tag, after=()):
    d = D_MODEL
    x_m, u, p, q, r = saved
    dw_out = _mm(r, dy_m, ta=True, name=f"{tag}_dwout", out_dtypes=(MXU_DTYPE,))
    dr = _mm(dy_m, w_out, tb=True, name=f"{tag}_dr", after=after)

    def dq_fn(rows, bc):
        return (rows[0] * rows[1][:, 0:d],), ()

    dq = _rowwise(dq_fn, [dr, u], [], [(d, F32)], [], name=f"{tag}_dq")[0]
    dp, dconv = _conv_bwd(dq, p, conv_w, name=f"{tag}_dconv")

    def du_fn(rows, bc):
        drv, qv, dpv, uv = rows
        return (jnp.concatenate([drv * qv, dpv * uv[:, 2 * d:3 * d], dpv * uv[:, d:2 * d]], axis=1),), ()

    du = _rowwise(du_fn, [dr, q, dp, u], [], [(3 * d, MXU_DTYPE)], [], name=f"{tag}_du")[0]
    dw_in = _mm(x_m, du, ta=True, out_dev=True, name=f"{tag}_dwin", out_dtypes=(MXU_DTYPE,))
    dx = _mm(du, w_in, tb=True, b_dev=True, name=f"{tag}_dx")
    return [dx], {"sc_w_in": dw_in, "sc_conv_w": dconv, "sc_w_out": dw_out}


def _cf_fwd(x_m, w, tag, after=()):
    d = D_MODEL

    def bias_epi(acc, bias):
        return (acc + bias,)

    u = _mm(x_m, w["cf_w_pw1"], b_dev=True, name=f"{tag}_pw1", epilogue=bias_epi, col_extras=(w["cf_b_pw1"],),
            after=after)

    def glu_fn(rows, bc):
        uv = rows[0]
        return (uv[:, 0:d] * _sigmoid(uv[:, d:2 * d]),), ()

    h1 = _rowwise(glu_fn, [u], [], [(d, F32)], [], name=f"{tag}_glu")[0]
    h2 = _conv_fwd(h1, w["cf_dw_w"], w["cf_dw_b"], name=f"{tag}_conv")

    def norm_fn(rows, bc):
        hv = rows[0]
        mu = jnp.mean(hv, axis=-1, keepdims=True)
        hc = hv - mu
        rstd = lax.rsqrt(jnp.mean(hc * hc, axis=-1, keepdims=True) + LN_EPS)
        h3 = hc * rstd * bc[0] + bc[1]
        return (h3 * _sigmoid(h3),), ()

    h4 = _rowwise(norm_fn, [h2], [w["cf_norm_g"], w["cf_norm_b"]], [(d, MXU_DTYPE)], [], name=f"{tag}_norm")[0]
    y = _mm(h4, w["cf_w_pw2"], name=f"{tag}_pw2", epilogue=bias_epi, col_extras=(w["cf_b_pw2"],))
    return y, (x_m, u, h1, h2, h4)


def _cf_bwd(dy_m, dy_colsum, saved, w, tag, after=()):
    d = D_MODEL
    x_m, u, h1, h2, h4 = saved
    dw_pw2 = _mm(h4, dy_m, ta=True, name=f"{tag}_dwpw2", out_dtypes=(MXU_DTYPE,))
    dh4 = _mm(dy_m, w["cf_w_pw2"], tb=True, name=f"{tag}_dh4", after=after)

    def dnorm_fn(rows, bc):
        dh4v, hv = rows
        g, b = bc
        mu = jnp.mean(hv, axis=-1, keepdims=True)
        hc = hv - mu
        rstd = lax.rsqrt(jnp.mean(hc * hc, axis=-1, keepdims=True) + LN_EPS)
        xh = hc * rstd
        h3 = xh * g + b
        sg = _sigmoid(h3)
        dh3 = dh4v * (sg * (1.0 + h3 * (1.0 - sg)))
        dxh = dh3 * g
        dh2 = rstd * (dxh - jnp.mean(dxh, axis=-1, keepdims=True) - xh * jnp.mean(dxh * xh, axis=-1, keepdims=True))
        return (dh2,), (_colsum(dh3 * xh), _colsum(dh3), _colsum(dh2))

    dh2, dnorm_g, dnorm_b, ddw_b = _rowwise(dnorm_fn, [dh4, h2], [w["cf_norm_g"], w["cf_norm_b"]], [(d, F32)],
                                            [(1, d), (1, d), (1, d)], name=f"{tag}_dnorm")
    dh1, ddw_w = _conv_bwd(dh2, h1, w["cf_dw_w"], name=f"{tag}_dconv")

    def dglu_fn(rows, bc):
        dh1v, uv = rows
        a, gate = uv[:, 0:d], uv[:, d:2 * d]
        sg = _sigmoid(gate)
        du = jnp.concatenate([dh1v * sg, dh1v * a * sg * (1.0 - sg)], axis=1)
        return (du,), (_colsum(du),)

    du, db_pw1 = _rowwise(dglu_fn, [dh1, u], [], [(2 * d, MXU_DTYPE)], [(1, 2 * d)], name=f"{tag}_dglu")
    dw_pw1 = _mm(x_m, du, ta=True, out_dev=True, name=f"{tag}_dwpw1", out_dtypes=(MXU_DTYPE,))
    dx = _mm(du, w["cf_w_pw1"], tb=True, b_dev=True, name=f"{tag}_dx")
    grads = {"cf_w_pw1": dw_pw1, "cf_b_pw1": db_pw1, "cf_dw_w": ddw_w, "cf_dw_b": ddw_b, "cf_norm_g": dnorm_g,
             "cf_norm_b": dnorm_b, "cf_w_pw2": dw_pw2, "cf_b_pw2": dy_colsum}
    return [dx], grads


def _rope_tables(t):
    pos = jnp.arange(t, dtype=F32)
    inv_freq = ROPE_THETA ** (-jnp.arange(0, QK_ROPE, 2, dtype=F32) / QK_ROPE)
    ang = pos[:, None] * inv_freq[None, :]
    return jnp.cos(ang), jnp.sin(ang)


def _heads(a, width):
    return a.reshape(a.shape[0], MLA_HEADS, width).transpose(1, 0, 2)


def _unheads(a):
    return a.transpose(1, 0, 2).reshape(a.shape[1], -1)


def _mla_fwd(x_m, w, tag, after=()):
    t = x_m.shape[0]
    half = QK_ROPE // 2
    cos, sin = _rope_tables(t)
    cos_h, sin_h = jnp.tile(cos, (1, MLA_HEADS)), jnp.tile(sin, (1, MLA_HEADS))
    cq_pre = _mm(x_m, w["mla_w_dq"], name=f"{tag}_dq", after=after)
    cq = _rms_fwd(cq_pre, w["mla_g_q"], name=f"{tag}_qnorm")
    q = _mm(cq, w["mla_w_uq"], name=f"{tag}_uq").reshape(t, MLA_HEADS, QK_NOPE + QK_ROPE)
    q1 = q[:, :, QK_NOPE:QK_NOPE + half].reshape(t, MLA_HEADS * half)
    q2 = q[:, :, QK_NOPE + half:].reshape(t, MLA_HEADS * half)
    r1, r2 = _rope(q1, q2, cos_h, sin_h, name=f"{tag}_qrope")
    qp = jnp.concatenate([r1.reshape(t, MLA_HEADS, half), r2.reshape(t, MLA_HEADS, half)], axis=-1)
    qn_h = q[:, :, :QK_NOPE].transpose(1, 0, 2).astype(MXU_DTYPE)
    qp_h = qp.transpose(1, 0, 2).astype(MXU_DTYPE)
    ckv_full = _mm(x_m, w["mla_w_dkv"], name=f"{tag}_dkv")
    ckv_pre = ckv_full[:, :KV_LORA]
    ckv = _rms_fwd(ckv_pre, w["mla_g_kv"], name=f"{tag}_kvnorm")
    k1, k2 = _rope(ckv_full[:, KV_LORA:KV_LORA + half], ckv_full[:, KV_LORA + half:], cos, sin, name=f"{tag}_krope")
    kp = jnp.concatenate([k1, k2], axis=-1).astype(MXU_DTYPE)
    kn_h = _heads(_mm(ckv, w["mla_w_uk"], name=f"{tag}_uk", out_dtypes=(MXU_DTYPE,)), QK_NOPE)
    v_h = _heads(_mm(ckv, w["mla_w_uv"], name=f"{tag}_uv", out_dtypes=(MXU_DTYPE,)), V_HEAD)
    o_h, o_f, lse = _attn_fwd(qn_h, qp_h, kn_h, kp, v_h, name=f"{tag}_attn")
    o_flat = _unheads(o_h)
    y = _mm(o_flat, w["mla_w_o"], name=f"{tag}_o")
    return y, (x_m, cq_pre, cq, ckv_pre, ckv, qn_h, qp_h, kn_h, kp, v_h, lse, o_flat, o_f, (cos, sin, cos_h, sin_h))


def _mla_bwd(dy_m, saved, w, tag, after=()):
    x_m, cq_pre, cq, ckv_pre, ckv, qn_h, qp_h, kn_h, kp, v_h, lse, o_flat, o_f, (cos, sin, cos_h, sin_h) = saved
    t = x_m.shape[0]
    half = QK_ROPE // 2
    dw_o = _mm(o_flat, dy_m, ta=True, name=f"{tag}_dwo", out_dtypes=(MXU_DTYPE,))
    do_h = _heads(_mm(dy_m, w["mla_w_o"], tb=True, name=f"{tag}_do", out_dtypes=(MXU_DTYPE,), after=after), V_HEAD)
    dqn_h, dqp_h, dkn_h, dkp, dv_h = _attn_bwd(qn_h, qp_h, kn_h, kp, v_h, do_h, o_f, lse, name=f"{tag}_dattn")
    dqp = dqp_h.transpose(1, 0, 2)
    d1, d2 = _rope(dqp[:, :, :half].reshape(t, -1), dqp[:, :, half:].reshape(t, -1), cos_h, sin_h,
                   name=f"{tag}_dqrope", inverse=True)
    dq = jnp.concatenate([dqn_h.transpose(1, 0, 2), d1.reshape(t, MLA_HEADS, half), d2.reshape(t, MLA_HEADS, half)],
                         axis=-1).reshape(t, -1).astype(MXU_DTYPE)
    dw_uq = _mm(cq, dq, ta=True, name=f"{tag}_dwuq", out_dtypes=(MXU_DTYPE,))
    dcq = _mm(dq, w["mla_w_uq"], tb=True, name=f"{tag}_dcq")
    dcq_pre, dg_q = _rms_bwd(cq_pre, w["mla_g_q"], dcq, name=f"{tag}_dqnorm")
    dcq_pre_m = dcq_pre.astype(MXU_DTYPE)
    dw_dq = _mm(x_m, dcq_pre_m, ta=True, name=f"{tag}_dwdq", out_dtypes=(MXU_DTYPE,))
    dx_q = _mm(dcq_pre_m, w["mla_w_dq"], tb=True, name=f"{tag}_dxq")
    dkn = _unheads(dkn_h).astype(MXU_DTYPE)
    dv = _unheads(dv_h).astype(MXU_DTYPE)
    dw_uk = _mm(ckv, dkn, ta=True, name=f"{tag}_dwuk", out_dtypes=(MXU_DTYPE,))
    dw_uv = _mm(ckv, dv, ta=True, name=f"{tag}_dwuv", out_dtypes=(MXU_DTYPE,))
    dckv_k = _mm(dkn, w["mla_w_uk"], tb=True, name=f"{tag}_dckvk")
    dckv = _mm(dv, w["mla_w_uv"], tb=True, name=f"{tag}_dckv", epilogue=lambda acc, other: (acc + other,),
               tile_extras=(dckv_k,))
    dckv_pre, dg_kv = _rms_bwd(ckv_pre, w["mla_g_kv"], dckv, name=f"{tag}_dkvnorm")
    dk1, dk2 = _rope(dkp[:, :half], dkp[:, half:], cos, sin, name=f"{tag}_dkrope", inverse=True)
    dckv_full = jnp.concatenate([dckv_pre, dk1, dk2], axis=-1).astype(MXU_DTYPE)
    dw_dkv = _mm(x_m, dckv_full, ta=True, name=f"{tag}_dwdkv", out_dtypes=(MXU_DTYPE,))
    dx_kv = _mm(dckv_full, w["mla_w_dkv"], tb=True, name=f"{tag}_dxkv")
    grads = {"mla_w_dq": dw_dq, "mla_g_q": dg_q, "mla_w_uq": dw_uq, "mla_w_dkv": dw_dkv, "mla_g_kv": dg_kv,
             "mla_w_uk": dw_uk, "mla_w_uv": dw_uv, "mla_w_o": dw_o}
    return [dx_q, dx_kv], grads


def _cast_input(x):
    cast = lambda rows, bc: ((rows[0],), ())
    return _rowwise(cast, [x], [], [(D_MODEL, MXU_DTYPE)], [], name="x_cast")[0]


def _hooked(value, hook):
    token = None if hook is None else hook(value)
    return () if token is None else (token,)


def _mix_fwd(i, cur, cur_m, wl, hook=None, after=()):
    mixer = i % N_MIXERS
    if mixer == 0:
        y, s_mix = _sc_fwd(cur_m, wl["sc_w_in"], wl["sc_conv_w"], wl["sc_w_out"], f"l{i}_sc", after=after)
    elif mixer == 1:
        y, s_mix = _mla_fwd(cur_m, wl, f"l{i}_mla", after=after)
    else:
        y, s_mix = _cf_fwd(cur_m, wl, f"l{i}_cf", after=after)
    cur, cur_m, xh_mix, rs_mix = _ln_fwd(cur, y, wl["ln_mix_g"], wl["ln_mix_b"], name=f"l{i}_lnmix",
                                         after=_hooked(y, hook))
    return cur, cur_m, (s_mix, xh_mix, rs_mix)


def _ff_fwd(i, cur, cur_m, wl, hook=None, after=()):
    y, s_ff = _mlp_fwd(cur_m, wl["ff_w1"], wl["ff_w2"], f"l{i}_ff", after=after)
    cur, cur_m, xh_ff, rs_ff = _ln_fwd(cur, y, wl["ln_ff_g"], wl["ln_ff_b"], name=f"l{i}_lnff", after=_hooked(y, hook))
    return cur, cur_m, (s_ff, xh_ff, rs_ff)


def _layer_fwd(i, cur, cur_m, wl):
    cur, cur_m, s_mix = _mix_fwd(i, cur, cur_m, wl)
    cur, cur_m, s_ff = _ff_fwd(i, cur, cur_m, wl)
    return cur, cur_m, (*s_mix, *s_ff)


def _loss_and_grad(out, target):
    d = D_MODEL

    def loss_fn(rows, bc):
        err = rows[0] - rows[1]
        return (err * (1.0 / d),), (_colsum(err * err),)

    dy, sq = _rowwise(loss_fn, [out, target], [], [(d, F32)], [(1, d)], name="loss")
    return (0.5 / d) * jnp.sum(sq), dy


def _ff_bwd(i, terms, saved, wl, hook=None, after=()):
    s_ff, xh_ff, rs_ff = saved
    grads = {}
    dz, dz_m, grads["ln_ff_g"], grads["ln_ff_b"], _ = _ln_bwd(terms, xh_ff, rs_ff, wl["ln_ff_g"], name=f"l{i}_dlnff",
                                                              after=after)
    dxs, g_ff = _mlp_bwd(dz_m, s_ff, wl["ff_w1"], wl["ff_w2"], f"l{i}_ff", after=_hooked(dz, hook))
    grads.update(g_ff)
    return [(ALPHA, dz)] + [(1.0, v) for v in dxs], grads


def _mix_bwd(i, terms, saved, wl, hook=None, after=()):
    mixer = i % N_MIXERS
    s_mix, xh_mix, rs_mix = saved
    grads = {}
    dz, dz_m, grads["ln_mix_g"], grads["ln_mix_b"], dz_sum = _ln_bwd(terms, xh_mix, rs_mix, wl["ln_mix_g"],
                                                                     name=f"l{i}_dlnmix", after=after)
    behind = _hooked(dz, hook)
    if mixer == 0:
        dxs, g_mix = _sc_bwd(dz_m, s_mix, wl["sc_w_in"], wl["sc_conv_w"], wl["sc_w_out"], f"l{i}_sc", after=behind)
    elif mixer == 1:
        dxs, g_mix = _mla_bwd(dz_m, s_mix, wl, f"l{i}_mla", after=behind)
    else:
        dxs, g_mix = _cf_bwd(dz_m, dz_sum, s_mix, wl, f"l{i}_cf", after=behind)
    grads.update(g_mix)
    return [(ALPHA, dz)] + [(1.0, v) for v in dxs], grads


def _layer_bwd(i, terms, saved, wl):
    terms, g_ff = _ff_bwd(i, terms, saved[3:], wl)
    terms, g_mix = _mix_bwd(i, terms, saved[:3], wl)
    return terms, {**g_ff, **g_mix}


def _sum_terms(terms, *, name):
    coefs = [c for c, _ in terms]

    def sum_fn(rows, bc):
        acc = None
        for c, v in zip(coefs, rows):
            tv = v if c == 1.0 else c * v
            acc = tv if acc is None else acc + tv
        return (acc,), ()

    return _rowwise(sum_fn, [v for _, v in terms], [], [(D_MODEL, F32)], [], name=name)[0]


def _local_step(x, target, layers):
    cur, cur_m = x, _cast_input(x)
    saved = []
    for i in range(DEPTH):
        cur, cur_m, s = _layer_fwd(i, cur, cur_m, layers[i])
        saved.append(s)
    loss, dy = _loss_and_grad(cur, target)
    terms = [(1.0, dy)]
    grads = [None] * DEPTH
    for i in reversed(range(DEPTH)):
        terms, grads[i] = _layer_bwd(i, terms, saved[i], layers[i])
    return loss, _sum_terms(terms, name="grad_x"), grads


MESH_ID = pl.DeviceIdType.MESH
ANY = pl.BlockSpec(memory_space=pl.ANY)


def _block_id(px, py, pc):
    return 4 * px + 2 * py + pc


COPIES_PER_GATHER = 7


def _all_gather(shards, *, name):
    n = len(shards)

    def body(*refs):
        x_refs, out_refs = refs[:n], refs[n:2 * n]
        send_sems, recv_sems, local_sems = refs[2 * n:]
        x, y, core = lax.axis_index("x"), lax.axis_index("y"), lax.axis_index("c")
        me, sibling = (x, y, core), (x, y, 1 - core)
        chips = [(1 - x, y), (x, 1 - y), (1 - x, 1 - y)]

        def copy(t, k, block, to, from_input=False):
            dst = out_refs[t].at[_block_id(*block)]
            return pltpu.make_async_remote_copy(
                src_ref=x_refs[t] if from_input else dst, dst_ref=dst,
                send_sem=send_sems.at[t * COPIES_PER_GATHER + k], recv_sem=recv_sems.at[t * COPIES_PER_GATHER + k],
                device_id=to, device_id_type=MESH_ID)

        started = []
        mine = [pltpu.make_async_copy(x_refs[t], out_refs[t].at[_block_id(*me)], local_sems.at[t]) for t in range(n)]
        for t in range(n):
            mine[t].start()
            first = [copy(t, 0, me, sibling, from_input=True)]
            first += [copy(t, 1 + j, me, (*chip, core), from_input=True) for j, chip in enumerate(chips)]
            for cp in first:
                cp.start()
            started += first
        for t in range(n):
            for j, chip in enumerate(chips):
                copy(t, 1 + j, (*chip, core), me).wait_recv()
                passed = copy(t, 4 + j, (*chip, core), sibling)
                passed.start()
                started.append(passed)
        for t in range(n):
            copy(t, 0, sibling, me).wait_recv()
            for j, chip in enumerate(chips):
                copy(t, 4 + j, (*chip, 1 - core), me).wait_recv()
        for cp in started:
            cp.wait_send()
        for cp in mine:
            cp.wait()

    n_sems = n * COPIES_PER_GATHER
    outs = pl.pallas_call(
        body,
        name=name,
        out_shape=[jax.ShapeDtypeStruct((N_DEV, *s.shape), s.dtype) for s in shards],
        in_specs=[ANY] * n,
        out_specs=[ANY] * n,
        scratch_shapes=[pltpu.SemaphoreType.DMA((n_sems,)), pltpu.SemaphoreType.DMA((n_sems,)),
                        pltpu.SemaphoreType.DMA((n,))],
    )(*shards)
    return list(outs)


def _swap_with_sibling(parts, *, name):
    n = len(parts)

    def body(*refs):
        p_refs, got_refs = refs[:n], refs[n:2 * n]
        send_sems, recv_sems = refs[2 * n:]
        x, y, core = lax.axis_index("x"), lax.axis_index("y"), lax.axis_index("c")
        chips = [(x, y), (1 - x, y), (x, 1 - y), (1 - x, 1 - y)]
        copies = []
        for t in range(n):
            for k, (px, py) in enumerate(chips):
                copies.append(pltpu.make_async_remote_copy(
                    src_ref=p_refs[t].at[_block_id(px, py, 1 - core)], dst_ref=got_refs[t].at[k],
                    send_sem=send_sems.at[4 * t + k], recv_sem=recv_sems.at[4 * t + k], device_id=(x, y, 1 - core),
                    device_id_type=MESH_ID))
        for cp in copies:
            cp.start()
        for cp in copies:
            cp.wait()

    outs = pl.pallas_call(
        body,
        name=name,
        out_shape=[jax.ShapeDtypeStruct((4, *p.shape[1:]), p.dtype) for p in parts],
        in_specs=[ANY] * n,
        out_specs=[ANY] * n,
        scratch_shapes=[pltpu.SemaphoreType.DMA((4 * n,)), pltpu.SemaphoreType.DMA((4 * n,))],
    )(*parts)
    return list(outs)


def _swap_between_chips(parts, *, name):
    n = len(parts)

    def body(*refs):
        p_refs, got_refs = refs[:n], refs[n:2 * n]
        send_sems, recv_sems = refs[2 * n:]
        x, y, core = lax.axis_index("x"), lax.axis_index("y"), lax.axis_index("c")
        chips = [(1 - x, y), (x, 1 - y), (1 - x, 1 - y)]
        copies = []
        for t in range(n):
            for j, (px, py) in enumerate(chips):
                copies.append(pltpu.make_async_remote_copy(
                    src_ref=p_refs[t].at[j], dst_ref=got_refs[t].at[j], send_sem=send_sems.at[3 * t + j],
                    recv_sem=recv_sems.at[3 * t + j], device_id=(px, py, core), device_id_type=MESH_ID))
        for cp in copies:
            cp.start()
        for cp in copies:
            cp.wait()

    outs = pl.pallas_call(
        body,
        name=name,
        out_shape=[jax.ShapeDtypeStruct(p.shape, p.dtype) for p in parts],
        in_specs=[ANY] * n,
        out_specs=[ANY] * n,
        scratch_shapes=[pltpu.SemaphoreType.DMA((3 * n,)), pltpu.SemaphoreType.DMA((3 * n,))],
    )(*parts)
    return list(outs)


def _chip_partials(parts, got, *, name):
    _, r, c = parts.shape
    tb = _row_tile(r, 512)
    x, y, core = lax.axis_index("x"), lax.axis_index("y"), lax.axis_index("c")
    chips = [(x, y), (1 - x, y), (x, 1 - y), (1 - x, 1 - y)]
    ids = jnp.stack([_block_id(px, py, core) for px, py in chips]).astype(jnp.int32)

    def body(ids_ref, p0, p1, p2, p3, g_ref, own_ref, out_ref):
        own_ref[...] = p0[...].astype(F32) + g_ref[0].astype(F32)
        for j, p in enumerate((p1, p2, p3)):
            out_ref[j] = (p[...].astype(F32) + g_ref[j + 1].astype(F32)).astype(out_ref.dtype)

    def part_spec(k):
        return pl.BlockSpec((None, tb, c), lambda i, ids_ref: (ids_ref[k], i, 0))

    grid_spec = pltpu.PrefetchScalarGridSpec(
        num_scalar_prefetch=1,
        grid=(r // tb,),
        in_specs=[part_spec(k) for k in range(4)] + [pl.BlockSpec((4, tb, c), lambda i, ids_ref: (0, i, 0))],
        out_specs=[pl.BlockSpec((tb, c), lambda i, ids_ref: (i, 0)), pl.BlockSpec((3, tb, c), lambda i, ids_ref: (0, i, 0))],
    )
    return pl.pallas_call(
        body,
        name=name,
        grid_spec=grid_spec,
        out_shape=[jax.ShapeDtypeStruct((r, c), F32), jax.ShapeDtypeStruct((3, r, c), parts.dtype)],
        compiler_params=_params(("parallel",)),
    )(ids, parts, parts, parts, parts, got)


def _sum_blocks(first, blocks, *, name):
    n, r, c = blocks.shape
    tb = _row_tile(r, 512)

    def body(*refs):
        b_ref, o_ref = refs[-2], refs[-1]
        acc = refs[0][...] if first is not None else b_ref[0].astype(F32)
        for j in range(0 if first is not None else 1, n):
            acc = acc + b_ref[j].astype(F32)
        o_ref[...] = acc

    row_spec = pl.BlockSpec((tb, c), lambda i: (i, 0))
    return pl.pallas_call(
        body,
        name=name,
        grid=(r // tb,),
        in_specs=([row_spec] if first is not None else []) + [pl.BlockSpec((n, tb, c), lambda i: (0, i, 0))],
        out_specs=row_spec,
        out_shape=jax.ShapeDtypeStruct((r, c), F32),
        compiler_params=_params(("parallel",)),
    )(*([first] if first is not None else []), blocks)


def _reduce_scatter(parts, *, tag):
    got = _swap_with_sibling(parts, name=f"{tag}_sibling")
    sums = [_chip_partials(p, g, name=f"{tag}_chipsum{t}") for t, (p, g) in enumerate(zip(parts, got))]
    arrived = _swap_between_chips([others for _, others in sums], name=f"{tag}_chips")
    return [(own, arr) for (own, _), arr in zip(sums, arrived)]


HBM = pl.BlockSpec(memory_space=pltpu.HBM)
SEM = pl.BlockSpec(memory_space=pltpu.SEMAPHORE)
DATAFLOW = pltpu.SideEffectType.DATAFLOW_SIDE_EFFECTING


def _tie(value, *before):
    return lax.optimization_barrier((value, *before))[0]


def _split_call(name, arrays, sems, n_new, body, after=None):
    n_a, n_s = len(arrays), len(sems)
    n_after = 0 if after is None else 1

    def kernel_body(*refs):
        new = refs[n_a + n_s + n_after:n_a + n_s + n_after + 2] if n_new else (None, None)
        body(refs[:n_a], refs[n_a:n_a + n_s], *new)
        refs[-1][...] = jnp.zeros_like(refs[-1])

    n_sem_out = 2 if n_new else 0
    out_shape = [pltpu.SemaphoreType.DMA((n_new,))] * n_sem_out
    out_shape += [pltpu.HBM(a.shape, a.dtype) for a in arrays]
    out_shape += [jax.ShapeDtypeStruct((SUBLANES, LANES), F32)]
    outs = pl.pallas_call(
        kernel_body,
        name=name,
        out_shape=out_shape,
        in_specs=[HBM] * n_a + [SEM] * n_s + [ANY] * n_after,
        out_specs=[SEM] * n_sem_out + [HBM] * n_a + [pl.BlockSpec(memory_space=pltpu.VMEM)],
        input_output_aliases={i: n_sem_out + i for i in range(n_a)},
        compiler_params=pltpu.CompilerParams(has_side_effects=DATAFLOW),
    )(*[pltpu.with_memory_space_constraint(a, pltpu.HBM) for a in arrays], *sems, *([after] if n_after else []))
    return tuple(outs[:n_sem_out]), list(outs[n_sem_out:n_sem_out + n_a]), outs[-1]


def _remote(src, dst, send, recv, k, to):
    return pltpu.make_async_remote_copy(src_ref=src, dst_ref=dst, send_sem=send.at[k], recv_sem=recv.at[k],
                                        device_id=to, device_id_type=MESH_ID)


def _gather_start(shards, *, name):
    n = len(shards)
    me = _block_id(lax.axis_index("x"), lax.axis_index("y"), lax.axis_index("c"))
    lands = [lax.dynamic_update_index_in_dim(lax.empty((N_DEV, *s.shape), s.dtype), s, me, 0) for s in shards]

    def body(refs, _, send, recv):
        x, y, core = lax.axis_index("x"), lax.axis_index("y"), lax.axis_index("c")
        targets = [(x, y, 1 - core), (1 - x, y, core), (x, 1 - y, core), (1 - x, 1 - y, core)]
        for t in range(n):
            for k, to in enumerate(targets):
                _remote(refs[t], refs[n + t].at[_block_id(x, y, core)], send, recv, 4 * t + k, to).start()

    sems, thru, token = _split_call(name, shards + lands, [], 4 * n, body)
    return {"n": n, "sems1": sems, "arrays": thru, "token": token}


def _gather_pass_on(st, *, name, after):
    n = st["n"]

    def body(refs, sems, send, recv):
        send1, recv1 = sems
        x, y, core = lax.axis_index("x"), lax.axis_index("y"), lax.axis_index("c")
        chips = [(1 - x, y), (x, 1 - y), (1 - x, 1 - y)]
        for t in range(n):
            for j, (px, py) in enumerate(chips):
                block = refs[n + t].at[_block_id(px, py, core)]
                _remote(refs[t], block, send1, recv1, 4 * t + 1 + j, (x, y, 1 - core)).wait_recv()
                _remote(block, block, send, recv, 3 * t + j, (x, y, 1 - core)).start()

    sems2, thru, token = _split_call(name, st["arrays"], list(st["sems1"]), 3 * n, body, after=after)
    return {**st, "sems2": sems2, "arrays": thru, "token": token}


def _gather_finish(st, *, name, after):
    n = st["n"]

    def body(refs, sems, *_):
        send1, recv1, send2, recv2 = sems
        x, y, core = lax.axis_index("x"), lax.axis_index("y"), lax.axis_index("c")
        sibling = (x, y, 1 - core)
        chips = [(1 - x, y), (x, 1 - y), (1 - x, 1 - y)]
        for t in range(n):
            land = refs[n + t]
            for k in range(4):
                _remote(refs[t], land.at[_block_id(x, y, core)], send1, recv1, 4 * t + k, sibling).wait_send()
            _remote(refs[t], land.at[_block_id(x, y, 1 - core)], send1, recv1, 4 * t, sibling).wait_recv()
            for j, (px, py) in enumerate(chips):
                _remote(land.at[_block_id(px, py, core)], land.at[_block_id(px, py, core)], send2, recv2, 3 * t + j,
                        sibling).wait_send()
                _remote(refs[t], land.at[_block_id(px, py, 1 - core)], send2, recv2, 3 * t + j, sibling).wait_recv()

    _, thru, _ = _split_call(name, st["arrays"], [*st["sems1"], *st["sems2"]], 0, body, after=after)
    return thru[n:]


def _swap_start(sources, lands, plan, per_array, *, name):
    n = len(sources)

    def body(refs, _, send, recv):
        x, y, core = lax.axis_index("x"), lax.axis_index("y"), lax.axis_index("c")
        for t in range(n):
            for k in range(per_array):
                block, to = plan(x, y, core, k)
                _remote(refs[t].at[block], refs[n + t].at[k], send, recv, per_array * t + k, to).start()

    sems, thru, token = _split_call(name, sources + lands, [], per_array * n, body)
    return {"n": n, "sems": sems, "arrays": thru, "token": token, "plan": plan, "per_array": per_array}


def _swap_finish(st, *, name, after):
    n, plan, per_array = st["n"], st["plan"], st["per_array"]

    def body(refs, sems, *_):
        send, recv = sems
        x, y, core = lax.axis_index("x"), lax.axis_index("y"), lax.axis_index("c")
        for t in range(n):
            for k in range(per_array):
                block, to = plan(x, y, core, k)
                cp = _remote(refs[t].at[block], refs[n + t].at[k], send, recv, per_array * t + k, to)
                cp.wait_send()
                cp.wait_recv()

    _, thru, _ = _split_call(name, st["arrays"], list(st["sems"]), 0, body, after=after)
    return thru[:n], thru[n:]


def _to_sibling(x, y, core, k):
    px, py = [(x, y), (1 - x, y), (x, 1 - y), (1 - x, 1 - y)][k]
    return _block_id(px, py, 1 - core), (x, y, 1 - core)


def _to_chip(x, y, core, k):
    px, py = [(1 - x, y), (x, 1 - y), (1 - x, 1 - y)][k]
    return k, (px, py, core)


PARAMS = {
    "sc_w_in": ((2, 1024, 3072), 2), "sc_conv_w": ((2, 3, 1024), 2), "sc_w_out": ((2, 1024, 1024), 1),
    "mla_w_dq": ((1, 1024, 384), 1), "mla_g_q": ((1, 384), None), "mla_w_uq": ((1, 384, 1536), 2),
    "mla_w_dkv": ((1, 1024, 320), 1), "mla_g_kv": ((1, 256), None), "mla_w_uk": ((1, 256, 8, 128), 1),
    "mla_w_uv": ((1, 256, 8, 128), 1), "mla_w_o": ((1, 1024, 1024), 1), "cf_w_pw1": ((1, 1024, 2048), 2),
    "cf_b_pw1": ((1, 2048), 1), "cf_dw_w": ((1, 31, 1024), 2), "cf_dw_b": ((1, 1024), 1), "cf_norm_g": ((1, 1024), 1),
    "cf_norm_b": ((1, 1024), 1), "cf_w_pw2": ((1, 1024, 1024), 1), "cf_b_pw2": ((1, 1024), 1),
    "ff_w1": ((4, 1024, 4096), 2), "ff_w2": ((4, 4096, 1024), 1), "ln_mix_g": ((4, 1024), None),
    "ln_mix_b": ((4, 1024), None), "ln_ff_g": ((4, 1024), None), "ln_ff_b": ((4, 1024), None),
}
NAMES = list(PARAMS)
BIG = ["sc_w_in", "sc_w_out", "mla_w_dq", "mla_w_uq", "mla_w_dkv", "mla_w_uk", "mla_w_uv", "mla_w_o", "cf_w_pw1",
       "cf_w_pw2", "ff_w1", "ff_w2"]
SMALL = [n for n in NAMES if n not in BIG]
SMALL_SHARDED = [n for n in SMALL if PARAMS[n][1] is not None]
SMALL_WIDTH = LANES
DEV_BLOCKED = ("sc_w_in", "cf_w_pw1", "ff_w1")
MIXER_BIG = {0: ["sc_w_in", "sc_w_out"], 1: ["mla_w_dq", "mla_w_uq", "mla_w_dkv", "mla_w_uk", "mla_w_uv", "mla_w_o"],
             2: ["cf_w_pw1", "cf_w_pw2"]}


def _layer_big(i):
    return [(n, i // N_MIXERS) for n in MIXER_BIG[i % N_MIXERS]] + [("ff_w1", i), ("ff_w2", i)]


def _shard_2d(name, a):
    return a.reshape(a.shape[0], -1)


def _as_operand(name, g):
    if name in DEV_BLOCKED:
        return g
    if name == "mla_w_uq":
        return g.transpose(1, 0, 2).reshape(g.shape[1], -1)
    return g.reshape(-1, g.shape[2])


def _as_parts(name, grad):
    if name in DEV_BLOCKED:
        return grad
    if name == "mla_w_uq":
        return grad.reshape(grad.shape[0], N_DEV, -1).transpose(1, 0, 2)
    return grad.reshape(N_DEV, -1, grad.shape[1])


def _shard_shape(name):
    shape, ax = PARAMS[name]
    if ax is None:
        return shape
    return tuple(s // N_DEV if i == ax else s for i, s in enumerate(shape))


def _rows_of(shape, width, align):
    n = 1
    for s in shape:
        n *= s
    rows = -(-n // width)
    return -(-rows // align) * align


def _pack_rows(arrays, width, align, dtype, lead=0):
    segs = []
    for a in arrays:
        batch = a.shape[:lead]
        flat = a.astype(dtype).reshape(*batch, -1)
        rows = _rows_of(a.shape[lead:], width, align)
        flat = jnp.pad(flat, [(0, 0)] * lead + [(0, rows * width - flat.shape[-1])])
        segs.append(flat.reshape(*batch, rows, width))
    return jnp.concatenate(segs, axis=lead)


def _unpack_rows(buf, shapes, width, align):
    out, off = [], 0
    lead = buf.shape[:-2]
    for shape in shapes:
        n = 1
        for s in shape:
            n *= s
        rows = _rows_of(shape, width, align)
        seg = buf[..., off:off + rows, :].reshape(*lead, rows * width)[..., :n]
        out.append(seg.reshape(*lead, *shape))
        off += rows
    return out


def _merge_shards(stacked, ax):
    moved = jnp.moveaxis(stacked, 0, ax)
    shape = moved.shape
    return moved.reshape(*shape[:ax], shape[ax] * shape[ax + 1], *shape[ax + 2:])


def _split_shards(full, ax):
    shape = full.shape
    split = full.reshape(*shape[:ax], N_DEV, shape[ax] // N_DEV, *shape[ax + 1:])
    return jnp.moveaxis(split, ax, 0)


def _adam_math(wv, gv, mv, vv):
    m_new = ADAM_B1 * mv + (1.0 - ADAM_B1) * gv
    v_new = ADAM_B2 * vv + (1.0 - ADAM_B2) * (gv * gv)
    m_hat = m_new / (1.0 - ADAM_B1 ** ADAM_STEP)
    v_hat = v_new / (1.0 - ADAM_B2 ** ADAM_STEP)
    delta = -ADAM_LR * (m_hat / (jnp.sqrt(v_hat) + ADAM_EPS) + ADAM_WD * wv)
    return delta, m_new, v_new


def _adamw(w, g, m, v, *, name):
    shape = w.shape
    c = shape[-1]

    def fn(rows, bc):
        return _adam_math(*rows), ()

    flat = [a.reshape(-1, c) for a in (w, g, m, v)]
    outs = _rowwise(fn, flat, [], [(c, F32)] * 3, [], name=name, tb_cap=512)
    return tuple(o.reshape(shape) for o in outs)


def _adamw_reduced(w, m, v, layer_grads, *, name):
    n_l, r, c = w.shape
    tb = _row_tile(r, 256)
    nb = r // tb

    def body(*refs):
        w_ref, m_ref, v_ref = refs[:3]
        g_refs = refs[3:3 + 2 * n_l]
        g_out, d_out, m_out, v_out = refs[3 + 2 * n_l:]
        layer = pl.program_id(0)
        for l in range(n_l):
            @pl.when(layer == l)
            def _(l=l):
                own, arr = g_refs[2 * l], g_refs[2 * l + 1]
                g = own[...] + arr[0].astype(F32)
                g = g + arr[1].astype(F32)
                g = g + arr[2].astype(F32)
                delta, m_new, v_new = _adam_math(w_ref[...], g, m_ref[...], v_ref[...])
                g_out[...] = g
                d_out[...] = delta
                m_out[...] = m_new
                v_out[...] = v_new

    stacked = pl.BlockSpec((tb, c), lambda l, i: (l * nb + i, 0))
    in_specs = [stacked] * 3
    operands = [a.reshape(n_l * r, c) for a in (w, m, v)]
    for l, (own, arr) in enumerate(layer_grads):
        in_specs.append(pl.BlockSpec((tb, c), lambda ll, i, l=l: (jnp.where(ll == l, i, 0), 0)))
        in_specs.append(pl.BlockSpec((3, tb, c), lambda ll, i, l=l: (0, jnp.where(ll == l, i, 0), 0)))
        operands += [own, arr]
    outs = pl.pallas_call(
        body,
        name=name,
        grid=(n_l, nb),
        in_specs=in_specs,
        out_specs=[stacked] * 4,
        out_shape=[jax.ShapeDtypeStruct((n_l * r, c), F32)] * 4,
        compiler_params=_params(("arbitrary", "arbitrary")),
    )(*operands)
    return tuple(o.reshape(n_l, r, c) for o in outs)


def _train_step(x, target, weights, m_state, v_state):
    me = _block_id(lax.axis_index("x"), lax.axis_index("y"), lax.axis_index("c"))

    small_shapes = [_shard_shape(n) for n in SMALL_SHARDED]
    small_all = _all_gather([_pack_rows([weights[n] for n in SMALL_SHARDED], SMALL_WIDTH, SUBLANES, F32)],
                            name="gather_small")[0]
    small_full = {n: weights[n] for n in SMALL if PARAMS[n][1] is None}
    for n, stacked in zip(SMALL_SHARDED, _unpack_rows(small_all, small_shapes, SMALL_WIDTH, SUBLANES)):
        small_full[n] = _merge_shards(stacked, PARAMS[n][1])

    def small_weights(i):
        wl = {}
        for n in SMALL:
            if n.startswith("ln_"):
                wl[n] = small_full[n][i:i + 1]
            elif n.startswith(("sc_", "mla_", "cf_")[i % N_MIXERS]):
                v = small_full[n][i // N_MIXERS]
                wl[n] = v if v.ndim == 2 else v.reshape(1, -1)
        return wl

    layers = [small_weights(i) for i in range(DEPTH)]
    groups = [(kind, i) for i in range(DEPTH) for kind in ("mix", "ff")]

    def group_names(g):
        kind, i = g
        if kind == "mix":
            return [(n, i // N_MIXERS) for n in MIXER_BIG[i % N_MIXERS]]
        return [("ff_w1", i), ("ff_w2", i)]

    def shards_of(g):
        return [_shard_2d(n, weights[n][j]).astype(MXU_DTYPE) for n, j in group_names(g)]

    def install(g, gathered):
        for (n, _), a in zip(group_names(g), gathered):
            layers[g[1]][n] = _as_operand(n, a)

    install(groups[0], _all_gather(shards_of(groups[0]), name="gather_g0"))
    gather = {1: _gather_start(shards_of(groups[1]), name="gather_g1_start")}
    cur, cur_m = x[0], _cast_input(x[0])
    saved = {}
    for k, g in enumerate(groups):
        kind, i = g
        tokens = [gather[1]["token"]] if k == 0 else []
        if k + 2 < len(groups):
            gather[k + 2] = _gather_start(shards_of(groups[k + 2]), name=f"gather_g{k + 2}_start")
            tokens.append(gather[k + 2]["token"])

        def pass_on(y, k=k):
            if k + 1 == len(groups):
                return None
            gather[k + 1] = _gather_pass_on(gather[k + 1], name=f"gather_g{k + 1}_pass", after=y)
            return gather[k + 1]["token"]

        phase = _mix_fwd if kind == "mix" else _ff_fwd
        cur, cur_m, saved[g] = phase(i, cur, cur_m, layers[i], hook=pass_on, after=tokens)
        if k + 1 < len(groups):
            install(groups[k + 1], _gather_finish(gather[k + 1], name=f"gather_g{k + 1}_finish", after=cur))

    loss_local, dy = _loss_and_grad(cur, target[0])
    loss = lax.psum(loss_local, MESH_AXES)

    reduced = {n: [None] * PARAMS[n][0][0] for n in BIG}
    small_grads = {n: [None] * PARAMS[n][0][0] for n in SMALL}
    reduce = {}

    def to_chips(p, after):
        st = reduce[p]
        parts, got = _swap_finish(st["sibling"], name=f"reduce_p{p}_sibling_finish", after=after)
        sums = [_chip_partials(a, b, name=f"reduce_p{p}_chipsum{t}") for t, (a, b) in enumerate(zip(parts, got))]
        st["own"] = [own for own, _ in sums]
        others = [o for _, o in sums]
        st["chips"] = _swap_start(others, [lax.empty(o.shape, o.dtype) for o in others], _to_chip, 3,
                                  name=f"reduce_p{p}_chips_start")
        return st["chips"]["token"]

    def arrived(p, after):
        st = reduce[p]
        _, got = _swap_finish(st["chips"], name=f"reduce_p{p}_chips_finish", after=after)
        for (n, j), own, arr in zip(st["names"], st["own"], got):
            reduced[n][j] = (own, arr)

    terms = [(1.0, dy)]
    back = list(reversed(groups))
    for p, g in enumerate(back):
        kind, i = g

        def after_first_kernel(dz, p=p):
            return to_chips(p - 1, dz) if p >= 1 else None

        phase = _mix_bwd if kind == "mix" else _ff_bwd
        tokens = [reduce[p - 1]["sibling"]["token"]] if p >= 1 else []
        terms, grads = phase(i, terms, saved[g], layers[i], hook=after_first_kernel, after=tokens)
        for n in SMALL:
            if n in grads:
                small_grads[n][i if n.startswith("ln_") else i // N_MIXERS] = grads[n]
        if p >= 2:
            arrived(p - 2, terms[-1][1])
        names = group_names(g)
        parts = [_as_parts(n, grads[n]) for n, _ in names]
        reduce[p] = {"names": names, "sibling": _swap_start(
            parts, [lax.empty((4, *a.shape[1:]), a.dtype) for a in parts], _to_sibling, 4,
            name=f"reduce_p{p}_sibling_start")}
    grad_x = _sum_terms(terms, name="grad_x")
    last = len(back) - 1
    arrived(last - 1, grad_x)
    tail = to_chips(last, grad_x)
    arrived(last, tail)

    small_mine = _pack_rows([jnp.stack(small_grads[n]).reshape(PARAMS[n][0]) for n in SMALL], SMALL_WIDTH, SUBLANES, F32)
    small_sum = _sum_blocks(None, _all_gather([small_mine], name="gather_small_grads")[0], name="sum_small_grads")

    grad, delta, new_m, new_v = {}, {}, {}, {}
    for n, g in zip(SMALL, _unpack_rows(small_sum, [PARAMS[n][0] for n in SMALL], SMALL_WIDTH, SUBLANES)):
        ax = PARAMS[n][1]
        grad[n] = g if ax is None else lax.dynamic_index_in_dim(_split_shards(g, ax), me, axis=0, keepdims=False)
        delta[n], new_m[n], new_v[n] = _adamw(weights[n], grad[n], m_state[n], v_state[n], name=f"adamw_{n}")
    for n in BIG:
        shape = weights[n].shape
        view = lambda a: a.reshape(shape[0], shape[1], -1)
        outs = _adamw_reduced(view(weights[n]), view(m_state[n]), view(v_state[n]), reduced[n], name=f"adamw_{n}")
        grad[n], delta[n], new_m[n], new_v[n] = [o.reshape(shape) for o in outs]
    return (loss, grad_x[None], *[grad[n] for n in NAMES], *[delta[n] for n in NAMES],
            *[new_m[n] for n in NAMES], *[new_v[n] for n in NAMES])


def kernel(x, sc_w_in, sc_conv_w, sc_w_out, mla_w_dq, mla_g_q, mla_w_uq, mla_w_dkv, mla_g_kv, mla_w_uk, mla_w_uv, mla_w_o, cf_w_pw1, cf_b_pw1, cf_dw_w, cf_dw_b, cf_norm_g, cf_norm_b, cf_w_pw2, cf_b_pw2, ff_w1, ff_w2, ln_mix_g, ln_mix_b, ln_ff_g, ln_ff_b, loss_target, m_sc_w_in, m_sc_conv_w, m_sc_w_out, m_mla_w_dq, m_mla_g_q, m_mla_w_uq, m_mla_w_dkv, m_mla_g_kv, m_mla_w_uk, m_mla_w_uv, m_mla_w_o, m_cf_w_pw1, m_cf_b_pw1, m_cf_dw_w, m_cf_dw_b, m_cf_norm_g, m_cf_norm_b, m_cf_w_pw2, m_cf_b_pw2, m_ff_w1, m_ff_w2, m_ln_mix_g, m_ln_mix_b, m_ln_ff_g, m_ln_ff_b, v_sc_w_in, v_sc_conv_w, v_sc_w_out, v_mla_w_dq, v_mla_g_q, v_mla_w_uq, v_mla_w_dkv, v_mla_g_kv, v_mla_w_uk, v_mla_w_uv, v_mla_w_o, v_cf_w_pw1, v_cf_b_pw1, v_cf_dw_w, v_cf_dw_b, v_cf_norm_g, v_cf_norm_b, v_cf_w_pw2, v_cf_b_pw2, v_ff_w1, v_ff_w2, v_ln_mix_g, v_ln_mix_b, v_ln_ff_g, v_ln_ff_b):
    w_list = (sc_w_in, sc_conv_w, sc_w_out, mla_w_dq, mla_g_q, mla_w_uq, mla_w_dkv, mla_g_kv, mla_w_uk, mla_w_uv, mla_w_o,
              cf_w_pw1, cf_b_pw1, cf_dw_w, cf_dw_b, cf_norm_g, cf_norm_b, cf_w_pw2, cf_b_pw2, ff_w1, ff_w2, ln_mix_g,
              ln_mix_b, ln_ff_g, ln_ff_b)
    m_list = (m_sc_w_in, m_sc_conv_w, m_sc_w_out, m_mla_w_dq, m_mla_g_q, m_mla_w_uq, m_mla_w_dkv, m_mla_g_kv, m_mla_w_uk,
              m_mla_w_uv, m_mla_w_o, m_cf_w_pw1, m_cf_b_pw1, m_cf_dw_w, m_cf_dw_b, m_cf_norm_g, m_cf_norm_b, m_cf_w_pw2,
              m_cf_b_pw2, m_ff_w1, m_ff_w2, m_ln_mix_g, m_ln_mix_b, m_ln_ff_g, m_ln_ff_b)
    v_list = (v_sc_w_in, v_sc_conv_w, v_sc_w_out, v_mla_w_dq, v_mla_g_q, v_mla_w_uq, v_mla_w_dkv, v_mla_g_kv, v_mla_w_uk,
              v_mla_w_uv, v_mla_w_o, v_cf_w_pw1, v_cf_b_pw1, v_cf_dw_w, v_cf_dw_b, v_cf_norm_g, v_cf_norm_b, v_cf_w_pw2,
              v_cf_b_pw2, v_ff_w1, v_ff_w2, v_ln_mix_g, v_ln_mix_b, v_ln_ff_g, v_ln_ff_b)
    return _train_step(x, loss_target, dict(zip(NAMES, w_list)), dict(zip(NAMES, m_list)), dict(zip(NAMES, v_list)))
```

```python
import functools

import jax
import jax.numpy as jnp
from jax import lax
from jax.experimental import pallas as pl
from jax.experimental.pallas import tpu as pltpu

F32 = jnp.float32
MXU_DTYPE = jnp.bfloat16

N_DEV = 8
D_MODEL = 1024
DEPTH = 4
N_MIXERS = 3
CHUNK = 64
ALPHA = (2.0 * DEPTH) ** 0.25
LN_EPS = 1e-5
RMS_EPS = 1e-6
MLA_HEADS = 8
QK_NOPE = 128
QK_ROPE = 64
V_HEAD = 128
Q_LORA = 384
KV_LORA = 256
ROPE_THETA = 10000.0
ADAM_LR = 0.001
ADAM_B1 = 0.9
ADAM_B2 = 0.999
ADAM_EPS = 1e-08
ADAM_WD = 0.01
ADAM_STEP = 10

LANES = 128
SUBLANES = 8
VMEM_LIMIT = 56 * 1024 * 1024
MM_VMEM_BUDGET = 40 * 1024 * 1024
CONV_PAD = 32
MASK_VALUE = -1e30

MESH_AXES = ("x", "y", "c")


def _params(semantics):
    return pltpu.CompilerParams(dimension_semantics=semantics, vmem_limit_bytes=VMEM_LIMIT)


def _tile(n, cap):
    if n <= cap:
        return n
    t = cap - cap % LANES
    while t >= LANES:
        if n % t == 0:
            return t
        t -= LANES
    raise ValueError(f"no tile for {n} under {cap}")


def _row_tile(rows, cap):
    if rows <= cap:
        return rows
    t = cap - cap % SUBLANES
    while t >= SUBLANES:
        if rows % t == 0:
            return t
        t -= SUBLANES
    raise ValueError(f"no row tile for {rows} under {cap}")


def _mm(a, b, *, name, ta=False, tb=False, b_dev=False, out_dev=False, out_dtypes=(F32,), epilogue=None,
        tile_extras=(), col_extras=(), after=(), tm_cap=1024, tn_cap=1024, tk_cap=4096):
    m, k = (a.shape[1], a.shape[0]) if ta else a.shape
    if b_dev:
        nd, b_rows, ns = b.shape
        n, kb = (b_rows, nd * ns) if tb else (nd * ns, b_rows)
    else:
        n, kb = (b.shape[0], b.shape[1]) if tb else (b.shape[1], b.shape[0])
    assert k == kb, (a.shape, b.shape, ta, tb)
    tk = ns if (b_dev and tb) else _tile(k, tk_cap)
    nk = k // tk
    fixed_tn = ns if (b_dev and not tb) else (n // N_DEV if out_dev else None)
    if out_dev:
        assert not tile_extras and not col_extras and n % N_DEV == 0
    out_bytes = sum(jnp.dtype(dt).itemsize for dt in out_dtypes) + sum(e.dtype.itemsize for e in tile_extras)
    tm, tn = None, None
    for cand_m, cand_n in ((tm_cap, tn_cap), (tm_cap, tn_cap // 2), (tm_cap // 2, tn_cap), (tm_cap // 2, tn_cap // 2),
                           (tm_cap // 4, tn_cap // 2), (tm_cap // 4, tn_cap // 4)):
        tm, tn = _tile(m, cand_m), (fixed_tn if fixed_tn is not None else _tile(n, cand_n))
        need = 2 * (tm * tk * a.dtype.itemsize + tk * tn * b.dtype.itemsize) + 2 * tm * tn * out_bytes
        need += tm * tn * 4 * (2 if nk > 1 else 1)
        if need <= MM_VMEM_BUDGET:
            break
    n_te, n_ce, n_out = len(tile_extras), len(col_extras), len(out_dtypes)
    dims = (((0 if ta else 1,), (1 if tb else 0,)), ((), ()))

    def body(*refs):
        a_ref, b_ref = refs[0], refs[1]
        te_refs = refs[2:2 + n_te]
        ce_refs = refs[2 + n_te:2 + n_te + n_ce]
        first_out = 2 + n_te + n_ce + len(after)
        out_refs = refs[first_out:first_out + n_out]

        def finish(acc):
            if epilogue is None:
                outs = (acc,)
            else:
                outs = epilogue(acc, *[r[...] for r in te_refs], *[r[...] for r in ce_refs])
            for o_ref, o in zip(out_refs, outs):
                o_ref[...] = o.astype(o_ref.dtype)

        def product():
            return lax.dot_general(a_ref[...].astype(MXU_DTYPE), b_ref[...].astype(MXU_DTYPE), dims,
                                   preferred_element_type=F32)

        if nk == 1:
            finish(product())
        else:
            acc_ref = refs[-1]
            kk = pl.program_id(2)

            @pl.when(kk == 0)
            def _():
                acc_ref[...] = jnp.zeros_like(acc_ref)

            acc_ref[...] += product()

            @pl.when(kk == nk - 1)
            def _():
                finish(acc_ref[...])

    a_spec = pl.BlockSpec((tk, tm), lambda i, j, kk: (kk, i)) if ta else pl.BlockSpec((tm, tk), lambda i, j, kk: (i, kk))
    if b_dev and tb:
        b_spec = pl.BlockSpec((None, tn, tk), lambda i, j, kk: (kk, j, 0))
    elif b_dev:
        b_spec = pl.BlockSpec((None, tk, tn), lambda i, j, kk: (j, kk, 0))
    elif tb:
        b_spec = pl.BlockSpec((tn, tk), lambda i, j, kk: (j, kk))
    else:
        b_spec = pl.BlockSpec((tk, tn), lambda i, j, kk: (kk, j))
    tile_spec = pl.BlockSpec((tm, tn), lambda i, j, kk: (i, j))
    col_spec = pl.BlockSpec((1, tn), lambda i, j, kk: (0, j))
    if out_dev:
        out_spec = pl.BlockSpec((None, tm, tn), lambda i, j, kk: (j, i, 0))
        out_shape = [jax.ShapeDtypeStruct((N_DEV, m, tn), dt) for dt in out_dtypes]
    else:
        out_spec = tile_spec
        out_shape = [jax.ShapeDtypeStruct((m, n), dt) for dt in out_dtypes]
    outs = pl.pallas_call(
        body,
        name=name,
        grid=(m // tm, n // tn, nk),
        in_specs=[a_spec, b_spec] + [tile_spec] * n_te + [col_spec] * n_ce + [ANY] * len(after),
        out_specs=[out_spec] * n_out,
        out_shape=out_shape,
        scratch_shapes=[pltpu.VMEM((tm, tn), F32)] if nk > 1 else [],
        compiler_params=_params(("parallel", "parallel", "arbitrary")),
    )(a, b, *tile_extras, *col_extras, *after)
    return outs[0] if n_out == 1 else tuple(outs)


def _rowwise(fn, rows, bcast, out_rows, out_accs, *, name, tb_cap=256, after=()):
    t = rows[0].shape[0]
    tb = _row_tile(t, tb_cap)
    n_r, n_b, n_o, n_a = len(rows), len(bcast), len(out_rows), len(out_accs)

    def body(*refs):
        r_refs = refs[:n_r]
        b_refs = refs[n_r:n_r + n_b]
        first_out = n_r + n_b + len(after)
        o_refs = refs[first_out:first_out + n_o]
        a_refs = refs[first_out + n_o:]
        outs, accs = fn([r[...] for r in r_refs], [r[...] for r in b_refs])
        for o_ref, o in zip(o_refs, outs):
            o_ref[...] = o.astype(o_ref.dtype)
        if n_a:
            i = pl.program_id(0)

            @pl.when(i == 0)
            def _():
                for a_ref, acc in zip(a_refs, accs):
                    a_ref[...] = acc

            @pl.when(i > 0)
            def _():
                for a_ref, acc in zip(a_refs, accs):
                    a_ref[...] += acc

    in_specs = [pl.BlockSpec((tb, r.shape[1]), lambda i: (i, 0)) for r in rows]
    in_specs += [pl.BlockSpec(b.shape, lambda i: (0, 0)) for b in bcast]
    in_specs += [ANY] * len(after)
    out_specs = [pl.BlockSpec((tb, c), lambda i: (i, 0)) for c, _ in out_rows]
    out_specs += [pl.BlockSpec(s, lambda i: (0, 0)) for s in out_accs]
    out_shape = [jax.ShapeDtypeStruct((t, c), dt) for c, dt in out_rows]
    out_shape += [jax.ShapeDtypeStruct(s, F32) for s in out_accs]
    outs = pl.pallas_call(
        body,
        name=name,
        grid=(t // tb,),
        in_specs=in_specs,
        out_specs=out_specs,
        out_shape=out_shape,
        compiler_params=_params(("arbitrary",)),
    )(*rows, *bcast, *after)
    return tuple(outs)


def _colsum(v):
    return jnp.sum(v, axis=0, keepdims=True)


def _sigmoid(v):
    return 1.0 / (1.0 + jnp.exp(-v))


def _ln_fwd(x_res, y_sub, g, b, *, name, after=()):
    d = x_res.shape[1]

    def fn(rows, bc):
        z = ALPHA * rows[0] + rows[1]
        mu = jnp.mean(z, axis=-1, keepdims=True)
        zc = z - mu
        var = jnp.mean(zc * zc, axis=-1, keepdims=True)
        rstd = lax.rsqrt(var + LN_EPS)
        xhat = zc * rstd
        out = xhat * bc[0] + bc[1]
        return (out, out, xhat, rstd), ()

    return _rowwise(fn, [x_res, y_sub], [g, b], [(d, F32), (d, MXU_DTYPE), (d, F32), (1, F32)], [], name=name,
                    after=after)


def _ln_bwd(terms, xhat, rstd, g, *, name, after=()):
    d = xhat.shape[1]
    coefs = [c for c, _ in terms]

    def fn(rows, bc):
        xh, rs = rows[0], rows[1]
        dout = None
        for c, v in zip(coefs, rows[2:]):
            tv = v if c == 1.0 else c * v
            dout = tv if dout is None else dout + tv
        dxh = dout * bc[0]
        m1 = jnp.mean(dxh, axis=-1, keepdims=True)
        m2 = jnp.mean(dxh * xh, axis=-1, keepdims=True)
        dz = rs * (dxh - m1 - xh * m2)
        return (dz, dz), (_colsum(dout * xh), _colsum(dout), _colsum(dz))

    return _rowwise(fn, [xhat, rstd] + [v for _, v in terms], [g], [(d, F32), (d, MXU_DTYPE)],
                    [(1, d), (1, d), (1, d)], name=name, after=after)


def _rms_fwd(x, g, *, name):
    c = x.shape[1]

    def fn(rows, bc):
        v = rows[0]
        r = lax.rsqrt(jnp.mean(v * v, axis=-1, keepdims=True) + RMS_EPS)
        return (v * r * bc[0],), ()

    return _rowwise(fn, [x], [g], [(c, MXU_DTYPE)], [], name=name)[0]


def _rms_bwd(x, g, dy, *, name):
    c = x.shape[1]

    def fn(rows, bc):
        v, dyv = rows
        r = lax.rsqrt(jnp.mean(v * v, axis=-1, keepdims=True) + RMS_EPS)
        xh = v * r
        dxh = dyv * bc[0]
        dx = r * (dxh - xh * jnp.mean(dxh * xh, axis=-1, keepdims=True))
        return (dx,), (_colsum(dyv * xh),)

    return _rowwise(fn, [x, dy], [g], [(c, F32)], [(1, c)], name=name)


def _conv_chunk(t):
    return _row_tile(t, 256)


def _conv_fwd(x, w, bias, *, name):
    t, c = x.shape
    kw = w.shape[0]
    ch = _conv_chunk(t)

    def body(x_ref, w_ref, b_ref, o_ref, pad_ref):
        pad_ref[0:CONV_PAD, :] = jnp.zeros((CONV_PAD, LANES), F32)
        pad_ref[CONV_PAD:CONV_PAD + t, :] = x_ref[...]
        for t0 in range(0, t, ch):
            acc = jnp.broadcast_to(b_ref[...], (ch, LANES))
            for k in range(kw):
                s = kw - 1 - k
                acc = acc + w_ref[k:k + 1, :] * pad_ref[CONV_PAD - s + t0:CONV_PAD - s + t0 + ch, :]
            o_ref[t0:t0 + ch, :] = acc

    return pl.pallas_call(
        body,
        name=name,
        grid=(c // LANES,),
        in_specs=[pl.BlockSpec((t, LANES), lambda j: (0, j)), pl.BlockSpec((kw, LANES), lambda j: (0, j)),
                  pl.BlockSpec((1, LANES), lambda j: (0, j))],
        out_specs=pl.BlockSpec((t, LANES), lambda j: (0, j)),
        out_shape=jax.ShapeDtypeStruct((t, c), F32),
        scratch_shapes=[pltpu.VMEM((CONV_PAD + t, LANES), F32)],
        compiler_params=_params(("parallel",)),
    )(x, w, bias)


def _conv_bwd(dy, x, w, *, name):
    t, c = x.shape
    kw = w.shape[0]
    ch = _conv_chunk(t)

    def body(dy_ref, x_ref, w_ref, dx_ref, dw_ref, xpad_ref, dpad_ref):
        xpad_ref[0:CONV_PAD, :] = jnp.zeros((CONV_PAD, LANES), F32)
        xpad_ref[CONV_PAD:CONV_PAD + t, :] = x_ref[...]
        dpad_ref[0:t, :] = dy_ref[...]
        dpad_ref[t:t + CONV_PAD, :] = jnp.zeros((CONV_PAD, LANES), F32)
        for t0 in range(0, t, ch):
            acc = jnp.zeros((ch, LANES), F32)
            for k in range(kw):
                s = kw - 1 - k
                acc = acc + w_ref[k:k + 1, :] * dpad_ref[t0 + s:t0 + s + ch, :]
            dx_ref[t0:t0 + ch, :] = acc
        for k in range(kw):
            s = kw - 1 - k
            acc = jnp.zeros((ch, LANES), F32)
            for t0 in range(0, t, ch):
                acc = acc + dy_ref[t0:t0 + ch, :] * xpad_ref[CONV_PAD - s + t0:CONV_PAD - s + t0 + ch, :]
            dw_ref[k:k + 1, :] = _colsum(acc)

    blk = pl.BlockSpec((t, LANES), lambda j: (0, j))
    wblk = pl.BlockSpec((kw, LANES), lambda j: (0, j))
    return pl.pallas_call(
        body,
        name=name,
        grid=(c // LANES,),
        in_specs=[blk, blk, wblk],
        out_specs=[blk, wblk],
        out_shape=[jax.ShapeDtypeStruct((t, c), F32), jax.ShapeDtypeStruct((kw, c), F32)],
        scratch_shapes=[pltpu.VMEM((CONV_PAD + t, LANES), F32), pltpu.VMEM((CONV_PAD + t, LANES), F32)],
        compiler_params=_params(("parallel",)),
    )(dy, x, w)


ATTN_SCALE = (QK_NOPE + QK_ROPE) ** -0.5
_NT = (((1,), (1,)), ((), ()))
_TN = (((0,), (0,)), ((), ()))
_NN = (((1,), (0,)), ((), ()))


def _dot(a, b, dims):
    return lax.dot_general(a, b, dims, preferred_element_type=F32)


def _block_scores(qn, qp, kn_ref, kp_ref, lo, hi, diagonal):
    s = _dot(qn, kn_ref[lo:hi, :], _NT) + _dot(qp, kp_ref[lo:hi, :], _NT)
    s = s * ATTN_SCALE
    if diagonal:
        shift = CHUNK.bit_length() - 1
        rows = lax.broadcasted_iota(jnp.int32, s.shape, 0)
        cols = lax.broadcasted_iota(jnp.int32, s.shape, 1)
        s = jnp.where(jnp.right_shift(cols, shift) <= jnp.right_shift(rows, shift), s, MASK_VALUE)
    return s


def _attn_fwd(qn, qp, kn, kp, v, *, name):
    h, t, _ = qn.shape
    tq = _row_tile(t, 256)
    assert tq % CHUNK == 0

    def body(qn_ref, qp_ref, kn_ref, kp_ref, v_ref, o_ref, of_ref, lse_ref):
        for r0 in range(0, t, tq):
            r1 = r0 + tq
            qn, qp = qn_ref[r0:r1, :], qp_ref[r0:r1, :]
            s_d = _block_scores(qn, qp, kn_ref, kp_ref, r0, r1, True)
            m = jnp.max(s_d, axis=-1, keepdims=True)
            if r0:
                s_b = _block_scores(qn, qp, kn_ref, kp_ref, 0, r0, False)
                m = jnp.maximum(m, jnp.max(s_b, axis=-1, keepdims=True))
            p_d = jnp.exp(s_d - m)
            l = jnp.sum(p_d, axis=-1, keepdims=True)
            acc = _dot(p_d.astype(MXU_DTYPE), v_ref[r0:r1, :], _NN)
            if r0:
                p_b = jnp.exp(s_b - m)
                l = l + jnp.sum(p_b, axis=-1, keepdims=True)
                acc = acc + _dot(p_b.astype(MXU_DTYPE), v_ref[0:r0, :], _NN)
            o = acc / l
            o_ref[r0:r1, :] = o.astype(o_ref.dtype)
            of_ref[r0:r1, :] = o
            lse_ref[r0:r1, :] = m + jnp.log(l)

    blk = lambda w: pl.BlockSpec((None, t, w), lambda hh: (hh, 0, 0))
    return pl.pallas_call(
        body,
        name=name,
        grid=(h,),
        in_specs=[blk(QK_NOPE), blk(QK_ROPE), blk(QK_NOPE), pl.BlockSpec((t, QK_ROPE), lambda hh: (0, 0)), blk(V_HEAD)],
        out_specs=[blk(V_HEAD), blk(V_HEAD), blk(1)],
        out_shape=[jax.ShapeDtypeStruct((h, t, V_HEAD), MXU_DTYPE), jax.ShapeDtypeStruct((h, t, V_HEAD), F32),
                   jax.ShapeDtypeStruct((h, t, 1), F32)],
        compiler_params=_params(("parallel",)),
    )(qn, qp, kn, kp, v)


def _attn_bwd(qn, qp, kn, kp, v, do, o, lse, *, name):
    h, t, _ = qn.shape
    tq = _row_tile(t, 256)

    def body(qn_ref, qp_ref, kn_ref, kp_ref, v_ref, do_ref, o_ref, lse_ref, dqn_ref, dqp_ref, dkn_ref, dkp_ref,
             dv_ref):
        dkn_ref[...] = jnp.zeros_like(dkn_ref)
        dv_ref[...] = jnp.zeros_like(dv_ref)

        @pl.when(pl.program_id(0) == 0)
        def _():
            dkp_ref[...] = jnp.zeros_like(dkp_ref)

        for r0 in range(0, t, tq):
            r1 = r0 + tq
            qn, qp, dov, lse = qn_ref[r0:r1, :], qp_ref[r0:r1, :], do_ref[r0:r1, :], lse_ref[r0:r1, :]
            delta = jnp.sum(dov.astype(F32) * o_ref[r0:r1, :], axis=-1, keepdims=True)

            def piece(lo, hi, diagonal):
                s = _block_scores(qn, qp, kn_ref, kp_ref, lo, hi, diagonal)
                p = jnp.exp(s - lse)
                dp = _dot(dov, v_ref[lo:hi, :], _NT)
                ds_m = (p * (dp - delta) * ATTN_SCALE).astype(MXU_DTYPE)
                dkn_ref[lo:hi, :] += _dot(ds_m, qn, _TN)
                dkp_ref[lo:hi, :] += _dot(ds_m, qp, _TN)
                dv_ref[lo:hi, :] += _dot(p.astype(MXU_DTYPE), dov, _TN)
                return _dot(ds_m, kn_ref[lo:hi, :], _NN), _dot(ds_m, kp_ref[lo:hi, :], _NN)

            dqn, dqp = piece(r0, r1, True)
            if r0:
                dqn_b, dqp_b = piece(0, r0, False)
                dqn, dqp = dqn + dqn_b, dqp + dqp_b
            dqn_ref[r0:r1, :] = dqn
            dqp_ref[r0:r1, :] = dqp

    blk = lambda w: pl.BlockSpec((None, t, w), lambda hh: (hh, 0, 0))
    kpblk = pl.BlockSpec((t, QK_ROPE), lambda hh: (0, 0))
    return pl.pallas_call(
        body,
        name=name,
        grid=(h,),
        in_specs=[blk(QK_NOPE), blk(QK_ROPE), blk(QK_NOPE), kpblk, blk(V_HEAD), blk(V_HEAD), blk(V_HEAD), blk(1)],
        out_specs=[blk(QK_NOPE), blk(QK_ROPE), blk(QK_NOPE), kpblk, blk(V_HEAD)],
        out_shape=[jax.ShapeDtypeStruct((h, t, QK_NOPE), F32), jax.ShapeDtypeStruct((h, t, QK_ROPE), F32),
                   jax.ShapeDtypeStruct((h, t, QK_NOPE), F32), jax.ShapeDtypeStruct((t, QK_ROPE), F32),
                   jax.ShapeDtypeStruct((h, t, V_HEAD), F32)],
        compiler_params=_params(("arbitrary",)),
    )(qn, qp, kn, kp, v, do, o, lse)


def _rope(x1, x2, cos, sin, *, name, inverse=False):
    w = x1.shape[1]
    sign = -1.0 if inverse else 1.0

    def fn(rows, bc):
        a, b, c, s = rows
        s = sign * s
        return (a * c - b * s, a * s + b * c), ()

    return _rowwise(fn, [x1, x2, cos, sin], [], [(w, F32), (w, F32)], [], name=name)


def _mlp_fwd(x_m, w1, w2, tag, after=()):
    def epi(acc):
        r = jnp.maximum(acc, 0.0)
        return (r * r,)

    act = _mm(x_m, w1, b_dev=True, name=f"{tag}_up", out_dtypes=(MXU_DTYPE,), epilogue=epi, after=after)
    y = _mm(act, w2, name=f"{tag}_down")
    return y, (x_m, act)


def _mlp_bwd(dy_m, saved, w1, w2, tag, after=()):
    x_m, act = saved
    dw2 = _mm(act, dy_m, ta=True, name=f"{tag}_dw2", out_dtypes=(MXU_DTYPE,))

    def epi(acc, av):
        return (acc * (2.0 * jnp.sqrt(av.astype(F32))),)

    dh = _mm(dy_m, w2, tb=True, name=f"{tag}_dact", out_dtypes=(MXU_DTYPE,), epilogue=epi, tile_extras=(act,),
             after=after)
    dw1 = _mm(x_m, dh, ta=True, out_dev=True, name=f"{tag}_dw1", out_dtypes=(MXU_DTYPE,))
    dx = _mm(dh, w1, tb=True, b_dev=True, name=f"{tag}_dx")
    return [dx], {"ff_w1": dw1, "ff_w2": dw2}


def _sc_fwd(x_m, w_in, conv_w, w_out, tag, after=()):
    d = D_MODEL
    u = _mm(x_m, w_in, b_dev=True, name=f"{tag}_in", after=after)

    def gate_fn(rows, bc):
        uv = rows[0]
        return (uv[:, d:2 * d] * uv[:, 2 * d:3 * d],), ()

    p = _rowwise(gate_fn, [u], [], [(d, F32)], [], name=f"{tag}_ch")[0]
    q = _conv_fwd(p, conv_w, jnp.zeros((1, d), F32), name=f"{tag}_conv")

    def out_fn(rows, bc):
        return (rows[0][:, 0:d] * rows[1],), ()

    r = _rowwise(out_fn, [u, q], [], [(d, MXU_DTYPE)], [], name=f"{tag}_bq")[0]
    y = _mm(r, w_out, name=f"{tag}_out")
    return y, (x_m, u, p, q, r)


def _sc_bwd(dy_m, saved, w_in, conv_w, w_out, tag, after=()):
    d = D_MODEL
    x_m, u, p, q, r = saved
    dw_out = _mm(r, dy_m, ta=True, name=f"{tag}_dwout", out_dtypes=(MXU_DTYPE,))
    dr = _mm(dy_m, w_out, tb=True, name=f"{tag}_dr", after=after)

    def dq_fn(rows, bc):
        return (rows[0] * rows[1][:, 0:d],), ()

    dq = _rowwise(dq_fn, [dr, u], [], [(d, F32)], [], name=f"{tag}_dq")[0]
    dp, dconv = _conv_bwd(dq, p, conv_w, name=f"{tag}_dconv")

    def du_fn(rows, bc):
        drv, qv, dpv, uv = rows
        return (jnp.concatenate([drv * qv, dpv * uv[:, 2 * d:3 * d], dpv * uv[:, d:2 * d]], axis=1),), ()

    du = _rowwise(du_fn, [dr, q, dp, u], [], [(3 * d, MXU_DTYPE)], [], name=f"{tag}_du")[0]
    dw_in = _mm(x_m, du, ta=True, out_dev=True, name=f"{tag}_dwin", out_dtypes=(MXU_DTYPE,))
    dx = _mm(du, w_in, tb=True, b_dev=True, name=f"{tag}_dx")
    return [dx], {"sc_w_in": dw_in, "sc_conv_w": dconv, "sc_w_out": dw_out}


def _cf_fwd(x_m, w, tag, after=()):
    d = D_MODEL

    def bias_epi(acc, bias):
        return (acc + bias,)

    u = _mm(x_m, w["cf_w_pw1"], b_dev=True, name=f"{tag}_pw1", epilogue=bias_epi, col_extras=(w["cf_b_pw1"],),
            after=after)

    def glu_fn(rows, bc):
        uv = rows[0]
        return (uv[:, 0:d] * _sigmoid(uv[:, d:2 * d]),), ()

    h1 = _rowwise(glu_fn, [u], [], [(d, F32)], [], name=f"{tag}_glu")[0]
    h2 = _conv_fwd(h1, w["cf_dw_w"], w["cf_dw_b"], name=f"{tag}_conv")

    def norm_fn(rows, bc):
        hv = rows[0]
        mu = jnp.mean(hv, axis=-1, keepdims=True)
        hc = hv - mu
        rstd = lax.rsqrt(jnp.mean(hc * hc, axis=-1, keepdims=True) + LN_EPS)
        h3 = hc * rstd * bc[0] + bc[1]
        return (h3 * _sigmoid(h3),), ()

    h4 = _rowwise(norm_fn, [h2], [w["cf_norm_g"], w["cf_norm_b"]], [(d, MXU_DTYPE)], [], name=f"{tag}_norm")[0]
    y = _mm(h4, w["cf_w_pw2"], name=f"{tag}_pw2", epilogue=bias_epi, col_extras=(w["cf_b_pw2"],))
    return y, (x_m, u, h1, h2, h4)


def _cf_bwd(dy_m, dy_colsum, saved, w, tag, after=()):
    d = D_MODEL
    x_m, u, h1, h2, h4 = saved
    dw_pw2 = _mm(h4, dy_m, ta=True, name=f"{tag}_dwpw2", out_dtypes=(MXU_DTYPE,))
    dh4 = _mm(dy_m, w["cf_w_pw2"], tb=True, name=f"{tag}_dh4", after=after)

    def dnorm_fn(rows, bc):
        dh4v, hv = rows
        g, b = bc
        mu = jnp.mean(hv, axis=-1, keepdims=True)
        hc = hv - mu
        rstd = lax.rsqrt(jnp.mean(hc * hc, axis=-1, keepdims=True) + LN_EPS)
        xh = hc * rstd
        h3 = xh * g + b
        sg = _sigmoid(h3)
        dh3 = dh4v * (sg * (1.0 + h3 * (1.0 - sg)))
        dxh = dh3 * g
        dh2 = rstd * (dxh - jnp.mean(dxh, axis=-1, keepdims=True) - xh * jnp.mean(dxh * xh, axis=-1, keepdims=True))
        return (dh2,), (_colsum(dh3 * xh), _colsum(dh3), _colsum(dh2))

    dh2, dnorm_g, dnorm_b, ddw_b = _rowwise(dnorm_fn, [dh4, h2], [w["cf_norm_g"], w["cf_norm_b"]], [(d, F32)],
                                            [(1, d), (1, d), (1, d)], name=f"{tag}_dnorm")
    dh1, ddw_w = _conv_bwd(dh2, h1, w["cf_dw_w"], name=f"{tag}_dconv")

    def dglu_fn(rows, bc):
        dh1v, uv = rows
        a, gate = uv[:, 0:d], uv[:, d:2 * d]
        sg = _sigmoid(gate)
        du = jnp.concatenate([dh1v * sg, dh1v * a * sg * (1.0 - sg)], axis=1)
        return (du,), (_colsum(du),)

    du, db_pw1 = _rowwise(dglu_fn, [dh1, u], [], [(2 * d, MXU_DTYPE)], [(1, 2 * d)], name=f"{tag}_dglu")
    dw_pw1 = _mm(x_m, du, ta=True, out_dev=True, name=f"{tag}_dwpw1", out_dtypes=(MXU_DTYPE,))
    dx = _mm(du, w["cf_w_pw1"], tb=True, b_dev=True, name=f"{tag}_dx")
    grads = {"cf_w_pw1": dw_pw1, "cf_b_pw1": db_pw1, "cf_dw_w": ddw_w, "cf_dw_b": ddw_b, "cf_norm_g": dnorm_g,
             "cf_norm_b": dnorm_b, "cf_w_pw2": dw_pw2, "cf_b_pw2": dy_colsum}
    return [dx], grads


def _rope_tables(t):
    pos = jnp.arange(t, dtype=F32)
    inv_freq = ROPE_THETA ** (-jnp.arange(0, QK_ROPE, 2, dtype=F32) / QK_ROPE)
    ang = pos[:, None] * inv_freq[None, :]
    return jnp.cos(ang), jnp.sin(ang)


def _heads(a, width):
    return a.reshape(a.shape[0], MLA_HEADS, width).transpose(1, 0, 2)


def _unheads(a):
    return a.transpose(1, 0, 2).reshape(a.shape[1], -1)


def _mla_fwd(x_m, w, tag, after=()):
    t = x_m.shape[0]
    half = QK_ROPE // 2
    cos, sin = _rope_tables(t)
    cos_h, sin_h = jnp.tile(cos, (1, MLA_HEADS)), jnp.tile(sin, (1, MLA_HEADS))
    cq_pre = _mm(x_m, w["mla_w_dq"], name=f"{tag}_dq", after=after)
    cq = _rms_fwd(cq_pre, w["mla_g_q"], name=f"{tag}_qnorm")
    q = _mm(cq, w["mla_w_uq"], name=f"{tag}_uq").reshape(t, MLA_HEADS, QK_NOPE + QK_ROPE)
    q1 = q[:, :, QK_NOPE:QK_NOPE + half].reshape(t, MLA_HEADS * half)
    q2 = q[:, :, QK_NOPE + half:].reshape(t, MLA_HEADS * half)
    r1, r2 = _rope(q1, q2, cos_h, sin_h, name=f"{tag}_qrope")
    qp = jnp.concatenate([r1.reshape(t, MLA_HEADS, half), r2.reshape(t, MLA_HEADS, half)], axis=-1)
    qn_h = q[:, :, :QK_NOPE].transpose(1, 0, 2).astype(MXU_DTYPE)
    qp_h = qp.transpose(1, 0, 2).astype(MXU_DTYPE)
    ckv_full = _mm(x_m, w["mla_w_dkv"], name=f"{tag}_dkv")
    ckv_pre = ckv_full[:, :KV_LORA]
    ckv = _rms_fwd(ckv_pre, w["mla_g_kv"], name=f"{tag}_kvnorm")
    k1, k2 = _rope(ckv_full[:, KV_LORA:KV_LORA + half], ckv_full[:, KV_LORA + half:], cos, sin, name=f"{tag}_krope")
    kp = jnp.concatenate([k1, k2], axis=-1).astype(MXU_DTYPE)
    kn_h = _heads(_mm(ckv, w["mla_w_uk"], name=f"{tag}_uk", out_dtypes=(MXU_DTYPE,)), QK_NOPE)
    v_h = _heads(_mm(ckv, w["mla_w_uv"], name=f"{tag}_uv", out_dtypes=(MXU_DTYPE,)), V_HEAD)
    o_h, o_f, lse = _attn_fwd(qn_h, qp_h, kn_h, kp, v_h, name=f"{tag}_attn")
    o_flat = _unheads(o_h)
    y = _mm(o_flat, w["mla_w_o"], name=f"{tag}_o")
    return y, (x_m, cq_pre, cq, ckv_pre, ckv, qn_h, qp_h, kn_h, kp, v_h, lse, o_flat, o_f, (cos, sin, cos_h, sin_h))


def _mla_bwd(dy_m, saved, w, tag, after=()):
    x_m, cq_pre, cq, ckv_pre, ckv, qn_h, qp_h, kn_h, kp, v_h, lse, o_flat, o_f, (cos, sin, cos_h, sin_h) = saved
    t = x_m.shape[0]
    half = QK_ROPE // 2
    dw_o = _mm(o_flat, dy_m, ta=True, name=f"{tag}_dwo", out_dtypes=(MXU_DTYPE,))
    do_h = _heads(_mm(dy_m, w["mla_w_o"], tb=True, name=f"{tag}_do", out_dtypes=(MXU_DTYPE,), after=after), V_HEAD)
    dqn_h, dqp_h, dkn_h, dkp, dv_h = _attn_bwd(qn_h, qp_h, kn_h, kp, v_h, do_h, o_f, lse, name=f"{tag}_dattn")
    dqp = dqp_h.transpose(1, 0, 2)
    d1, d2 = _rope(dqp[:, :, :half].reshape(t, -1), dqp[:, :, half:].reshape(t, -1), cos_h, sin_h,
                   name=f"{tag}_dqrope", inverse=True)
    dq = jnp.concatenate([dqn_h.transpose(1, 0, 2), d1.reshape(t, MLA_HEADS, half), d2.reshape(t, MLA_HEADS, half)],
                         axis=-1).reshape(t, -1).astype(MXU_DTYPE)
    dw_uq = _mm(cq, dq, ta=True, name=f"{tag}_dwuq", out_dtypes=(MXU_DTYPE,))
    dcq = _mm(dq, w["mla_w_uq"], tb=True, name=f"{tag}_dcq")
    dcq_pre, dg_q = _rms_bwd(cq_pre, w["mla_g_q"], dcq, name=f"{tag}_dqnorm")
    dcq_pre_m = dcq_pre.astype(MXU_DTYPE)
    dw_dq = _mm(x_m, dcq_pre_m, ta=True, name=f"{tag}_dwdq", out_dtypes=(MXU_DTYPE,))
    dx_q = _mm(dcq_pre_m, w["mla_w_dq"], tb=True, name=f"{tag}_dxq")
    dkn = _unheads(dkn_h).astype(MXU_DTYPE)
    dv = _unheads(dv_h).astype(MXU_DTYPE)
    dw_uk = _mm(ckv, dkn, ta=True, name=f"{tag}_dwuk", out_dtypes=(MXU_DTYPE,))
    dw_uv = _mm(ckv, dv, ta=True, name=f"{tag}_dwuv", out_dtypes=(MXU_DTYPE,))
    dckv_k = _mm(dkn, w["mla_w_uk"], tb=True, name=f"{tag}_dckvk")
    dckv = _mm(dv, w["mla_w_uv"], tb=True, name=f"{tag}_dckv", epilogue=lambda acc, other: (acc + other,),
               tile_extras=(dckv_k,))
    dckv_pre, dg_kv = _rms_bwd(ckv_pre, w["mla_g_kv"], dckv, name=f"{tag}_dkvnorm")
    dk1, dk2 = _rope(dkp[:, :half], dkp[:, half:], cos, sin, name=f"{tag}_dkrope", inverse=True)
    dckv_full = jnp.concatenate([dckv_pre, dk1, dk2], axis=-1).astype(MXU_DTYPE)
    dw_dkv = _mm(x_m, dckv_full, ta=True, name=f"{tag}_dwdkv", out_dtypes=(MXU_DTYPE,))
    dx_kv = _mm(dckv_full, w["mla_w_dkv"], tb=True, name=f"{tag}_dxkv")
    grads = {"mla_w_dq": dw_dq, "mla_g_q": dg_q, "mla_w_uq": dw_uq, "mla_w_dkv": dw_dkv, "mla_g_kv": dg_kv,
             "mla_w_uk": dw_uk, "mla_w_uv": dw_uv, "mla_w_o": dw_o}
    return [dx_q, dx_kv], grads


def _cast_input(x):
    cast = lambda rows, bc: ((rows[0],), ())
    return _rowwise(cast, [x], [], [(D_MODEL, MXU_DTYPE)], [], name="x_cast")[0]


def _hooked(value, hook):
    token = None if hook is None else hook(value)
    return () if token is None else (token,)


def _mix_fwd(i, cur, cur_m, wl, hook=None, after=()):
    mixer = i % N_MIXERS
    if mixer == 0:
        y, s_mix = _sc_fwd(cur_m, wl["sc_w_in"], wl["sc_conv_w"], wl["sc_w_out"], f"l{i}_sc", after=after)
    elif mixer == 1:
        y, s_mix = _mla_fwd(cur_m, wl, f"l{i}_mla", after=after)
    else:
        y, s_mix = _cf_fwd(cur_m, wl, f"l{i}_cf", after=after)
    cur, cur_m, xh_mix, rs_mix = _ln_fwd(cur, y, wl["ln_mix_g"], wl["ln_mix_b"], name=f"l{i}_lnmix",
                                         after=_hooked(y, hook))
    return cur, cur_m, (s_mix, xh_mix, rs_mix)


def _ff_fwd(i, cur, cur_m, wl, hook=None, after=()):
    y, s_ff = _mlp_fwd(cur_m, wl["ff_w1"], wl["ff_w2"], f"l{i}_ff", after=after)
    cur, cur_m, xh_ff, rs_ff = _ln_fwd(cur, y, wl["ln_ff_g"], wl["ln_ff_b"], name=f"l{i}_lnff", after=_hooked(y, hook))
    return cur, cur_m, (s_ff, xh_ff, rs_ff)


def _layer_fwd(i, cur, cur_m, wl):
    cur, cur_m, s_mix = _mix_fwd(i, cur, cur_m, wl)
    cur, cur_m, s_ff = _ff_fwd(i, cur, cur_m, wl)
    return cur, cur_m, (*s_mix, *s_ff)


def _loss_and_grad(out, target):
    d = D_MODEL

    def loss_fn(rows, bc):
        err = rows[0] - rows[1]
        return (err * (1.0 / d),), (_colsum(err * err),)

    dy, sq = _rowwise(loss_fn, [out, target], [], [(d, F32)], [(1, d)], name="loss")
    return (0.5 / d) * jnp.sum(sq), dy


def _ff_bwd(i, terms, saved, wl, hook=None, after=()):
    s_ff, xh_ff, rs_ff = saved
    grads = {}
    dz, dz_m, grads["ln_ff_g"], grads["ln_ff_b"], _ = _ln_bwd(terms, xh_ff, rs_ff, wl["ln_ff_g"], name=f"l{i}_dlnff",
                                                              after=after)
    dxs, g_ff = _mlp_bwd(dz_m, s_ff, wl["ff_w1"], wl["ff_w2"], f"l{i}_ff", after=_hooked(dz, hook))
    grads.update(g_ff)
    return [(ALPHA, dz)] + [(1.0, v) for v in dxs], grads


def _mix_bwd(i, terms, saved, wl, hook=None, after=()):
    mixer = i % N_MIXERS
    s_mix, xh_mix, rs_mix = saved
    grads = {}
    dz, dz_m, grads["ln_mix_g"], grads["ln_mix_b"], dz_sum = _ln_bwd(terms, xh_mix, rs_mix, wl["ln_mix_g"],
                                                                     name=f"l{i}_dlnmix", after=after)
    behind = _hooked(dz, hook)
    if mixer == 0:
        dxs, g_mix = _sc_bwd(dz_m, s_mix, wl["sc_w_in"], wl["sc_conv_w"], wl["sc_w_out"], f"l{i}_sc", after=behind)
    elif mixer == 1:
        dxs, g_mix = _mla_bwd(dz_m, s_mix, wl, f"l{i}_mla", after=behind)
    else:
        dxs, g_mix = _cf_bwd(dz_m, dz_sum, s_mix, wl, f"l{i}_cf", after=behind)
    grads.update(g_mix)
    return [(ALPHA, dz)] + [(1.0, v) for v in dxs], grads


def _layer_bwd(i, terms, saved, wl):
    terms, g_ff = _ff_bwd(i, terms, saved[3:], wl)
    terms, g_mix = _mix_bwd(i, terms, saved[:3], wl)
    return terms, {**g_ff, **g_mix}


def _sum_terms(terms, *, name):
    coefs = [c for c, _ in terms]

    def sum_fn(rows, bc):
        acc = None
        for c, v in zip(coefs, rows):
            tv = v if c == 1.0 else c * v
            acc = tv if acc is None else acc + tv
        return (acc,), ()

    return _rowwise(sum_fn, [v for _, v in terms], [], [(D_MODEL, F32)], [], name=name)[0]


def _local_step(x, target, layers):
    cur, cur_m = x, _cast_input(x)
    saved = []
    for i in range(DEPTH):
        cur, cur_m, s = _layer_fwd(i, cur, cur_m, layers[i])
        saved.append(s)
    loss, dy = _loss_and_grad(cur, target)
    terms = [(1.0, dy)]
    grads = [None] * DEPTH
    for i in reversed(range(DEPTH)):
        terms, grads[i] = _layer_bwd(i, terms, saved[i], layers[i])
    return loss, _sum_terms(terms, name="grad_x"), grads


MESH_ID = pl.DeviceIdType.MESH
ANY = pl.BlockSpec(memory_space=pl.ANY)


def _block_id(px, py, pc):
    return 4 * px + 2 * py + pc


COPIES_PER_GATHER = 7


def _all_gather(shards, *, name):
    n = len(shards)

    def body(*refs):
        x_refs, out_refs = refs[:n], refs[n:2 * n]
        send_sems, recv_sems, local_sems = refs[2 * n:]
        x, y, core = lax.axis_index("x"), lax.axis_index("y"), lax.axis_index("c")
        me, sibling = (x, y, core), (x, y, 1 - core)
        chips = [(1 - x, y), (x, 1 - y), (1 - x, 1 - y)]

        def copy(t, k, block, to, from_input=False):
            dst = out_refs[t].at[_block_id(*block)]
            return pltpu.make_async_remote_copy(
                src_ref=x_refs[t] if from_input else dst, dst_ref=dst,
                send_sem=send_sems.at[t * COPIES_PER_GATHER + k], recv_sem=recv_sems.at[t * COPIES_PER_GATHER + k],
                device_id=to, device_id_type=MESH_ID)

        started = []
        mine = [pltpu.make_async_copy(x_refs[t], out_refs[t].at[_block_id(*me)], local_sems.at[t]) for t in range(n)]
        for t in range(n):
            mine[t].start()
            first = [copy(t, 0, me, sibling, from_input=True)]
            first += [copy(t, 1 + j, me, (*chip, core), from_input=True) for j, chip in enumerate(chips)]
            for cp in first:
                cp.start()
            started += first
        for t in range(n):
            for j, chip in enumerate(chips):
                copy(t, 1 + j, (*chip, core), me).wait_recv()
                passed = copy(t, 4 + j, (*chip, core), sibling)
                passed.start()
                started.append(passed)
        for t in range(n):
            copy(t, 0, sibling, me).wait_recv()
            for j, chip in enumerate(chips):
                copy(t, 4 + j, (*chip, 1 - core), me).wait_recv()
        for cp in started:
            cp.wait_send()
        for cp in mine:
            cp.wait()

    n_sems = n * COPIES_PER_GATHER
    outs = pl.pallas_call(
        body,
        name=name,
        out_shape=[jax.ShapeDtypeStruct((N_DEV, *s.shape), s.dtype) for s in shards],
        in_specs=[ANY] * n,
        out_specs=[ANY] * n,
        scratch_shapes=[pltpu.SemaphoreType.DMA((n_sems,)), pltpu.SemaphoreType.DMA((n_sems,)),
                        pltpu.SemaphoreType.DMA((n,))],
    )(*shards)
    return list(outs)


def _swap_with_sibling(parts, *, name):
    n = len(parts)

    def body(*refs):
        p_refs, got_refs = refs[:n], refs[n:2 * n]
        send_sems, recv_sems = refs[2 * n:]
        x, y, core = lax.axis_index("x"), lax.axis_index("y"), lax.axis_index("c")
        chips = [(x, y), (1 - x, y), (x, 1 - y), (1 - x, 1 - y)]
        copies = []
        for t in range(n):
            for k, (px, py) in enumerate(chips):
                copies.append(pltpu.make_async_remote_copy(
                    src_ref=p_refs[t].at[_block_id(px, py, 1 - core)], dst_ref=got_refs[t].at[k],
                    send_sem=send_sems.at[4 * t + k], recv_sem=recv_sems.at[4 * t + k], device_id=(x, y, 1 - core),
                    device_id_type=MESH_ID))
        for cp in copies:
            cp.start()
        for cp in copies:
            cp.wait()

    outs = pl.pallas_call(
        body,
        name=name,
        out_shape=[jax.ShapeDtypeStruct((4, *p.shape[1:]), p.dtype) for p in parts],
        in_specs=[ANY] * n,
        out_specs=[ANY] * n,
        scratch_shapes=[pltpu.SemaphoreType.DMA((4 * n,)), pltpu.SemaphoreType.DMA((4 * n,))],
    )(*parts)
    return list(outs)


def _swap_between_chips(parts, *, name):
    n = len(parts)

    def body(*refs):
        p_refs, got_refs = refs[:n], refs[n:2 * n]
        send_sems, recv_sems = refs[2 * n:]
        x, y, core = lax.axis_index("x"), lax.axis_index("y"), lax.axis_index("c")
        chips = [(1 - x, y), (x, 1 - y), (1 - x, 1 - y)]
        copies = []
        for t in range(n):
            for j, (px, py) in enumerate(chips):
                copies.append(pltpu.make_async_remote_copy(
                    src_ref=p_refs[t].at[j], dst_ref=got_refs[t].at[j], send_sem=send_sems.at[3 * t + j],
                    recv_sem=recv_sems.at[3 * t + j], device_id=(px, py, core), device_id_type=MESH_ID))
        for cp in copies:
            cp.start()
        for cp in copies:
            cp.wait()

    outs = pl.pallas_call(
        body,
        name=name,
        out_shape=[jax.ShapeDtypeStruct(p.shape, p.dtype) for p in parts],
        in_specs=[ANY] * n,
        out_specs=[ANY] * n,
        scratch_shapes=[pltpu.SemaphoreType.DMA((3 * n,)), pltpu.SemaphoreType.DMA((3 * n,))],
    )(*parts)
    return list(outs)


def _chip_partials(parts, got, *, name):
    _, r, c = parts.shape
    tb = _row_tile(r, 512)
    x, y, core = lax.axis_index("x"), lax.axis_index("y"), lax.axis_index("c")
    chips = [(x, y), (1 - x, y), (x, 1 - y), (1 - x, 1 - y)]
    ids = jnp.stack([_block_id(px, py, core) for px, py in chips]).astype(jnp.int32)

    def body(ids_ref, p0, p1, p2, p3, g_ref, own_ref, out_ref):
        own_ref[...] = p0[...].astype(F32) + g_ref[0].astype(F32)
        for j, p in enumerate((p1, p2, p3)):
            out_ref[j] = (p[...].astype(F32) + g_ref[j + 1].astype(F32)).astype(out_ref.dtype)

    def part_spec(k):
        return pl.BlockSpec((None, tb, c), lambda i, ids_ref: (ids_ref[k], i, 0))

    grid_spec = pltpu.PrefetchScalarGridSpec(
        num_scalar_prefetch=1,
        grid=(r // tb,),
        in_specs=[part_spec(k) for k in range(4)] + [pl.BlockSpec((4, tb, c), lambda i, ids_ref: (0, i, 0))],
        out_specs=[pl.BlockSpec((tb, c), lambda i, ids_ref: (i, 0)), pl.BlockSpec((3, tb, c), lambda i, ids_ref: (0, i, 0))],
    )
    return pl.pallas_call(
        body,
        name=name,
        grid_spec=grid_spec,
        out_shape=[jax.ShapeDtypeStruct((r, c), F32), jax.ShapeDtypeStruct((3, r, c), parts.dtype)],
        compiler_params=_params(("parallel",)),
    )(ids, parts, parts, parts, parts, got)


def _sum_blocks(first, blocks, *, name):
    n, r, c = blocks.shape
    tb = _row_tile(r, 512)

    def body(*refs):
        b_ref, o_ref = refs[-2], refs[-1]
        acc = refs[0][...] if first is not None else b_ref[0].astype(F32)
        for j in range(0 if first is not None else 1, n):
            acc = acc + b_ref[j].astype(F32)
        o_ref[...] = acc

    row_spec = pl.BlockSpec((tb, c), lambda i: (i, 0))
    return pl.pallas_call(
        body,
        name=name,
        grid=(r // tb,),
        in_specs=([row_spec] if first is not None else []) + [pl.BlockSpec((n, tb, c), lambda i: (0, i, 0))],
        out_specs=row_spec,
        out_shape=jax.ShapeDtypeStruct((r, c), F32),
        compiler_params=_params(("parallel",)),
    )(*([first] if first is not None else []), blocks)


def _reduce_scatter(parts, *, tag):
    got = _swap_with_sibling(parts, name=f"{tag}_sibling")
    sums = [_chip_partials(p, g, name=f"{tag}_chipsum{t}") for t, (p, g) in enumerate(zip(parts, got))]
    arrived = _swap_between_chips([others for _, others in sums], name=f"{tag}_chips")
    return [(own, arr) for (own, _), arr in zip(sums, arrived)]


HBM = pl.BlockSpec(memory_space=pltpu.HBM)
SEM = pl.BlockSpec(memory_space=pltpu.SEMAPHORE)
DATAFLOW = pltpu.SideEffectType.DATAFLOW_SIDE_EFFECTING


def _tie(value, *before):
    return lax.optimization_barrier((value, *before))[0]


def _split_call(name, arrays, sems, n_new, body, after=None):
    n_a, n_s = len(arrays), len(sems)
    n_after = 0 if after is None else 1

    def kernel_body(*refs):
        new = refs[n_a + n_s + n_after:n_a + n_s + n_after + 2] if n_new else (None, None)
        body(refs[:n_a], refs[n_a:n_a + n_s], *new)
        refs[-1][...] = jnp.zeros_like(refs[-1])

    n_sem_out = 2 if n_new else 0
    out_shape = [pltpu.SemaphoreType.DMA((n_new,))] * n_sem_out
    out_shape += [pltpu.HBM(a.shape, a.dtype) for a in arrays]
    out_shape += [jax.ShapeDtypeStruct((SUBLANES, LANES), F32)]
    outs = pl.pallas_call(
        kernel_body,
        name=name,
        out_shape=out_shape,
        in_specs=[HBM] * n_a + [SEM] * n_s + [ANY] * n_after,
        out_specs=[SEM] * n_sem_out + [HBM] * n_a + [pl.BlockSpec(memory_space=pltpu.VMEM)],
        input_output_aliases={i: n_sem_out + i for i in range(n_a)},
        compiler_params=pltpu.CompilerParams(has_side_effects=DATAFLOW),
    )(*[pltpu.with_memory_space_constraint(a, pltpu.HBM) for a in arrays], *sems, *([after] if n_after else []))
    return tuple(outs[:n_sem_out]), list(outs[n_sem_out:n_sem_out + n_a]), outs[-1]


def _remote(src, dst, send, recv, k, to):
    return pltpu.make_async_remote_copy(src_ref=src, dst_ref=dst, send_sem=send.at[k], recv_sem=recv.at[k],
                                        device_id=to, device_id_type=MESH_ID)


def _gather_start(shards, *, name):
    n = len(shards)
    me = _block_id(lax.axis_index("x"), lax.axis_index("y"), lax.axis_index("c"))
    lands = [lax.dynamic_update_index_in_dim(lax.empty((N_DEV, *s.shape), s.dtype), s, me, 0) for s in shards]

    def body(refs, _, send, recv):
        x, y, core = lax.axis_index("x"), lax.axis_index("y"), lax.axis_index("c")
        targets = [(x, y, 1 - core), (1 - x, y, core), (x, 1 - y, core), (1 - x, 1 - y, core)]
        for t in range(n):
            for k, to in enumerate(targets):
                _remote(refs[t], refs[n + t].at[_block_id(x, y, core)], send, recv, 4 * t + k, to).start()

    sems, thru, token = _split_call(name, shards + lands, [], 4 * n, body)
    return {"n": n, "sems1": sems, "arrays": thru, "token": token}


def _gather_pass_on(st, *, name, after):
    n = st["n"]

    def body(refs, sems, send, recv):
        send1, recv1 = sems
        x, y, core = lax.axis_index("x"), lax.axis_index("y"), lax.axis_index("c")
        chips = [(1 - x, y), (x, 1 - y), (1 - x, 1 - y)]
        for t in range(n):
            for j, (px, py) in enumerate(chips):
                block = refs[n + t].at[_block_id(px, py, core)]
                _remote(refs[t], block, send1, recv1, 4 * t + 1 + j, (x, y, 1 - core)).wait_recv()
                _remote(block, block, send, recv, 3 * t + j, (x, y, 1 - core)).start()

    sems2, thru, token = _split_call(name, st["arrays"], list(st["sems1"]), 3 * n, body, after=after)
    return {**st, "sems2": sems2, "arrays": thru, "token": token}


def _gather_finish(st, *, name, after):
    n = st["n"]

    def body(refs, sems, *_):
        send1, recv1, send2, recv2 = sems
        x, y, core = lax.axis_index("x"), lax.axis_index("y"), lax.axis_index("c")
        sibling = (x, y, 1 - core)
        chips = [(1 - x, y), (x, 1 - y), (1 - x, 1 - y)]
        for t in range(n):
            land = refs[n + t]
            for k in range(4):
                _remote(refs[t], land.at[_block_id(x, y, core)], send1, recv1, 4 * t + k, sibling).wait_send()
            _remote(refs[t], land.at[_block_id(x, y, 1 - core)], send1, recv1, 4 * t, sibling).wait_recv()
            for j, (px, py) in enumerate(chips):
                _remote(land.at[_block_id(px, py, core)], land.at[_block_id(px, py, core)], send2, recv2, 3 * t + j,
                        sibling).wait_send()
                _remote(refs[t], land.at[_block_id(px, py, 1 - core)], send2, recv2, 3 * t + j, sibling).wait_recv()

    _, thru, _ = _split_call(name, st["arrays"], [*st["sems1"], *st["sems2"]], 0, body, after=after)
    return thru[n:]


def _swap_start(sources, lands, plan, per_array, *, name):
    n = len(sources)

    def body(refs, _, send, recv):
        x, y, core = lax.axis_index("x"), lax.axis_index("y"), lax.axis_index("c")
        for t in range(n):
            for k in range(per_array):
                block, to = plan(x, y, core, k)
                _remote(refs[t].at[block], refs[n + t].at[k], send, recv, per_array * t + k, to).start()

    sems, thru, token = _split_call(name, sources + lands, [], per_array * n, body)
    return {"n": n, "sems": sems, "arrays": thru, "token": token, "plan": plan, "per_array": per_array}


def _swap_finish(st, *, name, after):
    n, plan, per_array = st["n"], st["plan"], st["per_array"]

    def body(refs, sems, *_):
        send, recv = sems
        x, y, core = lax.axis_index("x"), lax.axis_index("y"), lax.axis_index("c")
        for t in range(n):
            for k in range(per_array):
                block, to = plan(x, y, core, k)
                cp = _remote(refs[t].at[block], refs[n + t].at[k], send, recv, per_array * t + k, to)
                cp.wait_send()
                cp.wait_recv()

    _, thru, _ = _split_call(name, st["arrays"], list(st["sems"]), 0, body, after=after)
    return thru[:n], thru[n:]


def _to_sibling(x, y, core, k):
    px, py = [(x, y), (1 - x, y), (x, 1 - y), (1 - x, 1 - y)][k]
    return _block_id(px, py, 1 - core), (x, y, 1 - core)


def _to_chip(x, y, core, k):
    px, py = [(1 - x, y), (x, 1 - y), (1 - x, 1 - y)][k]
    return k, (px, py, core)


PARAMS = {
    "sc_w_in": ((2, 1024, 3072), 2), "sc_conv_w": ((2, 3, 1024), 2), "sc_w_out": ((2, 1024, 1024), 1),
    "mla_w_dq": ((1, 1024, 384), 1), "mla_g_q": ((1, 384), None), "mla_w_uq": ((1, 384, 1536), 2),
    "mla_w_dkv": ((1, 1024, 320), 1), "mla_g_kv": ((1, 256), None), "mla_w_uk": ((1, 256, 8, 128), 1),
    "mla_w_uv": ((1, 256, 8, 128), 1), "mla_w_o": ((1, 1024, 1024), 1), "cf_w_pw1": ((1, 1024, 2048), 2),
    "cf_b_pw1": ((1, 2048), 1), "cf_dw_w": ((1, 31, 1024), 2), "cf_dw_b": ((1, 1024), 1), "cf_norm_g": ((1, 1024), 1),
    "cf_norm_b": ((1, 1024), 1), "cf_w_pw2": ((1, 1024, 1024), 1), "cf_b_pw2": ((1, 1024), 1),
    "ff_w1": ((4, 1024, 4096), 2), "ff_w2": ((4, 4096, 1024), 1), "ln_mix_g": ((4, 1024), None),
    "ln_mix_b": ((4, 1024), None), "ln_ff_g": ((4, 1024), None), "ln_ff_b": ((4, 1024), None),
}
NAMES = list(PARAMS)
BIG = ["sc_w_in", "sc_w_out", "mla_w_dq", "mla_w_uq", "mla_w_dkv", "mla_w_uk", "mla_w_uv", "mla_w_o", "cf_w_pw1",
       "cf_w_pw2", "ff_w1", "ff_w2"]
SMALL = [n for n in NAMES if n not in BIG]
SMALL_SHARDED = [n for n in SMALL if PARAMS[n][1] is not None]
SMALL_WIDTH = LANES
DEV_BLOCKED = ("sc_w_in", "cf_w_pw1", "ff_w1")
MIXER_BIG = {0: ["sc_w_in", "sc_w_out"], 1: ["mla_w_dq", "mla_w_uq", "mla_w_dkv", "mla_w_uk", "mla_w_uv", "mla_w_o"],
             2: ["cf_w_pw1", "cf_w_pw2"]}


def _layer_big(i):
    return [(n, i // N_MIXERS) for n in MIXER_BIG[i % N_MIXERS]] + [("ff_w1", i), ("ff_w2", i)]


def _shard_2d(name, a):
    return a.reshape(a.shape[0], -1)


def _as_operand(name, g):
    if name in DEV_BLOCKED:
        return g
    if name == "mla_w_uq":
        return g.transpose(1, 0, 2).reshape(g.shape[1], -1)
    return g.reshape(-1, g.shape[2])


def _as_parts(name, grad):
    if name in DEV_BLOCKED:
        return grad
    if name == "mla_w_uq":
        return grad.reshape(grad.shape[0], N_DEV, -1).transpose(1, 0, 2)
    return grad.reshape(N_DEV, -1, grad.shape[1])


def _shard_shape(name):
    shape, ax = PARAMS[name]
    if ax is None:
        return shape
    return tuple(s // N_DEV if i == ax else s for i, s in enumerate(shape))


def _rows_of(shape, width, align):
    n = 1
    for s in shape:
        n *= s
    rows = -(-n // width)
    return -(-rows // align) * align


def _pack_rows(arrays, width, align, dtype, lead=0):
    segs = []
    for a in arrays:
        batch = a.shape[:lead]
        flat = a.astype(dtype).reshape(*batch, -1)
        rows = _rows_of(a.shape[lead:], width, align)
        flat = jnp.pad(flat, [(0, 0)] * lead + [(0, rows * width - flat.shape[-1])])
        segs.append(flat.reshape(*batch, rows, width))
    return jnp.concatenate(segs, axis=lead)


def _unpack_rows(buf, shapes, width, align):
    out, off = [], 0
    lead = buf.shape[:-2]
    for shape in shapes:
        n = 1
        for s in shape:
            n *= s
        rows = _rows_of(shape, width, align)
        seg = buf[..., off:off + rows, :].reshape(*lead, rows * width)[..., :n]
        out.append(seg.reshape(*lead, *shape))
        off += rows
    return out


def _merge_shards(stacked, ax):
    moved = jnp.moveaxis(stacked, 0, ax)
    shape = moved.shape
    return moved.reshape(*shape[:ax], shape[ax] * shape[ax + 1], *shape[ax + 2:])


def _split_shards(full, ax):
    shape = full.shape
    split = full.reshape(*shape[:ax], N_DEV, shape[ax] // N_DEV, *shape[ax + 1:])
    return jnp.moveaxis(split, ax, 0)


def _adam_math(wv, gv, mv, vv):
    m_new = ADAM_B1 * mv + (1.0 - ADAM_B1) * gv
    v_new = ADAM_B2 * vv + (1.0 - ADAM_B2) * (gv * gv)
    m_hat = m_new / (1.0 - ADAM_B1 ** ADAM_STEP)
    v_hat = v_new / (1.0 - ADAM_B2 ** ADAM_STEP)
    delta = -ADAM_LR * (m_hat / (jnp.sqrt(v_hat) + ADAM_EPS) + ADAM_WD * wv)
    return delta, m_new, v_new


def _adamw(w, g, m, v, *, name):
    shape = w.shape
    c = shape[-1]

    def fn(rows, bc):
        return _adam_math(*rows), ()

    flat = [a.reshape(-1, c) for a in (w, g, m, v)]
    outs = _rowwise(fn, flat, [], [(c, F32)] * 3, [], name=name, tb_cap=512)
    return tuple(o.reshape(shape) for o in outs)


def _adamw_reduced(w, m, v, layer_grads, *, name):
    n_l, r, c = w.shape
    tb = _row_tile(r, 256)
    nb = r // tb

    def body(*refs):
        w_ref, m_ref, v_ref = refs[:3]
        g_refs = refs[3:3 + 2 * n_l]
        g_out, d_out, m_out, v_out = refs[3 + 2 * n_l:]
        layer = pl.program_id(0)
        for l in range(n_l):
            @pl.when(layer == l)
            def _(l=l):
                own, arr = g_refs[2 * l], g_refs[2 * l + 1]
                g = own[...] + arr[0].astype(F32)
                g = g + arr[1].astype(F32)
                g = g + arr[2].astype(F32)
                delta, m_new, v_new = _adam_math(w_ref[...], g, m_ref[...], v_ref[...])
                g_out[...] = g
                d_out[...] = delta
                m_out[...] = m_new
                v_out[...] = v_new

    stacked = pl.BlockSpec((tb, c), lambda l, i: (l * nb + i, 0))
    in_specs = [stacked] * 3
    operands = [a.reshape(n_l * r, c) for a in (w, m, v)]
    for l, (own, arr) in enumerate(layer_grads):
        in_specs.append(pl.BlockSpec((tb, c), lambda ll, i, l=l: (jnp.where(ll == l, i, 0), 0)))
        in_specs.append(pl.BlockSpec((3, tb, c), lambda ll, i, l=l: (0, jnp.where(ll == l, i, 0), 0)))
        operands += [own, arr]
    outs = pl.pallas_call(
        body,
        name=name,
        grid=(n_l, nb),
        in_specs=in_specs,
        out_specs=[stacked] * 4,
        out_shape=[jax.ShapeDtypeStruct((n_l * r, c), F32)] * 4,
        compiler_params=_params(("arbitrary", "arbitrary")),
    )(*operands)
    return tuple(o.reshape(n_l, r, c) for o in outs)


def _train_step(x, target, weights, m_state, v_state):
    me = _block_id(lax.axis_index("x"), lax.axis_index("y"), lax.axis_index("c"))

    small_shapes = [_shard_shape(n) for n in SMALL_SHARDED]
    small_all = _all_gather([_pack_rows([weights[n] for n in SMALL_SHARDED], SMALL_WIDTH, SUBLANES, F32)],
                            name="gather_small")[0]
    small_full = {n: weights[n] for n in SMALL if PARAMS[n][1] is None}
    for n, stacked in zip(SMALL_SHARDED, _unpack_rows(small_all, small_shapes, SMALL_WIDTH, SUBLANES)):
        small_full[n] = _merge_shards(stacked, PARAMS[n][1])

    def small_weights(i):
        wl = {}
        for n in SMALL:
            if n.startswith("ln_"):
                wl[n] = small_full[n][i:i + 1]
            elif n.startswith(("sc_", "mla_", "cf_")[i % N_MIXERS]):
                v = small_full[n][i // N_MIXERS]
                wl[n] = v if v.ndim == 2 else v.reshape(1, -1)
        return wl

    layers = [small_weights(i) for i in range(DEPTH)]
    groups = [(kind, i) for i in range(DEPTH) for kind in ("mix", "ff")]

    def group_names(g):
        kind, i = g
        if kind == "mix":
            return [(n, i // N_MIXERS) for n in MIXER_BIG[i % N_MIXERS]]
        return [("ff_w1", i), ("ff_w2", i)]

    def shards_of(g):
        return [_shard_2d(n, weights[n][j]).astype(MXU_DTYPE) for n, j in group_names(g)]

    def install(g, gathered):
        for (n, _), a in zip(group_names(g), gathered):
            layers[g[1]][n] = _as_operand(n, a)

    install(groups[0], _all_gather(shards_of(groups[0]), name="gather_g0"))
    gather = {1: _gather_start(shards_of(groups[1]), name="gather_g1_start")}
    cur, cur_m = x[0], _cast_input(x[0])
    saved = {}
    for k, g in enumerate(groups):
        kind, i = g
        tokens = [gather[1]["token"]] if k == 0 else []
        if k + 2 < len(groups):
            gather[k + 2] = _gather_start(shards_of(groups[k + 2]), name=f"gather_g{k + 2}_start")
            tokens.append(gather[k + 2]["token"])

        def pass_on(y, k=k):
            if k + 1 == len(groups):
                return None
            gather[k + 1] = _gather_pass_on(gather[k + 1], name=f"gather_g{k + 1}_pass", after=y)
            return gather[k + 1]["token"]

        phase = _mix_fwd if kind == "mix" else _ff_fwd
        cur, cur_m, saved[g] = phase(i, cur, cur_m, layers[i], hook=pass_on, after=tokens)
        if k + 1 < len(groups):
            install(groups[k + 1], _gather_finish(gather[k + 1], name=f"gather_g{k + 1}_finish", after=cur))

    loss_local, dy = _loss_and_grad(cur, target[0])
    loss = lax.psum(loss_local, MESH_AXES)

    reduced = {n: [None] * PARAMS[n][0][0] for n in BIG}
    small_grads = {n: [None] * PARAMS[n][0][0] for n in SMALL}
    reduce = {}

    def to_chips(p, after):
        st = reduce[p]
        parts, got = _swap_finish(st["sibling"], name=f"reduce_p{p}_sibling_finish", after=after)
        sums = [_chip_partials(a, b, name=f"reduce_p{p}_chipsum{t}") for t, (a, b) in enumerate(zip(parts, got))]
        st["own"] = [own for own, _ in sums]
        others = [o for _, o in sums]
        st["chips"] = _swap_start(others, [lax.empty(o.shape, o.dtype) for o in others], _to_chip, 3,
                                  name=f"reduce_p{p}_chips_start")
        return st["chips"]["token"]

    def arrived(p, after):
        st = reduce[p]
        _, got = _swap_finish(st["chips"], name=f"reduce_p{p}_chips_finish", after=after)
        for (n, j), own, arr in zip(st["names"], st["own"], got):
            reduced[n][j] = (own, arr)

    terms = [(1.0, dy)]
    back = list(reversed(groups))
    for p, g in enumerate(back):
        kind, i = g

        def after_first_kernel(dz, p=p):
            return to_chips(p - 1, dz) if p >= 1 else None

        phase = _mix_bwd if kind == "mix" else _ff_bwd
        tokens = [reduce[p - 1]["sibling"]["token"]] if p >= 1 else []
        terms, grads = phase(i, terms, saved[g], layers[i], hook=after_first_kernel, after=tokens)
        for n in SMALL:
            if n in grads:
                small_grads[n][i if n.startswith("ln_") else i // N_MIXERS] = grads[n]
        if p >= 2:
            arrived(p - 2, terms[-1][1])
        names = group_names(g)
        parts = [_as_parts(n, grads[n]) for n, _ in names]
        reduce[p] = {"names": names, "sibling": _swap_start(
            parts, [lax.empty((4, *a.shape[1:]), a.dtype) for a in parts], _to_sibling, 4,
            name=f"reduce_p{p}_sibling_start")}
    grad_x = _sum_terms(terms, name="grad_x")
    last = len(back) - 1
    arrived(last - 1, grad_x)
    tail = to_chips(last, grad_x)
    arrived(last, tail)

    small_mine = _pack_rows([jnp.stack(small_grads[n]).reshape(PARAMS[n][0]) for n in SMALL], SMALL_WIDTH, SUBLANES, F32)
    small_sum = _sum_blocks(None, _all_gather([small_mine], name="gather_small_grads")[0], name="sum_small_grads")

    grad, delta, new_m, new_v = {}, {}, {}, {}
    for n, g in zip(SMALL, _unpack_rows(small_sum, [PARAMS[n][0] for n in SMALL], SMALL_WIDTH, SUBLANES)):
        ax = PARAMS[n][1]
        grad[n] = g if ax is None else lax.dynamic_index_in_dim(_split_shards(g, ax), me, axis=0, keepdims=False)
        delta[n], new_m[n], new_v[n] = _adamw(weights[n], grad[n], m_state[n], v_state[n], name=f"adamw_{n}")
    for n in BIG:
        shape = weights[n].shape
        view = lambda a: a.reshape(shape[0], shape[1], -1)
        outs = _adamw_reduced(view(weights[n]), view(m_state[n]), view(v_state[n]), reduced[n], name=f"adamw_{n}")
        grad[n], delta[n], new_m[n], new_v[n] = [o.reshape(shape) for o in outs]
    return (loss, grad_x[None], *[grad[n] for n in NAMES], *[delta[n] for n in NAMES],
            *[new_m[n] for n in NAMES], *[new_v[n] for n in NAMES])


def kernel(x, sc_w_in, sc_conv_w, sc_w_out, mla_w_dq, mla_g_q, mla_w_uq, mla_w_dkv, mla_g_kv, mla_w_uk, mla_w_uv, mla_w_o, cf_w_pw1, cf_b_pw1, cf_dw_w, cf_dw_b, cf_norm_g, cf_norm_b, cf_w_pw2, cf_b_pw2, ff_w1, ff_w2, ln_mix_g, ln_mix_b, ln_ff_g, ln_ff_b, loss_target, m_sc_w_in, m_sc_conv_w, m_sc_w_out, m_mla_w_dq, m_mla_g_q, m_mla_w_uq, m_mla_w_dkv, m_mla_g_kv, m_mla_w_uk, m_mla_w_uv, m_mla_w_o, m_cf_w_pw1, m_cf_b_pw1, m_cf_dw_w, m_cf_dw_b, m_cf_norm_g, m_cf_norm_b, m_cf_w_pw2, m_cf_b_pw2, m_ff_w1, m_ff_w2, m_ln_mix_g, m_ln_mix_b, m_ln_ff_g, m_ln_ff_b, v_sc_w_in, v_sc_conv_w, v_sc_w_out, v_mla_w_dq, v_mla_g_q, v_mla_w_uq, v_mla_w_dkv, v_mla_g_kv, v_mla_w_uk, v_mla_w_uv, v_mla_w_o, v_cf_w_pw1, v_cf_b_pw1, v_cf_dw_w, v_cf_dw_b, v_cf_norm_g, v_cf_norm_b, v_cf_w_pw2, v_cf_b_pw2, v_ff_w1, v_ff_w2, v_ln_mix_g, v_ln_mix_b, v_ln_ff_g, v_ln_ff_b):
    w_list = (sc_w_in, sc_conv_w, sc_w_out, mla_w_dq, mla_g_q, mla_w_uq, mla_w_dkv, mla_g_kv, mla_w_uk, mla_w_uv, mla_w_o,
              cf_w_pw1, cf_b_pw1, cf_dw_w, cf_dw_b, cf_norm_g, cf_norm_b, cf_w_pw2, cf_b_pw2, ff_w1, ff_w2, ln_mix_g,
              ln_mix_b, ln_ff_g, ln_ff_b)
    m_list = (m_sc_w_in, m_sc_conv_w, m_sc_w_out, m_mla_w_dq, m_mla_g_q, m_mla_w_uq, m_mla_w_dkv, m_mla_g_kv, m_mla_w_uk,
              m_mla_w_uv, m_mla_w_o, m_cf_w_pw1, m_cf_b_pw1, m_cf_dw_w, m_cf_dw_b, m_cf_norm_g, m_cf_norm_b, m_cf_w_pw2,
              m_cf_b_pw2, m_ff_w1, m_ff_w2, m_ln_mix_g, m_ln_mix_b, m_ln_ff_g, m_ln_ff_b)
    v_list = (v_sc_w_in, v_sc_conv_w, v_sc_w_out, v_mla_w_dq, v_mla_g_q, v_mla_w_uq, v_mla_w_dkv, v_mla_g_kv, v_mla_w_uk,
              v_mla_w_uv, v_mla_w_o, v_cf_w_pw1, v_cf_b_pw1, v_cf_dw_w, v_cf_dw_b, v_cf_norm_g, v_cf_norm_b, v_cf_w_pw2,
              v_cf_b_pw2, v_ff_w1, v_ff_w2, v_ln_mix_g, v_ln_mix_b, v_ln_ff_g, v_ln_ff_b)
    return _train_step(x, loss_target, dict(zip(NAMES, w_list)), dict(zip(NAMES, m_list)), dict(zip(NAMES, v_list)))
```

```python
import functools

import jax
import jax.numpy as jnp
from jax import lax
from jax.experimental import pallas as pl
from jax.experimental.pallas import tpu as pltpu

F32 = jnp.float32
MXU_DTYPE = jnp.bfloat16

N_DEV = 8
D_MODEL = 1024
DEPTH = 4
N_MIXERS = 3
CHUNK = 64
ALPHA = (2.0 * DEPTH) ** 0.25
LN_EPS = 1e-5
RMS_EPS = 1e-6
MLA_HEADS = 8
QK_NOPE = 128
QK_ROPE = 64
V_HEAD = 128
Q_LORA = 384
KV_LORA = 256
ROPE_THETA = 10000.0
ADAM_LR = 0.001
ADAM_B1 = 0.9
ADAM_B2 = 0.999
ADAM_EPS = 1e-08
ADAM_WD = 0.01
ADAM_STEP = 10

LANES = 128
SUBLANES = 8
VMEM_LIMIT = 56 * 1024 * 1024
MM_VMEM_BUDGET = 40 * 1024 * 1024
CONV_PAD = 32
MASK_VALUE = -1e30

MESH_AXES = ("x", "y", "c")


def _params(semantics):
    return pltpu.CompilerParams(dimension_semantics=semantics, vmem_limit_bytes=VMEM_LIMIT)


def _tile(n, cap):
    if n <= cap:
        return n
    t = cap - cap % LANES
    while t >= LANES:
        if n % t == 0:
            return t
        t -= LANES
    raise ValueError(f"no tile for {n} under {cap}")


def _row_tile(rows, cap):
    if rows <= cap:
        return rows
    t = cap - cap % SUBLANES
    while t >= SUBLANES:
        if rows % t == 0:
            return t
        t -= SUBLANES
    raise ValueError(f"no row tile for {rows} under {cap}")


def _mm(a, b, *, name, ta=False, tb=False, b_dev=False, out_dev=False, out_dtypes=(F32,), epilogue=None,
        tile_extras=(), col_extras=(), after=(), tm_cap=1024, tn_cap=1024, tk_cap=4096):
    m, k = (a.shape[1], a.shape[0]) if ta else a.shape
    if b_dev:
        nd, b_rows, ns = b.shape
        n, kb = (b_rows, nd * ns) if tb else (nd * ns, b_rows)
    else:
        n, kb = (b.shape[0], b.shape[1]) if tb else (b.shape[1], b.shape[0])
    assert k == kb, (a.shape, b.shape, ta, tb)
    tk = ns if (b_dev and tb) else _tile(k, tk_cap)
    nk = k // tk
    fixed_tn = ns if (b_dev and not tb) else (n // N_DEV if out_dev else None)
    if out_dev:
        assert not tile_extras and not col_extras and n % N_DEV == 0
    out_bytes = sum(jnp.dtype(dt).itemsize for dt in out_dtypes) + sum(e.dtype.itemsize for e in tile_extras)
    tm, tn = None, None
    for cand_m, cand_n in ((tm_cap, tn_cap), (tm_cap, tn_cap // 2), (tm_cap // 2, tn_cap), (tm_cap // 2, tn_cap // 2),
                           (tm_cap // 4, tn_cap // 2), (tm_cap // 4, tn_cap // 4)):
        tm, tn = _tile(m, cand_m), (fixed_tn if fixed_tn is not None else _tile(n, cand_n))
        need = 2 * (tm * tk * a.dtype.itemsize + tk * tn * b.dtype.itemsize) + 2 * tm * tn * out_bytes
        need += tm * tn * 4 * (2 if nk > 1 else 1)
        if need <= MM_VMEM_BUDGET:
            break
    n_te, n_ce, n_out = len(tile_extras), len(col_extras), len(out_dtypes)
    dims = (((0 if ta else 1,), (1 if tb else 0,)), ((), ()))

    def body(*refs):
        a_ref, b_ref = refs[0], refs[1]
        te_refs = refs[2:2 + n_te]
        ce_refs = refs[2 + n_te:2 + n_te + n_ce]
        first_out = 2 + n_te + n_ce + len(after)
        out_refs = refs[first_out:first_out + n_out]

        def finish(acc):
            if epilogue is None:
                outs = (acc,)
            else:
                outs = epilogue(acc, *[r[...] for r in te_refs], *[r[...] for r in ce_refs])
            for o_ref, o in zip(out_refs, outs):
                o_ref[...] = o.astype(o_ref.dtype)

        def product():
            return lax.dot_general(a_ref[...].astype(MXU_DTYPE), b_ref[...].astype(MXU_DTYPE), dims,
                                   preferred_element_type=F32)

        if nk == 1:
            finish(product())
        else:
            acc_ref = refs[-1]
            kk = pl.program_id(2)

            @pl.when(kk == 0)
            def _():
                acc_ref[...] = jnp.zeros_like(acc_ref)

            acc_ref[...] += product()

            @pl.when(kk == nk - 1)
            def _():
                finish(acc_ref[...])

    a_spec = pl.BlockSpec((tk, tm), lambda i, j, kk: (kk, i)) if ta else pl.BlockSpec((tm, tk), lambda i, j, kk: (i, kk))
    if b_dev and tb:
        b_spec = pl.BlockSpec((None, tn, tk), lambda i, j, kk: (kk, j, 0))
    elif b_dev:
        b_spec = pl.BlockSpec((None, tk, tn), lambda i, j, kk: (j, kk, 0))
    elif tb:
        b_spec = pl.BlockSpec((tn, tk), lambda i, j, kk: (j, kk))
    else:
        b_spec = pl.BlockSpec((tk, tn), lambda i, j, kk: (kk, j))
    tile_spec = pl.BlockSpec((tm, tn), lambda i, j, kk: (i, j))
    col_spec = pl.BlockSpec((1, tn), lambda i, j, kk: (0, j))
    if out_dev:
        out_spec = pl.BlockSpec((None, tm, tn), lambda i, j, kk: (j, i, 0))
        out_shape = [jax.ShapeDtypeStruct((N_DEV, m, tn), dt) for dt in out_dtypes]
    else:
        out_spec = tile_spec
        out_shape = [jax.ShapeDtypeStruct((m, n), dt) for dt in out_dtypes]
    outs = pl.pallas_call(
        body,
        name=name,
        grid=(m // tm, n // tn, nk),
        in_specs=[a_spec, b_spec] + [tile_spec] * n_te + [col_spec] * n_ce + [ANY] * len(after),
        out_specs=[out_spec] * n_out,
        out_shape=out_shape,
        scratch_shapes=[pltpu.VMEM((tm, tn), F32)] if nk > 1 else [],
        compiler_params=_params(("parallel", "parallel", "arbitrary")),
    )(a, b, *tile_extras, *col_extras, *after)
    return outs[0] if n_out == 1 else tuple(outs)


def _rowwise(fn, rows, bcast, out_rows, out_accs, *, name, tb_cap=256, after=()):
    t = rows[0].shape[0]
    tb = _row_tile(t, tb_cap)
    n_r, n_b, n_o, n_a = len(rows), len(bcast), len(out_rows), len(out_accs)

    def body(*refs):
        r_refs = refs[:n_r]
        b_refs = refs[n_r:n_r + n_b]
        first_out = n_r + n_b + len(after)
        o_refs = refs[first_out:first_out + n_o]
        a_refs = refs[first_out + n_o:]
        outs, accs = fn([r[...] for r in r_refs], [r[...] for r in b_refs])
        for o_ref, o in zip(o_refs, outs):
            o_ref[...] = o.astype(o_ref.dtype)
        if n_a:
            i = pl.program_id(0)

            @pl.when(i == 0)
            def _():
                for a_ref, acc in zip(a_refs, accs):
                    a_ref[...] = acc

            @pl.when(i > 0)
            def _():
                for a_ref, acc in zip(a_refs, accs):
                    a_ref[...] += acc

    in_specs = [pl.BlockSpec((tb, r.shape[1]), lambda i: (i, 0)) for r in rows]
    in_specs += [pl.BlockSpec(b.shape, lambda i: (0, 0)) for b in bcast]
    in_specs += [ANY] * len(after)
    out_specs = [pl.BlockSpec((tb, c), lambda i: (i, 0)) for c, _ in out_rows]
    out_specs += [pl.BlockSpec(s, lambda i: (0, 0)) for s in out_accs]
    out_shape = [jax.ShapeDtypeStruct((t, c), dt) for c, dt in out_rows]
    out_shape += [jax.ShapeDtypeStruct(s, F32) for s in out_accs]
    outs = pl.pallas_call(
        body,
        name=name,
        grid=(t // tb,),
        in_specs=in_specs,
        out_specs=out_specs,
        out_shape=out_shape,
        compiler_params=_params(("arbitrary",)),
    )(*rows, *bcast, *after)
    return tuple(outs)


def _colsum(v):
    return jnp.sum(v, axis=0, keepdims=True)


def _sigmoid(v):
    return 1.0 / (1.0 + jnp.exp(-v))


def _ln_fwd(x_res, y_sub, g, b, *, name, after=()):
    d = x_res.shape[1]

    def fn(rows, bc):
        z = ALPHA * rows[0] + rows[1]
        mu = jnp.mean(z, axis=-1, keepdims=True)
        zc = z - mu
        var = jnp.mean(zc * zc, axis=-1, keepdims=True)
        rstd = lax.rsqrt(var + LN_EPS)
        xhat = zc * rstd
        out = xhat * bc[0] + bc[1]
        return (out, out, xhat, rstd), ()

    return _rowwise(fn, [x_res, y_sub], [g, b], [(d, F32), (d, MXU_DTYPE), (d, F32), (1, F32)], [], name=name,
                    after=after)


def _ln_bwd(terms, xhat, rstd, g, *, name, after=()):
    d = xhat.shape[1]
    coefs = [c for c, _ in terms]

    def fn(rows, bc):
        xh, rs = rows[0], rows[1]
        dout = None
        for c, v in zip(coefs, rows[2:]):
            tv = v if c == 1.0 else c * v
            dout = tv if dout is None else dout + tv
        dxh = dout * bc[0]
        m1 = jnp.mean(dxh, axis=-1, keepdims=True)
        m2 = jnp.mean(dxh * xh, axis=-1, keepdims=True)
        dz = rs * (dxh - m1 - xh * m2)
        return (dz, dz), (_colsum(dout * xh), _colsum(dout), _colsum(dz))

    return _rowwise(fn, [xhat, rstd] + [v for _, v in terms], [g], [(d, F32), (d, MXU_DTYPE)],
                    [(1, d), (1, d), (1, d)], name=name, after=after)


def _rms_fwd(x, g, *, name):
    c = x.shape[1]

    def fn(rows, bc):
        v = rows[0]
        r = lax.rsqrt(jnp.mean(v * v, axis=-1, keepdims=True) + RMS_EPS)
        return (v * r * bc[0],), ()

    return _rowwise(fn, [x], [g], [(c, MXU_DTYPE)], [], name=name)[0]


def _rms_bwd(x, g, dy, *, name):
    c = x.shape[1]

    def fn(rows, bc):
        v, dyv = rows
        r = lax.rsqrt(jnp.mean(v * v, axis=-1, keepdims=True) + RMS_EPS)
        xh = v * r
        dxh = dyv * bc[0]
        dx = r * (dxh - xh * jnp.mean(dxh * xh, axis=-1, keepdims=True))
        return (dx,), (_colsum(dyv * xh),)

    return _rowwise(fn, [x, dy], [g], [(c, F32)], [(1, c)], name=name)


def _conv_chunk(t):
    return _row_tile(t, 256)


def _conv_fwd(x, w, bias, *, name):
    t, c = x.shape
    kw = w.shape[0]
    ch = _conv_chunk(t)

    def body(x_ref, w_ref, b_ref, o_ref, pad_ref):
        pad_ref[0:CONV_PAD, :] = jnp.zeros((CONV_PAD, LANES), F32)
        pad_ref[CONV_PAD:CONV_PAD + t, :] = x_ref[...]
        for t0 in range(0, t, ch):
            acc = jnp.broadcast_to(b_ref[...], (ch, LANES))
            for k in range(kw):
                s = kw - 1 - k
                acc = acc + w_ref[k:k + 1, :] * pad_ref[CONV_PAD - s + t0:CONV_PAD - s + t0 + ch, :]
            o_ref[t0:t0 + ch, :] = acc

    return pl.pallas_call(
        body,
        name=name,
        grid=(c // LANES,),
        in_specs=[pl.BlockSpec((t, LANES), lambda j: (0, j)), pl.BlockSpec((kw, LANES), lambda j: (0, j)),
                  pl.BlockSpec((1, LANES), lambda j: (0, j))],
        out_specs=pl.BlockSpec((t, LANES), lambda j: (0, j)),
        out_shape=jax.ShapeDtypeStruct((t, c), F32),
        scratch_shapes=[pltpu.VMEM((CONV_PAD + t, LANES), F32)],
        compiler_params=_params(("parallel",)),
    )(x, w, bias)


def _conv_bwd(dy, x, w, *, name):
    t, c = x.shape
    kw = w.shape[0]
    ch = _conv_chunk(t)

    def body(dy_ref, x_ref, w_ref, dx_ref, dw_ref, xpad_ref, dpad_ref):
        xpad_ref[0:CONV_PAD, :] = jnp.zeros((CONV_PAD, LANES), F32)
        xpad_ref[CONV_PAD:CONV_PAD + t, :] = x_ref[...]
        dpad_ref[0:t, :] = dy_ref[...]
        dpad_ref[t:t + CONV_PAD, :] = jnp.zeros((CONV_PAD, LANES), F32)
        for t0 in range(0, t, ch):
            acc = jnp.zeros((ch, LANES), F32)
            for k in range(kw):
                s = kw - 1 - k
                acc = acc + w_ref[k:k + 1, :] * dpad_ref[t0 + s:t0 + s + ch, :]
            dx_ref[t0:t0 + ch, :] = acc
        for k in range(kw):
            s = kw - 1 - k
            acc = jnp.zeros((ch, LANES), F32)
            for t0 in range(0, t, ch):
                acc = acc + dy_ref[t0:t0 + ch, :] * xpad_ref[CONV_PAD - s + t0:CONV_PAD - s + t0 + ch, :]
            dw_ref[k:k + 1, :] = _colsum(acc)

    blk = pl.BlockSpec((t, LANES), lambda j: (0, j))
    wblk = pl.BlockSpec((kw, LANES), lambda j: (0, j))
    return pl.pallas_call(
        body,
        name=name,
        grid=(c // LANES,),
        in_specs=[blk, blk, wblk],
        out_specs=[blk, wblk],
        out_shape=[jax.ShapeDtypeStruct((t, c), F32), jax.ShapeDtypeStruct((kw, c), F32)],
        scratch_shapes=[pltpu.VMEM((CONV_PAD + t, LANES), F32), pltpu.VMEM((CONV_PAD + t, LANES), F32)],
        compiler_params=_params(("parallel",)),
    )(dy, x, w)


ATTN_SCALE = (QK_NOPE + QK_ROPE) ** -0.5
_NT = (((1,), (1,)), ((), ()))
_TN = (((0,), (0,)), ((), ()))
_NN = (((1,), (0,)), ((), ()))


def _dot(a, b, dims):
    return lax.dot_general(a, b, dims, preferred_element_type=F32)


def _block_scores(qn, qp, kn_ref, kp_ref, lo, hi, diagonal):
    s = _dot(qn, kn_ref[lo:hi, :], _NT) + _dot(qp, kp_ref[lo:hi, :], _NT)
    s = s * ATTN_SCALE
    if diagonal:
        shift = CHUNK.bit_length() - 1
        rows = lax.broadcasted_iota(jnp.int32, s.shape, 0)
        cols = lax.broadcasted_iota(jnp.int32, s.shape, 1)
        s = jnp.where(jnp.right_shift(cols, shift) <= jnp.right_shift(rows, shift), s, MASK_VALUE)
    return s


def _attn_fwd(qn, qp, kn, kp, v, *, name):
    h, t, _ = qp.shape
    tq = _row_tile(t, 256)
    assert tq % CHUNK == 0

    def body(qn_ref, qp_ref, kn_ref, kp_ref, v_ref, o_ref, of_ref, lse_ref):
        for r0 in range(0, t, tq):
            r1 = r0 + tq
            qn, qp = qn_ref[r0:r1, :].astype(MXU_DTYPE), qp_ref[r0:r1, :]
            s_d = _block_scores(qn, qp, kn_ref, kp_ref, r0, r1, True)
            m = jnp.max(s_d, axis=-1, keepdims=True)
            if r0:
                s_b = _block_scores(qn, qp, kn_ref, kp_ref, 0, r0, False)
                m = jnp.maximum(m, jnp.max(s_b, axis=-1, keepdims=True))
            p_d = jnp.exp(s_d - m)
            l = jnp.sum(p_d, axis=-1, keepdims=True)
            acc = _dot(p_d.astype(MXU_DTYPE), v_ref[r0:r1, :], _NN)
            if r0:
                p_b = jnp.exp(s_b - m)
                l = l + jnp.sum(p_b, axis=-1, keepdims=True)
                acc = acc + _dot(p_b.astype(MXU_DTYPE), v_ref[0:r0, :], _NN)
            o = acc / l
            o_ref[r0:r1, :] = o.astype(o_ref.dtype)
            of_ref[r0:r1, :] = o
            lse_ref[r0:r1, :] = m + jnp.log(l)

    blk = lambda w: pl.BlockSpec((None, t, w), lambda hh: (hh, 0, 0))
    cols = pl.BlockSpec((t, QK_NOPE), lambda hh: (0, hh))
    return pl.pallas_call(
        body,
        name=name,
        grid=(h,),
        in_specs=[cols, blk(QK_ROPE), cols, pl.BlockSpec((t, QK_ROPE), lambda hh: (0, 0)), cols],
        out_specs=[cols, cols, blk(1)],
        out_shape=[jax.ShapeDtypeStruct((t, h * V_HEAD), MXU_DTYPE), jax.ShapeDtypeStruct((t, h * V_HEAD), F32),
                   jax.ShapeDtypeStruct((h, t, 1), F32)],
        compiler_params=_params(("parallel",)),
    )(qn, qp, kn, kp, v)


def _attn_bwd(qn, qp, kn, kp, v, do, o, lse, *, name):
    h, t, _ = qp.shape
    tq = _row_tile(t, 256)

    def body(qn_ref, qp_ref, kn_ref, kp_ref, v_ref, do_ref, o_ref, lse_ref, dqn_ref, dqp_ref, dkn_ref, dkp_ref,
             dv_ref):
        dkn_ref[...] = jnp.zeros_like(dkn_ref)
        dv_ref[...] = jnp.zeros_like(dv_ref)

        @pl.when(pl.program_id(0) == 0)
        def _():
            dkp_ref[...] = jnp.zeros_like(dkp_ref)

        for r0 in range(0, t, tq):
            r1 = r0 + tq
            qn, qp = qn_ref[r0:r1, :].astype(MXU_DTYPE), qp_ref[r0:r1, :]
            dov, lse = do_ref[r0:r1, :], lse_ref[r0:r1, :]
            delta = jnp.sum(dov.astype(F32) * o_ref[r0:r1, :], axis=-1, keepdims=True)

            def piece(lo, hi, diagonal):
                s = _block_scores(qn, qp, kn_ref, kp_ref, lo, hi, diagonal)
                p = jnp.exp(s - lse)
                dp = _dot(dov, v_ref[lo:hi, :], _NT)
                ds_m = (p * (dp - delta) * ATTN_SCALE).astype(MXU_DTYPE)
                dkn_ref[lo:hi, :] += _dot(ds_m, qn, _TN)
                dkp_ref[lo:hi, :] += _dot(ds_m, qp, _TN)
                dv_ref[lo:hi, :] += _dot(p.astype(MXU_DTYPE), dov, _TN)
                return _dot(ds_m, kn_ref[lo:hi, :], _NN), _dot(ds_m, kp_ref[lo:hi, :], _NN)

            dqn, dqp = piece(r0, r1, True)
            if r0:
                dqn_b, dqp_b = piece(0, r0, False)
                dqn, dqp = dqn + dqn_b, dqp + dqp_b
            dqn_ref[r0:r1, :] = dqn
            dqp_ref[r0:r1, :] = dqp

    blk = lambda w: pl.BlockSpec((None, t, w), lambda hh: (hh, 0, 0))
    kpblk = pl.BlockSpec((t, QK_ROPE), lambda hh: (0, 0))
    cols = pl.BlockSpec((t, QK_NOPE), lambda hh: (0, hh))
    return pl.pallas_call(
        body,
        name=name,
        grid=(h,),
        in_specs=[cols, blk(QK_ROPE), cols, kpblk, cols, cols, cols, blk(1)],
        out_specs=[cols, blk(QK_ROPE), cols, kpblk, cols],
        out_shape=[jax.ShapeDtypeStruct((t, h * QK_NOPE), F32), jax.ShapeDtypeStruct((h, t, QK_ROPE), F32),
                   jax.ShapeDtypeStruct((t, h * QK_NOPE), F32), jax.ShapeDtypeStruct((t, QK_ROPE), F32),
                   jax.ShapeDtypeStruct((t, h * V_HEAD), F32)],
        compiler_params=_params(("arbitrary",)),
    )(qn, qp, kn, kp, v, do, o, lse)


def _rope(x1, x2, cos, sin, *, name, inverse=False):
    w = x1.shape[1]
    sign = -1.0 if inverse else 1.0

    def fn(rows, bc):
        a, b, c, s = rows
        s = sign * s
        return (a * c - b * s, a * s + b * c), ()

    return _rowwise(fn, [x1, x2, cos, sin], [], [(w, F32), (w, F32)], [], name=name)


def _mlp_fwd(x_m, w1, w2, tag, after=()):
    def epi(acc):
        r = jnp.maximum(acc, 0.0)
        return (r * r,)

    act = _mm(x_m, w1, b_dev=True, name=f"{tag}_up", out_dtypes=(MXU_DTYPE,), epilogue=epi, after=after)
    y = _mm(act, w2, name=f"{tag}_down")
    return y, (x_m, act)


def _mlp_bwd(dy_m, saved, w1, w2, tag, after=()):
    x_m, act = saved
    dw2 = _mm(act, dy_m, ta=True, name=f"{tag}_dw2", out_dtypes=(MXU_DTYPE,))

    def epi(acc, av):
        return (acc * (2.0 * jnp.sqrt(av.astype(F32))),)

    dh = _mm(dy_m, w2, tb=True, name=f"{tag}_dact", out_dtypes=(MXU_DTYPE,), epilogue=epi, tile_extras=(act,),
             after=after)
    dw1 = _mm(x_m, dh, ta=True, out_dev=True, name=f"{tag}_dw1", out_dtypes=(MXU_DTYPE,))
    dx = _mm(dh, w1, tb=True, b_dev=True, name=f"{tag}_dx")
    return [dx], {"ff_w1": dw1, "ff_w2": dw2}


def _sc_fwd(x_m, w_in, conv_w, w_out, tag, after=()):
    d = D_MODEL
    u = _mm(x_m, w_in, b_dev=True, name=f"{tag}_in", after=after)

    def gate_fn(rows, bc):
        uv = rows[0]
        return (uv[:, d:2 * d] * uv[:, 2 * d:3 * d],), ()

    p = _rowwise(gate_fn, [u], [], [(d, F32)], [], name=f"{tag}_ch")[0]
    q = _conv_fwd(p, conv_w, jnp.zeros((1, d), F32), name=f"{tag}_conv")

    def out_fn(rows, bc):
        return (rows[0][:, 0:d] * rows[1],), ()

    r = _rowwise(out_fn, [u, q], [], [(d, MXU_DTYPE)], [], name=f"{tag}_bq")[0]
    y = _mm(r, w_out, name=f"{tag}_out")
    return y, (x_m, u, p, q, r)


def _sc_bwd(dy_m, saved, w_in, conv_w, w_out, tag, after=()):
    d = D_MODEL
    x_m, u, p, q, r = saved
    dw_out = _mm(r, dy_m, ta=True, name=f"{tag}_dwout", out_dtypes=(MXU_DTYPE,))
    dr = _mm(dy_m, w_out, tb=True, name=f"{tag}_dr", after=after)

    def dq_fn(rows, bc):
        return (rows[0] * rows[1][:, 0:d],), ()

    dq = _rowwise(dq_fn, [dr, u], [], [(d, F32)], [], name=f"{tag}_dq")[0]
    dp, dconv = _conv_bwd(dq, p, conv_w, name=f"{tag}_dconv")

    def du_fn(rows, bc):
        drv, qv, dpv, uv = rows
        return (jnp.concatenate([drv * qv, dpv * uv[:, 2 * d:3 * d], dpv * uv[:, d:2 * d]], axis=1),), ()

    du = _rowwise(du_fn, [dr, q, dp, u], [], [(3 * d, MXU_DTYPE)], [], name=f"{tag}_du")[0]
    dw_in = _mm(x_m, du, ta=True, out_dev=True, name=f"{tag}_dwin", out_dtypes=(MXU_DTYPE,))
    dx = _mm(du, w_in, tb=True, b_dev=True, name=f"{tag}_dx")
    return [dx], {"sc_w_in": dw_in, "sc_conv_w": dconv, "sc_w_out": dw_out}


def _cf_fwd(x_m, w, tag, after=()):
    d = D_MODEL

    def bias_epi(acc, bias):
        return (acc + bias,)

    u = _mm(x_m, w["cf_w_pw1"], b_dev=True, name=f"{tag}_pw1", epilogue=bias_epi, col_extras=(w["cf_b_pw1"],),
            after=after)

    def glu_fn(rows, bc):
        uv = rows[0]
        return (uv[:, 0:d] * _sigmoid(uv[:, d:2 * d]),), ()

    h1 = _rowwise(glu_fn, [u], [], [(d, F32)], [], name=f"{tag}_glu")[0]
    h2 = _conv_fwd(h1, w["cf_dw_w"], w["cf_dw_b"], name=f"{tag}_conv")

    def norm_fn(rows, bc):
        hv = rows[0]
        mu = jnp.mean(hv, axis=-1, keepdims=True)
        hc = hv - mu
        rstd = lax.rsqrt(jnp.mean(hc * hc, axis=-1, keepdims=True) + LN_EPS)
        h3 = hc * rstd * bc[0] + bc[1]
        return (h3 * _sigmoid(h3),), ()

    h4 = _rowwise(norm_fn, [h2], [w["cf_norm_g"], w["cf_norm_b"]], [(d, MXU_DTYPE)], [], name=f"{tag}_norm")[0]
    y = _mm(h4, w["cf_w_pw2"], name=f"{tag}_pw2", epilogue=bias_epi, col_extras=(w["cf_b_pw2"],))
    return y, (x_m, u, h1, h2, h4)


def _cf_bwd(dy_m, dy_colsum, saved, w, tag, after=()):
    d = D_MODEL
    x_m, u, h1, h2, h4 = saved
    dw_pw2 = _mm(h4, dy_m, ta=True, name=f"{tag}_dwpw2", out_dtypes=(MXU_DTYPE,))
    dh4 = _mm(dy_m, w["cf_w_pw2"], tb=True, name=f"{tag}_dh4", after=after)

    def dnorm_fn(rows, bc):
        dh4v, hv = rows
        g, b = bc
        mu = jnp.mean(hv, axis=-1, keepdims=True)
        hc = hv - mu
        rstd = lax.rsqrt(jnp.mean(hc * hc, axis=-1, keepdims=True) + LN_EPS)
        xh = hc * rstd
        h3 = xh * g + b
        sg = _sigmoid(h3)
        dh3 = dh4v * (sg * (1.0 + h3 * (1.0 - sg)))
        dxh = dh3 * g
        dh2 = rstd * (dxh - jnp.mean(dxh, axis=-1, keepdims=True) - xh * jnp.mean(dxh * xh, axis=-1, keepdims=True))
        return (dh2,), (_colsum(dh3 * xh), _colsum(dh3), _colsum(dh2))

    dh2, dnorm_g, dnorm_b, ddw_b = _rowwise(dnorm_fn, [dh4, h2], [w["cf_norm_g"], w["cf_norm_b"]], [(d, F32)],
                                            [(1, d), (1, d), (1, d)], name=f"{tag}_dnorm")
    dh1, ddw_w = _conv_bwd(dh2, h1, w["cf_dw_w"], name=f"{tag}_dconv")

    def dglu_fn(rows, bc):
        dh1v, uv = rows
        a, gate = uv[:, 0:d], uv[:, d:2 * d]
        sg = _sigmoid(gate)
        du = jnp.concatenate([dh1v * sg, dh1v * a * sg * (1.0 - sg)], axis=1)
        return (du,), (_colsum(du),)

    du, db_pw1 = _rowwise(dglu_fn, [dh1, u], [], [(2 * d, MXU_DTYPE)], [(1, 2 * d)], name=f"{tag}_dglu")
    dw_pw1 = _mm(x_m, du, ta=True, out_dev=True, name=f"{tag}_dwpw1", out_dtypes=(MXU_DTYPE,))
    dx = _mm(du, w["cf_w_pw1"], tb=True, b_dev=True, name=f"{tag}_dx")
    grads = {"cf_w_pw1": dw_pw1, "cf_b_pw1": db_pw1, "cf_dw_w": ddw_w, "cf_dw_b": ddw_b, "cf_norm_g": dnorm_g,
             "cf_norm_b": dnorm_b, "cf_w_pw2": dw_pw2, "cf_b_pw2": dy_colsum}
    return [dx], grads


def _rope_tables(t):
    pos = jnp.arange(t, dtype=F32)
    inv_freq = ROPE_THETA ** (-jnp.arange(0, QK_ROPE, 2, dtype=F32) / QK_ROPE)
    ang = pos[:, None] * inv_freq[None, :]
    return jnp.cos(ang), jnp.sin(ang)


ROPE_HALF = QK_ROPE // 2
Q_NOPE_COLS = MLA_HEADS * QK_NOPE
Q_HALF_COLS = MLA_HEADS * ROPE_HALF


def _mla_fwd(x_m, w, tag, after=()):
    t = x_m.shape[0]
    cos, sin = _rope_tables(t)
    cos_h, sin_h = jnp.tile(cos, (1, MLA_HEADS)), jnp.tile(sin, (1, MLA_HEADS))
    cq_pre = _mm(x_m, w["mla_w_dq"], name=f"{tag}_dq", after=after)
    cq = _rms_fwd(cq_pre, w["mla_g_q"], name=f"{tag}_qnorm")
    q = _mm(cq, w["mla_w_uq"], name=f"{tag}_uq")
    r1, r2 = _rope(q[:, Q_NOPE_COLS:Q_NOPE_COLS + Q_HALF_COLS], q[:, Q_NOPE_COLS + Q_HALF_COLS:], cos_h, sin_h,
                   name=f"{tag}_qrope")
    qp_h = jnp.concatenate([r1.reshape(t, MLA_HEADS, ROPE_HALF), r2.reshape(t, MLA_HEADS, ROPE_HALF)],
                           axis=-1).transpose(1, 0, 2).astype(MXU_DTYPE)
    ckv_full = _mm(x_m, w["mla_w_dkv"], name=f"{tag}_dkv")
    ckv_pre = ckv_full[:, :KV_LORA]
    ckv = _rms_fwd(ckv_pre, w["mla_g_kv"], name=f"{tag}_kvnorm")
    k1, k2 = _rope(ckv_full[:, KV_LORA:KV_LORA + ROPE_HALF], ckv_full[:, KV_LORA + ROPE_HALF:], cos, sin,
                   name=f"{tag}_krope")
    kp = jnp.concatenate([k1, k2], axis=-1).astype(MXU_DTYPE)
    kn = _mm(ckv, w["mla_w_uk"], name=f"{tag}_uk", out_dtypes=(MXU_DTYPE,))
    v = _mm(ckv, w["mla_w_uv"], name=f"{tag}_uv", out_dtypes=(MXU_DTYPE,))
    o, o_f, lse = _attn_fwd(q, qp_h, kn, kp, v, name=f"{tag}_attn")
    y = _mm(o, w["mla_w_o"], name=f"{tag}_o")
    return y, (x_m, cq_pre, cq, ckv_pre, ckv, q, qp_h, kn, kp, v, lse, o, o_f, (cos, sin, cos_h, sin_h))


def _mla_bwd(dy_m, saved, w, tag, after=()):
    x_m, cq_pre, cq, ckv_pre, ckv, q, qp_h, kn, kp, v, lse, o, o_f, (cos, sin, cos_h, sin_h) = saved
    t = x_m.shape[0]
    half = ROPE_HALF
    dw_o = _mm(o, dy_m, ta=True, name=f"{tag}_dwo", out_dtypes=(MXU_DTYPE,))
    do = _mm(dy_m, w["mla_w_o"], tb=True, name=f"{tag}_do", out_dtypes=(MXU_DTYPE,), after=after)
    dqn, dqp_h, dkn, dkp, dv = _attn_bwd(q, qp_h, kn, kp, v, do, o_f, lse, name=f"{tag}_dattn")
    dqp = dqp_h.transpose(1, 0, 2)
    d1, d2 = _rope(dqp[:, :, :half].reshape(t, -1), dqp[:, :, half:].reshape(t, -1), cos_h, sin_h,
                   name=f"{tag}_dqrope", inverse=True)
    dq = jnp.concatenate([dqn, d1, d2], axis=-1).astype(MXU_DTYPE)
    dw_uq = _mm(cq, dq, ta=True, name=f"{tag}_dwuq", out_dtypes=(MXU_DTYPE,))
    dcq = _mm(dq, w["mla_w_uq"], tb=True, name=f"{tag}_dcq")
    dcq_pre, dg_q = _rms_bwd(cq_pre, w["mla_g_q"], dcq, name=f"{tag}_dqnorm")
    dcq_pre_m = dcq_pre.astype(MXU_DTYPE)
    dw_dq = _mm(x_m, dcq_pre_m, ta=True, name=f"{tag}_dwdq", out_dtypes=(MXU_DTYPE,))
    dx_q = _mm(dcq_pre_m, w["mla_w_dq"], tb=True, name=f"{tag}_dxq")
    dw_uk = _mm(ckv, dkn, ta=True, name=f"{tag}_dwuk", out_dtypes=(MXU_DTYPE,))
    dw_uv = _mm(ckv, dv, ta=True, name=f"{tag}_dwuv", out_dtypes=(MXU_DTYPE,))
    dckv_k = _mm(dkn, w["mla_w_uk"], tb=True, name=f"{tag}_dckvk")
    dckv = _mm(dv, w["mla_w_uv"], tb=True, name=f"{tag}_dckv", epilogue=lambda acc, other: (acc + other,),
               tile_extras=(dckv_k,))
    dckv_pre, dg_kv = _rms_bwd(ckv_pre, w["mla_g_kv"], dckv, name=f"{tag}_dkvnorm")
    dk1, dk2 = _rope(dkp[:, :half], dkp[:, half:], cos, sin, name=f"{tag}_dkrope", inverse=True)
    dckv_full = jnp.concatenate([dckv_pre, dk1, dk2], axis=-1).astype(MXU_DTYPE)
    dw_dkv = _mm(x_m, dckv_full, ta=True, name=f"{tag}_dwdkv", out_dtypes=(MXU_DTYPE,))
    dx_kv = _mm(dckv_full, w["mla_w_dkv"], tb=True, name=f"{tag}_dxkv")
    grads = {"mla_w_dq": dw_dq, "mla_g_q": dg_q, "mla_w_uq": dw_uq, "mla_w_dkv": dw_dkv, "mla_g_kv": dg_kv,
             "mla_w_uk": dw_uk, "mla_w_uv": dw_uv, "mla_w_o": dw_o}
    return [dx_q, dx_kv], grads


def _cast_input(x):
    cast = lambda rows, bc: ((rows[0],), ())
    return _rowwise(cast, [x], [], [(D_MODEL, MXU_DTYPE)], [], name="x_cast")[0]


def _hooked(value, hook):
    token = None if hook is None else hook(value)
    return () if token is None else (token,)


def _mix_fwd(i, cur, cur_m, wl, hook=None, after=()):
    mixer = i % N_MIXERS
    if mixer == 0:
        y, s_mix = _sc_fwd(cur_m, wl["sc_w_in"], wl["sc_conv_w"], wl["sc_w_out"], f"l{i}_sc", after=after)
    elif mixer == 1:
        y, s_mix = _mla_fwd(cur_m, wl, f"l{i}_mla", after=after)
    else:
        y, s_mix = _cf_fwd(cur_m, wl, f"l{i}_cf", after=after)
    cur, cur_m, xh_mix, rs_mix = _ln_fwd(cur, y, wl["ln_mix_g"], wl["ln_mix_b"], name=f"l{i}_lnmix",
                                         after=_hooked(y, hook))
    return cur, cur_m, (s_mix, xh_mix, rs_mix)


def _ff_fwd(i, cur, cur_m, wl, hook=None, after=()):
    y, s_ff = _mlp_fwd(cur_m, wl["ff_w1"], wl["ff_w2"], f"l{i}_ff", after=after)
    cur, cur_m, xh_ff, rs_ff = _ln_fwd(cur, y, wl["ln_ff_g"], wl["ln_ff_b"], name=f"l{i}_lnff", after=_hooked(y, hook))
    return cur, cur_m, (s_ff, xh_ff, rs_ff)


def _layer_fwd(i, cur, cur_m, wl):
    cur, cur_m, s_mix = _mix_fwd(i, cur, cur_m, wl)
    cur, cur_m, s_ff = _ff_fwd(i, cur, cur_m, wl)
    return cur, cur_m, (*s_mix, *s_ff)


def _loss_and_grad(out, target):
    d = D_MODEL

    def loss_fn(rows, bc):
        err = rows[0] - rows[1]
        return (err * (1.0 / d),), (_colsum(err * err),)

    dy, sq = _rowwise(loss_fn, [out, target], [], [(d, F32)], [(1, d)], name="loss")
    return (0.5 / d) * jnp.sum(sq), dy


def _ff_bwd(i, terms, saved, wl, hook=None, after=()):
    s_ff, xh_ff, rs_ff = saved
    grads = {}
    dz, dz_m, grads["ln_ff_g"], grads["ln_ff_b"], _ = _ln_bwd(terms, xh_ff, rs_ff, wl["ln_ff_g"], name=f"l{i}_dlnff",
                                                              after=after)
    dxs, g_ff = _mlp_bwd(dz_m, s_ff, wl["ff_w1"], wl["ff_w2"], f"l{i}_ff", after=_hooked(dz, hook))
    grads.update(g_ff)
    return [(ALPHA, dz)] + [(1.0, v) for v in dxs], grads


def _mix_bwd(i, terms, saved, wl, hook=None, after=()):
    mixer = i % N_MIXERS
    s_mix, xh_mix, rs_mix = saved
    grads = {}
    dz, dz_m, grads["ln_mix_g"], grads["ln_mix_b"], dz_sum = _ln_bwd(terms, xh_mix, rs_mix, wl["ln_mix_g"],
                                                                     name=f"l{i}_dlnmix", after=after)
    behind = _hooked(dz, hook)
    if mixer == 0:
        dxs, g_mix = _sc_bwd(dz_m, s_mix, wl["sc_w_in"], wl["sc_conv_w"], wl["sc_w_out"], f"l{i}_sc", after=behind)
    elif mixer == 1:
        dxs, g_mix = _mla_bwd(dz_m, s_mix, wl, f"l{i}_mla", after=behind)
    else:
        dxs, g_mix = _cf_bwd(dz_m, dz_sum, s_mix, wl, f"l{i}_cf", after=behind)
    grads.update(g_mix)
    return [(ALPHA, dz)] + [(1.0, v) for v in dxs], grads


def _layer_bwd(i, terms, saved, wl):
    terms, g_ff = _ff_bwd(i, terms, saved[3:], wl)
    terms, g_mix = _mix_bwd(i, terms, saved[:3], wl)
    return terms, {**g_ff, **g_mix}


def _sum_terms(terms, *, name):
    coefs = [c for c, _ in terms]

    def sum_fn(rows, bc):
        acc = None
        for c, v in zip(coefs, rows):
            tv = v if c == 1.0 else c * v
            acc = tv if acc is None else acc + tv
        return (acc,), ()

    return _rowwise(sum_fn, [v for _, v in terms], [], [(D_MODEL, F32)], [], name=name)[0]


def _local_step(x, target, layers):
    cur, cur_m = x, _cast_input(x)
    saved = []
    for i in range(DEPTH):
        cur, cur_m, s = _layer_fwd(i, cur, cur_m, layers[i])
        saved.append(s)
    loss, dy = _loss_and_grad(cur, target)
    terms = [(1.0, dy)]
    grads = [None] * DEPTH
    for i in reversed(range(DEPTH)):
        terms, grads[i] = _layer_bwd(i, terms, saved[i], layers[i])
    return loss, _sum_terms(terms, name="grad_x"), grads


MESH_ID = pl.DeviceIdType.MESH
ANY = pl.BlockSpec(memory_space=pl.ANY)


def _block_id(px, py, pc):
    return 4 * px + 2 * py + pc


COPIES_PER_GATHER = 7


def _all_gather(shards, *, name):
    n = len(shards)

    def body(*refs):
        x_refs, out_refs = refs[:n], refs[n:2 * n]
        send_sems, recv_sems, local_sems = refs[2 * n:]
        x, y, core = lax.axis_index("x"), lax.axis_index("y"), lax.axis_index("c")
        me, sibling = (x, y, core), (x, y, 1 - core)
        chips = [(1 - x, y), (x, 1 - y), (1 - x, 1 - y)]

        def copy(t, k, block, to, from_input=False):
            dst = out_refs[t].at[_block_id(*block)]
            return pltpu.make_async_remote_copy(
                src_ref=x_refs[t] if from_input else dst, dst_ref=dst,
                send_sem=send_sems.at[t * COPIES_PER_GATHER + k], recv_sem=recv_sems.at[t * COPIES_PER_GATHER + k],
                device_id=to, device_id_type=MESH_ID)

        started = []
        mine = [pltpu.make_async_copy(x_refs[t], out_refs[t].at[_block_id(*me)], local_sems.at[t]) for t in range(n)]
        for t in range(n):
            mine[t].start()
            first = [copy(t, 0, me, sibling, from_input=True)]
            first += [copy(t, 1 + j, me, (*chip, core), from_input=True) for j, chip in enumerate(chips)]
            for cp in first:
                cp.start()
            started += first
        for t in range(n):
            for j, chip in enumerate(chips):
                copy(t, 1 + j, (*chip, core), me).wait_recv()
                passed = copy(t, 4 + j, (*chip, core), sibling)
                passed.start()
                started.append(passed)
        for t in range(n):
            copy(t, 0, sibling, me).wait_recv()
            for j, chip in enumerate(chips):
                copy(t, 4 + j, (*chip, 1 - core), me).wait_recv()
        for cp in started:
            cp.wait_send()
        for cp in mine:
            cp.wait()

    n_sems = n * COPIES_PER_GATHER
    outs = pl.pallas_call(
        body,
        name=name,
        out_shape=[jax.ShapeDtypeStruct((N_DEV, *s.shape), s.dtype) for s in shards],
        in_specs=[ANY] * n,
        out_specs=[ANY] * n,
        scratch_shapes=[pltpu.SemaphoreType.DMA((n_sems,)), pltpu.SemaphoreType.DMA((n_sems,)),
                        pltpu.SemaphoreType.DMA((n,))],
    )(*shards)
    return list(outs)


def _swap_with_sibling(parts, *, name):
    n = len(parts)

    def body(*refs):
        p_refs, got_refs = refs[:n], refs[n:2 * n]
        send_sems, recv_sems = refs[2 * n:]
        x, y, core = lax.axis_index("x"), lax.axis_index("y"), lax.axis_index("c")
        chips = [(x, y), (1 - x, y), (x, 1 - y), (1 - x, 1 - y)]
        copies = []
        for t in range(n):
            for k, (px, py) in enumerate(chips):
                copies.append(pltpu.make_async_remote_copy(
                    src_ref=p_refs[t].at[_block_id(px, py, 1 - core)], dst_ref=got_refs[t].at[k],
                    send_sem=send_sems.at[4 * t + k], recv_sem=recv_sems.at[4 * t + k], device_id=(x, y, 1 - core),
                    device_id_type=MESH_ID))
        for cp in copies:
            cp.start()
        for cp in copies:
            cp.wait()

    outs = pl.pallas_call(
        body,
        name=name,
        out_shape=[jax.ShapeDtypeStruct((4, *p.shape[1:]), p.dtype) for p in parts],
        in_specs=[ANY] * n,
        out_specs=[ANY] * n,
        scratch_shapes=[pltpu.SemaphoreType.DMA((4 * n,)), pltpu.SemaphoreType.DMA((4 * n,))],
    )(*parts)
    return list(outs)


def _swap_between_chips(parts, *, name):
    n = len(parts)

    def body(*refs):
        p_refs, got_refs = refs[:n], refs[n:2 * n]
        send_sems, recv_sems = refs[2 * n:]
        x, y, core = lax.axis_index("x"), lax.axis_index("y"), lax.axis_index("c")
        chips = [(1 - x, y), (x, 1 - y), (1 - x, 1 - y)]
        copies = []
        for t in range(n):
            for j, (px, py) in enumerate(chips):
                copies.append(pltpu.make_async_remote_copy(
                    src_ref=p_refs[t].at[j], dst_ref=got_refs[t].at[j], send_sem=send_sems.at[3 * t + j],
                    recv_sem=recv_sems.at[3 * t + j], device_id=(px, py, core), device_id_type=MESH_ID))
        for cp in copies:
            cp.start()
        for cp in copies:
            cp.wait()

    outs = pl.pallas_call(
        body,
        name=name,
        out_shape=[jax.ShapeDtypeStruct(p.shape, p.dtype) for p in parts],
        in_specs=[ANY] * n,
        out_specs=[ANY] * n,
        scratch_shapes=[pltpu.SemaphoreType.DMA((3 * n,)), pltpu.SemaphoreType.DMA((3 * n,))],
    )(*parts)
    return list(outs)


def _chip_partials(parts, got, *, name):
    _, r, c = parts.shape
    tb = _row_tile(r, 512)
    x, y, core = lax.axis_index("x"), lax.axis_index("y"), lax.axis_index("c")
    chips = [(x, y), (1 - x, y), (x, 1 - y), (1 - x, 1 - y)]
    ids = jnp.stack([_block_id(px, py, core) for px, py in chips]).astype(jnp.int32)

    def body(ids_ref, p0, p1, p2, p3, g_ref, own_ref, out_ref):
        own_ref[...] = p0[...].astype(F32) + g_ref[0].astype(F32)
        for j, p in enumerate((p1, p2, p3)):
            out_ref[j] = (p[...].astype(F32) + g_ref[j + 1].astype(F32)).astype(out_ref.dtype)

    def part_spec(k):
        return pl.BlockSpec((None, tb, c), lambda i, ids_ref: (ids_ref[k], i, 0))

    grid_spec = pltpu.PrefetchScalarGridSpec(
        num_scalar_prefetch=1,
        grid=(r // tb,),
        in_specs=[part_spec(k) for k in range(4)] + [pl.BlockSpec((4, tb, c), lambda i, ids_ref: (0, i, 0))],
        out_specs=[pl.BlockSpec((tb, c), lambda i, ids_ref: (i, 0)), pl.BlockSpec((3, tb, c), lambda i, ids_ref: (0, i, 0))],
    )
    return pl.pallas_call(
        body,
        name=name,
        grid_spec=grid_spec,
        out_shape=[jax.ShapeDtypeStruct((r, c), F32), jax.ShapeDtypeStruct((3, r, c), parts.dtype)],
        compiler_params=_params(("parallel",)),
    )(ids, parts, parts, parts, parts, got)


def _sum_blocks(first, blocks, *, name):
    n, r, c = blocks.shape
    tb = _row_tile(r, 512)

    def body(*refs):
        b_ref, o_ref = refs[-2], refs[-1]
        acc = refs[0][...] if first is not None else b_ref[0].astype(F32)
        for j in range(0 if first is not None else 1, n):
            acc = acc + b_ref[j].astype(F32)
        o_ref[...] = acc

    row_spec = pl.BlockSpec((tb, c), lambda i: (i, 0))
    return pl.pallas_call(
        body,
        name=name,
        grid=(r // tb,),
        in_specs=([row_spec] if first is not None else []) + [pl.BlockSpec((n, tb, c), lambda i: (0, i, 0))],
        out_specs=row_spec,
        out_shape=jax.ShapeDtypeStruct((r, c), F32),
        compiler_params=_params(("parallel",)),
    )(*([first] if first is not None else []), blocks)


def _reduce_scatter(parts, *, tag):
    got = _swap_with_sibling(parts, name=f"{tag}_sibling")
    sums = [_chip_partials(p, g, name=f"{tag}_chipsum{t}") for t, (p, g) in enumerate(zip(parts, got))]
    arrived = _swap_between_chips([others for _, others in sums], name=f"{tag}_chips")
    return [(own, arr) for (own, _), arr in zip(sums, arrived)]


HBM = pl.BlockSpec(memory_space=pltpu.HBM)
SEM = pl.BlockSpec(memory_space=pltpu.SEMAPHORE)
DATAFLOW = pltpu.SideEffectType.DATAFLOW_SIDE_EFFECTING


def _tie(value, *before):
    return lax.optimization_barrier((value, *before))[0]


def _split_call(name, arrays, sems, n_new, body, after=None):
    n_a, n_s = len(arrays), len(sems)
    n_after = 0 if after is None else 1

    def kernel_body(*refs):
        new = refs[n_a + n_s + n_after:n_a + n_s + n_after + 2] if n_new else (None, None)
        body(refs[:n_a], refs[n_a:n_a + n_s], *new)
        refs[-1][...] = jnp.zeros_like(refs[-1])

    n_sem_out = 2 if n_new else 0
    out_shape = [pltpu.SemaphoreType.DMA((n_new,))] * n_sem_out
    out_shape += [pltpu.HBM(a.shape, a.dtype) for a in arrays]
    out_shape += [jax.ShapeDtypeStruct((SUBLANES, LANES), F32)]
    outs = pl.pallas_call(
        kernel_body,
        name=name,
        out_shape=out_shape,
        in_specs=[HBM] * n_a + [SEM] * n_s + [ANY] * n_after,
        out_specs=[SEM] * n_sem_out + [HBM] * n_a + [pl.BlockSpec(memory_space=pltpu.VMEM)],
        input_output_aliases={i: n_sem_out + i for i in range(n_a)},
        compiler_params=pltpu.CompilerParams(has_side_effects=DATAFLOW),
    )(*[pltpu.with_memory_space_constraint(a, pltpu.HBM) for a in arrays], *sems, *([after] if n_after else []))
    return tuple(outs[:n_sem_out]), list(outs[n_sem_out:n_sem_out + n_a]), outs[-1]


def _remote(src, dst, send, recv, k, to):
    return pltpu.make_async_remote_copy(src_ref=src, dst_ref=dst, send_sem=send.at[k], recv_sem=recv.at[k],
                                        device_id=to, device_id_type=MESH_ID)


def _gather_start(shards, *, name):
    n = len(shards)
    me = _block_id(lax.axis_index("x"), lax.axis_index("y"), lax.axis_index("c"))
    lands = [lax.dynamic_update_index_in_dim(lax.empty((N_DEV, *s.shape), s.dtype), s, me, 0) for s in shards]

    def body(refs, _, send, recv):
        x, y, core = lax.axis_index("x"), lax.axis_index("y"), lax.axis_index("c")
        targets = [(x, y, 1 - core), (1 - x, y, core), (x, 1 - y, core), (1 - x, 1 - y, core)]
        for t in range(n):
            for k, to in enumerate(targets):
                _remote(refs[t], refs[n + t].at[_block_id(x, y, core)], send, recv, 4 * t + k, to).start()

    sems, thru, token = _split_call(name, shards + lands, [], 4 * n, body)
    return {"n": n, "sems1": sems, "arrays": thru, "token": token}


def _gather_pass_on(st, *, name, after):
    n = st["n"]

    def body(refs, sems, send, recv):
        send1, recv1 = sems
        x, y, core = lax.axis_index("x"), lax.axis_index("y"), lax.axis_index("c")
        chips = [(1 - x, y), (x, 1 - y), (1 - x, 1 - y)]
        for t in range(n):
            for j, (px, py) in enumerate(chips):
                block = refs[n + t].at[_block_id(px, py, core)]
                _remote(refs[t], block, send1, recv1, 4 * t + 1 + j, (x, y, 1 - core)).wait_recv()
                _remote(block, block, send, recv, 3 * t + j, (x, y, 1 - core)).start()

    sems2, thru, token = _split_call(name, st["arrays"], list(st["sems1"]), 3 * n, body, after=after)
    return {**st, "sems2": sems2, "arrays": thru, "token": token}


def _gather_finish(st, *, name, after):
    n = st["n"]

    def body(refs, sems, *_):
        send1, recv1, send2, recv2 = sems
        x, y, core = lax.axis_index("x"), lax.axis_index("y"), lax.axis_index("c")
        sibling = (x, y, 1 - core)
        chips = [(1 - x, y), (x, 1 - y), (1 - x, 1 - y)]
        for t in range(n):
            land = refs[n + t]
            for k in range(4):
                _remote(refs[t], land.at[_block_id(x, y, core)], send1, recv1, 4 * t + k, sibling).wait_send()
            _remote(refs[t], land.at[_block_id(x, y, 1 - core)], send1, recv1, 4 * t, sibling).wait_recv()
            for j, (px, py) in enumerate(chips):
                _remote(land.at[_block_id(px, py, core)], land.at[_block_id(px, py, core)], send2, recv2, 3 * t + j,
                        sibling).wait_send()
                _remote(refs[t], land.at[_block_id(px, py, 1 - core)], send2, recv2, 3 * t + j, sibling).wait_recv()

    _, thru, _ = _split_call(name, st["arrays"], [*st["sems1"], *st["sems2"]], 0, body, after=after)
    return thru[n:]


def _swap_start(sources, lands, plan, per_array, *, name):
    n = len(sources)

    def body(refs, _, send, recv):
        x, y, core = lax.axis_index("x"), lax.axis_index("y"), lax.axis_index("c")
        for t in range(n):
            for k in range(per_array):
                block, to = plan(x, y, core, k)
                _remote(refs[t].at[block], refs[n + t].at[k], send, recv, per_array * t + k, to).start()

    sems, thru, token = _split_call(name, sources + lands, [], per_array * n, body)
    return {"n": n, "sems": sems, "arrays": thru, "token": token, "plan": plan, "per_array": per_array}


def _swap_finish(st, *, name, after):
    n, plan, per_array = st["n"], st["plan"], st["per_array"]

    def body(refs, sems, *_):
        send, recv = sems
        x, y, core = lax.axis_index("x"), lax.axis_index("y"), lax.axis_index("c")
        for t in range(n):
            for k in range(per_array):
                block, to = plan(x, y, core, k)
                cp = _remote(refs[t].at[block], refs[n + t].at[k], send, recv, per_array * t + k, to)
                cp.wait_send()
                cp.wait_recv()

    _, thru, _ = _split_call(name, st["arrays"], list(st["sems"]), 0, body, after=after)
    return thru[:n], thru[n:]


def _to_sibling(x, y, core, k):
    px, py = [(x, y), (1 - x, y), (x, 1 - y), (1 - x, 1 - y)][k]
    return _block_id(px, py, 1 - core), (x, y, 1 - core)


def _to_chip(x, y, core, k):
    px, py = [(1 - x, y), (x, 1 - y), (1 - x, 1 - y)][k]
    return k, (px, py, core)


PARAMS = {
    "sc_w_in": ((2, 1024, 3072), 2), "sc_conv_w": ((2, 3, 1024), 2), "sc_w_out": ((2, 1024, 1024), 1),
    "mla_w_dq": ((1, 1024, 384), 1), "mla_g_q": ((1, 384), None), "mla_w_uq": ((1, 384, 1536), 2),
    "mla_w_dkv": ((1, 1024, 320), 1), "mla_g_kv": ((1, 256), None), "mla_w_uk": ((1, 256, 8, 128), 1),
    "mla_w_uv": ((1, 256, 8, 128), 1), "mla_w_o": ((1, 1024, 1024), 1), "cf_w_pw1": ((1, 1024, 2048), 2),
    "cf_b_pw1": ((1, 2048), 1), "cf_dw_w": ((1, 31, 1024), 2), "cf_dw_b": ((1, 1024), 1), "cf_norm_g": ((1, 1024), 1),
    "cf_norm_b": ((1, 1024), 1), "cf_w_pw2": ((1, 1024, 1024), 1), "cf_b_pw2": ((1, 1024), 1),
    "ff_w1": ((4, 1024, 4096), 2), "ff_w2": ((4, 4096, 1024), 1), "ln_mix_g": ((4, 1024), None),
    "ln_mix_b": ((4, 1024), None), "ln_ff_g": ((4, 1024), None), "ln_ff_b": ((4, 1024), None),
}
NAMES = list(PARAMS)
BIG = ["sc_w_in", "sc_w_out", "mla_w_dq", "mla_w_uq", "mla_w_dkv", "mla_w_uk", "mla_w_uv", "mla_w_o", "cf_w_pw1",
       "cf_w_pw2", "ff_w1", "ff_w2"]
SMALL = [n for n in NAMES if n not in BIG]
SMALL_SHARDED = [n for n in SMALL if PARAMS[n][1] is not None]
SMALL_WIDTH = LANES
DEV_BLOCKED = ("sc_w_in", "cf_w_pw1", "ff_w1")
MIXER_BIG = {0: ["sc_w_in", "sc_w_out"], 1: ["mla_w_dq", "mla_w_uq", "mla_w_dkv", "mla_w_uk", "mla_w_uv", "mla_w_o"],
             2: ["cf_w_pw1", "cf_w_pw2"]}


def _layer_big(i):
    return [(n, i // N_MIXERS) for n in MIXER_BIG[i % N_MIXERS]] + [("ff_w1", i), ("ff_w2", i)]


def _shard_2d(name, a):
    return a.reshape(a.shape[0], -1)


def _as_operand(name, g):
    if name in DEV_BLOCKED:
        return g
    if name == "mla_w_uq":
        by_head = g.transpose(1, 0, 2)
        pieces = [by_head[:, :, :QK_NOPE], by_head[:, :, QK_NOPE:QK_NOPE + ROPE_HALF], by_head[:, :, QK_NOPE + ROPE_HALF:]]
        return jnp.concatenate([p.reshape(g.shape[1], -1) for p in pieces], axis=1)
    return g.reshape(-1, g.shape[2])


def _as_parts(name, grad):
    if name in DEV_BLOCKED:
        return grad
    if name == "mla_w_uq":
        rows = grad.shape[0]
        pieces = [grad[:, :Q_NOPE_COLS].reshape(rows, MLA_HEADS, QK_NOPE),
                  grad[:, Q_NOPE_COLS:Q_NOPE_COLS + Q_HALF_COLS].reshape(rows, MLA_HEADS, ROPE_HALF),
                  grad[:, Q_NOPE_COLS + Q_HALF_COLS:].reshape(rows, MLA_HEADS, ROPE_HALF)]
        return jnp.concatenate(pieces, axis=2).transpose(1, 0, 2)
    return grad.reshape(N_DEV, -1, grad.shape[1])


def _shard_shape(name):
    shape, ax = PARAMS[name]
    if ax is None:
        return shape
    return tuple(s // N_DEV if i == ax else s for i, s in enumerate(shape))


def _rows_of(shape, width, align):
    n = 1
    for s in shape:
        n *= s
    rows = -(-n // width)
    return -(-rows // align) * align


def _pack_rows(arrays, width, align, dtype, lead=0):
    segs = []
    for a in arrays:
        batch = a.shape[:lead]
        flat = a.astype(dtype).reshape(*batch, -1)
        rows = _rows_of(a.shape[lead:], width, align)
        flat = jnp.pad(flat, [(0, 0)] * lead + [(0, rows * width - flat.shape[-1])])
        segs.append(flat.reshape(*batch, rows, width))
    return jnp.concatenate(segs, axis=lead)


def _unpack_rows(buf, shapes, width, align):
    out, off = [], 0
    lead = buf.shape[:-2]
    for shape in shapes:
        n = 1
        for s in shape:
            n *= s
        rows = _rows_of(shape, width, align)
        seg = buf[..., off:off + rows, :].reshape(*lead, rows * width)[..., :n]
        out.append(seg.reshape(*lead, *shape))
        off += rows
    return out


def _merge_shards(stacked, ax):
    moved = jnp.moveaxis(stacked, 0, ax)
    shape = moved.shape
    return moved.reshape(*shape[:ax], shape[ax] * shape[ax + 1], *shape[ax + 2:])


def _split_shards(full, ax):
    shape = full.shape
    split = full.reshape(*shape[:ax], N_DEV, shape[ax] // N_DEV, *shape[ax + 1:])
    return jnp.moveaxis(split, ax, 0)


def _adam_math(wv, gv, mv, vv):
    m_new = ADAM_B1 * mv + (1.0 - ADAM_B1) * gv
    v_new = ADAM_B2 * vv + (1.0 - ADAM_B2) * (gv * gv)
    m_hat = m_new / (1.0 - ADAM_B1 ** ADAM_STEP)
    v_hat = v_new / (1.0 - ADAM_B2 ** ADAM_STEP)
    delta = -ADAM_LR * (m_hat / (jnp.sqrt(v_hat) + ADAM_EPS) + ADAM_WD * wv)
    return delta, m_new, v_new


def _adamw(w, g, m, v, *, name):
    shape = w.shape
    c = shape[-1]

    def fn(rows, bc):
        return _adam_math(*rows), ()

    flat = [a.reshape(-1, c) for a in (w, g, m, v)]
    outs = _rowwise(fn, flat, [], [(c, F32)] * 3, [], name=name, tb_cap=512)
    return tuple(o.reshape(shape) for o in outs)


def _adamw_reduced(w, m, v, layer_grads, *, name, after=()):
    n_l, r, c = w.shape
    tb = _row_tile(r, 256)
    nb = r // tb

    def body(*refs):
        w_ref, m_ref, v_ref = refs[:3]
        g_refs = refs[3:3 + 2 * n_l]
        g_out, d_out, m_out, v_out = refs[3 + 2 * n_l + len(after):]
        layer = pl.program_id(0)
        for l in range(n_l):
            @pl.when(layer == l)
            def _(l=l):
                own, arr = g_refs[2 * l], g_refs[2 * l + 1]
                g = own[...] + arr[0].astype(F32)
                g = g + arr[1].astype(F32)
                g = g + arr[2].astype(F32)
                delta, m_new, v_new = _adam_math(w_ref[...], g, m_ref[...], v_ref[...])
                g_out[...] = g
                d_out[...] = delta
                m_out[...] = m_new
                v_out[...] = v_new

    stacked = pl.BlockSpec((tb, c), lambda l, i: (l * nb + i, 0))
    in_specs = [stacked] * 3
    operands = [a.reshape(n_l * r, c) for a in (w, m, v)]
    for l, (own, arr) in enumerate(layer_grads):
        in_specs.append(pl.BlockSpec((tb, c), lambda ll, i, l=l: (jnp.where(ll == l, i, 0), 0)))
        in_specs.append(pl.BlockSpec((3, tb, c), lambda ll, i, l=l: (0, jnp.where(ll == l, i, 0), 0)))
        operands += [own, arr]
    in_specs += [ANY] * len(after)
    operands += list(after)
    outs = pl.pallas_call(
        body,
        name=name,
        grid=(n_l, nb),
        in_specs=in_specs,
        out_specs=[stacked] * 4,
        out_shape=[jax.ShapeDtypeStruct((n_l * r, c), F32)] * 4,
        compiler_params=_params(("arbitrary", "arbitrary")),
    )(*operands)
    return tuple(o.reshape(n_l, r, c) for o in outs)


def _train_step(x, target, weights, m_state, v_state):
    me = _block_id(lax.axis_index("x"), lax.axis_index("y"), lax.axis_index("c"))

    small_shapes = [_shard_shape(n) for n in SMALL_SHARDED]
    first = _all_gather([_pack_rows([weights[n] for n in SMALL_SHARDED], SMALL_WIDTH, SUBLANES, F32)]
                        + [_shard_2d(n, weights[n][0]).astype(MXU_DTYPE) for n in MIXER_BIG[0]], name="gather_g0")
    small_all, first_matrices = first[0], first[1:]
    small_full = {n: weights[n] for n in SMALL if PARAMS[n][1] is None}
    for n, stacked in zip(SMALL_SHARDED, _unpack_rows(small_all, small_shapes, SMALL_WIDTH, SUBLANES)):
        small_full[n] = _merge_shards(stacked, PARAMS[n][1])

    def small_weights(i):
        wl = {}
        for n in SMALL:
            if n.startswith("ln_"):
                wl[n] = small_full[n][i:i + 1]
            elif n.startswith(("sc_", "mla_", "cf_")[i % N_MIXERS]):
                v = small_full[n][i // N_MIXERS]
                wl[n] = v if v.ndim == 2 else v.reshape(1, -1)
        return wl

    layers = [small_weights(i) for i in range(DEPTH)]
    groups = [(kind, i) for i in range(DEPTH) for kind in ("mix", "ff")]

    def group_names(g):
        kind, i = g
        if kind == "mix":
            return [(n, i // N_MIXERS) for n in MIXER_BIG[i % N_MIXERS]]
        return [("ff_w1", i), ("ff_w2", i)]

    def shards_of(g):
        return [_shard_2d(n, weights[n][j]).astype(MXU_DTYPE) for n, j in group_names(g)]

    def install(g, gathered):
        for (n, _), a in zip(group_names(g), gathered):
            layers[g[1]][n] = _as_operand(n, a)

    install(groups[0], first_matrices)
    gather = {1: _gather_start(shards_of(groups[1]), name="gather_g1_start")}
    cur, cur_m = x[0], _cast_input(x[0])
    saved = {}
    for k, g in enumerate(groups):
        kind, i = g
        tokens = [gather[1]["token"]] if k == 0 else []
        if k + 2 < len(groups):
            gather[k + 2] = _gather_start(shards_of(groups[k + 2]), name=f"gather_g{k + 2}_start")
            tokens.append(gather[k + 2]["token"])

        def pass_on(y, k=k):
            if k + 1 == len(groups):
                return None
            gather[k + 1] = _gather_pass_on(gather[k + 1], name=f"gather_g{k + 1}_pass", after=y)
            return gather[k + 1]["token"]

        phase = _mix_fwd if kind == "mix" else _ff_fwd
        cur, cur_m, saved[g] = phase(i, cur, cur_m, layers[i], hook=pass_on, after=tokens)
        if k + 1 < len(groups):
            install(groups[k + 1], _gather_finish(gather[k + 1], name=f"gather_g{k + 1}_finish", after=cur))

    loss_local, dy = _loss_and_grad(cur, target[0])
    loss = lax.psum(loss_local, MESH_AXES)

    reduced = {n: [None] * PARAMS[n][0][0] for n in BIG}
    small_grads = {n: [None] * PARAMS[n][0][0] for n in SMALL}
    reduce = {}

    def to_chips(p, after):
        st = reduce[p]
        parts, got = _swap_finish(st["sibling"], name=f"reduce_p{p}_sibling_finish", after=after)
        sums = [_chip_partials(a, b, name=f"reduce_p{p}_chipsum{t}") for t, (a, b) in enumerate(zip(parts, got))]
        st["own"] = [own for own, _ in sums]
        others = [o for _, o in sums]
        st["chips"] = _swap_start(others, [lax.empty(o.shape, o.dtype) for o in others], _to_chip, 3,
                                  name=f"reduce_p{p}_chips_start")
        return st["chips"]["token"]

    def arrived(p, after):
        st = reduce[p]
        _, got = _swap_finish(st["chips"], name=f"reduce_p{p}_chips_finish", after=after)
        for (n, j), own, arr in zip(st["names"], st["own"], got):
            reduced[n][j] = (own, arr)

    terms = [(1.0, dy)]
    back = list(reversed(groups))
    for p, g in enumerate(back):
        kind, i = g

        def after_first_kernel(dz, p=p):
            return to_chips(p - 1, dz) if p >= 1 else None

        phase = _mix_bwd if kind == "mix" else _ff_bwd
        tokens = [reduce[p - 1]["sibling"]["token"]] if p >= 1 else []
        terms, grads = phase(i, terms, saved[g], layers[i], hook=after_first_kernel, after=tokens)
        for n in SMALL:
            if n in grads:
                small_grads[n][i if n.startswith("ln_") else i // N_MIXERS] = grads[n]
        if p >= 2:
            arrived(p - 2, terms[-1][1])
        names = group_names(g)
        parts = [_as_parts(n, grads[n]) for n, _ in names]
        reduce[p] = {"names": names, "sibling": _swap_start(
            parts, [lax.empty((4, *a.shape[1:]), a.dtype) for a in parts], _to_sibling, 4,
            name=f"reduce_p{p}_sibling_start")}
    grad_x = _sum_terms(terms, name="grad_x")
    last = len(back) - 1
    small_mine = _pack_rows([jnp.stack(small_grads[n]).reshape(PARAMS[n][0]) for n in SMALL], SMALL_WIDTH, SUBLANES, F32)
    small_gather = _gather_start([small_mine], name="gather_small_grads_start")
    tail = to_chips(last, small_gather["token"])
    arrived(last - 1, tail)

    grad, delta, new_m, new_v = {}, {}, {}, {}

    def adamw_matrix(n, after):
        shape = weights[n].shape
        view = lambda a: a.reshape(shape[0], shape[1], -1)
        outs = _adamw_reduced(view(weights[n]), view(m_state[n]), view(v_state[n]), reduced[n], name=f"adamw_{n}",
                              after=after)
        grad[n], delta[n], new_m[n], new_v[n] = [o.reshape(shape) for o in outs]
        return outs[0]

    done = tail
    for n in BIG:
        if n not in MIXER_BIG[0]:
            done = adamw_matrix(n, [done])
    small_gather = _gather_pass_on(small_gather, name="gather_small_grads_pass", after=done)
    small_all = _gather_finish(small_gather, name="gather_small_grads_finish", after=small_gather["token"])[0]
    small_sum = _sum_blocks(None, small_all, name="sum_small_grads")
    for n, g in zip(SMALL, _unpack_rows(small_sum, [PARAMS[n][0] for n in SMALL], SMALL_WIDTH, SUBLANES)):
        ax = PARAMS[n][1]
        grad[n] = g if ax is None else lax.dynamic_index_in_dim(_split_shards(g, ax), me, axis=0, keepdims=False)
        delta[n], new_m[n], new_v[n] = _adamw(weights[n], grad[n], m_state[n], v_state[n], name=f"adamw_{n}")
    arrived(last, small_sum)
    for n in MIXER_BIG[0]:
        adamw_matrix(n, [])
    return (loss, grad_x[None], *[grad[n] for n in NAMES], *[delta[n] for n in NAMES],
            *[new_m[n] for n in NAMES], *[new_v[n] for n in NAMES])


def kernel(x, sc_w_in, sc_conv_w, sc_w_out, mla_w_dq, mla_g_q, mla_w_uq, mla_w_dkv, mla_g_kv, mla_w_uk, mla_w_uv, mla_w_o, cf_w_pw1, cf_b_pw1, cf_dw_w, cf_dw_b, cf_norm_g, cf_norm_b, cf_w_pw2, cf_b_pw2, ff_w1, ff_w2, ln_mix_g, ln_mix_b, ln_ff_g, ln_ff_b, loss_target, m_sc_w_in, m_sc_conv_w, m_sc_w_out, m_mla_w_dq, m_mla_g_q, m_mla_w_uq, m_mla_w_dkv, m_mla_g_kv, m_mla_w_uk, m_mla_w_uv, m_mla_w_o, m_cf_w_pw1, m_cf_b_pw1, m_cf_dw_w, m_cf_dw_b, m_cf_norm_g, m_cf_norm_b, m_cf_w_pw2, m_cf_b_pw2, m_ff_w1, m_ff_w2, m_ln_mix_g, m_ln_mix_b, m_ln_ff_g, m_ln_ff_b, v_sc_w_in, v_sc_conv_w, v_sc_w_out, v_mla_w_dq, v_mla_g_q, v_mla_w_uq, v_mla_w_dkv, v_mla_g_kv, v_mla_w_uk, v_mla_w_uv, v_mla_w_o, v_cf_w_pw1, v_cf_b_pw1, v_cf_dw_w, v_cf_dw_b, v_cf_norm_g, v_cf_norm_b, v_cf_w_pw2, v_cf_b_pw2, v_ff_w1, v_ff_w2, v_ln_mix_g, v_ln_mix_b, v_ln_ff_g, v_ln_ff_b):
    w_list = (sc_w_in, sc_conv_w, sc_w_out, mla_w_dq, mla_g_q, mla_w_uq, mla_w_dkv, mla_g_kv, mla_w_uk, mla_w_uv, mla_w_o,
              cf_w_pw1, cf_b_pw1, cf_dw_w, cf_dw_b, cf_norm_g, cf_norm_b, cf_w_pw2, cf_b_pw2, ff_w1, ff_w2, ln_mix_g,
              ln_mix_b, ln_ff_g, ln_ff_b)
    m_list = (m_sc_w_in, m_sc_conv_w, m_sc_w_out, m_mla_w_dq, m_mla_g_q, m_mla_w_uq, m_mla_w_dkv, m_mla_g_kv, m_mla_w_uk,
              m_mla_w_uv, m_mla_w_o, m_cf_w_pw1, m_cf_b_pw1, m_cf_dw_w, m_cf_dw_b, m_cf_norm_g, m_cf_norm_b, m_cf_w_pw2,
              m_cf_b_pw2, m_ff_w1, m_ff_w2, m_ln_mix_g, m_ln_mix_b, m_ln_ff_g, m_ln_ff_b)
    v_list = (v_sc_w_in, v_sc_conv_w, v_sc_w_out, v_mla_w_dq, v_mla_g_q, v_mla_w_uq, v_mla_w_dkv, v_mla_g_kv, v_mla_w_uk,
              v_mla_w_uv, v_mla_w_o, v_cf_w_pw1, v_cf_b_pw1, v_cf_dw_w, v_cf_dw_b, v_cf_norm_g, v_cf_norm_b, v_cf_w_pw2,
              v_cf_b_pw2, v_ff_w1, v_ff_w2, v_ln_mix_g, v_ln_mix_b, v_ln_ff_g, v_ln_ff_b)
    return _train_step(x, loss_target, dict(zip(NAMES, w_list)), dict(zip(NAMES, m_list)), dict(zip(NAMES, v_list)))
```

```python
import functools

import jax
import jax.numpy as jnp
from jax import lax
from jax.experimental import pallas as pl
from jax.experimental.pallas import tpu as pltpu

F32 = jnp.float32
MXU_DTYPE = jnp.bfloat16

N_DEV = 8
D_MODEL = 1024
DEPTH = 4
N_MIXERS = 3
CHUNK = 64
ALPHA = (2.0 * DEPTH) ** 0.25
LN_EPS = 1e-5
RMS_EPS = 1e-6
MLA_HEADS = 8
QK_NOPE = 128
QK_ROPE = 64
V_HEAD = 128
Q_LORA = 384
KV_LORA = 256
ROPE_THETA = 10000.0
ADAM_LR = 0.001
ADAM_B1 = 0.9
ADAM_B2 = 0.999
ADAM_EPS = 1e-08
ADAM_WD = 0.01
ADAM_STEP = 10

LANES = 128
SUBLANES = 8
VMEM_LIMIT = 56 * 1024 * 1024
MM_VMEM_BUDGET = 36 * 1024 * 1024
CONV_PAD = 32
MASK_VALUE = -1e30

MESH_AXES = ("x", "y", "c")


def _params(semantics):
    return pltpu.CompilerParams(dimension_semantics=semantics, vmem_limit_bytes=VMEM_LIMIT)


def _tile(n, cap):
    if n <= cap:
        return n
    t = cap - cap % LANES
    while t >= LANES:
        if n % t == 0:
            return t
        t -= LANES
    raise ValueError(f"no tile for {n} under {cap}")


def _row_tile(rows, cap):
    if rows <= cap:
        return rows
    t = cap - cap % SUBLANES
    while t >= SUBLANES:
        if rows % t == 0:
            return t
        t -= SUBLANES
    raise ValueError(f"no row tile for {rows} under {cap}")


def _mm(a, b, *, name, ta=False, tb=False, b_dev=False, out_dev=False, out_dtypes=(F32,), epilogue=None,
        tile_extras=(), col_extras=(), after=(), tm_cap=2048, tn_cap=1024, tk_cap=4096):
    m, k = (a.shape[1], a.shape[0]) if ta else a.shape
    if b_dev:
        nd, b_rows, ns = b.shape
        n, kb = (b_rows, nd * ns) if tb else (nd * ns, b_rows)
    else:
        n, kb = (b.shape[0], b.shape[1]) if tb else (b.shape[1], b.shape[0])
    assert k == kb, (a.shape, b.shape, ta, tb)
    tk = ns if (b_dev and tb) else _tile(k, tk_cap)
    nk = k // tk
    fixed_tn = ns if (b_dev and not tb) else (n // N_DEV if out_dev else None)
    if out_dev:
        assert not tile_extras and not col_extras and n % N_DEV == 0
    out_bytes = sum(jnp.dtype(dt).itemsize for dt in out_dtypes) + sum(e.dtype.itemsize for e in tile_extras)
    tm, tn = None, None
    for cand_m, cand_n in ((tm_cap, tn_cap), (tm_cap, tn_cap // 2), (tm_cap // 2, tn_cap), (tm_cap // 2, tn_cap // 2),
                           (tm_cap // 4, tn_cap // 2), (tm_cap // 4, tn_cap // 4)):
        tm, tn = _tile(m, cand_m), (fixed_tn if fixed_tn is not None else _tile(n, cand_n))
        need = 2 * (tm * tk * a.dtype.itemsize + tk * tn * b.dtype.itemsize) + 2 * tm * tn * out_bytes
        need += tm * tn * 4 * (2 if nk > 1 else 1)
        if need <= MM_VMEM_BUDGET:
            break
    n_te, n_ce, n_out = len(tile_extras), len(col_extras), len(out_dtypes)
    dims = (((0 if ta else 1,), (1 if tb else 0,)), ((), ()))

    def body(*refs):
        a_ref, b_ref = refs[0], refs[1]
        te_refs = refs[2:2 + n_te]
        ce_refs = refs[2 + n_te:2 + n_te + n_ce]
        first_out = 2 + n_te + n_ce + len(after)
        out_refs = refs[first_out:first_out + n_out]

        def finish(acc):
            if epilogue is None:
                outs = (acc,)
            else:
                outs = epilogue(acc, *[r[...] for r in te_refs], *[r[...] for r in ce_refs])
            for o_ref, o in zip(out_refs, outs):
                o_ref[...] = o.astype(o_ref.dtype)

        def product():
            return lax.dot_general(a_ref[...].astype(MXU_DTYPE), b_ref[...].astype(MXU_DTYPE), dims,
                                   preferred_element_type=F32)

        if nk == 1:
            finish(product())
        else:
            acc_ref = refs[-1]
            kk = pl.program_id(2)

            @pl.when(kk == 0)
            def _():
                acc_ref[...] = jnp.zeros_like(acc_ref)

            acc_ref[...] += product()

            @pl.when(kk == nk - 1)
            def _():
                finish(acc_ref[...])

    a_spec = pl.BlockSpec((tk, tm), lambda i, j, kk: (kk, i)) if ta else pl.BlockSpec((tm, tk), lambda i, j, kk: (i, kk))
    if b_dev and tb:
        b_spec = pl.BlockSpec((None, tn, tk), lambda i, j, kk: (kk, j, 0))
    elif b_dev:
        b_spec = pl.BlockSpec((None, tk, tn), lambda i, j, kk: (j, kk, 0))
    elif tb:
        b_spec = pl.BlockSpec((tn, tk), lambda i, j, kk: (j, kk))
    else:
        b_spec = pl.BlockSpec((tk, tn), lambda i, j, kk: (kk, j))
    tile_spec = pl.BlockSpec((tm, tn), lambda i, j, kk: (i, j))
    col_spec = pl.BlockSpec((1, tn), lambda i, j, kk: (0, j))
    if out_dev:
        out_spec = pl.BlockSpec((None, tm, tn), lambda i, j, kk: (j, i, 0))
        out_shape = [jax.ShapeDtypeStruct((N_DEV, m, tn), dt) for dt in out_dtypes]
    else:
        out_spec = tile_spec
        out_shape = [jax.ShapeDtypeStruct((m, n), dt) for dt in out_dtypes]
    outs = pl.pallas_call(
        body,
        name=name,
        grid=(m // tm, n // tn, nk),
        in_specs=[a_spec, b_spec] + [tile_spec] * n_te + [col_spec] * n_ce + [ANY] * len(after),
        out_specs=[out_spec] * n_out,
        out_shape=out_shape,
        scratch_shapes=[pltpu.VMEM((tm, tn), F32)] if nk > 1 else [],
        compiler_params=_params(("parallel", "parallel", "arbitrary")),
    )(a, b, *tile_extras, *col_extras, *after)
    return outs[0] if n_out == 1 else tuple(outs)


def _rowwise(fn, rows, bcast, out_rows, out_accs, *, name, tb_cap=256, after=()):
    t = rows[0].shape[0]
    tb = _row_tile(t, tb_cap)
    n_r, n_b, n_o, n_a = len(rows), len(bcast), len(out_rows), len(out_accs)

    def body(*refs):
        r_refs = refs[:n_r]
        b_refs = refs[n_r:n_r + n_b]
        first_out = n_r + n_b + len(after)
        o_refs = refs[first_out:first_out + n_o]
        a_refs = refs[first_out + n_o:]
        outs, accs = fn([r[...] for r in r_refs], [r[...] for r in b_refs])
        for o_ref, o in zip(o_refs, outs):
            o_ref[...] = o.astype(o_ref.dtype)
        if n_a:
            i = pl.program_id(0)

            @pl.when(i == 0)
            def _():
                for a_ref, acc in zip(a_refs, accs):
                    a_ref[...] = acc

            @pl.when(i > 0)
            def _():
                for a_ref, acc in zip(a_refs, accs):
                    a_ref[...] += acc

    in_specs = [pl.BlockSpec((tb, r.shape[1]), lambda i: (i, 0)) for r in rows]
    in_specs += [pl.BlockSpec(b.shape, lambda i: (0, 0)) for b in bcast]
    in_specs += [ANY] * len(after)
    out_specs = [pl.BlockSpec((tb, c), lambda i: (i, 0)) for c, _ in out_rows]
    out_specs += [pl.BlockSpec(s, lambda i: (0, 0)) for s in out_accs]
    out_shape = [jax.ShapeDtypeStruct((t, c), dt) for c, dt in out_rows]
    out_shape += [jax.ShapeDtypeStruct(s, F32) for s in out_accs]
    outs = pl.pallas_call(
        body,
        name=name,
        grid=(t // tb,),
        in_specs=in_specs,
        out_specs=out_specs,
        out_shape=out_shape,
        compiler_params=_params(("arbitrary",)),
    )(*rows, *bcast, *after)
    return tuple(outs)


def _colsum(v):
    return jnp.sum(v, axis=0, keepdims=True)


def _sigmoid(v):
    return 1.0 / (1.0 + jnp.exp(-v))


def _ln_fwd(x_res, y_sub, g, b, *, name, after=()):
    d = x_res.shape[1]

    def fn(rows, bc):
        z = ALPHA * rows[0] + rows[1]
        mu = jnp.mean(z, axis=-1, keepdims=True)
        zc = z - mu
        var = jnp.mean(zc * zc, axis=-1, keepdims=True)
        rstd = lax.rsqrt(var + LN_EPS)
        xhat = zc * rstd
        out = xhat * bc[0] + bc[1]
        return (out, out, xhat, rstd), ()

    return _rowwise(fn, [x_res, y_sub], [g, b], [(d, F32), (d, MXU_DTYPE), (d, F32), (1, F32)], [], name=name,
                    after=after)


def _ln_bwd(terms, xhat, rstd, g, *, name, after=()):
    d = xhat.shape[1]
    coefs = [c for c, _ in terms]

    def fn(rows, bc):
        xh, rs = rows[0], rows[1]
        dout = None
        for c, v in zip(coefs, rows[2:]):
            tv = v if c == 1.0 else c * v
            dout = tv if dout is None else dout + tv
        dxh = dout * bc[0]
        m1 = jnp.mean(dxh, axis=-1, keepdims=True)
        m2 = jnp.mean(dxh * xh, axis=-1, keepdims=True)
        dz = rs * (dxh - m1 - xh * m2)
        return (dz, dz), (_colsum(dout * xh), _colsum(dout), _colsum(dz))

    return _rowwise(fn, [xhat, rstd] + [v for _, v in terms], [g], [(d, F32), (d, MXU_DTYPE)],
                    [(1, d), (1, d), (1, d)], name=name, after=after)


def _rms_fwd(x, g, *, name):
    c = x.shape[1]

    def fn(rows, bc):
        v = rows[0]
        r = lax.rsqrt(jnp.mean(v * v, axis=-1, keepdims=True) + RMS_EPS)
        return (v * r * bc[0],), ()

    return _rowwise(fn, [x], [g], [(c, MXU_DTYPE)], [], name=name)[0]


def _rms_bwd(x, g, dy, *, name):
    c = x.shape[1]

    def fn(rows, bc):
        v, dyv = rows
        r = lax.rsqrt(jnp.mean(v * v, axis=-1, keepdims=True) + RMS_EPS)
        xh = v * r
        dxh = dyv * bc[0]
        dx = r * (dxh - xh * jnp.mean(dxh * xh, axis=-1, keepdims=True))
        return (dx,), (_colsum(dyv * xh),)

    return _rowwise(fn, [x, dy], [g], [(c, F32)], [(1, c)], name=name)


def _conv_chunk(t):
    return _row_tile(t, 256)


def _conv_fwd(x, w, bias, *, name):
    t, c = x.shape
    kw = w.shape[0]
    ch = _conv_chunk(t)

    def body(x_ref, w_ref, b_ref, o_ref, pad_ref):
        pad_ref[0:CONV_PAD, :] = jnp.zeros((CONV_PAD, LANES), F32)
        pad_ref[CONV_PAD:CONV_PAD + t, :] = x_ref[...]
        for t0 in range(0, t, ch):
            acc = jnp.broadcast_to(b_ref[...], (ch, LANES))
            for k in range(kw):
                s = kw - 1 - k
                acc = acc + w_ref[k:k + 1, :] * pad_ref[CONV_PAD - s + t0:CONV_PAD - s + t0 + ch, :]
            o_ref[t0:t0 + ch, :] = acc

    return pl.pallas_call(
        body,
        name=name,
        grid=(c // LANES,),
        in_specs=[pl.BlockSpec((t, LANES), lambda j: (0, j)), pl.BlockSpec((kw, LANES), lambda j: (0, j)),
                  pl.BlockSpec((1, LANES), lambda j: (0, j))],
        out_specs=pl.BlockSpec((t, LANES), lambda j: (0, j)),
        out_shape=jax.ShapeDtypeStruct((t, c), F32),
        scratch_shapes=[pltpu.VMEM((CONV_PAD + t, LANES), F32)],
        compiler_params=_params(("parallel",)),
    )(x, w, bias)


def _conv_bwd(dy, x, w, *, name):
    t, c = x.shape
    kw = w.shape[0]
    ch = _conv_chunk(t)

    def body(dy_ref, x_ref, w_ref, dx_ref, dw_ref, xpad_ref, dpad_ref):
        xpad_ref[0:CONV_PAD, :] = jnp.zeros((CONV_PAD, LANES), F32)
        xpad_ref[CONV_PAD:CONV_PAD + t, :] = x_ref[...]
        dpad_ref[0:t, :] = dy_ref[...]
        dpad_ref[t:t + CONV_PAD, :] = jnp.zeros((CONV_PAD, LANES), F32)
        for t0 in range(0, t, ch):
            acc = jnp.zeros((ch, LANES), F32)
            for k in range(kw):
                s = kw - 1 - k
                acc = acc + w_ref[k:k + 1, :] * dpad_ref[t0 + s:t0 + s + ch, :]
            dx_ref[t0:t0 + ch, :] = acc
        for k in range(kw):
            s = kw - 1 - k
            acc = jnp.zeros((ch, LANES), F32)
            for t0 in range(0, t, ch):
                acc = acc + dy_ref[t0:t0 + ch, :] * xpad_ref[CONV_PAD - s + t0:CONV_PAD - s + t0 + ch, :]
            dw_ref[k:k + 1, :] = _colsum(acc)

    blk = pl.BlockSpec((t, LANES), lambda j: (0, j))
    wblk = pl.BlockSpec((kw, LANES), lambda j: (0, j))
    return pl.pallas_call(
        body,
        name=name,
        grid=(c // LANES,),
        in_specs=[blk, blk, wblk],
        out_specs=[blk, wblk],
        out_shape=[jax.ShapeDtypeStruct((t, c), F32), jax.ShapeDtypeStruct((kw, c), F32)],
        scratch_shapes=[pltpu.VMEM((CONV_PAD + t, LANES), F32), pltpu.VMEM((CONV_PAD + t, LANES), F32)],
        compiler_params=_params(("parallel",)),
    )(dy, x, w)


ATTN_SCALE = (QK_NOPE + QK_ROPE) ** -0.5
_NT = (((1,), (1,)), ((), ()))
_TN = (((0,), (0,)), ((), ()))
_NN = (((1,), (0,)), ((), ()))


def _dot(a, b, dims):
    return lax.dot_general(a, b, dims, preferred_element_type=F32)


def _block_scores(qn, qp, kn_ref, kp_ref, lo, hi, diagonal):
    s = _dot(qn, kn_ref[lo:hi, :], _NT) + _dot(qp, kp_ref[lo:hi, :], _NT)
    s = s * ATTN_SCALE
    if diagonal:
        shift = CHUNK.bit_length() - 1
        rows = lax.broadcasted_iota(jnp.int32, s.shape, 0)
        cols = lax.broadcasted_iota(jnp.int32, s.shape, 1)
        s = jnp.where(jnp.right_shift(cols, shift) <= jnp.right_shift(rows, shift), s, MASK_VALUE)
    return s


def _attn_fwd(qn, qp, kn, kp, v, *, name):
    h, t, _ = qp.shape
    tq = _row_tile(t, 256)
    assert tq % CHUNK == 0

    def body(qn_ref, qp_ref, kn_ref, kp_ref, v_ref, o_ref, of_ref, lse_ref):
        for r0 in range(0, t, tq):
            r1 = r0 + tq
            qn, qp = qn_ref[r0:r1, :].astype(MXU_DTYPE), qp_ref[r0:r1, :]
            s_d = _block_scores(qn, qp, kn_ref, kp_ref, r0, r1, True)
            m = jnp.max(s_d, axis=-1, keepdims=True)
            if r0:
                s_b = _block_scores(qn, qp, kn_ref, kp_ref, 0, r0, False)
                m = jnp.maximum(m, jnp.max(s_b, axis=-1, keepdims=True))
            p_d = jnp.exp(s_d - m)
            l = jnp.sum(p_d, axis=-1, keepdims=True)
            acc = _dot(p_d.astype(MXU_DTYPE), v_ref[r0:r1, :], _NN)
            if r0:
                p_b = jnp.exp(s_b - m)
                l = l + jnp.sum(p_b, axis=-1, keepdims=True)
                acc = acc + _dot(p_b.astype(MXU_DTYPE), v_ref[0:r0, :], _NN)
            o = acc / l
            o_ref[r0:r1, :] = o.astype(o_ref.dtype)
            of_ref[r0:r1, :] = o
            lse_ref[r0:r1, :] = m + jnp.log(l)

    blk = lambda w: pl.BlockSpec((None, t, w), lambda hh: (hh, 0, 0))
    cols = pl.BlockSpec((t, QK_NOPE), lambda hh: (0, hh))
    return pl.pallas_call(
        body,
        name=name,
        grid=(h,),
        in_specs=[cols, blk(QK_ROPE), cols, pl.BlockSpec((t, QK_ROPE), lambda hh: (0, 0)), cols],
        out_specs=[cols, cols, blk(1)],
        out_shape=[jax.ShapeDtypeStruct((t, h * V_HEAD), MXU_DTYPE), jax.ShapeDtypeStruct((t, h * V_HEAD), F32),
                   jax.ShapeDtypeStruct((h, t, 1), F32)],
        compiler_params=_params(("parallel",)),
    )(qn, qp, kn, kp, v)


def _attn_bwd(qn, qp, kn, kp, v, do, o, lse, *, name):
    h, t, _ = qp.shape
    tq = _row_tile(t, 256)

    def body(qn_ref, qp_ref, kn_ref, kp_ref, v_ref, do_ref, o_ref, lse_ref, dqn_ref, dqp_ref, dkn_ref, dkp_ref,
             dv_ref):
        dkn_ref[...] = jnp.zeros_like(dkn_ref)
        dv_ref[...] = jnp.zeros_like(dv_ref)

        @pl.when(pl.program_id(0) == 0)
        def _():
            dkp_ref[...] = jnp.zeros_like(dkp_ref)

        for r0 in range(0, t, tq):
            r1 = r0 + tq
            qn, qp = qn_ref[r0:r1, :].astype(MXU_DTYPE), qp_ref[r0:r1, :]
            dov, lse = do_ref[r0:r1, :], lse_ref[r0:r1, :]
            delta = jnp.sum(dov.astype(F32) * o_ref[r0:r1, :], axis=-1, keepdims=True)

            def piece(lo, hi, diagonal):
                s = _block_scores(qn, qp, kn_ref, kp_ref, lo, hi, diagonal)
                p = jnp.exp(s - lse)
                dp = _dot(dov, v_ref[lo:hi, :], _NT)
                ds_m = (p * (dp - delta) * ATTN_SCALE).astype(MXU_DTYPE)
                dkn_ref[lo:hi, :] += _dot(ds_m, qn, _TN)
                dkp_ref[lo:hi, :] += _dot(ds_m, qp, _TN)
                dv_ref[lo:hi, :] += _dot(p.astype(MXU_DTYPE), dov, _TN)
                return _dot(ds_m, kn_ref[lo:hi, :], _NN), _dot(ds_m, kp_ref[lo:hi, :], _NN)

            dqn, dqp = piece(r0, r1, True)
            if r0:
                dqn_b, dqp_b = piece(0, r0, False)
                dqn, dqp = dqn + dqn_b, dqp + dqp_b
            dqn_ref[r0:r1, :] = dqn
            dqp_ref[r0:r1, :] = dqp

    blk = lambda w: pl.BlockSpec((None, t, w), lambda hh: (hh, 0, 0))
    kpblk = pl.BlockSpec((t, QK_ROPE), lambda hh: (0, 0))
    cols = pl.BlockSpec((t, QK_NOPE), lambda hh: (0, hh))
    return pl.pallas_call(
        body,
        name=name,
        grid=(h,),
        in_specs=[cols, blk(QK_ROPE), cols, kpblk, cols, cols, cols, blk(1)],
        out_specs=[cols, blk(QK_ROPE), cols, kpblk, cols],
        out_shape=[jax.ShapeDtypeStruct((t, h * QK_NOPE), F32), jax.ShapeDtypeStruct((h, t, QK_ROPE), F32),
                   jax.ShapeDtypeStruct((t, h * QK_NOPE), F32), jax.ShapeDtypeStruct((t, QK_ROPE), F32),
                   jax.ShapeDtypeStruct((t, h * V_HEAD), F32)],
        compiler_params=_params(("arbitrary",)),
    )(qn, qp, kn, kp, v, do, o, lse)


def _rope(x1, x2, cos, sin, *, name, inverse=False):
    w = x1.shape[1]
    sign = -1.0 if inverse else 1.0

    def fn(rows, bc):
        a, b, c, s = rows
        s = sign * s
        return (a * c - b * s, a * s + b * c), ()

    return _rowwise(fn, [x1, x2, cos, sin], [], [(w, F32), (w, F32)], [], name=name)


def _mlp_fwd(x_m, w1, w2, tag, after=()):
    def epi(acc):
        r = jnp.maximum(acc, 0.0)
        return (r * r,)

    act = _mm(x_m, w1, b_dev=True, name=f"{tag}_up", out_dtypes=(MXU_DTYPE,), epilogue=epi, after=after)
    y = _mm(act, w2, name=f"{tag}_down")
    return y, (x_m, act)


def _mlp_bwd(dy_m, saved, w1, w2, tag, after=()):
    x_m, act = saved
    dw2 = _mm(act, dy_m, ta=True, name=f"{tag}_dw2", out_dtypes=(MXU_DTYPE,))

    def epi(acc, av):
        return (acc * (2.0 * jnp.sqrt(av.astype(F32))),)

    dh = _mm(dy_m, w2, tb=True, name=f"{tag}_dact", out_dtypes=(MXU_DTYPE,), epilogue=epi, tile_extras=(act,),
             after=after)
    dw1 = _mm(x_m, dh, ta=True, out_dev=True, name=f"{tag}_dw1", out_dtypes=(MXU_DTYPE,))
    dx = _mm(dh, w1, tb=True, b_dev=True, name=f"{tag}_dx")
    return [dx], {"ff_w1": dw1, "ff_w2": dw2}


def _sc_gate_specs(t, d):
    per = d // LANES
    return [pl.BlockSpec((t, LANES), lambda j, k=k: (0, k * per + j)) for k in range(3)]


def _sc_gate_fwd(u, conv_w, *, name):
    t, d = u.shape[0], u.shape[1] // 3
    kw = conv_w.shape[0]
    ch = _conv_chunk(t)

    def body(b_ref, c_ref, h_ref, w_ref, r_ref, pad_ref):
        pad_ref[0:CONV_PAD, :] = jnp.zeros((CONV_PAD, LANES), F32)
        pad_ref[CONV_PAD:CONV_PAD + t, :] = c_ref[...] * h_ref[...]
        for t0 in range(0, t, ch):
            q = jnp.zeros((ch, LANES), F32)
            for k in range(kw):
                s = kw - 1 - k
                q = q + w_ref[k:k + 1, :] * pad_ref[CONV_PAD - s + t0:CONV_PAD - s + t0 + ch, :]
            r_ref[t0:t0 + ch, :] = (b_ref[t0:t0 + ch, :] * q).astype(r_ref.dtype)

    blk = pl.BlockSpec((t, LANES), lambda j: (0, j))
    return pl.pallas_call(
        body,
        name=name,
        grid=(d // LANES,),
        in_specs=_sc_gate_specs(t, d) + [pl.BlockSpec((kw, LANES), lambda j: (0, j))],
        out_specs=blk,
        out_shape=jax.ShapeDtypeStruct((t, d), MXU_DTYPE),
        scratch_shapes=[pltpu.VMEM((CONV_PAD + t, LANES), F32)],
        compiler_params=_params(("parallel",)),
    )(u, u, u, conv_w)


def _sc_gate_bwd(dr, u, conv_w, *, name):
    t, d = u.shape[0], u.shape[1] // 3
    kw = conv_w.shape[0]
    ch = _conv_chunk(t)

    def body(dr_ref, b_ref, c_ref, h_ref, w_ref, db_ref, dp_ref, dw_ref, ppad_ref, dpad_ref):
        ppad_ref[0:CONV_PAD, :] = jnp.zeros((CONV_PAD, LANES), F32)
        ppad_ref[CONV_PAD:CONV_PAD + t, :] = c_ref[...] * h_ref[...]
        dpad_ref[0:t, :] = dr_ref[...] * b_ref[...]
        dpad_ref[t:t + CONV_PAD, :] = jnp.zeros((CONV_PAD, LANES), F32)
        for t0 in range(0, t, ch):
            q = jnp.zeros((ch, LANES), F32)
            dp = jnp.zeros((ch, LANES), F32)
            for k in range(kw):
                s = kw - 1 - k
                q = q + w_ref[k:k + 1, :] * ppad_ref[CONV_PAD - s + t0:CONV_PAD - s + t0 + ch, :]
                dp = dp + w_ref[k:k + 1, :] * dpad_ref[t0 + s:t0 + s + ch, :]
            db_ref[t0:t0 + ch, :] = (dr_ref[t0:t0 + ch, :] * q).astype(db_ref.dtype)
            dp_ref[t0:t0 + ch, :] = dp
        for k in range(kw):
            s = kw - 1 - k
            acc = jnp.zeros((ch, LANES), F32)
            for t0 in range(0, t, ch):
                acc = acc + dpad_ref[t0:t0 + ch, :] * ppad_ref[CONV_PAD - s + t0:CONV_PAD - s + t0 + ch, :]
            dw_ref[k:k + 1, :] = _colsum(acc)

    blk = pl.BlockSpec((t, LANES), lambda j: (0, j))
    wblk = pl.BlockSpec((kw, LANES), lambda j: (0, j))
    return pl.pallas_call(
        body,
        name=name,
        grid=(d // LANES,),
        in_specs=[blk] + _sc_gate_specs(t, d) + [wblk],
        out_specs=[blk, blk, wblk],
        out_shape=[jax.ShapeDtypeStruct((t, d), MXU_DTYPE), jax.ShapeDtypeStruct((t, d), F32),
                   jax.ShapeDtypeStruct((kw, d), F32)],
        scratch_shapes=[pltpu.VMEM((CONV_PAD + t, LANES), F32), pltpu.VMEM((CONV_PAD + t, LANES), F32)],
        compiler_params=_params(("parallel",)),
    )(dr, u, u, u, conv_w)


def _sc_fwd(x_m, w_in, conv_w, w_out, tag, after=()):
    u = _mm(x_m, w_in, b_dev=True, name=f"{tag}_in", after=after)
    r = _sc_gate_fwd(u, conv_w, name=f"{tag}_gate")
    y = _mm(r, w_out, name=f"{tag}_out")
    return y, (x_m, u, r)


def _sc_bwd(dy_m, saved, w_in, conv_w, w_out, tag, after=()):
    d = D_MODEL
    x_m, u, r = saved
    dw_out = _mm(r, dy_m, ta=True, name=f"{tag}_dwout", out_dtypes=(MXU_DTYPE,))
    dr = _mm(dy_m, w_out, tb=True, name=f"{tag}_dr", after=after)
    db, dp, dconv = _sc_gate_bwd(dr, u, conv_w, name=f"{tag}_dgate")

    def du_fn(rows, bc):
        dbv, dpv, uv = rows
        return (jnp.concatenate([dbv.astype(F32), dpv * uv[:, 2 * d:3 * d], dpv * uv[:, d:2 * d]], axis=1),), ()

    du = _rowwise(du_fn, [db, dp, u], [], [(3 * d, MXU_DTYPE)], [], name=f"{tag}_du")[0]
    dw_in = _mm(x_m, du, ta=True, out_dev=True, name=f"{tag}_dwin", out_dtypes=(MXU_DTYPE,))
    dx = _mm(du, w_in, tb=True, b_dev=True, name=f"{tag}_dx")
    return [dx], {"sc_w_in": dw_in, "sc_conv_w": dconv, "sc_w_out": dw_out}


def _cf_fwd(x_m, w, tag, after=()):
    d = D_MODEL

    def bias_epi(acc, bias):
        return (acc + bias,)

    u = _mm(x_m, w["cf_w_pw1"], b_dev=True, name=f"{tag}_pw1", epilogue=bias_epi, col_extras=(w["cf_b_pw1"],),
            after=after)

    def glu_fn(rows, bc):
        uv = rows[0]
        return (uv[:, 0:d] * _sigmoid(uv[:, d:2 * d]),), ()

    h1 = _rowwise(glu_fn, [u], [], [(d, F32)], [], name=f"{tag}_glu")[0]
    h2 = _conv_fwd(h1, w["cf_dw_w"], w["cf_dw_b"], name=f"{tag}_conv")

    def norm_fn(rows, bc):
        hv = rows[0]
        mu = jnp.mean(hv, axis=-1, keepdims=True)
        hc = hv - mu
        rstd = lax.rsqrt(jnp.mean(hc * hc, axis=-1, keepdims=True) + LN_EPS)
        h3 = hc * rstd * bc[0] + bc[1]
        return (h3 * _sigmoid(h3),), ()

    h4 = _rowwise(norm_fn, [h2], [w["cf_norm_g"], w["cf_norm_b"]], [(d, MXU_DTYPE)], [], name=f"{tag}_norm")[0]
    y = _mm(h4, w["cf_w_pw2"], name=f"{tag}_pw2", epilogue=bias_epi, col_extras=(w["cf_b_pw2"],))
    return y, (x_m, u, h1, h2, h4)


def _cf_bwd(dy_m, dy_colsum, saved, w, tag, after=()):
    d = D_MODEL
    x_m, u, h1, h2, h4 = saved
    dw_pw2 = _mm(h4, dy_m, ta=True, name=f"{tag}_dwpw2", out_dtypes=(MXU_DTYPE,))
    dh4 = _mm(dy_m, w["cf_w_pw2"], tb=True, name=f"{tag}_dh4", after=after)

    def dnorm_fn(rows, bc):
        dh4v, hv = rows
        g, b = bc
        mu = jnp.mean(hv, axis=-1, keepdims=True)
        hc = hv - mu
        rstd = lax.rsqrt(jnp.mean(hc * hc, axis=-1, keepdims=True) + LN_EPS)
        xh = hc * rstd
        h3 = xh * g + b
        sg = _sigmoid(h3)
        dh3 = dh4v * (sg * (1.0 + h3 * (1.0 - sg)))
        dxh = dh3 * g
        dh2 = rstd * (dxh - jnp.mean(dxh, axis=-1, keepdims=True) - xh * jnp.mean(dxh * xh, axis=-1, keepdims=True))
        return (dh2,), (_colsum(dh3 * xh), _colsum(dh3), _colsum(dh2))

    dh2, dnorm_g, dnorm_b, ddw_b = _rowwise(dnorm_fn, [dh4, h2], [w["cf_norm_g"], w["cf_norm_b"]], [(d, F32)],
                                            [(1, d), (1, d), (1, d)], name=f"{tag}_dnorm")
    dh1, ddw_w = _conv_bwd(dh2, h1, w["cf_dw_w"], name=f"{tag}_dconv")

    def dglu_fn(rows, bc):
        dh1v, uv = rows
        a, gate = uv[:, 0:d], uv[:, d:2 * d]
        sg = _sigmoid(gate)
        du = jnp.concatenate([dh1v * sg, dh1v * a * sg * (1.0 - sg)], axis=1)
        return (du,), (_colsum(du),)

    du, db_pw1 = _rowwise(dglu_fn, [dh1, u], [], [(2 * d, MXU_DTYPE)], [(1, 2 * d)], name=f"{tag}_dglu")
    dw_pw1 = _mm(x_m, du, ta=True, out_dev=True, name=f"{tag}_dwpw1", out_dtypes=(MXU_DTYPE,))
    dx = _mm(du, w["cf_w_pw1"], tb=True, b_dev=True, name=f"{tag}_dx")
    grads = {"cf_w_pw1": dw_pw1, "cf_b_pw1": db_pw1, "cf_dw_w": ddw_w, "cf_dw_b": ddw_b, "cf_norm_g": dnorm_g,
             "cf_norm_b": dnorm_b, "cf_w_pw2": dw_pw2, "cf_b_pw2": dy_colsum}
    return [dx], grads


def _rope_tables(t):
    pos = jnp.arange(t, dtype=F32)
    inv_freq = ROPE_THETA ** (-jnp.arange(0, QK_ROPE, 2, dtype=F32) / QK_ROPE)
    ang = pos[:, None] * inv_freq[None, :]
    return jnp.cos(ang), jnp.sin(ang)


ROPE_HALF = QK_ROPE // 2
Q_NOPE_COLS = MLA_HEADS * QK_NOPE
Q_HALF_COLS = MLA_HEADS * ROPE_HALF


def _mla_fwd(x_m, w, tag, after=()):
    t = x_m.shape[0]
    cos, sin = _rope_tables(t)
    cos_h, sin_h = jnp.tile(cos, (1, MLA_HEADS)), jnp.tile(sin, (1, MLA_HEADS))
    cq_pre = _mm(x_m, w["mla_w_dq"], name=f"{tag}_dq", after=after)
    cq = _rms_fwd(cq_pre, w["mla_g_q"], name=f"{tag}_qnorm")
    q = _mm(cq, w["mla_w_uq"], name=f"{tag}_uq")
    r1, r2 = _rope(q[:, Q_NOPE_COLS:Q_NOPE_COLS + Q_HALF_COLS], q[:, Q_NOPE_COLS + Q_HALF_COLS:], cos_h, sin_h,
                   name=f"{tag}_qrope")
    qp_h = jnp.concatenate([r1.reshape(t, MLA_HEADS, ROPE_HALF), r2.reshape(t, MLA_HEADS, ROPE_HALF)],
                           axis=-1).transpose(1, 0, 2).astype(MXU_DTYPE)
    ckv_full = _mm(x_m, w["mla_w_dkv"], name=f"{tag}_dkv")
    ckv_pre = ckv_full[:, :KV_LORA]
    ckv = _rms_fwd(ckv_pre, w["mla_g_kv"], name=f"{tag}_kvnorm")
    k1, k2 = _rope(ckv_full[:, KV_LORA:KV_LORA + ROPE_HALF], ckv_full[:, KV_LORA + ROPE_HALF:], cos, sin,
                   name=f"{tag}_krope")
    kp = jnp.concatenate([k1, k2], axis=-1).astype(MXU_DTYPE)
    kn = _mm(ckv, w["mla_w_uk"], name=f"{tag}_uk", out_dtypes=(MXU_DTYPE,))
    v = _mm(ckv, w["mla_w_uv"], name=f"{tag}_uv", out_dtypes=(MXU_DTYPE,))
    o, o_f, lse = _attn_fwd(q, qp_h, kn, kp, v, name=f"{tag}_attn")
    y = _mm(o, w["mla_w_o"], name=f"{tag}_o")
    return y, (x_m, cq_pre, cq, ckv_pre, ckv, q, qp_h, kn, kp, v, lse, o, o_f, (cos, sin, cos_h, sin_h))


def _mla_bwd(dy_m, saved, w, tag, after=()):
    x_m, cq_pre, cq, ckv_pre, ckv, q, qp_h, kn, kp, v, lse, o, o_f, (cos, sin, cos_h, sin_h) = saved
    t = x_m.shape[0]
    half = ROPE_HALF
    dw_o = _mm(o, dy_m, ta=True, name=f"{tag}_dwo", out_dtypes=(MXU_DTYPE,))
    do = _mm(dy_m, w["mla_w_o"], tb=True, name=f"{tag}_do", out_dtypes=(MXU_DTYPE,), after=after)
    dqn, dqp_h, dkn, dkp, dv = _attn_bwd(q, qp_h, kn, kp, v, do, o_f, lse, name=f"{tag}_dattn")
    dqp = dqp_h.transpose(1, 0, 2)
    d1, d2 = _rope(dqp[:, :, :half].reshape(t, -1), dqp[:, :, half:].reshape(t, -1), cos_h, sin_h,
                   name=f"{tag}_dqrope", inverse=True)
    dq = jnp.concatenate([dqn, d1, d2], axis=-1).astype(MXU_DTYPE)
    dw_uq = _mm(cq, dq, ta=True, name=f"{tag}_dwuq", out_dtypes=(MXU_DTYPE,))
    dcq = _mm(dq, w["mla_w_uq"], tb=True, name=f"{tag}_dcq")
    dcq_pre, dg_q = _rms_bwd(cq_pre, w["mla_g_q"], dcq, name=f"{tag}_dqnorm")
    dcq_pre_m = dcq_pre.astype(MXU_DTYPE)
    dw_dq = _mm(x_m, dcq_pre_m, ta=True, name=f"{tag}_dwdq", out_dtypes=(MXU_DTYPE,))
    dx_q = _mm(dcq_pre_m, w["mla_w_dq"], tb=True, name=f"{tag}_dxq")
    dw_uk = _mm(ckv, dkn, ta=True, name=f"{tag}_dwuk", out_dtypes=(MXU_DTYPE,))
    dw_uv = _mm(ckv, dv, ta=True, name=f"{tag}_dwuv", out_dtypes=(MXU_DTYPE,))
    dckv_k = _mm(dkn, w["mla_w_uk"], tb=True, name=f"{tag}_dckvk")
    dckv = _mm(dv, w["mla_w_uv"], tb=True, name=f"{tag}_dckv", epilogue=lambda acc, other: (acc + other,),
               tile_extras=(dckv_k,))
    dckv_pre, dg_kv = _rms_bwd(ckv_pre, w["mla_g_kv"], dckv, name=f"{tag}_dkvnorm")
    dk1, dk2 = _rope(dkp[:, :half], dkp[:, half:], cos, sin, name=f"{tag}_dkrope", inverse=True)
    dckv_full = jnp.concatenate([dckv_pre, dk1, dk2], axis=-1).astype(MXU_DTYPE)
    dw_dkv = _mm(x_m, dckv_full, ta=True, name=f"{tag}_dwdkv", out_dtypes=(MXU_DTYPE,))
    dx_kv = _mm(dckv_full, w["mla_w_dkv"], tb=True, name=f"{tag}_dxkv")
    grads = {"mla_w_dq": dw_dq, "mla_g_q": dg_q, "mla_w_uq": dw_uq, "mla_w_dkv": dw_dkv, "mla_g_kv": dg_kv,
             "mla_w_uk": dw_uk, "mla_w_uv": dw_uv, "mla_w_o": dw_o}
    return [dx_q, dx_kv], grads


def _cast_input(x):
    cast = lambda rows, bc: ((rows[0],), ())
    return _rowwise(cast, [x], [], [(D_MODEL, MXU_DTYPE)], [], name="x_cast")[0]


def _hooked(value, hook):
    token = None if hook is None else hook(value)
    return () if token is None else (token,)


def _mix_fwd(i, cur, cur_m, wl, hook=None, after=()):
    mixer = i % N_MIXERS
    if mixer == 0:
        y, s_mix = _sc_fwd(cur_m, wl["sc_w_in"], wl["sc_conv_w"], wl["sc_w_out"], f"l{i}_sc", after=after)
    elif mixer == 1:
        y, s_mix = _mla_fwd(cur_m, wl, f"l{i}_mla", after=after)
    else:
        y, s_mix = _cf_fwd(cur_m, wl, f"l{i}_cf", after=after)
    cur, cur_m, xh_mix, rs_mix = _ln_fwd(cur, y, wl["ln_mix_g"], wl["ln_mix_b"], name=f"l{i}_lnmix",
                                         after=_hooked(y, hook))
    return cur, cur_m, (s_mix, xh_mix, rs_mix)


def _ff_fwd(i, cur, cur_m, wl, hook=None, after=()):
    y, s_ff = _mlp_fwd(cur_m, wl["ff_w1"], wl["ff_w2"], f"l{i}_ff", after=after)
    cur, cur_m, xh_ff, rs_ff = _ln_fwd(cur, y, wl["ln_ff_g"], wl["ln_ff_b"], name=f"l{i}_lnff", after=_hooked(y, hook))
    return cur, cur_m, (s_ff, xh_ff, rs_ff)


def _layer_fwd(i, cur, cur_m, wl):
    cur, cur_m, s_mix = _mix_fwd(i, cur, cur_m, wl)
    cur, cur_m, s_ff = _ff_fwd(i, cur, cur_m, wl)
    return cur, cur_m, (*s_mix, *s_ff)


def _loss_and_grad(out, target):
    d = D_MODEL

    def loss_fn(rows, bc):
        err = rows[0] - rows[1]
        return (err * (1.0 / d),), (_colsum(err * err),)

    dy, sq = _rowwise(loss_fn, [out, target], [], [(d, F32)], [(1, d)], name="loss")
    return (0.5 / d) * jnp.sum(sq), dy


def _ff_bwd(i, terms, saved, wl, hook=None, after=()):
    s_ff, xh_ff, rs_ff = saved
    grads = {}
    dz, dz_m, grads["ln_ff_g"], grads["ln_ff_b"], _ = _ln_bwd(terms, xh_ff, rs_ff, wl["ln_ff_g"], name=f"l{i}_dlnff",
                                                              after=after)
    dxs, g_ff = _mlp_bwd(dz_m, s_ff, wl["ff_w1"], wl["ff_w2"], f"l{i}_ff", after=_hooked(dz, hook))
    grads.update(g_ff)
    return [(ALPHA, dz)] + [(1.0, v) for v in dxs], grads


def _mix_bwd(i, terms, saved, wl, hook=None, after=()):
    mixer = i % N_MIXERS
    s_mix, xh_mix, rs_mix = saved
    grads = {}
    dz, dz_m, grads["ln_mix_g"], grads["ln_mix_b"], dz_sum = _ln_bwd(terms, xh_mix, rs_mix, wl["ln_mix_g"],
                                                                     name=f"l{i}_dlnmix", after=after)
    behind = _hooked(dz, hook)
    if mixer == 0:
        dxs, g_mix = _sc_bwd(dz_m, s_mix, wl["sc_w_in"], wl["sc_conv_w"], wl["sc_w_out"], f"l{i}_sc", after=behind)
    elif mixer == 1:
        dxs, g_mix = _mla_bwd(dz_m, s_mix, wl, f"l{i}_mla", after=behind)
    else:
        dxs, g_mix = _cf_bwd(dz_m, dz_sum, s_mix, wl, f"l{i}_cf", after=behind)
    grads.update(g_mix)
    return [(ALPHA, dz)] + [(1.0, v) for v in dxs], grads


def _layer_bwd(i, terms, saved, wl):
    terms, g_ff = _ff_bwd(i, terms, saved[3:], wl)
    terms, g_mix = _mix_bwd(i, terms, saved[:3], wl)
    return terms, {**g_ff, **g_mix}


def _sum_terms(terms, *, name):
    coefs = [c for c, _ in terms]

    def sum_fn(rows, bc):
        acc = None
        for c, v in zip(coefs, rows):
            tv = v if c == 1.0 else c * v
            acc = tv if acc is None else acc + tv
        return (acc,), ()

    return _rowwise(sum_fn, [v for _, v in terms], [], [(D_MODEL, F32)], [], name=name)[0]


def _local_step(x, target, layers):
    cur, cur_m = x, _cast_input(x)
    saved = []
    for i in range(DEPTH):
        cur, cur_m, s = _layer_fwd(i, cur, cur_m, layers[i])
        saved.append(s)
    loss, dy = _loss_and_grad(cur, target)
    terms = [(1.0, dy)]
    grads = [None] * DEPTH
    for i in reversed(range(DEPTH)):
        terms, grads[i] = _layer_bwd(i, terms, saved[i], layers[i])
    return loss, _sum_terms(terms, name="grad_x"), grads


MESH_ID = pl.DeviceIdType.MESH
ANY = pl.BlockSpec(memory_space=pl.ANY)


def _block_id(px, py, pc):
    return 4 * px + 2 * py + pc


COPIES_PER_GATHER = 7


def _all_gather(shards, *, name):
    n = len(shards)

    def body(*refs):
        x_refs, out_refs = refs[:n], refs[n:2 * n]
        send_sems, recv_sems, local_sems = refs[2 * n:]
        x, y, core = lax.axis_index("x"), lax.axis_index("y"), lax.axis_index("c")
        me, sibling = (x, y, core), (x, y, 1 - core)
        chips = [(1 - x, y), (x, 1 - y), (1 - x, 1 - y)]

        def copy(t, k, block, to, from_input=False):
            dst = out_refs[t].at[_block_id(*block)]
            return pltpu.make_async_remote_copy(
                src_ref=x_refs[t] if from_input else dst, dst_ref=dst,
                send_sem=send_sems.at[t * COPIES_PER_GATHER + k], recv_sem=recv_sems.at[t * COPIES_PER_GATHER + k],
                device_id=to, device_id_type=MESH_ID)

        started = []
        mine = [pltpu.make_async_copy(x_refs[t], out_refs[t].at[_block_id(*me)], local_sems.at[t]) for t in range(n)]
        for t in range(n):
            mine[t].start()
            first = [copy(t, 0, me, sibling, from_input=True)]
            first += [copy(t, 1 + j, me, (*chip, core), from_input=True) for j, chip in enumerate(chips)]
            for cp in first:
                cp.start()
            started += first
        for t in range(n):
            for j, chip in enumerate(chips):
                copy(t, 1 + j, (*chip, core), me).wait_recv()
                passed = copy(t, 4 + j, (*chip, core), sibling)
                passed.start()
                started.append(passed)
        for t in range(n):
            copy(t, 0, sibling, me).wait_recv()
            for j, chip in enumerate(chips):
                copy(t, 4 + j, (*chip, 1 - core), me).wait_recv()
        for cp in started:
            cp.wait_send()
        for cp in mine:
            cp.wait()

    n_sems = n * COPIES_PER_GATHER
    outs = pl.pallas_call(
        body,
        name=name,
        out_shape=[jax.ShapeDtypeStruct((N_DEV, *s.shape), s.dtype) for s in shards],
        in_specs=[ANY] * n,
        out_specs=[ANY] * n,
        scratch_shapes=[pltpu.SemaphoreType.DMA((n_sems,)), pltpu.SemaphoreType.DMA((n_sems,)),
                        pltpu.SemaphoreType.DMA((n,))],
    )(*shards)
    return list(outs)


def _swap_with_sibling(parts, *, name):
    n = len(parts)

    def body(*refs):
        p_refs, got_refs = refs[:n], refs[n:2 * n]
        send_sems, recv_sems = refs[2 * n:]
        x, y, core = lax.axis_index("x"), lax.axis_index("y"), lax.axis_index("c")
        chips = [(x, y), (1 - x, y), (x, 1 - y), (1 - x, 1 - y)]
        copies = []
        for t in range(n):
            for k, (px, py) in enumerate(chips):
                copies.append(pltpu.make_async_remote_copy(
                    src_ref=p_refs[t].at[_block_id(px, py, 1 - core)], dst_ref=got_refs[t].at[k],
                    send_sem=send_sems.at[4 * t + k], recv_sem=recv_sems.at[4 * t + k], device_id=(x, y, 1 - core),
                    device_id_type=MESH_ID))
        for cp in copies:
            cp.start()
        for cp in copies:
            cp.wait()

    outs = pl.pallas_call(
        body,
        name=name,
        out_shape=[jax.ShapeDtypeStruct((4, *p.shape[1:]), p.dtype) for p in parts],
        in_specs=[ANY] * n,
        out_specs=[ANY] * n,
        scratch_shapes=[pltpu.SemaphoreType.DMA((4 * n,)), pltpu.SemaphoreType.DMA((4 * n,))],
    )(*parts)
    return list(outs)


def _swap_between_chips(parts, *, name):
    n = len(parts)

    def body(*refs):
        p_refs, got_refs = refs[:n], refs[n:2 * n]
        send_sems, recv_sems = refs[2 * n:]
        x, y, core = lax.axis_index("x"), lax.axis_index("y"), lax.axis_index("c")
        chips = [(1 - x, y), (x, 1 - y), (1 - x, 1 - y)]
        copies = []
        for t in range(n):
            for j, (px, py) in enumerate(chips):
                copies.append(pltpu.make_async_remote_copy(
                    src_ref=p_refs[t].at[j], dst_ref=got_refs[t].at[j], send_sem=send_sems.at[3 * t + j],
                    recv_sem=recv_sems.at[3 * t + j], device_id=(px, py, core), device_id_type=MESH_ID))
        for cp in copies:
            cp.start()
        for cp in copies:
            cp.wait()

    outs = pl.pallas_call(
        body,
        name=name,
        out_shape=[jax.ShapeDtypeStruct(p.shape, p.dtype) for p in parts],
        in_specs=[ANY] * n,
        out_specs=[ANY] * n,
        scratch_shapes=[pltpu.SemaphoreType.DMA((3 * n,)), pltpu.SemaphoreType.DMA((3 * n,))],
    )(*parts)
    return list(outs)


def _chip_partials(parts, got, *, name):
    _, r, c = parts.shape
    tb = _row_tile(r, 512)
    x, y, core = lax.axis_index("x"), lax.axis_index("y"), lax.axis_index("c")
    chips = [(x, y), (1 - x, y), (x, 1 - y), (1 - x, 1 - y)]
    ids = jnp.stack([_block_id(px, py, core) for px, py in chips]).astype(jnp.int32)

    def body(ids_ref, p0, p1, p2, p3, g_ref, own_ref, out_ref):
        own_ref[...] = p0[...].astype(F32) + g_ref[0].astype(F32)
        for j, p in enumerate((p1, p2, p3)):
            out_ref[j] = (p[...].astype(F32) + g_ref[j + 1].astype(F32)).astype(out_ref.dtype)

    def part_spec(k):
        return pl.BlockSpec((None, tb, c), lambda i, ids_ref: (ids_ref[k], i, 0))

    grid_spec = pltpu.PrefetchScalarGridSpec(
        num_scalar_prefetch=1,
        grid=(r // tb,),
        in_specs=[part_spec(k) for k in range(4)] + [pl.BlockSpec((4, tb, c), lambda i, ids_ref: (0, i, 0))],
        out_specs=[pl.BlockSpec((tb, c), lambda i, ids_ref: (i, 0)), pl.BlockSpec((3, tb, c), lambda i, ids_ref: (0, i, 0))],
    )
    return pl.pallas_call(
        body,
        name=name,
        grid_spec=grid_spec,
        out_shape=[jax.ShapeDtypeStruct((r, c), F32), jax.ShapeDtypeStruct((3, r, c), parts.dtype)],
        compiler_params=_params(("parallel",)),
    )(ids, parts, parts, parts, parts, got)


def _sum_blocks(first, blocks, *, name):
    n, r, c = blocks.shape
    tb = _row_tile(r, 512)

    def body(*refs):
        b_ref, o_ref = refs[-2], refs[-1]
        acc = refs[0][...] if first is not None else b_ref[0].astype(F32)
        for j in range(0 if first is not None else 1, n):
            acc = acc + b_ref[j].astype(F32)
        o_ref[...] = acc

    row_spec = pl.BlockSpec((tb, c), lambda i: (i, 0))
    return pl.pallas_call(
        body,
        name=name,
        grid=(r // tb,),
        in_specs=([row_spec] if first is not None else []) + [pl.BlockSpec((n, tb, c), lambda i: (0, i, 0))],
        out_specs=row_spec,
        out_shape=jax.ShapeDtypeStruct((r, c), F32),
        compiler_params=_params(("parallel",)),
    )(*([first] if first is not None else []), blocks)


def _reduce_scatter(parts, *, tag):
    got = _swap_with_sibling(parts, name=f"{tag}_sibling")
    sums = [_chip_partials(p, g, name=f"{tag}_chipsum{t}") for t, (p, g) in enumerate(zip(parts, got))]
    arrived = _swap_between_chips([others for _, others in sums], name=f"{tag}_chips")
    return [(own, arr) for (own, _), arr in zip(sums, arrived)]


HBM = pl.BlockSpec(memory_space=pltpu.HBM)
SEM = pl.BlockSpec(memory_space=pltpu.SEMAPHORE)
DATAFLOW = pltpu.SideEffectType.DATAFLOW_SIDE_EFFECTING


def _tie(value, *before):
    return lax.optimization_barrier((value, *before))[0]


def _split_call(name, arrays, sems, n_new, body, after=None):
    n_a, n_s = len(arrays), len(sems)
    n_after = 0 if after is None else 1

    def kernel_body(*refs):
        new = refs[n_a + n_s + n_after:n_a + n_s + n_after + 2] if n_new else (None, None)
        body(refs[:n_a], refs[n_a:n_a + n_s], *new)
        refs[-1][...] = jnp.zeros_like(refs[-1])

    n_sem_out = 2 if n_new else 0
    out_shape = [pltpu.SemaphoreType.DMA((n_new,))] * n_sem_out
    out_shape += [pltpu.HBM(a.shape, a.dtype) for a in arrays]
    out_shape += [jax.ShapeDtypeStruct((SUBLANES, LANES), F32)]
    outs = pl.pallas_call(
        kernel_body,
        name=name,
        out_shape=out_shape,
        in_specs=[HBM] * n_a + [SEM] * n_s + [ANY] * n_after,
        out_specs=[SEM] * n_sem_out + [HBM] * n_a + [pl.BlockSpec(memory_space=pltpu.VMEM)],
        input_output_aliases={i: n_sem_out + i for i in range(n_a)},
        compiler_params=pltpu.CompilerParams(has_side_effects=DATAFLOW),
    )(*[pltpu.with_memory_space_constraint(a, pltpu.HBM) for a in arrays], *sems, *([after] if n_after else []))
    return tuple(outs[:n_sem_out]), list(outs[n_sem_out:n_sem_out + n_a]), outs[-1]


def _remote(src, dst, send, recv, k, to):
    return pltpu.make_async_remote_copy(src_ref=src, dst_ref=dst, send_sem=send.at[k], recv_sem=recv.at[k],
                                        device_id=to, device_id_type=MESH_ID)


def _gather_start(shards, *, name, after=None):
    n = len(shards)
    me = _block_id(lax.axis_index("x"), lax.axis_index("y"), lax.axis_index("c"))
    lands = [lax.dynamic_update_index_in_dim(lax.empty((N_DEV, *s.shape), s.dtype), s, me, 0) for s in shards]

    def body(refs, _, send, recv):
        x, y, core = lax.axis_index("x"), lax.axis_index("y"), lax.axis_index("c")
        targets = [(x, y, 1 - core), (1 - x, y, core), (x, 1 - y, core), (1 - x, 1 - y, core)]
        for t in range(n):
            for k, to in enumerate(targets):
                _remote(refs[t], refs[n + t].at[_block_id(x, y, core)], send, recv, 4 * t + k, to).start()

    sems, thru, token = _split_call(name, shards + lands, [], 4 * n, body, after=after)
    return {"n": n, "sems1": sems, "arrays": thru, "token": token}


def _gather_pass_on(st, *, name, after):
    n = st["n"]

    def body(refs, sems, send, recv):
        send1, recv1 = sems
        x, y, core = lax.axis_index("x"), lax.axis_index("y"), lax.axis_index("c")
        chips = [(1 - x, y), (x, 1 - y), (1 - x, 1 - y)]
        for t in range(n):
            for j, (px, py) in enumerate(chips):
                block = refs[n + t].at[_block_id(px, py, core)]
                _remote(refs[t], block, send1, recv1, 4 * t + 1 + j, (x, y, 1 - core)).wait_recv()
                _remote(block, block, send, recv, 3 * t + j, (x, y, 1 - core)).start()

    sems2, thru, token = _split_call(name, st["arrays"], list(st["sems1"]), 3 * n, body, after=after)
    return {**st, "sems2": sems2, "arrays": thru, "token": token}


def _gather_finish(st, *, name, after):
    n = st["n"]

    def body(refs, sems, *_):
        send1, recv1, send2, recv2 = sems
        x, y, core = lax.axis_index("x"), lax.axis_index("y"), lax.axis_index("c")
        sibling = (x, y, 1 - core)
        chips = [(1 - x, y), (x, 1 - y), (1 - x, 1 - y)]
        for t in range(n):
            land = refs[n + t]
            for k in range(4):
                _remote(refs[t], land.at[_block_id(x, y, core)], send1, recv1, 4 * t + k, sibling).wait_send()
            _remote(refs[t], land.at[_block_id(x, y, 1 - core)], send1, recv1, 4 * t, sibling).wait_recv()
            for j, (px, py) in enumerate(chips):
                _remote(land.at[_block_id(px, py, core)], land.at[_block_id(px, py, core)], send2, recv2, 3 * t + j,
                        sibling).wait_send()
                _remote(refs[t], land.at[_block_id(px, py, 1 - core)], send2, recv2, 3 * t + j, sibling).wait_recv()

    _, thru, _ = _split_call(name, st["arrays"], [*st["sems1"], *st["sems2"]], 0, body, after=after)
    return thru[n:]


def _swap_start(sources, lands, plan, per_array, *, name):
    n = len(sources)

    def body(refs, _, send, recv):
        x, y, core = lax.axis_index("x"), lax.axis_index("y"), lax.axis_index("c")
        for t in range(n):
            for k in range(per_array):
                block, to = plan(x, y, core, k)
                _remote(refs[t].at[block], refs[n + t].at[k], send, recv, per_array * t + k, to).start()

    sems, thru, token = _split_call(name, sources + lands, [], per_array * n, body)
    return {"n": n, "sems": sems, "arrays": thru, "token": token, "plan": plan, "per_array": per_array}


def _swap_finish(st, *, name, after):
    n, plan, per_array = st["n"], st["plan"], st["per_array"]

    def body(refs, sems, *_):
        send, recv = sems
        x, y, core = lax.axis_index("x"), lax.axis_index("y"), lax.axis_index("c")
        for t in range(n):
            for k in range(per_array):
                block, to = plan(x, y, core, k)
                cp = _remote(refs[t].at[block], refs[n + t].at[k], send, recv, per_array * t + k, to)
                cp.wait_send()
                cp.wait_recv()

    _, thru, _ = _split_call(name, st["arrays"], list(st["sems"]), 0, body, after=after)
    return thru[:n], thru[n:]


def _to_sibling(x, y, core, k):
    px, py = [(x, y), (1 - x, y), (x, 1 - y), (1 - x, 1 - y)][k]
    return _block_id(px, py, 1 - core), (x, y, 1 - core)


def _to_chip(x, y, core, k):
    px, py = [(1 - x, y), (x, 1 - y), (1 - x, 1 - y)][k]
    return k, (px, py, core)


PARAMS = {
    "sc_w_in": ((2, 1024, 3072), 2), "sc_conv_w": ((2, 3, 1024), 2), "sc_w_out": ((2, 1024, 1024), 1),
    "mla_w_dq": ((1, 1024, 384), 1), "mla_g_q": ((1, 384), None), "mla_w_uq": ((1, 384, 1536), 2),
    "mla_w_dkv": ((1, 1024, 320), 1), "mla_g_kv": ((1, 256), None), "mla_w_uk": ((1, 256, 8, 128), 1),
    "mla_w_uv": ((1, 256, 8, 128), 1), "mla_w_o": ((1, 1024, 1024), 1), "cf_w_pw1": ((1, 1024, 2048), 2),
    "cf_b_pw1": ((1, 2048), 1), "cf_dw_w": ((1, 31, 1024), 2), "cf_dw_b": ((1, 1024), 1), "cf_norm_g": ((1, 1024), 1),
    "cf_norm_b": ((1, 1024), 1), "cf_w_pw2": ((1, 1024, 1024), 1), "cf_b_pw2": ((1, 1024), 1),
    "ff_w1": ((4, 1024, 4096), 2), "ff_w2": ((4, 4096, 1024), 1), "ln_mix_g": ((4, 1024), None),
    "ln_mix_b": ((4, 1024), None), "ln_ff_g": ((4, 1024), None), "ln_ff_b": ((4, 1024), None),
}
NAMES = list(PARAMS)
BIG = ["sc_w_in", "sc_w_out", "mla_w_dq", "mla_w_uq", "mla_w_dkv", "mla_w_uk", "mla_w_uv", "mla_w_o", "cf_w_pw1",
       "cf_w_pw2", "ff_w1", "ff_w2"]
SMALL = [n for n in NAMES if n not in BIG]
SMALL_SHARDED = [n for n in SMALL if PARAMS[n][1] is not None]
SMALL_WIDTH = LANES
DEV_BLOCKED = ("sc_w_in", "cf_w_pw1", "ff_w1")
MIXER_BIG = {0: ["sc_w_in", "sc_w_out"], 1: ["mla_w_dq", "mla_w_uq", "mla_w_dkv", "mla_w_uk", "mla_w_uv", "mla_w_o"],
             2: ["cf_w_pw1", "cf_w_pw2"]}


def _layer_big(i):
    return [(n, i // N_MIXERS) for n in MIXER_BIG[i % N_MIXERS]] + [("ff_w1", i), ("ff_w2", i)]


def _shard_2d(name, a):
    return a.reshape(a.shape[0], -1)


def _as_operand(name, g):
    if name in DEV_BLOCKED:
        return g
    if name == "mla_w_uq":
        by_head = g.transpose(1, 0, 2)
        pieces = [by_head[:, :, :QK_NOPE], by_head[:, :, QK_NOPE:QK_NOPE + ROPE_HALF], by_head[:, :, QK_NOPE + ROPE_HALF:]]
        return jnp.concatenate([p.reshape(g.shape[1], -1) for p in pieces], axis=1)
    return g.reshape(-1, g.shape[2])


def _as_parts(name, grad):
    if name in DEV_BLOCKED:
        return grad
    if name == "mla_w_uq":
        rows = grad.shape[0]
        pieces = [grad[:, :Q_NOPE_COLS].reshape(rows, MLA_HEADS, QK_NOPE),
                  grad[:, Q_NOPE_COLS:Q_NOPE_COLS + Q_HALF_COLS].reshape(rows, MLA_HEADS, ROPE_HALF),
                  grad[:, Q_NOPE_COLS + Q_HALF_COLS:].reshape(rows, MLA_HEADS, ROPE_HALF)]
        return jnp.concatenate(pieces, axis=2).transpose(1, 0, 2)
    return grad.reshape(N_DEV, -1, grad.shape[1])


def _shard_shape(name):
    shape, ax = PARAMS[name]
    if ax is None:
        return shape
    return tuple(s // N_DEV if i == ax else s for i, s in enumerate(shape))


def _rows_of(shape, width, align):
    n = 1
    for s in shape:
        n *= s
    rows = -(-n // width)
    return -(-rows // align) * align


def _pack_rows(arrays, width, align, dtype, lead=0):
    segs = []
    for a in arrays:
        batch = a.shape[:lead]
        flat = a.astype(dtype).reshape(*batch, -1)
        rows = _rows_of(a.shape[lead:], width, align)
        flat = jnp.pad(flat, [(0, 0)] * lead + [(0, rows * width - flat.shape[-1])])
        segs.append(flat.reshape(*batch, rows, width))
    return jnp.concatenate(segs, axis=lead)


def _unpack_rows(buf, shapes, width, align):
    out, off = [], 0
    lead = buf.shape[:-2]
    for shape in shapes:
        n = 1
        for s in shape:
            n *= s
        rows = _rows_of(shape, width, align)
        seg = buf[..., off:off + rows, :].reshape(*lead, rows * width)[..., :n]
        out.append(seg.reshape(*lead, *shape))
        off += rows
    return out


def _merge_shards(stacked, ax):
    moved = jnp.moveaxis(stacked, 0, ax)
    shape = moved.shape
    return moved.reshape(*shape[:ax], shape[ax] * shape[ax + 1], *shape[ax + 2:])


def _split_shards(full, ax):
    shape = full.shape
    split = full.reshape(*shape[:ax], N_DEV, shape[ax] // N_DEV, *shape[ax + 1:])
    return jnp.moveaxis(split, ax, 0)


def _adam_math(wv, gv, mv, vv):
    m_new = ADAM_B1 * mv + (1.0 - ADAM_B1) * gv
    v_new = ADAM_B2 * vv + (1.0 - ADAM_B2) * (gv * gv)
    m_hat = m_new / (1.0 - ADAM_B1 ** ADAM_STEP)
    v_hat = v_new / (1.0 - ADAM_B2 ** ADAM_STEP)
    delta = -ADAM_LR * (m_hat / (jnp.sqrt(v_hat) + ADAM_EPS) + ADAM_WD * wv)
    return delta, m_new, v_new


def _adamw(w, g, m, v, *, name):
    shape = w.shape
    c = shape[-1]

    def fn(rows, bc):
        return _adam_math(*rows), ()

    flat = [a.reshape(-1, c) for a in (w, g, m, v)]
    outs = _rowwise(fn, flat, [], [(c, F32)] * 3, [], name=name, tb_cap=512)
    return tuple(o.reshape(shape) for o in outs)


def _adamw_reduced(w, m, v, layer_grads, *, name, after=()):
    n_l, r, c = w.shape
    tb = _row_tile(r, 256 if c > 512 else 512)
    nb = r // tb

    def body(*refs):
        w_ref, m_ref, v_ref = refs[:3]
        g_refs = refs[3:3 + 2 * n_l]
        g_out, d_out, m_out, v_out = refs[3 + 2 * n_l + len(after):]
        layer = pl.program_id(0)
        for l in range(n_l):
            @pl.when(layer == l)
            def _(l=l):
                own, arr = g_refs[2 * l], g_refs[2 * l + 1]
                g = own[...] + arr[0].astype(F32)
                g = g + arr[1].astype(F32)
                g = g + arr[2].astype(F32)
                delta, m_new, v_new = _adam_math(w_ref[...], g, m_ref[...], v_ref[...])
                g_out[...] = g
                d_out[...] = delta
                m_out[...] = m_new
                v_out[...] = v_new

    stacked = pl.BlockSpec((tb, c), lambda l, i: (l * nb + i, 0))
    in_specs = [stacked] * 3
    operands = [a.reshape(n_l * r, c) for a in (w, m, v)]
    for l, (own, arr) in enumerate(layer_grads):
        in_specs.append(pl.BlockSpec((tb, c), lambda ll, i, l=l: (jnp.where(ll == l, i, 0), 0)))
        in_specs.append(pl.BlockSpec((3, tb, c), lambda ll, i, l=l: (0, jnp.where(ll == l, i, 0), 0)))
        operands += [own, arr]
    in_specs += [ANY] * len(after)
    operands += list(after)
    outs = pl.pallas_call(
        body,
        name=name,
        grid=(n_l, nb),
        in_specs=in_specs,
        out_specs=[stacked] * 4,
        out_shape=[jax.ShapeDtypeStruct((n_l * r, c), F32)] * 4,
        compiler_params=_params(("arbitrary", "arbitrary")),
    )(*operands)
    return tuple(o.reshape(n_l, r, c) for o in outs)


def _train_step(x, target, weights, m_state, v_state):
    me = _block_id(lax.axis_index("x"), lax.axis_index("y"), lax.axis_index("c"))

    small_shapes = [_shard_shape(n) for n in SMALL_SHARDED]
    gather = {0: _gather_start([_pack_rows([weights[n] for n in SMALL_SHARDED], SMALL_WIDTH, SUBLANES, F32)]
                               + [_shard_2d(n, weights[n][0]).astype(MXU_DTYPE) for n in MIXER_BIG[0]],
                               name="gather_g0_start")}
    cur, cur_m = x[0], _cast_input(x[0])
    gather[0] = _gather_pass_on(gather[0], name="gather_g0_pass", after=cur_m)
    first = _gather_finish(gather[0], name="gather_g0_finish", after=gather[0]["token"])
    small_all, first_matrices = first[0], first[1:]
    small_full = {n: weights[n] for n in SMALL if PARAMS[n][1] is None}
    for n, stacked in zip(SMALL_SHARDED, _unpack_rows(small_all, small_shapes, SMALL_WIDTH, SUBLANES)):
        small_full[n] = _merge_shards(stacked, PARAMS[n][1])

    def small_weights(i):
        wl = {}
        for n in SMALL:
            if n.startswith("ln_"):
                wl[n] = small_full[n][i:i + 1]
            elif n.startswith(("sc_", "mla_", "cf_")[i % N_MIXERS]):
                v = small_full[n][i // N_MIXERS]
                wl[n] = v if v.ndim == 2 else v.reshape(1, -1)
        return wl

    layers = [small_weights(i) for i in range(DEPTH)]
    groups = [(kind, i) for i in range(DEPTH) for kind in ("mix", "ff")]

    def group_names(g):
        kind, i = g
        if kind == "mix":
            return [(n, i // N_MIXERS) for n in MIXER_BIG[i % N_MIXERS]]
        return [("ff_w1", i), ("ff_w2", i)]

    def shards_of(g):
        return [_shard_2d(n, weights[n][j]).astype(MXU_DTYPE) for n, j in group_names(g)]

    def install(g, gathered):
        for (n, _), a in zip(group_names(g), gathered):
            layers[g[1]][n] = _as_operand(n, a)

    install(groups[0], first_matrices)
    gather[1] = _gather_start(shards_of(groups[1]), name="gather_g1_start", after=gather[0]["token"])
    saved = {}
    for k, g in enumerate(groups):
        kind, i = g
        tokens = [gather[1]["token"]] if k == 0 else []
        if k + 2 < len(groups):
            gather[k + 2] = _gather_start(shards_of(groups[k + 2]), name=f"gather_g{k + 2}_start",
                                          after=gather[k + 1]["token"])
            tokens.append(gather[k + 2]["token"])

        def pass_on(y, k=k):
            if k + 1 == len(groups):
                return None
            gather[k + 1] = _gather_pass_on(gather[k + 1], name=f"gather_g{k + 1}_pass", after=y)
            return gather[k + 1]["token"]

        phase = _mix_fwd if kind == "mix" else _ff_fwd
        cur, cur_m, saved[g] = phase(i, cur, cur_m, layers[i], hook=pass_on, after=tokens)
        if k + 1 < len(groups):
            install(groups[k + 1], _gather_finish(gather[k + 1], name=f"gather_g{k + 1}_finish", after=cur))

    loss_local, dy = _loss_and_grad(cur, target[0])
    loss = lax.psum(loss_local, MESH_AXES)

    reduced = {n: [None] * PARAMS[n][0][0] for n in BIG}
    small_grads = {n: [None] * PARAMS[n][0][0] for n in SMALL}
    reduce = {}

    def to_chips(p, after):
        st = reduce[p]
        parts, got = _swap_finish(st["sibling"], name=f"reduce_p{p}_sibling_finish", after=after)
        sums = [_chip_partials(a, b, name=f"reduce_p{p}_chipsum{t}") for t, (a, b) in enumerate(zip(parts, got))]
        st["own"] = [own for own, _ in sums]
        others = [o for _, o in sums]
        st["chips"] = _swap_start(others, [lax.empty(o.shape, o.dtype) for o in others], _to_chip, 3,
                                  name=f"reduce_p{p}_chips_start")
        return st["chips"]["token"]

    def arrived(p, after):
        st = reduce[p]
        _, got = _swap_finish(st["chips"], name=f"reduce_p{p}_chips_finish", after=after)
        for (n, j), own, arr in zip(st["names"], st["own"], got):
            reduced[n][j] = (own, arr)

    terms = [(1.0, dy)]
    back = list(reversed(groups))
    for p, g in enumerate(back):
        kind, i = g

        def after_first_kernel(dz, p=p):
            return to_chips(p - 1, dz) if p >= 1 else None

        phase = _mix_bwd if kind == "mix" else _ff_bwd
        tokens = [reduce[p - 1]["sibling"]["token"]] if p >= 1 else []
        terms, grads = phase(i, terms, saved[g], layers[i], hook=after_first_kernel, after=tokens)
        for n in SMALL:
            if n in grads:
                small_grads[n][i if n.startswith("ln_") else i // N_MIXERS] = grads[n]
        if p >= 2:
            arrived(p - 2, terms[-1][1])
        names = group_names(g)
        parts = [_as_parts(n, grads[n]) for n, _ in names]
        reduce[p] = {"names": names, "sibling": _swap_start(
            parts, [lax.empty((4, *a.shape[1:]), a.dtype) for a in parts], _to_sibling, 4,
            name=f"reduce_p{p}_sibling_start")}
    grad_x = _sum_terms(terms, name="grad_x")
    last = len(back) - 1
    small_mine = _pack_rows([jnp.stack(small_grads[n]).reshape(PARAMS[n][0]) for n in SMALL], SMALL_WIDTH, SUBLANES, F32)
    small_gather = _gather_start([small_mine], name="gather_small_grads_start")
    tail = to_chips(last, small_gather["token"])
    arrived(last - 1, tail)

    grad, delta, new_m, new_v = {}, {}, {}, {}

    def adamw_matrix(n, after):
        shape = weights[n].shape
        view = lambda a: a.reshape(shape[0], shape[1], -1)
        outs = _adamw_reduced(view(weights[n]), view(m_state[n]), view(v_state[n]), reduced[n], name=f"adamw_{n}",
                              after=after)
        grad[n], delta[n], new_m[n], new_v[n] = [o.reshape(shape) for o in outs]
        return outs[0]

    done = tail
    for n in BIG:
        if n not in MIXER_BIG[0]:
            done = adamw_matrix(n, [done])
    small_gather = _gather_pass_on(small_gather, name="gather_small_grads_pass", after=done)
    small_all = _gather_finish(small_gather, name="gather_small_grads_finish", after=small_gather["token"])[0]
    small_sum = _sum_blocks(None, small_all, name="sum_small_grads")
    for n, g in zip(SMALL, _unpack_rows(small_sum, [PARAMS[n][0] for n in SMALL], SMALL_WIDTH, SUBLANES)):
        ax = PARAMS[n][1]
        grad[n] = g if ax is None else lax.dynamic_index_in_dim(_split_shards(g, ax), me, axis=0, keepdims=False)
        delta[n], new_m[n], new_v[n] = _adamw(weights[n], grad[n], m_state[n], v_state[n], name=f"adamw_{n}")
    arrived(last, small_sum)
    for n in MIXER_BIG[0]:
        adamw_matrix(n, [])
    return (loss, grad_x[None], *[grad[n] for n in NAMES], *[delta[n] for n in NAMES],
            *[new_m[n] for n in NAMES], *[new_v[n] for n in NAMES])


def kernel(x, sc_w_in, sc_conv_w, sc_w_out, mla_w_dq, mla_g_q, mla_w_uq, mla_w_dkv, mla_g_kv, mla_w_uk, mla_w_uv, mla_w_o, cf_w_pw1, cf_b_pw1, cf_dw_w, cf_dw_b, cf_norm_g, cf_norm_b, cf_w_pw2, cf_b_pw2, ff_w1, ff_w2, ln_mix_g, ln_mix_b, ln_ff_g, ln_ff_b, loss_target, m_sc_w_in, m_sc_conv_w, m_sc_w_out, m_mla_w_dq, m_mla_g_q, m_mla_w_uq, m_mla_w_dkv, m_mla_g_kv, m_mla_w_uk, m_mla_w_uv, m_mla_w_o, m_cf_w_pw1, m_cf_b_pw1, m_cf_dw_w, m_cf_dw_b, m_cf_norm_g, m_cf_norm_b, m_cf_w_pw2, m_cf_b_pw2, m_ff_w1, m_ff_w2, m_ln_mix_g, m_ln_mix_b, m_ln_ff_g, m_ln_ff_b, v_sc_w_in, v_sc_conv_w, v_sc_w_out, v_mla_w_dq, v_mla_g_q, v_mla_w_uq, v_mla_w_dkv, v_mla_g_kv, v_mla_w_uk, v_mla_w_uv, v_mla_w_o, v_cf_w_pw1, v_cf_b_pw1, v_cf_dw_w, v_cf_dw_b, v_cf_norm_g, v_cf_norm_b, v_cf_w_pw2, v_cf_b_pw2, v_ff_w1, v_ff_w2, v_ln_mix_g, v_ln_mix_b, v_ln_ff_g, v_ln_ff_b):
    w_list = (sc_w_in, sc_conv_w, sc_w_out, mla_w_dq, mla_g_q, mla_w_uq, mla_w_dkv, mla_g_kv, mla_w_uk, mla_w_uv, mla_w_o,
              cf_w_pw1, cf_b_pw1, cf_dw_w, cf_dw_b, cf_norm_g, cf_norm_b, cf_w_pw2, cf_b_pw2, ff_w1, ff_w2, ln_mix_g,
              ln_mix_b, ln_ff_g, ln_ff_b)
    m_list = (m_sc_w_in, m_sc_conv_w, m_sc_w_out, m_mla_w_dq, m_mla_g_q, m_mla_w_uq, m_mla_w_dkv, m_mla_g_kv, m_mla_w_uk,
              m_mla_w_uv, m_mla_w_o, m_cf_w_pw1, m_cf_b_pw1, m_cf_dw_w, m_cf_dw_b, m_cf_norm_g, m_cf_norm_b, m_cf_w_pw2,
              m_cf_b_pw2, m_ff_w1, m_ff_w2, m_ln_mix_g, m_ln_mix_b, m_ln_ff_g, m_ln_ff_b)
    v_list = (v_sc_w_in, v_sc_conv_w, v_sc_w_out, v_mla_w_dq, v_mla_g_q, v_mla_w_uq, v_mla_w_dkv, v_mla_g_kv, v_mla_w_uk,
              v_mla_w_uv, v_mla_w_o, v_cf_w_pw1, v_cf_b_pw1, v_cf_dw_w, v_cf_dw_b, v_cf_norm_g, v_cf_norm_b, v_cf_w_pw2,
              v_cf_b_pw2, v_ff_w1, v_ff_w2, v_ln_mix_g, v_ln_mix_b, v_ln_ff_g, v_ln_ff_b)
    return _train_step(x, loss_target, dict(zip(NAMES, w_list)), dict(zip(NAMES, m_list)), dict(zip(NAMES, v_list)))
```

```python
import functools

import jax
import jax.numpy as jnp
from jax import lax
from jax.experimental import pallas as pl
from jax.experimental.pallas import tpu as pltpu

F32 = jnp.float32
MXU_DTYPE = jnp.bfloat16

N_DEV = 8
D_MODEL = 1024
DEPTH = 4
N_MIXERS = 3
CHUNK = 64
ALPHA = (2.0 * DEPTH) ** 0.25
LN_EPS = 1e-5
RMS_EPS = 1e-6
MLA_HEADS = 8
QK_NOPE = 128
QK_ROPE = 64
V_HEAD = 128
Q_LORA = 384
KV_LORA = 256
ROPE_THETA = 10000.0
ADAM_LR = 0.001
ADAM_B1 = 0.9
ADAM_B2 = 0.999
ADAM_EPS = 1e-08
ADAM_WD = 0.01
ADAM_STEP = 10

LANES = 128
SUBLANES = 8
BF16_ROWS = 16
VMEM_LIMIT = 56 * 1024 * 1024
MM_VMEM_BUDGET = 36 * 1024 * 1024
CONV_PAD = 32
MASK_VALUE = -1e30

MESH_AXES = ("x", "y", "c")


def _params(semantics):
    return pltpu.CompilerParams(dimension_semantics=semantics, vmem_limit_bytes=VMEM_LIMIT)


def _tile(n, cap):
    if n <= cap:
        return n
    t = cap - cap % LANES
    while t >= LANES:
        if n % t == 0:
            return t
        t -= LANES
    raise ValueError(f"no tile for {n} under {cap}")


def _row_tile(rows, cap):
    if rows <= cap:
        return rows
    t = cap - cap % SUBLANES
    while t >= SUBLANES:
        if rows % t == 0:
            return t
        t -= SUBLANES
    raise ValueError(f"no row tile for {rows} under {cap}")


def _mm(a, b, *, name, ta=False, tb=False, b_dev=False, out_dev=False, out_dtypes=(F32,), epilogue=None,
        tile_extras=(), col_extras=(), after=(), tm_cap=2048, tn_cap=1024, tk_cap=4096):
    m, k = (a.shape[1], a.shape[0]) if ta else a.shape
    if b_dev:
        nd, b_rows, ns = b.shape
        n, kb = (b_rows, nd * ns) if tb else (nd * ns, b_rows)
    else:
        n, kb = (b.shape[0], b.shape[1]) if tb else (b.shape[1], b.shape[0])
    assert k == kb, (a.shape, b.shape, ta, tb)
    if ta or tile_extras or (b_dev and tb):
        tm_cap = min(tm_cap, 1024)
    tk = ns if (b_dev and tb) else _tile(k, tk_cap)
    nk = k // tk
    fixed_tn = ns if (b_dev and not tb) else (n // N_DEV if out_dev else None)
    if out_dev:
        assert not tile_extras and not col_extras and n % N_DEV == 0
    out_bytes = sum(jnp.dtype(dt).itemsize for dt in out_dtypes) + sum(e.dtype.itemsize for e in tile_extras)
    tm, tn = None, None
    for cand_m, cand_n in ((tm_cap, tn_cap), (tm_cap, tn_cap // 2), (tm_cap // 2, tn_cap), (tm_cap // 2, tn_cap // 2),
                           (tm_cap // 4, tn_cap // 2), (tm_cap // 4, tn_cap // 4)):
        tm, tn = _tile(m, cand_m), (fixed_tn if fixed_tn is not None else _tile(n, cand_n))
        need = 2 * (tm * tk * a.dtype.itemsize + tk * tn * b.dtype.itemsize) + 2 * tm * tn * out_bytes
        need += tm * tn * 4 * (2 if nk > 1 else 1)
        if need <= MM_VMEM_BUDGET:
            break
    n_te, n_ce, n_out = len(tile_extras), len(col_extras), len(out_dtypes)
    dims = (((0 if ta else 1,), (1 if tb else 0,)), ((), ()))

    def body(*refs):
        a_ref, b_ref = refs[0], refs[1]
        te_refs = refs[2:2 + n_te]
        ce_refs = refs[2 + n_te:2 + n_te + n_ce]
        first_out = 2 + n_te + n_ce + len(after)
        out_refs = refs[first_out:first_out + n_out]

        def finish(acc):
            if epilogue is None:
                outs = (acc,)
            else:
                outs = epilogue(acc, *[r[...] for r in te_refs], *[r[...] for r in ce_refs])
            for o_ref, o in zip(out_refs, outs):
                o_ref[...] = o.astype(o_ref.dtype)

        def product():
            return lax.dot_general(a_ref[...].astype(MXU_DTYPE), b_ref[...].astype(MXU_DTYPE), dims,
                                   preferred_element_type=F32)

        if nk == 1:
            finish(product())
        else:
            acc_ref = refs[-1]
            kk = pl.program_id(2)

            @pl.when(kk == 0)
            def _():
                acc_ref[...] = jnp.zeros_like(acc_ref)

            acc_ref[...] += product()

            @pl.when(kk == nk - 1)
            def _():
                finish(acc_ref[...])

    a_spec = pl.BlockSpec((tk, tm), lambda i, j, kk: (kk, i)) if ta else pl.BlockSpec((tm, tk), lambda i, j, kk: (i, kk))
    if b_dev and tb:
        b_spec = pl.BlockSpec((None, tn, tk), lambda i, j, kk: (kk, j, 0))
    elif b_dev:
        b_spec = pl.BlockSpec((None, tk, tn), lambda i, j, kk: (j, kk, 0))
    elif tb:
        b_spec = pl.BlockSpec((tn, tk), lambda i, j, kk: (j, kk))
    else:
        b_spec = pl.BlockSpec((tk, tn), lambda i, j, kk: (kk, j))
    tile_spec = pl.BlockSpec((tm, tn), lambda i, j, kk: (i, j))
    col_spec = pl.BlockSpec((1, tn), lambda i, j, kk: (0, j))
    if out_dev:
        out_spec = pl.BlockSpec((None, tm, tn), lambda i, j, kk: (j, i, 0))
        out_shape = [jax.ShapeDtypeStruct((N_DEV, m, tn), dt) for dt in out_dtypes]
    else:
        out_spec = tile_spec
        out_shape = [jax.ShapeDtypeStruct((m, n), dt) for dt in out_dtypes]
    outs = pl.pallas_call(
        body,
        name=name,
        grid=(m // tm, n // tn, nk),
        in_specs=[a_spec, b_spec] + [tile_spec] * n_te + [col_spec] * n_ce + [ANY] * len(after),
        out_specs=[out_spec] * n_out,
        out_shape=out_shape,
        scratch_shapes=[pltpu.VMEM((tm, tn), F32)] if nk > 1 else [],
        compiler_params=_params(("parallel", "parallel", "arbitrary")),
    )(a, b, *tile_extras, *col_extras, *after)
    return outs[0] if n_out == 1 else tuple(outs)


def _rowwise(fn, rows, bcast, out_rows, out_accs, *, name, tb_cap=256, after=()):
    t = rows[0].shape[0]
    tb = _row_tile(t, tb_cap)
    n_r, n_b, n_o, n_a = len(rows), len(bcast), len(out_rows), len(out_accs)

    def body(*refs):
        r_refs = refs[:n_r]
        b_refs = refs[n_r:n_r + n_b]
        first_out = n_r + n_b + len(after)
        o_refs = refs[first_out:first_out + n_o]
        a_refs = refs[first_out + n_o:]
        outs, accs = fn([r[...] for r in r_refs], [r[...] for r in b_refs])
        for o_ref, o in zip(o_refs, outs):
            o_ref[...] = o.astype(o_ref.dtype)
        if n_a:
            i = pl.program_id(0)

            @pl.when(i == 0)
            def _():
                for a_ref, acc in zip(a_refs, accs):
                    a_ref[...] = acc

            @pl.when(i > 0)
            def _():
                for a_ref, acc in zip(a_refs, accs):
                    a_ref[...] += acc

    in_specs = [pl.BlockSpec((tb, r.shape[1]), lambda i: (i, 0)) for r in rows]
    in_specs += [pl.BlockSpec(b.shape, lambda i: (0, 0)) for b in bcast]
    in_specs += [ANY] * len(after)
    out_specs = [pl.BlockSpec((tb, c), lambda i: (i, 0)) for c, _ in out_rows]
    out_specs += [pl.BlockSpec(s, lambda i: (0, 0)) for s in out_accs]
    out_shape = [jax.ShapeDtypeStruct((t, c), dt) for c, dt in out_rows]
    out_shape += [jax.ShapeDtypeStruct(s, F32) for s in out_accs]
    outs = pl.pallas_call(
        body,
        name=name,
        grid=(t // tb,),
        in_specs=in_specs,
        out_specs=out_specs,
        out_shape=out_shape,
        compiler_params=_params(("arbitrary",)),
    )(*rows, *bcast, *after)
    return tuple(outs)


def _colsum(v):
    return jnp.sum(v, axis=0, keepdims=True)


def _sigmoid(v):
    return 1.0 / (1.0 + jnp.exp(-v))


def _ln_fwd(x_res, y_sub, g, b, *, name, after=()):
    d = x_res.shape[1]

    def fn(rows, bc):
        z = ALPHA * rows[0] + rows[1]
        mu = jnp.mean(z, axis=-1, keepdims=True)
        zc = z - mu
        var = jnp.mean(zc * zc, axis=-1, keepdims=True)
        rstd = lax.rsqrt(var + LN_EPS)
        xhat = zc * rstd
        out = xhat * bc[0] + bc[1]
        return (out, out, xhat, rstd), ()

    return _rowwise(fn, [x_res, y_sub], [g, b], [(d, F32), (d, MXU_DTYPE), (d, F32), (1, F32)], [], name=name,
                    after=after, tb_cap=512)


def _ln_bwd(terms, xhat, rstd, g, *, name, after=()):
    d = xhat.shape[1]
    coefs = [c for c, _ in terms]

    def fn(rows, bc):
        xh, rs = rows[0], rows[1]
        dout = None
        for c, v in zip(coefs, rows[2:]):
            tv = v if c == 1.0 else c * v
            dout = tv if dout is None else dout + tv
        dxh = dout * bc[0]
        m1 = jnp.mean(dxh, axis=-1, keepdims=True)
        m2 = jnp.mean(dxh * xh, axis=-1, keepdims=True)
        dz = rs * (dxh - m1 - xh * m2)
        return (dz, dz), (_colsum(dout * xh), _colsum(dout), _colsum(dz))

    return _rowwise(fn, [xhat, rstd] + [v for _, v in terms], [g], [(d, F32), (d, MXU_DTYPE)],
                    [(1, d), (1, d), (1, d)], name=name, after=after, tb_cap=512)


def _rms_fwd(x, g, *, name):
    c = x.shape[1]

    def fn(rows, bc):
        v = rows[0]
        r = lax.rsqrt(jnp.mean(v * v, axis=-1, keepdims=True) + RMS_EPS)
        return (v * r * bc[0],), ()

    return _rowwise(fn, [x], [g], [(c, MXU_DTYPE)], [], name=name)[0]


def _rms_bwd(x, g, dy, *, name):
    c = x.shape[1]

    def fn(rows, bc):
        v, dyv = rows
        r = lax.rsqrt(jnp.mean(v * v, axis=-1, keepdims=True) + RMS_EPS)
        xh = v * r
        dxh = dyv * bc[0]
        dx = r * (dxh - xh * jnp.mean(dxh * xh, axis=-1, keepdims=True))
        return (dx,), (_colsum(dyv * xh),)

    return _rowwise(fn, [x, dy], [g], [(c, F32)], [(1, c)], name=name)


def _conv_chunk(t):
    return _row_tile(t, 256)


def _conv_fwd(x, w, bias, *, name):
    t, c = x.shape
    kw = w.shape[0]
    ch = _conv_chunk(t)

    def body(x_ref, w_ref, b_ref, o_ref, pad_ref):
        pad_ref[0:CONV_PAD, :] = jnp.zeros((CONV_PAD, LANES), F32)
        pad_ref[CONV_PAD:CONV_PAD + t, :] = x_ref[...]
        for t0 in range(0, t, ch):
            acc = jnp.broadcast_to(b_ref[...], (ch, LANES))
            for k in range(kw):
                s = kw - 1 - k
                acc = acc + w_ref[k:k + 1, :] * pad_ref[CONV_PAD - s + t0:CONV_PAD - s + t0 + ch, :]
            o_ref[t0:t0 + ch, :] = acc

    return pl.pallas_call(
        body,
        name=name,
        grid=(c // LANES,),
        in_specs=[pl.BlockSpec((t, LANES), lambda j: (0, j)), pl.BlockSpec((kw, LANES), lambda j: (0, j)),
                  pl.BlockSpec((1, LANES), lambda j: (0, j))],
        out_specs=pl.BlockSpec((t, LANES), lambda j: (0, j)),
        out_shape=jax.ShapeDtypeStruct((t, c), F32),
        scratch_shapes=[pltpu.VMEM((CONV_PAD + t, LANES), F32)],
        compiler_params=_params(("parallel",)),
    )(x, w, bias)


def _conv_bwd(dy, x, w, *, name):
    t, c = x.shape
    kw = w.shape[0]
    ch = _conv_chunk(t)

    def body(dy_ref, x_ref, w_ref, dx_ref, dw_ref, xpad_ref, dpad_ref):
        xpad_ref[0:CONV_PAD, :] = jnp.zeros((CONV_PAD, LANES), F32)
        xpad_ref[CONV_PAD:CONV_PAD + t, :] = x_ref[...]
        dpad_ref[0:t, :] = dy_ref[...]
        dpad_ref[t:t + CONV_PAD, :] = jnp.zeros((CONV_PAD, LANES), F32)
        for t0 in range(0, t, ch):
            acc = jnp.zeros((ch, LANES), F32)
            for k in range(kw):
                s = kw - 1 - k
                acc = acc + w_ref[k:k + 1, :] * dpad_ref[t0 + s:t0 + s + ch, :]
            dx_ref[t0:t0 + ch, :] = acc
        for k in range(kw):
            s = kw - 1 - k
            acc = jnp.zeros((ch, LANES), F32)
            for t0 in range(0, t, ch):
                acc = acc + dy_ref[t0:t0 + ch, :] * xpad_ref[CONV_PAD - s + t0:CONV_PAD - s + t0 + ch, :]
            dw_ref[k:k + 1, :] = _colsum(acc)

    blk = pl.BlockSpec((t, LANES), lambda j: (0, j))
    wblk = pl.BlockSpec((kw, LANES), lambda j: (0, j))
    return pl.pallas_call(
        body,
        name=name,
        grid=(c // LANES,),
        in_specs=[blk, blk, wblk],
        out_specs=[blk, wblk],
        out_shape=[jax.ShapeDtypeStruct((t, c), F32), jax.ShapeDtypeStruct((kw, c), F32)],
        scratch_shapes=[pltpu.VMEM((CONV_PAD + t, LANES), F32), pltpu.VMEM((CONV_PAD + t, LANES), F32)],
        compiler_params=_params(("parallel",)),
    )(dy, x, w)


ATTN_SCALE = (QK_NOPE + QK_ROPE) ** -0.5
_NT = (((1,), (1,)), ((), ()))
_TN = (((0,), (0,)), ((), ()))
_NN = (((1,), (0,)), ((), ()))


def _dot(a, b, dims):
    return lax.dot_general(a, b, dims, preferred_element_type=F32)


def _block_scores(qn, qp, kn_ref, kp_ref, lo, hi, diagonal):
    s = _dot(qn, kn_ref[lo:hi, :], _NT) + _dot(qp, kp_ref[lo:hi, :], _NT)
    s = s * ATTN_SCALE
    if diagonal:
        shift = CHUNK.bit_length() - 1
        rows = lax.broadcasted_iota(jnp.int32, s.shape, 0)
        cols = lax.broadcasted_iota(jnp.int32, s.shape, 1)
        s = jnp.where(jnp.right_shift(cols, shift) <= jnp.right_shift(rows, shift), s, MASK_VALUE)
    return s


def _attn_fwd(qn, qp, kn, kp, v, *, name):
    h, t, _ = qp.shape
    tq = _row_tile(t, 256)
    assert tq % CHUNK == 0

    def body(qn_ref, qp_ref, kn_ref, kp_ref, v_ref, o_ref, of_ref, lse_ref):
        for r0 in range(0, t, tq):
            r1 = r0 + tq
            qn, qp = qn_ref[r0:r1, :].astype(MXU_DTYPE), qp_ref[r0:r1, :]
            s_d = _block_scores(qn, qp, kn_ref, kp_ref, r0, r1, True)
            m = jnp.max(s_d, axis=-1, keepdims=True)
            if r0:
                s_b = _block_scores(qn, qp, kn_ref, kp_ref, 0, r0, False)
                m = jnp.maximum(m, jnp.max(s_b, axis=-1, keepdims=True))
            p_d = jnp.exp(s_d - m)
            l = jnp.sum(p_d, axis=-1, keepdims=True)
            acc = _dot(p_d.astype(MXU_DTYPE), v_ref[r0:r1, :], _NN)
            if r0:
                p_b = jnp.exp(s_b - m)
                l = l + jnp.sum(p_b, axis=-1, keepdims=True)
                acc = acc + _dot(p_b.astype(MXU_DTYPE), v_ref[0:r0, :], _NN)
            o = acc / l
            o_ref[r0:r1, :] = o.astype(o_ref.dtype)
            of_ref[r0:r1, :] = o
            lse_ref[r0:r1, :] = m + jnp.log(l)

    blk = lambda w: pl.BlockSpec((None, t, w), lambda hh: (hh, 0, 0))
    cols = pl.BlockSpec((t, QK_NOPE), lambda hh: (0, hh))
    return pl.pallas_call(
        body,
        name=name,
        grid=(h,),
        in_specs=[cols, blk(QK_ROPE), cols, pl.BlockSpec((t, QK_ROPE), lambda hh: (0, 0)), cols],
        out_specs=[cols, cols, blk(1)],
        out_shape=[jax.ShapeDtypeStruct((t, h * V_HEAD), MXU_DTYPE), jax.ShapeDtypeStruct((t, h * V_HEAD), F32),
                   jax.ShapeDtypeStruct((h, t, 1), F32)],
        compiler_params=_params(("parallel",)),
    )(qn, qp, kn, kp, v)


def _attn_bwd(qn, qp, kn, kp, v, do, o, lse, *, name):
    h, t, _ = qp.shape
    tq = _row_tile(t, 256)

    def body(qn_ref, qp_ref, kn_ref, kp_ref, v_ref, do_ref, o_ref, lse_ref, dqn_ref, dqp_ref, dkn_ref, dkp_ref,
             dv_ref):
        dkn_ref[...] = jnp.zeros_like(dkn_ref)
        dv_ref[...] = jnp.zeros_like(dv_ref)

        @pl.when(pl.program_id(0) == 0)
        def _():
            dkp_ref[...] = jnp.zeros_like(dkp_ref)

        for r0 in range(0, t, tq):
            r1 = r0 + tq
            qn, qp = qn_ref[r0:r1, :].astype(MXU_DTYPE), qp_ref[r0:r1, :]
            dov, lse = do_ref[r0:r1, :], lse_ref[r0:r1, :]
            delta = jnp.sum(dov.astype(F32) * o_ref[r0:r1, :], axis=-1, keepdims=True)

            def piece(lo, hi, diagonal):
                s = _block_scores(qn, qp, kn_ref, kp_ref, lo, hi, diagonal)
                p = jnp.exp(s - lse)
                dp = _dot(dov, v_ref[lo:hi, :], _NT)
                ds_m = (p * (dp - delta) * ATTN_SCALE).astype(MXU_DTYPE)
                dkn_ref[lo:hi, :] += _dot(ds_m, qn, _TN)
                dkp_ref[lo:hi, :] += _dot(ds_m, qp, _TN)
                dv_ref[lo:hi, :] += _dot(p.astype(MXU_DTYPE), dov, _TN)
                return _dot(ds_m, kn_ref[lo:hi, :], _NN), _dot(ds_m, kp_ref[lo:hi, :], _NN)

            dqn, dqp = piece(r0, r1, True)
            if r0:
                dqn_b, dqp_b = piece(0, r0, False)
                dqn, dqp = dqn + dqn_b, dqp + dqp_b
            dqn_ref[r0:r1, :] = dqn
            dqp_ref[r0:r1, :] = dqp

    blk = lambda w: pl.BlockSpec((None, t, w), lambda hh: (hh, 0, 0))
    kpblk = pl.BlockSpec((t, QK_ROPE), lambda hh: (0, 0))
    cols = pl.BlockSpec((t, QK_NOPE), lambda hh: (0, hh))
    return pl.pallas_call(
        body,
        name=name,
        grid=(h,),
        in_specs=[cols, blk(QK_ROPE), cols, kpblk, cols, cols, cols, blk(1)],
        out_specs=[cols, blk(QK_ROPE), cols, kpblk, cols],
        out_shape=[jax.ShapeDtypeStruct((t, h * QK_NOPE), F32), jax.ShapeDtypeStruct((h, t, QK_ROPE), F32),
                   jax.ShapeDtypeStruct((t, h * QK_NOPE), F32), jax.ShapeDtypeStruct((t, QK_ROPE), F32),
                   jax.ShapeDtypeStruct((t, h * V_HEAD), F32)],
        compiler_params=_params(("arbitrary",)),
    )(qn, qp, kn, kp, v, do, o, lse)


def _rope(x1, x2, cos, sin, *, name, inverse=False):
    w = x1.shape[1]
    sign = -1.0 if inverse else 1.0

    def fn(rows, bc):
        a, b, c, s = rows
        s = sign * s
        return (a * c - b * s, a * s + b * c), ()

    return _rowwise(fn, [x1, x2, cos, sin], [], [(w, F32), (w, F32)], [], name=name)


def _mlp_fwd(x_m, w1, w2, tag, after=()):
    def epi(acc):
        r = jnp.maximum(acc, 0.0)
        return (r * r,)

    act = _mm(x_m, w1, b_dev=True, name=f"{tag}_up", out_dtypes=(MXU_DTYPE,), epilogue=epi, after=after)
    y = _mm(act, w2, name=f"{tag}_down")
    return y, (x_m, act)


def _mlp_bwd(dy_m, saved, w1, w2, tag, after=()):
    x_m, act = saved
    dw2 = _mm(act, dy_m, ta=True, name=f"{tag}_dw2", out_dtypes=(MXU_DTYPE,))

    def epi(acc, av):
        return (acc * (2.0 * jnp.sqrt(av.astype(F32))),)

    dh = _mm(dy_m, w2, tb=True, name=f"{tag}_dact", out_dtypes=(MXU_DTYPE,), epilogue=epi, tile_extras=(act,),
             after=after)
    dw1 = _mm(x_m, dh, ta=True, out_dev=True, name=f"{tag}_dw1", out_dtypes=(MXU_DTYPE,))
    dx = _mm(dh, w1, tb=True, b_dev=True, name=f"{tag}_dx")
    return [dx], {"ff_w1": dw1, "ff_w2": dw2}


def _sc_gate_specs(t, d):
    per = d // LANES
    return [pl.BlockSpec((t, LANES), lambda j, k=k: (0, k * per + j)) for k in range(3)]


def _sc_gate_fwd(u, conv_w, *, name):
    t, d = u.shape[0], u.shape[1] // 3
    kw = conv_w.shape[0]
    ch = _conv_chunk(t)

    def body(b_ref, c_ref, h_ref, w_ref, r_ref, pad_ref):
        pad_ref[0:CONV_PAD, :] = jnp.zeros((CONV_PAD, LANES), F32)
        pad_ref[CONV_PAD:CONV_PAD + t, :] = c_ref[...] * h_ref[...]
        for t0 in range(0, t, ch):
            q = jnp.zeros((ch, LANES), F32)
            for k in range(kw):
                s = kw - 1 - k
                q = q + w_ref[k:k + 1, :] * pad_ref[CONV_PAD - s + t0:CONV_PAD - s + t0 + ch, :]
            r_ref[t0:t0 + ch, :] = (b_ref[t0:t0 + ch, :] * q).astype(r_ref.dtype)

    blk = pl.BlockSpec((t, LANES), lambda j: (0, j))
    return pl.pallas_call(
        body,
        name=name,
        grid=(d // LANES,),
        in_specs=_sc_gate_specs(t, d) + [pl.BlockSpec((kw, LANES), lambda j: (0, j))],
        out_specs=blk,
        out_shape=jax.ShapeDtypeStruct((t, d), MXU_DTYPE),
        scratch_shapes=[pltpu.VMEM((CONV_PAD + t, LANES), F32)],
        compiler_params=_params(("parallel",)),
    )(u, u, u, conv_w)


def _sc_gate_bwd(dr, u, conv_w, *, name):
    t, d = u.shape[0], u.shape[1] // 3
    kw = conv_w.shape[0]
    ch = _conv_chunk(t)

    def body(dr_ref, b_ref, c_ref, h_ref, w_ref, db_ref, dp_ref, dw_ref, ppad_ref, dpad_ref):
        ppad_ref[0:CONV_PAD, :] = jnp.zeros((CONV_PAD, LANES), F32)
        ppad_ref[CONV_PAD:CONV_PAD + t, :] = c_ref[...] * h_ref[...]
        dpad_ref[0:t, :] = dr_ref[...] * b_ref[...]
        dpad_ref[t:t + CONV_PAD, :] = jnp.zeros((CONV_PAD, LANES), F32)
        for t0 in range(0, t, ch):
            q = jnp.zeros((ch, LANES), F32)
            dp = jnp.zeros((ch, LANES), F32)
            for k in range(kw):
                s = kw - 1 - k
                q = q + w_ref[k:k + 1, :] * ppad_ref[CONV_PAD - s + t0:CONV_PAD - s + t0 + ch, :]
                dp = dp + w_ref[k:k + 1, :] * dpad_ref[t0 + s:t0 + s + ch, :]
            db_ref[t0:t0 + ch, :] = (dr_ref[t0:t0 + ch, :] * q).astype(db_ref.dtype)
            dp_ref[t0:t0 + ch, :] = dp
        for k in range(kw):
            s = kw - 1 - k
            acc = jnp.zeros((ch, LANES), F32)
            for t0 in range(0, t, ch):
                acc = acc + dpad_ref[t0:t0 + ch, :] * ppad_ref[CONV_PAD - s + t0:CONV_PAD - s + t0 + ch, :]
            dw_ref[k:k + 1, :] = _colsum(acc)

    blk = pl.BlockSpec((t, LANES), lambda j: (0, j))
    wblk = pl.BlockSpec((kw, LANES), lambda j: (0, j))
    return pl.pallas_call(
        body,
        name=name,
        grid=(d // LANES,),
        in_specs=[blk] + _sc_gate_specs(t, d) + [wblk],
        out_specs=[blk, blk, wblk],
        out_shape=[jax.ShapeDtypeStruct((t, d), MXU_DTYPE), jax.ShapeDtypeStruct((t, d), F32),
                   jax.ShapeDtypeStruct((kw, d), F32)],
        scratch_shapes=[pltpu.VMEM((CONV_PAD + t, LANES), F32), pltpu.VMEM((CONV_PAD + t, LANES), F32)],
        compiler_params=_params(("parallel",)),
    )(dr, u, u, u, conv_w)


def _sc_fwd(x_m, w_in, conv_w, w_out, tag, after=()):
    u = _mm(x_m, w_in, b_dev=True, name=f"{tag}_in", after=after)
    r = _sc_gate_fwd(u, conv_w, name=f"{tag}_gate")
    y = _mm(r, w_out, name=f"{tag}_out")
    return y, (x_m, u, r)


def _sc_bwd(dy_m, saved, w_in, conv_w, w_out, tag, after=()):
    d = D_MODEL
    x_m, u, r = saved
    dw_out = _mm(r, dy_m, ta=True, name=f"{tag}_dwout", out_dtypes=(MXU_DTYPE,))
    dr = _mm(dy_m, w_out, tb=True, name=f"{tag}_dr", after=after)
    db, dp, dconv = _sc_gate_bwd(dr, u, conv_w, name=f"{tag}_dgate")

    def du_fn(rows, bc):
        dbv, dpv, uv = rows
        return (jnp.concatenate([dbv.astype(F32), dpv * uv[:, 2 * d:3 * d], dpv * uv[:, d:2 * d]], axis=1),), ()

    du = _rowwise(du_fn, [db, dp, u], [], [(3 * d, MXU_DTYPE)], [], name=f"{tag}_du")[0]
    dw_in = _mm(x_m, du, ta=True, out_dev=True, name=f"{tag}_dwin", out_dtypes=(MXU_DTYPE,))
    dx = _mm(du, w_in, tb=True, b_dev=True, name=f"{tag}_dx")
    return [dx], {"sc_w_in": dw_in, "sc_conv_w": dconv, "sc_w_out": dw_out}


def _cf_fwd(x_m, w, tag, after=()):
    d = D_MODEL

    def bias_epi(acc, bias):
        return (acc + bias,)

    u = _mm(x_m, w["cf_w_pw1"], b_dev=True, name=f"{tag}_pw1", epilogue=bias_epi, col_extras=(w["cf_b_pw1"],),
            after=after)

    def glu_fn(rows, bc):
        uv = rows[0]
        return (uv[:, 0:d] * _sigmoid(uv[:, d:2 * d]),), ()

    h1 = _rowwise(glu_fn, [u], [], [(d, F32)], [], name=f"{tag}_glu")[0]
    h2 = _conv_fwd(h1, w["cf_dw_w"], w["cf_dw_b"], name=f"{tag}_conv")

    def norm_fn(rows, bc):
        hv = rows[0]
        mu = jnp.mean(hv, axis=-1, keepdims=True)
        hc = hv - mu
        rstd = lax.rsqrt(jnp.mean(hc * hc, axis=-1, keepdims=True) + LN_EPS)
        h3 = hc * rstd * bc[0] + bc[1]
        return (h3 * _sigmoid(h3),), ()

    h4 = _rowwise(norm_fn, [h2], [w["cf_norm_g"], w["cf_norm_b"]], [(d, MXU_DTYPE)], [], name=f"{tag}_norm")[0]
    y = _mm(h4, w["cf_w_pw2"], name=f"{tag}_pw2", epilogue=bias_epi, col_extras=(w["cf_b_pw2"],))
    return y, (x_m, u, h1, h2, h4)


def _cf_bwd(dy_m, dy_colsum, saved, w, tag, after=()):
    d = D_MODEL
    x_m, u, h1, h2, h4 = saved
    dw_pw2 = _mm(h4, dy_m, ta=True, name=f"{tag}_dwpw2", out_dtypes=(MXU_DTYPE,))
    dh4 = _mm(dy_m, w["cf_w_pw2"], tb=True, name=f"{tag}_dh4", after=after)

    def dnorm_fn(rows, bc):
        dh4v, hv = rows
        g, b = bc
        mu = jnp.mean(hv, axis=-1, keepdims=True)
        hc = hv - mu
        rstd = lax.rsqrt(jnp.mean(hc * hc, axis=-1, keepdims=True) + LN_EPS)
        xh = hc * rstd
        h3 = xh * g + b
        sg = _sigmoid(h3)
        dh3 = dh4v * (sg * (1.0 + h3 * (1.0 - sg)))
        dxh = dh3 * g
        dh2 = rstd * (dxh - jnp.mean(dxh, axis=-1, keepdims=True) - xh * jnp.mean(dxh * xh, axis=-1, keepdims=True))
        return (dh2,), (_colsum(dh3 * xh), _colsum(dh3), _colsum(dh2))

    dh2, dnorm_g, dnorm_b, ddw_b = _rowwise(dnorm_fn, [dh4, h2], [w["cf_norm_g"], w["cf_norm_b"]], [(d, F32)],
                                            [(1, d), (1, d), (1, d)], name=f"{tag}_dnorm")
    dh1, ddw_w = _conv_bwd(dh2, h1, w["cf_dw_w"], name=f"{tag}_dconv")

    def dglu_fn(rows, bc):
        dh1v, uv = rows
        a, gate = uv[:, 0:d], uv[:, d:2 * d]
        sg = _sigmoid(gate)
        du = jnp.concatenate([dh1v * sg, dh1v * a * sg * (1.0 - sg)], axis=1)
        return (du,), (_colsum(du),)

    du, db_pw1 = _rowwise(dglu_fn, [dh1, u], [], [(2 * d, MXU_DTYPE)], [(1, 2 * d)], name=f"{tag}_dglu")
    dw_pw1 = _mm(x_m, du, ta=True, out_dev=True, name=f"{tag}_dwpw1", out_dtypes=(MXU_DTYPE,))
    dx = _mm(du, w["cf_w_pw1"], tb=True, b_dev=True, name=f"{tag}_dx")
    grads = {"cf_w_pw1": dw_pw1, "cf_b_pw1": db_pw1, "cf_dw_w": ddw_w, "cf_dw_b": ddw_b, "cf_norm_g": dnorm_g,
             "cf_norm_b": dnorm_b, "cf_w_pw2": dw_pw2, "cf_b_pw2": dy_colsum}
    return [dx], grads


def _rope_tables(t):
    pos = jnp.arange(t, dtype=F32)
    inv_freq = ROPE_THETA ** (-jnp.arange(0, QK_ROPE, 2, dtype=F32) / QK_ROPE)
    ang = pos[:, None] * inv_freq[None, :]
    return jnp.cos(ang), jnp.sin(ang)


ROPE_HALF = QK_ROPE // 2
Q_NOPE_COLS = MLA_HEADS * QK_NOPE
Q_HALF_COLS = MLA_HEADS * ROPE_HALF


def _mla_fwd(x_m, w, tag, after=()):
    t = x_m.shape[0]
    cos, sin = _rope_tables(t)
    cos_h, sin_h = jnp.tile(cos, (1, MLA_HEADS)), jnp.tile(sin, (1, MLA_HEADS))
    cq_pre = _mm(x_m, w["mla_w_dq"], name=f"{tag}_dq", after=after)
    cq = _rms_fwd(cq_pre, w["mla_g_q"], name=f"{tag}_qnorm")
    q = _mm(cq, w["mla_w_uq"], name=f"{tag}_uq")
    r1, r2 = _rope(q[:, Q_NOPE_COLS:Q_NOPE_COLS + Q_HALF_COLS], q[:, Q_NOPE_COLS + Q_HALF_COLS:], cos_h, sin_h,
                   name=f"{tag}_qrope")
    qp_h = jnp.concatenate([r1.reshape(t, MLA_HEADS, ROPE_HALF), r2.reshape(t, MLA_HEADS, ROPE_HALF)],
                           axis=-1).transpose(1, 0, 2).astype(MXU_DTYPE)
    ckv_full = _mm(x_m, w["mla_w_dkv"], name=f"{tag}_dkv")
    ckv_pre = ckv_full[:, :KV_LORA]
    ckv = _rms_fwd(ckv_pre, w["mla_g_kv"], name=f"{tag}_kvnorm")
    k1, k2 = _rope(ckv_full[:, KV_LORA:KV_LORA + ROPE_HALF], ckv_full[:, KV_LORA + ROPE_HALF:], cos, sin,
                   name=f"{tag}_krope")
    kp = jnp.concatenate([k1, k2], axis=-1).astype(MXU_DTYPE)
    kn = _mm(ckv, w["mla_w_uk"], name=f"{tag}_uk", out_dtypes=(MXU_DTYPE,))
    v = _mm(ckv, w["mla_w_uv"], name=f"{tag}_uv", out_dtypes=(MXU_DTYPE,))
    o, o_f, lse = _attn_fwd(q, qp_h, kn, kp, v, name=f"{tag}_attn")
    y = _mm(o, w["mla_w_o"], name=f"{tag}_o")
    return y, (x_m, cq_pre, cq, ckv_pre, ckv, q, qp_h, kn, kp, v, lse, o, o_f, (cos, sin, cos_h, sin_h))


def _mla_bwd(dy_m, saved, w, tag, after=()):
    x_m, cq_pre, cq, ckv_pre, ckv, q, qp_h, kn, kp, v, lse, o, o_f, (cos, sin, cos_h, sin_h) = saved
    t = x_m.shape[0]
    half = ROPE_HALF
    dw_o = _mm(o, dy_m, ta=True, name=f"{tag}_dwo", out_dtypes=(MXU_DTYPE,))
    do = _mm(dy_m, w["mla_w_o"], tb=True, name=f"{tag}_do", out_dtypes=(MXU_DTYPE,), after=after)
    dqn, dqp_h, dkn, dkp, dv = _attn_bwd(q, qp_h, kn, kp, v, do, o_f, lse, name=f"{tag}_dattn")
    dqp = dqp_h.transpose(1, 0, 2)
    d1, d2 = _rope(dqp[:, :, :half].reshape(t, -1), dqp[:, :, half:].reshape(t, -1), cos_h, sin_h,
                   name=f"{tag}_dqrope", inverse=True)
    dq = jnp.concatenate([dqn, d1, d2], axis=-1).astype(MXU_DTYPE)
    dw_uq = _mm(cq, dq, ta=True, name=f"{tag}_dwuq", out_dtypes=(MXU_DTYPE,))
    dcq = _mm(dq, w["mla_w_uq"], tb=True, name=f"{tag}_dcq")
    dcq_pre, dg_q = _rms_bwd(cq_pre, w["mla_g_q"], dcq, name=f"{tag}_dqnorm")
    dcq_pre_m = dcq_pre.astype(MXU_DTYPE)
    dw_dq = _mm(x_m, dcq_pre_m, ta=True, name=f"{tag}_dwdq", out_dtypes=(MXU_DTYPE,))
    dx_q = _mm(dcq_pre_m, w["mla_w_dq"], tb=True, name=f"{tag}_dxq")
    dw_uk = _mm(ckv, dkn, ta=True, name=f"{tag}_dwuk", out_dtypes=(MXU_DTYPE,))
    dw_uv = _mm(ckv, dv, ta=True, name=f"{tag}_dwuv", out_dtypes=(MXU_DTYPE,))
    dckv_k = _mm(dkn, w["mla_w_uk"], tb=True, name=f"{tag}_dckvk")
    dckv = _mm(dv, w["mla_w_uv"], tb=True, name=f"{tag}_dckv", epilogue=lambda acc, other: (acc + other,),
               tile_extras=(dckv_k,))
    dckv_pre, dg_kv = _rms_bwd(ckv_pre, w["mla_g_kv"], dckv, name=f"{tag}_dkvnorm")
    dk1, dk2 = _rope(dkp[:, :half], dkp[:, half:], cos, sin, name=f"{tag}_dkrope", inverse=True)
    dckv_full = jnp.concatenate([dckv_pre, dk1, dk2], axis=-1).astype(MXU_DTYPE)
    dw_dkv = _mm(x_m, dckv_full, ta=True, name=f"{tag}_dwdkv", out_dtypes=(MXU_DTYPE,))
    dx_kv = _mm(dckv_full, w["mla_w_dkv"], tb=True, name=f"{tag}_dxkv")
    grads = {"mla_w_dq": dw_dq, "mla_g_q": dg_q, "mla_w_uq": dw_uq, "mla_w_dkv": dw_dkv, "mla_g_kv": dg_kv,
             "mla_w_uk": dw_uk, "mla_w_uv": dw_uv, "mla_w_o": dw_o}
    return [dx_q, dx_kv], grads


def _cast_input(x):
    cast = lambda rows, bc: ((rows[0],), ())
    return _rowwise(cast, [x], [], [(D_MODEL, MXU_DTYPE)], [], name="x_cast")[0]


def _hooked(value, hook):
    token = None if hook is None else hook(value)
    return () if token is None else (token,)


def _mix_fwd(i, cur, cur_m, wl, hook=None, after=()):
    mixer = i % N_MIXERS
    if mixer == 0:
        y, s_mix = _sc_fwd(cur_m, wl["sc_w_in"], wl["sc_conv_w"], wl["sc_w_out"], f"l{i}_sc", after=after)
    elif mixer == 1:
        y, s_mix = _mla_fwd(cur_m, wl, f"l{i}_mla", after=after)
    else:
        y, s_mix = _cf_fwd(cur_m, wl, f"l{i}_cf", after=after)
    cur, cur_m, xh_mix, rs_mix = _ln_fwd(cur, y, wl["ln_mix_g"], wl["ln_mix_b"], name=f"l{i}_lnmix",
                                         after=_hooked(y, hook))
    return cur, cur_m, (s_mix, xh_mix, rs_mix)


def _ff_fwd(i, cur, cur_m, wl, hook=None, after=()):
    y, s_ff = _mlp_fwd(cur_m, wl["ff_w1"], wl["ff_w2"], f"l{i}_ff", after=after)
    cur, cur_m, xh_ff, rs_ff = _ln_fwd(cur, y, wl["ln_ff_g"], wl["ln_ff_b"], name=f"l{i}_lnff", after=_hooked(y, hook))
    return cur, cur_m, (s_ff, xh_ff, rs_ff)


def _layer_fwd(i, cur, cur_m, wl):
    cur, cur_m, s_mix = _mix_fwd(i, cur, cur_m, wl)
    cur, cur_m, s_ff = _ff_fwd(i, cur, cur_m, wl)
    return cur, cur_m, (*s_mix, *s_ff)


def _loss_and_grad(out, target):
    d = D_MODEL

    def loss_fn(rows, bc):
        err = rows[0] - rows[1]
        return (err * (1.0 / d),), (_colsum(err * err),)

    dy, sq = _rowwise(loss_fn, [out, target], [], [(d, F32)], [(1, d)], name="loss")
    return (0.5 / d) * jnp.sum(sq), dy


def _ff_bwd(i, terms, saved, wl, hook=None, after=()):
    s_ff, xh_ff, rs_ff = saved
    grads = {}
    dz, dz_m, grads["ln_ff_g"], grads["ln_ff_b"], _ = _ln_bwd(terms, xh_ff, rs_ff, wl["ln_ff_g"], name=f"l{i}_dlnff",
                                                              after=after)
    dxs, g_ff = _mlp_bwd(dz_m, s_ff, wl["ff_w1"], wl["ff_w2"], f"l{i}_ff", after=_hooked(dz, hook))
    grads.update(g_ff)
    return [(ALPHA, dz)] + [(1.0, v) for v in dxs], grads


def _mix_bwd(i, terms, saved, wl, hook=None, after=()):
    mixer = i % N_MIXERS
    s_mix, xh_mix, rs_mix = saved
    grads = {}
    dz, dz_m, grads["ln_mix_g"], grads["ln_mix_b"], dz_sum = _ln_bwd(terms, xh_mix, rs_mix, wl["ln_mix_g"],
                                                                     name=f"l{i}_dlnmix", after=after)
    behind = _hooked(dz, hook)
    if mixer == 0:
        dxs, g_mix = _sc_bwd(dz_m, s_mix, wl["sc_w_in"], wl["sc_conv_w"], wl["sc_w_out"], f"l{i}_sc", after=behind)
    elif mixer == 1:
        dxs, g_mix = _mla_bwd(dz_m, s_mix, wl, f"l{i}_mla", after=behind)
    else:
        dxs, g_mix = _cf_bwd(dz_m, dz_sum, s_mix, wl, f"l{i}_cf", after=behind)
    grads.update(g_mix)
    return [(ALPHA, dz)] + [(1.0, v) for v in dxs], grads


def _layer_bwd(i, terms, saved, wl):
    terms, g_ff = _ff_bwd(i, terms, saved[3:], wl)
    terms, g_mix = _mix_bwd(i, terms, saved[:3], wl)
    return terms, {**g_ff, **g_mix}


def _sum_terms(terms, *, name):
    coefs = [c for c, _ in terms]

    def sum_fn(rows, bc):
        acc = None
        for c, v in zip(coefs, rows):
            tv = v if c == 1.0 else c * v
            acc = tv if acc is None else acc + tv
        return (acc,), ()

    return _rowwise(sum_fn, [v for _, v in terms], [], [(D_MODEL, F32)], [], name=name)[0]


def _local_step(x, target, layers):
    cur, cur_m = x, _cast_input(x)
    saved = []
    for i in range(DEPTH):
        cur, cur_m, s = _layer_fwd(i, cur, cur_m, layers[i])
        saved.append(s)
    loss, dy = _loss_and_grad(cur, target)
    terms = [(1.0, dy)]
    grads = [None] * DEPTH
    for i in reversed(range(DEPTH)):
        terms, grads[i] = _layer_bwd(i, terms, saved[i], layers[i])
    return loss, _sum_terms(terms, name="grad_x"), grads


MESH_ID = pl.DeviceIdType.MESH
ANY = pl.BlockSpec(memory_space=pl.ANY)


def _block_id(px, py, pc):
    return 4 * px + 2 * py + pc


COPIES_PER_GATHER = 7


def _all_gather(shards, *, name):
    n = len(shards)

    def body(*refs):
        x_refs, out_refs = refs[:n], refs[n:2 * n]
        send_sems, recv_sems, local_sems = refs[2 * n:]
        x, y, core = lax.axis_index("x"), lax.axis_index("y"), lax.axis_index("c")
        me, sibling = (x, y, core), (x, y, 1 - core)
        chips = [(1 - x, y), (x, 1 - y), (1 - x, 1 - y)]

        def copy(t, k, block, to, from_input=False):
            dst = out_refs[t].at[_block_id(*block)]
            return pltpu.make_async_remote_copy(
                src_ref=x_refs[t] if from_input else dst, dst_ref=dst,
                send_sem=send_sems.at[t * COPIES_PER_GATHER + k], recv_sem=recv_sems.at[t * COPIES_PER_GATHER + k],
                device_id=to, device_id_type=MESH_ID)

        started = []
        mine = [pltpu.make_async_copy(x_refs[t], out_refs[t].at[_block_id(*me)], local_sems.at[t]) for t in range(n)]
        for t in range(n):
            mine[t].start()
            first = [copy(t, 0, me, sibling, from_input=True)]
            first += [copy(t, 1 + j, me, (*chip, core), from_input=True) for j, chip in enumerate(chips)]
            for cp in first:
                cp.start()
            started += first
        for t in range(n):
            for j, chip in enumerate(chips):
                copy(t, 1 + j, (*chip, core), me).wait_recv()
                passed = copy(t, 4 + j, (*chip, core), sibling)
                passed.start()
                started.append(passed)
        for t in range(n):
            copy(t, 0, sibling, me).wait_recv()
            for j, chip in enumerate(chips):
                copy(t, 4 + j, (*chip, 1 - core), me).wait_recv()
        for cp in started:
            cp.wait_send()
        for cp in mine:
            cp.wait()

    n_sems = n * COPIES_PER_GATHER
    outs = pl.pallas_call(
        body,
        name=name,
        out_shape=[jax.ShapeDtypeStruct((N_DEV, *s.shape), s.dtype) for s in shards],
        in_specs=[ANY] * n,
        out_specs=[ANY] * n,
        scratch_shapes=[pltpu.SemaphoreType.DMA((n_sems,)), pltpu.SemaphoreType.DMA((n_sems,)),
                        pltpu.SemaphoreType.DMA((n,))],
    )(*shards)
    return list(outs)


def _swap_with_sibling(parts, *, name):
    n = len(parts)

    def body(*refs):
        p_refs, got_refs = refs[:n], refs[n:2 * n]
        send_sems, recv_sems = refs[2 * n:]
        x, y, core = lax.axis_index("x"), lax.axis_index("y"), lax.axis_index("c")
        chips = [(x, y), (1 - x, y), (x, 1 - y), (1 - x, 1 - y)]
        copies = []
        for t in range(n):
            for k, (px, py) in enumerate(chips):
                copies.append(pltpu.make_async_remote_copy(
                    src_ref=p_refs[t].at[_block_id(px, py, 1 - core)], dst_ref=got_refs[t].at[k],
                    send_sem=send_sems.at[4 * t + k], recv_sem=recv_sems.at[4 * t + k], device_id=(x, y, 1 - core),
                    device_id_type=MESH_ID))
        for cp in copies:
            cp.start()
        for cp in copies:
            cp.wait()

    outs = pl.pallas_call(
        body,
        name=name,
        out_shape=[jax.ShapeDtypeStruct((4, *p.shape[1:]), p.dtype) for p in parts],
        in_specs=[ANY] * n,
        out_specs=[ANY] * n,
        scratch_shapes=[pltpu.SemaphoreType.DMA((4 * n,)), pltpu.SemaphoreType.DMA((4 * n,))],
    )(*parts)
    return list(outs)


def _swap_between_chips(parts, *, name):
    n = len(parts)

    def body(*refs):
        p_refs, got_refs = refs[:n], refs[n:2 * n]
        send_sems, recv_sems = refs[2 * n:]
        x, y, core = lax.axis_index("x"), lax.axis_index("y"), lax.axis_index("c")
        chips = [(1 - x, y), (x, 1 - y), (1 - x, 1 - y)]
        copies = []
        for t in range(n):
            for j, (px, py) in enumerate(chips):
                copies.append(pltpu.make_async_remote_copy(
                    src_ref=p_refs[t].at[j], dst_ref=got_refs[t].at[j], send_sem=send_sems.at[3 * t + j],
                    recv_sem=recv_sems.at[3 * t + j], device_id=(px, py, core), device_id_type=MESH_ID))
        for cp in copies:
            cp.start()
        for cp in copies:
            cp.wait()

    outs = pl.pallas_call(
        body,
        name=name,
        out_shape=[jax.ShapeDtypeStruct(p.shape, p.dtype) for p in parts],
        in_specs=[ANY] * n,
        out_specs=[ANY] * n,
        scratch_shapes=[pltpu.SemaphoreType.DMA((3 * n,)), pltpu.SemaphoreType.DMA((3 * n,))],
    )(*parts)
    return list(outs)


def _chip_partials(parts, got, *, name):
    _, r, c = parts.shape
    tb = _row_tile(r, 512)
    x, y, core = lax.axis_index("x"), lax.axis_index("y"), lax.axis_index("c")
    chips = [(x, y), (1 - x, y), (x, 1 - y), (1 - x, 1 - y)]
    ids = jnp.stack([_block_id(px, py, core) for px, py in chips]).astype(jnp.int32)

    def body(ids_ref, p0, p1, p2, p3, g_ref, own_ref, out_ref):
        own_ref[...] = p0[...].astype(F32) + g_ref[0].astype(F32)
        for j, p in enumerate((p1, p2, p3)):
            out_ref[j] = (p[...].astype(F32) + g_ref[j + 1].astype(F32)).astype(out_ref.dtype)

    def part_spec(k):
        return pl.BlockSpec((None, tb, c), lambda i, ids_ref: (ids_ref[k], i, 0))

    grid_spec = pltpu.PrefetchScalarGridSpec(
        num_scalar_prefetch=1,
        grid=(r // tb,),
        in_specs=[part_spec(k) for k in range(4)] + [pl.BlockSpec((4, tb, c), lambda i, ids_ref: (0, i, 0))],
        out_specs=[pl.BlockSpec((tb, c), lambda i, ids_ref: (i, 0)), pl.BlockSpec((3, tb, c), lambda i, ids_ref: (0, i, 0))],
    )
    return pl.pallas_call(
        body,
        name=name,
        grid_spec=grid_spec,
        out_shape=[jax.ShapeDtypeStruct((r, c), F32), jax.ShapeDtypeStruct((3, r, c), parts.dtype)],
        compiler_params=_params(("parallel",)),
    )(ids, parts, parts, parts, parts, got)


def _chip_partials_group(parts, got, *, name):
    n = len(parts)
    rows = [p.shape[1] for p in parts]
    steps = 4 if all(r % (4 * BF16_ROWS) == 0 for r in rows) else (2 if all(r % (2 * BF16_ROWS) == 0 for r in rows) else 1)
    x, y, core = lax.axis_index("x"), lax.axis_index("y"), lax.axis_index("c")
    chips = [(x, y), (1 - x, y), (x, 1 - y), (1 - x, 1 - y)]
    ids = jnp.stack([_block_id(px, py, core) for px, py in chips]).astype(jnp.int32)

    def body(ids_ref, *refs):
        ins, outs = refs[:5 * n], refs[5 * n:]
        for t in range(n):
            p0, p1, p2, p3, g_ref = ins[5 * t:5 * t + 5]
            own_ref, out_ref = outs[2 * t], outs[2 * t + 1]
            own_ref[...] = p0[...].astype(F32) + g_ref[0].astype(F32)
            for j, p in enumerate((p1, p2, p3)):
                out_ref[j] = (p[...].astype(F32) + g_ref[j + 1].astype(F32)).astype(out_ref.dtype)

    in_specs, out_specs, out_shape, operands = [], [], [], []
    for p, g in zip(parts, got):
        _, r, c = p.shape
        tb = r // steps
        in_specs += [pl.BlockSpec((None, tb, c), lambda i, ids_ref, k=k: (ids_ref[k], i, 0)) for k in range(4)]
        in_specs.append(pl.BlockSpec((4, tb, c), lambda i, ids_ref: (0, i, 0)))
        out_specs += [pl.BlockSpec((tb, c), lambda i, ids_ref: (i, 0)), pl.BlockSpec((3, tb, c), lambda i, ids_ref: (0, i, 0))]
        out_shape += [jax.ShapeDtypeStruct((r, c), F32), jax.ShapeDtypeStruct((3, r, c), p.dtype)]
        operands += [p, p, p, p, g]
    outs = pl.pallas_call(
        body,
        name=name,
        grid_spec=pltpu.PrefetchScalarGridSpec(num_scalar_prefetch=1, grid=(steps,), in_specs=in_specs,
                                               out_specs=out_specs),
        out_shape=out_shape,
        compiler_params=_params(("parallel",)),
    )(ids, *operands)
    return [(outs[2 * t], outs[2 * t + 1]) for t in range(n)]


def _sum_blocks(first, blocks, *, name):
    n, r, c = blocks.shape
    tb = _row_tile(r, 512)

    def body(*refs):
        b_ref, o_ref = refs[-2], refs[-1]
        acc = refs[0][...] if first is not None else b_ref[0].astype(F32)
        for j in range(0 if first is not None else 1, n):
            acc = acc + b_ref[j].astype(F32)
        o_ref[...] = acc

    row_spec = pl.BlockSpec((tb, c), lambda i: (i, 0))
    return pl.pallas_call(
        body,
        name=name,
        grid=(r // tb,),
        in_specs=([row_spec] if first is not None else []) + [pl.BlockSpec((n, tb, c), lambda i: (0, i, 0))],
        out_specs=row_spec,
        out_shape=jax.ShapeDtypeStruct((r, c), F32),
        compiler_params=_params(("parallel",)),
    )(*([first] if first is not None else []), blocks)


def _reduce_scatter(parts, *, tag):
    got = _swap_with_sibling(parts, name=f"{tag}_sibling")
    sums = [_chip_partials(p, g, name=f"{tag}_chipsum{t}") for t, (p, g) in enumerate(zip(parts, got))]
    arrived = _swap_between_chips([others for _, others in sums], name=f"{tag}_chips")
    return [(own, arr) for (own, _), arr in zip(sums, arrived)]


HBM = pl.BlockSpec(memory_space=pltpu.HBM)
SEM = pl.BlockSpec(memory_space=pltpu.SEMAPHORE)
DATAFLOW = pltpu.SideEffectType.DATAFLOW_SIDE_EFFECTING


def _tie(value, *before):
    return lax.optimization_barrier((value, *before))[0]


def _split_call(name, arrays, sems, n_new, body, after=None):
    n_a, n_s = len(arrays), len(sems)
    n_after = 0 if after is None else 1

    def kernel_body(*refs):
        new = refs[n_a + n_s + n_after:n_a + n_s + n_after + 2] if n_new else (None, None)
        body(refs[:n_a], refs[n_a:n_a + n_s], *new)
        refs[-1][...] = jnp.zeros_like(refs[-1])

    n_sem_out = 2 if n_new else 0
    out_shape = [pltpu.SemaphoreType.DMA((n_new,))] * n_sem_out
    out_shape += [pltpu.HBM(a.shape, a.dtype) for a in arrays]
    out_shape += [jax.ShapeDtypeStruct((SUBLANES, LANES), F32)]
    outs = pl.pallas_call(
        kernel_body,
        name=name,
        out_shape=out_shape,
        in_specs=[HBM] * n_a + [SEM] * n_s + [ANY] * n_after,
        out_specs=[SEM] * n_sem_out + [HBM] * n_a + [pl.BlockSpec(memory_space=pltpu.VMEM)],
        input_output_aliases={i: n_sem_out + i for i in range(n_a)},
        compiler_params=pltpu.CompilerParams(has_side_effects=DATAFLOW),
    )(*[pltpu.with_memory_space_constraint(a, pltpu.HBM) for a in arrays], *sems, *([after] if n_after else []))
    return tuple(outs[:n_sem_out]), list(outs[n_sem_out:n_sem_out + n_a]), outs[-1]


def _remote(src, dst, send, recv, k, to):
    return pltpu.make_async_remote_copy(src_ref=src, dst_ref=dst, send_sem=send.at[k], recv_sem=recv.at[k],
                                        device_id=to, device_id_type=MESH_ID)


def _gather_start(shards, *, name, after=None):
    n = len(shards)
    me = _block_id(lax.axis_index("x"), lax.axis_index("y"), lax.axis_index("c"))
    lands = [lax.dynamic_update_index_in_dim(lax.empty((N_DEV, *s.shape), s.dtype), s, me, 0) for s in shards]

    def body(refs, _, send, recv):
        x, y, core = lax.axis_index("x"), lax.axis_index("y"), lax.axis_index("c")
        targets = [(x, y, 1 - core), (1 - x, y, core), (x, 1 - y, core), (1 - x, 1 - y, core)]
        for t in range(n):
            for k, to in enumerate(targets):
                _remote(refs[t], refs[n + t].at[_block_id(x, y, core)], send, recv, 4 * t + k, to).start()

    sems, thru, token = _split_call(name, shards + lands, [], 4 * n, body, after=after)
    return {"n": n, "sems1": sems, "arrays": thru, "token": token}


def _gather_pass_on(st, *, name, after):
    n = st["n"]

    def body(refs, sems, send, recv):
        send1, recv1 = sems
        x, y, core = lax.axis_index("x"), lax.axis_index("y"), lax.axis_index("c")
        chips = [(1 - x, y), (x, 1 - y), (1 - x, 1 - y)]
        for t in range(n):
            for j, (px, py) in enumerate(chips):
                block = refs[n + t].at[_block_id(px, py, core)]
                _remote(refs[t], block, send1, recv1, 4 * t + 1 + j, (x, y, 1 - core)).wait_recv()
                _remote(block, block, send, recv, 3 * t + j, (x, y, 1 - core)).start()

    sems2, thru, token = _split_call(name, st["arrays"], list(st["sems1"]), 3 * n, body, after=after)
    return {**st, "sems2": sems2, "arrays": thru, "token": token}


def _gather_finish(st, *, name, after):
    n = st["n"]

    def body(refs, sems, *_):
        send1, recv1, send2, recv2 = sems
        x, y, core = lax.axis_index("x"), lax.axis_index("y"), lax.axis_index("c")
        sibling = (x, y, 1 - core)
        chips = [(1 - x, y), (x, 1 - y), (1 - x, 1 - y)]
        for t in range(n):
            land = refs[n + t]
            for k in range(4):
                _remote(refs[t], land.at[_block_id(x, y, core)], send1, recv1, 4 * t + k, sibling).wait_send()
            _remote(refs[t], land.at[_block_id(x, y, 1 - core)], send1, recv1, 4 * t, sibling).wait_recv()
            for j, (px, py) in enumerate(chips):
                _remote(land.at[_block_id(px, py, core)], land.at[_block_id(px, py, core)], send2, recv2, 3 * t + j,
                        sibling).wait_send()
                _remote(refs[t], land.at[_block_id(px, py, 1 - core)], send2, recv2, 3 * t + j, sibling).wait_recv()

    _, thru, _ = _split_call(name, st["arrays"], [*st["sems1"], *st["sems2"]], 0, body, after=after)
    return thru[n:]


def _swap_start(sources, lands, plan, per_array, *, name):
    n = len(sources)

    def body(refs, _, send, recv):
        x, y, core = lax.axis_index("x"), lax.axis_index("y"), lax.axis_index("c")
        for t in range(n):
            for k in range(per_array):
                block, to = plan(x, y, core, k)
                _remote(refs[t].at[block], refs[n + t].at[k], send, recv, per_array * t + k, to).start()

    sems, thru, token = _split_call(name, sources + lands, [], per_array * n, body)
    return {"n": n, "sems": sems, "arrays": thru, "token": token, "plan": plan, "per_array": per_array}


def _swap_finish(st, *, name, after):
    n, plan, per_array = st["n"], st["plan"], st["per_array"]

    def body(refs, sems, *_):
        send, recv = sems
        x, y, core = lax.axis_index("x"), lax.axis_index("y"), lax.axis_index("c")
        for t in range(n):
            for k in range(per_array):
                block, to = plan(x, y, core, k)
                cp = _remote(refs[t].at[block], refs[n + t].at[k], send, recv, per_array * t + k, to)
                cp.wait_send()
                cp.wait_recv()

    _, thru, _ = _split_call(name, st["arrays"], list(st["sems"]), 0, body, after=after)
    return thru[:n], thru[n:]


def _to_sibling(x, y, core, k):
    px, py = [(x, y), (1 - x, y), (x, 1 - y), (1 - x, 1 - y)][k]
    return _block_id(px, py, 1 - core), (x, y, 1 - core)


def _to_chip(x, y, core, k):
    px, py = [(1 - x, y), (x, 1 - y), (1 - x, 1 - y)][k]
    return k, (px, py, core)


PARAMS = {
    "sc_w_in": ((2, 1024, 3072), 2), "sc_conv_w": ((2, 3, 1024), 2), "sc_w_out": ((2, 1024, 1024), 1),
    "mla_w_dq": ((1, 1024, 384), 1), "mla_g_q": ((1, 384), None), "mla_w_uq": ((1, 384, 1536), 2),
    "mla_w_dkv": ((1, 1024, 320), 1), "mla_g_kv": ((1, 256), None), "mla_w_uk": ((1, 256, 8, 128), 1),
    "mla_w_uv": ((1, 256, 8, 128), 1), "mla_w_o": ((1, 1024, 1024), 1), "cf_w_pw1": ((1, 1024, 2048), 2),
    "cf_b_pw1": ((1, 2048), 1), "cf_dw_w": ((1, 31, 1024), 2), "cf_dw_b": ((1, 1024), 1), "cf_norm_g": ((1, 1024), 1),
    "cf_norm_b": ((1, 1024), 1), "cf_w_pw2": ((1, 1024, 1024), 1), "cf_b_pw2": ((1, 1024), 1),
    "ff_w1": ((4, 1024, 4096), 2), "ff_w2": ((4, 4096, 1024), 1), "ln_mix_g": ((4, 1024), None),
    "ln_mix_b": ((4, 1024), None), "ln_ff_g": ((4, 1024), None), "ln_ff_b": ((4, 1024), None),
}
NAMES = list(PARAMS)
BIG = ["sc_w_in", "sc_w_out", "mla_w_dq", "mla_w_uq", "mla_w_dkv", "mla_w_uk", "mla_w_uv", "mla_w_o", "cf_w_pw1",
       "cf_w_pw2", "ff_w1", "ff_w2"]
SMALL = [n for n in NAMES if n not in BIG]
SMALL_SHARDED = [n for n in SMALL if PARAMS[n][1] is not None]
SMALL_WIDTH = LANES
DEV_BLOCKED = ("sc_w_in", "cf_w_pw1", "ff_w1")
MIXER_BIG = {0: ["sc_w_in", "sc_w_out"], 1: ["mla_w_dq", "mla_w_uq", "mla_w_dkv", "mla_w_uk", "mla_w_uv", "mla_w_o"],
             2: ["cf_w_pw1", "cf_w_pw2"]}


def _layer_big(i):
    return [(n, i // N_MIXERS) for n in MIXER_BIG[i % N_MIXERS]] + [("ff_w1", i), ("ff_w2", i)]


def _shard_2d(name, a):
    return a.reshape(a.shape[0], -1)


def _as_operand(name, g):
    if name in DEV_BLOCKED:
        return g
    if name == "mla_w_uq":
        by_head = g.transpose(1, 0, 2)
        pieces = [by_head[:, :, :QK_NOPE], by_head[:, :, QK_NOPE:QK_NOPE + ROPE_HALF], by_head[:, :, QK_NOPE + ROPE_HALF:]]
        return jnp.concatenate([p.reshape(g.shape[1], -1) for p in pieces], axis=1)
    return g.reshape(-1, g.shape[2])


def _as_parts(name, grad):
    if name in DEV_BLOCKED:
        return grad
    if name == "mla_w_uq":
        rows = grad.shape[0]
        pieces = [grad[:, :Q_NOPE_COLS].reshape(rows, MLA_HEADS, QK_NOPE),
                  grad[:, Q_NOPE_COLS:Q_NOPE_COLS + Q_HALF_COLS].reshape(rows, MLA_HEADS, ROPE_HALF),
                  grad[:, Q_NOPE_COLS + Q_HALF_COLS:].reshape(rows, MLA_HEADS, ROPE_HALF)]
        return jnp.concatenate(pieces, axis=2).transpose(1, 0, 2)
    return grad.reshape(N_DEV, -1, grad.shape[1])


def _shard_shape(name):
    shape, ax = PARAMS[name]
    if ax is None:
        return shape
    return tuple(s // N_DEV if i == ax else s for i, s in enumerate(shape))


def _rows_of(shape, width, align):
    n = 1
    for s in shape:
        n *= s
    rows = -(-n // width)
    return -(-rows // align) * align


def _pack_rows(arrays, width, align, dtype, lead=0):
    segs = []
    for a in arrays:
        batch = a.shape[:lead]
        flat = a.astype(dtype).reshape(*batch, -1)
        rows = _rows_of(a.shape[lead:], width, align)
        flat = jnp.pad(flat, [(0, 0)] * lead + [(0, rows * width - flat.shape[-1])])
        segs.append(flat.reshape(*batch, rows, width))
    return jnp.concatenate(segs, axis=lead)


def _unpack_rows(buf, shapes, width, align):
    out, off = [], 0
    lead = buf.shape[:-2]
    for shape in shapes:
        n = 1
        for s in shape:
            n *= s
        rows = _rows_of(shape, width, align)
        seg = buf[..., off:off + rows, :].reshape(*lead, rows * width)[..., :n]
        out.append(seg.reshape(*lead, *shape))
        off += rows
    return out


def _merge_shards(stacked, ax):
    moved = jnp.moveaxis(stacked, 0, ax)
    shape = moved.shape
    return moved.reshape(*shape[:ax], shape[ax] * shape[ax + 1], *shape[ax + 2:])


def _split_shards(full, ax):
    shape = full.shape
    split = full.reshape(*shape[:ax], N_DEV, shape[ax] // N_DEV, *shape[ax + 1:])
    return jnp.moveaxis(split, ax, 0)


def _adam_math(wv, gv, mv, vv):
    m_new = ADAM_B1 * mv + (1.0 - ADAM_B1) * gv
    v_new = ADAM_B2 * vv + (1.0 - ADAM_B2) * (gv * gv)
    m_hat = m_new / (1.0 - ADAM_B1 ** ADAM_STEP)
    v_hat = v_new / (1.0 - ADAM_B2 ** ADAM_STEP)
    delta = -ADAM_LR * (m_hat / (jnp.sqrt(v_hat) + ADAM_EPS) + ADAM_WD * wv)
    return delta, m_new, v_new


def _adamw(w, g, m, v, *, name):
    shape = w.shape
    c = shape[-1]

    def fn(rows, bc):
        return _adam_math(*rows), ()

    flat = [a.reshape(-1, c) for a in (w, g, m, v)]
    outs = _rowwise(fn, flat, [], [(c, F32)] * 3, [], name=name, tb_cap=512)
    return tuple(o.reshape(shape) for o in outs)


def _adamw_reduced(w, m, v, layer_grads, *, name, after=()):
    n_l, r, c = w.shape
    tb = _row_tile(r, 256 if c > 512 else 512)
    nb = r // tb

    def body(*refs):
        w_ref, m_ref, v_ref = refs[:3]
        g_refs = refs[3:3 + 2 * n_l]
        g_out, d_out, m_out, v_out = refs[3 + 2 * n_l + len(after):]
        layer = pl.program_id(0)
        for l in range(n_l):
            @pl.when(layer == l)
            def _(l=l):
                own, arr = g_refs[2 * l], g_refs[2 * l + 1]
                g = own[...] + arr[0].astype(F32)
                g = g + arr[1].astype(F32)
                g = g + arr[2].astype(F32)
                delta, m_new, v_new = _adam_math(w_ref[...], g, m_ref[...], v_ref[...])
                g_out[...] = g
                d_out[...] = delta
                m_out[...] = m_new
                v_out[...] = v_new

    stacked = pl.BlockSpec((tb, c), lambda l, i: (l * nb + i, 0))
    in_specs = [stacked] * 3
    operands = [a.reshape(n_l * r, c) for a in (w, m, v)]
    for l, (own, arr) in enumerate(layer_grads):
        in_specs.append(pl.BlockSpec((tb, c), lambda ll, i, l=l: (jnp.where(ll == l, i, 0), 0)))
        in_specs.append(pl.BlockSpec((3, tb, c), lambda ll, i, l=l: (0, jnp.where(ll == l, i, 0), 0)))
        operands += [own, arr]
    in_specs += [ANY] * len(after)
    operands += list(after)
    outs = pl.pallas_call(
        body,
        name=name,
        grid=(n_l, nb),
        in_specs=in_specs,
        out_specs=[stacked] * 4,
        out_shape=[jax.ShapeDtypeStruct((n_l * r, c), F32)] * 4,
        compiler_params=_params(("arbitrary", "arbitrary")),
    )(*operands)
    return tuple(o.reshape(n_l, r, c) for o in outs)


def _train_step(x, target, weights, m_state, v_state):
    me = _block_id(lax.axis_index("x"), lax.axis_index("y"), lax.axis_index("c"))

    small_shapes = [_shard_shape(n) for n in SMALL_SHARDED]
    gather = {0: _gather_start([_pack_rows([weights[n] for n in SMALL_SHARDED], SMALL_WIDTH, SUBLANES, F32)]
                               + [_shard_2d(n, weights[n][0]).astype(MXU_DTYPE) for n in MIXER_BIG[0]],
                               name="gather_g0_start")}
    cur, cur_m = x[0], _cast_input(x[0])
    gather[0] = _gather_pass_on(gather[0], name="gather_g0_pass", after=cur_m)
    first = _gather_finish(gather[0], name="gather_g0_finish", after=gather[0]["token"])
    small_all, first_matrices = first[0], first[1:]
    small_full = {n: weights[n] for n in SMALL if PARAMS[n][1] is None}
    for n, stacked in zip(SMALL_SHARDED, _unpack_rows(small_all, small_shapes, SMALL_WIDTH, SUBLANES)):
        small_full[n] = _merge_shards(stacked, PARAMS[n][1])

    def small_weights(i):
        wl = {}
        for n in SMALL:
            if n.startswith("ln_"):
                wl[n] = small_full[n][i:i + 1]
            elif n.startswith(("sc_", "mla_", "cf_")[i % N_MIXERS]):
                v = small_full[n][i // N_MIXERS]
                wl[n] = v if v.ndim == 2 else v.reshape(1, -1)
        return wl

    layers = [small_weights(i) for i in range(DEPTH)]
    groups = [(kind, i) for i in range(DEPTH) for kind in ("mix", "ff")]

    def group_names(g):
        kind, i = g
        if kind == "mix":
            return [(n, i // N_MIXERS) for n in MIXER_BIG[i % N_MIXERS]]
        return [("ff_w1", i), ("ff_w2", i)]

    def shards_of(g):
        return [_shard_2d(n, weights[n][j]).astype(MXU_DTYPE) for n, j in group_names(g)]

    def install(g, gathered):
        for (n, _), a in zip(group_names(g), gathered):
            layers[g[1]][n] = _as_operand(n, a)

    install(groups[0], first_matrices)
    gather[1] = _gather_start(shards_of(groups[1]), name="gather_g1_start", after=gather[0]["token"])
    saved = {}
    for k, g in enumerate(groups):
        kind, i = g
        tokens = [gather[1]["token"]] if k == 0 else []
        if k + 2 < len(groups):
            gather[k + 2] = _gather_start(shards_of(groups[k + 2]), name=f"gather_g{k + 2}_start",
                                          after=gather[k + 1]["token"])
            tokens.append(gather[k + 2]["token"])

        def pass_on(y, k=k):
            if k + 1 == len(groups):
                return None
            gather[k + 1] = _gather_pass_on(gather[k + 1], name=f"gather_g{k + 1}_pass", after=y)
            return gather[k + 1]["token"]

        phase = _mix_fwd if kind == "mix" else _ff_fwd
        cur, cur_m, saved[g] = phase(i, cur, cur_m, layers[i], hook=pass_on, after=tokens)
        if k + 1 < len(groups):
            install(groups[k + 1], _gather_finish(gather[k + 1], name=f"gather_g{k + 1}_finish", after=cur))

    loss_local, dy = _loss_and_grad(cur, target[0])
    loss = lax.psum(loss_local, MESH_AXES)

    reduced = {n: [None] * PARAMS[n][0][0] for n in BIG}
    small_grads = {n: [None] * PARAMS[n][0][0] for n in SMALL}
    reduce = {}

    def to_chips(p, after):
        st = reduce[p]
        parts, got = _swap_finish(st["sibling"], name=f"reduce_p{p}_sibling_finish", after=after)
        sums = _chip_partials_group(parts, got, name=f"reduce_p{p}_chipsum")
        st["own"] = [own for own, _ in sums]
        others = [o for _, o in sums]
        st["chips"] = _swap_start(others, [lax.empty(o.shape, o.dtype) for o in others], _to_chip, 3,
                                  name=f"reduce_p{p}_chips_start")
        return st["chips"]["token"]

    def arrived(p, after):
        st = reduce[p]
        _, got = _swap_finish(st["chips"], name=f"reduce_p{p}_chips_finish", after=after)
        for (n, j), own, arr in zip(st["names"], st["own"], got):
            reduced[n][j] = (own, arr)

    terms = [(1.0, dy)]
    back = list(reversed(groups))
    for p, g in enumerate(back):
        kind, i = g

        def after_first_kernel(dz, p=p):
            return to_chips(p - 1, dz) if p >= 1 else None

        phase = _mix_bwd if kind == "mix" else _ff_bwd
        tokens = [reduce[p - 1]["sibling"]["token"]] if p >= 1 else []
        terms, grads = phase(i, terms, saved[g], layers[i], hook=after_first_kernel, after=tokens)
        for n in SMALL:
            if n in grads:
                small_grads[n][i if n.startswith("ln_") else i // N_MIXERS] = grads[n]
        if p >= 2:
            arrived(p - 2, terms[-1][1])
        names = group_names(g)
        parts = [_as_parts(n, grads[n]) for n, _ in names]
        reduce[p] = {"names": names, "sibling": _swap_start(
            parts, [lax.empty((4, *a.shape[1:]), a.dtype) for a in parts], _to_sibling, 4,
            name=f"reduce_p{p}_sibling_start")}
    grad_x = _sum_terms(terms, name="grad_x")
    last = len(back) - 1
    small_mine = _pack_rows([jnp.stack(small_grads[n]).reshape(PARAMS[n][0]) for n in SMALL], SMALL_WIDTH, SUBLANES, F32)
    small_gather = _gather_start([small_mine], name="gather_small_grads_start")
    tail = to_chips(last, small_gather["token"])
    arrived(last - 1, tail)

    grad, delta, new_m, new_v = {}, {}, {}, {}

    def adamw_matrix(n, after):
        shape = weights[n].shape
        view = lambda a: a.reshape(shape[0], shape[1], -1)
        outs = _adamw_reduced(view(weights[n]), view(m_state[n]), view(v_state[n]), reduced[n], name=f"adamw_{n}",
                              after=after)
        grad[n], delta[n], new_m[n], new_v[n] = [o.reshape(shape) for o in outs]
        return outs[0]

    done = tail
    for n in BIG:
        if n not in MIXER_BIG[0]:
            done = adamw_matrix(n, [done])
    small_gather = _gather_pass_on(small_gather, name="gather_small_grads_pass", after=done)
    small_all = _gather_finish(small_gather, name="gather_small_grads_finish", after=small_gather["token"])[0]
    small_sum = _sum_blocks(None, small_all, name="sum_small_grads")
    for n, g in zip(SMALL, _unpack_rows(small_sum, [PARAMS[n][0] for n in SMALL], SMALL_WIDTH, SUBLANES)):
        ax = PARAMS[n][1]
        grad[n] = g if ax is None else lax.dynamic_index_in_dim(_split_shards(g, ax), me, axis=0, keepdims=False)
    small_shards = [_shard_shape(n) for n in SMALL]
    packed = [_pack_rows([src[n] for n in SMALL], SMALL_WIDTH, SUBLANES, F32) for src in (weights, grad, m_state, v_state)]
    small_out = _adamw(*packed, name="adamw_small")
    for outs, dst in zip(small_out, (delta, new_m, new_v)):
        for n, a in zip(SMALL, _unpack_rows(outs, small_shards, SMALL_WIDTH, SUBLANES)):
            dst[n] = a
    arrived(last, small_sum)
    for n in MIXER_BIG[0]:
        adamw_matrix(n, [])
    return (loss, grad_x[None], *[grad[n] for n in NAMES], *[delta[n] for n in NAMES],
            *[new_m[n] for n in NAMES], *[new_v[n] for n in NAMES])


def kernel(x, sc_w_in, sc_conv_w, sc_w_out, mla_w_dq, mla_g_q, mla_w_uq, mla_w_dkv, mla_g_kv, mla_w_uk, mla_w_uv, mla_w_o, cf_w_pw1, cf_b_pw1, cf_dw_w, cf_dw_b, cf_norm_g, cf_norm_b, cf_w_pw2, cf_b_pw2, ff_w1, ff_w2, ln_mix_g, ln_mix_b, ln_ff_g, ln_ff_b, loss_target, m_sc_w_in, m_sc_conv_w, m_sc_w_out, m_mla_w_dq, m_mla_g_q, m_mla_w_uq, m_mla_w_dkv, m_mla_g_kv, m_mla_w_uk, m_mla_w_uv, m_mla_w_o, m_cf_w_pw1, m_cf_b_pw1, m_cf_dw_w, m_cf_dw_b, m_cf_norm_g, m_cf_norm_b, m_cf_w_pw2, m_cf_b_pw2, m_ff_w1, m_ff_w2, m_ln_mix_g, m_ln_mix_b, m_ln_ff_g, m_ln_ff_b, v_sc_w_in, v_sc_conv_w, v_sc_w_out, v_mla_w_dq, v_mla_g_q, v_mla_w_uq, v_mla_w_dkv, v_mla_g_kv, v_mla_w_uk, v_mla_w_uv, v_mla_w_o, v_cf_w_pw1, v_cf_b_pw1, v_cf_dw_w, v_cf_dw_b, v_cf_norm_g, v_cf_norm_b, v_cf_w_pw2, v_cf_b_pw2, v_ff_w1, v_ff_w2, v_ln_mix_g, v_ln_mix_b, v_ln_ff_g, v_ln_ff_b):
    w_list = (sc_w_in, sc_conv_w, sc_w_out, mla_w_dq, mla_g_q, mla_w_uq, mla_w_dkv, mla_g_kv, mla_w_uk, mla_w_uv, mla_w_o,
              cf_w_pw1, cf_b_pw1, cf_dw_w, cf_dw_b, cf_norm_g, cf_norm_b, cf_w_pw2, cf_b_pw2, ff_w1, ff_w2, ln_mix_g,
              ln_mix_b, ln_ff_g, ln_ff_b)
    m_list = (m_sc_w_in, m_sc_conv_w, m_sc_w_out, m_mla_w_dq, m_mla_g_q, m_mla_w_uq, m_mla_w_dkv, m_mla_g_kv, m_mla_w_uk,
              m_mla_w_uv, m_mla_w_o, m_cf_w_pw1, m_cf_b_pw1, m_cf_dw_w, m_cf_dw_b, m_cf_norm_g, m_cf_norm_b, m_cf_w_pw2,
              m_cf_b_pw2, m_ff_w1, m_ff_w2, m_ln_mix_g, m_ln_mix_b, m_ln_ff_g, m_ln_ff_b)
    v_list = (v_sc_w_in, v_sc_conv_w, v_sc_w_out, v_mla_w_dq, v_mla_g_q, v_mla_w_uq, v_mla_w_dkv, v_mla_g_kv, v_mla_w_uk,
              v_mla_w_uv, v_mla_w_o, v_cf_w_pw1, v_cf_b_pw1, v_cf_dw_w, v_cf_dw_b, v_cf_norm_g, v_cf_norm_b, v_cf_w_pw2,
              v_cf_b_pw2, v_ff_w1, v_ff_w2, v_ln_mix_g, v_ln_mix_b, v_ln_ff_g, v_ln_ff_b)
    return _train_step(x, loss_target, dict(zip(NAMES, w_list)), dict(zip(NAMES, m_list)), dict(zip(NAMES, v_list)))
```

```python
import functools

import jax
import jax.numpy as jnp
from jax import lax
from jax.experimental import pallas as pl
from jax.experimental.pallas import tpu as pltpu

F32 = jnp.float32
MXU_DTYPE = jnp.bfloat16

N_DEV = 8
D_MODEL = 1024
DEPTH = 4
N_MIXERS = 3
CHUNK = 64
ALPHA = (2.0 * DEPTH) ** 0.25
LN_EPS = 1e-5
RMS_EPS = 1e-6
MLA_HEADS = 8
QK_NOPE = 128
QK_ROPE = 64
V_HEAD = 128
Q_LORA = 384
KV_LORA = 256
ROPE_THETA = 10000.0
ADAM_LR = 0.001
ADAM_B1 = 0.9
ADAM_B2 = 0.999
ADAM_EPS = 1e-08
ADAM_WD = 0.01
ADAM_STEP = 10

LANES = 128
SUBLANES = 8
BF16_ROWS = 16
VMEM_LIMIT = 56 * 1024 * 1024
MM_VMEM_BUDGET = 36 * 1024 * 1024
CONV_PAD = 32
MASK_VALUE = -1e30

MESH_AXES = ("x", "y", "c")


def _params(semantics):
    return pltpu.CompilerParams(dimension_semantics=semantics, vmem_limit_bytes=VMEM_LIMIT)


def _tile(n, cap):
    if n <= cap:
        return n
    t = cap - cap % LANES
    while t >= LANES:
        if n % t == 0:
            return t
        t -= LANES
    raise ValueError(f"no tile for {n} under {cap}")


def _row_tile(rows, cap):
    if rows <= cap:
        return rows
    t = cap - cap % SUBLANES
    while t >= SUBLANES:
        if rows % t == 0:
            return t
        t -= SUBLANES
    raise ValueError(f"no row tile for {rows} under {cap}")


def _mm(a, b, *, name, ta=False, tb=False, b_dev=False, out_dev=False, out_dtypes=(F32,), epilogue=None,
        tile_extras=(), col_extras=(), after=(), tm_cap=2048, tn_cap=1024, tk_cap=4096):
    m, k = (a.shape[1], a.shape[0]) if ta else a.shape
    if b_dev:
        nd, b_rows, ns = b.shape
        n, kb = (b_rows, nd * ns) if tb else (nd * ns, b_rows)
    else:
        n, kb = (b.shape[0], b.shape[1]) if tb else (b.shape[1], b.shape[0])
    assert k == kb, (a.shape, b.shape, ta, tb)
    if ta or tile_extras or (b_dev and tb):
        tm_cap = min(tm_cap, 1024)
    tk = ns if (b_dev and tb) else _tile(k, tk_cap)
    nk = k // tk
    fixed_tn = ns if (b_dev and not tb) else (n // N_DEV if out_dev else None)
    if out_dev:
        assert not tile_extras and not col_extras and n % N_DEV == 0
    out_bytes = sum(jnp.dtype(dt).itemsize for dt in out_dtypes) + sum(e.dtype.itemsize for e in tile_extras)
    tm, tn = None, None
    for cand_m, cand_n in ((tm_cap, tn_cap), (tm_cap, tn_cap // 2), (tm_cap // 2, tn_cap), (tm_cap // 2, tn_cap // 2),
                           (tm_cap // 4, tn_cap // 2), (tm_cap // 4, tn_cap // 4)):
        tm, tn = _tile(m, cand_m), (fixed_tn if fixed_tn is not None else _tile(n, cand_n))
        need = 2 * (tm * tk * a.dtype.itemsize + tk * tn * b.dtype.itemsize) + 2 * tm * tn * out_bytes
        need += tm * tn * 4 * (2 if nk > 1 else 1)
        if need <= MM_VMEM_BUDGET:
            break
    n_te, n_ce, n_out = len(tile_extras), len(col_extras), len(out_dtypes)
    dims = (((0 if ta else 1,), (1 if tb else 0,)), ((), ()))

    def body(*refs):
        a_ref, b_ref = refs[0], refs[1]
        te_refs = refs[2:2 + n_te]
        ce_refs = refs[2 + n_te:2 + n_te + n_ce]
        first_out = 2 + n_te + n_ce + len(after)
        out_refs = refs[first_out:first_out + n_out]

        def finish(acc):
            if epilogue is None:
                outs = (acc,)
            else:
                outs = epilogue(acc, *[r[...] for r in te_refs], *[r[...] for r in ce_refs])
            for o_ref, o in zip(out_refs, outs):
                o_ref[...] = o.astype(o_ref.dtype)

        def product():
            return lax.dot_general(a_ref[...].astype(MXU_DTYPE), b_ref[...].astype(MXU_DTYPE), dims,
                                   preferred_element_type=F32)

        if nk == 1:
            finish(product())
        else:
            acc_ref = refs[-1]
            kk = pl.program_id(2)

            @pl.when(kk == 0)
            def _():
                acc_ref[...] = jnp.zeros_like(acc_ref)

            acc_ref[...] += product()

            @pl.when(kk == nk - 1)
            def _():
                finish(acc_ref[...])

    a_spec = pl.BlockSpec((tk, tm), lambda i, j, kk: (kk, i)) if ta else pl.BlockSpec((tm, tk), lambda i, j, kk: (i, kk))
    if b_dev and tb:
        b_spec = pl.BlockSpec((None, tn, tk), lambda i, j, kk: (kk, j, 0))
    elif b_dev:
        b_spec = pl.BlockSpec((None, tk, tn), lambda i, j, kk: (j, kk, 0))
    elif tb:
        b_spec = pl.BlockSpec((tn, tk), lambda i, j, kk: (j, kk))
    else:
        b_spec = pl.BlockSpec((tk, tn), lambda i, j, kk: (kk, j))
    tile_spec = pl.BlockSpec((tm, tn), lambda i, j, kk: (i, j))
    col_spec = pl.BlockSpec((1, tn), lambda i, j, kk: (0, j))
    if out_dev:
        out_spec = pl.BlockSpec((None, tm, tn), lambda i, j, kk: (j, i, 0))
        out_shape = [jax.ShapeDtypeStruct((N_DEV, m, tn), dt) for dt in out_dtypes]
    else:
        out_spec = tile_spec
        out_shape = [jax.ShapeDtypeStruct((m, n), dt) for dt in out_dtypes]
    outs = pl.pallas_call(
        body,
        name=name,
        grid=(m // tm, n // tn, nk),
        in_specs=[a_spec, b_spec] + [tile_spec] * n_te + [col_spec] * n_ce + [ANY] * len(after),
        out_specs=[out_spec] * n_out,
        out_shape=out_shape,
        scratch_shapes=[pltpu.VMEM((tm, tn), F32)] if nk > 1 else [],
        compiler_params=_params(("parallel", "parallel", "arbitrary")),
    )(a, b, *tile_extras, *col_extras, *after)
    return outs[0] if n_out == 1 else tuple(outs)


def _rowwise(fn, rows, bcast, out_rows, out_accs, *, name, tb_cap=256, after=()):
    t = rows[0].shape[0]
    tb = _row_tile(t, tb_cap)
    n_r, n_b, n_o, n_a = len(rows), len(bcast), len(out_rows), len(out_accs)

    def body(*refs):
        r_refs = refs[:n_r]
        b_refs = refs[n_r:n_r + n_b]
        first_out = n_r + n_b + len(after)
        o_refs = refs[first_out:first_out + n_o]
        a_refs = refs[first_out + n_o:]
        outs, accs = fn([r[...] for r in r_refs], [r[...] for r in b_refs])
        for o_ref, o in zip(o_refs, outs):
            o_ref[...] = o.astype(o_ref.dtype)
        if n_a:
            i = pl.program_id(0)

            @pl.when(i == 0)
            def _():
                for a_ref, acc in zip(a_refs, accs):
                    a_ref[...] = acc

            @pl.when(i > 0)
            def _():
                for a_ref, acc in zip(a_refs, accs):
                    a_ref[...] += acc

    in_specs = [pl.BlockSpec((tb, r.shape[1]), lambda i: (i, 0)) for r in rows]
    in_specs += [pl.BlockSpec(b.shape, lambda i: (0, 0)) for b in bcast]
    in_specs += [ANY] * len(after)
    out_specs = [pl.BlockSpec((tb, c), lambda i: (i, 0)) for c, _ in out_rows]
    out_specs += [pl.BlockSpec(s, lambda i: (0, 0)) for s in out_accs]
    out_shape = [jax.ShapeDtypeStruct((t, c), dt) for c, dt in out_rows]
    out_shape += [jax.ShapeDtypeStruct(s, F32) for s in out_accs]
    outs = pl.pallas_call(
        body,
        name=name,
        grid=(t // tb,),
        in_specs=in_specs,
        out_specs=out_specs,
        out_shape=out_shape,
        compiler_params=_params(("arbitrary",)),
    )(*rows, *bcast, *after)
    return tuple(outs)


def _colsum(v):
    return jnp.sum(v, axis=0, keepdims=True)


def _sigmoid(v):
    return 1.0 / (1.0 + jnp.exp(-v))


def _ln_fwd(x_res, y_sub, g, b, *, name, after=()):
    d = x_res.shape[1]

    def fn(rows, bc):
        z = ALPHA * rows[0] + rows[1]
        mu = jnp.mean(z, axis=-1, keepdims=True)
        zc = z - mu
        var = jnp.mean(zc * zc, axis=-1, keepdims=True)
        rstd = lax.rsqrt(var + LN_EPS)
        xhat = zc * rstd
        out = xhat * bc[0] + bc[1]
        return (out, out, xhat, rstd), ()

    return _rowwise(fn, [x_res, y_sub], [g, b], [(d, F32), (d, MXU_DTYPE), (d, F32), (1, F32)], [], name=name,
                    after=after, tb_cap=512)


def _ln_bwd(terms, xhat, rstd, g, *, name, after=()):
    d = xhat.shape[1]
    coefs = [c for c, _ in terms]

    def fn(rows, bc):
        xh, rs = rows[0], rows[1]
        dout = None
        for c, v in zip(coefs, rows[2:]):
            tv = v if c == 1.0 else c * v
            dout = tv if dout is None else dout + tv
        dxh = dout * bc[0]
        m1 = jnp.mean(dxh, axis=-1, keepdims=True)
        m2 = jnp.mean(dxh * xh, axis=-1, keepdims=True)
        dz = rs * (dxh - m1 - xh * m2)
        return (dz, dz), (_colsum(dout * xh), _colsum(dout), _colsum(dz))

    return _rowwise(fn, [xhat, rstd] + [v for _, v in terms], [g], [(d, F32), (d, MXU_DTYPE)],
                    [(1, d), (1, d), (1, d)], name=name, after=after, tb_cap=512)


def _rms_fwd(x, g, *, name):
    c = x.shape[1]

    def fn(rows, bc):
        v = rows[0]
        r = lax.rsqrt(jnp.mean(v * v, axis=-1, keepdims=True) + RMS_EPS)
        return (v * r * bc[0],), ()

    return _rowwise(fn, [x], [g], [(c, MXU_DTYPE)], [], name=name)[0]


def _rms_bwd(x, g, dy, *, name):
    c = x.shape[1]

    def fn(rows, bc):
        v, dyv = rows
        r = lax.rsqrt(jnp.mean(v * v, axis=-1, keepdims=True) + RMS_EPS)
        xh = v * r
        dxh = dyv * bc[0]
        dx = r * (dxh - xh * jnp.mean(dxh * xh, axis=-1, keepdims=True))
        return (dx,), (_colsum(dyv * xh),)

    return _rowwise(fn, [x, dy], [g], [(c, F32)], [(1, c)], name=name)


def _conv_chunk(t):
    return _row_tile(t, 256)


def _conv_fwd(x, w, bias, *, name):
    t, c = x.shape
    kw = w.shape[0]
    ch = _conv_chunk(t)

    def body(x_ref, w_ref, b_ref, o_ref, pad_ref):
        pad_ref[0:CONV_PAD, :] = jnp.zeros((CONV_PAD, LANES), F32)
        pad_ref[CONV_PAD:CONV_PAD + t, :] = x_ref[...]
        for t0 in range(0, t, ch):
            acc = jnp.broadcast_to(b_ref[...], (ch, LANES))
            for k in range(kw):
                s = kw - 1 - k
                acc = acc + w_ref[k:k + 1, :] * pad_ref[CONV_PAD - s + t0:CONV_PAD - s + t0 + ch, :]
            o_ref[t0:t0 + ch, :] = acc

    return pl.pallas_call(
        body,
        name=name,
        grid=(c // LANES,),
        in_specs=[pl.BlockSpec((t, LANES), lambda j: (0, j)), pl.BlockSpec((kw, LANES), lambda j: (0, j)),
                  pl.BlockSpec((1, LANES), lambda j: (0, j))],
        out_specs=pl.BlockSpec((t, LANES), lambda j: (0, j)),
        out_shape=jax.ShapeDtypeStruct((t, c), F32),
        scratch_shapes=[pltpu.VMEM((CONV_PAD + t, LANES), F32)],
        compiler_params=_params(("parallel",)),
    )(x, w, bias)


def _conv_bwd(dy, x, w, *, name):
    t, c = x.shape
    kw = w.shape[0]
    ch = _conv_chunk(t)

    def body(dy_ref, x_ref, w_ref, dx_ref, dw_ref, xpad_ref, dpad_ref):
        xpad_ref[0:CONV_PAD, :] = jnp.zeros((CONV_PAD, LANES), F32)
        xpad_ref[CONV_PAD:CONV_PAD + t, :] = x_ref[...]
        dpad_ref[0:t, :] = dy_ref[...]
        dpad_ref[t:t + CONV_PAD, :] = jnp.zeros((CONV_PAD, LANES), F32)
        for t0 in range(0, t, ch):
            acc = jnp.zeros((ch, LANES), F32)
            for k in range(kw):
                s = kw - 1 - k
                acc = acc + w_ref[k:k + 1, :] * dpad_ref[t0 + s:t0 + s + ch, :]
            dx_ref[t0:t0 + ch, :] = acc
        for k in range(kw):
            s = kw - 1 - k
            acc = jnp.zeros((ch, LANES), F32)
            for t0 in range(0, t, ch):
                acc = acc + dy_ref[t0:t0 + ch, :] * xpad_ref[CONV_PAD - s + t0:CONV_PAD - s + t0 + ch, :]
            dw_ref[k:k + 1, :] = _colsum(acc)

    blk = pl.BlockSpec((t, LANES), lambda j: (0, j))
    wblk = pl.BlockSpec((kw, LANES), lambda j: (0, j))
    return pl.pallas_call(
        body,
        name=name,
        grid=(c // LANES,),
        in_specs=[blk, blk, wblk],
        out_specs=[blk, wblk],
        out_shape=[jax.ShapeDtypeStruct((t, c), F32), jax.ShapeDtypeStruct((kw, c), F32)],
        scratch_shapes=[pltpu.VMEM((CONV_PAD + t, LANES), F32), pltpu.VMEM((CONV_PAD + t, LANES), F32)],
        compiler_params=_params(("parallel",)),
    )(dy, x, w)


ATTN_SCALE = (QK_NOPE + QK_ROPE) ** -0.5
_NT = (((1,), (1,)), ((), ()))
_TN = (((0,), (0,)), ((), ()))
_NN = (((1,), (0,)), ((), ()))


def _dot(a, b, dims):
    return lax.dot_general(a, b, dims, preferred_element_type=F32)


def _block_scores(qn, qp, kn_ref, kp_ref, lo, hi, diagonal):
    s = _dot(qn, kn_ref[lo:hi, :], _NT) + _dot(qp, kp_ref[lo:hi, :], _NT)
    s = s * ATTN_SCALE
    if diagonal:
        shift = CHUNK.bit_length() - 1
        rows = lax.broadcasted_iota(jnp.int32, s.shape, 0)
        cols = lax.broadcasted_iota(jnp.int32, s.shape, 1)
        s = jnp.where(jnp.right_shift(cols, shift) <= jnp.right_shift(rows, shift), s, MASK_VALUE)
    return s


def _attn_fwd(qn, qp, kn, kp, v, *, name):
    h, t, _ = qp.shape
    tq = _row_tile(t, 256)
    assert tq % CHUNK == 0

    def body(qn_ref, qp_ref, kn_ref, kp_ref, v_ref, o_ref, of_ref, lse_ref):
        for r0 in range(0, t, tq):
            r1 = r0 + tq
            qn, qp = qn_ref[r0:r1, :].astype(MXU_DTYPE), qp_ref[r0:r1, :]
            s_d = _block_scores(qn, qp, kn_ref, kp_ref, r0, r1, True)
            m = jnp.max(s_d, axis=-1, keepdims=True)
            if r0:
                s_b = _block_scores(qn, qp, kn_ref, kp_ref, 0, r0, False)
                m = jnp.maximum(m, jnp.max(s_b, axis=-1, keepdims=True))
            p_d = jnp.exp(s_d - m)
            l = jnp.sum(p_d, axis=-1, keepdims=True)
            acc = _dot(p_d.astype(MXU_DTYPE), v_ref[r0:r1, :], _NN)
            if r0:
                p_b = jnp.exp(s_b - m)
                l = l + jnp.sum(p_b, axis=-1, keepdims=True)
                acc = acc + _dot(p_b.astype(MXU_DTYPE), v_ref[0:r0, :], _NN)
            o = acc / l
            o_ref[r0:r1, :] = o.astype(o_ref.dtype)
            of_ref[r0:r1, :] = o
            lse_ref[r0:r1, :] = m + jnp.log(l)

    blk = lambda w: pl.BlockSpec((None, t, w), lambda hh: (hh, 0, 0))
    cols = pl.BlockSpec((t, QK_NOPE), lambda hh: (0, hh))
    return pl.pallas_call(
        body,
        name=name,
        grid=(h,),
        in_specs=[cols, blk(QK_ROPE), cols, pl.BlockSpec((t, QK_ROPE), lambda hh: (0, 0)), cols],
        out_specs=[cols, cols, blk(1)],
        out_shape=[jax.ShapeDtypeStruct((t, h * V_HEAD), MXU_DTYPE), jax.ShapeDtypeStruct((t, h * V_HEAD), F32),
                   jax.ShapeDtypeStruct((h, t, 1), F32)],
        compiler_params=_params(("parallel",)),
    )(qn, qp, kn, kp, v)


def _attn_bwd(qn, qp, kn, kp, v, do, o, lse, *, name):
    h, t, _ = qp.shape
    tq = _row_tile(t, 256)

    def body(qn_ref, qp_ref, kn_ref, kp_ref, v_ref, do_ref, o_ref, lse_ref, dqn_ref, dqp_ref, dkn_ref, dkp_ref,
             dv_ref):
        dkn_ref[...] = jnp.zeros_like(dkn_ref)
        dv_ref[...] = jnp.zeros_like(dv_ref)

        @pl.when(pl.program_id(0) == 0)
        def _():
            dkp_ref[...] = jnp.zeros_like(dkp_ref)

        for r0 in range(0, t, tq):
            r1 = r0 + tq
            qn, qp = qn_ref[r0:r1, :].astype(MXU_DTYPE), qp_ref[r0:r1, :]
            dov, lse = do_ref[r0:r1, :], lse_ref[r0:r1, :]
            delta = jnp.sum(dov.astype(F32) * o_ref[r0:r1, :], axis=-1, keepdims=True)

            def piece(lo, hi, diagonal):
                s = _block_scores(qn, qp, kn_ref, kp_ref, lo, hi, diagonal)
                p = jnp.exp(s - lse)
                dp = _dot(dov, v_ref[lo:hi, :], _NT)
                ds_m = (p * (dp - delta) * ATTN_SCALE).astype(MXU_DTYPE)
                dkn_ref[lo:hi, :] += _dot(ds_m, qn, _TN)
                dkp_ref[lo:hi, :] += _dot(ds_m, qp, _TN)
                dv_ref[lo:hi, :] += _dot(p.astype(MXU_DTYPE), dov, _TN)
                return _dot(ds_m, kn_ref[lo:hi, :], _NN), _dot(ds_m, kp_ref[lo:hi, :], _NN)

            dqn, dqp = piece(r0, r1, True)
            if r0:
                dqn_b, dqp_b = piece(0, r0, False)
                dqn, dqp = dqn + dqn_b, dqp + dqp_b
            dqn_ref[r0:r1, :] = dqn
            dqp_ref[r0:r1, :] = dqp

    blk = lambda w: pl.BlockSpec((None, t, w), lambda hh: (hh, 0, 0))
    kpblk = pl.BlockSpec((t, QK_ROPE), lambda hh: (0, 0))
    cols = pl.BlockSpec((t, QK_NOPE), lambda hh: (0, hh))
    return pl.pallas_call(
        body,
        name=name,
        grid=(h,),
        in_specs=[cols, blk(QK_ROPE), cols, kpblk, cols, cols, cols, blk(1)],
        out_specs=[cols, blk(QK_ROPE), cols, kpblk, cols],
        out_shape=[jax.ShapeDtypeStruct((t, h * QK_NOPE), F32), jax.ShapeDtypeStruct((h, t, QK_ROPE), F32),
                   jax.ShapeDtypeStruct((t, h * QK_NOPE), F32), jax.ShapeDtypeStruct((t, QK_ROPE), F32),
                   jax.ShapeDtypeStruct((t, h * V_HEAD), F32)],
        compiler_params=_params(("arbitrary",)),
    )(qn, qp, kn, kp, v, do, o, lse)


def _rope(x1, x2, cos, sin, *, name, inverse=False):
    w = x1.shape[1]
    sign = -1.0 if inverse else 1.0

    def fn(rows, bc):
        a, b, c, s = rows
        s = sign * s
        return (a * c - b * s, a * s + b * c), ()

    return _rowwise(fn, [x1, x2, cos, sin], [], [(w, F32), (w, F32)], [], name=name)


def _mlp_fwd(x_m, w1, w2, tag, after=()):
    def epi(acc):
        r = jnp.maximum(acc, 0.0)
        return (r * r,)

    act = _mm(x_m, w1, name=f"{tag}_up", out_dtypes=(MXU_DTYPE,), epilogue=epi, after=after)
    y = _mm(act, w2, name=f"{tag}_down")
    return y, (x_m, act)


def _mlp_bwd(dy_m, saved, w1, w2, tag, after=()):
    x_m, act = saved
    dw2 = _mm(act, dy_m, ta=True, name=f"{tag}_dw2", out_dtypes=(MXU_DTYPE,))

    def epi(acc, av):
        return (acc * (2.0 * jnp.sqrt(av.astype(F32))),)

    dh = _mm(dy_m, w2, tb=True, name=f"{tag}_dact", out_dtypes=(MXU_DTYPE,), epilogue=epi, tile_extras=(act,),
             after=after)
    dw1 = _mm(x_m, dh, ta=True, out_dev=True, name=f"{tag}_dw1", out_dtypes=(MXU_DTYPE,))
    dx = _mm(dh, w1, tb=True, name=f"{tag}_dx")
    return [dx], {"ff_w1": dw1, "ff_w2": dw2}


def _sc_gate_specs(t, d):
    per = d // LANES
    return [pl.BlockSpec((t, LANES), lambda j, k=k: (0, k * per + j)) for k in range(3)]


def _sc_gate_fwd(u, conv_w, *, name):
    t, d = u.shape[0], u.shape[1] // 3
    kw = conv_w.shape[0]
    ch = _conv_chunk(t)

    def body(b_ref, c_ref, h_ref, w_ref, r_ref, pad_ref):
        pad_ref[0:CONV_PAD, :] = jnp.zeros((CONV_PAD, LANES), F32)
        pad_ref[CONV_PAD:CONV_PAD + t, :] = c_ref[...] * h_ref[...]
        for t0 in range(0, t, ch):
            q = jnp.zeros((ch, LANES), F32)
            for k in range(kw):
                s = kw - 1 - k
                q = q + w_ref[k:k + 1, :] * pad_ref[CONV_PAD - s + t0:CONV_PAD - s + t0 + ch, :]
            r_ref[t0:t0 + ch, :] = (b_ref[t0:t0 + ch, :] * q).astype(r_ref.dtype)

    blk = pl.BlockSpec((t, LANES), lambda j: (0, j))
    return pl.pallas_call(
        body,
        name=name,
        grid=(d // LANES,),
        in_specs=_sc_gate_specs(t, d) + [pl.BlockSpec((kw, LANES), lambda j: (0, j))],
        out_specs=blk,
        out_shape=jax.ShapeDtypeStruct((t, d), MXU_DTYPE),
        scratch_shapes=[pltpu.VMEM((CONV_PAD + t, LANES), F32)],
        compiler_params=_params(("parallel",)),
    )(u, u, u, conv_w)


def _sc_gate_bwd(dr, u, conv_w, *, name):
    t, d = u.shape[0], u.shape[1] // 3
    kw = conv_w.shape[0]
    ch = _conv_chunk(t)

    def body(dr_ref, b_ref, c_ref, h_ref, w_ref, db_ref, dp_ref, dw_ref, ppad_ref, dpad_ref):
        ppad_ref[0:CONV_PAD, :] = jnp.zeros((CONV_PAD, LANES), F32)
        ppad_ref[CONV_PAD:CONV_PAD + t, :] = c_ref[...] * h_ref[...]
        dpad_ref[0:t, :] = dr_ref[...] * b_ref[...]
        dpad_ref[t:t + CONV_PAD, :] = jnp.zeros((CONV_PAD, LANES), F32)
        for t0 in range(0, t, ch):
            q = jnp.zeros((ch, LANES), F32)
            dp = jnp.zeros((ch, LANES), F32)
            for k in range(kw):
                s = kw - 1 - k
                q = q + w_ref[k:k + 1, :] * ppad_ref[CONV_PAD - s + t0:CONV_PAD - s + t0 + ch, :]
                dp = dp + w_ref[k:k + 1, :] * dpad_ref[t0 + s:t0 + s + ch, :]
            db_ref[t0:t0 + ch, :] = (dr_ref[t0:t0 + ch, :] * q).astype(db_ref.dtype)
            dp_ref[t0:t0 + ch, :] = dp
        for k in range(kw):
            s = kw - 1 - k
            acc = jnp.zeros((ch, LANES), F32)
            for t0 in range(0, t, ch):
                acc = acc + dpad_ref[t0:t0 + ch, :] * ppad_ref[CONV_PAD - s + t0:CONV_PAD - s + t0 + ch, :]
            dw_ref[k:k + 1, :] = _colsum(acc)

    blk = pl.BlockSpec((t, LANES), lambda j: (0, j))
    wblk = pl.BlockSpec((kw, LANES), lambda j: (0, j))
    return pl.pallas_call(
        body,
        name=name,
        grid=(d // LANES,),
        in_specs=[blk] + _sc_gate_specs(t, d) + [wblk],
        out_specs=[blk, blk, wblk],
        out_shape=[jax.ShapeDtypeStruct((t, d), MXU_DTYPE), jax.ShapeDtypeStruct((t, d), F32),
                   jax.ShapeDtypeStruct((kw, d), F32)],
        scratch_shapes=[pltpu.VMEM((CONV_PAD + t, LANES), F32), pltpu.VMEM((CONV_PAD + t, LANES), F32)],
        compiler_params=_params(("parallel",)),
    )(dr, u, u, u, conv_w)


def _sc_fwd(x_m, w_in, conv_w, w_out, tag, after=()):
    u = _mm(x_m, w_in, name=f"{tag}_in", after=after)
    r = _sc_gate_fwd(u, conv_w, name=f"{tag}_gate")
    y = _mm(r, w_out, name=f"{tag}_out")
    return y, (x_m, u, r)


def _sc_bwd(dy_m, saved, w_in, conv_w, w_out, tag, after=()):
    d = D_MODEL
    x_m, u, r = saved
    dw_out = _mm(r, dy_m, ta=True, name=f"{tag}_dwout", out_dtypes=(MXU_DTYPE,))
    dr = _mm(dy_m, w_out, tb=True, name=f"{tag}_dr", after=after)
    db, dp, dconv = _sc_gate_bwd(dr, u, conv_w, name=f"{tag}_dgate")

    def du_fn(rows, bc):
        dbv, dpv, uv = rows
        return (jnp.concatenate([dbv.astype(F32), dpv * uv[:, 2 * d:3 * d], dpv * uv[:, d:2 * d]], axis=1),), ()

    du = _rowwise(du_fn, [db, dp, u], [], [(3 * d, MXU_DTYPE)], [], name=f"{tag}_du")[0]
    dw_in = _mm(x_m, du, ta=True, out_dev=True, name=f"{tag}_dwin", out_dtypes=(MXU_DTYPE,))
    dx = _mm(du, w_in, tb=True, name=f"{tag}_dx")
    return [dx], {"sc_w_in": dw_in, "sc_conv_w": dconv, "sc_w_out": dw_out}


def _cf_fwd(x_m, w, tag, after=()):
    d = D_MODEL

    def bias_epi(acc, bias):
        return (acc + bias,)

    u = _mm(x_m, w["cf_w_pw1"], name=f"{tag}_pw1", epilogue=bias_epi, col_extras=(w["cf_b_pw1"],),
            after=after)

    def glu_fn(rows, bc):
        uv = rows[0]
        return (uv[:, 0:d] * _sigmoid(uv[:, d:2 * d]),), ()

    h1 = _rowwise(glu_fn, [u], [], [(d, F32)], [], name=f"{tag}_glu")[0]
    h2 = _conv_fwd(h1, w["cf_dw_w"], w["cf_dw_b"], name=f"{tag}_conv")

    def norm_fn(rows, bc):
        hv = rows[0]
        mu = jnp.mean(hv, axis=-1, keepdims=True)
        hc = hv - mu
        rstd = lax.rsqrt(jnp.mean(hc * hc, axis=-1, keepdims=True) + LN_EPS)
        h3 = hc * rstd * bc[0] + bc[1]
        return (h3 * _sigmoid(h3),), ()

    h4 = _rowwise(norm_fn, [h2], [w["cf_norm_g"], w["cf_norm_b"]], [(d, MXU_DTYPE)], [], name=f"{tag}_norm")[0]
    y = _mm(h4, w["cf_w_pw2"], name=f"{tag}_pw2", epilogue=bias_epi, col_extras=(w["cf_b_pw2"],))
    return y, (x_m, u, h1, h2, h4)


def _cf_bwd(dy_m, dy_colsum, saved, w, tag, after=()):
    d = D_MODEL
    x_m, u, h1, h2, h4 = saved
    dw_pw2 = _mm(h4, dy_m, ta=True, name=f"{tag}_dwpw2", out_dtypes=(MXU_DTYPE,))
    dh4 = _mm(dy_m, w["cf_w_pw2"], tb=True, name=f"{tag}_dh4", after=after)

    def dnorm_fn(rows, bc):
        dh4v, hv = rows
        g, b = bc
        mu = jnp.mean(hv, axis=-1, keepdims=True)
        hc = hv - mu
        rstd = lax.rsqrt(jnp.mean(hc * hc, axis=-1, keepdims=True) + LN_EPS)
        xh = hc * rstd
        h3 = xh * g + b
        sg = _sigmoid(h3)
        dh3 = dh4v * (sg * (1.0 + h3 * (1.0 - sg)))
        dxh = dh3 * g
        dh2 = rstd * (dxh - jnp.mean(dxh, axis=-1, keepdims=True) - xh * jnp.mean(dxh * xh, axis=-1, keepdims=True))
        return (dh2,), (_colsum(dh3 * xh), _colsum(dh3), _colsum(dh2))

    dh2, dnorm_g, dnorm_b, ddw_b = _rowwise(dnorm_fn, [dh4, h2], [w["cf_norm_g"], w["cf_norm_b"]], [(d, F32)],
                                            [(1, d), (1, d), (1, d)], name=f"{tag}_dnorm")
    dh1, ddw_w = _conv_bwd(dh2, h1, w["cf_dw_w"], name=f"{tag}_dconv")

    def dglu_fn(rows, bc):
        dh1v, uv = rows
        a, gate = uv[:, 0:d], uv[:, d:2 * d]
        sg = _sigmoid(gate)
        du = jnp.concatenate([dh1v * sg, dh1v * a * sg * (1.0 - sg)], axis=1)
        return (du,), (_colsum(du),)

    du, db_pw1 = _rowwise(dglu_fn, [dh1, u], [], [(2 * d, MXU_DTYPE)], [(1, 2 * d)], name=f"{tag}_dglu")
    dw_pw1 = _mm(x_m, du, ta=True, out_dev=True, name=f"{tag}_dwpw1", out_dtypes=(MXU_DTYPE,))
    dx = _mm(du, w["cf_w_pw1"], tb=True, name=f"{tag}_dx")
    grads = {"cf_w_pw1": dw_pw1, "cf_b_pw1": db_pw1, "cf_dw_w": ddw_w, "cf_dw_b": ddw_b, "cf_norm_g": dnorm_g,
             "cf_norm_b": dnorm_b, "cf_w_pw2": dw_pw2, "cf_b_pw2": dy_colsum}
    return [dx], grads


def _rope_tables(t):
    pos = jnp.arange(t, dtype=F32)
    inv_freq = ROPE_THETA ** (-jnp.arange(0, QK_ROPE, 2, dtype=F32) / QK_ROPE)
    ang = pos[:, None] * inv_freq[None, :]
    return jnp.cos(ang), jnp.sin(ang)


ROPE_HALF = QK_ROPE // 2
Q_NOPE_COLS = MLA_HEADS * QK_NOPE
Q_HALF_COLS = MLA_HEADS * ROPE_HALF


def _mla_fwd(x_m, w, tag, after=()):
    t = x_m.shape[0]
    cos, sin = _rope_tables(t)
    cos_h, sin_h = jnp.tile(cos, (1, MLA_HEADS)), jnp.tile(sin, (1, MLA_HEADS))
    cq_pre = _mm(x_m, w["mla_w_dq"], name=f"{tag}_dq", after=after)
    cq = _rms_fwd(cq_pre, w["mla_g_q"], name=f"{tag}_qnorm")
    q = _mm(cq, w["mla_w_uq"], name=f"{tag}_uq")
    r1, r2 = _rope(q[:, Q_NOPE_COLS:Q_NOPE_COLS + Q_HALF_COLS], q[:, Q_NOPE_COLS + Q_HALF_COLS:], cos_h, sin_h,
                   name=f"{tag}_qrope")
    qp_h = jnp.concatenate([r1.reshape(t, MLA_HEADS, ROPE_HALF), r2.reshape(t, MLA_HEADS, ROPE_HALF)],
                           axis=-1).transpose(1, 0, 2).astype(MXU_DTYPE)
    ckv_full = _mm(x_m, w["mla_w_dkv"], name=f"{tag}_dkv")
    ckv_pre = ckv_full[:, :KV_LORA]
    ckv = _rms_fwd(ckv_pre, w["mla_g_kv"], name=f"{tag}_kvnorm")
    k1, k2 = _rope(ckv_full[:, KV_LORA:KV_LORA + ROPE_HALF], ckv_full[:, KV_LORA + ROPE_HALF:], cos, sin,
                   name=f"{tag}_krope")
    kp = jnp.concatenate([k1, k2], axis=-1).astype(MXU_DTYPE)
    kn = _mm(ckv, w["mla_w_uk"], name=f"{tag}_uk", out_dtypes=(MXU_DTYPE,))
    v = _mm(ckv, w["mla_w_uv"], name=f"{tag}_uv", out_dtypes=(MXU_DTYPE,))
    o, o_f, lse = _attn_fwd(q, qp_h, kn, kp, v, name=f"{tag}_attn")
    y = _mm(o, w["mla_w_o"], name=f"{tag}_o")
    return y, (x_m, cq_pre, cq, ckv_pre, ckv, q, qp_h, kn, kp, v, lse, o, o_f, (cos, sin, cos_h, sin_h))


def _mla_bwd(dy_m, saved, w, tag, after=()):
    x_m, cq_pre, cq, ckv_pre, ckv, q, qp_h, kn, kp, v, lse, o, o_f, (cos, sin, cos_h, sin_h) = saved
    t = x_m.shape[0]
    half = ROPE_HALF
    dw_o = _mm(o, dy_m, ta=True, name=f"{tag}_dwo", out_dtypes=(MXU_DTYPE,))
    do = _mm(dy_m, w["mla_w_o"], tb=True, name=f"{tag}_do", out_dtypes=(MXU_DTYPE,), after=after)
    dqn, dqp_h, dkn, dkp, dv = _attn_bwd(q, qp_h, kn, kp, v, do, o_f, lse, name=f"{tag}_dattn")
    dqp = dqp_h.transpose(1, 0, 2)
    d1, d2 = _rope(dqp[:, :, :half].reshape(t, -1), dqp[:, :, half:].reshape(t, -1), cos_h, sin_h,
                   name=f"{tag}_dqrope", inverse=True)
    dq = jnp.concatenate([dqn, d1, d2], axis=-1).astype(MXU_DTYPE)
    dw_uq = _mm(cq, dq, ta=True, name=f"{tag}_dwuq", out_dtypes=(MXU_DTYPE,))
    dcq = _mm(dq, w["mla_w_uq"], tb=True, name=f"{tag}_dcq")
    dcq_pre, dg_q = _rms_bwd(cq_pre, w["mla_g_q"], dcq, name=f"{tag}_dqnorm")
    dcq_pre_m = dcq_pre.astype(MXU_DTYPE)
    dw_dq = _mm(x_m, dcq_pre_m, ta=True, name=f"{tag}_dwdq", out_dtypes=(MXU_DTYPE,))
    dx_q = _mm(dcq_pre_m, w["mla_w_dq"], tb=True, name=f"{tag}_dxq")
    dw_uk = _mm(ckv, dkn, ta=True, name=f"{tag}_dwuk", out_dtypes=(MXU_DTYPE,))
    dw_uv = _mm(ckv, dv, ta=True, name=f"{tag}_dwuv", out_dtypes=(MXU_DTYPE,))
    dckv_k = _mm(dkn, w["mla_w_uk"], tb=True, name=f"{tag}_dckvk")
    dckv = _mm(dv, w["mla_w_uv"], tb=True, name=f"{tag}_dckv", epilogue=lambda acc, other: (acc + other,),
               tile_extras=(dckv_k,))
    dckv_pre, dg_kv = _rms_bwd(ckv_pre, w["mla_g_kv"], dckv, name=f"{tag}_dkvnorm")
    dk1, dk2 = _rope(dkp[:, :half], dkp[:, half:], cos, sin, name=f"{tag}_dkrope", inverse=True)
    dckv_full = jnp.concatenate([dckv_pre, dk1, dk2], axis=-1).astype(MXU_DTYPE)
    dw_dkv = _mm(x_m, dckv_full, ta=True, name=f"{tag}_dwdkv", out_dtypes=(MXU_DTYPE,))
    dx_kv = _mm(dckv_full, w["mla_w_dkv"], tb=True, name=f"{tag}_dxkv")
    grads = {"mla_w_dq": dw_dq, "mla_g_q": dg_q, "mla_w_uq": dw_uq, "mla_w_dkv": dw_dkv, "mla_g_kv": dg_kv,
             "mla_w_uk": dw_uk, "mla_w_uv": dw_uv, "mla_w_o": dw_o}
    return [dx_q, dx_kv], grads


def _cast_input(x):
    cast = lambda rows, bc: ((rows[0],), ())
    return _rowwise(cast, [x], [], [(D_MODEL, MXU_DTYPE)], [], name="x_cast")[0]


def _hooked(value, hook):
    token = None if hook is None else hook(value)
    return () if token is None else (token,)


def _mix_fwd(i, cur, cur_m, wl, hook=None, after=()):
    mixer = i % N_MIXERS
    if mixer == 0:
        y, s_mix = _sc_fwd(cur_m, wl["sc_w_in"], wl["sc_conv_w"], wl["sc_w_out"], f"l{i}_sc", after=after)
    elif mixer == 1:
        y, s_mix = _mla_fwd(cur_m, wl, f"l{i}_mla", after=after)
    else:
        y, s_mix = _cf_fwd(cur_m, wl, f"l{i}_cf", after=after)
    cur, cur_m, xh_mix, rs_mix = _ln_fwd(cur, y, wl["ln_mix_g"], wl["ln_mix_b"], name=f"l{i}_lnmix",
                                         after=_hooked(y, hook))
    return cur, cur_m, (s_mix, xh_mix, rs_mix)


def _ff_fwd(i, cur, cur_m, wl, hook=None, after=()):
    y, s_ff = _mlp_fwd(cur_m, wl["ff_w1"], wl["ff_w2"], f"l{i}_ff", after=after)
    cur, cur_m, xh_ff, rs_ff = _ln_fwd(cur, y, wl["ln_ff_g"], wl["ln_ff_b"], name=f"l{i}_lnff", after=_hooked(y, hook))
    return cur, cur_m, (s_ff, xh_ff, rs_ff)


def _layer_fwd(i, cur, cur_m, wl):
    cur, cur_m, s_mix = _mix_fwd(i, cur, cur_m, wl)
    cur, cur_m, s_ff = _ff_fwd(i, cur, cur_m, wl)
    return cur, cur_m, (*s_mix, *s_ff)


def _loss_and_grad(out, target):
    d = D_MODEL

    def loss_fn(rows, bc):
        err = rows[0] - rows[1]
        return (err * (1.0 / d),), (_colsum(err * err),)

    dy, sq = _rowwise(loss_fn, [out, target], [], [(d, F32)], [(1, d)], name="loss")
    return (0.5 / d) * jnp.sum(sq), dy


def _ff_bwd(i, terms, saved, wl, hook=None, after=()):
    s_ff, xh_ff, rs_ff = saved
    grads = {}
    dz, dz_m, grads["ln_ff_g"], grads["ln_ff_b"], _ = _ln_bwd(terms, xh_ff, rs_ff, wl["ln_ff_g"], name=f"l{i}_dlnff",
                                                              after=after)
    dxs, g_ff = _mlp_bwd(dz_m, s_ff, wl["ff_w1"], wl["ff_w2"], f"l{i}_ff", after=_hooked(dz, hook))
    grads.update(g_ff)
    return [(ALPHA, dz)] + [(1.0, v) for v in dxs], grads


def _mix_bwd(i, terms, saved, wl, hook=None, after=()):
    mixer = i % N_MIXERS
    s_mix, xh_mix, rs_mix = saved
    grads = {}
    dz, dz_m, grads["ln_mix_g"], grads["ln_mix_b"], dz_sum = _ln_bwd(terms, xh_mix, rs_mix, wl["ln_mix_g"],
                                                                     name=f"l{i}_dlnmix", after=after)
    behind = _hooked(dz, hook)
    if mixer == 0:
        dxs, g_mix = _sc_bwd(dz_m, s_mix, wl["sc_w_in"], wl["sc_conv_w"], wl["sc_w_out"], f"l{i}_sc", after=behind)
    elif mixer == 1:
        dxs, g_mix = _mla_bwd(dz_m, s_mix, wl, f"l{i}_mla", after=behind)
    else:
        dxs, g_mix = _cf_bwd(dz_m, dz_sum, s_mix, wl, f"l{i}_cf", after=behind)
    grads.update(g_mix)
    return [(ALPHA, dz)] + [(1.0, v) for v in dxs], grads


def _layer_bwd(i, terms, saved, wl):
    terms, g_ff = _ff_bwd(i, terms, saved[3:], wl)
    terms, g_mix = _mix_bwd(i, terms, saved[:3], wl)
    return terms, {**g_ff, **g_mix}


def _sum_terms(terms, *, name):
    coefs = [c for c, _ in terms]

    def sum_fn(rows, bc):
        acc = None
        for c, v in zip(coefs, rows):
            tv = v if c == 1.0 else c * v
            acc = tv if acc is None else acc + tv
        return (acc,), ()

    return _rowwise(sum_fn, [v for _, v in terms], [], [(D_MODEL, F32)], [], name=name)[0]


def _local_step(x, target, layers):
    cur, cur_m = x, _cast_input(x)
    saved = []
    for i in range(DEPTH):
        cur, cur_m, s = _layer_fwd(i, cur, cur_m, layers[i])
        saved.append(s)
    loss, dy = _loss_and_grad(cur, target)
    terms = [(1.0, dy)]
    grads = [None] * DEPTH
    for i in reversed(range(DEPTH)):
        terms, grads[i] = _layer_bwd(i, terms, saved[i], layers[i])
    return loss, _sum_terms(terms, name="grad_x"), grads


MESH_ID = pl.DeviceIdType.MESH
ANY = pl.BlockSpec(memory_space=pl.ANY)


def _block_id(px, py, pc):
    return 4 * px + 2 * py + pc


COPIES_PER_GATHER = 7


def _all_gather(shards, *, name):
    n = len(shards)

    def body(*refs):
        x_refs, out_refs = refs[:n], refs[n:2 * n]
        send_sems, recv_sems, local_sems = refs[2 * n:]
        x, y, core = lax.axis_index("x"), lax.axis_index("y"), lax.axis_index("c")
        me, sibling = (x, y, core), (x, y, 1 - core)
        chips = [(1 - x, y), (x, 1 - y), (1 - x, 1 - y)]

        def copy(t, k, block, to, from_input=False):
            dst = out_refs[t].at[_block_id(*block)]
            return pltpu.make_async_remote_copy(
                src_ref=x_refs[t] if from_input else dst, dst_ref=dst,
                send_sem=send_sems.at[t * COPIES_PER_GATHER + k], recv_sem=recv_sems.at[t * COPIES_PER_GATHER + k],
                device_id=to, device_id_type=MESH_ID)

        started = []
        mine = [pltpu.make_async_copy(x_refs[t], out_refs[t].at[_block_id(*me)], local_sems.at[t]) for t in range(n)]
        for t in range(n):
            mine[t].start()
            first = [copy(t, 0, me, sibling, from_input=True)]
            first += [copy(t, 1 + j, me, (*chip, core), from_input=True) for j, chip in enumerate(chips)]
            for cp in first:
                cp.start()
            started += first
        for t in range(n):
            for j, chip in enumerate(chips):
                copy(t, 1 + j, (*chip, core), me).wait_recv()
                passed = copy(t, 4 + j, (*chip, core), sibling)
                passed.start()
                started.append(passed)
        for t in range(n):
            copy(t, 0, sibling, me).wait_recv()
            for j, chip in enumerate(chips):
                copy(t, 4 + j, (*chip, 1 - core), me).wait_recv()
        for cp in started:
            cp.wait_send()
        for cp in mine:
            cp.wait()

    n_sems = n * COPIES_PER_GATHER
    outs = pl.pallas_call(
        body,
        name=name,
        out_shape=[jax.ShapeDtypeStruct((N_DEV, *s.shape), s.dtype) for s in shards],
        in_specs=[ANY] * n,
        out_specs=[ANY] * n,
        scratch_shapes=[pltpu.SemaphoreType.DMA((n_sems,)), pltpu.SemaphoreType.DMA((n_sems,)),
                        pltpu.SemaphoreType.DMA((n,))],
    )(*shards)
    return list(outs)


def _swap_with_sibling(parts, *, name):
    n = len(parts)

    def body(*refs):
        p_refs, got_refs = refs[:n], refs[n:2 * n]
        send_sems, recv_sems = refs[2 * n:]
        x, y, core = lax.axis_index("x"), lax.axis_index("y"), lax.axis_index("c")
        chips = [(x, y), (1 - x, y), (x, 1 - y), (1 - x, 1 - y)]
        copies = []
        for t in range(n):
            for k, (px, py) in enumerate(chips):
                copies.append(pltpu.make_async_remote_copy(
                    src_ref=p_refs[t].at[_block_id(px, py, 1 - core)], dst_ref=got_refs[t].at[k],
                    send_sem=send_sems.at[4 * t + k], recv_sem=recv_sems.at[4 * t + k], device_id=(x, y, 1 - core),
                    device_id_type=MESH_ID))
        for cp in copies:
            cp.start()
        for cp in copies:
            cp.wait()

    outs = pl.pallas_call(
        body,
        name=name,
        out_shape=[jax.ShapeDtypeStruct((4, *p.shape[1:]), p.dtype) for p in parts],
        in_specs=[ANY] * n,
        out_specs=[ANY] * n,
        scratch_shapes=[pltpu.SemaphoreType.DMA((4 * n,)), pltpu.SemaphoreType.DMA((4 * n,))],
    )(*parts)
    return list(outs)


def _swap_between_chips(parts, *, name):
    n = len(parts)

    def body(*refs):
        p_refs, got_refs = refs[:n], refs[n:2 * n]
        send_sems, recv_sems = refs[2 * n:]
        x, y, core = lax.axis_index("x"), lax.axis_index("y"), lax.axis_index("c")
        chips = [(1 - x, y), (x, 1 - y), (1 - x, 1 - y)]
        copies = []
        for t in range(n):
            for j, (px, py) in enumerate(chips):
                copies.append(pltpu.make_async_remote_copy(
                    src_ref=p_refs[t].at[j], dst_ref=got_refs[t].at[j], send_sem=send_sems.at[3 * t + j],
                    recv_sem=recv_sems.at[3 * t + j], device_id=(px, py, core), device_id_type=MESH_ID))
        for cp in copies:
            cp.start()
        for cp in copies:
            cp.wait()

    outs = pl.pallas_call(
        body,
        name=name,
        out_shape=[jax.ShapeDtypeStruct(p.shape, p.dtype) for p in parts],
        in_specs=[ANY] * n,
        out_specs=[ANY] * n,
        scratch_shapes=[pltpu.SemaphoreType.DMA((3 * n,)), pltpu.SemaphoreType.DMA((3 * n,))],
    )(*parts)
    return list(outs)


def _chip_partials(parts, got, *, name):
    _, r, c = parts.shape
    tb = _row_tile(r, 512)
    x, y, core = lax.axis_index("x"), lax.axis_index("y"), lax.axis_index("c")
    chips = [(x, y), (1 - x, y), (x, 1 - y), (1 - x, 1 - y)]
    ids = jnp.stack([_block_id(px, py, core) for px, py in chips]).astype(jnp.int32)

    def body(ids_ref, p0, p1, p2, p3, g_ref, own_ref, out_ref):
        own_ref[...] = p0[...].astype(F32) + g_ref[0].astype(F32)
        for j, p in enumerate((p1, p2, p3)):
            out_ref[j] = (p[...].astype(F32) + g_ref[j + 1].astype(F32)).astype(out_ref.dtype)

    def part_spec(k):
        return pl.BlockSpec((None, tb, c), lambda i, ids_ref: (ids_ref[k], i, 0))

    grid_spec = pltpu.PrefetchScalarGridSpec(
        num_scalar_prefetch=1,
        grid=(r // tb,),
        in_specs=[part_spec(k) for k in range(4)] + [pl.BlockSpec((4, tb, c), lambda i, ids_ref: (0, i, 0))],
        out_specs=[pl.BlockSpec((tb, c), lambda i, ids_ref: (i, 0)), pl.BlockSpec((3, tb, c), lambda i, ids_ref: (0, i, 0))],
    )
    return pl.pallas_call(
        body,
        name=name,
        grid_spec=grid_spec,
        out_shape=[jax.ShapeDtypeStruct((r, c), F32), jax.ShapeDtypeStruct((3, r, c), parts.dtype)],
        compiler_params=_params(("parallel",)),
    )(ids, parts, parts, parts, parts, got)


def _chip_partials_group(parts, got, *, name):
    n = len(parts)
    rows = [p.shape[1] for p in parts]
    steps = 4 if all(r % (4 * BF16_ROWS) == 0 for r in rows) else (2 if all(r % (2 * BF16_ROWS) == 0 for r in rows) else 1)
    x, y, core = lax.axis_index("x"), lax.axis_index("y"), lax.axis_index("c")
    chips = [(x, y), (1 - x, y), (x, 1 - y), (1 - x, 1 - y)]
    ids = jnp.stack([_block_id(px, py, core) for px, py in chips]).astype(jnp.int32)

    def body(ids_ref, *refs):
        ins, outs = refs[:5 * n], refs[5 * n:]
        for t in range(n):
            p0, p1, p2, p3, g_ref = ins[5 * t:5 * t + 5]
            own_ref, out_ref = outs[2 * t], outs[2 * t + 1]
            own_ref[...] = p0[...].astype(F32) + g_ref[0].astype(F32)
            for j, p in enumerate((p1, p2, p3)):
                out_ref[j] = (p[...].astype(F32) + g_ref[j + 1].astype(F32)).astype(out_ref.dtype)

    in_specs, out_specs, out_shape, operands = [], [], [], []
    for p, g in zip(parts, got):
        _, r, c = p.shape
        tb = r // steps
        in_specs += [pl.BlockSpec((None, tb, c), lambda i, ids_ref, k=k: (ids_ref[k], i, 0)) for k in range(4)]
        in_specs.append(pl.BlockSpec((4, tb, c), lambda i, ids_ref: (0, i, 0)))
        out_specs += [pl.BlockSpec((tb, c), lambda i, ids_ref: (i, 0)), pl.BlockSpec((3, tb, c), lambda i, ids_ref: (0, i, 0))]
        out_shape += [jax.ShapeDtypeStruct((r, c), F32), jax.ShapeDtypeStruct((3, r, c), p.dtype)]
        operands += [p, p, p, p, g]
    outs = pl.pallas_call(
        body,
        name=name,
        grid_spec=pltpu.PrefetchScalarGridSpec(num_scalar_prefetch=1, grid=(steps,), in_specs=in_specs,
                                               out_specs=out_specs),
        out_shape=out_shape,
        compiler_params=_params(("parallel",)),
    )(ids, *operands)
    return [(outs[2 * t], outs[2 * t + 1]) for t in range(n)]


def _sum_blocks(first, blocks, *, name):
    n, r, c = blocks.shape
    tb = _row_tile(r, 512)

    def body(*refs):
        b_ref, o_ref = refs[-2], refs[-1]
        acc = refs[0][...] if first is not None else b_ref[0].astype(F32)
        for j in range(0 if first is not None else 1, n):
            acc = acc + b_ref[j].astype(F32)
        o_ref[...] = acc

    row_spec = pl.BlockSpec((tb, c), lambda i: (i, 0))
    return pl.pallas_call(
        body,
        name=name,
        grid=(r // tb,),
        in_specs=([row_spec] if first is not None else []) + [pl.BlockSpec((n, tb, c), lambda i: (0, i, 0))],
        out_specs=row_spec,
        out_shape=jax.ShapeDtypeStruct((r, c), F32),
        compiler_params=_params(("parallel",)),
    )(*([first] if first is not None else []), blocks)


def _reduce_scatter(parts, *, tag):
    got = _swap_with_sibling(parts, name=f"{tag}_sibling")
    sums = [_chip_partials(p, g, name=f"{tag}_chipsum{t}") for t, (p, g) in enumerate(zip(parts, got))]
    arrived = _swap_between_chips([others for _, others in sums], name=f"{tag}_chips")
    return [(own, arr) for (own, _), arr in zip(sums, arrived)]


HBM = pl.BlockSpec(memory_space=pltpu.HBM)
SEM = pl.BlockSpec(memory_space=pltpu.SEMAPHORE)
DATAFLOW = pltpu.SideEffectType.DATAFLOW_SIDE_EFFECTING


def _tie(value, *before):
    return lax.optimization_barrier((value, *before))[0]


def _split_call(name, arrays, sems, n_new, body, after=None):
    n_a, n_s = len(arrays), len(sems)
    n_after = 0 if after is None else 1

    def kernel_body(*refs):
        new = refs[n_a + n_s + n_after:n_a + n_s + n_after + 2] if n_new else (None, None)
        body(refs[:n_a], refs[n_a:n_a + n_s], *new)
        refs[-1][...] = jnp.zeros_like(refs[-1])

    n_sem_out = 2 if n_new else 0
    out_shape = [pltpu.SemaphoreType.DMA((n_new,))] * n_sem_out
    out_shape += [pltpu.HBM(a.shape, a.dtype) for a in arrays]
    out_shape += [jax.ShapeDtypeStruct((SUBLANES, LANES), F32)]
    outs = pl.pallas_call(
        kernel_body,
        name=name,
        out_shape=out_shape,
        in_specs=[HBM] * n_a + [SEM] * n_s + [ANY] * n_after,
        out_specs=[SEM] * n_sem_out + [HBM] * n_a + [pl.BlockSpec(memory_space=pltpu.VMEM)],
        input_output_aliases={i: n_sem_out + i for i in range(n_a)},
        compiler_params=pltpu.CompilerParams(has_side_effects=DATAFLOW),
    )(*[pltpu.with_memory_space_constraint(a, pltpu.HBM) for a in arrays], *sems, *([after] if n_after else []))
    return tuple(outs[:n_sem_out]), list(outs[n_sem_out:n_sem_out + n_a]), outs[-1]


def _remote(src, dst, send, recv, k, to):
    return pltpu.make_async_remote_copy(src_ref=src, dst_ref=dst, send_sem=send.at[k], recv_sem=recv.at[k],
                                        device_id=to, device_id_type=MESH_ID)


def _land_block(land_ref, block, by_cols, width):
    if by_cols:
        return land_ref.at[:, pl.ds(pl.multiple_of(block * width, LANES), width)]
    return land_ref.at[block]


def _gather_start(shards, *, name, after=None, by_cols=None):
    n = len(shards)
    by_cols = list(by_cols) if by_cols is not None else [False] * n
    me = _block_id(lax.axis_index("x"), lax.axis_index("y"), lax.axis_index("c"))
    lands = []
    for s, cols in zip(shards, by_cols):
        if cols:
            empty = lax.empty((s.shape[0], N_DEV * s.shape[1]), s.dtype)
            lands.append(lax.dynamic_update_slice_in_dim(empty, s, me * s.shape[1], axis=1))
        else:
            lands.append(lax.dynamic_update_index_in_dim(lax.empty((N_DEV, *s.shape), s.dtype), s, me, 0))
    widths = [s.shape[-1] for s in shards]

    def body(refs, _, send, recv):
        x, y, core = lax.axis_index("x"), lax.axis_index("y"), lax.axis_index("c")
        targets = [(x, y, 1 - core), (1 - x, y, core), (x, 1 - y, core), (1 - x, 1 - y, core)]
        for t in range(n):
            mine = _land_block(refs[n + t], _block_id(x, y, core), by_cols[t], widths[t])
            for k, to in enumerate(targets):
                _remote(refs[t], mine, send, recv, 4 * t + k, to).start()

    sems, thru, token = _split_call(name, shards + lands, [], 4 * n, body, after=after)
    return {"n": n, "sems1": sems, "arrays": thru, "token": token, "by_cols": by_cols, "widths": widths}


def _gather_pass_on(st, *, name, after):
    n = st["n"]

    def body(refs, sems, send, recv):
        send1, recv1 = sems
        x, y, core = lax.axis_index("x"), lax.axis_index("y"), lax.axis_index("c")
        chips = [(1 - x, y), (x, 1 - y), (1 - x, 1 - y)]
        for t in range(n):
            for j, (px, py) in enumerate(chips):
                block = _land_block(refs[n + t], _block_id(px, py, core), st["by_cols"][t], st["widths"][t])
                _remote(refs[t], block, send1, recv1, 4 * t + 1 + j, (x, y, 1 - core)).wait_recv()
                _remote(block, block, send, recv, 3 * t + j, (x, y, 1 - core)).start()

    sems2, thru, token = _split_call(name, st["arrays"], list(st["sems1"]), 3 * n, body, after=after)
    return {**st, "sems2": sems2, "arrays": thru, "token": token}


def _gather_finish(st, *, name, after):
    n = st["n"]

    def body(refs, sems, *_):
        send1, recv1, send2, recv2 = sems
        x, y, core = lax.axis_index("x"), lax.axis_index("y"), lax.axis_index("c")
        sibling = (x, y, 1 - core)
        chips = [(1 - x, y), (x, 1 - y), (1 - x, 1 - y)]
        for t in range(n):
            place = lambda px, py, pc, t=t: _land_block(refs[n + t], _block_id(px, py, pc), st["by_cols"][t],
                                                        st["widths"][t])
            for k in range(4):
                _remote(refs[t], place(x, y, core), send1, recv1, 4 * t + k, sibling).wait_send()
            _remote(refs[t], place(x, y, 1 - core), send1, recv1, 4 * t, sibling).wait_recv()
            for j, (px, py) in enumerate(chips):
                _remote(place(px, py, core), place(px, py, core), send2, recv2, 3 * t + j, sibling).wait_send()
                _remote(refs[t], place(px, py, 1 - core), send2, recv2, 3 * t + j, sibling).wait_recv()

    _, thru, _ = _split_call(name, st["arrays"], [*st["sems1"], *st["sems2"]], 0, body, after=after)
    return thru[n:]


def _swap_start(sources, lands, plan, per_array, *, name):
    n = len(sources)

    def body(refs, _, send, recv):
        x, y, core = lax.axis_index("x"), lax.axis_index("y"), lax.axis_index("c")
        for t in range(n):
            for k in range(per_array):
                block, to = plan(x, y, core, k)
                _remote(refs[t].at[block], refs[n + t].at[k], send, recv, per_array * t + k, to).start()

    sems, thru, token = _split_call(name, sources + lands, [], per_array * n, body)
    return {"n": n, "sems": sems, "arrays": thru, "token": token, "plan": plan, "per_array": per_array}


def _swap_finish(st, *, name, after):
    n, plan, per_array = st["n"], st["plan"], st["per_array"]

    def body(refs, sems, *_):
        send, recv = sems
        x, y, core = lax.axis_index("x"), lax.axis_index("y"), lax.axis_index("c")
        for t in range(n):
            for k in range(per_array):
                block, to = plan(x, y, core, k)
                cp = _remote(refs[t].at[block], refs[n + t].at[k], send, recv, per_array * t + k, to)
                cp.wait_send()
                cp.wait_recv()

    _, thru, _ = _split_call(name, st["arrays"], list(st["sems"]), 0, body, after=after)
    return thru[:n], thru[n:]


def _to_sibling(x, y, core, k):
    px, py = [(x, y), (1 - x, y), (x, 1 - y), (1 - x, 1 - y)][k]
    return _block_id(px, py, 1 - core), (x, y, 1 - core)


def _to_chip(x, y, core, k):
    px, py = [(1 - x, y), (x, 1 - y), (1 - x, 1 - y)][k]
    return k, (px, py, core)


PARAMS = {
    "sc_w_in": ((2, 1024, 3072), 2), "sc_conv_w": ((2, 3, 1024), 2), "sc_w_out": ((2, 1024, 1024), 1),
    "mla_w_dq": ((1, 1024, 384), 1), "mla_g_q": ((1, 384), None), "mla_w_uq": ((1, 384, 1536), 2),
    "mla_w_dkv": ((1, 1024, 320), 1), "mla_g_kv": ((1, 256), None), "mla_w_uk": ((1, 256, 8, 128), 1),
    "mla_w_uv": ((1, 256, 8, 128), 1), "mla_w_o": ((1, 1024, 1024), 1), "cf_w_pw1": ((1, 1024, 2048), 2),
    "cf_b_pw1": ((1, 2048), 1), "cf_dw_w": ((1, 31, 1024), 2), "cf_dw_b": ((1, 1024), 1), "cf_norm_g": ((1, 1024), 1),
    "cf_norm_b": ((1, 1024), 1), "cf_w_pw2": ((1, 1024, 1024), 1), "cf_b_pw2": ((1, 1024), 1),
    "ff_w1": ((4, 1024, 4096), 2), "ff_w2": ((4, 4096, 1024), 1), "ln_mix_g": ((4, 1024), None),
    "ln_mix_b": ((4, 1024), None), "ln_ff_g": ((4, 1024), None), "ln_ff_b": ((4, 1024), None),
}
NAMES = list(PARAMS)
BIG = ["sc_w_in", "sc_w_out", "mla_w_dq", "mla_w_uq", "mla_w_dkv", "mla_w_uk", "mla_w_uv", "mla_w_o", "cf_w_pw1",
       "cf_w_pw2", "ff_w1", "ff_w2"]
SMALL = [n for n in NAMES if n not in BIG]
SMALL_SHARDED = [n for n in SMALL if PARAMS[n][1] is not None]
SMALL_WIDTH = LANES
DEV_BLOCKED = ("sc_w_in", "cf_w_pw1", "ff_w1")
MIXER_BIG = {0: ["sc_w_in", "sc_w_out"], 1: ["mla_w_dq", "mla_w_uq", "mla_w_dkv", "mla_w_uk", "mla_w_uv", "mla_w_o"],
             2: ["cf_w_pw1", "cf_w_pw2"]}


def _layer_big(i):
    return [(n, i // N_MIXERS) for n in MIXER_BIG[i % N_MIXERS]] + [("ff_w1", i), ("ff_w2", i)]


def _shard_2d(name, a):
    return a.reshape(a.shape[0], -1)


def _as_operand(name, g):
    if name in DEV_BLOCKED:
        return g
    if name == "mla_w_uq":
        by_head = g.transpose(1, 0, 2)
        pieces = [by_head[:, :, :QK_NOPE], by_head[:, :, QK_NOPE:QK_NOPE + ROPE_HALF], by_head[:, :, QK_NOPE + ROPE_HALF:]]
        return jnp.concatenate([p.reshape(g.shape[1], -1) for p in pieces], axis=1)
    return g.reshape(-1, g.shape[2])


def _as_parts(name, grad):
    if name in DEV_BLOCKED:
        return grad
    if name == "mla_w_uq":
        rows = grad.shape[0]
        pieces = [grad[:, :Q_NOPE_COLS].reshape(rows, MLA_HEADS, QK_NOPE),
                  grad[:, Q_NOPE_COLS:Q_NOPE_COLS + Q_HALF_COLS].reshape(rows, MLA_HEADS, ROPE_HALF),
                  grad[:, Q_NOPE_COLS + Q_HALF_COLS:].reshape(rows, MLA_HEADS, ROPE_HALF)]
        return jnp.concatenate(pieces, axis=2).transpose(1, 0, 2)
    return grad.reshape(N_DEV, -1, grad.shape[1])


def _shard_shape(name):
    shape, ax = PARAMS[name]
    if ax is None:
        return shape
    return tuple(s // N_DEV if i == ax else s for i, s in enumerate(shape))


def _rows_of(shape, width, align):
    n = 1
    for s in shape:
        n *= s
    rows = -(-n // width)
    return -(-rows // align) * align


def _pack_rows(arrays, width, align, dtype, lead=0):
    segs = []
    for a in arrays:
        batch = a.shape[:lead]
        flat = a.astype(dtype).reshape(*batch, -1)
        rows = _rows_of(a.shape[lead:], width, align)
        flat = jnp.pad(flat, [(0, 0)] * lead + [(0, rows * width - flat.shape[-1])])
        segs.append(flat.reshape(*batch, rows, width))
    return jnp.concatenate(segs, axis=lead)


def _unpack_rows(buf, shapes, width, align):
    out, off = [], 0
    lead = buf.shape[:-2]
    for shape in shapes:
        n = 1
        for s in shape:
            n *= s
        rows = _rows_of(shape, width, align)
        seg = buf[..., off:off + rows, :].reshape(*lead, rows * width)[..., :n]
        out.append(seg.reshape(*lead, *shape))
        off += rows
    return out


def _merge_shards(stacked, ax):
    moved = jnp.moveaxis(stacked, 0, ax)
    shape = moved.shape
    return moved.reshape(*shape[:ax], shape[ax] * shape[ax + 1], *shape[ax + 2:])


def _split_shards(full, ax):
    shape = full.shape
    split = full.reshape(*shape[:ax], N_DEV, shape[ax] // N_DEV, *shape[ax + 1:])
    return jnp.moveaxis(split, ax, 0)


def _adam_math(wv, gv, mv, vv):
    m_new = ADAM_B1 * mv + (1.0 - ADAM_B1) * gv
    v_new = ADAM_B2 * vv + (1.0 - ADAM_B2) * (gv * gv)
    m_hat = m_new / (1.0 - ADAM_B1 ** ADAM_STEP)
    v_hat = v_new / (1.0 - ADAM_B2 ** ADAM_STEP)
    delta = -ADAM_LR * (m_hat / (jnp.sqrt(v_hat) + ADAM_EPS) + ADAM_WD * wv)
    return delta, m_new, v_new


def _adamw(w, g, m, v, *, name):
    shape = w.shape
    c = shape[-1]

    def fn(rows, bc):
        return _adam_math(*rows), ()

    flat = [a.reshape(-1, c) for a in (w, g, m, v)]
    outs = _rowwise(fn, flat, [], [(c, F32)] * 3, [], name=name, tb_cap=512)
    return tuple(o.reshape(shape) for o in outs)


def _adamw_reduced(w, m, v, layer_grads, *, name, after=()):
    n_l, r, c = w.shape
    tb = _row_tile(r, 256 if c > 512 else 512)
    nb = r // tb

    def body(*refs):
        w_ref, m_ref, v_ref = refs[:3]
        g_refs = refs[3:3 + 2 * n_l]
        g_out, d_out, m_out, v_out = refs[3 + 2 * n_l + len(after):]
        layer = pl.program_id(0)
        for l in range(n_l):
            @pl.when(layer == l)
            def _(l=l):
                own, arr = g_refs[2 * l], g_refs[2 * l + 1]
                g = own[...] + arr[0].astype(F32)
                g = g + arr[1].astype(F32)
                g = g + arr[2].astype(F32)
                delta, m_new, v_new = _adam_math(w_ref[...], g, m_ref[...], v_ref[...])
                g_out[...] = g
                d_out[...] = delta
                m_out[...] = m_new
                v_out[...] = v_new

    stacked = pl.BlockSpec((tb, c), lambda l, i: (l * nb + i, 0))
    in_specs = [stacked] * 3
    operands = [a.reshape(n_l * r, c) for a in (w, m, v)]
    for l, (own, arr) in enumerate(layer_grads):
        in_specs.append(pl.BlockSpec((tb, c), lambda ll, i, l=l: (jnp.where(ll == l, i, 0), 0)))
        in_specs.append(pl.BlockSpec((3, tb, c), lambda ll, i, l=l: (0, jnp.where(ll == l, i, 0), 0)))
        operands += [own, arr]
    in_specs += [ANY] * len(after)
    operands += list(after)
    outs = pl.pallas_call(
        body,
        name=name,
        grid=(n_l, nb),
        in_specs=in_specs,
        out_specs=[stacked] * 4,
        out_shape=[jax.ShapeDtypeStruct((n_l * r, c), F32)] * 4,
        compiler_params=_params(("arbitrary", "arbitrary")),
    )(*operands)
    return tuple(o.reshape(n_l, r, c) for o in outs)


def _train_step(x, target, weights, m_state, v_state):
    me = _block_id(lax.axis_index("x"), lax.axis_index("y"), lax.axis_index("c"))

    small_shapes = [_shard_shape(n) for n in SMALL_SHARDED]
    gather = {0: _gather_start([_pack_rows([weights[n] for n in SMALL_SHARDED], SMALL_WIDTH, SUBLANES, F32)]
                               + [_shard_2d(n, weights[n][0]).astype(MXU_DTYPE) for n in MIXER_BIG[0]],
                               name="gather_g0_start", by_cols=[False] + [n in DEV_BLOCKED for n in MIXER_BIG[0]])}
    cur, cur_m = x[0], _cast_input(x[0])
    gather[0] = _gather_pass_on(gather[0], name="gather_g0_pass", after=cur_m)
    first = _gather_finish(gather[0], name="gather_g0_finish", after=gather[0]["token"])
    small_all, first_matrices = first[0], first[1:]
    small_full = {n: weights[n] for n in SMALL if PARAMS[n][1] is None}
    for n, stacked in zip(SMALL_SHARDED, _unpack_rows(small_all, small_shapes, SMALL_WIDTH, SUBLANES)):
        small_full[n] = _merge_shards(stacked, PARAMS[n][1])

    def small_weights(i):
        wl = {}
        for n in SMALL:
            if n.startswith("ln_"):
                wl[n] = small_full[n][i:i + 1]
            elif n.startswith(("sc_", "mla_", "cf_")[i % N_MIXERS]):
                v = small_full[n][i // N_MIXERS]
                wl[n] = v if v.ndim == 2 else v.reshape(1, -1)
        return wl

    layers = [small_weights(i) for i in range(DEPTH)]
    groups = [(kind, i) for i in range(DEPTH) for kind in ("mix", "ff")]

    def group_names(g):
        kind, i = g
        if kind == "mix":
            return [(n, i // N_MIXERS) for n in MIXER_BIG[i % N_MIXERS]]
        return [("ff_w1", i), ("ff_w2", i)]

    def shards_of(g):
        return [_shard_2d(n, weights[n][j]).astype(MXU_DTYPE) for n, j in group_names(g)]

    def install(g, gathered):
        for (n, _), a in zip(group_names(g), gathered):
            layers[g[1]][n] = _as_operand(n, a)

    install(groups[0], first_matrices)
    whole = lambda g: [n in DEV_BLOCKED for n, _ in group_names(g)]
    gather[1] = _gather_start(shards_of(groups[1]), name="gather_g1_start", after=gather[0]["token"],
                              by_cols=whole(groups[1]))
    saved = {}
    for k, g in enumerate(groups):
        kind, i = g
        tokens = [gather[1]["token"]] if k == 0 else []
        if k + 2 < len(groups):
            gather[k + 2] = _gather_start(shards_of(groups[k + 2]), name=f"gather_g{k + 2}_start",
                                          after=gather[k + 1]["token"], by_cols=whole(groups[k + 2]))
            tokens.append(gather[k + 2]["token"])

        def pass_on(y, k=k):
            if k + 1 == len(groups):
                return None
            gather[k + 1] = _gather_pass_on(gather[k + 1], name=f"gather_g{k + 1}_pass", after=y)
            return gather[k + 1]["token"]

        phase = _mix_fwd if kind == "mix" else _ff_fwd
        cur, cur_m, saved[g] = phase(i, cur, cur_m, layers[i], hook=pass_on, after=tokens)
        if k + 1 < len(groups):
            install(groups[k + 1], _gather_finish(gather[k + 1], name=f"gather_g{k + 1}_finish", after=cur))

    loss_local, dy = _loss_and_grad(cur, target[0])
    loss = lax.psum(loss_local, MESH_AXES)

    reduced = {n: [None] * PARAMS[n][0][0] for n in BIG}
    small_grads = {n: [None] * PARAMS[n][0][0] for n in SMALL}
    reduce = {}

    def to_chips(p, after):
        st = reduce[p]
        parts, got = _swap_finish(st["sibling"], name=f"reduce_p{p}_sibling_finish", after=after)
        sums = _chip_partials_group(parts, got, name=f"reduce_p{p}_chipsum")
        st["own"] = [own for own, _ in sums]
        others = [o for _, o in sums]
        st["chips"] = _swap_start(others, [lax.empty(o.shape, o.dtype) for o in others], _to_chip, 3,
                                  name=f"reduce_p{p}_chips_start")
        return st["chips"]["token"]

    def arrived(p, after):
        st = reduce[p]
        _, got = _swap_finish(st["chips"], name=f"reduce_p{p}_chips_finish", after=after)
        for (n, j), own, arr in zip(st["names"], st["own"], got):
            reduced[n][j] = (own, arr)

    terms = [(1.0, dy)]
    back = list(reversed(groups))
    for p, g in enumerate(back):
        kind, i = g

        def after_first_kernel(dz, p=p):
            return to_chips(p - 1, dz) if p >= 1 else None

        phase = _mix_bwd if kind == "mix" else _ff_bwd
        tokens = [reduce[p - 1]["sibling"]["token"]] if p >= 1 else []
        terms, grads = phase(i, terms, saved[g], layers[i], hook=after_first_kernel, after=tokens)
        for n in SMALL:
            if n in grads:
                small_grads[n][i if n.startswith("ln_") else i // N_MIXERS] = grads[n]
        if p >= 2:
            arrived(p - 2, terms[-1][1])
        names = group_names(g)
        parts = [_as_parts(n, grads[n]) for n, _ in names]
        reduce[p] = {"names": names, "sibling": _swap_start(
            parts, [lax.empty((4, *a.shape[1:]), a.dtype) for a in parts], _to_sibling, 4,
            name=f"reduce_p{p}_sibling_start")}
    grad_x = _sum_terms(terms, name="grad_x")
    last = len(back) - 1
    small_mine = _pack_rows([jnp.stack(small_grads[n]).reshape(PARAMS[n][0]) for n in SMALL], SMALL_WIDTH, SUBLANES, F32)
    small_gather = _gather_start([small_mine], name="gather_small_grads_start")
    tail = to_chips(last, small_gather["token"])
    arrived(last - 1, tail)

    grad, delta, new_m, new_v = {}, {}, {}, {}

    def adamw_matrix(n, after):
        shape = weights[n].shape
        view = lambda a: a.reshape(shape[0], shape[1], -1)
        outs = _adamw_reduced(view(weights[n]), view(m_state[n]), view(v_state[n]), reduced[n], name=f"adamw_{n}",
                              after=after)
        grad[n], delta[n], new_m[n], new_v[n] = [o.reshape(shape) for o in outs]
        return outs[0]

    done = tail
    for n in BIG:
        if n not in MIXER_BIG[0]:
            done = adamw_matrix(n, [done])
    small_gather = _gather_pass_on(small_gather, name="gather_small_grads_pass", after=done)
    small_all = _gather_finish(small_gather, name="gather_small_grads_finish", after=small_gather["token"])[0]
    small_sum = _sum_blocks(None, small_all, name="sum_small_grads")
    for n, g in zip(SMALL, _unpack_rows(small_sum, [PARAMS[n][0] for n in SMALL], SMALL_WIDTH, SUBLANES)):
        ax = PARAMS[n][1]
        grad[n] = g if ax is None else lax.dynamic_index_in_dim(_split_shards(g, ax), me, axis=0, keepdims=False)
    small_shards = [_shard_shape(n) for n in SMALL]
    packed = [_pack_rows([src[n] for n in SMALL], SMALL_WIDTH, SUBLANES, F32) for src in (weights, grad, m_state, v_state)]
    small_out = _adamw(*packed, name="adamw_small")
    for outs, dst in zip(small_out, (delta, new_m, new_v)):
        for n, a in zip(SMALL, _unpack_rows(outs, small_shards, SMALL_WIDTH, SUBLANES)):
            dst[n] = a
    arrived(last, small_sum)
    for n in MIXER_BIG[0]:
        adamw_matrix(n, [])
    return (loss, grad_x[None], *[grad[n] for n in NAMES], *[delta[n] for n in NAMES],
            *[new_m[n] for n in NAMES], *[new_v[n] for n in NAMES])


def kernel(x, sc_w_in, sc_conv_w, sc_w_out, mla_w_dq, mla_g_q, mla_w_uq, mla_w_dkv, mla_g_kv, mla_w_uk, mla_w_uv, mla_w_o, cf_w_pw1, cf_b_pw1, cf_dw_w, cf_dw_b, cf_norm_g, cf_norm_b, cf_w_pw2, cf_b_pw2, ff_w1, ff_w2, ln_mix_g, ln_mix_b, ln_ff_g, ln_ff_b, loss_target, m_sc_w_in, m_sc_conv_w, m_sc_w_out, m_mla_w_dq, m_mla_g_q, m_mla_w_uq, m_mla_w_dkv, m_mla_g_kv, m_mla_w_uk, m_mla_w_uv, m_mla_w_o, m_cf_w_pw1, m_cf_b_pw1, m_cf_dw_w, m_cf_dw_b, m_cf_norm_g, m_cf_norm_b, m_cf_w_pw2, m_cf_b_pw2, m_ff_w1, m_ff_w2, m_ln_mix_g, m_ln_mix_b, m_ln_ff_g, m_ln_ff_b, v_sc_w_in, v_sc_conv_w, v_sc_w_out, v_mla_w_dq, v_mla_g_q, v_mla_w_uq, v_mla_w_dkv, v_mla_g_kv, v_mla_w_uk, v_mla_w_uv, v_mla_w_o, v_cf_w_pw1, v_cf_b_pw1, v_cf_dw_w, v_cf_dw_b, v_cf_norm_g, v_cf_norm_b, v_cf_w_pw2, v_cf_b_pw2, v_ff_w1, v_ff_w2, v_ln_mix_g, v_ln_mix_b, v_ln_ff_g, v_ln_ff_b):
    w_list = (sc_w_in, sc_conv_w, sc_w_out, mla_w_dq, mla_g_q, mla_w_uq, mla_w_dkv, mla_g_kv, mla_w_uk, mla_w_uv, mla_w_o,
              cf_w_pw1, cf_b_pw1, cf_dw_w, cf_dw_b, cf_norm_g, cf_norm_b, cf_w_pw2, cf_b_pw2, ff_w1, ff_w2, ln_mix_g,
              ln_mix_b, ln_ff_g, ln_ff_b)
    m_list = (m_sc_w_in, m_sc_conv_w, m_sc_w_out, m_mla_w_dq, m_mla_g_q, m_mla_w_uq, m_mla_w_dkv, m_mla_g_kv, m_mla_w_uk,
              m_mla_w_uv, m_mla_w_o, m_cf_w_pw1, m_cf_b_pw1, m_cf_dw_w, m_cf_dw_b, m_cf_norm_g, m_cf_norm_b, m_cf_w_pw2,
              m_cf_b_pw2, m_ff_w1, m_ff_w2, m_ln_mix_g, m_ln_mix_b, m_ln_ff_g, m_ln_ff_b)
    v_list = (v_sc_w_in, v_sc_conv_w, v_sc_w_out, v_mla_w_dq, v_mla_g_q, v_mla_w_uq, v_mla_w_dkv, v_mla_g_kv, v_mla_w_uk,
              v_mla_w_uv, v_mla_w_o, v_cf_w_pw1, v_cf_b_pw1, v_cf_dw_w, v_cf_dw_b, v_cf_norm_g, v_cf_norm_b, v_cf_w_pw2,
              v_cf_b_pw2, v_ff_w1, v_ff_w2, v_ln_mix_g, v_ln_mix_b, v_ln_ff_g, v_ln_ff_b)
    return _train_step(x, loss_target, dict(zip(NAMES, w_list)), dict(zip(NAMES, m_list)), dict(zip(NAMES, v_list)))
```

```python
import functools

import jax
import jax.numpy as jnp
from jax import lax
from jax.experimental import pallas as pl
from jax.experimental.pallas import tpu as pltpu

F32 = jnp.float32
MXU_DTYPE = jnp.bfloat16

N_DEV = 8
D_MODEL = 1024
DEPTH = 4
N_MIXERS = 3
CHUNK = 64
ALPHA = (2.0 * DEPTH) ** 0.25
LN_EPS = 1e-5
RMS_EPS = 1e-6
MLA_HEADS = 8
QK_NOPE = 128
QK_ROPE = 64
V_HEAD = 128
Q_LORA = 384
KV_LORA = 256
ROPE_THETA = 10000.0
ADAM_LR = 0.001
ADAM_B1 = 0.9
ADAM_B2 = 0.999
ADAM_EPS = 1e-08
ADAM_WD = 0.01
ADAM_STEP = 10

LANES = 128
SUBLANES = 8
BF16_ROWS = 16
VMEM_LIMIT = 56 * 1024 * 1024
MM_VMEM_BUDGET = 36 * 1024 * 1024
CONV_PAD = 32
MASK_VALUE = -1e30

MESH_AXES = ("x", "y", "c")


def _params(semantics):
    return pltpu.CompilerParams(dimension_semantics=semantics, vmem_limit_bytes=VMEM_LIMIT)


def _tile(n, cap):
    if n <= cap:
        return n
    t = cap - cap % LANES
    while t >= LANES:
        if n % t == 0:
            return t
        t -= LANES
    raise ValueError(f"no tile for {n} under {cap}")


def _row_tile(rows, cap):
    if rows <= cap:
        return rows
    t = cap - cap % SUBLANES
    while t >= SUBLANES:
        if rows % t == 0:
            return t
        t -= SUBLANES
    raise ValueError(f"no row tile for {rows} under {cap}")


def _mm(a, b, *, name, ta=False, tb=False, b_dev=False, out_dev=False, out_dtypes=(F32,), epilogue=None,
        tile_extras=(), col_extras=(), after=(), tm_cap=2048, tn_cap=1024, tk_cap=4096):
    m, k = (a.shape[1], a.shape[0]) if ta else a.shape
    if b_dev:
        nd, b_rows, ns = b.shape
        n, kb = (b_rows, nd * ns) if tb else (nd * ns, b_rows)
    else:
        n, kb = (b.shape[0], b.shape[1]) if tb else (b.shape[1], b.shape[0])
    assert k == kb, (a.shape, b.shape, ta, tb)
    if ta or tile_extras or (b_dev and tb):
        tm_cap = min(tm_cap, 1024)
    tk = ns if (b_dev and tb) else _tile(k, tk_cap)
    nk = k // tk
    fixed_tn = ns if (b_dev and not tb) else (n // N_DEV if out_dev else None)
    if out_dev:
        assert not tile_extras and not col_extras and n % N_DEV == 0
    out_bytes = sum(jnp.dtype(dt).itemsize for dt in out_dtypes) + sum(e.dtype.itemsize for e in tile_extras)
    tm, tn = None, None
    for cand_m, cand_n in ((tm_cap, tn_cap), (tm_cap, tn_cap // 2), (tm_cap // 2, tn_cap), (tm_cap // 2, tn_cap // 2),
                           (tm_cap // 4, tn_cap // 2), (tm_cap // 4, tn_cap // 4)):
        tm, tn = _tile(m, cand_m), (fixed_tn if fixed_tn is not None else _tile(n, cand_n))
        need = 2 * (tm * tk * a.dtype.itemsize + tk * tn * b.dtype.itemsize) + 2 * tm * tn * out_bytes
        need += tm * tn * 4 * (2 if nk > 1 else 1)
        if need <= MM_VMEM_BUDGET:
            break
    n_te, n_ce, n_out = len(tile_extras), len(col_extras), len(out_dtypes)
    dims = (((0 if ta else 1,), (1 if tb else 0,)), ((), ()))

    def body(*refs):
        a_ref, b_ref = refs[0], refs[1]
        te_refs = refs[2:2 + n_te]
        ce_refs = refs[2 + n_te:2 + n_te + n_ce]
        first_out = 2 + n_te + n_ce + len(after)
        out_refs = refs[first_out:first_out + n_out]

        def finish(acc):
            if epilogue is None:
                outs = (acc,)
            else:
                outs = epilogue(acc, *[r[...] for r in te_refs], *[r[...] for r in ce_refs])
            for o_ref, o in zip(out_refs, outs):
                o_ref[...] = o.astype(o_ref.dtype)

        def product():
            return lax.dot_general(a_ref[...].astype(MXU_DTYPE), b_ref[...].astype(MXU_DTYPE), dims,
                                   preferred_element_type=F32)

        if nk == 1:
            finish(product())
        else:
            acc_ref = refs[-1]
            kk = pl.program_id(2)

            @pl.when(kk == 0)
            def _():
                acc_ref[...] = jnp.zeros_like(acc_ref)

            acc_ref[...] += product()

            @pl.when(kk == nk - 1)
            def _():
                finish(acc_ref[...])

    a_spec = pl.BlockSpec((tk, tm), lambda i, j, kk: (kk, i)) if ta else pl.BlockSpec((tm, tk), lambda i, j, kk: (i, kk))
    if b_dev and tb:
        b_spec = pl.BlockSpec((None, tn, tk), lambda i, j, kk: (kk, j, 0))
    elif b_dev:
        b_spec = pl.BlockSpec((None, tk, tn), lambda i, j, kk: (j, kk, 0))
    elif tb:
        b_spec = pl.BlockSpec((tn, tk), lambda i, j, kk: (j, kk))
    else:
        b_spec = pl.BlockSpec((tk, tn), lambda i, j, kk: (kk, j))
    tile_spec = pl.BlockSpec((tm, tn), lambda i, j, kk: (i, j))
    col_spec = pl.BlockSpec((1, tn), lambda i, j, kk: (0, j))
    if out_dev:
        out_spec = pl.BlockSpec((None, tm, tn), lambda i, j, kk: (j, i, 0))
        out_shape = [jax.ShapeDtypeStruct((N_DEV, m, tn), dt) for dt in out_dtypes]
    else:
        out_spec = tile_spec
        out_shape = [jax.ShapeDtypeStruct((m, n), dt) for dt in out_dtypes]
    outs = pl.pallas_call(
        body,
        name=name,
        grid=(m // tm, n // tn, nk),
        in_specs=[a_spec, b_spec] + [tile_spec] * n_te + [col_spec] * n_ce + [ANY] * len(after),
        out_specs=[out_spec] * n_out,
        out_shape=out_shape,
        scratch_shapes=[pltpu.VMEM((tm, tn), F32)] if nk > 1 else [],
        compiler_params=_params(("parallel", "parallel", "arbitrary")),
    )(a, b, *tile_extras, *col_extras, *after)
    return outs[0] if n_out == 1 else tuple(outs)


def _rowwise(fn, rows, bcast, out_rows, out_accs, *, name, tb_cap=256, after=()):
    t = rows[0].shape[0]
    tb = _row_tile(t, tb_cap)
    n_r, n_b, n_o, n_a = len(rows), len(bcast), len(out_rows), len(out_accs)

    def body(*refs):
        r_refs = refs[:n_r]
        b_refs = refs[n_r:n_r + n_b]
        first_out = n_r + n_b + len(after)
        o_refs = refs[first_out:first_out + n_o]
        a_refs = refs[first_out + n_o:]
        outs, accs = fn([r[...] for r in r_refs], [r[...] for r in b_refs])
        for o_ref, o in zip(o_refs, outs):
            o_ref[...] = o.astype(o_ref.dtype)
        if n_a:
            i = pl.program_id(0)

            @pl.when(i == 0)
            def _():
                for a_ref, acc in zip(a_refs, accs):
                    a_ref[...] = acc

            @pl.when(i > 0)
            def _():
                for a_ref, acc in zip(a_refs, accs):
                    a_ref[...] += acc

    in_specs = [pl.BlockSpec((tb, r.shape[1]), lambda i: (i, 0)) for r in rows]
    in_specs += [pl.BlockSpec(b.shape, lambda i: (0, 0)) for b in bcast]
    in_specs += [ANY] * len(after)
    out_specs = [pl.BlockSpec((tb, c), lambda i: (i, 0)) for c, _ in out_rows]
    out_specs += [pl.BlockSpec(s, lambda i: (0, 0)) for s in out_accs]
    out_shape = [jax.ShapeDtypeStruct((t, c), dt) for c, dt in out_rows]
    out_shape += [jax.ShapeDtypeStruct(s, F32) for s in out_accs]
    outs = pl.pallas_call(
        body,
        name=name,
        grid=(t // tb,),
        in_specs=in_specs,
        out_specs=out_specs,
        out_shape=out_shape,
        compiler_params=_params(("arbitrary",)),
    )(*rows, *bcast, *after)
    return tuple(outs)


def _colsum(v):
    return jnp.sum(v, axis=0, keepdims=True)


def _sigmoid(v):
    return 1.0 / (1.0 + jnp.exp(-v))


def _ln_fwd(x_res, y_sub, g, b, *, name, after=()):
    d = x_res.shape[1]

    def fn(rows, bc):
        z = ALPHA * rows[0] + rows[1]
        mu = jnp.mean(z, axis=-1, keepdims=True)
        zc = z - mu
        var = jnp.mean(zc * zc, axis=-1, keepdims=True)
        rstd = lax.rsqrt(var + LN_EPS)
        xhat = zc * rstd
        out = xhat * bc[0] + bc[1]
        return (out, out, xhat, rstd), ()

    return _rowwise(fn, [x_res, y_sub], [g, b], [(d, F32), (d, MXU_DTYPE), (d, F32), (1, F32)], [], name=name,
                    after=after, tb_cap=512)


def _ln_bwd(terms, xhat, rstd, g, *, name, after=()):
    d = xhat.shape[1]
    coefs = [c for c, _ in terms]

    def fn(rows, bc):
        xh, rs = rows[0], rows[1]
        dout = None
        for c, v in zip(coefs, rows[2:]):
            tv = v if c == 1.0 else c * v
            dout = tv if dout is None else dout + tv
        dxh = dout * bc[0]
        m1 = jnp.mean(dxh, axis=-1, keepdims=True)
        m2 = jnp.mean(dxh * xh, axis=-1, keepdims=True)
        dz = rs * (dxh - m1 - xh * m2)
        return (dz, dz), (_colsum(dout * xh), _colsum(dout), _colsum(dz))

    return _rowwise(fn, [xhat, rstd] + [v for _, v in terms], [g], [(d, F32), (d, MXU_DTYPE)],
                    [(1, d), (1, d), (1, d)], name=name, after=after, tb_cap=512)


def _rms_fwd(x, g, *, name):
    c = x.shape[1]

    def fn(rows, bc):
        v = rows[0]
        r = lax.rsqrt(jnp.mean(v * v, axis=-1, keepdims=True) + RMS_EPS)
        return (v * r * bc[0],), ()

    return _rowwise(fn, [x], [g], [(c, MXU_DTYPE)], [], name=name)[0]


def _rms_bwd(x, g, dy, *, name):
    c = x.shape[1]

    def fn(rows, bc):
        v, dyv = rows
        r = lax.rsqrt(jnp.mean(v * v, axis=-1, keepdims=True) + RMS_EPS)
        xh = v * r
        dxh = dyv * bc[0]
        dx = r * (dxh - xh * jnp.mean(dxh * xh, axis=-1, keepdims=True))
        return (dx,), (_colsum(dyv * xh),)

    return _rowwise(fn, [x, dy], [g], [(c, F32)], [(1, c)], name=name)


def _conv_chunk(t):
    return _row_tile(t, 256)


def _conv_fwd(x, w, bias, *, name):
    t, c = x.shape
    kw = w.shape[0]
    ch = _conv_chunk(t)

    def body(x_ref, w_ref, b_ref, o_ref, pad_ref):
        pad_ref[0:CONV_PAD, :] = jnp.zeros((CONV_PAD, LANES), F32)
        pad_ref[CONV_PAD:CONV_PAD + t, :] = x_ref[...]
        for t0 in range(0, t, ch):
            acc = jnp.broadcast_to(b_ref[...], (ch, LANES))
            for k in range(kw):
                s = kw - 1 - k
                acc = acc + w_ref[k:k + 1, :] * pad_ref[CONV_PAD - s + t0:CONV_PAD - s + t0 + ch, :]
            o_ref[t0:t0 + ch, :] = acc

    return pl.pallas_call(
        body,
        name=name,
        grid=(c // LANES,),
        in_specs=[pl.BlockSpec((t, LANES), lambda j: (0, j)), pl.BlockSpec((kw, LANES), lambda j: (0, j)),
                  pl.BlockSpec((1, LANES), lambda j: (0, j))],
        out_specs=pl.BlockSpec((t, LANES), lambda j: (0, j)),
        out_shape=jax.ShapeDtypeStruct((t, c), F32),
        scratch_shapes=[pltpu.VMEM((CONV_PAD + t, LANES), F32)],
        compiler_params=_params(("parallel",)),
    )(x, w, bias)


def _conv_bwd(dy, x, w, *, name):
    t, c = x.shape
    kw = w.shape[0]
    ch = _conv_chunk(t)

    def body(dy_ref, x_ref, w_ref, dx_ref, dw_ref, xpad_ref, dpad_ref):
        xpad_ref[0:CONV_PAD, :] = jnp.zeros((CONV_PAD, LANES), F32)
        xpad_ref[CONV_PAD:CONV_PAD + t, :] = x_ref[...]
        dpad_ref[0:t, :] = dy_ref[...]
        dpad_ref[t:t + CONV_PAD, :] = jnp.zeros((CONV_PAD, LANES), F32)
        for t0 in range(0, t, ch):
            acc = jnp.zeros((ch, LANES), F32)
            for k in range(kw):
                s = kw - 1 - k
                acc = acc + w_ref[k:k + 1, :] * dpad_ref[t0 + s:t0 + s + ch, :]
            dx_ref[t0:t0 + ch, :] = acc
        for k in range(kw):
            s = kw - 1 - k
            acc = jnp.zeros((ch, LANES), F32)
            for t0 in range(0, t, ch):
                acc = acc + dy_ref[t0:t0 + ch, :] * xpad_ref[CONV_PAD - s + t0:CONV_PAD - s + t0 + ch, :]
            dw_ref[k:k + 1, :] = _colsum(acc)

    blk = pl.BlockSpec((t, LANES), lambda j: (0, j))
    wblk = pl.BlockSpec((kw, LANES), lambda j: (0, j))
    return pl.pallas_call(
        body,
        name=name,
        grid=(c // LANES,),
        in_specs=[blk, blk, wblk],
        out_specs=[blk, wblk],
        out_shape=[jax.ShapeDtypeStruct((t, c), F32), jax.ShapeDtypeStruct((kw, c), F32)],
        scratch_shapes=[pltpu.VMEM((CONV_PAD + t, LANES), F32), pltpu.VMEM((CONV_PAD + t, LANES), F32)],
        compiler_params=_params(("parallel",)),
    )(dy, x, w)


ATTN_SCALE = (QK_NOPE + QK_ROPE) ** -0.5
_NT = (((1,), (1,)), ((), ()))
_TN = (((0,), (0,)), ((), ()))
_NN = (((1,), (0,)), ((), ()))


def _dot(a, b, dims):
    return lax.dot_general(a, b, dims, preferred_element_type=F32)


def _block_scores(qn, qp, kn_ref, kp_ref, lo, hi, diagonal):
    s = _dot(qn, kn_ref[lo:hi, :], _NT) + _dot(qp, kp_ref[lo:hi, :], _NT)
    s = s * ATTN_SCALE
    if diagonal:
        shift = CHUNK.bit_length() - 1
        rows = lax.broadcasted_iota(jnp.int32, s.shape, 0)
        cols = lax.broadcasted_iota(jnp.int32, s.shape, 1)
        s = jnp.where(jnp.right_shift(cols, shift) <= jnp.right_shift(rows, shift), s, MASK_VALUE)
    return s


def _attn_fwd(qn, qp, kn, kp, v, *, name):
    h, t, _ = qp.shape
    tq = _row_tile(t, 256)
    assert tq % CHUNK == 0

    def body(qn_ref, qp_ref, kn_ref, kp_ref, v_ref, o_ref, of_ref, lse_ref):
        for r0 in range(0, t, tq):
            r1 = r0 + tq
            qn, qp = qn_ref[r0:r1, :].astype(MXU_DTYPE), qp_ref[r0:r1, :]
            s_d = _block_scores(qn, qp, kn_ref, kp_ref, r0, r1, True)
            m = jnp.max(s_d, axis=-1, keepdims=True)
            if r0:
                s_b = _block_scores(qn, qp, kn_ref, kp_ref, 0, r0, False)
                m = jnp.maximum(m, jnp.max(s_b, axis=-1, keepdims=True))
            p_d = jnp.exp(s_d - m)
            l = jnp.sum(p_d, axis=-1, keepdims=True)
            acc = _dot(p_d.astype(MXU_DTYPE), v_ref[r0:r1, :], _NN)
            if r0:
                p_b = jnp.exp(s_b - m)
                l = l + jnp.sum(p_b, axis=-1, keepdims=True)
                acc = acc + _dot(p_b.astype(MXU_DTYPE), v_ref[0:r0, :], _NN)
            o = acc / l
            o_ref[r0:r1, :] = o.astype(o_ref.dtype)
            of_ref[r0:r1, :] = o
            lse_ref[r0:r1, :] = m + jnp.log(l)

    blk = lambda w: pl.BlockSpec((None, t, w), lambda hh: (hh, 0, 0))
    cols = pl.BlockSpec((t, QK_NOPE), lambda hh: (0, hh))
    return pl.pallas_call(
        body,
        name=name,
        grid=(h,),
        in_specs=[cols, blk(QK_ROPE), cols, pl.BlockSpec((t, QK_ROPE), lambda hh: (0, 0)), cols],
        out_specs=[cols, cols, blk(1)],
        out_shape=[jax.ShapeDtypeStruct((t, h * V_HEAD), MXU_DTYPE), jax.ShapeDtypeStruct((t, h * V_HEAD), F32),
                   jax.ShapeDtypeStruct((h, t, 1), F32)],
        compiler_params=_params(("parallel",)),
    )(qn, qp, kn, kp, v)


def _attn_bwd(qn, qp, kn, kp, v, do, o, lse, *, name):
    h, t, _ = qp.shape
    tq = _row_tile(t, 256)

    def body(qn_ref, qp_ref, kn_ref, kp_ref, v_ref, do_ref, o_ref, lse_ref, dqn_ref, dqp_ref, dkn_ref, dkp_ref,
             dv_ref):
        dkn_ref[...] = jnp.zeros_like(dkn_ref)
        dv_ref[...] = jnp.zeros_like(dv_ref)

        @pl.when(pl.program_id(0) == 0)
        def _():
            dkp_ref[...] = jnp.zeros_like(dkp_ref)

        for r0 in range(0, t, tq):
            r1 = r0 + tq
            qn, qp = qn_ref[r0:r1, :].astype(MXU_DTYPE), qp_ref[r0:r1, :]
            dov, lse = do_ref[r0:r1, :], lse_ref[r0:r1, :]
            delta = jnp.sum(dov.astype(F32) * o_ref[r0:r1, :], axis=-1, keepdims=True)

            def piece(lo, hi, diagonal):
                s = _block_scores(qn, qp, kn_ref, kp_ref, lo, hi, diagonal)
                p = jnp.exp(s - lse)
                dp = _dot(dov, v_ref[lo:hi, :], _NT)
                ds_m = (p * (dp - delta) * ATTN_SCALE).astype(MXU_DTYPE)
                dkn_ref[lo:hi, :] += _dot(ds_m, qn, _TN)
                dkp_ref[lo:hi, :] += _dot(ds_m, qp, _TN)
                dv_ref[lo:hi, :] += _dot(p.astype(MXU_DTYPE), dov, _TN)
                return _dot(ds_m, kn_ref[lo:hi, :], _NN), _dot(ds_m, kp_ref[lo:hi, :], _NN)

            dqn, dqp = piece(r0, r1, True)
            if r0:
                dqn_b, dqp_b = piece(0, r0, False)
                dqn, dqp = dqn + dqn_b, dqp + dqp_b
            dqn_ref[r0:r1, :] = dqn
            dqp_ref[r0:r1, :] = dqp

    blk = lambda w: pl.BlockSpec((None, t, w), lambda hh: (hh, 0, 0))
    kpblk = pl.BlockSpec((t, QK_ROPE), lambda hh: (0, 0))
    cols = pl.BlockSpec((t, QK_NOPE), lambda hh: (0, hh))
    return pl.pallas_call(
        body,
        name=name,
        grid=(h,),
        in_specs=[cols, blk(QK_ROPE), cols, kpblk, cols, cols, cols, blk(1)],
        out_specs=[cols, blk(QK_ROPE), cols, kpblk, cols],
        out_shape=[jax.ShapeDtypeStruct((t, h * QK_NOPE), F32), jax.ShapeDtypeStruct((h, t, QK_ROPE), F32),
                   jax.ShapeDtypeStruct((t, h * QK_NOPE), F32), jax.ShapeDtypeStruct((t, QK_ROPE), F32),
                   jax.ShapeDtypeStruct((t, h * V_HEAD), F32)],
        compiler_params=_params(("arbitrary",)),
    )(qn, qp, kn, kp, v, do, o, lse)


def _rope(x1, x2, cos, sin, *, name, inverse=False):
    w = x1.shape[1]
    sign = -1.0 if inverse else 1.0

    def fn(rows, bc):
        a, b, c, s = rows
        s = sign * s
        return (a * c - b * s, a * s + b * c), ()

    return _rowwise(fn, [x1, x2, cos, sin], [], [(w, F32), (w, F32)], [], name=name)


def _mlp_fwd(x_m, w1, w2, tag, after=()):
    def epi(acc):
        r = jnp.maximum(acc, 0.0)
        return (r * r,)

    act = _mm(x_m, w1, name=f"{tag}_up", out_dtypes=(MXU_DTYPE,), epilogue=epi, after=after)
    y = _mm(act, w2, name=f"{tag}_down")
    return y, (x_m, act)


def _mlp_bwd(dy_m, saved, w1, w2, tag, after=()):
    x_m, act = saved
    dw2 = _mm(act, dy_m, ta=True, name=f"{tag}_dw2", out_dtypes=(MXU_DTYPE,))

    def epi(acc, av):
        return (acc * (2.0 * jnp.sqrt(av.astype(F32))),)

    dh = _mm(dy_m, w2, tb=True, name=f"{tag}_dact", out_dtypes=(MXU_DTYPE,), epilogue=epi, tile_extras=(act,),
             after=after)
    dw1 = _mm(x_m, dh, ta=True, out_dev=True, name=f"{tag}_dw1", out_dtypes=(MXU_DTYPE,))
    dx = _mm(dh, w1, tb=True, name=f"{tag}_dx")
    return [dx], {"ff_w1": dw1, "ff_w2": dw2}


def _sc_gate_specs(t, d):
    per = d // LANES
    return [pl.BlockSpec((t, LANES), lambda j, k=k: (0, k * per + j)) for k in range(3)]


def _sc_gate_fwd(u, conv_w, *, name):
    t, d = u.shape[0], u.shape[1] // 3
    kw = conv_w.shape[0]
    ch = _conv_chunk(t)

    def body(b_ref, c_ref, h_ref, w_ref, r_ref, pad_ref):
        pad_ref[0:CONV_PAD, :] = jnp.zeros((CONV_PAD, LANES), F32)
        pad_ref[CONV_PAD:CONV_PAD + t, :] = c_ref[...] * h_ref[...]
        for t0 in range(0, t, ch):
            q = jnp.zeros((ch, LANES), F32)
            for k in range(kw):
                s = kw - 1 - k
                q = q + w_ref[k:k + 1, :] * pad_ref[CONV_PAD - s + t0:CONV_PAD - s + t0 + ch, :]
            r_ref[t0:t0 + ch, :] = (b_ref[t0:t0 + ch, :] * q).astype(r_ref.dtype)

    blk = pl.BlockSpec((t, LANES), lambda j: (0, j))
    return pl.pallas_call(
        body,
        name=name,
        grid=(d // LANES,),
        in_specs=_sc_gate_specs(t, d) + [pl.BlockSpec((kw, LANES), lambda j: (0, j))],
        out_specs=blk,
        out_shape=jax.ShapeDtypeStruct((t, d), MXU_DTYPE),
        scratch_shapes=[pltpu.VMEM((CONV_PAD + t, LANES), F32)],
        compiler_params=_params(("parallel",)),
    )(u, u, u, conv_w)


def _sc_gate_bwd(dr, u, conv_w, *, name):
    t, d = u.shape[0], u.shape[1] // 3
    kw = conv_w.shape[0]
    ch = _conv_chunk(t)

    def body(dr_ref, b_ref, c_ref, h_ref, w_ref, db_ref, dp_ref, dw_ref, ppad_ref, dpad_ref):
        ppad_ref[0:CONV_PAD, :] = jnp.zeros((CONV_PAD, LANES), F32)
        ppad_ref[CONV_PAD:CONV_PAD + t, :] = c_ref[...] * h_ref[...]
        dpad_ref[0:t, :] = dr_ref[...] * b_ref[...]
        dpad_ref[t:t + CONV_PAD, :] = jnp.zeros((CONV_PAD, LANES), F32)
        for t0 in range(0, t, ch):
            q = jnp.zeros((ch, LANES), F32)
            dp = jnp.zeros((ch, LANES), F32)
            for k in range(kw):
                s = kw - 1 - k
                q = q + w_ref[k:k + 1, :] * ppad_ref[CONV_PAD - s + t0:CONV_PAD - s + t0 + ch, :]
                dp = dp + w_ref[k:k + 1, :] * dpad_ref[t0 + s:t0 + s + ch, :]
            db_ref[t0:t0 + ch, :] = (dr_ref[t0:t0 + ch, :] * q).astype(db_ref.dtype)
            dp_ref[t0:t0 + ch, :] = dp
        for k in range(kw):
            s = kw - 1 - k
            acc = jnp.zeros((ch, LANES), F32)
            for t0 in range(0, t, ch):
                acc = acc + dpad_ref[t0:t0 + ch, :] * ppad_ref[CONV_PAD - s + t0:CONV_PAD - s + t0 + ch, :]
            dw_ref[k:k + 1, :] = _colsum(acc)

    blk = pl.BlockSpec((t, LANES), lambda j: (0, j))
    wblk = pl.BlockSpec((kw, LANES), lambda j: (0, j))
    return pl.pallas_call(
        body,
        name=name,
        grid=(d // LANES,),
        in_specs=[blk] + _sc_gate_specs(t, d) + [wblk],
        out_specs=[blk, blk, wblk],
        out_shape=[jax.ShapeDtypeStruct((t, d), MXU_DTYPE), jax.ShapeDtypeStruct((t, d), F32),
                   jax.ShapeDtypeStruct((kw, d), F32)],
        scratch_shapes=[pltpu.VMEM((CONV_PAD + t, LANES), F32), pltpu.VMEM((CONV_PAD + t, LANES), F32)],
        compiler_params=_params(("parallel",)),
    )(dr, u, u, u, conv_w)


def _sc_fwd(x_m, w_in, conv_w, w_out, tag, after=()):
    u = _mm(x_m, w_in, name=f"{tag}_in", after=after)
    r = _sc_gate_fwd(u, conv_w, name=f"{tag}_gate")
    y = _mm(r, w_out, name=f"{tag}_out")
    return y, (x_m, u, r)


def _sc_bwd(dy_m, saved, w_in, conv_w, w_out, tag, after=()):
    d = D_MODEL
    x_m, u, r = saved
    dw_out = _mm(r, dy_m, ta=True, name=f"{tag}_dwout", out_dtypes=(MXU_DTYPE,))
    dr = _mm(dy_m, w_out, tb=True, name=f"{tag}_dr", after=after)
    db, dp, dconv = _sc_gate_bwd(dr, u, conv_w, name=f"{tag}_dgate")

    def du_fn(rows, bc):
        dbv, dpv, uv = rows
        return (jnp.concatenate([dbv.astype(F32), dpv * uv[:, 2 * d:3 * d], dpv * uv[:, d:2 * d]], axis=1),), ()

    du = _rowwise(du_fn, [db, dp, u], [], [(3 * d, MXU_DTYPE)], [], name=f"{tag}_du")[0]
    dw_in = _mm(x_m, du, ta=True, out_dev=True, name=f"{tag}_dwin", out_dtypes=(MXU_DTYPE,))
    dx = _mm(du, w_in, tb=True, name=f"{tag}_dx")
    return [dx], {"sc_w_in": dw_in, "sc_conv_w": dconv, "sc_w_out": dw_out}


def _cf_fwd(x_m, w, tag, after=()):
    d = D_MODEL

    def bias_epi(acc, bias):
        return (acc + bias,)

    u = _mm(x_m, w["cf_w_pw1"], name=f"{tag}_pw1", epilogue=bias_epi, col_extras=(w["cf_b_pw1"],),
            after=after)

    def glu_fn(rows, bc):
        uv = rows[0]
        return (uv[:, 0:d] * _sigmoid(uv[:, d:2 * d]),), ()

    h1 = _rowwise(glu_fn, [u], [], [(d, F32)], [], name=f"{tag}_glu")[0]
    h2 = _conv_fwd(h1, w["cf_dw_w"], w["cf_dw_b"], name=f"{tag}_conv")

    def norm_fn(rows, bc):
        hv = rows[0]
        mu = jnp.mean(hv, axis=-1, keepdims=True)
        hc = hv - mu
        rstd = lax.rsqrt(jnp.mean(hc * hc, axis=-1, keepdims=True) + LN_EPS)
        h3 = hc * rstd * bc[0] + bc[1]
        return (h3 * _sigmoid(h3),), ()

    h4 = _rowwise(norm_fn, [h2], [w["cf_norm_g"], w["cf_norm_b"]], [(d, MXU_DTYPE)], [], name=f"{tag}_norm")[0]
    y = _mm(h4, w["cf_w_pw2"], name=f"{tag}_pw2", epilogue=bias_epi, col_extras=(w["cf_b_pw2"],))
    return y, (x_m, u, h1, h2, h4)


def _cf_bwd(dy_m, dy_colsum, saved, w, tag, after=()):
    d = D_MODEL
    x_m, u, h1, h2, h4 = saved
    dw_pw2 = _mm(h4, dy_m, ta=True, name=f"{tag}_dwpw2", out_dtypes=(MXU_DTYPE,))
    dh4 = _mm(dy_m, w["cf_w_pw2"], tb=True, name=f"{tag}_dh4", after=after)

    def dnorm_fn(rows, bc):
        dh4v, hv = rows
        g, b = bc
        mu = jnp.mean(hv, axis=-1, keepdims=True)
        hc = hv - mu
        rstd = lax.rsqrt(jnp.mean(hc * hc, axis=-1, keepdims=True) + LN_EPS)
        xh = hc * rstd
        h3 = xh * g + b
        sg = _sigmoid(h3)
        dh3 = dh4v * (sg * (1.0 + h3 * (1.0 - sg)))
        dxh = dh3 * g
        dh2 = rstd * (dxh - jnp.mean(dxh, axis=-1, keepdims=True) - xh * jnp.mean(dxh * xh, axis=-1, keepdims=True))
        return (dh2,), (_colsum(dh3 * xh), _colsum(dh3), _colsum(dh2))

    dh2, dnorm_g, dnorm_b, ddw_b = _rowwise(dnorm_fn, [dh4, h2], [w["cf_norm_g"], w["cf_norm_b"]], [(d, F32)],
                                            [(1, d), (1, d), (1, d)], name=f"{tag}_dnorm")
    dh1, ddw_w = _conv_bwd(dh2, h1, w["cf_dw_w"], name=f"{tag}_dconv")

    def dglu_fn(rows, bc):
        dh1v, uv = rows
        a, gate = uv[:, 0:d], uv[:, d:2 * d]
        sg = _sigmoid(gate)
        du = jnp.concatenate([dh1v * sg, dh1v * a * sg * (1.0 - sg)], axis=1)
        return (du,), (_colsum(du),)

    du, db_pw1 = _rowwise(dglu_fn, [dh1, u], [], [(2 * d, MXU_DTYPE)], [(1, 2 * d)], name=f"{tag}_dglu")
    dw_pw1 = _mm(x_m, du, ta=True, out_dev=True, name=f"{tag}_dwpw1", out_dtypes=(MXU_DTYPE,))
    dx = _mm(du, w["cf_w_pw1"], tb=True, name=f"{tag}_dx")
    grads = {"cf_w_pw1": dw_pw1, "cf_b_pw1": db_pw1, "cf_dw_w": ddw_w, "cf_dw_b": ddw_b, "cf_norm_g": dnorm_g,
             "cf_norm_b": dnorm_b, "cf_w_pw2": dw_pw2, "cf_b_pw2": dy_colsum}
    return [dx], grads


def _rope_tables(t):
    pos = jnp.arange(t, dtype=F32)
    inv_freq = ROPE_THETA ** (-jnp.arange(0, QK_ROPE, 2, dtype=F32) / QK_ROPE)
    ang = pos[:, None] * inv_freq[None, :]
    return jnp.cos(ang), jnp.sin(ang)


ROPE_HALF = QK_ROPE // 2
Q_NOPE_COLS = MLA_HEADS * QK_NOPE
Q_HALF_COLS = MLA_HEADS * ROPE_HALF


def _mla_fwd(x_m, w, tag, after=()):
    t = x_m.shape[0]
    cos, sin = _rope_tables(t)
    cos_h, sin_h = jnp.tile(cos, (1, MLA_HEADS)), jnp.tile(sin, (1, MLA_HEADS))
    cq_pre = _mm(x_m, w["mla_w_dq"], name=f"{tag}_dq", after=after)
    cq = _rms_fwd(cq_pre, w["mla_g_q"], name=f"{tag}_qnorm")
    q = _mm(cq, w["mla_w_uq"], name=f"{tag}_uq")
    r1, r2 = _rope(q[:, Q_NOPE_COLS:Q_NOPE_COLS + Q_HALF_COLS], q[:, Q_NOPE_COLS + Q_HALF_COLS:], cos_h, sin_h,
                   name=f"{tag}_qrope")
    qp_h = jnp.concatenate([r1.reshape(t, MLA_HEADS, ROPE_HALF), r2.reshape(t, MLA_HEADS, ROPE_HALF)],
                           axis=-1).transpose(1, 0, 2).astype(MXU_DTYPE)
    ckv_full = _mm(x_m, w["mla_w_dkv"], name=f"{tag}_dkv")
    ckv_pre = ckv_full[:, :KV_LORA]
    ckv = _rms_fwd(ckv_pre, w["mla_g_kv"], name=f"{tag}_kvnorm")
    k1, k2 = _rope(ckv_full[:, KV_LORA:KV_LORA + ROPE_HALF], ckv_full[:, KV_LORA + ROPE_HALF:], cos, sin,
                   name=f"{tag}_krope")
    kp = jnp.concatenate([k1, k2], axis=-1).astype(MXU_DTYPE)
    kn = _mm(ckv, w["mla_w_uk"], name=f"{tag}_uk", out_dtypes=(MXU_DTYPE,))
    v = _mm(ckv, w["mla_w_uv"], name=f"{tag}_uv", out_dtypes=(MXU_DTYPE,))
    o, o_f, lse = _attn_fwd(q, qp_h, kn, kp, v, name=f"{tag}_attn")
    y = _mm(o, w["mla_w_o"], name=f"{tag}_o")
    return y, (x_m, cq_pre, cq, ckv_pre, ckv, q, qp_h, kn, kp, v, lse, o, o_f, (cos, sin, cos_h, sin_h))


def _mla_bwd(dy_m, saved, w, tag, after=()):
    x_m, cq_pre, cq, ckv_pre, ckv, q, qp_h, kn, kp, v, lse, o, o_f, (cos, sin, cos_h, sin_h) = saved
    t = x_m.shape[0]
    half = ROPE_HALF
    dw_o = _mm(o, dy_m, ta=True, name=f"{tag}_dwo", out_dtypes=(MXU_DTYPE,))
    do = _mm(dy_m, w["mla_w_o"], tb=True, name=f"{tag}_do", out_dtypes=(MXU_DTYPE,), after=after)
    dqn, dqp_h, dkn, dkp, dv = _attn_bwd(q, qp_h, kn, kp, v, do, o_f, lse, name=f"{tag}_dattn")
    dqp = dqp_h.transpose(1, 0, 2)
    d1, d2 = _rope(dqp[:, :, :half].reshape(t, -1), dqp[:, :, half:].reshape(t, -1), cos_h, sin_h,
                   name=f"{tag}_dqrope", inverse=True)
    dq = jnp.concatenate([dqn, d1, d2], axis=-1).astype(MXU_DTYPE)
    dw_uq = _mm(cq, dq, ta=True, name=f"{tag}_dwuq", out_dtypes=(MXU_DTYPE,))
    dcq = _mm(dq, w["mla_w_uq"], tb=True, name=f"{tag}_dcq")
    dcq_pre, dg_q = _rms_bwd(cq_pre, w["mla_g_q"], dcq, name=f"{tag}_dqnorm")
    dcq_pre_m = dcq_pre.astype(MXU_DTYPE)
    dw_dq = _mm(x_m, dcq_pre_m, ta=True, name=f"{tag}_dwdq", out_dtypes=(MXU_DTYPE,))
    dx_q = _mm(dcq_pre_m, w["mla_w_dq"], tb=True, name=f"{tag}_dxq")
    dw_uk = _mm(ckv, dkn, ta=True, name=f"{tag}_dwuk", out_dtypes=(MXU_DTYPE,))
    dw_uv = _mm(ckv, dv, ta=True, name=f"{tag}_dwuv", out_dtypes=(MXU_DTYPE,))
    dckv_k = _mm(dkn, w["mla_w_uk"], tb=True, name=f"{tag}_dckvk")
    dckv = _mm(dv, w["mla_w_uv"], tb=True, name=f"{tag}_dckv", epilogue=lambda acc, other: (acc + other,),
               tile_extras=(dckv_k,))
    dckv_pre, dg_kv = _rms_bwd(ckv_pre, w["mla_g_kv"], dckv, name=f"{tag}_dkvnorm")
    dk1, dk2 = _rope(dkp[:, :half], dkp[:, half:], cos, sin, name=f"{tag}_dkrope", inverse=True)
    dckv_full = jnp.concatenate([dckv_pre, dk1, dk2], axis=-1).astype(MXU_DTYPE)
    dw_dkv = _mm(x_m, dckv_full, ta=True, name=f"{tag}_dwdkv", out_dtypes=(MXU_DTYPE,))
    dx_kv = _mm(dckv_full, w["mla_w_dkv"], tb=True, name=f"{tag}_dxkv")
    grads = {"mla_w_dq": dw_dq, "mla_g_q": dg_q, "mla_w_uq": dw_uq, "mla_w_dkv": dw_dkv, "mla_g_kv": dg_kv,
             "mla_w_uk": dw_uk, "mla_w_uv": dw_uv, "mla_w_o": dw_o}
    return [dx_q, dx_kv], grads


def _cast_input(x):
    cast = lambda rows, bc: ((rows[0],), ())
    return _rowwise(cast, [x], [], [(D_MODEL, MXU_DTYPE)], [], name="x_cast")[0]


def _hooked(value, hook):
    token = None if hook is None else hook(value)
    return () if token is None else (token,)


def _mix_fwd(i, cur, cur_m, wl, hook=None, after=()):
    mixer = i % N_MIXERS
    if mixer == 0:
        y, s_mix = _sc_fwd(cur_m, wl["sc_w_in"], wl["sc_conv_w"], wl["sc_w_out"], f"l{i}_sc", after=after)
    elif mixer == 1:
        y, s_mix = _mla_fwd(cur_m, wl, f"l{i}_mla", after=after)
    else:
        y, s_mix = _cf_fwd(cur_m, wl, f"l{i}_cf", after=after)
    cur, cur_m, xh_mix, rs_mix = _ln_fwd(cur, y, wl["ln_mix_g"], wl["ln_mix_b"], name=f"l{i}_lnmix",
                                         after=_hooked(y, hook))
    return cur, cur_m, (s_mix, xh_mix, rs_mix)


def _ff_fwd(i, cur, cur_m, wl, hook=None, after=()):
    y, s_ff = _mlp_fwd(cur_m, wl["ff_w1"], wl["ff_w2"], f"l{i}_ff", after=after)
    cur, cur_m, xh_ff, rs_ff = _ln_fwd(cur, y, wl["ln_ff_g"], wl["ln_ff_b"], name=f"l{i}_lnff", after=_hooked(y, hook))
    return cur, cur_m, (s_ff, xh_ff, rs_ff)


def _layer_fwd(i, cur, cur_m, wl):
    cur, cur_m, s_mix = _mix_fwd(i, cur, cur_m, wl)
    cur, cur_m, s_ff = _ff_fwd(i, cur, cur_m, wl)
    return cur, cur_m, (*s_mix, *s_ff)


def _loss_and_grad(out, target):
    d = D_MODEL

    def loss_fn(rows, bc):
        err = rows[0] - rows[1]
        return (err * (1.0 / d),), (_colsum(err * err),)

    dy, sq = _rowwise(loss_fn, [out, target], [], [(d, F32)], [(1, d)], name="loss")
    return (0.5 / d) * jnp.sum(sq), dy


def _ff_bwd(i, terms, saved, wl, hook=None, after=()):
    s_ff, xh_ff, rs_ff = saved
    grads = {}
    dz, dz_m, grads["ln_ff_g"], grads["ln_ff_b"], _ = _ln_bwd(terms, xh_ff, rs_ff, wl["ln_ff_g"], name=f"l{i}_dlnff",
                                                              after=after)
    dxs, g_ff = _mlp_bwd(dz_m, s_ff, wl["ff_w1"], wl["ff_w2"], f"l{i}_ff", after=_hooked(dz, hook))
    grads.update(g_ff)
    return [(ALPHA, dz)] + [(1.0, v) for v in dxs], grads


def _mix_bwd(i, terms, saved, wl, hook=None, after=()):
    mixer = i % N_MIXERS
    s_mix, xh_mix, rs_mix = saved
    grads = {}
    dz, dz_m, grads["ln_mix_g"], grads["ln_mix_b"], dz_sum = _ln_bwd(terms, xh_mix, rs_mix, wl["ln_mix_g"],
                                                                     name=f"l{i}_dlnmix", after=after)
    behind = _hooked(dz, hook)
    if mixer == 0:
        dxs, g_mix = _sc_bwd(dz_m, s_mix, wl["sc_w_in"], wl["sc_conv_w"], wl["sc_w_out"], f"l{i}_sc", after=behind)
    elif mixer == 1:
        dxs, g_mix = _mla_bwd(dz_m, s_mix, wl, f"l{i}_mla", after=behind)
    else:
        dxs, g_mix = _cf_bwd(dz_m, dz_sum, s_mix, wl, f"l{i}_cf", after=behind)
    grads.update(g_mix)
    return [(ALPHA, dz)] + [(1.0, v) for v in dxs], grads


def _layer_bwd(i, terms, saved, wl):
    terms, g_ff = _ff_bwd(i, terms, saved[3:], wl)
    terms, g_mix = _mix_bwd(i, terms, saved[:3], wl)
    return terms, {**g_ff, **g_mix}


def _sum_terms(terms, *, name):
    coefs = [c for c, _ in terms]

    def sum_fn(rows, bc):
        acc = None
        for c, v in zip(coefs, rows):
            tv = v if c == 1.0 else c * v
            acc = tv if acc is None else acc + tv
        return (acc,), ()

    return _rowwise(sum_fn, [v for _, v in terms], [], [(D_MODEL, F32)], [], name=name)[0]


def _local_step(x, target, layers):
    cur, cur_m = x, _cast_input(x)
    saved = []
    for i in range(DEPTH):
        cur, cur_m, s = _layer_fwd(i, cur, cur_m, layers[i])
        saved.append(s)
    loss, dy = _loss_and_grad(cur, target)
    terms = [(1.0, dy)]
    grads = [None] * DEPTH
    for i in reversed(range(DEPTH)):
        terms, grads[i] = _layer_bwd(i, terms, saved[i], layers[i])
    return loss, _sum_terms(terms, name="grad_x"), grads


MESH_ID = pl.DeviceIdType.MESH
ANY = pl.BlockSpec(memory_space=pl.ANY)


def _block_id(px, py, pc):
    return 4 * px + 2 * py + pc


COPIES_PER_GATHER = 7


def _all_gather(shards, *, name):
    n = len(shards)

    def body(*refs):
        x_refs, out_refs = refs[:n], refs[n:2 * n]
        send_sems, recv_sems, local_sems = refs[2 * n:]
        x, y, core = lax.axis_index("x"), lax.axis_index("y"), lax.axis_index("c")
        me, sibling = (x, y, core), (x, y, 1 - core)
        chips = [(1 - x, y), (x, 1 - y), (1 - x, 1 - y)]

        def copy(t, k, block, to, from_input=False):
            dst = out_refs[t].at[_block_id(*block)]
            return pltpu.make_async_remote_copy(
                src_ref=x_refs[t] if from_input else dst, dst_ref=dst,
                send_sem=send_sems.at[t * COPIES_PER_GATHER + k], recv_sem=recv_sems.at[t * COPIES_PER_GATHER + k],
                device_id=to, device_id_type=MESH_ID)

        started = []
        mine = [pltpu.make_async_copy(x_refs[t], out_refs[t].at[_block_id(*me)], local_sems.at[t]) for t in range(n)]
        for t in range(n):
            mine[t].start()
            first = [copy(t, 0, me, sibling, from_input=True)]
            first += [copy(t, 1 + j, me, (*chip, core), from_input=True) for j, chip in enumerate(chips)]
            for cp in first:
                cp.start()
            started += first
        for t in range(n):
            for j, chip in enumerate(chips):
                copy(t, 1 + j, (*chip, core), me).wait_recv()
                passed = copy(t, 4 + j, (*chip, core), sibling)
                passed.start()
                started.append(passed)
        for t in range(n):
            copy(t, 0, sibling, me).wait_recv()
            for j, chip in enumerate(chips):
                copy(t, 4 + j, (*chip, 1 - core), me).wait_recv()
        for cp in started:
            cp.wait_send()
        for cp in mine:
            cp.wait()

    n_sems = n * COPIES_PER_GATHER
    outs = pl.pallas_call(
        body,
        name=name,
        out_shape=[jax.ShapeDtypeStruct((N_DEV, *s.shape), s.dtype) for s in shards],
        in_specs=[ANY] * n,
        out_specs=[ANY] * n,
        scratch_shapes=[pltpu.SemaphoreType.DMA((n_sems,)), pltpu.SemaphoreType.DMA((n_sems,)),
                        pltpu.SemaphoreType.DMA((n,))],
    )(*shards)
    return list(outs)


def _swap_with_sibling(parts, *, name):
    n = len(parts)

    def body(*refs):
        p_refs, got_refs = refs[:n], refs[n:2 * n]
        send_sems, recv_sems = refs[2 * n:]
        x, y, core = lax.axis_index("x"), lax.axis_index("y"), lax.axis_index("c")
        chips = [(x, y), (1 - x, y), (x, 1 - y), (1 - x, 1 - y)]
        copies = []
        for t in range(n):
            for k, (px, py) in enumerate(chips):
                copies.append(pltpu.make_async_remote_copy(
                    src_ref=p_refs[t].at[_block_id(px, py, 1 - core)], dst_ref=got_refs[t].at[k],
                    send_sem=send_sems.at[4 * t + k], recv_sem=recv_sems.at[4 * t + k], device_id=(x, y, 1 - core),
                    device_id_type=MESH_ID))
        for cp in copies:
            cp.start()
        for cp in copies:
            cp.wait()

    outs = pl.pallas_call(
        body,
        name=name,
        out_shape=[jax.ShapeDtypeStruct((4, *p.shape[1:]), p.dtype) for p in parts],
        in_specs=[ANY] * n,
        out_specs=[ANY] * n,
        scratch_shapes=[pltpu.SemaphoreType.DMA((4 * n,)), pltpu.SemaphoreType.DMA((4 * n,))],
    )(*parts)
    return list(outs)


def _swap_between_chips(parts, *, name):
    n = len(parts)

    def body(*refs):
        p_refs, got_refs = refs[:n], refs[n:2 * n]
        send_sems, recv_sems = refs[2 * n:]
        x, y, core = lax.axis_index("x"), lax.axis_index("y"), lax.axis_index("c")
        chips = [(1 - x, y), (x, 1 - y), (1 - x, 1 - y)]
        copies = []
        for t in range(n):
            for j, (px, py) in enumerate(chips):
                copies.append(pltpu.make_async_remote_copy(
                    src_ref=p_refs[t].at[j], dst_ref=got_refs[t].at[j], send_sem=send_sems.at[3 * t + j],
                    recv_sem=recv_sems.at[3 * t + j], device_id=(px, py, core), device_id_type=MESH_ID))
        for cp in copies:
            cp.start()
        for cp in copies:
            cp.wait()

    outs = pl.pallas_call(
        body,
        name=name,
        out_shape=[jax.ShapeDtypeStruct(p.shape, p.dtype) for p in parts],
        in_specs=[ANY] * n,
        out_specs=[ANY] * n,
        scratch_shapes=[pltpu.SemaphoreType.DMA((3 * n,)), pltpu.SemaphoreType.DMA((3 * n,))],
    )(*parts)
    return list(outs)


def _chip_partials(parts, got, *, name):
    _, r, c = parts.shape
    tb = _row_tile(r, 512)
    x, y, core = lax.axis_index("x"), lax.axis_index("y"), lax.axis_index("c")
    chips = [(x, y), (1 - x, y), (x, 1 - y), (1 - x, 1 - y)]
    ids = jnp.stack([_block_id(px, py, core) for px, py in chips]).astype(jnp.int32)

    def body(ids_ref, p0, p1, p2, p3, g_ref, own_ref, out_ref):
        own_ref[...] = p0[...].astype(F32) + g_ref[0].astype(F32)
        for j, p in enumerate((p1, p2, p3)):
            out_ref[j] = (p[...].astype(F32) + g_ref[j + 1].astype(F32)).astype(out_ref.dtype)

    def part_spec(k):
        return pl.BlockSpec((None, tb, c), lambda i, ids_ref: (ids_ref[k], i, 0))

    grid_spec = pltpu.PrefetchScalarGridSpec(
        num_scalar_prefetch=1,
        grid=(r // tb,),
        in_specs=[part_spec(k) for k in range(4)] + [pl.BlockSpec((4, tb, c), lambda i, ids_ref: (0, i, 0))],
        out_specs=[pl.BlockSpec((tb, c), lambda i, ids_ref: (i, 0)), pl.BlockSpec((3, tb, c), lambda i, ids_ref: (0, i, 0))],
    )
    return pl.pallas_call(
        body,
        name=name,
        grid_spec=grid_spec,
        out_shape=[jax.ShapeDtypeStruct((r, c), F32), jax.ShapeDtypeStruct((3, r, c), parts.dtype)],
        compiler_params=_params(("parallel",)),
    )(ids, parts, parts, parts, parts, got)


def _chip_partials_group(parts, got, *, name):
    n = len(parts)
    rows = [p.shape[1] for p in parts]
    steps = 4 if all(r % (4 * BF16_ROWS) == 0 for r in rows) else (2 if all(r % (2 * BF16_ROWS) == 0 for r in rows) else 1)
    x, y, core = lax.axis_index("x"), lax.axis_index("y"), lax.axis_index("c")
    chips = [(x, y), (1 - x, y), (x, 1 - y), (1 - x, 1 - y)]
    ids = jnp.stack([_block_id(px, py, core) for px, py in chips]).astype(jnp.int32)

    def body(ids_ref, *refs):
        ins, outs = refs[:5 * n], refs[5 * n:]
        for t in range(n):
            p0, p1, p2, p3, g_ref = ins[5 * t:5 * t + 5]
            own_ref, out_ref = outs[2 * t], outs[2 * t + 1]
            own_ref[...] = p0[...].astype(F32) + g_ref[0].astype(F32)
            for j, p in enumerate((p1, p2, p3)):
                out_ref[j] = (p[...].astype(F32) + g_ref[j + 1].astype(F32)).astype(out_ref.dtype)

    in_specs, out_specs, out_shape, operands = [], [], [], []
    for p, g in zip(parts, got):
        _, r, c = p.shape
        tb = r // steps
        in_specs += [pl.BlockSpec((None, tb, c), lambda i, ids_ref, k=k: (ids_ref[k], i, 0)) for k in range(4)]
        in_specs.append(pl.BlockSpec((4, tb, c), lambda i, ids_ref: (0, i, 0)))
        out_specs += [pl.BlockSpec((tb, c), lambda i, ids_ref: (i, 0)), pl.BlockSpec((3, tb, c), lambda i, ids_ref: (0, i, 0))]
        out_shape += [jax.ShapeDtypeStruct((r, c), F32), jax.ShapeDtypeStruct((3, r, c), p.dtype)]
        operands += [p, p, p, p, g]
    outs = pl.pallas_call(
        body,
        name=name,
        grid_spec=pltpu.PrefetchScalarGridSpec(num_scalar_prefetch=1, grid=(steps,), in_specs=in_specs,
                                               out_specs=out_specs),
        out_shape=out_shape,
        compiler_params=_params(("parallel",)),
    )(ids, *operands)
    return [(outs[2 * t], outs[2 * t + 1]) for t in range(n)]


def _sum_blocks(first, blocks, *, name):
    n, r, c = blocks.shape
    tb = _row_tile(r, 512)

    def body(*refs):
        b_ref, o_ref = refs[-2], refs[-1]
        acc = refs[0][...] if first is not None else b_ref[0].astype(F32)
        for j in range(0 if first is not None else 1, n):
            acc = acc + b_ref[j].astype(F32)
        o_ref[...] = acc

    row_spec = pl.BlockSpec((tb, c), lambda i: (i, 0))
    return pl.pallas_call(
        body,
        name=name,
        grid=(r // tb,),
        in_specs=([row_spec] if first is not None else []) + [pl.BlockSpec((n, tb, c), lambda i: (0, i, 0))],
        out_specs=row_spec,
        out_shape=jax.ShapeDtypeStruct((r, c), F32),
        compiler_params=_params(("parallel",)),
    )(*([first] if first is not None else []), blocks)


def _reduce_scatter(parts, *, tag):
    got = _swap_with_sibling(parts, name=f"{tag}_sibling")
    sums = [_chip_partials(p, g, name=f"{tag}_chipsum{t}") for t, (p, g) in enumerate(zip(parts, got))]
    arrived = _swap_between_chips([others for _, others in sums], name=f"{tag}_chips")
    return [(own, arr) for (own, _), arr in zip(sums, arrived)]


HBM = pl.BlockSpec(memory_space=pltpu.HBM)
SEM = pl.BlockSpec(memory_space=pltpu.SEMAPHORE)
DATAFLOW = pltpu.SideEffectType.DATAFLOW_SIDE_EFFECTING


def _tie(value, *before):
    return lax.optimization_barrier((value, *before))[0]


def _split_call(name, arrays, sems, n_new, body, after=None):
    n_a, n_s = len(arrays), len(sems)
    n_after = 0 if after is None else 1

    def kernel_body(*refs):
        new = refs[n_a + n_s + n_after:n_a + n_s + n_after + 2] if n_new else (None, None)
        body(refs[:n_a], refs[n_a:n_a + n_s], *new)
        refs[-1][...] = jnp.zeros_like(refs[-1])

    n_sem_out = 2 if n_new else 0
    out_shape = [pltpu.SemaphoreType.DMA((n_new,))] * n_sem_out
    out_shape += [pltpu.HBM(a.shape, a.dtype) for a in arrays]
    out_shape += [jax.ShapeDtypeStruct((SUBLANES, LANES), F32)]
    outs = pl.pallas_call(
        kernel_body,
        name=name,
        out_shape=out_shape,
        in_specs=[HBM] * n_a + [SEM] * n_s + [ANY] * n_after,
        out_specs=[SEM] * n_sem_out + [HBM] * n_a + [pl.BlockSpec(memory_space=pltpu.VMEM)],
        input_output_aliases={i: n_sem_out + i for i in range(n_a)},
        compiler_params=pltpu.CompilerParams(has_side_effects=DATAFLOW),
    )(*[pltpu.with_memory_space_constraint(a, pltpu.HBM) for a in arrays], *sems, *([after] if n_after else []))
    return tuple(outs[:n_sem_out]), list(outs[n_sem_out:n_sem_out + n_a]), outs[-1]


def _remote(src, dst, send, recv, k, to):
    return pltpu.make_async_remote_copy(src_ref=src, dst_ref=dst, send_sem=send.at[k], recv_sem=recv.at[k],
                                        device_id=to, device_id_type=MESH_ID)


def _land_block(land_ref, block, by_cols, width):
    if by_cols:
        return land_ref.at[:, pl.ds(pl.multiple_of(block * width, LANES), width)]
    return land_ref.at[block]


def _place_in_window(shard, *, name):
    r, c = shard.shape
    tb = _row_tile(r, 512)
    me = _block_id(lax.axis_index("x"), lax.axis_index("y"), lax.axis_index("c")).astype(jnp.int32).reshape(1)

    def body(me_ref, s_ref, o_ref):
        o_ref[...] = s_ref[...].astype(o_ref.dtype)

    return pl.pallas_call(
        body,
        name=name,
        grid_spec=pltpu.PrefetchScalarGridSpec(
            num_scalar_prefetch=1, grid=(r // tb,),
            in_specs=[pl.BlockSpec((tb, c), lambda i, me_ref: (i, 0))],
            out_specs=pl.BlockSpec((tb, c), lambda i, me_ref: (i, me_ref[0]))),
        out_shape=jax.ShapeDtypeStruct((r, N_DEV * c), MXU_DTYPE),
        compiler_params=_params(("parallel",)),
    )(me, shard)


def _gather_start(shards, *, name, after=None, by_cols=None):
    n = len(shards)
    by_cols = list(by_cols) if by_cols is not None else [False] * n
    me = _block_id(lax.axis_index("x"), lax.axis_index("y"), lax.axis_index("c"))
    lands = []
    for i, (s, cols) in enumerate(zip(shards, by_cols)):
        if cols:
            lands.append(_place_in_window(s, name=f"{name}_place{i}"))
        else:
            lands.append(lax.dynamic_update_index_in_dim(lax.empty((N_DEV, *s.shape), s.dtype), s, me, 0))
    widths = [s.shape[-1] for s in shards]

    def body(refs, _, send, recv):
        x, y, core = lax.axis_index("x"), lax.axis_index("y"), lax.axis_index("c")
        targets = [(x, y, 1 - core), (1 - x, y, core), (x, 1 - y, core), (1 - x, 1 - y, core)]
        for t in range(n):
            mine = _land_block(refs[t], _block_id(x, y, core), by_cols[t], widths[t])
            for k, to in enumerate(targets):
                _remote(mine, mine, send, recv, 4 * t + k, to).start()

    sems, thru, token = _split_call(name, lands, [], 4 * n, body, after=after)
    return {"n": n, "sems1": sems, "arrays": thru, "token": token, "by_cols": by_cols, "widths": widths}


def _gather_pass_on(st, *, name, after):
    n = st["n"]

    def body(refs, sems, send, recv):
        send1, recv1 = sems
        x, y, core = lax.axis_index("x"), lax.axis_index("y"), lax.axis_index("c")
        chips = [(1 - x, y), (x, 1 - y), (1 - x, 1 - y)]
        for t in range(n):
            for j, (px, py) in enumerate(chips):
                block = _land_block(refs[t], _block_id(px, py, core), st["by_cols"][t], st["widths"][t])
                _remote(block, block, send1, recv1, 4 * t + 1 + j, (x, y, 1 - core)).wait_recv()
                _remote(block, block, send, recv, 3 * t + j, (x, y, 1 - core)).start()

    sems2, thru, token = _split_call(name, st["arrays"], list(st["sems1"]), 3 * n, body, after=after)
    return {**st, "sems2": sems2, "arrays": thru, "token": token}


def _gather_finish(st, *, name, after):
    n = st["n"]

    def body(refs, sems, *_):
        send1, recv1, send2, recv2 = sems
        x, y, core = lax.axis_index("x"), lax.axis_index("y"), lax.axis_index("c")
        sibling = (x, y, 1 - core)
        chips = [(1 - x, y), (x, 1 - y), (1 - x, 1 - y)]
        for t in range(n):
            place = lambda px, py, pc, t=t: _land_block(refs[t], _block_id(px, py, pc), st["by_cols"][t],
                                                        st["widths"][t])
            mine = place(x, y, core)
            for k in range(4):
                _remote(mine, mine, send1, recv1, 4 * t + k, sibling).wait_send()
            _remote(mine, place(x, y, 1 - core), send1, recv1, 4 * t, sibling).wait_recv()
            for j, (px, py) in enumerate(chips):
                _remote(place(px, py, core), place(px, py, core), send2, recv2, 3 * t + j, sibling).wait_send()
                _remote(mine, place(px, py, 1 - core), send2, recv2, 3 * t + j, sibling).wait_recv()

    _, thru, _ = _split_call(name, st["arrays"], [*st["sems1"], *st["sems2"]], 0, body, after=after)
    return thru


def _swap_start(sources, lands, plan, per_array, *, name):
    n = len(sources)

    def body(refs, _, send, recv):
        x, y, core = lax.axis_index("x"), lax.axis_index("y"), lax.axis_index("c")
        for t in range(n):
            for k in range(per_array):
                block, to = plan(x, y, core, k)
                _remote(refs[t].at[block], refs[n + t].at[k], send, recv, per_array * t + k, to).start()

    sems, thru, token = _split_call(name, sources + lands, [], per_array * n, body)
    return {"n": n, "sems": sems, "arrays": thru, "token": token, "plan": plan, "per_array": per_array}


def _swap_finish(st, *, name, after):
    n, plan, per_array = st["n"], st["plan"], st["per_array"]

    def body(refs, sems, *_):
        send, recv = sems
        x, y, core = lax.axis_index("x"), lax.axis_index("y"), lax.axis_index("c")
        for t in range(n):
            for k in range(per_array):
                block, to = plan(x, y, core, k)
                cp = _remote(refs[t].at[block], refs[n + t].at[k], send, recv, per_array * t + k, to)
                cp.wait_send()
                cp.wait_recv()

    _, thru, _ = _split_call(name, st["arrays"], list(st["sems"]), 0, body, after=after)
    return thru[:n], thru[n:]


def _to_sibling(x, y, core, k):
    px, py = [(x, y), (1 - x, y), (x, 1 - y), (1 - x, 1 - y)][k]
    return _block_id(px, py, 1 - core), (x, y, 1 - core)


def _to_chip(x, y, core, k):
    px, py = [(1 - x, y), (x, 1 - y), (1 - x, 1 - y)][k]
    return k, (px, py, core)


PARAMS = {
    "sc_w_in": ((2, 1024, 3072), 2), "sc_conv_w": ((2, 3, 1024), 2), "sc_w_out": ((2, 1024, 1024), 1),
    "mla_w_dq": ((1, 1024, 384), 1), "mla_g_q": ((1, 384), None), "mla_w_uq": ((1, 384, 1536), 2),
    "mla_w_dkv": ((1, 1024, 320), 1), "mla_g_kv": ((1, 256), None), "mla_w_uk": ((1, 256, 8, 128), 1),
    "mla_w_uv": ((1, 256, 8, 128), 1), "mla_w_o": ((1, 1024, 1024), 1), "cf_w_pw1": ((1, 1024, 2048), 2),
    "cf_b_pw1": ((1, 2048), 1), "cf_dw_w": ((1, 31, 1024), 2), "cf_dw_b": ((1, 1024), 1), "cf_norm_g": ((1, 1024), 1),
    "cf_norm_b": ((1, 1024), 1), "cf_w_pw2": ((1, 1024, 1024), 1), "cf_b_pw2": ((1, 1024), 1),
    "ff_w1": ((4, 1024, 4096), 2), "ff_w2": ((4, 4096, 1024), 1), "ln_mix_g": ((4, 1024), None),
    "ln_mix_b": ((4, 1024), None), "ln_ff_g": ((4, 1024), None), "ln_ff_b": ((4, 1024), None),
}
NAMES = list(PARAMS)
BIG = ["sc_w_in", "sc_w_out", "mla_w_dq", "mla_w_uq", "mla_w_dkv", "mla_w_uk", "mla_w_uv", "mla_w_o", "cf_w_pw1",
       "cf_w_pw2", "ff_w1", "ff_w2"]
SMALL = [n for n in NAMES if n not in BIG]
SMALL_SHARDED = [n for n in SMALL if PARAMS[n][1] is not None]
SMALL_WIDTH = LANES
DEV_BLOCKED = ("sc_w_in", "cf_w_pw1", "ff_w1")
MIXER_BIG = {0: ["sc_w_in", "sc_w_out"], 1: ["mla_w_dq", "mla_w_uq", "mla_w_dkv", "mla_w_uk", "mla_w_uv", "mla_w_o"],
             2: ["cf_w_pw1", "cf_w_pw2"]}


def _layer_big(i):
    return [(n, i // N_MIXERS) for n in MIXER_BIG[i % N_MIXERS]] + [("ff_w1", i), ("ff_w2", i)]


def _shard_2d(name, a):
    return a.reshape(a.shape[0], -1)


def _shard_for_gather(name, a):
    flat = _shard_2d(name, a)
    return flat if name in DEV_BLOCKED else flat.astype(MXU_DTYPE)


def _as_operand(name, g):
    if name in DEV_BLOCKED:
        return g
    if name == "mla_w_uq":
        by_head = g.transpose(1, 0, 2)
        pieces = [by_head[:, :, :QK_NOPE], by_head[:, :, QK_NOPE:QK_NOPE + ROPE_HALF], by_head[:, :, QK_NOPE + ROPE_HALF:]]
        return jnp.concatenate([p.reshape(g.shape[1], -1) for p in pieces], axis=1)
    return g.reshape(-1, g.shape[2])


def _as_parts(name, grad):
    if name in DEV_BLOCKED:
        return grad
    if name == "mla_w_uq":
        rows = grad.shape[0]
        pieces = [grad[:, :Q_NOPE_COLS].reshape(rows, MLA_HEADS, QK_NOPE),
                  grad[:, Q_NOPE_COLS:Q_NOPE_COLS + Q_HALF_COLS].reshape(rows, MLA_HEADS, ROPE_HALF),
                  grad[:, Q_NOPE_COLS + Q_HALF_COLS:].reshape(rows, MLA_HEADS, ROPE_HALF)]
        return jnp.concatenate(pieces, axis=2).transpose(1, 0, 2)
    return grad.reshape(N_DEV, -1, grad.shape[1])


def _shard_shape(name):
    shape, ax = PARAMS[name]
    if ax is None:
        return shape
    return tuple(s // N_DEV if i == ax else s for i, s in enumerate(shape))


def _rows_of(shape, width, align):
    n = 1
    for s in shape:
        n *= s
    rows = -(-n // width)
    return -(-rows // align) * align


def _pack_rows(arrays, width, align, dtype, lead=0):
    segs = []
    for a in arrays:
        batch = a.shape[:lead]
        flat = a.astype(dtype).reshape(*batch, -1)
        rows = _rows_of(a.shape[lead:], width, align)
        flat = jnp.pad(flat, [(0, 0)] * lead + [(0, rows * width - flat.shape[-1])])
        segs.append(flat.reshape(*batch, rows, width))
    return jnp.concatenate(segs, axis=lead)


def _unpack_rows(buf, shapes, width, align):
    out, off = [], 0
    lead = buf.shape[:-2]
    for shape in shapes:
        n = 1
        for s in shape:
            n *= s
        rows = _rows_of(shape, width, align)
        seg = buf[..., off:off + rows, :].reshape(*lead, rows * width)[..., :n]
        out.append(seg.reshape(*lead, *shape))
        off += rows
    return out


def _merge_shards(stacked, ax):
    moved = jnp.moveaxis(stacked, 0, ax)
    shape = moved.shape
    return moved.reshape(*shape[:ax], shape[ax] * shape[ax + 1], *shape[ax + 2:])


def _split_shards(full, ax):
    shape = full.shape
    split = full.reshape(*shape[:ax], N_DEV, shape[ax] // N_DEV, *shape[ax + 1:])
    return jnp.moveaxis(split, ax, 0)


def _adam_math(wv, gv, mv, vv):
    m_new = ADAM_B1 * mv + (1.0 - ADAM_B1) * gv
    v_new = ADAM_B2 * vv + (1.0 - ADAM_B2) * (gv * gv)
    m_hat = m_new / (1.0 - ADAM_B1 ** ADAM_STEP)
    v_hat = v_new / (1.0 - ADAM_B2 ** ADAM_STEP)
    delta = -ADAM_LR * (m_hat / (jnp.sqrt(v_hat) + ADAM_EPS) + ADAM_WD * wv)
    return delta, m_new, v_new


def _adamw(w, g, m, v, *, name):
    shape = w.shape
    c = shape[-1]

    def fn(rows, bc):
        return _adam_math(*rows), ()

    flat = [a.reshape(-1, c) for a in (w, g, m, v)]
    outs = _rowwise(fn, flat, [], [(c, F32)] * 3, [], name=name, tb_cap=512)
    return tuple(o.reshape(shape) for o in outs)


def _adamw_reduced(w, m, v, layer_grads, *, name, after=()):
    n_l, r, c = w.shape
    tb = _row_tile(r, 256 if c > 512 else 512)
    nb = r // tb

    def body(*refs):
        w_ref, m_ref, v_ref = refs[:3]
        g_refs = refs[3:3 + 2 * n_l]
        g_out, d_out, m_out, v_out = refs[3 + 2 * n_l + len(after):]
        layer = pl.program_id(0)
        for l in range(n_l):
            @pl.when(layer == l)
            def _(l=l):
                own, arr = g_refs[2 * l], g_refs[2 * l + 1]
                g = own[...] + arr[0].astype(F32)
                g = g + arr[1].astype(F32)
                g = g + arr[2].astype(F32)
                delta, m_new, v_new = _adam_math(w_ref[...], g, m_ref[...], v_ref[...])
                g_out[...] = g
                d_out[...] = delta
                m_out[...] = m_new
                v_out[...] = v_new

    stacked = pl.BlockSpec((tb, c), lambda l, i: (l * nb + i, 0))
    in_specs = [stacked] * 3
    operands = [a.reshape(n_l * r, c) for a in (w, m, v)]
    for l, (own, arr) in enumerate(layer_grads):
        in_specs.append(pl.BlockSpec((tb, c), lambda ll, i, l=l: (jnp.where(ll == l, i, 0), 0)))
        in_specs.append(pl.BlockSpec((3, tb, c), lambda ll, i, l=l: (0, jnp.where(ll == l, i, 0), 0)))
        operands += [own, arr]
    in_specs += [ANY] * len(after)
    operands += list(after)
    outs = pl.pallas_call(
        body,
        name=name,
        grid=(n_l, nb),
        in_specs=in_specs,
        out_specs=[stacked] * 4,
        out_shape=[jax.ShapeDtypeStruct((n_l * r, c), F32)] * 4,
        compiler_params=_params(("arbitrary", "arbitrary")),
    )(*operands)
    return tuple(o.reshape(n_l, r, c) for o in outs)


def _train_step(x, target, weights, m_state, v_state):
    me = _block_id(lax.axis_index("x"), lax.axis_index("y"), lax.axis_index("c"))

    small_shapes = [_shard_shape(n) for n in SMALL_SHARDED]
    gather = {0: _gather_start([_pack_rows([weights[n] for n in SMALL_SHARDED], SMALL_WIDTH, SUBLANES, F32)]
                               + [_shard_for_gather(n, weights[n][0]) for n in MIXER_BIG[0]],
                               name="gather_g0_start", by_cols=[False] + [n in DEV_BLOCKED for n in MIXER_BIG[0]])}
    cur, cur_m = x[0], _cast_input(x[0])
    gather[0] = _gather_pass_on(gather[0], name="gather_g0_pass", after=cur_m)
    first = _gather_finish(gather[0], name="gather_g0_finish", after=gather[0]["token"])
    small_all, first_matrices = first[0], first[1:]
    small_full = {n: weights[n] for n in SMALL if PARAMS[n][1] is None}
    for n, stacked in zip(SMALL_SHARDED, _unpack_rows(small_all, small_shapes, SMALL_WIDTH, SUBLANES)):
        small_full[n] = _merge_shards(stacked, PARAMS[n][1])

    def small_weights(i):
        wl = {}
        for n in SMALL:
            if n.startswith("ln_"):
                wl[n] = small_full[n][i:i + 1]
            elif n.startswith(("sc_", "mla_", "cf_")[i % N_MIXERS]):
                v = small_full[n][i // N_MIXERS]
                wl[n] = v if v.ndim == 2 else v.reshape(1, -1)
        return wl

    layers = [small_weights(i) for i in range(DEPTH)]
    groups = [(kind, i) for i in range(DEPTH) for kind in ("mix", "ff")]

    def group_names(g):
        kind, i = g
        if kind == "mix":
            return [(n, i // N_MIXERS) for n in MIXER_BIG[i % N_MIXERS]]
        return [("ff_w1", i), ("ff_w2", i)]

    def shards_of(g):
        return [_shard_for_gather(n, weights[n][j]) for n, j in group_names(g)]

    def install(g, gathered):
        for (n, _), a in zip(group_names(g), gathered):
            layers[g[1]][n] = _as_operand(n, a)

    install(groups[0], first_matrices)
    whole = lambda g: [n in DEV_BLOCKED for n, _ in group_names(g)]
    gather[1] = _gather_start(shards_of(groups[1]), name="gather_g1_start", after=gather[0]["token"],
                              by_cols=whole(groups[1]))
    saved = {}
    for k, g in enumerate(groups):
        kind, i = g
        tokens = [gather[1]["token"]] if k == 0 else []
        if k + 2 < len(groups):
            gather[k + 2] = _gather_start(shards_of(groups[k + 2]), name=f"gather_g{k + 2}_start",
                                          after=gather[k + 1]["token"], by_cols=whole(groups[k + 2]))
            tokens.append(gather[k + 2]["token"])

        def pass_on(y, k=k):
            if k + 1 == len(groups):
                return None
            gather[k + 1] = _gather_pass_on(gather[k + 1], name=f"gather_g{k + 1}_pass", after=y)
            return gather[k + 1]["token"]

        phase = _mix_fwd if kind == "mix" else _ff_fwd
        cur, cur_m, saved[g] = phase(i, cur, cur_m, layers[i], hook=pass_on, after=tokens)
        if k + 1 < len(groups):
            install(groups[k + 1], _gather_finish(gather[k + 1], name=f"gather_g{k + 1}_finish", after=cur))

    loss_local, dy = _loss_and_grad(cur, target[0])
    loss = lax.psum(loss_local, MESH_AXES)

    reduced = {n: [None] * PARAMS[n][0][0] for n in BIG}
    small_grads = {n: [None] * PARAMS[n][0][0] for n in SMALL}
    reduce = {}

    def to_chips(p, after):
        st = reduce[p]
        parts, got = _swap_finish(st["sibling"], name=f"reduce_p{p}_sibling_finish", after=after)
        sums = _chip_partials_group(parts, got, name=f"reduce_p{p}_chipsum")
        st["own"] = [own for own, _ in sums]
        others = [o for _, o in sums]
        st["chips"] = _swap_start(others, [lax.empty(o.shape, o.dtype) for o in others], _to_chip, 3,
                                  name=f"reduce_p{p}_chips_start")
        return st["chips"]["token"]

    def arrived(p, after):
        st = reduce[p]
        _, got = _swap_finish(st["chips"], name=f"reduce_p{p}_chips_finish", after=after)
        for (n, j), own, arr in zip(st["names"], st["own"], got):
            reduced[n][j] = (own, arr)

    terms = [(1.0, dy)]
    back = list(reversed(groups))
    for p, g in enumerate(back):
        kind, i = g

        def after_first_kernel(dz, p=p):
            return to_chips(p - 1, dz) if p >= 1 else None

        phase = _mix_bwd if kind == "mix" else _ff_bwd
        tokens = [reduce[p - 1]["sibling"]["token"]] if p >= 1 else []
        terms, grads = phase(i, terms, saved[g], layers[i], hook=after_first_kernel, after=tokens)
        for n in SMALL:
            if n in grads:
                small_grads[n][i if n.startswith("ln_") else i // N_MIXERS] = grads[n]
        if p >= 2:
            arrived(p - 2, terms[-1][1])
        names = group_names(g)
        parts = [_as_parts(n, grads[n]) for n, _ in names]
        reduce[p] = {"names": names, "sibling": _swap_start(
            parts, [lax.empty((4, *a.shape[1:]), a.dtype) for a in parts], _to_sibling, 4,
            name=f"reduce_p{p}_sibling_start")}
    grad_x = _sum_terms(terms, name="grad_x")
    last = len(back) - 1
    small_mine = _pack_rows([jnp.stack(small_grads[n]).reshape(PARAMS[n][0]) for n in SMALL], SMALL_WIDTH, SUBLANES, F32)
    small_gather = _gather_start([small_mine], name="gather_small_grads_start")
    tail = to_chips(last, small_gather["token"])
    arrived(last - 1, tail)

    grad, delta, new_m, new_v = {}, {}, {}, {}

    def adamw_matrix(n, after):
        shape = weights[n].shape
        view = lambda a: a.reshape(shape[0], shape[1], -1)
        outs = _adamw_reduced(view(weights[n]), view(m_state[n]), view(v_state[n]), reduced[n], name=f"adamw_{n}",
                              after=after)
        grad[n], delta[n], new_m[n], new_v[n] = [o.reshape(shape) for o in outs]
        return outs[0]

    done = tail
    for n in BIG:
        if n not in MIXER_BIG[0]:
            done = adamw_matrix(n, [done])
    small_gather = _gather_pass_on(small_gather, name="gather_small_grads_pass", after=done)
    small_all = _gather_finish(small_gather, name="gather_small_grads_finish", after=small_gather["token"])[0]
    small_sum = _sum_blocks(None, small_all, name="sum_small_grads")
    for n, g in zip(SMALL, _unpack_rows(small_sum, [PARAMS[n][0] for n in SMALL], SMALL_WIDTH, SUBLANES)):
        ax = PARAMS[n][1]
        grad[n] = g if ax is None else lax.dynamic_index_in_dim(_split_shards(g, ax), me, axis=0, keepdims=False)
    small_shards = [_shard_shape(n) for n in SMALL]
    packed = [_pack_rows([src[n] for n in SMALL], SMALL_WIDTH, SUBLANES, F32) for src in (weights, grad, m_state, v_state)]
    small_out = _adamw(*packed, name="adamw_small")
    for outs, dst in zip(small_out, (delta, new_m, new_v)):
        for n, a in zip(SMALL, _unpack_rows(outs, small_shards, SMALL_WIDTH, SUBLANES)):
            dst[n] = a
    arrived(last, small_sum)
    for n in MIXER_BIG[0]:
        adamw_matrix(n, [])
    return (loss, grad_x[None], *[grad[n] for n in NAMES], *[delta[n] for n in NAMES],
            *[new_m[n] for n in NAMES], *[new_v[n] for n in NAMES])


def kernel(x, sc_w_in, sc_conv_w, sc_w_out, mla_w_dq, mla_g_q, mla_w_uq, mla_w_dkv, mla_g_kv, mla_w_uk, mla_w_uv, mla_w_o, cf_w_pw1, cf_b_pw1, cf_dw_w, cf_dw_b, cf_norm_g, cf_norm_b, cf_w_pw2, cf_b_pw2, ff_w1, ff_w2, ln_mix_g, ln_mix_b, ln_ff_g, ln_ff_b, loss_target, m_sc_w_in, m_sc_conv_w, m_sc_w_out, m_mla_w_dq, m_mla_g_q, m_mla_w_uq, m_mla_w_dkv, m_mla_g_kv, m_mla_w_uk, m_mla_w_uv, m_mla_w_o, m_cf_w_pw1, m_cf_b_pw1, m_cf_dw_w, m_cf_dw_b, m_cf_norm_g, m_cf_norm_b, m_cf_w_pw2, m_cf_b_pw2, m_ff_w1, m_ff_w2, m_ln_mix_g, m_ln_mix_b, m_ln_ff_g, m_ln_ff_b, v_sc_w_in, v_sc_conv_w, v_sc_w_out, v_mla_w_dq, v_mla_g_q, v_mla_w_uq, v_mla_w_dkv, v_mla_g_kv, v_mla_w_uk, v_mla_w_uv, v_mla_w_o, v_cf_w_pw1, v_cf_b_pw1, v_cf_dw_w, v_cf_dw_b, v_cf_norm_g, v_cf_norm_b, v_cf_w_pw2, v_cf_b_pw2, v_ff_w1, v_ff_w2, v_ln_mix_g, v_ln_mix_b, v_ln_ff_g, v_ln_ff_b):
    w_list = (sc_w_in, sc_conv_w, sc_w_out, mla_w_dq, mla_g_q, mla_w_uq, mla_w_dkv, mla_g_kv, mla_w_uk, mla_w_uv, mla_w_o,
              cf_w_pw1, cf_b_pw1, cf_dw_w, cf_dw_b, cf_norm_g, cf_norm_b, cf_w_pw2, cf_b_pw2, ff_w1, ff_w2, ln_mix_g,
              ln_mix_b, ln_ff_g, ln_ff_b)
    m_list = (m_sc_w_in, m_sc_conv_w, m_sc_w_out, m_mla_w_dq, m_mla_g_q, m_mla_w_uq, m_mla_w_dkv, m_mla_g_kv, m_mla_w_uk,
              m_mla_w_uv, m_mla_w_o, m_cf_w_pw1, m_cf_b_pw1, m_cf_dw_w, m_cf_dw_b, m_cf_norm_g, m_cf_norm_b, m_cf_w_pw2,
              m_cf_b_pw2, m_ff_w1, m_ff_w2, m_ln_mix_g, m_ln_mix_b, m_ln_ff_g, m_ln_ff_b)
    v_list = (v_sc_w_in, v_sc_conv_w, v_sc_w_out, v_mla_w_dq, v_mla_g_q, v_mla_w_uq, v_mla_w_dkv, v_mla_g_kv, v_mla_w_uk,
              v_mla_w_uv, v_mla_w_o, v_cf_w_pw1, v_cf_b_pw1, v_cf_dw_w, v_cf_dw_b, v_cf_norm_g, v_cf_norm_b, v_cf_w_pw2,
              v_cf_b_pw2, v_ff_w1, v_ff_w2, v_ln_mix_g, v_ln_mix_b, v_ln_ff_g, v_ln_ff_b)
    return _train_step(x, loss_target, dict(zip(NAMES, w_list)), dict(zip(NAMES, m_list)), dict(zip(NAMES, v_list)))
```

```python
import functools

import jax
import jax.numpy as jnp
from jax import lax
from jax.experimental import pallas as pl
from jax.experimental.pallas import tpu as pltpu

F32 = jnp.float32
MXU_DTYPE = jnp.bfloat16

N_DEV = 8
D_MODEL = 1024
DEPTH = 4
N_MIXERS = 3
CHUNK = 64
ALPHA = (2.0 * DEPTH) ** 0.25
LN_EPS = 1e-5
RMS_EPS = 1e-6
MLA_HEADS = 8
QK_NOPE = 128
QK_ROPE = 64
V_HEAD = 128
Q_LORA = 384
KV_LORA = 256
ROPE_THETA = 10000.0
ADAM_LR = 0.001
ADAM_B1 = 0.9
ADAM_B2 = 0.999
ADAM_EPS = 1e-08
ADAM_WD = 0.01
ADAM_STEP = 10

LANES = 128
SUBLANES = 8
BF16_ROWS = 16
VMEM_LIMIT = 56 * 1024 * 1024
MM_VMEM_BUDGET = 36 * 1024 * 1024
CONV_PAD = 32
MASK_VALUE = -1e30

MESH_AXES = ("x", "y", "c")


def _params(semantics):
    return pltpu.CompilerParams(dimension_semantics=semantics, vmem_limit_bytes=VMEM_LIMIT)


def _tile(n, cap):
    if n <= cap:
        return n
    t = cap - cap % LANES
    while t >= LANES:
        if n % t == 0:
            return t
        t -= LANES
    raise ValueError(f"no tile for {n} under {cap}")


def _row_tile(rows, cap):
    if rows <= cap:
        return rows
    t = cap - cap % SUBLANES
    while t >= SUBLANES:
        if rows % t == 0:
            return t
        t -= SUBLANES
    raise ValueError(f"no row tile for {rows} under {cap}")


def _mm(a, b, *, name, ta=False, tb=False, b_dev=False, out_dev=False, out_dtypes=(F32,), epilogue=None,
        tile_extras=(), col_extras=(), after=(), tm_cap=2048, tn_cap=1024, tk_cap=4096):
    m, k = (a.shape[1], a.shape[0]) if ta else a.shape
    if b_dev:
        nd, b_rows, ns = b.shape
        n, kb = (b_rows, nd * ns) if tb else (nd * ns, b_rows)
    else:
        n, kb = (b.shape[0], b.shape[1]) if tb else (b.shape[1], b.shape[0])
    assert k == kb, (a.shape, b.shape, ta, tb)
    if ta or tile_extras or (b_dev and tb):
        tm_cap = min(tm_cap, 1024)
    tk = ns if (b_dev and tb) else _tile(k, tk_cap)
    nk = k // tk
    fixed_tn = ns if (b_dev and not tb) else (n // N_DEV if out_dev else None)
    if out_dev:
        assert not tile_extras and not col_extras and n % N_DEV == 0
    out_bytes = sum(jnp.dtype(dt).itemsize for dt in out_dtypes) + sum(e.dtype.itemsize for e in tile_extras)
    tm, tn = None, None
    for cand_m, cand_n in ((tm_cap, tn_cap), (tm_cap, tn_cap // 2), (tm_cap // 2, tn_cap), (tm_cap // 2, tn_cap // 2),
                           (tm_cap // 4, tn_cap // 2), (tm_cap // 4, tn_cap // 4)):
        tm, tn = _tile(m, cand_m), (fixed_tn if fixed_tn is not None else _tile(n, cand_n))
        need = 2 * (tm * tk * a.dtype.itemsize + tk * tn * b.dtype.itemsize) + 2 * tm * tn * out_bytes
        need += tm * tn * 4 * (2 if nk > 1 else 1)
        if need <= MM_VMEM_BUDGET:
            break
    n_te, n_ce, n_out = len(tile_extras), len(col_extras), len(out_dtypes)
    dims = (((0 if ta else 1,), (1 if tb else 0,)), ((), ()))

    def body(*refs):
        a_ref, b_ref = refs[0], refs[1]
        te_refs = refs[2:2 + n_te]
        ce_refs = refs[2 + n_te:2 + n_te + n_ce]
        first_out = 2 + n_te + n_ce + len(after)
        out_refs = refs[first_out:first_out + n_out]

        def finish(acc):
            if epilogue is None:
                outs = (acc,)
            else:
                outs = epilogue(acc, *[r[...] for r in te_refs], *[r[...] for r in ce_refs])
            for o_ref, o in zip(out_refs, outs):
                o_ref[...] = o.astype(o_ref.dtype)

        def product():
            return lax.dot_general(a_ref[...].astype(MXU_DTYPE), b_ref[...].astype(MXU_DTYPE), dims,
                                   preferred_element_type=F32)

        if nk == 1:
            finish(product())
        else:
            acc_ref = refs[-1]
            kk = pl.program_id(2)

            @pl.when(kk == 0)
            def _():
                acc_ref[...] = jnp.zeros_like(acc_ref)

            acc_ref[...] += product()

            @pl.when(kk == nk - 1)
            def _():
                finish(acc_ref[...])

    a_spec = pl.BlockSpec((tk, tm), lambda i, j, kk: (kk, i)) if ta else pl.BlockSpec((tm, tk), lambda i, j, kk: (i, kk))
    if b_dev and tb:
        b_spec = pl.BlockSpec((None, tn, tk), lambda i, j, kk: (kk, j, 0))
    elif b_dev:
        b_spec = pl.BlockSpec((None, tk, tn), lambda i, j, kk: (j, kk, 0))
    elif tb:
        b_spec = pl.BlockSpec((tn, tk), lambda i, j, kk: (j, kk))
    else:
        b_spec = pl.BlockSpec((tk, tn), lambda i, j, kk: (kk, j))
    tile_spec = pl.BlockSpec((tm, tn), lambda i, j, kk: (i, j))
    col_spec = pl.BlockSpec((1, tn), lambda i, j, kk: (0, j))
    if out_dev:
        out_spec = pl.BlockSpec((None, tm, tn), lambda i, j, kk: (j, i, 0))
        out_shape = [jax.ShapeDtypeStruct((N_DEV, m, tn), dt) for dt in out_dtypes]
    else:
        out_spec = tile_spec
        out_shape = [jax.ShapeDtypeStruct((m, n), dt) for dt in out_dtypes]
    outs = pl.pallas_call(
        body,
        name=name,
        grid=(m // tm, n // tn, nk),
        in_specs=[a_spec, b_spec] + [tile_spec] * n_te + [col_spec] * n_ce + [ANY] * len(after),
        out_specs=[out_spec] * n_out,
        out_shape=out_shape,
        scratch_shapes=[pltpu.VMEM((tm, tn), F32)] if nk > 1 else [],
        compiler_params=_params(("parallel", "parallel", "arbitrary")),
    )(a, b, *tile_extras, *col_extras, *after)
    return outs[0] if n_out == 1 else tuple(outs)


def _rowwise(fn, rows, bcast, out_rows, out_accs, *, name, tb_cap=256, after=()):
    t = rows[0].shape[0]
    tb = _row_tile(t, tb_cap)
    n_r, n_b, n_o, n_a = len(rows), len(bcast), len(out_rows), len(out_accs)

    def body(*refs):
        r_refs = refs[:n_r]
        b_refs = refs[n_r:n_r + n_b]
        first_out = n_r + n_b + len(after)
        o_refs = refs[first_out:first_out + n_o]
        a_refs = refs[first_out + n_o:]
        outs, accs = fn([r[...] for r in r_refs], [r[...] for r in b_refs])
        for o_ref, o in zip(o_refs, outs):
            o_ref[...] = o.astype(o_ref.dtype)
        if n_a:
            i = pl.program_id(0)

            @pl.when(i == 0)
            def _():
                for a_ref, acc in zip(a_refs, accs):
                    a_ref[...] = acc

            @pl.when(i > 0)
            def _():
                for a_ref, acc in zip(a_refs, accs):
                    a_ref[...] += acc

    in_specs = [pl.BlockSpec((tb, r.shape[1]), lambda i: (i, 0)) for r in rows]
    in_specs += [pl.BlockSpec(b.shape, lambda i: (0, 0)) for b in bcast]
    in_specs += [ANY] * len(after)
    out_specs = [pl.BlockSpec((tb, c), lambda i: (i, 0)) for c, _ in out_rows]
    out_specs += [pl.BlockSpec(s, lambda i: (0, 0)) for s in out_accs]
    out_shape = [jax.ShapeDtypeStruct((t, c), dt) for c, dt in out_rows]
    out_shape += [jax.ShapeDtypeStruct(s, F32) for s in out_accs]
    outs = pl.pallas_call(
        body,
        name=name,
        grid=(t // tb,),
        in_specs=in_specs,
        out_specs=out_specs,
        out_shape=out_shape,
        compiler_params=_params(("arbitrary",)),
    )(*rows, *bcast, *after)
    return tuple(outs)


def _colsum(v):
    return jnp.sum(v, axis=0, keepdims=True)


def _sigmoid(v):
    return 1.0 / (1.0 + jnp.exp(-v))


def _ln_fwd(x_res, y_sub, g, b, *, name, after=()):
    d = x_res.shape[1]

    def fn(rows, bc):
        z = ALPHA * rows[0] + rows[1]
        mu = jnp.mean(z, axis=-1, keepdims=True)
        zc = z - mu
        var = jnp.mean(zc * zc, axis=-1, keepdims=True)
        rstd = lax.rsqrt(var + LN_EPS)
        xhat = zc * rstd
        out = xhat * bc[0] + bc[1]
        return (out, out, xhat, rstd), ()

    return _rowwise(fn, [x_res, y_sub], [g, b], [(d, F32), (d, MXU_DTYPE), (d, F32), (1, F32)], [], name=name,
                    after=after, tb_cap=512)


def _ln_bwd(terms, xhat, rstd, g, *, name, after=()):
    d = xhat.shape[1]
    coefs = [c for c, _ in terms]

    def fn(rows, bc):
        xh, rs = rows[0], rows[1]
        dout = None
        for c, v in zip(coefs, rows[2:]):
            tv = v if c == 1.0 else c * v
            dout = tv if dout is None else dout + tv
        dxh = dout * bc[0]
        m1 = jnp.mean(dxh, axis=-1, keepdims=True)
        m2 = jnp.mean(dxh * xh, axis=-1, keepdims=True)
        dz = rs * (dxh - m1 - xh * m2)
        return (dz, dz), (_colsum(dout * xh), _colsum(dout), _colsum(dz))

    return _rowwise(fn, [xhat, rstd] + [v for _, v in terms], [g], [(d, F32), (d, MXU_DTYPE)],
                    [(1, d), (1, d), (1, d)], name=name, after=after, tb_cap=512)


def _rms_fwd(x, g, *, name):
    c = x.shape[1]

    def fn(rows, bc):
        v = rows[0]
        r = lax.rsqrt(jnp.mean(v * v, axis=-1, keepdims=True) + RMS_EPS)
        return (v * r * bc[0],), ()

    return _rowwise(fn, [x], [g], [(c, MXU_DTYPE)], [], name=name)[0]


def _rms_bwd(x, g, dy, *, name):
    c = x.shape[1]

    def fn(rows, bc):
        v, dyv = rows
        r = lax.rsqrt(jnp.mean(v * v, axis=-1, keepdims=True) + RMS_EPS)
        xh = v * r
        dxh = dyv * bc[0]
        dx = r * (dxh - xh * jnp.mean(dxh * xh, axis=-1, keepdims=True))
        return (dx,), (_colsum(dyv * xh),)

    return _rowwise(fn, [x, dy], [g], [(c, F32)], [(1, c)], name=name)


def _conv_chunk(t):
    return _row_tile(t, 256)


def _conv_fwd(x, w, bias, *, name):
    t, c = x.shape
    kw = w.shape[0]
    ch = _conv_chunk(t)

    def body(x_ref, w_ref, b_ref, o_ref, pad_ref):
        pad_ref[0:CONV_PAD, :] = jnp.zeros((CONV_PAD, LANES), F32)
        pad_ref[CONV_PAD:CONV_PAD + t, :] = x_ref[...]
        for t0 in range(0, t, ch):
            acc = jnp.broadcast_to(b_ref[...], (ch, LANES))
            for k in range(kw):
                s = kw - 1 - k
                acc = acc + w_ref[k:k + 1, :] * pad_ref[CONV_PAD - s + t0:CONV_PAD - s + t0 + ch, :]
            o_ref[t0:t0 + ch, :] = acc

    return pl.pallas_call(
        body,
        name=name,
        grid=(c // LANES,),
        in_specs=[pl.BlockSpec((t, LANES), lambda j: (0, j)), pl.BlockSpec((kw, LANES), lambda j: (0, j)),
                  pl.BlockSpec((1, LANES), lambda j: (0, j))],
        out_specs=pl.BlockSpec((t, LANES), lambda j: (0, j)),
        out_shape=jax.ShapeDtypeStruct((t, c), F32),
        scratch_shapes=[pltpu.VMEM((CONV_PAD + t, LANES), F32)],
        compiler_params=_params(("parallel",)),
    )(x, w, bias)


def _conv_bwd(dy, x, w, *, name):
    t, c = x.shape
    kw = w.shape[0]
    ch = _conv_chunk(t)

    def body(dy_ref, x_ref, w_ref, dx_ref, dw_ref, xpad_ref, dpad_ref):
        xpad_ref[0:CONV_PAD, :] = jnp.zeros((CONV_PAD, LANES), F32)
        xpad_ref[CONV_PAD:CONV_PAD + t, :] = x_ref[...]
        dpad_ref[0:t, :] = dy_ref[...]
        dpad_ref[t:t + CONV_PAD, :] = jnp.zeros((CONV_PAD, LANES), F32)
        for t0 in range(0, t, ch):
            acc = jnp.zeros((ch, LANES), F32)
            for k in range(kw):
                s = kw - 1 - k
                acc = acc + w_ref[k:k + 1, :] * dpad_ref[t0 + s:t0 + s + ch, :]
            dx_ref[t0:t0 + ch, :] = acc
        for k in range(kw):
            s = kw - 1 - k
            acc = jnp.zeros((ch, LANES), F32)
            for t0 in range(0, t, ch):
                acc = acc + dy_ref[t0:t0 + ch, :] * xpad_ref[CONV_PAD - s + t0:CONV_PAD - s + t0 + ch, :]
            dw_ref[k:k + 1, :] = _colsum(acc)

    blk = pl.BlockSpec((t, LANES), lambda j: (0, j))
    wblk = pl.BlockSpec((kw, LANES), lambda j: (0, j))
    return pl.pallas_call(
        body,
        name=name,
        grid=(c // LANES,),
        in_specs=[blk, blk, wblk],
        out_specs=[blk, wblk],
        out_shape=[jax.ShapeDtypeStruct((t, c), F32), jax.ShapeDtypeStruct((kw, c), F32)],
        scratch_shapes=[pltpu.VMEM((CONV_PAD + t, LANES), F32), pltpu.VMEM((CONV_PAD + t, LANES), F32)],
        compiler_params=_params(("parallel",)),
    )(dy, x, w)


ATTN_SCALE = (QK_NOPE + QK_ROPE) ** -0.5
_NT = (((1,), (1,)), ((), ()))
_TN = (((0,), (0,)), ((), ()))
_NN = (((1,), (0,)), ((), ()))


def _dot(a, b, dims):
    return lax.dot_general(a, b, dims, preferred_element_type=F32)


def _block_scores(q_cat, kn_ref, kp_ref, lo, hi, diagonal):
    k_cat = jnp.concatenate([kn_ref[lo:hi, :], kp_ref[lo:hi, :]], axis=1)
    s = _dot(q_cat, k_cat, _NT) * ATTN_SCALE
    if diagonal:
        shift = CHUNK.bit_length() - 1
        rows = lax.broadcasted_iota(jnp.int32, s.shape, 0)
        cols = lax.broadcasted_iota(jnp.int32, s.shape, 1)
        s = jnp.where(jnp.right_shift(cols, shift) <= jnp.right_shift(rows, shift), s, MASK_VALUE)
    return s, k_cat


def _attn_fwd(qn, qp, kv, kp, *, name):
    h, t, _ = qp.shape
    tq = _row_tile(t, 256)
    assert tq % CHUNK == 0

    def body(qn_ref, qp_ref, kn_ref, kp_ref, v_ref, o_ref, of_ref, lse_ref):
        for r0 in range(0, t, tq):
            r1 = r0 + tq
            q_cat = jnp.concatenate([qn_ref[r0:r1, :].astype(MXU_DTYPE), qp_ref[r0:r1, :]], axis=1)
            s_d, _ = _block_scores(q_cat, kn_ref, kp_ref, r0, r1, True)
            m = jnp.max(s_d, axis=-1, keepdims=True)
            if r0:
                s_b, _ = _block_scores(q_cat, kn_ref, kp_ref, 0, r0, False)
                m = jnp.maximum(m, jnp.max(s_b, axis=-1, keepdims=True))
            p_d = jnp.exp(s_d - m)
            l = jnp.sum(p_d, axis=-1, keepdims=True)
            acc = _dot(p_d.astype(MXU_DTYPE), v_ref[r0:r1, :], _NN)
            if r0:
                p_b = jnp.exp(s_b - m)
                l = l + jnp.sum(p_b, axis=-1, keepdims=True)
                acc = acc + _dot(p_b.astype(MXU_DTYPE), v_ref[0:r0, :], _NN)
            o = acc / l
            o_ref[r0:r1, :] = o.astype(o_ref.dtype)
            of_ref[r0:r1, :] = o
            lse_ref[r0:r1, :] = m + jnp.log(l)

    blk = lambda w: pl.BlockSpec((None, t, w), lambda hh: (hh, 0, 0))
    cols = pl.BlockSpec((t, QK_NOPE), lambda hh: (0, hh))
    keys = pl.BlockSpec((t, QK_NOPE), lambda hh: (0, 2 * hh))
    values = pl.BlockSpec((t, V_HEAD), lambda hh: (0, 2 * hh + 1))
    return pl.pallas_call(
        body,
        name=name,
        grid=(h,),
        in_specs=[cols, blk(QK_ROPE), keys, pl.BlockSpec((t, QK_ROPE), lambda hh: (0, 0)), values],
        out_specs=[cols, cols, blk(1)],
        out_shape=[jax.ShapeDtypeStruct((t, h * V_HEAD), MXU_DTYPE), jax.ShapeDtypeStruct((t, h * V_HEAD), F32),
                   jax.ShapeDtypeStruct((h, t, 1), F32)],
        compiler_params=_params(("parallel",)),
    )(qn, qp, kv, kp, kv)


def _attn_bwd(qn, qp, kv, kp, do, o, lse, *, name):
    h, t, _ = qp.shape
    tq = _row_tile(t, 256)

    def body(qn_ref, qp_ref, kn_ref, kp_ref, v_ref, do_ref, o_ref, lse_ref, dqn_ref, dqp_ref, dkv_ref, dkp_ref):
        dkv_ref[...] = jnp.zeros_like(dkv_ref)

        @pl.when(pl.program_id(0) == 0)
        def _():
            dkp_ref[...] = jnp.zeros_like(dkp_ref)

        for r0 in range(0, t, tq):
            r1 = r0 + tq
            q_cat = jnp.concatenate([qn_ref[r0:r1, :].astype(MXU_DTYPE), qp_ref[r0:r1, :]], axis=1)
            dov, lse = do_ref[r0:r1, :], lse_ref[r0:r1, :]
            delta = jnp.sum(dov.astype(F32) * o_ref[r0:r1, :], axis=-1, keepdims=True)

            def piece(lo, hi, diagonal):
                s, k_cat = _block_scores(q_cat, kn_ref, kp_ref, lo, hi, diagonal)
                p = jnp.exp(s - lse)
                dp = _dot(dov, v_ref[lo:hi, :], _NT)
                ds_m = (p * (dp - delta) * ATTN_SCALE).astype(MXU_DTYPE)
                dk_cat = _dot(ds_m, q_cat, _TN)
                dkv_ref[lo:hi, :QK_NOPE] += dk_cat[:, :QK_NOPE]
                dkp_ref[lo:hi, :] += dk_cat[:, QK_NOPE:]
                dkv_ref[lo:hi, QK_NOPE:] += _dot(p.astype(MXU_DTYPE), dov, _TN)
                return _dot(ds_m, k_cat, _NN)

            dq_cat = piece(r0, r1, True)
            if r0:
                dq_cat = dq_cat + piece(0, r0, False)
            dqn_ref[r0:r1, :] = dq_cat[:, :QK_NOPE]
            dqp_ref[r0:r1, :] = dq_cat[:, QK_NOPE:]

    blk = lambda w: pl.BlockSpec((None, t, w), lambda hh: (hh, 0, 0))
    kpblk = pl.BlockSpec((t, QK_ROPE), lambda hh: (0, 0))
    cols = pl.BlockSpec((t, QK_NOPE), lambda hh: (0, hh))
    keys = pl.BlockSpec((t, QK_NOPE), lambda hh: (0, 2 * hh))
    values = pl.BlockSpec((t, V_HEAD), lambda hh: (0, 2 * hh + 1))
    both = pl.BlockSpec((t, QK_NOPE + V_HEAD), lambda hh: (0, hh))
    return pl.pallas_call(
        body,
        name=name,
        grid=(h,),
        in_specs=[cols, blk(QK_ROPE), keys, kpblk, values, cols, cols, blk(1)],
        out_specs=[cols, blk(QK_ROPE), both, kpblk],
        out_shape=[jax.ShapeDtypeStruct((t, h * QK_NOPE), F32), jax.ShapeDtypeStruct((h, t, QK_ROPE), F32),
                   jax.ShapeDtypeStruct((t, h * (QK_NOPE + V_HEAD)), F32), jax.ShapeDtypeStruct((t, QK_ROPE), F32)],
        compiler_params=_params(("arbitrary",)),
    )(qn, qp, kv, kp, kv, do, o, lse)


def _rope(x1, x2, cos, sin, *, name, inverse=False):
    w = x1.shape[1]
    sign = -1.0 if inverse else 1.0

    def fn(rows, bc):
        a, b, c, s = rows
        s = sign * s
        return (a * c - b * s, a * s + b * c), ()

    return _rowwise(fn, [x1, x2, cos, sin], [], [(w, F32), (w, F32)], [], name=name)


def _mlp_fwd(x_m, w1, w2, tag, after=()):
    def epi(acc):
        r = jnp.maximum(acc, 0.0)
        return (r * r,)

    act = _mm(x_m, w1, name=f"{tag}_up", out_dtypes=(MXU_DTYPE,), epilogue=epi, after=after)
    y = _mm(act, w2, name=f"{tag}_down")
    return y, (x_m, act)


def _mlp_bwd(dy_m, saved, w1, w2, tag, after=()):
    x_m, act = saved
    dw2 = _mm(act, dy_m, ta=True, name=f"{tag}_dw2", out_dtypes=(MXU_DTYPE,))

    def epi(acc, av):
        return (acc * (2.0 * jnp.sqrt(av.astype(F32))),)

    dh = _mm(dy_m, w2, tb=True, name=f"{tag}_dact", out_dtypes=(MXU_DTYPE,), epilogue=epi, tile_extras=(act,),
             after=after)
    dw1 = _mm(x_m, dh, ta=True, out_dev=True, name=f"{tag}_dw1", out_dtypes=(MXU_DTYPE,))
    dx = _mm(dh, w1, tb=True, name=f"{tag}_dx")
    return [dx], {"ff_w1": dw1, "ff_w2": dw2}


def _sc_gate_specs(t, d):
    per = d // LANES
    return [pl.BlockSpec((t, LANES), lambda j, k=k: (0, k * per + j)) for k in range(3)]


def _sc_gate_fwd(u, conv_w, *, name):
    t, d = u.shape[0], u.shape[1] // 3
    kw = conv_w.shape[0]
    ch = _conv_chunk(t)

    def body(b_ref, c_ref, h_ref, w_ref, r_ref, pad_ref):
        pad_ref[0:CONV_PAD, :] = jnp.zeros((CONV_PAD, LANES), F32)
        pad_ref[CONV_PAD:CONV_PAD + t, :] = c_ref[...] * h_ref[...]
        for t0 in range(0, t, ch):
            q = jnp.zeros((ch, LANES), F32)
            for k in range(kw):
                s = kw - 1 - k
                q = q + w_ref[k:k + 1, :] * pad_ref[CONV_PAD - s + t0:CONV_PAD - s + t0 + ch, :]
            r_ref[t0:t0 + ch, :] = (b_ref[t0:t0 + ch, :] * q).astype(r_ref.dtype)

    blk = pl.BlockSpec((t, LANES), lambda j: (0, j))
    return pl.pallas_call(
        body,
        name=name,
        grid=(d // LANES,),
        in_specs=_sc_gate_specs(t, d) + [pl.BlockSpec((kw, LANES), lambda j: (0, j))],
        out_specs=blk,
        out_shape=jax.ShapeDtypeStruct((t, d), MXU_DTYPE),
        scratch_shapes=[pltpu.VMEM((CONV_PAD + t, LANES), F32)],
        compiler_params=_params(("parallel",)),
    )(u, u, u, conv_w)


def _sc_gate_bwd(dr, u, conv_w, *, name):
    t, d = u.shape[0], u.shape[1] // 3
    kw = conv_w.shape[0]
    ch = _conv_chunk(t)

    def body(dr_ref, b_ref, c_ref, h_ref, w_ref, db_ref, dp_ref, dw_ref, ppad_ref, dpad_ref):
        ppad_ref[0:CONV_PAD, :] = jnp.zeros((CONV_PAD, LANES), F32)
        ppad_ref[CONV_PAD:CONV_PAD + t, :] = c_ref[...] * h_ref[...]
        dpad_ref[0:t, :] = dr_ref[...] * b_ref[...]
        dpad_ref[t:t + CONV_PAD, :] = jnp.zeros((CONV_PAD, LANES), F32)
        for t0 in range(0, t, ch):
            q = jnp.zeros((ch, LANES), F32)
            dp = jnp.zeros((ch, LANES), F32)
            for k in range(kw):
                s = kw - 1 - k
                q = q + w_ref[k:k + 1, :] * ppad_ref[CONV_PAD - s + t0:CONV_PAD - s + t0 + ch, :]
                dp = dp + w_ref[k:k + 1, :] * dpad_ref[t0 + s:t0 + s + ch, :]
            db_ref[t0:t0 + ch, :] = (dr_ref[t0:t0 + ch, :] * q).astype(db_ref.dtype)
            dp_ref[t0:t0 + ch, :] = dp
        for k in range(kw):
            s = kw - 1 - k
            acc = jnp.zeros((ch, LANES), F32)
            for t0 in range(0, t, ch):
                acc = acc + dpad_ref[t0:t0 + ch, :] * ppad_ref[CONV_PAD - s + t0:CONV_PAD - s + t0 + ch, :]
            dw_ref[k:k + 1, :] = _colsum(acc)

    blk = pl.BlockSpec((t, LANES), lambda j: (0, j))
    wblk = pl.BlockSpec((kw, LANES), lambda j: (0, j))
    return pl.pallas_call(
        body,
        name=name,
        grid=(d // LANES,),
        in_specs=[blk] + _sc_gate_specs(t, d) + [wblk],
        out_specs=[blk, blk, wblk],
        out_shape=[jax.ShapeDtypeStruct((t, d), MXU_DTYPE), jax.ShapeDtypeStruct((t, d), F32),
                   jax.ShapeDtypeStruct((kw, d), F32)],
        scratch_shapes=[pltpu.VMEM((CONV_PAD + t, LANES), F32), pltpu.VMEM((CONV_PAD + t, LANES), F32)],
        compiler_params=_params(("parallel",)),
    )(dr, u, u, u, conv_w)


def _sc_fwd(x_m, w_in, conv_w, w_out, tag, after=()):
    u = _mm(x_m, w_in, name=f"{tag}_in", after=after)
    r = _sc_gate_fwd(u, conv_w, name=f"{tag}_gate")
    y = _mm(r, w_out, name=f"{tag}_out")
    return y, (x_m, u, r)


def _sc_bwd(dy_m, saved, w_in, conv_w, w_out, tag, after=()):
    d = D_MODEL
    x_m, u, r = saved
    dw_out = _mm(r, dy_m, ta=True, name=f"{tag}_dwout", out_dtypes=(MXU_DTYPE,))
    dr = _mm(dy_m, w_out, tb=True, name=f"{tag}_dr", after=after)
    db, dp, dconv = _sc_gate_bwd(dr, u, conv_w, name=f"{tag}_dgate")

    def du_fn(rows, bc):
        dbv, dpv, uv = rows
        return (jnp.concatenate([dbv.astype(F32), dpv * uv[:, 2 * d:3 * d], dpv * uv[:, d:2 * d]], axis=1),), ()

    du = _rowwise(du_fn, [db, dp, u], [], [(3 * d, MXU_DTYPE)], [], name=f"{tag}_du")[0]
    dw_in = _mm(x_m, du, ta=True, out_dev=True, name=f"{tag}_dwin", out_dtypes=(MXU_DTYPE,))
    dx = _mm(du, w_in, tb=True, name=f"{tag}_dx")
    return [dx], {"sc_w_in": dw_in, "sc_conv_w": dconv, "sc_w_out": dw_out}


def _cf_fwd(x_m, w, tag, after=()):
    d = D_MODEL

    def bias_epi(acc, bias):
        return (acc + bias,)

    u = _mm(x_m, w["cf_w_pw1"], name=f"{tag}_pw1", epilogue=bias_epi, col_extras=(w["cf_b_pw1"],),
            after=after)

    def glu_fn(rows, bc):
        uv = rows[0]
        return (uv[:, 0:d] * _sigmoid(uv[:, d:2 * d]),), ()

    h1 = _rowwise(glu_fn, [u], [], [(d, F32)], [], name=f"{tag}_glu")[0]
    h2 = _conv_fwd(h1, w["cf_dw_w"], w["cf_dw_b"], name=f"{tag}_conv")

    def norm_fn(rows, bc):
        hv = rows[0]
        mu = jnp.mean(hv, axis=-1, keepdims=True)
        hc = hv - mu
        rstd = lax.rsqrt(jnp.mean(hc * hc, axis=-1, keepdims=True) + LN_EPS)
        h3 = hc * rstd * bc[0] + bc[1]
        return (h3 * _sigmoid(h3),), ()

    h4 = _rowwise(norm_fn, [h2], [w["cf_norm_g"], w["cf_norm_b"]], [(d, MXU_DTYPE)], [], name=f"{tag}_norm")[0]
    y = _mm(h4, w["cf_w_pw2"], name=f"{tag}_pw2", epilogue=bias_epi, col_extras=(w["cf_b_pw2"],))
    return y, (x_m, u, h1, h2, h4)


def _cf_bwd(dy_m, dy_colsum, saved, w, tag, after=()):
    d = D_MODEL
    x_m, u, h1, h2, h4 = saved
    dw_pw2 = _mm(h4, dy_m, ta=True, name=f"{tag}_dwpw2", out_dtypes=(MXU_DTYPE,))
    dh4 = _mm(dy_m, w["cf_w_pw2"], tb=True, name=f"{tag}_dh4", after=after)

    def dnorm_fn(rows, bc):
        dh4v, hv = rows
        g, b = bc
        mu = jnp.mean(hv, axis=-1, keepdims=True)
        hc = hv - mu
        rstd = lax.rsqrt(jnp.mean(hc * hc, axis=-1, keepdims=True) + LN_EPS)
        xh = hc * rstd
        h3 = xh * g + b
        sg = _sigmoid(h3)
        dh3 = dh4v * (sg * (1.0 + h3 * (1.0 - sg)))
        dxh = dh3 * g
        dh2 = rstd * (dxh - jnp.mean(dxh, axis=-1, keepdims=True) - xh * jnp.mean(dxh * xh, axis=-1, keepdims=True))
        return (dh2,), (_colsum(dh3 * xh), _colsum(dh3), _colsum(dh2))

    dh2, dnorm_g, dnorm_b, ddw_b = _rowwise(dnorm_fn, [dh4, h2], [w["cf_norm_g"], w["cf_norm_b"]], [(d, F32)],
                                            [(1, d), (1, d), (1, d)], name=f"{tag}_dnorm")
    dh1, ddw_w = _conv_bwd(dh2, h1, w["cf_dw_w"], name=f"{tag}_dconv")

    def dglu_fn(rows, bc):
        dh1v, uv = rows
        a, gate = uv[:, 0:d], uv[:, d:2 * d]
        sg = _sigmoid(gate)
        du = jnp.concatenate([dh1v * sg, dh1v * a * sg * (1.0 - sg)], axis=1)
        return (du,), (_colsum(du),)

    du, db_pw1 = _rowwise(dglu_fn, [dh1, u], [], [(2 * d, MXU_DTYPE)], [(1, 2 * d)], name=f"{tag}_dglu")
    dw_pw1 = _mm(x_m, du, ta=True, out_dev=True, name=f"{tag}_dwpw1", out_dtypes=(MXU_DTYPE,))
    dx = _mm(du, w["cf_w_pw1"], tb=True, name=f"{tag}_dx")
    grads = {"cf_w_pw1": dw_pw1, "cf_b_pw1": db_pw1, "cf_dw_w": ddw_w, "cf_dw_b": ddw_b, "cf_norm_g": dnorm_g,
             "cf_norm_b": dnorm_b, "cf_w_pw2": dw_pw2, "cf_b_pw2": dy_colsum}
    return [dx], grads


def _rope_tables(t):
    pos = jnp.arange(t, dtype=F32)
    inv_freq = ROPE_THETA ** (-jnp.arange(0, QK_ROPE, 2, dtype=F32) / QK_ROPE)
    ang = pos[:, None] * inv_freq[None, :]
    return jnp.cos(ang), jnp.sin(ang)


ROPE_HALF = QK_ROPE // 2
Q_NOPE_COLS = MLA_HEADS * QK_NOPE
Q_HALF_COLS = MLA_HEADS * ROPE_HALF


def _interleave_heads(w_uk, w_uv):
    c = w_uk.shape[0]
    return jnp.concatenate([w_uk.reshape(c, MLA_HEADS, QK_NOPE), w_uv.reshape(c, MLA_HEADS, V_HEAD)],
                           axis=2).reshape(c, -1)


def _mla_fwd(x_m, w, tag, after=()):
    t = x_m.shape[0]
    cos, sin = _rope_tables(t)
    cos_h, sin_h = jnp.tile(cos, (1, MLA_HEADS)), jnp.tile(sin, (1, MLA_HEADS))
    cq_pre = _mm(x_m, w["mla_w_dq"], name=f"{tag}_dq", after=after)
    cq = _rms_fwd(cq_pre, w["mla_g_q"], name=f"{tag}_qnorm")
    q = _mm(cq, w["mla_w_uq"], name=f"{tag}_uq")
    r1, r2 = _rope(q[:, Q_NOPE_COLS:Q_NOPE_COLS + Q_HALF_COLS], q[:, Q_NOPE_COLS + Q_HALF_COLS:], cos_h, sin_h,
                   name=f"{tag}_qrope")
    qp_h = jnp.concatenate([r1.reshape(t, MLA_HEADS, ROPE_HALF), r2.reshape(t, MLA_HEADS, ROPE_HALF)],
                           axis=-1).transpose(1, 0, 2).astype(MXU_DTYPE)
    ckv_full = _mm(x_m, w["mla_w_dkv"], name=f"{tag}_dkv")
    ckv_pre = ckv_full[:, :KV_LORA]
    ckv = _rms_fwd(ckv_pre, w["mla_g_kv"], name=f"{tag}_kvnorm")
    k1, k2 = _rope(ckv_full[:, KV_LORA:KV_LORA + ROPE_HALF], ckv_full[:, KV_LORA + ROPE_HALF:], cos, sin,
                   name=f"{tag}_krope")
    kp = jnp.concatenate([k1, k2], axis=-1).astype(MXU_DTYPE)
    kv = _mm(ckv, _interleave_heads(w["mla_w_uk"], w["mla_w_uv"]), name=f"{tag}_ukv", out_dtypes=(MXU_DTYPE,))
    o, o_f, lse = _attn_fwd(q, qp_h, kv, kp, name=f"{tag}_attn")
    y = _mm(o, w["mla_w_o"], name=f"{tag}_o")
    return y, (x_m, cq_pre, cq, ckv_pre, ckv, q, qp_h, kv, kp, lse, o, o_f, (cos, sin, cos_h, sin_h))


def _mla_bwd(dy_m, saved, w, tag, after=()):
    x_m, cq_pre, cq, ckv_pre, ckv, q, qp_h, kv, kp, lse, o, o_f, (cos, sin, cos_h, sin_h) = saved
    t = x_m.shape[0]
    half = ROPE_HALF
    dw_o = _mm(o, dy_m, ta=True, name=f"{tag}_dwo", out_dtypes=(MXU_DTYPE,))
    do = _mm(dy_m, w["mla_w_o"], tb=True, name=f"{tag}_do", out_dtypes=(MXU_DTYPE,), after=after)
    dqn, dqp_h, dkv, dkp = _attn_bwd(q, qp_h, kv, kp, do, o_f, lse, name=f"{tag}_dattn")
    dqp = dqp_h.transpose(1, 0, 2)
    d1, d2 = _rope(dqp[:, :, :half].reshape(t, -1), dqp[:, :, half:].reshape(t, -1), cos_h, sin_h,
                   name=f"{tag}_dqrope", inverse=True)
    dq = jnp.concatenate([dqn, d1, d2], axis=-1).astype(MXU_DTYPE)
    dw_uq = _mm(cq, dq, ta=True, name=f"{tag}_dwuq", out_dtypes=(MXU_DTYPE,))
    dcq = _mm(dq, w["mla_w_uq"], tb=True, name=f"{tag}_dcq")
    dcq_pre, dg_q = _rms_bwd(cq_pre, w["mla_g_q"], dcq, name=f"{tag}_dqnorm")
    dcq_pre_m = dcq_pre.astype(MXU_DTYPE)
    dw_dq = _mm(x_m, dcq_pre_m, ta=True, name=f"{tag}_dwdq", out_dtypes=(MXU_DTYPE,))
    dx_q = _mm(dcq_pre_m, w["mla_w_dq"], tb=True, name=f"{tag}_dxq")
    dw_kv = _mm(ckv, dkv, ta=True, name=f"{tag}_dwukv", out_dtypes=(MXU_DTYPE,))
    dw_kv = dw_kv.reshape(KV_LORA, MLA_HEADS, QK_NOPE + V_HEAD)
    dw_uk, dw_uv = dw_kv[:, :, :QK_NOPE].reshape(KV_LORA, -1), dw_kv[:, :, QK_NOPE:].reshape(KV_LORA, -1)
    dckv = _mm(dkv, _interleave_heads(w["mla_w_uk"], w["mla_w_uv"]), tb=True, name=f"{tag}_dckv")
    dckv_pre, dg_kv = _rms_bwd(ckv_pre, w["mla_g_kv"], dckv, name=f"{tag}_dkvnorm")
    dk1, dk2 = _rope(dkp[:, :half], dkp[:, half:], cos, sin, name=f"{tag}_dkrope", inverse=True)
    dckv_full = jnp.concatenate([dckv_pre, dk1, dk2], axis=-1).astype(MXU_DTYPE)
    dw_dkv = _mm(x_m, dckv_full, ta=True, name=f"{tag}_dwdkv", out_dtypes=(MXU_DTYPE,))
    dx_kv = _mm(dckv_full, w["mla_w_dkv"], tb=True, name=f"{tag}_dxkv")
    grads = {"mla_w_dq": dw_dq, "mla_g_q": dg_q, "mla_w_uq": dw_uq, "mla_w_dkv": dw_dkv, "mla_g_kv": dg_kv,
             "mla_w_uk": dw_uk, "mla_w_uv": dw_uv, "mla_w_o": dw_o}
    return [dx_q, dx_kv], grads


def _cast_input(x):
    cast = lambda rows, bc: ((rows[0],), ())
    return _rowwise(cast, [x], [], [(D_MODEL, MXU_DTYPE)], [], name="x_cast")[0]


def _hooked(value, hook):
    token = None if hook is None else hook(value)
    return () if token is None else (token,)


def _mix_fwd(i, cur, cur_m, wl, hook=None, after=()):
    mixer = i % N_MIXERS
    if mixer == 0:
        y, s_mix = _sc_fwd(cur_m, wl["sc_w_in"], wl["sc_conv_w"], wl["sc_w_out"], f"l{i}_sc", after=after)
    elif mixer == 1:
        y, s_mix = _mla_fwd(cur_m, wl, f"l{i}_mla", after=after)
    else:
        y, s_mix = _cf_fwd(cur_m, wl, f"l{i}_cf", after=after)
    cur, cur_m, xh_mix, rs_mix = _ln_fwd(cur, y, wl["ln_mix_g"], wl["ln_mix_b"], name=f"l{i}_lnmix",
                                         after=_hooked(y, hook))
    return cur, cur_m, (s_mix, xh_mix, rs_mix)


def _ff_fwd(i, cur, cur_m, wl, hook=None, after=()):
    y, s_ff = _mlp_fwd(cur_m, wl["ff_w1"], wl["ff_w2"], f"l{i}_ff", after=after)
    cur, cur_m, xh_ff, rs_ff = _ln_fwd(cur, y, wl["ln_ff_g"], wl["ln_ff_b"], name=f"l{i}_lnff", after=_hooked(y, hook))
    return cur, cur_m, (s_ff, xh_ff, rs_ff)


def _layer_fwd(i, cur, cur_m, wl):
    cur, cur_m, s_mix = _mix_fwd(i, cur, cur_m, wl)
    cur, cur_m, s_ff = _ff_fwd(i, cur, cur_m, wl)
    return cur, cur_m, (*s_mix, *s_ff)


def _loss_and_grad(out, target):
    d = D_MODEL

    def loss_fn(rows, bc):
        err = rows[0] - rows[1]
        return (err * (1.0 / d),), (_colsum(err * err),)

    dy, sq = _rowwise(loss_fn, [out, target], [], [(d, F32)], [(1, d)], name="loss")
    return (0.5 / d) * jnp.sum(sq), dy


def _ff_bwd(i, terms, saved, wl, hook=None, after=()):
    s_ff, xh_ff, rs_ff = saved
    grads = {}
    dz, dz_m, grads["ln_ff_g"], grads["ln_ff_b"], _ = _ln_bwd(terms, xh_ff, rs_ff, wl["ln_ff_g"], name=f"l{i}_dlnff",
                                                              after=after)
    dxs, g_ff = _mlp_bwd(dz_m, s_ff, wl["ff_w1"], wl["ff_w2"], f"l{i}_ff", after=_hooked(dz, hook))
    grads.update(g_ff)
    return [(ALPHA, dz)] + [(1.0, v) for v in dxs], grads


def _mix_bwd(i, terms, saved, wl, hook=None, after=()):
    mixer = i % N_MIXERS
    s_mix, xh_mix, rs_mix = saved
    grads = {}
    dz, dz_m, grads["ln_mix_g"], grads["ln_mix_b"], dz_sum = _ln_bwd(terms, xh_mix, rs_mix, wl["ln_mix_g"],
                                                                     name=f"l{i}_dlnmix", after=after)
    behind = _hooked(dz, hook)
    if mixer == 0:
        dxs, g_mix = _sc_bwd(dz_m, s_mix, wl["sc_w_in"], wl["sc_conv_w"], wl["sc_w_out"], f"l{i}_sc", after=behind)
    elif mixer == 1:
        dxs, g_mix = _mla_bwd(dz_m, s_mix, wl, f"l{i}_mla", after=behind)
    else:
        dxs, g_mix = _cf_bwd(dz_m, dz_sum, s_mix, wl, f"l{i}_cf", after=behind)
    grads.update(g_mix)
    return [(ALPHA, dz)] + [(1.0, v) for v in dxs], grads


def _layer_bwd(i, terms, saved, wl):
    terms, g_ff = _ff_bwd(i, terms, saved[3:], wl)
    terms, g_mix = _mix_bwd(i, terms, saved[:3], wl)
    return terms, {**g_ff, **g_mix}


def _sum_terms(terms, *, name):
    coefs = [c for c, _ in terms]

    def sum_fn(rows, bc):
        acc = None
        for c, v in zip(coefs, rows):
            tv = v if c == 1.0 else c * v
            acc = tv if acc is None else acc + tv
        return (acc,), ()

    return _rowwise(sum_fn, [v for _, v in terms], [], [(D_MODEL, F32)], [], name=name)[0]


def _local_step(x, target, layers):
    cur, cur_m = x, _cast_input(x)
    saved = []
    for i in range(DEPTH):
        cur, cur_m, s = _layer_fwd(i, cur, cur_m, layers[i])
        saved.append(s)
    loss, dy = _loss_and_grad(cur, target)
    terms = [(1.0, dy)]
    grads = [None] * DEPTH
    for i in reversed(range(DEPTH)):
        terms, grads[i] = _layer_bwd(i, terms, saved[i], layers[i])
    return loss, _sum_terms(terms, name="grad_x"), grads


MESH_ID = pl.DeviceIdType.MESH
ANY = pl.BlockSpec(memory_space=pl.ANY)


def _block_id(px, py, pc):
    return 4 * px + 2 * py + pc


COPIES_PER_GATHER = 7


def _all_gather(shards, *, name):
    n = len(shards)

    def body(*refs):
        x_refs, out_refs = refs[:n], refs[n:2 * n]
        send_sems, recv_sems, local_sems = refs[2 * n:]
        x, y, core = lax.axis_index("x"), lax.axis_index("y"), lax.axis_index("c")
        me, sibling = (x, y, core), (x, y, 1 - core)
        chips = [(1 - x, y), (x, 1 - y), (1 - x, 1 - y)]

        def copy(t, k, block, to, from_input=False):
            dst = out_refs[t].at[_block_id(*block)]
            return pltpu.make_async_remote_copy(
                src_ref=x_refs[t] if from_input else dst, dst_ref=dst,
                send_sem=send_sems.at[t * COPIES_PER_GATHER + k], recv_sem=recv_sems.at[t * COPIES_PER_GATHER + k],
                device_id=to, device_id_type=MESH_ID)

        started = []
        mine = [pltpu.make_async_copy(x_refs[t], out_refs[t].at[_block_id(*me)], local_sems.at[t]) for t in range(n)]
        for t in range(n):
            mine[t].start()
            first = [copy(t, 0, me, sibling, from_input=True)]
            first += [copy(t, 1 + j, me, (*chip, core), from_input=True) for j, chip in enumerate(chips)]
            for cp in first:
                cp.start()
            started += first
        for t in range(n):
            for j, chip in enumerate(chips):
                copy(t, 1 + j, (*chip, core), me).wait_recv()
                passed = copy(t, 4 + j, (*chip, core), sibling)
                passed.start()
                started.append(passed)
        for t in range(n):
            copy(t, 0, sibling, me).wait_recv()
            for j, chip in enumerate(chips):
                copy(t, 4 + j, (*chip, 1 - core), me).wait_recv()
        for cp in started:
            cp.wait_send()
        for cp in mine:
            cp.wait()

    n_sems = n * COPIES_PER_GATHER
    outs = pl.pallas_call(
        body,
        name=name,
        out_shape=[jax.ShapeDtypeStruct((N_DEV, *s.shape), s.dtype) for s in shards],
        in_specs=[ANY] * n,
        out_specs=[ANY] * n,
        scratch_shapes=[pltpu.SemaphoreType.DMA((n_sems,)), pltpu.SemaphoreType.DMA((n_sems,)),
                        pltpu.SemaphoreType.DMA((n,))],
    )(*shards)
    return list(outs)


def _swap_with_sibling(parts, *, name):
    n = len(parts)

    def body(*refs):
        p_refs, got_refs = refs[:n], refs[n:2 * n]
        send_sems, recv_sems = refs[2 * n:]
        x, y, core = lax.axis_index("x"), lax.axis_index("y"), lax.axis_index("c")
        chips = [(x, y), (1 - x, y), (x, 1 - y), (1 - x, 1 - y)]
        copies = []
        for t in range(n):
            for k, (px, py) in enumerate(chips):
                copies.append(pltpu.make_async_remote_copy(
                    src_ref=p_refs[t].at[_block_id(px, py, 1 - core)], dst_ref=got_refs[t].at[k],
                    send_sem=send_sems.at[4 * t + k], recv_sem=recv_sems.at[4 * t + k], device_id=(x, y, 1 - core),
                    device_id_type=MESH_ID))
        for cp in copies:
            cp.start()
        for cp in copies:
            cp.wait()

    outs = pl.pallas_call(
        body,
        name=name,
        out_shape=[jax.ShapeDtypeStruct((4, *p.shape[1:]), p.dtype) for p in parts],
        in_specs=[ANY] * n,
        out_specs=[ANY] * n,
        scratch_shapes=[pltpu.SemaphoreType.DMA((4 * n,)), pltpu.SemaphoreType.DMA((4 * n,))],
    )(*parts)
    return list(outs)


def _swap_between_chips(parts, *, name):
    n = len(parts)

    def body(*refs):
        p_refs, got_refs = refs[:n], refs[n:2 * n]
        send_sems, recv_sems = refs[2 * n:]
        x, y, core = lax.axis_index("x"), lax.axis_index("y"), lax.axis_index("c")
        chips = [(1 - x, y), (x, 1 - y), (1 - x, 1 - y)]
        copies = []
        for t in range(n):
            for j, (px, py) in enumerate(chips):
                copies.append(pltpu.make_async_remote_copy(
                    src_ref=p_refs[t].at[j], dst_ref=got_refs[t].at[j], send_sem=send_sems.at[3 * t + j],
                    recv_sem=recv_sems.at[3 * t + j], device_id=(px, py, core), device_id_type=MESH_ID))
        for cp in copies:
            cp.start()
        for cp in copies:
            cp.wait()

    outs = pl.pallas_call(
        body,
        name=name,
        out_shape=[jax.ShapeDtypeStruct(p.shape, p.dtype) for p in parts],
        in_specs=[ANY] * n,
        out_specs=[ANY] * n,
        scratch_shapes=[pltpu.SemaphoreType.DMA((3 * n,)), pltpu.SemaphoreType.DMA((3 * n,))],
    )(*parts)
    return list(outs)


def _chip_partials(parts, got, *, name):
    _, r, c = parts.shape
    tb = _row_tile(r, 512)
    x, y, core = lax.axis_index("x"), lax.axis_index("y"), lax.axis_index("c")
    chips = [(x, y), (1 - x, y), (x, 1 - y), (1 - x, 1 - y)]
    ids = jnp.stack([_block_id(px, py, core) for px, py in chips]).astype(jnp.int32)

    def body(ids_ref, p0, p1, p2, p3, g_ref, own_ref, out_ref):
        own_ref[...] = p0[...].astype(F32) + g_ref[0].astype(F32)
        for j, p in enumerate((p1, p2, p3)):
            out_ref[j] = (p[...].astype(F32) + g_ref[j + 1].astype(F32)).astype(out_ref.dtype)

    def part_spec(k):
        return pl.BlockSpec((None, tb, c), lambda i, ids_ref: (ids_ref[k], i, 0))

    grid_spec = pltpu.PrefetchScalarGridSpec(
        num_scalar_prefetch=1,
        grid=(r // tb,),
        in_specs=[part_spec(k) for k in range(4)] + [pl.BlockSpec((4, tb, c), lambda i, ids_ref: (0, i, 0))],
        out_specs=[pl.BlockSpec((tb, c), lambda i, ids_ref: (i, 0)), pl.BlockSpec((3, tb, c), lambda i, ids_ref: (0, i, 0))],
    )
    return pl.pallas_call(
        body,
        name=name,
        grid_spec=grid_spec,
        out_shape=[jax.ShapeDtypeStruct((r, c), F32), jax.ShapeDtypeStruct((3, r, c), parts.dtype)],
        compiler_params=_params(("parallel",)),
    )(ids, parts, parts, parts, parts, got)


def _chip_partials_group(parts, got, *, name):
    n = len(parts)
    rows = [p.shape[1] for p in parts]
    steps = 4 if all(r % (4 * BF16_ROWS) == 0 for r in rows) else (2 if all(r % (2 * BF16_ROWS) == 0 for r in rows) else 1)
    x, y, core = lax.axis_index("x"), lax.axis_index("y"), lax.axis_index("c")
    chips = [(x, y), (1 - x, y), (x, 1 - y), (1 - x, 1 - y)]
    ids = jnp.stack([_block_id(px, py, core) for px, py in chips]).astype(jnp.int32)

    def body(ids_ref, *refs):
        ins, outs = refs[:5 * n], refs[5 * n:]
        for t in range(n):
            p0, p1, p2, p3, g_ref = ins[5 * t:5 * t + 5]
            own_ref, out_ref = outs[2 * t], outs[2 * t + 1]
            own_ref[...] = p0[...].astype(F32) + g_ref[0].astype(F32)
            for j, p in enumerate((p1, p2, p3)):
                out_ref[j] = (p[...].astype(F32) + g_ref[j + 1].astype(F32)).astype(out_ref.dtype)

    in_specs, out_specs, out_shape, operands = [], [], [], []
    for p, g in zip(parts, got):
        _, r, c = p.shape
        tb = r // steps
        in_specs += [pl.BlockSpec((None, tb, c), lambda i, ids_ref, k=k: (ids_ref[k], i, 0)) for k in range(4)]
        in_specs.append(pl.BlockSpec((4, tb, c), lambda i, ids_ref: (0, i, 0)))
        out_specs += [pl.BlockSpec((tb, c), lambda i, ids_ref: (i, 0)), pl.BlockSpec((3, tb, c), lambda i, ids_ref: (0, i, 0))]
        out_shape += [jax.ShapeDtypeStruct((r, c), F32), jax.ShapeDtypeStruct((3, r, c), p.dtype)]
        operands += [p, p, p, p, g]
    outs = pl.pallas_call(
        body,
        name=name,
        grid_spec=pltpu.PrefetchScalarGridSpec(num_scalar_prefetch=1, grid=(steps,), in_specs=in_specs,
                                               out_specs=out_specs),
        out_shape=out_shape,
        compiler_params=_params(("parallel",)),
    )(ids, *operands)
    return [(outs[2 * t], outs[2 * t + 1]) for t in range(n)]


def _sum_blocks(first, blocks, *, name):
    n, r, c = blocks.shape
    tb = _row_tile(r, 512)

    def body(*refs):
        b_ref, o_ref = refs[-2], refs[-1]
        acc = refs[0][...] if first is not None else b_ref[0].astype(F32)
        for j in range(0 if first is not None else 1, n):
            acc = acc + b_ref[j].astype(F32)
        o_ref[...] = acc

    row_spec = pl.BlockSpec((tb, c), lambda i: (i, 0))
    return pl.pallas_call(
        body,
        name=name,
        grid=(r // tb,),
        in_specs=([row_spec] if first is not None else []) + [pl.BlockSpec((n, tb, c), lambda i: (0, i, 0))],
        out_specs=row_spec,
        out_shape=jax.ShapeDtypeStruct((r, c), F32),
        compiler_params=_params(("parallel",)),
    )(*([first] if first is not None else []), blocks)


def _reduce_scatter(parts, *, tag):
    got = _swap_with_sibling(parts, name=f"{tag}_sibling")
    sums = [_chip_partials(p, g, name=f"{tag}_chipsum{t}") for t, (p, g) in enumerate(zip(parts, got))]
    arrived = _swap_between_chips([others for _, others in sums], name=f"{tag}_chips")
    return [(own, arr) for (own, _), arr in zip(sums, arrived)]


HBM = pl.BlockSpec(memory_space=pltpu.HBM)
SEM = pl.BlockSpec(memory_space=pltpu.SEMAPHORE)
DATAFLOW = pltpu.SideEffectType.DATAFLOW_SIDE_EFFECTING


def _tie(value, *before):
    return lax.optimization_barrier((value, *before))[0]


def _split_call(name, arrays, sems, n_new, body, after=None):
    n_a, n_s = len(arrays), len(sems)
    n_after = 0 if after is None else 1

    def kernel_body(*refs):
        new = refs[n_a + n_s + n_after:n_a + n_s + n_after + 2] if n_new else (None, None)
        body(refs[:n_a], refs[n_a:n_a + n_s], *new)
        refs[-1][...] = jnp.zeros_like(refs[-1])

    n_sem_out = 2 if n_new else 0
    out_shape = [pltpu.SemaphoreType.DMA((n_new,))] * n_sem_out
    out_shape += [pltpu.HBM(a.shape, a.dtype) for a in arrays]
    out_shape += [jax.ShapeDtypeStruct((SUBLANES, LANES), F32)]
    outs = pl.pallas_call(
        kernel_body,
        name=name,
        out_shape=out_shape,
        in_specs=[HBM] * n_a + [SEM] * n_s + [ANY] * n_after,
        out_specs=[SEM] * n_sem_out + [HBM] * n_a + [pl.BlockSpec(memory_space=pltpu.VMEM)],
        input_output_aliases={i: n_sem_out + i for i in range(n_a)},
        compiler_params=pltpu.CompilerParams(has_side_effects=DATAFLOW),
    )(*[pltpu.with_memory_space_constraint(a, pltpu.HBM) for a in arrays], *sems, *([after] if n_after else []))
    return tuple(outs[:n_sem_out]), list(outs[n_sem_out:n_sem_out + n_a]), outs[-1]


def _remote(src, dst, send, recv, k, to):
    return pltpu.make_async_remote_copy(src_ref=src, dst_ref=dst, send_sem=send.at[k], recv_sem=recv.at[k],
                                        device_id=to, device_id_type=MESH_ID)


def _land_block(land_ref, block, by_cols, width):
    if by_cols:
        return land_ref.at[:, pl.ds(pl.multiple_of(block * width, LANES), width)]
    return land_ref.at[block]


def _place_in_window(shard, *, name):
    r, c = shard.shape
    tb = _row_tile(r, 512)
    me = _block_id(lax.axis_index("x"), lax.axis_index("y"), lax.axis_index("c")).astype(jnp.int32).reshape(1)

    def body(me_ref, s_ref, o_ref):
        o_ref[...] = s_ref[...].astype(o_ref.dtype)

    return pl.pallas_call(
        body,
        name=name,
        grid_spec=pltpu.PrefetchScalarGridSpec(
            num_scalar_prefetch=1, grid=(r // tb,),
            in_specs=[pl.BlockSpec((tb, c), lambda i, me_ref: (i, 0))],
            out_specs=pl.BlockSpec((tb, c), lambda i, me_ref: (i, me_ref[0]))),
        out_shape=jax.ShapeDtypeStruct((r, N_DEV * c), MXU_DTYPE),
        compiler_params=_params(("parallel",)),
    )(me, shard)


def _gather_start(shards, *, name, after=None, by_cols=None):
    n = len(shards)
    by_cols = list(by_cols) if by_cols is not None else [False] * n
    me = _block_id(lax.axis_index("x"), lax.axis_index("y"), lax.axis_index("c"))
    lands = []
    for i, (s, cols) in enumerate(zip(shards, by_cols)):
        if cols:
            lands.append(_place_in_window(s, name=f"{name}_place{i}"))
        else:
            lands.append(lax.dynamic_update_index_in_dim(lax.empty((N_DEV, *s.shape), s.dtype), s, me, 0))
    widths = [s.shape[-1] for s in shards]

    def body(refs, _, send, recv):
        x, y, core = lax.axis_index("x"), lax.axis_index("y"), lax.axis_index("c")
        targets = [(x, y, 1 - core), (1 - x, y, core), (x, 1 - y, core), (1 - x, 1 - y, core)]
        for t in range(n):
            mine = _land_block(refs[t], _block_id(x, y, core), by_cols[t], widths[t])
            for k, to in enumerate(targets):
                _remote(mine, mine, send, recv, 4 * t + k, to).start()

    sems, thru, token = _split_call(name, lands, [], 4 * n, body, after=after)
    return {"n": n, "sems1": sems, "arrays": thru, "token": token, "by_cols": by_cols, "widths": widths}


def _gather_pass_on(st, *, name, after):
    n = st["n"]

    def body(refs, sems, send, recv):
        send1, recv1 = sems
        x, y, core = lax.axis_index("x"), lax.axis_index("y"), lax.axis_index("c")
        chips = [(1 - x, y), (x, 1 - y), (1 - x, 1 - y)]
        for t in range(n):
            for j, (px, py) in enumerate(chips):
                block = _land_block(refs[t], _block_id(px, py, core), st["by_cols"][t], st["widths"][t])
                _remote(block, block, send1, recv1, 4 * t + 1 + j, (x, y, 1 - core)).wait_recv()
                _remote(block, block, send, recv, 3 * t + j, (x, y, 1 - core)).start()

    sems2, thru, token = _split_call(name, st["arrays"], list(st["sems1"]), 3 * n, body, after=after)
    return {**st, "sems2": sems2, "arrays": thru, "token": token}


def _gather_finish(st, *, name, after):
    n = st["n"]

    def body(refs, sems, *_):
        send1, recv1, send2, recv2 = sems
        x, y, core = lax.axis_index("x"), lax.axis_index("y"), lax.axis_index("c")
        sibling = (x, y, 1 - core)
        chips = [(1 - x, y), (x, 1 - y), (1 - x, 1 - y)]
        for t in range(n):
            place = lambda px, py, pc, t=t: _land_block(refs[t], _block_id(px, py, pc), st["by_cols"][t],
                                                        st["widths"][t])
            mine = place(x, y, core)
            for k in range(4):
                _remote(mine, mine, send1, recv1, 4 * t + k, sibling).wait_send()
            _remote(mine, place(x, y, 1 - core), send1, recv1, 4 * t, sibling).wait_recv()
            for j, (px, py) in enumerate(chips):
                _remote(place(px, py, core), place(px, py, core), send2, recv2, 3 * t + j, sibling).wait_send()
                _remote(mine, place(px, py, 1 - core), send2, recv2, 3 * t + j, sibling).wait_recv()

    _, thru, _ = _split_call(name, st["arrays"], [*st["sems1"], *st["sems2"]], 0, body, after=after)
    return thru


def _swap_start(sources, lands, plan, per_array, *, name):
    n = len(sources)

    def body(refs, _, send, recv):
        x, y, core = lax.axis_index("x"), lax.axis_index("y"), lax.axis_index("c")
        for t in range(n):
            for k in range(per_array):
                block, to = plan(x, y, core, k)
                _remote(refs[t].at[block], refs[n + t].at[k], send, recv, per_array * t + k, to).start()

    sems, thru, token = _split_call(name, sources + lands, [], per_array * n, body)
    return {"n": n, "sems": sems, "arrays": thru, "token": token, "plan": plan, "per_array": per_array}


def _swap_finish(st, *, name, after):
    n, plan, per_array = st["n"], st["plan"], st["per_array"]

    def body(refs, sems, *_):
        send, recv = sems
        x, y, core = lax.axis_index("x"), lax.axis_index("y"), lax.axis_index("c")
        for t in range(n):
            for k in range(per_array):
                block, to = plan(x, y, core, k)
                cp = _remote(refs[t].at[block], refs[n + t].at[k], send, recv, per_array * t + k, to)
                cp.wait_send()
                cp.wait_recv()

    _, thru, _ = _split_call(name, st["arrays"], list(st["sems"]), 0, body, after=after)
    return thru[:n], thru[n:]


def _to_sibling(x, y, core, k):
    px, py = [(x, y), (1 - x, y), (x, 1 - y), (1 - x, 1 - y)][k]
    return _block_id(px, py, 1 - core), (x, y, 1 - core)


def _to_chip(x, y, core, k):
    px, py = [(1 - x, y), (x, 1 - y), (1 - x, 1 - y)][k]
    return k, (px, py, core)


PARAMS = {
    "sc_w_in": ((2, 1024, 3072), 2), "sc_conv_w": ((2, 3, 1024), 2), "sc_w_out": ((2, 1024, 1024), 1),
    "mla_w_dq": ((1, 1024, 384), 1), "mla_g_q": ((1, 384), None), "mla_w_uq": ((1, 384, 1536), 2),
    "mla_w_dkv": ((1, 1024, 320), 1), "mla_g_kv": ((1, 256), None), "mla_w_uk": ((1, 256, 8, 128), 1),
    "mla_w_uv": ((1, 256, 8, 128), 1), "mla_w_o": ((1, 1024, 1024), 1), "cf_w_pw1": ((1, 1024, 2048), 2),
    "cf_b_pw1": ((1, 2048), 1), "cf_dw_w": ((1, 31, 1024), 2), "cf_dw_b": ((1, 1024), 1), "cf_norm_g": ((1, 1024), 1),
    "cf_norm_b": ((1, 1024), 1), "cf_w_pw2": ((1, 1024, 1024), 1), "cf_b_pw2": ((1, 1024), 1),
    "ff_w1": ((4, 1024, 4096), 2), "ff_w2": ((4, 4096, 1024), 1), "ln_mix_g": ((4, 1024), None),
    "ln_mix_b": ((4, 1024), None), "ln_ff_g": ((4, 1024), None), "ln_ff_b": ((4, 1024), None),
}
NAMES = list(PARAMS)
BIG = ["sc_w_in", "sc_w_out", "mla_w_dq", "mla_w_uq", "mla_w_dkv", "mla_w_uk", "mla_w_uv", "mla_w_o", "cf_w_pw1",
       "cf_w_pw2", "ff_w1", "ff_w2"]
SMALL = [n for n in NAMES if n not in BIG]
SMALL_SHARDED = [n for n in SMALL if PARAMS[n][1] is not None]
SMALL_WIDTH = LANES
DEV_BLOCKED = ("sc_w_in", "cf_w_pw1", "ff_w1")
MIXER_BIG = {0: ["sc_w_in", "sc_w_out"], 1: ["mla_w_dq", "mla_w_uq", "mla_w_dkv", "mla_w_uk", "mla_w_uv", "mla_w_o"],
             2: ["cf_w_pw1", "cf_w_pw2"]}


def _layer_big(i):
    return [(n, i // N_MIXERS) for n in MIXER_BIG[i % N_MIXERS]] + [("ff_w1", i), ("ff_w2", i)]


def _shard_2d(name, a):
    return a.reshape(a.shape[0], -1)


def _shard_for_gather(name, a):
    flat = _shard_2d(name, a)
    return flat if name in DEV_BLOCKED else flat.astype(MXU_DTYPE)


def _as_operand(name, g):
    if name in DEV_BLOCKED:
        return g
    if name == "mla_w_uq":
        by_head = g.transpose(1, 0, 2)
        pieces = [by_head[:, :, :QK_NOPE], by_head[:, :, QK_NOPE:QK_NOPE + ROPE_HALF], by_head[:, :, QK_NOPE + ROPE_HALF:]]
        return jnp.concatenate([p.reshape(g.shape[1], -1) for p in pieces], axis=1)
    return g.reshape(-1, g.shape[2])


def _as_parts(name, grad):
    if name in DEV_BLOCKED:
        return grad
    if name == "mla_w_uq":
        rows = grad.shape[0]
        pieces = [grad[:, :Q_NOPE_COLS].reshape(rows, MLA_HEADS, QK_NOPE),
                  grad[:, Q_NOPE_COLS:Q_NOPE_COLS + Q_HALF_COLS].reshape(rows, MLA_HEADS, ROPE_HALF),
                  grad[:, Q_NOPE_COLS + Q_HALF_COLS:].reshape(rows, MLA_HEADS, ROPE_HALF)]
        return jnp.concatenate(pieces, axis=2).transpose(1, 0, 2)
    return grad.reshape(N_DEV, -1, grad.shape[1])


def _shard_shape(name):
    shape, ax = PARAMS[name]
    if ax is None:
        return shape
    return tuple(s // N_DEV if i == ax else s for i, s in enumerate(shape))


def _rows_of(shape, width, align):
    n = 1
    for s in shape:
        n *= s
    rows = -(-n // width)
    return -(-rows // align) * align


def _pack_rows(arrays, width, align, dtype, lead=0):
    segs = []
    for a in arrays:
        batch = a.shape[:lead]
        flat = a.astype(dtype).reshape(*batch, -1)
        rows = _rows_of(a.shape[lead:], width, align)
        flat = jnp.pad(flat, [(0, 0)] * lead + [(0, rows * width - flat.shape[-1])])
        segs.append(flat.reshape(*batch, rows, width))
    return jnp.concatenate(segs, axis=lead)


def _unpack_rows(buf, shapes, width, align):
    out, off = [], 0
    lead = buf.shape[:-2]
    for shape in shapes:
        n = 1
        for s in shape:
            n *= s
        rows = _rows_of(shape, width, align)
        seg = buf[..., off:off + rows, :].reshape(*lead, rows * width)[..., :n]
        out.append(seg.reshape(*lead, *shape))
        off += rows
    return out


def _merge_shards(stacked, ax):
    moved = jnp.moveaxis(stacked, 0, ax)
    shape = moved.shape
    return moved.reshape(*shape[:ax], shape[ax] * shape[ax + 1], *shape[ax + 2:])


def _split_shards(full, ax):
    shape = full.shape
    split = full.reshape(*shape[:ax], N_DEV, shape[ax] // N_DEV, *shape[ax + 1:])
    return jnp.moveaxis(split, ax, 0)


def _adam_math(wv, gv, mv, vv):
    m_new = ADAM_B1 * mv + (1.0 - ADAM_B1) * gv
    v_new = ADAM_B2 * vv + (1.0 - ADAM_B2) * (gv * gv)
    m_hat = m_new / (1.0 - ADAM_B1 ** ADAM_STEP)
    v_hat = v_new / (1.0 - ADAM_B2 ** ADAM_STEP)
    delta = -ADAM_LR * (m_hat / (jnp.sqrt(v_hat) + ADAM_EPS) + ADAM_WD * wv)
    return delta, m_new, v_new


def _adamw(w, g, m, v, *, name):
    shape = w.shape
    c = shape[-1]

    def fn(rows, bc):
        return _adam_math(*rows), ()

    flat = [a.reshape(-1, c) for a in (w, g, m, v)]
    outs = _rowwise(fn, flat, [], [(c, F32)] * 3, [], name=name, tb_cap=512)
    return tuple(o.reshape(shape) for o in outs)


def _adamw_reduced(w, m, v, layer_grads, *, name, after=()):
    n_l, r, c = w.shape
    tb = _row_tile(r, 256 if c > 512 else 512)
    nb = r // tb

    def body(*refs):
        w_ref, m_ref, v_ref = refs[:3]
        g_refs = refs[3:3 + 2 * n_l]
        g_out, d_out, m_out, v_out = refs[3 + 2 * n_l + len(after):]
        layer = pl.program_id(0)
        for l in range(n_l):
            @pl.when(layer == l)
            def _(l=l):
                own, arr = g_refs[2 * l], g_refs[2 * l + 1]
                g = own[...] + arr[0].astype(F32)
                g = g + arr[1].astype(F32)
                g = g + arr[2].astype(F32)
                delta, m_new, v_new = _adam_math(w_ref[...], g, m_ref[...], v_ref[...])
                g_out[...] = g
                d_out[...] = delta
                m_out[...] = m_new
                v_out[...] = v_new

    stacked = pl.BlockSpec((tb, c), lambda l, i: (l * nb + i, 0))
    in_specs = [stacked] * 3
    operands = [a.reshape(n_l * r, c) for a in (w, m, v)]
    for l, (own, arr) in enumerate(layer_grads):
        in_specs.append(pl.BlockSpec((tb, c), lambda ll, i, l=l: (jnp.where(ll == l, i, 0), 0)))
        in_specs.append(pl.BlockSpec((3, tb, c), lambda ll, i, l=l: (0, jnp.where(ll == l, i, 0), 0)))
        operands += [own, arr]
    in_specs += [ANY] * len(after)
    operands += list(after)
    outs = pl.pallas_call(
        body,
        name=name,
        grid=(n_l, nb),
        in_specs=in_specs,
        out_specs=[stacked] * 4,
        out_shape=[jax.ShapeDtypeStruct((n_l * r, c), F32)] * 4,
        compiler_params=_params(("arbitrary", "arbitrary")),
    )(*operands)
    return tuple(o.reshape(n_l, r, c) for o in outs)


def _train_step(x, target, weights, m_state, v_state):
    me = _block_id(lax.axis_index("x"), lax.axis_index("y"), lax.axis_index("c"))

    small_shapes = [_shard_shape(n) for n in SMALL_SHARDED]
    gather = {0: _gather_start([_pack_rows([weights[n] for n in SMALL_SHARDED], SMALL_WIDTH, SUBLANES, F32)]
                               + [_shard_for_gather(n, weights[n][0]) for n in MIXER_BIG[0]],
                               name="gather_g0_start", by_cols=[False] + [n in DEV_BLOCKED for n in MIXER_BIG[0]])}
    cur, cur_m = x[0], _cast_input(x[0])
    gather[0] = _gather_pass_on(gather[0], name="gather_g0_pass", after=cur_m)
    first = _gather_finish(gather[0], name="gather_g0_finish", after=gather[0]["token"])
    small_all, first_matrices = first[0], first[1:]
    small_full = {n: weights[n] for n in SMALL if PARAMS[n][1] is None}
    for n, stacked in zip(SMALL_SHARDED, _unpack_rows(small_all, small_shapes, SMALL_WIDTH, SUBLANES)):
        small_full[n] = _merge_shards(stacked, PARAMS[n][1])

    def small_weights(i):
        wl = {}
        for n in SMALL:
            if n.startswith("ln_"):
                wl[n] = small_full[n][i:i + 1]
            elif n.startswith(("sc_", "mla_", "cf_")[i % N_MIXERS]):
                v = small_full[n][i // N_MIXERS]
                wl[n] = v if v.ndim == 2 else v.reshape(1, -1)
        return wl

    layers = [small_weights(i) for i in range(DEPTH)]
    groups = [(kind, i) for i in range(DEPTH) for kind in ("mix", "ff")]

    def group_names(g):
        kind, i = g
        if kind == "mix":
            return [(n, i // N_MIXERS) for n in MIXER_BIG[i % N_MIXERS]]
        return [("ff_w1", i), ("ff_w2", i)]

    def shards_of(g):
        return [_shard_for_gather(n, weights[n][j]) for n, j in group_names(g)]

    def install(g, gathered):
        for (n, _), a in zip(group_names(g), gathered):
            layers[g[1]][n] = _as_operand(n, a)

    install(groups[0], first_matrices)
    whole = lambda g: [n in DEV_BLOCKED for n, _ in group_names(g)]
    gather[1] = _gather_start(shards_of(groups[1]), name="gather_g1_start", after=gather[0]["token"],
                              by_cols=whole(groups[1]))
    saved = {}
    for k, g in enumerate(groups):
        kind, i = g
        tokens = [gather[1]["token"]] if k == 0 else []
        if k + 2 < len(groups):
            gather[k + 2] = _gather_start(shards_of(groups[k + 2]), name=f"gather_g{k + 2}_start",
                                          after=gather[k + 1]["token"], by_cols=whole(groups[k + 2]))
            tokens.append(gather[k + 2]["token"])

        def pass_on(y, k=k):
            if k + 1 == len(groups):
                return None
            gather[k + 1] = _gather_pass_on(gather[k + 1], name=f"gather_g{k + 1}_pass", after=y)
            return gather[k + 1]["token"]

        phase = _mix_fwd if kind == "mix" else _ff_fwd
        cur, cur_m, saved[g] = phase(i, cur, cur_m, layers[i], hook=pass_on, after=tokens)
        if k + 1 < len(groups):
            install(groups[k + 1], _gather_finish(gather[k + 1], name=f"gather_g{k + 1}_finish", after=cur))

    loss_local, dy = _loss_and_grad(cur, target[0])
    loss = lax.psum(loss_local, MESH_AXES)

    reduced = {n: [None] * PARAMS[n][0][0] for n in BIG}
    small_grads = {n: [None] * PARAMS[n][0][0] for n in SMALL}
    reduce = {}

    def to_chips(p, after):
        st = reduce[p]
        parts, got = _swap_finish(st["sibling"], name=f"reduce_p{p}_sibling_finish", after=after)
        sums = _chip_partials_group(parts, got, name=f"reduce_p{p}_chipsum")
        st["own"] = [own for own, _ in sums]
        others = [o for _, o in sums]
        st["chips"] = _swap_start(others, [lax.empty(o.shape, o.dtype) for o in others], _to_chip, 3,
                                  name=f"reduce_p{p}_chips_start")
        return st["chips"]["token"]

    def arrived(p, after):
        st = reduce[p]
        _, got = _swap_finish(st["chips"], name=f"reduce_p{p}_chips_finish", after=after)
        for (n, j), own, arr in zip(st["names"], st["own"], got):
            reduced[n][j] = (own, arr)

    terms = [(1.0, dy)]
    back = list(reversed(groups))
    for p, g in enumerate(back):
        kind, i = g

        def after_first_kernel(dz, p=p):
            return to_chips(p - 1, dz) if p >= 1 else None

        phase = _mix_bwd if kind == "mix" else _ff_bwd
        tokens = [reduce[p - 1]["sibling"]["token"]] if p >= 1 else []
        terms, grads = phase(i, terms, saved[g], layers[i], hook=after_first_kernel, after=tokens)
        for n in SMALL:
            if n in grads:
                small_grads[n][i if n.startswith("ln_") else i // N_MIXERS] = grads[n]
        if p >= 2:
            arrived(p - 2, terms[-1][1])
        names = group_names(g)
        parts = [_as_parts(n, grads[n]) for n, _ in names]
        reduce[p] = {"names": names, "sibling": _swap_start(
            parts, [lax.empty((4, *a.shape[1:]), a.dtype) for a in parts], _to_sibling, 4,
            name=f"reduce_p{p}_sibling_start")}
    grad_x = _sum_terms(terms, name="grad_x")
    last = len(back) - 1
    small_mine = _pack_rows([jnp.stack(small_grads[n]).reshape(PARAMS[n][0]) for n in SMALL], SMALL_WIDTH, SUBLANES, F32)
    small_gather = _gather_start([small_mine], name="gather_small_grads_start")
    tail = to_chips(last, small_gather["token"])
    arrived(last - 1, tail)

    grad, delta, new_m, new_v = {}, {}, {}, {}

    def adamw_matrix(n, after):
        shape = weights[n].shape
        view = lambda a: a.reshape(shape[0], shape[1], -1)
        outs = _adamw_reduced(view(weights[n]), view(m_state[n]), view(v_state[n]), reduced[n], name=f"adamw_{n}",
                              after=after)
        grad[n], delta[n], new_m[n], new_v[n] = [o.reshape(shape) for o in outs]
        return outs[0]

    done = tail
    for n in BIG:
        if n not in MIXER_BIG[0]:
            done = adamw_matrix(n, [done])
    small_gather = _gather_pass_on(small_gather, name="gather_small_grads_pass", after=done)
    small_all = _gather_finish(small_gather, name="gather_small_grads_finish", after=small_gather["token"])[0]
    small_sum = _sum_blocks(None, small_all, name="sum_small_grads")
    for n, g in zip(SMALL, _unpack_rows(small_sum, [PARAMS[n][0] for n in SMALL], SMALL_WIDTH, SUBLANES)):
        ax = PARAMS[n][1]
        grad[n] = g if ax is None else lax.dynamic_index_in_dim(_split_shards(g, ax), me, axis=0, keepdims=False)
    small_shards = [_shard_shape(n) for n in SMALL]
    packed = [_pack_rows([src[n] for n in SMALL], SMALL_WIDTH, SUBLANES, F32) for src in (weights, grad, m_state, v_state)]
    small_out = _adamw(*packed, name="adamw_small")
    for outs, dst in zip(small_out, (delta, new_m, new_v)):
        for n, a in zip(SMALL, _unpack_rows(outs, small_shards, SMALL_WIDTH, SUBLANES)):
            dst[n] = a
    arrived(last, small_sum)
    for n in MIXER_BIG[0]:
        adamw_matrix(n, [])
    return (loss, grad_x[None], *[grad[n] for n in NAMES], *[delta[n] for n in NAMES],
            *[new_m[n] for n in NAMES], *[new_v[n] for n in NAMES])


def kernel(x, sc_w_in, sc_conv_w, sc_w_out, mla_w_dq, mla_g_q, mla_w_uq, mla_w_dkv, mla_g_kv, mla_w_uk, mla_w_uv, mla_w_o, cf_w_pw1, cf_b_pw1, cf_dw_w, cf_dw_b, cf_norm_g, cf_norm_b, cf_w_pw2, cf_b_pw2, ff_w1, ff_w2, ln_mix_g, ln_mix_b, ln_ff_g, ln_ff_b, loss_target, m_sc_w_in, m_sc_conv_w, m_sc_w_out, m_mla_w_dq, m_mla_g_q, m_mla_w_uq, m_mla_w_dkv, m_mla_g_kv, m_mla_w_uk, m_mla_w_uv, m_mla_w_o, m_cf_w_pw1, m_cf_b_pw1, m_cf_dw_w, m_cf_dw_b, m_cf_norm_g, m_cf_norm_b, m_cf_w_pw2, m_cf_b_pw2, m_ff_w1, m_ff_w2, m_ln_mix_g, m_ln_mix_b, m_ln_ff_g, m_ln_ff_b, v_sc_w_in, v_sc_conv_w, v_sc_w_out, v_mla_w_dq, v_mla_g_q, v_mla_w_uq, v_mla_w_dkv, v_mla_g_kv, v_mla_w_uk, v_mla_w_uv, v_mla_w_o, v_cf_w_pw1, v_cf_b_pw1, v_cf_dw_w, v_cf_dw_b, v_cf_norm_g, v_cf_norm_b, v_cf_w_pw2, v_cf_b_pw2, v_ff_w1, v_ff_w2, v_ln_mix_g, v_ln_mix_b, v_ln_ff_g, v_ln_ff_b):
    w_list = (sc_w_in, sc_conv_w, sc_w_out, mla_w_dq, mla_g_q, mla_w_uq, mla_w_dkv, mla_g_kv, mla_w_uk, mla_w_uv, mla_w_o,
              cf_w_pw1, cf_b_pw1, cf_dw_w, cf_dw_b, cf_norm_g, cf_norm_b, cf_w_pw2, cf_b_pw2, ff_w1, ff_w2, ln_mix_g,
              ln_mix_b, ln_ff_g, ln_ff_b)
    m_list = (m_sc_w_in, m_sc_conv_w, m_sc_w_out, m_mla_w_dq, m_mla_g_q, m_mla_w_uq, m_mla_w_dkv, m_mla_g_kv, m_mla_w_uk,
              m_mla_w_uv, m_mla_w_o, m_cf_w_pw1, m_cf_b_pw1, m_cf_dw_w, m_cf_dw_b, m_cf_norm_g, m_cf_norm_b, m_cf_w_pw2,
              m_cf_b_pw2, m_ff_w1, m_ff_w2, m_ln_mix_g, m_ln_mix_b, m_ln_ff_g, m_ln_ff_b)
    v_list = (v_sc_w_in, v_sc_conv_w, v_sc_w_out, v_mla_w_dq, v_mla_g_q, v_mla_w_uq, v_mla_w_dkv, v_mla_g_kv, v_mla_w_uk,
              v_mla_w_uv, v_mla_w_o, v_cf_w_pw1, v_cf_b_pw1, v_cf_dw_w, v_cf_dw_b, v_cf_norm_g, v_cf_norm_b, v_cf_w_pw2,
              v_cf_b_pw2, v_ff_w1, v_ff_w2, v_ln_mix_g, v_ln_mix_b, v_ln_ff_g, v_ln_ff_b)
    return _train_step(x, loss_target, dict(zip(NAMES, w_list)), dict(zip(NAMES, m_list)), dict(zip(NAMES, v_list)))
```

```python
import functools

import jax
import jax.numpy as jnp
from jax import lax
from jax.experimental import pallas as pl
from jax.experimental.pallas import tpu as pltpu

F32 = jnp.float32
MXU_DTYPE = jnp.bfloat16

N_DEV = 8
D_MODEL = 1024
DEPTH = 4
N_MIXERS = 3
CHUNK = 64
ALPHA = (2.0 * DEPTH) ** 0.25
LN_EPS = 1e-5
RMS_EPS = 1e-6
MLA_HEADS = 8
QK_NOPE = 128
QK_ROPE = 64
V_HEAD = 128
Q_LORA = 384
KV_LORA = 256
ROPE_THETA = 10000.0
ADAM_LR = 0.001
ADAM_B1 = 0.9
ADAM_B2 = 0.999
ADAM_EPS = 1e-08
ADAM_WD = 0.01
ADAM_STEP = 10

LANES = 128
SUBLANES = 8
BF16_ROWS = 16
VMEM_LIMIT = 56 * 1024 * 1024
MM_VMEM_BUDGET = 36 * 1024 * 1024
CONV_PAD = 32
MASK_VALUE = -1e30

MESH_AXES = ("x", "y", "c")


def _params(semantics):
    return pltpu.CompilerParams(dimension_semantics=semantics, vmem_limit_bytes=VMEM_LIMIT)


def _tile(n, cap):
    if n <= cap:
        return n
    t = cap - cap % LANES
    while t >= LANES:
        if n % t == 0:
            return t
        t -= LANES
    raise ValueError(f"no tile for {n} under {cap}")


def _row_tile(rows, cap):
    if rows <= cap:
        return rows
    t = cap - cap % SUBLANES
    while t >= SUBLANES:
        if rows % t == 0:
            return t
        t -= SUBLANES
    raise ValueError(f"no row tile for {rows} under {cap}")


def _mm(a, b, *, name, ta=False, tb=False, b_dev=False, out_dev=False, out_dtypes=(F32,), epilogue=None,
        tile_extras=(), col_extras=(), after=(), tm_cap=2048, tn_cap=1024, tk_cap=4096):
    m, k = (a.shape[1], a.shape[0]) if ta else a.shape
    if b_dev:
        nd, b_rows, ns = b.shape
        n, kb = (b_rows, nd * ns) if tb else (nd * ns, b_rows)
    else:
        n, kb = (b.shape[0], b.shape[1]) if tb else (b.shape[1], b.shape[0])
    assert k == kb, (a.shape, b.shape, ta, tb)
    if ta or tile_extras or (b_dev and tb):
        tm_cap = min(tm_cap, 1024)
    tk = ns if (b_dev and tb) else _tile(k, tk_cap)
    nk = k // tk
    fixed_tn = ns if (b_dev and not tb) else (n // N_DEV if out_dev else None)
    if out_dev:
        assert not tile_extras and not col_extras and n % N_DEV == 0
    out_bytes = sum(jnp.dtype(dt).itemsize for dt in out_dtypes) + sum(e.dtype.itemsize for e in tile_extras)
    tm, tn = None, None
    for cand_m, cand_n in ((tm_cap, tn_cap), (tm_cap, tn_cap // 2), (tm_cap // 2, tn_cap), (tm_cap // 2, tn_cap // 2),
                           (tm_cap // 4, tn_cap // 2), (tm_cap // 4, tn_cap // 4)):
        tm, tn = _tile(m, cand_m), (fixed_tn if fixed_tn is not None else _tile(n, cand_n))
        need = 2 * (tm * tk * a.dtype.itemsize + tk * tn * b.dtype.itemsize) + 2 * tm * tn * out_bytes
        need += tm * tn * 4 * (2 if nk > 1 else 1)
        if need <= MM_VMEM_BUDGET:
            break
    n_te, n_ce, n_out = len(tile_extras), len(col_extras), len(out_dtypes)
    dims = (((0 if ta else 1,), (1 if tb else 0,)), ((), ()))

    def body(*refs):
        a_ref, b_ref = refs[0], refs[1]
        te_refs = refs[2:2 + n_te]
        ce_refs = refs[2 + n_te:2 + n_te + n_ce]
        first_out = 2 + n_te + n_ce + len(after)
        out_refs = refs[first_out:first_out + n_out]

        def finish(acc):
            if epilogue is None:
                outs = (acc,)
            else:
                outs = epilogue(acc, *[r[...] for r in te_refs], *[r[...] for r in ce_refs])
            for o_ref, o in zip(out_refs, outs):
                o_ref[...] = o.astype(o_ref.dtype)

        def product():
            return lax.dot_general(a_ref[...].astype(MXU_DTYPE), b_ref[...].astype(MXU_DTYPE), dims,
                                   preferred_element_type=F32)

        if nk == 1:
            finish(product())
        else:
            acc_ref = refs[-1]
            kk = pl.program_id(2)

            @pl.when(kk == 0)
            def _():
                acc_ref[...] = jnp.zeros_like(acc_ref)

            acc_ref[...] += product()

            @pl.when(kk == nk - 1)
            def _():
                finish(acc_ref[...])

    a_spec = pl.BlockSpec((tk, tm), lambda i, j, kk: (kk, i)) if ta else pl.BlockSpec((tm, tk), lambda i, j, kk: (i, kk))
    if b_dev and tb:
        b_spec = pl.BlockSpec((None, tn, tk), lambda i, j, kk: (kk, j, 0))
    elif b_dev:
        b_spec = pl.BlockSpec((None, tk, tn), lambda i, j, kk: (j, kk, 0))
    elif tb:
        b_spec = pl.BlockSpec((tn, tk), lambda i, j, kk: (j, kk))
    else:
        b_spec = pl.BlockSpec((tk, tn), lambda i, j, kk: (kk, j))
    tile_spec = pl.BlockSpec((tm, tn), lambda i, j, kk: (i, j))
    col_spec = pl.BlockSpec((1, tn), lambda i, j, kk: (0, j))
    if out_dev:
        out_spec = pl.BlockSpec((None, tm, tn), lambda i, j, kk: (j, i, 0))
        out_shape = [jax.ShapeDtypeStruct((N_DEV, m, tn), dt) for dt in out_dtypes]
    else:
        out_spec = tile_spec
        out_shape = [jax.ShapeDtypeStruct((m, n), dt) for dt in out_dtypes]
    outs = pl.pallas_call(
        body,
        name=name,
        grid=(m // tm, n // tn, nk),
        in_specs=[a_spec, b_spec] + [tile_spec] * n_te + [col_spec] * n_ce + [ANY] * len(after),
        out_specs=[out_spec] * n_out,
        out_shape=out_shape,
        scratch_shapes=[pltpu.VMEM((tm, tn), F32)] if nk > 1 else [],
        compiler_params=_params(("parallel", "parallel", "arbitrary")),
    )(a, b, *tile_extras, *col_extras, *after)
    return outs[0] if n_out == 1 else tuple(outs)


def _rowwise(fn, rows, bcast, out_rows, out_accs, *, name, tb_cap=256, after=()):
    t = rows[0].shape[0]
    tb = _row_tile(t, tb_cap)
    n_r, n_b, n_o, n_a = len(rows), len(bcast), len(out_rows), len(out_accs)

    def body(*refs):
        r_refs = refs[:n_r]
        b_refs = refs[n_r:n_r + n_b]
        first_out = n_r + n_b + len(after)
        o_refs = refs[first_out:first_out + n_o]
        a_refs = refs[first_out + n_o:]
        outs, accs = fn([r[...] for r in r_refs], [r[...] for r in b_refs])
        for o_ref, o in zip(o_refs, outs):
            o_ref[...] = o.astype(o_ref.dtype)
        if n_a:
            i = pl.program_id(0)

            @pl.when(i == 0)
            def _():
                for a_ref, acc in zip(a_refs, accs):
                    a_ref[...] = acc

            @pl.when(i > 0)
            def _():
                for a_ref, acc in zip(a_refs, accs):
                    a_ref[...] += acc

    in_specs = [pl.BlockSpec((tb, r.shape[1]), lambda i: (i, 0)) for r in rows]
    in_specs += [pl.BlockSpec(b.shape, lambda i: (0, 0)) for b in bcast]
    in_specs += [ANY] * len(after)
    out_specs = [pl.BlockSpec((tb, c), lambda i: (i, 0)) for c, _ in out_rows]
    out_specs += [pl.BlockSpec(s, lambda i: (0, 0)) for s in out_accs]
    out_shape = [jax.ShapeDtypeStruct((t, c), dt) for c, dt in out_rows]
    out_shape += [jax.ShapeDtypeStruct(s, F32) for s in out_accs]
    outs = pl.pallas_call(
        body,
        name=name,
        grid=(t // tb,),
        in_specs=in_specs,
        out_specs=out_specs,
        out_shape=out_shape,
        compiler_params=_params(("arbitrary",)),
    )(*rows, *bcast, *after)
    return tuple(outs)


def _colsum(v):
    return jnp.sum(v, axis=0, keepdims=True)


def _sigmoid(v):
    return 1.0 / (1.0 + jnp.exp(-v))


def _ln_fwd(x_res, y_sub, g, b, *, name, after=()):
    d = x_res.shape[1]

    def fn(rows, bc):
        z = ALPHA * rows[0] + rows[1]
        mu = jnp.mean(z, axis=-1, keepdims=True)
        zc = z - mu
        var = jnp.mean(zc * zc, axis=-1, keepdims=True)
        rstd = lax.rsqrt(var + LN_EPS)
        xhat = zc * rstd
        out = xhat * bc[0] + bc[1]
        return (out, out, xhat, rstd), ()

    return _rowwise(fn, [x_res, y_sub], [g, b], [(d, F32), (d, MXU_DTYPE), (d, F32), (1, F32)], [], name=name,
                    after=after, tb_cap=512)


def _ln_bwd(terms, xhat, rstd, g, *, name, after=()):
    d = xhat.shape[1]
    coefs = [c for c, _ in terms]

    def fn(rows, bc):
        xh, rs = rows[0], rows[1]
        dout = None
        for c, v in zip(coefs, rows[2:]):
            tv = v if c == 1.0 else c * v
            dout = tv if dout is None else dout + tv
        dxh = dout * bc[0]
        m1 = jnp.mean(dxh, axis=-1, keepdims=True)
        m2 = jnp.mean(dxh * xh, axis=-1, keepdims=True)
        dz = rs * (dxh - m1 - xh * m2)
        return (dz, dz), (_colsum(dout * xh), _colsum(dout), _colsum(dz))

    return _rowwise(fn, [xhat, rstd] + [v for _, v in terms], [g], [(d, F32), (d, MXU_DTYPE)],
                    [(1, d), (1, d), (1, d)], name=name, after=after, tb_cap=512)


def _rms_fwd(x, g, *, name):
    c = x.shape[1]

    def fn(rows, bc):
        v = rows[0]
        r = lax.rsqrt(jnp.mean(v * v, axis=-1, keepdims=True) + RMS_EPS)
        return (v * r * bc[0],), ()

    return _rowwise(fn, [x], [g], [(c, MXU_DTYPE)], [], name=name)[0]


def _rms_bwd(x, g, dy, *, name):
    c = x.shape[1]

    def fn(rows, bc):
        v, dyv = rows
        r = lax.rsqrt(jnp.mean(v * v, axis=-1, keepdims=True) + RMS_EPS)
        xh = v * r
        dxh = dyv * bc[0]
        dx = r * (dxh - xh * jnp.mean(dxh * xh, axis=-1, keepdims=True))
        return (dx,), (_colsum(dyv * xh),)

    return _rowwise(fn, [x, dy], [g], [(c, F32)], [(1, c)], name=name)


def _conv_chunk(t):
    return _row_tile(t, 256)


def _conv_fwd(x, w, bias, *, name):
    t, c = x.shape
    kw = w.shape[0]
    ch = _conv_chunk(t)

    def body(x_ref, w_ref, b_ref, o_ref, pad_ref):
        pad_ref[0:CONV_PAD, :] = jnp.zeros((CONV_PAD, LANES), F32)
        pad_ref[CONV_PAD:CONV_PAD + t, :] = x_ref[...]
        for t0 in range(0, t, ch):
            acc = jnp.broadcast_to(b_ref[...], (ch, LANES))
            for k in range(kw):
                s = kw - 1 - k
                acc = acc + w_ref[k:k + 1, :] * pad_ref[CONV_PAD - s + t0:CONV_PAD - s + t0 + ch, :]
            o_ref[t0:t0 + ch, :] = acc

    return pl.pallas_call(
        body,
        name=name,
        grid=(c // LANES,),
        in_specs=[pl.BlockSpec((t, LANES), lambda j: (0, j)), pl.BlockSpec((kw, LANES), lambda j: (0, j)),
                  pl.BlockSpec((1, LANES), lambda j: (0, j))],
        out_specs=pl.BlockSpec((t, LANES), lambda j: (0, j)),
        out_shape=jax.ShapeDtypeStruct((t, c), F32),
        scratch_shapes=[pltpu.VMEM((CONV_PAD + t, LANES), F32)],
        compiler_params=_params(("parallel",)),
    )(x, w, bias)


def _conv_bwd(dy, x, w, *, name):
    t, c = x.shape
    kw = w.shape[0]
    ch = _conv_chunk(t)

    def body(dy_ref, x_ref, w_ref, dx_ref, dw_ref, xpad_ref, dpad_ref):
        xpad_ref[0:CONV_PAD, :] = jnp.zeros((CONV_PAD, LANES), F32)
        xpad_ref[CONV_PAD:CONV_PAD + t, :] = x_ref[...]
        dpad_ref[0:t, :] = dy_ref[...]
        dpad_ref[t:t + CONV_PAD, :] = jnp.zeros((CONV_PAD, LANES), F32)
        for t0 in range(0, t, ch):
            acc = jnp.zeros((ch, LANES), F32)
            for k in range(kw):
                s = kw - 1 - k
                acc = acc + w_ref[k:k + 1, :] * dpad_ref[t0 + s:t0 + s + ch, :]
            dx_ref[t0:t0 + ch, :] = acc
        for k in range(kw):
            s = kw - 1 - k
            acc = jnp.zeros((ch, LANES), F32)
            for t0 in range(0, t, ch):
                acc = acc + dy_ref[t0:t0 + ch, :] * xpad_ref[CONV_PAD - s + t0:CONV_PAD - s + t0 + ch, :]
            dw_ref[k:k + 1, :] = _colsum(acc)

    blk = pl.BlockSpec((t, LANES), lambda j: (0, j))
    wblk = pl.BlockSpec((kw, LANES), lambda j: (0, j))
    return pl.pallas_call(
        body,
        name=name,
        grid=(c // LANES,),
        in_specs=[blk, blk, wblk],
        out_specs=[blk, wblk],
        out_shape=[jax.ShapeDtypeStruct((t, c), F32), jax.ShapeDtypeStruct((kw, c), F32)],
        scratch_shapes=[pltpu.VMEM((CONV_PAD + t, LANES), F32), pltpu.VMEM((CONV_PAD + t, LANES), F32)],
        compiler_params=_params(("parallel",)),
    )(dy, x, w)


ATTN_SCALE = (QK_NOPE + QK_ROPE) ** -0.5
_NT = (((1,), (1,)), ((), ()))
_TN = (((0,), (0,)), ((), ()))
_NN = (((1,), (0,)), ((), ()))


def _dot(a, b, dims):
    return lax.dot_general(a, b, dims, preferred_element_type=F32)


def _block_scores(q_cat, kn_ref, kp_ref, lo, hi, diagonal):
    k_cat = jnp.concatenate([kn_ref[lo:hi, :], kp_ref[lo:hi, :]], axis=1)
    s = _dot(q_cat, k_cat, _NT) * ATTN_SCALE
    if diagonal:
        shift = CHUNK.bit_length() - 1
        rows = lax.broadcasted_iota(jnp.int32, s.shape, 0)
        cols = lax.broadcasted_iota(jnp.int32, s.shape, 1)
        s = jnp.where(jnp.right_shift(cols, shift) <= jnp.right_shift(rows, shift), s, MASK_VALUE)
    return s, k_cat


def _attn_fwd(qn, qp, kv, kp, *, name):
    h, t, _ = qp.shape
    tq = _row_tile(t, 256)
    assert tq % CHUNK == 0

    def body(qn_ref, qp_ref, kn_ref, kp_ref, v_ref, o_ref, of_ref, lse_ref):
        for r0 in range(0, t, tq):
            r1 = r0 + tq
            q_cat = jnp.concatenate([qn_ref[r0:r1, :].astype(MXU_DTYPE), qp_ref[r0:r1, :]], axis=1)
            s_d, _ = _block_scores(q_cat, kn_ref, kp_ref, r0, r1, True)
            m = jnp.max(s_d, axis=-1, keepdims=True)
            if r0:
                s_b, _ = _block_scores(q_cat, kn_ref, kp_ref, 0, r0, False)
                m = jnp.maximum(m, jnp.max(s_b, axis=-1, keepdims=True))
            p_d = jnp.exp(s_d - m)
            l = jnp.sum(p_d, axis=-1, keepdims=True)
            acc = _dot(p_d.astype(MXU_DTYPE), v_ref[r0:r1, :], _NN)
            if r0:
                p_b = jnp.exp(s_b - m)
                l = l + jnp.sum(p_b, axis=-1, keepdims=True)
                acc = acc + _dot(p_b.astype(MXU_DTYPE), v_ref[0:r0, :], _NN)
            o = acc / l
            o_ref[r0:r1, :] = o.astype(o_ref.dtype)
            of_ref[r0:r1, :] = o
            lse_ref[r0:r1, :] = m + jnp.log(l)

    blk = lambda w: pl.BlockSpec((None, t, w), lambda hh: (hh, 0, 0))
    cols = pl.BlockSpec((t, QK_NOPE), lambda hh: (0, hh))
    keys = pl.BlockSpec((t, QK_NOPE), lambda hh: (0, 2 * hh))
    values = pl.BlockSpec((t, V_HEAD), lambda hh: (0, 2 * hh + 1))
    return pl.pallas_call(
        body,
        name=name,
        grid=(h,),
        in_specs=[cols, blk(QK_ROPE), keys, pl.BlockSpec((t, QK_ROPE), lambda hh: (0, 0)), values],
        out_specs=[cols, cols, blk(1)],
        out_shape=[jax.ShapeDtypeStruct((t, h * V_HEAD), MXU_DTYPE), jax.ShapeDtypeStruct((t, h * V_HEAD), F32),
                   jax.ShapeDtypeStruct((h, t, 1), F32)],
        compiler_params=_params(("parallel",)),
    )(qn, qp, kv, kp, kv)


def _attn_bwd(qn, qp, kv, kp, do, o, lse, *, name):
    h, t, _ = qp.shape
    tq = _row_tile(t, 256)

    def body(qn_ref, qp_ref, kn_ref, kp_ref, v_ref, do_ref, o_ref, lse_ref, dqn_ref, dqp_ref, dkv_ref, dkp_ref):
        dkv_ref[...] = jnp.zeros_like(dkv_ref)

        @pl.when(pl.program_id(0) == 0)
        def _():
            dkp_ref[...] = jnp.zeros_like(dkp_ref)

        for r0 in range(0, t, tq):
            r1 = r0 + tq
            q_cat = jnp.concatenate([qn_ref[r0:r1, :].astype(MXU_DTYPE), qp_ref[r0:r1, :]], axis=1)
            dov, lse = do_ref[r0:r1, :], lse_ref[r0:r1, :]
            delta = jnp.sum(dov.astype(F32) * o_ref[r0:r1, :], axis=-1, keepdims=True)

            def piece(lo, hi, diagonal):
                s, k_cat = _block_scores(q_cat, kn_ref, kp_ref, lo, hi, diagonal)
                p = jnp.exp(s - lse)
                dp = _dot(dov, v_ref[lo:hi, :], _NT)
                ds_m = (p * (dp - delta) * ATTN_SCALE).astype(MXU_DTYPE)
                dk_cat = _dot(ds_m, q_cat, _TN)
                dkv_ref[lo:hi, :QK_NOPE] += dk_cat[:, :QK_NOPE]
                dkp_ref[lo:hi, :] += dk_cat[:, QK_NOPE:]
                dkv_ref[lo:hi, QK_NOPE:] += _dot(p.astype(MXU_DTYPE), dov, _TN)
                return _dot(ds_m, k_cat, _NN)

            dq_cat = piece(r0, r1, True)
            if r0:
                dq_cat = dq_cat + piece(0, r0, False)
            dqn_ref[r0:r1, :] = dq_cat[:, :QK_NOPE]
            dqp_ref[r0:r1, :] = dq_cat[:, QK_NOPE:]

    blk = lambda w: pl.BlockSpec((None, t, w), lambda hh: (hh, 0, 0))
    kpblk = pl.BlockSpec((t, QK_ROPE), lambda hh: (0, 0))
    cols = pl.BlockSpec((t, QK_NOPE), lambda hh: (0, hh))
    keys = pl.BlockSpec((t, QK_NOPE), lambda hh: (0, 2 * hh))
    values = pl.BlockSpec((t, V_HEAD), lambda hh: (0, 2 * hh + 1))
    both = pl.BlockSpec((t, QK_NOPE + V_HEAD), lambda hh: (0, hh))
    return pl.pallas_call(
        body,
        name=name,
        grid=(h,),
        in_specs=[cols, blk(QK_ROPE), keys, kpblk, values, cols, cols, blk(1)],
        out_specs=[cols, blk(QK_ROPE), both, kpblk],
        out_shape=[jax.ShapeDtypeStruct((t, h * QK_NOPE), F32), jax.ShapeDtypeStruct((h, t, QK_ROPE), F32),
                   jax.ShapeDtypeStruct((t, h * (QK_NOPE + V_HEAD)), F32), jax.ShapeDtypeStruct((t, QK_ROPE), F32)],
        compiler_params=_params(("arbitrary",)),
    )(qn, qp, kv, kp, kv, do, o, lse)


def _rope(x1, x2, cos, sin, *, name, inverse=False):
    w = x1.shape[1]
    sign = -1.0 if inverse else 1.0

    def fn(rows, bc):
        a, b, c, s = rows
        s = sign * s
        return (a * c - b * s, a * s + b * c), ()

    return _rowwise(fn, [x1, x2, cos, sin], [], [(w, F32), (w, F32)], [], name=name)


def _mlp_fwd(x_m, w1, w2, tag, after=()):
    def epi(acc):
        r = jnp.maximum(acc, 0.0)
        return (r * r,)

    act = _mm(x_m, w1, name=f"{tag}_up", out_dtypes=(MXU_DTYPE,), epilogue=epi, after=after)
    y = _mm(act, w2, name=f"{tag}_down")
    return y, (x_m, act)


def _plus_residual(acc, res):
    return (acc + ALPHA * res,)


def _mlp_bwd(dy_m, saved, w1, w2, tag, residual, after=()):
    x_m, act = saved
    dw2 = _mm(act, dy_m, ta=True, name=f"{tag}_dw2", out_dtypes=(MXU_DTYPE,))

    def epi(acc, av):
        return (acc * (2.0 * jnp.sqrt(av.astype(F32))),)

    dh = _mm(dy_m, w2, tb=True, name=f"{tag}_dact", out_dtypes=(MXU_DTYPE,), epilogue=epi, tile_extras=(act,),
             after=after)
    dw1 = _mm(x_m, dh, ta=True, out_dev=True, name=f"{tag}_dw1", out_dtypes=(MXU_DTYPE,))
    dx = _mm(dh, w1, tb=True, name=f"{tag}_dx", epilogue=_plus_residual, tile_extras=(residual,))
    return [dx], {"ff_w1": dw1, "ff_w2": dw2}


def _sc_gate_specs(t, d):
    per = d // LANES
    return [pl.BlockSpec((t, LANES), lambda j, k=k: (0, k * per + j)) for k in range(3)]


def _sc_gate_fwd(u, conv_w, *, name):
    t, d = u.shape[0], u.shape[1] // 3
    kw = conv_w.shape[0]
    ch = _conv_chunk(t)

    def body(b_ref, c_ref, h_ref, w_ref, r_ref, pad_ref):
        pad_ref[0:CONV_PAD, :] = jnp.zeros((CONV_PAD, LANES), F32)
        pad_ref[CONV_PAD:CONV_PAD + t, :] = c_ref[...] * h_ref[...]
        for t0 in range(0, t, ch):
            q = jnp.zeros((ch, LANES), F32)
            for k in range(kw):
                s = kw - 1 - k
                q = q + w_ref[k:k + 1, :] * pad_ref[CONV_PAD - s + t0:CONV_PAD - s + t0 + ch, :]
            r_ref[t0:t0 + ch, :] = (b_ref[t0:t0 + ch, :] * q).astype(r_ref.dtype)

    blk = pl.BlockSpec((t, LANES), lambda j: (0, j))
    return pl.pallas_call(
        body,
        name=name,
        grid=(d // LANES,),
        in_specs=_sc_gate_specs(t, d) + [pl.BlockSpec((kw, LANES), lambda j: (0, j))],
        out_specs=blk,
        out_shape=jax.ShapeDtypeStruct((t, d), MXU_DTYPE),
        scratch_shapes=[pltpu.VMEM((CONV_PAD + t, LANES), F32)],
        compiler_params=_params(("parallel",)),
    )(u, u, u, conv_w)


def _sc_gate_bwd(dr, u, conv_w, *, name):
    t, d = u.shape[0], u.shape[1] // 3
    kw = conv_w.shape[0]
    ch = _conv_chunk(t)

    def body(dr_ref, b_ref, c_ref, h_ref, w_ref, db_ref, dp_ref, dw_ref, ppad_ref, dpad_ref):
        ppad_ref[0:CONV_PAD, :] = jnp.zeros((CONV_PAD, LANES), F32)
        ppad_ref[CONV_PAD:CONV_PAD + t, :] = c_ref[...] * h_ref[...]
        dpad_ref[0:t, :] = dr_ref[...] * b_ref[...]
        dpad_ref[t:t + CONV_PAD, :] = jnp.zeros((CONV_PAD, LANES), F32)
        for t0 in range(0, t, ch):
            q = jnp.zeros((ch, LANES), F32)
            dp = jnp.zeros((ch, LANES), F32)
            for k in range(kw):
                s = kw - 1 - k
                q = q + w_ref[k:k + 1, :] * ppad_ref[CONV_PAD - s + t0:CONV_PAD - s + t0 + ch, :]
                dp = dp + w_ref[k:k + 1, :] * dpad_ref[t0 + s:t0 + s + ch, :]
            db_ref[t0:t0 + ch, :] = (dr_ref[t0:t0 + ch, :] * q).astype(db_ref.dtype)
            dp_ref[t0:t0 + ch, :] = dp
        for k in range(kw):
            s = kw - 1 - k
            acc = jnp.zeros((ch, LANES), F32)
            for t0 in range(0, t, ch):
                acc = acc + dpad_ref[t0:t0 + ch, :] * ppad_ref[CONV_PAD - s + t0:CONV_PAD - s + t0 + ch, :]
            dw_ref[k:k + 1, :] = _colsum(acc)

    blk = pl.BlockSpec((t, LANES), lambda j: (0, j))
    wblk = pl.BlockSpec((kw, LANES), lambda j: (0, j))
    return pl.pallas_call(
        body,
        name=name,
        grid=(d // LANES,),
        in_specs=[blk] + _sc_gate_specs(t, d) + [wblk],
        out_specs=[blk, blk, wblk],
        out_shape=[jax.ShapeDtypeStruct((t, d), MXU_DTYPE), jax.ShapeDtypeStruct((t, d), F32),
                   jax.ShapeDtypeStruct((kw, d), F32)],
        scratch_shapes=[pltpu.VMEM((CONV_PAD + t, LANES), F32), pltpu.VMEM((CONV_PAD + t, LANES), F32)],
        compiler_params=_params(("parallel",)),
    )(dr, u, u, u, conv_w)


def _sc_fwd(x_m, w_in, conv_w, w_out, tag, after=()):
    u = _mm(x_m, w_in, name=f"{tag}_in", after=after)
    r = _sc_gate_fwd(u, conv_w, name=f"{tag}_gate")
    y = _mm(r, w_out, name=f"{tag}_out")
    return y, (x_m, u, r)


def _sc_bwd(dy_m, saved, w_in, conv_w, w_out, tag, residual, after=()):
    d = D_MODEL
    x_m, u, r = saved
    dw_out = _mm(r, dy_m, ta=True, name=f"{tag}_dwout", out_dtypes=(MXU_DTYPE,))
    dr = _mm(dy_m, w_out, tb=True, name=f"{tag}_dr", after=after)
    db, dp, dconv = _sc_gate_bwd(dr, u, conv_w, name=f"{tag}_dgate")

    def du_fn(rows, bc):
        dbv, dpv, uv = rows
        return (jnp.concatenate([dbv.astype(F32), dpv * uv[:, 2 * d:3 * d], dpv * uv[:, d:2 * d]], axis=1),), ()

    du = _rowwise(du_fn, [db, dp, u], [], [(3 * d, MXU_DTYPE)], [], name=f"{tag}_du")[0]
    dw_in = _mm(x_m, du, ta=True, out_dev=True, name=f"{tag}_dwin", out_dtypes=(MXU_DTYPE,))
    dx = _mm(du, w_in, tb=True, name=f"{tag}_dx", epilogue=_plus_residual, tile_extras=(residual,))
    return [dx], {"sc_w_in": dw_in, "sc_conv_w": dconv, "sc_w_out": dw_out}


def _cf_fwd(x_m, w, tag, after=()):
    d = D_MODEL

    def bias_epi(acc, bias):
        return (acc + bias,)

    u = _mm(x_m, w["cf_w_pw1"], name=f"{tag}_pw1", epilogue=bias_epi, col_extras=(w["cf_b_pw1"],),
            after=after)

    def glu_fn(rows, bc):
        uv = rows[0]
        return (uv[:, 0:d] * _sigmoid(uv[:, d:2 * d]),), ()

    h1 = _rowwise(glu_fn, [u], [], [(d, F32)], [], name=f"{tag}_glu")[0]
    h2 = _conv_fwd(h1, w["cf_dw_w"], w["cf_dw_b"], name=f"{tag}_conv")

    def norm_fn(rows, bc):
        hv = rows[0]
        mu = jnp.mean(hv, axis=-1, keepdims=True)
        hc = hv - mu
        rstd = lax.rsqrt(jnp.mean(hc * hc, axis=-1, keepdims=True) + LN_EPS)
        h3 = hc * rstd * bc[0] + bc[1]
        return (h3 * _sigmoid(h3),), ()

    h4 = _rowwise(norm_fn, [h2], [w["cf_norm_g"], w["cf_norm_b"]], [(d, MXU_DTYPE)], [], name=f"{tag}_norm")[0]
    y = _mm(h4, w["cf_w_pw2"], name=f"{tag}_pw2", epilogue=bias_epi, col_extras=(w["cf_b_pw2"],))
    return y, (x_m, u, h1, h2, h4)


def _cf_bwd(dy_m, dy_colsum, saved, w, tag, residual, after=()):
    d = D_MODEL
    x_m, u, h1, h2, h4 = saved
    dw_pw2 = _mm(h4, dy_m, ta=True, name=f"{tag}_dwpw2", out_dtypes=(MXU_DTYPE,))
    dh4 = _mm(dy_m, w["cf_w_pw2"], tb=True, name=f"{tag}_dh4", after=after)

    def dnorm_fn(rows, bc):
        dh4v, hv = rows
        g, b = bc
        mu = jnp.mean(hv, axis=-1, keepdims=True)
        hc = hv - mu
        rstd = lax.rsqrt(jnp.mean(hc * hc, axis=-1, keepdims=True) + LN_EPS)
        xh = hc * rstd
        h3 = xh * g + b
        sg = _sigmoid(h3)
        dh3 = dh4v * (sg * (1.0 + h3 * (1.0 - sg)))
        dxh = dh3 * g
        dh2 = rstd * (dxh - jnp.mean(dxh, axis=-1, keepdims=True) - xh * jnp.mean(dxh * xh, axis=-1, keepdims=True))
        return (dh2,), (_colsum(dh3 * xh), _colsum(dh3), _colsum(dh2))

    dh2, dnorm_g, dnorm_b, ddw_b = _rowwise(dnorm_fn, [dh4, h2], [w["cf_norm_g"], w["cf_norm_b"]], [(d, F32)],
                                            [(1, d), (1, d), (1, d)], name=f"{tag}_dnorm")
    dh1, ddw_w = _conv_bwd(dh2, h1, w["cf_dw_w"], name=f"{tag}_dconv")

    def dglu_fn(rows, bc):
        dh1v, uv = rows
        a, gate = uv[:, 0:d], uv[:, d:2 * d]
        sg = _sigmoid(gate)
        du = jnp.concatenate([dh1v * sg, dh1v * a * sg * (1.0 - sg)], axis=1)
        return (du,), (_colsum(du),)

    du, db_pw1 = _rowwise(dglu_fn, [dh1, u], [], [(2 * d, MXU_DTYPE)], [(1, 2 * d)], name=f"{tag}_dglu")
    dw_pw1 = _mm(x_m, du, ta=True, out_dev=True, name=f"{tag}_dwpw1", out_dtypes=(MXU_DTYPE,))
    dx = _mm(du, w["cf_w_pw1"], tb=True, name=f"{tag}_dx", epilogue=_plus_residual, tile_extras=(residual,))
    grads = {"cf_w_pw1": dw_pw1, "cf_b_pw1": db_pw1, "cf_dw_w": ddw_w, "cf_dw_b": ddw_b, "cf_norm_g": dnorm_g,
             "cf_norm_b": dnorm_b, "cf_w_pw2": dw_pw2, "cf_b_pw2": dy_colsum}
    return [dx], grads


def _rope_tables(t):
    pos = jnp.arange(t, dtype=F32)
    inv_freq = ROPE_THETA ** (-jnp.arange(0, QK_ROPE, 2, dtype=F32) / QK_ROPE)
    ang = pos[:, None] * inv_freq[None, :]
    return jnp.cos(ang), jnp.sin(ang)


ROPE_HALF = QK_ROPE // 2
Q_NOPE_COLS = MLA_HEADS * QK_NOPE
Q_HALF_COLS = MLA_HEADS * ROPE_HALF


def _interleave_heads(w_uk, w_uv):
    c = w_uk.shape[0]
    return jnp.concatenate([w_uk.reshape(c, MLA_HEADS, QK_NOPE), w_uv.reshape(c, MLA_HEADS, V_HEAD)],
                           axis=2).reshape(c, -1)


def _mla_fwd(x_m, w, tag, after=()):
    t = x_m.shape[0]
    cos, sin = _rope_tables(t)
    cos_h, sin_h = jnp.tile(cos, (1, MLA_HEADS)), jnp.tile(sin, (1, MLA_HEADS))
    cq_pre = _mm(x_m, w["mla_w_dq"], name=f"{tag}_dq", after=after)
    cq = _rms_fwd(cq_pre, w["mla_g_q"], name=f"{tag}_qnorm")
    q = _mm(cq, w["mla_w_uq"], name=f"{tag}_uq")
    r1, r2 = _rope(q[:, Q_NOPE_COLS:Q_NOPE_COLS + Q_HALF_COLS], q[:, Q_NOPE_COLS + Q_HALF_COLS:], cos_h, sin_h,
                   name=f"{tag}_qrope")
    qp_h = jnp.concatenate([r1.reshape(t, MLA_HEADS, ROPE_HALF), r2.reshape(t, MLA_HEADS, ROPE_HALF)],
                           axis=-1).transpose(1, 0, 2).astype(MXU_DTYPE)
    ckv_full = _mm(x_m, w["mla_w_dkv"], name=f"{tag}_dkv")
    ckv_pre = ckv_full[:, :KV_LORA]
    ckv = _rms_fwd(ckv_pre, w["mla_g_kv"], name=f"{tag}_kvnorm")
    k1, k2 = _rope(ckv_full[:, KV_LORA:KV_LORA + ROPE_HALF], ckv_full[:, KV_LORA + ROPE_HALF:], cos, sin,
                   name=f"{tag}_krope")
    kp = jnp.concatenate([k1, k2], axis=-1).astype(MXU_DTYPE)
    kv = _mm(ckv, _interleave_heads(w["mla_w_uk"], w["mla_w_uv"]), name=f"{tag}_ukv", out_dtypes=(MXU_DTYPE,))
    o, o_f, lse = _attn_fwd(q, qp_h, kv, kp, name=f"{tag}_attn")
    y = _mm(o, w["mla_w_o"], name=f"{tag}_o")
    return y, (x_m, cq_pre, cq, ckv_pre, ckv, q, qp_h, kv, kp, lse, o, o_f, (cos, sin, cos_h, sin_h))


def _mla_bwd(dy_m, saved, w, tag, residual, after=()):
    x_m, cq_pre, cq, ckv_pre, ckv, q, qp_h, kv, kp, lse, o, o_f, (cos, sin, cos_h, sin_h) = saved
    t = x_m.shape[0]
    half = ROPE_HALF
    dw_o = _mm(o, dy_m, ta=True, name=f"{tag}_dwo", out_dtypes=(MXU_DTYPE,))
    do = _mm(dy_m, w["mla_w_o"], tb=True, name=f"{tag}_do", out_dtypes=(MXU_DTYPE,), after=after)
    dqn, dqp_h, dkv, dkp = _attn_bwd(q, qp_h, kv, kp, do, o_f, lse, name=f"{tag}_dattn")
    dqp = dqp_h.transpose(1, 0, 2)
    d1, d2 = _rope(dqp[:, :, :half].reshape(t, -1), dqp[:, :, half:].reshape(t, -1), cos_h, sin_h,
                   name=f"{tag}_dqrope", inverse=True)
    dq = jnp.concatenate([dqn, d1, d2], axis=-1).astype(MXU_DTYPE)
    dw_uq = _mm(cq, dq, ta=True, name=f"{tag}_dwuq", out_dtypes=(MXU_DTYPE,))
    dcq = _mm(dq, w["mla_w_uq"], tb=True, name=f"{tag}_dcq")
    dcq_pre, dg_q = _rms_bwd(cq_pre, w["mla_g_q"], dcq, name=f"{tag}_dqnorm")
    dcq_pre_m = dcq_pre.astype(MXU_DTYPE)
    dw_dq = _mm(x_m, dcq_pre_m, ta=True, name=f"{tag}_dwdq", out_dtypes=(MXU_DTYPE,))
    dx_q = _mm(dcq_pre_m, w["mla_w_dq"], tb=True, name=f"{tag}_dxq", epilogue=_plus_residual, tile_extras=(residual,))
    dw_kv = _mm(ckv, dkv, ta=True, name=f"{tag}_dwukv", out_dtypes=(MXU_DTYPE,))
    dw_kv = dw_kv.reshape(KV_LORA, MLA_HEADS, QK_NOPE + V_HEAD)
    dw_uk, dw_uv = dw_kv[:, :, :QK_NOPE].reshape(KV_LORA, -1), dw_kv[:, :, QK_NOPE:].reshape(KV_LORA, -1)
    dckv = _mm(dkv, _interleave_heads(w["mla_w_uk"], w["mla_w_uv"]), tb=True, name=f"{tag}_dckv")
    dckv_pre, dg_kv = _rms_bwd(ckv_pre, w["mla_g_kv"], dckv, name=f"{tag}_dkvnorm")
    dk1, dk2 = _rope(dkp[:, :half], dkp[:, half:], cos, sin, name=f"{tag}_dkrope", inverse=True)
    dckv_full = jnp.concatenate([dckv_pre, dk1, dk2], axis=-1).astype(MXU_DTYPE)
    dw_dkv = _mm(x_m, dckv_full, ta=True, name=f"{tag}_dwdkv", out_dtypes=(MXU_DTYPE,))
    dx_kv = _mm(dckv_full, w["mla_w_dkv"], tb=True, name=f"{tag}_dxkv", epilogue=lambda acc, other: (acc + other,),
                tile_extras=(dx_q,))
    grads = {"mla_w_dq": dw_dq, "mla_g_q": dg_q, "mla_w_uq": dw_uq, "mla_w_dkv": dw_dkv, "mla_g_kv": dg_kv,
             "mla_w_uk": dw_uk, "mla_w_uv": dw_uv, "mla_w_o": dw_o}
    return [dx_kv], grads


def _cast_input(x):
    cast = lambda rows, bc: ((rows[0],), ())
    return _rowwise(cast, [x], [], [(D_MODEL, MXU_DTYPE)], [], name="x_cast")[0]


def _hooked(value, hook):
    token = None if hook is None else hook(value)
    return () if token is None else (token,)


def _mix_fwd(i, cur, cur_m, wl, hook=None, after=()):
    mixer = i % N_MIXERS
    if mixer == 0:
        y, s_mix = _sc_fwd(cur_m, wl["sc_w_in"], wl["sc_conv_w"], wl["sc_w_out"], f"l{i}_sc", after=after)
    elif mixer == 1:
        y, s_mix = _mla_fwd(cur_m, wl, f"l{i}_mla", after=after)
    else:
        y, s_mix = _cf_fwd(cur_m, wl, f"l{i}_cf", after=after)
    cur, cur_m, xh_mix, rs_mix = _ln_fwd(cur, y, wl["ln_mix_g"], wl["ln_mix_b"], name=f"l{i}_lnmix",
                                         after=_hooked(y, hook))
    return cur, cur_m, (s_mix, xh_mix, rs_mix)


def _ff_fwd(i, cur, cur_m, wl, hook=None, after=()):
    y, s_ff = _mlp_fwd(cur_m, wl["ff_w1"], wl["ff_w2"], f"l{i}_ff", after=after)
    cur, cur_m, xh_ff, rs_ff = _ln_fwd(cur, y, wl["ln_ff_g"], wl["ln_ff_b"], name=f"l{i}_lnff", after=_hooked(y, hook))
    return cur, cur_m, (s_ff, xh_ff, rs_ff)


def _layer_fwd(i, cur, cur_m, wl):
    cur, cur_m, s_mix = _mix_fwd(i, cur, cur_m, wl)
    cur, cur_m, s_ff = _ff_fwd(i, cur, cur_m, wl)
    return cur, cur_m, (*s_mix, *s_ff)


def _loss_and_grad(out, target):
    d = D_MODEL

    def loss_fn(rows, bc):
        err = rows[0] - rows[1]
        return (err * (1.0 / d),), (_colsum(err * err),)

    dy, sq = _rowwise(loss_fn, [out, target], [], [(d, F32)], [(1, d)], name="loss")
    return (0.5 / d) * jnp.sum(sq), dy


def _ff_bwd(i, terms, saved, wl, hook=None, after=()):
    s_ff, xh_ff, rs_ff = saved
    grads = {}
    dz, dz_m, grads["ln_ff_g"], grads["ln_ff_b"], _ = _ln_bwd(terms, xh_ff, rs_ff, wl["ln_ff_g"], name=f"l{i}_dlnff",
                                                              after=after)
    dxs, g_ff = _mlp_bwd(dz_m, s_ff, wl["ff_w1"], wl["ff_w2"], f"l{i}_ff", dz, after=_hooked(dz, hook))
    grads.update(g_ff)
    return [(1.0, v) for v in dxs], grads


def _mix_bwd(i, terms, saved, wl, hook=None, after=()):
    mixer = i % N_MIXERS
    s_mix, xh_mix, rs_mix = saved
    grads = {}
    dz, dz_m, grads["ln_mix_g"], grads["ln_mix_b"], dz_sum = _ln_bwd(terms, xh_mix, rs_mix, wl["ln_mix_g"],
                                                                     name=f"l{i}_dlnmix", after=after)
    behind = _hooked(dz, hook)
    if mixer == 0:
        dxs, g_mix = _sc_bwd(dz_m, s_mix, wl["sc_w_in"], wl["sc_conv_w"], wl["sc_w_out"], f"l{i}_sc", dz, after=behind)
    elif mixer == 1:
        dxs, g_mix = _mla_bwd(dz_m, s_mix, wl, f"l{i}_mla", dz, after=behind)
    else:
        dxs, g_mix = _cf_bwd(dz_m, dz_sum, s_mix, wl, f"l{i}_cf", dz, after=behind)
    grads.update(g_mix)
    return [(1.0, v) for v in dxs], grads


def _layer_bwd(i, terms, saved, wl):
    terms, g_ff = _ff_bwd(i, terms, saved[3:], wl)
    terms, g_mix = _mix_bwd(i, terms, saved[:3], wl)
    return terms, {**g_ff, **g_mix}


def _sum_terms(terms, *, name):
    coefs = [c for c, _ in terms]

    def sum_fn(rows, bc):
        acc = None
        for c, v in zip(coefs, rows):
            tv = v if c == 1.0 else c * v
            acc = tv if acc is None else acc + tv
        return (acc,), ()

    return _rowwise(sum_fn, [v for _, v in terms], [], [(D_MODEL, F32)], [], name=name)[0]


def _local_step(x, target, layers):
    cur, cur_m = x, _cast_input(x)
    saved = []
    for i in range(DEPTH):
        cur, cur_m, s = _layer_fwd(i, cur, cur_m, layers[i])
        saved.append(s)
    loss, dy = _loss_and_grad(cur, target)
    terms = [(1.0, dy)]
    grads = [None] * DEPTH
    for i in reversed(range(DEPTH)):
        terms, grads[i] = _layer_bwd(i, terms, saved[i], layers[i])
    return loss, _sum_terms(terms, name="grad_x"), grads


MESH_ID = pl.DeviceIdType.MESH
ANY = pl.BlockSpec(memory_space=pl.ANY)


def _block_id(px, py, pc):
    return 4 * px + 2 * py + pc


COPIES_PER_GATHER = 7


def _all_gather(shards, *, name):
    n = len(shards)

    def body(*refs):
        x_refs, out_refs = refs[:n], refs[n:2 * n]
        send_sems, recv_sems, local_sems = refs[2 * n:]
        x, y, core = lax.axis_index("x"), lax.axis_index("y"), lax.axis_index("c")
        me, sibling = (x, y, core), (x, y, 1 - core)
        chips = [(1 - x, y), (x, 1 - y), (1 - x, 1 - y)]

        def copy(t, k, block, to, from_input=False):
            dst = out_refs[t].at[_block_id(*block)]
            return pltpu.make_async_remote_copy(
                src_ref=x_refs[t] if from_input else dst, dst_ref=dst,
                send_sem=send_sems.at[t * COPIES_PER_GATHER + k], recv_sem=recv_sems.at[t * COPIES_PER_GATHER + k],
                device_id=to, device_id_type=MESH_ID)

        started = []
        mine = [pltpu.make_async_copy(x_refs[t], out_refs[t].at[_block_id(*me)], local_sems.at[t]) for t in range(n)]
        for t in range(n):
            mine[t].start()
            first = [copy(t, 0, me, sibling, from_input=True)]
            first += [copy(t, 1 + j, me, (*chip, core), from_input=True) for j, chip in enumerate(chips)]
            for cp in first:
                cp.start()
            started += first
        for t in range(n):
            for j, chip in enumerate(chips):
                copy(t, 1 + j, (*chip, core), me).wait_recv()
                passed = copy(t, 4 + j, (*chip, core), sibling)
                passed.start()
                started.append(passed)
        for t in range(n):
            copy(t, 0, sibling, me).wait_recv()
            for j, chip in enumerate(chips):
                copy(t, 4 + j, (*chip, 1 - core), me).wait_recv()
        for cp in started:
            cp.wait_send()
        for cp in mine:
            cp.wait()

    n_sems = n * COPIES_PER_GATHER
    outs = pl.pallas_call(
        body,
        name=name,
        out_shape=[jax.ShapeDtypeStruct((N_DEV, *s.shape), s.dtype) for s in shards],
        in_specs=[ANY] * n,
        out_specs=[ANY] * n,
        scratch_shapes=[pltpu.SemaphoreType.DMA((n_sems,)), pltpu.SemaphoreType.DMA((n_sems,)),
                        pltpu.SemaphoreType.DMA((n,))],
    )(*shards)
    return list(outs)


def _swap_with_sibling(parts, *, name):
    n = len(parts)

    def body(*refs):
        p_refs, got_refs = refs[:n], refs[n:2 * n]
        send_sems, recv_sems = refs[2 * n:]
        x, y, core = lax.axis_index("x"), lax.axis_index("y"), lax.axis_index("c")
        chips = [(x, y), (1 - x, y), (x, 1 - y), (1 - x, 1 - y)]
        copies = []
        for t in range(n):
            for k, (px, py) in enumerate(chips):
                copies.append(pltpu.make_async_remote_copy(
                    src_ref=p_refs[t].at[_block_id(px, py, 1 - core)], dst_ref=got_refs[t].at[k],
                    send_sem=send_sems.at[4 * t + k], recv_sem=recv_sems.at[4 * t + k], device_id=(x, y, 1 - core),
                    device_id_type=MESH_ID))
        for cp in copies:
            cp.start()
        for cp in copies:
            cp.wait()

    outs = pl.pallas_call(
        body,
        name=name,
        out_shape=[jax.ShapeDtypeStruct((4, *p.shape[1:]), p.dtype) for p in parts],
        in_specs=[ANY] * n,
        out_specs=[ANY] * n,
        scratch_shapes=[pltpu.SemaphoreType.DMA((4 * n,)), pltpu.SemaphoreType.DMA((4 * n,))],
    )(*parts)
    return list(outs)


def _swap_between_chips(parts, *, name):
    n = len(parts)

    def body(*refs):
        p_refs, got_refs = refs[:n], refs[n:2 * n]
        send_sems, recv_sems = refs[2 * n:]
        x, y, core = lax.axis_index("x"), lax.axis_index("y"), lax.axis_index("c")
        chips = [(1 - x, y), (x, 1 - y), (1 - x, 1 - y)]
        copies = []
        for t in range(n):
            for j, (px, py) in enumerate(chips):
                copies.append(pltpu.make_async_remote_copy(
                    src_ref=p_refs[t].at[j], dst_ref=got_refs[t].at[j], send_sem=send_sems.at[3 * t + j],
                    recv_sem=recv_sems.at[3 * t + j], device_id=(px, py, core), device_id_type=MESH_ID))
        for cp in copies:
            cp.start()
        for cp in copies:
            cp.wait()

    outs = pl.pallas_call(
        body,
        name=name,
        out_shape=[jax.ShapeDtypeStruct(p.shape, p.dtype) for p in parts],
        in_specs=[ANY] * n,
        out_specs=[ANY] * n,
        scratch_shapes=[pltpu.SemaphoreType.DMA((3 * n,)), pltpu.SemaphoreType.DMA((3 * n,))],
    )(*parts)
    return list(outs)


def _chip_partials(parts, got, *, name):
    _, r, c = parts.shape
    tb = _row_tile(r, 512)
    x, y, core = lax.axis_index("x"), lax.axis_index("y"), lax.axis_index("c")
    chips = [(x, y), (1 - x, y), (x, 1 - y), (1 - x, 1 - y)]
    ids = jnp.stack([_block_id(px, py, core) for px, py in chips]).astype(jnp.int32)

    def body(ids_ref, p0, p1, p2, p3, g_ref, own_ref, out_ref):
        own_ref[...] = p0[...].astype(F32) + g_ref[0].astype(F32)
        for j, p in enumerate((p1, p2, p3)):
            out_ref[j] = (p[...].astype(F32) + g_ref[j + 1].astype(F32)).astype(out_ref.dtype)

    def part_spec(k):
        return pl.BlockSpec((None, tb, c), lambda i, ids_ref: (ids_ref[k], i, 0))

    grid_spec = pltpu.PrefetchScalarGridSpec(
        num_scalar_prefetch=1,
        grid=(r // tb,),
        in_specs=[part_spec(k) for k in range(4)] + [pl.BlockSpec((4, tb, c), lambda i, ids_ref: (0, i, 0))],
        out_specs=[pl.BlockSpec((tb, c), lambda i, ids_ref: (i, 0)), pl.BlockSpec((3, tb, c), lambda i, ids_ref: (0, i, 0))],
    )
    return pl.pallas_call(
        body,
        name=name,
        grid_spec=grid_spec,
        out_shape=[jax.ShapeDtypeStruct((r, c), F32), jax.ShapeDtypeStruct((3, r, c), parts.dtype)],
        compiler_params=_params(("parallel",)),
    )(ids, parts, parts, parts, parts, got)


def _chip_partials_group(parts, got, *, name):
    n = len(parts)
    rows = [p.shape[1] for p in parts]
    steps = 4 if all(r % (4 * BF16_ROWS) == 0 for r in rows) else (2 if all(r % (2 * BF16_ROWS) == 0 for r in rows) else 1)
    x, y, core = lax.axis_index("x"), lax.axis_index("y"), lax.axis_index("c")
    chips = [(x, y), (1 - x, y), (x, 1 - y), (1 - x, 1 - y)]
    ids = jnp.stack([_block_id(px, py, core) for px, py in chips]).astype(jnp.int32)

    def body(ids_ref, *refs):
        ins, outs = refs[:5 * n], refs[5 * n:]
        for t in range(n):
            p0, p1, p2, p3, g_ref = ins[5 * t:5 * t + 5]
            own_ref, out_ref = outs[2 * t], outs[2 * t + 1]
            own_ref[...] = p0[...].astype(F32) + g_ref[0].astype(F32)
            for j, p in enumerate((p1, p2, p3)):
                out_ref[j] = (p[...].astype(F32) + g_ref[j + 1].astype(F32)).astype(out_ref.dtype)

    in_specs, out_specs, out_shape, operands = [], [], [], []
    for p, g in zip(parts, got):
        _, r, c = p.shape
        tb = r // steps
        in_specs += [pl.BlockSpec((None, tb, c), lambda i, ids_ref, k=k: (ids_ref[k], i, 0)) for k in range(4)]
        in_specs.append(pl.BlockSpec((4, tb, c), lambda i, ids_ref: (0, i, 0)))
        out_specs += [pl.BlockSpec((tb, c), lambda i, ids_ref: (i, 0)), pl.BlockSpec((3, tb, c), lambda i, ids_ref: (0, i, 0))]
        out_shape += [jax.ShapeDtypeStruct((r, c), F32), jax.ShapeDtypeStruct((3, r, c), p.dtype)]
        operands += [p, p, p, p, g]
    outs = pl.pallas_call(
        body,
        name=name,
        grid_spec=pltpu.PrefetchScalarGridSpec(num_scalar_prefetch=1, grid=(steps,), in_specs=in_specs,
                                               out_specs=out_specs),
        out_shape=out_shape,
        compiler_params=_params(("parallel",)),
    )(ids, *operands)
    return [(outs[2 * t], outs[2 * t + 1]) for t in range(n)]


def _sum_blocks(first, blocks, *, name):
    n, r, c = blocks.shape
    tb = _row_tile(r, 512)

    def body(*refs):
        b_ref, o_ref = refs[-2], refs[-1]
        acc = refs[0][...] if first is not None else b_ref[0].astype(F32)
        for j in range(0 if first is not None else 1, n):
            acc = acc + b_ref[j].astype(F32)
        o_ref[...] = acc

    row_spec = pl.BlockSpec((tb, c), lambda i: (i, 0))
    return pl.pallas_call(
        body,
        name=name,
        grid=(r // tb,),
        in_specs=([row_spec] if first is not None else []) + [pl.BlockSpec((n, tb, c), lambda i: (0, i, 0))],
        out_specs=row_spec,
        out_shape=jax.ShapeDtypeStruct((r, c), F32),
        compiler_params=_params(("parallel",)),
    )(*([first] if first is not None else []), blocks)


def _reduce_scatter(parts, *, tag):
    got = _swap_with_sibling(parts, name=f"{tag}_sibling")
    sums = [_chip_partials(p, g, name=f"{tag}_chipsum{t}") for t, (p, g) in enumerate(zip(parts, got))]
    arrived = _swap_between_chips([others for _, others in sums], name=f"{tag}_chips")
    return [(own, arr) for (own, _), arr in zip(sums, arrived)]


HBM = pl.BlockSpec(memory_space=pltpu.HBM)
SEM = pl.BlockSpec(memory_space=pltpu.SEMAPHORE)
DATAFLOW = pltpu.SideEffectType.DATAFLOW_SIDE_EFFECTING


def _tie(value, *before):
    return lax.optimization_barrier((value, *before))[0]


def _split_call(name, arrays, sems, n_new, body, after=None):
    n_a, n_s = len(arrays), len(sems)
    n_after = 0 if after is None else 1

    def kernel_body(*refs):
        new = refs[n_a + n_s + n_after:n_a + n_s + n_after + 2] if n_new else (None, None)
        body(refs[:n_a], refs[n_a:n_a + n_s], *new)
        refs[-1][...] = jnp.zeros_like(refs[-1])

    n_sem_out = 2 if n_new else 0
    out_shape = [pltpu.SemaphoreType.DMA((n_new,))] * n_sem_out
    out_shape += [pltpu.HBM(a.shape, a.dtype) for a in arrays]
    out_shape += [jax.ShapeDtypeStruct((SUBLANES, LANES), F32)]
    outs = pl.pallas_call(
        kernel_body,
        name=name,
        out_shape=out_shape,
        in_specs=[HBM] * n_a + [SEM] * n_s + [ANY] * n_after,
        out_specs=[SEM] * n_sem_out + [HBM] * n_a + [pl.BlockSpec(memory_space=pltpu.VMEM)],
        input_output_aliases={i: n_sem_out + i for i in range(n_a)},
        compiler_params=pltpu.CompilerParams(has_side_effects=DATAFLOW),
    )(*[pltpu.with_memory_space_constraint(a, pltpu.HBM) for a in arrays], *sems, *([after] if n_after else []))
    return tuple(outs[:n_sem_out]), list(outs[n_sem_out:n_sem_out + n_a]), outs[-1]


def _remote(src, dst, send, recv, k, to):
    return pltpu.make_async_remote_copy(src_ref=src, dst_ref=dst, send_sem=send.at[k], recv_sem=recv.at[k],
                                        device_id=to, device_id_type=MESH_ID)


def _land_block(land_ref, block, by_cols, width):
    if by_cols:
        return land_ref.at[:, pl.ds(pl.multiple_of(block * width, LANES), width)]
    return land_ref.at[block]


def _place_in_window(shard, *, name):
    r, c = shard.shape
    tb = _row_tile(r, 512)
    me = _block_id(lax.axis_index("x"), lax.axis_index("y"), lax.axis_index("c")).astype(jnp.int32).reshape(1)

    def body(me_ref, s_ref, o_ref):
        o_ref[...] = s_ref[...].astype(o_ref.dtype)

    return pl.pallas_call(
        body,
        name=name,
        grid_spec=pltpu.PrefetchScalarGridSpec(
            num_scalar_prefetch=1, grid=(r // tb,),
            in_specs=[pl.BlockSpec((tb, c), lambda i, me_ref: (i, 0))],
            out_specs=pl.BlockSpec((tb, c), lambda i, me_ref: (i, me_ref[0]))),
        out_shape=jax.ShapeDtypeStruct((r, N_DEV * c), MXU_DTYPE),
        compiler_params=_params(("parallel",)),
    )(me, shard)


def _gather_start(shards, *, name, after=None, by_cols=None):
    n = len(shards)
    by_cols = list(by_cols) if by_cols is not None else [False] * n
    me = _block_id(lax.axis_index("x"), lax.axis_index("y"), lax.axis_index("c"))
    lands = []
    for i, (s, cols) in enumerate(zip(shards, by_cols)):
        if cols:
            lands.append(_place_in_window(s, name=f"{name}_place{i}"))
        else:
            lands.append(lax.dynamic_update_index_in_dim(lax.empty((N_DEV, *s.shape), s.dtype), s, me, 0))
    widths = [s.shape[-1] for s in shards]

    def body(refs, _, send, recv):
        x, y, core = lax.axis_index("x"), lax.axis_index("y"), lax.axis_index("c")
        targets = [(x, y, 1 - core), (1 - x, y, core), (x, 1 - y, core), (1 - x, 1 - y, core)]
        for t in range(n):
            mine = _land_block(refs[t], _block_id(x, y, core), by_cols[t], widths[t])
            for k, to in enumerate(targets):
                _remote(mine, mine, send, recv, 4 * t + k, to).start()

    sems, thru, token = _split_call(name, lands, [], 4 * n, body, after=after)
    return {"n": n, "sems1": sems, "arrays": thru, "token": token, "by_cols": by_cols, "widths": widths}


def _gather_pass_on(st, *, name, after):
    n = st["n"]

    def body(refs, sems, send, recv):
        send1, recv1 = sems
        x, y, core = lax.axis_index("x"), lax.axis_index("y"), lax.axis_index("c")
        chips = [(1 - x, y), (x, 1 - y), (1 - x, 1 - y)]
        for t in range(n):
            for j, (px, py) in enumerate(chips):
                block = _land_block(refs[t], _block_id(px, py, core), st["by_cols"][t], st["widths"][t])
                _remote(block, block, send1, recv1, 4 * t + 1 + j, (x, y, 1 - core)).wait_recv()
                _remote(block, block, send, recv, 3 * t + j, (x, y, 1 - core)).start()

    sems2, thru, token = _split_call(name, st["arrays"], list(st["sems1"]), 3 * n, body, after=after)
    return {**st, "sems2": sems2, "arrays": thru, "token": token}


def _gather_finish(st, *, name, after):
    n = st["n"]

    def body(refs, sems, *_):
        send1, recv1, send2, recv2 = sems
        x, y, core = lax.axis_index("x"), lax.axis_index("y"), lax.axis_index("c")
        sibling = (x, y, 1 - core)
        chips = [(1 - x, y), (x, 1 - y), (1 - x, 1 - y)]
        for t in range(n):
            place = lambda px, py, pc, t=t: _land_block(refs[t], _block_id(px, py, pc), st["by_cols"][t],
                                                        st["widths"][t])
            mine = place(x, y, core)
            for k in range(4):
                _remote(mine, mine, send1, recv1, 4 * t + k, sibling).wait_send()
            _remote(mine, place(x, y, 1 - core), send1, recv1, 4 * t, sibling).wait_recv()
            for j, (px, py) in enumerate(chips):
                _remote(place(px, py, core), place(px, py, core), send2, recv2, 3 * t + j, sibling).wait_send()
                _remote(mine, place(px, py, 1 - core), send2, recv2, 3 * t + j, sibling).wait_recv()

    _, thru, _ = _split_call(name, st["arrays"], [*st["sems1"], *st["sems2"]], 0, body, after=after)
    return thru


def _swap_start(sources, lands, plan, per_array, *, name):
    n = len(sources)

    def body(refs, _, send, recv):
        x, y, core = lax.axis_index("x"), lax.axis_index("y"), lax.axis_index("c")
        for t in range(n):
            for k in range(per_array):
                block, to = plan(x, y, core, k)
                _remote(refs[t].at[block], refs[n + t].at[k], send, recv, per_array * t + k, to).start()

    sems, thru, token = _split_call(name, sources + lands, [], per_array * n, body)
    return {"n": n, "sems": sems, "arrays": thru, "token": token, "plan": plan, "per_array": per_array}


def _swap_finish(st, *, name, after):
    n, plan, per_array = st["n"], st["plan"], st["per_array"]

    def body(refs, sems, *_):
        send, recv = sems
        x, y, core = lax.axis_index("x"), lax.axis_index("y"), lax.axis_index("c")
        for t in range(n):
            for k in range(per_array):
                block, to = plan(x, y, core, k)
                cp = _remote(refs[t].at[block], refs[n + t].at[k], send, recv, per_array * t + k, to)
                cp.wait_send()
                cp.wait_recv()

    _, thru, _ = _split_call(name, st["arrays"], list(st["sems"]), 0, body, after=after)
    return thru[:n], thru[n:]


def _to_sibling(x, y, core, k):
    px, py = [(x, y), (1 - x, y), (x, 1 - y), (1 - x, 1 - y)][k]
    return _block_id(px, py, 1 - core), (x, y, 1 - core)


def _to_chip(x, y, core, k):
    px, py = [(1 - x, y), (x, 1 - y), (1 - x, 1 - y)][k]
    return k, (px, py, core)


PARAMS = {
    "sc_w_in": ((2, 1024, 3072), 2), "sc_conv_w": ((2, 3, 1024), 2), "sc_w_out": ((2, 1024, 1024), 1),
    "mla_w_dq": ((1, 1024, 384), 1), "mla_g_q": ((1, 384), None), "mla_w_uq": ((1, 384, 1536), 2),
    "mla_w_dkv": ((1, 1024, 320), 1), "mla_g_kv": ((1, 256), None), "mla_w_uk": ((1, 256, 8, 128), 1),
    "mla_w_uv": ((1, 256, 8, 128), 1), "mla_w_o": ((1, 1024, 1024), 1), "cf_w_pw1": ((1, 1024, 2048), 2),
    "cf_b_pw1": ((1, 2048), 1), "cf_dw_w": ((1, 31, 1024), 2), "cf_dw_b": ((1, 1024), 1), "cf_norm_g": ((1, 1024), 1),
    "cf_norm_b": ((1, 1024), 1), "cf_w_pw2": ((1, 1024, 1024), 1), "cf_b_pw2": ((1, 1024), 1),
    "ff_w1": ((4, 1024, 4096), 2), "ff_w2": ((4, 4096, 1024), 1), "ln_mix_g": ((4, 1024), None),
    "ln_mix_b": ((4, 1024), None), "ln_ff_g": ((4, 1024), None), "ln_ff_b": ((4, 1024), None),
}
NAMES = list(PARAMS)
BIG = ["sc_w_in", "sc_w_out", "mla_w_dq", "mla_w_uq", "mla_w_dkv", "mla_w_uk", "mla_w_uv", "mla_w_o", "cf_w_pw1",
       "cf_w_pw2", "ff_w1", "ff_w2"]
SMALL = [n for n in NAMES if n not in BIG]
SMALL_SHARDED = [n for n in SMALL if PARAMS[n][1] is not None]
SMALL_WIDTH = LANES
DEV_BLOCKED = ("sc_w_in", "cf_w_pw1", "ff_w1")
MIXER_BIG = {0: ["sc_w_in", "sc_w_out"], 1: ["mla_w_dq", "mla_w_uq", "mla_w_dkv", "mla_w_uk", "mla_w_uv", "mla_w_o"],
             2: ["cf_w_pw1", "cf_w_pw2"]}


def _layer_big(i):
    return [(n, i // N_MIXERS) for n in MIXER_BIG[i % N_MIXERS]] + [("ff_w1", i), ("ff_w2", i)]


def _shard_2d(name, a):
    return a.reshape(a.shape[0], -1)


def _shard_for_gather(name, a):
    flat = _shard_2d(name, a)
    return flat if name in DEV_BLOCKED else flat.astype(MXU_DTYPE)


def _as_operand(name, g):
    if name in DEV_BLOCKED:
        return g
    if name == "mla_w_uq":
        by_head = g.transpose(1, 0, 2)
        pieces = [by_head[:, :, :QK_NOPE], by_head[:, :, QK_NOPE:QK_NOPE + ROPE_HALF], by_head[:, :, QK_NOPE + ROPE_HALF:]]
        return jnp.concatenate([p.reshape(g.shape[1], -1) for p in pieces], axis=1)
    return g.reshape(-1, g.shape[2])


def _as_parts(name, grad):
    if name in DEV_BLOCKED:
        return grad
    if name == "mla_w_uq":
        rows = grad.shape[0]
        pieces = [grad[:, :Q_NOPE_COLS].reshape(rows, MLA_HEADS, QK_NOPE),
                  grad[:, Q_NOPE_COLS:Q_NOPE_COLS + Q_HALF_COLS].reshape(rows, MLA_HEADS, ROPE_HALF),
                  grad[:, Q_NOPE_COLS + Q_HALF_COLS:].reshape(rows, MLA_HEADS, ROPE_HALF)]
        return jnp.concatenate(pieces, axis=2).transpose(1, 0, 2)
    return grad.reshape(N_DEV, -1, grad.shape[1])


def _shard_shape(name):
    shape, ax = PARAMS[name]
    if ax is None:
        return shape
    return tuple(s // N_DEV if i == ax else s for i, s in enumerate(shape))


def _rows_of(shape, width, align):
    n = 1
    for s in shape:
        n *= s
    rows = -(-n // width)
    return -(-rows // align) * align


def _pack_rows(arrays, width, align, dtype, lead=0):
    segs = []
    for a in arrays:
        batch = a.shape[:lead]
        flat = a.astype(dtype).reshape(*batch, -1)
        rows = _rows_of(a.shape[lead:], width, align)
        flat = jnp.pad(flat, [(0, 0)] * lead + [(0, rows * width - flat.shape[-1])])
        segs.append(flat.reshape(*batch, rows, width))
    return jnp.concatenate(segs, axis=lead)


def _unpack_rows(buf, shapes, width, align):
    out, off = [], 0
    lead = buf.shape[:-2]
    for shape in shapes:
        n = 1
        for s in shape:
            n *= s
        rows = _rows_of(shape, width, align)
        seg = buf[..., off:off + rows, :].reshape(*lead, rows * width)[..., :n]
        out.append(seg.reshape(*lead, *shape))
        off += rows
    return out


def _merge_shards(stacked, ax):
    moved = jnp.moveaxis(stacked, 0, ax)
    shape = moved.shape
    return moved.reshape(*shape[:ax], shape[ax] * shape[ax + 1], *shape[ax + 2:])


def _split_shards(full, ax):
    shape = full.shape
    split = full.reshape(*shape[:ax], N_DEV, shape[ax] // N_DEV, *shape[ax + 1:])
    return jnp.moveaxis(split, ax, 0)


def _adam_math(wv, gv, mv, vv):
    m_new = ADAM_B1 * mv + (1.0 - ADAM_B1) * gv
    v_new = ADAM_B2 * vv + (1.0 - ADAM_B2) * (gv * gv)
    m_hat = m_new / (1.0 - ADAM_B1 ** ADAM_STEP)
    v_hat = v_new / (1.0 - ADAM_B2 ** ADAM_STEP)
    delta = -ADAM_LR * (m_hat / (jnp.sqrt(v_hat) + ADAM_EPS) + ADAM_WD * wv)
    return delta, m_new, v_new


def _adamw(w, g, m, v, *, name):
    shape = w.shape
    c = shape[-1]

    def fn(rows, bc):
        return _adam_math(*rows), ()

    flat = [a.reshape(-1, c) for a in (w, g, m, v)]
    outs = _rowwise(fn, flat, [], [(c, F32)] * 3, [], name=name, tb_cap=512)
    return tuple(o.reshape(shape) for o in outs)


def _adamw_reduced(w, m, v, layer_grads, *, name, after=()):
    n_l, r, c = w.shape
    tb = _row_tile(r, 256 if c > 512 else 512)
    nb = r // tb

    def body(*refs):
        w_ref, m_ref, v_ref = refs[:3]
        g_refs = refs[3:3 + 2 * n_l]
        g_out, d_out, m_out, v_out = refs[3 + 2 * n_l + len(after):]
        layer = pl.program_id(0)
        for l in range(n_l):
            @pl.when(layer == l)
            def _(l=l):
                own, arr = g_refs[2 * l], g_refs[2 * l + 1]
                g = own[...] + arr[0].astype(F32)
                g = g + arr[1].astype(F32)
                g = g + arr[2].astype(F32)
                delta, m_new, v_new = _adam_math(w_ref[...], g, m_ref[...], v_ref[...])
                g_out[...] = g
                d_out[...] = delta
                m_out[...] = m_new
                v_out[...] = v_new

    stacked = pl.BlockSpec((tb, c), lambda l, i: (l * nb + i, 0))
    in_specs = [stacked] * 3
    operands = [a.reshape(n_l * r, c) for a in (w, m, v)]
    for l, (own, arr) in enumerate(layer_grads):
        in_specs.append(pl.BlockSpec((tb, c), lambda ll, i, l=l: (jnp.where(ll == l, i, 0), 0)))
        in_specs.append(pl.BlockSpec((3, tb, c), lambda ll, i, l=l: (0, jnp.where(ll == l, i, 0), 0)))
        operands += [own, arr]
    in_specs += [ANY] * len(after)
    operands += list(after)
    outs = pl.pallas_call(
        body,
        name=name,
        grid=(n_l, nb),
        in_specs=in_specs,
        out_specs=[stacked] * 4,
        out_shape=[jax.ShapeDtypeStruct((n_l * r, c), F32)] * 4,
        compiler_params=_params(("arbitrary", "arbitrary")),
    )(*operands)
    return tuple(o.reshape(n_l, r, c) for o in outs)


def _train_step(x, target, weights, m_state, v_state):
    me = _block_id(lax.axis_index("x"), lax.axis_index("y"), lax.axis_index("c"))

    small_shapes = [_shard_shape(n) for n in SMALL_SHARDED]
    gather = {0: _gather_start([_pack_rows([weights[n] for n in SMALL_SHARDED], SMALL_WIDTH, SUBLANES, F32)]
                               + [_shard_for_gather(n, weights[n][0]) for n in MIXER_BIG[0]],
                               name="gather_g0_start", by_cols=[False] + [n in DEV_BLOCKED for n in MIXER_BIG[0]])}
    cur, cur_m = x[0], _cast_input(x[0])
    gather[0] = _gather_pass_on(gather[0], name="gather_g0_pass", after=cur_m)
    first = _gather_finish(gather[0], name="gather_g0_finish", after=gather[0]["token"])
    small_all, first_matrices = first[0], first[1:]
    small_full = {n: weights[n] for n in SMALL if PARAMS[n][1] is None}
    for n, stacked in zip(SMALL_SHARDED, _unpack_rows(small_all, small_shapes, SMALL_WIDTH, SUBLANES)):
        small_full[n] = _merge_shards(stacked, PARAMS[n][1])

    def small_weights(i):
        wl = {}
        for n in SMALL:
            if n.startswith("ln_"):
                wl[n] = small_full[n][i:i + 1]
            elif n.startswith(("sc_", "mla_", "cf_")[i % N_MIXERS]):
                v = small_full[n][i // N_MIXERS]
                wl[n] = v if v.ndim == 2 else v.reshape(1, -1)
        return wl

    layers = [small_weights(i) for i in range(DEPTH)]
    groups = [(kind, i) for i in range(DEPTH) for kind in ("mix", "ff")]

    def group_names(g):
        kind, i = g
        if kind == "mix":
            return [(n, i // N_MIXERS) for n in MIXER_BIG[i % N_MIXERS]]
        return [("ff_w1", i), ("ff_w2", i)]

    def shards_of(g):
        return [_shard_for_gather(n, weights[n][j]) for n, j in group_names(g)]

    def install(g, gathered):
        for (n, _), a in zip(group_names(g), gathered):
            layers[g[1]][n] = _as_operand(n, a)

    install(groups[0], first_matrices)
    whole = lambda g: [n in DEV_BLOCKED for n, _ in group_names(g)]
    gather[1] = _gather_start(shards_of(groups[1]), name="gather_g1_start", after=gather[0]["token"],
                              by_cols=whole(groups[1]))
    saved = {}
    for k, g in enumerate(groups):
        kind, i = g
        tokens = [gather[1]["token"]] if k == 0 else []
        if k + 2 < len(groups):
            gather[k + 2] = _gather_start(shards_of(groups[k + 2]), name=f"gather_g{k + 2}_start",
                                          after=gather[k + 1]["token"], by_cols=whole(groups[k + 2]))
            tokens.append(gather[k + 2]["token"])

        def pass_on(y, k=k):
            if k + 1 == len(groups):
                return None
            gather[k + 1] = _gather_pass_on(gather[k + 1], name=f"gather_g{k + 1}_pass", after=y)
            return gather[k + 1]["token"]

        phase = _mix_fwd if kind == "mix" else _ff_fwd
        cur, cur_m, saved[g] = phase(i, cur, cur_m, layers[i], hook=pass_on, after=tokens)
        if k + 1 < len(groups):
            install(groups[k + 1], _gather_finish(gather[k + 1], name=f"gather_g{k + 1}_finish", after=cur))

    loss_local, dy = _loss_and_grad(cur, target[0])
    loss = lax.psum(loss_local, MESH_AXES)

    reduced = {n: [None] * PARAMS[n][0][0] for n in BIG}
    small_grads = {n: [None] * PARAMS[n][0][0] for n in SMALL}
    reduce = {}

    def to_chips(p, after):
        st = reduce[p]
        parts, got = _swap_finish(st["sibling"], name=f"reduce_p{p}_sibling_finish", after=after)
        sums = _chip_partials_group(parts, got, name=f"reduce_p{p}_chipsum")
        st["own"] = [own for own, _ in sums]
        others = [o for _, o in sums]
        st["chips"] = _swap_start(others, [lax.empty(o.shape, o.dtype) for o in others], _to_chip, 3,
                                  name=f"reduce_p{p}_chips_start")
        return st["chips"]["token"]

    def arrived(p, after):
        st = reduce[p]
        _, got = _swap_finish(st["chips"], name=f"reduce_p{p}_chips_finish", after=after)
        for (n, j), own, arr in zip(st["names"], st["own"], got):
            reduced[n][j] = (own, arr)

    terms = [(1.0, dy)]
    back = list(reversed(groups))
    for p, g in enumerate(back):
        kind, i = g

        def after_first_kernel(dz, p=p):
            return to_chips(p - 1, dz) if p >= 1 else None

        phase = _mix_bwd if kind == "mix" else _ff_bwd
        tokens = [reduce[p - 1]["sibling"]["token"]] if p >= 1 else []
        terms, grads = phase(i, terms, saved[g], layers[i], hook=after_first_kernel, after=tokens)
        for n in SMALL:
            if n in grads:
                small_grads[n][i if n.startswith("ln_") else i // N_MIXERS] = grads[n]
        if p >= 2:
            arrived(p - 2, terms[-1][1])
        names = group_names(g)
        parts = [_as_parts(n, grads[n]) for n, _ in names]
        reduce[p] = {"names": names, "sibling": _swap_start(
            parts, [lax.empty((4, *a.shape[1:]), a.dtype) for a in parts], _to_sibling, 4,
            name=f"reduce_p{p}_sibling_start")}
    (_, grad_x), = terms
    last = len(back) - 1
    small_mine = _pack_rows([jnp.stack(small_grads[n]).reshape(PARAMS[n][0]) for n in SMALL], SMALL_WIDTH, SUBLANES, F32)
    small_gather = _gather_start([small_mine], name="gather_small_grads_start")
    tail = to_chips(last, small_gather["token"])
    arrived(last - 1, tail)

    grad, delta, new_m, new_v = {}, {}, {}, {}

    def adamw_matrix(n, after):
        shape = weights[n].shape
        view = lambda a: a.reshape(shape[0], shape[1], -1)
        outs = _adamw_reduced(view(weights[n]), view(m_state[n]), view(v_state[n]), reduced[n], name=f"adamw_{n}",
                              after=after)
        grad[n], delta[n], new_m[n], new_v[n] = [o.reshape(shape) for o in outs]
        return outs[0]

    done = tail
    for n in BIG:
        if n not in MIXER_BIG[0]:
            done = adamw_matrix(n, [done])
    small_gather = _gather_pass_on(small_gather, name="gather_small_grads_pass", after=done)
    small_all = _gather_finish(small_gather, name="gather_small_grads_finish", after=small_gather["token"])[0]
    small_sum = _sum_blocks(None, small_all, name="sum_small_grads")
    for n, g in zip(SMALL, _unpack_rows(small_sum, [PARAMS[n][0] for n in SMALL], SMALL_WIDTH, SUBLANES)):
        ax = PARAMS[n][1]
        grad[n] = g if ax is None else lax.dynamic_index_in_dim(_split_shards(g, ax), me, axis=0, keepdims=False)
    small_shards = [_shard_shape(n) for n in SMALL]
    packed = [_pack_rows([src[n] for n in SMALL], SMALL_WIDTH, SUBLANES, F32) for src in (weights, grad, m_state, v_state)]
    small_out = _adamw(*packed, name="adamw_small")
    for outs, dst in zip(small_out, (delta, new_m, new_v)):
        for n, a in zip(SMALL, _unpack_rows(outs, small_shards, SMALL_WIDTH, SUBLANES)):
            dst[n] = a
    arrived(last, small_sum)
    for n in MIXER_BIG[0]:
        adamw_matrix(n, [])
    return (loss, grad_x[None], *[grad[n] for n in NAMES], *[delta[n] for n in NAMES],
            *[new_m[n] for n in NAMES], *[new_v[n] for n in NAMES])


def kernel(x, sc_w_in, sc_conv_w, sc_w_out, mla_w_dq, mla_g_q, mla_w_uq, mla_w_dkv, mla_g_kv, mla_w_uk, mla_w_uv, mla_w_o, cf_w_pw1, cf_b_pw1, cf_dw_w, cf_dw_b, cf_norm_g, cf_norm_b, cf_w_pw2, cf_b_pw2, ff_w1, ff_w2, ln_mix_g, ln_mix_b, ln_ff_g, ln_ff_b, loss_target, m_sc_w_in, m_sc_conv_w, m_sc_w_out, m_mla_w_dq, m_mla_g_q, m_mla_w_uq, m_mla_w_dkv, m_mla_g_kv, m_mla_w_uk, m_mla_w_uv, m_mla_w_o, m_cf_w_pw1, m_cf_b_pw1, m_cf_dw_w, m_cf_dw_b, m_cf_norm_g, m_cf_norm_b, m_cf_w_pw2, m_cf_b_pw2, m_ff_w1, m_ff_w2, m_ln_mix_g, m_ln_mix_b, m_ln_ff_g, m_ln_ff_b, v_sc_w_in, v_sc_conv_w, v_sc_w_out, v_mla_w_dq, v_mla_g_q, v_mla_w_uq, v_mla_w_dkv, v_mla_g_kv, v_mla_w_uk, v_mla_w_uv, v_mla_w_o, v_cf_w_pw1, v_cf_b_pw1, v_cf_dw_w, v_cf_dw_b, v_cf_norm_g, v_cf_norm_b, v_cf_w_pw2, v_cf_b_pw2, v_ff_w1, v_ff_w2, v_ln_mix_g, v_ln_mix_b, v_ln_ff_g, v_ln_ff_b):
    w_list = (sc_w_in, sc_conv_w, sc_w_out, mla_w_dq, mla_g_q, mla_w_uq, mla_w_dkv, mla_g_kv, mla_w_uk, mla_w_uv, mla_w_o,
              cf_w_pw1, cf_b_pw1, cf_dw_w, cf_dw_b, cf_norm_g, cf_norm_b, cf_w_pw2, cf_b_pw2, ff_w1, ff_w2, ln_mix_g,
              ln_mix_b, ln_ff_g, ln_ff_b)
    m_list = (m_sc_w_in, m_sc_conv_w, m_sc_w_out, m_mla_w_dq, m_mla_g_q, m_mla_w_uq, m_mla_w_dkv, m_mla_g_kv, m_mla_w_uk,
              m_mla_w_uv, m_mla_w_o, m_cf_w_pw1, m_cf_b_pw1, m_cf_dw_w, m_cf_dw_b, m_cf_norm_g, m_cf_norm_b, m_cf_w_pw2,
              m_cf_b_pw2, m_ff_w1, m_ff_w2, m_ln_mix_g, m_ln_mix_b, m_ln_ff_g, m_ln_ff_b)
    v_list = (v_sc_w_in, v_sc_conv_w, v_sc_w_out, v_mla_w_dq, v_mla_g_q, v_mla_w_uq, v_mla_w_dkv, v_mla_g_kv, v_mla_w_uk,
              v_mla_w_uv, v_mla_w_o, v_cf_w_pw1, v_cf_b_pw1, v_cf_dw_w, v_cf_dw_b, v_cf_norm_g, v_cf_norm_b, v_cf_w_pw2,
              v_cf_b_pw2, v_ff_w1, v_ff_w2, v_ln_mix_g, v_ln_mix_b, v_ln_ff_g, v_ln_ff_b)
    return _train_step(x, loss_target, dict(zip(NAMES, w_list)), dict(zip(NAMES, m_list)), dict(zip(NAMES, v_list)))
```

```python
import functools

import jax
import jax.numpy as jnp
from jax import lax
from jax.experimental import pallas as pl
from jax.experimental.pallas import tpu as pltpu

F32 = jnp.float32
MXU_DTYPE = jnp.bfloat16

N_DEV = 8
D_MODEL = 1024
DEPTH = 4
N_MIXERS = 3
CHUNK = 64
ALPHA = (2.0 * DEPTH) ** 0.25
LN_EPS = 1e-5
RMS_EPS = 1e-6
MLA_HEADS = 8
QK_NOPE = 128
QK_ROPE = 64
V_HEAD = 128
Q_LORA = 384
KV_LORA = 256
ROPE_THETA = 10000.0
ADAM_LR = 0.001
ADAM_B1 = 0.9
ADAM_B2 = 0.999
ADAM_EPS = 1e-08
ADAM_WD = 0.01
ADAM_STEP = 10

LANES = 128
SUBLANES = 8
BF16_ROWS = 16
VMEM_LIMIT = 56 * 1024 * 1024
MM_VMEM_BUDGET = 36 * 1024 * 1024
CONV_PAD = 32
MASK_VALUE = -1e30

MESH_AXES = ("x", "y", "c")


def _params(semantics):
    return pltpu.CompilerParams(dimension_semantics=semantics, vmem_limit_bytes=VMEM_LIMIT)


def _tile(n, cap):
    if n <= cap:
        return n
    t = cap - cap % LANES
    while t >= LANES:
        if n % t == 0:
            return t
        t -= LANES
    raise ValueError(f"no tile for {n} under {cap}")


def _row_tile(rows, cap):
    if rows <= cap:
        return rows
    t = cap - cap % SUBLANES
    while t >= SUBLANES:
        if rows % t == 0:
            return t
        t -= SUBLANES
    raise ValueError(f"no row tile for {rows} under {cap}")


def _mm(a, b, *, name, ta=False, tb=False, b_dev=False, out_dev=False, out_dtypes=(F32,), epilogue=None,
        tile_extras=(), col_extras=(), after=(), tm_cap=2048, tn_cap=1024, tk_cap=4096):
    m, k = (a.shape[1], a.shape[0]) if ta else a.shape
    if b_dev:
        nd, b_rows, ns = b.shape
        n, kb = (b_rows, nd * ns) if tb else (nd * ns, b_rows)
    else:
        n, kb = (b.shape[0], b.shape[1]) if tb else (b.shape[1], b.shape[0])
    assert k == kb, (a.shape, b.shape, ta, tb)
    if ta or tile_extras or (b_dev and tb):
        tm_cap = min(tm_cap, 1024)
    tk = ns if (b_dev and tb) else _tile(k, tk_cap)
    nk = k // tk
    fixed_tn = ns if (b_dev and not tb) else (n // N_DEV if out_dev else None)
    if out_dev:
        assert not tile_extras and not col_extras and n % N_DEV == 0
    out_bytes = sum(jnp.dtype(dt).itemsize for dt in out_dtypes) + sum(e.dtype.itemsize for e in tile_extras)
    tm, tn = None, None
    for cand_m, cand_n in ((tm_cap, tn_cap), (tm_cap, tn_cap // 2), (tm_cap // 2, tn_cap), (tm_cap // 2, tn_cap // 2),
                           (tm_cap // 4, tn_cap // 2), (tm_cap // 4, tn_cap // 4)):
        tm, tn = _tile(m, cand_m), (fixed_tn if fixed_tn is not None else _tile(n, cand_n))
        need = 2 * (tm * tk * a.dtype.itemsize + tk * tn * b.dtype.itemsize) + 2 * tm * tn * out_bytes
        need += tm * tn * 4 * (2 if nk > 1 else 1)
        if need <= MM_VMEM_BUDGET:
            break
    n_te, n_ce, n_out = len(tile_extras), len(col_extras), len(out_dtypes)
    dims = (((0 if ta else 1,), (1 if tb else 0,)), ((), ()))

    def body(*refs):
        a_ref, b_ref = refs[0], refs[1]
        te_refs = refs[2:2 + n_te]
        ce_refs = refs[2 + n_te:2 + n_te + n_ce]
        first_out = 2 + n_te + n_ce + len(after)
        out_refs = refs[first_out:first_out + n_out]

        def finish(acc):
            if epilogue is None:
                outs = (acc,)
            else:
                outs = epilogue(acc, *[r[...] for r in te_refs], *[r[...] for r in ce_refs])
            for o_ref, o in zip(out_refs, outs):
                o_ref[...] = o.astype(o_ref.dtype)

        def product():
            return lax.dot_general(a_ref[...].astype(MXU_DTYPE), b_ref[...].astype(MXU_DTYPE), dims,
                                   preferred_element_type=F32)

        if nk == 1:
            finish(product())
        else:
            acc_ref = refs[-1]
            kk = pl.program_id(2)

            @pl.when(kk == 0)
            def _():
                acc_ref[...] = jnp.zeros_like(acc_ref)

            acc_ref[...] += product()

            @pl.when(kk == nk - 1)
            def _():
                finish(acc_ref[...])

    a_spec = pl.BlockSpec((tk, tm), lambda i, j, kk: (kk, i)) if ta else pl.BlockSpec((tm, tk), lambda i, j, kk: (i, kk))
    if b_dev and tb:
        b_spec = pl.BlockSpec((None, tn, tk), lambda i, j, kk: (kk, j, 0))
    elif b_dev:
        b_spec = pl.BlockSpec((None, tk, tn), lambda i, j, kk: (j, kk, 0))
    elif tb:
        b_spec = pl.BlockSpec((tn, tk), lambda i, j, kk: (j, kk))
    else:
        b_spec = pl.BlockSpec((tk, tn), lambda i, j, kk: (kk, j))
    tile_spec = pl.BlockSpec((tm, tn), lambda i, j, kk: (i, j))
    col_spec = pl.BlockSpec((1, tn), lambda i, j, kk: (0, j))
    if out_dev:
        out_spec = pl.BlockSpec((None, tm, tn), lambda i, j, kk: (j, i, 0))
        out_shape = [jax.ShapeDtypeStruct((N_DEV, m, tn), dt) for dt in out_dtypes]
    else:
        out_spec = tile_spec
        out_shape = [jax.ShapeDtypeStruct((m, n), dt) for dt in out_dtypes]
    outs = pl.pallas_call(
        body,
        name=name,
        grid=(m // tm, n // tn, nk),
        in_specs=[a_spec, b_spec] + [tile_spec] * n_te + [col_spec] * n_ce + [ANY] * len(after),
        out_specs=[out_spec] * n_out,
        out_shape=out_shape,
        scratch_shapes=[pltpu.VMEM((tm, tn), F32)] if nk > 1 else [],
        compiler_params=_params(("parallel", "parallel", "arbitrary")),
    )(a, b, *tile_extras, *col_extras, *after)
    return outs[0] if n_out == 1 else tuple(outs)


def _rowwise(fn, rows, bcast, out_rows, out_accs, *, name, tb_cap=256, after=()):
    t = rows[0].shape[0]
    tb = _row_tile(t, tb_cap)
    n_r, n_b, n_o, n_a = len(rows), len(bcast), len(out_rows), len(out_accs)

    def body(*refs):
        r_refs = refs[:n_r]
        b_refs = refs[n_r:n_r + n_b]
        first_out = n_r + n_b + len(after)
        o_refs = refs[first_out:first_out + n_o]
        a_refs = refs[first_out + n_o:]
        outs, accs = fn([r[...] for r in r_refs], [r[...] for r in b_refs])
        for o_ref, o in zip(o_refs, outs):
            o_ref[...] = o.astype(o_ref.dtype)
        if n_a:
            i = pl.program_id(0)

            @pl.when(i == 0)
            def _():
                for a_ref, acc in zip(a_refs, accs):
                    a_ref[...] = acc

            @pl.when(i > 0)
            def _():
                for a_ref, acc in zip(a_refs, accs):
                    a_ref[...] += acc

    in_specs = [pl.BlockSpec((tb, r.shape[1]), lambda i: (i, 0)) for r in rows]
    in_specs += [pl.BlockSpec(b.shape, lambda i: (0, 0)) for b in bcast]
    in_specs += [ANY] * len(after)
    out_specs = [pl.BlockSpec((tb, c), lambda i: (i, 0)) for c, _ in out_rows]
    out_specs += [pl.BlockSpec(s, lambda i: (0, 0)) for s in out_accs]
    out_shape = [jax.ShapeDtypeStruct((t, c), dt) for c, dt in out_rows]
    out_shape += [jax.ShapeDtypeStruct(s, F32) for s in out_accs]
    outs = pl.pallas_call(
        body,
        name=name,
        grid=(t // tb,),
        in_specs=in_specs,
        out_specs=out_specs,
        out_shape=out_shape,
        compiler_params=_params(("arbitrary",)),
    )(*rows, *bcast, *after)
    return tuple(outs)


def _colsum(v):
    return jnp.sum(v, axis=0, keepdims=True)


def _sigmoid(v):
    return 1.0 / (1.0 + jnp.exp(-v))


def _ln_fwd(x_res, y_sub, g, b, *, name, after=()):
    d = x_res.shape[1]

    def fn(rows, bc):
        z = ALPHA * rows[0] + rows[1]
        mu = jnp.mean(z, axis=-1, keepdims=True)
        zc = z - mu
        var = jnp.mean(zc * zc, axis=-1, keepdims=True)
        rstd = lax.rsqrt(var + LN_EPS)
        xhat = zc * rstd
        out = xhat * bc[0] + bc[1]
        return (out, out, xhat, rstd), ()

    return _rowwise(fn, [x_res, y_sub], [g, b], [(d, F32), (d, MXU_DTYPE), (d, F32), (1, F32)], [], name=name,
                    after=after, tb_cap=512)


def _ln_bwd(terms, xhat, rstd, g, *, name, after=()):
    d = xhat.shape[1]
    coefs = [c for c, _ in terms]

    def fn(rows, bc):
        xh, rs = rows[0], rows[1]
        dout = None
        for c, v in zip(coefs, rows[2:]):
            tv = v if c == 1.0 else c * v
            dout = tv if dout is None else dout + tv
        dxh = dout * bc[0]
        m1 = jnp.mean(dxh, axis=-1, keepdims=True)
        m2 = jnp.mean(dxh * xh, axis=-1, keepdims=True)
        dz = rs * (dxh - m1 - xh * m2)
        return (dz, dz), (_colsum(dout * xh), _colsum(dout), _colsum(dz))

    return _rowwise(fn, [xhat, rstd] + [v for _, v in terms], [g], [(d, F32), (d, MXU_DTYPE)],
                    [(1, d), (1, d), (1, d)], name=name, after=after, tb_cap=512)


def _rms_fwd(x, g, *, name):
    c = x.shape[1]

    def fn(rows, bc):
        v = rows[0]
        r = lax.rsqrt(jnp.mean(v * v, axis=-1, keepdims=True) + RMS_EPS)
        return (v * r * bc[0],), ()

    return _rowwise(fn, [x], [g], [(c, MXU_DTYPE)], [], name=name)[0]


def _rms_bwd(x, g, dy, *, name):
    c = x.shape[1]

    def fn(rows, bc):
        v, dyv = rows
        r = lax.rsqrt(jnp.mean(v * v, axis=-1, keepdims=True) + RMS_EPS)
        xh = v * r
        dxh = dyv * bc[0]
        dx = r * (dxh - xh * jnp.mean(dxh * xh, axis=-1, keepdims=True))
        return (dx,), (_colsum(dyv * xh),)

    return _rowwise(fn, [x, dy], [g], [(c, F32)], [(1, c)], name=name)


def _conv_chunk(t):
    return _row_tile(t, 256)


def _conv_fwd(x, w, bias, *, name):
    t, c = x.shape
    kw = w.shape[0]
    ch = _conv_chunk(t)

    def body(x_ref, w_ref, b_ref, o_ref, pad_ref):
        pad_ref[0:CONV_PAD, :] = jnp.zeros((CONV_PAD, LANES), F32)
        pad_ref[CONV_PAD:CONV_PAD + t, :] = x_ref[...]
        for t0 in range(0, t, ch):
            acc = jnp.broadcast_to(b_ref[...], (ch, LANES))
            for k in range(kw):
                s = kw - 1 - k
                acc = acc + w_ref[k:k + 1, :] * pad_ref[CONV_PAD - s + t0:CONV_PAD - s + t0 + ch, :]
            o_ref[t0:t0 + ch, :] = acc

    return pl.pallas_call(
        body,
        name=name,
        grid=(c // LANES,),
        in_specs=[pl.BlockSpec((t, LANES), lambda j: (0, j)), pl.BlockSpec((kw, LANES), lambda j: (0, j)),
                  pl.BlockSpec((1, LANES), lambda j: (0, j))],
        out_specs=pl.BlockSpec((t, LANES), lambda j: (0, j)),
        out_shape=jax.ShapeDtypeStruct((t, c), F32),
        scratch_shapes=[pltpu.VMEM((CONV_PAD + t, LANES), F32)],
        compiler_params=_params(("parallel",)),
    )(x, w, bias)


def _conv_bwd(dy, x, w, *, name):
    t, c = x.shape
    kw = w.shape[0]
    ch = _conv_chunk(t)

    def body(dy_ref, x_ref, w_ref, dx_ref, dw_ref, xpad_ref, dpad_ref):
        xpad_ref[0:CONV_PAD, :] = jnp.zeros((CONV_PAD, LANES), F32)
        xpad_ref[CONV_PAD:CONV_PAD + t, :] = x_ref[...]
        dpad_ref[0:t, :] = dy_ref[...]
        dpad_ref[t:t + CONV_PAD, :] = jnp.zeros((CONV_PAD, LANES), F32)
        for t0 in range(0, t, ch):
            acc = jnp.zeros((ch, LANES), F32)
            for k in range(kw):
                s = kw - 1 - k
                acc = acc + w_ref[k:k + 1, :] * dpad_ref[t0 + s:t0 + s + ch, :]
            dx_ref[t0:t0 + ch, :] = acc
        for k in range(kw):
            s = kw - 1 - k
            acc = jnp.zeros((ch, LANES), F32)
            for t0 in range(0, t, ch):
                acc = acc + dy_ref[t0:t0 + ch, :] * xpad_ref[CONV_PAD - s + t0:CONV_PAD - s + t0 + ch, :]
            dw_ref[k:k + 1, :] = _colsum(acc)

    blk = pl.BlockSpec((t, LANES), lambda j: (0, j))
    wblk = pl.BlockSpec((kw, LANES), lambda j: (0, j))
    return pl.pallas_call(
        body,
        name=name,
        grid=(c // LANES,),
        in_specs=[blk, blk, wblk],
        out_specs=[blk, wblk],
        out_shape=[jax.ShapeDtypeStruct((t, c), F32), jax.ShapeDtypeStruct((kw, c), F32)],
        scratch_shapes=[pltpu.VMEM((CONV_PAD + t, LANES), F32), pltpu.VMEM((CONV_PAD + t, LANES), F32)],
        compiler_params=_params(("parallel",)),
    )(dy, x, w)


ATTN_SCALE = (QK_NOPE + QK_ROPE) ** -0.5
_NT = (((1,), (1,)), ((), ()))
_TN = (((0,), (0,)), ((), ()))
_NN = (((1,), (0,)), ((), ()))


def _dot(a, b, dims):
    return lax.dot_general(a, b, dims, preferred_element_type=F32)


def _block_scores(q_cat, kn_ref, kp_ref, lo, hi, diagonal):
    k_cat = jnp.concatenate([kn_ref[lo:hi, :], kp_ref[lo:hi, :]], axis=1)
    s = _dot(q_cat, k_cat, _NT) * ATTN_SCALE
    if diagonal:
        shift = CHUNK.bit_length() - 1
        rows = lax.broadcasted_iota(jnp.int32, s.shape, 0)
        cols = lax.broadcasted_iota(jnp.int32, s.shape, 1)
        s = jnp.where(jnp.right_shift(cols, shift) <= jnp.right_shift(rows, shift), s, MASK_VALUE)
    return s, k_cat


def _attn_fwd(qn, qp, kv, kp, *, name):
    h, t, _ = qp.shape
    tq = _row_tile(t, 256)
    assert tq % CHUNK == 0

    def body(qn_ref, qp_ref, kn_ref, kp_ref, v_ref, o_ref, of_ref, lse_ref):
        for r0 in range(0, t, tq):
            r1 = r0 + tq
            q_cat = jnp.concatenate([qn_ref[r0:r1, :].astype(MXU_DTYPE), qp_ref[r0:r1, :]], axis=1)
            s_d, _ = _block_scores(q_cat, kn_ref, kp_ref, r0, r1, True)
            m = jnp.max(s_d, axis=-1, keepdims=True)
            if r0:
                s_b, _ = _block_scores(q_cat, kn_ref, kp_ref, 0, r0, False)
                m = jnp.maximum(m, jnp.max(s_b, axis=-1, keepdims=True))
            p_d = jnp.exp(s_d - m)
            l = jnp.sum(p_d, axis=-1, keepdims=True)
            acc = _dot(p_d.astype(MXU_DTYPE), v_ref[r0:r1, :], _NN)
            if r0:
                p_b = jnp.exp(s_b - m)
                l = l + jnp.sum(p_b, axis=-1, keepdims=True)
                acc = acc + _dot(p_b.astype(MXU_DTYPE), v_ref[0:r0, :], _NN)
            o = acc / l
            o_ref[r0:r1, :] = o.astype(o_ref.dtype)
            of_ref[r0:r1, :] = o
            lse_ref[r0:r1, :] = m + jnp.log(l)

    blk = lambda w: pl.BlockSpec((None, t, w), lambda hh: (hh, 0, 0))
    cols = pl.BlockSpec((t, QK_NOPE), lambda hh: (0, hh))
    keys = pl.BlockSpec((t, QK_NOPE), lambda hh: (0, 2 * hh))
    values = pl.BlockSpec((t, V_HEAD), lambda hh: (0, 2 * hh + 1))
    return pl.pallas_call(
        body,
        name=name,
        grid=(h,),
        in_specs=[cols, blk(QK_ROPE), keys, pl.BlockSpec((t, QK_ROPE), lambda hh: (0, 0)), values],
        out_specs=[cols, cols, blk(1)],
        out_shape=[jax.ShapeDtypeStruct((t, h * V_HEAD), MXU_DTYPE), jax.ShapeDtypeStruct((t, h * V_HEAD), F32),
                   jax.ShapeDtypeStruct((h, t, 1), F32)],
        compiler_params=_params(("parallel",)),
    )(qn, qp, kv, kp, kv)


def _attn_bwd(qn, qp, kv, kp, do, o, lse, *, name):
    h, t, _ = qp.shape
    tq = _row_tile(t, 256)

    def body(qn_ref, qp_ref, kn_ref, kp_ref, v_ref, do_ref, o_ref, lse_ref, dqn_ref, dqp_ref, dkv_ref, dkp_ref):
        dkv_ref[...] = jnp.zeros_like(dkv_ref)

        @pl.when(pl.program_id(0) == 0)
        def _():
            dkp_ref[...] = jnp.zeros_like(dkp_ref)

        for r0 in range(0, t, tq):
            r1 = r0 + tq
            q_cat = jnp.concatenate([qn_ref[r0:r1, :].astype(MXU_DTYPE), qp_ref[r0:r1, :]], axis=1)
            dov, lse = do_ref[r0:r1, :], lse_ref[r0:r1, :]
            delta = jnp.sum(dov.astype(F32) * o_ref[r0:r1, :], axis=-1, keepdims=True)

            def piece(lo, hi, diagonal):
                s, k_cat = _block_scores(q_cat, kn_ref, kp_ref, lo, hi, diagonal)
                p = jnp.exp(s - lse)
                dp = _dot(dov, v_ref[lo:hi, :], _NT)
                ds_m = (p * (dp - delta) * ATTN_SCALE).astype(MXU_DTYPE)
                dk_cat = _dot(ds_m, q_cat, _TN)
                dkv_ref[lo:hi, :QK_NOPE] += dk_cat[:, :QK_NOPE]
                dkp_ref[lo:hi, :] += dk_cat[:, QK_NOPE:]
                dkv_ref[lo:hi, QK_NOPE:] += _dot(p.astype(MXU_DTYPE), dov, _TN)
                return _dot(ds_m, k_cat, _NN)

            dq_cat = piece(r0, r1, True)
            if r0:
                dq_cat = dq_cat + piece(0, r0, False)
            dqn_ref[r0:r1, :] = dq_cat[:, :QK_NOPE]
            dqp_ref[r0:r1, :] = dq_cat[:, QK_NOPE:]

    blk = lambda w: pl.BlockSpec((None, t, w), lambda hh: (hh, 0, 0))
    kpblk = pl.BlockSpec((t, QK_ROPE), lambda hh: (0, 0))
    cols = pl.BlockSpec((t, QK_NOPE), lambda hh: (0, hh))
    keys = pl.BlockSpec((t, QK_NOPE), lambda hh: (0, 2 * hh))
    values = pl.BlockSpec((t, V_HEAD), lambda hh: (0, 2 * hh + 1))
    both = pl.BlockSpec((t, QK_NOPE + V_HEAD), lambda hh: (0, hh))
    return pl.pallas_call(
        body,
        name=name,
        grid=(h,),
        in_specs=[cols, blk(QK_ROPE), keys, kpblk, values, cols, cols, blk(1)],
        out_specs=[cols, blk(QK_ROPE), both, kpblk],
        out_shape=[jax.ShapeDtypeStruct((t, h * QK_NOPE), F32), jax.ShapeDtypeStruct((h, t, QK_ROPE), F32),
                   jax.ShapeDtypeStruct((t, h * (QK_NOPE + V_HEAD)), F32), jax.ShapeDtypeStruct((t, QK_ROPE), F32)],
        compiler_params=_params(("arbitrary",)),
    )(qn, qp, kv, kp, kv, do, o, lse)


def _rope(x1, x2, cos, sin, *, name, inverse=False):
    w = x1.shape[1]
    sign = -1.0 if inverse else 1.0

    def fn(rows, bc):
        a, b, c, s = rows
        s = sign * s
        return (a * c - b * s, a * s + b * c), ()

    return _rowwise(fn, [x1, x2, cos, sin], [], [(w, F32), (w, F32)], [], name=name)


def _mlp_fwd(x_m, w1, w2, tag, after=()):
    def epi(acc):
        r = jnp.maximum(acc, 0.0)
        return (r * r,)

    act = _mm(x_m, w1, name=f"{tag}_up", out_dtypes=(MXU_DTYPE,), epilogue=epi, after=after)
    y = _mm(act, w2, name=f"{tag}_down")
    return y, (x_m, act)


def _plus_residual(acc, res):
    return (acc + ALPHA * res,)


def _mlp_bwd(dy_m, saved, w1, w2, tag, residual, after=()):
    x_m, act = saved
    dw2 = _mm(act, dy_m, ta=True, name=f"{tag}_dw2", out_dtypes=(MXU_DTYPE,))

    def epi(acc, av):
        return (acc * (2.0 * jnp.sqrt(av.astype(F32))),)

    dh = _mm(dy_m, w2, tb=True, name=f"{tag}_dact", out_dtypes=(MXU_DTYPE,), epilogue=epi, tile_extras=(act,),
             after=after)
    dw1 = _mm(x_m, dh, ta=True, out_dev=True, name=f"{tag}_dw1", out_dtypes=(MXU_DTYPE,))
    dx = _mm(dh, w1, tb=True, name=f"{tag}_dx", epilogue=_plus_residual, tile_extras=(residual,))
    return [dx], {"ff_w1": dw1, "ff_w2": dw2}


def _sc_gate_specs(t, d):
    per = d // LANES
    return [pl.BlockSpec((t, LANES), lambda j, k=k: (0, k * per + j)) for k in range(3)]


def _sc_gate_fwd(u, conv_w, *, name):
    t, d = u.shape[0], u.shape[1] // 3
    kw = conv_w.shape[0]
    ch = _conv_chunk(t)

    def body(b_ref, c_ref, h_ref, w_ref, r_ref, pad_ref):
        pad_ref[0:CONV_PAD, :] = jnp.zeros((CONV_PAD, LANES), F32)
        pad_ref[CONV_PAD:CONV_PAD + t, :] = c_ref[...] * h_ref[...]
        for t0 in range(0, t, ch):
            q = jnp.zeros((ch, LANES), F32)
            for k in range(kw):
                s = kw - 1 - k
                q = q + w_ref[k:k + 1, :] * pad_ref[CONV_PAD - s + t0:CONV_PAD - s + t0 + ch, :]
            r_ref[t0:t0 + ch, :] = (b_ref[t0:t0 + ch, :] * q).astype(r_ref.dtype)

    blk = pl.BlockSpec((t, LANES), lambda j: (0, j))
    return pl.pallas_call(
        body,
        name=name,
        grid=(d // LANES,),
        in_specs=_sc_gate_specs(t, d) + [pl.BlockSpec((kw, LANES), lambda j: (0, j))],
        out_specs=blk,
        out_shape=jax.ShapeDtypeStruct((t, d), MXU_DTYPE),
        scratch_shapes=[pltpu.VMEM((CONV_PAD + t, LANES), F32)],
        compiler_params=_params(("parallel",)),
    )(u, u, u, conv_w)


def _sc_gate_bwd(dr, u, conv_w, *, name):
    t, d = u.shape[0], u.shape[1] // 3
    kw = conv_w.shape[0]
    ch = _conv_chunk(t)

    def body(dr_ref, b_ref, c_ref, h_ref, w_ref, db_ref, dp_ref, dw_ref, ppad_ref, dpad_ref):
        ppad_ref[0:CONV_PAD, :] = jnp.zeros((CONV_PAD, LANES), F32)
        ppad_ref[CONV_PAD:CONV_PAD + t, :] = c_ref[...] * h_ref[...]
        dpad_ref[0:t, :] = dr_ref[...] * b_ref[...]
        dpad_ref[t:t + CONV_PAD, :] = jnp.zeros((CONV_PAD, LANES), F32)
        for t0 in range(0, t, ch):
            q = jnp.zeros((ch, LANES), F32)
            dp = jnp.zeros((ch, LANES), F32)
            for k in range(kw):
                s = kw - 1 - k
                q = q + w_ref[k:k + 1, :] * ppad_ref[CONV_PAD - s + t0:CONV_PAD - s + t0 + ch, :]
                dp = dp + w_ref[k:k + 1, :] * dpad_ref[t0 + s:t0 + s + ch, :]
            db_ref[t0:t0 + ch, :] = (dr_ref[t0:t0 + ch, :] * q).astype(db_ref.dtype)
            dp_ref[t0:t0 + ch, :] = dp
        for k in range(kw):
            s = kw - 1 - k
            acc = jnp.zeros((ch, LANES), F32)
            for t0 in range(0, t, ch):
                acc = acc + dpad_ref[t0:t0 + ch, :] * ppad_ref[CONV_PAD - s + t0:CONV_PAD - s + t0 + ch, :]
            dw_ref[k:k + 1, :] = _colsum(acc)

    blk = pl.BlockSpec((t, LANES), lambda j: (0, j))
    wblk = pl.BlockSpec((kw, LANES), lambda j: (0, j))
    return pl.pallas_call(
        body,
        name=name,
        grid=(d // LANES,),
        in_specs=[blk] + _sc_gate_specs(t, d) + [wblk],
        out_specs=[blk, blk, wblk],
        out_shape=[jax.ShapeDtypeStruct((t, d), MXU_DTYPE), jax.ShapeDtypeStruct((t, d), F32),
                   jax.ShapeDtypeStruct((kw, d), F32)],
        scratch_shapes=[pltpu.VMEM((CONV_PAD + t, LANES), F32), pltpu.VMEM((CONV_PAD + t, LANES), F32)],
        compiler_params=_params(("parallel",)),
    )(dr, u, u, u, conv_w)


def _sc_fwd(x_m, w_in, conv_w, w_out, tag, after=()):
    u = _mm(x_m, w_in, name=f"{tag}_in", after=after)
    r = _sc_gate_fwd(u, conv_w, name=f"{tag}_gate")
    y = _mm(r, w_out, name=f"{tag}_out")
    return y, (x_m, u, r)


def _sc_bwd(dy_m, saved, w_in, conv_w, w_out, tag, residual, after=()):
    d = D_MODEL
    x_m, u, r = saved
    dw_out = _mm(r, dy_m, ta=True, name=f"{tag}_dwout", out_dtypes=(MXU_DTYPE,))
    dr = _mm(dy_m, w_out, tb=True, name=f"{tag}_dr", after=after)
    db, dp, dconv = _sc_gate_bwd(dr, u, conv_w, name=f"{tag}_dgate")

    def du_fn(rows, bc):
        dbv, dpv, uv = rows
        return (jnp.concatenate([dbv.astype(F32), dpv * uv[:, 2 * d:3 * d], dpv * uv[:, d:2 * d]], axis=1),), ()

    du = _rowwise(du_fn, [db, dp, u], [], [(3 * d, MXU_DTYPE)], [], name=f"{tag}_du")[0]
    dw_in = _mm(x_m, du, ta=True, out_dev=True, name=f"{tag}_dwin", out_dtypes=(MXU_DTYPE,))
    dx = _mm(du, w_in, tb=True, name=f"{tag}_dx", epilogue=_plus_residual, tile_extras=(residual,))
    return [dx], {"sc_w_in": dw_in, "sc_conv_w": dconv, "sc_w_out": dw_out}


def _cf_fwd(x_m, w, tag, after=()):
    d = D_MODEL

    def bias_epi(acc, bias):
        return (acc + bias,)

    u = _mm(x_m, w["cf_w_pw1"], name=f"{tag}_pw1", epilogue=bias_epi, col_extras=(w["cf_b_pw1"],),
            after=after)

    def glu_fn(rows, bc):
        uv = rows[0]
        return (uv[:, 0:d] * _sigmoid(uv[:, d:2 * d]),), ()

    h1 = _rowwise(glu_fn, [u], [], [(d, F32)], [], name=f"{tag}_glu")[0]
    h2 = _conv_fwd(h1, w["cf_dw_w"], w["cf_dw_b"], name=f"{tag}_conv")

    def norm_fn(rows, bc):
        hv = rows[0]
        mu = jnp.mean(hv, axis=-1, keepdims=True)
        hc = hv - mu
        rstd = lax.rsqrt(jnp.mean(hc * hc, axis=-1, keepdims=True) + LN_EPS)
        h3 = hc * rstd * bc[0] + bc[1]
        return (h3 * _sigmoid(h3),), ()

    h4 = _rowwise(norm_fn, [h2], [w["cf_norm_g"], w["cf_norm_b"]], [(d, MXU_DTYPE)], [], name=f"{tag}_norm")[0]
    y = _mm(h4, w["cf_w_pw2"], name=f"{tag}_pw2", epilogue=bias_epi, col_extras=(w["cf_b_pw2"],))
    return y, (x_m, u, h1, h2, h4)


def _cf_bwd(dy_m, dy_colsum, saved, w, tag, residual, after=()):
    d = D_MODEL
    x_m, u, h1, h2, h4 = saved
    dw_pw2 = _mm(h4, dy_m, ta=True, name=f"{tag}_dwpw2", out_dtypes=(MXU_DTYPE,))
    dh4 = _mm(dy_m, w["cf_w_pw2"], tb=True, name=f"{tag}_dh4", after=after)

    def dnorm_fn(rows, bc):
        dh4v, hv = rows
        g, b = bc
        mu = jnp.mean(hv, axis=-1, keepdims=True)
        hc = hv - mu
        rstd = lax.rsqrt(jnp.mean(hc * hc, axis=-1, keepdims=True) + LN_EPS)
        xh = hc * rstd
        h3 = xh * g + b
        sg = _sigmoid(h3)
        dh3 = dh4v * (sg * (1.0 + h3 * (1.0 - sg)))
        dxh = dh3 * g
        dh2 = rstd * (dxh - jnp.mean(dxh, axis=-1, keepdims=True) - xh * jnp.mean(dxh * xh, axis=-1, keepdims=True))
        return (dh2,), (_colsum(dh3 * xh), _colsum(dh3), _colsum(dh2))

    dh2, dnorm_g, dnorm_b, ddw_b = _rowwise(dnorm_fn, [dh4, h2], [w["cf_norm_g"], w["cf_norm_b"]], [(d, F32)],
                                            [(1, d), (1, d), (1, d)], name=f"{tag}_dnorm")
    dh1, ddw_w = _conv_bwd(dh2, h1, w["cf_dw_w"], name=f"{tag}_dconv")

    def dglu_fn(rows, bc):
        dh1v, uv = rows
        a, gate = uv[:, 0:d], uv[:, d:2 * d]
        sg = _sigmoid(gate)
        du = jnp.concatenate([dh1v * sg, dh1v * a * sg * (1.0 - sg)], axis=1)
        return (du,), (_colsum(du),)

    du, db_pw1 = _rowwise(dglu_fn, [dh1, u], [], [(2 * d, MXU_DTYPE)], [(1, 2 * d)], name=f"{tag}_dglu")
    dw_pw1 = _mm(x_m, du, ta=True, out_dev=True, name=f"{tag}_dwpw1", out_dtypes=(MXU_DTYPE,))
    dx = _mm(du, w["cf_w_pw1"], tb=True, name=f"{tag}_dx", epilogue=_plus_residual, tile_extras=(residual,))
    grads = {"cf_w_pw1": dw_pw1, "cf_b_pw1": db_pw1, "cf_dw_w": ddw_w, "cf_dw_b": ddw_b, "cf_norm_g": dnorm_g,
             "cf_norm_b": dnorm_b, "cf_w_pw2": dw_pw2, "cf_b_pw2": dy_colsum}
    return [dx], grads


def _rope_tables(t):
    pos = jnp.arange(t, dtype=F32)
    inv_freq = ROPE_THETA ** (-jnp.arange(0, QK_ROPE, 2, dtype=F32) / QK_ROPE)
    ang = pos[:, None] * inv_freq[None, :]
    return jnp.cos(ang), jnp.sin(ang)


ROPE_HALF = QK_ROPE // 2
Q_NOPE_COLS = MLA_HEADS * QK_NOPE
Q_HALF_COLS = MLA_HEADS * ROPE_HALF


def _interleave_heads(w_uk, w_uv):
    c = w_uk.shape[0]
    return jnp.concatenate([w_uk.reshape(c, MLA_HEADS, QK_NOPE), w_uv.reshape(c, MLA_HEADS, V_HEAD)],
                           axis=2).reshape(c, -1)


def _mla_fwd(x_m, w, tag, after=()):
    t = x_m.shape[0]
    cos, sin = _rope_tables(t)
    cos_h, sin_h = jnp.tile(cos, (1, MLA_HEADS)), jnp.tile(sin, (1, MLA_HEADS))
    cq_pre = _mm(x_m, w["mla_w_dq"], name=f"{tag}_dq", after=after)
    cq = _rms_fwd(cq_pre, w["mla_g_q"], name=f"{tag}_qnorm")
    q = _mm(cq, w["mla_w_uq"], name=f"{tag}_uq")
    r1, r2 = _rope(q[:, Q_NOPE_COLS:Q_NOPE_COLS + Q_HALF_COLS], q[:, Q_NOPE_COLS + Q_HALF_COLS:], cos_h, sin_h,
                   name=f"{tag}_qrope")
    qp_h = jnp.concatenate([r1.reshape(t, MLA_HEADS, ROPE_HALF), r2.reshape(t, MLA_HEADS, ROPE_HALF)],
                           axis=-1).transpose(1, 0, 2).astype(MXU_DTYPE)
    ckv_full = _mm(x_m, w["mla_w_dkv"], name=f"{tag}_dkv")
    ckv_pre = ckv_full[:, :KV_LORA]
    ckv = _rms_fwd(ckv_pre, w["mla_g_kv"], name=f"{tag}_kvnorm")
    k1, k2 = _rope(ckv_full[:, KV_LORA:KV_LORA + ROPE_HALF], ckv_full[:, KV_LORA + ROPE_HALF:], cos, sin,
                   name=f"{tag}_krope")
    kp = jnp.concatenate([k1, k2], axis=-1).astype(MXU_DTYPE)
    kv = _mm(ckv, _interleave_heads(w["mla_w_uk"], w["mla_w_uv"]), name=f"{tag}_ukv", out_dtypes=(MXU_DTYPE,))
    o, o_f, lse = _attn_fwd(q, qp_h, kv, kp, name=f"{tag}_attn")
    y = _mm(o, w["mla_w_o"], name=f"{tag}_o")
    return y, (x_m, cq_pre, cq, ckv_pre, ckv, q, qp_h, kv, kp, lse, o, o_f, (cos, sin, cos_h, sin_h))


def _mla_bwd(dy_m, saved, w, tag, residual, after=()):
    x_m, cq_pre, cq, ckv_pre, ckv, q, qp_h, kv, kp, lse, o, o_f, (cos, sin, cos_h, sin_h) = saved
    t = x_m.shape[0]
    half = ROPE_HALF
    dw_o = _mm(o, dy_m, ta=True, name=f"{tag}_dwo", out_dtypes=(MXU_DTYPE,))
    do = _mm(dy_m, w["mla_w_o"], tb=True, name=f"{tag}_do", out_dtypes=(MXU_DTYPE,), after=after)
    dqn, dqp_h, dkv, dkp = _attn_bwd(q, qp_h, kv, kp, do, o_f, lse, name=f"{tag}_dattn")
    dqp = dqp_h.transpose(1, 0, 2)
    d1, d2 = _rope(dqp[:, :, :half].reshape(t, -1), dqp[:, :, half:].reshape(t, -1), cos_h, sin_h,
                   name=f"{tag}_dqrope", inverse=True)
    dq = jnp.concatenate([dqn, d1, d2], axis=-1).astype(MXU_DTYPE)
    dw_uq = _mm(cq, dq, ta=True, name=f"{tag}_dwuq", out_dtypes=(MXU_DTYPE,))
    dcq = _mm(dq, w["mla_w_uq"], tb=True, name=f"{tag}_dcq")
    dcq_pre, dg_q = _rms_bwd(cq_pre, w["mla_g_q"], dcq, name=f"{tag}_dqnorm")
    dcq_pre_m = dcq_pre.astype(MXU_DTYPE)
    dw_dq = _mm(x_m, dcq_pre_m, ta=True, name=f"{tag}_dwdq", out_dtypes=(MXU_DTYPE,))
    dx_q = _mm(dcq_pre_m, w["mla_w_dq"], tb=True, name=f"{tag}_dxq", epilogue=_plus_residual, tile_extras=(residual,))
    dw_kv = _mm(ckv, dkv, ta=True, name=f"{tag}_dwukv", out_dtypes=(MXU_DTYPE,))
    dw_kv = dw_kv.reshape(KV_LORA, MLA_HEADS, QK_NOPE + V_HEAD)
    dw_uk, dw_uv = dw_kv[:, :, :QK_NOPE].reshape(KV_LORA, -1), dw_kv[:, :, QK_NOPE:].reshape(KV_LORA, -1)
    dckv = _mm(dkv, _interleave_heads(w["mla_w_uk"], w["mla_w_uv"]), tb=True, name=f"{tag}_dckv")
    dckv_pre, dg_kv = _rms_bwd(ckv_pre, w["mla_g_kv"], dckv, name=f"{tag}_dkvnorm")
    dk1, dk2 = _rope(dkp[:, :half], dkp[:, half:], cos, sin, name=f"{tag}_dkrope", inverse=True)
    dckv_full = jnp.concatenate([dckv_pre, dk1, dk2], axis=-1).astype(MXU_DTYPE)
    dw_dkv = _mm(x_m, dckv_full, ta=True, name=f"{tag}_dwdkv", out_dtypes=(MXU_DTYPE,))
    dx_kv = _mm(dckv_full, w["mla_w_dkv"], tb=True, name=f"{tag}_dxkv", epilogue=lambda acc, other: (acc + other,),
                tile_extras=(dx_q,))
    grads = {"mla_w_dq": dw_dq, "mla_g_q": dg_q, "mla_w_uq": dw_uq, "mla_w_dkv": dw_dkv, "mla_g_kv": dg_kv,
             "mla_w_uk": dw_uk, "mla_w_uv": dw_uv, "mla_w_o": dw_o}
    return [dx_kv], grads


def _cast_input(x):
    cast = lambda rows, bc: ((rows[0],), ())
    return _rowwise(cast, [x], [], [(D_MODEL, MXU_DTYPE)], [], name="x_cast")[0]


def _hooked(value, hook):
    token = None if hook is None else hook(value)
    return () if token is None else (token,)


def _mix_fwd(i, cur, cur_m, wl, hook=None, after=()):
    mixer = i % N_MIXERS
    if mixer == 0:
        y, s_mix = _sc_fwd(cur_m, wl["sc_w_in"], wl["sc_conv_w"], wl["sc_w_out"], f"l{i}_sc", after=after)
    elif mixer == 1:
        y, s_mix = _mla_fwd(cur_m, wl, f"l{i}_mla", after=after)
    else:
        y, s_mix = _cf_fwd(cur_m, wl, f"l{i}_cf", after=after)
    cur, cur_m, xh_mix, rs_mix = _ln_fwd(cur, y, wl["ln_mix_g"], wl["ln_mix_b"], name=f"l{i}_lnmix",
                                         after=_hooked(y, hook))
    return cur, cur_m, (s_mix, xh_mix, rs_mix)


def _ff_fwd(i, cur, cur_m, wl, hook=None, after=()):
    y, s_ff = _mlp_fwd(cur_m, wl["ff_w1"], wl["ff_w2"], f"l{i}_ff", after=after)
    cur, cur_m, xh_ff, rs_ff = _ln_fwd(cur, y, wl["ln_ff_g"], wl["ln_ff_b"], name=f"l{i}_lnff", after=_hooked(y, hook))
    return cur, cur_m, (s_ff, xh_ff, rs_ff)


def _layer_fwd(i, cur, cur_m, wl):
    cur, cur_m, s_mix = _mix_fwd(i, cur, cur_m, wl)
    cur, cur_m, s_ff = _ff_fwd(i, cur, cur_m, wl)
    return cur, cur_m, (*s_mix, *s_ff)


def _loss_and_grad(out, target):
    d = D_MODEL

    def loss_fn(rows, bc):
        err = rows[0] - rows[1]
        return (err * (1.0 / d),), (_colsum(err * err),)

    dy, sq = _rowwise(loss_fn, [out, target], [], [(d, F32)], [(1, d)], name="loss")
    return (0.5 / d) * jnp.sum(sq), dy


def _ff_bwd(i, terms, saved, wl, hook=None, after=()):
    s_ff, xh_ff, rs_ff = saved
    grads = {}
    dz, dz_m, grads["ln_ff_g"], grads["ln_ff_b"], _ = _ln_bwd(terms, xh_ff, rs_ff, wl["ln_ff_g"], name=f"l{i}_dlnff",
                                                              after=after)
    dxs, g_ff = _mlp_bwd(dz_m, s_ff, wl["ff_w1"], wl["ff_w2"], f"l{i}_ff", dz, after=_hooked(dz, hook))
    grads.update(g_ff)
    return [(1.0, v) for v in dxs], grads


def _mix_bwd(i, terms, saved, wl, hook=None, after=()):
    mixer = i % N_MIXERS
    s_mix, xh_mix, rs_mix = saved
    grads = {}
    dz, dz_m, grads["ln_mix_g"], grads["ln_mix_b"], dz_sum = _ln_bwd(terms, xh_mix, rs_mix, wl["ln_mix_g"],
                                                                     name=f"l{i}_dlnmix", after=after)
    behind = _hooked(dz, hook)
    if mixer == 0:
        dxs, g_mix = _sc_bwd(dz_m, s_mix, wl["sc_w_in"], wl["sc_conv_w"], wl["sc_w_out"], f"l{i}_sc", dz, after=behind)
    elif mixer == 1:
        dxs, g_mix = _mla_bwd(dz_m, s_mix, wl, f"l{i}_mla", dz, after=behind)
    else:
        dxs, g_mix = _cf_bwd(dz_m, dz_sum, s_mix, wl, f"l{i}_cf", dz, after=behind)
    grads.update(g_mix)
    return [(1.0, v) for v in dxs], grads


def _layer_bwd(i, terms, saved, wl):
    terms, g_ff = _ff_bwd(i, terms, saved[3:], wl)
    terms, g_mix = _mix_bwd(i, terms, saved[:3], wl)
    return terms, {**g_ff, **g_mix}


def _sum_terms(terms, *, name):
    coefs = [c for c, _ in terms]

    def sum_fn(rows, bc):
        acc = None
        for c, v in zip(coefs, rows):
            tv = v if c == 1.0 else c * v
            acc = tv if acc is None else acc + tv
        return (acc,), ()

    return _rowwise(sum_fn, [v for _, v in terms], [], [(D_MODEL, F32)], [], name=name)[0]


def _local_step(x, target, layers):
    cur, cur_m = x, _cast_input(x)
    saved = []
    for i in range(DEPTH):
        cur, cur_m, s = _layer_fwd(i, cur, cur_m, layers[i])
        saved.append(s)
    loss, dy = _loss_and_grad(cur, target)
    terms = [(1.0, dy)]
    grads = [None] * DEPTH
    for i in reversed(range(DEPTH)):
        terms, grads[i] = _layer_bwd(i, terms, saved[i], layers[i])
    return loss, _sum_terms(terms, name="grad_x"), grads


MESH_ID = pl.DeviceIdType.MESH
ANY = pl.BlockSpec(memory_space=pl.ANY)


def _block_id(px, py, pc):
    return 4 * px + 2 * py + pc


COPIES_PER_GATHER = 7


def _all_gather(shards, *, name):
    n = len(shards)

    def body(*refs):
        x_refs, out_refs = refs[:n], refs[n:2 * n]
        send_sems, recv_sems, local_sems = refs[2 * n:]
        x, y, core = lax.axis_index("x"), lax.axis_index("y"), lax.axis_index("c")
        me, sibling = (x, y, core), (x, y, 1 - core)
        chips = [(1 - x, y), (x, 1 - y), (1 - x, 1 - y)]

        def copy(t, k, block, to, from_input=False):
            dst = out_refs[t].at[_block_id(*block)]
            return pltpu.make_async_remote_copy(
                src_ref=x_refs[t] if from_input else dst, dst_ref=dst,
                send_sem=send_sems.at[t * COPIES_PER_GATHER + k], recv_sem=recv_sems.at[t * COPIES_PER_GATHER + k],
                device_id=to, device_id_type=MESH_ID)

        started = []
        mine = [pltpu.make_async_copy(x_refs[t], out_refs[t].at[_block_id(*me)], local_sems.at[t]) for t in range(n)]
        for t in range(n):
            mine[t].start()
            first = [copy(t, 0, me, sibling, from_input=True)]
            first += [copy(t, 1 + j, me, (*chip, core), from_input=True) for j, chip in enumerate(chips)]
            for cp in first:
                cp.start()
            started += first
        for t in range(n):
            for j, chip in enumerate(chips):
                copy(t, 1 + j, (*chip, core), me).wait_recv()
                passed = copy(t, 4 + j, (*chip, core), sibling)
                passed.start()
                started.append(passed)
        for t in range(n):
            copy(t, 0, sibling, me).wait_recv()
            for j, chip in enumerate(chips):
                copy(t, 4 + j, (*chip, 1 - core), me).wait_recv()
        for cp in started:
            cp.wait_send()
        for cp in mine:
            cp.wait()

    n_sems = n * COPIES_PER_GATHER
    outs = pl.pallas_call(
        body,
        name=name,
        out_shape=[jax.ShapeDtypeStruct((N_DEV, *s.shape), s.dtype) for s in shards],
        in_specs=[ANY] * n,
        out_specs=[ANY] * n,
        scratch_shapes=[pltpu.SemaphoreType.DMA((n_sems,)), pltpu.SemaphoreType.DMA((n_sems,)),
                        pltpu.SemaphoreType.DMA((n,))],
    )(*shards)
    return list(outs)


def _swap_with_sibling(parts, *, name):
    n = len(parts)

    def body(*refs):
        p_refs, got_refs = refs[:n], refs[n:2 * n]
        send_sems, recv_sems = refs[2 * n:]
        x, y, core = lax.axis_index("x"), lax.axis_index("y"), lax.axis_index("c")
        chips = [(x, y), (1 - x, y), (x, 1 - y), (1 - x, 1 - y)]
        copies = []
        for t in range(n):
            for k, (px, py) in enumerate(chips):
                copies.append(pltpu.make_async_remote_copy(
                    src_ref=p_refs[t].at[_block_id(px, py, 1 - core)], dst_ref=got_refs[t].at[k],
                    send_sem=send_sems.at[4 * t + k], recv_sem=recv_sems.at[4 * t + k], device_id=(x, y, 1 - core),
                    device_id_type=MESH_ID))
        for cp in copies:
            cp.start()
        for cp in copies:
            cp.wait()

    outs = pl.pallas_call(
        body,
        name=name,
        out_shape=[jax.ShapeDtypeStruct((4, *p.shape[1:]), p.dtype) for p in parts],
        in_specs=[ANY] * n,
        out_specs=[ANY] * n,
        scratch_shapes=[pltpu.SemaphoreType.DMA((4 * n,)), pltpu.SemaphoreType.DMA((4 * n,))],
    )(*parts)
    return list(outs)


def _swap_between_chips(parts, *, name):
    n = len(parts)

    def body(*refs):
        p_refs, got_refs = refs[:n], refs[n:2 * n]
        send_sems, recv_sems = refs[2 * n:]
        x, y, core = lax.axis_index("x"), lax.axis_index("y"), lax.axis_index("c")
        chips = [(1 - x, y), (x, 1 - y), (1 - x, 1 - y)]
        copies = []
        for t in range(n):
            for j, (px, py) in enumerate(chips):
                copies.append(pltpu.make_async_remote_copy(
                    src_ref=p_refs[t].at[j], dst_ref=got_refs[t].at[j], send_sem=send_sems.at[3 * t + j],
                    recv_sem=recv_sems.at[3 * t + j], device_id=(px, py, core), device_id_type=MESH_ID))
        for cp in copies:
            cp.start()
        for cp in copies:
            cp.wait()

    outs = pl.pallas_call(
        body,
        name=name,
        out_shape=[jax.ShapeDtypeStruct(p.shape, p.dtype) for p in parts],
        in_specs=[ANY] * n,
        out_specs=[ANY] * n,
        scratch_shapes=[pltpu.SemaphoreType.DMA((3 * n,)), pltpu.SemaphoreType.DMA((3 * n,))],
    )(*parts)
    return list(outs)


def _chip_partials(parts, got, *, name):
    _, r, c = parts.shape
    tb = _row_tile(r, 512)
    x, y, core = lax.axis_index("x"), lax.axis_index("y"), lax.axis_index("c")
    chips = [(x, y), (1 - x, y), (x, 1 - y), (1 - x, 1 - y)]
    ids = jnp.stack([_block_id(px, py, core) for px, py in chips]).astype(jnp.int32)

    def body(ids_ref, p0, p1, p2, p3, g_ref, own_ref, out_ref):
        own_ref[...] = p0[...].astype(F32) + g_ref[0].astype(F32)
        for j, p in enumerate((p1, p2, p3)):
            out_ref[j] = (p[...].astype(F32) + g_ref[j + 1].astype(F32)).astype(out_ref.dtype)

    def part_spec(k):
        return pl.BlockSpec((None, tb, c), lambda i, ids_ref: (ids_ref[k], i, 0))

    grid_spec = pltpu.PrefetchScalarGridSpec(
        num_scalar_prefetch=1,
        grid=(r // tb,),
        in_specs=[part_spec(k) for k in range(4)] + [pl.BlockSpec((4, tb, c), lambda i, ids_ref: (0, i, 0))],
        out_specs=[pl.BlockSpec((tb, c), lambda i, ids_ref: (i, 0)), pl.BlockSpec((3, tb, c), lambda i, ids_ref: (0, i, 0))],
    )
    return pl.pallas_call(
        body,
        name=name,
        grid_spec=grid_spec,
        out_shape=[jax.ShapeDtypeStruct((r, c), F32), jax.ShapeDtypeStruct((3, r, c), parts.dtype)],
        compiler_params=_params(("parallel",)),
    )(ids, parts, parts, parts, parts, got)


def _chip_partials_group(parts, got, *, name):
    n = len(parts)
    rows = [p.shape[1] for p in parts]
    steps = 4 if all(r % (4 * BF16_ROWS) == 0 for r in rows) else (2 if all(r % (2 * BF16_ROWS) == 0 for r in rows) else 1)
    x, y, core = lax.axis_index("x"), lax.axis_index("y"), lax.axis_index("c")
    chips = [(x, y), (1 - x, y), (x, 1 - y), (1 - x, 1 - y)]
    ids = jnp.stack([_block_id(px, py, core) for px, py in chips]).astype(jnp.int32)

    def body(ids_ref, *refs):
        ins, outs = refs[:5 * n], refs[5 * n:]
        for t in range(n):
            p0, p1, p2, p3, g_ref = ins[5 * t:5 * t + 5]
            own_ref, out_ref = outs[2 * t], outs[2 * t + 1]
            own_ref[...] = p0[...].astype(F32) + g_ref[0].astype(F32)
            for j, p in enumerate((p1, p2, p3)):
                out_ref[j] = (p[...].astype(F32) + g_ref[j + 1].astype(F32)).astype(out_ref.dtype)

    in_specs, out_specs, out_shape, operands = [], [], [], []
    for p, g in zip(parts, got):
        _, r, c = p.shape
        tb = r // steps
        in_specs += [pl.BlockSpec((None, tb, c), lambda i, ids_ref, k=k: (ids_ref[k], i, 0)) for k in range(4)]
        in_specs.append(pl.BlockSpec((4, tb, c), lambda i, ids_ref: (0, i, 0)))
        out_specs += [pl.BlockSpec((tb, c), lambda i, ids_ref: (i, 0)), pl.BlockSpec((3, tb, c), lambda i, ids_ref: (0, i, 0))]
        out_shape += [jax.ShapeDtypeStruct((r, c), F32), jax.ShapeDtypeStruct((3, r, c), p.dtype)]
        operands += [p, p, p, p, g]
    outs = pl.pallas_call(
        body,
        name=name,
        grid_spec=pltpu.PrefetchScalarGridSpec(num_scalar_prefetch=1, grid=(steps,), in_specs=in_specs,
                                               out_specs=out_specs),
        out_shape=out_shape,
        compiler_params=_params(("parallel",)),
    )(ids, *operands)
    return [(outs[2 * t], outs[2 * t + 1]) for t in range(n)]


def _sum_blocks(first, blocks, *, name):
    n, r, c = blocks.shape
    tb = _row_tile(r, 512)

    def body(*refs):
        b_ref, o_ref = refs[-2], refs[-1]
        acc = refs[0][...] if first is not None else b_ref[0].astype(F32)
        for j in range(0 if first is not None else 1, n):
            acc = acc + b_ref[j].astype(F32)
        o_ref[...] = acc

    row_spec = pl.BlockSpec((tb, c), lambda i: (i, 0))
    return pl.pallas_call(
        body,
        name=name,
        grid=(r // tb,),
        in_specs=([row_spec] if first is not None else []) + [pl.BlockSpec((n, tb, c), lambda i: (0, i, 0))],
        out_specs=row_spec,
        out_shape=jax.ShapeDtypeStruct((r, c), F32),
        compiler_params=_params(("parallel",)),
    )(*([first] if first is not None else []), blocks)


def _reduce_scatter(parts, *, tag):
    got = _swap_with_sibling(parts, name=f"{tag}_sibling")
    sums = [_chip_partials(p, g, name=f"{tag}_chipsum{t}") for t, (p, g) in enumerate(zip(parts, got))]
    arrived = _swap_between_chips([others for _, others in sums], name=f"{tag}_chips")
    return [(own, arr) for (own, _), arr in zip(sums, arrived)]


HBM = pl.BlockSpec(memory_space=pltpu.HBM)
SEM = pl.BlockSpec(memory_space=pltpu.SEMAPHORE)
DATAFLOW = pltpu.SideEffectType.DATAFLOW_SIDE_EFFECTING


def _tie(value, *before):
    return lax.optimization_barrier((value, *before))[0]


def _split_call(name, arrays, sems, n_new, body, after=None):
    n_a, n_s = len(arrays), len(sems)
    after = [] if after is None else (list(after) if isinstance(after, (list, tuple)) else [after])
    n_after = len(after)

    def kernel_body(*refs):
        new = refs[n_a + n_s + n_after:n_a + n_s + n_after + 2] if n_new else (None, None)
        body(refs[:n_a], refs[n_a:n_a + n_s], *new)
        refs[-1][...] = jnp.zeros_like(refs[-1])

    n_sem_out = 2 if n_new else 0
    out_shape = [pltpu.SemaphoreType.DMA((n_new,))] * n_sem_out
    out_shape += [pltpu.HBM(a.shape, a.dtype) for a in arrays]
    out_shape += [jax.ShapeDtypeStruct((SUBLANES, LANES), F32)]
    outs = pl.pallas_call(
        kernel_body,
        name=name,
        out_shape=out_shape,
        in_specs=[HBM] * n_a + [SEM] * n_s + [ANY] * n_after,
        out_specs=[SEM] * n_sem_out + [HBM] * n_a + [pl.BlockSpec(memory_space=pltpu.VMEM)],
        input_output_aliases={i: n_sem_out + i for i in range(n_a)},
        compiler_params=pltpu.CompilerParams(has_side_effects=DATAFLOW),
    )(*[pltpu.with_memory_space_constraint(a, pltpu.HBM) for a in arrays], *sems, *after)
    return tuple(outs[:n_sem_out]), list(outs[n_sem_out:n_sem_out + n_a]), outs[-1]


def _remote(src, dst, send, recv, k, to):
    return pltpu.make_async_remote_copy(src_ref=src, dst_ref=dst, send_sem=send.at[k], recv_sem=recv.at[k],
                                        device_id=to, device_id_type=MESH_ID)


def _land_block(land_ref, block, by_cols, width):
    if by_cols:
        return land_ref.at[:, pl.ds(pl.multiple_of(block * width, LANES), width)]
    return land_ref.at[block]


def _place_in_window(shard, *, name):
    r, c = shard.shape
    tb = _row_tile(r, 512)
    me = _block_id(lax.axis_index("x"), lax.axis_index("y"), lax.axis_index("c")).astype(jnp.int32).reshape(1)

    def body(me_ref, s_ref, o_ref, token_ref):
        o_ref[...] = s_ref[...].astype(o_ref.dtype)
        token_ref[...] = jnp.zeros_like(token_ref)

    return pl.pallas_call(
        body,
        name=name,
        grid_spec=pltpu.PrefetchScalarGridSpec(
            num_scalar_prefetch=1, grid=(r // tb,),
            in_specs=[pl.BlockSpec((tb, c), lambda i, me_ref: (i, 0))],
            out_specs=[pl.BlockSpec((tb, c), lambda i, me_ref: (i, me_ref[0])),
                       pl.BlockSpec((SUBLANES, LANES), lambda i, me_ref: (0, 0))]),
        out_shape=[jax.ShapeDtypeStruct((r, N_DEV * c), MXU_DTYPE), jax.ShapeDtypeStruct((SUBLANES, LANES), F32)],
        compiler_params=_params(("arbitrary",)),
    )(me, shard)


def _gather_prepare(shards, by_cols, *, name):
    me = _block_id(lax.axis_index("x"), lax.axis_index("y"), lax.axis_index("c"))
    lands, early = [], []
    for i, (s, cols) in enumerate(zip(shards, by_cols)):
        if cols:
            land, token = _place_in_window(s, name=f"{name}_place{i}")
            lands.append(land)
            early.append(token)
        else:
            lands.append(lax.dynamic_update_index_in_dim(lax.empty((N_DEV, *s.shape), s.dtype), s, me, 0))
            early.append(s)
    return lands, [s.shape[-1] for s in shards], list(by_cols), early


def _gather_start(shards, *, name, after=None, by_cols=None, prepared=None):
    if prepared is None:
        prepared = _gather_prepare(shards, by_cols if by_cols is not None else [False] * len(shards), name=name)
    lands, widths, by_cols, _ = prepared
    n = len(lands)

    def body(refs, _, send, recv):
        x, y, core = lax.axis_index("x"), lax.axis_index("y"), lax.axis_index("c")
        targets = [(x, y, 1 - core), (1 - x, y, core), (x, 1 - y, core), (1 - x, 1 - y, core)]
        for t in range(n):
            mine = _land_block(refs[t], _block_id(x, y, core), by_cols[t], widths[t])
            for k, to in enumerate(targets):
                _remote(mine, mine, send, recv, 4 * t + k, to).start()

    sems, thru, token = _split_call(name, lands, [], 4 * n, body, after=after)
    return {"n": n, "sems1": sems, "arrays": thru, "token": token, "by_cols": by_cols, "widths": widths}


def _gather_pass_on(st, *, name, after):
    n = st["n"]

    def body(refs, sems, send, recv):
        send1, recv1 = sems
        x, y, core = lax.axis_index("x"), lax.axis_index("y"), lax.axis_index("c")
        chips = [(1 - x, y), (x, 1 - y), (1 - x, 1 - y)]
        for t in range(n):
            for j, (px, py) in enumerate(chips):
                block = _land_block(refs[t], _block_id(px, py, core), st["by_cols"][t], st["widths"][t])
                _remote(block, block, send1, recv1, 4 * t + 1 + j, (x, y, 1 - core)).wait_recv()
                _remote(block, block, send, recv, 3 * t + j, (x, y, 1 - core)).start()

    sems2, thru, token = _split_call(name, st["arrays"], list(st["sems1"]), 3 * n, body, after=after)
    return {**st, "sems2": sems2, "arrays": thru, "token": token}


def _gather_finish(st, *, name, after):
    n = st["n"]

    def body(refs, sems, *_):
        send1, recv1, send2, recv2 = sems
        x, y, core = lax.axis_index("x"), lax.axis_index("y"), lax.axis_index("c")
        sibling = (x, y, 1 - core)
        chips = [(1 - x, y), (x, 1 - y), (1 - x, 1 - y)]
        for t in range(n):
            place = lambda px, py, pc, t=t: _land_block(refs[t], _block_id(px, py, pc), st["by_cols"][t],
                                                        st["widths"][t])
            mine = place(x, y, core)
            for k in range(4):
                _remote(mine, mine, send1, recv1, 4 * t + k, sibling).wait_send()
            _remote(mine, place(x, y, 1 - core), send1, recv1, 4 * t, sibling).wait_recv()
            for j, (px, py) in enumerate(chips):
                _remote(place(px, py, core), place(px, py, core), send2, recv2, 3 * t + j, sibling).wait_send()
                _remote(mine, place(px, py, 1 - core), send2, recv2, 3 * t + j, sibling).wait_recv()

    _, thru, _ = _split_call(name, st["arrays"], [*st["sems1"], *st["sems2"]], 0, body, after=after)
    return thru


def _swap_start(sources, lands, plan, per_array, *, name):
    n = len(sources)

    def body(refs, _, send, recv):
        x, y, core = lax.axis_index("x"), lax.axis_index("y"), lax.axis_index("c")
        for t in range(n):
            for k in range(per_array):
                block, to = plan(x, y, core, k)
                _remote(refs[t].at[block], refs[n + t].at[k], send, recv, per_array * t + k, to).start()

    sems, thru, token = _split_call(name, sources + lands, [], per_array * n, body)
    return {"n": n, "sems": sems, "arrays": thru, "token": token, "plan": plan, "per_array": per_array}


def _swap_finish(st, *, name, after):
    n, plan, per_array = st["n"], st["plan"], st["per_array"]

    def body(refs, sems, *_):
        send, recv = sems
        x, y, core = lax.axis_index("x"), lax.axis_index("y"), lax.axis_index("c")
        for t in range(n):
            for k in range(per_array):
                block, to = plan(x, y, core, k)
                cp = _remote(refs[t].at[block], refs[n + t].at[k], send, recv, per_array * t + k, to)
                cp.wait_send()
                cp.wait_recv()

    _, thru, _ = _split_call(name, st["arrays"], list(st["sems"]), 0, body, after=after)
    return thru[:n], thru[n:]


def _to_sibling(x, y, core, k):
    px, py = [(x, y), (1 - x, y), (x, 1 - y), (1 - x, 1 - y)][k]
    return _block_id(px, py, 1 - core), (x, y, 1 - core)


def _to_chip(x, y, core, k):
    px, py = [(1 - x, y), (x, 1 - y), (1 - x, 1 - y)][k]
    return k, (px, py, core)


PARAMS = {
    "sc_w_in": ((2, 1024, 3072), 2), "sc_conv_w": ((2, 3, 1024), 2), "sc_w_out": ((2, 1024, 1024), 1),
    "mla_w_dq": ((1, 1024, 384), 1), "mla_g_q": ((1, 384), None), "mla_w_uq": ((1, 384, 1536), 2),
    "mla_w_dkv": ((1, 1024, 320), 1), "mla_g_kv": ((1, 256), None), "mla_w_uk": ((1, 256, 8, 128), 1),
    "mla_w_uv": ((1, 256, 8, 128), 1), "mla_w_o": ((1, 1024, 1024), 1), "cf_w_pw1": ((1, 1024, 2048), 2),
    "cf_b_pw1": ((1, 2048), 1), "cf_dw_w": ((1, 31, 1024), 2), "cf_dw_b": ((1, 1024), 1), "cf_norm_g": ((1, 1024), 1),
    "cf_norm_b": ((1, 1024), 1), "cf_w_pw2": ((1, 1024, 1024), 1), "cf_b_pw2": ((1, 1024), 1),
    "ff_w1": ((4, 1024, 4096), 2), "ff_w2": ((4, 4096, 1024), 1), "ln_mix_g": ((4, 1024), None),
    "ln_mix_b": ((4, 1024), None), "ln_ff_g": ((4, 1024), None), "ln_ff_b": ((4, 1024), None),
}
NAMES = list(PARAMS)
BIG = ["sc_w_in", "sc_w_out", "mla_w_dq", "mla_w_uq", "mla_w_dkv", "mla_w_uk", "mla_w_uv", "mla_w_o", "cf_w_pw1",
       "cf_w_pw2", "ff_w1", "ff_w2"]
SMALL = [n for n in NAMES if n not in BIG]
SMALL_SHARDED = [n for n in SMALL if PARAMS[n][1] is not None]
SMALL_WIDTH = LANES
DEV_BLOCKED = ("sc_w_in", "cf_w_pw1", "ff_w1")
MIXER_BIG = {0: ["sc_w_in", "sc_w_out"], 1: ["mla_w_dq", "mla_w_uq", "mla_w_dkv", "mla_w_uk", "mla_w_uv", "mla_w_o"],
             2: ["cf_w_pw1", "cf_w_pw2"]}


def _layer_big(i):
    return [(n, i // N_MIXERS) for n in MIXER_BIG[i % N_MIXERS]] + [("ff_w1", i), ("ff_w2", i)]


def _shard_2d(name, a):
    return a.reshape(a.shape[0], -1)


def _shard_for_gather(name, a):
    flat = _shard_2d(name, a)
    return flat if name in DEV_BLOCKED else flat.astype(MXU_DTYPE)


def _as_operand(name, g):
    if name in DEV_BLOCKED:
        return g
    if name == "mla_w_uq":
        by_head = g.transpose(1, 0, 2)
        pieces = [by_head[:, :, :QK_NOPE], by_head[:, :, QK_NOPE:QK_NOPE + ROPE_HALF], by_head[:, :, QK_NOPE + ROPE_HALF:]]
        return jnp.concatenate([p.reshape(g.shape[1], -1) for p in pieces], axis=1)
    return g.reshape(-1, g.shape[2])


def _as_parts(name, grad):
    if name in DEV_BLOCKED:
        return grad
    if name == "mla_w_uq":
        rows = grad.shape[0]
        pieces = [grad[:, :Q_NOPE_COLS].reshape(rows, MLA_HEADS, QK_NOPE),
                  grad[:, Q_NOPE_COLS:Q_NOPE_COLS + Q_HALF_COLS].reshape(rows, MLA_HEADS, ROPE_HALF),
                  grad[:, Q_NOPE_COLS + Q_HALF_COLS:].reshape(rows, MLA_HEADS, ROPE_HALF)]
        return jnp.concatenate(pieces, axis=2).transpose(1, 0, 2)
    return grad.reshape(N_DEV, -1, grad.shape[1])


def _shard_shape(name):
    shape, ax = PARAMS[name]
    if ax is None:
        return shape
    return tuple(s // N_DEV if i == ax else s for i, s in enumerate(shape))


def _rows_of(shape, width, align):
    n = 1
    for s in shape:
        n *= s
    rows = -(-n // width)
    return -(-rows // align) * align


def _pack_rows(arrays, width, align, dtype, lead=0):
    segs = []
    for a in arrays:
        batch = a.shape[:lead]
        flat = a.astype(dtype).reshape(*batch, -1)
        rows = _rows_of(a.shape[lead:], width, align)
        flat = jnp.pad(flat, [(0, 0)] * lead + [(0, rows * width - flat.shape[-1])])
        segs.append(flat.reshape(*batch, rows, width))
    return jnp.concatenate(segs, axis=lead)


def _unpack_rows(buf, shapes, width, align):
    out, off = [], 0
    lead = buf.shape[:-2]
    for shape in shapes:
        n = 1
        for s in shape:
            n *= s
        rows = _rows_of(shape, width, align)
        seg = buf[..., off:off + rows, :].reshape(*lead, rows * width)[..., :n]
        out.append(seg.reshape(*lead, *shape))
        off += rows
    return out


def _merge_shards(stacked, ax):
    moved = jnp.moveaxis(stacked, 0, ax)
    shape = moved.shape
    return moved.reshape(*shape[:ax], shape[ax] * shape[ax + 1], *shape[ax + 2:])


def _split_shards(full, ax):
    shape = full.shape
    split = full.reshape(*shape[:ax], N_DEV, shape[ax] // N_DEV, *shape[ax + 1:])
    return jnp.moveaxis(split, ax, 0)


def _adam_math(wv, gv, mv, vv):
    m_new = ADAM_B1 * mv + (1.0 - ADAM_B1) * gv
    v_new = ADAM_B2 * vv + (1.0 - ADAM_B2) * (gv * gv)
    m_hat = m_new / (1.0 - ADAM_B1 ** ADAM_STEP)
    v_hat = v_new / (1.0 - ADAM_B2 ** ADAM_STEP)
    delta = -ADAM_LR * (m_hat / (jnp.sqrt(v_hat) + ADAM_EPS) + ADAM_WD * wv)
    return delta, m_new, v_new


def _adamw(w, g, m, v, *, name):
    shape = w.shape
    c = shape[-1]

    def fn(rows, bc):
        return _adam_math(*rows), ()

    flat = [a.reshape(-1, c) for a in (w, g, m, v)]
    outs = _rowwise(fn, flat, [], [(c, F32)] * 3, [], name=name, tb_cap=512)
    return tuple(o.reshape(shape) for o in outs)


def _adamw_reduced(w, m, v, layer_grads, *, name, after=()):
    n_l, r, c = w.shape
    tb = _row_tile(r, 256 if c > 512 else 512)
    nb = r // tb

    def body(*refs):
        w_ref, m_ref, v_ref = refs[:3]
        g_refs = refs[3:3 + 2 * n_l]
        g_out, d_out, m_out, v_out = refs[3 + 2 * n_l + len(after):]
        layer = pl.program_id(0)
        for l in range(n_l):
            @pl.when(layer == l)
            def _(l=l):
                own, arr = g_refs[2 * l], g_refs[2 * l + 1]
                g = own[...] + arr[0].astype(F32)
                g = g + arr[1].astype(F32)
                g = g + arr[2].astype(F32)
                delta, m_new, v_new = _adam_math(w_ref[...], g, m_ref[...], v_ref[...])
                g_out[...] = g
                d_out[...] = delta
                m_out[...] = m_new
                v_out[...] = v_new

    stacked = pl.BlockSpec((tb, c), lambda l, i: (l * nb + i, 0))
    in_specs = [stacked] * 3
    operands = [a.reshape(n_l * r, c) for a in (w, m, v)]
    for l, (own, arr) in enumerate(layer_grads):
        in_specs.append(pl.BlockSpec((tb, c), lambda ll, i, l=l: (jnp.where(ll == l, i, 0), 0)))
        in_specs.append(pl.BlockSpec((3, tb, c), lambda ll, i, l=l: (0, jnp.where(ll == l, i, 0), 0)))
        operands += [own, arr]
    in_specs += [ANY] * len(after)
    operands += list(after)
    outs = pl.pallas_call(
        body,
        name=name,
        grid=(n_l, nb),
        in_specs=in_specs,
        out_specs=[stacked] * 4,
        out_shape=[jax.ShapeDtypeStruct((n_l * r, c), F32)] * 4,
        compiler_params=_params(("arbitrary", "arbitrary")),
    )(*operands)
    return tuple(o.reshape(n_l, r, c) for o in outs)


def _train_step(x, target, weights, m_state, v_state):
    me = _block_id(lax.axis_index("x"), lax.axis_index("y"), lax.axis_index("c"))

    groups = [(kind, i) for i in range(DEPTH) for kind in ("mix", "ff")]

    def group_names(g):
        kind, i = g
        if kind == "mix":
            return [(n, i // N_MIXERS) for n in MIXER_BIG[i % N_MIXERS]]
        return [("ff_w1", i), ("ff_w2", i)]

    def shards_of(g):
        return [_shard_for_gather(n, weights[n][j]) for n, j in group_names(g)]

    whole = lambda g: [n in DEV_BLOCKED for n, _ in group_names(g)]

    small_shapes = [_shard_shape(n) for n in SMALL_SHARDED]
    gather = {0: _gather_start([_pack_rows([weights[n] for n in SMALL_SHARDED], SMALL_WIDTH, SUBLANES, F32)]
                               + shards_of(groups[0]), name="gather_g0_start", by_cols=[False] + whole(groups[0]))}
    cur, cur_m = x[0], _cast_input(x[0])
    prepared = {k: _gather_prepare(shards_of(groups[k]), whole(groups[k]), name=f"gather_g{k}_start")
                for k in range(1, len(groups))}
    early = [a for k in prepared for a in prepared[k][3]]
    gather[0] = _gather_pass_on(gather[0], name="gather_g0_pass", after=[cur_m] + early)
    first = _gather_finish(gather[0], name="gather_g0_finish", after=gather[0]["token"])
    small_all, first_matrices = first[0], first[1:]
    small_full = {n: weights[n] for n in SMALL if PARAMS[n][1] is None}
    for n, stacked in zip(SMALL_SHARDED, _unpack_rows(small_all, small_shapes, SMALL_WIDTH, SUBLANES)):
        small_full[n] = _merge_shards(stacked, PARAMS[n][1])

    def small_weights(i):
        wl = {}
        for n in SMALL:
            if n.startswith("ln_"):
                wl[n] = small_full[n][i:i + 1]
            elif n.startswith(("sc_", "mla_", "cf_")[i % N_MIXERS]):
                v = small_full[n][i // N_MIXERS]
                wl[n] = v if v.ndim == 2 else v.reshape(1, -1)
        return wl

    layers = [small_weights(i) for i in range(DEPTH)]

    def install(g, gathered):
        for (n, _), a in zip(group_names(g), gathered):
            layers[g[1]][n] = _as_operand(n, a)

    install(groups[0], first_matrices)
    gather[1] = _gather_start(None, name="gather_g1_start", after=gather[0]["token"], prepared=prepared[1])
    saved = {}
    for k, g in enumerate(groups):
        kind, i = g
        tokens = [gather[1]["token"]] if k == 0 else []
        if k + 2 < len(groups):
            gather[k + 2] = _gather_start(None, name=f"gather_g{k + 2}_start", after=gather[k + 1]["token"],
                                          prepared=prepared[k + 2])
            tokens.append(gather[k + 2]["token"])

        def pass_on(y, k=k):
            if k + 1 == len(groups):
                return None
            gather[k + 1] = _gather_pass_on(gather[k + 1], name=f"gather_g{k + 1}_pass", after=y)
            return gather[k + 1]["token"]

        phase = _mix_fwd if kind == "mix" else _ff_fwd
        cur, cur_m, saved[g] = phase(i, cur, cur_m, layers[i], hook=pass_on, after=tokens)
        if k + 1 < len(groups):
            install(groups[k + 1], _gather_finish(gather[k + 1], name=f"gather_g{k + 1}_finish", after=cur))

    loss_local, dy = _loss_and_grad(cur, target[0])
    loss = lax.psum(loss_local, MESH_AXES)

    reduced = {n: [None] * PARAMS[n][0][0] for n in BIG}
    small_grads = {n: [None] * PARAMS[n][0][0] for n in SMALL}
    reduce = {}

    def to_chips(p, after):
        st = reduce[p]
        parts, got = _swap_finish(st["sibling"], name=f"reduce_p{p}_sibling_finish", after=after)
        sums = _chip_partials_group(parts, got, name=f"reduce_p{p}_chipsum")
        st["own"] = [own for own, _ in sums]
        others = [o for _, o in sums]
        st["chips"] = _swap_start(others, [lax.empty(o.shape, o.dtype) for o in others], _to_chip, 3,
                                  name=f"reduce_p{p}_chips_start")
        return st["chips"]["token"]

    def arrived(p, after):
        st = reduce[p]
        _, got = _swap_finish(st["chips"], name=f"reduce_p{p}_chips_finish", after=after)
        for (n, j), own, arr in zip(st["names"], st["own"], got):
            reduced[n][j] = (own, arr)

    terms = [(1.0, dy)]
    back = list(reversed(groups))
    for p, g in enumerate(back):
        kind, i = g

        def after_first_kernel(dz, p=p):
            return to_chips(p - 1, dz) if p >= 1 else None

        phase = _mix_bwd if kind == "mix" else _ff_bwd
        tokens = [reduce[p - 1]["sibling"]["token"]] if p >= 1 else []
        terms, grads = phase(i, terms, saved[g], layers[i], hook=after_first_kernel, after=tokens)
        for n in SMALL:
            if n in grads:
                small_grads[n][i if n.startswith("ln_") else i // N_MIXERS] = grads[n]
        if p >= 2:
            arrived(p - 2, terms[-1][1])
        names = group_names(g)
        parts = [_as_parts(n, grads[n]) for n, _ in names]
        reduce[p] = {"names": names, "sibling": _swap_start(
            parts, [lax.empty((4, *a.shape[1:]), a.dtype) for a in parts], _to_sibling, 4,
            name=f"reduce_p{p}_sibling_start")}
    (_, grad_x), = terms
    last = len(back) - 1
    small_mine = _pack_rows([jnp.stack(small_grads[n]).reshape(PARAMS[n][0]) for n in SMALL], SMALL_WIDTH, SUBLANES, F32)
    small_gather = _gather_start([small_mine], name="gather_small_grads_start")
    tail = to_chips(last, small_gather["token"])
    arrived(last - 1, tail)

    grad, delta, new_m, new_v = {}, {}, {}, {}

    def adamw_matrix(n, after):
        shape = weights[n].shape
        view = lambda a: a.reshape(shape[0], shape[1], -1)
        outs = _adamw_reduced(view(weights[n]), view(m_state[n]), view(v_state[n]), reduced[n], name=f"adamw_{n}",
                              after=after)
        grad[n], delta[n], new_m[n], new_v[n] = [o.reshape(shape) for o in outs]
        return outs[0]

    done = tail
    for n in BIG:
        if n not in MIXER_BIG[0]:
            done = adamw_matrix(n, [done])
    small_gather = _gather_pass_on(small_gather, name="gather_small_grads_pass", after=done)
    small_all = _gather_finish(small_gather, name="gather_small_grads_finish", after=small_gather["token"])[0]
    small_sum = _sum_blocks(None, small_all, name="sum_small_grads")
    for n, g in zip(SMALL, _unpack_rows(small_sum, [PARAMS[n][0] for n in SMALL], SMALL_WIDTH, SUBLANES)):
        ax = PARAMS[n][1]
        grad[n] = g if ax is None else lax.dynamic_index_in_dim(_split_shards(g, ax), me, axis=0, keepdims=False)
    small_shards = [_shard_shape(n) for n in SMALL]
    packed = [_pack_rows([src[n] for n in SMALL], SMALL_WIDTH, SUBLANES, F32) for src in (weights, grad, m_state, v_state)]
    small_out = _adamw(*packed, name="adamw_small")
    for outs, dst in zip(small_out, (delta, new_m, new_v)):
        for n, a in zip(SMALL, _unpack_rows(outs, small_shards, SMALL_WIDTH, SUBLANES)):
            dst[n] = a
    arrived(last, small_sum)
    for n in MIXER_BIG[0]:
        adamw_matrix(n, [])
    return (loss, grad_x[None], *[grad[n] for n in NAMES], *[delta[n] for n in NAMES],
            *[new_m[n] for n in NAMES], *[new_v[n] for n in NAMES])


def kernel(x, sc_w_in, sc_conv_w, sc_w_out, mla_w_dq, mla_g_q, mla_w_uq, mla_w_dkv, mla_g_kv, mla_w_uk, mla_w_uv, mla_w_o, cf_w_pw1, cf_b_pw1, cf_dw_w, cf_dw_b, cf_norm_g, cf_norm_b, cf_w_pw2, cf_b_pw2, ff_w1, ff_w2, ln_mix_g, ln_mix_b, ln_ff_g, ln_ff_b, loss_target, m_sc_w_in, m_sc_conv_w, m_sc_w_out, m_mla_w_dq, m_mla_g_q, m_mla_w_uq, m_mla_w_dkv, m_mla_g_kv, m_mla_w_uk, m_mla_w_uv, m_mla_w_o, m_cf_w_pw1, m_cf_b_pw1, m_cf_dw_w, m_cf_dw_b, m_cf_norm_g, m_cf_norm_b, m_cf_w_pw2, m_cf_b_pw2, m_ff_w1, m_ff_w2, m_ln_mix_g, m_ln_mix_b, m_ln_ff_g, m_ln_ff_b, v_sc_w_in, v_sc_conv_w, v_sc_w_out, v_mla_w_dq, v_mla_g_q, v_mla_w_uq, v_mla_w_dkv, v_mla_g_kv, v_mla_w_uk, v_mla_w_uv, v_mla_w_o, v_cf_w_pw1, v_cf_b_pw1, v_cf_dw_w, v_cf_dw_b, v_cf_norm_g, v_cf_norm_b, v_cf_w_pw2, v_cf_b_pw2, v_ff_w1, v_ff_w2, v_ln_mix_g, v_ln_mix_b, v_ln_ff_g, v_ln_ff_b):
    w_list = (sc_w_in, sc_conv_w, sc_w_out, mla_w_dq, mla_g_q, mla_w_uq, mla_w_dkv, mla_g_kv, mla_w_uk, mla_w_uv, mla_w_o,
              cf_w_pw1, cf_b_pw1, cf_dw_w, cf_dw_b, cf_norm_g, cf_norm_b, cf_w_pw2, cf_b_pw2, ff_w1, ff_w2, ln_mix_g,
              ln_mix_b, ln_ff_g, ln_ff_b)
    m_list = (m_sc_w_in, m_sc_conv_w, m_sc_w_out, m_mla_w_dq, m_mla_g_q, m_mla_w_uq, m_mla_w_dkv, m_mla_g_kv, m_mla_w_uk,
              m_mla_w_uv, m_mla_w_o, m_cf_w_pw1, m_cf_b_pw1, m_cf_dw_w, m_cf_dw_b, m_cf_norm_g, m_cf_norm_b, m_cf_w_pw2,
              m_cf_b_pw2, m_ff_w1, m_ff_w2, m_ln_mix_g, m_ln_mix_b, m_ln_ff_g, m_ln_ff_b)
    v_list = (v_sc_w_in, v_sc_conv_w, v_sc_w_out, v_mla_w_dq, v_mla_g_q, v_mla_w_uq, v_mla_w_dkv, v_mla_g_kv, v_mla_w_uk,
              v_mla_w_uv, v_mla_w_o, v_cf_w_pw1, v_cf_b_pw1, v_cf_dw_w, v_cf_dw_b, v_cf_norm_g, v_cf_norm_b, v_cf_w_pw2,
              v_cf_b_pw2, v_ff_w1, v_ff_w2, v_ln_mix_g, v_ln_mix_b, v_ln_ff_g, v_ln_ff_b)
    return _train_step(x, loss_target, dict(zip(NAMES, w_list)), dict(zip(NAMES, m_list)), dict(zip(NAMES, v_list)))
```

```python
import functools

import jax
import jax.numpy as jnp
from jax import lax
from jax.experimental import pallas as pl
from jax.experimental.pallas import tpu as pltpu

F32 = jnp.float32
MXU_DTYPE = jnp.bfloat16

N_DEV = 8
D_MODEL = 1024
DEPTH = 4
N_MIXERS = 3
CHUNK = 64
ALPHA = (2.0 * DEPTH) ** 0.25
LN_EPS = 1e-5
RMS_EPS = 1e-6
MLA_HEADS = 8
QK_NOPE = 128
QK_ROPE = 64
V_HEAD = 128
Q_LORA = 384
KV_LORA = 256
ROPE_THETA = 10000.0
ADAM_LR = 0.001
ADAM_B1 = 0.9
ADAM_B2 = 0.999
ADAM_EPS = 1e-08
ADAM_WD = 0.01
ADAM_STEP = 10

LANES = 128
SUBLANES = 8
BF16_ROWS = 16
VMEM_LIMIT = 56 * 1024 * 1024
MM_VMEM_BUDGET = 36 * 1024 * 1024
CONV_PAD = 32
MASK_VALUE = -1e30

MESH_AXES = ("x", "y", "c")


def _params(semantics):
    return pltpu.CompilerParams(dimension_semantics=semantics, vmem_limit_bytes=VMEM_LIMIT)


def _tile(n, cap):
    if n <= cap:
        return n
    t = cap - cap % LANES
    while t >= LANES:
        if n % t == 0:
            return t
        t -= LANES
    raise ValueError(f"no tile for {n} under {cap}")


def _row_tile(rows, cap):
    if rows <= cap:
        return rows
    t = cap - cap % SUBLANES
    while t >= SUBLANES:
        if rows % t == 0:
            return t
        t -= SUBLANES
    raise ValueError(f"no row tile for {rows} under {cap}")


def _mm(a, b, *, name, ta=False, tb=False, b_dev=False, out_dev=False, out_dtypes=(F32,), epilogue=None,
        tile_extras=(), col_extras=(), after=(), tm_cap=2048, tn_cap=1024, tk_cap=4096):
    m, k = (a.shape[1], a.shape[0]) if ta else a.shape
    if b_dev:
        nd, b_rows, ns = b.shape
        n, kb = (b_rows, nd * ns) if tb else (nd * ns, b_rows)
    else:
        n, kb = (b.shape[0], b.shape[1]) if tb else (b.shape[1], b.shape[0])
    assert k == kb, (a.shape, b.shape, ta, tb)
    if ta or tile_extras or (b_dev and tb):
        tm_cap = min(tm_cap, 1024)
    tk = ns if (b_dev and tb) else _tile(k, tk_cap)
    nk = k // tk
    fixed_tn = ns if (b_dev and not tb) else (n // N_DEV if out_dev else None)
    if out_dev:
        assert not tile_extras and not col_extras and n % N_DEV == 0
    out_bytes = sum(jnp.dtype(dt).itemsize for dt in out_dtypes) + sum(e.dtype.itemsize for e in tile_extras)
    tm, tn = None, None
    for cand_m, cand_n in ((tm_cap, tn_cap), (tm_cap, tn_cap // 2), (tm_cap // 2, tn_cap), (tm_cap // 2, tn_cap // 2),
                           (tm_cap // 4, tn_cap // 2), (tm_cap // 4, tn_cap // 4)):
        tm, tn = _tile(m, cand_m), (fixed_tn if fixed_tn is not None else _tile(n, cand_n))
        need = 2 * (tm * tk * a.dtype.itemsize + tk * tn * b.dtype.itemsize) + 2 * tm * tn * out_bytes
        need += tm * tn * 4 * (2 if nk > 1 else 1)
        if need <= MM_VMEM_BUDGET:
            break
    n_te, n_ce, n_out = len(tile_extras), len(col_extras), len(out_dtypes)
    dims = (((0 if ta else 1,), (1 if tb else 0,)), ((), ()))

    def body(*refs):
        a_ref, b_ref = refs[0], refs[1]
        te_refs = refs[2:2 + n_te]
        ce_refs = refs[2 + n_te:2 + n_te + n_ce]
        first_out = 2 + n_te + n_ce + len(after)
        out_refs = refs[first_out:first_out + n_out]

        def finish(acc):
            if epilogue is None:
                outs = (acc,)
            else:
                outs = epilogue(acc, *[r[...] for r in te_refs], *[r[...] for r in ce_refs])
            for o_ref, o in zip(out_refs, outs):
                o_ref[...] = o.astype(o_ref.dtype)

        def product():
            return lax.dot_general(a_ref[...].astype(MXU_DTYPE), b_ref[...].astype(MXU_DTYPE), dims,
                                   preferred_element_type=F32)

        if nk == 1:
            finish(product())
        else:
            acc_ref = refs[-1]
            kk = pl.program_id(2)

            @pl.when(kk == 0)
            def _():
                acc_ref[...] = jnp.zeros_like(acc_ref)

            acc_ref[...] += product()

            @pl.when(kk == nk - 1)
            def _():
                finish(acc_ref[...])

    a_spec = pl.BlockSpec((tk, tm), lambda i, j, kk: (kk, i)) if ta else pl.BlockSpec((tm, tk), lambda i, j, kk: (i, kk))
    if b_dev and tb:
        b_spec = pl.BlockSpec((None, tn, tk), lambda i, j, kk: (kk, j, 0))
    elif b_dev:
        b_spec = pl.BlockSpec((None, tk, tn), lambda i, j, kk: (j, kk, 0))
    elif tb:
        b_spec = pl.BlockSpec((tn, tk), lambda i, j, kk: (j, kk))
    else:
        b_spec = pl.BlockSpec((tk, tn), lambda i, j, kk: (kk, j))
    tile_spec = pl.BlockSpec((tm, tn), lambda i, j, kk: (i, j))
    col_spec = pl.BlockSpec((1, tn), lambda i, j, kk: (0, j))
    if out_dev:
        out_spec = pl.BlockSpec((None, tm, tn), lambda i, j, kk: (j, i, 0))
        out_shape = [jax.ShapeDtypeStruct((N_DEV, m, tn), dt) for dt in out_dtypes]
    else:
        out_spec = tile_spec
        out_shape = [jax.ShapeDtypeStruct((m, n), dt) for dt in out_dtypes]
    outs = pl.pallas_call(
        body,
        name=name,
        grid=(m // tm, n // tn, nk),
        in_specs=[a_spec, b_spec] + [tile_spec] * n_te + [col_spec] * n_ce + [ANY] * len(after),
        out_specs=[out_spec] * n_out,
        out_shape=out_shape,
        scratch_shapes=[pltpu.VMEM((tm, tn), F32)] if nk > 1 else [],
        compiler_params=_params(("parallel", "parallel", "arbitrary")),
    )(a, b, *tile_extras, *col_extras, *after)
    return outs[0] if n_out == 1 else tuple(outs)


def _rowwise(fn, rows, bcast, out_rows, out_accs, *, name, tb_cap=256, after=()):
    t = rows[0].shape[0]
    tb = _row_tile(t, tb_cap)
    n_r, n_b, n_o, n_a = len(rows), len(bcast), len(out_rows), len(out_accs)

    def body(*refs):
        r_refs = refs[:n_r]
        b_refs = refs[n_r:n_r + n_b]
        first_out = n_r + n_b + len(after)
        o_refs = refs[first_out:first_out + n_o]
        a_refs = refs[first_out + n_o:]
        outs, accs = fn([r[...] for r in r_refs], [r[...] for r in b_refs])
        for o_ref, o in zip(o_refs, outs):
            o_ref[...] = o.astype(o_ref.dtype)
        if n_a:
            i = pl.program_id(0)

            @pl.when(i == 0)
            def _():
                for a_ref, acc in zip(a_refs, accs):
                    a_ref[...] = acc

            @pl.when(i > 0)
            def _():
                for a_ref, acc in zip(a_refs, accs):
                    a_ref[...] += acc

    in_specs = [pl.BlockSpec((tb, r.shape[1]), lambda i: (i, 0)) for r in rows]
    in_specs += [pl.BlockSpec(b.shape, lambda i: (0, 0)) for b in bcast]
    in_specs += [ANY] * len(after)
    out_specs = [pl.BlockSpec((tb, c), lambda i: (i, 0)) for c, _ in out_rows]
    out_specs += [pl.BlockSpec(s, lambda i: (0, 0)) for s in out_accs]
    out_shape = [jax.ShapeDtypeStruct((t, c), dt) for c, dt in out_rows]
    out_shape += [jax.ShapeDtypeStruct(s, F32) for s in out_accs]
    outs = pl.pallas_call(
        body,
        name=name,
        grid=(t // tb,),
        in_specs=in_specs,
        out_specs=out_specs,
        out_shape=out_shape,
        compiler_params=_params(("arbitrary",)),
    )(*rows, *bcast, *after)
    return tuple(outs)


def _colsum(v):
    return jnp.sum(v, axis=0, keepdims=True)


def _sigmoid(v):
    return 1.0 / (1.0 + jnp.exp(-v))


def _ln_fwd(x_res, y_sub, g, b, *, name, after=()):
    d = x_res.shape[1]

    def fn(rows, bc):
        z = ALPHA * rows[0] + rows[1]
        mu = jnp.mean(z, axis=-1, keepdims=True)
        zc = z - mu
        var = jnp.mean(zc * zc, axis=-1, keepdims=True)
        rstd = lax.rsqrt(var + LN_EPS)
        xhat = zc * rstd
        out = xhat * bc[0] + bc[1]
        return (out, out, xhat, rstd), ()

    return _rowwise(fn, [x_res, y_sub], [g, b], [(d, F32), (d, MXU_DTYPE), (d, F32), (1, F32)], [], name=name,
                    after=after, tb_cap=512)


def _ln_fwd_loss(x_res, y_sub, g, b, target, *, name, after=()):
    d = x_res.shape[1]

    def fn(rows, bc):
        z = ALPHA * rows[0] + rows[1]
        mu = jnp.mean(z, axis=-1, keepdims=True)
        zc = z - mu
        var = jnp.mean(zc * zc, axis=-1, keepdims=True)
        rstd = lax.rsqrt(var + LN_EPS)
        xhat = zc * rstd
        err = xhat * bc[0] + bc[1] - rows[2]
        return (xhat, rstd, err * (1.0 / d)), (_colsum(err * err),)

    return _rowwise(fn, [x_res, y_sub, target], [g, b], [(d, F32), (1, F32), (d, F32)], [(1, d)], name=name,
                    after=after, tb_cap=512)


def _ln_bwd(terms, xhat, rstd, g, *, name, after=()):
    d = xhat.shape[1]
    coefs = [c for c, _ in terms]

    def fn(rows, bc):
        xh, rs = rows[0], rows[1]
        dout = None
        for c, v in zip(coefs, rows[2:]):
            tv = v if c == 1.0 else c * v
            dout = tv if dout is None else dout + tv
        dxh = dout * bc[0]
        m1 = jnp.mean(dxh, axis=-1, keepdims=True)
        m2 = jnp.mean(dxh * xh, axis=-1, keepdims=True)
        dz = rs * (dxh - m1 - xh * m2)
        return (dz, dz), (_colsum(dout * xh), _colsum(dout), _colsum(dz))

    return _rowwise(fn, [xhat, rstd] + [v for _, v in terms], [g], [(d, F32), (d, MXU_DTYPE)],
                    [(1, d), (1, d), (1, d)], name=name, after=after, tb_cap=512)


def _rms_fwd(x, g, *, name):
    c = x.shape[1]

    def fn(rows, bc):
        v = rows[0]
        r = lax.rsqrt(jnp.mean(v * v, axis=-1, keepdims=True) + RMS_EPS)
        return (v * r * bc[0],), ()

    return _rowwise(fn, [x], [g], [(c, MXU_DTYPE)], [], name=name)[0]


def _rms_bwd(x, g, dy, *, name):
    c = x.shape[1]

    def fn(rows, bc):
        v, dyv = rows
        r = lax.rsqrt(jnp.mean(v * v, axis=-1, keepdims=True) + RMS_EPS)
        xh = v * r
        dxh = dyv * bc[0]
        dx = r * (dxh - xh * jnp.mean(dxh * xh, axis=-1, keepdims=True))
        return (dx,), (_colsum(dyv * xh),)

    return _rowwise(fn, [x, dy], [g], [(c, F32)], [(1, c)], name=name)


def _conv_chunk(t):
    return _row_tile(t, 256)


def _conv_fwd(x, w, bias, *, name):
    t, c = x.shape
    kw = w.shape[0]
    ch = _conv_chunk(t)

    def body(x_ref, w_ref, b_ref, o_ref, pad_ref):
        pad_ref[0:CONV_PAD, :] = jnp.zeros((CONV_PAD, LANES), F32)
        pad_ref[CONV_PAD:CONV_PAD + t, :] = x_ref[...]
        for t0 in range(0, t, ch):
            acc = jnp.broadcast_to(b_ref[...], (ch, LANES))
            for k in range(kw):
                s = kw - 1 - k
                acc = acc + w_ref[k:k + 1, :] * pad_ref[CONV_PAD - s + t0:CONV_PAD - s + t0 + ch, :]
            o_ref[t0:t0 + ch, :] = acc

    return pl.pallas_call(
        body,
        name=name,
        grid=(c // LANES,),
        in_specs=[pl.BlockSpec((t, LANES), lambda j: (0, j)), pl.BlockSpec((kw, LANES), lambda j: (0, j)),
                  pl.BlockSpec((1, LANES), lambda j: (0, j))],
        out_specs=pl.BlockSpec((t, LANES), lambda j: (0, j)),
        out_shape=jax.ShapeDtypeStruct((t, c), F32),
        scratch_shapes=[pltpu.VMEM((CONV_PAD + t, LANES), F32)],
        compiler_params=_params(("parallel",)),
    )(x, w, bias)


def _conv_bwd(dy, x, w, *, name):
    t, c = x.shape
    kw = w.shape[0]
    ch = _conv_chunk(t)

    def body(dy_ref, x_ref, w_ref, dx_ref, dw_ref, xpad_ref, dpad_ref):
        xpad_ref[0:CONV_PAD, :] = jnp.zeros((CONV_PAD, LANES), F32)
        xpad_ref[CONV_PAD:CONV_PAD + t, :] = x_ref[...]
        dpad_ref[0:t, :] = dy_ref[...]
        dpad_ref[t:t + CONV_PAD, :] = jnp.zeros((CONV_PAD, LANES), F32)
        for t0 in range(0, t, ch):
            acc = jnp.zeros((ch, LANES), F32)
            for k in range(kw):
                s = kw - 1 - k
                acc = acc + w_ref[k:k + 1, :] * dpad_ref[t0 + s:t0 + s + ch, :]
            dx_ref[t0:t0 + ch, :] = acc
        for k in range(kw):
            s = kw - 1 - k
            acc = jnp.zeros((ch, LANES), F32)
            for t0 in range(0, t, ch):
                acc = acc + dy_ref[t0:t0 + ch, :] * xpad_ref[CONV_PAD - s + t0:CONV_PAD - s + t0 + ch, :]
            dw_ref[k:k + 1, :] = _colsum(acc)

    blk = pl.BlockSpec((t, LANES), lambda j: (0, j))
    wblk = pl.BlockSpec((kw, LANES), lambda j: (0, j))
    return pl.pallas_call(
        body,
        name=name,
        grid=(c // LANES,),
        in_specs=[blk, blk, wblk],
        out_specs=[blk, wblk],
        out_shape=[jax.ShapeDtypeStruct((t, c), F32), jax.ShapeDtypeStruct((kw, c), F32)],
        scratch_shapes=[pltpu.VMEM((CONV_PAD + t, LANES), F32), pltpu.VMEM((CONV_PAD + t, LANES), F32)],
        compiler_params=_params(("parallel",)),
    )(dy, x, w)


ATTN_SCALE = (QK_NOPE + QK_ROPE) ** -0.5
_NT = (((1,), (1,)), ((), ()))
_TN = (((0,), (0,)), ((), ()))
_NN = (((1,), (0,)), ((), ()))


def _dot(a, b, dims):
    return lax.dot_general(a, b, dims, preferred_element_type=F32)


def _block_scores(q_cat, kn_ref, kp_ref, lo, hi, diagonal):
    k_cat = jnp.concatenate([kn_ref[lo:hi, :], kp_ref[lo:hi, :]], axis=1)
    s = _dot(q_cat, k_cat, _NT) * ATTN_SCALE
    if diagonal:
        shift = CHUNK.bit_length() - 1
        rows = lax.broadcasted_iota(jnp.int32, s.shape, 0)
        cols = lax.broadcasted_iota(jnp.int32, s.shape, 1)
        s = jnp.where(jnp.right_shift(cols, shift) <= jnp.right_shift(rows, shift), s, MASK_VALUE)
    return s, k_cat


def _attn_fwd(qn, qp, kv, kp, *, name):
    h, t, _ = qp.shape
    tq = _row_tile(t, 256)
    assert tq % CHUNK == 0

    def body(qn_ref, qp_ref, kn_ref, kp_ref, v_ref, o_ref, of_ref, lse_ref):
        for r0 in range(0, t, tq):
            r1 = r0 + tq
            q_cat = jnp.concatenate([qn_ref[r0:r1, :].astype(MXU_DTYPE), qp_ref[r0:r1, :]], axis=1)
            s_d, _ = _block_scores(q_cat, kn_ref, kp_ref, r0, r1, True)
            m = jnp.max(s_d, axis=-1, keepdims=True)
            if r0:
                s_b, _ = _block_scores(q_cat, kn_ref, kp_ref, 0, r0, False)
                m = jnp.maximum(m, jnp.max(s_b, axis=-1, keepdims=True))
            p_d = jnp.exp(s_d - m)
            l = jnp.sum(p_d, axis=-1, keepdims=True)
            acc = _dot(p_d.astype(MXU_DTYPE), v_ref[r0:r1, :], _NN)
            if r0:
                p_b = jnp.exp(s_b - m)
                l = l + jnp.sum(p_b, axis=-1, keepdims=True)
                acc = acc + _dot(p_b.astype(MXU_DTYPE), v_ref[0:r0, :], _NN)
            o = acc / l
            o_ref[r0:r1, :] = o.astype(o_ref.dtype)
            of_ref[r0:r1, :] = o
            lse_ref[r0:r1, :] = m + jnp.log(l)

    blk = lambda w: pl.BlockSpec((None, t, w), lambda hh: (hh, 0, 0))
    cols = pl.BlockSpec((t, QK_NOPE), lambda hh: (0, hh))
    keys = pl.BlockSpec((t, QK_NOPE), lambda hh: (0, 2 * hh))
    values = pl.BlockSpec((t, V_HEAD), lambda hh: (0, 2 * hh + 1))
    return pl.pallas_call(
        body,
        name=name,
        grid=(h,),
        in_specs=[cols, blk(QK_ROPE), keys, pl.BlockSpec((t, QK_ROPE), lambda hh: (0, 0)), values],
        out_specs=[cols, cols, blk(1)],
        out_shape=[jax.ShapeDtypeStruct((t, h * V_HEAD), MXU_DTYPE), jax.ShapeDtypeStruct((t, h * V_HEAD), F32),
                   jax.ShapeDtypeStruct((h, t, 1), F32)],
        compiler_params=_params(("parallel",)),
    )(qn, qp, kv, kp, kv)


def _attn_bwd(qn, qp, kv, kp, do, o, lse, *, name):
    h, t, _ = qp.shape
    tq = _row_tile(t, 256)

    def body(qn_ref, qp_ref, kn_ref, kp_ref, v_ref, do_ref, o_ref, lse_ref, dqn_ref, dqp_ref, dkv_ref, dkp_ref):
        dkv_ref[...] = jnp.zeros_like(dkv_ref)

        @pl.when(pl.program_id(0) == 0)
        def _():
            dkp_ref[...] = jnp.zeros_like(dkp_ref)

        for r0 in range(0, t, tq):
            r1 = r0 + tq
            q_cat = jnp.concatenate([qn_ref[r0:r1, :].astype(MXU_DTYPE), qp_ref[r0:r1, :]], axis=1)
            dov, lse = do_ref[r0:r1, :], lse_ref[r0:r1, :]
            delta = jnp.sum(dov.astype(F32) * o_ref[r0:r1, :], axis=-1, keepdims=True)

            def piece(lo, hi, diagonal):
                s, k_cat = _block_scores(q_cat, kn_ref, kp_ref, lo, hi, diagonal)
                p = jnp.exp(s - lse)
                dp = _dot(dov, v_ref[lo:hi, :], _NT)
                ds_m = (p * (dp - delta) * ATTN_SCALE).astype(MXU_DTYPE)
                dk_cat = _dot(ds_m, q_cat, _TN)
                dkv_ref[lo:hi, :QK_NOPE] += dk_cat[:, :QK_NOPE]
                dkp_ref[lo:hi, :] += dk_cat[:, QK_NOPE:]
                dkv_ref[lo:hi, QK_NOPE:] += _dot(p.astype(MXU_DTYPE), dov, _TN)
                return _dot(ds_m, k_cat, _NN)

            dq_cat = piece(r0, r1, True)
            if r0:
                dq_cat = dq_cat + piece(0, r0, False)
            dqn_ref[r0:r1, :] = dq_cat[:, :QK_NOPE]
            dqp_ref[r0:r1, :] = dq_cat[:, QK_NOPE:]

    blk = lambda w: pl.BlockSpec((None, t, w), lambda hh: (hh, 0, 0))
    kpblk = pl.BlockSpec((t, QK_ROPE), lambda hh: (0, 0))
    cols = pl.BlockSpec((t, QK_NOPE), lambda hh: (0, hh))
    keys = pl.BlockSpec((t, QK_NOPE), lambda hh: (0, 2 * hh))
    values = pl.BlockSpec((t, V_HEAD), lambda hh: (0, 2 * hh + 1))
    both = pl.BlockSpec((t, QK_NOPE + V_HEAD), lambda hh: (0, hh))
    return pl.pallas_call(
        body,
        name=name,
        grid=(h,),
        in_specs=[cols, blk(QK_ROPE), keys, kpblk, values, cols, cols, blk(1)],
        out_specs=[cols, blk(QK_ROPE), both, kpblk],
        out_shape=[jax.ShapeDtypeStruct((t, h * QK_NOPE), F32), jax.ShapeDtypeStruct((h, t, QK_ROPE), F32),
                   jax.ShapeDtypeStruct((t, h * (QK_NOPE + V_HEAD)), F32), jax.ShapeDtypeStruct((t, QK_ROPE), F32)],
        compiler_params=_params(("arbitrary",)),
    )(qn, qp, kv, kp, kv, do, o, lse)


def _rope(x1, x2, cos, sin, *, name, inverse=False):
    w = x1.shape[1]
    sign = -1.0 if inverse else 1.0

    def fn(rows, bc):
        a, b, c, s = rows
        s = sign * s
        return (a * c - b * s, a * s + b * c), ()

    return _rowwise(fn, [x1, x2, cos, sin], [], [(w, F32), (w, F32)], [], name=name)


def _mlp_up(x_m, w1, tag, after=()):
    def epi(acc):
        r = jnp.maximum(acc, 0.0)
        return (r * r,)

    return _mm(x_m, w1, name=f"{tag}_up", out_dtypes=(MXU_DTYPE,), epilogue=epi, after=after)


def _mlp_fwd(x_m, w1, w2, tag, after=(), act=None):
    if act is None:
        act, after = _mlp_up(x_m, w1, tag, after=after), ()
    y = _mm(act, w2, name=f"{tag}_down", after=after)
    return y, (x_m, act)


def _plus_residual(acc, res):
    return (acc + ALPHA * res,)


def _mlp_bwd(dy_m, saved, w1, w2, tag, residual, after=()):
    x_m, act = saved
    dw2 = _mm(act, dy_m, ta=True, name=f"{tag}_dw2", out_dtypes=(MXU_DTYPE,))

    def epi(acc, av):
        return (acc * (2.0 * jnp.sqrt(av.astype(F32))),)

    dh = _mm(dy_m, w2, tb=True, name=f"{tag}_dact", out_dtypes=(MXU_DTYPE,), epilogue=epi, tile_extras=(act,),
             after=after)
    dw1 = _mm(x_m, dh, ta=True, out_dev=True, name=f"{tag}_dw1", out_dtypes=(MXU_DTYPE,))
    dx = _mm(dh, w1, tb=True, name=f"{tag}_dx", epilogue=_plus_residual, tile_extras=(residual,))
    return [dx], {"ff_w1": dw1, "ff_w2": dw2}


def _sc_gate_specs(t, d):
    per = d // LANES
    return [pl.BlockSpec((t, LANES), lambda j, k=k: (0, k * per + j)) for k in range(3)]


def _sc_gate_fwd(u, conv_w, *, name):
    t, d = u.shape[0], u.shape[1] // 3
    kw = conv_w.shape[0]
    ch = _conv_chunk(t)

    def body(b_ref, c_ref, h_ref, w_ref, r_ref, pad_ref):
        pad_ref[0:CONV_PAD, :] = jnp.zeros((CONV_PAD, LANES), F32)
        pad_ref[CONV_PAD:CONV_PAD + t, :] = c_ref[...] * h_ref[...]
        for t0 in range(0, t, ch):
            q = jnp.zeros((ch, LANES), F32)
            for k in range(kw):
                s = kw - 1 - k
                q = q + w_ref[k:k + 1, :] * pad_ref[CONV_PAD - s + t0:CONV_PAD - s + t0 + ch, :]
            r_ref[t0:t0 + ch, :] = (b_ref[t0:t0 + ch, :] * q).astype(r_ref.dtype)

    blk = pl.BlockSpec((t, LANES), lambda j: (0, j))
    return pl.pallas_call(
        body,
        name=name,
        grid=(d // LANES,),
        in_specs=_sc_gate_specs(t, d) + [pl.BlockSpec((kw, LANES), lambda j: (0, j))],
        out_specs=blk,
        out_shape=jax.ShapeDtypeStruct((t, d), MXU_DTYPE),
        scratch_shapes=[pltpu.VMEM((CONV_PAD + t, LANES), F32)],
        compiler_params=_params(("parallel",)),
    )(u, u, u, conv_w)


def _sc_gate_bwd(dr, u, conv_w, *, name):
    t, d = u.shape[0], u.shape[1] // 3
    kw = conv_w.shape[0]
    ch = _conv_chunk(t)

    def body(dr_ref, b_ref, c_ref, h_ref, w_ref, db_ref, dp_ref, dw_ref, ppad_ref, dpad_ref):
        ppad_ref[0:CONV_PAD, :] = jnp.zeros((CONV_PAD, LANES), F32)
        ppad_ref[CONV_PAD:CONV_PAD + t, :] = c_ref[...] * h_ref[...]
        dpad_ref[0:t, :] = dr_ref[...] * b_ref[...]
        dpad_ref[t:t + CONV_PAD, :] = jnp.zeros((CONV_PAD, LANES), F32)
        for t0 in range(0, t, ch):
            q = jnp.zeros((ch, LANES), F32)
            dp = jnp.zeros((ch, LANES), F32)
            for k in range(kw):
                s = kw - 1 - k
                q = q + w_ref[k:k + 1, :] * ppad_ref[CONV_PAD - s + t0:CONV_PAD - s + t0 + ch, :]
                dp = dp + w_ref[k:k + 1, :] * dpad_ref[t0 + s:t0 + s + ch, :]
            db_ref[t0:t0 + ch, :] = (dr_ref[t0:t0 + ch, :] * q).astype(db_ref.dtype)
            dp_ref[t0:t0 + ch, :] = dp
        for k in range(kw):
            s = kw - 1 - k
            acc = jnp.zeros((ch, LANES), F32)
            for t0 in range(0, t, ch):
                acc = acc + dpad_ref[t0:t0 + ch, :] * ppad_ref[CONV_PAD - s + t0:CONV_PAD - s + t0 + ch, :]
            dw_ref[k:k + 1, :] = _colsum(acc)

    blk = pl.BlockSpec((t, LANES), lambda j: (0, j))
    wblk = pl.BlockSpec((kw, LANES), lambda j: (0, j))
    return pl.pallas_call(
        body,
        name=name,
        grid=(d // LANES,),
        in_specs=[blk] + _sc_gate_specs(t, d) + [wblk],
        out_specs=[blk, blk, wblk],
        out_shape=[jax.ShapeDtypeStruct((t, d), MXU_DTYPE), jax.ShapeDtypeStruct((t, d), F32),
                   jax.ShapeDtypeStruct((kw, d), F32)],
        scratch_shapes=[pltpu.VMEM((CONV_PAD + t, LANES), F32), pltpu.VMEM((CONV_PAD + t, LANES), F32)],
        compiler_params=_params(("parallel",)),
    )(dr, u, u, u, conv_w)


def _sc_fwd(x_m, w_in, conv_w, w_out, tag, after=()):
    u = _mm(x_m, w_in, name=f"{tag}_in", after=after)
    r = _sc_gate_fwd(u, conv_w, name=f"{tag}_gate")
    y = _mm(r, w_out, name=f"{tag}_out")
    return y, (x_m, u, r)


def _sc_bwd(dy_m, saved, w_in, conv_w, w_out, tag, residual, after=()):
    d = D_MODEL
    x_m, u, r = saved
    dw_out = _mm(r, dy_m, ta=True, name=f"{tag}_dwout", out_dtypes=(MXU_DTYPE,))
    dr = _mm(dy_m, w_out, tb=True, name=f"{tag}_dr", after=after)
    db, dp, dconv = _sc_gate_bwd(dr, u, conv_w, name=f"{tag}_dgate")

    def du_fn(rows, bc):
        dbv, dpv, uv = rows
        return (jnp.concatenate([dbv.astype(F32), dpv * uv[:, 2 * d:3 * d], dpv * uv[:, d:2 * d]], axis=1),), ()

    du = _rowwise(du_fn, [db, dp, u], [], [(3 * d, MXU_DTYPE)], [], name=f"{tag}_du")[0]
    dw_in = _mm(x_m, du, ta=True, out_dev=True, name=f"{tag}_dwin", out_dtypes=(MXU_DTYPE,))
    dx = _mm(du, w_in, tb=True, name=f"{tag}_dx", epilogue=_plus_residual, tile_extras=(residual,))
    return [dx], {"sc_w_in": dw_in, "sc_conv_w": dconv, "sc_w_out": dw_out}


def _cf_fwd(x_m, w, tag, after=()):
    d = D_MODEL

    def bias_epi(acc, bias):
        return (acc + bias,)

    u = _mm(x_m, w["cf_w_pw1"], name=f"{tag}_pw1", epilogue=bias_epi, col_extras=(w["cf_b_pw1"],),
            after=after)

    def glu_fn(rows, bc):
        uv = rows[0]
        return (uv[:, 0:d] * _sigmoid(uv[:, d:2 * d]),), ()

    h1 = _rowwise(glu_fn, [u], [], [(d, F32)], [], name=f"{tag}_glu")[0]
    h2 = _conv_fwd(h1, w["cf_dw_w"], w["cf_dw_b"], name=f"{tag}_conv")

    def norm_fn(rows, bc):
        hv = rows[0]
        mu = jnp.mean(hv, axis=-1, keepdims=True)
        hc = hv - mu
        rstd = lax.rsqrt(jnp.mean(hc * hc, axis=-1, keepdims=True) + LN_EPS)
        h3 = hc * rstd * bc[0] + bc[1]
        return (h3 * _sigmoid(h3),), ()

    h4 = _rowwise(norm_fn, [h2], [w["cf_norm_g"], w["cf_norm_b"]], [(d, MXU_DTYPE)], [], name=f"{tag}_norm")[0]
    y = _mm(h4, w["cf_w_pw2"], name=f"{tag}_pw2", epilogue=bias_epi, col_extras=(w["cf_b_pw2"],))
    return y, (x_m, u, h1, h2, h4)


def _cf_bwd(dy_m, dy_colsum, saved, w, tag, residual, after=()):
    d = D_MODEL
    x_m, u, h1, h2, h4 = saved
    dw_pw2 = _mm(h4, dy_m, ta=True, name=f"{tag}_dwpw2", out_dtypes=(MXU_DTYPE,))
    dh4 = _mm(dy_m, w["cf_w_pw2"], tb=True, name=f"{tag}_dh4", after=after)

    def dnorm_fn(rows, bc):
        dh4v, hv = rows
        g, b = bc
        mu = jnp.mean(hv, axis=-1, keepdims=True)
        hc = hv - mu
        rstd = lax.rsqrt(jnp.mean(hc * hc, axis=-1, keepdims=True) + LN_EPS)
        xh = hc * rstd
        h3 = xh * g + b
        sg = _sigmoid(h3)
        dh3 = dh4v * (sg * (1.0 + h3 * (1.0 - sg)))
        dxh = dh3 * g
        dh2 = rstd * (dxh - jnp.mean(dxh, axis=-1, keepdims=True) - xh * jnp.mean(dxh * xh, axis=-1, keepdims=True))
        return (dh2,), (_colsum(dh3 * xh), _colsum(dh3), _colsum(dh2))

    dh2, dnorm_g, dnorm_b, ddw_b = _rowwise(dnorm_fn, [dh4, h2], [w["cf_norm_g"], w["cf_norm_b"]], [(d, F32)],
                                            [(1, d), (1, d), (1, d)], name=f"{tag}_dnorm")
    dh1, ddw_w = _conv_bwd(dh2, h1, w["cf_dw_w"], name=f"{tag}_dconv")

    def dglu_fn(rows, bc):
        dh1v, uv = rows
        a, gate = uv[:, 0:d], uv[:, d:2 * d]
        sg = _sigmoid(gate)
        du = jnp.concatenate([dh1v * sg, dh1v * a * sg * (1.0 - sg)], axis=1)
        return (du,), (_colsum(du),)

    du, db_pw1 = _rowwise(dglu_fn, [dh1, u], [], [(2 * d, MXU_DTYPE)], [(1, 2 * d)], name=f"{tag}_dglu")
    dw_pw1 = _mm(x_m, du, ta=True, out_dev=True, name=f"{tag}_dwpw1", out_dtypes=(MXU_DTYPE,))
    dx = _mm(du, w["cf_w_pw1"], tb=True, name=f"{tag}_dx", epilogue=_plus_residual, tile_extras=(residual,))
    grads = {"cf_w_pw1": dw_pw1, "cf_b_pw1": db_pw1, "cf_dw_w": ddw_w, "cf_dw_b": ddw_b, "cf_norm_g": dnorm_g,
             "cf_norm_b": dnorm_b, "cf_w_pw2": dw_pw2, "cf_b_pw2": dy_colsum}
    return [dx], grads


def _rope_tables(t):
    pos = jnp.arange(t, dtype=F32)
    inv_freq = ROPE_THETA ** (-jnp.arange(0, QK_ROPE, 2, dtype=F32) / QK_ROPE)
    ang = pos[:, None] * inv_freq[None, :]
    return jnp.cos(ang), jnp.sin(ang)


ROPE_HALF = QK_ROPE // 2
Q_NOPE_COLS = MLA_HEADS * QK_NOPE
Q_HALF_COLS = MLA_HEADS * ROPE_HALF


def _interleave_heads(w_uk, w_uv):
    c = w_uk.shape[0]
    return jnp.concatenate([w_uk.reshape(c, MLA_HEADS, QK_NOPE), w_uv.reshape(c, MLA_HEADS, V_HEAD)],
                           axis=2).reshape(c, -1)


def _mla_fwd(x_m, w, tag, after=()):
    t = x_m.shape[0]
    cos, sin = _rope_tables(t)
    cos_h, sin_h = jnp.tile(cos, (1, MLA_HEADS)), jnp.tile(sin, (1, MLA_HEADS))
    cq_pre = _mm(x_m, w["mla_w_dq"], name=f"{tag}_dq", after=after)
    cq = _rms_fwd(cq_pre, w["mla_g_q"], name=f"{tag}_qnorm")
    q = _mm(cq, w["mla_w_uq"], name=f"{tag}_uq")
    r1, r2 = _rope(q[:, Q_NOPE_COLS:Q_NOPE_COLS + Q_HALF_COLS], q[:, Q_NOPE_COLS + Q_HALF_COLS:], cos_h, sin_h,
                   name=f"{tag}_qrope")
    qp_h = jnp.concatenate([r1.reshape(t, MLA_HEADS, ROPE_HALF), r2.reshape(t, MLA_HEADS, ROPE_HALF)],
                           axis=-1).transpose(1, 0, 2).astype(MXU_DTYPE)
    ckv_full = _mm(x_m, w["mla_w_dkv"], name=f"{tag}_dkv")
    ckv_pre = ckv_full[:, :KV_LORA]
    ckv = _rms_fwd(ckv_pre, w["mla_g_kv"], name=f"{tag}_kvnorm")
    k1, k2 = _rope(ckv_full[:, KV_LORA:KV_LORA + ROPE_HALF], ckv_full[:, KV_LORA + ROPE_HALF:], cos, sin,
                   name=f"{tag}_krope")
    kp = jnp.concatenate([k1, k2], axis=-1).astype(MXU_DTYPE)
    kv = _mm(ckv, _interleave_heads(w["mla_w_uk"], w["mla_w_uv"]), name=f"{tag}_ukv", out_dtypes=(MXU_DTYPE,))
    o, o_f, lse = _attn_fwd(q, qp_h, kv, kp, name=f"{tag}_attn")
    y = _mm(o, w["mla_w_o"], name=f"{tag}_o")
    return y, (x_m, cq_pre, cq, ckv_pre, ckv, q, qp_h, kv, kp, lse, o, o_f, (cos, sin, cos_h, sin_h))


def _mla_bwd(dy_m, saved, w, tag, residual, after=()):
    x_m, cq_pre, cq, ckv_pre, ckv, q, qp_h, kv, kp, lse, o, o_f, (cos, sin, cos_h, sin_h) = saved
    t = x_m.shape[0]
    half = ROPE_HALF
    dw_o = _mm(o, dy_m, ta=True, name=f"{tag}_dwo", out_dtypes=(MXU_DTYPE,))
    do = _mm(dy_m, w["mla_w_o"], tb=True, name=f"{tag}_do", out_dtypes=(MXU_DTYPE,), after=after)
    dqn, dqp_h, dkv, dkp = _attn_bwd(q, qp_h, kv, kp, do, o_f, lse, name=f"{tag}_dattn")
    dqp = dqp_h.transpose(1, 0, 2)
    d1, d2 = _rope(dqp[:, :, :half].reshape(t, -1), dqp[:, :, half:].reshape(t, -1), cos_h, sin_h,
                   name=f"{tag}_dqrope", inverse=True)
    dq = jnp.concatenate([dqn, d1, d2], axis=-1).astype(MXU_DTYPE)
    dw_uq = _mm(cq, dq, ta=True, name=f"{tag}_dwuq", out_dtypes=(MXU_DTYPE,))
    dcq = _mm(dq, w["mla_w_uq"], tb=True, name=f"{tag}_dcq")
    dcq_pre, dg_q = _rms_bwd(cq_pre, w["mla_g_q"], dcq, name=f"{tag}_dqnorm")
    dcq_pre_m = dcq_pre.astype(MXU_DTYPE)
    dw_dq = _mm(x_m, dcq_pre_m, ta=True, name=f"{tag}_dwdq", out_dtypes=(MXU_DTYPE,))
    dx_q = _mm(dcq_pre_m, w["mla_w_dq"], tb=True, name=f"{tag}_dxq", epilogue=_plus_residual, tile_extras=(residual,))
    dw_kv = _mm(ckv, dkv, ta=True, name=f"{tag}_dwukv", out_dtypes=(MXU_DTYPE,))
    dw_kv = dw_kv.reshape(KV_LORA, MLA_HEADS, QK_NOPE + V_HEAD)
    dw_uk, dw_uv = dw_kv[:, :, :QK_NOPE].reshape(KV_LORA, -1), dw_kv[:, :, QK_NOPE:].reshape(KV_LORA, -1)
    dckv = _mm(dkv, _interleave_heads(w["mla_w_uk"], w["mla_w_uv"]), tb=True, name=f"{tag}_dckv")
    dckv_pre, dg_kv = _rms_bwd(ckv_pre, w["mla_g_kv"], dckv, name=f"{tag}_dkvnorm")
    dk1, dk2 = _rope(dkp[:, :half], dkp[:, half:], cos, sin, name=f"{tag}_dkrope", inverse=True)
    dckv_full = jnp.concatenate([dckv_pre, dk1, dk2], axis=-1).astype(MXU_DTYPE)
    dw_dkv = _mm(x_m, dckv_full, ta=True, name=f"{tag}_dwdkv", out_dtypes=(MXU_DTYPE,))
    dx_kv = _mm(dckv_full, w["mla_w_dkv"], tb=True, name=f"{tag}_dxkv", epilogue=lambda acc, other: (acc + other,),
                tile_extras=(dx_q,))
    grads = {"mla_w_dq": dw_dq, "mla_g_q": dg_q, "mla_w_uq": dw_uq, "mla_w_dkv": dw_dkv, "mla_g_kv": dg_kv,
             "mla_w_uk": dw_uk, "mla_w_uv": dw_uv, "mla_w_o": dw_o}
    return [dx_kv], grads


def _cast_input(x):
    cast = lambda rows, bc: ((rows[0],), ())
    return _rowwise(cast, [x], [], [(D_MODEL, MXU_DTYPE)], [], name="x_cast")[0]


def _hooked(value, hook):
    token = None if hook is None else hook(value)
    return () if token is None else (token,)


def _mix_fwd(i, cur, cur_m, wl, hook=None, after=()):
    mixer = i % N_MIXERS
    if mixer == 0:
        y, s_mix = _sc_fwd(cur_m, wl["sc_w_in"], wl["sc_conv_w"], wl["sc_w_out"], f"l{i}_sc", after=after)
    elif mixer == 1:
        y, s_mix = _mla_fwd(cur_m, wl, f"l{i}_mla", after=after)
    else:
        y, s_mix = _cf_fwd(cur_m, wl, f"l{i}_cf", after=after)
    cur, cur_m, xh_mix, rs_mix = _ln_fwd(cur, y, wl["ln_mix_g"], wl["ln_mix_b"], name=f"l{i}_lnmix",
                                         after=_hooked(y, hook))
    return cur, cur_m, (s_mix, xh_mix, rs_mix)


def _ff_fwd(i, cur, cur_m, wl, hook=None, after=(), loss_target=None, act=None):
    y, s_ff = _mlp_fwd(cur_m, wl["ff_w1"], wl["ff_w2"], f"l{i}_ff", after=after, act=act)
    if loss_target is not None:
        xh_ff, rs_ff, dy, sq = _ln_fwd_loss(cur, y, wl["ln_ff_g"], wl["ln_ff_b"], loss_target, name=f"l{i}_lnff_loss",
                                            after=_hooked(y, hook))
        return (0.5 / D_MODEL) * jnp.sum(sq), dy, (s_ff, xh_ff, rs_ff)
    cur, cur_m, xh_ff, rs_ff = _ln_fwd(cur, y, wl["ln_ff_g"], wl["ln_ff_b"], name=f"l{i}_lnff", after=_hooked(y, hook))
    return cur, cur_m, (s_ff, xh_ff, rs_ff)


def _layer_fwd(i, cur, cur_m, wl):
    cur, cur_m, s_mix = _mix_fwd(i, cur, cur_m, wl)
    cur, cur_m, s_ff = _ff_fwd(i, cur, cur_m, wl)
    return cur, cur_m, (*s_mix, *s_ff)


def _loss_and_grad(out, target):
    d = D_MODEL

    def loss_fn(rows, bc):
        err = rows[0] - rows[1]
        return (err * (1.0 / d),), (_colsum(err * err),)

    dy, sq = _rowwise(loss_fn, [out, target], [], [(d, F32)], [(1, d)], name="loss")
    return (0.5 / d) * jnp.sum(sq), dy


def _ff_bwd(i, terms, saved, wl, hook=None, after=()):
    s_ff, xh_ff, rs_ff = saved
    grads = {}
    dz, dz_m, grads["ln_ff_g"], grads["ln_ff_b"], _ = _ln_bwd(terms, xh_ff, rs_ff, wl["ln_ff_g"], name=f"l{i}_dlnff",
                                                              after=after)
    dxs, g_ff = _mlp_bwd(dz_m, s_ff, wl["ff_w1"], wl["ff_w2"], f"l{i}_ff", dz, after=_hooked(dz, hook))
    grads.update(g_ff)
    return [(1.0, v) for v in dxs], grads


def _mix_bwd(i, terms, saved, wl, hook=None, after=()):
    mixer = i % N_MIXERS
    s_mix, xh_mix, rs_mix = saved
    grads = {}
    dz, dz_m, grads["ln_mix_g"], grads["ln_mix_b"], dz_sum = _ln_bwd(terms, xh_mix, rs_mix, wl["ln_mix_g"],
                                                                     name=f"l{i}_dlnmix", after=after)
    behind = _hooked(dz, hook)
    if mixer == 0:
        dxs, g_mix = _sc_bwd(dz_m, s_mix, wl["sc_w_in"], wl["sc_conv_w"], wl["sc_w_out"], f"l{i}_sc", dz, after=behind)
    elif mixer == 1:
        dxs, g_mix = _mla_bwd(dz_m, s_mix, wl, f"l{i}_mla", dz, after=behind)
    else:
        dxs, g_mix = _cf_bwd(dz_m, dz_sum, s_mix, wl, f"l{i}_cf", dz, after=behind)
    grads.update(g_mix)
    return [(1.0, v) for v in dxs], grads


def _layer_bwd(i, terms, saved, wl):
    terms, g_ff = _ff_bwd(i, terms, saved[3:], wl)
    terms, g_mix = _mix_bwd(i, terms, saved[:3], wl)
    return terms, {**g_ff, **g_mix}


def _sum_terms(terms, *, name):
    coefs = [c for c, _ in terms]

    def sum_fn(rows, bc):
        acc = None
        for c, v in zip(coefs, rows):
            tv = v if c == 1.0 else c * v
            acc = tv if acc is None else acc + tv
        return (acc,), ()

    return _rowwise(sum_fn, [v for _, v in terms], [], [(D_MODEL, F32)], [], name=name)[0]


def _local_step(x, target, layers):
    cur, cur_m = x, _cast_input(x)
    saved = []
    for i in range(DEPTH):
        cur, cur_m, s = _layer_fwd(i, cur, cur_m, layers[i])
        saved.append(s)
    loss, dy = _loss_and_grad(cur, target)
    terms = [(1.0, dy)]
    grads = [None] * DEPTH
    for i in reversed(range(DEPTH)):
        terms, grads[i] = _layer_bwd(i, terms, saved[i], layers[i])
    return loss, _sum_terms(terms, name="grad_x"), grads


MESH_ID = pl.DeviceIdType.MESH
ANY = pl.BlockSpec(memory_space=pl.ANY)


def _block_id(px, py, pc):
    return 4 * px + 2 * py + pc


COPIES_PER_GATHER = 7


def _all_gather(shards, *, name):
    n = len(shards)

    def body(*refs):
        x_refs, out_refs = refs[:n], refs[n:2 * n]
        send_sems, recv_sems, local_sems = refs[2 * n:]
        x, y, core = lax.axis_index("x"), lax.axis_index("y"), lax.axis_index("c")
        me, sibling = (x, y, core), (x, y, 1 - core)
        chips = [(1 - x, y), (x, 1 - y), (1 - x, 1 - y)]

        def copy(t, k, block, to, from_input=False):
            dst = out_refs[t].at[_block_id(*block)]
            return pltpu.make_async_remote_copy(
                src_ref=x_refs[t] if from_input else dst, dst_ref=dst,
                send_sem=send_sems.at[t * COPIES_PER_GATHER + k], recv_sem=recv_sems.at[t * COPIES_PER_GATHER + k],
                device_id=to, device_id_type=MESH_ID)

        started = []
        mine = [pltpu.make_async_copy(x_refs[t], out_refs[t].at[_block_id(*me)], local_sems.at[t]) for t in range(n)]
        for t in range(n):
            mine[t].start()
            first = [copy(t, 0, me, sibling, from_input=True)]
            first += [copy(t, 1 + j, me, (*chip, core), from_input=True) for j, chip in enumerate(chips)]
            for cp in first:
                cp.start()
            started += first
        for t in range(n):
            for j, chip in enumerate(chips):
                copy(t, 1 + j, (*chip, core), me).wait_recv()
                passed = copy(t, 4 + j, (*chip, core), sibling)
                passed.start()
                started.append(passed)
        for t in range(n):
            copy(t, 0, sibling, me).wait_recv()
            for j, chip in enumerate(chips):
                copy(t, 4 + j, (*chip, 1 - core), me).wait_recv()
        for cp in started:
            cp.wait_send()
        for cp in mine:
            cp.wait()

    n_sems = n * COPIES_PER_GATHER
    outs = pl.pallas_call(
        body,
        name=name,
        out_shape=[jax.ShapeDtypeStruct((N_DEV, *s.shape), s.dtype) for s in shards],
        in_specs=[ANY] * n,
        out_specs=[ANY] * n,
        scratch_shapes=[pltpu.SemaphoreType.DMA((n_sems,)), pltpu.SemaphoreType.DMA((n_sems,)),
                        pltpu.SemaphoreType.DMA((n,))],
    )(*shards)
    return list(outs)


def _swap_with_sibling(parts, *, name):
    n = len(parts)

    def body(*refs):
        p_refs, got_refs = refs[:n], refs[n:2 * n]
        send_sems, recv_sems = refs[2 * n:]
        x, y, core = lax.axis_index("x"), lax.axis_index("y"), lax.axis_index("c")
        chips = [(x, y), (1 - x, y), (x, 1 - y), (1 - x, 1 - y)]
        copies = []
        for t in range(n):
            for k, (px, py) in enumerate(chips):
                copies.append(pltpu.make_async_remote_copy(
                    src_ref=p_refs[t].at[_block_id(px, py, 1 - core)], dst_ref=got_refs[t].at[k],
                    send_sem=send_sems.at[4 * t + k], recv_sem=recv_sems.at[4 * t + k], device_id=(x, y, 1 - core),
                    device_id_type=MESH_ID))
        for cp in copies:
            cp.start()
        for cp in copies:
            cp.wait()

    outs = pl.pallas_call(
        body,
        name=name,
        out_shape=[jax.ShapeDtypeStruct((4, *p.shape[1:]), p.dtype) for p in parts],
        in_specs=[ANY] * n,
        out_specs=[ANY] * n,
        scratch_shapes=[pltpu.SemaphoreType.DMA((4 * n,)), pltpu.SemaphoreType.DMA((4 * n,))],
    )(*parts)
    return list(outs)


def _swap_between_chips(parts, *, name):
    n = len(parts)

    def body(*refs):
        p_refs, got_refs = refs[:n], refs[n:2 * n]
        send_sems, recv_sems = refs[2 * n:]
        x, y, core = lax.axis_index("x"), lax.axis_index("y"), lax.axis_index("c")
        chips = [(1 - x, y), (x, 1 - y), (1 - x, 1 - y)]
        copies = []
        for t in range(n):
            for j, (px, py) in enumerate(chips):
                copies.append(pltpu.make_async_remote_copy(
                    src_ref=p_refs[t].at[j], dst_ref=got_refs[t].at[j], send_sem=send_sems.at[3 * t + j],
                    recv_sem=recv_sems.at[3 * t + j], device_id=(px, py, core), device_id_type=MESH_ID))
        for cp in copies:
            cp.start()
        for cp in copies:
            cp.wait()

    outs = pl.pallas_call(
        body,
        name=name,
        out_shape=[jax.ShapeDtypeStruct(p.shape, p.dtype) for p in parts],
        in_specs=[ANY] * n,
        out_specs=[ANY] * n,
        scratch_shapes=[pltpu.SemaphoreType.DMA((3 * n,)), pltpu.SemaphoreType.DMA((3 * n,))],
    )(*parts)
    return list(outs)


def _chip_partials(parts, got, *, name):
    _, r, c = parts.shape
    tb = _row_tile(r, 512)
    x, y, core = lax.axis_index("x"), lax.axis_index("y"), lax.axis_index("c")
    chips = [(x, y), (1 - x, y), (x, 1 - y), (1 - x, 1 - y)]
    ids = jnp.stack([_block_id(px, py, core) for px, py in chips]).astype(jnp.int32)

    def body(ids_ref, p0, p1, p2, p3, g_ref, own_ref, out_ref):
        own_ref[...] = p0[...].astype(F32) + g_ref[0].astype(F32)
        for j, p in enumerate((p1, p2, p3)):
            out_ref[j] = (p[...].astype(F32) + g_ref[j + 1].astype(F32)).astype(out_ref.dtype)

    def part_spec(k):
        return pl.BlockSpec((None, tb, c), lambda i, ids_ref: (ids_ref[k], i, 0))

    grid_spec = pltpu.PrefetchScalarGridSpec(
        num_scalar_prefetch=1,
        grid=(r // tb,),
        in_specs=[part_spec(k) for k in range(4)] + [pl.BlockSpec((4, tb, c), lambda i, ids_ref: (0, i, 0))],
        out_specs=[pl.BlockSpec((tb, c), lambda i, ids_ref: (i, 0)), pl.BlockSpec((3, tb, c), lambda i, ids_ref: (0, i, 0))],
    )
    return pl.pallas_call(
        body,
        name=name,
        grid_spec=grid_spec,
        out_shape=[jax.ShapeDtypeStruct((r, c), F32), jax.ShapeDtypeStruct((3, r, c), parts.dtype)],
        compiler_params=_params(("parallel",)),
    )(ids, parts, parts, parts, parts, got)


def _chip_partials_group(parts, got, *, name):
    n = len(parts)
    rows = [p.shape[1] for p in parts]
    steps = 4 if all(r % (4 * BF16_ROWS) == 0 for r in rows) else (2 if all(r % (2 * BF16_ROWS) == 0 for r in rows) else 1)
    x, y, core = lax.axis_index("x"), lax.axis_index("y"), lax.axis_index("c")
    chips = [(x, y), (1 - x, y), (x, 1 - y), (1 - x, 1 - y)]
    ids = jnp.stack([_block_id(px, py, core) for px, py in chips]).astype(jnp.int32)

    def body(ids_ref, *refs):
        ins, outs = refs[:5 * n], refs[5 * n:]
        for t in range(n):
            p0, p1, p2, p3, g_ref = ins[5 * t:5 * t + 5]
            own_ref, out_ref = outs[2 * t], outs[2 * t + 1]
            own_ref[...] = p0[...].astype(F32) + g_ref[0].astype(F32)
            for j, p in enumerate((p1, p2, p3)):
                out_ref[j] = (p[...].astype(F32) + g_ref[j + 1].astype(F32)).astype(out_ref.dtype)

    in_specs, out_specs, out_shape, operands = [], [], [], []
    for p, g in zip(parts, got):
        _, r, c = p.shape
        tb = r // steps
        in_specs += [pl.BlockSpec((None, tb, c), lambda i, ids_ref, k=k: (ids_ref[k], i, 0)) for k in range(4)]
        in_specs.append(pl.BlockSpec((4, tb, c), lambda i, ids_ref: (0, i, 0)))
        out_specs += [pl.BlockSpec((tb, c), lambda i, ids_ref: (i, 0)), pl.BlockSpec((3, tb, c), lambda i, ids_ref: (0, i, 0))]
        out_shape += [jax.ShapeDtypeStruct((r, c), F32), jax.ShapeDtypeStruct((3, r, c), p.dtype)]
        operands += [p, p, p, p, g]
    outs = pl.pallas_call(
        body,
        name=name,
        grid_spec=pltpu.PrefetchScalarGridSpec(num_scalar_prefetch=1, grid=(steps,), in_specs=in_specs,
                                               out_specs=out_specs),
        out_shape=out_shape,
        compiler_params=_params(("parallel",)),
    )(ids, *operands)
    return [(outs[2 * t], outs[2 * t + 1]) for t in range(n)]


def _sum_blocks(first, blocks, *, name):
    n, r, c = blocks.shape
    tb = _row_tile(r, 512)

    def body(*refs):
        b_ref, o_ref = refs[-2], refs[-1]
        acc = refs[0][...] if first is not None else b_ref[0].astype(F32)
        for j in range(0 if first is not None else 1, n):
            acc = acc + b_ref[j].astype(F32)
        o_ref[...] = acc

    row_spec = pl.BlockSpec((tb, c), lambda i: (i, 0))
    return pl.pallas_call(
        body,
        name=name,
        grid=(r // tb,),
        in_specs=([row_spec] if first is not None else []) + [pl.BlockSpec((n, tb, c), lambda i: (0, i, 0))],
        out_specs=row_spec,
        out_shape=jax.ShapeDtypeStruct((r, c), F32),
        compiler_params=_params(("parallel",)),
    )(*([first] if first is not None else []), blocks)


def _reduce_scatter(parts, *, tag):
    got = _swap_with_sibling(parts, name=f"{tag}_sibling")
    sums = [_chip_partials(p, g, name=f"{tag}_chipsum{t}") for t, (p, g) in enumerate(zip(parts, got))]
    arrived = _swap_between_chips([others for _, others in sums], name=f"{tag}_chips")
    return [(own, arr) for (own, _), arr in zip(sums, arrived)]


HBM = pl.BlockSpec(memory_space=pltpu.HBM)
SEM = pl.BlockSpec(memory_space=pltpu.SEMAPHORE)
DATAFLOW = pltpu.SideEffectType.DATAFLOW_SIDE_EFFECTING


def _tie(value, *before):
    return lax.optimization_barrier((value, *before))[0]


def _split_call(name, arrays, sems, n_new, body, after=None):
    n_a, n_s = len(arrays), len(sems)
    after = [] if after is None else (list(after) if isinstance(after, (list, tuple)) else [after])
    n_after = len(after)

    def kernel_body(*refs):
        new = refs[n_a + n_s + n_after:n_a + n_s + n_after + 2] if n_new else (None, None)
        body(refs[:n_a], refs[n_a:n_a + n_s], *new)
        refs[-1][...] = jnp.zeros_like(refs[-1])

    n_sem_out = 2 if n_new else 0
    out_shape = [pltpu.SemaphoreType.DMA((n_new,))] * n_sem_out
    out_shape += [pltpu.HBM(a.shape, a.dtype) for a in arrays]
    out_shape += [jax.ShapeDtypeStruct((SUBLANES, LANES), F32)]
    outs = pl.pallas_call(
        kernel_body,
        name=name,
        out_shape=out_shape,
        in_specs=[HBM] * n_a + [SEM] * n_s + [ANY] * n_after,
        out_specs=[SEM] * n_sem_out + [HBM] * n_a + [pl.BlockSpec(memory_space=pltpu.VMEM)],
        input_output_aliases={i: n_sem_out + i for i in range(n_a)},
        compiler_params=pltpu.CompilerParams(has_side_effects=DATAFLOW),
    )(*[pltpu.with_memory_space_constraint(a, pltpu.HBM) for a in arrays], *sems, *after)
    return tuple(outs[:n_sem_out]), list(outs[n_sem_out:n_sem_out + n_a]), outs[-1]


def _remote(src, dst, send, recv, k, to):
    return pltpu.make_async_remote_copy(src_ref=src, dst_ref=dst, send_sem=send.at[k], recv_sem=recv.at[k],
                                        device_id=to, device_id_type=MESH_ID)


def _land_block(land_ref, block, by_cols, width):
    if by_cols:
        return land_ref.at[:, pl.ds(pl.multiple_of(block * width, LANES), width)]
    return land_ref.at[block]


def _place_in_window(shard, *, name):
    r, c = shard.shape
    tb = _row_tile(r, 512)
    me = _block_id(lax.axis_index("x"), lax.axis_index("y"), lax.axis_index("c")).astype(jnp.int32).reshape(1)

    def body(me_ref, s_ref, o_ref, token_ref):
        o_ref[...] = s_ref[...].astype(o_ref.dtype)
        token_ref[...] = jnp.zeros_like(token_ref)

    return pl.pallas_call(
        body,
        name=name,
        grid_spec=pltpu.PrefetchScalarGridSpec(
            num_scalar_prefetch=1, grid=(r // tb,),
            in_specs=[pl.BlockSpec((tb, c), lambda i, me_ref: (i, 0))],
            out_specs=[pl.BlockSpec((tb, c), lambda i, me_ref: (i, me_ref[0])),
                       pl.BlockSpec((SUBLANES, LANES), lambda i, me_ref: (0, 0))]),
        out_shape=[jax.ShapeDtypeStruct((r, N_DEV * c), MXU_DTYPE), jax.ShapeDtypeStruct((SUBLANES, LANES), F32)],
        compiler_params=_params(("arbitrary",)),
    )(me, shard)


def _gather_prepare(shards, by_cols, *, name):
    me = _block_id(lax.axis_index("x"), lax.axis_index("y"), lax.axis_index("c"))
    lands, early = [], []
    for i, (s, cols) in enumerate(zip(shards, by_cols)):
        if cols:
            land, token = _place_in_window(s, name=f"{name}_place{i}")
            lands.append(land)
            early.append(token)
        else:
            lands.append(lax.dynamic_update_index_in_dim(lax.empty((N_DEV, *s.shape), s.dtype), s, me, 0))
            early.append(s)
    return lands, [s.shape[-1] for s in shards], list(by_cols), early


def _gather_start(shards, *, name, after=None, by_cols=None, prepared=None):
    if prepared is None:
        prepared = _gather_prepare(shards, by_cols if by_cols is not None else [False] * len(shards), name=name)
    lands, widths, by_cols, _ = prepared
    n = len(lands)

    def body(refs, _, send, recv):
        x, y, core = lax.axis_index("x"), lax.axis_index("y"), lax.axis_index("c")
        targets = [(x, y, 1 - core), (1 - x, y, core), (x, 1 - y, core), (1 - x, 1 - y, core)]
        for t in range(n):
            mine = _land_block(refs[t], _block_id(x, y, core), by_cols[t], widths[t])
            for k, to in enumerate(targets):
                _remote(mine, mine, send, recv, 4 * t + k, to).start()

    sems, thru, token = _split_call(name, lands, [], 4 * n, body, after=after)
    return {"n": n, "sems1": sems, "arrays": thru, "token": token, "by_cols": by_cols, "widths": widths}


def _gather_pass_on(st, *, name, after):
    n = st["n"]

    def body(refs, sems, send, recv):
        send1, recv1 = sems
        x, y, core = lax.axis_index("x"), lax.axis_index("y"), lax.axis_index("c")
        chips = [(1 - x, y), (x, 1 - y), (1 - x, 1 - y)]
        for t in range(n):
            for j, (px, py) in enumerate(chips):
                block = _land_block(refs[t], _block_id(px, py, core), st["by_cols"][t], st["widths"][t])
                _remote(block, block, send1, recv1, 4 * t + 1 + j, (x, y, 1 - core)).wait_recv()
                _remote(block, block, send, recv, 3 * t + j, (x, y, 1 - core)).start()

    sems2, thru, token = _split_call(name, st["arrays"], list(st["sems1"]), 3 * n, body, after=after)
    return {**st, "sems2": sems2, "arrays": thru, "token": token}


def _gather_finish(st, *, name, after):
    n = st["n"]

    def body(refs, sems, *_):
        send1, recv1, send2, recv2 = sems
        x, y, core = lax.axis_index("x"), lax.axis_index("y"), lax.axis_index("c")
        sibling = (x, y, 1 - core)
        chips = [(1 - x, y), (x, 1 - y), (1 - x, 1 - y)]
        for t in range(n):
            place = lambda px, py, pc, t=t: _land_block(refs[t], _block_id(px, py, pc), st["by_cols"][t],
                                                        st["widths"][t])
            mine = place(x, y, core)
            for k in range(4):
                _remote(mine, mine, send1, recv1, 4 * t + k, sibling).wait_send()
            _remote(mine, place(x, y, 1 - core), send1, recv1, 4 * t, sibling).wait_recv()
            for j, (px, py) in enumerate(chips):
                _remote(place(px, py, core), place(px, py, core), send2, recv2, 3 * t + j, sibling).wait_send()
                _remote(mine, place(px, py, 1 - core), send2, recv2, 3 * t + j, sibling).wait_recv()

    _, thru, _ = _split_call(name, st["arrays"], [*st["sems1"], *st["sems2"]], 0, body, after=after)
    return thru


def _swap_start(sources, lands, plan, per_array, *, name):
    n = len(sources)

    def body(refs, _, send, recv):
        x, y, core = lax.axis_index("x"), lax.axis_index("y"), lax.axis_index("c")
        for t in range(n):
            for k in range(per_array):
                block, to = plan(x, y, core, k)
                _remote(refs[t].at[block], refs[n + t].at[k], send, recv, per_array * t + k, to).start()

    sems, thru, token = _split_call(name, sources + lands, [], per_array * n, body)
    return {"n": n, "sems": sems, "arrays": thru, "token": token, "plan": plan, "per_array": per_array}


def _swap_finish(st, *, name, after):
    n, plan, per_array = st["n"], st["plan"], st["per_array"]

    def body(refs, sems, *_):
        send, recv = sems
        x, y, core = lax.axis_index("x"), lax.axis_index("y"), lax.axis_index("c")
        for t in range(n):
            for k in range(per_array):
                block, to = plan(x, y, core, k)
                cp = _remote(refs[t].at[block], refs[n + t].at[k], send, recv, per_array * t + k, to)
                cp.wait_send()
                cp.wait_recv()

    _, thru, _ = _split_call(name, st["arrays"], list(st["sems"]), 0, body, after=after)
    return thru[:n], thru[n:]


def _to_sibling(x, y, core, k):
    px, py = [(x, y), (1 - x, y), (x, 1 - y), (1 - x, 1 - y)][k]
    return _block_id(px, py, 1 - core), (x, y, 1 - core)


def _to_chip(x, y, core, k):
    px, py = [(1 - x, y), (x, 1 - y), (1 - x, 1 - y)][k]
    return k, (px, py, core)


PARAMS = {
    "sc_w_in": ((2, 1024, 3072), 2), "sc_conv_w": ((2, 3, 1024), 2), "sc_w_out": ((2, 1024, 1024), 1),
    "mla_w_dq": ((1, 1024, 384), 1), "mla_g_q": ((1, 384), None), "mla_w_uq": ((1, 384, 1536), 2),
    "mla_w_dkv": ((1, 1024, 320), 1), "mla_g_kv": ((1, 256), None), "mla_w_uk": ((1, 256, 8, 128), 1),
    "mla_w_uv": ((1, 256, 8, 128), 1), "mla_w_o": ((1, 1024, 1024), 1), "cf_w_pw1": ((1, 1024, 2048), 2),
    "cf_b_pw1": ((1, 2048), 1), "cf_dw_w": ((1, 31, 1024), 2), "cf_dw_b": ((1, 1024), 1), "cf_norm_g": ((1, 1024), 1),
    "cf_norm_b": ((1, 1024), 1), "cf_w_pw2": ((1, 1024, 1024), 1), "cf_b_pw2": ((1, 1024), 1),
    "ff_w1": ((4, 1024, 4096), 2), "ff_w2": ((4, 4096, 1024), 1), "ln_mix_g": ((4, 1024), None),
    "ln_mix_b": ((4, 1024), None), "ln_ff_g": ((4, 1024), None), "ln_ff_b": ((4, 1024), None),
}
NAMES = list(PARAMS)
BIG = ["sc_w_in", "sc_w_out", "mla_w_dq", "mla_w_uq", "mla_w_dkv", "mla_w_uk", "mla_w_uv", "mla_w_o", "cf_w_pw1",
       "cf_w_pw2", "ff_w1", "ff_w2"]
SMALL = [n for n in NAMES if n not in BIG]
SMALL_SHARDED = [n for n in SMALL if PARAMS[n][1] is not None]
SMALL_WIDTH = LANES
DEV_BLOCKED = ("sc_w_in", "cf_w_pw1", "ff_w1")
MIXER_BIG = {0: ["sc_w_in", "sc_w_out"], 1: ["mla_w_dq", "mla_w_uq", "mla_w_dkv", "mla_w_uk", "mla_w_uv", "mla_w_o"],
             2: ["cf_w_pw1", "cf_w_pw2"]}


def _layer_big(i):
    return [(n, i // N_MIXERS) for n in MIXER_BIG[i % N_MIXERS]] + [("ff_w1", i), ("ff_w2", i)]


def _shard_2d(name, a):
    return a.reshape(a.shape[0], -1)


def _shard_for_gather(name, a):
    flat = _shard_2d(name, a)
    return flat if name in DEV_BLOCKED else flat.astype(MXU_DTYPE)


def _as_operand(name, g):
    if name in DEV_BLOCKED:
        return g
    if name == "mla_w_uq":
        by_head = g.transpose(1, 0, 2)
        pieces = [by_head[:, :, :QK_NOPE], by_head[:, :, QK_NOPE:QK_NOPE + ROPE_HALF], by_head[:, :, QK_NOPE + ROPE_HALF:]]
        return jnp.concatenate([p.reshape(g.shape[1], -1) for p in pieces], axis=1)
    return g.reshape(-1, g.shape[2])


def _as_parts(name, grad):
    if name in DEV_BLOCKED:
        return grad
    if name == "mla_w_uq":
        rows = grad.shape[0]
        pieces = [grad[:, :Q_NOPE_COLS].reshape(rows, MLA_HEADS, QK_NOPE),
                  grad[:, Q_NOPE_COLS:Q_NOPE_COLS + Q_HALF_COLS].reshape(rows, MLA_HEADS, ROPE_HALF),
                  grad[:, Q_NOPE_COLS + Q_HALF_COLS:].reshape(rows, MLA_HEADS, ROPE_HALF)]
        return jnp.concatenate(pieces, axis=2).transpose(1, 0, 2)
    return grad.reshape(N_DEV, -1, grad.shape[1])


def _shard_shape(name):
    shape, ax = PARAMS[name]
    if ax is None:
        return shape
    return tuple(s // N_DEV if i == ax else s for i, s in enumerate(shape))


def _rows_of(shape, width, align):
    n = 1
    for s in shape:
        n *= s
    rows = -(-n // width)
    return -(-rows // align) * align


def _pack_rows(arrays, width, align, dtype, lead=0):
    segs = []
    for a in arrays:
        batch = a.shape[:lead]
        flat = a.astype(dtype).reshape(*batch, -1)
        rows = _rows_of(a.shape[lead:], width, align)
        flat = jnp.pad(flat, [(0, 0)] * lead + [(0, rows * width - flat.shape[-1])])
        segs.append(flat.reshape(*batch, rows, width))
    return jnp.concatenate(segs, axis=lead)


def _unpack_rows(buf, shapes, width, align):
    out, off = [], 0
    lead = buf.shape[:-2]
    for shape in shapes:
        n = 1
        for s in shape:
            n *= s
        rows = _rows_of(shape, width, align)
        seg = buf[..., off:off + rows, :].reshape(*lead, rows * width)[..., :n]
        out.append(seg.reshape(*lead, *shape))
        off += rows
    return out


def _merge_shards(stacked, ax):
    moved = jnp.moveaxis(stacked, 0, ax)
    shape = moved.shape
    return moved.reshape(*shape[:ax], shape[ax] * shape[ax + 1], *shape[ax + 2:])


def _split_shards(full, ax):
    shape = full.shape
    split = full.reshape(*shape[:ax], N_DEV, shape[ax] // N_DEV, *shape[ax + 1:])
    return jnp.moveaxis(split, ax, 0)


def _adam_math(wv, gv, mv, vv):
    m_new = ADAM_B1 * mv + (1.0 - ADAM_B1) * gv
    v_new = ADAM_B2 * vv + (1.0 - ADAM_B2) * (gv * gv)
    m_hat = m_new / (1.0 - ADAM_B1 ** ADAM_STEP)
    v_hat = v_new / (1.0 - ADAM_B2 ** ADAM_STEP)
    delta = -ADAM_LR * (m_hat / (jnp.sqrt(v_hat) + ADAM_EPS) + ADAM_WD * wv)
    return delta, m_new, v_new


def _adamw(w, g, m, v, *, name):
    shape = w.shape
    c = shape[-1]

    def fn(rows, bc):
        return _adam_math(*rows), ()

    flat = [a.reshape(-1, c) for a in (w, g, m, v)]
    outs = _rowwise(fn, flat, [], [(c, F32)] * 3, [], name=name, tb_cap=512)
    return tuple(o.reshape(shape) for o in outs)


def _adamw_reduced(w, m, v, layer_grads, *, name, after=()):
    n_l, r, c = w.shape
    tb = _row_tile(r, 256 if c > 512 else 512)
    nb = r // tb

    def body(*refs):
        w_ref, m_ref, v_ref = refs[:3]
        g_refs = refs[3:3 + 2 * n_l]
        g_out, d_out, m_out, v_out = refs[3 + 2 * n_l + len(after):]
        layer = pl.program_id(0)
        for l in range(n_l):
            @pl.when(layer == l)
            def _(l=l):
                own, arr = g_refs[2 * l], g_refs[2 * l + 1]
                g = own[...] + arr[0].astype(F32)
                g = g + arr[1].astype(F32)
                g = g + arr[2].astype(F32)
                delta, m_new, v_new = _adam_math(w_ref[...], g, m_ref[...], v_ref[...])
                g_out[...] = g
                d_out[...] = delta
                m_out[...] = m_new
                v_out[...] = v_new

    stacked = pl.BlockSpec((tb, c), lambda l, i: (l * nb + i, 0))
    in_specs = [stacked] * 3
    operands = [a.reshape(n_l * r, c) for a in (w, m, v)]
    for l, (own, arr) in enumerate(layer_grads):
        in_specs.append(pl.BlockSpec((tb, c), lambda ll, i, l=l: (jnp.where(ll == l, i, 0), 0)))
        in_specs.append(pl.BlockSpec((3, tb, c), lambda ll, i, l=l: (0, jnp.where(ll == l, i, 0), 0)))
        operands += [own, arr]
    in_specs += [ANY] * len(after)
    operands += list(after)
    outs = pl.pallas_call(
        body,
        name=name,
        grid=(n_l, nb),
        in_specs=in_specs,
        out_specs=[stacked] * 4,
        out_shape=[jax.ShapeDtypeStruct((n_l * r, c), F32)] * 4,
        compiler_params=_params(("arbitrary", "arbitrary")),
    )(*operands)
    return tuple(o.reshape(n_l, r, c) for o in outs)


def _train_step(x, target, weights, m_state, v_state):
    me = _block_id(lax.axis_index("x"), lax.axis_index("y"), lax.axis_index("c"))

    groups = [(kind, i) for i in range(DEPTH) for kind in ("mix", "ff")]
    forward = [groups[0], ("ff_up", 0), ("ff_down", 0)] + groups[2:]

    def group_names(g):
        kind, i = g
        if kind == "mix":
            return [(n, i // N_MIXERS) for n in MIXER_BIG[i % N_MIXERS]]
        return [(n, i) for n in {"ff": ["ff_w1", "ff_w2"], "ff_up": ["ff_w1"], "ff_down": ["ff_w2"]}[kind]]

    def shards_of(g):
        return [_shard_for_gather(n, weights[n][j]) for n, j in group_names(g)]

    whole = lambda g: [n in DEV_BLOCKED for n, _ in group_names(g)]

    small_shapes = [_shard_shape(n) for n in SMALL_SHARDED]
    gather = {0: _gather_start([_pack_rows([weights[n] for n in SMALL_SHARDED], SMALL_WIDTH, SUBLANES, F32)]
                               + shards_of(groups[0]), name="gather_g0_start", by_cols=[False] + whole(groups[0]))}
    cur, cur_m = x[0], x[0]
    prepared = {k: _gather_prepare(shards_of(forward[k]), whole(forward[k]), name=f"gather_g{k}_start")
                for k in range(1, len(forward))}
    early = [a for k in prepared for a in prepared[k][3]]
    gather[0] = _gather_pass_on(gather[0], name="gather_g0_pass", after=early)
    first = _gather_finish(gather[0], name="gather_g0_finish", after=gather[0]["token"])
    small_all, first_matrices = first[0], first[1:]
    small_full = {n: weights[n] for n in SMALL if PARAMS[n][1] is None}
    for n, stacked in zip(SMALL_SHARDED, _unpack_rows(small_all, small_shapes, SMALL_WIDTH, SUBLANES)):
        small_full[n] = _merge_shards(stacked, PARAMS[n][1])

    def small_weights(i):
        wl = {}
        for n in SMALL:
            if n.startswith("ln_"):
                wl[n] = small_full[n][i:i + 1]
            elif n.startswith(("sc_", "mla_", "cf_")[i % N_MIXERS]):
                v = small_full[n][i // N_MIXERS]
                wl[n] = v if v.ndim == 2 else v.reshape(1, -1)
        return wl

    layers = [small_weights(i) for i in range(DEPTH)]

    def install(g, gathered):
        for (n, _), a in zip(group_names(g), gathered):
            layers[g[1]][n] = _as_operand(n, a)

    install(forward[0], first_matrices)
    gather[1] = _gather_start(None, name="gather_g1_start", after=gather[0]["token"], prepared=prepared[1])
    saved = {}
    hidden = None
    for k, g in enumerate(forward):
        kind, i = g
        tokens = [gather[1]["token"]] if k == 0 else []
        if k + 2 < len(forward):
            gather[k + 2] = _gather_start(None, name=f"gather_g{k + 2}_start", after=gather[k + 1]["token"],
                                          prepared=prepared[k + 2])
            tokens.append(gather[k + 2]["token"])

        def pass_on(y, k=k):
            if k + 1 == len(forward):
                return None
            gather[k + 1] = _gather_pass_on(gather[k + 1], name=f"gather_g{k + 1}_pass", after=y)
            return gather[k + 1]["token"]

        if k + 1 == len(forward):
            loss_local, dy, saved[g] = _ff_fwd(i, cur, cur_m, layers[i], after=tokens, loss_target=target[0])
            break
        if kind == "ff_up":
            hidden = _mlp_up(cur_m, layers[i]["ff_w1"], f"l{i}_ff", after=tokens)
            behind, done = _hooked(hidden, pass_on), hidden
        elif kind == "ff_down":
            cur, cur_m, saved[("ff", i)] = _ff_fwd(i, cur, cur_m, layers[i], hook=pass_on, after=[*tokens, *behind],
                                                   act=hidden)
            done = cur
        else:
            phase = _mix_fwd if kind == "mix" else _ff_fwd
            cur, cur_m, saved[g] = phase(i, cur, cur_m, layers[i], hook=pass_on, after=tokens)
            done = cur
        install(forward[k + 1], _gather_finish(gather[k + 1], name=f"gather_g{k + 1}_finish", after=done))

    loss = lax.psum(loss_local, MESH_AXES)

    reduced = {n: [None] * PARAMS[n][0][0] for n in BIG}
    small_grads = {n: [None] * PARAMS[n][0][0] for n in SMALL}
    reduce = {}

    def to_chips(p, after):
        st = reduce[p]
        parts, got = _swap_finish(st["sibling"], name=f"reduce_p{p}_sibling_finish", after=after)
        sums = _chip_partials_group(parts, got, name=f"reduce_p{p}_chipsum")
        st["own"] = [own for own, _ in sums]
        others = [o for _, o in sums]
        st["chips"] = _swap_start(others, [lax.empty(o.shape, o.dtype) for o in others], _to_chip, 3,
                                  name=f"reduce_p{p}_chips_start")
        return st["chips"]["token"]

    def arrived(p, after):
        st = reduce[p]
        _, got = _swap_finish(st["chips"], name=f"reduce_p{p}_chips_finish", after=after)
        for (n, j), own, arr in zip(st["names"], st["own"], got):
            reduced[n][j] = (own, arr)

    terms = [(1.0, dy)]
    back = list(reversed(groups))
    for p, g in enumerate(back):
        kind, i = g

        def after_first_kernel(dz, p=p):
            return to_chips(p - 1, dz) if p >= 1 else None

        phase = _mix_bwd if kind == "mix" else _ff_bwd
        tokens = [reduce[p - 1]["sibling"]["token"]] if p >= 1 else []
        terms, grads = phase(i, terms, saved[g], layers[i], hook=after_first_kernel, after=tokens)
        for n in SMALL:
            if n in grads:
                small_grads[n][i if n.startswith("ln_") else i // N_MIXERS] = grads[n]
        if p >= 2:
            arrived(p - 2, terms[-1][1])
        names = group_names(g)
        parts = [_as_parts(n, grads[n]) for n, _ in names]
        reduce[p] = {"names": names, "sibling": _swap_start(
            parts, [lax.empty((4, *a.shape[1:]), a.dtype) for a in parts], _to_sibling, 4,
            name=f"reduce_p{p}_sibling_start")}
    (_, grad_x), = terms
    last = len(back) - 1
    small_mine = _pack_rows([jnp.stack(small_grads[n]).reshape(PARAMS[n][0]) for n in SMALL], SMALL_WIDTH, SUBLANES, F32)
    small_gather = _gather_start([small_mine], name="gather_small_grads_start")
    tail = to_chips(last, small_gather["token"])
    arrived(last - 1, tail)

    grad, delta, new_m, new_v = {}, {}, {}, {}

    def adamw_matrix(n, after):
        shape = weights[n].shape
        view = lambda a: a.reshape(shape[0], shape[1], -1)
        outs = _adamw_reduced(view(weights[n]), view(m_state[n]), view(v_state[n]), reduced[n], name=f"adamw_{n}",
                              after=after)
        grad[n], delta[n], new_m[n], new_v[n] = [o.reshape(shape) for o in outs]
        return outs[0]

    done = tail
    for n in BIG:
        if n not in MIXER_BIG[0]:
            done = adamw_matrix(n, [done])
    small_gather = _gather_pass_on(small_gather, name="gather_small_grads_pass", after=done)
    small_all = _gather_finish(small_gather, name="gather_small_grads_finish", after=small_gather["token"])[0]
    small_sum = _sum_blocks(None, small_all, name="sum_small_grads")
    for n, g in zip(SMALL, _unpack_rows(small_sum, [PARAMS[n][0] for n in SMALL], SMALL_WIDTH, SUBLANES)):
        ax = PARAMS[n][1]
        grad[n] = g if ax is None else lax.dynamic_index_in_dim(_split_shards(g, ax), me, axis=0, keepdims=False)
    small_shards = [_shard_shape(n) for n in SMALL]
    packed = [_pack_rows([src[n] for n in SMALL], SMALL_WIDTH, SUBLANES, F32) for src in (weights, grad, m_state, v_state)]
    small_out = _adamw(*packed, name="adamw_small")
    for outs, dst in zip(small_out, (delta, new_m, new_v)):
        for n, a in zip(SMALL, _unpack_rows(outs, small_shards, SMALL_WIDTH, SUBLANES)):
            dst[n] = a
    arrived(last, small_sum)
    for n in MIXER_BIG[0]:
        adamw_matrix(n, [])
    return (loss, grad_x[None], *[grad[n] for n in NAMES], *[delta[n] for n in NAMES],
            *[new_m[n] for n in NAMES], *[new_v[n] for n in NAMES])


def kernel(x, sc_w_in, sc_conv_w, sc_w_out, mla_w_dq, mla_g_q, mla_w_uq, mla_w_dkv, mla_g_kv, mla_w_uk, mla_w_uv, mla_w_o, cf_w_pw1, cf_b_pw1, cf_dw_w, cf_dw_b, cf_norm_g, cf_norm_b, cf_w_pw2, cf_b_pw2, ff_w1, ff_w2, ln_mix_g, ln_mix_b, ln_ff_g, ln_ff_b, loss_target, m_sc_w_in, m_sc_conv_w, m_sc_w_out, m_mla_w_dq, m_mla_g_q, m_mla_w_uq, m_mla_w_dkv, m_mla_g_kv, m_mla_w_uk, m_mla_w_uv, m_mla_w_o, m_cf_w_pw1, m_cf_b_pw1, m_cf_dw_w, m_cf_dw_b, m_cf_norm_g, m_cf_norm_b, m_cf_w_pw2, m_cf_b_pw2, m_ff_w1, m_ff_w2, m_ln_mix_g, m_ln_mix_b, m_ln_ff_g, m_ln_ff_b, v_sc_w_in, v_sc_conv_w, v_sc_w_out, v_mla_w_dq, v_mla_g_q, v_mla_w_uq, v_mla_w_dkv, v_mla_g_kv, v_mla_w_uk, v_mla_w_uv, v_mla_w_o, v_cf_w_pw1, v_cf_b_pw1, v_cf_dw_w, v_cf_dw_b, v_cf_norm_g, v_cf_norm_b, v_cf_w_pw2, v_cf_b_pw2, v_ff_w1, v_ff_w2, v_ln_mix_g, v_ln_mix_b, v_ln_ff_g, v_ln_ff_b):
    w_list = (sc_w_in, sc_conv_w, sc_w_out, mla_w_dq, mla_g_q, mla_w_uq, mla_w_dkv, mla_g_kv, mla_w_uk, mla_w_uv, mla_w_o,
              cf_w_pw1, cf_b_pw1, cf_dw_w, cf_dw_b, cf_norm_g, cf_norm_b, cf_w_pw2, cf_b_pw2, ff_w1, ff_w2, ln_mix_g,
              ln_mix_b, ln_ff_g, ln_ff_b)
    m_list = (m_sc_w_in, m_sc_conv_w, m_sc_w_out, m_mla_w_dq, m_mla_g_q, m_mla_w_uq, m_mla_w_dkv, m_mla_g_kv, m_mla_w_uk,
              m_mla_w_uv, m_mla_w_o, m_cf_w_pw1, m_cf_b_pw1, m_cf_dw_w, m_cf_dw_b, m_cf_norm_g, m_cf_norm_b, m_cf_w_pw2,
              m_cf_b_pw2, m_ff_w1, m_ff_w2, m_ln_mix_g, m_ln_mix_b, m_ln_ff_g, m_ln_ff_b)
    v_list = (v_sc_w_in, v_sc_conv_w, v_sc_w_out, v_mla_w_dq, v_mla_g_q, v_mla_w_uq, v_mla_w_dkv, v_mla_g_kv, v_mla_w_uk,
              v_mla_w_uv, v_mla_w_o, v_cf_w_pw1, v_cf_b_pw1, v_cf_dw_w, v_cf_dw_b, v_cf_norm_g, v_cf_norm_b, v_cf_w_pw2,
              v_cf_b_pw2, v_ff_w1, v_ff_w2, v_ln_mix_g, v_ln_mix_b, v_ln_ff_g, v_ln_ff_b)
    return _train_step(x, loss_target, dict(zip(NAMES, w_list)), dict(zip(NAMES, m_list)), dict(zip(NAMES, v_list)))
```

```python
import functools

import jax
import jax.numpy as jnp
from jax import lax
from jax.experimental import pallas as pl
from jax.experimental.pallas import tpu as pltpu

F32 = jnp.float32
MXU_DTYPE = jnp.bfloat16

N_DEV = 8
D_MODEL = 1024
DEPTH = 4
N_MIXERS = 3
CHUNK = 64
ALPHA = (2.0 * DEPTH) ** 0.25
LN_EPS = 1e-5
RMS_EPS = 1e-6
MLA_HEADS = 8
QK_NOPE = 128
QK_ROPE = 64
V_HEAD = 128
Q_LORA = 384
KV_LORA = 256
ROPE_THETA = 10000.0
ADAM_LR = 0.001
ADAM_B1 = 0.9
ADAM_B2 = 0.999
ADAM_EPS = 1e-08
ADAM_WD = 0.01
ADAM_STEP = 10

LANES = 128
SUBLANES = 8
BF16_ROWS = 16
VMEM_LIMIT = 56 * 1024 * 1024
MM_VMEM_BUDGET = 36 * 1024 * 1024
CONV_PAD = 32
MASK_VALUE = -1e30

MESH_AXES = ("x", "y", "c")


def _params(semantics):
    return pltpu.CompilerParams(dimension_semantics=semantics, vmem_limit_bytes=VMEM_LIMIT)


def _tile(n, cap):
    if n <= cap:
        return n
    t = cap - cap % LANES
    while t >= LANES:
        if n % t == 0:
            return t
        t -= LANES
    raise ValueError(f"no tile for {n} under {cap}")


def _row_tile(rows, cap):
    if rows <= cap:
        return rows
    t = cap - cap % SUBLANES
    while t >= SUBLANES:
        if rows % t == 0:
            return t
        t -= SUBLANES
    raise ValueError(f"no row tile for {rows} under {cap}")


def _mm(a, b, *, name, ta=False, tb=False, b_dev=False, out_dev=False, out_dtypes=(F32,), epilogue=None,
        tile_extras=(), col_extras=(), after=(), tm_cap=2048, tn_cap=1024, tk_cap=4096):
    m, k = (a.shape[1], a.shape[0]) if ta else a.shape
    if b_dev:
        nd, b_rows, ns = b.shape
        n, kb = (b_rows, nd * ns) if tb else (nd * ns, b_rows)
    else:
        n, kb = (b.shape[0], b.shape[1]) if tb else (b.shape[1], b.shape[0])
    assert k == kb, (a.shape, b.shape, ta, tb)
    if ta or tile_extras or (b_dev and tb):
        tm_cap = min(tm_cap, 1024)
    tk = ns if (b_dev and tb) else _tile(k, tk_cap)
    nk = k // tk
    fixed_tn = ns if (b_dev and not tb) else (n // N_DEV if out_dev else None)
    if out_dev:
        assert not tile_extras and not col_extras and n % N_DEV == 0
    out_bytes = sum(jnp.dtype(dt).itemsize for dt in out_dtypes) + sum(e.dtype.itemsize for e in tile_extras)
    tm, tn = None, None
    for cand_m, cand_n in ((tm_cap, tn_cap), (tm_cap, tn_cap // 2), (tm_cap // 2, tn_cap), (tm_cap // 2, tn_cap // 2),
                           (tm_cap // 4, tn_cap // 2), (tm_cap // 4, tn_cap // 4)):
        tm, tn = _tile(m, cand_m), (fixed_tn if fixed_tn is not None else _tile(n, cand_n))
        need = 2 * (tm * tk * a.dtype.itemsize + tk * tn * b.dtype.itemsize) + 2 * tm * tn * out_bytes
        need += tm * tn * 4 * (2 if nk > 1 else 1)
        if need <= MM_VMEM_BUDGET:
            break
    n_te, n_ce, n_out = len(tile_extras), len(col_extras), len(out_dtypes)
    dims = (((0 if ta else 1,), (1 if tb else 0,)), ((), ()))

    def body(*refs):
        a_ref, b_ref = refs[0], refs[1]
        te_refs = refs[2:2 + n_te]
        ce_refs = refs[2 + n_te:2 + n_te + n_ce]
        first_out = 2 + n_te + n_ce + len(after)
        out_refs = refs[first_out:first_out + n_out]

        def finish(acc):
            if epilogue is None:
                outs = (acc,)
            else:
                outs = epilogue(acc, *[r[...] for r in te_refs], *[r[...] for r in ce_refs])
            for o_ref, o in zip(out_refs, outs):
                o_ref[...] = o.astype(o_ref.dtype)

        def product():
            return lax.dot_general(a_ref[...].astype(MXU_DTYPE), b_ref[...].astype(MXU_DTYPE), dims,
                                   preferred_element_type=F32)

        if nk == 1:
            finish(product())
        else:
            acc_ref = refs[-1]
            kk = pl.program_id(2)

            @pl.when(kk == 0)
            def _():
                acc_ref[...] = jnp.zeros_like(acc_ref)

            acc_ref[...] += product()

            @pl.when(kk == nk - 1)
            def _():
                finish(acc_ref[...])

    a_spec = pl.BlockSpec((tk, tm), lambda i, j, kk: (kk, i)) if ta else pl.BlockSpec((tm, tk), lambda i, j, kk: (i, kk))
    if b_dev and tb:
        b_spec = pl.BlockSpec((None, tn, tk), lambda i, j, kk: (kk, j, 0))
    elif b_dev:
        b_spec = pl.BlockSpec((None, tk, tn), lambda i, j, kk: (j, kk, 0))
    elif tb:
        b_spec = pl.BlockSpec((tn, tk), lambda i, j, kk: (j, kk))
    else:
        b_spec = pl.BlockSpec((tk, tn), lambda i, j, kk: (kk, j))
    tile_spec = pl.BlockSpec((tm, tn), lambda i, j, kk: (i, j))
    col_spec = pl.BlockSpec((1, tn), lambda i, j, kk: (0, j))
    if out_dev:
        out_spec = pl.BlockSpec((None, tm, tn), lambda i, j, kk: (j, i, 0))
        out_shape = [jax.ShapeDtypeStruct((N_DEV, m, tn), dt) for dt in out_dtypes]
    else:
        out_spec = tile_spec
        out_shape = [jax.ShapeDtypeStruct((m, n), dt) for dt in out_dtypes]
    outs = pl.pallas_call(
        body,
        name=name,
        grid=(m // tm, n // tn, nk),
        in_specs=[a_spec, b_spec] + [tile_spec] * n_te + [col_spec] * n_ce + [ANY] * len(after),
        out_specs=[out_spec] * n_out,
        out_shape=out_shape,
        scratch_shapes=[pltpu.VMEM((tm, tn), F32)] if nk > 1 else [],
        compiler_params=_params(("parallel", "parallel", "arbitrary")),
    )(a, b, *tile_extras, *col_extras, *after)
    return outs[0] if n_out == 1 else tuple(outs)


def _rowwise(fn, rows, bcast, out_rows, out_accs, *, name, tb_cap=256, after=()):
    t = rows[0].shape[0]
    tb = _row_tile(t, tb_cap)
    n_r, n_b, n_o, n_a = len(rows), len(bcast), len(out_rows), len(out_accs)

    def body(*refs):
        r_refs = refs[:n_r]
        b_refs = refs[n_r:n_r + n_b]
        first_out = n_r + n_b + len(after)
        o_refs = refs[first_out:first_out + n_o]
        a_refs = refs[first_out + n_o:]
        outs, accs = fn([r[...] for r in r_refs], [r[...] for r in b_refs])
        for o_ref, o in zip(o_refs, outs):
            o_ref[...] = o.astype(o_ref.dtype)
        if n_a:
            i = pl.program_id(0)

            @pl.when(i == 0)
            def _():
                for a_ref, acc in zip(a_refs, accs):
                    a_ref[...] = acc

            @pl.when(i > 0)
            def _():
                for a_ref, acc in zip(a_refs, accs):
                    a_ref[...] += acc

    in_specs = [pl.BlockSpec((tb, r.shape[1]), lambda i: (i, 0)) for r in rows]
    in_specs += [pl.BlockSpec(b.shape, lambda i: (0, 0)) for b in bcast]
    in_specs += [ANY] * len(after)
    out_specs = [pl.BlockSpec((tb, c), lambda i: (i, 0)) for c, _ in out_rows]
    out_specs += [pl.BlockSpec(s, lambda i: (0, 0)) for s in out_accs]
    out_shape = [jax.ShapeDtypeStruct((t, c), dt) for c, dt in out_rows]
    out_shape += [jax.ShapeDtypeStruct(s, F32) for s in out_accs]
    outs = pl.pallas_call(
        body,
        name=name,
        grid=(t // tb,),
        in_specs=in_specs,
        out_specs=out_specs,
        out_shape=out_shape,
        compiler_params=_params(("arbitrary",)),
    )(*rows, *bcast, *after)
    return tuple(outs)


def _colsum(v):
    return jnp.sum(v, axis=0, keepdims=True)


def _sigmoid(v):
    return 1.0 / (1.0 + jnp.exp(-v))


def _ln_fwd(x_res, y_sub, g, b, *, name, after=()):
    d = x_res.shape[1]

    def fn(rows, bc):
        z = ALPHA * rows[0] + rows[1]
        mu = jnp.mean(z, axis=-1, keepdims=True)
        zc = z - mu
        var = jnp.mean(zc * zc, axis=-1, keepdims=True)
        rstd = lax.rsqrt(var + LN_EPS)
        xhat = zc * rstd
        out = xhat * bc[0] + bc[1]
        return (out, out, xhat, rstd), ()

    return _rowwise(fn, [x_res, y_sub], [g, b], [(d, F32), (d, MXU_DTYPE), (d, F32), (1, F32)], [], name=name,
                    after=after, tb_cap=512)


def _ln_fwd_loss(x_res, y_sub, g, b, target, *, name, after=()):
    d = x_res.shape[1]

    def fn(rows, bc):
        z = ALPHA * rows[0] + rows[1]
        mu = jnp.mean(z, axis=-1, keepdims=True)
        zc = z - mu
        var = jnp.mean(zc * zc, axis=-1, keepdims=True)
        rstd = lax.rsqrt(var + LN_EPS)
        xhat = zc * rstd
        err = xhat * bc[0] + bc[1] - rows[2]
        return (xhat, rstd, err * (1.0 / d)), (_colsum(err * err),)

    return _rowwise(fn, [x_res, y_sub, target], [g, b], [(d, F32), (1, F32), (d, F32)], [(1, d)], name=name,
                    after=after, tb_cap=512)


def _ln_bwd(terms, xhat, rstd, g, *, name, after=()):
    d = xhat.shape[1]
    coefs = [c for c, _ in terms]

    def fn(rows, bc):
        xh, rs = rows[0], rows[1]
        dout = None
        for c, v in zip(coefs, rows[2:]):
            tv = v if c == 1.0 else c * v
            dout = tv if dout is None else dout + tv
        dxh = dout * bc[0]
        m1 = jnp.mean(dxh, axis=-1, keepdims=True)
        m2 = jnp.mean(dxh * xh, axis=-1, keepdims=True)
        dz = rs * (dxh - m1 - xh * m2)
        return (dz, dz), (_colsum(dout * xh), _colsum(dout), _colsum(dz))

    return _rowwise(fn, [xhat, rstd] + [v for _, v in terms], [g], [(d, F32), (d, MXU_DTYPE)],
                    [(1, d), (1, d), (1, d)], name=name, after=after, tb_cap=512)


def _rms_fwd(x, g, *, name):
    c = x.shape[1]

    def fn(rows, bc):
        v = rows[0]
        r = lax.rsqrt(jnp.mean(v * v, axis=-1, keepdims=True) + RMS_EPS)
        return (v * r * bc[0],), ()

    return _rowwise(fn, [x], [g], [(c, MXU_DTYPE)], [], name=name)[0]


def _rms_bwd(x, g, dy, *, name):
    c = x.shape[1]

    def fn(rows, bc):
        v, dyv = rows
        r = lax.rsqrt(jnp.mean(v * v, axis=-1, keepdims=True) + RMS_EPS)
        xh = v * r
        dxh = dyv * bc[0]
        dx = r * (dxh - xh * jnp.mean(dxh * xh, axis=-1, keepdims=True))
        return (dx,), (_colsum(dyv * xh),)

    return _rowwise(fn, [x, dy], [g], [(c, F32)], [(1, c)], name=name)


def _conv_chunk(t):
    return _row_tile(t, 256)


def _conv_fwd(x, w, bias, *, name):
    t, c = x.shape
    kw = w.shape[0]
    ch = _conv_chunk(t)

    def body(x_ref, w_ref, b_ref, o_ref, pad_ref):
        pad_ref[0:CONV_PAD, :] = jnp.zeros((CONV_PAD, LANES), F32)
        pad_ref[CONV_PAD:CONV_PAD + t, :] = x_ref[...]
        for t0 in range(0, t, ch):
            acc = jnp.broadcast_to(b_ref[...], (ch, LANES))
            for k in range(kw):
                s = kw - 1 - k
                acc = acc + w_ref[k:k + 1, :] * pad_ref[CONV_PAD - s + t0:CONV_PAD - s + t0 + ch, :]
            o_ref[t0:t0 + ch, :] = acc

    return pl.pallas_call(
        body,
        name=name,
        grid=(c // LANES,),
        in_specs=[pl.BlockSpec((t, LANES), lambda j: (0, j)), pl.BlockSpec((kw, LANES), lambda j: (0, j)),
                  pl.BlockSpec((1, LANES), lambda j: (0, j))],
        out_specs=pl.BlockSpec((t, LANES), lambda j: (0, j)),
        out_shape=jax.ShapeDtypeStruct((t, c), F32),
        scratch_shapes=[pltpu.VMEM((CONV_PAD + t, LANES), F32)],
        compiler_params=_params(("parallel",)),
    )(x, w, bias)


def _conv_bwd(dy, x, w, *, name):
    t, c = x.shape
    kw = w.shape[0]
    ch = _conv_chunk(t)

    def body(dy_ref, x_ref, w_ref, dx_ref, dw_ref, xpad_ref, dpad_ref):
        xpad_ref[0:CONV_PAD, :] = jnp.zeros((CONV_PAD, LANES), F32)
        xpad_ref[CONV_PAD:CONV_PAD + t, :] = x_ref[...]
        dpad_ref[0:t, :] = dy_ref[...]
        dpad_ref[t:t + CONV_PAD, :] = jnp.zeros((CONV_PAD, LANES), F32)
        for t0 in range(0, t, ch):
            acc = jnp.zeros((ch, LANES), F32)
            for k in range(kw):
                s = kw - 1 - k
                acc = acc + w_ref[k:k + 1, :] * dpad_ref[t0 + s:t0 + s + ch, :]
            dx_ref[t0:t0 + ch, :] = acc
        for k in range(kw):
            s = kw - 1 - k
            acc = jnp.zeros((ch, LANES), F32)
            for t0 in range(0, t, ch):
                acc = acc + dy_ref[t0:t0 + ch, :] * xpad_ref[CONV_PAD - s + t0:CONV_PAD - s + t0 + ch, :]
            dw_ref[k:k + 1, :] = _colsum(acc)

    blk = pl.BlockSpec((t, LANES), lambda j: (0, j))
    wblk = pl.BlockSpec((kw, LANES), lambda j: (0, j))
    return pl.pallas_call(
        body,
        name=name,
        grid=(c // LANES,),
        in_specs=[blk, blk, wblk],
        out_specs=[blk, wblk],
        out_shape=[jax.ShapeDtypeStruct((t, c), F32), jax.ShapeDtypeStruct((kw, c), F32)],
        scratch_shapes=[pltpu.VMEM((CONV_PAD + t, LANES), F32), pltpu.VMEM((CONV_PAD + t, LANES), F32)],
        compiler_params=_params(("parallel",)),
    )(dy, x, w)


ATTN_SCALE = (QK_NOPE + QK_ROPE) ** -0.5
_NT = (((1,), (1,)), ((), ()))
_TN = (((0,), (0,)), ((), ()))
_NN = (((1,), (0,)), ((), ()))


def _dot(a, b, dims):
    return lax.dot_general(a, b, dims, preferred_element_type=F32)


def _block_scores(q_cat, kn_ref, kp_ref, lo, hi, diagonal):
    k_cat = jnp.concatenate([kn_ref[lo:hi, :], kp_ref[lo:hi, :]], axis=1)
    s = _dot(q_cat, k_cat, _NT) * ATTN_SCALE
    if diagonal:
        shift = CHUNK.bit_length() - 1
        rows = lax.broadcasted_iota(jnp.int32, s.shape, 0)
        cols = lax.broadcasted_iota(jnp.int32, s.shape, 1)
        s = jnp.where(jnp.right_shift(cols, shift) <= jnp.right_shift(rows, shift), s, MASK_VALUE)
    return s, k_cat


def _attn_fwd(qn, qp, kv, kp, *, name):
    h, t, _ = qp.shape
    tq = _row_tile(t, 256)
    assert tq % CHUNK == 0

    def body(qn_ref, qp_ref, kn_ref, kp_ref, v_ref, o_ref, of_ref, lse_ref):
        for r0 in range(0, t, tq):
            r1 = r0 + tq
            q_cat = jnp.concatenate([qn_ref[r0:r1, :].astype(MXU_DTYPE), qp_ref[r0:r1, :]], axis=1)
            s_d, _ = _block_scores(q_cat, kn_ref, kp_ref, r0, r1, True)
            m = jnp.max(s_d, axis=-1, keepdims=True)
            if r0:
                s_b, _ = _block_scores(q_cat, kn_ref, kp_ref, 0, r0, False)
                m = jnp.maximum(m, jnp.max(s_b, axis=-1, keepdims=True))
            p_d = jnp.exp(s_d - m)
            l = jnp.sum(p_d, axis=-1, keepdims=True)
            acc = _dot(p_d.astype(MXU_DTYPE), v_ref[r0:r1, :], _NN)
            if r0:
                p_b = jnp.exp(s_b - m)
                l = l + jnp.sum(p_b, axis=-1, keepdims=True)
                acc = acc + _dot(p_b.astype(MXU_DTYPE), v_ref[0:r0, :], _NN)
            o = acc / l
            o_ref[r0:r1, :] = o.astype(o_ref.dtype)
            of_ref[r0:r1, :] = o
            lse_ref[r0:r1, :] = m + jnp.log(l)

    blk = lambda w: pl.BlockSpec((None, t, w), lambda hh: (hh, 0, 0))
    cols = pl.BlockSpec((t, QK_NOPE), lambda hh: (0, hh))
    keys = pl.BlockSpec((t, QK_NOPE), lambda hh: (0, 2 * hh))
    values = pl.BlockSpec((t, V_HEAD), lambda hh: (0, 2 * hh + 1))
    return pl.pallas_call(
        body,
        name=name,
        grid=(h,),
        in_specs=[cols, blk(QK_ROPE), keys, pl.BlockSpec((t, QK_ROPE), lambda hh: (0, 0)), values],
        out_specs=[cols, cols, blk(1)],
        out_shape=[jax.ShapeDtypeStruct((t, h * V_HEAD), MXU_DTYPE), jax.ShapeDtypeStruct((t, h * V_HEAD), F32),
                   jax.ShapeDtypeStruct((h, t, 1), F32)],
        compiler_params=_params(("parallel",)),
    )(qn, qp, kv, kp, kv)


def _attn_bwd(qn, qp, kv, kp, do, o, lse, *, name):
    h, t, _ = qp.shape
    tq = _row_tile(t, 256)

    def body(qn_ref, qp_ref, kn_ref, kp_ref, v_ref, do_ref, o_ref, lse_ref, dqn_ref, dqp_ref, dkv_ref, dkp_ref):
        dkv_ref[...] = jnp.zeros_like(dkv_ref)

        @pl.when(pl.program_id(0) == 0)
        def _():
            dkp_ref[...] = jnp.zeros_like(dkp_ref)

        for r0 in range(0, t, tq):
            r1 = r0 + tq
            q_cat = jnp.concatenate([qn_ref[r0:r1, :].astype(MXU_DTYPE), qp_ref[r0:r1, :]], axis=1)
            dov, lse = do_ref[r0:r1, :], lse_ref[r0:r1, :]
            delta = jnp.sum(dov.astype(F32) * o_ref[r0:r1, :], axis=-1, keepdims=True)

            def piece(lo, hi, diagonal):
                s, k_cat = _block_scores(q_cat, kn_ref, kp_ref, lo, hi, diagonal)
                p = jnp.exp(s - lse)
                dp = _dot(dov, v_ref[lo:hi, :], _NT)
                ds_m = (p * (dp - delta) * ATTN_SCALE).astype(MXU_DTYPE)
                dk_cat = _dot(ds_m, q_cat, _TN)
                dkv_ref[lo:hi, :QK_NOPE] += dk_cat[:, :QK_NOPE]
                dkp_ref[lo:hi, :] += dk_cat[:, QK_NOPE:]
                dkv_ref[lo:hi, QK_NOPE:] += _dot(p.astype(MXU_DTYPE), dov, _TN)
                return _dot(ds_m, k_cat, _NN)

            dq_cat = piece(r0, r1, True)
            if r0:
                dq_cat = dq_cat + piece(0, r0, False)
            dqn_ref[r0:r1, :] = dq_cat[:, :QK_NOPE]
            dqp_ref[r0:r1, :] = dq_cat[:, QK_NOPE:]

    blk = lambda w: pl.BlockSpec((None, t, w), lambda hh: (hh, 0, 0))
    kpblk = pl.BlockSpec((t, QK_ROPE), lambda hh: (0, 0))
    cols = pl.BlockSpec((t, QK_NOPE), lambda hh: (0, hh))
    keys = pl.BlockSpec((t, QK_NOPE), lambda hh: (0, 2 * hh))
    values = pl.BlockSpec((t, V_HEAD), lambda hh: (0, 2 * hh + 1))
    both = pl.BlockSpec((t, QK_NOPE + V_HEAD), lambda hh: (0, hh))
    return pl.pallas_call(
        body,
        name=name,
        grid=(h,),
        in_specs=[cols, blk(QK_ROPE), keys, kpblk, values, cols, cols, blk(1)],
        out_specs=[cols, blk(QK_ROPE), both, kpblk],
        out_shape=[jax.ShapeDtypeStruct((t, h * QK_NOPE), F32), jax.ShapeDtypeStruct((h, t, QK_ROPE), F32),
                   jax.ShapeDtypeStruct((t, h * (QK_NOPE + V_HEAD)), F32), jax.ShapeDtypeStruct((t, QK_ROPE), F32)],
        compiler_params=_params(("arbitrary",)),
    )(qn, qp, kv, kp, kv, do, o, lse)


def _rope(x1, x2, cos, sin, *, name, inverse=False):
    w = x1.shape[1]
    sign = -1.0 if inverse else 1.0

    def fn(rows, bc):
        a, b, c, s = rows
        s = sign * s
        return (a * c - b * s, a * s + b * c), ()

    return _rowwise(fn, [x1, x2, cos, sin], [], [(w, F32), (w, F32)], [], name=name)


def _mlp_up(x_m, w1, tag, after=()):
    def epi(acc):
        r = jnp.maximum(acc, 0.0)
        return (r * r,)

    return _mm(x_m, w1, name=f"{tag}_up", out_dtypes=(MXU_DTYPE,), epilogue=epi, after=after)


def _mlp_fwd(x_m, w1, w2, tag, after=(), act=None):
    if act is None:
        act, after = _mlp_up(x_m, w1, tag, after=after), ()
    y = _mm(act, w2, name=f"{tag}_down", after=after)
    return y, (x_m, act)


def _plus_residual(acc, res):
    return (acc + ALPHA * res,)


def _mlp_bwd(dy_m, saved, w1, w2, tag, residual, after=()):
    x_m, act = saved
    dw2 = _mm(act, dy_m, ta=True, name=f"{tag}_dw2", out_dtypes=(MXU_DTYPE,))

    def epi(acc, av):
        return (acc * (2.0 * jnp.sqrt(av.astype(F32))),)

    dh = _mm(dy_m, w2, tb=True, name=f"{tag}_dact", out_dtypes=(MXU_DTYPE,), epilogue=epi, tile_extras=(act,),
             after=after)
    dw1 = _mm(x_m, dh, ta=True, out_dev=True, name=f"{tag}_dw1", out_dtypes=(MXU_DTYPE,))
    dx = _mm(dh, w1, tb=True, name=f"{tag}_dx", epilogue=_plus_residual, tile_extras=(residual,))
    return [dx], {"ff_w1": dw1, "ff_w2": dw2}


def _sc_gate_specs(t, d):
    per = d // LANES
    return [pl.BlockSpec((t, LANES), lambda j, k=k: (0, k * per + j)) for k in range(3)]


def _sc_gate_fwd(u, conv_w, *, name):
    t, d = u.shape[0], u.shape[1] // 3
    kw = conv_w.shape[0]
    ch = _conv_chunk(t)

    def body(b_ref, c_ref, h_ref, w_ref, r_ref, pad_ref):
        pad_ref[0:CONV_PAD, :] = jnp.zeros((CONV_PAD, LANES), F32)
        pad_ref[CONV_PAD:CONV_PAD + t, :] = c_ref[...] * h_ref[...]
        for t0 in range(0, t, ch):
            q = jnp.zeros((ch, LANES), F32)
            for k in range(kw):
                s = kw - 1 - k
                q = q + w_ref[k:k + 1, :] * pad_ref[CONV_PAD - s + t0:CONV_PAD - s + t0 + ch, :]
            r_ref[t0:t0 + ch, :] = (b_ref[t0:t0 + ch, :] * q).astype(r_ref.dtype)

    blk = pl.BlockSpec((t, LANES), lambda j: (0, j))
    return pl.pallas_call(
        body,
        name=name,
        grid=(d // LANES,),
        in_specs=_sc_gate_specs(t, d) + [pl.BlockSpec((kw, LANES), lambda j: (0, j))],
        out_specs=blk,
        out_shape=jax.ShapeDtypeStruct((t, d), MXU_DTYPE),
        scratch_shapes=[pltpu.VMEM((CONV_PAD + t, LANES), F32)],
        compiler_params=_params(("parallel",)),
    )(u, u, u, conv_w)


def _sc_gate_bwd(dr, u, conv_w, *, name):
    t, d = u.shape[0], u.shape[1] // 3
    kw = conv_w.shape[0]
    ch = _conv_chunk(t)

    def body(dr_ref, b_ref, c_ref, h_ref, w_ref, db_ref, dp_ref, dw_ref, ppad_ref, dpad_ref):
        ppad_ref[0:CONV_PAD, :] = jnp.zeros((CONV_PAD, LANES), F32)
        ppad_ref[CONV_PAD:CONV_PAD + t, :] = c_ref[...] * h_ref[...]
        dpad_ref[0:t, :] = dr_ref[...] * b_ref[...]
        dpad_ref[t:t + CONV_PAD, :] = jnp.zeros((CONV_PAD, LANES), F32)
        for t0 in range(0, t, ch):
            q = jnp.zeros((ch, LANES), F32)
            dp = jnp.zeros((ch, LANES), F32)
            for k in range(kw):
                s = kw - 1 - k
                q = q + w_ref[k:k + 1, :] * ppad_ref[CONV_PAD - s + t0:CONV_PAD - s + t0 + ch, :]
                dp = dp + w_ref[k:k + 1, :] * dpad_ref[t0 + s:t0 + s + ch, :]
            db_ref[t0:t0 + ch, :] = (dr_ref[t0:t0 + ch, :] * q).astype(db_ref.dtype)
            dp_ref[t0:t0 + ch, :] = dp
        for k in range(kw):
            s = kw - 1 - k
            acc = jnp.zeros((ch, LANES), F32)
            for t0 in range(0, t, ch):
                acc = acc + dpad_ref[t0:t0 + ch, :] * ppad_ref[CONV_PAD - s + t0:CONV_PAD - s + t0 + ch, :]
            dw_ref[k:k + 1, :] = _colsum(acc)

    blk = pl.BlockSpec((t, LANES), lambda j: (0, j))
    wblk = pl.BlockSpec((kw, LANES), lambda j: (0, j))
    return pl.pallas_call(
        body,
        name=name,
        grid=(d // LANES,),
        in_specs=[blk] + _sc_gate_specs(t, d) + [wblk],
        out_specs=[blk, blk, wblk],
        out_shape=[jax.ShapeDtypeStruct((t, d), MXU_DTYPE), jax.ShapeDtypeStruct((t, d), F32),
                   jax.ShapeDtypeStruct((kw, d), F32)],
        scratch_shapes=[pltpu.VMEM((CONV_PAD + t, LANES), F32), pltpu.VMEM((CONV_PAD + t, LANES), F32)],
        compiler_params=_params(("parallel",)),
    )(dr, u, u, u, conv_w)


def _sc_fwd(x_m, w_in, conv_w, w_out, tag, after=()):
    u = _mm(x_m, w_in, name=f"{tag}_in", after=after)
    r = _sc_gate_fwd(u, conv_w, name=f"{tag}_gate")
    y = _mm(r, w_out, name=f"{tag}_out")
    return y, (x_m, u, r)


def _sc_bwd(dy_m, saved, w_in, conv_w, w_out, tag, residual, after=()):
    d = D_MODEL
    x_m, u, r = saved
    dw_out = _mm(r, dy_m, ta=True, name=f"{tag}_dwout", out_dtypes=(MXU_DTYPE,))
    dr = _mm(dy_m, w_out, tb=True, name=f"{tag}_dr", after=after)
    db, dp, dconv = _sc_gate_bwd(dr, u, conv_w, name=f"{tag}_dgate")

    def du_fn(rows, bc):
        dbv, dpv, uv = rows
        return (jnp.concatenate([dbv.astype(F32), dpv * uv[:, 2 * d:3 * d], dpv * uv[:, d:2 * d]], axis=1),), ()

    du = _rowwise(du_fn, [db, dp, u], [], [(3 * d, MXU_DTYPE)], [], name=f"{tag}_du")[0]
    dw_in = _mm(x_m, du, ta=True, out_dev=True, name=f"{tag}_dwin", out_dtypes=(MXU_DTYPE,))
    dx = _mm(du, w_in, tb=True, name=f"{tag}_dx", epilogue=_plus_residual, tile_extras=(residual,))
    return [dx], {"sc_w_in": dw_in, "sc_conv_w": dconv, "sc_w_out": dw_out}


def _cf_fwd(x_m, w, tag, after=()):
    d = D_MODEL

    def bias_epi(acc, bias):
        return (acc + bias,)

    u = _mm(x_m, w["cf_w_pw1"], name=f"{tag}_pw1", epilogue=bias_epi, col_extras=(w["cf_b_pw1"],),
            after=after)

    def glu_fn(rows, bc):
        uv = rows[0]
        return (uv[:, 0:d] * _sigmoid(uv[:, d:2 * d]),), ()

    h1 = _rowwise(glu_fn, [u], [], [(d, F32)], [], name=f"{tag}_glu")[0]
    h2 = _conv_fwd(h1, w["cf_dw_w"], w["cf_dw_b"], name=f"{tag}_conv")

    def norm_fn(rows, bc):
        hv = rows[0]
        mu = jnp.mean(hv, axis=-1, keepdims=True)
        hc = hv - mu
        rstd = lax.rsqrt(jnp.mean(hc * hc, axis=-1, keepdims=True) + LN_EPS)
        h3 = hc * rstd * bc[0] + bc[1]
        return (h3 * _sigmoid(h3),), ()

    h4 = _rowwise(norm_fn, [h2], [w["cf_norm_g"], w["cf_norm_b"]], [(d, MXU_DTYPE)], [], name=f"{tag}_norm")[0]
    y = _mm(h4, w["cf_w_pw2"], name=f"{tag}_pw2", epilogue=bias_epi, col_extras=(w["cf_b_pw2"],))
    return y, (x_m, u, h1, h2, h4)


def _cf_bwd(dy_m, dy_colsum, saved, w, tag, residual, after=()):
    d = D_MODEL
    x_m, u, h1, h2, h4 = saved
    dw_pw2 = _mm(h4, dy_m, ta=True, name=f"{tag}_dwpw2", out_dtypes=(MXU_DTYPE,))
    dh4 = _mm(dy_m, w["cf_w_pw2"], tb=True, name=f"{tag}_dh4", after=after)

    def dnorm_fn(rows, bc):
        dh4v, hv = rows
        g, b = bc
        mu = jnp.mean(hv, axis=-1, keepdims=True)
        hc = hv - mu
        rstd = lax.rsqrt(jnp.mean(hc * hc, axis=-1, keepdims=True) + LN_EPS)
        xh = hc * rstd
        h3 = xh * g + b
        sg = _sigmoid(h3)
        dh3 = dh4v * (sg * (1.0 + h3 * (1.0 - sg)))
        dxh = dh3 * g
        dh2 = rstd * (dxh - jnp.mean(dxh, axis=-1, keepdims=True) - xh * jnp.mean(dxh * xh, axis=-1, keepdims=True))
        return (dh2,), (_colsum(dh3 * xh), _colsum(dh3), _colsum(dh2))

    dh2, dnorm_g, dnorm_b, ddw_b = _rowwise(dnorm_fn, [dh4, h2], [w["cf_norm_g"], w["cf_norm_b"]], [(d, F32)],
                                            [(1, d), (1, d), (1, d)], name=f"{tag}_dnorm")
    dh1, ddw_w = _conv_bwd(dh2, h1, w["cf_dw_w"], name=f"{tag}_dconv")

    def dglu_fn(rows, bc):
        dh1v, uv = rows
        a, gate = uv[:, 0:d], uv[:, d:2 * d]
        sg = _sigmoid(gate)
        du = jnp.concatenate([dh1v * sg, dh1v * a * sg * (1.0 - sg)], axis=1)
        return (du,), (_colsum(du),)

    du, db_pw1 = _rowwise(dglu_fn, [dh1, u], [], [(2 * d, MXU_DTYPE)], [(1, 2 * d)], name=f"{tag}_dglu")
    dw_pw1 = _mm(x_m, du, ta=True, out_dev=True, name=f"{tag}_dwpw1", out_dtypes=(MXU_DTYPE,))
    dx = _mm(du, w["cf_w_pw1"], tb=True, name=f"{tag}_dx", epilogue=_plus_residual, tile_extras=(residual,))
    grads = {"cf_w_pw1": dw_pw1, "cf_b_pw1": db_pw1, "cf_dw_w": ddw_w, "cf_dw_b": ddw_b, "cf_norm_g": dnorm_g,
             "cf_norm_b": dnorm_b, "cf_w_pw2": dw_pw2, "cf_b_pw2": dy_colsum}
    return [dx], grads


def _rope_tables(t):
    pos = jnp.arange(t, dtype=F32)
    inv_freq = ROPE_THETA ** (-jnp.arange(0, QK_ROPE, 2, dtype=F32) / QK_ROPE)
    ang = pos[:, None] * inv_freq[None, :]
    return jnp.cos(ang), jnp.sin(ang)


ROPE_HALF = QK_ROPE // 2
Q_NOPE_COLS = MLA_HEADS * QK_NOPE
Q_HALF_COLS = MLA_HEADS * ROPE_HALF


def _interleave_heads(w_uk, w_uv):
    c = w_uk.shape[0]
    return jnp.concatenate([w_uk.reshape(c, MLA_HEADS, QK_NOPE), w_uv.reshape(c, MLA_HEADS, V_HEAD)],
                           axis=2).reshape(c, -1)


def _mla_fwd(x_m, w, tag, after=()):
    t = x_m.shape[0]
    cos, sin = _rope_tables(t)
    cos_h, sin_h = jnp.tile(cos, (1, MLA_HEADS)), jnp.tile(sin, (1, MLA_HEADS))
    cq_pre = _mm(x_m, w["mla_w_dq"], name=f"{tag}_dq", after=after)
    cq = _rms_fwd(cq_pre, w["mla_g_q"], name=f"{tag}_qnorm")
    q = _mm(cq, w["mla_w_uq"], name=f"{tag}_uq")
    r1, r2 = _rope(q[:, Q_NOPE_COLS:Q_NOPE_COLS + Q_HALF_COLS], q[:, Q_NOPE_COLS + Q_HALF_COLS:], cos_h, sin_h,
                   name=f"{tag}_qrope")
    qp_h = jnp.concatenate([r1.reshape(t, MLA_HEADS, ROPE_HALF), r2.reshape(t, MLA_HEADS, ROPE_HALF)],
                           axis=-1).transpose(1, 0, 2).astype(MXU_DTYPE)
    ckv_full = _mm(x_m, w["mla_w_dkv"], name=f"{tag}_dkv")
    ckv_pre = ckv_full[:, :KV_LORA]
    ckv = _rms_fwd(ckv_pre, w["mla_g_kv"], name=f"{tag}_kvnorm")
    k1, k2 = _rope(ckv_full[:, KV_LORA:KV_LORA + ROPE_HALF], ckv_full[:, KV_LORA + ROPE_HALF:], cos, sin,
                   name=f"{tag}_krope")
    kp = jnp.concatenate([k1, k2], axis=-1).astype(MXU_DTYPE)
    kv = _mm(ckv, _interleave_heads(w["mla_w_uk"], w["mla_w_uv"]), name=f"{tag}_ukv", out_dtypes=(MXU_DTYPE,))
    o, o_f, lse = _attn_fwd(q, qp_h, kv, kp, name=f"{tag}_attn")
    y = _mm(o, w["mla_w_o"], name=f"{tag}_o")
    return y, (x_m, cq_pre, cq, ckv_pre, ckv, q, qp_h, kv, kp, lse, o, o_f, (cos, sin, cos_h, sin_h))


def _mla_bwd(dy_m, saved, w, tag, residual, after=()):
    x_m, cq_pre, cq, ckv_pre, ckv, q, qp_h, kv, kp, lse, o, o_f, (cos, sin, cos_h, sin_h) = saved
    t = x_m.shape[0]
    half = ROPE_HALF
    dw_o = _mm(o, dy_m, ta=True, name=f"{tag}_dwo", out_dtypes=(MXU_DTYPE,))
    do = _mm(dy_m, w["mla_w_o"], tb=True, name=f"{tag}_do", out_dtypes=(MXU_DTYPE,), after=after)
    dqn, dqp_h, dkv, dkp = _attn_bwd(q, qp_h, kv, kp, do, o_f, lse, name=f"{tag}_dattn")
    dqp = dqp_h.transpose(1, 0, 2)
    d1, d2 = _rope(dqp[:, :, :half].reshape(t, -1), dqp[:, :, half:].reshape(t, -1), cos_h, sin_h,
                   name=f"{tag}_dqrope", inverse=True)
    dq = jnp.concatenate([dqn, d1, d2], axis=-1).astype(MXU_DTYPE)
    dw_uq = _mm(cq, dq, ta=True, name=f"{tag}_dwuq", out_dtypes=(MXU_DTYPE,))
    dcq = _mm(dq, w["mla_w_uq"], tb=True, name=f"{tag}_dcq")
    dcq_pre, dg_q = _rms_bwd(cq_pre, w["mla_g_q"], dcq, name=f"{tag}_dqnorm")
    dcq_pre_m = dcq_pre.astype(MXU_DTYPE)
    dw_dq = _mm(x_m, dcq_pre_m, ta=True, name=f"{tag}_dwdq", out_dtypes=(MXU_DTYPE,))
    dx_q = _mm(dcq_pre_m, w["mla_w_dq"], tb=True, name=f"{tag}_dxq", epilogue=_plus_residual, tile_extras=(residual,))
    dw_kv = _mm(ckv, dkv, ta=True, name=f"{tag}_dwukv", out_dtypes=(MXU_DTYPE,))
    dw_kv = dw_kv.reshape(KV_LORA, MLA_HEADS, QK_NOPE + V_HEAD)
    dw_uk, dw_uv = dw_kv[:, :, :QK_NOPE].reshape(KV_LORA, -1), dw_kv[:, :, QK_NOPE:].reshape(KV_LORA, -1)
    dckv = _mm(dkv, _interleave_heads(w["mla_w_uk"], w["mla_w_uv"]), tb=True, name=f"{tag}_dckv")
    dckv_pre, dg_kv = _rms_bwd(ckv_pre, w["mla_g_kv"], dckv, name=f"{tag}_dkvnorm")
    dk1, dk2 = _rope(dkp[:, :half], dkp[:, half:], cos, sin, name=f"{tag}_dkrope", inverse=True)
    dckv_full = jnp.concatenate([dckv_pre, dk1, dk2], axis=-1).astype(MXU_DTYPE)
    dw_dkv = _mm(x_m, dckv_full, ta=True, name=f"{tag}_dwdkv", out_dtypes=(MXU_DTYPE,))
    dx_kv = _mm(dckv_full, w["mla_w_dkv"], tb=True, name=f"{tag}_dxkv", epilogue=lambda acc, other: (acc + other,),
                tile_extras=(dx_q,))
    grads = {"mla_w_dq": dw_dq, "mla_g_q": dg_q, "mla_w_uq": dw_uq, "mla_w_dkv": dw_dkv, "mla_g_kv": dg_kv,
             "mla_w_uk": dw_uk, "mla_w_uv": dw_uv, "mla_w_o": dw_o}
    return [dx_kv], grads


def _cast_input(x):
    cast = lambda rows, bc: ((rows[0],), ())
    return _rowwise(cast, [x], [], [(D_MODEL, MXU_DTYPE)], [], name="x_cast")[0]


def _hooked(value, hook):
    token = None if hook is None else hook(value)
    return () if token is None else (token,)


def _mix_fwd(i, cur, cur_m, wl, hook=None, after=()):
    mixer = i % N_MIXERS
    if mixer == 0:
        y, s_mix = _sc_fwd(cur_m, wl["sc_w_in"], wl["sc_conv_w"], wl["sc_w_out"], f"l{i}_sc", after=after)
    elif mixer == 1:
        y, s_mix = _mla_fwd(cur_m, wl, f"l{i}_mla", after=after)
    else:
        y, s_mix = _cf_fwd(cur_m, wl, f"l{i}_cf", after=after)
    cur, cur_m, xh_mix, rs_mix = _ln_fwd(cur, y, wl["ln_mix_g"], wl["ln_mix_b"], name=f"l{i}_lnmix",
                                         after=_hooked(y, hook))
    return cur, cur_m, (s_mix, xh_mix, rs_mix)


def _ff_fwd(i, cur, cur_m, wl, hook=None, after=(), loss_target=None, act=None):
    y, s_ff = _mlp_fwd(cur_m, wl["ff_w1"], wl["ff_w2"], f"l{i}_ff", after=after, act=act)
    if loss_target is not None:
        xh_ff, rs_ff, dy, sq = _ln_fwd_loss(cur, y, wl["ln_ff_g"], wl["ln_ff_b"], loss_target, name=f"l{i}_lnff_loss",
                                            after=_hooked(y, hook))
        return (0.5 / D_MODEL) * jnp.sum(sq), dy, (s_ff, xh_ff, rs_ff)
    cur, cur_m, xh_ff, rs_ff = _ln_fwd(cur, y, wl["ln_ff_g"], wl["ln_ff_b"], name=f"l{i}_lnff", after=_hooked(y, hook))
    return cur, cur_m, (s_ff, xh_ff, rs_ff)


def _layer_fwd(i, cur, cur_m, wl):
    cur, cur_m, s_mix = _mix_fwd(i, cur, cur_m, wl)
    cur, cur_m, s_ff = _ff_fwd(i, cur, cur_m, wl)
    return cur, cur_m, (*s_mix, *s_ff)


def _loss_and_grad(out, target):
    d = D_MODEL

    def loss_fn(rows, bc):
        err = rows[0] - rows[1]
        return (err * (1.0 / d),), (_colsum(err * err),)

    dy, sq = _rowwise(loss_fn, [out, target], [], [(d, F32)], [(1, d)], name="loss")
    return (0.5 / d) * jnp.sum(sq), dy


def _ff_bwd(i, terms, saved, wl, hook=None, after=()):
    s_ff, xh_ff, rs_ff = saved
    grads = {}
    dz, dz_m, grads["ln_ff_g"], grads["ln_ff_b"], _ = _ln_bwd(terms, xh_ff, rs_ff, wl["ln_ff_g"], name=f"l{i}_dlnff",
                                                              after=after)
    dxs, g_ff = _mlp_bwd(dz_m, s_ff, wl["ff_w1"], wl["ff_w2"], f"l{i}_ff", dz, after=_hooked(dz, hook))
    grads.update(g_ff)
    return [(1.0, v) for v in dxs], grads


def _mix_bwd(i, terms, saved, wl, hook=None, after=()):
    mixer = i % N_MIXERS
    s_mix, xh_mix, rs_mix = saved
    grads = {}
    dz, dz_m, grads["ln_mix_g"], grads["ln_mix_b"], dz_sum = _ln_bwd(terms, xh_mix, rs_mix, wl["ln_mix_g"],
                                                                     name=f"l{i}_dlnmix", after=after)
    behind = _hooked(dz, hook)
    if mixer == 0:
        dxs, g_mix = _sc_bwd(dz_m, s_mix, wl["sc_w_in"], wl["sc_conv_w"], wl["sc_w_out"], f"l{i}_sc", dz, after=behind)
    elif mixer == 1:
        dxs, g_mix = _mla_bwd(dz_m, s_mix, wl, f"l{i}_mla", dz, after=behind)
    else:
        dxs, g_mix = _cf_bwd(dz_m, dz_sum, s_mix, wl, f"l{i}_cf", dz, after=behind)
    grads.update(g_mix)
    return [(1.0, v) for v in dxs], grads


def _layer_bwd(i, terms, saved, wl):
    terms, g_ff = _ff_bwd(i, terms, saved[3:], wl)
    terms, g_mix = _mix_bwd(i, terms, saved[:3], wl)
    return terms, {**g_ff, **g_mix}


def _sum_terms(terms, *, name):
    coefs = [c for c, _ in terms]

    def sum_fn(rows, bc):
        acc = None
        for c, v in zip(coefs, rows):
            tv = v if c == 1.0 else c * v
            acc = tv if acc is None else acc + tv
        return (acc,), ()

    return _rowwise(sum_fn, [v for _, v in terms], [], [(D_MODEL, F32)], [], name=name)[0]


def _local_step(x, target, layers):
    cur, cur_m = x, _cast_input(x)
    saved = []
    for i in range(DEPTH):
        cur, cur_m, s = _layer_fwd(i, cur, cur_m, layers[i])
        saved.append(s)
    loss, dy = _loss_and_grad(cur, target)
    terms = [(1.0, dy)]
    grads = [None] * DEPTH
    for i in reversed(range(DEPTH)):
        terms, grads[i] = _layer_bwd(i, terms, saved[i], layers[i])
    return loss, _sum_terms(terms, name="grad_x"), grads


MESH_ID = pl.DeviceIdType.MESH
ANY = pl.BlockSpec(memory_space=pl.ANY)


def _block_id(px, py, pc):
    return 4 * px + 2 * py + pc


COPIES_PER_GATHER = 7


def _all_gather(shards, *, name):
    n = len(shards)

    def body(*refs):
        x_refs, out_refs = refs[:n], refs[n:2 * n]
        send_sems, recv_sems, local_sems = refs[2 * n:]
        x, y, core = lax.axis_index("x"), lax.axis_index("y"), lax.axis_index("c")
        me, sibling = (x, y, core), (x, y, 1 - core)
        chips = [(1 - x, y), (x, 1 - y), (1 - x, 1 - y)]

        def copy(t, k, block, to, from_input=False):
            dst = out_refs[t].at[_block_id(*block)]
            return pltpu.make_async_remote_copy(
                src_ref=x_refs[t] if from_input else dst, dst_ref=dst,
                send_sem=send_sems.at[t * COPIES_PER_GATHER + k], recv_sem=recv_sems.at[t * COPIES_PER_GATHER + k],
                device_id=to, device_id_type=MESH_ID)

        started = []
        mine = [pltpu.make_async_copy(x_refs[t], out_refs[t].at[_block_id(*me)], local_sems.at[t]) for t in range(n)]
        for t in range(n):
            mine[t].start()
            first = [copy(t, 0, me, sibling, from_input=True)]
            first += [copy(t, 1 + j, me, (*chip, core), from_input=True) for j, chip in enumerate(chips)]
            for cp in first:
                cp.start()
            started += first
        for t in range(n):
            for j, chip in enumerate(chips):
                copy(t, 1 + j, (*chip, core), me).wait_recv()
                passed = copy(t, 4 + j, (*chip, core), sibling)
                passed.start()
                started.append(passed)
        for t in range(n):
            copy(t, 0, sibling, me).wait_recv()
            for j, chip in enumerate(chips):
                copy(t, 4 + j, (*chip, 1 - core), me).wait_recv()
        for cp in started:
            cp.wait_send()
        for cp in mine:
            cp.wait()

    n_sems = n * COPIES_PER_GATHER
    outs = pl.pallas_call(
        body,
        name=name,
        out_shape=[jax.ShapeDtypeStruct((N_DEV, *s.shape), s.dtype) for s in shards],
        in_specs=[ANY] * n,
        out_specs=[ANY] * n,
        scratch_shapes=[pltpu.SemaphoreType.DMA((n_sems,)), pltpu.SemaphoreType.DMA((n_sems,)),
                        pltpu.SemaphoreType.DMA((n,))],
    )(*shards)
    return list(outs)


def _swap_with_sibling(parts, *, name):
    n = len(parts)

    def body(*refs):
        p_refs, got_refs = refs[:n], refs[n:2 * n]
        send_sems, recv_sems = refs[2 * n:]
        x, y, core = lax.axis_index("x"), lax.axis_index("y"), lax.axis_index("c")
        chips = [(x, y), (1 - x, y), (x, 1 - y), (1 - x, 1 - y)]
        copies = []
        for t in range(n):
            for k, (px, py) in enumerate(chips):
                copies.append(pltpu.make_async_remote_copy(
                    src_ref=p_refs[t].at[_block_id(px, py, 1 - core)], dst_ref=got_refs[t].at[k],
                    send_sem=send_sems.at[4 * t + k], recv_sem=recv_sems.at[4 * t + k], device_id=(x, y, 1 - core),
                    device_id_type=MESH_ID))
        for cp in copies:
            cp.start()
        for cp in copies:
            cp.wait()

    outs = pl.pallas_call(
        body,
        name=name,
        out_shape=[jax.ShapeDtypeStruct((4, *p.shape[1:]), p.dtype) for p in parts],
        in_specs=[ANY] * n,
        out_specs=[ANY] * n,
        scratch_shapes=[pltpu.SemaphoreType.DMA((4 * n,)), pltpu.SemaphoreType.DMA((4 * n,))],
    )(*parts)
    return list(outs)


def _swap_between_chips(parts, *, name):
    n = len(parts)

    def body(*refs):
        p_refs, got_refs = refs[:n], refs[n:2 * n]
        send_sems, recv_sems = refs[2 * n:]
        x, y, core = lax.axis_index("x"), lax.axis_index("y"), lax.axis_index("c")
        chips = [(1 - x, y), (x, 1 - y), (1 - x, 1 - y)]
        copies = []
        for t in range(n):
            for j, (px, py) in enumerate(chips):
                copies.append(pltpu.make_async_remote_copy(
                    src_ref=p_refs[t].at[j], dst_ref=got_refs[t].at[j], send_sem=send_sems.at[3 * t + j],
                    recv_sem=recv_sems.at[3 * t + j], device_id=(px, py, core), device_id_type=MESH_ID))
        for cp in copies:
            cp.start()
        for cp in copies:
            cp.wait()

    outs = pl.pallas_call(
        body,
        name=name,
        out_shape=[jax.ShapeDtypeStruct(p.shape, p.dtype) for p in parts],
        in_specs=[ANY] * n,
        out_specs=[ANY] * n,
        scratch_shapes=[pltpu.SemaphoreType.DMA((3 * n,)), pltpu.SemaphoreType.DMA((3 * n,))],
    )(*parts)
    return list(outs)


def _chip_partials(parts, got, *, name):
    _, r, c = parts.shape
    tb = _row_tile(r, 512)
    x, y, core = lax.axis_index("x"), lax.axis_index("y"), lax.axis_index("c")
    chips = [(x, y), (1 - x, y), (x, 1 - y), (1 - x, 1 - y)]
    ids = jnp.stack([_block_id(px, py, core) for px, py in chips]).astype(jnp.int32)

    def body(ids_ref, p0, p1, p2, p3, g_ref, own_ref, out_ref):
        own_ref[...] = p0[...].astype(F32) + g_ref[0].astype(F32)
        for j, p in enumerate((p1, p2, p3)):
            out_ref[j] = (p[...].astype(F32) + g_ref[j + 1].astype(F32)).astype(out_ref.dtype)

    def part_spec(k):
        return pl.BlockSpec((None, tb, c), lambda i, ids_ref: (ids_ref[k], i, 0))

    grid_spec = pltpu.PrefetchScalarGridSpec(
        num_scalar_prefetch=1,
        grid=(r // tb,),
        in_specs=[part_spec(k) for k in range(4)] + [pl.BlockSpec((4, tb, c), lambda i, ids_ref: (0, i, 0))],
        out_specs=[pl.BlockSpec((tb, c), lambda i, ids_ref: (i, 0)), pl.BlockSpec((3, tb, c), lambda i, ids_ref: (0, i, 0))],
    )
    return pl.pallas_call(
        body,
        name=name,
        grid_spec=grid_spec,
        out_shape=[jax.ShapeDtypeStruct((r, c), F32), jax.ShapeDtypeStruct((3, r, c), parts.dtype)],
        compiler_params=_params(("parallel",)),
    )(ids, parts, parts, parts, parts, got)


def _chip_partials_group(parts, got, *, name):
    n = len(parts)
    rows = [p.shape[1] for p in parts]
    steps = 4 if all(r % (4 * BF16_ROWS) == 0 for r in rows) else (2 if all(r % (2 * BF16_ROWS) == 0 for r in rows) else 1)
    x, y, core = lax.axis_index("x"), lax.axis_index("y"), lax.axis_index("c")
    chips = [(x, y), (1 - x, y), (x, 1 - y), (1 - x, 1 - y)]
    ids = jnp.stack([_block_id(px, py, core) for px, py in chips]).astype(jnp.int32)

    def body(ids_ref, *refs):
        ins, outs = refs[:5 * n], refs[5 * n:]
        for t in range(n):
            p0, p1, p2, p3, g_ref = ins[5 * t:5 * t + 5]
            own_ref, out_ref = outs[2 * t], outs[2 * t + 1]
            own_ref[...] = p0[...].astype(F32) + g_ref[0].astype(F32)
            for j, p in enumerate((p1, p2, p3)):
                out_ref[j] = (p[...].astype(F32) + g_ref[j + 1].astype(F32)).astype(out_ref.dtype)

    in_specs, out_specs, out_shape, operands = [], [], [], []
    for p, g in zip(parts, got):
        _, r, c = p.shape
        tb = r // steps
        in_specs += [pl.BlockSpec((None, tb, c), lambda i, ids_ref, k=k: (ids_ref[k], i, 0)) for k in range(4)]
        in_specs.append(pl.BlockSpec((4, tb, c), lambda i, ids_ref: (0, i, 0)))
        out_specs += [pl.BlockSpec((tb, c), lambda i, ids_ref: (i, 0)), pl.BlockSpec((3, tb, c), lambda i, ids_ref: (0, i, 0))]
        out_shape += [jax.ShapeDtypeStruct((r, c), F32), jax.ShapeDtypeStruct((3, r, c), p.dtype)]
        operands += [p, p, p, p, g]
    outs = pl.pallas_call(
        body,
        name=name,
        grid_spec=pltpu.PrefetchScalarGridSpec(num_scalar_prefetch=1, grid=(steps,), in_specs=in_specs,
                                               out_specs=out_specs),
        out_shape=out_shape,
        compiler_params=_params(("parallel",)),
    )(ids, *operands)
    return [(outs[2 * t], outs[2 * t + 1]) for t in range(n)]


def _sum_blocks(first, blocks, *, name):
    n, r, c = blocks.shape
    tb = _row_tile(r, 512)

    def body(*refs):
        b_ref, o_ref = refs[-2], refs[-1]
        acc = refs[0][...] if first is not None else b_ref[0].astype(F32)
        for j in range(0 if first is not None else 1, n):
            acc = acc + b_ref[j].astype(F32)
        o_ref[...] = acc

    row_spec = pl.BlockSpec((tb, c), lambda i: (i, 0))
    return pl.pallas_call(
        body,
        name=name,
        grid=(r // tb,),
        in_specs=([row_spec] if first is not None else []) + [pl.BlockSpec((n, tb, c), lambda i: (0, i, 0))],
        out_specs=row_spec,
        out_shape=jax.ShapeDtypeStruct((r, c), F32),
        compiler_params=_params(("parallel",)),
    )(*([first] if first is not None else []), blocks)


def _reduce_scatter(parts, *, tag):
    got = _swap_with_sibling(parts, name=f"{tag}_sibling")
    sums = [_chip_partials(p, g, name=f"{tag}_chipsum{t}") for t, (p, g) in enumerate(zip(parts, got))]
    arrived = _swap_between_chips([others for _, others in sums], name=f"{tag}_chips")
    return [(own, arr) for (own, _), arr in zip(sums, arrived)]


HBM = pl.BlockSpec(memory_space=pltpu.HBM)
SEM = pl.BlockSpec(memory_space=pltpu.SEMAPHORE)
DATAFLOW = pltpu.SideEffectType.DATAFLOW_SIDE_EFFECTING


def _tie(value, *before):
    return lax.optimization_barrier((value, *before))[0]


def _split_call(name, arrays, sems, n_new, body, after=None):
    n_a, n_s = len(arrays), len(sems)
    after = [] if after is None else (list(after) if isinstance(after, (list, tuple)) else [after])
    n_after = len(after)

    def kernel_body(*refs):
        new = refs[n_a + n_s + n_after:n_a + n_s + n_after + 2] if n_new else (None, None)
        body(refs[:n_a], refs[n_a:n_a + n_s], *new)
        refs[-1][...] = jnp.zeros_like(refs[-1])

    n_sem_out = 2 if n_new else 0
    out_shape = [pltpu.SemaphoreType.DMA((n_new,))] * n_sem_out
    out_shape += [pltpu.HBM(a.shape, a.dtype) for a in arrays]
    out_shape += [jax.ShapeDtypeStruct((SUBLANES, LANES), F32)]
    outs = pl.pallas_call(
        kernel_body,
        name=name,
        out_shape=out_shape,
        in_specs=[HBM] * n_a + [SEM] * n_s + [ANY] * n_after,
        out_specs=[SEM] * n_sem_out + [HBM] * n_a + [pl.BlockSpec(memory_space=pltpu.VMEM)],
        input_output_aliases={i: n_sem_out + i for i in range(n_a)},
        compiler_params=pltpu.CompilerParams(has_side_effects=DATAFLOW),
    )(*[pltpu.with_memory_space_constraint(a, pltpu.HBM) for a in arrays], *sems, *after)
    return tuple(outs[:n_sem_out]), list(outs[n_sem_out:n_sem_out + n_a]), outs[-1]


def _remote(src, dst, send, recv, k, to):
    return pltpu.make_async_remote_copy(src_ref=src, dst_ref=dst, send_sem=send.at[k], recv_sem=recv.at[k],
                                        device_id=to, device_id_type=MESH_ID)


def _land_block(land_ref, block, by_cols, width):
    if by_cols:
        return land_ref.at[:, pl.ds(pl.multiple_of(block * width, LANES), width)]
    return land_ref.at[block]


def _place_in_window(shard, *, name):
    r, c = shard.shape
    tb = _row_tile(r, 512)
    me = _block_id(lax.axis_index("x"), lax.axis_index("y"), lax.axis_index("c")).astype(jnp.int32).reshape(1)

    def body(me_ref, s_ref, o_ref, token_ref):
        o_ref[...] = s_ref[...].astype(o_ref.dtype)
        token_ref[...] = jnp.zeros_like(token_ref)

    return pl.pallas_call(
        body,
        name=name,
        grid_spec=pltpu.PrefetchScalarGridSpec(
            num_scalar_prefetch=1, grid=(r // tb,),
            in_specs=[pl.BlockSpec((tb, c), lambda i, me_ref: (i, 0))],
            out_specs=[pl.BlockSpec((tb, c), lambda i, me_ref: (i, me_ref[0])),
                       pl.BlockSpec((SUBLANES, LANES), lambda i, me_ref: (0, 0))]),
        out_shape=[jax.ShapeDtypeStruct((r, N_DEV * c), MXU_DTYPE), jax.ShapeDtypeStruct((SUBLANES, LANES), F32)],
        compiler_params=_params(("arbitrary",)),
    )(me, shard)


def _gather_prepare(shards, by_cols, *, name):
    me = _block_id(lax.axis_index("x"), lax.axis_index("y"), lax.axis_index("c"))
    lands, early = [], []
    for i, (s, cols) in enumerate(zip(shards, by_cols)):
        if cols:
            land, token = _place_in_window(s, name=f"{name}_place{i}")
            lands.append(land)
            early.append(token)
        else:
            lands.append(lax.dynamic_update_index_in_dim(lax.empty((N_DEV, *s.shape), s.dtype), s, me, 0))
            early.append(s)
    return lands, [s.shape[-1] for s in shards], list(by_cols), early


def _gather_start(shards, *, name, after=None, by_cols=None, prepared=None):
    if prepared is None:
        prepared = _gather_prepare(shards, by_cols if by_cols is not None else [False] * len(shards), name=name)
    lands, widths, by_cols, _ = prepared
    n = len(lands)

    def body(refs, _, send, recv):
        x, y, core = lax.axis_index("x"), lax.axis_index("y"), lax.axis_index("c")
        targets = [(x, y, 1 - core), (1 - x, y, core), (x, 1 - y, core), (1 - x, 1 - y, core)]
        for t in range(n):
            mine = _land_block(refs[t], _block_id(x, y, core), by_cols[t], widths[t])
            for k, to in enumerate(targets):
                _remote(mine, mine, send, recv, 4 * t + k, to).start()

    sems, thru, token = _split_call(name, lands, [], 4 * n, body, after=after)
    return {"n": n, "sems1": sems, "arrays": thru, "token": token, "by_cols": by_cols, "widths": widths}


def _gather_pass_on(st, *, name, after):
    n = st["n"]

    def body(refs, sems, send, recv):
        send1, recv1 = sems
        x, y, core = lax.axis_index("x"), lax.axis_index("y"), lax.axis_index("c")
        chips = [(1 - x, y), (x, 1 - y), (1 - x, 1 - y)]
        for t in range(n):
            for j, (px, py) in enumerate(chips):
                block = _land_block(refs[t], _block_id(px, py, core), st["by_cols"][t], st["widths"][t])
                _remote(block, block, send1, recv1, 4 * t + 1 + j, (x, y, 1 - core)).wait_recv()
                _remote(block, block, send, recv, 3 * t + j, (x, y, 1 - core)).start()

    sems2, thru, token = _split_call(name, st["arrays"], list(st["sems1"]), 3 * n, body, after=after)
    return {**st, "sems2": sems2, "arrays": thru, "token": token}


def _gather_finish(st, *, name, after, then_start=None):
    n = st["n"]
    n2 = 0 if then_start is None else len(then_start[0])

    def body(refs, sems, new_send=None, new_recv=None):
        send1, recv1, send2, recv2 = sems
        x, y, core = lax.axis_index("x"), lax.axis_index("y"), lax.axis_index("c")
        sibling = (x, y, 1 - core)
        chips = [(1 - x, y), (x, 1 - y), (1 - x, 1 - y)]
        for t in range(n):
            place = lambda px, py, pc, t=t: _land_block(refs[t], _block_id(px, py, pc), st["by_cols"][t],
                                                        st["widths"][t])
            mine = place(x, y, core)
            for k in range(4):
                _remote(mine, mine, send1, recv1, 4 * t + k, sibling).wait_send()
            _remote(mine, place(x, y, 1 - core), send1, recv1, 4 * t, sibling).wait_recv()
            for j, (px, py) in enumerate(chips):
                _remote(place(px, py, core), place(px, py, core), send2, recv2, 3 * t + j, sibling).wait_send()
                _remote(mine, place(px, py, 1 - core), send2, recv2, 3 * t + j, sibling).wait_recv()
        if n2:
            _, widths, by_cols, _ = then_start
            targets = [sibling, (1 - x, y, core), (x, 1 - y, core), (1 - x, 1 - y, core)]
            for t in range(n2):
                mine = _land_block(refs[n + t], _block_id(x, y, core), by_cols[t], widths[t])
                for k, to in enumerate(targets):
                    _remote(mine, mine, new_send, new_recv, 4 * t + k, to).start()

    arrays = st["arrays"] + (list(then_start[0]) if n2 else [])
    sems, thru, token = _split_call(name, arrays, [*st["sems1"], *st["sems2"]], 4 * n2, body, after=after)
    if not n2:
        return thru
    _, widths, by_cols, _ = then_start
    return thru[:n], {"n": n2, "sems1": sems, "arrays": thru[n:], "token": token, "by_cols": by_cols, "widths": widths}


def _swap_start(sources, lands, plan, per_array, *, name):
    n = len(sources)

    def body(refs, _, send, recv):
        x, y, core = lax.axis_index("x"), lax.axis_index("y"), lax.axis_index("c")
        for t in range(n):
            for k in range(per_array):
                block, to = plan(x, y, core, k)
                _remote(refs[t].at[block], refs[n + t].at[k], send, recv, per_array * t + k, to).start()

    sems, thru, token = _split_call(name, sources + lands, [], per_array * n, body)
    return {"n": n, "sems": sems, "arrays": thru, "token": token, "plan": plan, "per_array": per_array}


def _swap_finish(states, *, name, after):
    many = isinstance(states, (list, tuple))
    states = list(states) if many else [states]

    def body(refs, sems, *_):
        x, y, core = lax.axis_index("x"), lax.axis_index("y"), lax.axis_index("c")
        first = 0
        for j, st in enumerate(states):
            n, plan, per_array = st["n"], st["plan"], st["per_array"]
            send, recv = sems[2 * j], sems[2 * j + 1]
            for t in range(n):
                for k in range(per_array):
                    block, to = plan(x, y, core, k)
                    cp = _remote(refs[first + t].at[block], refs[first + n + t].at[k], send, recv, per_array * t + k, to)
                    cp.wait_send()
                    cp.wait_recv()
            first += 2 * n

    arrays = [a for st in states for a in st["arrays"]]
    _, thru, _ = _split_call(name, arrays, [s for st in states for s in st["sems"]], 0, body, after=after)
    out, first = [], 0
    for st in states:
        n = st["n"]
        out.append((thru[first:first + n], thru[first + n:first + 2 * n]))
        first += 2 * n
    return out if many else out[0]


def _to_sibling(x, y, core, k):
    px, py = [(x, y), (1 - x, y), (x, 1 - y), (1 - x, 1 - y)][k]
    return _block_id(px, py, 1 - core), (x, y, 1 - core)


def _to_chip(x, y, core, k):
    px, py = [(1 - x, y), (x, 1 - y), (1 - x, 1 - y)][k]
    return k, (px, py, core)


PARAMS = {
    "sc_w_in": ((2, 1024, 3072), 2), "sc_conv_w": ((2, 3, 1024), 2), "sc_w_out": ((2, 1024, 1024), 1),
    "mla_w_dq": ((1, 1024, 384), 1), "mla_g_q": ((1, 384), None), "mla_w_uq": ((1, 384, 1536), 2),
    "mla_w_dkv": ((1, 1024, 320), 1), "mla_g_kv": ((1, 256), None), "mla_w_uk": ((1, 256, 8, 128), 1),
    "mla_w_uv": ((1, 256, 8, 128), 1), "mla_w_o": ((1, 1024, 1024), 1), "cf_w_pw1": ((1, 1024, 2048), 2),
    "cf_b_pw1": ((1, 2048), 1), "cf_dw_w": ((1, 31, 1024), 2), "cf_dw_b": ((1, 1024), 1), "cf_norm_g": ((1, 1024), 1),
    "cf_norm_b": ((1, 1024), 1), "cf_w_pw2": ((1, 1024, 1024), 1), "cf_b_pw2": ((1, 1024), 1),
    "ff_w1": ((4, 1024, 4096), 2), "ff_w2": ((4, 4096, 1024), 1), "ln_mix_g": ((4, 1024), None),
    "ln_mix_b": ((4, 1024), None), "ln_ff_g": ((4, 1024), None), "ln_ff_b": ((4, 1024), None),
}
NAMES = list(PARAMS)
BIG = ["sc_w_in", "sc_w_out", "mla_w_dq", "mla_w_uq", "mla_w_dkv", "mla_w_uk", "mla_w_uv", "mla_w_o", "cf_w_pw1",
       "cf_w_pw2", "ff_w1", "ff_w2"]
SMALL = [n for n in NAMES if n not in BIG]
SMALL_SHARDED = [n for n in SMALL if PARAMS[n][1] is not None]
SMALL_WIDTH = LANES
DEV_BLOCKED = ("sc_w_in", "cf_w_pw1", "ff_w1")
MIXER_BIG = {0: ["sc_w_in", "sc_w_out"], 1: ["mla_w_dq", "mla_w_uq", "mla_w_dkv", "mla_w_uk", "mla_w_uv", "mla_w_o"],
             2: ["cf_w_pw1", "cf_w_pw2"]}


def _layer_big(i):
    return [(n, i // N_MIXERS) for n in MIXER_BIG[i % N_MIXERS]] + [("ff_w1", i), ("ff_w2", i)]


def _shard_2d(name, a):
    return a.reshape(a.shape[0], -1)


def _shard_for_gather(name, a):
    flat = _shard_2d(name, a)
    return flat if name in DEV_BLOCKED else flat.astype(MXU_DTYPE)


def _as_operand(name, g):
    if name in DEV_BLOCKED:
        return g
    if name == "mla_w_uq":
        by_head = g.transpose(1, 0, 2)
        pieces = [by_head[:, :, :QK_NOPE], by_head[:, :, QK_NOPE:QK_NOPE + ROPE_HALF], by_head[:, :, QK_NOPE + ROPE_HALF:]]
        return jnp.concatenate([p.reshape(g.shape[1], -1) for p in pieces], axis=1)
    return g.reshape(-1, g.shape[2])


def _as_parts(name, grad):
    if name in DEV_BLOCKED:
        return grad
    if name == "mla_w_uq":
        rows = grad.shape[0]
        pieces = [grad[:, :Q_NOPE_COLS].reshape(rows, MLA_HEADS, QK_NOPE),
                  grad[:, Q_NOPE_COLS:Q_NOPE_COLS + Q_HALF_COLS].reshape(rows, MLA_HEADS, ROPE_HALF),
                  grad[:, Q_NOPE_COLS + Q_HALF_COLS:].reshape(rows, MLA_HEADS, ROPE_HALF)]
        return jnp.concatenate(pieces, axis=2).transpose(1, 0, 2)
    return grad.reshape(N_DEV, -1, grad.shape[1])


def _shard_shape(name):
    shape, ax = PARAMS[name]
    if ax is None:
        return shape
    return tuple(s // N_DEV if i == ax else s for i, s in enumerate(shape))


def _rows_of(shape, width, align):
    n = 1
    for s in shape:
        n *= s
    rows = -(-n // width)
    return -(-rows // align) * align


def _pack_rows(arrays, width, align, dtype, lead=0):
    segs = []
    for a in arrays:
        batch = a.shape[:lead]
        flat = a.astype(dtype).reshape(*batch, -1)
        rows = _rows_of(a.shape[lead:], width, align)
        flat = jnp.pad(flat, [(0, 0)] * lead + [(0, rows * width - flat.shape[-1])])
        segs.append(flat.reshape(*batch, rows, width))
    return jnp.concatenate(segs, axis=lead)


def _unpack_rows(buf, shapes, width, align):
    out, off = [], 0
    lead = buf.shape[:-2]
    for shape in shapes:
        n = 1
        for s in shape:
            n *= s
        rows = _rows_of(shape, width, align)
        seg = buf[..., off:off + rows, :].reshape(*lead, rows * width)[..., :n]
        out.append(seg.reshape(*lead, *shape))
        off += rows
    return out


def _merge_shards(stacked, ax):
    moved = jnp.moveaxis(stacked, 0, ax)
    shape = moved.shape
    return moved.reshape(*shape[:ax], shape[ax] * shape[ax + 1], *shape[ax + 2:])


def _split_shards(full, ax):
    shape = full.shape
    split = full.reshape(*shape[:ax], N_DEV, shape[ax] // N_DEV, *shape[ax + 1:])
    return jnp.moveaxis(split, ax, 0)


def _adam_math(wv, gv, mv, vv):
    m_new = ADAM_B1 * mv + (1.0 - ADAM_B1) * gv
    v_new = ADAM_B2 * vv + (1.0 - ADAM_B2) * (gv * gv)
    m_hat = m_new / (1.0 - ADAM_B1 ** ADAM_STEP)
    v_hat = v_new / (1.0 - ADAM_B2 ** ADAM_STEP)
    delta = -ADAM_LR * (m_hat / (jnp.sqrt(v_hat) + ADAM_EPS) + ADAM_WD * wv)
    return delta, m_new, v_new


def _adamw(w, g, m, v, *, name):
    shape = w.shape
    c = shape[-1]

    def fn(rows, bc):
        return _adam_math(*rows), ()

    flat = [a.reshape(-1, c) for a in (w, g, m, v)]
    outs = _rowwise(fn, flat, [], [(c, F32)] * 3, [], name=name, tb_cap=512)
    return tuple(o.reshape(shape) for o in outs)


def _adamw_reduced(w, m, v, layer_grads, *, name, after=()):
    n_l, r, c = w.shape
    tb = _row_tile(r, 256 if c > 512 else 512)
    nb = r // tb

    def body(*refs):
        w_ref, m_ref, v_ref = refs[:3]
        g_refs = refs[3:3 + 2 * n_l]
        g_out, d_out, m_out, v_out = refs[3 + 2 * n_l + len(after):]
        layer = pl.program_id(0)
        for l in range(n_l):
            @pl.when(layer == l)
            def _(l=l):
                own, arr = g_refs[2 * l], g_refs[2 * l + 1]
                g = own[...] + arr[0].astype(F32)
                g = g + arr[1].astype(F32)
                g = g + arr[2].astype(F32)
                delta, m_new, v_new = _adam_math(w_ref[...], g, m_ref[...], v_ref[...])
                g_out[...] = g
                d_out[...] = delta
                m_out[...] = m_new
                v_out[...] = v_new

    stacked = pl.BlockSpec((tb, c), lambda l, i: (l * nb + i, 0))
    in_specs = [stacked] * 3
    operands = [a.reshape(n_l * r, c) for a in (w, m, v)]
    for l, (own, arr) in enumerate(layer_grads):
        in_specs.append(pl.BlockSpec((tb, c), lambda ll, i, l=l: (jnp.where(ll == l, i, 0), 0)))
        in_specs.append(pl.BlockSpec((3, tb, c), lambda ll, i, l=l: (0, jnp.where(ll == l, i, 0), 0)))
        operands += [own, arr]
    in_specs += [ANY] * len(after)
    operands += list(after)
    outs = pl.pallas_call(
        body,
        name=name,
        grid=(n_l, nb),
        in_specs=in_specs,
        out_specs=[stacked] * 4,
        out_shape=[jax.ShapeDtypeStruct((n_l * r, c), F32)] * 4,
        compiler_params=_params(("arbitrary", "arbitrary")),
    )(*operands)
    return tuple(o.reshape(n_l, r, c) for o in outs)


def _train_step(x, target, weights, m_state, v_state):
    me = _block_id(lax.axis_index("x"), lax.axis_index("y"), lax.axis_index("c"))

    groups = [(kind, i) for i in range(DEPTH) for kind in ("mix", "ff")]
    forward = [groups[0], ("ff_up", 0), ("ff_down", 0)] + groups[2:]

    def group_names(g):
        kind, i = g
        if kind == "mix":
            return [(n, i // N_MIXERS) for n in MIXER_BIG[i % N_MIXERS]]
        return [(n, i) for n in {"ff": ["ff_w1", "ff_w2"], "ff_up": ["ff_w1"], "ff_down": ["ff_w2"]}[kind]]

    def shards_of(g):
        return [_shard_for_gather(n, weights[n][j]) for n, j in group_names(g)]

    whole = lambda g: [n in DEV_BLOCKED for n, _ in group_names(g)]

    small_shapes = [_shard_shape(n) for n in SMALL_SHARDED]
    gather = {0: _gather_start([_pack_rows([weights[n] for n in SMALL_SHARDED], SMALL_WIDTH, SUBLANES, F32)]
                               + shards_of(groups[0]), name="gather_g0_start", by_cols=[False] + whole(groups[0]))}
    cur, cur_m = x[0], x[0]
    prepared = {k: _gather_prepare(shards_of(forward[k]), whole(forward[k]), name=f"gather_g{k}_start")
                for k in range(1, len(forward))}
    early = [a for k in prepared for a in prepared[k][3]]
    gather[0] = _gather_pass_on(gather[0], name="gather_g0_pass", after=early)
    first = _gather_finish(gather[0], name="gather_g0_finish", after=gather[0]["token"])
    small_all, first_matrices = first[0], first[1:]
    small_full = {n: weights[n] for n in SMALL if PARAMS[n][1] is None}
    for n, stacked in zip(SMALL_SHARDED, _unpack_rows(small_all, small_shapes, SMALL_WIDTH, SUBLANES)):
        small_full[n] = _merge_shards(stacked, PARAMS[n][1])

    def small_weights(i):
        wl = {}
        for n in SMALL:
            if n.startswith("ln_"):
                wl[n] = small_full[n][i:i + 1]
            elif n.startswith(("sc_", "mla_", "cf_")[i % N_MIXERS]):
                v = small_full[n][i // N_MIXERS]
                wl[n] = v if v.ndim == 2 else v.reshape(1, -1)
        return wl

    layers = [small_weights(i) for i in range(DEPTH)]

    def install(g, gathered):
        for (n, _), a in zip(group_names(g), gathered):
            layers[g[1]][n] = _as_operand(n, a)

    install(forward[0], first_matrices)
    gather[1] = _gather_start(None, name="gather_g1_start", after=gather[0]["token"], prepared=prepared[1])
    gather[2] = _gather_start(None, name="gather_g2_start", after=gather[1]["token"], prepared=prepared[2])
    saved = {}
    hidden = None
    for k, g in enumerate(forward):
        kind, i = g
        tokens = [gather[1]["token"]] if k == 0 else []
        if k + 2 < len(forward):
            tokens.append(gather[k + 2]["token"])

        def pass_on(y, k=k):
            if k + 1 == len(forward):
                return None
            gather[k + 1] = _gather_pass_on(gather[k + 1], name=f"gather_g{k + 1}_pass", after=y)
            return gather[k + 1]["token"]

        if k + 1 == len(forward):
            loss_local, dy, saved[g] = _ff_fwd(i, cur, cur_m, layers[i], after=tokens, loss_target=target[0])
            break
        if kind == "ff_up":
            hidden = _mlp_up(cur_m, layers[i]["ff_w1"], f"l{i}_ff", after=tokens)
            behind, done = _hooked(hidden, pass_on), hidden
        elif kind == "ff_down":
            cur, cur_m, saved[("ff", i)] = _ff_fwd(i, cur, cur_m, layers[i], hook=pass_on, after=[*tokens, *behind],
                                                   act=hidden)
            done = cur
        else:
            phase = _mix_fwd if kind == "mix" else _ff_fwd
            cur, cur_m, saved[g] = phase(i, cur, cur_m, layers[i], hook=pass_on, after=tokens)
            done = cur
        if k + 3 < len(forward):
            gathered, gather[k + 3] = _gather_finish(gather[k + 1], name=f"gather_g{k + 1}_finish", after=done,
                                                     then_start=prepared[k + 3])
        else:
            gathered = _gather_finish(gather[k + 1], name=f"gather_g{k + 1}_finish", after=done)
        install(forward[k + 1], gathered)

    loss = lax.psum(loss_local, MESH_AXES)

    reduced = {n: [None] * PARAMS[n][0][0] for n in BIG}
    small_grads = {n: [None] * PARAMS[n][0][0] for n in SMALL}
    reduce = {}

    def to_chips(p, after):
        st = reduce[p]
        parts, got = _swap_finish(st["sibling"], name=f"reduce_p{p}_sibling_finish", after=after)
        sums = _chip_partials_group(parts, got, name=f"reduce_p{p}_chipsum")
        st["own"] = [own for own, _ in sums]
        others = [o for _, o in sums]
        st["chips"] = _swap_start(others, [lax.empty(o.shape, o.dtype) for o in others], _to_chip, 3,
                                  name=f"reduce_p{p}_chips_start")
        return st["chips"]["token"]

    def arrived(phases, after):
        done = _swap_finish([reduce[p]["chips"] for p in phases], name=f"reduce_p{phases[0]}_chips_finish", after=after)
        for p, (_, got) in zip(phases, done):
            for (n, j), own, arr in zip(reduce[p]["names"], reduce[p]["own"], got):
                reduced[n][j] = (own, arr)

    terms = [(1.0, dy)]
    back = list(reversed(groups))
    for p, g in enumerate(back):
        kind, i = g

        def after_first_kernel(dz, p=p):
            return to_chips(p - 1, dz) if p >= 1 else None

        phase = _mix_bwd if kind == "mix" else _ff_bwd
        tokens = [reduce[p - 1]["sibling"]["token"]] if p >= 1 else []
        terms, grads = phase(i, terms, saved[g], layers[i], hook=after_first_kernel, after=tokens)
        for n in SMALL:
            if n in grads:
                small_grads[n][i if n.startswith("ln_") else i // N_MIXERS] = grads[n]
        names = group_names(g)
        parts = [_as_parts(n, grads[n]) for n, _ in names]
        reduce[p] = {"names": names, "sibling": _swap_start(
            parts, [lax.empty((4, *a.shape[1:]), a.dtype) for a in parts], _to_sibling, 4,
            name=f"reduce_p{p}_sibling_start")}
    (_, grad_x), = terms
    last = len(back) - 1
    small_mine = _pack_rows([jnp.stack(small_grads[n]).reshape(PARAMS[n][0]) for n in SMALL], SMALL_WIDTH, SUBLANES, F32)
    small_gather = _gather_start([small_mine], name="gather_small_grads_start")
    tail = to_chips(last, small_gather["token"])
    arrived(list(range(last)), tail)

    grad, delta, new_m, new_v = {}, {}, {}, {}

    def adamw_matrix(n, after):
        shape = weights[n].shape
        view = lambda a: a.reshape(shape[0], shape[1], -1)
        outs = _adamw_reduced(view(weights[n]), view(m_state[n]), view(v_state[n]), reduced[n], name=f"adamw_{n}",
                              after=after)
        grad[n], delta[n], new_m[n], new_v[n] = [o.reshape(shape) for o in outs]
        return outs[0]

    done = tail
    for n in BIG:
        if n not in MIXER_BIG[0]:
            done = adamw_matrix(n, [done])
    small_gather = _gather_pass_on(small_gather, name="gather_small_grads_pass", after=done)
    small_all = _gather_finish(small_gather, name="gather_small_grads_finish", after=small_gather["token"])[0]
    small_sum = _sum_blocks(None, small_all, name="sum_small_grads")
    for n, g in zip(SMALL, _unpack_rows(small_sum, [PARAMS[n][0] for n in SMALL], SMALL_WIDTH, SUBLANES)):
        ax = PARAMS[n][1]
        grad[n] = g if ax is None else lax.dynamic_index_in_dim(_split_shards(g, ax), me, axis=0, keepdims=False)
    small_shards = [_shard_shape(n) for n in SMALL]
    packed = [_pack_rows([src[n] for n in SMALL], SMALL_WIDTH, SUBLANES, F32) for src in (weights, grad, m_state, v_state)]
    small_out = _adamw(*packed, name="adamw_small")
    for outs, dst in zip(small_out, (delta, new_m, new_v)):
        for n, a in zip(SMALL, _unpack_rows(outs, small_shards, SMALL_WIDTH, SUBLANES)):
            dst[n] = a
    arrived([last], small_sum)
    for n in MIXER_BIG[0]:
        adamw_matrix(n, [])
    return (loss, grad_x[None], *[grad[n] for n in NAMES], *[delta[n] for n in NAMES],
            *[new_m[n] for n in NAMES], *[new_v[n] for n in NAMES])


def kernel(x, sc_w_in, sc_conv_w, sc_w_out, mla_w_dq, mla_g_q, mla_w_uq, mla_w_dkv, mla_g_kv, mla_w_uk, mla_w_uv, mla_w_o, cf_w_pw1, cf_b_pw1, cf_dw_w, cf_dw_b, cf_norm_g, cf_norm_b, cf_w_pw2, cf_b_pw2, ff_w1, ff_w2, ln_mix_g, ln_mix_b, ln_ff_g, ln_ff_b, loss_target, m_sc_w_in, m_sc_conv_w, m_sc_w_out, m_mla_w_dq, m_mla_g_q, m_mla_w_uq, m_mla_w_dkv, m_mla_g_kv, m_mla_w_uk, m_mla_w_uv, m_mla_w_o, m_cf_w_pw1, m_cf_b_pw1, m_cf_dw_w, m_cf_dw_b, m_cf_norm_g, m_cf_norm_b, m_cf_w_pw2, m_cf_b_pw2, m_ff_w1, m_ff_w2, m_ln_mix_g, m_ln_mix_b, m_ln_ff_g, m_ln_ff_b, v_sc_w_in, v_sc_conv_w, v_sc_w_out, v_mla_w_dq, v_mla_g_q, v_mla_w_uq, v_mla_w_dkv, v_mla_g_kv, v_mla_w_uk, v_mla_w_uv, v_mla_w_o, v_cf_w_pw1, v_cf_b_pw1, v_cf_dw_w, v_cf_dw_b, v_cf_norm_g, v_cf_norm_b, v_cf_w_pw2, v_cf_b_pw2, v_ff_w1, v_ff_w2, v_ln_mix_g, v_ln_mix_b, v_ln_ff_g, v_ln_ff_b):
    w_list = (sc_w_in, sc_conv_w, sc_w_out, mla_w_dq, mla_g_q, mla_w_uq, mla_w_dkv, mla_g_kv, mla_w_uk, mla_w_uv, mla_w_o,
              cf_w_pw1, cf_b_pw1, cf_dw_w, cf_dw_b, cf_norm_g, cf_norm_b, cf_w_pw2, cf_b_pw2, ff_w1, ff_w2, ln_mix_g,
              ln_mix_b, ln_ff_g, ln_ff_b)
    m_list = (m_sc_w_in, m_sc_conv_w, m_sc_w_out, m_mla_w_dq, m_mla_g_q, m_mla_w_uq, m_mla_w_dkv, m_mla_g_kv, m_mla_w_uk,
              m_mla_w_uv, m_mla_w_o, m_cf_w_pw1, m_cf_b_pw1, m_cf_dw_w, m_cf_dw_b, m_cf_norm_g, m_cf_norm_b, m_cf_w_pw2,
              m_cf_b_pw2, m_ff_w1, m_ff_w2, m_ln_mix_g, m_ln_mix_b, m_ln_ff_g, m_ln_ff_b)
    v_list = (v_sc_w_in, v_sc_conv_w, v_sc_w_out, v_mla_w_dq, v_mla_g_q, v_mla_w_uq, v_mla_w_dkv, v_mla_g_kv, v_mla_w_uk,
              v_mla_w_uv, v_mla_w_o, v_cf_w_pw1, v_cf_b_pw1, v_cf_dw_w, v_cf_dw_b, v_cf_norm_g, v_cf_norm_b, v_cf_w_pw2,
              v_cf_b_pw2, v_ff_w1, v_ff_w2, v_ln_mix_g, v_ln_mix_b, v_ln_ff_g, v_ln_ff_b)
    return _train_step(x, loss_target, dict(zip(NAMES, w_list)), dict(zip(NAMES, m_list)), dict(zip(NAMES, v_list)))
```

```python
import functools

import jax
import jax.numpy as jnp
from jax import lax
from jax.experimental import pallas as pl
from jax.experimental.pallas import tpu as pltpu

F32 = jnp.float32
MXU_DTYPE = jnp.bfloat16

N_DEV = 8
D_MODEL = 1024
DEPTH = 4
N_MIXERS = 3
CHUNK = 64
ALPHA = (2.0 * DEPTH) ** 0.25
LN_EPS = 1e-5
RMS_EPS = 1e-6
MLA_HEADS = 8
QK_NOPE = 128
QK_ROPE = 64
V_HEAD = 128
Q_LORA = 384
KV_LORA = 256
ROPE_THETA = 10000.0
ADAM_LR = 0.001
ADAM_B1 = 0.9
ADAM_B2 = 0.999
ADAM_EPS = 1e-08
ADAM_WD = 0.01
ADAM_STEP = 10

LANES = 128
SUBLANES = 8
BF16_ROWS = 16
VMEM_LIMIT = 56 * 1024 * 1024
MM_VMEM_BUDGET = 36 * 1024 * 1024
CONV_PAD = 32
MASK_VALUE = -1e30

MESH_AXES = ("x", "y", "c")


def _params(semantics):
    return pltpu.CompilerParams(dimension_semantics=semantics, vmem_limit_bytes=VMEM_LIMIT)


def _tile(n, cap):
    if n <= cap:
        return n
    t = cap - cap % LANES
    while t >= LANES:
        if n % t == 0:
            return t
        t -= LANES
    raise ValueError(f"no tile for {n} under {cap}")


def _row_tile(rows, cap):
    if rows <= cap:
        return rows
    t = cap - cap % SUBLANES
    while t >= SUBLANES:
        if rows % t == 0:
            return t
        t -= SUBLANES
    raise ValueError(f"no row tile for {rows} under {cap}")


def _mm(a, b, *, name, ta=False, tb=False, b_dev=False, out_dev=False, out_dtypes=(F32,), epilogue=None,
        tile_extras=(), col_extras=(), after=(), tm_cap=2048, tn_cap=1024, tk_cap=4096):
    m, k = (a.shape[1], a.shape[0]) if ta else a.shape
    if b_dev:
        nd, b_rows, ns = b.shape
        n, kb = (b_rows, nd * ns) if tb else (nd * ns, b_rows)
    else:
        n, kb = (b.shape[0], b.shape[1]) if tb else (b.shape[1], b.shape[0])
    assert k == kb, (a.shape, b.shape, ta, tb)
    if ta or tile_extras or (b_dev and tb):
        tm_cap = min(tm_cap, 1024)
    tk = ns if (b_dev and tb) else _tile(k, tk_cap)
    nk = k // tk
    fixed_tn = ns if (b_dev and not tb) else (n // N_DEV if out_dev else None)
    if out_dev:
        assert not tile_extras and not col_extras and n % N_DEV == 0
    out_bytes = sum(jnp.dtype(dt).itemsize for dt in out_dtypes) + sum(e.dtype.itemsize for e in tile_extras)
    tm, tn = None, None
    for cand_m, cand_n in ((tm_cap, tn_cap), (tm_cap, tn_cap // 2), (tm_cap // 2, tn_cap), (tm_cap // 2, tn_cap // 2),
                           (tm_cap // 4, tn_cap // 2), (tm_cap // 4, tn_cap // 4)):
        tm, tn = _tile(m, cand_m), (fixed_tn if fixed_tn is not None else _tile(n, cand_n))
        need = 2 * (tm * tk * a.dtype.itemsize + tk * tn * b.dtype.itemsize) + 2 * tm * tn * out_bytes
        need += tm * tn * 4 * (2 if nk > 1 else 1)
        if need <= MM_VMEM_BUDGET:
            break
    n_te, n_ce, n_out = len(tile_extras), len(col_extras), len(out_dtypes)
    dims = (((0 if ta else 1,), (1 if tb else 0,)), ((), ()))

    def body(*refs):
        a_ref, b_ref = refs[0], refs[1]
        te_refs = refs[2:2 + n_te]
        ce_refs = refs[2 + n_te:2 + n_te + n_ce]
        first_out = 2 + n_te + n_ce + len(after)
        out_refs = refs[first_out:first_out + n_out]

        def finish(acc):
            if epilogue is None:
                outs = (acc,)
            else:
                outs = epilogue(acc, *[r[...] for r in te_refs], *[r[...] for r in ce_refs])
            for o_ref, o in zip(out_refs, outs):
                o_ref[...] = o.astype(o_ref.dtype)

        def product():
            return lax.dot_general(a_ref[...].astype(MXU_DTYPE), b_ref[...].astype(MXU_DTYPE), dims,
                                   preferred_element_type=F32)

        if nk == 1:
            finish(product())
        else:
            acc_ref = refs[-1]
            kk = pl.program_id(2)

            @pl.when(kk == 0)
            def _():
                acc_ref[...] = jnp.zeros_like(acc_ref)

            acc_ref[...] += product()

            @pl.when(kk == nk - 1)
            def _():
                finish(acc_ref[...])

    a_spec = pl.BlockSpec((tk, tm), lambda i, j, kk: (kk, i)) if ta else pl.BlockSpec((tm, tk), lambda i, j, kk: (i, kk))
    if b_dev and tb:
        b_spec = pl.BlockSpec((None, tn, tk), lambda i, j, kk: (kk, j, 0))
    elif b_dev:
        b_spec = pl.BlockSpec((None, tk, tn), lambda i, j, kk: (j, kk, 0))
    elif tb:
        b_spec = pl.BlockSpec((tn, tk), lambda i, j, kk: (j, kk))
    else:
        b_spec = pl.BlockSpec((tk, tn), lambda i, j, kk: (kk, j))
    tile_spec = pl.BlockSpec((tm, tn), lambda i, j, kk: (i, j))
    col_spec = pl.BlockSpec((1, tn), lambda i, j, kk: (0, j))
    if out_dev:
        out_spec = pl.BlockSpec((None, tm, tn), lambda i, j, kk: (j, i, 0))
        out_shape = [jax.ShapeDtypeStruct((N_DEV, m, tn), dt) for dt in out_dtypes]
    else:
        out_spec = tile_spec
        out_shape = [jax.ShapeDtypeStruct((m, n), dt) for dt in out_dtypes]
    outs = pl.pallas_call(
        body,
        name=name,
        grid=(m // tm, n // tn, nk),
        in_specs=[a_spec, b_spec] + [tile_spec] * n_te + [col_spec] * n_ce + [ANY] * len(after),
        out_specs=[out_spec] * n_out,
        out_shape=out_shape,
        scratch_shapes=[pltpu.VMEM((tm, tn), F32)] if nk > 1 else [],
        compiler_params=_params(("parallel", "parallel", "arbitrary")),
    )(a, b, *tile_extras, *col_extras, *after)
    return outs[0] if n_out == 1 else tuple(outs)


def _rowwise(fn, rows, bcast, out_rows, out_accs, *, name, tb_cap=256, after=()):
    t = rows[0].shape[0]
    tb = _row_tile(t, tb_cap)
    n_r, n_b, n_o, n_a = len(rows), len(bcast), len(out_rows), len(out_accs)

    def body(*refs):
        r_refs = refs[:n_r]
        b_refs = refs[n_r:n_r + n_b]
        first_out = n_r + n_b + len(after)
        o_refs = refs[first_out:first_out + n_o]
        a_refs = refs[first_out + n_o:]
        outs, accs = fn([r[...] for r in r_refs], [r[...] for r in b_refs])
        for o_ref, o in zip(o_refs, outs):
            o_ref[...] = o.astype(o_ref.dtype)
        if n_a:
            i = pl.program_id(0)

            @pl.when(i == 0)
            def _():
                for a_ref, acc in zip(a_refs, accs):
                    a_ref[...] = acc

            @pl.when(i > 0)
            def _():
                for a_ref, acc in zip(a_refs, accs):
                    a_ref[...] += acc

    in_specs = [pl.BlockSpec((tb, r.shape[1]), lambda i: (i, 0)) for r in rows]
    in_specs += [pl.BlockSpec(b.shape, lambda i: (0, 0)) for b in bcast]
    in_specs += [ANY] * len(after)
    out_specs = [pl.BlockSpec((tb, c), lambda i: (i, 0)) for c, _ in out_rows]
    out_specs += [pl.BlockSpec(s, lambda i: (0, 0)) for s in out_accs]
    out_shape = [jax.ShapeDtypeStruct((t, c), dt) for c, dt in out_rows]
    out_shape += [jax.ShapeDtypeStruct(s, F32) for s in out_accs]
    outs = pl.pallas_call(
        body,
        name=name,
        grid=(t // tb,),
        in_specs=in_specs,
        out_specs=out_specs,
        out_shape=out_shape,
        compiler_params=_params(("arbitrary",)),
    )(*rows, *bcast, *after)
    return tuple(outs)


def _colsum(v):
    return jnp.sum(v, axis=0, keepdims=True)


def _sigmoid(v):
    return 1.0 / (1.0 + jnp.exp(-v))


def _ln_fwd(x_res, y_sub, g, b, *, name, after=()):
    d = x_res.shape[1]

    def fn(rows, bc):
        z = ALPHA * rows[0] + rows[1]
        mu = jnp.mean(z, axis=-1, keepdims=True)
        zc = z - mu
        var = jnp.mean(zc * zc, axis=-1, keepdims=True)
        rstd = lax.rsqrt(var + LN_EPS)
        xhat = zc * rstd
        out = xhat * bc[0] + bc[1]
        return (out, out, xhat, rstd), ()

    return _rowwise(fn, [x_res, y_sub], [g, b], [(d, F32), (d, MXU_DTYPE), (d, F32), (1, F32)], [], name=name,
                    after=after, tb_cap=512)


def _ln_fwd_loss(x_res, y_sub, g, b, target, *, name, after=()):
    d = x_res.shape[1]

    def fn(rows, bc):
        z = ALPHA * rows[0] + rows[1]
        mu = jnp.mean(z, axis=-1, keepdims=True)
        zc = z - mu
        var = jnp.mean(zc * zc, axis=-1, keepdims=True)
        rstd = lax.rsqrt(var + LN_EPS)
        xhat = zc * rstd
        err = xhat * bc[0] + bc[1] - rows[2]
        return (xhat, rstd, err * (1.0 / d)), (_colsum(err * err),)

    return _rowwise(fn, [x_res, y_sub, target], [g, b], [(d, F32), (1, F32), (d, F32)], [(1, d)], name=name,
                    after=after, tb_cap=512)


def _ln_bwd(terms, xhat, rstd, g, *, name, after=()):
    d = xhat.shape[1]
    coefs = [c for c, _ in terms]

    def fn(rows, bc):
        xh, rs = rows[0], rows[1]
        dout = None
        for c, v in zip(coefs, rows[2:]):
            tv = v if c == 1.0 else c * v
            dout = tv if dout is None else dout + tv
        dxh = dout * bc[0]
        m1 = jnp.mean(dxh, axis=-1, keepdims=True)
        m2 = jnp.mean(dxh * xh, axis=-1, keepdims=True)
        dz = rs * (dxh - m1 - xh * m2)
        return (dz, dz), (_colsum(dout * xh), _colsum(dout), _colsum(dz))

    return _rowwise(fn, [xhat, rstd] + [v for _, v in terms], [g], [(d, F32), (d, MXU_DTYPE)],
                    [(1, d), (1, d), (1, d)], name=name, after=after, tb_cap=512)


def _rms_fwd(x, g, *, name):
    c = x.shape[1]

    def fn(rows, bc):
        v = rows[0]
        r = lax.rsqrt(jnp.mean(v * v, axis=-1, keepdims=True) + RMS_EPS)
        return (v * r * bc[0],), ()

    return _rowwise(fn, [x], [g], [(c, MXU_DTYPE)], [], name=name)[0]


def _rms_bwd(x, g, dy, *, name):
    c = x.shape[1]

    def fn(rows, bc):
        v, dyv = rows
        r = lax.rsqrt(jnp.mean(v * v, axis=-1, keepdims=True) + RMS_EPS)
        xh = v * r
        dxh = dyv * bc[0]
        dx = r * (dxh - xh * jnp.mean(dxh * xh, axis=-1, keepdims=True))
        return (dx,), (_colsum(dyv * xh),)

    return _rowwise(fn, [x, dy], [g], [(c, F32)], [(1, c)], name=name)


def _conv_chunk(t):
    return _row_tile(t, 256)


def _conv_fwd(x, w, bias, *, name):
    t, c = x.shape
    kw = w.shape[0]
    ch = _conv_chunk(t)

    def body(x_ref, w_ref, b_ref, o_ref, pad_ref):
        pad_ref[0:CONV_PAD, :] = jnp.zeros((CONV_PAD, LANES), F32)
        pad_ref[CONV_PAD:CONV_PAD + t, :] = x_ref[...]
        for t0 in range(0, t, ch):
            acc = jnp.broadcast_to(b_ref[...], (ch, LANES))
            for k in range(kw):
                s = kw - 1 - k
                acc = acc + w_ref[k:k + 1, :] * pad_ref[CONV_PAD - s + t0:CONV_PAD - s + t0 + ch, :]
            o_ref[t0:t0 + ch, :] = acc

    return pl.pallas_call(
        body,
        name=name,
        grid=(c // LANES,),
        in_specs=[pl.BlockSpec((t, LANES), lambda j: (0, j)), pl.BlockSpec((kw, LANES), lambda j: (0, j)),
                  pl.BlockSpec((1, LANES), lambda j: (0, j))],
        out_specs=pl.BlockSpec((t, LANES), lambda j: (0, j)),
        out_shape=jax.ShapeDtypeStruct((t, c), F32),
        scratch_shapes=[pltpu.VMEM((CONV_PAD + t, LANES), F32)],
        compiler_params=_params(("parallel",)),
    )(x, w, bias)


def _conv_bwd(dy, x, w, *, name):
    t, c = x.shape
    kw = w.shape[0]
    ch = _conv_chunk(t)

    def body(dy_ref, x_ref, w_ref, dx_ref, dw_ref, xpad_ref, dpad_ref):
        xpad_ref[0:CONV_PAD, :] = jnp.zeros((CONV_PAD, LANES), F32)
        xpad_ref[CONV_PAD:CONV_PAD + t, :] = x_ref[...]
        dpad_ref[0:t, :] = dy_ref[...]
        dpad_ref[t:t + CONV_PAD, :] = jnp.zeros((CONV_PAD, LANES), F32)
        for t0 in range(0, t, ch):
            acc = jnp.zeros((ch, LANES), F32)
            for k in range(kw):
                s = kw - 1 - k
                acc = acc + w_ref[k:k + 1, :] * dpad_ref[t0 + s:t0 + s + ch, :]
            dx_ref[t0:t0 + ch, :] = acc
        for k in range(kw):
            s = kw - 1 - k
            acc = jnp.zeros((ch, LANES), F32)
            for t0 in range(0, t, ch):
                acc = acc + dy_ref[t0:t0 + ch, :] * xpad_ref[CONV_PAD - s + t0:CONV_PAD - s + t0 + ch, :]
            dw_ref[k:k + 1, :] = _colsum(acc)

    blk = pl.BlockSpec((t, LANES), lambda j: (0, j))
    wblk = pl.BlockSpec((kw, LANES), lambda j: (0, j))
    return pl.pallas_call(
        body,
        name=name,
        grid=(c // LANES,),
        in_specs=[blk, blk, wblk],
        out_specs=[blk, wblk],
        out_shape=[jax.ShapeDtypeStruct((t, c), F32), jax.ShapeDtypeStruct((kw, c), F32)],
        scratch_shapes=[pltpu.VMEM((CONV_PAD + t, LANES), F32), pltpu.VMEM((CONV_PAD + t, LANES), F32)],
        compiler_params=_params(("parallel",)),
    )(dy, x, w)


ATTN_SCALE = (QK_NOPE + QK_ROPE) ** -0.5
_NT = (((1,), (1,)), ((), ()))
_TN = (((0,), (0,)), ((), ()))
_NN = (((1,), (0,)), ((), ()))


def _dot(a, b, dims):
    return lax.dot_general(a, b, dims, preferred_element_type=F32)


def _block_scores(q_cat, kn_ref, kp_ref, lo, hi, diagonal):
    k_cat = jnp.concatenate([kn_ref[lo:hi, :], kp_ref[lo:hi, :]], axis=1)
    s = _dot(q_cat, k_cat, _NT) * ATTN_SCALE
    if diagonal:
        shift = CHUNK.bit_length() - 1
        rows = lax.broadcasted_iota(jnp.int32, s.shape, 0)
        cols = lax.broadcasted_iota(jnp.int32, s.shape, 1)
        s = jnp.where(jnp.right_shift(cols, shift) <= jnp.right_shift(rows, shift), s, MASK_VALUE)
    return s, k_cat


def _attn_fwd(qn, qp, kv, kp, *, name):
    h, t, _ = qp.shape
    tq = _row_tile(t, 256)
    assert tq % CHUNK == 0

    def body(qn_ref, qp_ref, kn_ref, kp_ref, v_ref, o_ref, of_ref, lse_ref):
        for r0 in range(0, t, tq):
            r1 = r0 + tq
            q_cat = jnp.concatenate([qn_ref[r0:r1, :].astype(MXU_DTYPE), qp_ref[r0:r1, :]], axis=1)
            s_d, _ = _block_scores(q_cat, kn_ref, kp_ref, r0, r1, True)
            m = jnp.max(s_d, axis=-1, keepdims=True)
            if r0:
                s_b, _ = _block_scores(q_cat, kn_ref, kp_ref, 0, r0, False)
                m = jnp.maximum(m, jnp.max(s_b, axis=-1, keepdims=True))
            p_d = jnp.exp(s_d - m)
            l = jnp.sum(p_d, axis=-1, keepdims=True)
            acc = _dot(p_d.astype(MXU_DTYPE), v_ref[r0:r1, :], _NN)
            if r0:
                p_b = jnp.exp(s_b - m)
                l = l + jnp.sum(p_b, axis=-1, keepdims=True)
                acc = acc + _dot(p_b.astype(MXU_DTYPE), v_ref[0:r0, :], _NN)
            o = acc / l
            o_ref[r0:r1, :] = o.astype(o_ref.dtype)
            of_ref[r0:r1, :] = o
            lse_ref[r0:r1, :] = m + jnp.log(l)

    blk = lambda w: pl.BlockSpec((None, t, w), lambda hh: (hh, 0, 0))
    cols = pl.BlockSpec((t, QK_NOPE), lambda hh: (0, hh))
    keys = pl.BlockSpec((t, QK_NOPE), lambda hh: (0, 2 * hh))
    values = pl.BlockSpec((t, V_HEAD), lambda hh: (0, 2 * hh + 1))
    return pl.pallas_call(
        body,
        name=name,
        grid=(h,),
        in_specs=[cols, blk(QK_ROPE), keys, pl.BlockSpec((t, QK_ROPE), lambda hh: (0, 0)), values],
        out_specs=[cols, cols, blk(1)],
        out_shape=[jax.ShapeDtypeStruct((t, h * V_HEAD), MXU_DTYPE), jax.ShapeDtypeStruct((t, h * V_HEAD), F32),
                   jax.ShapeDtypeStruct((h, t, 1), F32)],
        compiler_params=_params(("parallel",)),
    )(qn, qp, kv, kp, kv)


def _attn_bwd(qn, qp, kv, kp, do, o, lse, *, name):
    h, t, _ = qp.shape
    tq = _row_tile(t, 256)

    def body(qn_ref, qp_ref, kn_ref, kp_ref, v_ref, do_ref, o_ref, lse_ref, dqn_ref, dqp_ref, dkv_ref, dkp_ref):
        dkv_ref[...] = jnp.zeros_like(dkv_ref)

        @pl.when(pl.program_id(0) == 0)
        def _():
            dkp_ref[...] = jnp.zeros_like(dkp_ref)

        for r0 in range(0, t, tq):
            r1 = r0 + tq
            q_cat = jnp.concatenate([qn_ref[r0:r1, :].astype(MXU_DTYPE), qp_ref[r0:r1, :]], axis=1)
            dov, lse = do_ref[r0:r1, :], lse_ref[r0:r1, :]
            delta = jnp.sum(dov.astype(F32) * o_ref[r0:r1, :], axis=-1, keepdims=True)

            def piece(lo, hi, diagonal):
                s, k_cat = _block_scores(q_cat, kn_ref, kp_ref, lo, hi, diagonal)
                p = jnp.exp(s - lse)
                dp = _dot(dov, v_ref[lo:hi, :], _NT)
                ds_m = (p * (dp - delta) * ATTN_SCALE).astype(MXU_DTYPE)
                dk_cat = _dot(ds_m, q_cat, _TN)
                dkv_ref[lo:hi, :QK_NOPE] += dk_cat[:, :QK_NOPE]
                dkp_ref[lo:hi, :] += dk_cat[:, QK_NOPE:]
                dkv_ref[lo:hi, QK_NOPE:] += _dot(p.astype(MXU_DTYPE), dov, _TN)
                return _dot(ds_m, k_cat, _NN)

            dq_cat = piece(r0, r1, True)
            if r0:
                dq_cat = dq_cat + piece(0, r0, False)
            dqn_ref[r0:r1, :] = dq_cat[:, :QK_NOPE]
            dqp_ref[r0:r1, :] = dq_cat[:, QK_NOPE:]

    blk = lambda w: pl.BlockSpec((None, t, w), lambda hh: (hh, 0, 0))
    kpblk = pl.BlockSpec((t, QK_ROPE), lambda hh: (0, 0))
    cols = pl.BlockSpec((t, QK_NOPE), lambda hh: (0, hh))
    keys = pl.BlockSpec((t, QK_NOPE), lambda hh: (0, 2 * hh))
    values = pl.BlockSpec((t, V_HEAD), lambda hh: (0, 2 * hh + 1))
    both = pl.BlockSpec((t, QK_NOPE + V_HEAD), lambda hh: (0, hh))
    return pl.pallas_call(
        body,
        name=name,
        grid=(h,),
        in_specs=[cols, blk(QK_ROPE), keys, kpblk, values, cols, cols, blk(1)],
        out_specs=[cols, blk(QK_ROPE), both, kpblk],
        out_shape=[jax.ShapeDtypeStruct((t, h * QK_NOPE), F32), jax.ShapeDtypeStruct((h, t, QK_ROPE), F32),
                   jax.ShapeDtypeStruct((t, h * (QK_NOPE + V_HEAD)), F32), jax.ShapeDtypeStruct((t, QK_ROPE), F32)],
        compiler_params=_params(("arbitrary",)),
    )(qn, qp, kv, kp, kv, do, o, lse)


def _rope(x1, x2, cos, sin, *, name, inverse=False):
    w = x1.shape[1]
    sign = -1.0 if inverse else 1.0

    def fn(rows, bc):
        a, b, c, s = rows
        s = sign * s
        return (a * c - b * s, a * s + b * c), ()

    return _rowwise(fn, [x1, x2, cos, sin], [], [(w, F32), (w, F32)], [], name=name)


def _rope_in_place(x, cos_pairs, sin_signed, *, name, inverse=False):
    w = x.shape[1]
    half = ROPE_HALF
    sign = -1.0 if inverse else 1.0

    def fn(rows, bc):
        v, c, s = rows
        lane = lax.broadcasted_iota(jnp.int32, v.shape, 1)
        partner = jnp.where((lane & half) == 0, pltpu.roll(v, w - half, 1), pltpu.roll(v, half, 1))
        return (v * c + partner * (sign * s),), ()

    return _rowwise(fn, [x, cos_pairs, sin_signed], [], [(w, F32)], [], name=name)[0]


def _mlp_up(x_m, w1, tag, after=()):
    def epi(acc):
        r = jnp.maximum(acc, 0.0)
        return (r * r,)

    return _mm(x_m, w1, name=f"{tag}_up", out_dtypes=(MXU_DTYPE,), epilogue=epi, after=after)


def _mlp_fwd(x_m, w1, w2, tag, after=(), act=None):
    if act is None:
        act, after = _mlp_up(x_m, w1, tag, after=after), ()
    y = _mm(act, w2, name=f"{tag}_down", after=after)
    return y, (x_m, act)


def _plus_residual(acc, res):
    return (acc + ALPHA * res,)


def _mlp_bwd(dy_m, saved, w1, w2, tag, residual, after=()):
    x_m, act = saved
    dw2 = _mm(act, dy_m, ta=True, name=f"{tag}_dw2", out_dtypes=(MXU_DTYPE,))

    def epi(acc, av):
        return (acc * (2.0 * jnp.sqrt(av.astype(F32))),)

    dh = _mm(dy_m, w2, tb=True, name=f"{tag}_dact", out_dtypes=(MXU_DTYPE,), epilogue=epi, tile_extras=(act,),
             after=after)
    dw1 = _mm(x_m, dh, ta=True, out_dev=True, name=f"{tag}_dw1", out_dtypes=(MXU_DTYPE,))
    dx = _mm(dh, w1, tb=True, name=f"{tag}_dx", epilogue=_plus_residual, tile_extras=(residual,))
    return [dx], {"ff_w1": dw1, "ff_w2": dw2}


def _sc_gate_specs(t, d):
    per = d // LANES
    return [pl.BlockSpec((t, LANES), lambda j, k=k: (0, k * per + j)) for k in range(3)]


def _sc_gate_fwd(u, conv_w, *, name):
    t, d = u.shape[0], u.shape[1] // 3
    kw = conv_w.shape[0]
    ch = _conv_chunk(t)

    def body(b_ref, c_ref, h_ref, w_ref, r_ref, pad_ref):
        pad_ref[0:CONV_PAD, :] = jnp.zeros((CONV_PAD, LANES), F32)
        pad_ref[CONV_PAD:CONV_PAD + t, :] = c_ref[...] * h_ref[...]
        for t0 in range(0, t, ch):
            q = jnp.zeros((ch, LANES), F32)
            for k in range(kw):
                s = kw - 1 - k
                q = q + w_ref[k:k + 1, :] * pad_ref[CONV_PAD - s + t0:CONV_PAD - s + t0 + ch, :]
            r_ref[t0:t0 + ch, :] = (b_ref[t0:t0 + ch, :] * q).astype(r_ref.dtype)

    blk = pl.BlockSpec((t, LANES), lambda j: (0, j))
    return pl.pallas_call(
        body,
        name=name,
        grid=(d // LANES,),
        in_specs=_sc_gate_specs(t, d) + [pl.BlockSpec((kw, LANES), lambda j: (0, j))],
        out_specs=blk,
        out_shape=jax.ShapeDtypeStruct((t, d), MXU_DTYPE),
        scratch_shapes=[pltpu.VMEM((CONV_PAD + t, LANES), F32)],
        compiler_params=_params(("parallel",)),
    )(u, u, u, conv_w)


def _sc_gate_bwd(dr, u, conv_w, *, name):
    t, d = u.shape[0], u.shape[1] // 3
    kw = conv_w.shape[0]
    ch = _conv_chunk(t)

    def body(dr_ref, b_ref, c_ref, h_ref, w_ref, db_ref, dp_ref, dw_ref, ppad_ref, dpad_ref):
        ppad_ref[0:CONV_PAD, :] = jnp.zeros((CONV_PAD, LANES), F32)
        ppad_ref[CONV_PAD:CONV_PAD + t, :] = c_ref[...] * h_ref[...]
        dpad_ref[0:t, :] = dr_ref[...] * b_ref[...]
        dpad_ref[t:t + CONV_PAD, :] = jnp.zeros((CONV_PAD, LANES), F32)
        for t0 in range(0, t, ch):
            q = jnp.zeros((ch, LANES), F32)
            dp = jnp.zeros((ch, LANES), F32)
            for k in range(kw):
                s = kw - 1 - k
                q = q + w_ref[k:k + 1, :] * ppad_ref[CONV_PAD - s + t0:CONV_PAD - s + t0 + ch, :]
                dp = dp + w_ref[k:k + 1, :] * dpad_ref[t0 + s:t0 + s + ch, :]
            db_ref[t0:t0 + ch, :] = (dr_ref[t0:t0 + ch, :] * q).astype(db_ref.dtype)
            dp_ref[t0:t0 + ch, :] = dp
        for k in range(kw):
            s = kw - 1 - k
            acc = jnp.zeros((ch, LANES), F32)
            for t0 in range(0, t, ch):
                acc = acc + dpad_ref[t0:t0 + ch, :] * ppad_ref[CONV_PAD - s + t0:CONV_PAD - s + t0 + ch, :]
            dw_ref[k:k + 1, :] = _colsum(acc)

    blk = pl.BlockSpec((t, LANES), lambda j: (0, j))
    wblk = pl.BlockSpec((kw, LANES), lambda j: (0, j))
    return pl.pallas_call(
        body,
        name=name,
        grid=(d // LANES,),
        in_specs=[blk] + _sc_gate_specs(t, d) + [wblk],
        out_specs=[blk, blk, wblk],
        out_shape=[jax.ShapeDtypeStruct((t, d), MXU_DTYPE), jax.ShapeDtypeStruct((t, d), F32),
                   jax.ShapeDtypeStruct((kw, d), F32)],
        scratch_shapes=[pltpu.VMEM((CONV_PAD + t, LANES), F32), pltpu.VMEM((CONV_PAD + t, LANES), F32)],
        compiler_params=_params(("parallel",)),
    )(dr, u, u, u, conv_w)


def _sc_fwd(x_m, w_in, conv_w, w_out, tag, after=()):
    u = _mm(x_m, w_in, name=f"{tag}_in", after=after)
    r = _sc_gate_fwd(u, conv_w, name=f"{tag}_gate")
    y = _mm(r, w_out, name=f"{tag}_out")
    return y, (x_m, u, r)


def _sc_bwd(dy_m, saved, w_in, conv_w, w_out, tag, residual, after=()):
    d = D_MODEL
    x_m, u, r = saved
    dw_out = _mm(r, dy_m, ta=True, name=f"{tag}_dwout", out_dtypes=(MXU_DTYPE,))
    dr = _mm(dy_m, w_out, tb=True, name=f"{tag}_dr", after=after)
    db, dp, dconv = _sc_gate_bwd(dr, u, conv_w, name=f"{tag}_dgate")

    def du_fn(rows, bc):
        dbv, dpv, uv = rows
        return (jnp.concatenate([dbv.astype(F32), dpv * uv[:, 2 * d:3 * d], dpv * uv[:, d:2 * d]], axis=1),), ()

    du = _rowwise(du_fn, [db, dp, u], [], [(3 * d, MXU_DTYPE)], [], name=f"{tag}_du")[0]
    dw_in = _mm(x_m, du, ta=True, out_dev=True, name=f"{tag}_dwin", out_dtypes=(MXU_DTYPE,))
    dx = _mm(du, w_in, tb=True, name=f"{tag}_dx", epilogue=_plus_residual, tile_extras=(residual,))
    return [dx], {"sc_w_in": dw_in, "sc_conv_w": dconv, "sc_w_out": dw_out}


def _cf_fwd(x_m, w, tag, after=()):
    d = D_MODEL

    def bias_epi(acc, bias):
        return (acc + bias,)

    u = _mm(x_m, w["cf_w_pw1"], name=f"{tag}_pw1", epilogue=bias_epi, col_extras=(w["cf_b_pw1"],),
            after=after)

    def glu_fn(rows, bc):
        uv = rows[0]
        return (uv[:, 0:d] * _sigmoid(uv[:, d:2 * d]),), ()

    h1 = _rowwise(glu_fn, [u], [], [(d, F32)], [], name=f"{tag}_glu")[0]
    h2 = _conv_fwd(h1, w["cf_dw_w"], w["cf_dw_b"], name=f"{tag}_conv")

    def norm_fn(rows, bc):
        hv = rows[0]
        mu = jnp.mean(hv, axis=-1, keepdims=True)
        hc = hv - mu
        rstd = lax.rsqrt(jnp.mean(hc * hc, axis=-1, keepdims=True) + LN_EPS)
        h3 = hc * rstd * bc[0] + bc[1]
        return (h3 * _sigmoid(h3),), ()

    h4 = _rowwise(norm_fn, [h2], [w["cf_norm_g"], w["cf_norm_b"]], [(d, MXU_DTYPE)], [], name=f"{tag}_norm")[0]
    y = _mm(h4, w["cf_w_pw2"], name=f"{tag}_pw2", epilogue=bias_epi, col_extras=(w["cf_b_pw2"],))
    return y, (x_m, u, h1, h2, h4)


def _cf_bwd(dy_m, dy_colsum, saved, w, tag, residual, after=()):
    d = D_MODEL
    x_m, u, h1, h2, h4 = saved
    dw_pw2 = _mm(h4, dy_m, ta=True, name=f"{tag}_dwpw2", out_dtypes=(MXU_DTYPE,))
    dh4 = _mm(dy_m, w["cf_w_pw2"], tb=True, name=f"{tag}_dh4", after=after)

    def dnorm_fn(rows, bc):
        dh4v, hv = rows
        g, b = bc
        mu = jnp.mean(hv, axis=-1, keepdims=True)
        hc = hv - mu
        rstd = lax.rsqrt(jnp.mean(hc * hc, axis=-1, keepdims=True) + LN_EPS)
        xh = hc * rstd
        h3 = xh * g + b
        sg = _sigmoid(h3)
        dh3 = dh4v * (sg * (1.0 + h3 * (1.0 - sg)))
        dxh = dh3 * g
        dh2 = rstd * (dxh - jnp.mean(dxh, axis=-1, keepdims=True) - xh * jnp.mean(dxh * xh, axis=-1, keepdims=True))
        return (dh2,), (_colsum(dh3 * xh), _colsum(dh3), _colsum(dh2))

    dh2, dnorm_g, dnorm_b, ddw_b = _rowwise(dnorm_fn, [dh4, h2], [w["cf_norm_g"], w["cf_norm_b"]], [(d, F32)],
                                            [(1, d), (1, d), (1, d)], name=f"{tag}_dnorm")
    dh1, ddw_w = _conv_bwd(dh2, h1, w["cf_dw_w"], name=f"{tag}_dconv")

    def dglu_fn(rows, bc):
        dh1v, uv = rows
        a, gate = uv[:, 0:d], uv[:, d:2 * d]
        sg = _sigmoid(gate)
        du = jnp.concatenate([dh1v * sg, dh1v * a * sg * (1.0 - sg)], axis=1)
        return (du,), (_colsum(du),)

    du, db_pw1 = _rowwise(dglu_fn, [dh1, u], [], [(2 * d, MXU_DTYPE)], [(1, 2 * d)], name=f"{tag}_dglu")
    dw_pw1 = _mm(x_m, du, ta=True, out_dev=True, name=f"{tag}_dwpw1", out_dtypes=(MXU_DTYPE,))
    dx = _mm(du, w["cf_w_pw1"], tb=True, name=f"{tag}_dx", epilogue=_plus_residual, tile_extras=(residual,))
    grads = {"cf_w_pw1": dw_pw1, "cf_b_pw1": db_pw1, "cf_dw_w": ddw_w, "cf_dw_b": ddw_b, "cf_norm_g": dnorm_g,
             "cf_norm_b": dnorm_b, "cf_w_pw2": dw_pw2, "cf_b_pw2": dy_colsum}
    return [dx], grads


def _rope_tables(t):
    pos = jnp.arange(t, dtype=F32)
    inv_freq = ROPE_THETA ** (-jnp.arange(0, QK_ROPE, 2, dtype=F32) / QK_ROPE)
    ang = pos[:, None] * inv_freq[None, :]
    return jnp.cos(ang), jnp.sin(ang)


ROPE_HALF = QK_ROPE // 2
Q_NOPE_COLS = MLA_HEADS * QK_NOPE


def _interleave_heads(w_uk, w_uv):
    c = w_uk.shape[0]
    return jnp.concatenate([w_uk.reshape(c, MLA_HEADS, QK_NOPE), w_uv.reshape(c, MLA_HEADS, V_HEAD)],
                           axis=2).reshape(c, -1)


def _mla_fwd(x_m, w, tag, after=()):
    t = x_m.shape[0]
    cos, sin = _rope_tables(t)
    cos_h = jnp.tile(jnp.concatenate([cos, cos], axis=1), (1, MLA_HEADS))
    sin_h = jnp.tile(jnp.concatenate([-sin, sin], axis=1), (1, MLA_HEADS))
    cq_pre = _mm(x_m, w["mla_w_dq"], name=f"{tag}_dq", after=after)
    cq = _rms_fwd(cq_pre, w["mla_g_q"], name=f"{tag}_qnorm")
    q = _mm(cq, w["mla_w_uq"], name=f"{tag}_uq")
    qp = _rope_in_place(q[:, Q_NOPE_COLS:], cos_h, sin_h, name=f"{tag}_qrope")
    qp_h = qp.reshape(t, MLA_HEADS, QK_ROPE).transpose(1, 0, 2).astype(MXU_DTYPE)
    ckv_full = _mm(x_m, w["mla_w_dkv"], name=f"{tag}_dkv")
    ckv_pre = ckv_full[:, :KV_LORA]
    ckv = _rms_fwd(ckv_pre, w["mla_g_kv"], name=f"{tag}_kvnorm")
    k1, k2 = _rope(ckv_full[:, KV_LORA:KV_LORA + ROPE_HALF], ckv_full[:, KV_LORA + ROPE_HALF:], cos, sin,
                   name=f"{tag}_krope")
    kp = jnp.concatenate([k1, k2], axis=-1).astype(MXU_DTYPE)
    kv = _mm(ckv, _interleave_heads(w["mla_w_uk"], w["mla_w_uv"]), name=f"{tag}_ukv", out_dtypes=(MXU_DTYPE,))
    o, o_f, lse = _attn_fwd(q, qp_h, kv, kp, name=f"{tag}_attn")
    y = _mm(o, w["mla_w_o"], name=f"{tag}_o")
    return y, (x_m, cq_pre, cq, ckv_pre, ckv, q, qp_h, kv, kp, lse, o, o_f, (cos, sin, cos_h, sin_h))


def _mla_bwd(dy_m, saved, w, tag, residual, after=()):
    x_m, cq_pre, cq, ckv_pre, ckv, q, qp_h, kv, kp, lse, o, o_f, (cos, sin, cos_h, sin_h) = saved
    t = x_m.shape[0]
    half = ROPE_HALF
    dw_o = _mm(o, dy_m, ta=True, name=f"{tag}_dwo", out_dtypes=(MXU_DTYPE,))
    do = _mm(dy_m, w["mla_w_o"], tb=True, name=f"{tag}_do", out_dtypes=(MXU_DTYPE,), after=after)
    dqn, dqp_h, dkv, dkp = _attn_bwd(q, qp_h, kv, kp, do, o_f, lse, name=f"{tag}_dattn")
    dqp = _rope_in_place(dqp_h.transpose(1, 0, 2).reshape(t, -1), cos_h, sin_h, name=f"{tag}_dqrope", inverse=True)
    dq = jnp.concatenate([dqn, dqp], axis=-1).astype(MXU_DTYPE)
    dw_uq = _mm(cq, dq, ta=True, name=f"{tag}_dwuq", out_dtypes=(MXU_DTYPE,))
    dcq = _mm(dq, w["mla_w_uq"], tb=True, name=f"{tag}_dcq")
    dcq_pre, dg_q = _rms_bwd(cq_pre, w["mla_g_q"], dcq, name=f"{tag}_dqnorm")
    dcq_pre_m = dcq_pre.astype(MXU_DTYPE)
    dw_dq = _mm(x_m, dcq_pre_m, ta=True, name=f"{tag}_dwdq", out_dtypes=(MXU_DTYPE,))
    dx_q = _mm(dcq_pre_m, w["mla_w_dq"], tb=True, name=f"{tag}_dxq", epilogue=_plus_residual, tile_extras=(residual,))
    dw_kv = _mm(ckv, dkv, ta=True, name=f"{tag}_dwukv", out_dtypes=(MXU_DTYPE,))
    dw_kv = dw_kv.reshape(KV_LORA, MLA_HEADS, QK_NOPE + V_HEAD)
    dw_uk, dw_uv = dw_kv[:, :, :QK_NOPE].reshape(KV_LORA, -1), dw_kv[:, :, QK_NOPE:].reshape(KV_LORA, -1)
    dckv = _mm(dkv, _interleave_heads(w["mla_w_uk"], w["mla_w_uv"]), tb=True, name=f"{tag}_dckv")
    dckv_pre, dg_kv = _rms_bwd(ckv_pre, w["mla_g_kv"], dckv, name=f"{tag}_dkvnorm")
    dk1, dk2 = _rope(dkp[:, :half], dkp[:, half:], cos, sin, name=f"{tag}_dkrope", inverse=True)
    dckv_full = jnp.concatenate([dckv_pre, dk1, dk2], axis=-1).astype(MXU_DTYPE)
    dw_dkv = _mm(x_m, dckv_full, ta=True, name=f"{tag}_dwdkv", out_dtypes=(MXU_DTYPE,))
    dx_kv = _mm(dckv_full, w["mla_w_dkv"], tb=True, name=f"{tag}_dxkv", epilogue=lambda acc, other: (acc + other,),
                tile_extras=(dx_q,))
    grads = {"mla_w_dq": dw_dq, "mla_g_q": dg_q, "mla_w_uq": dw_uq, "mla_w_dkv": dw_dkv, "mla_g_kv": dg_kv,
             "mla_w_uk": dw_uk, "mla_w_uv": dw_uv, "mla_w_o": dw_o}
    return [dx_kv], grads


def _cast_input(x):
    cast = lambda rows, bc: ((rows[0],), ())
    return _rowwise(cast, [x], [], [(D_MODEL, MXU_DTYPE)], [], name="x_cast")[0]


def _hooked(value, hook):
    token = None if hook is None else hook(value)
    return () if token is None else (token,)


def _mix_fwd(i, cur, cur_m, wl, hook=None, after=()):
    mixer = i % N_MIXERS
    if mixer == 0:
        y, s_mix = _sc_fwd(cur_m, wl["sc_w_in"], wl["sc_conv_w"], wl["sc_w_out"], f"l{i}_sc", after=after)
    elif mixer == 1:
        y, s_mix = _mla_fwd(cur_m, wl, f"l{i}_mla", after=after)
    else:
        y, s_mix = _cf_fwd(cur_m, wl, f"l{i}_cf", after=after)
    cur, cur_m, xh_mix, rs_mix = _ln_fwd(cur, y, wl["ln_mix_g"], wl["ln_mix_b"], name=f"l{i}_lnmix",
                                         after=_hooked(y, hook))
    return cur, cur_m, (s_mix, xh_mix, rs_mix)


def _ff_fwd(i, cur, cur_m, wl, hook=None, after=(), loss_target=None, act=None):
    y, s_ff = _mlp_fwd(cur_m, wl["ff_w1"], wl["ff_w2"], f"l{i}_ff", after=after, act=act)
    if loss_target is not None:
        xh_ff, rs_ff, dy, sq = _ln_fwd_loss(cur, y, wl["ln_ff_g"], wl["ln_ff_b"], loss_target, name=f"l{i}_lnff_loss",
                                            after=_hooked(y, hook))
        return (0.5 / D_MODEL) * jnp.sum(sq), dy, (s_ff, xh_ff, rs_ff)
    cur, cur_m, xh_ff, rs_ff = _ln_fwd(cur, y, wl["ln_ff_g"], wl["ln_ff_b"], name=f"l{i}_lnff", after=_hooked(y, hook))
    return cur, cur_m, (s_ff, xh_ff, rs_ff)


def _layer_fwd(i, cur, cur_m, wl):
    cur, cur_m, s_mix = _mix_fwd(i, cur, cur_m, wl)
    cur, cur_m, s_ff = _ff_fwd(i, cur, cur_m, wl)
    return cur, cur_m, (*s_mix, *s_ff)


def _loss_and_grad(out, target):
    d = D_MODEL

    def loss_fn(rows, bc):
        err = rows[0] - rows[1]
        return (err * (1.0 / d),), (_colsum(err * err),)

    dy, sq = _rowwise(loss_fn, [out, target], [], [(d, F32)], [(1, d)], name="loss")
    return (0.5 / d) * jnp.sum(sq), dy


def _ff_bwd(i, terms, saved, wl, hook=None, after=()):
    s_ff, xh_ff, rs_ff = saved
    grads = {}
    dz, dz_m, grads["ln_ff_g"], grads["ln_ff_b"], _ = _ln_bwd(terms, xh_ff, rs_ff, wl["ln_ff_g"], name=f"l{i}_dlnff",
                                                              after=after)
    dxs, g_ff = _mlp_bwd(dz_m, s_ff, wl["ff_w1"], wl["ff_w2"], f"l{i}_ff", dz, after=_hooked(dz, hook))
    grads.update(g_ff)
    return [(1.0, v) for v in dxs], grads


def _mix_bwd(i, terms, saved, wl, hook=None, after=()):
    mixer = i % N_MIXERS
    s_mix, xh_mix, rs_mix = saved
    grads = {}
    dz, dz_m, grads["ln_mix_g"], grads["ln_mix_b"], dz_sum = _ln_bwd(terms, xh_mix, rs_mix, wl["ln_mix_g"],
                                                                     name=f"l{i}_dlnmix", after=after)
    behind = _hooked(dz, hook)
    if mixer == 0:
        dxs, g_mix = _sc_bwd(dz_m, s_mix, wl["sc_w_in"], wl["sc_conv_w"], wl["sc_w_out"], f"l{i}_sc", dz, after=behind)
    elif mixer == 1:
        dxs, g_mix = _mla_bwd(dz_m, s_mix, wl, f"l{i}_mla", dz, after=behind)
    else:
        dxs, g_mix = _cf_bwd(dz_m, dz_sum, s_mix, wl, f"l{i}_cf", dz, after=behind)
    grads.update(g_mix)
    return [(1.0, v) for v in dxs], grads


def _layer_bwd(i, terms, saved, wl):
    terms, g_ff = _ff_bwd(i, terms, saved[3:], wl)
    terms, g_mix = _mix_bwd(i, terms, saved[:3], wl)
    return terms, {**g_ff, **g_mix}


def _sum_terms(terms, *, name):
    coefs = [c for c, _ in terms]

    def sum_fn(rows, bc):
        acc = None
        for c, v in zip(coefs, rows):
            tv = v if c == 1.0 else c * v
            acc = tv if acc is None else acc + tv
        return (acc,), ()

    return _rowwise(sum_fn, [v for _, v in terms], [], [(D_MODEL, F32)], [], name=name)[0]


def _local_step(x, target, layers):
    cur, cur_m = x, _cast_input(x)
    saved = []
    for i in range(DEPTH):
        cur, cur_m, s = _layer_fwd(i, cur, cur_m, layers[i])
        saved.append(s)
    loss, dy = _loss_and_grad(cur, target)
    terms = [(1.0, dy)]
    grads = [None] * DEPTH
    for i in reversed(range(DEPTH)):
        terms, grads[i] = _layer_bwd(i, terms, saved[i], layers[i])
    return loss, _sum_terms(terms, name="grad_x"), grads


MESH_ID = pl.DeviceIdType.MESH
ANY = pl.BlockSpec(memory_space=pl.ANY)


def _block_id(px, py, pc):
    return 4 * px + 2 * py + pc


COPIES_PER_GATHER = 7


def _all_gather(shards, *, name):
    n = len(shards)

    def body(*refs):
        x_refs, out_refs = refs[:n], refs[n:2 * n]
        send_sems, recv_sems, local_sems = refs[2 * n:]
        x, y, core = lax.axis_index("x"), lax.axis_index("y"), lax.axis_index("c")
        me, sibling = (x, y, core), (x, y, 1 - core)
        chips = [(1 - x, y), (x, 1 - y), (1 - x, 1 - y)]

        def copy(t, k, block, to, from_input=False):
            dst = out_refs[t].at[_block_id(*block)]
            return pltpu.make_async_remote_copy(
                src_ref=x_refs[t] if from_input else dst, dst_ref=dst,
                send_sem=send_sems.at[t * COPIES_PER_GATHER + k], recv_sem=recv_sems.at[t * COPIES_PER_GATHER + k],
                device_id=to, device_id_type=MESH_ID)

        started = []
        mine = [pltpu.make_async_copy(x_refs[t], out_refs[t].at[_block_id(*me)], local_sems.at[t]) for t in range(n)]
        for t in range(n):
            mine[t].start()
            first = [copy(t, 0, me, sibling, from_input=True)]
            first += [copy(t, 1 + j, me, (*chip, core), from_input=True) for j, chip in enumerate(chips)]
            for cp in first:
                cp.start()
            started += first
        for t in range(n):
            for j, chip in enumerate(chips):
                copy(t, 1 + j, (*chip, core), me).wait_recv()
                passed = copy(t, 4 + j, (*chip, core), sibling)
                passed.start()
                started.append(passed)
        for t in range(n):
            copy(t, 0, sibling, me).wait_recv()
            for j, chip in enumerate(chips):
                copy(t, 4 + j, (*chip, 1 - core), me).wait_recv()
        for cp in started:
            cp.wait_send()
        for cp in mine:
            cp.wait()

    n_sems = n * COPIES_PER_GATHER
    outs = pl.pallas_call(
        body,
        name=name,
        out_shape=[jax.ShapeDtypeStruct((N_DEV, *s.shape), s.dtype) for s in shards],
        in_specs=[ANY] * n,
        out_specs=[ANY] * n,
        scratch_shapes=[pltpu.SemaphoreType.DMA((n_sems,)), pltpu.SemaphoreType.DMA((n_sems,)),
                        pltpu.SemaphoreType.DMA((n,))],
    )(*shards)
    return list(outs)


def _swap_with_sibling(parts, *, name):
    n = len(parts)

    def body(*refs):
        p_refs, got_refs = refs[:n], refs[n:2 * n]
        send_sems, recv_sems = refs[2 * n:]
        x, y, core = lax.axis_index("x"), lax.axis_index("y"), lax.axis_index("c")
        chips = [(x, y), (1 - x, y), (x, 1 - y), (1 - x, 1 - y)]
        copies = []
        for t in range(n):
            for k, (px, py) in enumerate(chips):
                copies.append(pltpu.make_async_remote_copy(
                    src_ref=p_refs[t].at[_block_id(px, py, 1 - core)], dst_ref=got_refs[t].at[k],
                    send_sem=send_sems.at[4 * t + k], recv_sem=recv_sems.at[4 * t + k], device_id=(x, y, 1 - core),
                    device_id_type=MESH_ID))
        for cp in copies:
            cp.start()
        for cp in copies:
            cp.wait()

    outs = pl.pallas_call(
        body,
        name=name,
        out_shape=[jax.ShapeDtypeStruct((4, *p.shape[1:]), p.dtype) for p in parts],
        in_specs=[ANY] * n,
        out_specs=[ANY] * n,
        scratch_shapes=[pltpu.SemaphoreType.DMA((4 * n,)), pltpu.SemaphoreType.DMA((4 * n,))],
    )(*parts)
    return list(outs)


def _swap_between_chips(parts, *, name):
    n = len(parts)

    def body(*refs):
        p_refs, got_refs = refs[:n], refs[n:2 * n]
        send_sems, recv_sems = refs[2 * n:]
        x, y, core = lax.axis_index("x"), lax.axis_index("y"), lax.axis_index("c")
        chips = [(1 - x, y), (x, 1 - y), (1 - x, 1 - y)]
        copies = []
        for t in range(n):
            for j, (px, py) in enumerate(chips):
                copies.append(pltpu.make_async_remote_copy(
                    src_ref=p_refs[t].at[j], dst_ref=got_refs[t].at[j], send_sem=send_sems.at[3 * t + j],
                    recv_sem=recv_sems.at[3 * t + j], device_id=(px, py, core), device_id_type=MESH_ID))
        for cp in copies:
            cp.start()
        for cp in copies:
            cp.wait()

    outs = pl.pallas_call(
        body,
        name=name,
        out_shape=[jax.ShapeDtypeStruct(p.shape, p.dtype) for p in parts],
        in_specs=[ANY] * n,
        out_specs=[ANY] * n,
        scratch_shapes=[pltpu.SemaphoreType.DMA((3 * n,)), pltpu.SemaphoreType.DMA((3 * n,))],
    )(*parts)
    return list(outs)


def _chip_partials(parts, got, *, name):
    _, r, c = parts.shape
    tb = _row_tile(r, 512)
    x, y, core = lax.axis_index("x"), lax.axis_index("y"), lax.axis_index("c")
    chips = [(x, y), (1 - x, y), (x, 1 - y), (1 - x, 1 - y)]
    ids = jnp.stack([_block_id(px, py, core) for px, py in chips]).astype(jnp.int32)

    def body(ids_ref, p0, p1, p2, p3, g_ref, own_ref, out_ref):
        own_ref[...] = p0[...].astype(F32) + g_ref[0].astype(F32)
        for j, p in enumerate((p1, p2, p3)):
            out_ref[j] = (p[...].astype(F32) + g_ref[j + 1].astype(F32)).astype(out_ref.dtype)

    def part_spec(k):
        return pl.BlockSpec((None, tb, c), lambda i, ids_ref: (ids_ref[k], i, 0))

    grid_spec = pltpu.PrefetchScalarGridSpec(
        num_scalar_prefetch=1,
        grid=(r // tb,),
        in_specs=[part_spec(k) for k in range(4)] + [pl.BlockSpec((4, tb, c), lambda i, ids_ref: (0, i, 0))],
        out_specs=[pl.BlockSpec((tb, c), lambda i, ids_ref: (i, 0)), pl.BlockSpec((3, tb, c), lambda i, ids_ref: (0, i, 0))],
    )
    return pl.pallas_call(
        body,
        name=name,
        grid_spec=grid_spec,
        out_shape=[jax.ShapeDtypeStruct((r, c), F32), jax.ShapeDtypeStruct((3, r, c), parts.dtype)],
        compiler_params=_params(("parallel",)),
    )(ids, parts, parts, parts, parts, got)


def _chip_partials_group(parts, got, *, name):
    n = len(parts)
    rows = [p.shape[1] for p in parts]
    steps = 4 if all(r % (4 * BF16_ROWS) == 0 for r in rows) else (2 if all(r % (2 * BF16_ROWS) == 0 for r in rows) else 1)
    x, y, core = lax.axis_index("x"), lax.axis_index("y"), lax.axis_index("c")
    chips = [(x, y), (1 - x, y), (x, 1 - y), (1 - x, 1 - y)]
    ids = jnp.stack([_block_id(px, py, core) for px, py in chips]).astype(jnp.int32)

    def body(ids_ref, *refs):
        ins, outs = refs[:5 * n], refs[5 * n:]
        for t in range(n):
            p0, p1, p2, p3, g_ref = ins[5 * t:5 * t + 5]
            own_ref, out_ref = outs[2 * t], outs[2 * t + 1]
            own_ref[...] = p0[...].astype(F32) + g_ref[0].astype(F32)
            for j, p in enumerate((p1, p2, p3)):
                out_ref[j] = (p[...].astype(F32) + g_ref[j + 1].astype(F32)).astype(out_ref.dtype)

    in_specs, out_specs, out_shape, operands = [], [], [], []
    for p, g in zip(parts, got):
        _, r, c = p.shape
        tb = r // steps
        in_specs += [pl.BlockSpec((None, tb, c), lambda i, ids_ref, k=k: (ids_ref[k], i, 0)) for k in range(4)]
        in_specs.append(pl.BlockSpec((4, tb, c), lambda i, ids_ref: (0, i, 0)))
        out_specs += [pl.BlockSpec((tb, c), lambda i, ids_ref: (i, 0)), pl.BlockSpec((3, tb, c), lambda i, ids_ref: (0, i, 0))]
        out_shape += [jax.ShapeDtypeStruct((r, c), F32), jax.ShapeDtypeStruct((3, r, c), p.dtype)]
        operands += [p, p, p, p, g]
    outs = pl.pallas_call(
        body,
        name=name,
        grid_spec=pltpu.PrefetchScalarGridSpec(num_scalar_prefetch=1, grid=(steps,), in_specs=in_specs,
                                               out_specs=out_specs),
        out_shape=out_shape,
        compiler_params=_params(("parallel",)),
    )(ids, *operands)
    return [(outs[2 * t], outs[2 * t + 1]) for t in range(n)]


def _sum_blocks(first, blocks, *, name):
    n, r, c = blocks.shape
    tb = _row_tile(r, 512)

    def body(*refs):
        b_ref, o_ref = refs[-2], refs[-1]
        acc = refs[0][...] if first is not None else b_ref[0].astype(F32)
        for j in range(0 if first is not None else 1, n):
            acc = acc + b_ref[j].astype(F32)
        o_ref[...] = acc

    row_spec = pl.BlockSpec((tb, c), lambda i: (i, 0))
    return pl.pallas_call(
        body,
        name=name,
        grid=(r // tb,),
        in_specs=([row_spec] if first is not None else []) + [pl.BlockSpec((n, tb, c), lambda i: (0, i, 0))],
        out_specs=row_spec,
        out_shape=jax.ShapeDtypeStruct((r, c), F32),
        compiler_params=_params(("parallel",)),
    )(*([first] if first is not None else []), blocks)


def _reduce_scatter(parts, *, tag):
    got = _swap_with_sibling(parts, name=f"{tag}_sibling")
    sums = [_chip_partials(p, g, name=f"{tag}_chipsum{t}") for t, (p, g) in enumerate(zip(parts, got))]
    arrived = _swap_between_chips([others for _, others in sums], name=f"{tag}_chips")
    return [(own, arr) for (own, _), arr in zip(sums, arrived)]


HBM = pl.BlockSpec(memory_space=pltpu.HBM)
SEM = pl.BlockSpec(memory_space=pltpu.SEMAPHORE)
DATAFLOW = pltpu.SideEffectType.DATAFLOW_SIDE_EFFECTING


def _tie(value, *before):
    return lax.optimization_barrier((value, *before))[0]


def _split_call(name, arrays, sems, n_new, body, after=None):
    n_a, n_s = len(arrays), len(sems)
    after = [] if after is None else (list(after) if isinstance(after, (list, tuple)) else [after])
    n_after = len(after)

    def kernel_body(*refs):
        new = refs[n_a + n_s + n_after:n_a + n_s + n_after + 2] if n_new else (None, None)
        body(refs[:n_a], refs[n_a:n_a + n_s], *new)
        refs[-1][...] = jnp.zeros_like(refs[-1])

    n_sem_out = 2 if n_new else 0
    out_shape = [pltpu.SemaphoreType.DMA((n_new,))] * n_sem_out
    out_shape += [pltpu.HBM(a.shape, a.dtype) for a in arrays]
    out_shape += [jax.ShapeDtypeStruct((SUBLANES, LANES), F32)]
    outs = pl.pallas_call(
        kernel_body,
        name=name,
        out_shape=out_shape,
        in_specs=[HBM] * n_a + [SEM] * n_s + [ANY] * n_after,
        out_specs=[SEM] * n_sem_out + [HBM] * n_a + [pl.BlockSpec(memory_space=pltpu.VMEM)],
        input_output_aliases={i: n_sem_out + i for i in range(n_a)},
        compiler_params=pltpu.CompilerParams(has_side_effects=DATAFLOW),
    )(*[pltpu.with_memory_space_constraint(a, pltpu.HBM) for a in arrays], *sems, *after)
    return tuple(outs[:n_sem_out]), list(outs[n_sem_out:n_sem_out + n_a]), outs[-1]


def _remote(src, dst, send, recv, k, to):
    return pltpu.make_async_remote_copy(src_ref=src, dst_ref=dst, send_sem=send.at[k], recv_sem=recv.at[k],
                                        device_id=to, device_id_type=MESH_ID)


def _land_block(land_ref, block, by_cols, width):
    if by_cols:
        return land_ref.at[:, pl.ds(pl.multiple_of(block * width, LANES), width)]
    return land_ref.at[block]


def _place_in_window(shard, *, name):
    r, c = shard.shape
    tb = _row_tile(r, 512)
    me = _block_id(lax.axis_index("x"), lax.axis_index("y"), lax.axis_index("c")).astype(jnp.int32).reshape(1)

    def body(me_ref, s_ref, o_ref, token_ref):
        o_ref[...] = s_ref[...].astype(o_ref.dtype)
        token_ref[...] = jnp.zeros_like(token_ref)

    return pl.pallas_call(
        body,
        name=name,
        grid_spec=pltpu.PrefetchScalarGridSpec(
            num_scalar_prefetch=1, grid=(r // tb,),
            in_specs=[pl.BlockSpec((tb, c), lambda i, me_ref: (i, 0))],
            out_specs=[pl.BlockSpec((tb, c), lambda i, me_ref: (i, me_ref[0])),
                       pl.BlockSpec((SUBLANES, LANES), lambda i, me_ref: (0, 0))]),
        out_shape=[jax.ShapeDtypeStruct((r, N_DEV * c), MXU_DTYPE), jax.ShapeDtypeStruct((SUBLANES, LANES), F32)],
        compiler_params=_params(("arbitrary",)),
    )(me, shard)


def _gather_prepare(shards, by_cols, *, name):
    me = _block_id(lax.axis_index("x"), lax.axis_index("y"), lax.axis_index("c"))
    lands, early = [], []
    for i, (s, cols) in enumerate(zip(shards, by_cols)):
        if cols:
            land, token = _place_in_window(s, name=f"{name}_place{i}")
            lands.append(land)
            early.append(token)
        else:
            lands.append(lax.dynamic_update_index_in_dim(lax.empty((N_DEV, *s.shape), s.dtype), s, me, 0))
            early.append(s)
    return lands, [s.shape[-1] for s in shards], list(by_cols), early


def _gather_start(shards, *, name, after=None, by_cols=None, prepared=None):
    if prepared is None:
        prepared = _gather_prepare(shards, by_cols if by_cols is not None else [False] * len(shards), name=name)
    lands, widths, by_cols, _ = prepared
    n = len(lands)

    def body(refs, _, send, recv):
        x, y, core = lax.axis_index("x"), lax.axis_index("y"), lax.axis_index("c")
        targets = [(x, y, 1 - core), (1 - x, y, core), (x, 1 - y, core), (1 - x, 1 - y, core)]
        for t in range(n):
            mine = _land_block(refs[t], _block_id(x, y, core), by_cols[t], widths[t])
            for k, to in enumerate(targets):
                _remote(mine, mine, send, recv, 4 * t + k, to).start()

    sems, thru, token = _split_call(name, lands, [], 4 * n, body, after=after)
    return {"n": n, "sems1": sems, "arrays": thru, "token": token, "by_cols": by_cols, "widths": widths}


def _gather_pass_on(st, *, name, after):
    n = st["n"]

    def body(refs, sems, send, recv):
        send1, recv1 = sems
        x, y, core = lax.axis_index("x"), lax.axis_index("y"), lax.axis_index("c")
        chips = [(1 - x, y), (x, 1 - y), (1 - x, 1 - y)]
        for t in range(n):
            for j, (px, py) in enumerate(chips):
                block = _land_block(refs[t], _block_id(px, py, core), st["by_cols"][t], st["widths"][t])
                _remote(block, block, send1, recv1, 4 * t + 1 + j, (x, y, 1 - core)).wait_recv()
                _remote(block, block, send, recv, 3 * t + j, (x, y, 1 - core)).start()

    sems2, thru, token = _split_call(name, st["arrays"], list(st["sems1"]), 3 * n, body, after=after)
    return {**st, "sems2": sems2, "arrays": thru, "token": token}


def _gather_finish(st, *, name, after, then_start=None):
    n = st["n"]
    n2 = 0 if then_start is None else len(then_start[0])

    def body(refs, sems, new_send=None, new_recv=None):
        send1, recv1, send2, recv2 = sems
        x, y, core = lax.axis_index("x"), lax.axis_index("y"), lax.axis_index("c")
        sibling = (x, y, 1 - core)
        chips = [(1 - x, y), (x, 1 - y), (1 - x, 1 - y)]
        for t in range(n):
            place = lambda px, py, pc, t=t: _land_block(refs[t], _block_id(px, py, pc), st["by_cols"][t],
                                                        st["widths"][t])
            mine = place(x, y, core)
            for k in range(4):
                _remote(mine, mine, send1, recv1, 4 * t + k, sibling).wait_send()
            _remote(mine, place(x, y, 1 - core), send1, recv1, 4 * t, sibling).wait_recv()
            for j, (px, py) in enumerate(chips):
                _remote(place(px, py, core), place(px, py, core), send2, recv2, 3 * t + j, sibling).wait_send()
                _remote(mine, place(px, py, 1 - core), send2, recv2, 3 * t + j, sibling).wait_recv()
        if n2:
            _, widths, by_cols, _ = then_start
            targets = [sibling, (1 - x, y, core), (x, 1 - y, core), (1 - x, 1 - y, core)]
            for t in range(n2):
                mine = _land_block(refs[n + t], _block_id(x, y, core), by_cols[t], widths[t])
                for k, to in enumerate(targets):
                    _remote(mine, mine, new_send, new_recv, 4 * t + k, to).start()

    arrays = st["arrays"] + (list(then_start[0]) if n2 else [])
    sems, thru, token = _split_call(name, arrays, [*st["sems1"], *st["sems2"]], 4 * n2, body, after=after)
    if not n2:
        return thru
    _, widths, by_cols, _ = then_start
    return thru[:n], {"n": n2, "sems1": sems, "arrays": thru[n:], "token": token, "by_cols": by_cols, "widths": widths}


def _swap_start(sources, lands, plan, per_array, *, name):
    n = len(sources)

    def body(refs, _, send, recv):
        x, y, core = lax.axis_index("x"), lax.axis_index("y"), lax.axis_index("c")
        for t in range(n):
            for k in range(per_array):
                block, to = plan(x, y, core, k)
                _remote(refs[t].at[block], refs[n + t].at[k], send, recv, per_array * t + k, to).start()

    sems, thru, token = _split_call(name, sources + lands, [], per_array * n, body)
    return {"n": n, "sems": sems, "arrays": thru, "token": token, "plan": plan, "per_array": per_array}


def _swap_finish(states, *, name, after):
    many = isinstance(states, (list, tuple))
    states = list(states) if many else [states]

    def body(refs, sems, *_):
        x, y, core = lax.axis_index("x"), lax.axis_index("y"), lax.axis_index("c")
        first = 0
        for j, st in enumerate(states):
            n, plan, per_array = st["n"], st["plan"], st["per_array"]
            send, recv = sems[2 * j], sems[2 * j + 1]
            for t in range(n):
                for k in range(per_array):
                    block, to = plan(x, y, core, k)
                    cp = _remote(refs[first + t].at[block], refs[first + n + t].at[k], send, recv, per_array * t + k, to)
                    cp.wait_send()
                    cp.wait_recv()
            first += 2 * n

    arrays = [a for st in states for a in st["arrays"]]
    _, thru, _ = _split_call(name, arrays, [s for st in states for s in st["sems"]], 0, body, after=after)
    out, first = [], 0
    for st in states:
        n = st["n"]
        out.append((thru[first:first + n], thru[first + n:first + 2 * n]))
        first += 2 * n
    return out if many else out[0]


def _to_sibling(x, y, core, k):
    px, py = [(x, y), (1 - x, y), (x, 1 - y), (1 - x, 1 - y)][k]
    return _block_id(px, py, 1 - core), (x, y, 1 - core)


def _to_chip(x, y, core, k):
    px, py = [(1 - x, y), (x, 1 - y), (1 - x, 1 - y)][k]
    return k, (px, py, core)


PARAMS = {
    "sc_w_in": ((2, 1024, 3072), 2), "sc_conv_w": ((2, 3, 1024), 2), "sc_w_out": ((2, 1024, 1024), 1),
    "mla_w_dq": ((1, 1024, 384), 1), "mla_g_q": ((1, 384), None), "mla_w_uq": ((1, 384, 1536), 2),
    "mla_w_dkv": ((1, 1024, 320), 1), "mla_g_kv": ((1, 256), None), "mla_w_uk": ((1, 256, 8, 128), 1),
    "mla_w_uv": ((1, 256, 8, 128), 1), "mla_w_o": ((1, 1024, 1024), 1), "cf_w_pw1": ((1, 1024, 2048), 2),
    "cf_b_pw1": ((1, 2048), 1), "cf_dw_w": ((1, 31, 1024), 2), "cf_dw_b": ((1, 1024), 1), "cf_norm_g": ((1, 1024), 1),
    "cf_norm_b": ((1, 1024), 1), "cf_w_pw2": ((1, 1024, 1024), 1), "cf_b_pw2": ((1, 1024), 1),
    "ff_w1": ((4, 1024, 4096), 2), "ff_w2": ((4, 4096, 1024), 1), "ln_mix_g": ((4, 1024), None),
    "ln_mix_b": ((4, 1024), None), "ln_ff_g": ((4, 1024), None), "ln_ff_b": ((4, 1024), None),
}
NAMES = list(PARAMS)
BIG = ["sc_w_in", "sc_w_out", "mla_w_dq", "mla_w_uq", "mla_w_dkv", "mla_w_uk", "mla_w_uv", "mla_w_o", "cf_w_pw1",
       "cf_w_pw2", "ff_w1", "ff_w2"]
SMALL = [n for n in NAMES if n not in BIG]
SMALL_SHARDED = [n for n in SMALL if PARAMS[n][1] is not None]
SMALL_WIDTH = LANES
DEV_BLOCKED = ("sc_w_in", "cf_w_pw1", "ff_w1")
MIXER_BIG = {0: ["sc_w_in", "sc_w_out"], 1: ["mla_w_dq", "mla_w_uq", "mla_w_dkv", "mla_w_uk", "mla_w_uv", "mla_w_o"],
             2: ["cf_w_pw1", "cf_w_pw2"]}


def _layer_big(i):
    return [(n, i // N_MIXERS) for n in MIXER_BIG[i % N_MIXERS]] + [("ff_w1", i), ("ff_w2", i)]


def _shard_2d(name, a):
    return a.reshape(a.shape[0], -1)


def _shard_for_gather(name, a):
    flat = _shard_2d(name, a)
    return flat if name in DEV_BLOCKED else flat.astype(MXU_DTYPE)


def _as_operand(name, g):
    if name in DEV_BLOCKED:
        return g
    if name == "mla_w_uq":
        by_head = g.transpose(1, 0, 2)
        pieces = [by_head[:, :, :QK_NOPE], by_head[:, :, QK_NOPE:]]
        return jnp.concatenate([p.reshape(g.shape[1], -1) for p in pieces], axis=1)
    return g.reshape(-1, g.shape[2])


def _as_parts(name, grad):
    if name in DEV_BLOCKED:
        return grad
    if name == "mla_w_uq":
        rows = grad.shape[0]
        pieces = [grad[:, :Q_NOPE_COLS].reshape(rows, MLA_HEADS, QK_NOPE),
                  grad[:, Q_NOPE_COLS:].reshape(rows, MLA_HEADS, QK_ROPE)]
        return jnp.concatenate(pieces, axis=2).transpose(1, 0, 2)
    return grad.reshape(N_DEV, -1, grad.shape[1])


def _shard_shape(name):
    shape, ax = PARAMS[name]
    if ax is None:
        return shape
    return tuple(s // N_DEV if i == ax else s for i, s in enumerate(shape))


def _rows_of(shape, width, align):
    n = 1
    for s in shape:
        n *= s
    rows = -(-n // width)
    return -(-rows // align) * align


def _pack_rows(arrays, width, align, dtype, lead=0):
    segs = []
    for a in arrays:
        batch = a.shape[:lead]
        flat = a.astype(dtype).reshape(*batch, -1)
        rows = _rows_of(a.shape[lead:], width, align)
        flat = jnp.pad(flat, [(0, 0)] * lead + [(0, rows * width - flat.shape[-1])])
        segs.append(flat.reshape(*batch, rows, width))
    return jnp.concatenate(segs, axis=lead)


def _unpack_rows(buf, shapes, width, align):
    out, off = [], 0
    lead = buf.shape[:-2]
    for shape in shapes:
        n = 1
        for s in shape:
            n *= s
        rows = _rows_of(shape, width, align)
        seg = buf[..., off:off + rows, :].reshape(*lead, rows * width)[..., :n]
        out.append(seg.reshape(*lead, *shape))
        off += rows
    return out


def _merge_shards(stacked, ax):
    moved = jnp.moveaxis(stacked, 0, ax)
    shape = moved.shape
    return moved.reshape(*shape[:ax], shape[ax] * shape[ax + 1], *shape[ax + 2:])


def _split_shards(full, ax):
    shape = full.shape
    split = full.reshape(*shape[:ax], N_DEV, shape[ax] // N_DEV, *shape[ax + 1:])
    return jnp.moveaxis(split, ax, 0)


def _adam_math(wv, gv, mv, vv):
    m_new = ADAM_B1 * mv + (1.0 - ADAM_B1) * gv
    v_new = ADAM_B2 * vv + (1.0 - ADAM_B2) * (gv * gv)
    m_hat = m_new / (1.0 - ADAM_B1 ** ADAM_STEP)
    v_hat = v_new / (1.0 - ADAM_B2 ** ADAM_STEP)
    delta = -ADAM_LR * (m_hat / (jnp.sqrt(v_hat) + ADAM_EPS) + ADAM_WD * wv)
    return delta, m_new, v_new


def _adamw(w, g, m, v, *, name):
    shape = w.shape
    c = shape[-1]

    def fn(rows, bc):
        return _adam_math(*rows), ()

    flat = [a.reshape(-1, c) for a in (w, g, m, v)]
    outs = _rowwise(fn, flat, [], [(c, F32)] * 3, [], name=name, tb_cap=512)
    return tuple(o.reshape(shape) for o in outs)


def _adamw_reduced(w, m, v, layer_grads, *, name, after=()):
    n_l, r, c = w.shape
    tb = _row_tile(r, 256 if c > 512 else 512)
    nb = r // tb

    def body(*refs):
        w_ref, m_ref, v_ref = refs[:3]
        g_refs = refs[3:3 + 2 * n_l]
        g_out, d_out, m_out, v_out = refs[3 + 2 * n_l + len(after):]
        layer = pl.program_id(0)
        for l in range(n_l):
            @pl.when(layer == l)
            def _(l=l):
                own, arr = g_refs[2 * l], g_refs[2 * l + 1]
                g = own[...] + arr[0].astype(F32)
                g = g + arr[1].astype(F32)
                g = g + arr[2].astype(F32)
                delta, m_new, v_new = _adam_math(w_ref[...], g, m_ref[...], v_ref[...])
                g_out[...] = g
                d_out[...] = delta
                m_out[...] = m_new
                v_out[...] = v_new

    stacked = pl.BlockSpec((tb, c), lambda l, i: (l * nb + i, 0))
    in_specs = [stacked] * 3
    operands = [a.reshape(n_l * r, c) for a in (w, m, v)]
    for l, (own, arr) in enumerate(layer_grads):
        in_specs.append(pl.BlockSpec((tb, c), lambda ll, i, l=l: (jnp.where(ll == l, i, 0), 0)))
        in_specs.append(pl.BlockSpec((3, tb, c), lambda ll, i, l=l: (0, jnp.where(ll == l, i, 0), 0)))
        operands += [own, arr]
    in_specs += [ANY] * len(after)
    operands += list(after)
    outs = pl.pallas_call(
        body,
        name=name,
        grid=(n_l, nb),
        in_specs=in_specs,
        out_specs=[stacked] * 4,
        out_shape=[jax.ShapeDtypeStruct((n_l * r, c), F32)] * 4,
        compiler_params=_params(("arbitrary", "arbitrary")),
    )(*operands)
    return tuple(o.reshape(n_l, r, c) for o in outs)


def _train_step(x, target, weights, m_state, v_state):
    me = _block_id(lax.axis_index("x"), lax.axis_index("y"), lax.axis_index("c"))

    groups = [(kind, i) for i in range(DEPTH) for kind in ("mix", "ff")]
    forward = [groups[0], ("ff_up", 0), ("ff_down", 0)] + groups[2:]

    def group_names(g):
        kind, i = g
        if kind == "mix":
            return [(n, i // N_MIXERS) for n in MIXER_BIG[i % N_MIXERS]]
        return [(n, i) for n in {"ff": ["ff_w1", "ff_w2"], "ff_up": ["ff_w1"], "ff_down": ["ff_w2"]}[kind]]

    def shards_of(g):
        return [_shard_for_gather(n, weights[n][j]) for n, j in group_names(g)]

    whole = lambda g: [n in DEV_BLOCKED for n, _ in group_names(g)]

    small_shapes = [_shard_shape(n) for n in SMALL_SHARDED]
    gather = {0: _gather_start([_pack_rows([weights[n] for n in SMALL_SHARDED], SMALL_WIDTH, SUBLANES, F32)]
                               + shards_of(groups[0]), name="gather_g0_start", by_cols=[False] + whole(groups[0]))}
    cur, cur_m = x[0], x[0]
    prepared = {k: _gather_prepare(shards_of(forward[k]), whole(forward[k]), name=f"gather_g{k}_start")
                for k in range(1, len(forward))}
    early = [a for k in prepared for a in prepared[k][3]]
    gather[0] = _gather_pass_on(gather[0], name="gather_g0_pass", after=early)
    first = _gather_finish(gather[0], name="gather_g0_finish", after=gather[0]["token"])
    small_all, first_matrices = first[0], first[1:]
    small_full = {n: weights[n] for n in SMALL if PARAMS[n][1] is None}
    for n, stacked in zip(SMALL_SHARDED, _unpack_rows(small_all, small_shapes, SMALL_WIDTH, SUBLANES)):
        small_full[n] = _merge_shards(stacked, PARAMS[n][1])

    def small_weights(i):
        wl = {}
        for n in SMALL:
            if n.startswith("ln_"):
                wl[n] = small_full[n][i:i + 1]
            elif n.startswith(("sc_", "mla_", "cf_")[i % N_MIXERS]):
                v = small_full[n][i // N_MIXERS]
                wl[n] = v if v.ndim == 2 else v.reshape(1, -1)
        return wl

    layers = [small_weights(i) for i in range(DEPTH)]

    def install(g, gathered):
        for (n, _), a in zip(group_names(g), gathered):
            layers[g[1]][n] = _as_operand(n, a)

    install(forward[0], first_matrices)
    gather[1] = _gather_start(None, name="gather_g1_start", after=gather[0]["token"], prepared=prepared[1])
    gather[2] = _gather_start(None, name="gather_g2_start", after=gather[1]["token"], prepared=prepared[2])
    saved = {}
    hidden = None
    for k, g in enumerate(forward):
        kind, i = g
        tokens = [gather[1]["token"]] if k == 0 else []
        if k + 2 < len(forward):
            tokens.append(gather[k + 2]["token"])

        def pass_on(y, k=k):
            if k + 1 == len(forward):
                return None
            gather[k + 1] = _gather_pass_on(gather[k + 1], name=f"gather_g{k + 1}_pass", after=y)
            return gather[k + 1]["token"]

        if k + 1 == len(forward):
            loss_local, dy, saved[g] = _ff_fwd(i, cur, cur_m, layers[i], after=tokens, loss_target=target[0])
            break
        if kind == "ff_up":
            hidden = _mlp_up(cur_m, layers[i]["ff_w1"], f"l{i}_ff", after=tokens)
            behind, done = _hooked(hidden, pass_on), hidden
        elif kind == "ff_down":
            cur, cur_m, saved[("ff", i)] = _ff_fwd(i, cur, cur_m, layers[i], hook=pass_on, after=[*tokens, *behind],
                                                   act=hidden)
            done = cur
        else:
            phase = _mix_fwd if kind == "mix" else _ff_fwd
            cur, cur_m, saved[g] = phase(i, cur, cur_m, layers[i], hook=pass_on, after=tokens)
            done = cur
        if k + 3 < len(forward):
            gathered, gather[k + 3] = _gather_finish(gather[k + 1], name=f"gather_g{k + 1}_finish", after=done,
                                                     then_start=prepared[k + 3])
        else:
            gathered = _gather_finish(gather[k + 1], name=f"gather_g{k + 1}_finish", after=done)
        install(forward[k + 1], gathered)

    loss = lax.psum(loss_local, MESH_AXES)

    reduced = {n: [None] * PARAMS[n][0][0] for n in BIG}
    small_grads = {n: [None] * PARAMS[n][0][0] for n in SMALL}
    reduce = {}

    def to_chips(p, after):
        st = reduce[p]
        parts, got = _swap_finish(st["sibling"], name=f"reduce_p{p}_sibling_finish", after=after)
        sums = _chip_partials_group(parts, got, name=f"reduce_p{p}_chipsum")
        st["own"] = [own for own, _ in sums]
        others = [o for _, o in sums]
        st["chips"] = _swap_start(others, [lax.empty(o.shape, o.dtype) for o in others], _to_chip, 3,
                                  name=f"reduce_p{p}_chips_start")
        return st["chips"]["token"]

    def arrived(phases, after):
        done = _swap_finish([reduce[p]["chips"] for p in phases], name=f"reduce_p{phases[0]}_chips_finish", after=after)
        for p, (_, got) in zip(phases, done):
            for (n, j), own, arr in zip(reduce[p]["names"], reduce[p]["own"], got):
                reduced[n][j] = (own, arr)

    terms = [(1.0, dy)]
    back = list(reversed(groups))
    for p, g in enumerate(back):
        kind, i = g

        def after_first_kernel(dz, p=p):
            return to_chips(p - 1, dz) if p >= 1 else None

        phase = _mix_bwd if kind == "mix" else _ff_bwd
        tokens = [reduce[p - 1]["sibling"]["token"]] if p >= 1 else []
        terms, grads = phase(i, terms, saved[g], layers[i], hook=after_first_kernel, after=tokens)
        for n in SMALL:
            if n in grads:
                small_grads[n][i if n.startswith("ln_") else i // N_MIXERS] = grads[n]
        names = group_names(g)
        parts = [_as_parts(n, grads[n]) for n, _ in names]
        reduce[p] = {"names": names, "sibling": _swap_start(
            parts, [lax.empty((4, *a.shape[1:]), a.dtype) for a in parts], _to_sibling, 4,
            name=f"reduce_p{p}_sibling_start")}
    (_, grad_x), = terms
    last = len(back) - 1
    small_mine = _pack_rows([jnp.stack(small_grads[n]).reshape(PARAMS[n][0]) for n in SMALL], SMALL_WIDTH, SUBLANES, F32)
    small_gather = _gather_start([small_mine], name="gather_small_grads_start")
    tail = to_chips(last, small_gather["token"])
    arrived(list(range(last)), tail)

    grad, delta, new_m, new_v = {}, {}, {}, {}

    def adamw_matrix(n, after):
        shape = weights[n].shape
        view = lambda a: a.reshape(shape[0], shape[1], -1)
        outs = _adamw_reduced(view(weights[n]), view(m_state[n]), view(v_state[n]), reduced[n], name=f"adamw_{n}",
                              after=after)
        grad[n], delta[n], new_m[n], new_v[n] = [o.reshape(shape) for o in outs]
        return outs[0]

    done = tail
    for n in BIG:
        if n not in MIXER_BIG[0]:
            done = adamw_matrix(n, [done])
    small_gather = _gather_pass_on(small_gather, name="gather_small_grads_pass", after=done)
    small_all = _gather_finish(small_gather, name="gather_small_grads_finish", after=small_gather["token"])[0]
    small_sum = _sum_blocks(None, small_all, name="sum_small_grads")
    for n, g in zip(SMALL, _unpack_rows(small_sum, [PARAMS[n][0] for n in SMALL], SMALL_WIDTH, SUBLANES)):
        ax = PARAMS[n][1]
        grad[n] = g if ax is None else lax.dynamic_index_in_dim(_split_shards(g, ax), me, axis=0, keepdims=False)
    small_shards = [_shard_shape(n) for n in SMALL]
    packed = [_pack_rows([src[n] for n in SMALL], SMALL_WIDTH, SUBLANES, F32) for src in (weights, grad, m_state, v_state)]
    small_out = _adamw(*packed, name="adamw_small")
    for outs, dst in zip(small_out, (delta, new_m, new_v)):
        for n, a in zip(SMALL, _unpack_rows(outs, small_shards, SMALL_WIDTH, SUBLANES)):
            dst[n] = a
    arrived([last], small_sum)
    for n in MIXER_BIG[0]:
        adamw_matrix(n, [])
    return (loss, grad_x[None], *[grad[n] for n in NAMES], *[delta[n] for n in NAMES],
            *[new_m[n] for n in NAMES], *[new_v[n] for n in NAMES])


def kernel(x, sc_w_in, sc_conv_w, sc_w_out, mla_w_dq, mla_g_q, mla_w_uq, mla_w_dkv, mla_g_kv, mla_w_uk, mla_w_uv, mla_w_o, cf_w_pw1, cf_b_pw1, cf_dw_w, cf_dw_b, cf_norm_g, cf_norm_b, cf_w_pw2, cf_b_pw2, ff_w1, ff_w2, ln_mix_g, ln_mix_b, ln_ff_g, ln_ff_b, loss_target, m_sc_w_in, m_sc_conv_w, m_sc_w_out, m_mla_w_dq, m_mla_g_q, m_mla_w_uq, m_mla_w_dkv, m_mla_g_kv, m_mla_w_uk, m_mla_w_uv, m_mla_w_o, m_cf_w_pw1, m_cf_b_pw1, m_cf_dw_w, m_cf_dw_b, m_cf_norm_g, m_cf_norm_b, m_cf_w_pw2, m_cf_b_pw2, m_ff_w1, m_ff_w2, m_ln_mix_g, m_ln_mix_b, m_ln_ff_g, m_ln_ff_b, v_sc_w_in, v_sc_conv_w, v_sc_w_out, v_mla_w_dq, v_mla_g_q, v_mla_w_uq, v_mla_w_dkv, v_mla_g_kv, v_mla_w_uk, v_mla_w_uv, v_mla_w_o, v_cf_w_pw1, v_cf_b_pw1, v_cf_dw_w, v_cf_dw_b, v_cf_norm_g, v_cf_norm_b, v_cf_w_pw2, v_cf_b_pw2, v_ff_w1, v_ff_w2, v_ln_mix_g, v_ln_mix_b, v_ln_ff_g, v_ln_ff_b):
    w_list = (sc_w_in, sc_conv_w, sc_w_out, mla_w_dq, mla_g_q, mla_w_uq, mla_w_dkv, mla_g_kv, mla_w_uk, mla_w_uv, mla_w_o,
              cf_w_pw1, cf_b_pw1, cf_dw_w, cf_dw_b, cf_norm_g, cf_norm_b, cf_w_pw2, cf_b_pw2, ff_w1, ff_w2, ln_mix_g,
              ln_mix_b, ln_ff_g, ln_ff_b)
    m_list = (m_sc_w_in, m_sc_conv_w, m_sc_w_out, m_mla_w_dq, m_mla_g_q, m_mla_w_uq, m_mla_w_dkv, m_mla_g_kv, m_mla_w_uk,
              m_mla_w_uv, m_mla_w_o, m_cf_w_pw1, m_cf_b_pw1, m_cf_dw_w, m_cf_dw_b, m_cf_norm_g, m_cf_norm_b, m_cf_w_pw2,
              m_cf_b_pw2, m_ff_w1, m_ff_w2, m_ln_mix_g, m_ln_mix_b, m_ln_ff_g, m_ln_ff_b)
    v_list = (v_sc_w_in, v_sc_conv_w, v_sc_w_out, v_mla_w_dq, v_mla_g_q, v_mla_w_uq, v_mla_w_dkv, v_mla_g_kv, v_mla_w_uk,
              v_mla_w_uv, v_mla_w_o, v_cf_w_pw1, v_cf_b_pw1, v_cf_dw_w, v_cf_dw_b, v_cf_norm_g, v_cf_norm_b, v_cf_w_pw2,
              v_cf_b_pw2, v_ff_w1, v_ff_w2, v_ln_mix_g, v_ln_mix_b, v_ln_ff_g, v_ln_ff_b)
    return _train_step(x, loss_target, dict(zip(NAMES, w_list)), dict(zip(NAMES, m_list)), dict(zip(NAMES, v_list)))
```

```python
import functools

import jax
import jax.numpy as jnp
from jax import lax
from jax.experimental import pallas as pl
from jax.experimental.pallas import tpu as pltpu

F32 = jnp.float32
MXU_DTYPE = jnp.bfloat16

N_DEV = 8
D_MODEL = 1024
DEPTH = 4
N_MIXERS = 3
CHUNK = 64
ALPHA = (2.0 * DEPTH) ** 0.25
LN_EPS = 1e-5
RMS_EPS = 1e-6
MLA_HEADS = 8
QK_NOPE = 128
QK_ROPE = 64
V_HEAD = 128
Q_LORA = 384
KV_LORA = 256
ROPE_THETA = 10000.0
ADAM_LR = 0.001
ADAM_B1 = 0.9
ADAM_B2 = 0.999
ADAM_EPS = 1e-08
ADAM_WD = 0.01
ADAM_STEP = 10

LANES = 128
SUBLANES = 8
BF16_ROWS = 16
VMEM_LIMIT = 56 * 1024 * 1024
MM_VMEM_BUDGET = 36 * 1024 * 1024
CONV_PAD = 32
MASK_VALUE = -1e30

MESH_AXES = ("x", "y", "c")


def _params(semantics):
    return pltpu.CompilerParams(dimension_semantics=semantics, vmem_limit_bytes=VMEM_LIMIT)


def _tile(n, cap):
    if n <= cap:
        return n
    t = cap - cap % LANES
    while t >= LANES:
        if n % t == 0:
            return t
        t -= LANES
    raise ValueError(f"no tile for {n} under {cap}")


def _row_tile(rows, cap):
    if rows <= cap:
        return rows
    t = cap - cap % SUBLANES
    while t >= SUBLANES:
        if rows % t == 0:
            return t
        t -= SUBLANES
    raise ValueError(f"no row tile for {rows} under {cap}")


def _mm(a, b, *, name, ta=False, tb=False, b_dev=False, out_dev=False, out_dtypes=(F32,), epilogue=None,
        tile_extras=(), col_extras=(), after=(), tm_cap=2048, tn_cap=1024, tk_cap=4096):
    m, k = (a.shape[1], a.shape[0]) if ta else a.shape
    if b_dev:
        nd, b_rows, ns = b.shape
        n, kb = (b_rows, nd * ns) if tb else (nd * ns, b_rows)
    else:
        n, kb = (b.shape[0], b.shape[1]) if tb else (b.shape[1], b.shape[0])
    assert k == kb, (a.shape, b.shape, ta, tb)
    if ta or tile_extras or (b_dev and tb):
        tm_cap = min(tm_cap, 1024)
    tk = ns if (b_dev and tb) else _tile(k, tk_cap)
    nk = k // tk
    fixed_tn = ns if (b_dev and not tb) else (n // N_DEV if out_dev else None)
    if out_dev:
        assert not tile_extras and not col_extras and n % N_DEV == 0
    out_bytes = sum(jnp.dtype(dt).itemsize for dt in out_dtypes) + sum(e.dtype.itemsize for e in tile_extras)
    tm, tn = None, None
    for cand_m, cand_n in ((tm_cap, tn_cap), (tm_cap, tn_cap // 2), (tm_cap // 2, tn_cap), (tm_cap // 2, tn_cap // 2),
                           (tm_cap // 4, tn_cap // 2), (tm_cap // 4, tn_cap // 4)):
        tm, tn = _tile(m, cand_m), (fixed_tn if fixed_tn is not None else _tile(n, cand_n))
        need = 2 * (tm * tk * a.dtype.itemsize + tk * tn * b.dtype.itemsize) + 2 * tm * tn * out_bytes
        need += tm * tn * 4 * (2 if nk > 1 else 1)
        if need <= MM_VMEM_BUDGET:
            break
    n_te, n_ce, n_out = len(tile_extras), len(col_extras), len(out_dtypes)
    dims = (((0 if ta else 1,), (1 if tb else 0,)), ((), ()))

    def body(*refs):
        a_ref, b_ref = refs[0], refs[1]
        te_refs = refs[2:2 + n_te]
        ce_refs = refs[2 + n_te:2 + n_te + n_ce]
        first_out = 2 + n_te + n_ce + len(after)
        out_refs = refs[first_out:first_out + n_out]

        def finish(acc):
            if epilogue is None:
                outs = (acc,)
            else:
                outs = epilogue(acc, *[r[...] for r in te_refs], *[r[...] for r in ce_refs])
            for o_ref, o in zip(out_refs, outs):
                o_ref[...] = o.astype(o_ref.dtype)

        def product():
            return lax.dot_general(a_ref[...].astype(MXU_DTYPE), b_ref[...].astype(MXU_DTYPE), dims,
                                   preferred_element_type=F32)

        if nk == 1:
            finish(product())
        else:
            acc_ref = refs[-1]
            kk = pl.program_id(2)

            @pl.when(kk == 0)
            def _():
                acc_ref[...] = jnp.zeros_like(acc_ref)

            acc_ref[...] += product()

            @pl.when(kk == nk - 1)
            def _():
                finish(acc_ref[...])

    a_spec = pl.BlockSpec((tk, tm), lambda i, j, kk: (kk, i)) if ta else pl.BlockSpec((tm, tk), lambda i, j, kk: (i, kk))
    if b_dev and tb:
        b_spec = pl.BlockSpec((None, tn, tk), lambda i, j, kk: (kk, j, 0))
    elif b_dev:
        b_spec = pl.BlockSpec((None, tk, tn), lambda i, j, kk: (j, kk, 0))
    elif tb:
        b_spec = pl.BlockSpec((tn, tk), lambda i, j, kk: (j, kk))
    else:
        b_spec = pl.BlockSpec((tk, tn), lambda i, j, kk: (kk, j))
    tile_spec = pl.BlockSpec((tm, tn), lambda i, j, kk: (i, j))
    col_spec = pl.BlockSpec((1, tn), lambda i, j, kk: (0, j))
    if out_dev:
        out_spec = pl.BlockSpec((None, tm, tn), lambda i, j, kk: (j, i, 0))
        out_shape = [jax.ShapeDtypeStruct((N_DEV, m, tn), dt) for dt in out_dtypes]
    else:
        out_spec = tile_spec
        out_shape = [jax.ShapeDtypeStruct((m, n), dt) for dt in out_dtypes]
    outs = pl.pallas_call(
        body,
        name=name,
        grid=(m // tm, n // tn, nk),
        in_specs=[a_spec, b_spec] + [tile_spec] * n_te + [col_spec] * n_ce + [ANY] * len(after),
        out_specs=[out_spec] * n_out,
        out_shape=out_shape,
        scratch_shapes=[pltpu.VMEM((tm, tn), F32)] if nk > 1 else [],
        compiler_params=_params(("parallel", "parallel", "arbitrary")),
    )(a, b, *tile_extras, *col_extras, *after)
    return outs[0] if n_out == 1 else tuple(outs)


def _rowwise(fn, rows, bcast, out_rows, out_accs, *, name, tb_cap=256, after=()):
    t = rows[0].shape[0]
    tb = _row_tile(t, tb_cap)
    n_r, n_b, n_o, n_a = len(rows), len(bcast), len(out_rows), len(out_accs)

    def body(*refs):
        r_refs = refs[:n_r]
        b_refs = refs[n_r:n_r + n_b]
        first_out = n_r + n_b + len(after)
        o_refs = refs[first_out:first_out + n_o]
        a_refs = refs[first_out + n_o:]
        outs, accs = fn([r[...] for r in r_refs], [r[...] for r in b_refs])
        for o_ref, o in zip(o_refs, outs):
            o_ref[...] = o.astype(o_ref.dtype)
        if n_a:
            i = pl.program_id(0)

            @pl.when(i == 0)
            def _():
                for a_ref, acc in zip(a_refs, accs):
                    a_ref[...] = acc

            @pl.when(i > 0)
            def _():
                for a_ref, acc in zip(a_refs, accs):
                    a_ref[...] += acc

    in_specs = [pl.BlockSpec((tb, r.shape[1]), lambda i: (i, 0)) for r in rows]
    in_specs += [pl.BlockSpec(b.shape, lambda i: (0, 0)) for b in bcast]
    in_specs += [ANY] * len(after)
    out_specs = [pl.BlockSpec((tb, c), lambda i: (i, 0)) for c, _ in out_rows]
    out_specs += [pl.BlockSpec(s, lambda i: (0, 0)) for s in out_accs]
    out_shape = [jax.ShapeDtypeStruct((t, c), dt) for c, dt in out_rows]
    out_shape += [jax.ShapeDtypeStruct(s, F32) for s in out_accs]
    outs = pl.pallas_call(
        body,
        name=name,
        grid=(t // tb,),
        in_specs=in_specs,
        out_specs=out_specs,
        out_shape=out_shape,
        compiler_params=_params(("arbitrary",)),
    )(*rows, *bcast, *after)
    return tuple(outs)


def _colsum(v):
    return jnp.sum(v, axis=0, keepdims=True)


def _sigmoid(v):
    return 1.0 / (1.0 + jnp.exp(-v))


def _ln_fwd(x_res, y_sub, g, b, *, name, after=()):
    d = x_res.shape[1]

    def fn(rows, bc):
        z = ALPHA * rows[0] + rows[1]
        mu = jnp.mean(z, axis=-1, keepdims=True)
        zc = z - mu
        var = jnp.mean(zc * zc, axis=-1, keepdims=True)
        rstd = lax.rsqrt(var + LN_EPS)
        xhat = zc * rstd
        out = xhat * bc[0] + bc[1]
        return (out, out, xhat, rstd), ()

    return _rowwise(fn, [x_res, y_sub], [g, b], [(d, F32), (d, MXU_DTYPE), (d, F32), (1, F32)], [], name=name,
                    after=after, tb_cap=512)


def _ln_fwd_loss(x_res, y_sub, g, b, target, *, name, after=()):
    d = x_res.shape[1]

    def fn(rows, bc):
        z = ALPHA * rows[0] + rows[1]
        mu = jnp.mean(z, axis=-1, keepdims=True)
        zc = z - mu
        var = jnp.mean(zc * zc, axis=-1, keepdims=True)
        rstd = lax.rsqrt(var + LN_EPS)
        xhat = zc * rstd
        err = xhat * bc[0] + bc[1] - rows[2]
        return (xhat, rstd, err * (1.0 / d)), (_colsum(err * err),)

    return _rowwise(fn, [x_res, y_sub, target], [g, b], [(d, F32), (1, F32), (d, F32)], [(1, d)], name=name,
                    after=after, tb_cap=512)


def _ln_bwd(terms, xhat, rstd, g, *, name, after=()):
    d = xhat.shape[1]
    coefs = [c for c, _ in terms]

    def fn(rows, bc):
        xh, rs = rows[0], rows[1]
        dout = None
        for c, v in zip(coefs, rows[2:]):
            tv = v if c == 1.0 else c * v
            dout = tv if dout is None else dout + tv
        dxh = dout * bc[0]
        m1 = jnp.mean(dxh, axis=-1, keepdims=True)
        m2 = jnp.mean(dxh * xh, axis=-1, keepdims=True)
        dz = rs * (dxh - m1 - xh * m2)
        return (dz, dz), (_colsum(dout * xh), _colsum(dout), _colsum(dz))

    return _rowwise(fn, [xhat, rstd] + [v for _, v in terms], [g], [(d, F32), (d, MXU_DTYPE)],
                    [(1, d), (1, d), (1, d)], name=name, after=after, tb_cap=512)


def _rms_fwd(x, g, *, name):
    c = x.shape[1]

    def fn(rows, bc):
        v = rows[0]
        r = lax.rsqrt(jnp.mean(v * v, axis=-1, keepdims=True) + RMS_EPS)
        return (v * r * bc[0],), ()

    return _rowwise(fn, [x], [g], [(c, MXU_DTYPE)], [], name=name)[0]


def _rms_bwd(x, g, dy, *, name):
    c = x.shape[1]

    def fn(rows, bc):
        v, dyv = rows
        r = lax.rsqrt(jnp.mean(v * v, axis=-1, keepdims=True) + RMS_EPS)
        xh = v * r
        dxh = dyv * bc[0]
        dx = r * (dxh - xh * jnp.mean(dxh * xh, axis=-1, keepdims=True))
        return (dx,), (_colsum(dyv * xh),)

    return _rowwise(fn, [x, dy], [g], [(c, F32)], [(1, c)], name=name)


def _conv_chunk(t):
    return _row_tile(t, 256)


def _conv_fwd(x, w, bias, *, name):
    t, c = x.shape
    kw = w.shape[0]
    ch = _conv_chunk(t)

    def body(x_ref, w_ref, b_ref, o_ref, pad_ref):
        pad_ref[0:CONV_PAD, :] = jnp.zeros((CONV_PAD, LANES), F32)
        pad_ref[CONV_PAD:CONV_PAD + t, :] = x_ref[...]
        for t0 in range(0, t, ch):
            acc = jnp.broadcast_to(b_ref[...], (ch, LANES))
            for k in range(kw):
                s = kw - 1 - k
                acc = acc + w_ref[k:k + 1, :] * pad_ref[CONV_PAD - s + t0:CONV_PAD - s + t0 + ch, :]
            o_ref[t0:t0 + ch, :] = acc

    return pl.pallas_call(
        body,
        name=name,
        grid=(c // LANES,),
        in_specs=[pl.BlockSpec((t, LANES), lambda j: (0, j)), pl.BlockSpec((kw, LANES), lambda j: (0, j)),
                  pl.BlockSpec((1, LANES), lambda j: (0, j))],
        out_specs=pl.BlockSpec((t, LANES), lambda j: (0, j)),
        out_shape=jax.ShapeDtypeStruct((t, c), F32),
        scratch_shapes=[pltpu.VMEM((CONV_PAD + t, LANES), F32)],
        compiler_params=_params(("parallel",)),
    )(x, w, bias)


def _conv_bwd(dy, x, w, *, name):
    t, c = x.shape
    kw = w.shape[0]
    ch = _conv_chunk(t)

    def body(dy_ref, x_ref, w_ref, dx_ref, dw_ref, xpad_ref, dpad_ref):
        xpad_ref[0:CONV_PAD, :] = jnp.zeros((CONV_PAD, LANES), F32)
        xpad_ref[CONV_PAD:CONV_PAD + t, :] = x_ref[...]
        dpad_ref[0:t, :] = dy_ref[...]
        dpad_ref[t:t + CONV_PAD, :] = jnp.zeros((CONV_PAD, LANES), F32)
        for t0 in range(0, t, ch):
            acc = jnp.zeros((ch, LANES), F32)
            for k in range(kw):
                s = kw - 1 - k
                acc = acc + w_ref[k:k + 1, :] * dpad_ref[t0 + s:t0 + s + ch, :]
            dx_ref[t0:t0 + ch, :] = acc
        for k in range(kw):
            s = kw - 1 - k
            acc = jnp.zeros((ch, LANES), F32)
            for t0 in range(0, t, ch):
                acc = acc + dy_ref[t0:t0 + ch, :] * xpad_ref[CONV_PAD - s + t0:CONV_PAD - s + t0 + ch, :]
            dw_ref[k:k + 1, :] = _colsum(acc)

    blk = pl.BlockSpec((t, LANES), lambda j: (0, j))
    wblk = pl.BlockSpec((kw, LANES), lambda j: (0, j))
    return pl.pallas_call(
        body,
        name=name,
        grid=(c // LANES,),
        in_specs=[blk, blk, wblk],
        out_specs=[blk, wblk],
        out_shape=[jax.ShapeDtypeStruct((t, c), F32), jax.ShapeDtypeStruct((kw, c), F32)],
        scratch_shapes=[pltpu.VMEM((CONV_PAD + t, LANES), F32), pltpu.VMEM((CONV_PAD + t, LANES), F32)],
        compiler_params=_params(("parallel",)),
    )(dy, x, w)


ATTN_SCALE = (QK_NOPE + QK_ROPE) ** -0.5
_NT = (((1,), (1,)), ((), ()))
_TN = (((0,), (0,)), ((), ()))
_NN = (((1,), (0,)), ((), ()))


def _dot(a, b, dims):
    return lax.dot_general(a, b, dims, preferred_element_type=F32)


def _block_scores(q_cat, kn_ref, kp_ref, lo, hi, diagonal):
    k_cat = jnp.concatenate([kn_ref[lo:hi, :], kp_ref[lo:hi, :]], axis=1)
    s = _dot(q_cat, k_cat, _NT) * ATTN_SCALE
    if diagonal:
        shift = CHUNK.bit_length() - 1
        rows = lax.broadcasted_iota(jnp.int32, s.shape, 0)
        cols = lax.broadcasted_iota(jnp.int32, s.shape, 1)
        s = jnp.where(jnp.right_shift(cols, shift) <= jnp.right_shift(rows, shift), s, MASK_VALUE)
    return s, k_cat


def _attn_fwd(qn, qp, kv, kp, *, name):
    h, t, _ = qp.shape
    tq = _row_tile(t, 256)
    assert tq % CHUNK == 0

    def body(qn_ref, qp_ref, kn_ref, kp_ref, v_ref, o_ref, of_ref, lse_ref):
        for r0 in range(0, t, tq):
            r1 = r0 + tq
            q_cat = jnp.concatenate([qn_ref[r0:r1, :].astype(MXU_DTYPE), qp_ref[r0:r1, :]], axis=1)
            s_d, _ = _block_scores(q_cat, kn_ref, kp_ref, r0, r1, True)
            m = jnp.max(s_d, axis=-1, keepdims=True)
            if r0:
                s_b, _ = _block_scores(q_cat, kn_ref, kp_ref, 0, r0, False)
                m = jnp.maximum(m, jnp.max(s_b, axis=-1, keepdims=True))
            p_d = jnp.exp(s_d - m)
            l = jnp.sum(p_d, axis=-1, keepdims=True)
            acc = _dot(p_d.astype(MXU_DTYPE), v_ref[r0:r1, :], _NN)
            if r0:
                p_b = jnp.exp(s_b - m)
                l = l + jnp.sum(p_b, axis=-1, keepdims=True)
                acc = acc + _dot(p_b.astype(MXU_DTYPE), v_ref[0:r0, :], _NN)
            o = acc / l
            o_ref[r0:r1, :] = o.astype(o_ref.dtype)
            of_ref[r0:r1, :] = o
            lse_ref[r0:r1, :] = m + jnp.log(l)

    blk = lambda w: pl.BlockSpec((None, t, w), lambda hh: (hh, 0, 0))
    cols = pl.BlockSpec((t, QK_NOPE), lambda hh: (0, hh))
    keys = pl.BlockSpec((t, QK_NOPE), lambda hh: (0, 2 * hh))
    values = pl.BlockSpec((t, V_HEAD), lambda hh: (0, 2 * hh + 1))
    return pl.pallas_call(
        body,
        name=name,
        grid=(h,),
        in_specs=[cols, blk(QK_ROPE), keys, pl.BlockSpec((t, QK_ROPE), lambda hh: (0, 0)), values],
        out_specs=[cols, cols, blk(1)],
        out_shape=[jax.ShapeDtypeStruct((t, h * V_HEAD), MXU_DTYPE), jax.ShapeDtypeStruct((t, h * V_HEAD), F32),
                   jax.ShapeDtypeStruct((h, t, 1), F32)],
        compiler_params=_params(("parallel",)),
    )(qn, qp, kv, kp, kv)


def _attn_bwd(qn, qp, kv, kp, do, o, lse, *, name):
    h, t, _ = qp.shape
    tq = _row_tile(t, 256)

    def body(qn_ref, qp_ref, kn_ref, kp_ref, v_ref, do_ref, o_ref, lse_ref, dqn_ref, dqp_ref, dkv_ref, dkp_ref):
        dkv_ref[...] = jnp.zeros_like(dkv_ref)

        @pl.when(pl.program_id(0) == 0)
        def _():
            dkp_ref[...] = jnp.zeros_like(dkp_ref)

        for r0 in range(0, t, tq):
            r1 = r0 + tq
            q_cat = jnp.concatenate([qn_ref[r0:r1, :].astype(MXU_DTYPE), qp_ref[r0:r1, :]], axis=1)
            dov, lse = do_ref[r0:r1, :], lse_ref[r0:r1, :]
            delta = jnp.sum(dov.astype(F32) * o_ref[r0:r1, :], axis=-1, keepdims=True)

            def piece(lo, hi, diagonal):
                s, k_cat = _block_scores(q_cat, kn_ref, kp_ref, lo, hi, diagonal)
                p = jnp.exp(s - lse)
                dp = _dot(dov, v_ref[lo:hi, :], _NT)
                ds_m = (p * (dp - delta) * ATTN_SCALE).astype(MXU_DTYPE)
                dk_cat = _dot(ds_m, q_cat, _TN)
                dkv_ref[lo:hi, :QK_NOPE] += dk_cat[:, :QK_NOPE]
                dkp_ref[lo:hi, :] += dk_cat[:, QK_NOPE:]
                dkv_ref[lo:hi, QK_NOPE:] += _dot(p.astype(MXU_DTYPE), dov, _TN)
                return _dot(ds_m, k_cat, _NN)

            dq_cat = piece(r0, r1, True)
            if r0:
                dq_cat = dq_cat + piece(0, r0, False)
            dqn_ref[r0:r1, :] = dq_cat[:, :QK_NOPE]
            dqp_ref[r0:r1, :] = dq_cat[:, QK_NOPE:]

    blk = lambda w: pl.BlockSpec((None, t, w), lambda hh: (hh, 0, 0))
    kpblk = pl.BlockSpec((t, QK_ROPE), lambda hh: (0, 0))
    cols = pl.BlockSpec((t, QK_NOPE), lambda hh: (0, hh))
    keys = pl.BlockSpec((t, QK_NOPE), lambda hh: (0, 2 * hh))
    values = pl.BlockSpec((t, V_HEAD), lambda hh: (0, 2 * hh + 1))
    both = pl.BlockSpec((t, QK_NOPE + V_HEAD), lambda hh: (0, hh))
    return pl.pallas_call(
        body,
        name=name,
        grid=(h,),
        in_specs=[cols, blk(QK_ROPE), keys, kpblk, values, cols, cols, blk(1)],
        out_specs=[cols, blk(QK_ROPE), both, kpblk],
        out_shape=[jax.ShapeDtypeStruct((t, h * QK_NOPE), F32), jax.ShapeDtypeStruct((h, t, QK_ROPE), F32),
                   jax.ShapeDtypeStruct((t, h * (QK_NOPE + V_HEAD)), F32), jax.ShapeDtypeStruct((t, QK_ROPE), F32)],
        compiler_params=_params(("arbitrary",)),
    )(qn, qp, kv, kp, kv, do, o, lse)


def _rope(x1, x2, cos, sin, *, name, inverse=False):
    w = x1.shape[1]
    sign = -1.0 if inverse else 1.0

    def fn(rows, bc):
        a, b, c, s = rows
        s = sign * s
        return (a * c - b * s, a * s + b * c), ()

    return _rowwise(fn, [x1, x2, cos, sin], [], [(w, F32), (w, F32)], [], name=name)


def _rope_in_place(x, cos_pairs, sin_signed, *, name, inverse=False):
    w = x.shape[1]
    half = ROPE_HALF
    sign = -1.0 if inverse else 1.0

    def fn(rows, bc):
        v, c, s = rows
        lane = lax.broadcasted_iota(jnp.int32, v.shape, 1)
        partner = jnp.where((lane & half) == 0, pltpu.roll(v, w - half, 1), pltpu.roll(v, half, 1))
        return (v * c + partner * (sign * s),), ()

    return _rowwise(fn, [x, cos_pairs, sin_signed], [], [(w, F32)], [], name=name)[0]


def _mlp_up(x_m, w1, tag, after=()):
    def epi(acc):
        r = jnp.maximum(acc, 0.0)
        return (r * r,)

    return _mm(x_m, w1, name=f"{tag}_up", out_dtypes=(MXU_DTYPE,), epilogue=epi, after=after)


def _mlp_fwd(x_m, w1, w2, tag, after=(), act=None):
    if act is None:
        act, after = _mlp_up(x_m, w1, tag, after=after), ()
    y = _mm(act, w2, name=f"{tag}_down", after=after)
    return y, (x_m, act)


def _plus_residual(acc, res):
    return (acc + ALPHA * res,)


def _mlp_bwd(dy_m, saved, w1, w2, tag, residual, after=()):
    x_m, act = saved
    dw2 = _mm(act, dy_m, ta=True, name=f"{tag}_dw2", out_dtypes=(MXU_DTYPE,))

    def epi(acc, av):
        return (acc * (2.0 * jnp.sqrt(av.astype(F32))),)

    dh = _mm(dy_m, w2, tb=True, name=f"{tag}_dact", out_dtypes=(MXU_DTYPE,), epilogue=epi, tile_extras=(act,),
             after=after)
    dw1 = _mm(x_m, dh, ta=True, out_dev=True, name=f"{tag}_dw1", out_dtypes=(MXU_DTYPE,))
    dx = _mm(dh, w1, tb=True, name=f"{tag}_dx", epilogue=_plus_residual, tile_extras=(residual,))
    return [dx], {"ff_w1": dw1, "ff_w2": dw2}


def _sc_gate_specs(t, d):
    per = d // LANES
    return [pl.BlockSpec((t, LANES), lambda j, k=k: (0, k * per + j)) for k in range(3)]


def _sc_gate_fwd(u, conv_w, *, name):
    t, d = u.shape[0], u.shape[1] // 3
    kw = conv_w.shape[0]
    ch = _conv_chunk(t)

    def body(b_ref, c_ref, h_ref, w_ref, r_ref, pad_ref):
        pad_ref[0:CONV_PAD, :] = jnp.zeros((CONV_PAD, LANES), F32)
        pad_ref[CONV_PAD:CONV_PAD + t, :] = c_ref[...] * h_ref[...]
        for t0 in range(0, t, ch):
            q = jnp.zeros((ch, LANES), F32)
            for k in range(kw):
                s = kw - 1 - k
                q = q + w_ref[k:k + 1, :] * pad_ref[CONV_PAD - s + t0:CONV_PAD - s + t0 + ch, :]
            r_ref[t0:t0 + ch, :] = (b_ref[t0:t0 + ch, :] * q).astype(r_ref.dtype)

    blk = pl.BlockSpec((t, LANES), lambda j: (0, j))
    return pl.pallas_call(
        body,
        name=name,
        grid=(d // LANES,),
        in_specs=_sc_gate_specs(t, d) + [pl.BlockSpec((kw, LANES), lambda j: (0, j))],
        out_specs=blk,
        out_shape=jax.ShapeDtypeStruct((t, d), MXU_DTYPE),
        scratch_shapes=[pltpu.VMEM((CONV_PAD + t, LANES), F32)],
        compiler_params=_params(("parallel",)),
    )(u, u, u, conv_w)


def _sc_gate_bwd(dr, u, conv_w, *, name):
    t, d = u.shape[0], u.shape[1] // 3
    kw = conv_w.shape[0]
    ch = _conv_chunk(t)

    def body(dr_ref, b_ref, c_ref, h_ref, w_ref, db_ref, dc_ref, dh_ref, dw_ref, ppad_ref, dpad_ref):
        ppad_ref[0:CONV_PAD, :] = jnp.zeros((CONV_PAD, LANES), F32)
        ppad_ref[CONV_PAD:CONV_PAD + t, :] = c_ref[...] * h_ref[...]
        dpad_ref[0:t, :] = dr_ref[...] * b_ref[...]
        dpad_ref[t:t + CONV_PAD, :] = jnp.zeros((CONV_PAD, LANES), F32)
        for t0 in range(0, t, ch):
            q = jnp.zeros((ch, LANES), F32)
            dp = jnp.zeros((ch, LANES), F32)
            for k in range(kw):
                s = kw - 1 - k
                q = q + w_ref[k:k + 1, :] * ppad_ref[CONV_PAD - s + t0:CONV_PAD - s + t0 + ch, :]
                dp = dp + w_ref[k:k + 1, :] * dpad_ref[t0 + s:t0 + s + ch, :]
            db_ref[t0:t0 + ch, :] = (dr_ref[t0:t0 + ch, :] * q).astype(db_ref.dtype)
            dc_ref[t0:t0 + ch, :] = (dp * h_ref[t0:t0 + ch, :]).astype(dc_ref.dtype)
            dh_ref[t0:t0 + ch, :] = (dp * c_ref[t0:t0 + ch, :]).astype(dh_ref.dtype)
        for k in range(kw):
            s = kw - 1 - k
            acc = jnp.zeros((ch, LANES), F32)
            for t0 in range(0, t, ch):
                acc = acc + dpad_ref[t0:t0 + ch, :] * ppad_ref[CONV_PAD - s + t0:CONV_PAD - s + t0 + ch, :]
            dw_ref[k:k + 1, :] = _colsum(acc)

    blk = pl.BlockSpec((t, LANES), lambda j: (0, j))
    wblk = pl.BlockSpec((kw, LANES), lambda j: (0, j))
    return pl.pallas_call(
        body,
        name=name,
        grid=(d // LANES,),
        in_specs=[blk] + _sc_gate_specs(t, d) + [wblk],
        out_specs=[blk, blk, blk, wblk],
        out_shape=[jax.ShapeDtypeStruct((t, d), MXU_DTYPE)] * 3 + [jax.ShapeDtypeStruct((kw, d), F32)],
        scratch_shapes=[pltpu.VMEM((CONV_PAD + t, LANES), F32), pltpu.VMEM((CONV_PAD + t, LANES), F32)],
        compiler_params=_params(("parallel",)),
    )(dr, u, u, u, conv_w)


def _sc_fwd(x_m, w_in, conv_w, w_out, tag, after=()):
    u = _mm(x_m, w_in, name=f"{tag}_in", after=after)
    r = _sc_gate_fwd(u, conv_w, name=f"{tag}_gate")
    y = _mm(r, w_out, name=f"{tag}_out")
    return y, (x_m, u, r)


def _sc_bwd(dy_m, saved, w_in, conv_w, w_out, tag, residual, after=()):
    d = D_MODEL
    x_m, u, r = saved
    dw_out = _mm(r, dy_m, ta=True, name=f"{tag}_dwout", out_dtypes=(MXU_DTYPE,))
    dr = _mm(dy_m, w_out, tb=True, name=f"{tag}_dr", after=after)
    db, dc, dh, dconv = _sc_gate_bwd(dr, u, conv_w, name=f"{tag}_dgate")
    du = jnp.concatenate([db, dc, dh], axis=1)
    dw_in = _mm(x_m, du, ta=True, out_dev=True, name=f"{tag}_dwin", out_dtypes=(MXU_DTYPE,))
    dx = _mm(du, w_in, tb=True, name=f"{tag}_dx", epilogue=_plus_residual, tile_extras=(residual,))
    return [dx], {"sc_w_in": dw_in, "sc_conv_w": dconv, "sc_w_out": dw_out}


def _cf_fwd(x_m, w, tag, after=()):
    d = D_MODEL

    def bias_epi(acc, bias):
        return (acc + bias,)

    u = _mm(x_m, w["cf_w_pw1"], name=f"{tag}_pw1", epilogue=bias_epi, col_extras=(w["cf_b_pw1"],),
            after=after)

    def glu_fn(rows, bc):
        uv = rows[0]
        return (uv[:, 0:d] * _sigmoid(uv[:, d:2 * d]),), ()

    h1 = _rowwise(glu_fn, [u], [], [(d, F32)], [], name=f"{tag}_glu")[0]
    h2 = _conv_fwd(h1, w["cf_dw_w"], w["cf_dw_b"], name=f"{tag}_conv")

    def norm_fn(rows, bc):
        hv = rows[0]
        mu = jnp.mean(hv, axis=-1, keepdims=True)
        hc = hv - mu
        rstd = lax.rsqrt(jnp.mean(hc * hc, axis=-1, keepdims=True) + LN_EPS)
        h3 = hc * rstd * bc[0] + bc[1]
        return (h3 * _sigmoid(h3),), ()

    h4 = _rowwise(norm_fn, [h2], [w["cf_norm_g"], w["cf_norm_b"]], [(d, MXU_DTYPE)], [], name=f"{tag}_norm")[0]
    y = _mm(h4, w["cf_w_pw2"], name=f"{tag}_pw2", epilogue=bias_epi, col_extras=(w["cf_b_pw2"],))
    return y, (x_m, u, h1, h2, h4)


def _cf_bwd(dy_m, dy_colsum, saved, w, tag, residual, after=()):
    d = D_MODEL
    x_m, u, h1, h2, h4 = saved
    dw_pw2 = _mm(h4, dy_m, ta=True, name=f"{tag}_dwpw2", out_dtypes=(MXU_DTYPE,))
    dh4 = _mm(dy_m, w["cf_w_pw2"], tb=True, name=f"{tag}_dh4", after=after)

    def dnorm_fn(rows, bc):
        dh4v, hv = rows
        g, b = bc
        mu = jnp.mean(hv, axis=-1, keepdims=True)
        hc = hv - mu
        rstd = lax.rsqrt(jnp.mean(hc * hc, axis=-1, keepdims=True) + LN_EPS)
        xh = hc * rstd
        h3 = xh * g + b
        sg = _sigmoid(h3)
        dh3 = dh4v * (sg * (1.0 + h3 * (1.0 - sg)))
        dxh = dh3 * g
        dh2 = rstd * (dxh - jnp.mean(dxh, axis=-1, keepdims=True) - xh * jnp.mean(dxh * xh, axis=-1, keepdims=True))
        return (dh2,), (_colsum(dh3 * xh), _colsum(dh3), _colsum(dh2))

    dh2, dnorm_g, dnorm_b, ddw_b = _rowwise(dnorm_fn, [dh4, h2], [w["cf_norm_g"], w["cf_norm_b"]], [(d, F32)],
                                            [(1, d), (1, d), (1, d)], name=f"{tag}_dnorm")
    dh1, ddw_w = _conv_bwd(dh2, h1, w["cf_dw_w"], name=f"{tag}_dconv")

    def dglu_fn(rows, bc):
        dh1v, uv = rows
        a, gate = uv[:, 0:d], uv[:, d:2 * d]
        sg = _sigmoid(gate)
        du = jnp.concatenate([dh1v * sg, dh1v * a * sg * (1.0 - sg)], axis=1)
        return (du,), (_colsum(du),)

    du, db_pw1 = _rowwise(dglu_fn, [dh1, u], [], [(2 * d, MXU_DTYPE)], [(1, 2 * d)], name=f"{tag}_dglu")
    dw_pw1 = _mm(x_m, du, ta=True, out_dev=True, name=f"{tag}_dwpw1", out_dtypes=(MXU_DTYPE,))
    dx = _mm(du, w["cf_w_pw1"], tb=True, name=f"{tag}_dx", epilogue=_plus_residual, tile_extras=(residual,))
    grads = {"cf_w_pw1": dw_pw1, "cf_b_pw1": db_pw1, "cf_dw_w": ddw_w, "cf_dw_b": ddw_b, "cf_norm_g": dnorm_g,
             "cf_norm_b": dnorm_b, "cf_w_pw2": dw_pw2, "cf_b_pw2": dy_colsum}
    return [dx], grads


def _rope_tables(t):
    pos = jnp.arange(t, dtype=F32)
    inv_freq = ROPE_THETA ** (-jnp.arange(0, QK_ROPE, 2, dtype=F32) / QK_ROPE)
    ang = pos[:, None] * inv_freq[None, :]
    return jnp.cos(ang), jnp.sin(ang)


ROPE_HALF = QK_ROPE // 2
Q_NOPE_COLS = MLA_HEADS * QK_NOPE


def _interleave_heads(w_uk, w_uv):
    c = w_uk.shape[0]
    return jnp.concatenate([w_uk.reshape(c, MLA_HEADS, QK_NOPE), w_uv.reshape(c, MLA_HEADS, V_HEAD)],
                           axis=2).reshape(c, -1)


def _mla_fwd(x_m, w, tag, after=()):
    t = x_m.shape[0]
    cos, sin = _rope_tables(t)
    cos_h = jnp.tile(jnp.concatenate([cos, cos], axis=1), (1, MLA_HEADS))
    sin_h = jnp.tile(jnp.concatenate([-sin, sin], axis=1), (1, MLA_HEADS))
    cq_pre = _mm(x_m, w["mla_w_dq"], name=f"{tag}_dq", after=after)
    cq = _rms_fwd(cq_pre, w["mla_g_q"], name=f"{tag}_qnorm")
    q = _mm(cq, w["mla_w_uq"], name=f"{tag}_uq")
    qp = _rope_in_place(q[:, Q_NOPE_COLS:], cos_h, sin_h, name=f"{tag}_qrope")
    qp_h = qp.reshape(t, MLA_HEADS, QK_ROPE).transpose(1, 0, 2).astype(MXU_DTYPE)
    ckv_full = _mm(x_m, w["mla_w_dkv"], name=f"{tag}_dkv")
    ckv_pre = ckv_full[:, :KV_LORA]
    ckv = _rms_fwd(ckv_pre, w["mla_g_kv"], name=f"{tag}_kvnorm")
    k1, k2 = _rope(ckv_full[:, KV_LORA:KV_LORA + ROPE_HALF], ckv_full[:, KV_LORA + ROPE_HALF:], cos, sin,
                   name=f"{tag}_krope")
    kp = jnp.concatenate([k1, k2], axis=-1).astype(MXU_DTYPE)
    kv = _mm(ckv, _interleave_heads(w["mla_w_uk"], w["mla_w_uv"]), name=f"{tag}_ukv", out_dtypes=(MXU_DTYPE,))
    o, o_f, lse = _attn_fwd(q, qp_h, kv, kp, name=f"{tag}_attn")
    y = _mm(o, w["mla_w_o"], name=f"{tag}_o")
    return y, (x_m, cq_pre, cq, ckv_pre, ckv, q, qp_h, kv, kp, lse, o, o_f, (cos, sin, cos_h, sin_h))


def _mla_bwd(dy_m, saved, w, tag, residual, after=()):
    x_m, cq_pre, cq, ckv_pre, ckv, q, qp_h, kv, kp, lse, o, o_f, (cos, sin, cos_h, sin_h) = saved
    t = x_m.shape[0]
    half = ROPE_HALF
    dw_o = _mm(o, dy_m, ta=True, name=f"{tag}_dwo", out_dtypes=(MXU_DTYPE,))
    do = _mm(dy_m, w["mla_w_o"], tb=True, name=f"{tag}_do", out_dtypes=(MXU_DTYPE,), after=after)
    dqn, dqp_h, dkv, dkp = _attn_bwd(q, qp_h, kv, kp, do, o_f, lse, name=f"{tag}_dattn")
    dqp = _rope_in_place(dqp_h.transpose(1, 0, 2).reshape(t, -1), cos_h, sin_h, name=f"{tag}_dqrope", inverse=True)
    dq = jnp.concatenate([dqn, dqp], axis=-1).astype(MXU_DTYPE)
    dw_uq = _mm(cq, dq, ta=True, name=f"{tag}_dwuq", out_dtypes=(MXU_DTYPE,))
    dcq = _mm(dq, w["mla_w_uq"], tb=True, name=f"{tag}_dcq")
    dcq_pre, dg_q = _rms_bwd(cq_pre, w["mla_g_q"], dcq, name=f"{tag}_dqnorm")
    dcq_pre_m = dcq_pre.astype(MXU_DTYPE)
    dw_dq = _mm(x_m, dcq_pre_m, ta=True, name=f"{tag}_dwdq", out_dtypes=(MXU_DTYPE,))
    dx_q = _mm(dcq_pre_m, w["mla_w_dq"], tb=True, name=f"{tag}_dxq", epilogue=_plus_residual, tile_extras=(residual,))
    dw_kv = _mm(ckv, dkv, ta=True, name=f"{tag}_dwukv", out_dtypes=(MXU_DTYPE,))
    dw_kv = dw_kv.reshape(KV_LORA, MLA_HEADS, QK_NOPE + V_HEAD)
    dw_uk, dw_uv = dw_kv[:, :, :QK_NOPE].reshape(KV_LORA, -1), dw_kv[:, :, QK_NOPE:].reshape(KV_LORA, -1)
    dckv = _mm(dkv, _interleave_heads(w["mla_w_uk"], w["mla_w_uv"]), tb=True, name=f"{tag}_dckv")
    dckv_pre, dg_kv = _rms_bwd(ckv_pre, w["mla_g_kv"], dckv, name=f"{tag}_dkvnorm")
    dk1, dk2 = _rope(dkp[:, :half], dkp[:, half:], cos, sin, name=f"{tag}_dkrope", inverse=True)
    dckv_full = jnp.concatenate([dckv_pre, dk1, dk2], axis=-1).astype(MXU_DTYPE)
    dw_dkv = _mm(x_m, dckv_full, ta=True, name=f"{tag}_dwdkv", out_dtypes=(MXU_DTYPE,))
    dx_kv = _mm(dckv_full, w["mla_w_dkv"], tb=True, name=f"{tag}_dxkv", epilogue=lambda acc, other: (acc + other,),
                tile_extras=(dx_q,))
    grads = {"mla_w_dq": dw_dq, "mla_g_q": dg_q, "mla_w_uq": dw_uq, "mla_w_dkv": dw_dkv, "mla_g_kv": dg_kv,
             "mla_w_uk": dw_uk, "mla_w_uv": dw_uv, "mla_w_o": dw_o}
    return [dx_kv], grads


def _cast_input(x):
    cast = lambda rows, bc: ((rows[0],), ())
    return _rowwise(cast, [x], [], [(D_MODEL, MXU_DTYPE)], [], name="x_cast")[0]


def _hooked(value, hook):
    token = None if hook is None else hook(value)
    return () if token is None else (token,)


def _mix_fwd(i, cur, cur_m, wl, hook=None, after=()):
    mixer = i % N_MIXERS
    if mixer == 0:
        y, s_mix = _sc_fwd(cur_m, wl["sc_w_in"], wl["sc_conv_w"], wl["sc_w_out"], f"l{i}_sc", after=after)
    elif mixer == 1:
        y, s_mix = _mla_fwd(cur_m, wl, f"l{i}_mla", after=after)
    else:
        y, s_mix = _cf_fwd(cur_m, wl, f"l{i}_cf", after=after)
    cur, cur_m, xh_mix, rs_mix = _ln_fwd(cur, y, wl["ln_mix_g"], wl["ln_mix_b"], name=f"l{i}_lnmix",
                                         after=_hooked(y, hook))
    return cur, cur_m, (s_mix, xh_mix, rs_mix)


def _ff_fwd(i, cur, cur_m, wl, hook=None, after=(), loss_target=None, act=None):
    y, s_ff = _mlp_fwd(cur_m, wl["ff_w1"], wl["ff_w2"], f"l{i}_ff", after=after, act=act)
    if loss_target is not None:
        xh_ff, rs_ff, dy, sq = _ln_fwd_loss(cur, y, wl["ln_ff_g"], wl["ln_ff_b"], loss_target, name=f"l{i}_lnff_loss",
                                            after=_hooked(y, hook))
        return (0.5 / D_MODEL) * jnp.sum(sq), dy, (s_ff, xh_ff, rs_ff)
    cur, cur_m, xh_ff, rs_ff = _ln_fwd(cur, y, wl["ln_ff_g"], wl["ln_ff_b"], name=f"l{i}_lnff", after=_hooked(y, hook))
    return cur, cur_m, (s_ff, xh_ff, rs_ff)


def _layer_fwd(i, cur, cur_m, wl):
    cur, cur_m, s_mix = _mix_fwd(i, cur, cur_m, wl)
    cur, cur_m, s_ff = _ff_fwd(i, cur, cur_m, wl)
    return cur, cur_m, (*s_mix, *s_ff)


def _loss_and_grad(out, target):
    d = D_MODEL

    def loss_fn(rows, bc):
        err = rows[0] - rows[1]
        return (err * (1.0 / d),), (_colsum(err * err),)

    dy, sq = _rowwise(loss_fn, [out, target], [], [(d, F32)], [(1, d)], name="loss")
    return (0.5 / d) * jnp.sum(sq), dy


def _ff_bwd(i, terms, saved, wl, hook=None, after=()):
    s_ff, xh_ff, rs_ff = saved
    grads = {}
    dz, dz_m, grads["ln_ff_g"], grads["ln_ff_b"], _ = _ln_bwd(terms, xh_ff, rs_ff, wl["ln_ff_g"], name=f"l{i}_dlnff",
                                                              after=after)
    dxs, g_ff = _mlp_bwd(dz_m, s_ff, wl["ff_w1"], wl["ff_w2"], f"l{i}_ff", dz, after=_hooked(dz, hook))
    grads.update(g_ff)
    return [(1.0, v) for v in dxs], grads


def _mix_bwd(i, terms, saved, wl, hook=None, after=()):
    mixer = i % N_MIXERS
    s_mix, xh_mix, rs_mix = saved
    grads = {}
    dz, dz_m, grads["ln_mix_g"], grads["ln_mix_b"], dz_sum = _ln_bwd(terms, xh_mix, rs_mix, wl["ln_mix_g"],
                                                                     name=f"l{i}_dlnmix", after=after)
    behind = _hooked(dz, hook)
    if mixer == 0:
        dxs, g_mix = _sc_bwd(dz_m, s_mix, wl["sc_w_in"], wl["sc_conv_w"], wl["sc_w_out"], f"l{i}_sc", dz, after=behind)
    elif mixer == 1:
        dxs, g_mix = _mla_bwd(dz_m, s_mix, wl, f"l{i}_mla", dz, after=behind)
    else:
        dxs, g_mix = _cf_bwd(dz_m, dz_sum, s_mix, wl, f"l{i}_cf", dz, after=behind)
    grads.update(g_mix)
    return [(1.0, v) for v in dxs], grads


def _layer_bwd(i, terms, saved, wl):
    terms, g_ff = _ff_bwd(i, terms, saved[3:], wl)
    terms, g_mix = _mix_bwd(i, terms, saved[:3], wl)
    return terms, {**g_ff, **g_mix}


def _sum_terms(terms, *, name):
    coefs = [c for c, _ in terms]

    def sum_fn(rows, bc):
        acc = None
        for c, v in zip(coefs, rows):
            tv = v if c == 1.0 else c * v
            acc = tv if acc is None else acc + tv
        return (acc,), ()

    return _rowwise(sum_fn, [v for _, v in terms], [], [(D_MODEL, F32)], [], name=name)[0]


def _local_step(x, target, layers):
    cur, cur_m = x, _cast_input(x)
    saved = []
    for i in range(DEPTH):
        cur, cur_m, s = _layer_fwd(i, cur, cur_m, layers[i])
        saved.append(s)
    loss, dy = _loss_and_grad(cur, target)
    terms = [(1.0, dy)]
    grads = [None] * DEPTH
    for i in reversed(range(DEPTH)):
        terms, grads[i] = _layer_bwd(i, terms, saved[i], layers[i])
    return loss, _sum_terms(terms, name="grad_x"), grads


MESH_ID = pl.DeviceIdType.MESH
ANY = pl.BlockSpec(memory_space=pl.ANY)


def _block_id(px, py, pc):
    return 4 * px + 2 * py + pc


COPIES_PER_GATHER = 7


def _all_gather(shards, *, name):
    n = len(shards)

    def body(*refs):
        x_refs, out_refs = refs[:n], refs[n:2 * n]
        send_sems, recv_sems, local_sems = refs[2 * n:]
        x, y, core = lax.axis_index("x"), lax.axis_index("y"), lax.axis_index("c")
        me, sibling = (x, y, core), (x, y, 1 - core)
        chips = [(1 - x, y), (x, 1 - y), (1 - x, 1 - y)]

        def copy(t, k, block, to, from_input=False):
            dst = out_refs[t].at[_block_id(*block)]
            return pltpu.make_async_remote_copy(
                src_ref=x_refs[t] if from_input else dst, dst_ref=dst,
                send_sem=send_sems.at[t * COPIES_PER_GATHER + k], recv_sem=recv_sems.at[t * COPIES_PER_GATHER + k],
                device_id=to, device_id_type=MESH_ID)

        started = []
        mine = [pltpu.make_async_copy(x_refs[t], out_refs[t].at[_block_id(*me)], local_sems.at[t]) for t in range(n)]
        for t in range(n):
            mine[t].start()
            first = [copy(t, 0, me, sibling, from_input=True)]
            first += [copy(t, 1 + j, me, (*chip, core), from_input=True) for j, chip in enumerate(chips)]
            for cp in first:
                cp.start()
            started += first
        for t in range(n):
            for j, chip in enumerate(chips):
                copy(t, 1 + j, (*chip, core), me).wait_recv()
                passed = copy(t, 4 + j, (*chip, core), sibling)
                passed.start()
                started.append(passed)
        for t in range(n):
            copy(t, 0, sibling, me).wait_recv()
            for j, chip in enumerate(chips):
                copy(t, 4 + j, (*chip, 1 - core), me).wait_recv()
        for cp in started:
            cp.wait_send()
        for cp in mine:
            cp.wait()

    n_sems = n * COPIES_PER_GATHER
    outs = pl.pallas_call(
        body,
        name=name,
        out_shape=[jax.ShapeDtypeStruct((N_DEV, *s.shape), s.dtype) for s in shards],
        in_specs=[ANY] * n,
        out_specs=[ANY] * n,
        scratch_shapes=[pltpu.SemaphoreType.DMA((n_sems,)), pltpu.SemaphoreType.DMA((n_sems,)),
                        pltpu.SemaphoreType.DMA((n,))],
    )(*shards)
    return list(outs)


def _swap_with_sibling(parts, *, name):
    n = len(parts)

    def body(*refs):
        p_refs, got_refs = refs[:n], refs[n:2 * n]
        send_sems, recv_sems = refs[2 * n:]
        x, y, core = lax.axis_index("x"), lax.axis_index("y"), lax.axis_index("c")
        chips = [(x, y), (1 - x, y), (x, 1 - y), (1 - x, 1 - y)]
        copies = []
        for t in range(n):
            for k, (px, py) in enumerate(chips):
                copies.append(pltpu.make_async_remote_copy(
                    src_ref=p_refs[t].at[_block_id(px, py, 1 - core)], dst_ref=got_refs[t].at[k],
                    send_sem=send_sems.at[4 * t + k], recv_sem=recv_sems.at[4 * t + k], device_id=(x, y, 1 - core),
                    device_id_type=MESH_ID))
        for cp in copies:
            cp.start()
        for cp in copies:
            cp.wait()

    outs = pl.pallas_call(
        body,
        name=name,
        out_shape=[jax.ShapeDtypeStruct((4, *p.shape[1:]), p.dtype) for p in parts],
        in_specs=[ANY] * n,
        out_specs=[ANY] * n,
        scratch_shapes=[pltpu.SemaphoreType.DMA((4 * n,)), pltpu.SemaphoreType.DMA((4 * n,))],
    )(*parts)
    return list(outs)


def _swap_between_chips(parts, *, name):
    n = len(parts)

    def body(*refs):
        p_refs, got_refs = refs[:n], refs[n:2 * n]
        send_sems, recv_sems = refs[2 * n:]
        x, y, core = lax.axis_index("x"), lax.axis_index("y"), lax.axis_index("c")
        chips = [(1 - x, y), (x, 1 - y), (1 - x, 1 - y)]
        copies = []
        for t in range(n):
            for j, (px, py) in enumerate(chips):
                copies.append(pltpu.make_async_remote_copy(
                    src_ref=p_refs[t].at[j], dst_ref=got_refs[t].at[j], send_sem=send_sems.at[3 * t + j],
                    recv_sem=recv_sems.at[3 * t + j], device_id=(px, py, core), device_id_type=MESH_ID))
        for cp in copies:
            cp.start()
        for cp in copies:
            cp.wait()

    outs = pl.pallas_call(
        body,
        name=name,
        out_shape=[jax.ShapeDtypeStruct(p.shape, p.dtype) for p in parts],
        in_specs=[ANY] * n,
        out_specs=[ANY] * n,
        scratch_shapes=[pltpu.SemaphoreType.DMA((3 * n,)), pltpu.SemaphoreType.DMA((3 * n,))],
    )(*parts)
    return list(outs)


def _chip_partials(parts, got, *, name):
    _, r, c = parts.shape
    tb = _row_tile(r, 512)
    x, y, core = lax.axis_index("x"), lax.axis_index("y"), lax.axis_index("c")
    chips = [(x, y), (1 - x, y), (x, 1 - y), (1 - x, 1 - y)]
    ids = jnp.stack([_block_id(px, py, core) for px, py in chips]).astype(jnp.int32)

    def body(ids_ref, p0, p1, p2, p3, g_ref, own_ref, out_ref):
        own_ref[...] = p0[...].astype(F32) + g_ref[0].astype(F32)
        for j, p in enumerate((p1, p2, p3)):
            out_ref[j] = (p[...].astype(F32) + g_ref[j + 1].astype(F32)).astype(out_ref.dtype)

    def part_spec(k):
        return pl.BlockSpec((None, tb, c), lambda i, ids_ref: (ids_ref[k], i, 0))

    grid_spec = pltpu.PrefetchScalarGridSpec(
        num_scalar_prefetch=1,
        grid=(r // tb,),
        in_specs=[part_spec(k) for k in range(4)] + [pl.BlockSpec((4, tb, c), lambda i, ids_ref: (0, i, 0))],
        out_specs=[pl.BlockSpec((tb, c), lambda i, ids_ref: (i, 0)), pl.BlockSpec((3, tb, c), lambda i, ids_ref: (0, i, 0))],
    )
    return pl.pallas_call(
        body,
        name=name,
        grid_spec=grid_spec,
        out_shape=[jax.ShapeDtypeStruct((r, c), F32), jax.ShapeDtypeStruct((3, r, c), parts.dtype)],
        compiler_params=_params(("parallel",)),
    )(ids, parts, parts, parts, parts, got)


def _chip_partials_group(parts, got, *, name):
    n = len(parts)
    rows = [p.shape[1] for p in parts]
    steps = 4 if all(r % (4 * BF16_ROWS) == 0 for r in rows) else (2 if all(r % (2 * BF16_ROWS) == 0 for r in rows) else 1)
    x, y, core = lax.axis_index("x"), lax.axis_index("y"), lax.axis_index("c")
    chips = [(x, y), (1 - x, y), (x, 1 - y), (1 - x, 1 - y)]
    ids = jnp.stack([_block_id(px, py, core) for px, py in chips]).astype(jnp.int32)

    def body(ids_ref, *refs):
        ins, outs = refs[:5 * n], refs[5 * n:]
        for t in range(n):
            p0, p1, p2, p3, g_ref = ins[5 * t:5 * t + 5]
            own_ref, out_ref = outs[2 * t], outs[2 * t + 1]
            own_ref[...] = p0[...].astype(F32) + g_ref[0].astype(F32)
            for j, p in enumerate((p1, p2, p3)):
                out_ref[j] = (p[...].astype(F32) + g_ref[j + 1].astype(F32)).astype(out_ref.dtype)

    in_specs, out_specs, out_shape, operands = [], [], [], []
    for p, g in zip(parts, got):
        _, r, c = p.shape
        tb = r // steps
        in_specs += [pl.BlockSpec((None, tb, c), lambda i, ids_ref, k=k: (ids_ref[k], i, 0)) for k in range(4)]
        in_specs.append(pl.BlockSpec((4, tb, c), lambda i, ids_ref: (0, i, 0)))
        out_specs += [pl.BlockSpec((tb, c), lambda i, ids_ref: (i, 0)), pl.BlockSpec((3, tb, c), lambda i, ids_ref: (0, i, 0))]
        out_shape += [jax.ShapeDtypeStruct((r, c), F32), jax.ShapeDtypeStruct((3, r, c), p.dtype)]
        operands += [p, p, p, p, g]
    outs = pl.pallas_call(
        body,
        name=name,
        grid_spec=pltpu.PrefetchScalarGridSpec(num_scalar_prefetch=1, grid=(steps,), in_specs=in_specs,
                                               out_specs=out_specs),
        out_shape=out_shape,
        compiler_params=_params(("parallel",)),
    )(ids, *operands)
    return [(outs[2 * t], outs[2 * t + 1]) for t in range(n)]


def _sum_blocks(first, blocks, *, name):
    n, r, c = blocks.shape
    tb = _row_tile(r, 512)

    def body(*refs):
        b_ref, o_ref = refs[-2], refs[-1]
        acc = refs[0][...] if first is not None else b_ref[0].astype(F32)
        for j in range(0 if first is not None else 1, n):
            acc = acc + b_ref[j].astype(F32)
        o_ref[...] = acc

    row_spec = pl.BlockSpec((tb, c), lambda i: (i, 0))
    return pl.pallas_call(
        body,
        name=name,
        grid=(r // tb,),
        in_specs=([row_spec] if first is not None else []) + [pl.BlockSpec((n, tb, c), lambda i: (0, i, 0))],
        out_specs=row_spec,
        out_shape=jax.ShapeDtypeStruct((r, c), F32),
        compiler_params=_params(("parallel",)),
    )(*([first] if first is not None else []), blocks)


def _reduce_scatter(parts, *, tag):
    got = _swap_with_sibling(parts, name=f"{tag}_sibling")
    sums = [_chip_partials(p, g, name=f"{tag}_chipsum{t}") for t, (p, g) in enumerate(zip(parts, got))]
    arrived = _swap_between_chips([others for _, others in sums], name=f"{tag}_chips")
    return [(own, arr) for (own, _), arr in zip(sums, arrived)]


HBM = pl.BlockSpec(memory_space=pltpu.HBM)
SEM = pl.BlockSpec(memory_space=pltpu.SEMAPHORE)
DATAFLOW = pltpu.SideEffectType.DATAFLOW_SIDE_EFFECTING


def _tie(value, *before):
    return lax.optimization_barrier((value, *before))[0]


def _split_call(name, arrays, sems, n_new, body, after=None):
    n_a, n_s = len(arrays), len(sems)
    after = [] if after is None else (list(after) if isinstance(after, (list, tuple)) else [after])
    n_after = len(after)

    def kernel_body(*refs):
        new = refs[n_a + n_s + n_after:n_a + n_s + n_after + 2] if n_new else (None, None)
        body(refs[:n_a], refs[n_a:n_a + n_s], *new)
        refs[-1][...] = jnp.zeros_like(refs[-1])

    n_sem_out = 2 if n_new else 0
    out_shape = [pltpu.SemaphoreType.DMA((n_new,))] * n_sem_out
    out_shape += [pltpu.HBM(a.shape, a.dtype) for a in arrays]
    out_shape += [jax.ShapeDtypeStruct((SUBLANES, LANES), F32)]
    outs = pl.pallas_call(
        kernel_body,
        name=name,
        out_shape=out_shape,
        in_specs=[HBM] * n_a + [SEM] * n_s + [ANY] * n_after,
        out_specs=[SEM] * n_sem_out + [HBM] * n_a + [pl.BlockSpec(memory_space=pltpu.VMEM)],
        input_output_aliases={i: n_sem_out + i for i in range(n_a)},
        compiler_params=pltpu.CompilerParams(has_side_effects=DATAFLOW),
    )(*[pltpu.with_memory_space_constraint(a, pltpu.HBM) for a in arrays], *sems, *after)
    return tuple(outs[:n_sem_out]), list(outs[n_sem_out:n_sem_out + n_a]), outs[-1]


def _remote(src, dst, send, recv, k, to):
    return pltpu.make_async_remote_copy(src_ref=src, dst_ref=dst, send_sem=send.at[k], recv_sem=recv.at[k],
                                        device_id=to, device_id_type=MESH_ID)


def _land_block(land_ref, block, by_cols, width):
    if by_cols:
        return land_ref.at[:, pl.ds(pl.multiple_of(block * width, LANES), width)]
    return land_ref.at[block]


def _place_in_window(stacked, layer, *, name):
    _, r, c = stacked.shape
    tb = _row_tile(r, 512)
    me = _block_id(lax.axis_index("x"), lax.axis_index("y"), lax.axis_index("c")).astype(jnp.int32).reshape(1)

    def body(me_ref, s_ref, o_ref, token_ref):
        o_ref[...] = s_ref[...].astype(o_ref.dtype)
        token_ref[...] = jnp.zeros_like(token_ref)

    return pl.pallas_call(
        body,
        name=name,
        grid_spec=pltpu.PrefetchScalarGridSpec(
            num_scalar_prefetch=1, grid=(r // tb,),
            in_specs=[pl.BlockSpec((None, tb, c), lambda i, me_ref: (layer, i, 0))],
            out_specs=[pl.BlockSpec((tb, c), lambda i, me_ref: (i, me_ref[0])),
                       pl.BlockSpec((SUBLANES, LANES), lambda i, me_ref: (0, 0))]),
        out_shape=[jax.ShapeDtypeStruct((r, N_DEV * c), MXU_DTYPE), jax.ShapeDtypeStruct((SUBLANES, LANES), F32)],
        compiler_params=_params(("arbitrary",)),
    )(me, stacked)


def _gather_prepare(shards, by_cols, *, name):
    me = _block_id(lax.axis_index("x"), lax.axis_index("y"), lax.axis_index("c"))
    lands, early, widths = [], [], []
    for i, (s, cols) in enumerate(zip(shards, by_cols)):
        if cols:
            land, token = _place_in_window(*s, name=f"{name}_place{i}")
            lands.append(land)
            early.append(token)
            widths.append(s[0].shape[-1])
        else:
            lands.append(lax.dynamic_update_index_in_dim(lax.empty((N_DEV, *s.shape), s.dtype), s, me, 0))
            early.append(s)
            widths.append(s.shape[-1])
    return lands, widths, list(by_cols), early


def _gather_start(shards, *, name, after=None, by_cols=None, prepared=None):
    if prepared is None:
        prepared = _gather_prepare(shards, by_cols if by_cols is not None else [False] * len(shards), name=name)
    lands, widths, by_cols, _ = prepared
    n = len(lands)

    def body(refs, _, send, recv):
        x, y, core = lax.axis_index("x"), lax.axis_index("y"), lax.axis_index("c")
        targets = [(x, y, 1 - core), (1 - x, y, core), (x, 1 - y, core), (1 - x, 1 - y, core)]
        for t in range(n):
            mine = _land_block(refs[t], _block_id(x, y, core), by_cols[t], widths[t])
            for k, to in enumerate(targets):
                _remote(mine, mine, send, recv, 4 * t + k, to).start()

    sems, thru, token = _split_call(name, lands, [], 4 * n, body, after=after)
    return {"n": n, "sems1": sems, "arrays": thru, "token": token, "by_cols": by_cols, "widths": widths}


def _gather_pass_on(st, *, name, after):
    n = st["n"]

    def body(refs, sems, send, recv):
        send1, recv1 = sems
        x, y, core = lax.axis_index("x"), lax.axis_index("y"), lax.axis_index("c")
        chips = [(1 - x, y), (x, 1 - y), (1 - x, 1 - y)]
        for t in range(n):
            for j, (px, py) in enumerate(chips):
                block = _land_block(refs[t], _block_id(px, py, core), st["by_cols"][t], st["widths"][t])
                _remote(block, block, send1, recv1, 4 * t + 1 + j, (x, y, 1 - core)).wait_recv()
                _remote(block, block, send, recv, 3 * t + j, (x, y, 1 - core)).start()

    sems2, thru, token = _split_call(name, st["arrays"], list(st["sems1"]), 3 * n, body, after=after)
    return {**st, "sems2": sems2, "arrays": thru, "token": token}


def _gather_finish(st, *, name, after, then_start=None):
    n = st["n"]
    n2 = 0 if then_start is None else len(then_start[0])

    def body(refs, sems, new_send=None, new_recv=None):
        send1, recv1, send2, recv2 = sems
        x, y, core = lax.axis_index("x"), lax.axis_index("y"), lax.axis_index("c")
        sibling = (x, y, 1 - core)
        chips = [(1 - x, y), (x, 1 - y), (1 - x, 1 - y)]
        for t in range(n):
            place = lambda px, py, pc, t=t: _land_block(refs[t], _block_id(px, py, pc), st["by_cols"][t],
                                                        st["widths"][t])
            mine = place(x, y, core)
            for k in range(4):
                _remote(mine, mine, send1, recv1, 4 * t + k, sibling).wait_send()
            _remote(mine, place(x, y, 1 - core), send1, recv1, 4 * t, sibling).wait_recv()
            for j, (px, py) in enumerate(chips):
                _remote(place(px, py, core), place(px, py, core), send2, recv2, 3 * t + j, sibling).wait_send()
                _remote(mine, place(px, py, 1 - core), send2, recv2, 3 * t + j, sibling).wait_recv()
        if n2:
            _, widths, by_cols, _ = then_start
            targets = [sibling, (1 - x, y, core), (x, 1 - y, core), (1 - x, 1 - y, core)]
            for t in range(n2):
                mine = _land_block(refs[n + t], _block_id(x, y, core), by_cols[t], widths[t])
                for k, to in enumerate(targets):
                    _remote(mine, mine, new_send, new_recv, 4 * t + k, to).start()

    arrays = st["arrays"] + (list(then_start[0]) if n2 else [])
    sems, thru, token = _split_call(name, arrays, [*st["sems1"], *st["sems2"]], 4 * n2, body, after=after)
    if not n2:
        return thru
    _, widths, by_cols, _ = then_start
    return thru[:n], {"n": n2, "sems1": sems, "arrays": thru[n:], "token": token, "by_cols": by_cols, "widths": widths}


def _swap_start(sources, lands, plan, per_array, *, name):
    n = len(sources)

    def body(refs, _, send, recv):
        x, y, core = lax.axis_index("x"), lax.axis_index("y"), lax.axis_index("c")
        for t in range(n):
            for k in range(per_array):
                block, to = plan(x, y, core, k)
                _remote(refs[t].at[block], refs[n + t].at[k], send, recv, per_array * t + k, to).start()

    sems, thru, token = _split_call(name, sources + lands, [], per_array * n, body)
    return {"n": n, "sems": sems, "arrays": thru, "token": token, "plan": plan, "per_array": per_array}


def _swap_finish(states, *, name, after):
    many = isinstance(states, (list, tuple))
    states = list(states) if many else [states]

    def body(refs, sems, *_):
        x, y, core = lax.axis_index("x"), lax.axis_index("y"), lax.axis_index("c")
        first = 0
        for j, st in enumerate(states):
            n, plan, per_array = st["n"], st["plan"], st["per_array"]
            send, recv = sems[2 * j], sems[2 * j + 1]
            for t in range(n):
                for k in range(per_array):
                    block, to = plan(x, y, core, k)
                    cp = _remote(refs[first + t].at[block], refs[first + n + t].at[k], send, recv, per_array * t + k, to)
                    cp.wait_send()
                    cp.wait_recv()
            first += 2 * n

    arrays = [a for st in states for a in st["arrays"]]
    _, thru, _ = _split_call(name, arrays, [s for st in states for s in st["sems"]], 0, body, after=after)
    out, first = [], 0
    for st in states:
        n = st["n"]
        out.append((thru[first:first + n], thru[first + n:first + 2 * n]))
        first += 2 * n
    return out if many else out[0]


def _to_sibling(x, y, core, k):
    px, py = [(x, y), (1 - x, y), (x, 1 - y), (1 - x, 1 - y)][k]
    return _block_id(px, py, 1 - core), (x, y, 1 - core)


def _to_chip(x, y, core, k):
    px, py = [(1 - x, y), (x, 1 - y), (1 - x, 1 - y)][k]
    return k, (px, py, core)


PARAMS = {
    "sc_w_in": ((2, 1024, 3072), 2), "sc_conv_w": ((2, 3, 1024), 2), "sc_w_out": ((2, 1024, 1024), 1),
    "mla_w_dq": ((1, 1024, 384), 1), "mla_g_q": ((1, 384), None), "mla_w_uq": ((1, 384, 1536), 2),
    "mla_w_dkv": ((1, 1024, 320), 1), "mla_g_kv": ((1, 256), None), "mla_w_uk": ((1, 256, 8, 128), 1),
    "mla_w_uv": ((1, 256, 8, 128), 1), "mla_w_o": ((1, 1024, 1024), 1), "cf_w_pw1": ((1, 1024, 2048), 2),
    "cf_b_pw1": ((1, 2048), 1), "cf_dw_w": ((1, 31, 1024), 2), "cf_dw_b": ((1, 1024), 1), "cf_norm_g": ((1, 1024), 1),
    "cf_norm_b": ((1, 1024), 1), "cf_w_pw2": ((1, 1024, 1024), 1), "cf_b_pw2": ((1, 1024), 1),
    "ff_w1": ((4, 1024, 4096), 2), "ff_w2": ((4, 4096, 1024), 1), "ln_mix_g": ((4, 1024), None),
    "ln_mix_b": ((4, 1024), None), "ln_ff_g": ((4, 1024), None), "ln_ff_b": ((4, 1024), None),
}
NAMES = list(PARAMS)
BIG = ["sc_w_in", "sc_w_out", "mla_w_dq", "mla_w_uq", "mla_w_dkv", "mla_w_uk", "mla_w_uv", "mla_w_o", "cf_w_pw1",
       "cf_w_pw2", "ff_w1", "ff_w2"]
SMALL = [n for n in NAMES if n not in BIG]
SMALL_SHARDED = [n for n in SMALL if PARAMS[n][1] is not None]
SMALL_WIDTH = LANES
DEV_BLOCKED = ("sc_w_in", "cf_w_pw1", "ff_w1")
MIXER_BIG = {0: ["sc_w_in", "sc_w_out"], 1: ["mla_w_dq", "mla_w_uq", "mla_w_dkv", "mla_w_uk", "mla_w_uv", "mla_w_o"],
             2: ["cf_w_pw1", "cf_w_pw2"]}


def _layer_big(i):
    return [(n, i // N_MIXERS) for n in MIXER_BIG[i % N_MIXERS]] + [("ff_w1", i), ("ff_w2", i)]


def _shard_2d(name, a):
    return a.reshape(a.shape[0], -1)


def _shard_for_gather(name, stacked, layer):
    if name in DEV_BLOCKED:
        return stacked, layer
    return _shard_2d(name, stacked[layer]).astype(MXU_DTYPE)


def _as_operand(name, g):
    if name in DEV_BLOCKED:
        return g
    if name == "mla_w_uq":
        by_head = g.transpose(1, 0, 2)
        pieces = [by_head[:, :, :QK_NOPE], by_head[:, :, QK_NOPE:]]
        return jnp.concatenate([p.reshape(g.shape[1], -1) for p in pieces], axis=1)
    return g.reshape(-1, g.shape[2])


def _as_parts(name, grad):
    if name in DEV_BLOCKED:
        return grad
    if name == "mla_w_uq":
        rows = grad.shape[0]
        pieces = [grad[:, :Q_NOPE_COLS].reshape(rows, MLA_HEADS, QK_NOPE),
                  grad[:, Q_NOPE_COLS:].reshape(rows, MLA_HEADS, QK_ROPE)]
        return jnp.concatenate(pieces, axis=2).transpose(1, 0, 2)
    return grad.reshape(N_DEV, -1, grad.shape[1])


def _shard_shape(name):
    shape, ax = PARAMS[name]
    if ax is None:
        return shape
    return tuple(s // N_DEV if i == ax else s for i, s in enumerate(shape))


def _rows_of(shape, width, align):
    n = 1
    for s in shape:
        n *= s
    rows = -(-n // width)
    return -(-rows // align) * align


def _pack_rows(arrays, width, align, dtype, lead=0):
    segs = []
    for a in arrays:
        batch = a.shape[:lead]
        flat = a.astype(dtype).reshape(*batch, -1)
        rows = _rows_of(a.shape[lead:], width, align)
        flat = jnp.pad(flat, [(0, 0)] * lead + [(0, rows * width - flat.shape[-1])])
        segs.append(flat.reshape(*batch, rows, width))
    return jnp.concatenate(segs, axis=lead)


def _unpack_rows(buf, shapes, width, align):
    out, off = [], 0
    lead = buf.shape[:-2]
    for shape in shapes:
        n = 1
        for s in shape:
            n *= s
        rows = _rows_of(shape, width, align)
        seg = buf[..., off:off + rows, :].reshape(*lead, rows * width)[..., :n]
        out.append(seg.reshape(*lead, *shape))
        off += rows
    return out


def _merge_shards(stacked, ax):
    moved = jnp.moveaxis(stacked, 0, ax)
    shape = moved.shape
    return moved.reshape(*shape[:ax], shape[ax] * shape[ax + 1], *shape[ax + 2:])


def _split_shards(full, ax):
    shape = full.shape
    split = full.reshape(*shape[:ax], N_DEV, shape[ax] // N_DEV, *shape[ax + 1:])
    return jnp.moveaxis(split, ax, 0)


def _adam_math(wv, gv, mv, vv):
    m_new = ADAM_B1 * mv + (1.0 - ADAM_B1) * gv
    v_new = ADAM_B2 * vv + (1.0 - ADAM_B2) * (gv * gv)
    m_hat = m_new / (1.0 - ADAM_B1 ** ADAM_STEP)
    v_hat = v_new / (1.0 - ADAM_B2 ** ADAM_STEP)
    delta = -ADAM_LR * (m_hat / (jnp.sqrt(v_hat) + ADAM_EPS) + ADAM_WD * wv)
    return delta, m_new, v_new


def _adamw(w, g, m, v, *, name):
    shape = w.shape
    c = shape[-1]

    def fn(rows, bc):
        return _adam_math(*rows), ()

    flat = [a.reshape(-1, c) for a in (w, g, m, v)]
    outs = _rowwise(fn, flat, [], [(c, F32)] * 3, [], name=name, tb_cap=512)
    return tuple(o.reshape(shape) for o in outs)


def _adamw_reduced(w, m, v, layer_grads, *, name, after=()):
    n_l, r, c = w.shape
    tb = _row_tile(r, 256 if c > 512 else 512)
    nb = r // tb

    def body(*refs):
        w_ref, m_ref, v_ref = refs[:3]
        g_refs = refs[3:3 + 2 * n_l]
        g_out, d_out, m_out, v_out = refs[3 + 2 * n_l + len(after):]
        layer = pl.program_id(0)
        for l in range(n_l):
            @pl.when(layer == l)
            def _(l=l):
                own, arr = g_refs[2 * l], g_refs[2 * l + 1]
                g = own[...] + arr[0].astype(F32)
                g = g + arr[1].astype(F32)
                g = g + arr[2].astype(F32)
                delta, m_new, v_new = _adam_math(w_ref[...], g, m_ref[...], v_ref[...])
                g_out[...] = g
                d_out[...] = delta
                m_out[...] = m_new
                v_out[...] = v_new

    stacked = pl.BlockSpec((tb, c), lambda l, i: (l * nb + i, 0))
    in_specs = [stacked] * 3
    operands = [a.reshape(n_l * r, c) for a in (w, m, v)]
    for l, (own, arr) in enumerate(layer_grads):
        in_specs.append(pl.BlockSpec((tb, c), lambda ll, i, l=l: (jnp.where(ll == l, i, 0), 0)))
        in_specs.append(pl.BlockSpec((3, tb, c), lambda ll, i, l=l: (0, jnp.where(ll == l, i, 0), 0)))
        operands += [own, arr]
    in_specs += [ANY] * len(after)
    operands += list(after)
    outs = pl.pallas_call(
        body,
        name=name,
        grid=(n_l, nb),
        in_specs=in_specs,
        out_specs=[stacked] * 4,
        out_shape=[jax.ShapeDtypeStruct((n_l * r, c), F32)] * 4,
        compiler_params=_params(("arbitrary", "arbitrary")),
    )(*operands)
    return tuple(o.reshape(n_l, r, c) for o in outs)


def _train_step(x, target, weights, m_state, v_state):
    me = _block_id(lax.axis_index("x"), lax.axis_index("y"), lax.axis_index("c"))

    groups = [(kind, i) for i in range(DEPTH) for kind in ("mix", "ff")]
    forward = [groups[0], ("ff_up", 0), ("ff_down", 0)] + groups[2:]

    def group_names(g):
        kind, i = g
        if kind == "mix":
            return [(n, i // N_MIXERS) for n in MIXER_BIG[i % N_MIXERS]]
        return [(n, i) for n in {"ff": ["ff_w1", "ff_w2"], "ff_up": ["ff_w1"], "ff_down": ["ff_w2"]}[kind]]

    def shards_of(g):
        return [_shard_for_gather(n, weights[n], j) for n, j in group_names(g)]

    whole = lambda g: [n in DEV_BLOCKED for n, _ in group_names(g)]

    small_shapes = [_shard_shape(n) for n in SMALL_SHARDED]
    gather = {0: _gather_start([_pack_rows([weights[n] for n in SMALL_SHARDED], SMALL_WIDTH, SUBLANES, F32)]
                               + shards_of(groups[0]), name="gather_g0_start", by_cols=[False] + whole(groups[0]))}
    cur, cur_m = x[0], x[0]
    prepared = {k: _gather_prepare(shards_of(forward[k]), whole(forward[k]), name=f"gather_g{k}_start")
                for k in range(1, len(forward))}
    early = [a for k in prepared for a in prepared[k][3]]
    gather[0] = _gather_pass_on(gather[0], name="gather_g0_pass", after=early)
    first = _gather_finish(gather[0], name="gather_g0_finish", after=gather[0]["token"])
    small_all, first_matrices = first[0], first[1:]
    small_full = {n: weights[n] for n in SMALL if PARAMS[n][1] is None}
    for n, stacked in zip(SMALL_SHARDED, _unpack_rows(small_all, small_shapes, SMALL_WIDTH, SUBLANES)):
        small_full[n] = _merge_shards(stacked, PARAMS[n][1])

    def small_weights(i):
        wl = {}
        for n in SMALL:
            if n.startswith("ln_"):
                wl[n] = small_full[n][i:i + 1]
            elif n.startswith(("sc_", "mla_", "cf_")[i % N_MIXERS]):
                v = small_full[n][i // N_MIXERS]
                wl[n] = v if v.ndim == 2 else v.reshape(1, -1)
        return wl

    layers = [small_weights(i) for i in range(DEPTH)]

    def install(g, gathered):
        for (n, _), a in zip(group_names(g), gathered):
            layers[g[1]][n] = _as_operand(n, a)

    install(forward[0], first_matrices)
    gather[1] = _gather_start(None, name="gather_g1_start", after=gather[0]["token"], prepared=prepared[1])
    gather[2] = _gather_start(None, name="gather_g2_start", after=gather[1]["token"], prepared=prepared[2])
    saved = {}
    hidden = None
    for k, g in enumerate(forward):
        kind, i = g
        tokens = [gather[1]["token"]] if k == 0 else []
        if k + 2 < len(forward):
            tokens.append(gather[k + 2]["token"])

        def pass_on(y, k=k):
            if k + 1 == len(forward):
                return None
            gather[k + 1] = _gather_pass_on(gather[k + 1], name=f"gather_g{k + 1}_pass", after=y)
            return gather[k + 1]["token"]

        if k + 1 == len(forward):
            loss_local, dy, saved[g] = _ff_fwd(i, cur, cur_m, layers[i], after=tokens, loss_target=target[0])
            break
        if kind == "ff_up":
            hidden = _mlp_up(cur_m, layers[i]["ff_w1"], f"l{i}_ff", after=tokens)
            behind, done = _hooked(hidden, pass_on), hidden
        elif kind == "ff_down":
            cur, cur_m, saved[("ff", i)] = _ff_fwd(i, cur, cur_m, layers[i], hook=pass_on, after=[*tokens, *behind],
                                                   act=hidden)
            done = cur
        else:
            phase = _mix_fwd if kind == "mix" else _ff_fwd
            cur, cur_m, saved[g] = phase(i, cur, cur_m, layers[i], hook=pass_on, after=tokens)
            done = cur
        if k + 3 < len(forward):
            gathered, gather[k + 3] = _gather_finish(gather[k + 1], name=f"gather_g{k + 1}_finish", after=done,
                                                     then_start=prepared[k + 3])
        else:
            gathered = _gather_finish(gather[k + 1], name=f"gather_g{k + 1}_finish", after=done)
        install(forward[k + 1], gathered)

    loss = lax.psum(loss_local, MESH_AXES)

    reduced = {n: [None] * PARAMS[n][0][0] for n in BIG}
    small_grads = {n: [None] * PARAMS[n][0][0] for n in SMALL}
    reduce = {}

    def to_chips(p, after):
        st = reduce[p]
        parts, got = _swap_finish(st["sibling"], name=f"reduce_p{p}_sibling_finish", after=after)
        sums = _chip_partials_group(parts, got, name=f"reduce_p{p}_chipsum")
        st["own"] = [own for own, _ in sums]
        others = [o for _, o in sums]
        st["chips"] = _swap_start(others, [lax.empty(o.shape, o.dtype) for o in others], _to_chip, 3,
                                  name=f"reduce_p{p}_chips_start")
        return st["chips"]["token"]

    def arrived(phases, after):
        done = _swap_finish([reduce[p]["chips"] for p in phases], name=f"reduce_p{phases[0]}_chips_finish", after=after)
        for p, (_, got) in zip(phases, done):
            for (n, j), own, arr in zip(reduce[p]["names"], reduce[p]["own"], got):
                reduced[n][j] = (own, arr)

    terms = [(1.0, dy)]
    back = list(reversed(groups))
    for p, g in enumerate(back):
        kind, i = g

        def after_first_kernel(dz, p=p):
            return to_chips(p - 1, dz) if p >= 1 else None

        phase = _mix_bwd if kind == "mix" else _ff_bwd
        tokens = [reduce[p - 1]["sibling"]["token"]] if p >= 1 else []
        terms, grads = phase(i, terms, saved[g], layers[i], hook=after_first_kernel, after=tokens)
        for n in SMALL:
            if n in grads:
                small_grads[n][i if n.startswith("ln_") else i // N_MIXERS] = grads[n]
        names = group_names(g)
        parts = [_as_parts(n, grads[n]) for n, _ in names]
        reduce[p] = {"names": names, "sibling": _swap_start(
            parts, [lax.empty((4, *a.shape[1:]), a.dtype) for a in parts], _to_sibling, 4,
            name=f"reduce_p{p}_sibling_start")}
    (_, grad_x), = terms
    last = len(back) - 1
    small_mine = _pack_rows([jnp.stack(small_grads[n]).reshape(PARAMS[n][0]) for n in SMALL], SMALL_WIDTH, SUBLANES, F32)
    small_gather = _gather_start([small_mine], name="gather_small_grads_start")
    tail = to_chips(last, small_gather["token"])
    arrived(list(range(last)), tail)

    grad, delta, new_m, new_v = {}, {}, {}, {}

    def adamw_matrix(n, after):
        shape = weights[n].shape
        view = lambda a: a.reshape(shape[0], shape[1], -1)
        outs = _adamw_reduced(view(weights[n]), view(m_state[n]), view(v_state[n]), reduced[n], name=f"adamw_{n}",
                              after=after)
        grad[n], delta[n], new_m[n], new_v[n] = [o.reshape(shape) for o in outs]
        return outs[0]

    done = tail
    for n in BIG:
        if n not in MIXER_BIG[0]:
            done = adamw_matrix(n, [done])
    small_gather = _gather_pass_on(small_gather, name="gather_small_grads_pass", after=done)
    small_all = _gather_finish(small_gather, name="gather_small_grads_finish", after=small_gather["token"])[0]
    small_sum = _sum_blocks(None, small_all, name="sum_small_grads")
    for n, g in zip(SMALL, _unpack_rows(small_sum, [PARAMS[n][0] for n in SMALL], SMALL_WIDTH, SUBLANES)):
        ax = PARAMS[n][1]
        grad[n] = g if ax is None else lax.dynamic_index_in_dim(_split_shards(g, ax), me, axis=0, keepdims=False)
    small_shards = [_shard_shape(n) for n in SMALL]
    packed = [_pack_rows([src[n] for n in SMALL], SMALL_WIDTH, SUBLANES, F32) for src in (weights, grad, m_state, v_state)]
    small_out = _adamw(*packed, name="adamw_small")
    for outs, dst in zip(small_out, (delta, new_m, new_v)):
        for n, a in zip(SMALL, _unpack_rows(outs, small_shards, SMALL_WIDTH, SUBLANES)):
            dst[n] = a
    arrived([last], small_sum)
    for n in MIXER_BIG[0]:
        adamw_matrix(n, [])
    return (loss, grad_x[None], *[grad[n] for n in NAMES], *[delta[n] for n in NAMES],
            *[new_m[n] for n in NAMES], *[new_v[n] for n in NAMES])


def kernel(x, sc_w_in, sc_conv_w, sc_w_out, mla_w_dq, mla_g_q, mla_w_uq, mla_w_dkv, mla_g_kv, mla_w_uk, mla_w_uv, mla_w_o, cf_w_pw1, cf_b_pw1, cf_dw_w, cf_dw_b, cf_norm_g, cf_norm_b, cf_w_pw2, cf_b_pw2, ff_w1, ff_w2, ln_mix_g, ln_mix_b, ln_ff_g, ln_ff_b, loss_target, m_sc_w_in, m_sc_conv_w, m_sc_w_out, m_mla_w_dq, m_mla_g_q, m_mla_w_uq, m_mla_w_dkv, m_mla_g_kv, m_mla_w_uk, m_mla_w_uv, m_mla_w_o, m_cf_w_pw1, m_cf_b_pw1, m_cf_dw_w, m_cf_dw_b, m_cf_norm_g, m_cf_norm_b, m_cf_w_pw2, m_cf_b_pw2, m_ff_w1, m_ff_w2, m_ln_mix_g, m_ln_mix_b, m_ln_ff_g, m_ln_ff_b, v_sc_w_in, v_sc_conv_w, v_sc_w_out, v_mla_w_dq, v_mla_g_q, v_mla_w_uq, v_mla_w_dkv, v_mla_g_kv, v_mla_w_uk, v_mla_w_uv, v_mla_w_o, v_cf_w_pw1, v_cf_b_pw1, v_cf_dw_w, v_cf_dw_b, v_cf_norm_g, v_cf_norm_b, v_cf_w_pw2, v_cf_b_pw2, v_ff_w1, v_ff_w2, v_ln_mix_g, v_ln_mix_b, v_ln_ff_g, v_ln_ff_b):
    w_list = (sc_w_in, sc_conv_w, sc_w_out, mla_w_dq, mla_g_q, mla_w_uq, mla_w_dkv, mla_g_kv, mla_w_uk, mla_w_uv, mla_w_o,
              cf_w_pw1, cf_b_pw1, cf_dw_w, cf_dw_b, cf_norm_g, cf_norm_b, cf_w_pw2, cf_b_pw2, ff_w1, ff_w2, ln_mix_g,
              ln_mix_b, ln_ff_g, ln_ff_b)
    m_list = (m_sc_w_in, m_sc_conv_w, m_sc_w_out, m_mla_w_dq, m_mla_g_q, m_mla_w_uq, m_mla_w_dkv, m_mla_g_kv, m_mla_w_uk,
              m_mla_w_uv, m_mla_w_o, m_cf_w_pw1, m_cf_b_pw1, m_cf_dw_w, m_cf_dw_b, m_cf_norm_g, m_cf_norm_b, m_cf_w_pw2,
              m_cf_b_pw2, m_ff_w1, m_ff_w2, m_ln_mix_g, m_ln_mix_b, m_ln_ff_g, m_ln_ff_b)
    v_list = (v_sc_w_in, v_sc_conv_w, v_sc_w_out, v_mla_w_dq, v_mla_g_q, v_mla_w_uq, v_mla_w_dkv, v_mla_g_kv, v_mla_w_uk,
              v_mla_w_uv, v_mla_w_o, v_cf_w_pw1, v_cf_b_pw1, v_cf_dw_w, v_cf_dw_b, v_cf_norm_g, v_cf_norm_b, v_cf_w_pw2,
              v_cf_b_pw2, v_ff_w1, v_ff_w2, v_ln_mix_g, v_ln_mix_b, v_ln_ff_g, v_ln_ff_b)
    return _train_step(x, loss_target, dict(zip(NAMES, w_list)), dict(zip(NAMES, m_list)), dict(zip(NAMES, v_list)))
```

```python
import jax
import jax.numpy as jnp
from jax import lax
from jax.experimental import pallas as pl
from jax.experimental.pallas import tpu as pltpu

F32 = jnp.float32
MXU_DTYPE = jnp.bfloat16

N_DEV = 8
D_MODEL = 1024
DEPTH = 4
N_MIXERS = 3
CHUNK = 64
ALPHA = (2.0 * DEPTH) ** 0.25
LN_EPS = 1e-5
RMS_EPS = 1e-6
MLA_HEADS = 8
QK_NOPE = 128
QK_ROPE = 64
V_HEAD = 128
Q_LORA = 384
KV_LORA = 256
ROPE_THETA = 10000.0
ADAM_LR = 0.001
ADAM_B1 = 0.9
ADAM_B2 = 0.999
ADAM_EPS = 1e-08
ADAM_WD = 0.01
ADAM_STEP = 10

LANES = 128
SUBLANES = 8
BF16_ROWS = 16
VMEM_LIMIT = 56 * 1024 * 1024
MM_VMEM_BUDGET = 36 * 1024 * 1024
CONV_PAD = 32
MASK_VALUE = -1e30

MESH_AXES = ("x", "y", "c")


def _params(semantics):
    return pltpu.CompilerParams(dimension_semantics=semantics, vmem_limit_bytes=VMEM_LIMIT)


def _tile(n, cap):
    if n <= cap:
        return n
    t = cap - cap % LANES
    while t >= LANES:
        if n % t == 0:
            return t
        t -= LANES
    raise ValueError(f"no tile for {n} under {cap}")


def _row_tile(rows, cap):
    if rows <= cap:
        return rows
    t = cap - cap % SUBLANES
    while t >= SUBLANES:
        if rows % t == 0:
            return t
        t -= SUBLANES
    raise ValueError(f"no row tile for {rows} under {cap}")


def _mm(a, b, *, name, ta=False, tb=False, b_dev=False, out_dev=False, out_dtypes=(F32,), epilogue=None,
        tile_extras=(), col_extras=(), after=(), tm_cap=2048, tn_cap=1024, tk_cap=4096):
    m, k = (a.shape[1], a.shape[0]) if ta else a.shape
    if b_dev:
        nd, b_rows, ns = b.shape
        n, kb = (b_rows, nd * ns) if tb else (nd * ns, b_rows)
    else:
        n, kb = (b.shape[0], b.shape[1]) if tb else (b.shape[1], b.shape[0])
    assert k == kb, (a.shape, b.shape, ta, tb)
    if ta or tile_extras or (b_dev and tb):
        tm_cap = min(tm_cap, 1024)
    tk = ns if (b_dev and tb) else _tile(k, tk_cap)
    nk = k // tk
    fixed_tn = ns if (b_dev and not tb) else (n // N_DEV if out_dev else None)
    if out_dev:
        assert not tile_extras and not col_extras and n % N_DEV == 0
    out_bytes = sum(jnp.dtype(dt).itemsize for dt in out_dtypes) + sum(e.dtype.itemsize for e in tile_extras)
    tm, tn = None, None
    for cand_m, cand_n in ((tm_cap, tn_cap), (tm_cap, tn_cap // 2), (tm_cap // 2, tn_cap), (tm_cap // 2, tn_cap // 2),
                           (tm_cap // 4, tn_cap // 2), (tm_cap // 4, tn_cap // 4)):
        tm, tn = _tile(m, cand_m), (fixed_tn if fixed_tn is not None else _tile(n, cand_n))
        need = 2 * (tm * tk * a.dtype.itemsize + tk * tn * b.dtype.itemsize) + 2 * tm * tn * out_bytes
        need += tm * tn * 4 * (2 if nk > 1 else 1)
        if need <= MM_VMEM_BUDGET:
            break
    n_te, n_ce, n_out = len(tile_extras), len(col_extras), len(out_dtypes)
    dims = (((0 if ta else 1,), (1 if tb else 0,)), ((), ()))

    def body(*refs):
        a_ref, b_ref = refs[0], refs[1]
        te_refs = refs[2:2 + n_te]
        ce_refs = refs[2 + n_te:2 + n_te + n_ce]
        first_out = 2 + n_te + n_ce + len(after)
        out_refs = refs[first_out:first_out + n_out]

        def finish(acc):
            if epilogue is None:
                outs = (acc,)
            else:
                outs = epilogue(acc, *[r[...] for r in te_refs], *[r[...] for r in ce_refs])
            for o_ref, o in zip(out_refs, outs):
                o_ref[...] = o.astype(o_ref.dtype)

        def product():
            return lax.dot_general(a_ref[...].astype(MXU_DTYPE), b_ref[...].astype(MXU_DTYPE), dims,
                                   preferred_element_type=F32)

        if nk == 1:
            finish(product())
        else:
            acc_ref = refs[-1]
            kk = pl.program_id(2)

            @pl.when(kk == 0)
            def _():
                acc_ref[...] = jnp.zeros_like(acc_ref)

            acc_ref[...] += product()

            @pl.when(kk == nk - 1)
            def _():
                finish(acc_ref[...])

    a_spec = pl.BlockSpec((tk, tm), lambda i, j, kk: (kk, i)) if ta else pl.BlockSpec((tm, tk), lambda i, j, kk: (i, kk))
    if b_dev and tb:
        b_spec = pl.BlockSpec((None, tn, tk), lambda i, j, kk: (kk, j, 0))
    elif b_dev:
        b_spec = pl.BlockSpec((None, tk, tn), lambda i, j, kk: (j, kk, 0))
    elif tb:
        b_spec = pl.BlockSpec((tn, tk), lambda i, j, kk: (j, kk))
    else:
        b_spec = pl.BlockSpec((tk, tn), lambda i, j, kk: (kk, j))
    tile_spec = pl.BlockSpec((tm, tn), lambda i, j, kk: (i, j))
    col_spec = pl.BlockSpec((1, tn), lambda i, j, kk: (0, j))
    if out_dev:
        out_spec = pl.BlockSpec((None, tm, tn), lambda i, j, kk: (j, i, 0))
        out_shape = [jax.ShapeDtypeStruct((N_DEV, m, tn), dt) for dt in out_dtypes]
    else:
        out_spec = tile_spec
        out_shape = [jax.ShapeDtypeStruct((m, n), dt) for dt in out_dtypes]
    outs = pl.pallas_call(
        body,
        name=name,
        grid=(m // tm, n // tn, nk),
        in_specs=[a_spec, b_spec] + [tile_spec] * n_te + [col_spec] * n_ce + [ANY] * len(after),
        out_specs=[out_spec] * n_out,
        out_shape=out_shape,
        scratch_shapes=[pltpu.VMEM((tm, tn), F32)] if nk > 1 else [],
        compiler_params=_params(("parallel", "parallel", "arbitrary")),
    )(a, b, *tile_extras, *col_extras, *after)
    return outs[0] if n_out == 1 else tuple(outs)


def _rowwise(fn, rows, bcast, out_rows, out_accs, *, name, tb_cap=256, after=()):
    t = rows[0].shape[0]
    tb = _row_tile(t, tb_cap)
    n_r, n_b, n_o, n_a = len(rows), len(bcast), len(out_rows), len(out_accs)

    def body(*refs):
        r_refs = refs[:n_r]
        b_refs = refs[n_r:n_r + n_b]
        first_out = n_r + n_b + len(after)
        o_refs = refs[first_out:first_out + n_o]
        a_refs = refs[first_out + n_o:]
        outs, accs = fn([r[...] for r in r_refs], [r[...] for r in b_refs])
        for o_ref, o in zip(o_refs, outs):
            o_ref[...] = o.astype(o_ref.dtype)
        if n_a:
            i = pl.program_id(0)

            @pl.when(i == 0)
            def _():
                for a_ref, acc in zip(a_refs, accs):
                    a_ref[...] = acc

            @pl.when(i > 0)
            def _():
                for a_ref, acc in zip(a_refs, accs):
                    a_ref[...] += acc

    in_specs = [pl.BlockSpec((tb, r.shape[1]), lambda i: (i, 0)) for r in rows]
    in_specs += [pl.BlockSpec(b.shape, lambda i: (0, 0)) for b in bcast]
    in_specs += [ANY] * len(after)
    out_specs = [pl.BlockSpec((tb, c), lambda i: (i, 0)) for c, _ in out_rows]
    out_specs += [pl.BlockSpec(s, lambda i: (0, 0)) for s in out_accs]
    out_shape = [jax.ShapeDtypeStruct((t, c), dt) for c, dt in out_rows]
    out_shape += [jax.ShapeDtypeStruct(s, F32) for s in out_accs]
    outs = pl.pallas_call(
        body,
        name=name,
        grid=(t // tb,),
        in_specs=in_specs,
        out_specs=out_specs,
        out_shape=out_shape,
        compiler_params=_params(("arbitrary",)),
    )(*rows, *bcast, *after)
    return tuple(outs)


def _colsum(v):
    return jnp.sum(v, axis=0, keepdims=True)


def _sigmoid(v):
    return 1.0 / (1.0 + jnp.exp(-v))


def _ln_fwd(x_res, y_sub, g, b, *, name, after=()):
    d = x_res.shape[1]

    def fn(rows, bc):
        z = ALPHA * rows[0] + rows[1]
        mu = jnp.mean(z, axis=-1, keepdims=True)
        zc = z - mu
        var = jnp.mean(zc * zc, axis=-1, keepdims=True)
        rstd = lax.rsqrt(var + LN_EPS)
        xhat = zc * rstd
        out = xhat * bc[0] + bc[1]
        return (out, out, xhat, rstd), ()

    return _rowwise(fn, [x_res, y_sub], [g, b], [(d, F32), (d, MXU_DTYPE), (d, F32), (1, F32)], [], name=name,
                    after=after, tb_cap=512)


def _ln_fwd_loss(x_res, y_sub, g, b, target, *, name, after=()):
    d = x_res.shape[1]

    def fn(rows, bc):
        z = ALPHA * rows[0] + rows[1]
        mu = jnp.mean(z, axis=-1, keepdims=True)
        zc = z - mu
        var = jnp.mean(zc * zc, axis=-1, keepdims=True)
        rstd = lax.rsqrt(var + LN_EPS)
        xhat = zc * rstd
        err = xhat * bc[0] + bc[1] - rows[2]
        return (xhat, rstd, err * (1.0 / d)), (_colsum(err * err),)

    return _rowwise(fn, [x_res, y_sub, target], [g, b], [(d, F32), (1, F32), (d, F32)], [(1, d)], name=name,
                    after=after, tb_cap=512)


def _ln_bwd(terms, xhat, rstd, g, *, name, after=()):
    d = xhat.shape[1]
    coefs = [c for c, _ in terms]

    def fn(rows, bc):
        xh, rs = rows[0], rows[1]
        dout = None
        for c, v in zip(coefs, rows[2:]):
            tv = v if c == 1.0 else c * v
            dout = tv if dout is None else dout + tv
        dxh = dout * bc[0]
        m1 = jnp.mean(dxh, axis=-1, keepdims=True)
        m2 = jnp.mean(dxh * xh, axis=-1, keepdims=True)
        dz = rs * (dxh - m1 - xh * m2)
        return (dz, dz), (_colsum(dout * xh), _colsum(dout), _colsum(dz))

    return _rowwise(fn, [xhat, rstd] + [v for _, v in terms], [g], [(d, F32), (d, MXU_DTYPE)],
                    [(1, d), (1, d), (1, d)], name=name, after=after, tb_cap=512)


def _rms_fwd(x, g, *, name):
    c = x.shape[1]

    def fn(rows, bc):
        v = rows[0]
        r = lax.rsqrt(jnp.mean(v * v, axis=-1, keepdims=True) + RMS_EPS)
        return (v * r * bc[0],), ()

    return _rowwise(fn, [x], [g], [(c, MXU_DTYPE)], [], name=name)[0]


def _rms_bwd(x, g, dy, *, name):
    c = x.shape[1]

    def fn(rows, bc):
        v, dyv = rows
        r = lax.rsqrt(jnp.mean(v * v, axis=-1, keepdims=True) + RMS_EPS)
        xh = v * r
        dxh = dyv * bc[0]
        dx = r * (dxh - xh * jnp.mean(dxh * xh, axis=-1, keepdims=True))
        return (dx,), (_colsum(dyv * xh),)

    return _rowwise(fn, [x, dy], [g], [(c, F32)], [(1, c)], name=name)


def _conv_chunk(t):
    return _row_tile(t, 256)


ATTN_SCALE = (QK_NOPE + QK_ROPE) ** -0.5
_NT = (((1,), (1,)), ((), ()))
_TN = (((0,), (0,)), ((), ()))
_NN = (((1,), (0,)), ((), ()))


def _dot(a, b, dims):
    return lax.dot_general(a, b, dims, preferred_element_type=F32)


def _block_scores(q_cat, kn_ref, kp_ref, lo, hi, diagonal):
    k_cat = jnp.concatenate([kn_ref[lo:hi, :], kp_ref[lo:hi, :]], axis=1)
    s = _dot(q_cat, k_cat, _NT) * ATTN_SCALE
    if diagonal:
        shift = CHUNK.bit_length() - 1
        rows = lax.broadcasted_iota(jnp.int32, s.shape, 0)
        cols = lax.broadcasted_iota(jnp.int32, s.shape, 1)
        s = jnp.where(jnp.right_shift(cols, shift) <= jnp.right_shift(rows, shift), s, MASK_VALUE)
    return s, k_cat


def _attn_fwd(qn, qp, kv, kp, *, name):
    h, t, _ = qp.shape
    tq = _row_tile(t, 256)
    assert tq % CHUNK == 0

    def body(qn_ref, qp_ref, kn_ref, kp_ref, v_ref, o_ref, of_ref, lse_ref):
        for r0 in range(0, t, tq):
            r1 = r0 + tq
            q_cat = jnp.concatenate([qn_ref[r0:r1, :].astype(MXU_DTYPE), qp_ref[r0:r1, :]], axis=1)
            s_d, _ = _block_scores(q_cat, kn_ref, kp_ref, r0, r1, True)
            m = jnp.max(s_d, axis=-1, keepdims=True)
            if r0:
                s_b, _ = _block_scores(q_cat, kn_ref, kp_ref, 0, r0, False)
                m = jnp.maximum(m, jnp.max(s_b, axis=-1, keepdims=True))
            p_d = jnp.exp(s_d - m)
            l = jnp.sum(p_d, axis=-1, keepdims=True)
            acc = _dot(p_d.astype(MXU_DTYPE), v_ref[r0:r1, :], _NN)
            if r0:
                p_b = jnp.exp(s_b - m)
                l = l + jnp.sum(p_b, axis=-1, keepdims=True)
                acc = acc + _dot(p_b.astype(MXU_DTYPE), v_ref[0:r0, :], _NN)
            o = acc / l
            o_ref[r0:r1, :] = o.astype(o_ref.dtype)
            of_ref[r0:r1, :] = o
            lse_ref[r0:r1, :] = m + jnp.log(l)

    blk = lambda w: pl.BlockSpec((None, t, w), lambda hh: (hh, 0, 0))
    cols = pl.BlockSpec((t, QK_NOPE), lambda hh: (0, hh))
    keys = pl.BlockSpec((t, QK_NOPE), lambda hh: (0, 2 * hh))
    values = pl.BlockSpec((t, V_HEAD), lambda hh: (0, 2 * hh + 1))
    return pl.pallas_call(
        body,
        name=name,
        grid=(h,),
        in_specs=[cols, blk(QK_ROPE), keys, pl.BlockSpec((t, QK_ROPE), lambda hh: (0, 0)), values],
        out_specs=[cols, cols, blk(1)],
        out_shape=[jax.ShapeDtypeStruct((t, h * V_HEAD), MXU_DTYPE), jax.ShapeDtypeStruct((t, h * V_HEAD), F32),
                   jax.ShapeDtypeStruct((h, t, 1), F32)],
        compiler_params=_params(("parallel",)),
    )(qn, qp, kv, kp, kv)


def _attn_bwd(qn, qp, kv, kp, do, o, lse, *, name):
    h, t, _ = qp.shape
    tq = _row_tile(t, 256)

    def body(qn_ref, qp_ref, kn_ref, kp_ref, v_ref, do_ref, o_ref, lse_ref, dqn_ref, dqp_ref, dkv_ref, dkp_ref):
        dkv_ref[...] = jnp.zeros_like(dkv_ref)

        @pl.when(pl.program_id(0) == 0)
        def _():
            dkp_ref[...] = jnp.zeros_like(dkp_ref)

        for r0 in range(0, t, tq):
            r1 = r0 + tq
            q_cat = jnp.concatenate([qn_ref[r0:r1, :].astype(MXU_DTYPE), qp_ref[r0:r1, :]], axis=1)
            dov, lse = do_ref[r0:r1, :], lse_ref[r0:r1, :]
            delta = jnp.sum(dov.astype(F32) * o_ref[r0:r1, :], axis=-1, keepdims=True)

            def piece(lo, hi, diagonal):
                s, k_cat = _block_scores(q_cat, kn_ref, kp_ref, lo, hi, diagonal)
                p = jnp.exp(s - lse)
                dp = _dot(dov, v_ref[lo:hi, :], _NT)
                ds_m = (p * (dp - delta) * ATTN_SCALE).astype(MXU_DTYPE)
                dk_cat = _dot(ds_m, q_cat, _TN)
                dkv_ref[lo:hi, :QK_NOPE] += dk_cat[:, :QK_NOPE]
                dkp_ref[lo:hi, :] += dk_cat[:, QK_NOPE:]
                dkv_ref[lo:hi, QK_NOPE:] += _dot(p.astype(MXU_DTYPE), dov, _TN)
                return _dot(ds_m, k_cat, _NN)

            dq_cat = piece(r0, r1, True)
            if r0:
                dq_cat = dq_cat + piece(0, r0, False)
            dqn_ref[r0:r1, :] = dq_cat[:, :QK_NOPE]
            dqp_ref[r0:r1, :] = dq_cat[:, QK_NOPE:]

    blk = lambda w: pl.BlockSpec((None, t, w), lambda hh: (hh, 0, 0))
    kpblk = pl.BlockSpec((t, QK_ROPE), lambda hh: (0, 0))
    cols = pl.BlockSpec((t, QK_NOPE), lambda hh: (0, hh))
    keys = pl.BlockSpec((t, QK_NOPE), lambda hh: (0, 2 * hh))
    values = pl.BlockSpec((t, V_HEAD), lambda hh: (0, 2 * hh + 1))
    both = pl.BlockSpec((t, QK_NOPE + V_HEAD), lambda hh: (0, hh))
    return pl.pallas_call(
        body,
        name=name,
        grid=(h,),
        in_specs=[cols, blk(QK_ROPE), keys, kpblk, values, cols, cols, blk(1)],
        out_specs=[cols, blk(QK_ROPE), both, kpblk],
        out_shape=[jax.ShapeDtypeStruct((t, h * QK_NOPE), F32), jax.ShapeDtypeStruct((h, t, QK_ROPE), F32),
                   jax.ShapeDtypeStruct((t, h * (QK_NOPE + V_HEAD)), F32), jax.ShapeDtypeStruct((t, QK_ROPE), F32)],
        compiler_params=_params(("arbitrary",)),
    )(qn, qp, kv, kp, kv, do, o, lse)


def _rope(x1, x2, cos, sin, *, name, inverse=False):
    w = x1.shape[1]
    sign = -1.0 if inverse else 1.0

    def fn(rows, bc):
        a, b, c, s = rows
        s = sign * s
        return (a * c - b * s, a * s + b * c), ()

    return _rowwise(fn, [x1, x2, cos, sin], [], [(w, F32), (w, F32)], [], name=name)


def _rope_in_place(x, cos_pairs, sin_signed, *, name, inverse=False):
    w = x.shape[1]
    half = ROPE_HALF
    sign = -1.0 if inverse else 1.0

    def fn(rows, bc):
        v, c, s = rows
        lane = lax.broadcasted_iota(jnp.int32, v.shape, 1)
        partner = jnp.where((lane & half) == 0, pltpu.roll(v, w - half, 1), pltpu.roll(v, half, 1))
        return (v * c + partner * (sign * s),), ()

    return _rowwise(fn, [x, cos_pairs, sin_signed], [], [(w, F32)], [], name=name)[0]


def _mlp_up(x_m, w1, tag, after=()):
    def epi(acc):
        r = jnp.maximum(acc, 0.0)
        return (r * r,)

    return _mm(x_m, w1, name=f"{tag}_up", out_dtypes=(MXU_DTYPE,), epilogue=epi, after=after)


def _mlp_fwd(x_m, w1, w2, tag, after=(), act=None):
    if act is None:
        act, after = _mlp_up(x_m, w1, tag, after=after), ()
    y = _mm(act, w2, name=f"{tag}_down", after=after)
    return y, (x_m, act)


def _plus_residual(acc, res):
    return (acc + ALPHA * res,)


def _mlp_bwd(dy_m, saved, w1, w2, tag, residual, after=()):
    x_m, act = saved
    dw2 = _mm(act, dy_m, ta=True, name=f"{tag}_dw2", out_dtypes=(MXU_DTYPE,))

    def epi(acc, av):
        return (acc * (2.0 * jnp.sqrt(av.astype(F32))),)

    dh = _mm(dy_m, w2, tb=True, name=f"{tag}_dact", out_dtypes=(MXU_DTYPE,), epilogue=epi, tile_extras=(act,),
             after=after)
    dw1 = _mm(x_m, dh, ta=True, out_dev=True, name=f"{tag}_dw1", out_dtypes=(MXU_DTYPE,))
    dx = _mm(dh, w1, tb=True, name=f"{tag}_dx", epilogue=_plus_residual, tile_extras=(residual,))
    return [dx], {"ff_w1": dw1, "ff_w2": dw2}


def _sc_gate_specs(t, d):
    per = d // LANES
    return [pl.BlockSpec((t, LANES), lambda j, k=k: (0, k * per + j)) for k in range(3)]


def _sc_gate_fwd(u, conv_w, *, name):
    t, d = u.shape[0], u.shape[1] // 3
    kw = conv_w.shape[0]
    ch = _conv_chunk(t)

    def body(b_ref, c_ref, h_ref, w_ref, r_ref, pad_ref):
        pad_ref[0:CONV_PAD, :] = jnp.zeros((CONV_PAD, LANES), F32)
        pad_ref[CONV_PAD:CONV_PAD + t, :] = c_ref[...] * h_ref[...]
        for t0 in range(0, t, ch):
            q = jnp.zeros((ch, LANES), F32)
            for k in range(kw):
                s = kw - 1 - k
                q = q + w_ref[k:k + 1, :] * pad_ref[CONV_PAD - s + t0:CONV_PAD - s + t0 + ch, :]
            r_ref[t0:t0 + ch, :] = (b_ref[t0:t0 + ch, :] * q).astype(r_ref.dtype)

    blk = pl.BlockSpec((t, LANES), lambda j: (0, j))
    return pl.pallas_call(
        body,
        name=name,
        grid=(d // LANES,),
        in_specs=_sc_gate_specs(t, d) + [pl.BlockSpec((kw, LANES), lambda j: (0, j))],
        out_specs=blk,
        out_shape=jax.ShapeDtypeStruct((t, d), MXU_DTYPE),
        scratch_shapes=[pltpu.VMEM((CONV_PAD + t, LANES), F32)],
        compiler_params=_params(("parallel",)),
    )(u, u, u, conv_w)


def _sc_gate_bwd(dr, u, conv_w, *, name):
    t, d = u.shape[0], u.shape[1] // 3
    kw = conv_w.shape[0]
    ch = _conv_chunk(t)

    def body(dr_ref, b_ref, c_ref, h_ref, w_ref, db_ref, dc_ref, dh_ref, dw_ref, ppad_ref, dpad_ref):
        ppad_ref[0:CONV_PAD, :] = jnp.zeros((CONV_PAD, LANES), F32)
        ppad_ref[CONV_PAD:CONV_PAD + t, :] = c_ref[...] * h_ref[...]
        dpad_ref[0:t, :] = dr_ref[...] * b_ref[...]
        dpad_ref[t:t + CONV_PAD, :] = jnp.zeros((CONV_PAD, LANES), F32)
        for t0 in range(0, t, ch):
            q = jnp.zeros((ch, LANES), F32)
            dp = jnp.zeros((ch, LANES), F32)
            for k in range(kw):
                s = kw - 1 - k
                q = q + w_ref[k:k + 1, :] * ppad_ref[CONV_PAD - s + t0:CONV_PAD - s + t0 + ch, :]
                dp = dp + w_ref[k:k + 1, :] * dpad_ref[t0 + s:t0 + s + ch, :]
            db_ref[t0:t0 + ch, :] = (dr_ref[t0:t0 + ch, :] * q).astype(db_ref.dtype)
            dc_ref[t0:t0 + ch, :] = (dp * h_ref[t0:t0 + ch, :]).astype(dc_ref.dtype)
            dh_ref[t0:t0 + ch, :] = (dp * c_ref[t0:t0 + ch, :]).astype(dh_ref.dtype)
        for k in range(kw):
            s = kw - 1 - k
            acc = jnp.zeros((ch, LANES), F32)
            for t0 in range(0, t, ch):
                acc = acc + dpad_ref[t0:t0 + ch, :] * ppad_ref[CONV_PAD - s + t0:CONV_PAD - s + t0 + ch, :]
            dw_ref[k:k + 1, :] = _colsum(acc)

    blk = pl.BlockSpec((t, LANES), lambda j: (0, j))
    wblk = pl.BlockSpec((kw, LANES), lambda j: (0, j))
    return pl.pallas_call(
        body,
        name=name,
        grid=(d // LANES,),
        in_specs=[blk] + _sc_gate_specs(t, d) + [wblk],
        out_specs=[blk, blk, blk, wblk],
        out_shape=[jax.ShapeDtypeStruct((t, d), MXU_DTYPE)] * 3 + [jax.ShapeDtypeStruct((kw, d), F32)],
        scratch_shapes=[pltpu.VMEM((CONV_PAD + t, LANES), F32), pltpu.VMEM((CONV_PAD + t, LANES), F32)],
        compiler_params=_params(("parallel",)),
    )(dr, u, u, u, conv_w)


def _sc_fwd(x_m, w_in, conv_w, w_out, tag, after=()):
    u = _mm(x_m, w_in, name=f"{tag}_in", after=after)
    r = _sc_gate_fwd(u, conv_w, name=f"{tag}_gate")
    y = _mm(r, w_out, name=f"{tag}_out")
    return y, (x_m, u, r)


def _sc_bwd(dy_m, saved, w_in, conv_w, w_out, tag, residual, after=()):
    d = D_MODEL
    x_m, u, r = saved
    dw_out = _mm(r, dy_m, ta=True, name=f"{tag}_dwout", out_dtypes=(MXU_DTYPE,))
    dr = _mm(dy_m, w_out, tb=True, name=f"{tag}_dr", after=after)
    db, dc, dh, dconv = _sc_gate_bwd(dr, u, conv_w, name=f"{tag}_dgate")
    du = jnp.concatenate([db, dc, dh], axis=1)
    dw_in = _mm(x_m, du, ta=True, out_dev=True, name=f"{tag}_dwin", out_dtypes=(MXU_DTYPE,))
    dx = _mm(du, w_in, tb=True, name=f"{tag}_dx", epilogue=_plus_residual, tile_extras=(residual,))
    return [dx], {"sc_w_in": dw_in, "sc_conv_w": dconv, "sc_w_out": dw_out}


def _cf_glu_specs(t, d):
    per = d // LANES
    return [pl.BlockSpec((t, LANES), lambda j, k=k: (0, k * per + j)) for k in range(2)]


def _cf_glu_conv_fwd(u, w, bias, *, name):
    t, d = u.shape[0], u.shape[1] // 2
    kw = w.shape[0]
    ch = _conv_chunk(t)

    def body(a_ref, g_ref, w_ref, b_ref, o_ref, pad_ref):
        pad_ref[0:CONV_PAD, :] = jnp.zeros((CONV_PAD, LANES), F32)
        pad_ref[CONV_PAD:CONV_PAD + t, :] = a_ref[...] * _sigmoid(g_ref[...])
        for t0 in range(0, t, ch):
            acc = jnp.broadcast_to(b_ref[...], (ch, LANES))
            for k in range(kw):
                s = kw - 1 - k
                acc = acc + w_ref[k:k + 1, :] * pad_ref[CONV_PAD - s + t0:CONV_PAD - s + t0 + ch, :]
            o_ref[t0:t0 + ch, :] = acc

    return pl.pallas_call(
        body,
        name=name,
        grid=(d // LANES,),
        in_specs=_cf_glu_specs(t, d) + [pl.BlockSpec((kw, LANES), lambda j: (0, j)),
                                        pl.BlockSpec((1, LANES), lambda j: (0, j))],
        out_specs=pl.BlockSpec((t, LANES), lambda j: (0, j)),
        out_shape=jax.ShapeDtypeStruct((t, d), F32),
        scratch_shapes=[pltpu.VMEM((CONV_PAD + t, LANES), F32)],
        compiler_params=_params(("parallel",)),
    )(u, u, w, bias)


def _cf_glu_conv_bwd(dy, u, w, *, name):
    t, d = u.shape[0], u.shape[1] // 2
    kw = w.shape[0]
    ch = _conv_chunk(t)

    def body(dy_ref, a_ref, g_ref, w_ref, da_ref, dg_ref, sa_ref, sg_ref, dw_ref, xpad_ref, dpad_ref):
        xpad_ref[0:CONV_PAD, :] = jnp.zeros((CONV_PAD, LANES), F32)
        xpad_ref[CONV_PAD:CONV_PAD + t, :] = a_ref[...] * _sigmoid(g_ref[...])
        dpad_ref[0:t, :] = dy_ref[...]
        dpad_ref[t:t + CONV_PAD, :] = jnp.zeros((CONV_PAD, LANES), F32)
        sum_a = jnp.zeros((1, LANES), F32)
        sum_g = jnp.zeros((1, LANES), F32)
        for t0 in range(0, t, ch):
            dh1 = jnp.zeros((ch, LANES), F32)
            for k in range(kw):
                s = kw - 1 - k
                dh1 = dh1 + w_ref[k:k + 1, :] * dpad_ref[t0 + s:t0 + s + ch, :]
            sg = _sigmoid(g_ref[t0:t0 + ch, :])
            da = dh1 * sg
            dgate = da * a_ref[t0:t0 + ch, :] * (1.0 - sg)
            da_ref[t0:t0 + ch, :] = da.astype(da_ref.dtype)
            dg_ref[t0:t0 + ch, :] = dgate.astype(dg_ref.dtype)
            sum_a = sum_a + _colsum(da)
            sum_g = sum_g + _colsum(dgate)
        sa_ref[...] = sum_a
        sg_ref[...] = sum_g
        for k in range(kw):
            s = kw - 1 - k
            acc = jnp.zeros((ch, LANES), F32)
            for t0 in range(0, t, ch):
                acc = acc + dy_ref[t0:t0 + ch, :] * xpad_ref[CONV_PAD - s + t0:CONV_PAD - s + t0 + ch, :]
            dw_ref[k:k + 1, :] = _colsum(acc)

    blk = pl.BlockSpec((t, LANES), lambda j: (0, j))
    row = pl.BlockSpec((1, LANES), lambda j: (0, j))
    wblk = pl.BlockSpec((kw, LANES), lambda j: (0, j))
    return pl.pallas_call(
        body,
        name=name,
        grid=(d // LANES,),
        in_specs=[blk] + _cf_glu_specs(t, d) + [wblk],
        out_specs=[blk, blk, row, row, wblk],
        out_shape=[jax.ShapeDtypeStruct((t, d), MXU_DTYPE)] * 2 + [jax.ShapeDtypeStruct((1, d), F32)] * 2
        + [jax.ShapeDtypeStruct((kw, d), F32)],
        scratch_shapes=[pltpu.VMEM((CONV_PAD + t, LANES), F32), pltpu.VMEM((CONV_PAD + t, LANES), F32)],
        compiler_params=_params(("parallel",)),
    )(dy, u, u, w)


def _cf_fwd(x_m, w, tag, after=()):
    d = D_MODEL

    def bias_epi(acc, bias):
        return (acc + bias,)

    u = _mm(x_m, w["cf_w_pw1"], name=f"{tag}_pw1", epilogue=bias_epi, col_extras=(w["cf_b_pw1"],),
            after=after)
    h2 = _cf_glu_conv_fwd(u, w["cf_dw_w"], w["cf_dw_b"], name=f"{tag}_conv")

    def norm_fn(rows, bc):
        hv = rows[0]
        mu = jnp.mean(hv, axis=-1, keepdims=True)
        hc = hv - mu
        rstd = lax.rsqrt(jnp.mean(hc * hc, axis=-1, keepdims=True) + LN_EPS)
        h3 = hc * rstd * bc[0] + bc[1]
        return (h3 * _sigmoid(h3),), ()

    h4 = _rowwise(norm_fn, [h2], [w["cf_norm_g"], w["cf_norm_b"]], [(d, MXU_DTYPE)], [], name=f"{tag}_norm")[0]
    y = _mm(h4, w["cf_w_pw2"], name=f"{tag}_pw2", epilogue=bias_epi, col_extras=(w["cf_b_pw2"],))
    return y, (x_m, u, h2, h4)


def _cf_bwd(dy_m, dy_colsum, saved, w, tag, residual, after=()):
    d = D_MODEL
    x_m, u, h2, h4 = saved
    dw_pw2 = _mm(h4, dy_m, ta=True, name=f"{tag}_dwpw2", out_dtypes=(MXU_DTYPE,))
    dh4 = _mm(dy_m, w["cf_w_pw2"], tb=True, name=f"{tag}_dh4", after=after)

    def dnorm_fn(rows, bc):
        dh4v, hv = rows
        g, b = bc
        mu = jnp.mean(hv, axis=-1, keepdims=True)
        hc = hv - mu
        rstd = lax.rsqrt(jnp.mean(hc * hc, axis=-1, keepdims=True) + LN_EPS)
        xh = hc * rstd
        h3 = xh * g + b
        sg = _sigmoid(h3)
        dh3 = dh4v * (sg * (1.0 + h3 * (1.0 - sg)))
        dxh = dh3 * g
        dh2 = rstd * (dxh - jnp.mean(dxh, axis=-1, keepdims=True) - xh * jnp.mean(dxh * xh, axis=-1, keepdims=True))
        return (dh2,), (_colsum(dh3 * xh), _colsum(dh3), _colsum(dh2))

    dh2, dnorm_g, dnorm_b, ddw_b = _rowwise(dnorm_fn, [dh4, h2], [w["cf_norm_g"], w["cf_norm_b"]], [(d, F32)],
                                            [(1, d), (1, d), (1, d)], name=f"{tag}_dnorm")
    da, dgate, sum_a, sum_g, ddw_w = _cf_glu_conv_bwd(dh2, u, w["cf_dw_w"], name=f"{tag}_dconv")
    du = jnp.concatenate([da, dgate], axis=1)
    db_pw1 = jnp.concatenate([sum_a, sum_g], axis=1)
    dw_pw1 = _mm(x_m, du, ta=True, out_dev=True, name=f"{tag}_dwpw1", out_dtypes=(MXU_DTYPE,))
    dx = _mm(du, w["cf_w_pw1"], tb=True, name=f"{tag}_dx", epilogue=_plus_residual, tile_extras=(residual,))
    grads = {"cf_w_pw1": dw_pw1, "cf_b_pw1": db_pw1, "cf_dw_w": ddw_w, "cf_dw_b": ddw_b, "cf_norm_g": dnorm_g,
             "cf_norm_b": dnorm_b, "cf_w_pw2": dw_pw2, "cf_b_pw2": dy_colsum}
    return [dx], grads


def _rope_tables(t):
    pos = jnp.arange(t, dtype=F32)
    inv_freq = ROPE_THETA ** (-jnp.arange(0, QK_ROPE, 2, dtype=F32) / QK_ROPE)
    ang = pos[:, None] * inv_freq[None, :]
    return jnp.cos(ang), jnp.sin(ang)


ROPE_HALF = QK_ROPE // 2
Q_NOPE_COLS = MLA_HEADS * QK_NOPE


def _interleave_heads(w_uk, w_uv):
    c = w_uk.shape[0]
    return jnp.concatenate([w_uk.reshape(c, MLA_HEADS, QK_NOPE), w_uv.reshape(c, MLA_HEADS, V_HEAD)],
                           axis=2).reshape(c, -1)


def _mla_fwd(x_m, w, tag, after=()):
    t = x_m.shape[0]
    cos, sin = _rope_tables(t)
    cos_h = jnp.tile(jnp.concatenate([cos, cos], axis=1), (1, MLA_HEADS))
    sin_h = jnp.tile(jnp.concatenate([-sin, sin], axis=1), (1, MLA_HEADS))
    cq_pre = _mm(x_m, w["mla_w_dq"], name=f"{tag}_dq", after=after)
    cq = _rms_fwd(cq_pre, w["mla_g_q"], name=f"{tag}_qnorm")
    q = _mm(cq, w["mla_w_uq"], name=f"{tag}_uq")
    qp = _rope_in_place(q[:, Q_NOPE_COLS:], cos_h, sin_h, name=f"{tag}_qrope")
    qp_h = qp.reshape(t, MLA_HEADS, QK_ROPE).transpose(1, 0, 2).astype(MXU_DTYPE)
    ckv_full = _mm(x_m, w["mla_w_dkv"], name=f"{tag}_dkv")
    ckv_pre = ckv_full[:, :KV_LORA]
    ckv = _rms_fwd(ckv_pre, w["mla_g_kv"], name=f"{tag}_kvnorm")
    k1, k2 = _rope(ckv_full[:, KV_LORA:KV_LORA + ROPE_HALF], ckv_full[:, KV_LORA + ROPE_HALF:], cos, sin,
                   name=f"{tag}_krope")
    kp = jnp.concatenate([k1, k2], axis=-1).astype(MXU_DTYPE)
    kv = _mm(ckv, _interleave_heads(w["mla_w_uk"], w["mla_w_uv"]), name=f"{tag}_ukv", out_dtypes=(MXU_DTYPE,))
    o, o_f, lse = _attn_fwd(q, qp_h, kv, kp, name=f"{tag}_attn")
    y = _mm(o, w["mla_w_o"], name=f"{tag}_o")
    return y, (x_m, cq_pre, cq, ckv_pre, ckv, q, qp_h, kv, kp, lse, o, o_f, (cos, sin, cos_h, sin_h))


def _mla_bwd(dy_m, saved, w, tag, residual, after=()):
    x_m, cq_pre, cq, ckv_pre, ckv, q, qp_h, kv, kp, lse, o, o_f, (cos, sin, cos_h, sin_h) = saved
    t = x_m.shape[0]
    half = ROPE_HALF
    dw_o = _mm(o, dy_m, ta=True, name=f"{tag}_dwo", out_dtypes=(MXU_DTYPE,))
    do = _mm(dy_m, w["mla_w_o"], tb=True, name=f"{tag}_do", out_dtypes=(MXU_DTYPE,), after=after)
    dqn, dqp_h, dkv, dkp = _attn_bwd(q, qp_h, kv, kp, do, o_f, lse, name=f"{tag}_dattn")
    dqp = _rope_in_place(dqp_h.transpose(1, 0, 2).reshape(t, -1), cos_h, sin_h, name=f"{tag}_dqrope", inverse=True)
    dq = jnp.concatenate([dqn, dqp], axis=-1).astype(MXU_DTYPE)
    dw_uq = _mm(cq, dq, ta=True, name=f"{tag}_dwuq", out_dtypes=(MXU_DTYPE,))
    dcq = _mm(dq, w["mla_w_uq"], tb=True, name=f"{tag}_dcq")
    dcq_pre, dg_q = _rms_bwd(cq_pre, w["mla_g_q"], dcq, name=f"{tag}_dqnorm")
    dcq_pre_m = dcq_pre.astype(MXU_DTYPE)
    dw_dq = _mm(x_m, dcq_pre_m, ta=True, name=f"{tag}_dwdq", out_dtypes=(MXU_DTYPE,))
    dx_q = _mm(dcq_pre_m, w["mla_w_dq"], tb=True, name=f"{tag}_dxq", epilogue=_plus_residual, tile_extras=(residual,))
    dw_kv = _mm(ckv, dkv, ta=True, name=f"{tag}_dwukv", out_dtypes=(MXU_DTYPE,))
    dw_kv = dw_kv.reshape(KV_LORA, MLA_HEADS, QK_NOPE + V_HEAD)
    dw_uk, dw_uv = dw_kv[:, :, :QK_NOPE].reshape(KV_LORA, -1), dw_kv[:, :, QK_NOPE:].reshape(KV_LORA, -1)
    dckv = _mm(dkv, _interleave_heads(w["mla_w_uk"], w["mla_w_uv"]), tb=True, name=f"{tag}_dckv")
    dckv_pre, dg_kv = _rms_bwd(ckv_pre, w["mla_g_kv"], dckv, name=f"{tag}_dkvnorm")
    dk1, dk2 = _rope(dkp[:, :half], dkp[:, half:], cos, sin, name=f"{tag}_dkrope", inverse=True)
    dckv_full = jnp.concatenate([dckv_pre, dk1, dk2], axis=-1).astype(MXU_DTYPE)
    dw_dkv = _mm(x_m, dckv_full, ta=True, name=f"{tag}_dwdkv", out_dtypes=(MXU_DTYPE,))
    dx_kv = _mm(dckv_full, w["mla_w_dkv"], tb=True, name=f"{tag}_dxkv", epilogue=lambda acc, other: (acc + other,),
                tile_extras=(dx_q,))
    grads = {"mla_w_dq": dw_dq, "mla_g_q": dg_q, "mla_w_uq": dw_uq, "mla_w_dkv": dw_dkv, "mla_g_kv": dg_kv,
             "mla_w_uk": dw_uk, "mla_w_uv": dw_uv, "mla_w_o": dw_o}
    return [dx_kv], grads


def _cast_input(x):
    cast = lambda rows, bc: ((rows[0],), ())
    return _rowwise(cast, [x], [], [(D_MODEL, MXU_DTYPE)], [], name="x_cast")[0]


def _hooked(value, hook):
    token = None if hook is None else hook(value)
    return () if token is None else (token,)


def _mix_fwd(i, cur, cur_m, wl, hook=None, after=()):
    mixer = i % N_MIXERS
    if mixer == 0:
        y, s_mix = _sc_fwd(cur_m, wl["sc_w_in"], wl["sc_conv_w"], wl["sc_w_out"], f"l{i}_sc", after=after)
    elif mixer == 1:
        y, s_mix = _mla_fwd(cur_m, wl, f"l{i}_mla", after=after)
    else:
        y, s_mix = _cf_fwd(cur_m, wl, f"l{i}_cf", after=after)
    cur, cur_m, xh_mix, rs_mix = _ln_fwd(cur, y, wl["ln_mix_g"], wl["ln_mix_b"], name=f"l{i}_lnmix",
                                         after=_hooked(y, hook))
    return cur, cur_m, (s_mix, xh_mix, rs_mix)


def _ff_fwd(i, cur, cur_m, wl, hook=None, after=(), loss_target=None, act=None):
    y, s_ff = _mlp_fwd(cur_m, wl["ff_w1"], wl["ff_w2"], f"l{i}_ff", after=after, act=act)
    if loss_target is not None:
        xh_ff, rs_ff, dy, sq = _ln_fwd_loss(cur, y, wl["ln_ff_g"], wl["ln_ff_b"], loss_target, name=f"l{i}_lnff_loss",
                                            after=_hooked(y, hook))
        return (0.5 / D_MODEL) * jnp.sum(sq), dy, (s_ff, xh_ff, rs_ff)
    cur, cur_m, xh_ff, rs_ff = _ln_fwd(cur, y, wl["ln_ff_g"], wl["ln_ff_b"], name=f"l{i}_lnff", after=_hooked(y, hook))
    return cur, cur_m, (s_ff, xh_ff, rs_ff)


def _layer_fwd(i, cur, cur_m, wl):
    cur, cur_m, s_mix = _mix_fwd(i, cur, cur_m, wl)
    cur, cur_m, s_ff = _ff_fwd(i, cur, cur_m, wl)
    return cur, cur_m, (*s_mix, *s_ff)


def _loss_and_grad(out, target):
    d = D_MODEL

    def loss_fn(rows, bc):
        err = rows[0] - rows[1]
        return (err * (1.0 / d),), (_colsum(err * err),)

    dy, sq = _rowwise(loss_fn, [out, target], [], [(d, F32)], [(1, d)], name="loss")
    return (0.5 / d) * jnp.sum(sq), dy


def _ff_bwd(i, terms, saved, wl, hook=None, after=()):
    s_ff, xh_ff, rs_ff = saved
    grads = {}
    dz, dz_m, grads["ln_ff_g"], grads["ln_ff_b"], _ = _ln_bwd(terms, xh_ff, rs_ff, wl["ln_ff_g"], name=f"l{i}_dlnff",
                                                              after=after)
    dxs, g_ff = _mlp_bwd(dz_m, s_ff, wl["ff_w1"], wl["ff_w2"], f"l{i}_ff", dz, after=_hooked(dz, hook))
    grads.update(g_ff)
    return [(1.0, v) for v in dxs], grads


def _mix_bwd(i, terms, saved, wl, hook=None, after=()):
    mixer = i % N_MIXERS
    s_mix, xh_mix, rs_mix = saved
    grads = {}
    dz, dz_m, grads["ln_mix_g"], grads["ln_mix_b"], dz_sum = _ln_bwd(terms, xh_mix, rs_mix, wl["ln_mix_g"],
                                                                     name=f"l{i}_dlnmix", after=after)
    behind = _hooked(dz, hook)
    if mixer == 0:
        dxs, g_mix = _sc_bwd(dz_m, s_mix, wl["sc_w_in"], wl["sc_conv_w"], wl["sc_w_out"], f"l{i}_sc", dz, after=behind)
    elif mixer == 1:
        dxs, g_mix = _mla_bwd(dz_m, s_mix, wl, f"l{i}_mla", dz, after=behind)
    else:
        dxs, g_mix = _cf_bwd(dz_m, dz_sum, s_mix, wl, f"l{i}_cf", dz, after=behind)
    grads.update(g_mix)
    return [(1.0, v) for v in dxs], grads


def _layer_bwd(i, terms, saved, wl):
    terms, g_ff = _ff_bwd(i, terms, saved[3:], wl)
    terms, g_mix = _mix_bwd(i, terms, saved[:3], wl)
    return terms, {**g_ff, **g_mix}


def _sum_terms(terms, *, name):
    coefs = [c for c, _ in terms]

    def sum_fn(rows, bc):
        acc = None
        for c, v in zip(coefs, rows):
            tv = v if c == 1.0 else c * v
            acc = tv if acc is None else acc + tv
        return (acc,), ()

    return _rowwise(sum_fn, [v for _, v in terms], [], [(D_MODEL, F32)], [], name=name)[0]


def _local_step(x, target, layers):
    cur, cur_m = x, _cast_input(x)
    saved = []
    for i in range(DEPTH):
        cur, cur_m, s = _layer_fwd(i, cur, cur_m, layers[i])
        saved.append(s)
    loss, dy = _loss_and_grad(cur, target)
    terms = [(1.0, dy)]
    grads = [None] * DEPTH
    for i in reversed(range(DEPTH)):
        terms, grads[i] = _layer_bwd(i, terms, saved[i], layers[i])
    return loss, _sum_terms(terms, name="grad_x"), grads


MESH_ID = pl.DeviceIdType.MESH
ANY = pl.BlockSpec(memory_space=pl.ANY)


def _block_id(px, py, pc):
    return 4 * px + 2 * py + pc


COPIES_PER_GATHER = 7


def _all_gather(shards, *, name):
    n = len(shards)

    def body(*refs):
        x_refs, out_refs = refs[:n], refs[n:2 * n]
        send_sems, recv_sems, local_sems = refs[2 * n:]
        x, y, core = lax.axis_index("x"), lax.axis_index("y"), lax.axis_index("c")
        me, sibling = (x, y, core), (x, y, 1 - core)
        chips = [(1 - x, y), (x, 1 - y), (1 - x, 1 - y)]

        def copy(t, k, block, to, from_input=False):
            dst = out_refs[t].at[_block_id(*block)]
            return pltpu.make_async_remote_copy(
                src_ref=x_refs[t] if from_input else dst, dst_ref=dst,
                send_sem=send_sems.at[t * COPIES_PER_GATHER + k], recv_sem=recv_sems.at[t * COPIES_PER_GATHER + k],
                device_id=to, device_id_type=MESH_ID)

        started = []
        mine = [pltpu.make_async_copy(x_refs[t], out_refs[t].at[_block_id(*me)], local_sems.at[t]) for t in range(n)]
        for t in range(n):
            mine[t].start()
            first = [copy(t, 0, me, sibling, from_input=True)]
            first += [copy(t, 1 + j, me, (*chip, core), from_input=True) for j, chip in enumerate(chips)]
            for cp in first:
                cp.start()
            started += first
        for t in range(n):
            for j, chip in enumerate(chips):
                copy(t, 1 + j, (*chip, core), me).wait_recv()
                passed = copy(t, 4 + j, (*chip, core), sibling)
                passed.start()
                started.append(passed)
        for t in range(n):
            copy(t, 0, sibling, me).wait_recv()
            for j, chip in enumerate(chips):
                copy(t, 4 + j, (*chip, 1 - core), me).wait_recv()
        for cp in started:
            cp.wait_send()
        for cp in mine:
            cp.wait()

    n_sems = n * COPIES_PER_GATHER
    outs = pl.pallas_call(
        body,
        name=name,
        out_shape=[jax.ShapeDtypeStruct((N_DEV, *s.shape), s.dtype) for s in shards],
        in_specs=[ANY] * n,
        out_specs=[ANY] * n,
        scratch_shapes=[pltpu.SemaphoreType.DMA((n_sems,)), pltpu.SemaphoreType.DMA((n_sems,)),
                        pltpu.SemaphoreType.DMA((n,))],
    )(*shards)
    return list(outs)


def _swap_with_sibling(parts, *, name):
    n = len(parts)

    def body(*refs):
        p_refs, got_refs = refs[:n], refs[n:2 * n]
        send_sems, recv_sems = refs[2 * n:]
        x, y, core = lax.axis_index("x"), lax.axis_index("y"), lax.axis_index("c")
        chips = [(x, y), (1 - x, y), (x, 1 - y), (1 - x, 1 - y)]
        copies = []
        for t in range(n):
            for k, (px, py) in enumerate(chips):
                copies.append(pltpu.make_async_remote_copy(
                    src_ref=p_refs[t].at[_block_id(px, py, 1 - core)], dst_ref=got_refs[t].at[k],
                    send_sem=send_sems.at[4 * t + k], recv_sem=recv_sems.at[4 * t + k], device_id=(x, y, 1 - core),
                    device_id_type=MESH_ID))
        for cp in copies:
            cp.start()
        for cp in copies:
            cp.wait()

    outs = pl.pallas_call(
        body,
        name=name,
        out_shape=[jax.ShapeDtypeStruct((4, *p.shape[1:]), p.dtype) for p in parts],
        in_specs=[ANY] * n,
        out_specs=[ANY] * n,
        scratch_shapes=[pltpu.SemaphoreType.DMA((4 * n,)), pltpu.SemaphoreType.DMA((4 * n,))],
    )(*parts)
    return list(outs)


def _swap_between_chips(parts, *, name):
    n = len(parts)

    def body(*refs):
        p_refs, got_refs = refs[:n], refs[n:2 * n]
        send_sems, recv_sems = refs[2 * n:]
        x, y, core = lax.axis_index("x"), lax.axis_index("y"), lax.axis_index("c")
        chips = [(1 - x, y), (x, 1 - y), (1 - x, 1 - y)]
        copies = []
        for t in range(n):
            for j, (px, py) in enumerate(chips):
                copies.append(pltpu.make_async_remote_copy(
                    src_ref=p_refs[t].at[j], dst_ref=got_refs[t].at[j], send_sem=send_sems.at[3 * t + j],
                    recv_sem=recv_sems.at[3 * t + j], device_id=(px, py, core), device_id_type=MESH_ID))
        for cp in copies:
            cp.start()
        for cp in copies:
            cp.wait()

    outs = pl.pallas_call(
        body,
        name=name,
        out_shape=[jax.ShapeDtypeStruct(p.shape, p.dtype) for p in parts],
        in_specs=[ANY] * n,
        out_specs=[ANY] * n,
        scratch_shapes=[pltpu.SemaphoreType.DMA((3 * n,)), pltpu.SemaphoreType.DMA((3 * n,))],
    )(*parts)
    return list(outs)


def _chip_partials_group(parts, got, *, name):
    n = len(parts)
    rows = [p.shape[1] for p in parts]
    steps = 4 if all(r % (4 * BF16_ROWS) == 0 for r in rows) else (2 if all(r % (2 * BF16_ROWS) == 0 for r in rows) else 1)
    x, y, core = lax.axis_index("x"), lax.axis_index("y"), lax.axis_index("c")
    chips = [(x, y), (1 - x, y), (x, 1 - y), (1 - x, 1 - y)]
    ids = jnp.stack([_block_id(px, py, core) for px, py in chips]).astype(jnp.int32)

    def body(ids_ref, *refs):
        ins, outs = refs[:5 * n], refs[5 * n:]
        for t in range(n):
            p0, p1, p2, p3, g_ref = ins[5 * t:5 * t + 5]
            own_ref, out_ref = outs[2 * t], outs[2 * t + 1]
            own_ref[...] = p0[...].astype(F32) + g_ref[0].astype(F32)
            for j, p in enumerate((p1, p2, p3)):
                out_ref[j] = (p[...].astype(F32) + g_ref[j + 1].astype(F32)).astype(out_ref.dtype)

    in_specs, out_specs, out_shape, operands = [], [], [], []
    for p, g in zip(parts, got):
        _, r, c = p.shape
        tb = r // steps
        in_specs += [pl.BlockSpec((None, tb, c), lambda i, ids_ref, k=k: (ids_ref[k], i, 0)) for k in range(4)]
        in_specs.append(pl.BlockSpec((4, tb, c), lambda i, ids_ref: (0, i, 0)))
        out_specs += [pl.BlockSpec((tb, c), lambda i, ids_ref: (i, 0)), pl.BlockSpec((3, tb, c), lambda i, ids_ref: (0, i, 0))]
        out_shape += [jax.ShapeDtypeStruct((r, c), F32), jax.ShapeDtypeStruct((3, r, c), p.dtype)]
        operands += [p, p, p, p, g]
    outs = pl.pallas_call(
        body,
        name=name,
        grid_spec=pltpu.PrefetchScalarGridSpec(num_scalar_prefetch=1, grid=(steps,), in_specs=in_specs,
                                               out_specs=out_specs),
        out_shape=out_shape,
        compiler_params=_params(("parallel",)),
    )(ids, *operands)
    return [(outs[2 * t], outs[2 * t + 1]) for t in range(n)]


def _sum_blocks(first, blocks, *, name):
    n, r, c = blocks.shape
    tb = _row_tile(r, 512)

    def body(*refs):
        b_ref, o_ref = refs[-2], refs[-1]
        acc = refs[0][...] if first is not None else b_ref[0].astype(F32)
        for j in range(0 if first is not None else 1, n):
            acc = acc + b_ref[j].astype(F32)
        o_ref[...] = acc

    row_spec = pl.BlockSpec((tb, c), lambda i: (i, 0))
    return pl.pallas_call(
        body,
        name=name,
        grid=(r // tb,),
        in_specs=([row_spec] if first is not None else []) + [pl.BlockSpec((n, tb, c), lambda i: (0, i, 0))],
        out_specs=row_spec,
        out_shape=jax.ShapeDtypeStruct((r, c), F32),
        compiler_params=_params(("parallel",)),
    )(*([first] if first is not None else []), blocks)


HBM = pl.BlockSpec(memory_space=pltpu.HBM)
SEM = pl.BlockSpec(memory_space=pltpu.SEMAPHORE)
DATAFLOW = pltpu.SideEffectType.DATAFLOW_SIDE_EFFECTING


def _split_call(name, arrays, sems, n_new, body, after=None):
    n_a, n_s = len(arrays), len(sems)
    after = [] if after is None else (list(after) if isinstance(after, (list, tuple)) else [after])
    n_after = len(after)

    def kernel_body(*refs):
        new = refs[n_a + n_s + n_after:n_a + n_s + n_after + 2] if n_new else (None, None)
        body(refs[:n_a], refs[n_a:n_a + n_s], *new)
        refs[-1][...] = jnp.zeros_like(refs[-1])

    n_sem_out = 2 if n_new else 0
    out_shape = [pltpu.SemaphoreType.DMA((n_new,))] * n_sem_out
    out_shape += [pltpu.HBM(a.shape, a.dtype) for a in arrays]
    out_shape += [jax.ShapeDtypeStruct((SUBLANES, LANES), F32)]
    outs = pl.pallas_call(
        kernel_body,
        name=name,
        out_shape=out_shape,
        in_specs=[HBM] * n_a + [SEM] * n_s + [ANY] * n_after,
        out_specs=[SEM] * n_sem_out + [HBM] * n_a + [pl.BlockSpec(memory_space=pltpu.VMEM)],
        input_output_aliases={i: n_sem_out + i for i in range(n_a)},
        compiler_params=pltpu.CompilerParams(has_side_effects=DATAFLOW),
    )(*[pltpu.with_memory_space_constraint(a, pltpu.HBM) for a in arrays], *sems, *after)
    return tuple(outs[:n_sem_out]), list(outs[n_sem_out:n_sem_out + n_a]), outs[-1]


def _remote(src, dst, send, recv, k, to):
    return pltpu.make_async_remote_copy(src_ref=src, dst_ref=dst, send_sem=send.at[k], recv_sem=recv.at[k],
                                        device_id=to, device_id_type=MESH_ID)


def _land_block(land_ref, block, by_cols, width):
    if by_cols:
        return land_ref.at[:, pl.ds(pl.multiple_of(block * width, LANES), width)]
    return land_ref.at[block]


def _place_in_window(stacked, layer, *, name):
    _, r, c = stacked.shape
    tb = _row_tile(r, 512)
    me = _block_id(lax.axis_index("x"), lax.axis_index("y"), lax.axis_index("c")).astype(jnp.int32).reshape(1)

    def body(me_ref, s_ref, o_ref, token_ref):
        o_ref[...] = s_ref[...].astype(o_ref.dtype)
        token_ref[...] = jnp.zeros_like(token_ref)

    return pl.pallas_call(
        body,
        name=name,
        grid_spec=pltpu.PrefetchScalarGridSpec(
            num_scalar_prefetch=1, grid=(r // tb,),
            in_specs=[pl.BlockSpec((None, tb, c), lambda i, me_ref: (layer, i, 0))],
            out_specs=[pl.BlockSpec((tb, c), lambda i, me_ref: (i, me_ref[0])),
                       pl.BlockSpec((SUBLANES, LANES), lambda i, me_ref: (0, 0))]),
        out_shape=[jax.ShapeDtypeStruct((r, N_DEV * c), MXU_DTYPE), jax.ShapeDtypeStruct((SUBLANES, LANES), F32)],
        compiler_params=_params(("arbitrary",)),
    )(me, stacked)


def _gather_prepare(shards, by_cols, *, name):
    me = _block_id(lax.axis_index("x"), lax.axis_index("y"), lax.axis_index("c"))
    lands, early, widths = [], [], []
    for i, (s, cols) in enumerate(zip(shards, by_cols)):
        if cols:
            land, token = _place_in_window(*s, name=f"{name}_place{i}")
            lands.append(land)
            early.append(token)
            widths.append(s[0].shape[-1])
        else:
            lands.append(lax.dynamic_update_index_in_dim(lax.empty((N_DEV, *s.shape), s.dtype), s, me, 0))
            early.append(s)
            widths.append(s.shape[-1])
    return lands, widths, list(by_cols), early


def _gather_start(shards, *, name, after=None, by_cols=None, prepared=None):
    if prepared is None:
        prepared = _gather_prepare(shards, by_cols if by_cols is not None else [False] * len(shards), name=name)
    lands, widths, by_cols, _ = prepared
    n = len(lands)

    def body(refs, _, send, recv):
        x, y, core = lax.axis_index("x"), lax.axis_index("y"), lax.axis_index("c")
        targets = [(x, y, 1 - core), (1 - x, y, core), (x, 1 - y, core), (1 - x, 1 - y, core)]
        for t in range(n):
            mine = _land_block(refs[t], _block_id(x, y, core), by_cols[t], widths[t])
            for k, to in enumerate(targets):
                _remote(mine, mine, send, recv, 4 * t + k, to).start()

    sems, thru, token = _split_call(name, lands, [], 4 * n, body, after=after)
    return {"n": n, "sems1": sems, "arrays": thru, "token": token, "by_cols": by_cols, "widths": widths}


def _gather_pass_on(st, *, name, after):
    n = st["n"]

    def body(refs, sems, send, recv):
        send1, recv1 = sems
        x, y, core = lax.axis_index("x"), lax.axis_index("y"), lax.axis_index("c")
        chips = [(1 - x, y), (x, 1 - y), (1 - x, 1 - y)]
        for t in range(n):
            for j, (px, py) in enumerate(chips):
                block = _land_block(refs[t], _block_id(px, py, core), st["by_cols"][t], st["widths"][t])
                _remote(block, block, send1, recv1, 4 * t + 1 + j, (x, y, 1 - core)).wait_recv()
                _remote(block, block, send, recv, 3 * t + j, (x, y, 1 - core)).start()

    sems2, thru, token = _split_call(name, st["arrays"], list(st["sems1"]), 3 * n, body, after=after)
    return {**st, "sems2": sems2, "arrays": thru, "token": token}


def _gather_finish(st, *, name, after, then_start=None):
    n = st["n"]
    n2 = 0 if then_start is None else len(then_start[0])

    def body(refs, sems, new_send=None, new_recv=None):
        send1, recv1, send2, recv2 = sems
        x, y, core = lax.axis_index("x"), lax.axis_index("y"), lax.axis_index("c")
        sibling = (x, y, 1 - core)
        chips = [(1 - x, y), (x, 1 - y), (1 - x, 1 - y)]
        for t in range(n):
            place = lambda px, py, pc, t=t: _land_block(refs[t], _block_id(px, py, pc), st["by_cols"][t],
                                                        st["widths"][t])
            mine = place(x, y, core)
            for k in range(4):
                _remote(mine, mine, send1, recv1, 4 * t + k, sibling).wait_send()
            _remote(mine, place(x, y, 1 - core), send1, recv1, 4 * t, sibling).wait_recv()
            for j, (px, py) in enumerate(chips):
                _remote(place(px, py, core), place(px, py, core), send2, recv2, 3 * t + j, sibling).wait_send()
                _remote(mine, place(px, py, 1 - core), send2, recv2, 3 * t + j, sibling).wait_recv()
        if n2:
            _, widths, by_cols, _ = then_start
            targets = [sibling, (1 - x, y, core), (x, 1 - y, core), (1 - x, 1 - y, core)]
            for t in range(n2):
                mine = _land_block(refs[n + t], _block_id(x, y, core), by_cols[t], widths[t])
                for k, to in enumerate(targets):
                    _remote(mine, mine, new_send, new_recv, 4 * t + k, to).start()

    arrays = st["arrays"] + (list(then_start[0]) if n2 else [])
    sems, thru, token = _split_call(name, arrays, [*st["sems1"], *st["sems2"]], 4 * n2, body, after=after)
    if not n2:
        return thru
    _, widths, by_cols, _ = then_start
    return thru[:n], {"n": n2, "sems1": sems, "arrays": thru[n:], "token": token, "by_cols": by_cols, "widths": widths}


def _swap_start(sources, lands, plan, per_array, *, name):
    n = len(sources)

    def body(refs, _, send, recv):
        x, y, core = lax.axis_index("x"), lax.axis_index("y"), lax.axis_index("c")
        for t in range(n):
            for k in range(per_array):
                block, to = plan(x, y, core, k)
                _remote(refs[t].at[block], refs[n + t].at[k], send, recv, per_array * t + k, to).start()

    sems, thru, token = _split_call(name, sources + lands, [], per_array * n, body)
    return {"n": n, "sems": sems, "arrays": thru, "token": token, "plan": plan, "per_array": per_array}


def _swap_finish(states, *, name, after):
    many = isinstance(states, (list, tuple))
    states = list(states) if many else [states]

    def body(refs, sems, *_):
        x, y, core = lax.axis_index("x"), lax.axis_index("y"), lax.axis_index("c")
        first = 0
        for j, st in enumerate(states):
            n, plan, per_array = st["n"], st["plan"], st["per_array"]
            send, recv = sems[2 * j], sems[2 * j + 1]
            for t in range(n):
                for k in range(per_array):
                    block, to = plan(x, y, core, k)
                    cp = _remote(refs[first + t].at[block], refs[first + n + t].at[k], send, recv, per_array * t + k, to)
                    cp.wait_send()
                    cp.wait_recv()
            first += 2 * n

    arrays = [a for st in states for a in st["arrays"]]
    _, thru, _ = _split_call(name, arrays, [s for st in states for s in st["sems"]], 0, body, after=after)
    out, first = [], 0
    for st in states:
        n = st["n"]
        out.append((thru[first:first + n], thru[first + n:first + 2 * n]))
        first += 2 * n
    return out if many else out[0]


def _to_sibling(x, y, core, k):
    px, py = [(x, y), (1 - x, y), (x, 1 - y), (1 - x, 1 - y)][k]
    return _block_id(px, py, 1 - core), (x, y, 1 - core)


def _to_chip(x, y, core, k):
    px, py = [(1 - x, y), (x, 1 - y), (1 - x, 1 - y)][k]
    return k, (px, py, core)


PARAMS = {
    "sc_w_in": ((2, 1024, 3072), 2), "sc_conv_w": ((2, 3, 1024), 2), "sc_w_out": ((2, 1024, 1024), 1),
    "mla_w_dq": ((1, 1024, 384), 1), "mla_g_q": ((1, 384), None), "mla_w_uq": ((1, 384, 1536), 2),
    "mla_w_dkv": ((1, 1024, 320), 1), "mla_g_kv": ((1, 256), None), "mla_w_uk": ((1, 256, 8, 128), 1),
    "mla_w_uv": ((1, 256, 8, 128), 1), "mla_w_o": ((1, 1024, 1024), 1), "cf_w_pw1": ((1, 1024, 2048), 2),
    "cf_b_pw1": ((1, 2048), 1), "cf_dw_w": ((1, 31, 1024), 2), "cf_dw_b": ((1, 1024), 1), "cf_norm_g": ((1, 1024), 1),
    "cf_norm_b": ((1, 1024), 1), "cf_w_pw2": ((1, 1024, 1024), 1), "cf_b_pw2": ((1, 1024), 1),
    "ff_w1": ((4, 1024, 4096), 2), "ff_w2": ((4, 4096, 1024), 1), "ln_mix_g": ((4, 1024), None),
    "ln_mix_b": ((4, 1024), None), "ln_ff_g": ((4, 1024), None), "ln_ff_b": ((4, 1024), None),
}
NAMES = list(PARAMS)
BIG = ["sc_w_in", "sc_w_out", "mla_w_dq", "mla_w_uq", "mla_w_dkv", "mla_w_uk", "mla_w_uv", "mla_w_o", "cf_w_pw1",
       "cf_w_pw2", "ff_w1", "ff_w2"]
SMALL = [n for n in NAMES if n not in BIG]
SMALL_SHARDED = [n for n in SMALL if PARAMS[n][1] is not None]
SMALL_WIDTH = LANES
DEV_BLOCKED = ("sc_w_in", "cf_w_pw1", "ff_w1")
MIXER_BIG = {0: ["sc_w_in", "sc_w_out"], 1: ["mla_w_dq", "mla_w_uq", "mla_w_dkv", "mla_w_uk", "mla_w_uv", "mla_w_o"],
             2: ["cf_w_pw1", "cf_w_pw2"]}


def _layer_big(i):
    return [(n, i // N_MIXERS) for n in MIXER_BIG[i % N_MIXERS]] + [("ff_w1", i), ("ff_w2", i)]


def _shard_2d(name, a):
    return a.reshape(a.shape[0], -1)


def _shard_for_gather(name, stacked, layer):
    if name in DEV_BLOCKED:
        return stacked, layer
    return _shard_2d(name, stacked[layer]).astype(MXU_DTYPE)


def _as_operand(name, g):
    if name in DEV_BLOCKED:
        return g
    if name == "mla_w_uq":
        by_head = g.transpose(1, 0, 2)
        pieces = [by_head[:, :, :QK_NOPE], by_head[:, :, QK_NOPE:]]
        return jnp.concatenate([p.reshape(g.shape[1], -1) for p in pieces], axis=1)
    return g.reshape(-1, g.shape[2])


def _as_parts(name, grad):
    if name in DEV_BLOCKED:
        return grad
    if name == "mla_w_uq":
        rows = grad.shape[0]
        pieces = [grad[:, :Q_NOPE_COLS].reshape(rows, MLA_HEADS, QK_NOPE),
                  grad[:, Q_NOPE_COLS:].reshape(rows, MLA_HEADS, QK_ROPE)]
        return jnp.concatenate(pieces, axis=2).transpose(1, 0, 2)
    return grad.reshape(N_DEV, -1, grad.shape[1])


def _shard_shape(name):
    shape, ax = PARAMS[name]
    if ax is None:
        return shape
    return tuple(s // N_DEV if i == ax else s for i, s in enumerate(shape))


def _rows_of(shape, width, align):
    n = 1
    for s in shape:
        n *= s
    rows = -(-n // width)
    return -(-rows // align) * align


def _pack_rows(arrays, width, align, dtype, lead=0):
    segs = []
    for a in arrays:
        batch = a.shape[:lead]
        flat = a.astype(dtype).reshape(*batch, -1)
        rows = _rows_of(a.shape[lead:], width, align)
        flat = jnp.pad(flat, [(0, 0)] * lead + [(0, rows * width - flat.shape[-1])])
        segs.append(flat.reshape(*batch, rows, width))
    return jnp.concatenate(segs, axis=lead)


def _unpack_rows(buf, shapes, width, align):
    out, off = [], 0
    lead = buf.shape[:-2]
    for shape in shapes:
        n = 1
        for s in shape:
            n *= s
        rows = _rows_of(shape, width, align)
        seg = buf[..., off:off + rows, :].reshape(*lead, rows * width)[..., :n]
        out.append(seg.reshape(*lead, *shape))
        off += rows
    return out


def _merge_shards(stacked, ax):
    moved = jnp.moveaxis(stacked, 0, ax)
    shape = moved.shape
    return moved.reshape(*shape[:ax], shape[ax] * shape[ax + 1], *shape[ax + 2:])


def _split_shards(full, ax):
    shape = full.shape
    split = full.reshape(*shape[:ax], N_DEV, shape[ax] // N_DEV, *shape[ax + 1:])
    return jnp.moveaxis(split, ax, 0)


def _adam_math(wv, gv, mv, vv):
    m_new = ADAM_B1 * mv + (1.0 - ADAM_B1) * gv
    v_new = ADAM_B2 * vv + (1.0 - ADAM_B2) * (gv * gv)
    m_hat = m_new / (1.0 - ADAM_B1 ** ADAM_STEP)
    v_hat = v_new / (1.0 - ADAM_B2 ** ADAM_STEP)
    delta = -ADAM_LR * (m_hat / (jnp.sqrt(v_hat) + ADAM_EPS) + ADAM_WD * wv)
    return delta, m_new, v_new


def _adamw(w, g, m, v, *, name):
    shape = w.shape
    c = shape[-1]

    def fn(rows, bc):
        return _adam_math(*rows), ()

    flat = [a.reshape(-1, c) for a in (w, g, m, v)]
    outs = _rowwise(fn, flat, [], [(c, F32)] * 3, [], name=name, tb_cap=512)
    return tuple(o.reshape(shape) for o in outs)


def _adamw_reduced(w, m, v, layer_grads, *, name, after=()):
    n_l, r, c = w.shape
    tb = _row_tile(r, 256 if c > 512 else 512)
    nb = r // tb

    def body(*refs):
        w_ref, m_ref, v_ref = refs[:3]
        g_refs = refs[3:3 + 2 * n_l]
        g_out, d_out, m_out, v_out = refs[3 + 2 * n_l + len(after):]
        layer = pl.program_id(0)
        for l in range(n_l):
            @pl.when(layer == l)
            def _(l=l):
                own, arr = g_refs[2 * l], g_refs[2 * l + 1]
                g = own[...] + arr[0].astype(F32)
                g = g + arr[1].astype(F32)
                g = g + arr[2].astype(F32)
                delta, m_new, v_new = _adam_math(w_ref[...], g, m_ref[...], v_ref[...])
                g_out[...] = g
                d_out[...] = delta
                m_out[...] = m_new
                v_out[...] = v_new

    stacked = pl.BlockSpec((tb, c), lambda l, i: (l * nb + i, 0))
    in_specs = [stacked] * 3
    operands = [a.reshape(n_l * r, c) for a in (w, m, v)]
    for l, (own, arr) in enumerate(layer_grads):
        in_specs.append(pl.BlockSpec((tb, c), lambda ll, i, l=l: (jnp.where(ll == l, i, 0), 0)))
        in_specs.append(pl.BlockSpec((3, tb, c), lambda ll, i, l=l: (0, jnp.where(ll == l, i, 0), 0)))
        operands += [own, arr]
    in_specs += [ANY] * len(after)
    operands += list(after)
    outs = pl.pallas_call(
        body,
        name=name,
        grid=(n_l, nb),
        in_specs=in_specs,
        out_specs=[stacked] * 4,
        out_shape=[jax.ShapeDtypeStruct((n_l * r, c), F32)] * 4,
        compiler_params=_params(("arbitrary", "arbitrary")),
    )(*operands)
    return tuple(o.reshape(n_l, r, c) for o in outs)


def _train_step(x, target, weights, m_state, v_state):
    me = _block_id(lax.axis_index("x"), lax.axis_index("y"), lax.axis_index("c"))

    groups = [(kind, i) for i in range(DEPTH) for kind in ("mix", "ff")]
    forward = [groups[0], ("ff_up", 0), ("ff_down", 0)] + groups[2:]

    def group_names(g):
        kind, i = g
        if kind == "mix":
            return [(n, i // N_MIXERS) for n in MIXER_BIG[i % N_MIXERS]]
        return [(n, i) for n in {"ff": ["ff_w1", "ff_w2"], "ff_up": ["ff_w1"], "ff_down": ["ff_w2"]}[kind]]

    def shards_of(g):
        return [_shard_for_gather(n, weights[n], j) for n, j in group_names(g)]

    whole = lambda g: [n in DEV_BLOCKED for n, _ in group_names(g)]

    small_shapes = [_shard_shape(n) for n in SMALL_SHARDED]
    gather = {0: _gather_start([_pack_rows([weights[n] for n in SMALL_SHARDED], SMALL_WIDTH, SUBLANES, F32)]
                               + shards_of(groups[0]), name="gather_g0_start", by_cols=[False] + whole(groups[0]))}
    cur, cur_m = x[0], x[0]
    prepared = {k: _gather_prepare(shards_of(forward[k]), whole(forward[k]), name=f"gather_g{k}_start")
                for k in range(1, len(forward))}
    early = [a for k in prepared for a in prepared[k][3]]
    gather[0] = _gather_pass_on(gather[0], name="gather_g0_pass", after=early)
    first = _gather_finish(gather[0], name="gather_g0_finish", after=gather[0]["token"])
    small_all, first_matrices = first[0], first[1:]
    small_full = {n: weights[n] for n in SMALL if PARAMS[n][1] is None}
    for n, stacked in zip(SMALL_SHARDED, _unpack_rows(small_all, small_shapes, SMALL_WIDTH, SUBLANES)):
        small_full[n] = _merge_shards(stacked, PARAMS[n][1])

    def small_weights(i):
        wl = {}
        for n in SMALL:
            if n.startswith("ln_"):
                wl[n] = small_full[n][i:i + 1]
            elif n.startswith(("sc_", "mla_", "cf_")[i % N_MIXERS]):
                v = small_full[n][i // N_MIXERS]
                wl[n] = v if v.ndim == 2 else v.reshape(1, -1)
        return wl

    layers = [small_weights(i) for i in range(DEPTH)]

    def install(g, gathered):
        for (n, _), a in zip(group_names(g), gathered):
            layers[g[1]][n] = _as_operand(n, a)

    install(forward[0], first_matrices)
    gather[1] = _gather_start(None, name="gather_g1_start", after=gather[0]["token"], prepared=prepared[1])
    gather[2] = _gather_start(None, name="gather_g2_start", after=gather[1]["token"], prepared=prepared[2])
    saved = {}
    hidden = None
    for k, g in enumerate(forward):
        kind, i = g
        tokens = [gather[1]["token"]] if k == 0 else []
        if k + 2 < len(forward):
            tokens.append(gather[k + 2]["token"])

        def pass_on(y, k=k):
            if k + 1 == len(forward):
                return None
            gather[k + 1] = _gather_pass_on(gather[k + 1], name=f"gather_g{k + 1}_pass", after=y)
            return gather[k + 1]["token"]

        if k + 1 == len(forward):
            loss_local, dy, saved[g] = _ff_fwd(i, cur, cur_m, layers[i], after=tokens, loss_target=target[0])
            break
        if kind == "ff_up":
            hidden = _mlp_up(cur_m, layers[i]["ff_w1"], f"l{i}_ff", after=tokens)
            behind, done = _hooked(hidden, pass_on), hidden
        elif kind == "ff_down":
            cur, cur_m, saved[("ff", i)] = _ff_fwd(i, cur, cur_m, layers[i], hook=pass_on, after=[*tokens, *behind],
                                                   act=hidden)
            done = cur
        else:
            phase = _mix_fwd if kind == "mix" else _ff_fwd
            cur, cur_m, saved[g] = phase(i, cur, cur_m, layers[i], hook=pass_on, after=tokens)
            done = cur
        if k + 3 < len(forward):
            gathered, gather[k + 3] = _gather_finish(gather[k + 1], name=f"gather_g{k + 1}_finish", after=done,
                                                     then_start=prepared[k + 3])
        else:
            gathered = _gather_finish(gather[k + 1], name=f"gather_g{k + 1}_finish", after=done)
        install(forward[k + 1], gathered)

    loss = lax.psum(loss_local, MESH_AXES)

    reduced = {n: [None] * PARAMS[n][0][0] for n in BIG}
    small_grads = {n: [None] * PARAMS[n][0][0] for n in SMALL}
    reduce = {}

    def to_chips(p, after):
        st = reduce[p]
        parts, got = _swap_finish(st["sibling"], name=f"reduce_p{p}_sibling_finish", after=after)
        sums = _chip_partials_group(parts, got, name=f"reduce_p{p}_chipsum")
        st["own"] = [own for own, _ in sums]
        others = [o for _, o in sums]
        st["chips"] = _swap_start(others, [lax.empty(o.shape, o.dtype) for o in others], _to_chip, 3,
                                  name=f"reduce_p{p}_chips_start")
        return st["chips"]["token"]

    def arrived(phases, after):
        done = _swap_finish([reduce[p]["chips"] for p in phases], name=f"reduce_p{phases[0]}_chips_finish", after=after)
        for p, (_, got) in zip(phases, done):
            for (n, j), own, arr in zip(reduce[p]["names"], reduce[p]["own"], got):
                reduced[n][j] = (own, arr)

    terms = [(1.0, dy)]
    back = list(reversed(groups))
    for p, g in enumerate(back):
        kind, i = g

        def after_first_kernel(dz, p=p):
            return to_chips(p - 1, dz) if p >= 1 else None

        phase = _mix_bwd if kind == "mix" else _ff_bwd
        tokens = [reduce[p - 1]["sibling"]["token"]] if p >= 1 else []
        terms, grads = phase(i, terms, saved[g], layers[i], hook=after_first_kernel, after=tokens)
        for n in SMALL:
            if n in grads:
                small_grads[n][i if n.startswith("ln_") else i // N_MIXERS] = grads[n]
        names = group_names(g)
        parts = [_as_parts(n, grads[n]) for n, _ in names]
        reduce[p] = {"names": names, "sibling": _swap_start(
            parts, [lax.empty((4, *a.shape[1:]), a.dtype) for a in parts], _to_sibling, 4,
            name=f"reduce_p{p}_sibling_start")}
    (_, grad_x), = terms
    last = len(back) - 1
    small_mine = _pack_rows([jnp.stack(small_grads[n]).reshape(PARAMS[n][0]) for n in SMALL], SMALL_WIDTH, SUBLANES, F32)
    small_gather = _gather_start([small_mine], name="gather_small_grads_start")
    tail = to_chips(last, small_gather["token"])
    arrived(list(range(last)), tail)

    grad, delta, new_m, new_v = {}, {}, {}, {}

    def adamw_matrix(n, after):
        shape = weights[n].shape
        view = lambda a: a.reshape(shape[0], shape[1], -1)
        outs = _adamw_reduced(view(weights[n]), view(m_state[n]), view(v_state[n]), reduced[n], name=f"adamw_{n}",
                              after=after)
        grad[n], delta[n], new_m[n], new_v[n] = [o.reshape(shape) for o in outs]
        return outs[0]

    done = tail
    for n in BIG:
        if n not in MIXER_BIG[0]:
            done = adamw_matrix(n, [done])
    small_gather = _gather_pass_on(small_gather, name="gather_small_grads_pass", after=done)
    small_all = _gather_finish(small_gather, name="gather_small_grads_finish", after=small_gather["token"])[0]
    small_sum = _sum_blocks(None, small_all, name="sum_small_grads")
    for n, g in zip(SMALL, _unpack_rows(small_sum, [PARAMS[n][0] for n in SMALL], SMALL_WIDTH, SUBLANES)):
        ax = PARAMS[n][1]
        grad[n] = g if ax is None else lax.dynamic_index_in_dim(_split_shards(g, ax), me, axis=0, keepdims=False)
    small_shards = [_shard_shape(n) for n in SMALL]
    packed = [_pack_rows([src[n] for n in SMALL], SMALL_WIDTH, SUBLANES, F32) for src in (weights, grad, m_state, v_state)]
    small_out = _adamw(*packed, name="adamw_small")
    for outs, dst in zip(small_out, (delta, new_m, new_v)):
        for n, a in zip(SMALL, _unpack_rows(outs, small_shards, SMALL_WIDTH, SUBLANES)):
            dst[n] = a
    arrived([last], small_sum)
    for n in MIXER_BIG[0]:
        adamw_matrix(n, [])
    return (loss, grad_x[None], *[grad[n] for n in NAMES], *[delta[n] for n in NAMES],
            *[new_m[n] for n in NAMES], *[new_v[n] for n in NAMES])


def kernel(x, sc_w_in, sc_conv_w, sc_w_out, mla_w_dq, mla_g_q, mla_w_uq, mla_w_dkv, mla_g_kv, mla_w_uk, mla_w_uv, mla_w_o, cf_w_pw1, cf_b_pw1, cf_dw_w, cf_dw_b, cf_norm_g, cf_norm_b, cf_w_pw2, cf_b_pw2, ff_w1, ff_w2, ln_mix_g, ln_mix_b, ln_ff_g, ln_ff_b, loss_target, m_sc_w_in, m_sc_conv_w, m_sc_w_out, m_mla_w_dq, m_mla_g_q, m_mla_w_uq, m_mla_w_dkv, m_mla_g_kv, m_mla_w_uk, m_mla_w_uv, m_mla_w_o, m_cf_w_pw1, m_cf_b_pw1, m_cf_dw_w, m_cf_dw_b, m_cf_norm_g, m_cf_norm_b, m_cf_w_pw2, m_cf_b_pw2, m_ff_w1, m_ff_w2, m_ln_mix_g, m_ln_mix_b, m_ln_ff_g, m_ln_ff_b, v_sc_w_in, v_sc_conv_w, v_sc_w_out, v_mla_w_dq, v_mla_g_q, v_mla_w_uq, v_mla_w_dkv, v_mla_g_kv, v_mla_w_uk, v_mla_w_uv, v_mla_w_o, v_cf_w_pw1, v_cf_b_pw1, v_cf_dw_w, v_cf_dw_b, v_cf_norm_g, v_cf_norm_b, v_cf_w_pw2, v_cf_b_pw2, v_ff_w1, v_ff_w2, v_ln_mix_g, v_ln_mix_b, v_ln_ff_g, v_ln_ff_b):
    w_list = (sc_w_in, sc_conv_w, sc_w_out, mla_w_dq, mla_g_q, mla_w_uq, mla_w_dkv, mla_g_kv, mla_w_uk, mla_w_uv, mla_w_o,
              cf_w_pw1, cf_b_pw1, cf_dw_w, cf_dw_b, cf_norm_g, cf_norm_b, cf_w_pw2, cf_b_pw2, ff_w1, ff_w2, ln_mix_g,
              ln_mix_b, ln_ff_g, ln_ff_b)
    m_list = (m_sc_w_in, m_sc_conv_w, m_sc_w_out, m_mla_w_dq, m_mla_g_q, m_mla_w_uq, m_mla_w_dkv, m_mla_g_kv, m_mla_w_uk,
              m_mla_w_uv, m_mla_w_o, m_cf_w_pw1, m_cf_b_pw1, m_cf_dw_w, m_cf_dw_b, m_cf_norm_g, m_cf_norm_b, m_cf_w_pw2,
              m_cf_b_pw2, m_ff_w1, m_ff_w2, m_ln_mix_g, m_ln_mix_b, m_ln_ff_g, m_ln_ff_b)
    v_list = (v_sc_w_in, v_sc_conv_w, v_sc_w_out, v_mla_w_dq, v_mla_g_q, v_mla_w_uq, v_mla_w_dkv, v_mla_g_kv, v_mla_w_uk,
              v_mla_w_uv, v_mla_w_o, v_cf_w_pw1, v_cf_b_pw1, v_cf_dw_w, v_cf_dw_b, v_cf_norm_g, v_cf_norm_b, v_cf_w_pw2,
              v_cf_b_pw2, v_ff_w1, v_ff_w2, v_ln_mix_g, v_ln_mix_b, v_ln_ff_g, v_ln_ff_b)
    return _train_step(x, loss_target, dict(zip(NAMES, w_list)), dict(zip(NAMES, m_list)), dict(zip(NAMES, v_list)))
```

```python
import jax
import jax.numpy as jnp
from jax import lax
from jax.experimental import pallas as pl
from jax.experimental.pallas import tpu as pltpu

F32 = jnp.float32
MXU_DTYPE = jnp.bfloat16

N_DEV = 8
D_MODEL = 1024
DEPTH = 4
N_MIXERS = 3
CHUNK = 64
ALPHA = (2.0 * DEPTH) ** 0.25
LN_EPS = 1e-5
RMS_EPS = 1e-6
MLA_HEADS = 8
QK_NOPE = 128
QK_ROPE = 64
V_HEAD = 128
Q_LORA = 384
KV_LORA = 256
ROPE_THETA = 10000.0
ADAM_LR = 0.001
ADAM_B1 = 0.9
ADAM_B2 = 0.999
ADAM_EPS = 1e-08
ADAM_WD = 0.01
ADAM_STEP = 10

LANES = 128
SUBLANES = 8
BF16_ROWS = 16
VMEM_LIMIT = 56 * 1024 * 1024
MM_VMEM_BUDGET = 36 * 1024 * 1024
CONV_PAD = 32
MASK_VALUE = -1e30

MESH_AXES = ("x", "y", "c")


def _params(semantics):
    return pltpu.CompilerParams(dimension_semantics=semantics, vmem_limit_bytes=VMEM_LIMIT)


def _tile(n, cap):
    if n <= cap:
        return n
    t = cap - cap % LANES
    while t >= LANES:
        if n % t == 0:
            return t
        t -= LANES
    raise ValueError(f"no tile for {n} under {cap}")


def _row_tile(rows, cap):
    if rows <= cap:
        return rows
    t = cap - cap % SUBLANES
    while t >= SUBLANES:
        if rows % t == 0:
            return t
        t -= SUBLANES
    raise ValueError(f"no row tile for {rows} under {cap}")


def _mm(a, b, *, name, ta=False, tb=False, a_dev=False, b_dev=False, out_dev=False, out_dtypes=(F32,), epilogue=None,
        tile_extras=(), col_extras=(), after=(), tm_cap=2048, tn_cap=1024, tk_cap=4096):
    if a_dev:
        assert not ta
        m, k = a.shape[1], a.shape[0] * a.shape[2]
        tk_cap = a.shape[2]
    else:
        m, k = (a.shape[1], a.shape[0]) if ta else a.shape
    if b_dev:
        nd, b_rows, ns = b.shape
        n, kb = (b_rows, nd * ns) if tb else (nd * ns, b_rows)
    else:
        n, kb = (b.shape[0], b.shape[1]) if tb else (b.shape[1], b.shape[0])
    assert k == kb, (a.shape, b.shape, ta, tb)
    if ta or tile_extras or a_dev or (b_dev and tb):
        tm_cap = min(tm_cap, 1024)
    tk = ns if (b_dev and tb) else (a.shape[2] if a_dev else _tile(k, tk_cap))
    nk = k // tk
    fixed_tn = ns if (b_dev and not tb) else (n // N_DEV if out_dev else None)
    if out_dev:
        assert not tile_extras and not col_extras and n % N_DEV == 0
    out_bytes = sum(jnp.dtype(dt).itemsize for dt in out_dtypes) + sum(e.dtype.itemsize for e in tile_extras)
    tm, tn = None, None
    for cand_m, cand_n in ((tm_cap, tn_cap), (tm_cap, tn_cap // 2), (tm_cap // 2, tn_cap), (tm_cap // 2, tn_cap // 2),
                           (tm_cap // 4, tn_cap // 2), (tm_cap // 4, tn_cap // 4)):
        tm, tn = _tile(m, cand_m), (fixed_tn if fixed_tn is not None else _tile(n, cand_n))
        need = 2 * (tm * tk * a.dtype.itemsize + tk * tn * b.dtype.itemsize) + 2 * tm * tn * out_bytes
        need += tm * tn * 4 * (2 if nk > 1 else 1)
        if need <= MM_VMEM_BUDGET:
            break
    n_te, n_ce, n_out = len(tile_extras), len(col_extras), len(out_dtypes)
    dims = (((0 if ta else 1,), (1 if tb else 0,)), ((), ()))

    def body(*refs):
        a_ref, b_ref = refs[0], refs[1]
        te_refs = refs[2:2 + n_te]
        ce_refs = refs[2 + n_te:2 + n_te + n_ce]
        first_out = 2 + n_te + n_ce + len(after)
        out_refs = refs[first_out:first_out + n_out]

        def finish(acc):
            if epilogue is None:
                outs = (acc,)
            else:
                outs = epilogue(acc, *[r[...] for r in te_refs], *[r[...] for r in ce_refs])
            for o_ref, o in zip(out_refs, outs):
                o_ref[...] = o.astype(o_ref.dtype)

        def product():
            return lax.dot_general(a_ref[...].astype(MXU_DTYPE), b_ref[...].astype(MXU_DTYPE), dims,
                                   preferred_element_type=F32)

        if nk == 1:
            finish(product())
        else:
            acc_ref = refs[-1]
            kk = pl.program_id(2)

            @pl.when(kk == 0)
            def _():
                acc_ref[...] = jnp.zeros_like(acc_ref)

            acc_ref[...] += product()

            @pl.when(kk == nk - 1)
            def _():
                finish(acc_ref[...])

    if a_dev:
        a_spec = pl.BlockSpec((None, tm, tk), lambda i, j, kk: (kk, i, 0))
    elif ta:
        a_spec = pl.BlockSpec((tk, tm), lambda i, j, kk: (kk, i))
    else:
        a_spec = pl.BlockSpec((tm, tk), lambda i, j, kk: (i, kk))
    if b_dev and tb:
        b_spec = pl.BlockSpec((None, tn, tk), lambda i, j, kk: (kk, j, 0))
    elif b_dev:
        b_spec = pl.BlockSpec((None, tk, tn), lambda i, j, kk: (j, kk, 0))
    elif tb:
        b_spec = pl.BlockSpec((tn, tk), lambda i, j, kk: (j, kk))
    else:
        b_spec = pl.BlockSpec((tk, tn), lambda i, j, kk: (kk, j))
    tile_spec = pl.BlockSpec((tm, tn), lambda i, j, kk: (i, j))
    col_spec = pl.BlockSpec((1, tn), lambda i, j, kk: (0, j))
    if out_dev:
        out_spec = pl.BlockSpec((None, tm, tn), lambda i, j, kk: (j, i, 0))
        out_shape = [jax.ShapeDtypeStruct((N_DEV, m, tn), dt) for dt in out_dtypes]
    else:
        out_spec = tile_spec
        out_shape = [jax.ShapeDtypeStruct((m, n), dt) for dt in out_dtypes]
    outs = pl.pallas_call(
        body,
        name=name,
        grid=(m // tm, n // tn, nk),
        in_specs=[a_spec, b_spec] + [tile_spec] * n_te + [col_spec] * n_ce + [ANY] * len(after),
        out_specs=[out_spec] * n_out,
        out_shape=out_shape,
        scratch_shapes=[pltpu.VMEM((tm, tn), F32)] if nk > 1 else [],
        compiler_params=_params(("parallel", "parallel", "arbitrary")),
    )(a, b, *tile_extras, *col_extras, *after)
    return outs[0] if n_out == 1 else tuple(outs)


def _rowwise(fn, rows, bcast, out_rows, out_accs, *, name, tb_cap=256, after=()):
    t = rows[0].shape[0]
    tb = _row_tile(t, tb_cap)
    n_r, n_b, n_o, n_a = len(rows), len(bcast), len(out_rows), len(out_accs)

    def body(*refs):
        r_refs = refs[:n_r]
        b_refs = refs[n_r:n_r + n_b]
        first_out = n_r + n_b + len(after)
        o_refs = refs[first_out:first_out + n_o]
        a_refs = refs[first_out + n_o:]
        outs, accs = fn([r[...] for r in r_refs], [r[...] for r in b_refs])
        for o_ref, o in zip(o_refs, outs):
            o_ref[...] = o.astype(o_ref.dtype)
        if n_a:
            i = pl.program_id(0)

            @pl.when(i == 0)
            def _():
                for a_ref, acc in zip(a_refs, accs):
                    a_ref[...] = acc

            @pl.when(i > 0)
            def _():
                for a_ref, acc in zip(a_refs, accs):
                    a_ref[...] += acc

    in_specs = [pl.BlockSpec((tb, r.shape[1]), lambda i: (i, 0)) for r in rows]
    in_specs += [pl.BlockSpec(b.shape, lambda i: (0, 0)) for b in bcast]
    in_specs += [ANY] * len(after)
    out_specs = [pl.BlockSpec((tb, c), lambda i: (i, 0)) for c, _ in out_rows]
    out_specs += [pl.BlockSpec(s, lambda i: (0, 0)) for s in out_accs]
    out_shape = [jax.ShapeDtypeStruct((t, c), dt) for c, dt in out_rows]
    out_shape += [jax.ShapeDtypeStruct(s, F32) for s in out_accs]
    outs = pl.pallas_call(
        body,
        name=name,
        grid=(t // tb,),
        in_specs=in_specs,
        out_specs=out_specs,
        out_shape=out_shape,
        compiler_params=_params(("arbitrary",)),
    )(*rows, *bcast, *after)
    return tuple(outs)


def _colsum(v):
    return jnp.sum(v, axis=0, keepdims=True)


def _sigmoid(v):
    return 1.0 / (1.0 + jnp.exp(-v))


def _ln_fwd(x_res, y_sub, g, b, *, name, after=()):
    d = x_res.shape[1]

    def fn(rows, bc):
        z = ALPHA * rows[0] + rows[1]
        mu = jnp.mean(z, axis=-1, keepdims=True)
        zc = z - mu
        var = jnp.mean(zc * zc, axis=-1, keepdims=True)
        rstd = lax.rsqrt(var + LN_EPS)
        xhat = zc * rstd
        out = xhat * bc[0] + bc[1]
        return (out, out, xhat, rstd), ()

    return _rowwise(fn, [x_res, y_sub], [g, b], [(d, F32), (d, MXU_DTYPE), (d, F32), (1, F32)], [], name=name,
                    after=after, tb_cap=512)


def _ln_fwd_loss(x_res, y_sub, g, b, target, *, name, after=()):
    d = x_res.shape[1]

    def fn(rows, bc):
        z = ALPHA * rows[0] + rows[1]
        mu = jnp.mean(z, axis=-1, keepdims=True)
        zc = z - mu
        var = jnp.mean(zc * zc, axis=-1, keepdims=True)
        rstd = lax.rsqrt(var + LN_EPS)
        xhat = zc * rstd
        err = xhat * bc[0] + bc[1] - rows[2]
        return (xhat, rstd, err * (1.0 / d)), (_colsum(err * err),)

    return _rowwise(fn, [x_res, y_sub, target], [g, b], [(d, F32), (1, F32), (d, F32)], [(1, d)], name=name,
                    after=after, tb_cap=512)


def _ln_bwd(terms, xhat, rstd, g, *, name, after=()):
    d = xhat.shape[1]
    coefs = [c for c, _ in terms]

    def fn(rows, bc):
        xh, rs = rows[0], rows[1]
        dout = None
        for c, v in zip(coefs, rows[2:]):
            tv = v if c == 1.0 else c * v
            dout = tv if dout is None else dout + tv
        dxh = dout * bc[0]
        m1 = jnp.mean(dxh, axis=-1, keepdims=True)
        m2 = jnp.mean(dxh * xh, axis=-1, keepdims=True)
        dz = rs * (dxh - m1 - xh * m2)
        return (dz, dz), (_colsum(dout * xh), _colsum(dout), _colsum(dz))

    return _rowwise(fn, [xhat, rstd] + [v for _, v in terms], [g], [(d, F32), (d, MXU_DTYPE)],
                    [(1, d), (1, d), (1, d)], name=name, after=after, tb_cap=512)


def _rms_fwd(x, g, *, name):
    c = x.shape[1]

    def fn(rows, bc):
        v = rows[0]
        r = lax.rsqrt(jnp.mean(v * v, axis=-1, keepdims=True) + RMS_EPS)
        return (v * r * bc[0],), ()

    return _rowwise(fn, [x], [g], [(c, MXU_DTYPE)], [], name=name)[0]


def _rms_bwd(x, g, dy, *, name):
    c = x.shape[1]

    def fn(rows, bc):
        v, dyv = rows
        r = lax.rsqrt(jnp.mean(v * v, axis=-1, keepdims=True) + RMS_EPS)
        xh = v * r
        dxh = dyv * bc[0]
        dx = r * (dxh - xh * jnp.mean(dxh * xh, axis=-1, keepdims=True))
        return (dx,), (_colsum(dyv * xh),)

    return _rowwise(fn, [x, dy], [g], [(c, F32)], [(1, c)], name=name)


def _conv_chunk(t):
    return _row_tile(t, 256)


ATTN_SCALE = (QK_NOPE + QK_ROPE) ** -0.5
_NT = (((1,), (1,)), ((), ()))
_TN = (((0,), (0,)), ((), ()))
_NN = (((1,), (0,)), ((), ()))


def _dot(a, b, dims):
    return lax.dot_general(a, b, dims, preferred_element_type=F32)


def _block_scores(q_cat, kn_ref, kp_ref, lo, hi, diagonal):
    k_cat = jnp.concatenate([kn_ref[lo:hi, :], kp_ref[lo:hi, :]], axis=1)
    s = _dot(q_cat, k_cat, _NT) * ATTN_SCALE
    if diagonal:
        shift = CHUNK.bit_length() - 1
        rows = lax.broadcasted_iota(jnp.int32, s.shape, 0)
        cols = lax.broadcasted_iota(jnp.int32, s.shape, 1)
        s = jnp.where(jnp.right_shift(cols, shift) <= jnp.right_shift(rows, shift), s, MASK_VALUE)
    return s, k_cat


def _attn_fwd(qn, qp, kv, kp, *, name):
    h, t, _ = qp.shape
    tq = _row_tile(t, 256)
    assert tq % CHUNK == 0

    def body(qn_ref, qp_ref, kn_ref, kp_ref, v_ref, o_ref, of_ref, lse_ref):
        for r0 in range(0, t, tq):
            r1 = r0 + tq
            q_cat = jnp.concatenate([qn_ref[r0:r1, :].astype(MXU_DTYPE), qp_ref[r0:r1, :]], axis=1)
            s_d, _ = _block_scores(q_cat, kn_ref, kp_ref, r0, r1, True)
            m = jnp.max(s_d, axis=-1, keepdims=True)
            if r0:
                s_b, _ = _block_scores(q_cat, kn_ref, kp_ref, 0, r0, False)
                m = jnp.maximum(m, jnp.max(s_b, axis=-1, keepdims=True))
            p_d = jnp.exp(s_d - m)
            l = jnp.sum(p_d, axis=-1, keepdims=True)
            acc = _dot(p_d.astype(MXU_DTYPE), v_ref[r0:r1, :], _NN)
            if r0:
                p_b = jnp.exp(s_b - m)
                l = l + jnp.sum(p_b, axis=-1, keepdims=True)
                acc = acc + _dot(p_b.astype(MXU_DTYPE), v_ref[0:r0, :], _NN)
            o = acc / l
            o_ref[r0:r1, :] = o.astype(o_ref.dtype)
            of_ref[r0:r1, :] = o
            lse_ref[r0:r1, :] = m + jnp.log(l)

    blk = lambda w: pl.BlockSpec((None, t, w), lambda hh: (hh, 0, 0))
    cols = pl.BlockSpec((t, QK_NOPE), lambda hh: (0, hh))
    keys = pl.BlockSpec((t, QK_NOPE), lambda hh: (0, 2 * hh))
    values = pl.BlockSpec((t, V_HEAD), lambda hh: (0, 2 * hh + 1))
    return pl.pallas_call(
        body,
        name=name,
        grid=(h,),
        in_specs=[cols, blk(QK_ROPE), keys, pl.BlockSpec((t, QK_ROPE), lambda hh: (0, 0)), values],
        out_specs=[cols, cols, blk(1)],
        out_shape=[jax.ShapeDtypeStruct((t, h * V_HEAD), MXU_DTYPE), jax.ShapeDtypeStruct((t, h * V_HEAD), F32),
                   jax.ShapeDtypeStruct((h, t, 1), F32)],
        compiler_params=_params(("parallel",)),
    )(qn, qp, kv, kp, kv)


def _attn_bwd(qn, qp, kv, kp, do, o, lse, *, name):
    h, t, _ = qp.shape
    tq = _row_tile(t, 256)

    def body(qn_ref, qp_ref, kn_ref, kp_ref, v_ref, do_ref, o_ref, lse_ref, dqn_ref, dqp_ref, dkv_ref, dkp_ref):
        dkv_ref[...] = jnp.zeros_like(dkv_ref)

        @pl.when(pl.program_id(0) == 0)
        def _():
            dkp_ref[...] = jnp.zeros_like(dkp_ref)

        for r0 in range(0, t, tq):
            r1 = r0 + tq
            q_cat = jnp.concatenate([qn_ref[r0:r1, :].astype(MXU_DTYPE), qp_ref[r0:r1, :]], axis=1)
            dov, lse = do_ref[r0:r1, :], lse_ref[r0:r1, :]
            delta = jnp.sum(dov.astype(F32) * o_ref[r0:r1, :], axis=-1, keepdims=True)

            def piece(lo, hi, diagonal):
                s, k_cat = _block_scores(q_cat, kn_ref, kp_ref, lo, hi, diagonal)
                p = jnp.exp(s - lse)
                dp = _dot(dov, v_ref[lo:hi, :], _NT)
                ds_m = (p * (dp - delta) * ATTN_SCALE).astype(MXU_DTYPE)
                dk_cat = _dot(ds_m, q_cat, _TN)
                dkv_ref[lo:hi, :QK_NOPE] += dk_cat[:, :QK_NOPE]
                dkp_ref[lo:hi, :] += dk_cat[:, QK_NOPE:]
                dkv_ref[lo:hi, QK_NOPE:] += _dot(p.astype(MXU_DTYPE), dov, _TN)
                return _dot(ds_m, k_cat, _NN)

            dq_cat = piece(r0, r1, True)
            if r0:
                dq_cat = dq_cat + piece(0, r0, False)
            dqn_ref[r0:r1, :] = dq_cat[:, :QK_NOPE]
            dqp_ref[r0:r1, :] = dq_cat[:, QK_NOPE:]

    blk = lambda w: pl.BlockSpec((None, t, w), lambda hh: (hh, 0, 0))
    kpblk = pl.BlockSpec((t, QK_ROPE), lambda hh: (0, 0))
    cols = pl.BlockSpec((t, QK_NOPE), lambda hh: (0, hh))
    keys = pl.BlockSpec((t, QK_NOPE), lambda hh: (0, 2 * hh))
    values = pl.BlockSpec((t, V_HEAD), lambda hh: (0, 2 * hh + 1))
    both = pl.BlockSpec((t, QK_NOPE + V_HEAD), lambda hh: (0, hh))
    return pl.pallas_call(
        body,
        name=name,
        grid=(h,),
        in_specs=[cols, blk(QK_ROPE), keys, kpblk, values, cols, cols, blk(1)],
        out_specs=[cols, blk(QK_ROPE), both, kpblk],
        out_shape=[jax.ShapeDtypeStruct((t, h * QK_NOPE), F32), jax.ShapeDtypeStruct((h, t, QK_ROPE), F32),
                   jax.ShapeDtypeStruct((t, h * (QK_NOPE + V_HEAD)), F32), jax.ShapeDtypeStruct((t, QK_ROPE), F32)],
        compiler_params=_params(("arbitrary",)),
    )(qn, qp, kv, kp, kv, do, o, lse)


def _rope(x1, x2, cos, sin, *, name, inverse=False):
    w = x1.shape[1]
    sign = -1.0 if inverse else 1.0

    def fn(rows, bc):
        a, b, c, s = rows
        s = sign * s
        return (a * c - b * s, a * s + b * c), ()

    return _rowwise(fn, [x1, x2, cos, sin], [], [(w, F32), (w, F32)], [], name=name)


def _rope_in_place(x, cos_pairs, sin_signed, *, name, inverse=False):
    w = x.shape[1]
    half = ROPE_HALF
    sign = -1.0 if inverse else 1.0

    def fn(rows, bc):
        v, c, s = rows
        lane = lax.broadcasted_iota(jnp.int32, v.shape, 1)
        partner = jnp.where((lane & half) == 0, pltpu.roll(v, w - half, 1), pltpu.roll(v, half, 1))
        return (v * c + partner * (sign * s),), ()

    return _rowwise(fn, [x, cos_pairs, sin_signed], [], [(w, F32)], [], name=name)[0]


def _mlp_up(x_m, w1, tag, after=()):
    def epi(acc):
        r = jnp.maximum(acc, 0.0)
        return (r * r,)

    return _mm(x_m, w1, name=f"{tag}_up", out_dtypes=(MXU_DTYPE,), epilogue=epi, after=after)


def _mlp_fwd(x_m, w1, w2, tag, after=(), act=None):
    if act is None:
        act, after = _mlp_up(x_m, w1, tag, after=after), ()
    y = _mm(act, w2, name=f"{tag}_down", after=after)
    return y, (x_m, act)


def _plus_residual(acc, res):
    return (acc + ALPHA * res,)


def _mlp_bwd(dy_m, saved, w1, w2, tag, residual, after=()):
    x_m, act = saved
    dw2 = _mm(act, dy_m, ta=True, name=f"{tag}_dw2", out_dtypes=(MXU_DTYPE,))

    def epi(acc, av):
        return (acc * (2.0 * jnp.sqrt(av.astype(F32))),)

    dh = _mm(dy_m, w2, tb=True, name=f"{tag}_dact", out_dtypes=(MXU_DTYPE,), epilogue=epi, tile_extras=(act,),
             after=after)
    dw1 = _mm(x_m, dh, ta=True, out_dev=True, name=f"{tag}_dw1", out_dtypes=(MXU_DTYPE,))
    dx = _mm(dh, w1, tb=True, name=f"{tag}_dx", epilogue=_plus_residual, tile_extras=(residual,))
    return [dx], {"ff_w1": dw1, "ff_w2": dw2}


def _sc_gate_specs(t, d):
    per = d // LANES
    return [pl.BlockSpec((t, LANES), lambda j, k=k: (0, k * per + j)) for k in range(3)]


def _sc_gate_fwd(u, conv_w, *, name):
    t, d = u.shape[0], u.shape[1] // 3
    kw = conv_w.shape[0]
    ch = _conv_chunk(t)

    def body(b_ref, c_ref, h_ref, w_ref, r_ref, pad_ref):
        pad_ref[0:CONV_PAD, :] = jnp.zeros((CONV_PAD, LANES), F32)
        pad_ref[CONV_PAD:CONV_PAD + t, :] = c_ref[...] * h_ref[...]
        for t0 in range(0, t, ch):
            q = jnp.zeros((ch, LANES), F32)
            for k in range(kw):
                s = kw - 1 - k
                q = q + w_ref[k:k + 1, :] * pad_ref[CONV_PAD - s + t0:CONV_PAD - s + t0 + ch, :]
            r_ref[t0:t0 + ch, :] = (b_ref[t0:t0 + ch, :] * q).astype(r_ref.dtype)

    blk = pl.BlockSpec((t, LANES), lambda j: (0, j))
    return pl.pallas_call(
        body,
        name=name,
        grid=(d // LANES,),
        in_specs=_sc_gate_specs(t, d) + [pl.BlockSpec((kw, LANES), lambda j: (0, j))],
        out_specs=blk,
        out_shape=jax.ShapeDtypeStruct((t, d), MXU_DTYPE),
        scratch_shapes=[pltpu.VMEM((CONV_PAD + t, LANES), F32)],
        compiler_params=_params(("parallel",)),
    )(u, u, u, conv_w)


def _sc_gate_bwd(dr, u, conv_w, *, name):
    t, d = u.shape[0], u.shape[1] // 3
    kw = conv_w.shape[0]
    ch = _conv_chunk(t)

    def body(dr_ref, b_ref, c_ref, h_ref, w_ref, du_ref, dw_ref, ppad_ref, dpad_ref):
        ppad_ref[0:CONV_PAD, :] = jnp.zeros((CONV_PAD, LANES), F32)
        ppad_ref[CONV_PAD:CONV_PAD + t, :] = c_ref[...] * h_ref[...]
        dpad_ref[0:t, :] = dr_ref[...] * b_ref[...]
        dpad_ref[t:t + CONV_PAD, :] = jnp.zeros((CONV_PAD, LANES), F32)
        for t0 in range(0, t, ch):
            q = jnp.zeros((ch, LANES), F32)
            dp = jnp.zeros((ch, LANES), F32)
            for k in range(kw):
                s = kw - 1 - k
                q = q + w_ref[k:k + 1, :] * ppad_ref[CONV_PAD - s + t0:CONV_PAD - s + t0 + ch, :]
                dp = dp + w_ref[k:k + 1, :] * dpad_ref[t0 + s:t0 + s + ch, :]
            du_ref[0, t0:t0 + ch, :] = (dr_ref[t0:t0 + ch, :] * q).astype(du_ref.dtype)
            du_ref[1, t0:t0 + ch, :] = (dp * h_ref[t0:t0 + ch, :]).astype(du_ref.dtype)
            du_ref[2, t0:t0 + ch, :] = (dp * c_ref[t0:t0 + ch, :]).astype(du_ref.dtype)
        for k in range(kw):
            s = kw - 1 - k
            acc = jnp.zeros((ch, LANES), F32)
            for t0 in range(0, t, ch):
                acc = acc + dpad_ref[t0:t0 + ch, :] * ppad_ref[CONV_PAD - s + t0:CONV_PAD - s + t0 + ch, :]
            dw_ref[k:k + 1, :] = _colsum(acc)

    blk = pl.BlockSpec((t, LANES), lambda j: (0, j))
    wblk = pl.BlockSpec((kw, LANES), lambda j: (0, j))
    return pl.pallas_call(
        body,
        name=name,
        grid=(d // LANES,),
        in_specs=[blk] + _sc_gate_specs(t, d) + [wblk],
        out_specs=[pl.BlockSpec((3, t, LANES), lambda j: (0, 0, j)), wblk],
        out_shape=[jax.ShapeDtypeStruct((3, t, d), MXU_DTYPE), jax.ShapeDtypeStruct((kw, d), F32)],
        scratch_shapes=[pltpu.VMEM((CONV_PAD + t, LANES), F32), pltpu.VMEM((CONV_PAD + t, LANES), F32)],
        compiler_params=_params(("parallel",)),
    )(dr, u, u, u, conv_w)


def _sc_fwd(x_m, w_in, conv_w, w_out, tag, after=()):
    u = _mm(x_m, w_in, name=f"{tag}_in", after=after)
    r = _sc_gate_fwd(u, conv_w, name=f"{tag}_gate")
    y = _mm(r, w_out, name=f"{tag}_out")
    return y, (x_m, u, r)


def _sc_bwd(dy_m, saved, w_in, conv_w, w_out, tag, residual, after=()):
    d = D_MODEL
    x_m, u, r = saved
    dw_out = _mm(r, dy_m, ta=True, name=f"{tag}_dwout", out_dtypes=(MXU_DTYPE,))
    dr = _mm(dy_m, w_out, tb=True, name=f"{tag}_dr", after=after)
    du, dconv = _sc_gate_bwd(dr, u, conv_w, name=f"{tag}_dgate")
    dw_in = _mm(x_m, du, ta=True, b_dev=True, name=f"{tag}_dwin", out_dtypes=(MXU_DTYPE,))
    dx = _mm(du, w_in, tb=True, a_dev=True, name=f"{tag}_dx", epilogue=_plus_residual, tile_extras=(residual,))
    return [dx], {"sc_w_in": dw_in, "sc_conv_w": dconv, "sc_w_out": dw_out}


def _cf_glu_specs(t, d):
    per = d // LANES
    return [pl.BlockSpec((t, LANES), lambda j, k=k: (0, k * per + j)) for k in range(2)]


def _cf_glu_conv_fwd(u, w, bias, *, name):
    t, d = u.shape[0], u.shape[1] // 2
    kw = w.shape[0]
    ch = _conv_chunk(t)

    def body(a_ref, g_ref, w_ref, b_ref, o_ref, pad_ref):
        pad_ref[0:CONV_PAD, :] = jnp.zeros((CONV_PAD, LANES), F32)
        pad_ref[CONV_PAD:CONV_PAD + t, :] = a_ref[...] * _sigmoid(g_ref[...])
        for t0 in range(0, t, ch):
            acc = jnp.broadcast_to(b_ref[...], (ch, LANES))
            for k in range(kw):
                s = kw - 1 - k
                acc = acc + w_ref[k:k + 1, :] * pad_ref[CONV_PAD - s + t0:CONV_PAD - s + t0 + ch, :]
            o_ref[t0:t0 + ch, :] = acc

    return pl.pallas_call(
        body,
        name=name,
        grid=(d // LANES,),
        in_specs=_cf_glu_specs(t, d) + [pl.BlockSpec((kw, LANES), lambda j: (0, j)),
                                        pl.BlockSpec((1, LANES), lambda j: (0, j))],
        out_specs=pl.BlockSpec((t, LANES), lambda j: (0, j)),
        out_shape=jax.ShapeDtypeStruct((t, d), F32),
        scratch_shapes=[pltpu.VMEM((CONV_PAD + t, LANES), F32)],
        compiler_params=_params(("parallel",)),
    )(u, u, w, bias)


def _cf_glu_conv_bwd(dy, u, w, *, name):
    t, d = u.shape[0], u.shape[1] // 2
    kw = w.shape[0]
    ch = _conv_chunk(t)

    def body(dy_ref, a_ref, g_ref, w_ref, du_ref, sa_ref, sg_ref, dw_ref, xpad_ref, dpad_ref):
        xpad_ref[0:CONV_PAD, :] = jnp.zeros((CONV_PAD, LANES), F32)
        xpad_ref[CONV_PAD:CONV_PAD + t, :] = a_ref[...] * _sigmoid(g_ref[...])
        dpad_ref[0:t, :] = dy_ref[...]
        dpad_ref[t:t + CONV_PAD, :] = jnp.zeros((CONV_PAD, LANES), F32)
        sum_a = jnp.zeros((1, LANES), F32)
        sum_g = jnp.zeros((1, LANES), F32)
        for t0 in range(0, t, ch):
            dh1 = jnp.zeros((ch, LANES), F32)
            for k in range(kw):
                s = kw - 1 - k
                dh1 = dh1 + w_ref[k:k + 1, :] * dpad_ref[t0 + s:t0 + s + ch, :]
            sg = _sigmoid(g_ref[t0:t0 + ch, :])
            da = dh1 * sg
            dgate = da * a_ref[t0:t0 + ch, :] * (1.0 - sg)
            du_ref[0, t0:t0 + ch, :] = da.astype(du_ref.dtype)
            du_ref[1, t0:t0 + ch, :] = dgate.astype(du_ref.dtype)
            sum_a = sum_a + _colsum(da)
            sum_g = sum_g + _colsum(dgate)
        sa_ref[...] = sum_a
        sg_ref[...] = sum_g
        for k in range(kw):
            s = kw - 1 - k
            acc = jnp.zeros((ch, LANES), F32)
            for t0 in range(0, t, ch):
                acc = acc + dy_ref[t0:t0 + ch, :] * xpad_ref[CONV_PAD - s + t0:CONV_PAD - s + t0 + ch, :]
            dw_ref[k:k + 1, :] = _colsum(acc)

    blk = pl.BlockSpec((t, LANES), lambda j: (0, j))
    row = pl.BlockSpec((1, LANES), lambda j: (0, j))
    wblk = pl.BlockSpec((kw, LANES), lambda j: (0, j))
    return pl.pallas_call(
        body,
        name=name,
        grid=(d // LANES,),
        in_specs=[blk] + _cf_glu_specs(t, d) + [wblk],
        out_specs=[pl.BlockSpec((2, t, LANES), lambda j: (0, 0, j)), row, row, wblk],
        out_shape=[jax.ShapeDtypeStruct((2, t, d), MXU_DTYPE)] + [jax.ShapeDtypeStruct((1, d), F32)] * 2
        + [jax.ShapeDtypeStruct((kw, d), F32)],
        scratch_shapes=[pltpu.VMEM((CONV_PAD + t, LANES), F32), pltpu.VMEM((CONV_PAD + t, LANES), F32)],
        compiler_params=_params(("parallel",)),
    )(dy, u, u, w)


def _cf_fwd(x_m, w, tag, after=()):
    d = D_MODEL

    def bias_epi(acc, bias):
        return (acc + bias,)

    u = _mm(x_m, w["cf_w_pw1"], name=f"{tag}_pw1", epilogue=bias_epi, col_extras=(w["cf_b_pw1"],),
            after=after)
    h2 = _cf_glu_conv_fwd(u, w["cf_dw_w"], w["cf_dw_b"], name=f"{tag}_conv")

    def norm_fn(rows, bc):
        hv = rows[0]
        mu = jnp.mean(hv, axis=-1, keepdims=True)
        hc = hv - mu
        rstd = lax.rsqrt(jnp.mean(hc * hc, axis=-1, keepdims=True) + LN_EPS)
        h3 = hc * rstd * bc[0] + bc[1]
        return (h3 * _sigmoid(h3),), ()

    h4 = _rowwise(norm_fn, [h2], [w["cf_norm_g"], w["cf_norm_b"]], [(d, MXU_DTYPE)], [], name=f"{tag}_norm")[0]
    y = _mm(h4, w["cf_w_pw2"], name=f"{tag}_pw2", epilogue=bias_epi, col_extras=(w["cf_b_pw2"],))
    return y, (x_m, u, h2, h4)


def _cf_bwd(dy_m, dy_colsum, saved, w, tag, residual, after=()):
    d = D_MODEL
    x_m, u, h2, h4 = saved
    dw_pw2 = _mm(h4, dy_m, ta=True, name=f"{tag}_dwpw2", out_dtypes=(MXU_DTYPE,))
    dh4 = _mm(dy_m, w["cf_w_pw2"], tb=True, name=f"{tag}_dh4", after=after)

    def dnorm_fn(rows, bc):
        dh4v, hv = rows
        g, b = bc
        mu = jnp.mean(hv, axis=-1, keepdims=True)
        hc = hv - mu
        rstd = lax.rsqrt(jnp.mean(hc * hc, axis=-1, keepdims=True) + LN_EPS)
        xh = hc * rstd
        h3 = xh * g + b
        sg = _sigmoid(h3)
        dh3 = dh4v * (sg * (1.0 + h3 * (1.0 - sg)))
        dxh = dh3 * g
        dh2 = rstd * (dxh - jnp.mean(dxh, axis=-1, keepdims=True) - xh * jnp.mean(dxh * xh, axis=-1, keepdims=True))
        return (dh2,), (_colsum(dh3 * xh), _colsum(dh3), _colsum(dh2))

    dh2, dnorm_g, dnorm_b, ddw_b = _rowwise(dnorm_fn, [dh4, h2], [w["cf_norm_g"], w["cf_norm_b"]], [(d, F32)],
                                            [(1, d), (1, d), (1, d)], name=f"{tag}_dnorm")
    du, sum_a, sum_g, ddw_w = _cf_glu_conv_bwd(dh2, u, w["cf_dw_w"], name=f"{tag}_dconv")
    db_pw1 = jnp.concatenate([sum_a, sum_g], axis=1)
    dw_pw1 = _mm(x_m, du, ta=True, b_dev=True, name=f"{tag}_dwpw1", out_dtypes=(MXU_DTYPE,))
    dx = _mm(du, w["cf_w_pw1"], tb=True, a_dev=True, name=f"{tag}_dx", epilogue=_plus_residual,
             tile_extras=(residual,))
    grads = {"cf_w_pw1": dw_pw1, "cf_b_pw1": db_pw1, "cf_dw_w": ddw_w, "cf_dw_b": ddw_b, "cf_norm_g": dnorm_g,
             "cf_norm_b": dnorm_b, "cf_w_pw2": dw_pw2, "cf_b_pw2": dy_colsum}
    return [dx], grads


def _rope_tables(t):
    pos = jnp.arange(t, dtype=F32)
    inv_freq = ROPE_THETA ** (-jnp.arange(0, QK_ROPE, 2, dtype=F32) / QK_ROPE)
    ang = pos[:, None] * inv_freq[None, :]
    return jnp.cos(ang), jnp.sin(ang)


ROPE_HALF = QK_ROPE // 2
Q_NOPE_COLS = MLA_HEADS * QK_NOPE


def _interleave_heads(w_uk, w_uv):
    c = w_uk.shape[0]
    return jnp.concatenate([w_uk.reshape(c, MLA_HEADS, QK_NOPE), w_uv.reshape(c, MLA_HEADS, V_HEAD)],
                           axis=2).reshape(c, -1)


def _mla_fwd(x_m, w, tag, after=()):
    t = x_m.shape[0]
    cos, sin = _rope_tables(t)
    cos_h = jnp.tile(jnp.concatenate([cos, cos], axis=1), (1, MLA_HEADS))
    sin_h = jnp.tile(jnp.concatenate([-sin, sin], axis=1), (1, MLA_HEADS))
    cq_pre = _mm(x_m, w["mla_w_dq"], name=f"{tag}_dq", after=after)
    cq = _rms_fwd(cq_pre, w["mla_g_q"], name=f"{tag}_qnorm")
    q = _mm(cq, w["mla_w_uq"], name=f"{tag}_uq")
    qp = _rope_in_place(q[:, Q_NOPE_COLS:], cos_h, sin_h, name=f"{tag}_qrope")
    qp_h = qp.reshape(t, MLA_HEADS, QK_ROPE).transpose(1, 0, 2).astype(MXU_DTYPE)
    ckv_full = _mm(x_m, w["mla_w_dkv"], name=f"{tag}_dkv")
    ckv_pre = ckv_full[:, :KV_LORA]
    ckv = _rms_fwd(ckv_pre, w["mla_g_kv"], name=f"{tag}_kvnorm")
    k1, k2 = _rope(ckv_full[:, KV_LORA:KV_LORA + ROPE_HALF], ckv_full[:, KV_LORA + ROPE_HALF:], cos, sin,
                   name=f"{tag}_krope")
    kp = jnp.concatenate([k1, k2], axis=-1).astype(MXU_DTYPE)
    kv = _mm(ckv, _interleave_heads(w["mla_w_uk"], w["mla_w_uv"]), name=f"{tag}_ukv", out_dtypes=(MXU_DTYPE,))
    o, o_f, lse = _attn_fwd(q, qp_h, kv, kp, name=f"{tag}_attn")
    y = _mm(o, w["mla_w_o"], name=f"{tag}_o")
    return y, (x_m, cq_pre, cq, ckv_pre, ckv, q, qp_h, kv, kp, lse, o, o_f, (cos, sin, cos_h, sin_h))


def _mla_bwd(dy_m, saved, w, tag, residual, after=()):
    x_m, cq_pre, cq, ckv_pre, ckv, q, qp_h, kv, kp, lse, o, o_f, (cos, sin, cos_h, sin_h) = saved
    t = x_m.shape[0]
    half = ROPE_HALF
    dw_o = _mm(o, dy_m, ta=True, name=f"{tag}_dwo", out_dtypes=(MXU_DTYPE,))
    do = _mm(dy_m, w["mla_w_o"], tb=True, name=f"{tag}_do", out_dtypes=(MXU_DTYPE,), after=after)
    dqn, dqp_h, dkv, dkp = _attn_bwd(q, qp_h, kv, kp, do, o_f, lse, name=f"{tag}_dattn")
    dqp = _rope_in_place(dqp_h.transpose(1, 0, 2).reshape(t, -1), cos_h, sin_h, name=f"{tag}_dqrope", inverse=True)
    dq = jnp.concatenate([dqn, dqp], axis=-1).astype(MXU_DTYPE)
    dw_uq = _mm(cq, dq, ta=True, name=f"{tag}_dwuq", out_dtypes=(MXU_DTYPE,))
    dcq = _mm(dq, w["mla_w_uq"], tb=True, name=f"{tag}_dcq")
    dcq_pre, dg_q = _rms_bwd(cq_pre, w["mla_g_q"], dcq, name=f"{tag}_dqnorm")
    dcq_pre_m = dcq_pre.astype(MXU_DTYPE)
    dw_dq = _mm(x_m, dcq_pre_m, ta=True, name=f"{tag}_dwdq", out_dtypes=(MXU_DTYPE,))
    dx_q = _mm(dcq_pre_m, w["mla_w_dq"], tb=True, name=f"{tag}_dxq", epilogue=_plus_residual, tile_extras=(residual,))
    dw_kv = _mm(ckv, dkv, ta=True, name=f"{tag}_dwukv", out_dtypes=(MXU_DTYPE,))
    dw_kv = dw_kv.reshape(KV_LORA, MLA_HEADS, QK_NOPE + V_HEAD)
    dw_uk, dw_uv = dw_kv[:, :, :QK_NOPE].reshape(KV_LORA, -1), dw_kv[:, :, QK_NOPE:].reshape(KV_LORA, -1)
    dckv = _mm(dkv, _interleave_heads(w["mla_w_uk"], w["mla_w_uv"]), tb=True, name=f"{tag}_dckv")
    dckv_pre, dg_kv = _rms_bwd(ckv_pre, w["mla_g_kv"], dckv, name=f"{tag}_dkvnorm")
    dk1, dk2 = _rope(dkp[:, :half], dkp[:, half:], cos, sin, name=f"{tag}_dkrope", inverse=True)
    dckv_full = jnp.concatenate([dckv_pre, dk1, dk2], axis=-1).astype(MXU_DTYPE)
    dw_dkv = _mm(x_m, dckv_full, ta=True, name=f"{tag}_dwdkv", out_dtypes=(MXU_DTYPE,))
    dx_kv = _mm(dckv_full, w["mla_w_dkv"], tb=True, name=f"{tag}_dxkv", epilogue=lambda acc, other: (acc + other,),
                tile_extras=(dx_q,))
    grads = {"mla_w_dq": dw_dq, "mla_g_q": dg_q, "mla_w_uq": dw_uq, "mla_w_dkv": dw_dkv, "mla_g_kv": dg_kv,
             "mla_w_uk": dw_uk, "mla_w_uv": dw_uv, "mla_w_o": dw_o}
    return [dx_kv], grads


def _cast_input(x):
    cast = lambda rows, bc: ((rows[0],), ())
    return _rowwise(cast, [x], [], [(D_MODEL, MXU_DTYPE)], [], name="x_cast")[0]


def _hooked(value, hook):
    token = None if hook is None else hook(value)
    return () if token is None else (token,)


def _mix_fwd(i, cur, cur_m, wl, hook=None, after=()):
    mixer = i % N_MIXERS
    if mixer == 0:
        y, s_mix = _sc_fwd(cur_m, wl["sc_w_in"], wl["sc_conv_w"], wl["sc_w_out"], f"l{i}_sc", after=after)
    elif mixer == 1:
        y, s_mix = _mla_fwd(cur_m, wl, f"l{i}_mla", after=after)
    else:
        y, s_mix = _cf_fwd(cur_m, wl, f"l{i}_cf", after=after)
    cur, cur_m, xh_mix, rs_mix = _ln_fwd(cur, y, wl["ln_mix_g"], wl["ln_mix_b"], name=f"l{i}_lnmix",
                                         after=_hooked(y, hook))
    return cur, cur_m, (s_mix, xh_mix, rs_mix)


def _ff_fwd(i, cur, cur_m, wl, hook=None, after=(), loss_target=None, act=None):
    y, s_ff = _mlp_fwd(cur_m, wl["ff_w1"], wl["ff_w2"], f"l{i}_ff", after=after, act=act)
    if loss_target is not None:
        xh_ff, rs_ff, dy, sq = _ln_fwd_loss(cur, y, wl["ln_ff_g"], wl["ln_ff_b"], loss_target, name=f"l{i}_lnff_loss",
                                            after=_hooked(y, hook))
        return (0.5 / D_MODEL) * jnp.sum(sq), dy, (s_ff, xh_ff, rs_ff)
    cur, cur_m, xh_ff, rs_ff = _ln_fwd(cur, y, wl["ln_ff_g"], wl["ln_ff_b"], name=f"l{i}_lnff", after=_hooked(y, hook))
    return cur, cur_m, (s_ff, xh_ff, rs_ff)


def _layer_fwd(i, cur, cur_m, wl):
    cur, cur_m, s_mix = _mix_fwd(i, cur, cur_m, wl)
    cur, cur_m, s_ff = _ff_fwd(i, cur, cur_m, wl)
    return cur, cur_m, (*s_mix, *s_ff)


def _loss_and_grad(out, target):
    d = D_MODEL

    def loss_fn(rows, bc):
        err = rows[0] - rows[1]
        return (err * (1.0 / d),), (_colsum(err * err),)

    dy, sq = _rowwise(loss_fn, [out, target], [], [(d, F32)], [(1, d)], name="loss")
    return (0.5 / d) * jnp.sum(sq), dy


def _ff_bwd(i, terms, saved, wl, hook=None, after=()):
    s_ff, xh_ff, rs_ff = saved
    grads = {}
    dz, dz_m, grads["ln_ff_g"], grads["ln_ff_b"], _ = _ln_bwd(terms, xh_ff, rs_ff, wl["ln_ff_g"], name=f"l{i}_dlnff",
                                                              after=after)
    dxs, g_ff = _mlp_bwd(dz_m, s_ff, wl["ff_w1"], wl["ff_w2"], f"l{i}_ff", dz, after=_hooked(dz, hook))
    grads.update(g_ff)
    return [(1.0, v) for v in dxs], grads


def _mix_bwd(i, terms, saved, wl, hook=None, after=()):
    mixer = i % N_MIXERS
    s_mix, xh_mix, rs_mix = saved
    grads = {}
    dz, dz_m, grads["ln_mix_g"], grads["ln_mix_b"], dz_sum = _ln_bwd(terms, xh_mix, rs_mix, wl["ln_mix_g"],
                                                                     name=f"l{i}_dlnmix", after=after)
    behind = _hooked(dz, hook)
    if mixer == 0:
        dxs, g_mix = _sc_bwd(dz_m, s_mix, wl["sc_w_in"], wl["sc_conv_w"], wl["sc_w_out"], f"l{i}_sc", dz, after=behind)
    elif mixer == 1:
        dxs, g_mix = _mla_bwd(dz_m, s_mix, wl, f"l{i}_mla", dz, after=behind)
    else:
        dxs, g_mix = _cf_bwd(dz_m, dz_sum, s_mix, wl, f"l{i}_cf", dz, after=behind)
    grads.update(g_mix)
    return [(1.0, v) for v in dxs], grads


def _layer_bwd(i, terms, saved, wl):
    terms, g_ff = _ff_bwd(i, terms, saved[3:], wl)
    terms, g_mix = _mix_bwd(i, terms, saved[:3], wl)
    return terms, {**g_ff, **g_mix}


def _sum_terms(terms, *, name):
    coefs = [c for c, _ in terms]

    def sum_fn(rows, bc):
        acc = None
        for c, v in zip(coefs, rows):
            tv = v if c == 1.0 else c * v
            acc = tv if acc is None else acc + tv
        return (acc,), ()

    return _rowwise(sum_fn, [v for _, v in terms], [], [(D_MODEL, F32)], [], name=name)[0]


def _local_step(x, target, layers):
    cur, cur_m = x, _cast_input(x)
    saved = []
    for i in range(DEPTH):
        cur, cur_m, s = _layer_fwd(i, cur, cur_m, layers[i])
        saved.append(s)
    loss, dy = _loss_and_grad(cur, target)
    terms = [(1.0, dy)]
    grads = [None] * DEPTH
    for i in reversed(range(DEPTH)):
        terms, grads[i] = _layer_bwd(i, terms, saved[i], layers[i])
    return loss, _sum_terms(terms, name="grad_x"), grads


MESH_ID = pl.DeviceIdType.MESH
ANY = pl.BlockSpec(memory_space=pl.ANY)


def _block_id(px, py, pc):
    return 4 * px + 2 * py + pc


COPIES_PER_GATHER = 7


def _all_gather(shards, *, name):
    n = len(shards)

    def body(*refs):
        x_refs, out_refs = refs[:n], refs[n:2 * n]
        send_sems, recv_sems, local_sems = refs[2 * n:]
        x, y, core = lax.axis_index("x"), lax.axis_index("y"), lax.axis_index("c")
        me, sibling = (x, y, core), (x, y, 1 - core)
        chips = [(1 - x, y), (x, 1 - y), (1 - x, 1 - y)]

        def copy(t, k, block, to, from_input=False):
            dst = out_refs[t].at[_block_id(*block)]
            return pltpu.make_async_remote_copy(
                src_ref=x_refs[t] if from_input else dst, dst_ref=dst,
                send_sem=send_sems.at[t * COPIES_PER_GATHER + k], recv_sem=recv_sems.at[t * COPIES_PER_GATHER + k],
                device_id=to, device_id_type=MESH_ID)

        started = []
        mine = [pltpu.make_async_copy(x_refs[t], out_refs[t].at[_block_id(*me)], local_sems.at[t]) for t in range(n)]
        for t in range(n):
            mine[t].start()
            first = [copy(t, 0, me, sibling, from_input=True)]
            first += [copy(t, 1 + j, me, (*chip, core), from_input=True) for j, chip in enumerate(chips)]
            for cp in first:
                cp.start()
            started += first
        for t in range(n):
            for j, chip in enumerate(chips):
                copy(t, 1 + j, (*chip, core), me).wait_recv()
                passed = copy(t, 4 + j, (*chip, core), sibling)
                passed.start()
                started.append(passed)
        for t in range(n):
            copy(t, 0, sibling, me).wait_recv()
            for j, chip in enumerate(chips):
                copy(t, 4 + j, (*chip, 1 - core), me).wait_recv()
        for cp in started:
            cp.wait_send()
        for cp in mine:
            cp.wait()

    n_sems = n * COPIES_PER_GATHER
    outs = pl.pallas_call(
        body,
        name=name,
        out_shape=[jax.ShapeDtypeStruct((N_DEV, *s.shape), s.dtype) for s in shards],
        in_specs=[ANY] * n,
        out_specs=[ANY] * n,
        scratch_shapes=[pltpu.SemaphoreType.DMA((n_sems,)), pltpu.SemaphoreType.DMA((n_sems,)),
                        pltpu.SemaphoreType.DMA((n,))],
    )(*shards)
    return list(outs)


def _swap_with_sibling(parts, *, name):
    n = len(parts)

    def body(*refs):
        p_refs, got_refs = refs[:n], refs[n:2 * n]
        send_sems, recv_sems = refs[2 * n:]
        x, y, core = lax.axis_index("x"), lax.axis_index("y"), lax.axis_index("c")
        chips = [(x, y), (1 - x, y), (x, 1 - y), (1 - x, 1 - y)]
        copies = []
        for t in range(n):
            for k, (px, py) in enumerate(chips):
                copies.append(pltpu.make_async_remote_copy(
                    src_ref=p_refs[t].at[_block_id(px, py, 1 - core)], dst_ref=got_refs[t].at[k],
                    send_sem=send_sems.at[4 * t + k], recv_sem=recv_sems.at[4 * t + k], device_id=(x, y, 1 - core),
                    device_id_type=MESH_ID))
        for cp in copies:
            cp.start()
        for cp in copies:
            cp.wait()

    outs = pl.pallas_call(
        body,
        name=name,
        out_shape=[jax.ShapeDtypeStruct((4, *p.shape[1:]), p.dtype) for p in parts],
        in_specs=[ANY] * n,
        out_specs=[ANY] * n,
        scratch_shapes=[pltpu.SemaphoreType.DMA((4 * n,)), pltpu.SemaphoreType.DMA((4 * n,))],
    )(*parts)
    return list(outs)


def _swap_between_chips(parts, *, name):
    n = len(parts)

    def body(*refs):
        p_refs, got_refs = refs[:n], refs[n:2 * n]
        send_sems, recv_sems = refs[2 * n:]
        x, y, core = lax.axis_index("x"), lax.axis_index("y"), lax.axis_index("c")
        chips = [(1 - x, y), (x, 1 - y), (1 - x, 1 - y)]
        copies = []
        for t in range(n):
            for j, (px, py) in enumerate(chips):
                copies.append(pltpu.make_async_remote_copy(
                    src_ref=p_refs[t].at[j], dst_ref=got_refs[t].at[j], send_sem=send_sems.at[3 * t + j],
                    recv_sem=recv_sems.at[3 * t + j], device_id=(px, py, core), device_id_type=MESH_ID))
        for cp in copies:
            cp.start()
        for cp in copies:
            cp.wait()

    outs = pl.pallas_call(
        body,
        name=name,
        out_shape=[jax.ShapeDtypeStruct(p.shape, p.dtype) for p in parts],
        in_specs=[ANY] * n,
        out_specs=[ANY] * n,
        scratch_shapes=[pltpu.SemaphoreType.DMA((3 * n,)), pltpu.SemaphoreType.DMA((3 * n,))],
    )(*parts)
    return list(outs)


def _chip_partials_group(parts, got, *, name):
    n = len(parts)
    rows = [p.shape[1] for p in parts]
    steps = 4 if all(r % (4 * BF16_ROWS) == 0 for r in rows) else (2 if all(r % (2 * BF16_ROWS) == 0 for r in rows) else 1)
    x, y, core = lax.axis_index("x"), lax.axis_index("y"), lax.axis_index("c")
    chips = [(x, y), (1 - x, y), (x, 1 - y), (1 - x, 1 - y)]
    ids = jnp.stack([_block_id(px, py, core) for px, py in chips]).astype(jnp.int32)

    def body(ids_ref, *refs):
        ins, outs = refs[:5 * n], refs[5 * n:]
        for t in range(n):
            p0, p1, p2, p3, g_ref = ins[5 * t:5 * t + 5]
            own_ref, out_ref = outs[2 * t], outs[2 * t + 1]
            own_ref[...] = p0[...].astype(F32) + g_ref[0].astype(F32)
            for j, p in enumerate((p1, p2, p3)):
                out_ref[j] = (p[...].astype(F32) + g_ref[j + 1].astype(F32)).astype(out_ref.dtype)

    in_specs, out_specs, out_shape, operands = [], [], [], []
    for p, g in zip(parts, got):
        _, r, c = p.shape
        tb = r // steps
        in_specs += [pl.BlockSpec((None, tb, c), lambda i, ids_ref, k=k: (ids_ref[k], i, 0)) for k in range(4)]
        in_specs.append(pl.BlockSpec((4, tb, c), lambda i, ids_ref: (0, i, 0)))
        out_specs += [pl.BlockSpec((tb, c), lambda i, ids_ref: (i, 0)), pl.BlockSpec((3, tb, c), lambda i, ids_ref: (0, i, 0))]
        out_shape += [jax.ShapeDtypeStruct((r, c), F32), jax.ShapeDtypeStruct((3, r, c), p.dtype)]
        operands += [p, p, p, p, g]
    outs = pl.pallas_call(
        body,
        name=name,
        grid_spec=pltpu.PrefetchScalarGridSpec(num_scalar_prefetch=1, grid=(steps,), in_specs=in_specs,
                                               out_specs=out_specs),
        out_shape=out_shape,
        compiler_params=_params(("parallel",)),
    )(ids, *operands)
    return [(outs[2 * t], outs[2 * t + 1]) for t in range(n)]


def _sum_blocks(first, blocks, *, name):
    n, r, c = blocks.shape
    tb = _row_tile(r, 512)

    def body(*refs):
        b_ref, o_ref = refs[-2], refs[-1]
        acc = refs[0][...] if first is not None else b_ref[0].astype(F32)
        for j in range(0 if first is not None else 1, n):
            acc = acc + b_ref[j].astype(F32)
        o_ref[...] = acc

    row_spec = pl.BlockSpec((tb, c), lambda i: (i, 0))
    return pl.pallas_call(
        body,
        name=name,
        grid=(r // tb,),
        in_specs=([row_spec] if first is not None else []) + [pl.BlockSpec((n, tb, c), lambda i: (0, i, 0))],
        out_specs=row_spec,
        out_shape=jax.ShapeDtypeStruct((r, c), F32),
        compiler_params=_params(("parallel",)),
    )(*([first] if first is not None else []), blocks)


HBM = pl.BlockSpec(memory_space=pltpu.HBM)
SEM = pl.BlockSpec(memory_space=pltpu.SEMAPHORE)
DATAFLOW = pltpu.SideEffectType.DATAFLOW_SIDE_EFFECTING


def _split_call(name, arrays, sems, n_new, body, after=None):
    n_a, n_s = len(arrays), len(sems)
    after = [] if after is None else (list(after) if isinstance(after, (list, tuple)) else [after])
    n_after = len(after)

    def kernel_body(*refs):
        new = refs[n_a + n_s + n_after:n_a + n_s + n_after + 2] if n_new else (None, None)
        body(refs[:n_a], refs[n_a:n_a + n_s], *new)
        refs[-1][...] = jnp.zeros_like(refs[-1])

    n_sem_out = 2 if n_new else 0
    out_shape = [pltpu.SemaphoreType.DMA((n_new,))] * n_sem_out
    out_shape += [pltpu.HBM(a.shape, a.dtype) for a in arrays]
    out_shape += [jax.ShapeDtypeStruct((SUBLANES, LANES), F32)]
    outs = pl.pallas_call(
        kernel_body,
        name=name,
        out_shape=out_shape,
        in_specs=[HBM] * n_a + [SEM] * n_s + [ANY] * n_after,
        out_specs=[SEM] * n_sem_out + [HBM] * n_a + [pl.BlockSpec(memory_space=pltpu.VMEM)],
        input_output_aliases={i: n_sem_out + i for i in range(n_a)},
        compiler_params=pltpu.CompilerParams(has_side_effects=DATAFLOW),
    )(*[pltpu.with_memory_space_constraint(a, pltpu.HBM) for a in arrays], *sems, *after)
    return tuple(outs[:n_sem_out]), list(outs[n_sem_out:n_sem_out + n_a]), outs[-1]


def _remote(src, dst, send, recv, k, to):
    return pltpu.make_async_remote_copy(src_ref=src, dst_ref=dst, send_sem=send.at[k], recv_sem=recv.at[k],
                                        device_id=to, device_id_type=MESH_ID)


def _land_block(land_ref, block, by_cols, width):
    if by_cols:
        return land_ref.at[:, pl.ds(pl.multiple_of(block * width, LANES), width)]
    return land_ref.at[block]


def _place_in_window(stacked, layer, *, name):
    _, r, c = stacked.shape
    tb = _row_tile(r, 512)
    me = _block_id(lax.axis_index("x"), lax.axis_index("y"), lax.axis_index("c")).astype(jnp.int32).reshape(1)

    def body(me_ref, s_ref, o_ref, token_ref):
        o_ref[...] = s_ref[...].astype(o_ref.dtype)
        token_ref[...] = jnp.zeros_like(token_ref)

    return pl.pallas_call(
        body,
        name=name,
        grid_spec=pltpu.PrefetchScalarGridSpec(
            num_scalar_prefetch=1, grid=(r // tb,),
            in_specs=[pl.BlockSpec((None, tb, c), lambda i, me_ref: (layer, i, 0))],
            out_specs=[pl.BlockSpec((tb, c), lambda i, me_ref: (i, me_ref[0])),
                       pl.BlockSpec((SUBLANES, LANES), lambda i, me_ref: (0, 0))]),
        out_shape=[jax.ShapeDtypeStruct((r, N_DEV * c), MXU_DTYPE), jax.ShapeDtypeStruct((SUBLANES, LANES), F32)],
        compiler_params=_params(("arbitrary",)),
    )(me, stacked)


def _gather_prepare(shards, by_cols, *, name):
    me = _block_id(lax.axis_index("x"), lax.axis_index("y"), lax.axis_index("c"))
    lands, early, widths = [], [], []
    for i, (s, cols) in enumerate(zip(shards, by_cols)):
        if cols:
            land, token = _place_in_window(*s, name=f"{name}_place{i}")
            lands.append(land)
            early.append(token)
            widths.append(s[0].shape[-1])
        else:
            lands.append(lax.dynamic_update_index_in_dim(lax.empty((N_DEV, *s.shape), s.dtype), s, me, 0))
            early.append(s)
            widths.append(s.shape[-1])
    return lands, widths, list(by_cols), early


def _gather_start(shards, *, name, after=None, by_cols=None, prepared=None):
    if prepared is None:
        prepared = _gather_prepare(shards, by_cols if by_cols is not None else [False] * len(shards), name=name)
    lands, widths, by_cols, _ = prepared
    n = len(lands)

    def body(refs, _, send, recv):
        x, y, core = lax.axis_index("x"), lax.axis_index("y"), lax.axis_index("c")
        targets = [(x, y, 1 - core), (1 - x, y, core), (x, 1 - y, core), (1 - x, 1 - y, core)]
        for t in range(n):
            mine = _land_block(refs[t], _block_id(x, y, core), by_cols[t], widths[t])
            for k, to in enumerate(targets):
                _remote(mine, mine, send, recv, 4 * t + k, to).start()

    sems, thru, token = _split_call(name, lands, [], 4 * n, body, after=after)
    return {"n": n, "sems1": sems, "arrays": thru, "token": token, "by_cols": by_cols, "widths": widths}


def _gather_pass_on(st, *, name, after):
    n = st["n"]

    def body(refs, sems, send, recv):
        send1, recv1 = sems
        x, y, core = lax.axis_index("x"), lax.axis_index("y"), lax.axis_index("c")
        chips = [(1 - x, y), (x, 1 - y), (1 - x, 1 - y)]
        for t in range(n):
            for j, (px, py) in enumerate(chips):
                block = _land_block(refs[t], _block_id(px, py, core), st["by_cols"][t], st["widths"][t])
                _remote(block, block, send1, recv1, 4 * t + 1 + j, (x, y, 1 - core)).wait_recv()
                _remote(block, block, send, recv, 3 * t + j, (x, y, 1 - core)).start()

    sems2, thru, token = _split_call(name, st["arrays"], list(st["sems1"]), 3 * n, body, after=after)
    return {**st, "sems2": sems2, "arrays": thru, "token": token}


def _gather_finish(st, *, name, after, then_start=None):
    n = st["n"]
    n2 = 0 if then_start is None else len(then_start[0])

    def body(refs, sems, new_send=None, new_recv=None):
        send1, recv1, send2, recv2 = sems
        x, y, core = lax.axis_index("x"), lax.axis_index("y"), lax.axis_index("c")
        sibling = (x, y, 1 - core)
        chips = [(1 - x, y), (x, 1 - y), (1 - x, 1 - y)]
        for t in range(n):
            place = lambda px, py, pc, t=t: _land_block(refs[t], _block_id(px, py, pc), st["by_cols"][t],
                                                        st["widths"][t])
            mine = place(x, y, core)
            for k in range(4):
                _remote(mine, mine, send1, recv1, 4 * t + k, sibling).wait_send()
            _remote(mine, place(x, y, 1 - core), send1, recv1, 4 * t, sibling).wait_recv()
            for j, (px, py) in enumerate(chips):
                _remote(place(px, py, core), place(px, py, core), send2, recv2, 3 * t + j, sibling).wait_send()
                _remote(mine, place(px, py, 1 - core), send2, recv2, 3 * t + j, sibling).wait_recv()
        if n2:
            _, widths, by_cols, _ = then_start
            targets = [sibling, (1 - x, y, core), (x, 1 - y, core), (1 - x, 1 - y, core)]
            for t in range(n2):
                mine = _land_block(refs[n + t], _block_id(x, y, core), by_cols[t], widths[t])
                for k, to in enumerate(targets):
                    _remote(mine, mine, new_send, new_recv, 4 * t + k, to).start()

    arrays = st["arrays"] + (list(then_start[0]) if n2 else [])
    sems, thru, token = _split_call(name, arrays, [*st["sems1"], *st["sems2"]], 4 * n2, body, after=after)
    if not n2:
        return thru
    _, widths, by_cols, _ = then_start
    return thru[:n], {"n": n2, "sems1": sems, "arrays": thru[n:], "token": token, "by_cols": by_cols, "widths": widths}


def _swap_start(sources, lands, plan, per_array, *, name):
    n = len(sources)

    def body(refs, _, send, recv):
        x, y, core = lax.axis_index("x"), lax.axis_index("y"), lax.axis_index("c")
        for t in range(n):
            for k in range(per_array):
                block, to = plan(x, y, core, k)
                _remote(refs[t].at[block], refs[n + t].at[k], send, recv, per_array * t + k, to).start()

    sems, thru, token = _split_call(name, sources + lands, [], per_array * n, body)
    return {"n": n, "sems": sems, "arrays": thru, "token": token, "plan": plan, "per_array": per_array}


def _swap_finish(states, *, name, after):
    many = isinstance(states, (list, tuple))
    states = list(states) if many else [states]

    def body(refs, sems, *_):
        x, y, core = lax.axis_index("x"), lax.axis_index("y"), lax.axis_index("c")
        first = 0
        for j, st in enumerate(states):
            n, plan, per_array = st["n"], st["plan"], st["per_array"]
            send, recv = sems[2 * j], sems[2 * j + 1]
            for t in range(n):
                for k in range(per_array):
                    block, to = plan(x, y, core, k)
                    cp = _remote(refs[first + t].at[block], refs[first + n + t].at[k], send, recv, per_array * t + k, to)
                    cp.wait_send()
                    cp.wait_recv()
            first += 2 * n

    arrays = [a for st in states for a in st["arrays"]]
    _, thru, _ = _split_call(name, arrays, [s for st in states for s in st["sems"]], 0, body, after=after)
    out, first = [], 0
    for st in states:
        n = st["n"]
        out.append((thru[first:first + n], thru[first + n:first + 2 * n]))
        first += 2 * n
    return out if many else out[0]


def _to_sibling(x, y, core, k):
    px, py = [(x, y), (1 - x, y), (x, 1 - y), (1 - x, 1 - y)][k]
    return _block_id(px, py, 1 - core), (x, y, 1 - core)


def _to_chip(x, y, core, k):
    px, py = [(1 - x, y), (x, 1 - y), (1 - x, 1 - y)][k]
    return k, (px, py, core)


PARAMS = {
    "sc_w_in": ((2, 1024, 3072), 2), "sc_conv_w": ((2, 3, 1024), 2), "sc_w_out": ((2, 1024, 1024), 1),
    "mla_w_dq": ((1, 1024, 384), 1), "mla_g_q": ((1, 384), None), "mla_w_uq": ((1, 384, 1536), 2),
    "mla_w_dkv": ((1, 1024, 320), 1), "mla_g_kv": ((1, 256), None), "mla_w_uk": ((1, 256, 8, 128), 1),
    "mla_w_uv": ((1, 256, 8, 128), 1), "mla_w_o": ((1, 1024, 1024), 1), "cf_w_pw1": ((1, 1024, 2048), 2),
    "cf_b_pw1": ((1, 2048), 1), "cf_dw_w": ((1, 31, 1024), 2), "cf_dw_b": ((1, 1024), 1), "cf_norm_g": ((1, 1024), 1),
    "cf_norm_b": ((1, 1024), 1), "cf_w_pw2": ((1, 1024, 1024), 1), "cf_b_pw2": ((1, 1024), 1),
    "ff_w1": ((4, 1024, 4096), 2), "ff_w2": ((4, 4096, 1024), 1), "ln_mix_g": ((4, 1024), None),
    "ln_mix_b": ((4, 1024), None), "ln_ff_g": ((4, 1024), None), "ln_ff_b": ((4, 1024), None),
}
NAMES = list(PARAMS)
BIG = ["sc_w_in", "sc_w_out", "mla_w_dq", "mla_w_uq", "mla_w_dkv", "mla_w_uk", "mla_w_uv", "mla_w_o", "cf_w_pw1",
       "cf_w_pw2", "ff_w1", "ff_w2"]
SMALL = [n for n in NAMES if n not in BIG]
SMALL_SHARDED = [n for n in SMALL if PARAMS[n][1] is not None]
SMALL_WIDTH = LANES
DEV_BLOCKED = ("sc_w_in", "cf_w_pw1", "ff_w1")
MIXER_BIG = {0: ["sc_w_in", "sc_w_out"], 1: ["mla_w_dq", "mla_w_uq", "mla_w_dkv", "mla_w_uk", "mla_w_uv", "mla_w_o"],
             2: ["cf_w_pw1", "cf_w_pw2"]}


def _layer_big(i):
    return [(n, i // N_MIXERS) for n in MIXER_BIG[i % N_MIXERS]] + [("ff_w1", i), ("ff_w2", i)]


def _shard_2d(name, a):
    return a.reshape(a.shape[0], -1)


def _shard_for_gather(name, stacked, layer):
    if name in DEV_BLOCKED:
        return stacked, layer
    return _shard_2d(name, stacked[layer]).astype(MXU_DTYPE)


def _as_operand(name, g):
    if name in DEV_BLOCKED:
        return g
    if name == "mla_w_uq":
        by_head = g.transpose(1, 0, 2)
        pieces = [by_head[:, :, :QK_NOPE], by_head[:, :, QK_NOPE:]]
        return jnp.concatenate([p.reshape(g.shape[1], -1) for p in pieces], axis=1)
    return g.reshape(-1, g.shape[2])


def _as_parts(name, grad):
    if name in DEV_BLOCKED:
        return grad if grad.ndim == 3 else grad.reshape(grad.shape[0], N_DEV, -1).transpose(1, 0, 2)
    if name == "mla_w_uq":
        rows = grad.shape[0]
        pieces = [grad[:, :Q_NOPE_COLS].reshape(rows, MLA_HEADS, QK_NOPE),
                  grad[:, Q_NOPE_COLS:].reshape(rows, MLA_HEADS, QK_ROPE)]
        return jnp.concatenate(pieces, axis=2).transpose(1, 0, 2)
    return grad.reshape(N_DEV, -1, grad.shape[1])


def _shard_shape(name):
    shape, ax = PARAMS[name]
    if ax is None:
        return shape
    return tuple(s // N_DEV if i == ax else s for i, s in enumerate(shape))


def _rows_of(shape, width, align):
    n = 1
    for s in shape:
        n *= s
    rows = -(-n // width)
    return -(-rows // align) * align


def _pack_rows(arrays, width, align, dtype, lead=0):
    segs = []
    for a in arrays:
        batch = a.shape[:lead]
        flat = a.astype(dtype).reshape(*batch, -1)
        rows = _rows_of(a.shape[lead:], width, align)
        flat = jnp.pad(flat, [(0, 0)] * lead + [(0, rows * width - flat.shape[-1])])
        segs.append(flat.reshape(*batch, rows, width))
    return jnp.concatenate(segs, axis=lead)


def _unpack_rows(buf, shapes, width, align):
    out, off = [], 0
    lead = buf.shape[:-2]
    for shape in shapes:
        n = 1
        for s in shape:
            n *= s
        rows = _rows_of(shape, width, align)
        seg = buf[..., off:off + rows, :].reshape(*lead, rows * width)[..., :n]
        out.append(seg.reshape(*lead, *shape))
        off += rows
    return out


def _merge_shards(stacked, ax):
    moved = jnp.moveaxis(stacked, 0, ax)
    shape = moved.shape
    return moved.reshape(*shape[:ax], shape[ax] * shape[ax + 1], *shape[ax + 2:])


def _split_shards(full, ax):
    shape = full.shape
    split = full.reshape(*shape[:ax], N_DEV, shape[ax] // N_DEV, *shape[ax + 1:])
    return jnp.moveaxis(split, ax, 0)


def _adam_math(wv, gv, mv, vv):
    m_new = ADAM_B1 * mv + (1.0 - ADAM_B1) * gv
    v_new = ADAM_B2 * vv + (1.0 - ADAM_B2) * (gv * gv)
    m_hat = m_new / (1.0 - ADAM_B1 ** ADAM_STEP)
    v_hat = v_new / (1.0 - ADAM_B2 ** ADAM_STEP)
    delta = -ADAM_LR * (m_hat / (jnp.sqrt(v_hat) + ADAM_EPS) + ADAM_WD * wv)
    return delta, m_new, v_new


def _adamw(w, g, m, v, *, name):
    shape = w.shape
    c = shape[-1]

    def fn(rows, bc):
        return _adam_math(*rows), ()

    flat = [a.reshape(-1, c) for a in (w, g, m, v)]
    outs = _rowwise(fn, flat, [], [(c, F32)] * 3, [], name=name, tb_cap=512)
    return tuple(o.reshape(shape) for o in outs)


def _adamw_reduced(w, m, v, layer_grads, *, name, after=()):
    n_l, r, c = w.shape
    tb = _row_tile(r, 256 if c > 512 else 512)
    nb = r // tb

    def body(*refs):
        w_ref, m_ref, v_ref = refs[:3]
        g_refs = refs[3:3 + 2 * n_l]
        g_out, d_out, m_out, v_out = refs[3 + 2 * n_l + len(after):]
        layer = pl.program_id(0)
        for l in range(n_l):
            @pl.when(layer == l)
            def _(l=l):
                own, arr = g_refs[2 * l], g_refs[2 * l + 1]
                g = own[...] + arr[0].astype(F32)
                g = g + arr[1].astype(F32)
                g = g + arr[2].astype(F32)
                delta, m_new, v_new = _adam_math(w_ref[...], g, m_ref[...], v_ref[...])
                g_out[...] = g
                d_out[...] = delta
                m_out[...] = m_new
                v_out[...] = v_new

    stacked = pl.BlockSpec((tb, c), lambda l, i: (l * nb + i, 0))
    in_specs = [stacked] * 3
    operands = [a.reshape(n_l * r, c) for a in (w, m, v)]
    for l, (own, arr) in enumerate(layer_grads):
        in_specs.append(pl.BlockSpec((tb, c), lambda ll, i, l=l: (jnp.where(ll == l, i, 0), 0)))
        in_specs.append(pl.BlockSpec((3, tb, c), lambda ll, i, l=l: (0, jnp.where(ll == l, i, 0), 0)))
        operands += [own, arr]
    in_specs += [ANY] * len(after)
    operands += list(after)
    outs = pl.pallas_call(
        body,
        name=name,
        grid=(n_l, nb),
        in_specs=in_specs,
        out_specs=[stacked] * 4,
        out_shape=[jax.ShapeDtypeStruct((n_l * r, c), F32)] * 4,
        compiler_params=_params(("arbitrary", "arbitrary")),
    )(*operands)
    return tuple(o.reshape(n_l, r, c) for o in outs)


def _train_step(x, target, weights, m_state, v_state):
    me = _block_id(lax.axis_index("x"), lax.axis_index("y"), lax.axis_index("c"))

    groups = [(kind, i) for i in range(DEPTH) for kind in ("mix", "ff")]
    forward = [groups[0], ("ff_up", 0), ("ff_down", 0)] + groups[2:]

    def group_names(g):
        kind, i = g
        if kind == "mix":
            return [(n, i // N_MIXERS) for n in MIXER_BIG[i % N_MIXERS]]
        return [(n, i) for n in {"ff": ["ff_w1", "ff_w2"], "ff_up": ["ff_w1"], "ff_down": ["ff_w2"]}[kind]]

    def shards_of(g):
        return [_shard_for_gather(n, weights[n], j) for n, j in group_names(g)]

    whole = lambda g: [n in DEV_BLOCKED for n, _ in group_names(g)]

    small_shapes = [_shard_shape(n) for n in SMALL_SHARDED]
    gather = {0: _gather_start([_pack_rows([weights[n] for n in SMALL_SHARDED], SMALL_WIDTH, SUBLANES, F32)]
                               + shards_of(groups[0]), name="gather_g0_start", by_cols=[False] + whole(groups[0]))}
    cur, cur_m = x[0], x[0]
    prepared = {k: _gather_prepare(shards_of(forward[k]), whole(forward[k]), name=f"gather_g{k}_start")
                for k in range(1, len(forward))}
    early = [a for k in prepared for a in prepared[k][3]]
    gather[0] = _gather_pass_on(gather[0], name="gather_g0_pass", after=early)
    first = _gather_finish(gather[0], name="gather_g0_finish", after=gather[0]["token"])
    small_all, first_matrices = first[0], first[1:]
    small_full = {n: weights[n] for n in SMALL if PARAMS[n][1] is None}
    for n, stacked in zip(SMALL_SHARDED, _unpack_rows(small_all, small_shapes, SMALL_WIDTH, SUBLANES)):
        small_full[n] = _merge_shards(stacked, PARAMS[n][1])

    def small_weights(i):
        wl = {}
        for n in SMALL:
            if n.startswith("ln_"):
                wl[n] = small_full[n][i:i + 1]
            elif n.startswith(("sc_", "mla_", "cf_")[i % N_MIXERS]):
                v = small_full[n][i // N_MIXERS]
                wl[n] = v if v.ndim == 2 else v.reshape(1, -1)
        return wl

    layers = [small_weights(i) for i in range(DEPTH)]

    def install(g, gathered):
        for (n, _), a in zip(group_names(g), gathered):
            layers[g[1]][n] = _as_operand(n, a)

    install(forward[0], first_matrices)
    gather[1] = _gather_start(None, name="gather_g1_start", after=gather[0]["token"], prepared=prepared[1])
    gather[2] = _gather_start(None, name="gather_g2_start", after=gather[1]["token"], prepared=prepared[2])
    saved = {}
    hidden = None
    for k, g in enumerate(forward):
        kind, i = g
        tokens = [gather[1]["token"]] if k == 0 else []
        if k + 2 < len(forward):
            tokens.append(gather[k + 2]["token"])

        def pass_on(y, k=k):
            if k + 1 == len(forward):
                return None
            gather[k + 1] = _gather_pass_on(gather[k + 1], name=f"gather_g{k + 1}_pass", after=y)
            return gather[k + 1]["token"]

        if k + 1 == len(forward):
            loss_local, dy, saved[g] = _ff_fwd(i, cur, cur_m, layers[i], after=tokens, loss_target=target[0])
            break
        if kind == "ff_up":
            hidden = _mlp_up(cur_m, layers[i]["ff_w1"], f"l{i}_ff", after=tokens)
            behind, done = _hooked(hidden, pass_on), hidden
        elif kind == "ff_down":
            cur, cur_m, saved[("ff", i)] = _ff_fwd(i, cur, cur_m, layers[i], hook=pass_on, after=[*tokens, *behind],
                                                   act=hidden)
            done = cur
        else:
            phase = _mix_fwd if kind == "mix" else _ff_fwd
            cur, cur_m, saved[g] = phase(i, cur, cur_m, layers[i], hook=pass_on, after=tokens)
            done = cur
        if k + 3 < len(forward):
            gathered, gather[k + 3] = _gather_finish(gather[k + 1], name=f"gather_g{k + 1}_finish", after=done,
                                                     then_start=prepared[k + 3])
        else:
            gathered = _gather_finish(gather[k + 1], name=f"gather_g{k + 1}_finish", after=done)
        install(forward[k + 1], gathered)

    loss = lax.psum(loss_local, MESH_AXES)

    reduced = {n: [None] * PARAMS[n][0][0] for n in BIG}
    small_grads = {n: [None] * PARAMS[n][0][0] for n in SMALL}
    reduce = {}

    def to_chips(p, after):
        st = reduce[p]
        parts, got = _swap_finish(st["sibling"], name=f"reduce_p{p}_sibling_finish", after=after)
        sums = _chip_partials_group(parts, got, name=f"reduce_p{p}_chipsum")
        st["own"] = [own for own, _ in sums]
        others = [o for _, o in sums]
        st["chips"] = _swap_start(others, [lax.empty(o.shape, o.dtype) for o in others], _to_chip, 3,
                                  name=f"reduce_p{p}_chips_start")
        return st["chips"]["token"]

    def arrived(phases, after):
        done = _swap_finish([reduce[p]["chips"] for p in phases], name=f"reduce_p{phases[0]}_chips_finish", after=after)
        for p, (_, got) in zip(phases, done):
            for (n, j), own, arr in zip(reduce[p]["names"], reduce[p]["own"], got):
                reduced[n][j] = (own, arr)

    terms = [(1.0, dy)]
    back = list(reversed(groups))
    for p, g in enumerate(back):
        kind, i = g

        def after_first_kernel(dz, p=p):
            return to_chips(p - 1, dz) if p >= 1 else None

        phase = _mix_bwd if kind == "mix" else _ff_bwd
        tokens = [reduce[p - 1]["sibling"]["token"]] if p >= 1 else []
        terms, grads = phase(i, terms, saved[g], layers[i], hook=after_first_kernel, after=tokens)
        for n in SMALL:
            if n in grads:
                small_grads[n][i if n.startswith("ln_") else i // N_MIXERS] = grads[n]
        names = group_names(g)
        parts = [_as_parts(n, grads[n]) for n, _ in names]
        reduce[p] = {"names": names, "sibling": _swap_start(
            parts, [lax.empty((4, *a.shape[1:]), a.dtype) for a in parts], _to_sibling, 4,
            name=f"reduce_p{p}_sibling_start")}
    (_, grad_x), = terms
    last = len(back) - 1
    small_mine = _pack_rows([jnp.stack(small_grads[n]).reshape(PARAMS[n][0]) for n in SMALL], SMALL_WIDTH, SUBLANES, F32)
    small_gather = _gather_start([small_mine], name="gather_small_grads_start")
    tail = to_chips(last, small_gather["token"])
    arrived(list(range(last)), tail)

    grad, delta, new_m, new_v = {}, {}, {}, {}

    def adamw_matrix(n, after):
        shape = weights[n].shape
        view = lambda a: a.reshape(shape[0], shape[1], -1)
        outs = _adamw_reduced(view(weights[n]), view(m_state[n]), view(v_state[n]), reduced[n], name=f"adamw_{n}",
                              after=after)
        grad[n], delta[n], new_m[n], new_v[n] = [o.reshape(shape) for o in outs]
        return outs[0]

    done = tail
    for n in BIG:
        if n not in MIXER_BIG[0]:
            done = adamw_matrix(n, [done])
    small_gather = _gather_pass_on(small_gather, name="gather_small_grads_pass", after=done)
    small_all = _gather_finish(small_gather, name="gather_small_grads_finish", after=small_gather["token"])[0]
    small_sum = _sum_blocks(None, small_all, name="sum_small_grads")
    for n, g in zip(SMALL, _unpack_rows(small_sum, [PARAMS[n][0] for n in SMALL], SMALL_WIDTH, SUBLANES)):
        ax = PARAMS[n][1]
        grad[n] = g if ax is None else lax.dynamic_index_in_dim(_split_shards(g, ax), me, axis=0, keepdims=False)
    small_shards = [_shard_shape(n) for n in SMALL]
    packed = [_pack_rows([src[n] for n in SMALL], SMALL_WIDTH, SUBLANES, F32) for src in (weights, grad, m_state, v_state)]
    small_out = _adamw(*packed, name="adamw_small")
    for outs, dst in zip(small_out, (delta, new_m, new_v)):
        for n, a in zip(SMALL, _unpack_rows(outs, small_shards, SMALL_WIDTH, SUBLANES)):
            dst[n] = a
    arrived([last], small_sum)
    for n in MIXER_BIG[0]:
        adamw_matrix(n, [])
    return (loss, grad_x[None], *[grad[n] for n in NAMES], *[delta[n] for n in NAMES],
            *[new_m[n] for n in NAMES], *[new_v[n] for n in NAMES])


def kernel(x, sc_w_in, sc_conv_w, sc_w_out, mla_w_dq, mla_g_q, mla_w_uq, mla_w_dkv, mla_g_kv, mla_w_uk, mla_w_uv, mla_w_o, cf_w_pw1, cf_b_pw1, cf_dw_w, cf_dw_b, cf_norm_g, cf_norm_b, cf_w_pw2, cf_b_pw2, ff_w1, ff_w2, ln_mix_g, ln_mix_b, ln_ff_g, ln_ff_b, loss_target, m_sc_w_in, m_sc_conv_w, m_sc_w_out, m_mla_w_dq, m_mla_g_q, m_mla_w_uq, m_mla_w_dkv, m_mla_g_kv, m_mla_w_uk, m_mla_w_uv, m_mla_w_o, m_cf_w_pw1, m_cf_b_pw1, m_cf_dw_w, m_cf_dw_b, m_cf_norm_g, m_cf_norm_b, m_cf_w_pw2, m_cf_b_pw2, m_ff_w1, m_ff_w2, m_ln_mix_g, m_ln_mix_b, m_ln_ff_g, m_ln_ff_b, v_sc_w_in, v_sc_conv_w, v_sc_w_out, v_mla_w_dq, v_mla_g_q, v_mla_w_uq, v_mla_w_dkv, v_mla_g_kv, v_mla_w_uk, v_mla_w_uv, v_mla_w_o, v_cf_w_pw1, v_cf_b_pw1, v_cf_dw_w, v_cf_dw_b, v_cf_norm_g, v_cf_norm_b, v_cf_w_pw2, v_cf_b_pw2, v_ff_w1, v_ff_w2, v_ln_mix_g, v_ln_mix_b, v_ln_ff_g, v_ln_ff_b):
    w_list = (sc_w_in, sc_conv_w, sc_w_out, mla_w_dq, mla_g_q, mla_w_uq, mla_w_dkv, mla_g_kv, mla_w_uk, mla_w_uv, mla_w_o,
              cf_w_pw1, cf_b_pw1, cf_dw_w, cf_dw_b, cf_norm_g, cf_norm_b, cf_w_pw2, cf_b_pw2, ff_w1, ff_w2, ln_mix_g,
              ln_mix_b, ln_ff_g, ln_ff_b)
    m_list = (m_sc_w_in, m_sc_conv_w, m_sc_w_out, m_mla_w_dq, m_mla_g_q, m_mla_w_uq, m_mla_w_dkv, m_mla_g_kv, m_mla_w_uk,
              m_mla_w_uv, m_mla_w_o, m_cf_w_pw1, m_cf_b_pw1, m_cf_dw_w, m_cf_dw_b, m_cf_norm_g, m_cf_norm_b, m_cf_w_pw2,
              m_cf_b_pw2, m_ff_w1, m_ff_w2, m_ln_mix_g, m_ln_mix_b, m_ln_ff_g, m_ln_ff_b)
    v_list = (v_sc_w_in, v_sc_conv_w, v_sc_w_out, v_mla_w_dq, v_mla_g_q, v_mla_w_uq, v_mla_w_dkv, v_mla_g_kv, v_mla_w_uk,
              v_mla_w_uv, v_mla_w_o, v_cf_w_pw1, v_cf_b_pw1, v_cf_dw_w, v_cf_dw_b, v_cf_norm_g, v_cf_norm_b, v_cf_w_pw2,
              v_cf_b_pw2, v_ff_w1, v_ff_w2, v_ln_mix_g, v_ln_mix_b, v_ln_ff_g, v_ln_ff_b)
    return _train_step(x, loss_target, dict(zip(NAMES, w_list)), dict(zip(NAMES, m_list)), dict(zip(NAMES, v_list)))
```

```python
import jax
import jax.numpy as jnp
from jax import lax
from jax.experimental import pallas as pl
from jax.experimental.pallas import tpu as pltpu

F32 = jnp.float32
MXU_DTYPE = jnp.bfloat16

N_DEV = 8
D_MODEL = 1024
DEPTH = 4
N_MIXERS = 3
CHUNK = 64
ALPHA = (2.0 * DEPTH) ** 0.25
LN_EPS = 1e-5
RMS_EPS = 1e-6
MLA_HEADS = 8
QK_NOPE = 128
QK_ROPE = 64
V_HEAD = 128
Q_LORA = 384
KV_LORA = 256
ROPE_THETA = 10000.0
ADAM_LR = 0.001
ADAM_B1 = 0.9
ADAM_B2 = 0.999
ADAM_EPS = 1e-08
ADAM_WD = 0.01
ADAM_STEP = 10

LANES = 128
SUBLANES = 8
BF16_ROWS = 16
VMEM_LIMIT = 56 * 1024 * 1024
MM_VMEM_BUDGET = 36 * 1024 * 1024
CONV_PAD = 32
MASK_VALUE = -1e30

MESH_AXES = ("x", "y", "c")


def _params(semantics):
    return pltpu.CompilerParams(dimension_semantics=semantics, vmem_limit_bytes=VMEM_LIMIT)


def _tile(n, cap):
    if n <= cap:
        return n
    t = cap - cap % LANES
    while t >= LANES:
        if n % t == 0:
            return t
        t -= LANES
    raise ValueError(f"no tile for {n} under {cap}")


def _row_tile(rows, cap):
    if rows <= cap:
        return rows
    t = cap - cap % SUBLANES
    while t >= SUBLANES:
        if rows % t == 0:
            return t
        t -= SUBLANES
    raise ValueError(f"no row tile for {rows} under {cap}")


def _mm(a, b, *, name, ta=False, tb=False, b_dev=False, out_dev=False, out_dtypes=(F32,), epilogue=None,
        tile_extras=(), col_extras=(), after=(), tm_cap=2048, tn_cap=1024, tk_cap=4096):
    m, k = (a.shape[1], a.shape[0]) if ta else a.shape
    if b_dev:
        nd, b_rows, ns = b.shape
        n, kb = (b_rows, nd * ns) if tb else (nd * ns, b_rows)
    else:
        n, kb = (b.shape[0], b.shape[1]) if tb else (b.shape[1], b.shape[0])
    assert k == kb, (a.shape, b.shape, ta, tb)
    if ta or tile_extras or (b_dev and tb):
        tm_cap = min(tm_cap, 1024)
    tk = ns if (b_dev and tb) else _tile(k, tk_cap)
    nk = k // tk
    fixed_tn = ns if (b_dev and not tb) else (n // N_DEV if out_dev else None)
    if out_dev:
        assert not tile_extras and not col_extras and n % N_DEV == 0
    out_bytes = sum(jnp.dtype(dt).itemsize for dt in out_dtypes) + sum(e.dtype.itemsize for e in tile_extras)
    tm, tn = None, None
    for cand_m, cand_n in ((tm_cap, tn_cap), (tm_cap, tn_cap // 2), (tm_cap // 2, tn_cap), (tm_cap // 2, tn_cap // 2),
                           (tm_cap // 4, tn_cap // 2), (tm_cap // 4, tn_cap // 4)):
        tm, tn = _tile(m, cand_m), (fixed_tn if fixed_tn is not None else _tile(n, cand_n))
        need = 2 * (tm * tk * a.dtype.itemsize + tk * tn * b.dtype.itemsize) + 2 * tm * tn * out_bytes
        need += tm * tn * 4 * (2 if nk > 1 else 1)
        if need <= MM_VMEM_BUDGET:
            break
    n_te, n_ce, n_out = len(tile_extras), len(col_extras), len(out_dtypes)
    dims = (((0 if ta else 1,), (1 if tb else 0,)), ((), ()))

    def body(*refs):
        a_ref, b_ref = refs[0], refs[1]
        te_refs = refs[2:2 + n_te]
        ce_refs = refs[2 + n_te:2 + n_te + n_ce]
        first_out = 2 + n_te + n_ce + len(after)
        out_refs = refs[first_out:first_out + n_out]

        def finish(acc):
            if epilogue is None:
                outs = (acc,)
            else:
                outs = epilogue(acc, *[r[...] for r in te_refs], *[r[...] for r in ce_refs])
            for o_ref, o in zip(out_refs, outs):
                o_ref[...] = o.astype(o_ref.dtype)

        def product():
            return lax.dot_general(a_ref[...].astype(MXU_DTYPE), b_ref[...].astype(MXU_DTYPE), dims,
                                   preferred_element_type=F32)

        if nk == 1:
            finish(product())
        else:
            acc_ref = refs[-1]
            kk = pl.program_id(2)

            @pl.when(kk == 0)
            def _():
                acc_ref[...] = jnp.zeros_like(acc_ref)

            acc_ref[...] += product()

            @pl.when(kk == nk - 1)
            def _():
                finish(acc_ref[...])

    a_spec = pl.BlockSpec((tk, tm), lambda i, j, kk: (kk, i)) if ta else pl.BlockSpec((tm, tk), lambda i, j, kk: (i, kk))
    if b_dev and tb:
        b_spec = pl.BlockSpec((None, tn, tk), lambda i, j, kk: (kk, j, 0))
    elif b_dev:
        b_spec = pl.BlockSpec((None, tk, tn), lambda i, j, kk: (j, kk, 0))
    elif tb:
        b_spec = pl.BlockSpec((tn, tk), lambda i, j, kk: (j, kk))
    else:
        b_spec = pl.BlockSpec((tk, tn), lambda i, j, kk: (kk, j))
    tile_spec = pl.BlockSpec((tm, tn), lambda i, j, kk: (i, j))
    col_spec = pl.BlockSpec((1, tn), lambda i, j, kk: (0, j))
    if out_dev:
        out_spec = pl.BlockSpec((None, tm, tn), lambda i, j, kk: (j, i, 0))
        out_shape = [jax.ShapeDtypeStruct((N_DEV, m, tn), dt) for dt in out_dtypes]
    else:
        out_spec = tile_spec
        out_shape = [jax.ShapeDtypeStruct((m, n), dt) for dt in out_dtypes]
    outs = pl.pallas_call(
        body,
        name=name,
        grid=(m // tm, n // tn, nk),
        in_specs=[a_spec, b_spec] + [tile_spec] * n_te + [col_spec] * n_ce + [ANY] * len(after),
        out_specs=[out_spec] * n_out,
        out_shape=out_shape,
        scratch_shapes=[pltpu.VMEM((tm, tn), F32)] if nk > 1 else [],
        compiler_params=_params(("parallel", "parallel", "arbitrary")),
    )(a, b, *tile_extras, *col_extras, *after)
    return outs[0] if n_out == 1 else tuple(outs)


def _rowwise(fn, rows, bcast, out_rows, out_accs, *, name, tb_cap=256, after=()):
    t = rows[0].shape[0]
    tb = _row_tile(t, tb_cap)
    n_r, n_b, n_o, n_a = len(rows), len(bcast), len(out_rows), len(out_accs)

    def body(*refs):
        r_refs = refs[:n_r]
        b_refs = refs[n_r:n_r + n_b]
        first_out = n_r + n_b + len(after)
        o_refs = refs[first_out:first_out + n_o]
        a_refs = refs[first_out + n_o:]
        outs, accs = fn([r[...] for r in r_refs], [r[...] for r in b_refs])
        for o_ref, o in zip(o_refs, outs):
            o_ref[...] = o.astype(o_ref.dtype)
        if n_a:
            i = pl.program_id(0)

            @pl.when(i == 0)
            def _():
                for a_ref, acc in zip(a_refs, accs):
                    a_ref[...] = acc

            @pl.when(i > 0)
            def _():
                for a_ref, acc in zip(a_refs, accs):
                    a_ref[...] += acc

    in_specs = [pl.BlockSpec((tb, r.shape[1]), lambda i: (i, 0)) for r in rows]
    in_specs += [pl.BlockSpec(b.shape, lambda i: (0, 0)) for b in bcast]
    in_specs += [ANY] * len(after)
    out_specs = [pl.BlockSpec((tb, c), lambda i: (i, 0)) for c, _ in out_rows]
    out_specs += [pl.BlockSpec(s, lambda i: (0, 0)) for s in out_accs]
    out_shape = [jax.ShapeDtypeStruct((t, c), dt) for c, dt in out_rows]
    out_shape += [jax.ShapeDtypeStruct(s, F32) for s in out_accs]
    outs = pl.pallas_call(
        body,
        name=name,
        grid=(t // tb,),
        in_specs=in_specs,
        out_specs=out_specs,
        out_shape=out_shape,
        compiler_params=_params(("arbitrary",)),
    )(*rows, *bcast, *after)
    return tuple(outs)


def _colsum(v):
    return jnp.sum(v, axis=0, keepdims=True)


def _sigmoid(v):
    return 1.0 / (1.0 + jnp.exp(-v))


def _ln_fwd(x_res, y_sub, g, b, *, name, after=()):
    d = x_res.shape[1]

    def fn(rows, bc):
        z = ALPHA * rows[0] + rows[1]
        mu = jnp.mean(z, axis=-1, keepdims=True)
        zc = z - mu
        var = jnp.mean(zc * zc, axis=-1, keepdims=True)
        rstd = lax.rsqrt(var + LN_EPS)
        xhat = zc * rstd
        out = xhat * bc[0] + bc[1]
        return (out, out, xhat, rstd), ()

    return _rowwise(fn, [x_res, y_sub], [g, b], [(d, F32), (d, MXU_DTYPE), (d, F32), (1, F32)], [], name=name,
                    after=after, tb_cap=512)


def _ln_fwd_loss(x_res, y_sub, g, b, target, *, name, after=()):
    d = x_res.shape[1]

    def fn(rows, bc):
        z = ALPHA * rows[0] + rows[1]
        mu = jnp.mean(z, axis=-1, keepdims=True)
        zc = z - mu
        var = jnp.mean(zc * zc, axis=-1, keepdims=True)
        rstd = lax.rsqrt(var + LN_EPS)
        xhat = zc * rstd
        err = xhat * bc[0] + bc[1] - rows[2]
        return (xhat, rstd, err * (1.0 / d)), (_colsum(err * err),)

    return _rowwise(fn, [x_res, y_sub, target], [g, b], [(d, F32), (1, F32), (d, F32)], [(1, d)], name=name,
                    after=after, tb_cap=512)


def _ln_bwd(terms, xhat, rstd, g, *, name, after=()):
    d = xhat.shape[1]
    coefs = [c for c, _ in terms]

    def fn(rows, bc):
        xh, rs = rows[0], rows[1]
        dout = None
        for c, v in zip(coefs, rows[2:]):
            tv = v if c == 1.0 else c * v
            dout = tv if dout is None else dout + tv
        dxh = dout * bc[0]
        m1 = jnp.mean(dxh, axis=-1, keepdims=True)
        m2 = jnp.mean(dxh * xh, axis=-1, keepdims=True)
        dz = rs * (dxh - m1 - xh * m2)
        return (dz, dz), (_colsum(dout * xh), _colsum(dout), _colsum(dz))

    return _rowwise(fn, [xhat, rstd] + [v for _, v in terms], [g], [(d, F32), (d, MXU_DTYPE)],
                    [(1, d), (1, d), (1, d)], name=name, after=after, tb_cap=512)


def _rms_fwd(x, g, *, name):
    c = x.shape[1]

    def fn(rows, bc):
        v = rows[0]
        r = lax.rsqrt(jnp.mean(v * v, axis=-1, keepdims=True) + RMS_EPS)
        return (v * r * bc[0],), ()

    return _rowwise(fn, [x], [g], [(c, MXU_DTYPE)], [], name=name)[0]


def _rms_bwd(x, g, dy, *, name):
    c = x.shape[1]

    def fn(rows, bc):
        v, dyv = rows
        r = lax.rsqrt(jnp.mean(v * v, axis=-1, keepdims=True) + RMS_EPS)
        xh = v * r
        dxh = dyv * bc[0]
        dx = r * (dxh - xh * jnp.mean(dxh * xh, axis=-1, keepdims=True))
        return (dx,), (_colsum(dyv * xh),)

    return _rowwise(fn, [x, dy], [g], [(c, F32)], [(1, c)], name=name)


def _conv_chunk(t):
    return _row_tile(t, 256)


ATTN_SCALE = (QK_NOPE + QK_ROPE) ** -0.5
_NT = (((1,), (1,)), ((), ()))
_TN = (((0,), (0,)), ((), ()))
_NN = (((1,), (0,)), ((), ()))


def _dot(a, b, dims):
    return lax.dot_general(a, b, dims, preferred_element_type=F32)


def _block_scores(q_cat, kn_ref, kp_ref, lo, hi, diagonal):
    k_cat = jnp.concatenate([kn_ref[lo:hi, :], kp_ref[lo:hi, :]], axis=1)
    s = _dot(q_cat, k_cat, _NT) * ATTN_SCALE
    if diagonal:
        shift = CHUNK.bit_length() - 1
        rows = lax.broadcasted_iota(jnp.int32, s.shape, 0)
        cols = lax.broadcasted_iota(jnp.int32, s.shape, 1)
        s = jnp.where(jnp.right_shift(cols, shift) <= jnp.right_shift(rows, shift), s, MASK_VALUE)
    return s, k_cat


def _attn_fwd(qn, qp, kv, kp, *, name):
    h, t, _ = qp.shape
    tq = _row_tile(t, 256)
    assert tq % CHUNK == 0

    def body(qn_ref, qp_ref, kn_ref, kp_ref, v_ref, o_ref, of_ref, lse_ref):
        for r0 in range(0, t, tq):
            r1 = r0 + tq
            q_cat = jnp.concatenate([qn_ref[r0:r1, :].astype(MXU_DTYPE), qp_ref[r0:r1, :]], axis=1)
            s_d, _ = _block_scores(q_cat, kn_ref, kp_ref, r0, r1, True)
            m = jnp.max(s_d, axis=-1, keepdims=True)
            if r0:
                s_b, _ = _block_scores(q_cat, kn_ref, kp_ref, 0, r0, False)
                m = jnp.maximum(m, jnp.max(s_b, axis=-1, keepdims=True))
            p_d = jnp.exp(s_d - m)
            l = jnp.sum(p_d, axis=-1, keepdims=True)
            acc = _dot(p_d.astype(MXU_DTYPE), v_ref[r0:r1, :], _NN)
            if r0:
                p_b = jnp.exp(s_b - m)
                l = l + jnp.sum(p_b, axis=-1, keepdims=True)
                acc = acc + _dot(p_b.astype(MXU_DTYPE), v_ref[0:r0, :], _NN)
            o = acc / l
            o_ref[r0:r1, :] = o.astype(o_ref.dtype)
            of_ref[r0:r1, :] = o
            lse_ref[r0:r1, :] = m + jnp.log(l)

    blk = lambda w: pl.BlockSpec((None, t, w), lambda hh: (hh, 0, 0))
    cols = pl.BlockSpec((t, QK_NOPE), lambda hh: (0, hh))
    keys = pl.BlockSpec((t, QK_NOPE), lambda hh: (0, 2 * hh))
    values = pl.BlockSpec((t, V_HEAD), lambda hh: (0, 2 * hh + 1))
    return pl.pallas_call(
        body,
        name=name,
        grid=(h,),
        in_specs=[cols, blk(QK_ROPE), keys, pl.BlockSpec((t, QK_ROPE), lambda hh: (0, 0)), values],
        out_specs=[cols, cols, blk(1)],
        out_shape=[jax.ShapeDtypeStruct((t, h * V_HEAD), MXU_DTYPE), jax.ShapeDtypeStruct((t, h * V_HEAD), F32),
                   jax.ShapeDtypeStruct((h, t, 1), F32)],
        compiler_params=_params(("parallel",)),
    )(qn, qp, kv, kp, kv)


def _attn_bwd(qn, qp, kv, kp, do, o, lse, *, name):
    h, t, _ = qp.shape
    tq = _row_tile(t, 256)

    def body(qn_ref, qp_ref, kn_ref, kp_ref, v_ref, do_ref, o_ref, lse_ref, dqn_ref, dqp_ref, dkv_ref, dkp_ref):
        dkv_ref[...] = jnp.zeros_like(dkv_ref)

        @pl.when(pl.program_id(0) == 0)
        def _():
            dkp_ref[...] = jnp.zeros_like(dkp_ref)

        for r0 in range(0, t, tq):
            r1 = r0 + tq
            q_cat = jnp.concatenate([qn_ref[r0:r1, :].astype(MXU_DTYPE), qp_ref[r0:r1, :]], axis=1)
            dov, lse = do_ref[r0:r1, :], lse_ref[r0:r1, :]
            delta = jnp.sum(dov.astype(F32) * o_ref[r0:r1, :], axis=-1, keepdims=True)

            def piece(lo, hi, diagonal):
                s, k_cat = _block_scores(q_cat, kn_ref, kp_ref, lo, hi, diagonal)
                p = jnp.exp(s - lse)
                dp = _dot(dov, v_ref[lo:hi, :], _NT)
                ds_m = (p * (dp - delta) * ATTN_SCALE).astype(MXU_DTYPE)
                dk_cat = _dot(ds_m, q_cat, _TN)
                dkv_ref[lo:hi, :QK_NOPE] += dk_cat[:, :QK_NOPE]
                dkp_ref[lo:hi, :] += dk_cat[:, QK_NOPE:]
                dkv_ref[lo:hi, QK_NOPE:] += _dot(p.astype(MXU_DTYPE), dov, _TN)
                return _dot(ds_m, k_cat, _NN)

            dq_cat = piece(r0, r1, True)
            if r0:
                dq_cat = dq_cat + piece(0, r0, False)
            dqn_ref[r0:r1, :] = dq_cat[:, :QK_NOPE]
            dqp_ref[r0:r1, :] = dq_cat[:, QK_NOPE:]

    blk = lambda w: pl.BlockSpec((None, t, w), lambda hh: (hh, 0, 0))
    kpblk = pl.BlockSpec((t, QK_ROPE), lambda hh: (0, 0))
    cols = pl.BlockSpec((t, QK_NOPE), lambda hh: (0, hh))
    keys = pl.BlockSpec((t, QK_NOPE), lambda hh: (0, 2 * hh))
    values = pl.BlockSpec((t, V_HEAD), lambda hh: (0, 2 * hh + 1))
    both = pl.BlockSpec((t, QK_NOPE + V_HEAD), lambda hh: (0, hh))
    return pl.pallas_call(
        body,
        name=name,
        grid=(h,),
        in_specs=[cols, blk(QK_ROPE), keys, kpblk, values, cols, cols, blk(1)],
        out_specs=[cols, blk(QK_ROPE), both, kpblk],
        out_shape=[jax.ShapeDtypeStruct((t, h * QK_NOPE), F32), jax.ShapeDtypeStruct((h, t, QK_ROPE), F32),
                   jax.ShapeDtypeStruct((t, h * (QK_NOPE + V_HEAD)), F32), jax.ShapeDtypeStruct((t, QK_ROPE), F32)],
        compiler_params=_params(("arbitrary",)),
    )(qn, qp, kv, kp, kv, do, o, lse)


def _rope(x1, x2, cos, sin, *, name, inverse=False):
    w = x1.shape[1]
    sign = -1.0 if inverse else 1.0

    def fn(rows, bc):
        a, b, c, s = rows
        s = sign * s
        return (a * c - b * s, a * s + b * c), ()

    return _rowwise(fn, [x1, x2, cos, sin], [], [(w, F32), (w, F32)], [], name=name)


def _rope_in_place(x, cos_pairs, sin_signed, *, name, inverse=False):
    w = x.shape[1]
    half = ROPE_HALF
    sign = -1.0 if inverse else 1.0

    def fn(rows, bc):
        v, c, s = rows
        lane = lax.broadcasted_iota(jnp.int32, v.shape, 1)
        partner = jnp.where((lane & half) == 0, pltpu.roll(v, w - half, 1), pltpu.roll(v, half, 1))
        return (v * c + partner * (sign * s),), ()

    return _rowwise(fn, [x, cos_pairs, sin_signed], [], [(w, F32)], [], name=name)[0]


def _mlp_up(x_m, w1, tag, after=()):
    def epi(acc):
        r = jnp.maximum(acc, 0.0)
        return (r * r,)

    return _mm(x_m, w1, name=f"{tag}_up", out_dtypes=(MXU_DTYPE,), epilogue=epi, after=after)


def _mlp_fwd(x_m, w1, w2, tag, after=(), act=None):
    if act is None:
        act, after = _mlp_up(x_m, w1, tag, after=after), ()
    y = _mm(act, w2, name=f"{tag}_down", after=after)
    return y, (x_m, act)


def _plus_residual(acc, res):
    return (acc + ALPHA * res,)


def _mlp_bwd(dy_m, saved, w1, w2, tag, residual, after=()):
    x_m, act = saved
    dw2 = _mm(act, dy_m, ta=True, name=f"{tag}_dw2", out_dtypes=(MXU_DTYPE,))

    def epi(acc, av):
        return (acc * (2.0 * jnp.sqrt(av.astype(F32))),)

    dh = _mm(dy_m, w2, tb=True, name=f"{tag}_dact", out_dtypes=(MXU_DTYPE,), epilogue=epi, tile_extras=(act,),
             after=after)
    dw1 = _mm(x_m, dh, ta=True, out_dev=True, name=f"{tag}_dw1", out_dtypes=(MXU_DTYPE,))
    dx = _mm(dh, w1, tb=True, name=f"{tag}_dx", epilogue=_plus_residual, tile_extras=(residual,))
    return [dx], {"ff_w1": dw1, "ff_w2": dw2}


def _sc_gate_specs(t, d):
    per = d // LANES
    return [pl.BlockSpec((t, LANES), lambda j, k=k: (0, k * per + j)) for k in range(3)]


def _sc_gate_fwd(u, conv_w, *, name):
    t, d = u.shape[0], u.shape[1] // 3
    kw = conv_w.shape[0]
    ch = _conv_chunk(t)

    def body(b_ref, c_ref, h_ref, w_ref, r_ref, pad_ref):
        pad_ref[0:CONV_PAD, :] = jnp.zeros((CONV_PAD, LANES), F32)
        pad_ref[CONV_PAD:CONV_PAD + t, :] = c_ref[...] * h_ref[...]
        for t0 in range(0, t, ch):
            q = jnp.zeros((ch, LANES), F32)
            for k in range(kw):
                s = kw - 1 - k
                q = q + w_ref[k:k + 1, :] * pad_ref[CONV_PAD - s + t0:CONV_PAD - s + t0 + ch, :]
            r_ref[t0:t0 + ch, :] = (b_ref[t0:t0 + ch, :] * q).astype(r_ref.dtype)

    blk = pl.BlockSpec((t, LANES), lambda j: (0, j))
    return pl.pallas_call(
        body,
        name=name,
        grid=(d // LANES,),
        in_specs=_sc_gate_specs(t, d) + [pl.BlockSpec((kw, LANES), lambda j: (0, j))],
        out_specs=blk,
        out_shape=jax.ShapeDtypeStruct((t, d), MXU_DTYPE),
        scratch_shapes=[pltpu.VMEM((CONV_PAD + t, LANES), F32)],
        compiler_params=_params(("parallel",)),
    )(u, u, u, conv_w)


def _sc_gate_bwd(dr, u, conv_w, *, name):
    t, d = u.shape[0], u.shape[1] // 3
    per = d // LANES
    kw = conv_w.shape[0]
    ch = _conv_chunk(t)

    def body(dr_ref, b_ref, c_ref, h_ref, w_ref, du_ref, dw_ref, ppad_ref, dpad_ref, stage_ref):
        third = pl.program_id(1)

        @pl.when(third == 0)
        def _():
            ppad_ref[0:CONV_PAD, :] = jnp.zeros((CONV_PAD, LANES), F32)
            ppad_ref[CONV_PAD:CONV_PAD + t, :] = c_ref[...] * h_ref[...]
            dpad_ref[0:t, :] = dr_ref[...] * b_ref[...]
            dpad_ref[t:t + CONV_PAD, :] = jnp.zeros((CONV_PAD, LANES), F32)
            for t0 in range(0, t, ch):
                q = jnp.zeros((ch, LANES), F32)
                dp = jnp.zeros((ch, LANES), F32)
                for k in range(kw):
                    s = kw - 1 - k
                    q = q + w_ref[k:k + 1, :] * ppad_ref[CONV_PAD - s + t0:CONV_PAD - s + t0 + ch, :]
                    dp = dp + w_ref[k:k + 1, :] * dpad_ref[t0 + s:t0 + s + ch, :]
                stage_ref[0, t0:t0 + ch, :] = (dr_ref[t0:t0 + ch, :] * q).astype(stage_ref.dtype)
                stage_ref[1, t0:t0 + ch, :] = (dp * h_ref[t0:t0 + ch, :]).astype(stage_ref.dtype)
                stage_ref[2, t0:t0 + ch, :] = (dp * c_ref[t0:t0 + ch, :]).astype(stage_ref.dtype)
            for k in range(kw):
                s = kw - 1 - k
                acc = jnp.zeros((ch, LANES), F32)
                for t0 in range(0, t, ch):
                    acc = acc + dpad_ref[t0:t0 + ch, :] * ppad_ref[CONV_PAD - s + t0:CONV_PAD - s + t0 + ch, :]
                dw_ref[k:k + 1, :] = _colsum(acc)

        du_ref[...] = stage_ref[third]

    col = lambda k: pl.BlockSpec((t, LANES), lambda j, p, k=k: (0, k * per + j))
    blk = pl.BlockSpec((t, LANES), lambda j, p: (0, j))
    wblk = pl.BlockSpec((kw, LANES), lambda j, p: (0, j))
    return pl.pallas_call(
        body,
        name=name,
        grid=(per, 3),
        in_specs=[blk, col(0), col(1), col(2), wblk],
        out_specs=[pl.BlockSpec((t, LANES), lambda j, p: (0, p * per + j)), wblk],
        out_shape=[jax.ShapeDtypeStruct((t, 3 * d), MXU_DTYPE), jax.ShapeDtypeStruct((kw, d), F32)],
        scratch_shapes=[pltpu.VMEM((CONV_PAD + t, LANES), F32), pltpu.VMEM((CONV_PAD + t, LANES), F32),
                        pltpu.VMEM((3, t, LANES), MXU_DTYPE)],
        compiler_params=_params(("parallel", "arbitrary")),
    )(dr, u, u, u, conv_w)


def _sc_fwd(x_m, w_in, conv_w, w_out, tag, after=()):
    u = _mm(x_m, w_in, name=f"{tag}_in", after=after)
    r = _sc_gate_fwd(u, conv_w, name=f"{tag}_gate")
    y = _mm(r, w_out, name=f"{tag}_out")
    return y, (x_m, u, r)


def _sc_bwd(dy_m, saved, w_in, conv_w, w_out, tag, residual, after=()):
    d = D_MODEL
    x_m, u, r = saved
    dw_out = _mm(r, dy_m, ta=True, name=f"{tag}_dwout", out_dtypes=(MXU_DTYPE,))
    dr = _mm(dy_m, w_out, tb=True, name=f"{tag}_dr", after=after)
    du, dconv = _sc_gate_bwd(dr, u, conv_w, name=f"{tag}_dgate")
    dw_in = _mm(x_m, du, ta=True, out_dev=True, name=f"{tag}_dwin", out_dtypes=(MXU_DTYPE,))
    dx = _mm(du, w_in, tb=True, name=f"{tag}_dx", epilogue=_plus_residual, tile_extras=(residual,))
    return [dx], {"sc_w_in": dw_in, "sc_conv_w": dconv, "sc_w_out": dw_out}


def _cf_glu_specs(t, d):
    per = d // LANES
    return [pl.BlockSpec((t, LANES), lambda j, k=k: (0, k * per + j)) for k in range(2)]


def _cf_glu_conv_fwd(u, w, bias, *, name):
    t, d = u.shape[0], u.shape[1] // 2
    kw = w.shape[0]
    ch = _conv_chunk(t)

    def body(a_ref, g_ref, w_ref, b_ref, o_ref, pad_ref):
        pad_ref[0:CONV_PAD, :] = jnp.zeros((CONV_PAD, LANES), F32)
        pad_ref[CONV_PAD:CONV_PAD + t, :] = a_ref[...] * _sigmoid(g_ref[...])
        for t0 in range(0, t, ch):
            acc = jnp.broadcast_to(b_ref[...], (ch, LANES))
            for k in range(kw):
                s = kw - 1 - k
                acc = acc + w_ref[k:k + 1, :] * pad_ref[CONV_PAD - s + t0:CONV_PAD - s + t0 + ch, :]
            o_ref[t0:t0 + ch, :] = acc

    return pl.pallas_call(
        body,
        name=name,
        grid=(d // LANES,),
        in_specs=_cf_glu_specs(t, d) + [pl.BlockSpec((kw, LANES), lambda j: (0, j)),
                                        pl.BlockSpec((1, LANES), lambda j: (0, j))],
        out_specs=pl.BlockSpec((t, LANES), lambda j: (0, j)),
        out_shape=jax.ShapeDtypeStruct((t, d), F32),
        scratch_shapes=[pltpu.VMEM((CONV_PAD + t, LANES), F32)],
        compiler_params=_params(("parallel",)),
    )(u, u, w, bias)


def _cf_glu_conv_bwd(dy, u, w, *, name):
    t, d = u.shape[0], u.shape[1] // 2
    per = d // LANES
    kw = w.shape[0]
    ch = _conv_chunk(t)

    def body(dy_ref, a_ref, g_ref, w_ref, du_ref, sa_ref, sg_ref, dw_ref, xpad_ref, dpad_ref, stage_ref):
        half = pl.program_id(1)

        @pl.when(half == 0)
        def _():
            xpad_ref[0:CONV_PAD, :] = jnp.zeros((CONV_PAD, LANES), F32)
            xpad_ref[CONV_PAD:CONV_PAD + t, :] = a_ref[...] * _sigmoid(g_ref[...])
            dpad_ref[0:t, :] = dy_ref[...]
            dpad_ref[t:t + CONV_PAD, :] = jnp.zeros((CONV_PAD, LANES), F32)
            sum_a = jnp.zeros((1, LANES), F32)
            sum_g = jnp.zeros((1, LANES), F32)
            for t0 in range(0, t, ch):
                dh1 = jnp.zeros((ch, LANES), F32)
                for k in range(kw):
                    s = kw - 1 - k
                    dh1 = dh1 + w_ref[k:k + 1, :] * dpad_ref[t0 + s:t0 + s + ch, :]
                sg = _sigmoid(g_ref[t0:t0 + ch, :])
                da = dh1 * sg
                dgate = da * a_ref[t0:t0 + ch, :] * (1.0 - sg)
                stage_ref[0, t0:t0 + ch, :] = da.astype(stage_ref.dtype)
                stage_ref[1, t0:t0 + ch, :] = dgate.astype(stage_ref.dtype)
                sum_a = sum_a + _colsum(da)
                sum_g = sum_g + _colsum(dgate)
            sa_ref[...] = sum_a
            sg_ref[...] = sum_g
            for k in range(kw):
                s = kw - 1 - k
                acc = jnp.zeros((ch, LANES), F32)
                for t0 in range(0, t, ch):
                    acc = acc + dy_ref[t0:t0 + ch, :] * xpad_ref[CONV_PAD - s + t0:CONV_PAD - s + t0 + ch, :]
                dw_ref[k:k + 1, :] = _colsum(acc)

        du_ref[...] = stage_ref[half]

    col = lambda k: pl.BlockSpec((t, LANES), lambda j, p, k=k: (0, k * per + j))
    blk = pl.BlockSpec((t, LANES), lambda j, p: (0, j))
    row = pl.BlockSpec((1, LANES), lambda j, p: (0, j))
    wblk = pl.BlockSpec((kw, LANES), lambda j, p: (0, j))
    return pl.pallas_call(
        body,
        name=name,
        grid=(per, 2),
        in_specs=[blk, col(0), col(1), wblk],
        out_specs=[pl.BlockSpec((t, LANES), lambda j, p: (0, p * per + j)), row, row, wblk],
        out_shape=[jax.ShapeDtypeStruct((t, 2 * d), MXU_DTYPE)] + [jax.ShapeDtypeStruct((1, d), F32)] * 2
        + [jax.ShapeDtypeStruct((kw, d), F32)],
        scratch_shapes=[pltpu.VMEM((CONV_PAD + t, LANES), F32), pltpu.VMEM((CONV_PAD + t, LANES), F32),
                        pltpu.VMEM((2, t, LANES), MXU_DTYPE)],
        compiler_params=_params(("parallel", "arbitrary")),
    )(dy, u, u, w)


def _cf_fwd(x_m, w, tag, after=()):
    d = D_MODEL

    def bias_epi(acc, bias):
        return (acc + bias,)

    u = _mm(x_m, w["cf_w_pw1"], name=f"{tag}_pw1", epilogue=bias_epi, col_extras=(w["cf_b_pw1"],),
            after=after)
    h2 = _cf_glu_conv_fwd(u, w["cf_dw_w"], w["cf_dw_b"], name=f"{tag}_conv")

    def norm_fn(rows, bc):
        hv = rows[0]
        mu = jnp.mean(hv, axis=-1, keepdims=True)
        hc = hv - mu
        rstd = lax.rsqrt(jnp.mean(hc * hc, axis=-1, keepdims=True) + LN_EPS)
        h3 = hc * rstd * bc[0] + bc[1]
        return (h3 * _sigmoid(h3),), ()

    h4 = _rowwise(norm_fn, [h2], [w["cf_norm_g"], w["cf_norm_b"]], [(d, MXU_DTYPE)], [], name=f"{tag}_norm")[0]
    y = _mm(h4, w["cf_w_pw2"], name=f"{tag}_pw2", epilogue=bias_epi, col_extras=(w["cf_b_pw2"],))
    return y, (x_m, u, h2, h4)


def _cf_bwd(dy_m, dy_colsum, saved, w, tag, residual, after=()):
    d = D_MODEL
    x_m, u, h2, h4 = saved
    dw_pw2 = _mm(h4, dy_m, ta=True, name=f"{tag}_dwpw2", out_dtypes=(MXU_DTYPE,))
    dh4 = _mm(dy_m, w["cf_w_pw2"], tb=True, name=f"{tag}_dh4", after=after)

    def dnorm_fn(rows, bc):
        dh4v, hv = rows
        g, b = bc
        mu = jnp.mean(hv, axis=-1, keepdims=True)
        hc = hv - mu
        rstd = lax.rsqrt(jnp.mean(hc * hc, axis=-1, keepdims=True) + LN_EPS)
        xh = hc * rstd
        h3 = xh * g + b
        sg = _sigmoid(h3)
        dh3 = dh4v * (sg * (1.0 + h3 * (1.0 - sg)))
        dxh = dh3 * g
        dh2 = rstd * (dxh - jnp.mean(dxh, axis=-1, keepdims=True) - xh * jnp.mean(dxh * xh, axis=-1, keepdims=True))
        return (dh2,), (_colsum(dh3 * xh), _colsum(dh3), _colsum(dh2))

    dh2, dnorm_g, dnorm_b, ddw_b = _rowwise(dnorm_fn, [dh4, h2], [w["cf_norm_g"], w["cf_norm_b"]], [(d, F32)],
                                            [(1, d), (1, d), (1, d)], name=f"{tag}_dnorm")
    du, sum_a, sum_g, ddw_w = _cf_glu_conv_bwd(dh2, u, w["cf_dw_w"], name=f"{tag}_dconv")
    db_pw1 = jnp.concatenate([sum_a, sum_g], axis=1)
    dw_pw1 = _mm(x_m, du, ta=True, out_dev=True, name=f"{tag}_dwpw1", out_dtypes=(MXU_DTYPE,))
    dx = _mm(du, w["cf_w_pw1"], tb=True, name=f"{tag}_dx", epilogue=_plus_residual, tile_extras=(residual,))
    grads = {"cf_w_pw1": dw_pw1, "cf_b_pw1": db_pw1, "cf_dw_w": ddw_w, "cf_dw_b": ddw_b, "cf_norm_g": dnorm_g,
             "cf_norm_b": dnorm_b, "cf_w_pw2": dw_pw2, "cf_b_pw2": dy_colsum}
    return [dx], grads


def _rope_tables(t):
    pos = jnp.arange(t, dtype=F32)
    inv_freq = ROPE_THETA ** (-jnp.arange(0, QK_ROPE, 2, dtype=F32) / QK_ROPE)
    ang = pos[:, None] * inv_freq[None, :]
    return jnp.cos(ang), jnp.sin(ang)


ROPE_HALF = QK_ROPE // 2
Q_NOPE_COLS = MLA_HEADS * QK_NOPE


def _interleave_heads(w_uk, w_uv):
    c = w_uk.shape[0]
    return jnp.concatenate([w_uk.reshape(c, MLA_HEADS, QK_NOPE), w_uv.reshape(c, MLA_HEADS, V_HEAD)],
                           axis=2).reshape(c, -1)


def _mla_fwd(x_m, w, tag, after=()):
    t = x_m.shape[0]
    cos, sin = _rope_tables(t)
    cos_h = jnp.tile(jnp.concatenate([cos, cos], axis=1), (1, MLA_HEADS))
    sin_h = jnp.tile(jnp.concatenate([-sin, sin], axis=1), (1, MLA_HEADS))
    cq_pre = _mm(x_m, w["mla_w_dq"], name=f"{tag}_dq", after=after)
    cq = _rms_fwd(cq_pre, w["mla_g_q"], name=f"{tag}_qnorm")
    q = _mm(cq, w["mla_w_uq"], name=f"{tag}_uq")
    qp = _rope_in_place(q[:, Q_NOPE_COLS:], cos_h, sin_h, name=f"{tag}_qrope")
    qp_h = qp.reshape(t, MLA_HEADS, QK_ROPE).transpose(1, 0, 2).astype(MXU_DTYPE)
    ckv_full = _mm(x_m, w["mla_w_dkv"], name=f"{tag}_dkv")
    ckv_pre = ckv_full[:, :KV_LORA]
    ckv = _rms_fwd(ckv_pre, w["mla_g_kv"], name=f"{tag}_kvnorm")
    k1, k2 = _rope(ckv_full[:, KV_LORA:KV_LORA + ROPE_HALF], ckv_full[:, KV_LORA + ROPE_HALF:], cos, sin,
                   name=f"{tag}_krope")
    kp = jnp.concatenate([k1, k2], axis=-1).astype(MXU_DTYPE)
    kv = _mm(ckv, _interleave_heads(w["mla_w_uk"], w["mla_w_uv"]), name=f"{tag}_ukv", out_dtypes=(MXU_DTYPE,))
    o, o_f, lse = _attn_fwd(q, qp_h, kv, kp, name=f"{tag}_attn")
    y = _mm(o, w["mla_w_o"], name=f"{tag}_o")
    return y, (x_m, cq_pre, cq, ckv_pre, ckv, q, qp_h, kv, kp, lse, o, o_f, (cos, sin, cos_h, sin_h))


def _mla_bwd(dy_m, saved, w, tag, residual, after=()):
    x_m, cq_pre, cq, ckv_pre, ckv, q, qp_h, kv, kp, lse, o, o_f, (cos, sin, cos_h, sin_h) = saved
    t = x_m.shape[0]
    half = ROPE_HALF
    dw_o = _mm(o, dy_m, ta=True, name=f"{tag}_dwo", out_dtypes=(MXU_DTYPE,))
    do = _mm(dy_m, w["mla_w_o"], tb=True, name=f"{tag}_do", out_dtypes=(MXU_DTYPE,), after=after)
    dqn, dqp_h, dkv, dkp = _attn_bwd(q, qp_h, kv, kp, do, o_f, lse, name=f"{tag}_dattn")
    dqp = _rope_in_place(dqp_h.transpose(1, 0, 2).reshape(t, -1), cos_h, sin_h, name=f"{tag}_dqrope", inverse=True)
    dq = jnp.concatenate([dqn, dqp], axis=-1).astype(MXU_DTYPE)
    dw_uq = _mm(cq, dq, ta=True, name=f"{tag}_dwuq", out_dtypes=(MXU_DTYPE,))
    dcq = _mm(dq, w["mla_w_uq"], tb=True, name=f"{tag}_dcq")
    dcq_pre, dg_q = _rms_bwd(cq_pre, w["mla_g_q"], dcq, name=f"{tag}_dqnorm")
    dcq_pre_m = dcq_pre.astype(MXU_DTYPE)
    dw_dq = _mm(x_m, dcq_pre_m, ta=True, name=f"{tag}_dwdq", out_dtypes=(MXU_DTYPE,))
    dx_q = _mm(dcq_pre_m, w["mla_w_dq"], tb=True, name=f"{tag}_dxq", epilogue=_plus_residual, tile_extras=(residual,))
    dw_kv = _mm(ckv, dkv, ta=True, name=f"{tag}_dwukv", out_dtypes=(MXU_DTYPE,))
    dw_kv = dw_kv.reshape(KV_LORA, MLA_HEADS, QK_NOPE + V_HEAD)
    dw_uk, dw_uv = dw_kv[:, :, :QK_NOPE].reshape(KV_LORA, -1), dw_kv[:, :, QK_NOPE:].reshape(KV_LORA, -1)
    dckv = _mm(dkv, _interleave_heads(w["mla_w_uk"], w["mla_w_uv"]), tb=True, name=f"{tag}_dckv")
    dckv_pre, dg_kv = _rms_bwd(ckv_pre, w["mla_g_kv"], dckv, name=f"{tag}_dkvnorm")
    dk1, dk2 = _rope(dkp[:, :half], dkp[:, half:], cos, sin, name=f"{tag}_dkrope", inverse=True)
    dckv_full = jnp.concatenate([dckv_pre, dk1, dk2], axis=-1).astype(MXU_DTYPE)
    dw_dkv = _mm(x_m, dckv_full, ta=True, name=f"{tag}_dwdkv", out_dtypes=(MXU_DTYPE,))
    dx_kv = _mm(dckv_full, w["mla_w_dkv"], tb=True, name=f"{tag}_dxkv", epilogue=lambda acc, other: (acc + other,),
                tile_extras=(dx_q,))
    grads = {"mla_w_dq": dw_dq, "mla_g_q": dg_q, "mla_w_uq": dw_uq, "mla_w_dkv": dw_dkv, "mla_g_kv": dg_kv,
             "mla_w_uk": dw_uk, "mla_w_uv": dw_uv, "mla_w_o": dw_o}
    return [dx_kv], grads


def _cast_input(x):
    cast = lambda rows, bc: ((rows[0],), ())
    return _rowwise(cast, [x], [], [(D_MODEL, MXU_DTYPE)], [], name="x_cast")[0]


def _hooked(value, hook):
    token = None if hook is None else hook(value)
    return () if token is None else (token,)


def _mix_fwd(i, cur, cur_m, wl, hook=None, after=()):
    mixer = i % N_MIXERS
    if mixer == 0:
        y, s_mix = _sc_fwd(cur_m, wl["sc_w_in"], wl["sc_conv_w"], wl["sc_w_out"], f"l{i}_sc", after=after)
    elif mixer == 1:
        y, s_mix = _mla_fwd(cur_m, wl, f"l{i}_mla", after=after)
    else:
        y, s_mix = _cf_fwd(cur_m, wl, f"l{i}_cf", after=after)
    cur, cur_m, xh_mix, rs_mix = _ln_fwd(cur, y, wl["ln_mix_g"], wl["ln_mix_b"], name=f"l{i}_lnmix",
                                         after=_hooked(y, hook))
    return cur, cur_m, (s_mix, xh_mix, rs_mix)


def _ff_fwd(i, cur, cur_m, wl, hook=None, after=(), loss_target=None, act=None):
    y, s_ff = _mlp_fwd(cur_m, wl["ff_w1"], wl["ff_w2"], f"l{i}_ff", after=after, act=act)
    if loss_target is not None:
        xh_ff, rs_ff, dy, sq = _ln_fwd_loss(cur, y, wl["ln_ff_g"], wl["ln_ff_b"], loss_target, name=f"l{i}_lnff_loss",
                                            after=_hooked(y, hook))
        return (0.5 / D_MODEL) * jnp.sum(sq), dy, (s_ff, xh_ff, rs_ff)
    cur, cur_m, xh_ff, rs_ff = _ln_fwd(cur, y, wl["ln_ff_g"], wl["ln_ff_b"], name=f"l{i}_lnff", after=_hooked(y, hook))
    return cur, cur_m, (s_ff, xh_ff, rs_ff)


def _layer_fwd(i, cur, cur_m, wl):
    cur, cur_m, s_mix = _mix_fwd(i, cur, cur_m, wl)
    cur, cur_m, s_ff = _ff_fwd(i, cur, cur_m, wl)
    return cur, cur_m, (*s_mix, *s_ff)


def _loss_and_grad(out, target):
    d = D_MODEL

    def loss_fn(rows, bc):
        err = rows[0] - rows[1]
        return (err * (1.0 / d),), (_colsum(err * err),)

    dy, sq = _rowwise(loss_fn, [out, target], [], [(d, F32)], [(1, d)], name="loss")
    return (0.5 / d) * jnp.sum(sq), dy


def _ff_bwd(i, terms, saved, wl, hook=None, after=()):
    s_ff, xh_ff, rs_ff = saved
    grads = {}
    dz, dz_m, grads["ln_ff_g"], grads["ln_ff_b"], _ = _ln_bwd(terms, xh_ff, rs_ff, wl["ln_ff_g"], name=f"l{i}_dlnff",
                                                              after=after)
    dxs, g_ff = _mlp_bwd(dz_m, s_ff, wl["ff_w1"], wl["ff_w2"], f"l{i}_ff", dz, after=_hooked(dz, hook))
    grads.update(g_ff)
    return [(1.0, v) for v in dxs], grads


def _mix_bwd(i, terms, saved, wl, hook=None, after=()):
    mixer = i % N_MIXERS
    s_mix, xh_mix, rs_mix = saved
    grads = {}
    dz, dz_m, grads["ln_mix_g"], grads["ln_mix_b"], dz_sum = _ln_bwd(terms, xh_mix, rs_mix, wl["ln_mix_g"],
                                                                     name=f"l{i}_dlnmix", after=after)
    behind = _hooked(dz, hook)
    if mixer == 0:
        dxs, g_mix = _sc_bwd(dz_m, s_mix, wl["sc_w_in"], wl["sc_conv_w"], wl["sc_w_out"], f"l{i}_sc", dz, after=behind)
    elif mixer == 1:
        dxs, g_mix = _mla_bwd(dz_m, s_mix, wl, f"l{i}_mla", dz, after=behind)
    else:
        dxs, g_mix = _cf_bwd(dz_m, dz_sum, s_mix, wl, f"l{i}_cf", dz, after=behind)
    grads.update(g_mix)
    return [(1.0, v) for v in dxs], grads


def _layer_bwd(i, terms, saved, wl):
    terms, g_ff = _ff_bwd(i, terms, saved[3:], wl)
    terms, g_mix = _mix_bwd(i, terms, saved[:3], wl)
    return terms, {**g_ff, **g_mix}


def _sum_terms(terms, *, name):
    coefs = [c for c, _ in terms]

    def sum_fn(rows, bc):
        acc = None
        for c, v in zip(coefs, rows):
            tv = v if c == 1.0 else c * v
            acc = tv if acc is None else acc + tv
        return (acc,), ()

    return _rowwise(sum_fn, [v for _, v in terms], [], [(D_MODEL, F32)], [], name=name)[0]


def _local_step(x, target, layers):
    cur, cur_m = x, _cast_input(x)
    saved = []
    for i in range(DEPTH):
        cur, cur_m, s = _layer_fwd(i, cur, cur_m, layers[i])
        saved.append(s)
    loss, dy = _loss_and_grad(cur, target)
    terms = [(1.0, dy)]
    grads = [None] * DEPTH
    for i in reversed(range(DEPTH)):
        terms, grads[i] = _layer_bwd(i, terms, saved[i], layers[i])
    return loss, _sum_terms(terms, name="grad_x"), grads


MESH_ID = pl.DeviceIdType.MESH
ANY = pl.BlockSpec(memory_space=pl.ANY)


def _block_id(px, py, pc):
    return 4 * px + 2 * py + pc


COPIES_PER_GATHER = 7


def _all_gather(shards, *, name):
    n = len(shards)

    def body(*refs):
        x_refs, out_refs = refs[:n], refs[n:2 * n]
        send_sems, recv_sems, local_sems = refs[2 * n:]
        x, y, core = lax.axis_index("x"), lax.axis_index("y"), lax.axis_index("c")
        me, sibling = (x, y, core), (x, y, 1 - core)
        chips = [(1 - x, y), (x, 1 - y), (1 - x, 1 - y)]

        def copy(t, k, block, to, from_input=False):
            dst = out_refs[t].at[_block_id(*block)]
            return pltpu.make_async_remote_copy(
                src_ref=x_refs[t] if from_input else dst, dst_ref=dst,
                send_sem=send_sems.at[t * COPIES_PER_GATHER + k], recv_sem=recv_sems.at[t * COPIES_PER_GATHER + k],
                device_id=to, device_id_type=MESH_ID)

        started = []
        mine = [pltpu.make_async_copy(x_refs[t], out_refs[t].at[_block_id(*me)], local_sems.at[t]) for t in range(n)]
        for t in range(n):
            mine[t].start()
            first = [copy(t, 0, me, sibling, from_input=True)]
            first += [copy(t, 1 + j, me, (*chip, core), from_input=True) for j, chip in enumerate(chips)]
            for cp in first:
                cp.start()
            started += first
        for t in range(n):
            for j, chip in enumerate(chips):
                copy(t, 1 + j, (*chip, core), me).wait_recv()
                passed = copy(t, 4 + j, (*chip, core), sibling)
                passed.start()
                started.append(passed)
        for t in range(n):
            copy(t, 0, sibling, me).wait_recv()
            for j, chip in enumerate(chips):
                copy(t, 4 + j, (*chip, 1 - core), me).wait_recv()
        for cp in started:
            cp.wait_send()
        for cp in mine:
            cp.wait()

    n_sems = n * COPIES_PER_GATHER
    outs = pl.pallas_call(
        body,
        name=name,
        out_shape=[jax.ShapeDtypeStruct((N_DEV, *s.shape), s.dtype) for s in shards],
        in_specs=[ANY] * n,
        out_specs=[ANY] * n,
        scratch_shapes=[pltpu.SemaphoreType.DMA((n_sems,)), pltpu.SemaphoreType.DMA((n_sems,)),
                        pltpu.SemaphoreType.DMA((n,))],
    )(*shards)
    return list(outs)


def _swap_with_sibling(parts, *, name):
    n = len(parts)

    def body(*refs):
        p_refs, got_refs = refs[:n], refs[n:2 * n]
        send_sems, recv_sems = refs[2 * n:]
        x, y, core = lax.axis_index("x"), lax.axis_index("y"), lax.axis_index("c")
        chips = [(x, y), (1 - x, y), (x, 1 - y), (1 - x, 1 - y)]
        copies = []
        for t in range(n):
            for k, (px, py) in enumerate(chips):
                copies.append(pltpu.make_async_remote_copy(
                    src_ref=p_refs[t].at[_block_id(px, py, 1 - core)], dst_ref=got_refs[t].at[k],
                    send_sem=send_sems.at[4 * t + k], recv_sem=recv_sems.at[4 * t + k], device_id=(x, y, 1 - core),
                    device_id_type=MESH_ID))
        for cp in copies:
            cp.start()
        for cp in copies:
            cp.wait()

    outs = pl.pallas_call(
        body,
        name=name,
        out_shape=[jax.ShapeDtypeStruct((4, *p.shape[1:]), p.dtype) for p in parts],
        in_specs=[ANY] * n,
        out_specs=[ANY] * n,
        scratch_shapes=[pltpu.SemaphoreType.DMA((4 * n,)), pltpu.SemaphoreType.DMA((4 * n,))],
    )(*parts)
    return list(outs)


def _swap_between_chips(parts, *, name):
    n = len(parts)

    def body(*refs):
        p_refs, got_refs = refs[:n], refs[n:2 * n]
        send_sems, recv_sems = refs[2 * n:]
        x, y, core = lax.axis_index("x"), lax.axis_index("y"), lax.axis_index("c")
        chips = [(1 - x, y), (x, 1 - y), (1 - x, 1 - y)]
        copies = []
        for t in range(n):
            for j, (px, py) in enumerate(chips):
                copies.append(pltpu.make_async_remote_copy(
                    src_ref=p_refs[t].at[j], dst_ref=got_refs[t].at[j], send_sem=send_sems.at[3 * t + j],
                    recv_sem=recv_sems.at[3 * t + j], device_id=(px, py, core), device_id_type=MESH_ID))
        for cp in copies:
            cp.start()
        for cp in copies:
            cp.wait()

    outs = pl.pallas_call(
        body,
        name=name,
        out_shape=[jax.ShapeDtypeStruct(p.shape, p.dtype) for p in parts],
        in_specs=[ANY] * n,
        out_specs=[ANY] * n,
        scratch_shapes=[pltpu.SemaphoreType.DMA((3 * n,)), pltpu.SemaphoreType.DMA((3 * n,))],
    )(*parts)
    return list(outs)


def _chip_partials_group(parts, got, *, name):
    n = len(parts)
    rows = [p.shape[1] for p in parts]
    steps = 4 if all(r % (4 * BF16_ROWS) == 0 for r in rows) else (2 if all(r % (2 * BF16_ROWS) == 0 for r in rows) else 1)
    x, y, core = lax.axis_index("x"), lax.axis_index("y"), lax.axis_index("c")
    chips = [(x, y), (1 - x, y), (x, 1 - y), (1 - x, 1 - y)]
    ids = jnp.stack([_block_id(px, py, core) for px, py in chips]).astype(jnp.int32)

    def body(ids_ref, *refs):
        ins, outs = refs[:5 * n], refs[5 * n:]
        for t in range(n):
            p0, p1, p2, p3, g_ref = ins[5 * t:5 * t + 5]
            own_ref, out_ref = outs[2 * t], outs[2 * t + 1]
            own_ref[...] = p0[...].astype(F32) + g_ref[0].astype(F32)
            for j, p in enumerate((p1, p2, p3)):
                out_ref[j] = (p[...].astype(F32) + g_ref[j + 1].astype(F32)).astype(out_ref.dtype)

    in_specs, out_specs, out_shape, operands = [], [], [], []
    for p, g in zip(parts, got):
        _, r, c = p.shape
        tb = r // steps
        in_specs += [pl.BlockSpec((None, tb, c), lambda i, ids_ref, k=k: (ids_ref[k], i, 0)) for k in range(4)]
        in_specs.append(pl.BlockSpec((4, tb, c), lambda i, ids_ref: (0, i, 0)))
        out_specs += [pl.BlockSpec((tb, c), lambda i, ids_ref: (i, 0)), pl.BlockSpec((3, tb, c), lambda i, ids_ref: (0, i, 0))]
        out_shape += [jax.ShapeDtypeStruct((r, c), F32), jax.ShapeDtypeStruct((3, r, c), p.dtype)]
        operands += [p, p, p, p, g]
    outs = pl.pallas_call(
        body,
        name=name,
        grid_spec=pltpu.PrefetchScalarGridSpec(num_scalar_prefetch=1, grid=(steps,), in_specs=in_specs,
                                               out_specs=out_specs),
        out_shape=out_shape,
        compiler_params=_params(("parallel",)),
    )(ids, *operands)
    return [(outs[2 * t], outs[2 * t + 1]) for t in range(n)]


def _sum_blocks(first, blocks, *, name):
    n, r, c = blocks.shape
    tb = _row_tile(r, 512)

    def body(*refs):
        b_ref, o_ref = refs[-2], refs[-1]
        acc = refs[0][...] if first is not None else b_ref[0].astype(F32)
        for j in range(0 if first is not None else 1, n):
            acc = acc + b_ref[j].astype(F32)
        o_ref[...] = acc

    row_spec = pl.BlockSpec((tb, c), lambda i: (i, 0))
    return pl.pallas_call(
        body,
        name=name,
        grid=(r // tb,),
        in_specs=([row_spec] if first is not None else []) + [pl.BlockSpec((n, tb, c), lambda i: (0, i, 0))],
        out_specs=row_spec,
        out_shape=jax.ShapeDtypeStruct((r, c), F32),
        compiler_params=_params(("parallel",)),
    )(*([first] if first is not None else []), blocks)


HBM = pl.BlockSpec(memory_space=pltpu.HBM)
SEM = pl.BlockSpec(memory_space=pltpu.SEMAPHORE)
DATAFLOW = pltpu.SideEffectType.DATAFLOW_SIDE_EFFECTING


def _split_call(name, arrays, sems, n_new, body, after=None):
    n_a, n_s = len(arrays), len(sems)
    after = [] if after is None else (list(after) if isinstance(after, (list, tuple)) else [after])
    n_after = len(after)

    def kernel_body(*refs):
        new = refs[n_a + n_s + n_after:n_a + n_s + n_after + 2] if n_new else (None, None)
        body(refs[:n_a], refs[n_a:n_a + n_s], *new)
        refs[-1][...] = jnp.zeros_like(refs[-1])

    n_sem_out = 2 if n_new else 0
    out_shape = [pltpu.SemaphoreType.DMA((n_new,))] * n_sem_out
    out_shape += [pltpu.HBM(a.shape, a.dtype) for a in arrays]
    out_shape += [jax.ShapeDtypeStruct((SUBLANES, LANES), F32)]
    outs = pl.pallas_call(
        kernel_body,
        name=name,
        out_shape=out_shape,
        in_specs=[HBM] * n_a + [SEM] * n_s + [ANY] * n_after,
        out_specs=[SEM] * n_sem_out + [HBM] * n_a + [pl.BlockSpec(memory_space=pltpu.VMEM)],
        input_output_aliases={i: n_sem_out + i for i in range(n_a)},
        compiler_params=pltpu.CompilerParams(has_side_effects=DATAFLOW),
    )(*[pltpu.with_memory_space_constraint(a, pltpu.HBM) for a in arrays], *sems, *after)
    return tuple(outs[:n_sem_out]), list(outs[n_sem_out:n_sem_out + n_a]), outs[-1]


def _remote(src, dst, send, recv, k, to):
    return pltpu.make_async_remote_copy(src_ref=src, dst_ref=dst, send_sem=send.at[k], recv_sem=recv.at[k],
                                        device_id=to, device_id_type=MESH_ID)


def _land_block(land_ref, block, by_cols, width):
    if by_cols:
        return land_ref.at[:, pl.ds(pl.multiple_of(block * width, LANES), width)]
    return land_ref.at[block]


def _place_in_window(stacked, layer, *, name):
    _, r, c = stacked.shape
    tb = _row_tile(r, 512)
    me = _block_id(lax.axis_index("x"), lax.axis_index("y"), lax.axis_index("c")).astype(jnp.int32).reshape(1)

    def body(me_ref, s_ref, o_ref, token_ref):
        o_ref[...] = s_ref[...].astype(o_ref.dtype)
        token_ref[...] = jnp.zeros_like(token_ref)

    return pl.pallas_call(
        body,
        name=name,
        grid_spec=pltpu.PrefetchScalarGridSpec(
            num_scalar_prefetch=1, grid=(r // tb,),
            in_specs=[pl.BlockSpec((None, tb, c), lambda i, me_ref: (layer, i, 0))],
            out_specs=[pl.BlockSpec((tb, c), lambda i, me_ref: (i, me_ref[0])),
                       pl.BlockSpec((SUBLANES, LANES), lambda i, me_ref: (0, 0))]),
        out_shape=[jax.ShapeDtypeStruct((r, N_DEV * c), MXU_DTYPE), jax.ShapeDtypeStruct((SUBLANES, LANES), F32)],
        compiler_params=_params(("arbitrary",)),
    )(me, stacked)


def _gather_prepare(shards, by_cols, *, name):
    me = _block_id(lax.axis_index("x"), lax.axis_index("y"), lax.axis_index("c"))
    lands, early, widths = [], [], []
    for i, (s, cols) in enumerate(zip(shards, by_cols)):
        if cols:
            land, token = _place_in_window(*s, name=f"{name}_place{i}")
            lands.append(land)
            early.append(token)
            widths.append(s[0].shape[-1])
        else:
            lands.append(lax.dynamic_update_index_in_dim(lax.empty((N_DEV, *s.shape), s.dtype), s, me, 0))
            early.append(s)
            widths.append(s.shape[-1])
    return lands, widths, list(by_cols), early


def _gather_start(shards, *, name, after=None, by_cols=None, prepared=None):
    if prepared is None:
        prepared = _gather_prepare(shards, by_cols if by_cols is not None else [False] * len(shards), name=name)
    lands, widths, by_cols, _ = prepared
    n = len(lands)

    def body(refs, _, send, recv):
        x, y, core = lax.axis_index("x"), lax.axis_index("y"), lax.axis_index("c")
        targets = [(x, y, 1 - core), (1 - x, y, core), (x, 1 - y, core), (1 - x, 1 - y, core)]
        for t in range(n):
            mine = _land_block(refs[t], _block_id(x, y, core), by_cols[t], widths[t])
            for k, to in enumerate(targets):
                _remote(mine, mine, send, recv, 4 * t + k, to).start()

    sems, thru, token = _split_call(name, lands, [], 4 * n, body, after=after)
    return {"n": n, "sems1": sems, "arrays": thru, "token": token, "by_cols": by_cols, "widths": widths}


def _gather_pass_on(st, *, name, after):
    n = st["n"]

    def body(refs, sems, send, recv):
        send1, recv1 = sems
        x, y, core = lax.axis_index("x"), lax.axis_index("y"), lax.axis_index("c")
        chips = [(1 - x, y), (x, 1 - y), (1 - x, 1 - y)]
        for t in range(n):
            for j, (px, py) in enumerate(chips):
                block = _land_block(refs[t], _block_id(px, py, core), st["by_cols"][t], st["widths"][t])
                _remote(block, block, send1, recv1, 4 * t + 1 + j, (x, y, 1 - core)).wait_recv()
                _remote(block, block, send, recv, 3 * t + j, (x, y, 1 - core)).start()

    sems2, thru, token = _split_call(name, st["arrays"], list(st["sems1"]), 3 * n, body, after=after)
    return {**st, "sems2": sems2, "arrays": thru, "token": token}


def _gather_finish(st, *, name, after, then_start=None):
    n = st["n"]
    n2 = 0 if then_start is None else len(then_start[0])

    def body(refs, sems, new_send=None, new_recv=None):
        send1, recv1, send2, recv2 = sems
        x, y, core = lax.axis_index("x"), lax.axis_index("y"), lax.axis_index("c")
        sibling = (x, y, 1 - core)
        chips = [(1 - x, y), (x, 1 - y), (1 - x, 1 - y)]
        for t in range(n):
            place = lambda px, py, pc, t=t: _land_block(refs[t], _block_id(px, py, pc), st["by_cols"][t],
                                                        st["widths"][t])
            mine = place(x, y, core)
            for k in range(4):
                _remote(mine, mine, send1, recv1, 4 * t + k, sibling).wait_send()
            _remote(mine, place(x, y, 1 - core), send1, recv1, 4 * t, sibling).wait_recv()
            for j, (px, py) in enumerate(chips):
                _remote(place(px, py, core), place(px, py, core), send2, recv2, 3 * t + j, sibling).wait_send()
                _remote(mine, place(px, py, 1 - core), send2, recv2, 3 * t + j, sibling).wait_recv()
        if n2:
            _, widths, by_cols, _ = then_start
            targets = [sibling, (1 - x, y, core), (x, 1 - y, core), (1 - x, 1 - y, core)]
            for t in range(n2):
                mine = _land_block(refs[n + t], _block_id(x, y, core), by_cols[t], widths[t])
                for k, to in enumerate(targets):
                    _remote(mine, mine, new_send, new_recv, 4 * t + k, to).start()

    arrays = st["arrays"] + (list(then_start[0]) if n2 else [])
    sems, thru, token = _split_call(name, arrays, [*st["sems1"], *st["sems2"]], 4 * n2, body, after=after)
    if not n2:
        return thru
    _, widths, by_cols, _ = then_start
    return thru[:n], {"n": n2, "sems1": sems, "arrays": thru[n:], "token": token, "by_cols": by_cols, "widths": widths}


def _swap_start(sources, lands, plan, per_array, *, name):
    n = len(sources)

    def body(refs, _, send, recv):
        x, y, core = lax.axis_index("x"), lax.axis_index("y"), lax.axis_index("c")
        for t in range(n):
            for k in range(per_array):
                block, to = plan(x, y, core, k)
                _remote(refs[t].at[block], refs[n + t].at[k], send, recv, per_array * t + k, to).start()

    sems, thru, token = _split_call(name, sources + lands, [], per_array * n, body)
    return {"n": n, "sems": sems, "arrays": thru, "token": token, "plan": plan, "per_array": per_array}


def _swap_finish(states, *, name, after):
    many = isinstance(states, (list, tuple))
    states = list(states) if many else [states]

    def body(refs, sems, *_):
        x, y, core = lax.axis_index("x"), lax.axis_index("y"), lax.axis_index("c")
        first = 0
        for j, st in enumerate(states):
            n, plan, per_array = st["n"], st["plan"], st["per_array"]
            send, recv = sems[2 * j], sems[2 * j + 1]
            for t in range(n):
                for k in range(per_array):
                    block, to = plan(x, y, core, k)
                    cp = _remote(refs[first + t].at[block], refs[first + n + t].at[k], send, recv, per_array * t + k, to)
                    cp.wait_send()
                    cp.wait_recv()
            first += 2 * n

    arrays = [a for st in states for a in st["arrays"]]
    _, thru, _ = _split_call(name, arrays, [s for st in states for s in st["sems"]], 0, body, after=after)
    out, first = [], 0
    for st in states:
        n = st["n"]
        out.append((thru[first:first + n], thru[first + n:first + 2 * n]))
        first += 2 * n
    return out if many else out[0]


def _to_sibling(x, y, core, k):
    px, py = [(x, y), (1 - x, y), (x, 1 - y), (1 - x, 1 - y)][k]
    return _block_id(px, py, 1 - core), (x, y, 1 - core)


def _to_chip(x, y, core, k):
    px, py = [(1 - x, y), (x, 1 - y), (1 - x, 1 - y)][k]
    return k, (px, py, core)


PARAMS = {
    "sc_w_in": ((2, 1024, 3072), 2), "sc_conv_w": ((2, 3, 1024), 2), "sc_w_out": ((2, 1024, 1024), 1),
    "mla_w_dq": ((1, 1024, 384), 1), "mla_g_q": ((1, 384), None), "mla_w_uq": ((1, 384, 1536), 2),
    "mla_w_dkv": ((1, 1024, 320), 1), "mla_g_kv": ((1, 256), None), "mla_w_uk": ((1, 256, 8, 128), 1),
    "mla_w_uv": ((1, 256, 8, 128), 1), "mla_w_o": ((1, 1024, 1024), 1), "cf_w_pw1": ((1, 1024, 2048), 2),
    "cf_b_pw1": ((1, 2048), 1), "cf_dw_w": ((1, 31, 1024), 2), "cf_dw_b": ((1, 1024), 1), "cf_norm_g": ((1, 1024), 1),
    "cf_norm_b": ((1, 1024), 1), "cf_w_pw2": ((1, 1024, 1024), 1), "cf_b_pw2": ((1, 1024), 1),
    "ff_w1": ((4, 1024, 4096), 2), "ff_w2": ((4, 4096, 1024), 1), "ln_mix_g": ((4, 1024), None),
    "ln_mix_b": ((4, 1024), None), "ln_ff_g": ((4, 1024), None), "ln_ff_b": ((4, 1024), None),
}
NAMES = list(PARAMS)
BIG = ["sc_w_in", "sc_w_out", "mla_w_dq", "mla_w_uq", "mla_w_dkv", "mla_w_uk", "mla_w_uv", "mla_w_o", "cf_w_pw1",
       "cf_w_pw2", "ff_w1", "ff_w2"]
SMALL = [n for n in NAMES if n not in BIG]
SMALL_SHARDED = [n for n in SMALL if PARAMS[n][1] is not None]
SMALL_WIDTH = LANES
DEV_BLOCKED = ("sc_w_in", "cf_w_pw1", "ff_w1")
MIXER_BIG = {0: ["sc_w_in", "sc_w_out"], 1: ["mla_w_dq", "mla_w_uq", "mla_w_dkv", "mla_w_uk", "mla_w_uv", "mla_w_o"],
             2: ["cf_w_pw1", "cf_w_pw2"]}


def _layer_big(i):
    return [(n, i // N_MIXERS) for n in MIXER_BIG[i % N_MIXERS]] + [("ff_w1", i), ("ff_w2", i)]


def _shard_2d(name, a):
    return a.reshape(a.shape[0], -1)


def _shard_for_gather(name, stacked, layer):
    if name in DEV_BLOCKED:
        return stacked, layer
    return _shard_2d(name, stacked[layer]).astype(MXU_DTYPE)


def _as_operand(name, g):
    if name in DEV_BLOCKED:
        return g
    if name == "mla_w_uq":
        by_head = g.transpose(1, 0, 2)
        pieces = [by_head[:, :, :QK_NOPE], by_head[:, :, QK_NOPE:]]
        return jnp.concatenate([p.reshape(g.shape[1], -1) for p in pieces], axis=1)
    return g.reshape(-1, g.shape[2])


def _as_parts(name, grad):
    if name in DEV_BLOCKED:
        return grad
    if name == "mla_w_uq":
        rows = grad.shape[0]
        pieces = [grad[:, :Q_NOPE_COLS].reshape(rows, MLA_HEADS, QK_NOPE),
                  grad[:, Q_NOPE_COLS:].reshape(rows, MLA_HEADS, QK_ROPE)]
        return jnp.concatenate(pieces, axis=2).transpose(1, 0, 2)
    return grad.reshape(N_DEV, -1, grad.shape[1])


def _shard_shape(name):
    shape, ax = PARAMS[name]
    if ax is None:
        return shape
    return tuple(s // N_DEV if i == ax else s for i, s in enumerate(shape))


def _rows_of(shape, width, align):
    n = 1
    for s in shape:
        n *= s
    rows = -(-n // width)
    return -(-rows // align) * align


def _pack_rows(arrays, width, align, dtype, lead=0):
    segs = []
    for a in arrays:
        batch = a.shape[:lead]
        flat = a.astype(dtype).reshape(*batch, -1)
        rows = _rows_of(a.shape[lead:], width, align)
        flat = jnp.pad(flat, [(0, 0)] * lead + [(0, rows * width - flat.shape[-1])])
        segs.append(flat.reshape(*batch, rows, width))
    return jnp.concatenate(segs, axis=lead)


def _unpack_rows(buf, shapes, width, align):
    out, off = [], 0
    lead = buf.shape[:-2]
    for shape in shapes:
        n = 1
        for s in shape:
            n *= s
        rows = _rows_of(shape, width, align)
        seg = buf[..., off:off + rows, :].reshape(*lead, rows * width)[..., :n]
        out.append(seg.reshape(*lead, *shape))
        off += rows
    return out


def _merge_shards(stacked, ax):
    moved = jnp.moveaxis(stacked, 0, ax)
    shape = moved.shape
    return moved.reshape(*shape[:ax], shape[ax] * shape[ax + 1], *shape[ax + 2:])


def _split_shards(full, ax):
    shape = full.shape
    split = full.reshape(*shape[:ax], N_DEV, shape[ax] // N_DEV, *shape[ax + 1:])
    return jnp.moveaxis(split, ax, 0)


def _adam_math(wv, gv, mv, vv):
    m_new = ADAM_B1 * mv + (1.0 - ADAM_B1) * gv
    v_new = ADAM_B2 * vv + (1.0 - ADAM_B2) * (gv * gv)
    m_hat = m_new / (1.0 - ADAM_B1 ** ADAM_STEP)
    v_hat = v_new / (1.0 - ADAM_B2 ** ADAM_STEP)
    delta = -ADAM_LR * (m_hat / (jnp.sqrt(v_hat) + ADAM_EPS) + ADAM_WD * wv)
    return delta, m_new, v_new


def _adamw(w, g, m, v, *, name):
    shape = w.shape
    c = shape[-1]

    def fn(rows, bc):
        return _adam_math(*rows), ()

    flat = [a.reshape(-1, c) for a in (w, g, m, v)]
    outs = _rowwise(fn, flat, [], [(c, F32)] * 3, [], name=name, tb_cap=512)
    return tuple(o.reshape(shape) for o in outs)


def _adamw_reduced(w, m, v, layer_grads, *, name, after=()):
    n_l, r, c = w.shape
    tb = _row_tile(r, 256 if c > 512 else 512)
    nb = r // tb

    def body(*refs):
        w_ref, m_ref, v_ref = refs[:3]
        g_refs = refs[3:3 + 2 * n_l]
        g_out, d_out, m_out, v_out = refs[3 + 2 * n_l + len(after):]
        layer = pl.program_id(0)
        for l in range(n_l):
            @pl.when(layer == l)
            def _(l=l):
                own, arr = g_refs[2 * l], g_refs[2 * l + 1]
                g = own[...] + arr[0].astype(F32)
                g = g + arr[1].astype(F32)
                g = g + arr[2].astype(F32)
                delta, m_new, v_new = _adam_math(w_ref[...], g, m_ref[...], v_ref[...])
                g_out[...] = g
                d_out[...] = delta
                m_out[...] = m_new
                v_out[...] = v_new

    stacked = pl.BlockSpec((tb, c), lambda l, i: (l * nb + i, 0))
    in_specs = [stacked] * 3
    operands = [a.reshape(n_l * r, c) for a in (w, m, v)]
    for l, (own, arr) in enumerate(layer_grads):
        in_specs.append(pl.BlockSpec((tb, c), lambda ll, i, l=l: (jnp.where(ll == l, i, 0), 0)))
        in_specs.append(pl.BlockSpec((3, tb, c), lambda ll, i, l=l: (0, jnp.where(ll == l, i, 0), 0)))
        operands += [own, arr]
    in_specs += [ANY] * len(after)
    operands += list(after)
    outs = pl.pallas_call(
        body,
        name=name,
        grid=(n_l, nb),
        in_specs=in_specs,
        out_specs=[stacked] * 4,
        out_shape=[jax.ShapeDtypeStruct((n_l * r, c), F32)] * 4,
        compiler_params=_params(("arbitrary", "arbitrary")),
    )(*operands)
    return tuple(o.reshape(n_l, r, c) for o in outs)


def _train_step(x, target, weights, m_state, v_state):
    me = _block_id(lax.axis_index("x"), lax.axis_index("y"), lax.axis_index("c"))

    groups = [(kind, i) for i in range(DEPTH) for kind in ("mix", "ff")]
    forward = [groups[0], ("ff_up", 0), ("ff_down", 0)] + groups[2:]

    def group_names(g):
        kind, i = g
        if kind == "mix":
            return [(n, i // N_MIXERS) for n in MIXER_BIG[i % N_MIXERS]]
        return [(n, i) for n in {"ff": ["ff_w1", "ff_w2"], "ff_up": ["ff_w1"], "ff_down": ["ff_w2"]}[kind]]

    def shards_of(g):
        return [_shard_for_gather(n, weights[n], j) for n, j in group_names(g)]

    whole = lambda g: [n in DEV_BLOCKED for n, _ in group_names(g)]

    small_shapes = [_shard_shape(n) for n in SMALL_SHARDED]
    gather = {0: _gather_start([_pack_rows([weights[n] for n in SMALL_SHARDED], SMALL_WIDTH, SUBLANES, F32)]
                               + shards_of(groups[0]), name="gather_g0_start", by_cols=[False] + whole(groups[0]))}
    cur, cur_m = x[0], x[0]
    prepared = {k: _gather_prepare(shards_of(forward[k]), whole(forward[k]), name=f"gather_g{k}_start")
                for k in range(1, len(forward))}
    early = [a for k in prepared for a in prepared[k][3]]
    gather[0] = _gather_pass_on(gather[0], name="gather_g0_pass", after=early)
    first = _gather_finish(gather[0], name="gather_g0_finish", after=gather[0]["token"])
    small_all, first_matrices = first[0], first[1:]
    small_full = {n: weights[n] for n in SMALL if PARAMS[n][1] is None}
    for n, stacked in zip(SMALL_SHARDED, _unpack_rows(small_all, small_shapes, SMALL_WIDTH, SUBLANES)):
        small_full[n] = _merge_shards(stacked, PARAMS[n][1])

    def small_weights(i):
        wl = {}
        for n in SMALL:
            if n.startswith("ln_"):
                wl[n] = small_full[n][i:i + 1]
            elif n.startswith(("sc_", "mla_", "cf_")[i % N_MIXERS]):
                v = small_full[n][i // N_MIXERS]
                wl[n] = v if v.ndim == 2 else v.reshape(1, -1)
        return wl

    layers = [small_weights(i) for i in range(DEPTH)]

    def install(g, gathered):
        for (n, _), a in zip(group_names(g), gathered):
            layers[g[1]][n] = _as_operand(n, a)

    install(forward[0], first_matrices)
    gather[1] = _gather_start(None, name="gather_g1_start", after=gather[0]["token"], prepared=prepared[1])
    gather[2] = _gather_start(None, name="gather_g2_start", after=gather[1]["token"], prepared=prepared[2])
    saved = {}
    hidden = None
    for k, g in enumerate(forward):
        kind, i = g
        tokens = [gather[1]["token"]] if k == 0 else []
        if k + 2 < len(forward):
            tokens.append(gather[k + 2]["token"])

        def pass_on(y, k=k):
            if k + 1 == len(forward):
                return None
            gather[k + 1] = _gather_pass_on(gather[k + 1], name=f"gather_g{k + 1}_pass", after=y)
            return gather[k + 1]["token"]

        if k + 1 == len(forward):
            loss_local, dy, saved[g] = _ff_fwd(i, cur, cur_m, layers[i], after=tokens, loss_target=target[0])
            break
        if kind == "ff_up":
            hidden = _mlp_up(cur_m, layers[i]["ff_w1"], f"l{i}_ff", after=tokens)
            behind, done = _hooked(hidden, pass_on), hidden
        elif kind == "ff_down":
            cur, cur_m, saved[("ff", i)] = _ff_fwd(i, cur, cur_m, layers[i], hook=pass_on, after=[*tokens, *behind],
                                                   act=hidden)
            done = cur
        else:
            phase = _mix_fwd if kind == "mix" else _ff_fwd
            cur, cur_m, saved[g] = phase(i, cur, cur_m, layers[i], hook=pass_on, after=tokens)
            done = cur
        if k + 3 < len(forward):
            gathered, gather[k + 3] = _gather_finish(gather[k + 1], name=f"gather_g{k + 1}_finish", after=done,
                                                     then_start=prepared[k + 3])
        else:
            gathered = _gather_finish(gather[k + 1], name=f"gather_g{k + 1}_finish", after=done)
        install(forward[k + 1], gathered)

    loss = lax.psum(loss_local, MESH_AXES)

    reduced = {n: [None] * PARAMS[n][0][0] for n in BIG}
    small_grads = {n: [None] * PARAMS[n][0][0] for n in SMALL}
    reduce = {}

    def to_chips(p, after):
        st = reduce[p]
        parts, got = _swap_finish(st["sibling"], name=f"reduce_p{p}_sibling_finish", after=after)
        sums = _chip_partials_group(parts, got, name=f"reduce_p{p}_chipsum")
        st["own"] = [own for own, _ in sums]
        others = [o for _, o in sums]
        st["chips"] = _swap_start(others, [lax.empty(o.shape, o.dtype) for o in others], _to_chip, 3,
                                  name=f"reduce_p{p}_chips_start")
        return st["chips"]["token"]

    def arrived(phases, after):
        done = _swap_finish([reduce[p]["chips"] for p in phases], name=f"reduce_p{phases[0]}_chips_finish", after=after)
        for p, (_, got) in zip(phases, done):
            for (n, j), own, arr in zip(reduce[p]["names"], reduce[p]["own"], got):
                reduced[n][j] = (own, arr)

    terms = [(1.0, dy)]
    back = list(reversed(groups))
    for p, g in enumerate(back):
        kind, i = g

        def after_first_kernel(dz, p=p):
            return to_chips(p - 1, dz) if p >= 1 else None

        phase = _mix_bwd if kind == "mix" else _ff_bwd
        tokens = [reduce[p - 1]["sibling"]["token"]] if p >= 1 else []
        terms, grads = phase(i, terms, saved[g], layers[i], hook=after_first_kernel, after=tokens)
        for n in SMALL:
            if n in grads:
                small_grads[n][i if n.startswith("ln_") else i // N_MIXERS] = grads[n]
        names = group_names(g)
        parts = [_as_parts(n, grads[n]) for n, _ in names]
        reduce[p] = {"names": names, "sibling": _swap_start(
            parts, [lax.empty((4, *a.shape[1:]), a.dtype) for a in parts], _to_sibling, 4,
            name=f"reduce_p{p}_sibling_start")}
    (_, grad_x), = terms
    last = len(back) - 1
    small_mine = _pack_rows([jnp.stack(small_grads[n]).reshape(PARAMS[n][0]) for n in SMALL], SMALL_WIDTH, SUBLANES, F32)
    small_gather = _gather_start([small_mine], name="gather_small_grads_start")
    tail = to_chips(last, small_gather["token"])
    arrived(list(range(last)), tail)

    grad, delta, new_m, new_v = {}, {}, {}, {}

    def adamw_matrix(n, after):
        shape = weights[n].shape
        view = lambda a: a.reshape(shape[0], shape[1], -1)
        outs = _adamw_reduced(view(weights[n]), view(m_state[n]), view(v_state[n]), reduced[n], name=f"adamw_{n}",
                              after=after)
        grad[n], delta[n], new_m[n], new_v[n] = [o.reshape(shape) for o in outs]
        return outs[0]

    done = tail
    for n in BIG:
        if n not in MIXER_BIG[0]:
            done = adamw_matrix(n, [done])
    small_gather = _gather_pass_on(small_gather, name="gather_small_grads_pass", after=done)
    small_all = _gather_finish(small_gather, name="gather_small_grads_finish", after=small_gather["token"])[0]
    small_sum = _sum_blocks(None, small_all, name="sum_small_grads")
    for n, g in zip(SMALL, _unpack_rows(small_sum, [PARAMS[n][0] for n in SMALL], SMALL_WIDTH, SUBLANES)):
        ax = PARAMS[n][1]
        grad[n] = g if ax is None else lax.dynamic_index_in_dim(_split_shards(g, ax), me, axis=0, keepdims=False)
    small_shards = [_shard_shape(n) for n in SMALL]
    packed = [_pack_rows([src[n] for n in SMALL], SMALL_WIDTH, SUBLANES, F32) for src in (weights, grad, m_state, v_state)]
    small_out = _adamw(*packed, name="adamw_small")
    for outs, dst in zip(small_out, (delta, new_m, new_v)):
        for n, a in zip(SMALL, _unpack_rows(outs, small_shards, SMALL_WIDTH, SUBLANES)):
            dst[n] = a
    arrived([last], small_sum)
    for n in MIXER_BIG[0]:
        adamw_matrix(n, [])
    return (loss, grad_x[None], *[grad[n] for n in NAMES], *[delta[n] for n in NAMES],
            *[new_m[n] for n in NAMES], *[new_v[n] for n in NAMES])


def kernel(x, sc_w_in, sc_conv_w, sc_w_out, mla_w_dq, mla_g_q, mla_w_uq, mla_w_dkv, mla_g_kv, mla_w_uk, mla_w_uv, mla_w_o, cf_w_pw1, cf_b_pw1, cf_dw_w, cf_dw_b, cf_norm_g, cf_norm_b, cf_w_pw2, cf_b_pw2, ff_w1, ff_w2, ln_mix_g, ln_mix_b, ln_ff_g, ln_ff_b, loss_target, m_sc_w_in, m_sc_conv_w, m_sc_w_out, m_mla_w_dq, m_mla_g_q, m_mla_w_uq, m_mla_w_dkv, m_mla_g_kv, m_mla_w_uk, m_mla_w_uv, m_mla_w_o, m_cf_w_pw1, m_cf_b_pw1, m_cf_dw_w, m_cf_dw_b, m_cf_norm_g, m_cf_norm_b, m_cf_w_pw2, m_cf_b_pw2, m_ff_w1, m_ff_w2, m_ln_mix_g, m_ln_mix_b, m_ln_ff_g, m_ln_ff_b, v_sc_w_in, v_sc_conv_w, v_sc_w_out, v_mla_w_dq, v_mla_g_q, v_mla_w_uq, v_mla_w_dkv, v_mla_g_kv, v_mla_w_uk, v_mla_w_uv, v_mla_w_o, v_cf_w_pw1, v_cf_b_pw1, v_cf_dw_w, v_cf_dw_b, v_cf_norm_g, v_cf_norm_b, v_cf_w_pw2, v_cf_b_pw2, v_ff_w1, v_ff_w2, v_ln_mix_g, v_ln_mix_b, v_ln_ff_g, v_ln_ff_b):
    w_list = (sc_w_in, sc_conv_w, sc_w_out, mla_w_dq, mla_g_q, mla_w_uq, mla_w_dkv, mla_g_kv, mla_w_uk, mla_w_uv, mla_w_o,
              cf_w_pw1, cf_b_pw1, cf_dw_w, cf_dw_b, cf_norm_g, cf_norm_b, cf_w_pw2, cf_b_pw2, ff_w1, ff_w2, ln_mix_g,
              ln_mix_b, ln_ff_g, ln_ff_b)
    m_list = (m_sc_w_in, m_sc_conv_w, m_sc_w_out, m_mla_w_dq, m_mla_g_q, m_mla_w_uq, m_mla_w_dkv, m_mla_g_kv, m_mla_w_uk,
              m_mla_w_uv, m_mla_w_o, m_cf_w_pw1, m_cf_b_pw1, m_cf_dw_w, m_cf_dw_b, m_cf_norm_g, m_cf_norm_b, m_cf_w_pw2,
              m_cf_b_pw2, m_ff_w1, m_ff_w2, m_ln_mix_g, m_ln_mix_b, m_ln_ff_g, m_ln_ff_b)
    v_list = (v_sc_w_in, v_sc_conv_w, v_sc_w_out, v_mla_w_dq, v_mla_g_q, v_mla_w_uq, v_mla_w_dkv, v_mla_g_kv, v_mla_w_uk,
              v_mla_w_uv, v_mla_w_o, v_cf_w_pw1, v_cf_b_pw1, v_cf_dw_w, v_cf_dw_b, v_cf_norm_g, v_cf_norm_b, v_cf_w_pw2,
              v_cf_b_pw2, v_ff_w1, v_ff_w2, v_ln_mix_g, v_ln_mix_b, v_ln_ff_g, v_ln_ff_b)
    return _train_step(x, loss_target, dict(zip(NAMES, w_list)), dict(zip(NAMES, m_list)), dict(zip(NAMES, v_list)))
```

```python
import jax
import jax.numpy as jnp
from jax import lax
from jax.experimental import pallas as pl
from jax.experimental.pallas import tpu as pltpu

F32 = jnp.float32
MXU_DTYPE = jnp.bfloat16

N_DEV = 8
D_MODEL = 1024
DEPTH = 4
N_MIXERS = 3
CHUNK = 64
ALPHA = (2.0 * DEPTH) ** 0.25
LN_EPS = 1e-5
RMS_EPS = 1e-6
MLA_HEADS = 8
QK_NOPE = 128
QK_ROPE = 64
V_HEAD = 128
Q_LORA = 384
KV_LORA = 256
ROPE_THETA = 10000.0
ADAM_LR = 0.001
ADAM_B1 = 0.9
ADAM_B2 = 0.999
ADAM_EPS = 1e-08
ADAM_WD = 0.01
ADAM_STEP = 10

LANES = 128
SUBLANES = 8
BF16_ROWS = 16
VMEM_LIMIT = 56 * 1024 * 1024
MM_VMEM_BUDGET = 36 * 1024 * 1024
CONV_PAD = 32
MASK_VALUE = -1e30

MESH_AXES = ("x", "y", "c")


def _params(semantics):
    return pltpu.CompilerParams(dimension_semantics=semantics, vmem_limit_bytes=VMEM_LIMIT)


def _tile(n, cap):
    if n <= cap:
        return n
    t = cap - cap % LANES
    while t >= LANES:
        if n % t == 0:
            return t
        t -= LANES
    raise ValueError(f"no tile for {n} under {cap}")


def _row_tile(rows, cap):
    if rows <= cap:
        return rows
    t = cap - cap % SUBLANES
    while t >= SUBLANES:
        if rows % t == 0:
            return t
        t -= SUBLANES
    raise ValueError(f"no row tile for {rows} under {cap}")


def _mm(a, b, *, name, ta=False, tb=False, b_dev=False, out_dev=False, out_dtypes=(F32,), epilogue=None,
        tile_extras=(), col_extras=(), after=(), tm_cap=2048, tn_cap=1024, tk_cap=4096):
    m, k = (a.shape[1], a.shape[0]) if ta else a.shape
    if b_dev:
        nd, b_rows, ns = b.shape
        n, kb = (b_rows, nd * ns) if tb else (nd * ns, b_rows)
    else:
        n, kb = (b.shape[0], b.shape[1]) if tb else (b.shape[1], b.shape[0])
    assert k == kb, (a.shape, b.shape, ta, tb)
    if ta or tile_extras or (b_dev and tb):
        tm_cap = min(tm_cap, 1024)
    tk = ns if (b_dev and tb) else _tile(k, tk_cap)
    nk = k // tk
    fixed_tn = ns if (b_dev and not tb) else (n // N_DEV if out_dev else None)
    if out_dev:
        assert not tile_extras and not col_extras and n % N_DEV == 0
    out_bytes = sum(jnp.dtype(dt).itemsize for dt in out_dtypes) + sum(e.dtype.itemsize for e in tile_extras)
    tm, tn = None, None
    for cand_m, cand_n in ((tm_cap, tn_cap), (tm_cap, tn_cap // 2), (tm_cap // 2, tn_cap), (tm_cap // 2, tn_cap // 2),
                           (tm_cap // 4, tn_cap // 2), (tm_cap // 4, tn_cap // 4)):
        tm, tn = _tile(m, cand_m), (fixed_tn if fixed_tn is not None else _tile(n, cand_n))
        need = 2 * (tm * tk * a.dtype.itemsize + tk * tn * b.dtype.itemsize) + 2 * tm * tn * out_bytes
        need += tm * tn * 4 * (2 if nk > 1 else 1)
        if need <= MM_VMEM_BUDGET:
            break
    n_te, n_ce, n_out = len(tile_extras), len(col_extras), len(out_dtypes)
    dims = (((0 if ta else 1,), (1 if tb else 0,)), ((), ()))

    def body(*refs):
        a_ref, b_ref = refs[0], refs[1]
        te_refs = refs[2:2 + n_te]
        ce_refs = refs[2 + n_te:2 + n_te + n_ce]
        first_out = 2 + n_te + n_ce + len(after)
        out_refs = refs[first_out:first_out + n_out]

        def finish(acc):
            if epilogue is None:
                outs = (acc,)
            else:
                outs = epilogue(acc, *[r[...] for r in te_refs], *[r[...] for r in ce_refs])
            for o_ref, o in zip(out_refs, outs):
                o_ref[...] = o.astype(o_ref.dtype)

        def product():
            return lax.dot_general(a_ref[...].astype(MXU_DTYPE), b_ref[...].astype(MXU_DTYPE), dims,
                                   preferred_element_type=F32)

        if nk == 1:
            finish(product())
        else:
            acc_ref = refs[-1]
            kk = pl.program_id(2)

            @pl.when(kk == 0)
            def _():
                acc_ref[...] = jnp.zeros_like(acc_ref)

            acc_ref[...] += product()

            @pl.when(kk == nk - 1)
            def _():
                finish(acc_ref[...])

    a_spec = pl.BlockSpec((tk, tm), lambda i, j, kk: (kk, i)) if ta else pl.BlockSpec((tm, tk), lambda i, j, kk: (i, kk))
    if b_dev and tb:
        b_spec = pl.BlockSpec((None, tn, tk), lambda i, j, kk: (kk, j, 0))
    elif b_dev:
        b_spec = pl.BlockSpec((None, tk, tn), lambda i, j, kk: (j, kk, 0))
    elif tb:
        b_spec = pl.BlockSpec((tn, tk), lambda i, j, kk: (j, kk))
    else:
        b_spec = pl.BlockSpec((tk, tn), lambda i, j, kk: (kk, j))
    tile_spec = pl.BlockSpec((tm, tn), lambda i, j, kk: (i, j))
    col_spec = pl.BlockSpec((1, tn), lambda i, j, kk: (0, j))
    if out_dev:
        out_spec = pl.BlockSpec((None, tm, tn), lambda i, j, kk: (j, i, 0))
        out_shape = [jax.ShapeDtypeStruct((N_DEV, m, tn), dt) for dt in out_dtypes]
    else:
        out_spec = tile_spec
        out_shape = [jax.ShapeDtypeStruct((m, n), dt) for dt in out_dtypes]
    outs = pl.pallas_call(
        body,
        name=name,
        grid=(m // tm, n // tn, nk),
        in_specs=[a_spec, b_spec] + [tile_spec] * n_te + [col_spec] * n_ce + [ANY] * len(after),
        out_specs=[out_spec] * n_out,
        out_shape=out_shape,
        scratch_shapes=[pltpu.VMEM((tm, tn), F32)] if nk > 1 else [],
        compiler_params=_params(("parallel", "parallel", "arbitrary")),
    )(a, b, *tile_extras, *col_extras, *after)
    return outs[0] if n_out == 1 else tuple(outs)


def _rowwise(fn, rows, bcast, out_rows, out_accs, *, name, tb_cap=256, after=()):
    t = rows[0].shape[0]
    tb = _row_tile(t, tb_cap)
    n_r, n_b, n_o, n_a = len(rows), len(bcast), len(out_rows), len(out_accs)

    def body(*refs):
        r_refs = refs[:n_r]
        b_refs = refs[n_r:n_r + n_b]
        first_out = n_r + n_b + len(after)
        o_refs = refs[first_out:first_out + n_o]
        a_refs = refs[first_out + n_o:]
        outs, accs = fn([r[...] for r in r_refs], [r[...] for r in b_refs])
        for o_ref, o in zip(o_refs, outs):
            o_ref[...] = o.astype(o_ref.dtype)
        if n_a:
            i = pl.program_id(0)

            @pl.when(i == 0)
            def _():
                for a_ref, acc in zip(a_refs, accs):
                    a_ref[...] = acc

            @pl.when(i > 0)
            def _():
                for a_ref, acc in zip(a_refs, accs):
                    a_ref[...] += acc

    in_specs = [pl.BlockSpec((tb, r.shape[1]), lambda i: (i, 0)) for r in rows]
    in_specs += [pl.BlockSpec(b.shape, lambda i: (0, 0)) for b in bcast]
    in_specs += [ANY] * len(after)
    out_specs = [pl.BlockSpec((tb, c), lambda i: (i, 0)) for c, _ in out_rows]
    out_specs += [pl.BlockSpec(s, lambda i: (0, 0)) for s in out_accs]
    out_shape = [jax.ShapeDtypeStruct((t, c), dt) for c, dt in out_rows]
    out_shape += [jax.ShapeDtypeStruct(s, F32) for s in out_accs]
    outs = pl.pallas_call(
        body,
        name=name,
        grid=(t // tb,),
        in_specs=in_specs,
        out_specs=out_specs,
        out_shape=out_shape,
        compiler_params=_params(("arbitrary",)),
    )(*rows, *bcast, *after)
    return tuple(outs)


def _colsum(v):
    return jnp.sum(v, axis=0, keepdims=True)


def _sigmoid(v):
    return 1.0 / (1.0 + jnp.exp(-v))


def _ln_fwd(x_res, y_sub, g, b, *, name, after=()):
    d = x_res.shape[1]

    def fn(rows, bc):
        z = ALPHA * rows[0] + rows[1]
        mu = jnp.mean(z, axis=-1, keepdims=True)
        zc = z - mu
        var = jnp.mean(zc * zc, axis=-1, keepdims=True)
        rstd = lax.rsqrt(var + LN_EPS)
        xhat = zc * rstd
        out = xhat * bc[0] + bc[1]
        return (out, out, xhat, rstd), ()

    return _rowwise(fn, [x_res, y_sub], [g, b], [(d, F32), (d, MXU_DTYPE), (d, F32), (1, F32)], [], name=name,
                    after=after, tb_cap=512)


def _ln_fwd_loss(x_res, y_sub, g, b, target, *, name, after=()):
    d = x_res.shape[1]

    def fn(rows, bc):
        z = ALPHA * rows[0] + rows[1]
        mu = jnp.mean(z, axis=-1, keepdims=True)
        zc = z - mu
        var = jnp.mean(zc * zc, axis=-1, keepdims=True)
        rstd = lax.rsqrt(var + LN_EPS)
        xhat = zc * rstd
        err = xhat * bc[0] + bc[1] - rows[2]
        return (xhat, rstd, err * (1.0 / d)), (_colsum(err * err),)

    return _rowwise(fn, [x_res, y_sub, target], [g, b], [(d, F32), (1, F32), (d, F32)], [(1, d)], name=name,
                    after=after, tb_cap=512)


def _ln_bwd(terms, xhat, rstd, g, *, name, after=()):
    d = xhat.shape[1]
    coefs = [c for c, _ in terms]

    def fn(rows, bc):
        xh, rs = rows[0], rows[1]
        dout = None
        for c, v in zip(coefs, rows[2:]):
            tv = v if c == 1.0 else c * v
            dout = tv if dout is None else dout + tv
        dxh = dout * bc[0]
        m1 = jnp.mean(dxh, axis=-1, keepdims=True)
        m2 = jnp.mean(dxh * xh, axis=-1, keepdims=True)
        dz = rs * (dxh - m1 - xh * m2)
        return (dz, dz), (_colsum(dout * xh), _colsum(dout), _colsum(dz))

    return _rowwise(fn, [xhat, rstd] + [v for _, v in terms], [g], [(d, F32), (d, MXU_DTYPE)],
                    [(1, d), (1, d), (1, d)], name=name, after=after, tb_cap=512)


def _rms_fwd(x, g, *, name):
    c = x.shape[1]

    def fn(rows, bc):
        v = rows[0]
        r = lax.rsqrt(jnp.mean(v * v, axis=-1, keepdims=True) + RMS_EPS)
        return (v * r * bc[0],), ()

    return _rowwise(fn, [x], [g], [(c, MXU_DTYPE)], [], name=name)[0]


def _rms_bwd(x, g, dy, *, name):
    c = x.shape[1]

    def fn(rows, bc):
        v, dyv = rows
        r = lax.rsqrt(jnp.mean(v * v, axis=-1, keepdims=True) + RMS_EPS)
        xh = v * r
        dxh = dyv * bc[0]
        dx = r * (dxh - xh * jnp.mean(dxh * xh, axis=-1, keepdims=True))
        return (dx,), (_colsum(dyv * xh),)

    return _rowwise(fn, [x, dy], [g], [(c, F32)], [(1, c)], name=name)


def _conv_chunk(t):
    return _row_tile(t, 256)


ATTN_SCALE = (QK_NOPE + QK_ROPE) ** -0.5
_NT = (((1,), (1,)), ((), ()))
_TN = (((0,), (0,)), ((), ()))
_NN = (((1,), (0,)), ((), ()))


def _dot(a, b, dims):
    return lax.dot_general(a, b, dims, preferred_element_type=F32)


def _block_scores(q_cat, kn_ref, kp_ref, lo, hi, diagonal):
    k_cat = jnp.concatenate([kn_ref[lo:hi, :], kp_ref[lo:hi, :]], axis=1)
    s = _dot(q_cat, k_cat, _NT) * ATTN_SCALE
    if diagonal:
        shift = CHUNK.bit_length() - 1
        rows = lax.broadcasted_iota(jnp.int32, s.shape, 0)
        cols = lax.broadcasted_iota(jnp.int32, s.shape, 1)
        s = jnp.where(jnp.right_shift(cols, shift) <= jnp.right_shift(rows, shift), s, MASK_VALUE)
    return s, k_cat


def _attn_fwd(qn, qp, kv, kp, *, name):
    h, t, _ = qp.shape
    tq = _row_tile(t, 256)
    assert tq % CHUNK == 0

    def body(qn_ref, qp_ref, kn_ref, kp_ref, v_ref, o_ref, of_ref, lse_ref):
        for r0 in range(0, t, tq):
            r1 = r0 + tq
            q_cat = jnp.concatenate([qn_ref[r0:r1, :].astype(MXU_DTYPE), qp_ref[r0:r1, :]], axis=1)
            s_d, _ = _block_scores(q_cat, kn_ref, kp_ref, r0, r1, True)
            m = jnp.max(s_d, axis=-1, keepdims=True)
            if r0:
                s_b, _ = _block_scores(q_cat, kn_ref, kp_ref, 0, r0, False)
                m = jnp.maximum(m, jnp.max(s_b, axis=-1, keepdims=True))
            p_d = jnp.exp(s_d - m)
            l = jnp.sum(p_d, axis=-1, keepdims=True)
            acc = _dot(p_d.astype(MXU_DTYPE), v_ref[r0:r1, :], _NN)
            if r0:
                p_b = jnp.exp(s_b - m)
                l = l + jnp.sum(p_b, axis=-1, keepdims=True)
                acc = acc + _dot(p_b.astype(MXU_DTYPE), v_ref[0:r0, :], _NN)
            o = acc / l
            o_ref[r0:r1, :] = o.astype(o_ref.dtype)
            of_ref[r0:r1, :] = o
            lse_ref[r0:r1, :] = m + jnp.log(l)

    blk = lambda w: pl.BlockSpec((None, t, w), lambda hh: (hh, 0, 0))
    cols = pl.BlockSpec((t, QK_NOPE), lambda hh: (0, hh))
    keys = pl.BlockSpec((t, QK_NOPE), lambda hh: (0, 2 * hh))
    values = pl.BlockSpec((t, V_HEAD), lambda hh: (0, 2 * hh + 1))
    return pl.pallas_call(
        body,
        name=name,
        grid=(h,),
        in_specs=[cols, blk(QK_ROPE), keys, pl.BlockSpec((t, QK_ROPE), lambda hh: (0, 0)), values],
        out_specs=[cols, cols, blk(1)],
        out_shape=[jax.ShapeDtypeStruct((t, h * V_HEAD), MXU_DTYPE), jax.ShapeDtypeStruct((t, h * V_HEAD), F32),
                   jax.ShapeDtypeStruct((h, t, 1), F32)],
        compiler_params=_params(("parallel",)),
    )(qn, qp, kv, kp, kv)


def _attn_bwd(qn, qp, kv, kp, do, o, lse, *, name):
    h, t, _ = qp.shape
    tq = _row_tile(t, 256)

    def body(qn_ref, qp_ref, kn_ref, kp_ref, v_ref, do_ref, o_ref, lse_ref, dqn_ref, dqp_ref, dkv_ref, dkp_ref):
        dkv_ref[...] = jnp.zeros_like(dkv_ref)

        @pl.when(pl.program_id(0) == 0)
        def _():
            dkp_ref[...] = jnp.zeros_like(dkp_ref)

        for r0 in range(0, t, tq):
            r1 = r0 + tq
            q_cat = jnp.concatenate([qn_ref[r0:r1, :].astype(MXU_DTYPE), qp_ref[r0:r1, :]], axis=1)
            dov, lse = do_ref[r0:r1, :], lse_ref[r0:r1, :]
            delta = jnp.sum(dov.astype(F32) * o_ref[r0:r1, :], axis=-1, keepdims=True)

            def piece(lo, hi, diagonal):
                s, k_cat = _block_scores(q_cat, kn_ref, kp_ref, lo, hi, diagonal)
                p = jnp.exp(s - lse)
                dp = _dot(dov, v_ref[lo:hi, :], _NT)
                ds_m = (p * (dp - delta) * ATTN_SCALE).astype(MXU_DTYPE)
                dk_cat = _dot(ds_m, q_cat, _TN)
                dkv_ref[lo:hi, :QK_NOPE] += dk_cat[:, :QK_NOPE]
                dkp_ref[lo:hi, :] += dk_cat[:, QK_NOPE:]
                dkv_ref[lo:hi, QK_NOPE:] += _dot(p.astype(MXU_DTYPE), dov, _TN)
                return _dot(ds_m, k_cat, _NN)

            dq_cat = piece(r0, r1, True)
            if r0:
                dq_cat = dq_cat + piece(0, r0, False)
            dqn_ref[r0:r1, :] = dq_cat[:, :QK_NOPE]
            dqp_ref[r0:r1, :] = dq_cat[:, QK_NOPE:]

    blk = lambda w: pl.BlockSpec((None, t, w), lambda hh: (hh, 0, 0))
    kpblk = pl.BlockSpec((t, QK_ROPE), lambda hh: (0, 0))
    cols = pl.BlockSpec((t, QK_NOPE), lambda hh: (0, hh))
    keys = pl.BlockSpec((t, QK_NOPE), lambda hh: (0, 2 * hh))
    values = pl.BlockSpec((t, V_HEAD), lambda hh: (0, 2 * hh + 1))
    both = pl.BlockSpec((t, QK_NOPE + V_HEAD), lambda hh: (0, hh))
    return pl.pallas_call(
        body,
        name=name,
        grid=(h,),
        in_specs=[cols, blk(QK_ROPE), keys, kpblk, values, cols, cols, blk(1)],
        out_specs=[cols, blk(QK_ROPE), both, kpblk],
        out_shape=[jax.ShapeDtypeStruct((t, h * QK_NOPE), F32), jax.ShapeDtypeStruct((h, t, QK_ROPE), F32),
                   jax.ShapeDtypeStruct((t, h * (QK_NOPE + V_HEAD)), F32), jax.ShapeDtypeStruct((t, QK_ROPE), F32)],
        compiler_params=_params(("arbitrary",)),
    )(qn, qp, kv, kp, kv, do, o, lse)


def _rope(x1, x2, cos, sin, *, name, inverse=False):
    w = x1.shape[1]
    sign = -1.0 if inverse else 1.0

    def fn(rows, bc):
        a, b, c, s = rows
        s = sign * s
        return (a * c - b * s, a * s + b * c), ()

    return _rowwise(fn, [x1, x2, cos, sin], [], [(w, F32), (w, F32)], [], name=name)


def _rope_in_place(x, cos_pairs, sin_signed, *, name, inverse=False):
    w = x.shape[1]
    half = ROPE_HALF
    sign = -1.0 if inverse else 1.0

    def fn(rows, bc):
        v, c, s = rows
        lane = lax.broadcasted_iota(jnp.int32, v.shape, 1)
        partner = jnp.where((lane & half) == 0, pltpu.roll(v, w - half, 1), pltpu.roll(v, half, 1))
        return (v * c + partner * (sign * s),), ()

    return _rowwise(fn, [x, cos_pairs, sin_signed], [], [(w, F32)], [], name=name)[0]


def _mlp_up(x_m, w1, tag, after=()):
    def epi(acc):
        r = jnp.maximum(acc, 0.0)
        return (r * r,)

    return _mm(x_m, w1, name=f"{tag}_up", out_dtypes=(MXU_DTYPE,), epilogue=epi, after=after)


def _mlp_fwd(x_m, w1, w2, tag, after=(), act=None):
    if act is None:
        act, after = _mlp_up(x_m, w1, tag, after=after), ()
    y = _mm(act, w2, name=f"{tag}_down", after=after)
    return y, (x_m, act)


def _plus_residual(acc, res):
    return (acc + ALPHA * res,)


def _mlp_bwd(dy_m, saved, w1, w2, tag, residual, after=()):
    x_m, act = saved
    dw2 = _mm(act, dy_m, ta=True, name=f"{tag}_dw2", out_dtypes=(MXU_DTYPE,))

    def epi(acc, av):
        return (acc * (2.0 * jnp.sqrt(av.astype(F32))),)

    dh = _mm(dy_m, w2, tb=True, name=f"{tag}_dact", out_dtypes=(MXU_DTYPE,), epilogue=epi, tile_extras=(act,),
             after=after)
    dw1 = _mm(x_m, dh, ta=True, out_dev=True, name=f"{tag}_dw1", out_dtypes=(MXU_DTYPE,))
    dx = _mm(dh, w1, tb=True, name=f"{tag}_dx", epilogue=_plus_residual, tile_extras=(residual,))
    return [dx], {"ff_w1": dw1, "ff_w2": dw2}


def _sc_gate_specs(t, d):
    per = d // LANES
    return [pl.BlockSpec((t, LANES), lambda j, k=k: (0, k * per + j)) for k in range(3)]


def _sc_gate_fwd(u, conv_w, *, name):
    t, d = u.shape[0], u.shape[1] // 3
    kw = conv_w.shape[0]
    ch = _conv_chunk(t)

    def body(b_ref, c_ref, h_ref, w_ref, r_ref, pad_ref):
        pad_ref[0:CONV_PAD, :] = jnp.zeros((CONV_PAD, LANES), F32)
        pad_ref[CONV_PAD:CONV_PAD + t, :] = c_ref[...] * h_ref[...]
        for t0 in range(0, t, ch):
            q = jnp.zeros((ch, LANES), F32)
            for k in range(kw):
                s = kw - 1 - k
                q = q + w_ref[k:k + 1, :] * pad_ref[CONV_PAD - s + t0:CONV_PAD - s + t0 + ch, :]
            r_ref[t0:t0 + ch, :] = (b_ref[t0:t0 + ch, :] * q).astype(r_ref.dtype)

    blk = pl.BlockSpec((t, LANES), lambda j: (0, j))
    return pl.pallas_call(
        body,
        name=name,
        grid=(d // LANES,),
        in_specs=_sc_gate_specs(t, d) + [pl.BlockSpec((kw, LANES), lambda j: (0, j))],
        out_specs=blk,
        out_shape=jax.ShapeDtypeStruct((t, d), MXU_DTYPE),
        scratch_shapes=[pltpu.VMEM((CONV_PAD + t, LANES), F32)],
        compiler_params=_params(("parallel",)),
    )(u, u, u, conv_w)


def _sc_gate_bwd(dr, u, conv_w, *, name):
    t, d = u.shape[0], u.shape[1] // 3
    kw = conv_w.shape[0]
    ch = _conv_chunk(t)

    def body(dr_ref, b_ref, c_ref, h_ref, w_ref, db_ref, dc_ref, dh_ref, dw_ref, ppad_ref, dpad_ref):
        ppad_ref[0:CONV_PAD, :] = jnp.zeros((CONV_PAD, LANES), F32)
        ppad_ref[CONV_PAD:CONV_PAD + t, :] = c_ref[...] * h_ref[...]
        dpad_ref[0:t, :] = dr_ref[...] * b_ref[...]
        dpad_ref[t:t + CONV_PAD, :] = jnp.zeros((CONV_PAD, LANES), F32)
        for t0 in range(0, t, ch):
            q = jnp.zeros((ch, LANES), F32)
            dp = jnp.zeros((ch, LANES), F32)
            for k in range(kw):
                s = kw - 1 - k
                q = q + w_ref[k:k + 1, :] * ppad_ref[CONV_PAD - s + t0:CONV_PAD - s + t0 + ch, :]
                dp = dp + w_ref[k:k + 1, :] * dpad_ref[t0 + s:t0 + s + ch, :]
            db_ref[t0:t0 + ch, :] = (dr_ref[t0:t0 + ch, :] * q).astype(db_ref.dtype)
            dc_ref[t0:t0 + ch, :] = (dp * h_ref[t0:t0 + ch, :]).astype(dc_ref.dtype)
            dh_ref[t0:t0 + ch, :] = (dp * c_ref[t0:t0 + ch, :]).astype(dh_ref.dtype)
        for k in range(kw):
            s = kw - 1 - k
            acc = jnp.zeros((ch, LANES), F32)
            for t0 in range(0, t, ch):
                acc = acc + dpad_ref[t0:t0 + ch, :] * ppad_ref[CONV_PAD - s + t0:CONV_PAD - s + t0 + ch, :]
            dw_ref[k:k + 1, :] = _colsum(acc)

    blk = pl.BlockSpec((t, LANES), lambda j: (0, j))
    wblk = pl.BlockSpec((kw, LANES), lambda j: (0, j))
    return pl.pallas_call(
        body,
        name=name,
        grid=(d // LANES,),
        in_specs=[blk] + _sc_gate_specs(t, d) + [wblk],
        out_specs=[blk, blk, blk, wblk],
        out_shape=[jax.ShapeDtypeStruct((t, d), MXU_DTYPE)] * 3 + [jax.ShapeDtypeStruct((kw, d), F32)],
        scratch_shapes=[pltpu.VMEM((CONV_PAD + t, LANES), F32), pltpu.VMEM((CONV_PAD + t, LANES), F32)],
        compiler_params=_params(("parallel",)),
    )(dr, u, u, u, conv_w)


def _sc_fwd(x_m, w_in, conv_w, w_out, tag, after=()):
    u = _mm(x_m, w_in, name=f"{tag}_in", after=after)
    r = _sc_gate_fwd(u, conv_w, name=f"{tag}_gate")
    y = _mm(r, w_out, name=f"{tag}_out")
    return y, (x_m, u, r)


def _sc_bwd(dy_m, saved, w_in, conv_w, w_out, tag, residual, after=()):
    d = D_MODEL
    x_m, u, r = saved
    dw_out = _mm(r, dy_m, ta=True, name=f"{tag}_dwout", out_dtypes=(MXU_DTYPE,))
    dr = _mm(dy_m, w_out, tb=True, name=f"{tag}_dr", after=after)
    db, dc, dh, dconv = _sc_gate_bwd(dr, u, conv_w, name=f"{tag}_dgate")
    du = jnp.concatenate([db, dc, dh], axis=1)
    dw_in = _mm(x_m, du, ta=True, out_dev=True, name=f"{tag}_dwin", out_dtypes=(MXU_DTYPE,))
    dx = _mm(du, w_in, tb=True, name=f"{tag}_dx", epilogue=_plus_residual, tile_extras=(residual,))
    return [dx], {"sc_w_in": dw_in, "sc_conv_w": dconv, "sc_w_out": dw_out}


def _cf_glu_specs(t, d):
    per = d // LANES
    return [pl.BlockSpec((t, LANES), lambda j, k=k: (0, k * per + j)) for k in range(2)]


def _cf_glu_conv_fwd(u, w, bias, *, name):
    t, d = u.shape[0], u.shape[1] // 2
    kw = w.shape[0]
    ch = _conv_chunk(t)

    def body(a_ref, g_ref, w_ref, b_ref, o_ref, pad_ref):
        pad_ref[0:CONV_PAD, :] = jnp.zeros((CONV_PAD, LANES), F32)
        pad_ref[CONV_PAD:CONV_PAD + t, :] = a_ref[...] * _sigmoid(g_ref[...])
        for t0 in range(0, t, ch):
            acc = jnp.broadcast_to(b_ref[...], (ch, LANES))
            for k in range(kw):
                s = kw - 1 - k
                acc = acc + w_ref[k:k + 1, :] * pad_ref[CONV_PAD - s + t0:CONV_PAD - s + t0 + ch, :]
            o_ref[t0:t0 + ch, :] = acc

    return pl.pallas_call(
        body,
        name=name,
        grid=(d // LANES,),
        in_specs=_cf_glu_specs(t, d) + [pl.BlockSpec((kw, LANES), lambda j: (0, j)),
                                        pl.BlockSpec((1, LANES), lambda j: (0, j))],
        out_specs=pl.BlockSpec((t, LANES), lambda j: (0, j)),
        out_shape=jax.ShapeDtypeStruct((t, d), F32),
        scratch_shapes=[pltpu.VMEM((CONV_PAD + t, LANES), F32)],
        compiler_params=_params(("parallel",)),
    )(u, u, w, bias)


def _cf_glu_conv_bwd(dy, u, w, *, name):
    t, d = u.shape[0], u.shape[1] // 2
    kw = w.shape[0]
    ch = _conv_chunk(t)

    def body(dy_ref, a_ref, g_ref, w_ref, da_ref, dg_ref, sa_ref, sg_ref, dw_ref, xpad_ref, dpad_ref):
        xpad_ref[0:CONV_PAD, :] = jnp.zeros((CONV_PAD, LANES), F32)
        xpad_ref[CONV_PAD:CONV_PAD + t, :] = a_ref[...] * _sigmoid(g_ref[...])
        dpad_ref[0:t, :] = dy_ref[...]
        dpad_ref[t:t + CONV_PAD, :] = jnp.zeros((CONV_PAD, LANES), F32)
        sum_a = jnp.zeros((1, LANES), F32)
        sum_g = jnp.zeros((1, LANES), F32)
        for t0 in range(0, t, ch):
            dh1 = jnp.zeros((ch, LANES), F32)
            for k in range(kw):
                s = kw - 1 - k
                dh1 = dh1 + w_ref[k:k + 1, :] * dpad_ref[t0 + s:t0 + s + ch, :]
            sg = _sigmoid(g_ref[t0:t0 + ch, :])
            da = dh1 * sg
            dgate = da * a_ref[t0:t0 + ch, :] * (1.0 - sg)
            da_ref[t0:t0 + ch, :] = da.astype(da_ref.dtype)
            dg_ref[t0:t0 + ch, :] = dgate.astype(dg_ref.dtype)
            sum_a = sum_a + _colsum(da)
            sum_g = sum_g + _colsum(dgate)
        sa_ref[...] = sum_a
        sg_ref[...] = sum_g
        for k in range(kw):
            s = kw - 1 - k
            acc = jnp.zeros((ch, LANES), F32)
            for t0 in range(0, t, ch):
                acc = acc + dy_ref[t0:t0 + ch, :] * xpad_ref[CONV_PAD - s + t0:CONV_PAD - s + t0 + ch, :]
            dw_ref[k:k + 1, :] = _colsum(acc)

    blk = pl.BlockSpec((t, LANES), lambda j: (0, j))
    row = pl.BlockSpec((1, LANES), lambda j: (0, j))
    wblk = pl.BlockSpec((kw, LANES), lambda j: (0, j))
    return pl.pallas_call(
        body,
        name=name,
        grid=(d // LANES,),
        in_specs=[blk] + _cf_glu_specs(t, d) + [wblk],
        out_specs=[blk, blk, row, row, wblk],
        out_shape=[jax.ShapeDtypeStruct((t, d), MXU_DTYPE)] * 2 + [jax.ShapeDtypeStruct((1, d), F32)] * 2
        + [jax.ShapeDtypeStruct((kw, d), F32)],
        scratch_shapes=[pltpu.VMEM((CONV_PAD + t, LANES), F32), pltpu.VMEM((CONV_PAD + t, LANES), F32)],
        compiler_params=_params(("parallel",)),
    )(dy, u, u, w)


def _cf_fwd(x_m, w, tag, after=()):
    d = D_MODEL

    def bias_epi(acc, bias):
        return (acc + bias,)

    u = _mm(x_m, w["cf_w_pw1"], name=f"{tag}_pw1", epilogue=bias_epi, col_extras=(w["cf_b_pw1"],),
            after=after)
    h2 = _cf_glu_conv_fwd(u, w["cf_dw_w"], w["cf_dw_b"], name=f"{tag}_conv")

    def norm_fn(rows, bc):
        hv = rows[0]
        mu = jnp.mean(hv, axis=-1, keepdims=True)
        hc = hv - mu
        rstd = lax.rsqrt(jnp.mean(hc * hc, axis=-1, keepdims=True) + LN_EPS)
        h3 = hc * rstd * bc[0] + bc[1]
        return (h3 * _sigmoid(h3),), ()

    h4 = _rowwise(norm_fn, [h2], [w["cf_norm_g"], w["cf_norm_b"]], [(d, MXU_DTYPE)], [], name=f"{tag}_norm")[0]
    y = _mm(h4, w["cf_w_pw2"], name=f"{tag}_pw2", epilogue=bias_epi, col_extras=(w["cf_b_pw2"],))
    return y, (x_m, u, h2, h4)


def _cf_bwd(dy_m, dy_colsum, saved, w, tag, residual, after=()):
    d = D_MODEL
    x_m, u, h2, h4 = saved
    dw_pw2 = _mm(h4, dy_m, ta=True, name=f"{tag}_dwpw2", out_dtypes=(MXU_DTYPE,))
    dh4 = _mm(dy_m, w["cf_w_pw2"], tb=True, name=f"{tag}_dh4", after=after)

    def dnorm_fn(rows, bc):
        dh4v, hv = rows
        g, b = bc
        mu = jnp.mean(hv, axis=-1, keepdims=True)
        hc = hv - mu
        rstd = lax.rsqrt(jnp.mean(hc * hc, axis=-1, keepdims=True) + LN_EPS)
        xh = hc * rstd
        h3 = xh * g + b
        sg = _sigmoid(h3)
        dh3 = dh4v * (sg * (1.0 + h3 * (1.0 - sg)))
        dxh = dh3 * g
        dh2 = rstd * (dxh - jnp.mean(dxh, axis=-1, keepdims=True) - xh * jnp.mean(dxh * xh, axis=-1, keepdims=True))
        return (dh2,), (_colsum(dh3 * xh), _colsum(dh3), _colsum(dh2))

    dh2, dnorm_g, dnorm_b, ddw_b = _rowwise(dnorm_fn, [dh4, h2], [w["cf_norm_g"], w["cf_norm_b"]], [(d, F32)],
                                            [(1, d), (1, d), (1, d)], name=f"{tag}_dnorm")
    da, dgate, sum_a, sum_g, ddw_w = _cf_glu_conv_bwd(dh2, u, w["cf_dw_w"], name=f"{tag}_dconv")
    du = jnp.concatenate([da, dgate], axis=1)
    db_pw1 = jnp.concatenate([sum_a, sum_g], axis=1)
    dw_pw1 = _mm(x_m, du, ta=True, out_dev=True, name=f"{tag}_dwpw1", out_dtypes=(MXU_DTYPE,))
    dx = _mm(du, w["cf_w_pw1"], tb=True, name=f"{tag}_dx", epilogue=_plus_residual, tile_extras=(residual,))
    grads = {"cf_w_pw1": dw_pw1, "cf_b_pw1": db_pw1, "cf_dw_w": ddw_w, "cf_dw_b": ddw_b, "cf_norm_g": dnorm_g,
             "cf_norm_b": dnorm_b, "cf_w_pw2": dw_pw2, "cf_b_pw2": dy_colsum}
    return [dx], grads


def _rope_tables(t):
    pos = jnp.arange(t, dtype=F32)
    inv_freq = ROPE_THETA ** (-jnp.arange(0, QK_ROPE, 2, dtype=F32) / QK_ROPE)
    ang = pos[:, None] * inv_freq[None, :]
    return jnp.cos(ang), jnp.sin(ang)


ROPE_HALF = QK_ROPE // 2
Q_NOPE_COLS = MLA_HEADS * QK_NOPE


def _interleave_heads(w_uk, w_uv):
    c = w_uk.shape[0]
    return jnp.concatenate([w_uk.reshape(c, MLA_HEADS, QK_NOPE), w_uv.reshape(c, MLA_HEADS, V_HEAD)],
                           axis=2).reshape(c, -1)


def _mla_fwd(x_m, w, tag, after=()):
    t = x_m.shape[0]
    cos, sin = _rope_tables(t)
    cos_h = jnp.tile(jnp.concatenate([cos, cos], axis=1), (1, MLA_HEADS))
    sin_h = jnp.tile(jnp.concatenate([-sin, sin], axis=1), (1, MLA_HEADS))
    cq_pre = _mm(x_m, w["mla_w_dq"], name=f"{tag}_dq", after=after)
    cq = _rms_fwd(cq_pre, w["mla_g_q"], name=f"{tag}_qnorm")
    q = _mm(cq, w["mla_w_uq"], name=f"{tag}_uq")
    qp = _rope_in_place(q[:, Q_NOPE_COLS:], cos_h, sin_h, name=f"{tag}_qrope")
    qp_h = qp.reshape(t, MLA_HEADS, QK_ROPE).transpose(1, 0, 2).astype(MXU_DTYPE)
    ckv_full = _mm(x_m, w["mla_w_dkv"], name=f"{tag}_dkv")
    ckv_pre = ckv_full[:, :KV_LORA]
    ckv = _rms_fwd(ckv_pre, w["mla_g_kv"], name=f"{tag}_kvnorm")
    k1, k2 = _rope(ckv_full[:, KV_LORA:KV_LORA + ROPE_HALF], ckv_full[:, KV_LORA + ROPE_HALF:], cos, sin,
                   name=f"{tag}_krope")
    kp = jnp.concatenate([k1, k2], axis=-1).astype(MXU_DTYPE)
    kv = _mm(ckv, _interleave_heads(w["mla_w_uk"], w["mla_w_uv"]), name=f"{tag}_ukv", out_dtypes=(MXU_DTYPE,))
    o, o_f, lse = _attn_fwd(q, qp_h, kv, kp, name=f"{tag}_attn")
    y = _mm(o, w["mla_w_o"], name=f"{tag}_o")
    return y, (x_m, cq_pre, cq, ckv_pre, ckv, q, qp_h, kv, kp, lse, o, o_f, (cos, sin, cos_h, sin_h))


def _mla_bwd(dy_m, saved, w, tag, residual, after=()):
    x_m, cq_pre, cq, ckv_pre, ckv, q, qp_h, kv, kp, lse, o, o_f, (cos, sin, cos_h, sin_h) = saved
    t = x_m.shape[0]
    half = ROPE_HALF
    dw_o = _mm(o, dy_m, ta=True, name=f"{tag}_dwo", out_dtypes=(MXU_DTYPE,))
    do = _mm(dy_m, w["mla_w_o"], tb=True, name=f"{tag}_do", out_dtypes=(MXU_DTYPE,), after=after)
    dqn, dqp_h, dkv, dkp = _attn_bwd(q, qp_h, kv, kp, do, o_f, lse, name=f"{tag}_dattn")
    dqp = _rope_in_place(dqp_h.transpose(1, 0, 2).reshape(t, -1), cos_h, sin_h, name=f"{tag}_dqrope", inverse=True)
    dq = jnp.concatenate([dqn, dqp], axis=-1).astype(MXU_DTYPE)
    dw_uq = _mm(cq, dq, ta=True, name=f"{tag}_dwuq", out_dtypes=(MXU_DTYPE,))
    dcq = _mm(dq, w["mla_w_uq"], tb=True, name=f"{tag}_dcq")
    dcq_pre, dg_q = _rms_bwd(cq_pre, w["mla_g_q"], dcq, name=f"{tag}_dqnorm")
    dcq_pre_m = dcq_pre.astype(MXU_DTYPE)
    dw_dq = _mm(x_m, dcq_pre_m, ta=True, name=f"{tag}_dwdq", out_dtypes=(MXU_DTYPE,))
    dx_q = _mm(dcq_pre_m, w["mla_w_dq"], tb=True, name=f"{tag}_dxq", epilogue=_plus_residual, tile_extras=(residual,))
    dw_kv = _mm(ckv, dkv, ta=True, name=f"{tag}_dwukv", out_dtypes=(MXU_DTYPE,))
    dw_kv = dw_kv.reshape(KV_LORA, MLA_HEADS, QK_NOPE + V_HEAD)
    dw_uk, dw_uv = dw_kv[:, :, :QK_NOPE].reshape(KV_LORA, -1), dw_kv[:, :, QK_NOPE:].reshape(KV_LORA, -1)
    dckv = _mm(dkv, _interleave_heads(w["mla_w_uk"], w["mla_w_uv"]), tb=True, name=f"{tag}_dckv")
    dckv_pre, dg_kv = _rms_bwd(ckv_pre, w["mla_g_kv"], dckv, name=f"{tag}_dkvnorm")
    dk1, dk2 = _rope(dkp[:, :half], dkp[:, half:], cos, sin, name=f"{tag}_dkrope", inverse=True)
    dckv_full = jnp.concatenate([dckv_pre, dk1, dk2], axis=-1).astype(MXU_DTYPE)
    dw_dkv = _mm(x_m, dckv_full, ta=True, name=f"{tag}_dwdkv", out_dtypes=(MXU_DTYPE,))
    dx_kv = _mm(dckv_full, w["mla_w_dkv"], tb=True, name=f"{tag}_dxkv", epilogue=lambda acc, other: (acc + other,),
                tile_extras=(dx_q,))
    grads = {"mla_w_dq": dw_dq, "mla_g_q": dg_q, "mla_w_uq": dw_uq, "mla_w_dkv": dw_dkv, "mla_g_kv": dg_kv,
             "mla_w_uk": dw_uk, "mla_w_uv": dw_uv, "mla_w_o": dw_o}
    return [dx_kv], grads


def _cast_input(x):
    cast = lambda rows, bc: ((rows[0],), ())
    return _rowwise(cast, [x], [], [(D_MODEL, MXU_DTYPE)], [], name="x_cast")[0]


def _hooked(value, hook):
    token = None if hook is None else hook(value)
    return () if token is None else (token,)


def _mix_fwd(i, cur, cur_m, wl, hook=None, after=()):
    mixer = i % N_MIXERS
    if mixer == 0:
        y, s_mix = _sc_fwd(cur_m, wl["sc_w_in"], wl["sc_conv_w"], wl["sc_w_out"], f"l{i}_sc", after=after)
    elif mixer == 1:
        y, s_mix = _mla_fwd(cur_m, wl, f"l{i}_mla", after=after)
    else:
        y, s_mix = _cf_fwd(cur_m, wl, f"l{i}_cf", after=after)
    cur, cur_m, xh_mix, rs_mix = _ln_fwd(cur, y, wl["ln_mix_g"], wl["ln_mix_b"], name=f"l{i}_lnmix",
                                         after=_hooked(y, hook))
    return cur, cur_m, (s_mix, xh_mix, rs_mix)


def _ff_fwd(i, cur, cur_m, wl, hook=None, after=(), loss_target=None, act=None):
    y, s_ff = _mlp_fwd(cur_m, wl["ff_w1"], wl["ff_w2"], f"l{i}_ff", after=after, act=act)
    if loss_target is not None:
        xh_ff, rs_ff, dy, sq = _ln_fwd_loss(cur, y, wl["ln_ff_g"], wl["ln_ff_b"], loss_target, name=f"l{i}_lnff_loss",
                                            after=_hooked(y, hook))
        return (0.5 / D_MODEL) * jnp.sum(sq), dy, (s_ff, xh_ff, rs_ff)
    cur, cur_m, xh_ff, rs_ff = _ln_fwd(cur, y, wl["ln_ff_g"], wl["ln_ff_b"], name=f"l{i}_lnff", after=_hooked(y, hook))
    return cur, cur_m, (s_ff, xh_ff, rs_ff)


def _layer_fwd(i, cur, cur_m, wl):
    cur, cur_m, s_mix = _mix_fwd(i, cur, cur_m, wl)
    cur, cur_m, s_ff = _ff_fwd(i, cur, cur_m, wl)
    return cur, cur_m, (*s_mix, *s_ff)


def _loss_and_grad(out, target):
    d = D_MODEL

    def loss_fn(rows, bc):
        err = rows[0] - rows[1]
        return (err * (1.0 / d),), (_colsum(err * err),)

    dy, sq = _rowwise(loss_fn, [out, target], [], [(d, F32)], [(1, d)], name="loss")
    return (0.5 / d) * jnp.sum(sq), dy


def _ff_bwd(i, terms, saved, wl, hook=None, after=()):
    s_ff, xh_ff, rs_ff = saved
    grads = {}
    dz, dz_m, grads["ln_ff_g"], grads["ln_ff_b"], _ = _ln_bwd(terms, xh_ff, rs_ff, wl["ln_ff_g"], name=f"l{i}_dlnff",
                                                              after=after)
    dxs, g_ff = _mlp_bwd(dz_m, s_ff, wl["ff_w1"], wl["ff_w2"], f"l{i}_ff", dz, after=_hooked(dz, hook))
    grads.update(g_ff)
    return [(1.0, v) for v in dxs], grads


def _mix_bwd(i, terms, saved, wl, hook=None, after=()):
    mixer = i % N_MIXERS
    s_mix, xh_mix, rs_mix = saved
    grads = {}
    dz, dz_m, grads["ln_mix_g"], grads["ln_mix_b"], dz_sum = _ln_bwd(terms, xh_mix, rs_mix, wl["ln_mix_g"],
                                                                     name=f"l{i}_dlnmix", after=after)
    behind = _hooked(dz, hook)
    if mixer == 0:
        dxs, g_mix = _sc_bwd(dz_m, s_mix, wl["sc_w_in"], wl["sc_conv_w"], wl["sc_w_out"], f"l{i}_sc", dz, after=behind)
    elif mixer == 1:
        dxs, g_mix = _mla_bwd(dz_m, s_mix, wl, f"l{i}_mla", dz, after=behind)
    else:
        dxs, g_mix = _cf_bwd(dz_m, dz_sum, s_mix, wl, f"l{i}_cf", dz, after=behind)
    grads.update(g_mix)
    return [(1.0, v) for v in dxs], grads


def _layer_bwd(i, terms, saved, wl):
    terms, g_ff = _ff_bwd(i, terms, saved[3:], wl)
    terms, g_mix = _mix_bwd(i, terms, saved[:3], wl)
    return terms, {**g_ff, **g_mix}


def _sum_terms(terms, *, name):
    coefs = [c for c, _ in terms]

    def sum_fn(rows, bc):
        acc = None
        for c, v in zip(coefs, rows):
            tv = v if c == 1.0 else c * v
            acc = tv if acc is None else acc + tv
        return (acc,), ()

    return _rowwise(sum_fn, [v for _, v in terms], [], [(D_MODEL, F32)], [], name=name)[0]


def _local_step(x, target, layers):
    cur, cur_m = x, _cast_input(x)
    saved = []
    for i in range(DEPTH):
        cur, cur_m, s = _layer_fwd(i, cur, cur_m, layers[i])
        saved.append(s)
    loss, dy = _loss_and_grad(cur, target)
    terms = [(1.0, dy)]
    grads = [None] * DEPTH
    for i in reversed(range(DEPTH)):
        terms, grads[i] = _layer_bwd(i, terms, saved[i], layers[i])
    return loss, _sum_terms(terms, name="grad_x"), grads


MESH_ID = pl.DeviceIdType.MESH
ANY = pl.BlockSpec(memory_space=pl.ANY)


def _block_id(px, py, pc):
    return 4 * px + 2 * py + pc


COPIES_PER_GATHER = 7


def _all_gather(shards, *, name):
    n = len(shards)

    def body(*refs):
        x_refs, out_refs = refs[:n], refs[n:2 * n]
        send_sems, recv_sems, local_sems = refs[2 * n:]
        x, y, core = lax.axis_index("x"), lax.axis_index("y"), lax.axis_index("c")
        me, sibling = (x, y, core), (x, y, 1 - core)
        chips = [(1 - x, y), (x, 1 - y), (1 - x, 1 - y)]

        def copy(t, k, block, to, from_input=False):
            dst = out_refs[t].at[_block_id(*block)]
            return pltpu.make_async_remote_copy(
                src_ref=x_refs[t] if from_input else dst, dst_ref=dst,
                send_sem=send_sems.at[t * COPIES_PER_GATHER + k], recv_sem=recv_sems.at[t * COPIES_PER_GATHER + k],
                device_id=to, device_id_type=MESH_ID)

        started = []
        mine = [pltpu.make_async_copy(x_refs[t], out_refs[t].at[_block_id(*me)], local_sems.at[t]) for t in range(n)]
        for t in range(n):
            mine[t].start()
            first = [copy(t, 0, me, sibling, from_input=True)]
            first += [copy(t, 1 + j, me, (*chip, core), from_input=True) for j, chip in enumerate(chips)]
            for cp in first:
                cp.start()
            started += first
        for t in range(n):
            for j, chip in enumerate(chips):
                copy(t, 1 + j, (*chip, core), me).wait_recv()
                passed = copy(t, 4 + j, (*chip, core), sibling)
                passed.start()
                started.append(passed)
        for t in range(n):
            copy(t, 0, sibling, me).wait_recv()
            for j, chip in enumerate(chips):
                copy(t, 4 + j, (*chip, 1 - core), me).wait_recv()
        for cp in started:
            cp.wait_send()
        for cp in mine:
            cp.wait()

    n_sems = n * COPIES_PER_GATHER
    outs = pl.pallas_call(
        body,
        name=name,
        out_shape=[jax.ShapeDtypeStruct((N_DEV, *s.shape), s.dtype) for s in shards],
        in_specs=[ANY] * n,
        out_specs=[ANY] * n,
        scratch_shapes=[pltpu.SemaphoreType.DMA((n_sems,)), pltpu.SemaphoreType.DMA((n_sems,)),
                        pltpu.SemaphoreType.DMA((n,))],
    )(*shards)
    return list(outs)


def _swap_with_sibling(parts, *, name):
    n = len(parts)

    def body(*refs):
        p_refs, got_refs = refs[:n], refs[n:2 * n]
        send_sems, recv_sems = refs[2 * n:]
        x, y, core = lax.axis_index("x"), lax.axis_index("y"), lax.axis_index("c")
        chips = [(x, y), (1 - x, y), (x, 1 - y), (1 - x, 1 - y)]
        copies = []
        for t in range(n):
            for k, (px, py) in enumerate(chips):
                copies.append(pltpu.make_async_remote_copy(
                    src_ref=p_refs[t].at[_block_id(px, py, 1 - core)], dst_ref=got_refs[t].at[k],
                    send_sem=send_sems.at[4 * t + k], recv_sem=recv_sems.at[4 * t + k], device_id=(x, y, 1 - core),
                    device_id_type=MESH_ID))
        for cp in copies:
            cp.start()
        for cp in copies:
            cp.wait()

    outs = pl.pallas_call(
        body,
        name=name,
        out_shape=[jax.ShapeDtypeStruct((4, *p.shape[1:]), p.dtype) for p in parts],
        in_specs=[ANY] * n,
        out_specs=[ANY] * n,
        scratch_shapes=[pltpu.SemaphoreType.DMA((4 * n,)), pltpu.SemaphoreType.DMA((4 * n,))],
    )(*parts)
    return list(outs)


def _swap_between_chips(parts, *, name):
    n = len(parts)

    def body(*refs):
        p_refs, got_refs = refs[:n], refs[n:2 * n]
        send_sems, recv_sems = refs[2 * n:]
        x, y, core = lax.axis_index("x"), lax.axis_index("y"), lax.axis_index("c")
        chips = [(1 - x, y), (x, 1 - y), (1 - x, 1 - y)]
        copies = []
        for t in range(n):
            for j, (px, py) in enumerate(chips):
                copies.append(pltpu.make_async_remote_copy(
                    src_ref=p_refs[t].at[j], dst_ref=got_refs[t].at[j], send_sem=send_sems.at[3 * t + j],
                    recv_sem=recv_sems.at[3 * t + j], device_id=(px, py, core), device_id_type=MESH_ID))
        for cp in copies:
            cp.start()
        for cp in copies:
            cp.wait()

    outs = pl.pallas_call(
        body,
        name=name,
        out_shape=[jax.ShapeDtypeStruct(p.shape, p.dtype) for p in parts],
        in_specs=[ANY] * n,
        out_specs=[ANY] * n,
        scratch_shapes=[pltpu.SemaphoreType.DMA((3 * n,)), pltpu.SemaphoreType.DMA((3 * n,))],
    )(*parts)
    return list(outs)


def _chip_partials_group(parts, got, *, name):
    n = len(parts)
    rows = [p.shape[1] for p in parts]
    steps = 4 if all(r % (4 * BF16_ROWS) == 0 for r in rows) else (2 if all(r % (2 * BF16_ROWS) == 0 for r in rows) else 1)
    x, y, core = lax.axis_index("x"), lax.axis_index("y"), lax.axis_index("c")
    chips = [(x, y), (1 - x, y), (x, 1 - y), (1 - x, 1 - y)]
    ids = jnp.stack([_block_id(px, py, core) for px, py in chips]).astype(jnp.int32)

    def body(ids_ref, *refs):
        ins, outs = refs[:5 * n], refs[5 * n:]
        for t in range(n):
            p0, p1, p2, p3, g_ref = ins[5 * t:5 * t + 5]
            own_ref, out_ref = outs[2 * t], outs[2 * t + 1]
            own_ref[...] = p0[...].astype(F32) + g_ref[0].astype(F32)
            for j, p in enumerate((p1, p2, p3)):
                out_ref[j] = (p[...].astype(F32) + g_ref[j + 1].astype(F32)).astype(out_ref.dtype)

    in_specs, out_specs, out_shape, operands = [], [], [], []
    for p, g in zip(parts, got):
        _, r, c = p.shape
        tb = r // steps
        in_specs += [pl.BlockSpec((None, tb, c), lambda i, ids_ref, k=k: (ids_ref[k], i, 0)) for k in range(4)]
        in_specs.append(pl.BlockSpec((4, tb, c), lambda i, ids_ref: (0, i, 0)))
        out_specs += [pl.BlockSpec((tb, c), lambda i, ids_ref: (i, 0)), pl.BlockSpec((3, tb, c), lambda i, ids_ref: (0, i, 0))]
        out_shape += [jax.ShapeDtypeStruct((r, c), F32), jax.ShapeDtypeStruct((3, r, c), p.dtype)]
        operands += [p, p, p, p, g]
    outs = pl.pallas_call(
        body,
        name=name,
        grid_spec=pltpu.PrefetchScalarGridSpec(num_scalar_prefetch=1, grid=(steps,), in_specs=in_specs,
                                               out_specs=out_specs),
        out_shape=out_shape,
        compiler_params=_params(("parallel",)),
    )(ids, *operands)
    return [(outs[2 * t], outs[2 * t + 1]) for t in range(n)]


def _sum_blocks(first, blocks, *, name):
    n, r, c = blocks.shape
    tb = _row_tile(r, 512)

    def body(*refs):
        b_ref, o_ref = refs[-2], refs[-1]
        acc = refs[0][...] if first is not None else b_ref[0].astype(F32)
        for j in range(0 if first is not None else 1, n):
            acc = acc + b_ref[j].astype(F32)
        o_ref[...] = acc

    row_spec = pl.BlockSpec((tb, c), lambda i: (i, 0))
    return pl.pallas_call(
        body,
        name=name,
        grid=(r // tb,),
        in_specs=([row_spec] if first is not None else []) + [pl.BlockSpec((n, tb, c), lambda i: (0, i, 0))],
        out_specs=row_spec,
        out_shape=jax.ShapeDtypeStruct((r, c), F32),
        compiler_params=_params(("parallel",)),
    )(*([first] if first is not None else []), blocks)


HBM = pl.BlockSpec(memory_space=pltpu.HBM)
SEM = pl.BlockSpec(memory_space=pltpu.SEMAPHORE)
DATAFLOW = pltpu.SideEffectType.DATAFLOW_SIDE_EFFECTING


def _split_call(name, arrays, sems, n_new, body, after=None):
    n_a, n_s = len(arrays), len(sems)
    after = [] if after is None else (list(after) if isinstance(after, (list, tuple)) else [after])
    n_after = len(after)

    def kernel_body(*refs):
        new = refs[n_a + n_s + n_after:n_a + n_s + n_after + 2] if n_new else (None, None)
        body(refs[:n_a], refs[n_a:n_a + n_s], *new)
        refs[-1][...] = jnp.zeros_like(refs[-1])

    n_sem_out = 2 if n_new else 0
    out_shape = [pltpu.SemaphoreType.DMA((n_new,))] * n_sem_out
    out_shape += [pltpu.HBM(a.shape, a.dtype) for a in arrays]
    out_shape += [jax.ShapeDtypeStruct((SUBLANES, LANES), F32)]
    outs = pl.pallas_call(
        kernel_body,
        name=name,
        out_shape=out_shape,
        in_specs=[HBM] * n_a + [SEM] * n_s + [ANY] * n_after,
        out_specs=[SEM] * n_sem_out + [HBM] * n_a + [pl.BlockSpec(memory_space=pltpu.VMEM)],
        input_output_aliases={i: n_sem_out + i for i in range(n_a)},
        compiler_params=pltpu.CompilerParams(has_side_effects=DATAFLOW),
    )(*[pltpu.with_memory_space_constraint(a, pltpu.HBM) for a in arrays], *sems, *after)
    return tuple(outs[:n_sem_out]), list(outs[n_sem_out:n_sem_out + n_a]), outs[-1]


def _remote(src, dst, send, recv, k, to):
    return pltpu.make_async_remote_copy(src_ref=src, dst_ref=dst, send_sem=send.at[k], recv_sem=recv.at[k],
                                        device_id=to, device_id_type=MESH_ID)


def _land_block(land_ref, block, by_cols, width):
    if by_cols:
        return land_ref.at[:, pl.ds(pl.multiple_of(block * width, LANES), width)]
    return land_ref.at[block]


def _place_in_window(stacked, layer, *, name):
    _, r, c = stacked.shape
    tb = _row_tile(r, 512)
    me = _block_id(lax.axis_index("x"), lax.axis_index("y"), lax.axis_index("c")).astype(jnp.int32).reshape(1)

    def body(me_ref, s_ref, o_ref, token_ref):
        o_ref[...] = s_ref[...].astype(o_ref.dtype)
        token_ref[...] = jnp.zeros_like(token_ref)

    return pl.pallas_call(
        body,
        name=name,
        grid_spec=pltpu.PrefetchScalarGridSpec(
            num_scalar_prefetch=1, grid=(r // tb,),
            in_specs=[pl.BlockSpec((None, tb, c), lambda i, me_ref: (layer, i, 0))],
            out_specs=[pl.BlockSpec((tb, c), lambda i, me_ref: (i, me_ref[0])),
                       pl.BlockSpec((SUBLANES, LANES), lambda i, me_ref: (0, 0))]),
        out_shape=[jax.ShapeDtypeStruct((r, N_DEV * c), MXU_DTYPE), jax.ShapeDtypeStruct((SUBLANES, LANES), F32)],
        compiler_params=_params(("arbitrary",)),
    )(me, stacked)


def _gather_prepare(shards, by_cols, *, name):
    me = _block_id(lax.axis_index("x"), lax.axis_index("y"), lax.axis_index("c"))
    lands, early, widths = [], [], []
    for i, (s, cols) in enumerate(zip(shards, by_cols)):
        if cols:
            land, token = _place_in_window(*s, name=f"{name}_place{i}")
            lands.append(land)
            early.append(token)
            widths.append(s[0].shape[-1])
        else:
            lands.append(lax.dynamic_update_index_in_dim(lax.empty((N_DEV, *s.shape), s.dtype), s, me, 0))
            early.append(s)
            widths.append(s.shape[-1])
    return lands, widths, list(by_cols), early


def _gather_start(shards, *, name, after=None, by_cols=None, prepared=None):
    if prepared is None:
        prepared = _gather_prepare(shards, by_cols if by_cols is not None else [False] * len(shards), name=name)
    lands, widths, by_cols, _ = prepared
    n = len(lands)

    def body(refs, _, send, recv):
        x, y, core = lax.axis_index("x"), lax.axis_index("y"), lax.axis_index("c")
        targets = [(x, y, 1 - core), (1 - x, y, core), (x, 1 - y, core), (1 - x, 1 - y, core)]
        for t in range(n):
            mine = _land_block(refs[t], _block_id(x, y, core), by_cols[t], widths[t])
            for k, to in enumerate(targets):
                _remote(mine, mine, send, recv, 4 * t + k, to).start()

    sems, thru, token = _split_call(name, lands, [], 4 * n, body, after=after)
    return {"n": n, "sems1": sems, "arrays": thru, "token": token, "by_cols": by_cols, "widths": widths}


def _gather_pass_on(st, *, name, after):
    n = st["n"]

    def body(refs, sems, send, recv):
        send1, recv1 = sems
        x, y, core = lax.axis_index("x"), lax.axis_index("y"), lax.axis_index("c")
        chips = [(1 - x, y), (x, 1 - y), (1 - x, 1 - y)]
        for t in range(n):
            for j, (px, py) in enumerate(chips):
                block = _land_block(refs[t], _block_id(px, py, core), st["by_cols"][t], st["widths"][t])
                _remote(block, block, send1, recv1, 4 * t + 1 + j, (x, y, 1 - core)).wait_recv()
                _remote(block, block, send, recv, 3 * t + j, (x, y, 1 - core)).start()

    sems2, thru, token = _split_call(name, st["arrays"], list(st["sems1"]), 3 * n, body, after=after)
    return {**st, "sems2": sems2, "arrays": thru, "token": token}


def _gather_finish(st, *, name, after, then_start=None):
    n = st["n"]
    n2 = 0 if then_start is None else len(then_start[0])

    def body(refs, sems, new_send=None, new_recv=None):
        send1, recv1, send2, recv2 = sems
        x, y, core = lax.axis_index("x"), lax.axis_index("y"), lax.axis_index("c")
        sibling = (x, y, 1 - core)
        chips = [(1 - x, y), (x, 1 - y), (1 - x, 1 - y)]
        for t in range(n):
            place = lambda px, py, pc, t=t: _land_block(refs[t], _block_id(px, py, pc), st["by_cols"][t],
                                                        st["widths"][t])
            mine = place(x, y, core)
            for k in range(4):
                _remote(mine, mine, send1, recv1, 4 * t + k, sibling).wait_send()
            _remote(mine, place(x, y, 1 - core), send1, recv1, 4 * t, sibling).wait_recv()
            for j, (px, py) in enumerate(chips):
                _remote(place(px, py, core), place(px, py, core), send2, recv2, 3 * t + j, sibling).wait_send()
                _remote(mine, place(px, py, 1 - core), send2, recv2, 3 * t + j, sibling).wait_recv()
        if n2:
            _, widths, by_cols, _ = then_start
            targets = [sibling, (1 - x, y, core), (x, 1 - y, core), (1 - x, 1 - y, core)]
            for t in range(n2):
                mine = _land_block(refs[n + t], _block_id(x, y, core), by_cols[t], widths[t])
                for k, to in enumerate(targets):
                    _remote(mine, mine, new_send, new_recv, 4 * t + k, to).start()

    arrays = st["arrays"] + (list(then_start[0]) if n2 else [])
    sems, thru, token = _split_call(name, arrays, [*st["sems1"], *st["sems2"]], 4 * n2, body, after=after)
    if not n2:
        return thru
    _, widths, by_cols, _ = then_start
    return thru[:n], {"n": n2, "sems1": sems, "arrays": thru[n:], "token": token, "by_cols": by_cols, "widths": widths}


def _swap_start(sources, lands, plan, per_array, *, name):
    n = len(sources)

    def body(refs, _, send, recv):
        x, y, core = lax.axis_index("x"), lax.axis_index("y"), lax.axis_index("c")
        for t in range(n):
            for k in range(per_array):
                block, to = plan(x, y, core, k)
                _remote(refs[t].at[block], refs[n + t].at[k], send, recv, per_array * t + k, to).start()

    sems, thru, token = _split_call(name, sources + lands, [], per_array * n, body)
    return {"n": n, "sems": sems, "arrays": thru, "token": token, "plan": plan, "per_array": per_array}


def _swap_finish(states, *, name, after):
    many = isinstance(states, (list, tuple))
    states = list(states) if many else [states]

    def body(refs, sems, *_):
        x, y, core = lax.axis_index("x"), lax.axis_index("y"), lax.axis_index("c")
        first = 0
        for j, st in enumerate(states):
            n, plan, per_array = st["n"], st["plan"], st["per_array"]
            send, recv = sems[2 * j], sems[2 * j + 1]
            for t in range(n):
                for k in range(per_array):
                    block, to = plan(x, y, core, k)
                    cp = _remote(refs[first + t].at[block], refs[first + n + t].at[k], send, recv, per_array * t + k, to)
                    cp.wait_send()
                    cp.wait_recv()
            first += 2 * n

    arrays = [a for st in states for a in st["arrays"]]
    _, thru, _ = _split_call(name, arrays, [s for st in states for s in st["sems"]], 0, body, after=after)
    out, first = [], 0
    for st in states:
        n = st["n"]
        out.append((thru[first:first + n], thru[first + n:first + 2 * n]))
        first += 2 * n
    return out if many else out[0]


def _to_sibling(x, y, core, k):
    px, py = [(x, y), (1 - x, y), (x, 1 - y), (1 - x, 1 - y)][k]
    return _block_id(px, py, 1 - core), (x, y, 1 - core)


def _to_chip(x, y, core, k):
    px, py = [(1 - x, y), (x, 1 - y), (1 - x, 1 - y)][k]
    return k, (px, py, core)


PARAMS = {
    "sc_w_in": ((2, 1024, 3072), 2), "sc_conv_w": ((2, 3, 1024), 2), "sc_w_out": ((2, 1024, 1024), 1),
    "mla_w_dq": ((1, 1024, 384), 1), "mla_g_q": ((1, 384), None), "mla_w_uq": ((1, 384, 1536), 2),
    "mla_w_dkv": ((1, 1024, 320), 1), "mla_g_kv": ((1, 256), None), "mla_w_uk": ((1, 256, 8, 128), 1),
    "mla_w_uv": ((1, 256, 8, 128), 1), "mla_w_o": ((1, 1024, 1024), 1), "cf_w_pw1": ((1, 1024, 2048), 2),
    "cf_b_pw1": ((1, 2048), 1), "cf_dw_w": ((1, 31, 1024), 2), "cf_dw_b": ((1, 1024), 1), "cf_norm_g": ((1, 1024), 1),
    "cf_norm_b": ((1, 1024), 1), "cf_w_pw2": ((1, 1024, 1024), 1), "cf_b_pw2": ((1, 1024), 1),
    "ff_w1": ((4, 1024, 4096), 2), "ff_w2": ((4, 4096, 1024), 1), "ln_mix_g": ((4, 1024), None),
    "ln_mix_b": ((4, 1024), None), "ln_ff_g": ((4, 1024), None), "ln_ff_b": ((4, 1024), None),
}
NAMES = list(PARAMS)
BIG = ["sc_w_in", "sc_w_out", "mla_w_dq", "mla_w_uq", "mla_w_dkv", "mla_w_uk", "mla_w_uv", "mla_w_o", "cf_w_pw1",
       "cf_w_pw2", "ff_w1", "ff_w2"]
SMALL = [n for n in NAMES if n not in BIG]
SMALL_SHARDED = [n for n in SMALL if PARAMS[n][1] is not None]
SMALL_WIDTH = LANES
DEV_BLOCKED = ("sc_w_in", "cf_w_pw1", "ff_w1")
MIXER_BIG = {0: ["sc_w_in", "sc_w_out"], 1: ["mla_w_dq", "mla_w_uq", "mla_w_dkv", "mla_w_uk", "mla_w_uv", "mla_w_o"],
             2: ["cf_w_pw1", "cf_w_pw2"]}


def _layer_big(i):
    return [(n, i // N_MIXERS) for n in MIXER_BIG[i % N_MIXERS]] + [("ff_w1", i), ("ff_w2", i)]


def _shard_2d(name, a):
    return a.reshape(a.shape[0], -1)


def _shard_for_gather(name, stacked, layer):
    if name in DEV_BLOCKED:
        return stacked, layer
    return _shard_2d(name, stacked[layer]).astype(MXU_DTYPE)


def _as_operand(name, g):
    if name in DEV_BLOCKED:
        return g
    if name == "mla_w_uq":
        by_head = g.transpose(1, 0, 2)
        pieces = [by_head[:, :, :QK_NOPE], by_head[:, :, QK_NOPE:]]
        return jnp.concatenate([p.reshape(g.shape[1], -1) for p in pieces], axis=1)
    return g.reshape(-1, g.shape[2])


def _as_parts(name, grad):
    if name in DEV_BLOCKED:
        return grad
    if name == "mla_w_uq":
        rows = grad.shape[0]
        pieces = [grad[:, :Q_NOPE_COLS].reshape(rows, MLA_HEADS, QK_NOPE),
                  grad[:, Q_NOPE_COLS:].reshape(rows, MLA_HEADS, QK_ROPE)]
        return jnp.concatenate(pieces, axis=2).transpose(1, 0, 2)
    return grad.reshape(N_DEV, -1, grad.shape[1])


def _shard_shape(name):
    shape, ax = PARAMS[name]
    if ax is None:
        return shape
    return tuple(s // N_DEV if i == ax else s for i, s in enumerate(shape))


def _rows_of(shape, width, align):
    n = 1
    for s in shape:
        n *= s
    rows = -(-n // width)
    return -(-rows // align) * align


def _pack_rows(arrays, width, align, dtype, lead=0):
    segs = []
    for a in arrays:
        batch = a.shape[:lead]
        flat = a.astype(dtype).reshape(*batch, -1)
        rows = _rows_of(a.shape[lead:], width, align)
        flat = jnp.pad(flat, [(0, 0)] * lead + [(0, rows * width - flat.shape[-1])])
        segs.append(flat.reshape(*batch, rows, width))
    return jnp.concatenate(segs, axis=lead)


def _unpack_rows(buf, shapes, width, align):
    out, off = [], 0
    lead = buf.shape[:-2]
    for shape in shapes:
        n = 1
        for s in shape:
            n *= s
        rows = _rows_of(shape, width, align)
        seg = buf[..., off:off + rows, :].reshape(*lead, rows * width)[..., :n]
        out.append(seg.reshape(*lead, *shape))
        off += rows
    return out


def _merge_shards(stacked, ax):
    moved = jnp.moveaxis(stacked, 0, ax)
    shape = moved.shape
    return moved.reshape(*shape[:ax], shape[ax] * shape[ax + 1], *shape[ax + 2:])


def _split_shards(full, ax):
    shape = full.shape
    split = full.reshape(*shape[:ax], N_DEV, shape[ax] // N_DEV, *shape[ax + 1:])
    return jnp.moveaxis(split, ax, 0)


def _adam_math(wv, gv, mv, vv):
    m_new = ADAM_B1 * mv + (1.0 - ADAM_B1) * gv
    v_new = ADAM_B2 * vv + (1.0 - ADAM_B2) * (gv * gv)
    m_hat = m_new / (1.0 - ADAM_B1 ** ADAM_STEP)
    v_hat = v_new / (1.0 - ADAM_B2 ** ADAM_STEP)
    delta = -ADAM_LR * (m_hat / (jnp.sqrt(v_hat) + ADAM_EPS) + ADAM_WD * wv)
    return delta, m_new, v_new


def _adamw(w, g, m, v, *, name):
    shape = w.shape
    c = shape[-1]

    def fn(rows, bc):
        return _adam_math(*rows), ()

    flat = [a.reshape(-1, c) for a in (w, g, m, v)]
    outs = _rowwise(fn, flat, [], [(c, F32)] * 3, [], name=name, tb_cap=512)
    return tuple(o.reshape(shape) for o in outs)


def _adamw_reduced(w, m, v, layer_grads, *, name, after=()):
    n_l, r, c = w.shape
    tb = _row_tile(r, 256 if c > 512 else 512)
    nb = r // tb

    def body(*refs):
        w_ref, m_ref, v_ref = refs[:3]
        g_refs = refs[3:3 + 2 * n_l]
        g_out, d_out, m_out, v_out = refs[3 + 2 * n_l + len(after):]
        layer = pl.program_id(0)
        for l in range(n_l):
            @pl.when(layer == l)
            def _(l=l):
                own, arr = g_refs[2 * l], g_refs[2 * l + 1]
                g = own[...] + arr[0].astype(F32)
                g = g + arr[1].astype(F32)
                g = g + arr[2].astype(F32)
                delta, m_new, v_new = _adam_math(w_ref[...], g, m_ref[...], v_ref[...])
                g_out[...] = g
                d_out[...] = delta
                m_out[...] = m_new
                v_out[...] = v_new

    stacked = pl.BlockSpec((tb, c), lambda l, i: (l * nb + i, 0))
    in_specs = [stacked] * 3
    operands = [a.reshape(n_l * r, c) for a in (w, m, v)]
    for l, (own, arr) in enumerate(layer_grads):
        in_specs.append(pl.BlockSpec((tb, c), lambda ll, i, l=l: (jnp.where(ll == l, i, 0), 0)))
        in_specs.append(pl.BlockSpec((3, tb, c), lambda ll, i, l=l: (0, jnp.where(ll == l, i, 0), 0)))
        operands += [own, arr]
    in_specs += [ANY] * len(after)
    operands += list(after)
    outs = pl.pallas_call(
        body,
        name=name,
        grid=(n_l, nb),
        in_specs=in_specs,
        out_specs=[stacked] * 4,
        out_shape=[jax.ShapeDtypeStruct((n_l * r, c), F32)] * 4,
        compiler_params=_params(("arbitrary", "arbitrary")),
    )(*operands)
    return tuple(o.reshape(n_l, r, c) for o in outs)


def _adamw_reduced_group(ws, ms, vs, grads, *, name, after=()):
    n = len(ws)

    def body(*refs):
        ins, outs = refs[:5 * n], refs[5 * n + len(after):]
        for i in range(n):
            w_ref, m_ref, v_ref, own, arr = ins[5 * i:5 * i + 5]
            g = own[...] + arr[0].astype(F32)
            g = g + arr[1].astype(F32)
            g = g + arr[2].astype(F32)
            delta, m_new, v_new = _adam_math(w_ref[...], g, m_ref[...], v_ref[...])
            for o_ref, val in zip(outs[4 * i:4 * i + 4], (g, delta, m_new, v_new)):
                o_ref[...] = val

    whole = lambda a: pl.BlockSpec(a.shape, lambda i, nd=a.ndim: (0,) * nd)
    operands, in_specs, out_specs, out_shape = [], [], [], []
    for w, m, v, (own, arr) in zip(ws, ms, vs, grads):
        operands += [w, m, v, own, arr]
        in_specs += [whole(a) for a in (w, m, v, own, arr)]
        out_specs += [whole(w)] * 4
        out_shape += [jax.ShapeDtypeStruct(w.shape, F32)] * 4
    outs = pl.pallas_call(
        body,
        name=name,
        grid=(1,),
        in_specs=in_specs + [ANY] * len(after),
        out_specs=out_specs,
        out_shape=out_shape,
        compiler_params=_params(("arbitrary",)),
    )(*operands, *after)
    return [tuple(outs[4 * i:4 * i + 4]) for i in range(n)]


def _train_step(x, target, weights, m_state, v_state):
    me = _block_id(lax.axis_index("x"), lax.axis_index("y"), lax.axis_index("c"))

    groups = [(kind, i) for i in range(DEPTH) for kind in ("mix", "ff")]
    forward = [groups[0], ("ff_up", 0), ("ff_down", 0)] + groups[2:]

    def group_names(g):
        kind, i = g
        if kind == "mix":
            return [(n, i // N_MIXERS) for n in MIXER_BIG[i % N_MIXERS]]
        return [(n, i) for n in {"ff": ["ff_w1", "ff_w2"], "ff_up": ["ff_w1"], "ff_down": ["ff_w2"]}[kind]]

    def shards_of(g):
        return [_shard_for_gather(n, weights[n], j) for n, j in group_names(g)]

    whole = lambda g: [n in DEV_BLOCKED for n, _ in group_names(g)]

    small_shapes = [_shard_shape(n) for n in SMALL_SHARDED]
    gather = {0: _gather_start([_pack_rows([weights[n] for n in SMALL_SHARDED], SMALL_WIDTH, SUBLANES, F32)]
                               + shards_of(groups[0]), name="gather_g0_start", by_cols=[False] + whole(groups[0]))}
    cur, cur_m = x[0], x[0]
    prepared = {k: _gather_prepare(shards_of(forward[k]), whole(forward[k]), name=f"gather_g{k}_start")
                for k in range(1, len(forward))}
    early = [a for k in prepared for a in prepared[k][3]]
    gather[0] = _gather_pass_on(gather[0], name="gather_g0_pass", after=early)
    first = _gather_finish(gather[0], name="gather_g0_finish", after=gather[0]["token"])
    small_all, first_matrices = first[0], first[1:]
    small_full = {n: weights[n] for n in SMALL if PARAMS[n][1] is None}
    for n, stacked in zip(SMALL_SHARDED, _unpack_rows(small_all, small_shapes, SMALL_WIDTH, SUBLANES)):
        small_full[n] = _merge_shards(stacked, PARAMS[n][1])

    def small_weights(i):
        wl = {}
        for n in SMALL:
            if n.startswith("ln_"):
                wl[n] = small_full[n][i:i + 1]
            elif n.startswith(("sc_", "mla_", "cf_")[i % N_MIXERS]):
                v = small_full[n][i // N_MIXERS]
                wl[n] = v if v.ndim == 2 else v.reshape(1, -1)
        return wl

    layers = [small_weights(i) for i in range(DEPTH)]

    def install(g, gathered):
        for (n, _), a in zip(group_names(g), gathered):
            layers[g[1]][n] = _as_operand(n, a)

    install(forward[0], first_matrices)
    gather[1] = _gather_start(None, name="gather_g1_start", after=gather[0]["token"], prepared=prepared[1])
    gather[2] = _gather_start(None, name="gather_g2_start", after=gather[1]["token"], prepared=prepared[2])
    saved = {}
    hidden = None
    for k, g in enumerate(forward):
        kind, i = g
        tokens = [gather[1]["token"]] if k == 0 else []
        if k + 2 < len(forward):
            tokens.append(gather[k + 2]["token"])

        def pass_on(y, k=k):
            if k + 1 == len(forward):
                return None
            gather[k + 1] = _gather_pass_on(gather[k + 1], name=f"gather_g{k + 1}_pass", after=y)
            return gather[k + 1]["token"]

        if k + 1 == len(forward):
            loss_local, dy, saved[g] = _ff_fwd(i, cur, cur_m, layers[i], after=tokens, loss_target=target[0])
            break
        if kind == "ff_up":
            hidden = _mlp_up(cur_m, layers[i]["ff_w1"], f"l{i}_ff", after=tokens)
            behind, done = _hooked(hidden, pass_on), hidden
        elif kind == "ff_down":
            cur, cur_m, saved[("ff", i)] = _ff_fwd(i, cur, cur_m, layers[i], hook=pass_on, after=[*tokens, *behind],
                                                   act=hidden)
            done = cur
        else:
            phase = _mix_fwd if kind == "mix" else _ff_fwd
            cur, cur_m, saved[g] = phase(i, cur, cur_m, layers[i], hook=pass_on, after=tokens)
            done = cur
        if k + 3 < len(forward):
            gathered, gather[k + 3] = _gather_finish(gather[k + 1], name=f"gather_g{k + 1}_finish", after=done,
                                                     then_start=prepared[k + 3])
        else:
            gathered = _gather_finish(gather[k + 1], name=f"gather_g{k + 1}_finish", after=done)
        install(forward[k + 1], gathered)

    loss = lax.psum(loss_local, MESH_AXES)

    reduced = {n: [None] * PARAMS[n][0][0] for n in BIG}
    small_grads = {n: [None] * PARAMS[n][0][0] for n in SMALL}
    reduce = {}

    def to_chips(p, after):
        st = reduce[p]
        parts, got = _swap_finish(st["sibling"], name=f"reduce_p{p}_sibling_finish", after=after)
        sums = _chip_partials_group(parts, got, name=f"reduce_p{p}_chipsum")
        st["own"] = [own for own, _ in sums]
        others = [o for _, o in sums]
        st["chips"] = _swap_start(others, [lax.empty(o.shape, o.dtype) for o in others], _to_chip, 3,
                                  name=f"reduce_p{p}_chips_start")
        return st["chips"]["token"]

    def arrived(phases, after):
        done = _swap_finish([reduce[p]["chips"] for p in phases], name=f"reduce_p{phases[0]}_chips_finish", after=after)
        for p, (_, got) in zip(phases, done):
            for (n, j), own, arr in zip(reduce[p]["names"], reduce[p]["own"], got):
                reduced[n][j] = (own, arr)

    terms = [(1.0, dy)]
    back = list(reversed(groups))
    for p, g in enumerate(back):
        kind, i = g

        def after_first_kernel(dz, p=p):
            return to_chips(p - 1, dz) if p >= 1 else None

        phase = _mix_bwd if kind == "mix" else _ff_bwd
        tokens = [reduce[p - 1]["sibling"]["token"]] if p >= 1 else []
        terms, grads = phase(i, terms, saved[g], layers[i], hook=after_first_kernel, after=tokens)
        for n in SMALL:
            if n in grads:
                small_grads[n][i if n.startswith("ln_") else i // N_MIXERS] = grads[n]
        names = group_names(g)
        parts = [_as_parts(n, grads[n]) for n, _ in names]
        reduce[p] = {"names": names, "sibling": _swap_start(
            parts, [lax.empty((4, *a.shape[1:]), a.dtype) for a in parts], _to_sibling, 4,
            name=f"reduce_p{p}_sibling_start")}
    (_, grad_x), = terms
    last = len(back) - 1
    small_mine = _pack_rows([jnp.stack(small_grads[n]).reshape(PARAMS[n][0]) for n in SMALL], SMALL_WIDTH, SUBLANES, F32)
    small_gather = _gather_start([small_mine], name="gather_small_grads_start")
    tail = to_chips(last, small_gather["token"])
    arrived(list(range(last)), tail)

    grad, delta, new_m, new_v = {}, {}, {}, {}

    def adamw_matrix(n, after):
        shape = weights[n].shape
        view = lambda a: a.reshape(shape[0], shape[1], -1)
        outs = _adamw_reduced(view(weights[n]), view(m_state[n]), view(v_state[n]), reduced[n], name=f"adamw_{n}",
                              after=after)
        grad[n], delta[n], new_m[n], new_v[n] = [o.reshape(shape) for o in outs]
        return outs[0]

    done = tail
    single = [n for n in BIG if PARAMS[n][0][0] == 1 and n != "cf_w_pw1"]
    flat = lambda a: a.reshape(a.shape[1], -1)
    outs = _adamw_reduced_group([flat(weights[n]) for n in single], [flat(m_state[n]) for n in single],
                                [flat(v_state[n]) for n in single], [reduced[n][0] for n in single],
                                name="adamw_single_layer", after=[done])
    for n, quad in zip(single, outs):
        grad[n], delta[n], new_m[n], new_v[n] = [o.reshape(weights[n].shape) for o in quad]
    done = outs[-1][0]
    for n in BIG:
        if n not in MIXER_BIG[0] and n not in single:
            done = adamw_matrix(n, [done])
    small_gather = _gather_pass_on(small_gather, name="gather_small_grads_pass", after=done)
    small_all = _gather_finish(small_gather, name="gather_small_grads_finish", after=small_gather["token"])[0]
    small_sum = _sum_blocks(None, small_all, name="sum_small_grads")
    for n, g in zip(SMALL, _unpack_rows(small_sum, [PARAMS[n][0] for n in SMALL], SMALL_WIDTH, SUBLANES)):
        ax = PARAMS[n][1]
        grad[n] = g if ax is None else lax.dynamic_index_in_dim(_split_shards(g, ax), me, axis=0, keepdims=False)
    small_shards = [_shard_shape(n) for n in SMALL]
    packed = [_pack_rows([src[n] for n in SMALL], SMALL_WIDTH, SUBLANES, F32) for src in (weights, grad, m_state, v_state)]
    small_out = _adamw(*packed, name="adamw_small")
    for outs, dst in zip(small_out, (delta, new_m, new_v)):
        for n, a in zip(SMALL, _unpack_rows(outs, small_shards, SMALL_WIDTH, SUBLANES)):
            dst[n] = a
    arrived([last], small_sum)
    for n in MIXER_BIG[0]:
        adamw_matrix(n, [])
    return (loss, grad_x[None], *[grad[n] for n in NAMES], *[delta[n] for n in NAMES],
            *[new_m[n] for n in NAMES], *[new_v[n] for n in NAMES])


def kernel(x, sc_w_in, sc_conv_w, sc_w_out, mla_w_dq, mla_g_q, mla_w_uq, mla_w_dkv, mla_g_kv, mla_w_uk, mla_w_uv, mla_w_o, cf_w_pw1, cf_b_pw1, cf_dw_w, cf_dw_b, cf_norm_g, cf_norm_b, cf_w_pw2, cf_b_pw2, ff_w1, ff_w2, ln_mix_g, ln_mix_b, ln_ff_g, ln_ff_b, loss_target, m_sc_w_in, m_sc_conv_w, m_sc_w_out, m_mla_w_dq, m_mla_g_q, m_mla_w_uq, m_mla_w_dkv, m_mla_g_kv, m_mla_w_uk, m_mla_w_uv, m_mla_w_o, m_cf_w_pw1, m_cf_b_pw1, m_cf_dw_w, m_cf_dw_b, m_cf_norm_g, m_cf_norm_b, m_cf_w_pw2, m_cf_b_pw2, m_ff_w1, m_ff_w2, m_ln_mix_g, m_ln_mix_b, m_ln_ff_g, m_ln_ff_b, v_sc_w_in, v_sc_conv_w, v_sc_w_out, v_mla_w_dq, v_mla_g_q, v_mla_w_uq, v_mla_w_dkv, v_mla_g_kv, v_mla_w_uk, v_mla_w_uv, v_mla_w_o, v_cf_w_pw1, v_cf_b_pw1, v_cf_dw_w, v_cf_dw_b, v_cf_norm_g, v_cf_norm_b, v_cf_w_pw2, v_cf_b_pw2, v_ff_w1, v_ff_w2, v_ln_mix_g, v_ln_mix_b, v_ln_ff_g, v_ln_ff_b):
    w_list = (sc_w_in, sc_conv_w, sc_w_out, mla_w_dq, mla_g_q, mla_w_uq, mla_w_dkv, mla_g_kv, mla_w_uk, mla_w_uv, mla_w_o,
              cf_w_pw1, cf_b_pw1, cf_dw_w, cf_dw_b, cf_norm_g, cf_norm_b, cf_w_pw2, cf_b_pw2, ff_w1, ff_w2, ln_mix_g,
              ln_mix_b, ln_ff_g, ln_ff_b)
    m_list = (m_sc_w_in, m_sc_conv_w, m_sc_w_out, m_mla_w_dq, m_mla_g_q, m_mla_w_uq, m_mla_w_dkv, m_mla_g_kv, m_mla_w_uk,
              m_mla_w_uv, m_mla_w_o, m_cf_w_pw1, m_cf_b_pw1, m_cf_dw_w, m_cf_dw_b, m_cf_norm_g, m_cf_norm_b, m_cf_w_pw2,
              m_cf_b_pw2, m_ff_w1, m_ff_w2, m_ln_mix_g, m_ln_mix_b, m_ln_ff_g, m_ln_ff_b)
    v_list = (v_sc_w_in, v_sc_conv_w, v_sc_w_out, v_mla_w_dq, v_mla_g_q, v_mla_w_uq, v_mla_w_dkv, v_mla_g_kv, v_mla_w_uk,
              v_mla_w_uv, v_mla_w_o, v_cf_w_pw1, v_cf_b_pw1, v_cf_dw_w, v_cf_dw_b, v_cf_norm_g, v_cf_norm_b, v_cf_w_pw2,
              v_cf_b_pw2, v_ff_w1, v_ff_w2, v_ln_mix_g, v_ln_mix_b, v_ln_ff_g, v_ln_ff_b)
    return _train_step(x, loss_target, dict(zip(NAMES, w_list)), dict(zip(NAMES, m_list)), dict(zip(NAMES, v_list)))
```

```python
import jax
import jax.numpy as jnp
from jax import lax
from jax.experimental import pallas as pl
from jax.experimental.pallas import tpu as pltpu

F32 = jnp.float32
MXU_DTYPE = jnp.bfloat16

N_DEV = 8
D_MODEL = 1024
DEPTH = 4
N_MIXERS = 3
CHUNK = 64
ALPHA = (2.0 * DEPTH) ** 0.25
LN_EPS = 1e-5
RMS_EPS = 1e-6
MLA_HEADS = 8
QK_NOPE = 128
QK_ROPE = 64
V_HEAD = 128
Q_LORA = 384
KV_LORA = 256
ROPE_THETA = 10000.0
ADAM_LR = 0.001
ADAM_B1 = 0.9
ADAM_B2 = 0.999
ADAM_EPS = 1e-08
ADAM_WD = 0.01
ADAM_STEP = 10

LANES = 128
SUBLANES = 8
BF16_ROWS = 16
VMEM_LIMIT = 56 * 1024 * 1024
MM_VMEM_BUDGET = 36 * 1024 * 1024
CONV_PAD = 32
MASK_VALUE = -1e30

MESH_AXES = ("x", "y", "c")


def _params(semantics):
    return pltpu.CompilerParams(dimension_semantics=semantics, vmem_limit_bytes=VMEM_LIMIT)


def _tile(n, cap):
    if n <= cap:
        return n
    t = cap - cap % LANES
    while t >= LANES:
        if n % t == 0:
            return t
        t -= LANES
    raise ValueError(f"no tile for {n} under {cap}")


def _row_tile(rows, cap):
    if rows <= cap:
        return rows
    t = cap - cap % SUBLANES
    while t >= SUBLANES:
        if rows % t == 0:
            return t
        t -= SUBLANES
    raise ValueError(f"no row tile for {rows} under {cap}")


def _mm(a, b, *, name, ta=False, tb=False, b_dev=False, out_dev=False, out_dtypes=(F32,), epilogue=None,
        tile_extras=(), col_extras=(), after=(), tm_cap=2048, tn_cap=1024, tk_cap=4096):
    m, k = (a.shape[1], a.shape[0]) if ta else a.shape
    if b_dev:
        nd, b_rows, ns = b.shape
        n, kb = (b_rows, nd * ns) if tb else (nd * ns, b_rows)
    else:
        n, kb = (b.shape[0], b.shape[1]) if tb else (b.shape[1], b.shape[0])
    assert k == kb, (a.shape, b.shape, ta, tb)
    if ta or tile_extras or (b_dev and tb):
        tm_cap = min(tm_cap, 1024)
    tk = ns if (b_dev and tb) else _tile(k, tk_cap)
    nk = k // tk
    fixed_tn = ns if (b_dev and not tb) else (n // N_DEV if out_dev else None)
    if out_dev:
        assert not tile_extras and not col_extras and n % N_DEV == 0
    out_bytes = sum(jnp.dtype(dt).itemsize for dt in out_dtypes) + sum(e.dtype.itemsize for e in tile_extras)
    tm, tn = None, None
    for cand_m, cand_n in ((tm_cap, tn_cap), (tm_cap, tn_cap // 2), (tm_cap // 2, tn_cap), (tm_cap // 2, tn_cap // 2),
                           (tm_cap // 4, tn_cap // 2), (tm_cap // 4, tn_cap // 4)):
        tm, tn = _tile(m, cand_m), (fixed_tn if fixed_tn is not None else _tile(n, cand_n))
        need = 2 * (tm * tk * a.dtype.itemsize + tk * tn * b.dtype.itemsize) + 2 * tm * tn * out_bytes
        need += tm * tn * 4 * (2 if nk > 1 else 1)
        if need <= MM_VMEM_BUDGET:
            break
    n_te, n_ce, n_out = len(tile_extras), len(col_extras), len(out_dtypes)
    dims = (((0 if ta else 1,), (1 if tb else 0,)), ((), ()))

    def body(*refs):
        a_ref, b_ref = refs[0], refs[1]
        te_refs = refs[2:2 + n_te]
        ce_refs = refs[2 + n_te:2 + n_te + n_ce]
        first_out = 2 + n_te + n_ce + len(after)
        out_refs = refs[first_out:first_out + n_out]

        def finish(acc):
            if epilogue is None:
                outs = (acc,)
            else:
                outs = epilogue(acc, *[r[...] for r in te_refs], *[r[...] for r in ce_refs])
            for o_ref, o in zip(out_refs, outs):
                o_ref[...] = o.astype(o_ref.dtype)

        def product():
            return lax.dot_general(a_ref[...].astype(MXU_DTYPE), b_ref[...].astype(MXU_DTYPE), dims,
                                   preferred_element_type=F32)

        if nk == 1:
            finish(product())
        else:
            acc_ref = refs[-1]
            kk = pl.program_id(2)

            @pl.when(kk == 0)
            def _():
                acc_ref[...] = jnp.zeros_like(acc_ref)

            acc_ref[...] += product()

            @pl.when(kk == nk - 1)
            def _():
                finish(acc_ref[...])

    a_spec = pl.BlockSpec((tk, tm), lambda i, j, kk: (kk, i)) if ta else pl.BlockSpec((tm, tk), lambda i, j, kk: (i, kk))
    if b_dev and tb:
        b_spec = pl.BlockSpec((None, tn, tk), lambda i, j, kk: (kk, j, 0))
    elif b_dev:
        b_spec = pl.BlockSpec((None, tk, tn), lambda i, j, kk: (j, kk, 0))
    elif tb:
        b_spec = pl.BlockSpec((tn, tk), lambda i, j, kk: (j, kk))
    else:
        b_spec = pl.BlockSpec((tk, tn), lambda i, j, kk: (kk, j))
    tile_spec = pl.BlockSpec((tm, tn), lambda i, j, kk: (i, j))
    col_spec = pl.BlockSpec((1, tn), lambda i, j, kk: (0, j))
    if out_dev:
        out_spec = pl.BlockSpec((None, tm, tn), lambda i, j, kk: (j, i, 0))
        out_shape = [jax.ShapeDtypeStruct((N_DEV, m, tn), dt) for dt in out_dtypes]
    else:
        out_spec = tile_spec
        out_shape = [jax.ShapeDtypeStruct((m, n), dt) for dt in out_dtypes]
    outs = pl.pallas_call(
        body,
        name=name,
        grid=(m // tm, n // tn, nk),
        in_specs=[a_spec, b_spec] + [tile_spec] * n_te + [col_spec] * n_ce + [ANY] * len(after),
        out_specs=[out_spec] * n_out,
        out_shape=out_shape,
        scratch_shapes=[pltpu.VMEM((tm, tn), F32)] if nk > 1 else [],
        compiler_params=_params(("parallel", "parallel", "arbitrary")),
    )(a, b, *tile_extras, *col_extras, *after)
    return outs[0] if n_out == 1 else tuple(outs)


def _rowwise(fn, rows, bcast, out_rows, out_accs, *, name, tb_cap=256, after=()):
    t = rows[0].shape[0]
    tb = _row_tile(t, tb_cap)
    n_r, n_b, n_o, n_a = len(rows), len(bcast), len(out_rows), len(out_accs)

    def body(*refs):
        r_refs = refs[:n_r]
        b_refs = refs[n_r:n_r + n_b]
        first_out = n_r + n_b + len(after)
        o_refs = refs[first_out:first_out + n_o]
        a_refs = refs[first_out + n_o:]
        outs, accs = fn([r[...] for r in r_refs], [r[...] for r in b_refs])
        for o_ref, o in zip(o_refs, outs):
            o_ref[...] = o.astype(o_ref.dtype)
        if n_a:
            i = pl.program_id(0)

            @pl.when(i == 0)
            def _():
                for a_ref, acc in zip(a_refs, accs):
                    a_ref[...] = acc

            @pl.when(i > 0)
            def _():
                for a_ref, acc in zip(a_refs, accs):
                    a_ref[...] += acc

    in_specs = [pl.BlockSpec((tb, r.shape[1]), lambda i: (i, 0)) for r in rows]
    in_specs += [pl.BlockSpec(b.shape, lambda i: (0, 0)) for b in bcast]
    in_specs += [ANY] * len(after)
    out_specs = [pl.BlockSpec((tb, c), lambda i: (i, 0)) for c, _ in out_rows]
    out_specs += [pl.BlockSpec(s, lambda i: (0, 0)) for s in out_accs]
    out_shape = [jax.ShapeDtypeStruct((t, c), dt) for c, dt in out_rows]
    out_shape += [jax.ShapeDtypeStruct(s, F32) for s in out_accs]
    outs = pl.pallas_call(
        body,
        name=name,
        grid=(t // tb,),
        in_specs=in_specs,
        out_specs=out_specs,
        out_shape=out_shape,
        compiler_params=_params(("arbitrary",)),
    )(*rows, *bcast, *after)
    return tuple(outs)


def _colsum(v):
    return jnp.sum(v, axis=0, keepdims=True)


def _sigmoid(v):
    return 1.0 / (1.0 + jnp.exp(-v))


def _ln_fwd(x_res, y_sub, g, b, *, name, after=()):
    d = x_res.shape[1]

    def fn(rows, bc):
        z = ALPHA * rows[0] + rows[1]
        mu = jnp.mean(z, axis=-1, keepdims=True)
        zc = z - mu
        var = jnp.mean(zc * zc, axis=-1, keepdims=True)
        rstd = lax.rsqrt(var + LN_EPS)
        xhat = zc * rstd
        out = xhat * bc[0] + bc[1]
        return (out, out, xhat, rstd), ()

    return _rowwise(fn, [x_res, y_sub], [g, b], [(d, F32), (d, MXU_DTYPE), (d, F32), (1, F32)], [], name=name,
                    after=after, tb_cap=512)


def _ln_fwd_loss(x_res, y_sub, g, b, target, *, name, after=()):
    d = x_res.shape[1]

    def fn(rows, bc):
        z = ALPHA * rows[0] + rows[1]
        mu = jnp.mean(z, axis=-1, keepdims=True)
        zc = z - mu
        var = jnp.mean(zc * zc, axis=-1, keepdims=True)
        rstd = lax.rsqrt(var + LN_EPS)
        xhat = zc * rstd
        err = xhat * bc[0] + bc[1] - rows[2]
        return (xhat, rstd, err * (1.0 / d)), (_colsum(err * err),)

    return _rowwise(fn, [x_res, y_sub, target], [g, b], [(d, F32), (1, F32), (d, F32)], [(1, d)], name=name,
                    after=after, tb_cap=512)


def _ln_bwd(terms, xhat, rstd, g, *, name, after=()):
    d = xhat.shape[1]
    coefs = [c for c, _ in terms]

    def fn(rows, bc):
        xh, rs = rows[0], rows[1]
        dout = None
        for c, v in zip(coefs, rows[2:]):
            tv = v if c == 1.0 else c * v
            dout = tv if dout is None else dout + tv
        dxh = dout * bc[0]
        m1 = jnp.mean(dxh, axis=-1, keepdims=True)
        m2 = jnp.mean(dxh * xh, axis=-1, keepdims=True)
        dz = rs * (dxh - m1 - xh * m2)
        return (dz, dz), (_colsum(dout * xh), _colsum(dout), _colsum(dz))

    return _rowwise(fn, [xhat, rstd] + [v for _, v in terms], [g], [(d, F32), (d, MXU_DTYPE)],
                    [(1, d), (1, d), (1, d)], name=name, after=after, tb_cap=512)


def _rms_fwd(x, g, *, name):
    c = x.shape[1]

    def fn(rows, bc):
        v = rows[0]
        r = lax.rsqrt(jnp.mean(v * v, axis=-1, keepdims=True) + RMS_EPS)
        return (v * r * bc[0],), ()

    return _rowwise(fn, [x], [g], [(c, MXU_DTYPE)], [], name=name)[0]


def _rms_bwd(x, g, dy, *, name):
    c = x.shape[1]

    def fn(rows, bc):
        v, dyv = rows
        r = lax.rsqrt(jnp.mean(v * v, axis=-1, keepdims=True) + RMS_EPS)
        xh = v * r
        dxh = dyv * bc[0]
        dx = r * (dxh - xh * jnp.mean(dxh * xh, axis=-1, keepdims=True))
        return (dx,), (_colsum(dyv * xh),)

    return _rowwise(fn, [x, dy], [g], [(c, F32)], [(1, c)], name=name)


def _conv_chunk(t):
    return _row_tile(t, 256)


ATTN_SCALE = (QK_NOPE + QK_ROPE) ** -0.5
_NT = (((1,), (1,)), ((), ()))
_TN = (((0,), (0,)), ((), ()))
_NN = (((1,), (0,)), ((), ()))


def _dot(a, b, dims):
    return lax.dot_general(a, b, dims, preferred_element_type=F32)


def _block_scores(q_cat, kn_ref, kp_ref, lo, hi, diagonal):
    k_cat = jnp.concatenate([kn_ref[lo:hi, :], kp_ref[lo:hi, :]], axis=1)
    s = _dot(q_cat, k_cat, _NT) * ATTN_SCALE
    if diagonal:
        shift = CHUNK.bit_length() - 1
        rows = lax.broadcasted_iota(jnp.int32, s.shape, 0)
        cols = lax.broadcasted_iota(jnp.int32, s.shape, 1)
        s = jnp.where(jnp.right_shift(cols, shift) <= jnp.right_shift(rows, shift), s, MASK_VALUE)
    return s, k_cat


def _attn_fwd(qn, qp, kv, kp, *, name):
    h, t, _ = qp.shape
    tq = _row_tile(t, 256)
    assert tq % CHUNK == 0

    def body(qn_ref, qp_ref, kn_ref, kp_ref, v_ref, o_ref, of_ref, lse_ref):
        for r0 in range(0, t, tq):
            r1 = r0 + tq
            q_cat = jnp.concatenate([qn_ref[r0:r1, :].astype(MXU_DTYPE), qp_ref[r0:r1, :]], axis=1)
            s_d, _ = _block_scores(q_cat, kn_ref, kp_ref, r0, r1, True)
            m = jnp.max(s_d, axis=-1, keepdims=True)
            if r0:
                s_b, _ = _block_scores(q_cat, kn_ref, kp_ref, 0, r0, False)
                m = jnp.maximum(m, jnp.max(s_b, axis=-1, keepdims=True))
            p_d = jnp.exp(s_d - m)
            l = jnp.sum(p_d, axis=-1, keepdims=True)
            acc = _dot(p_d.astype(MXU_DTYPE), v_ref[r0:r1, :], _NN)
            if r0:
                p_b = jnp.exp(s_b - m)
                l = l + jnp.sum(p_b, axis=-1, keepdims=True)
                acc = acc + _dot(p_b.astype(MXU_DTYPE), v_ref[0:r0, :], _NN)
            o = acc / l
            o_ref[r0:r1, :] = o.astype(o_ref.dtype)
            of_ref[r0:r1, :] = o
            lse_ref[r0:r1, :] = m + jnp.log(l)

    blk = lambda w: pl.BlockSpec((None, t, w), lambda hh: (hh, 0, 0))
    cols = pl.BlockSpec((t, QK_NOPE), lambda hh: (0, hh))
    keys = pl.BlockSpec((t, QK_NOPE), lambda hh: (0, 2 * hh))
    values = pl.BlockSpec((t, V_HEAD), lambda hh: (0, 2 * hh + 1))
    return pl.pallas_call(
        body,
        name=name,
        grid=(h,),
        in_specs=[cols, blk(QK_ROPE), keys, pl.BlockSpec((t, QK_ROPE), lambda hh: (0, 0)), values],
        out_specs=[cols, cols, blk(1)],
        out_shape=[jax.ShapeDtypeStruct((t, h * V_HEAD), MXU_DTYPE), jax.ShapeDtypeStruct((t, h * V_HEAD), F32),
                   jax.ShapeDtypeStruct((h, t, 1), F32)],
        compiler_params=_params(("parallel",)),
    )(qn, qp, kv, kp, kv)


def _attn_bwd(qn, qp, kv, kp, do, o, lse, *, name):
    h, t, _ = qp.shape
    tq = _row_tile(t, 256)

    def body(qn_ref, qp_ref, kn_ref, kp_ref, v_ref, do_ref, o_ref, lse_ref, dqn_ref, dqp_ref, dkv_ref, dkp_ref):
        dkv_ref[...] = jnp.zeros_like(dkv_ref)

        @pl.when(pl.program_id(0) == 0)
        def _():
            dkp_ref[...] = jnp.zeros_like(dkp_ref)

        for r0 in range(0, t, tq):
            r1 = r0 + tq
            q_cat = jnp.concatenate([qn_ref[r0:r1, :].astype(MXU_DTYPE), qp_ref[r0:r1, :]], axis=1)
            dov, lse = do_ref[r0:r1, :], lse_ref[r0:r1, :]
            delta = jnp.sum(dov.astype(F32) * o_ref[r0:r1, :], axis=-1, keepdims=True)

            def piece(lo, hi, diagonal):
                s, k_cat = _block_scores(q_cat, kn_ref, kp_ref, lo, hi, diagonal)
                p = jnp.exp(s - lse)
                dp = _dot(dov, v_ref[lo:hi, :], _NT)
                ds_m = (p * (dp - delta) * ATTN_SCALE).astype(MXU_DTYPE)
                dk_cat = _dot(ds_m, q_cat, _TN)
                dkv_ref[lo:hi, :QK_NOPE] += dk_cat[:, :QK_NOPE]
                dkp_ref[lo:hi, :] += dk_cat[:, QK_NOPE:]
                dkv_ref[lo:hi, QK_NOPE:] += _dot(p.astype(MXU_DTYPE), dov, _TN)
                return _dot(ds_m, k_cat, _NN)

            dq_cat = piece(r0, r1, True)
            if r0:
                dq_cat = dq_cat + piece(0, r0, False)
            dqn_ref[r0:r1, :] = dq_cat[:, :QK_NOPE]
            dqp_ref[r0:r1, :] = dq_cat[:, QK_NOPE:]

    blk = lambda w: pl.BlockSpec((None, t, w), lambda hh: (hh, 0, 0))
    kpblk = pl.BlockSpec((t, QK_ROPE), lambda hh: (0, 0))
    cols = pl.BlockSpec((t, QK_NOPE), lambda hh: (0, hh))
    keys = pl.BlockSpec((t, QK_NOPE), lambda hh: (0, 2 * hh))
    values = pl.BlockSpec((t, V_HEAD), lambda hh: (0, 2 * hh + 1))
    both = pl.BlockSpec((t, QK_NOPE + V_HEAD), lambda hh: (0, hh))
    return pl.pallas_call(
        body,
        name=name,
        grid=(h,),
        in_specs=[cols, blk(QK_ROPE), keys, kpblk, values, cols, cols, blk(1)],
        out_specs=[cols, blk(QK_ROPE), both, kpblk],
        out_shape=[jax.ShapeDtypeStruct((t, h * QK_NOPE), F32), jax.ShapeDtypeStruct((h, t, QK_ROPE), F32),
                   jax.ShapeDtypeStruct((t, h * (QK_NOPE + V_HEAD)), F32), jax.ShapeDtypeStruct((t, QK_ROPE), F32)],
        compiler_params=_params(("arbitrary",)),
    )(qn, qp, kv, kp, kv, do, o, lse)


def _rope(x1, x2, cos, sin, *, name, inverse=False):
    w = x1.shape[1]
    sign = -1.0 if inverse else 1.0

    def fn(rows, bc):
        a, b, c, s = rows
        s = sign * s
        return (a * c - b * s, a * s + b * c), ()

    return _rowwise(fn, [x1, x2, cos, sin], [], [(w, F32), (w, F32)], [], name=name)


def _rope_in_place(x, cos_pairs, sin_signed, *, name, inverse=False):
    w = x.shape[1]
    half = ROPE_HALF
    sign = -1.0 if inverse else 1.0

    def fn(rows, bc):
        v, c, s = rows
        lane = lax.broadcasted_iota(jnp.int32, v.shape, 1)
        partner = jnp.where((lane & half) == 0, pltpu.roll(v, w - half, 1), pltpu.roll(v, half, 1))
        return (v * c + partner * (sign * s),), ()

    return _rowwise(fn, [x, cos_pairs, sin_signed], [], [(w, F32)], [], name=name)[0]


def _mlp_up(x_m, w1, tag, after=()):
    def epi(acc):
        r = jnp.maximum(acc, 0.0)
        return (r * r,)

    return _mm(x_m, w1, name=f"{tag}_up", out_dtypes=(MXU_DTYPE,), epilogue=epi, after=after)


def _mlp_fwd(x_m, w1, w2, tag, after=(), act=None):
    if act is None:
        act, after = _mlp_up(x_m, w1, tag, after=after), ()
    y = _mm(act, w2, name=f"{tag}_down", after=after)
    return y, (x_m, act)


def _plus_residual(acc, res):
    return (acc + ALPHA * res,)


def _mlp_bwd(dy_m, saved, w1, w2, tag, residual, after=()):
    x_m, act = saved
    dw2 = _mm(act, dy_m, ta=True, name=f"{tag}_dw2", out_dtypes=(MXU_DTYPE,))

    def epi(acc, av):
        return (acc * (2.0 * jnp.sqrt(av.astype(F32))),)

    dh = _mm(dy_m, w2, tb=True, name=f"{tag}_dact", out_dtypes=(MXU_DTYPE,), epilogue=epi, tile_extras=(act,),
             after=after)
    dw1 = _mm(x_m, dh, ta=True, out_dev=True, name=f"{tag}_dw1", out_dtypes=(MXU_DTYPE,))
    dx = _mm(dh, w1, tb=True, name=f"{tag}_dx", epilogue=_plus_residual, tile_extras=(residual,))
    return [dx], {"ff_w1": dw1, "ff_w2": dw2}


def _sc_gate_specs(t, d):
    per = d // LANES
    return [pl.BlockSpec((t, LANES), lambda j, k=k: (0, k * per + j)) for k in range(3)]


def _sc_gate_fwd(u, conv_w, *, name):
    t, d = u.shape[0], u.shape[1] // 3
    kw = conv_w.shape[0]
    ch = _conv_chunk(t)

    def body(b_ref, c_ref, h_ref, w_ref, r_ref, pad_ref):
        pad_ref[0:CONV_PAD, :] = jnp.zeros((CONV_PAD, LANES), F32)
        pad_ref[CONV_PAD:CONV_PAD + t, :] = c_ref[...] * h_ref[...]
        for t0 in range(0, t, ch):
            q = jnp.zeros((ch, LANES), F32)
            for k in range(kw):
                s = kw - 1 - k
                q = q + w_ref[k:k + 1, :] * pad_ref[CONV_PAD - s + t0:CONV_PAD - s + t0 + ch, :]
            r_ref[t0:t0 + ch, :] = (b_ref[t0:t0 + ch, :] * q).astype(r_ref.dtype)

    blk = pl.BlockSpec((t, LANES), lambda j: (0, j))
    return pl.pallas_call(
        body,
        name=name,
        grid=(d // LANES,),
        in_specs=_sc_gate_specs(t, d) + [pl.BlockSpec((kw, LANES), lambda j: (0, j))],
        out_specs=blk,
        out_shape=jax.ShapeDtypeStruct((t, d), MXU_DTYPE),
        scratch_shapes=[pltpu.VMEM((CONV_PAD + t, LANES), F32)],
        compiler_params=_params(("parallel",)),
    )(u, u, u, conv_w)


def _sc_gate_bwd(dr, u, conv_w, *, name):
    t, d = u.shape[0], u.shape[1] // 3
    kw = conv_w.shape[0]
    ch = _conv_chunk(t)

    def body(dr_ref, b_ref, c_ref, h_ref, w_ref, db_ref, dc_ref, dh_ref, dw_ref, ppad_ref, dpad_ref):
        ppad_ref[0:CONV_PAD, :] = jnp.zeros((CONV_PAD, LANES), F32)
        ppad_ref[CONV_PAD:CONV_PAD + t, :] = c_ref[...] * h_ref[...]
        dpad_ref[0:t, :] = dr_ref[...] * b_ref[...]
        dpad_ref[t:t + CONV_PAD, :] = jnp.zeros((CONV_PAD, LANES), F32)
        for t0 in range(0, t, ch):
            q = jnp.zeros((ch, LANES), F32)
            dp = jnp.zeros((ch, LANES), F32)
            for k in range(kw):
                s = kw - 1 - k
                q = q + w_ref[k:k + 1, :] * ppad_ref[CONV_PAD - s + t0:CONV_PAD - s + t0 + ch, :]
                dp = dp + w_ref[k:k + 1, :] * dpad_ref[t0 + s:t0 + s + ch, :]
            db_ref[t0:t0 + ch, :] = (dr_ref[t0:t0 + ch, :] * q).astype(db_ref.dtype)
            dc_ref[t0:t0 + ch, :] = (dp * h_ref[t0:t0 + ch, :]).astype(dc_ref.dtype)
            dh_ref[t0:t0 + ch, :] = (dp * c_ref[t0:t0 + ch, :]).astype(dh_ref.dtype)
        for k in range(kw):
            s = kw - 1 - k
            acc = jnp.zeros((ch, LANES), F32)
            for t0 in range(0, t, ch):
                acc = acc + dpad_ref[t0:t0 + ch, :] * ppad_ref[CONV_PAD - s + t0:CONV_PAD - s + t0 + ch, :]
            dw_ref[k:k + 1, :] = _colsum(acc)

    blk = pl.BlockSpec((t, LANES), lambda j: (0, j))
    wblk = pl.BlockSpec((kw, LANES), lambda j: (0, j))
    return pl.pallas_call(
        body,
        name=name,
        grid=(d // LANES,),
        in_specs=[blk] + _sc_gate_specs(t, d) + [wblk],
        out_specs=[blk, blk, blk, wblk],
        out_shape=[jax.ShapeDtypeStruct((t, d), MXU_DTYPE)] * 3 + [jax.ShapeDtypeStruct((kw, d), F32)],
        scratch_shapes=[pltpu.VMEM((CONV_PAD + t, LANES), F32), pltpu.VMEM((CONV_PAD + t, LANES), F32)],
        compiler_params=_params(("parallel",)),
    )(dr, u, u, u, conv_w)


def _sc_fwd(x_m, w_in, conv_w, w_out, tag, after=()):
    u = _mm(x_m, w_in, name=f"{tag}_in", after=after)
    r = _sc_gate_fwd(u, conv_w, name=f"{tag}_gate")
    y = _mm(r, w_out, name=f"{tag}_out")
    return y, (x_m, u, r)


def _sc_bwd(dy_m, saved, w_in, conv_w, w_out, tag, residual, after=()):
    d = D_MODEL
    x_m, u, r = saved
    dw_out = _mm(r, dy_m, ta=True, name=f"{tag}_dwout", out_dtypes=(MXU_DTYPE,))
    dr = _mm(dy_m, w_out, tb=True, name=f"{tag}_dr", after=after)
    db, dc, dh, dconv = _sc_gate_bwd(dr, u, conv_w, name=f"{tag}_dgate")
    du = jnp.concatenate([db, dc, dh], axis=1)
    dw_in = _mm(x_m, du, ta=True, out_dev=True, name=f"{tag}_dwin", out_dtypes=(MXU_DTYPE,))
    dx = _mm(du, w_in, tb=True, name=f"{tag}_dx", epilogue=_plus_residual, tile_extras=(residual,))
    return [dx], {"sc_w_in": dw_in, "sc_conv_w": dconv, "sc_w_out": dw_out}


def _cf_glu_specs(t, d):
    per = d // LANES
    return [pl.BlockSpec((t, LANES), lambda j, k=k: (0, k * per + j)) for k in range(2)]


def _cf_glu_conv_fwd(u, w, bias, *, name):
    t, d = u.shape[0], u.shape[1] // 2
    kw = w.shape[0]
    ch = _conv_chunk(t)

    def body(a_ref, g_ref, w_ref, b_ref, o_ref, pad_ref):
        pad_ref[0:CONV_PAD, :] = jnp.zeros((CONV_PAD, LANES), F32)
        pad_ref[CONV_PAD:CONV_PAD + t, :] = a_ref[...] * _sigmoid(g_ref[...])
        for t0 in range(0, t, ch):
            acc = jnp.broadcast_to(b_ref[...], (ch, LANES))
            for k in range(kw):
                s = kw - 1 - k
                acc = acc + w_ref[k:k + 1, :] * pad_ref[CONV_PAD - s + t0:CONV_PAD - s + t0 + ch, :]
            o_ref[t0:t0 + ch, :] = acc

    return pl.pallas_call(
        body,
        name=name,
        grid=(d // LANES,),
        in_specs=_cf_glu_specs(t, d) + [pl.BlockSpec((kw, LANES), lambda j: (0, j)),
                                        pl.BlockSpec((1, LANES), lambda j: (0, j))],
        out_specs=pl.BlockSpec((t, LANES), lambda j: (0, j)),
        out_shape=jax.ShapeDtypeStruct((t, d), F32),
        scratch_shapes=[pltpu.VMEM((CONV_PAD + t, LANES), F32)],
        compiler_params=_params(("parallel",)),
    )(u, u, w, bias)


def _cf_glu_conv_bwd(dy, u, w, *, name):
    t, d = u.shape[0], u.shape[1] // 2
    kw = w.shape[0]
    ch = _conv_chunk(t)

    def body(dy_ref, a_ref, g_ref, w_ref, da_ref, dg_ref, sa_ref, sg_ref, dw_ref, xpad_ref, dpad_ref):
        xpad_ref[0:CONV_PAD, :] = jnp.zeros((CONV_PAD, LANES), F32)
        xpad_ref[CONV_PAD:CONV_PAD + t, :] = a_ref[...] * _sigmoid(g_ref[...])
        dpad_ref[0:t, :] = dy_ref[...]
        dpad_ref[t:t + CONV_PAD, :] = jnp.zeros((CONV_PAD, LANES), F32)
        sum_a = jnp.zeros((1, LANES), F32)
        sum_g = jnp.zeros((1, LANES), F32)
        for t0 in range(0, t, ch):
            dh1 = jnp.zeros((ch, LANES), F32)
            for k in range(kw):
                s = kw - 1 - k
                dh1 = dh1 + w_ref[k:k + 1, :] * dpad_ref[t0 + s:t0 + s + ch, :]
            sg = _sigmoid(g_ref[t0:t0 + ch, :])
            da = dh1 * sg
            dgate = da * a_ref[t0:t0 + ch, :] * (1.0 - sg)
            da_ref[t0:t0 + ch, :] = da.astype(da_ref.dtype)
            dg_ref[t0:t0 + ch, :] = dgate.astype(dg_ref.dtype)
            sum_a = sum_a + _colsum(da)
            sum_g = sum_g + _colsum(dgate)
        sa_ref[...] = sum_a
        sg_ref[...] = sum_g
        for k in range(kw):
            s = kw - 1 - k
            acc = jnp.zeros((ch, LANES), F32)
            for t0 in range(0, t, ch):
                acc = acc + dy_ref[t0:t0 + ch, :] * xpad_ref[CONV_PAD - s + t0:CONV_PAD - s + t0 + ch, :]
            dw_ref[k:k + 1, :] = _colsum(acc)

    blk = pl.BlockSpec((t, LANES), lambda j: (0, j))
    row = pl.BlockSpec((1, LANES), lambda j: (0, j))
    wblk = pl.BlockSpec((kw, LANES), lambda j: (0, j))
    return pl.pallas_call(
        body,
        name=name,
        grid=(d // LANES,),
        in_specs=[blk] + _cf_glu_specs(t, d) + [wblk],
        out_specs=[blk, blk, row, row, wblk],
        out_shape=[jax.ShapeDtypeStruct((t, d), MXU_DTYPE)] * 2 + [jax.ShapeDtypeStruct((1, d), F32)] * 2
        + [jax.ShapeDtypeStruct((kw, d), F32)],
        scratch_shapes=[pltpu.VMEM((CONV_PAD + t, LANES), F32), pltpu.VMEM((CONV_PAD + t, LANES), F32)],
        compiler_params=_params(("parallel",)),
    )(dy, u, u, w)


def _cf_fwd(x_m, w, tag, after=()):
    d = D_MODEL

    def bias_epi(acc, bias):
        return (acc + bias,)

    u = _mm(x_m, w["cf_w_pw1"], name=f"{tag}_pw1", epilogue=bias_epi, col_extras=(w["cf_b_pw1"],),
            after=after)
    h2 = _cf_glu_conv_fwd(u, w["cf_dw_w"], w["cf_dw_b"], name=f"{tag}_conv")

    def norm_fn(rows, bc):
        hv = rows[0]
        mu = jnp.mean(hv, axis=-1, keepdims=True)
        hc = hv - mu
        rstd = lax.rsqrt(jnp.mean(hc * hc, axis=-1, keepdims=True) + LN_EPS)
        h3 = hc * rstd * bc[0] + bc[1]
        return (h3 * _sigmoid(h3),), ()

    h4 = _rowwise(norm_fn, [h2], [w["cf_norm_g"], w["cf_norm_b"]], [(d, MXU_DTYPE)], [], name=f"{tag}_norm")[0]
    y = _mm(h4, w["cf_w_pw2"], name=f"{tag}_pw2", epilogue=bias_epi, col_extras=(w["cf_b_pw2"],))
    return y, (x_m, u, h2, h4)


def _cf_bwd(dy_m, dy_colsum, saved, w, tag, residual, after=()):
    d = D_MODEL
    x_m, u, h2, h4 = saved
    dw_pw2 = _mm(h4, dy_m, ta=True, name=f"{tag}_dwpw2", out_dtypes=(MXU_DTYPE,))
    dh4 = _mm(dy_m, w["cf_w_pw2"], tb=True, name=f"{tag}_dh4", after=after)

    def dnorm_fn(rows, bc):
        dh4v, hv = rows
        g, b = bc
        mu = jnp.mean(hv, axis=-1, keepdims=True)
        hc = hv - mu
        rstd = lax.rsqrt(jnp.mean(hc * hc, axis=-1, keepdims=True) + LN_EPS)
        xh = hc * rstd
        h3 = xh * g + b
        sg = _sigmoid(h3)
        dh3 = dh4v * (sg * (1.0 + h3 * (1.0 - sg)))
        dxh = dh3 * g
        dh2 = rstd * (dxh - jnp.mean(dxh, axis=-1, keepdims=True) - xh * jnp.mean(dxh * xh, axis=-1, keepdims=True))
        return (dh2,), (_colsum(dh3 * xh), _colsum(dh3), _colsum(dh2))

    dh2, dnorm_g, dnorm_b, ddw_b = _rowwise(dnorm_fn, [dh4, h2], [w["cf_norm_g"], w["cf_norm_b"]], [(d, F32)],
                                            [(1, d), (1, d), (1, d)], name=f"{tag}_dnorm")
    da, dgate, sum_a, sum_g, ddw_w = _cf_glu_conv_bwd(dh2, u, w["cf_dw_w"], name=f"{tag}_dconv")
    du = jnp.concatenate([da, dgate], axis=1)
    db_pw1 = jnp.concatenate([sum_a, sum_g], axis=1)
    dw_pw1 = _mm(x_m, du, ta=True, out_dev=True, name=f"{tag}_dwpw1", out_dtypes=(MXU_DTYPE,))
    dx = _mm(du, w["cf_w_pw1"], tb=True, name=f"{tag}_dx", epilogue=_plus_residual, tile_extras=(residual,))
    grads = {"cf_w_pw1": dw_pw1, "cf_b_pw1": db_pw1, "cf_dw_w": ddw_w, "cf_dw_b": ddw_b, "cf_norm_g": dnorm_g,
             "cf_norm_b": dnorm_b, "cf_w_pw2": dw_pw2, "cf_b_pw2": dy_colsum}
    return [dx], grads


def _rope_tables(t):
    pos = jnp.arange(t, dtype=F32)
    inv_freq = ROPE_THETA ** (-jnp.arange(0, QK_ROPE, 2, dtype=F32) / QK_ROPE)
    ang = pos[:, None] * inv_freq[None, :]
    return jnp.cos(ang), jnp.sin(ang)


ROPE_HALF = QK_ROPE // 2
Q_NOPE_COLS = MLA_HEADS * QK_NOPE


def _interleave_heads(w_uk, w_uv):
    c = w_uk.shape[0]
    return jnp.concatenate([w_uk.reshape(c, MLA_HEADS, QK_NOPE), w_uv.reshape(c, MLA_HEADS, V_HEAD)],
                           axis=2).reshape(c, -1)


def _mla_fwd(x_m, w, tag, after=()):
    t = x_m.shape[0]
    cos, sin = _rope_tables(t)
    cos_h = jnp.tile(jnp.concatenate([cos, cos], axis=1), (1, MLA_HEADS))
    sin_h = jnp.tile(jnp.concatenate([-sin, sin], axis=1), (1, MLA_HEADS))
    def rms_epilogue(acc, g):
        r = lax.rsqrt(jnp.mean(acc * acc, axis=-1, keepdims=True) + RMS_EPS)
        return acc, acc * r * g

    assert w["mla_w_dq"].shape[1] <= 1024
    cq_pre, cq = _mm(x_m, w["mla_w_dq"], name=f"{tag}_dq", out_dtypes=(F32, MXU_DTYPE), epilogue=rms_epilogue,
                     col_extras=(w["mla_g_q"],), after=after)
    q = _mm(cq, w["mla_w_uq"], name=f"{tag}_uq")
    qp = _rope_in_place(q[:, Q_NOPE_COLS:], cos_h, sin_h, name=f"{tag}_qrope")
    qp_h = qp.reshape(t, MLA_HEADS, QK_ROPE).transpose(1, 0, 2).astype(MXU_DTYPE)
    ckv_full = _mm(x_m, w["mla_w_dkv"], name=f"{tag}_dkv")
    ckv_pre = ckv_full[:, :KV_LORA]
    ckv = _rms_fwd(ckv_pre, w["mla_g_kv"], name=f"{tag}_kvnorm")
    k1, k2 = _rope(ckv_full[:, KV_LORA:KV_LORA + ROPE_HALF], ckv_full[:, KV_LORA + ROPE_HALF:], cos, sin,
                   name=f"{tag}_krope")
    kp = jnp.concatenate([k1, k2], axis=-1).astype(MXU_DTYPE)
    kv = _mm(ckv, _interleave_heads(w["mla_w_uk"], w["mla_w_uv"]), name=f"{tag}_ukv", out_dtypes=(MXU_DTYPE,))
    o, o_f, lse = _attn_fwd(q, qp_h, kv, kp, name=f"{tag}_attn")
    y = _mm(o, w["mla_w_o"], name=f"{tag}_o")
    return y, (x_m, cq_pre, cq, ckv_pre, ckv, q, qp_h, kv, kp, lse, o, o_f, (cos, sin, cos_h, sin_h))


def _mla_bwd(dy_m, saved, w, tag, residual, after=()):
    x_m, cq_pre, cq, ckv_pre, ckv, q, qp_h, kv, kp, lse, o, o_f, (cos, sin, cos_h, sin_h) = saved
    t = x_m.shape[0]
    half = ROPE_HALF
    dw_o = _mm(o, dy_m, ta=True, name=f"{tag}_dwo", out_dtypes=(MXU_DTYPE,))
    do = _mm(dy_m, w["mla_w_o"], tb=True, name=f"{tag}_do", out_dtypes=(MXU_DTYPE,), after=after)
    dqn, dqp_h, dkv, dkp = _attn_bwd(q, qp_h, kv, kp, do, o_f, lse, name=f"{tag}_dattn")
    dqp = _rope_in_place(dqp_h.transpose(1, 0, 2).reshape(t, -1), cos_h, sin_h, name=f"{tag}_dqrope", inverse=True)
    dq = jnp.concatenate([dqn, dqp], axis=-1).astype(MXU_DTYPE)
    dw_uq = _mm(cq, dq, ta=True, name=f"{tag}_dwuq", out_dtypes=(MXU_DTYPE,))
    dcq = _mm(dq, w["mla_w_uq"], tb=True, name=f"{tag}_dcq")
    dcq_pre, dg_q = _rms_bwd(cq_pre, w["mla_g_q"], dcq, name=f"{tag}_dqnorm")
    dcq_pre_m = dcq_pre.astype(MXU_DTYPE)
    dw_dq = _mm(x_m, dcq_pre_m, ta=True, name=f"{tag}_dwdq", out_dtypes=(MXU_DTYPE,))
    dx_q = _mm(dcq_pre_m, w["mla_w_dq"], tb=True, name=f"{tag}_dxq", epilogue=_plus_residual, tile_extras=(residual,))
    dw_kv = _mm(ckv, dkv, ta=True, name=f"{tag}_dwukv", out_dtypes=(MXU_DTYPE,))
    dw_kv = dw_kv.reshape(KV_LORA, MLA_HEADS, QK_NOPE + V_HEAD)
    dw_uk, dw_uv = dw_kv[:, :, :QK_NOPE].reshape(KV_LORA, -1), dw_kv[:, :, QK_NOPE:].reshape(KV_LORA, -1)
    dckv = _mm(dkv, _interleave_heads(w["mla_w_uk"], w["mla_w_uv"]), tb=True, name=f"{tag}_dckv")
    dckv_pre, dg_kv = _rms_bwd(ckv_pre, w["mla_g_kv"], dckv, name=f"{tag}_dkvnorm")
    dk1, dk2 = _rope(dkp[:, :half], dkp[:, half:], cos, sin, name=f"{tag}_dkrope", inverse=True)
    dckv_full = jnp.concatenate([dckv_pre, dk1, dk2], axis=-1).astype(MXU_DTYPE)
    dw_dkv = _mm(x_m, dckv_full, ta=True, name=f"{tag}_dwdkv", out_dtypes=(MXU_DTYPE,))
    dx_kv = _mm(dckv_full, w["mla_w_dkv"], tb=True, name=f"{tag}_dxkv", epilogue=lambda acc, other: (acc + other,),
                tile_extras=(dx_q,))
    grads = {"mla_w_dq": dw_dq, "mla_g_q": dg_q, "mla_w_uq": dw_uq, "mla_w_dkv": dw_dkv, "mla_g_kv": dg_kv,
             "mla_w_uk": dw_uk, "mla_w_uv": dw_uv, "mla_w_o": dw_o}
    return [dx_kv], grads


def _cast_input(x):
    cast = lambda rows, bc: ((rows[0],), ())
    return _rowwise(cast, [x], [], [(D_MODEL, MXU_DTYPE)], [], name="x_cast")[0]


def _hooked(value, hook):
    token = None if hook is None else hook(value)
    return () if token is None else (token,)


def _mix_fwd(i, cur, cur_m, wl, hook=None, after=()):
    mixer = i % N_MIXERS
    if mixer == 0:
        y, s_mix = _sc_fwd(cur_m, wl["sc_w_in"], wl["sc_conv_w"], wl["sc_w_out"], f"l{i}_sc", after=after)
    elif mixer == 1:
        y, s_mix = _mla_fwd(cur_m, wl, f"l{i}_mla", after=after)
    else:
        y, s_mix = _cf_fwd(cur_m, wl, f"l{i}_cf", after=after)
    cur, cur_m, xh_mix, rs_mix = _ln_fwd(cur, y, wl["ln_mix_g"], wl["ln_mix_b"], name=f"l{i}_lnmix",
                                         after=_hooked(y, hook))
    return cur, cur_m, (s_mix, xh_mix, rs_mix)


def _ff_fwd(i, cur, cur_m, wl, hook=None, after=(), loss_target=None, act=None):
    y, s_ff = _mlp_fwd(cur_m, wl["ff_w1"], wl["ff_w2"], f"l{i}_ff", after=after, act=act)
    if loss_target is not None:
        xh_ff, rs_ff, dy, sq = _ln_fwd_loss(cur, y, wl["ln_ff_g"], wl["ln_ff_b"], loss_target, name=f"l{i}_lnff_loss",
                                            after=_hooked(y, hook))
        return (0.5 / D_MODEL) * jnp.sum(sq), dy, (s_ff, xh_ff, rs_ff)
    cur, cur_m, xh_ff, rs_ff = _ln_fwd(cur, y, wl["ln_ff_g"], wl["ln_ff_b"], name=f"l{i}_lnff", after=_hooked(y, hook))
    return cur, cur_m, (s_ff, xh_ff, rs_ff)


def _layer_fwd(i, cur, cur_m, wl):
    cur, cur_m, s_mix = _mix_fwd(i, cur, cur_m, wl)
    cur, cur_m, s_ff = _ff_fwd(i, cur, cur_m, wl)
    return cur, cur_m, (*s_mix, *s_ff)


def _loss_and_grad(out, target):
    d = D_MODEL

    def loss_fn(rows, bc):
        err = rows[0] - rows[1]
        return (err * (1.0 / d),), (_colsum(err * err),)

    dy, sq = _rowwise(loss_fn, [out, target], [], [(d, F32)], [(1, d)], name="loss")
    return (0.5 / d) * jnp.sum(sq), dy


def _ff_bwd(i, terms, saved, wl, hook=None, after=()):
    s_ff, xh_ff, rs_ff = saved
    grads = {}
    dz, dz_m, grads["ln_ff_g"], grads["ln_ff_b"], _ = _ln_bwd(terms, xh_ff, rs_ff, wl["ln_ff_g"], name=f"l{i}_dlnff",
                                                              after=after)
    dxs, g_ff = _mlp_bwd(dz_m, s_ff, wl["ff_w1"], wl["ff_w2"], f"l{i}_ff", dz, after=_hooked(dz, hook))
    grads.update(g_ff)
    return [(1.0, v) for v in dxs], grads


def _mix_bwd(i, terms, saved, wl, hook=None, after=()):
    mixer = i % N_MIXERS
    s_mix, xh_mix, rs_mix = saved
    grads = {}
    dz, dz_m, grads["ln_mix_g"], grads["ln_mix_b"], dz_sum = _ln_bwd(terms, xh_mix, rs_mix, wl["ln_mix_g"],
                                                                     name=f"l{i}_dlnmix", after=after)
    behind = _hooked(dz, hook)
    if mixer == 0:
        dxs, g_mix = _sc_bwd(dz_m, s_mix, wl["sc_w_in"], wl["sc_conv_w"], wl["sc_w_out"], f"l{i}_sc", dz, after=behind)
    elif mixer == 1:
        dxs, g_mix = _mla_bwd(dz_m, s_mix, wl, f"l{i}_mla", dz, after=behind)
    else:
        dxs, g_mix = _cf_bwd(dz_m, dz_sum, s_mix, wl, f"l{i}_cf", dz, after=behind)
    grads.update(g_mix)
    return [(1.0, v) for v in dxs], grads


def _layer_bwd(i, terms, saved, wl):
    terms, g_ff = _ff_bwd(i, terms, saved[3:], wl)
    terms, g_mix = _mix_bwd(i, terms, saved[:3], wl)
    return terms, {**g_ff, **g_mix}


def _sum_terms(terms, *, name):
    coefs = [c for c, _ in terms]

    def sum_fn(rows, bc):
        acc = None
        for c, v in zip(coefs, rows):
            tv = v if c == 1.0 else c * v
            acc = tv if acc is None else acc + tv
        return (acc,), ()

    return _rowwise(sum_fn, [v for _, v in terms], [], [(D_MODEL, F32)], [], name=name)[0]


def _local_step(x, target, layers):
    cur, cur_m = x, _cast_input(x)
    saved = []
    for i in range(DEPTH):
        cur, cur_m, s = _layer_fwd(i, cur, cur_m, layers[i])
        saved.append(s)
    loss, dy = _loss_and_grad(cur, target)
    terms = [(1.0, dy)]
    grads = [None] * DEPTH
    for i in reversed(range(DEPTH)):
        terms, grads[i] = _layer_bwd(i, terms, saved[i], layers[i])
    return loss, _sum_terms(terms, name="grad_x"), grads


MESH_ID = pl.DeviceIdType.MESH
ANY = pl.BlockSpec(memory_space=pl.ANY)


def _block_id(px, py, pc):
    return 4 * px + 2 * py + pc


COPIES_PER_GATHER = 7


def _all_gather(shards, *, name):
    n = len(shards)

    def body(*refs):
        x_refs, out_refs = refs[:n], refs[n:2 * n]
        send_sems, recv_sems, local_sems = refs[2 * n:]
        x, y, core = lax.axis_index("x"), lax.axis_index("y"), lax.axis_index("c")
        me, sibling = (x, y, core), (x, y, 1 - core)
        chips = [(1 - x, y), (x, 1 - y), (1 - x, 1 - y)]

        def copy(t, k, block, to, from_input=False):
            dst = out_refs[t].at[_block_id(*block)]
            return pltpu.make_async_remote_copy(
                src_ref=x_refs[t] if from_input else dst, dst_ref=dst,
                send_sem=send_sems.at[t * COPIES_PER_GATHER + k], recv_sem=recv_sems.at[t * COPIES_PER_GATHER + k],
                device_id=to, device_id_type=MESH_ID)

        started = []
        mine = [pltpu.make_async_copy(x_refs[t], out_refs[t].at[_block_id(*me)], local_sems.at[t]) for t in range(n)]
        for t in range(n):
            mine[t].start()
            first = [copy(t, 0, me, sibling, from_input=True)]
            first += [copy(t, 1 + j, me, (*chip, core), from_input=True) for j, chip in enumerate(chips)]
            for cp in first:
                cp.start()
            started += first
        for t in range(n):
            for j, chip in enumerate(chips):
                copy(t, 1 + j, (*chip, core), me).wait_recv()
                passed = copy(t, 4 + j, (*chip, core), sibling)
                passed.start()
                started.append(passed)
        for t in range(n):
            copy(t, 0, sibling, me).wait_recv()
            for j, chip in enumerate(chips):
                copy(t, 4 + j, (*chip, 1 - core), me).wait_recv()
        for cp in started:
            cp.wait_send()
        for cp in mine:
            cp.wait()

    n_sems = n * COPIES_PER_GATHER
    outs = pl.pallas_call(
        body,
        name=name,
        out_shape=[jax.ShapeDtypeStruct((N_DEV, *s.shape), s.dtype) for s in shards],
        in_specs=[ANY] * n,
        out_specs=[ANY] * n,
        scratch_shapes=[pltpu.SemaphoreType.DMA((n_sems,)), pltpu.SemaphoreType.DMA((n_sems,)),
                        pltpu.SemaphoreType.DMA((n,))],
    )(*shards)
    return list(outs)


def _swap_with_sibling(parts, *, name):
    n = len(parts)

    def body(*refs):
        p_refs, got_refs = refs[:n], refs[n:2 * n]
        send_sems, recv_sems = refs[2 * n:]
        x, y, core = lax.axis_index("x"), lax.axis_index("y"), lax.axis_index("c")
        chips = [(x, y), (1 - x, y), (x, 1 - y), (1 - x, 1 - y)]
        copies = []
        for t in range(n):
            for k, (px, py) in enumerate(chips):
                copies.append(pltpu.make_async_remote_copy(
                    src_ref=p_refs[t].at[_block_id(px, py, 1 - core)], dst_ref=got_refs[t].at[k],
                    send_sem=send_sems.at[4 * t + k], recv_sem=recv_sems.at[4 * t + k], device_id=(x, y, 1 - core),
                    device_id_type=MESH_ID))
        for cp in copies:
            cp.start()
        for cp in copies:
            cp.wait()

    outs = pl.pallas_call(
        body,
        name=name,
        out_shape=[jax.ShapeDtypeStruct((4, *p.shape[1:]), p.dtype) for p in parts],
        in_specs=[ANY] * n,
        out_specs=[ANY] * n,
        scratch_shapes=[pltpu.SemaphoreType.DMA((4 * n,)), pltpu.SemaphoreType.DMA((4 * n,))],
    )(*parts)
    return list(outs)


def _swap_between_chips(parts, *, name):
    n = len(parts)

    def body(*refs):
        p_refs, got_refs = refs[:n], refs[n:2 * n]
        send_sems, recv_sems = refs[2 * n:]
        x, y, core = lax.axis_index("x"), lax.axis_index("y"), lax.axis_index("c")
        chips = [(1 - x, y), (x, 1 - y), (1 - x, 1 - y)]
        copies = []
        for t in range(n):
            for j, (px, py) in enumerate(chips):
                copies.append(pltpu.make_async_remote_copy(
                    src_ref=p_refs[t].at[j], dst_ref=got_refs[t].at[j], send_sem=send_sems.at[3 * t + j],
                    recv_sem=recv_sems.at[3 * t + j], device_id=(px, py, core), device_id_type=MESH_ID))
        for cp in copies:
            cp.start()
        for cp in copies:
            cp.wait()

    outs = pl.pallas_call(
        body,
        name=name,
        out_shape=[jax.ShapeDtypeStruct(p.shape, p.dtype) for p in parts],
        in_specs=[ANY] * n,
        out_specs=[ANY] * n,
        scratch_shapes=[pltpu.SemaphoreType.DMA((3 * n,)), pltpu.SemaphoreType.DMA((3 * n,))],
    )(*parts)
    return list(outs)


def _chip_partials_group(parts, got, *, name):
    n = len(parts)
    rows = [p.shape[1] for p in parts]
    steps = 4 if all(r % (4 * BF16_ROWS) == 0 for r in rows) else (2 if all(r % (2 * BF16_ROWS) == 0 for r in rows) else 1)
    x, y, core = lax.axis_index("x"), lax.axis_index("y"), lax.axis_index("c")
    chips = [(x, y), (1 - x, y), (x, 1 - y), (1 - x, 1 - y)]
    ids = jnp.stack([_block_id(px, py, core) for px, py in chips]).astype(jnp.int32)

    def body(ids_ref, *refs):
        ins, outs = refs[:5 * n], refs[5 * n:]
        for t in range(n):
            p0, p1, p2, p3, g_ref = ins[5 * t:5 * t + 5]
            own_ref, out_ref = outs[2 * t], outs[2 * t + 1]
            own_ref[...] = p0[...].astype(F32) + g_ref[0].astype(F32)
            for j, p in enumerate((p1, p2, p3)):
                out_ref[j] = (p[...].astype(F32) + g_ref[j + 1].astype(F32)).astype(out_ref.dtype)

    in_specs, out_specs, out_shape, operands = [], [], [], []
    for p, g in zip(parts, got):
        _, r, c = p.shape
        tb = r // steps
        in_specs += [pl.BlockSpec((None, tb, c), lambda i, ids_ref, k=k: (ids_ref[k], i, 0)) for k in range(4)]
        in_specs.append(pl.BlockSpec((4, tb, c), lambda i, ids_ref: (0, i, 0)))
        out_specs += [pl.BlockSpec((tb, c), lambda i, ids_ref: (i, 0)), pl.BlockSpec((3, tb, c), lambda i, ids_ref: (0, i, 0))]
        out_shape += [jax.ShapeDtypeStruct((r, c), F32), jax.ShapeDtypeStruct((3, r, c), p.dtype)]
        operands += [p, p, p, p, g]
    outs = pl.pallas_call(
        body,
        name=name,
        grid_spec=pltpu.PrefetchScalarGridSpec(num_scalar_prefetch=1, grid=(steps,), in_specs=in_specs,
                                               out_specs=out_specs),
        out_shape=out_shape,
        compiler_params=_params(("parallel",)),
    )(ids, *operands)
    return [(outs[2 * t], outs[2 * t + 1]) for t in range(n)]


def _sum_blocks(first, blocks, *, name):
    n, r, c = blocks.shape
    tb = _row_tile(r, 512)

    def body(*refs):
        b_ref, o_ref = refs[-2], refs[-1]
        acc = refs[0][...] if first is not None else b_ref[0].astype(F32)
        for j in range(0 if first is not None else 1, n):
            acc = acc + b_ref[j].astype(F32)
        o_ref[...] = acc

    row_spec = pl.BlockSpec((tb, c), lambda i: (i, 0))
    return pl.pallas_call(
        body,
        name=name,
        grid=(r // tb,),
        in_specs=([row_spec] if first is not None else []) + [pl.BlockSpec((n, tb, c), lambda i: (0, i, 0))],
        out_specs=row_spec,
        out_shape=jax.ShapeDtypeStruct((r, c), F32),
        compiler_params=_params(("parallel",)),
    )(*([first] if first is not None else []), blocks)


HBM = pl.BlockSpec(memory_space=pltpu.HBM)
SEM = pl.BlockSpec(memory_space=pltpu.SEMAPHORE)
DATAFLOW = pltpu.SideEffectType.DATAFLOW_SIDE_EFFECTING


def _split_call(name, arrays, sems, n_new, body, after=None):
    n_a, n_s = len(arrays), len(sems)
    after = [] if after is None else (list(after) if isinstance(after, (list, tuple)) else [after])
    n_after = len(after)

    def kernel_body(*refs):
        new = refs[n_a + n_s + n_after:n_a + n_s + n_after + 2] if n_new else (None, None)
        body(refs[:n_a], refs[n_a:n_a + n_s], *new)
        refs[-1][...] = jnp.zeros_like(refs[-1])

    n_sem_out = 2 if n_new else 0
    out_shape = [pltpu.SemaphoreType.DMA((n_new,))] * n_sem_out
    out_shape += [pltpu.HBM(a.shape, a.dtype) for a in arrays]
    out_shape += [jax.ShapeDtypeStruct((SUBLANES, LANES), F32)]
    outs = pl.pallas_call(
        kernel_body,
        name=name,
        out_shape=out_shape,
        in_specs=[HBM] * n_a + [SEM] * n_s + [ANY] * n_after,
        out_specs=[SEM] * n_sem_out + [HBM] * n_a + [pl.BlockSpec(memory_space=pltpu.VMEM)],
        input_output_aliases={i: n_sem_out + i for i in range(n_a)},
        compiler_params=pltpu.CompilerParams(has_side_effects=DATAFLOW),
    )(*[pltpu.with_memory_space_constraint(a, pltpu.HBM) for a in arrays], *sems, *after)
    return tuple(outs[:n_sem_out]), list(outs[n_sem_out:n_sem_out + n_a]), outs[-1]


def _remote(src, dst, send, recv, k, to):
    return pltpu.make_async_remote_copy(src_ref=src, dst_ref=dst, send_sem=send.at[k], recv_sem=recv.at[k],
                                        device_id=to, device_id_type=MESH_ID)


def _land_block(land_ref, block, by_cols, width):
    if by_cols:
        return land_ref.at[:, pl.ds(pl.multiple_of(block * width, LANES), width)]
    return land_ref.at[block]


def _place_in_window(stacked, layer, *, name):
    _, r, c = stacked.shape
    tb = _row_tile(r, 512)
    me = _block_id(lax.axis_index("x"), lax.axis_index("y"), lax.axis_index("c")).astype(jnp.int32).reshape(1)

    def body(me_ref, s_ref, o_ref, token_ref):
        o_ref[...] = s_ref[...].astype(o_ref.dtype)
        token_ref[...] = jnp.zeros_like(token_ref)

    return pl.pallas_call(
        body,
        name=name,
        grid_spec=pltpu.PrefetchScalarGridSpec(
            num_scalar_prefetch=1, grid=(r // tb,),
            in_specs=[pl.BlockSpec((None, tb, c), lambda i, me_ref: (layer, i, 0))],
            out_specs=[pl.BlockSpec((tb, c), lambda i, me_ref: (i, me_ref[0])),
                       pl.BlockSpec((SUBLANES, LANES), lambda i, me_ref: (0, 0))]),
        out_shape=[jax.ShapeDtypeStruct((r, N_DEV * c), MXU_DTYPE), jax.ShapeDtypeStruct((SUBLANES, LANES), F32)],
        compiler_params=_params(("arbitrary",)),
    )(me, stacked)


def _gather_prepare(shards, by_cols, *, name):
    me = _block_id(lax.axis_index("x"), lax.axis_index("y"), lax.axis_index("c"))
    lands, early, widths = [], [], []
    for i, (s, cols) in enumerate(zip(shards, by_cols)):
        if cols:
            land, token = _place_in_window(*s, name=f"{name}_place{i}")
            lands.append(land)
            early.append(token)
            widths.append(s[0].shape[-1])
        else:
            lands.append(lax.dynamic_update_index_in_dim(lax.empty((N_DEV, *s.shape), s.dtype), s, me, 0))
            early.append(s)
            widths.append(s.shape[-1])
    return lands, widths, list(by_cols), early


def _gather_start(shards, *, name, after=None, by_cols=None, prepared=None):
    if prepared is None:
        prepared = _gather_prepare(shards, by_cols if by_cols is not None else [False] * len(shards), name=name)
    lands, widths, by_cols, _ = prepared
    n = len(lands)

    def body(refs, _, send, recv):
        x, y, core = lax.axis_index("x"), lax.axis_index("y"), lax.axis_index("c")
        targets = [(x, y, 1 - core), (1 - x, y, core), (x, 1 - y, core), (1 - x, 1 - y, core)]
        for t in range(n):
            mine = _land_block(refs[t], _block_id(x, y, core), by_cols[t], widths[t])
            for k, to in enumerate(targets):
                _remote(mine, mine, send, recv, 4 * t + k, to).start()

    sems, thru, token = _split_call(name, lands, [], 4 * n, body, after=after)
    return {"n": n, "sems1": sems, "arrays": thru, "token": token, "by_cols": by_cols, "widths": widths}


def _gather_pass_on(st, *, name, after):
    n = st["n"]

    def body(refs, sems, send, recv):
        send1, recv1 = sems
        x, y, core = lax.axis_index("x"), lax.axis_index("y"), lax.axis_index("c")
        chips = [(1 - x, y), (x, 1 - y), (1 - x, 1 - y)]
        for t in range(n):
            for j, (px, py) in enumerate(chips):
                block = _land_block(refs[t], _block_id(px, py, core), st["by_cols"][t], st["widths"][t])
                _remote(block, block, send1, recv1, 4 * t + 1 + j, (x, y, 1 - core)).wait_recv()
                _remote(block, block, send, recv, 3 * t + j, (x, y, 1 - core)).start()

    sems2, thru, token = _split_call(name, st["arrays"], list(st["sems1"]), 3 * n, body, after=after)
    return {**st, "sems2": sems2, "arrays": thru, "token": token}


def _gather_finish(st, *, name, after, then_start=None):
    n = st["n"]
    n2 = 0 if then_start is None else len(then_start[0])

    def body(refs, sems, new_send=None, new_recv=None):
        send1, recv1, send2, recv2 = sems
        x, y, core = lax.axis_index("x"), lax.axis_index("y"), lax.axis_index("c")
        sibling = (x, y, 1 - core)
        chips = [(1 - x, y), (x, 1 - y), (1 - x, 1 - y)]
        for t in range(n):
            place = lambda px, py, pc, t=t: _land_block(refs[t], _block_id(px, py, pc), st["by_cols"][t],
                                                        st["widths"][t])
            mine = place(x, y, core)
            for k in range(4):
                _remote(mine, mine, send1, recv1, 4 * t + k, sibling).wait_send()
            _remote(mine, place(x, y, 1 - core), send1, recv1, 4 * t, sibling).wait_recv()
            for j, (px, py) in enumerate(chips):
                _remote(place(px, py, core), place(px, py, core), send2, recv2, 3 * t + j, sibling).wait_send()
                _remote(mine, place(px, py, 1 - core), send2, recv2, 3 * t + j, sibling).wait_recv()
        if n2:
            _, widths, by_cols, _ = then_start
            targets = [sibling, (1 - x, y, core), (x, 1 - y, core), (1 - x, 1 - y, core)]
            for t in range(n2):
                mine = _land_block(refs[n + t], _block_id(x, y, core), by_cols[t], widths[t])
                for k, to in enumerate(targets):
                    _remote(mine, mine, new_send, new_recv, 4 * t + k, to).start()

    arrays = st["arrays"] + (list(then_start[0]) if n2 else [])
    sems, thru, token = _split_call(name, arrays, [*st["sems1"], *st["sems2"]], 4 * n2, body, after=after)
    if not n2:
        return thru
    _, widths, by_cols, _ = then_start
    return thru[:n], {"n": n2, "sems1": sems, "arrays": thru[n:], "token": token, "by_cols": by_cols, "widths": widths}


def _swap_start(sources, lands, plan, per_array, *, name):
    n = len(sources)

    def body(refs, _, send, recv):
        x, y, core = lax.axis_index("x"), lax.axis_index("y"), lax.axis_index("c")
        for t in range(n):
            for k in range(per_array):
                block, to = plan(x, y, core, k)
                _remote(refs[t].at[block], refs[n + t].at[k], send, recv, per_array * t + k, to).start()

    sems, thru, token = _split_call(name, sources + lands, [], per_array * n, body)
    return {"n": n, "sems": sems, "arrays": thru, "token": token, "plan": plan, "per_array": per_array}


def _swap_finish(states, *, name, after):
    many = isinstance(states, (list, tuple))
    states = list(states) if many else [states]

    def body(refs, sems, *_):
        x, y, core = lax.axis_index("x"), lax.axis_index("y"), lax.axis_index("c")
        first = 0
        for j, st in enumerate(states):
            n, plan, per_array = st["n"], st["plan"], st["per_array"]
            send, recv = sems[2 * j], sems[2 * j + 1]
            for t in range(n):
                for k in range(per_array):
                    block, to = plan(x, y, core, k)
                    cp = _remote(refs[first + t].at[block], refs[first + n + t].at[k], send, recv, per_array * t + k, to)
                    cp.wait_send()
                    cp.wait_recv()
            first += 2 * n

    arrays = [a for st in states for a in st["arrays"]]
    _, thru, _ = _split_call(name, arrays, [s for st in states for s in st["sems"]], 0, body, after=after)
    out, first = [], 0
    for st in states:
        n = st["n"]
        out.append((thru[first:first + n], thru[first + n:first + 2 * n]))
        first += 2 * n
    return out if many else out[0]


def _to_sibling(x, y, core, k):
    px, py = [(x, y), (1 - x, y), (x, 1 - y), (1 - x, 1 - y)][k]
    return _block_id(px, py, 1 - core), (x, y, 1 - core)


def _to_chip(x, y, core, k):
    px, py = [(1 - x, y), (x, 1 - y), (1 - x, 1 - y)][k]
    return k, (px, py, core)


PARAMS = {
    "sc_w_in": ((2, 1024, 3072), 2), "sc_conv_w": ((2, 3, 1024), 2), "sc_w_out": ((2, 1024, 1024), 1),
    "mla_w_dq": ((1, 1024, 384), 1), "mla_g_q": ((1, 384), None), "mla_w_uq": ((1, 384, 1536), 2),
    "mla_w_dkv": ((1, 1024, 320), 1), "mla_g_kv": ((1, 256), None), "mla_w_uk": ((1, 256, 8, 128), 1),
    "mla_w_uv": ((1, 256, 8, 128), 1), "mla_w_o": ((1, 1024, 1024), 1), "cf_w_pw1": ((1, 1024, 2048), 2),
    "cf_b_pw1": ((1, 2048), 1), "cf_dw_w": ((1, 31, 1024), 2), "cf_dw_b": ((1, 1024), 1), "cf_norm_g": ((1, 1024), 1),
    "cf_norm_b": ((1, 1024), 1), "cf_w_pw2": ((1, 1024, 1024), 1), "cf_b_pw2": ((1, 1024), 1),
    "ff_w1": ((4, 1024, 4096), 2), "ff_w2": ((4, 4096, 1024), 1), "ln_mix_g": ((4, 1024), None),
    "ln_mix_b": ((4, 1024), None), "ln_ff_g": ((4, 1024), None), "ln_ff_b": ((4, 1024), None),
}
NAMES = list(PARAMS)
BIG = ["sc_w_in", "sc_w_out", "mla_w_dq", "mla_w_uq", "mla_w_dkv", "mla_w_uk", "mla_w_uv", "mla_w_o", "cf_w_pw1",
       "cf_w_pw2", "ff_w1", "ff_w2"]
SMALL = [n for n in NAMES if n not in BIG]
SMALL_SHARDED = [n for n in SMALL if PARAMS[n][1] is not None]
SMALL_WIDTH = LANES
DEV_BLOCKED = ("sc_w_in", "cf_w_pw1", "ff_w1")
MIXER_BIG = {0: ["sc_w_in", "sc_w_out"], 1: ["mla_w_dq", "mla_w_uq", "mla_w_dkv", "mla_w_uk", "mla_w_uv", "mla_w_o"],
             2: ["cf_w_pw1", "cf_w_pw2"]}


def _layer_big(i):
    return [(n, i // N_MIXERS) for n in MIXER_BIG[i % N_MIXERS]] + [("ff_w1", i), ("ff_w2", i)]


def _shard_2d(name, a):
    return a.reshape(a.shape[0], -1)


def _shard_for_gather(name, stacked, layer):
    if name in DEV_BLOCKED:
        return stacked, layer
    return _shard_2d(name, stacked[layer]).astype(MXU_DTYPE)


def _as_operand(name, g):
    if name in DEV_BLOCKED:
        return g
    if name == "mla_w_uq":
        by_head = g.transpose(1, 0, 2)
        pieces = [by_head[:, :, :QK_NOPE], by_head[:, :, QK_NOPE:]]
        return jnp.concatenate([p.reshape(g.shape[1], -1) for p in pieces], axis=1)
    return g.reshape(-1, g.shape[2])


def _as_parts(name, grad):
    if name in DEV_BLOCKED:
        return grad
    if name == "mla_w_uq":
        rows = grad.shape[0]
        pieces = [grad[:, :Q_NOPE_COLS].reshape(rows, MLA_HEADS, QK_NOPE),
                  grad[:, Q_NOPE_COLS:].reshape(rows, MLA_HEADS, QK_ROPE)]
        return jnp.concatenate(pieces, axis=2).transpose(1, 0, 2)
    return grad.reshape(N_DEV, -1, grad.shape[1])


def _shard_shape(name):
    shape, ax = PARAMS[name]
    if ax is None:
        return shape
    return tuple(s // N_DEV if i == ax else s for i, s in enumerate(shape))


def _rows_of(shape, width, align):
    n = 1
    for s in shape:
        n *= s
    rows = -(-n // width)
    return -(-rows // align) * align


def _pack_rows(arrays, width, align, dtype, lead=0):
    segs = []
    for a in arrays:
        batch = a.shape[:lead]
        flat = a.astype(dtype).reshape(*batch, -1)
        rows = _rows_of(a.shape[lead:], width, align)
        flat = jnp.pad(flat, [(0, 0)] * lead + [(0, rows * width - flat.shape[-1])])
        segs.append(flat.reshape(*batch, rows, width))
    return jnp.concatenate(segs, axis=lead)


def _unpack_rows(buf, shapes, width, align):
    out, off = [], 0
    lead = buf.shape[:-2]
    for shape in shapes:
        n = 1
        for s in shape:
            n *= s
        rows = _rows_of(shape, width, align)
        seg = buf[..., off:off + rows, :].reshape(*lead, rows * width)[..., :n]
        out.append(seg.reshape(*lead, *shape))
        off += rows
    return out


def _merge_shards(stacked, ax):
    moved = jnp.moveaxis(stacked, 0, ax)
    shape = moved.shape
    return moved.reshape(*shape[:ax], shape[ax] * shape[ax + 1], *shape[ax + 2:])


def _split_shards(full, ax):
    shape = full.shape
    split = full.reshape(*shape[:ax], N_DEV, shape[ax] // N_DEV, *shape[ax + 1:])
    return jnp.moveaxis(split, ax, 0)


def _adam_math(wv, gv, mv, vv):
    m_new = ADAM_B1 * mv + (1.0 - ADAM_B1) * gv
    v_new = ADAM_B2 * vv + (1.0 - ADAM_B2) * (gv * gv)
    m_hat = m_new / (1.0 - ADAM_B1 ** ADAM_STEP)
    v_hat = v_new / (1.0 - ADAM_B2 ** ADAM_STEP)
    delta = -ADAM_LR * (m_hat / (jnp.sqrt(v_hat) + ADAM_EPS) + ADAM_WD * wv)
    return delta, m_new, v_new


def _adamw(w, g, m, v, *, name):
    shape = w.shape
    c = shape[-1]

    def fn(rows, bc):
        return _adam_math(*rows), ()

    flat = [a.reshape(-1, c) for a in (w, g, m, v)]
    outs = _rowwise(fn, flat, [], [(c, F32)] * 3, [], name=name, tb_cap=512)
    return tuple(o.reshape(shape) for o in outs)


def _adamw_reduced(w, m, v, layer_grads, *, name, after=()):
    n_l, r, c = w.shape
    tb = _row_tile(r, 256 if c > 512 else 512)
    nb = r // tb

    def body(*refs):
        w_ref, m_ref, v_ref = refs[:3]
        g_refs = refs[3:3 + 2 * n_l]
        g_out, d_out, m_out, v_out = refs[3 + 2 * n_l + len(after):]
        layer = pl.program_id(0)
        for l in range(n_l):
            @pl.when(layer == l)
            def _(l=l):
                own, arr = g_refs[2 * l], g_refs[2 * l + 1]
                g = own[...] + arr[0].astype(F32)
                g = g + arr[1].astype(F32)
                g = g + arr[2].astype(F32)
                delta, m_new, v_new = _adam_math(w_ref[...], g, m_ref[...], v_ref[...])
                g_out[...] = g
                d_out[...] = delta
                m_out[...] = m_new
                v_out[...] = v_new

    stacked = pl.BlockSpec((tb, c), lambda l, i: (l * nb + i, 0))
    in_specs = [stacked] * 3
    operands = [a.reshape(n_l * r, c) for a in (w, m, v)]
    for l, (own, arr) in enumerate(layer_grads):
        in_specs.append(pl.BlockSpec((tb, c), lambda ll, i, l=l: (jnp.where(ll == l, i, 0), 0)))
        in_specs.append(pl.BlockSpec((3, tb, c), lambda ll, i, l=l: (0, jnp.where(ll == l, i, 0), 0)))
        operands += [own, arr]
    in_specs += [ANY] * len(after)
    operands += list(after)
    outs = pl.pallas_call(
        body,
        name=name,
        grid=(n_l, nb),
        in_specs=in_specs,
        out_specs=[stacked] * 4,
        out_shape=[jax.ShapeDtypeStruct((n_l * r, c), F32)] * 4,
        compiler_params=_params(("arbitrary", "arbitrary")),
    )(*operands)
    return tuple(o.reshape(n_l, r, c) for o in outs)


def _adamw_reduced_group(ws, ms, vs, grads, *, name, after=()):
    n = len(ws)

    def body(*refs):
        ins, outs = refs[:5 * n], refs[5 * n + len(after):]
        for i in range(n):
            w_ref, m_ref, v_ref, own, arr = ins[5 * i:5 * i + 5]
            g = own[...] + arr[0].astype(F32)
            g = g + arr[1].astype(F32)
            g = g + arr[2].astype(F32)
            delta, m_new, v_new = _adam_math(w_ref[...], g, m_ref[...], v_ref[...])
            for o_ref, val in zip(outs[4 * i:4 * i + 4], (g, delta, m_new, v_new)):
                o_ref[...] = val

    whole = lambda a: pl.BlockSpec(a.shape, lambda i, nd=a.ndim: (0,) * nd)
    operands, in_specs, out_specs, out_shape = [], [], [], []
    for w, m, v, (own, arr) in zip(ws, ms, vs, grads):
        operands += [w, m, v, own, arr]
        in_specs += [whole(a) for a in (w, m, v, own, arr)]
        out_specs += [whole(w)] * 4
        out_shape += [jax.ShapeDtypeStruct(w.shape, F32)] * 4
    outs = pl.pallas_call(
        body,
        name=name,
        grid=(1,),
        in_specs=in_specs + [ANY] * len(after),
        out_specs=out_specs,
        out_shape=out_shape,
        compiler_params=_params(("arbitrary",)),
    )(*operands, *after)
    return [tuple(outs[4 * i:4 * i + 4]) for i in range(n)]


def _train_step(x, target, weights, m_state, v_state):
    me = _block_id(lax.axis_index("x"), lax.axis_index("y"), lax.axis_index("c"))

    groups = [(kind, i) for i in range(DEPTH) for kind in ("mix", "ff")]
    forward = [groups[0], ("ff_up", 0), ("ff_down", 0)] + groups[2:]

    def group_names(g):
        kind, i = g
        if kind == "mix":
            return [(n, i // N_MIXERS) for n in MIXER_BIG[i % N_MIXERS]]
        return [(n, i) for n in {"ff": ["ff_w1", "ff_w2"], "ff_up": ["ff_w1"], "ff_down": ["ff_w2"]}[kind]]

    def shards_of(g):
        return [_shard_for_gather(n, weights[n], j) for n, j in group_names(g)]

    whole = lambda g: [n in DEV_BLOCKED for n, _ in group_names(g)]

    small_shapes = [_shard_shape(n) for n in SMALL_SHARDED]
    gather = {0: _gather_start([_pack_rows([weights[n] for n in SMALL_SHARDED], SMALL_WIDTH, SUBLANES, F32)]
                               + shards_of(groups[0]), name="gather_g0_start", by_cols=[False] + whole(groups[0]))}
    cur, cur_m = x[0], x[0]
    prepared = {k: _gather_prepare(shards_of(forward[k]), whole(forward[k]), name=f"gather_g{k}_start")
                for k in range(1, len(forward))}
    early = [a for k in prepared for a in prepared[k][3]]
    gather[0] = _gather_pass_on(gather[0], name="gather_g0_pass", after=early)
    first = _gather_finish(gather[0], name="gather_g0_finish", after=gather[0]["token"])
    small_all, first_matrices = first[0], first[1:]
    small_full = {n: weights[n] for n in SMALL if PARAMS[n][1] is None}
    for n, stacked in zip(SMALL_SHARDED, _unpack_rows(small_all, small_shapes, SMALL_WIDTH, SUBLANES)):
        small_full[n] = _merge_shards(stacked, PARAMS[n][1])

    def small_weights(i):
        wl = {}
        for n in SMALL:
            if n.startswith("ln_"):
                wl[n] = small_full[n][i:i + 1]
            elif n.startswith(("sc_", "mla_", "cf_")[i % N_MIXERS]):
                v = small_full[n][i // N_MIXERS]
                wl[n] = v if v.ndim == 2 else v.reshape(1, -1)
        return wl

    layers = [small_weights(i) for i in range(DEPTH)]

    def install(g, gathered):
        for (n, _), a in zip(group_names(g), gathered):
            layers[g[1]][n] = _as_operand(n, a)

    install(forward[0], first_matrices)
    gather[1] = _gather_start(None, name="gather_g1_start", after=gather[0]["token"], prepared=prepared[1])
    gather[2] = _gather_start(None, name="gather_g2_start", after=gather[1]["token"], prepared=prepared[2])
    saved = {}
    hidden = None
    for k, g in enumerate(forward):
        kind, i = g
        tokens = [gather[1]["token"]] if k == 0 else []
        if k + 2 < len(forward):
            tokens.append(gather[k + 2]["token"])

        def pass_on(y, k=k):
            if k + 1 == len(forward):
                return None
            gather[k + 1] = _gather_pass_on(gather[k + 1], name=f"gather_g{k + 1}_pass", after=y)
            return gather[k + 1]["token"]

        if k + 1 == len(forward):
            loss_local, dy, saved[g] = _ff_fwd(i, cur, cur_m, layers[i], after=tokens, loss_target=target[0])
            break
        if kind == "ff_up":
            hidden = _mlp_up(cur_m, layers[i]["ff_w1"], f"l{i}_ff", after=tokens)
            behind, done = _hooked(hidden, pass_on), hidden
        elif kind == "ff_down":
            cur, cur_m, saved[("ff", i)] = _ff_fwd(i, cur, cur_m, layers[i], hook=pass_on, after=[*tokens, *behind],
                                                   act=hidden)
            done = cur
        else:
            phase = _mix_fwd if kind == "mix" else _ff_fwd
            cur, cur_m, saved[g] = phase(i, cur, cur_m, layers[i], hook=pass_on, after=tokens)
            done = cur
        if k + 3 < len(forward):
            gathered, gather[k + 3] = _gather_finish(gather[k + 1], name=f"gather_g{k + 1}_finish", after=done,
                                                     then_start=prepared[k + 3])
        else:
            gathered = _gather_finish(gather[k + 1], name=f"gather_g{k + 1}_finish", after=done)
        install(forward[k + 1], gathered)

    loss = lax.psum(loss_local, MESH_AXES)

    reduced = {n: [None] * PARAMS[n][0][0] for n in BIG}
    small_grads = {n: [None] * PARAMS[n][0][0] for n in SMALL}
    reduce = {}

    def to_chips(p, after):
        st = reduce[p]
        parts, got = _swap_finish(st["sibling"], name=f"reduce_p{p}_sibling_finish", after=after)
        sums = _chip_partials_group(parts, got, name=f"reduce_p{p}_chipsum")
        st["own"] = [own for own, _ in sums]
        others = [o for _, o in sums]
        st["chips"] = _swap_start(others, [lax.empty(o.shape, o.dtype) for o in others], _to_chip, 3,
                                  name=f"reduce_p{p}_chips_start")
        return st["chips"]["token"]

    def arrived(phases, after):
        done = _swap_finish([reduce[p]["chips"] for p in phases], name=f"reduce_p{phases[0]}_chips_finish", after=after)
        for p, (_, got) in zip(phases, done):
            for (n, j), own, arr in zip(reduce[p]["names"], reduce[p]["own"], got):
                reduced[n][j] = (own, arr)

    terms = [(1.0, dy)]
    back = list(reversed(groups))
    for p, g in enumerate(back):
        kind, i = g

        def after_first_kernel(dz, p=p):
            return to_chips(p - 1, dz) if p >= 1 else None

        phase = _mix_bwd if kind == "mix" else _ff_bwd
        tokens = [reduce[p - 1]["sibling"]["token"]] if p >= 1 else []
        terms, grads = phase(i, terms, saved[g], layers[i], hook=after_first_kernel, after=tokens)
        for n in SMALL:
            if n in grads:
                small_grads[n][i if n.startswith("ln_") else i // N_MIXERS] = grads[n]
        names = group_names(g)
        parts = [_as_parts(n, grads[n]) for n, _ in names]
        reduce[p] = {"names": names, "sibling": _swap_start(
            parts, [lax.empty((4, *a.shape[1:]), a.dtype) for a in parts], _to_sibling, 4,
            name=f"reduce_p{p}_sibling_start")}
    (_, grad_x), = terms
    last = len(back) - 1
    small_mine = _pack_rows([jnp.stack(small_grads[n]).reshape(PARAMS[n][0]) for n in SMALL], SMALL_WIDTH, SUBLANES, F32)
    small_gather = _gather_start([small_mine], name="gather_small_grads_start")
    tail = to_chips(last, small_gather["token"])
    arrived(list(range(last)), tail)

    grad, delta, new_m, new_v = {}, {}, {}, {}

    def adamw_matrix(n, after):
        shape = weights[n].shape
        view = lambda a: a.reshape(shape[0], shape[1], -1)
        outs = _adamw_reduced(view(weights[n]), view(m_state[n]), view(v_state[n]), reduced[n], name=f"adamw_{n}",
                              after=after)
        grad[n], delta[n], new_m[n], new_v[n] = [o.reshape(shape) for o in outs]
        return outs[0]

    done = tail
    single = [n for n in BIG if PARAMS[n][0][0] == 1 and n != "cf_w_pw1"]
    flat = lambda a: a.reshape(a.shape[1], -1)
    outs = _adamw_reduced_group([flat(weights[n]) for n in single], [flat(m_state[n]) for n in single],
                                [flat(v_state[n]) for n in single], [reduced[n][0] for n in single],
                                name="adamw_single_layer", after=[done])
    for n, quad in zip(single, outs):
        grad[n], delta[n], new_m[n], new_v[n] = [o.reshape(weights[n].shape) for o in quad]
    done = outs[-1][0]
    for n in BIG:
        if n not in MIXER_BIG[0] and n not in single:
            done = adamw_matrix(n, [done])
    small_gather = _gather_pass_on(small_gather, name="gather_small_grads_pass", after=done)
    small_all = _gather_finish(small_gather, name="gather_small_grads_finish", after=small_gather["token"])[0]
    small_sum = _sum_blocks(None, small_all, name="sum_small_grads")
    for n, g in zip(SMALL, _unpack_rows(small_sum, [PARAMS[n][0] for n in SMALL], SMALL_WIDTH, SUBLANES)):
        ax = PARAMS[n][1]
        grad[n] = g if ax is None else lax.dynamic_index_in_dim(_split_shards(g, ax), me, axis=0, keepdims=False)
    small_shards = [_shard_shape(n) for n in SMALL]
    packed = [_pack_rows([src[n] for n in SMALL], SMALL_WIDTH, SUBLANES, F32) for src in (weights, grad, m_state, v_state)]
    small_out = _adamw(*packed, name="adamw_small")
    for outs, dst in zip(small_out, (delta, new_m, new_v)):
        for n, a in zip(SMALL, _unpack_rows(outs, small_shards, SMALL_WIDTH, SUBLANES)):
            dst[n] = a
    arrived([last], small_sum)
    for n in MIXER_BIG[0]:
        adamw_matrix(n, [])
    return (loss, grad_x[None], *[grad[n] for n in NAMES], *[delta[n] for n in NAMES],
            *[new_m[n] for n in NAMES], *[new_v[n] for n in NAMES])


def kernel(x, sc_w_in, sc_conv_w, sc_w_out, mla_w_dq, mla_g_q, mla_w_uq, mla_w_dkv, mla_g_kv, mla_w_uk, mla_w_uv, mla_w_o, cf_w_pw1, cf_b_pw1, cf_dw_w, cf_dw_b, cf_norm_g, cf_norm_b, cf_w_pw2, cf_b_pw2, ff_w1, ff_w2, ln_mix_g, ln_mix_b, ln_ff_g, ln_ff_b, loss_target, m_sc_w_in, m_sc_conv_w, m_sc_w_out, m_mla_w_dq, m_mla_g_q, m_mla_w_uq, m_mla_w_dkv, m_mla_g_kv, m_mla_w_uk, m_mla_w_uv, m_mla_w_o, m_cf_w_pw1, m_cf_b_pw1, m_cf_dw_w, m_cf_dw_b, m_cf_norm_g, m_cf_norm_b, m_cf_w_pw2, m_cf_b_pw2, m_ff_w1, m_ff_w2, m_ln_mix_g, m_ln_mix_b, m_ln_ff_g, m_ln_ff_b, v_sc_w_in, v_sc_conv_w, v_sc_w_out, v_mla_w_dq, v_mla_g_q, v_mla_w_uq, v_mla_w_dkv, v_mla_g_kv, v_mla_w_uk, v_mla_w_uv, v_mla_w_o, v_cf_w_pw1, v_cf_b_pw1, v_cf_dw_w, v_cf_dw_b, v_cf_norm_g, v_cf_norm_b, v_cf_w_pw2, v_cf_b_pw2, v_ff_w1, v_ff_w2, v_ln_mix_g, v_ln_mix_b, v_ln_ff_g, v_ln_ff_b):
    w_list = (sc_w_in, sc_conv_w, sc_w_out, mla_w_dq, mla_g_q, mla_w_uq, mla_w_dkv, mla_g_kv, mla_w_uk, mla_w_uv, mla_w_o,
              cf_w_pw1, cf_b_pw1, cf_dw_w, cf_dw_b, cf_norm_g, cf_norm_b, cf_w_pw2, cf_b_pw2, ff_w1, ff_w2, ln_mix_g,
              ln_mix_b, ln_ff_g, ln_ff_b)
    m_list = (m_sc_w_in, m_sc_conv_w, m_sc_w_out, m_mla_w_dq, m_mla_g_q, m_mla_w_uq, m_mla_w_dkv, m_mla_g_kv, m_mla_w_uk,
              m_mla_w_uv, m_mla_w_o, m_cf_w_pw1, m_cf_b_pw1, m_cf_dw_w, m_cf_dw_b, m_cf_norm_g, m_cf_norm_b, m_cf_w_pw2,
              m_cf_b_pw2, m_ff_w1, m_ff_w2, m_ln_mix_g, m_ln_mix_b, m_ln_ff_g, m_ln_ff_b)
    v_list = (v_sc_w_in, v_sc_conv_w, v_sc_w_out, v_mla_w_dq, v_mla_g_q, v_mla_w_uq, v_mla_w_dkv, v_mla_g_kv, v_mla_w_uk,
              v_mla_w_uv, v_mla_w_o, v_cf_w_pw1, v_cf_b_pw1, v_cf_dw_w, v_cf_dw_b, v_cf_norm_g, v_cf_norm_b, v_cf_w_pw2,
              v_cf_b_pw2, v_ff_w1, v_ff_w2, v_ln_mix_g, v_ln_mix_b, v_ln_ff_g, v_ln_ff_b)
    return _train_step(x, loss_target, dict(zip(NAMES, w_list)), dict(zip(NAMES, m_list)), dict(zip(NAMES, v_list)))
```

```python
import jax
import jax.numpy as jnp
from jax import lax
from jax.experimental import pallas as pl
from jax.experimental.pallas import tpu as pltpu

F32 = jnp.float32
MXU_DTYPE = jnp.bfloat16

N_DEV = 8
D_MODEL = 1024
DEPTH = 4
N_MIXERS = 3
CHUNK = 64
ALPHA = (2.0 * DEPTH) ** 0.25
LN_EPS = 1e-5
RMS_EPS = 1e-6
MLA_HEADS = 8
QK_NOPE = 128
QK_ROPE = 64
V_HEAD = 128
Q_LORA = 384
KV_LORA = 256
ROPE_THETA = 10000.0
ADAM_LR = 0.001
ADAM_B1 = 0.9
ADAM_B2 = 0.999
ADAM_EPS = 1e-08
ADAM_WD = 0.01
ADAM_STEP = 10

LANES = 128
SUBLANES = 8
BF16_ROWS = 16
VMEM_LIMIT = 56 * 1024 * 1024
MM_VMEM_BUDGET = 36 * 1024 * 1024
NARROW_ROWS = 1024
CONV_PAD = 32
MASK_VALUE = -1e30

MESH_AXES = ("x", "y", "c")


def _params(semantics):
    return pltpu.CompilerParams(dimension_semantics=semantics, vmem_limit_bytes=VMEM_LIMIT)


def _tile(n, cap):
    if n <= cap:
        return n
    t = cap - cap % LANES
    while t >= LANES:
        if n % t == 0:
            return t
        t -= LANES
    raise ValueError(f"no tile for {n} under {cap}")


def _row_tile(rows, cap):
    if rows <= cap:
        return rows
    t = cap - cap % SUBLANES
    while t >= SUBLANES:
        if rows % t == 0:
            return t
        t -= SUBLANES
    raise ValueError(f"no row tile for {rows} under {cap}")


def _mm(a, b, *, name, ta=False, tb=False, b_dev=False, out_dev=False, out_dtypes=(F32,), epilogue=None,
        tile_extras=(), col_extras=(), after=(), tm_cap=2048, tn_cap=1024, tk_cap=4096):
    m, k = (a.shape[1], a.shape[0]) if ta else a.shape
    if b_dev:
        nd, b_rows, ns = b.shape
        n, kb = (b_rows, nd * ns) if tb else (nd * ns, b_rows)
    else:
        n, kb = (b.shape[0], b.shape[1]) if tb else (b.shape[1], b.shape[0])
    assert k == kb, (a.shape, b.shape, ta, tb)
    if ta or tile_extras or (b_dev and tb):
        tm_cap = min(tm_cap, 1024)
    tk = ns if (b_dev and tb) else _tile(k, tk_cap)
    nk = k // tk
    fixed_tn = ns if (b_dev and not tb) else (n // N_DEV if out_dev else None)
    if out_dev:
        assert not tile_extras and not col_extras and n % N_DEV == 0
    out_bytes = sum(jnp.dtype(dt).itemsize for dt in out_dtypes) + sum(e.dtype.itemsize for e in tile_extras)
    tm, tn = None, None
    for cand_m, cand_n in ((tm_cap, tn_cap), (tm_cap, tn_cap // 2), (tm_cap // 2, tn_cap), (tm_cap // 2, tn_cap // 2),
                           (tm_cap // 4, tn_cap // 2), (tm_cap // 4, tn_cap // 4)):
        tm, tn = _tile(m, cand_m), (fixed_tn if fixed_tn is not None else _tile(n, cand_n))
        need = 2 * (tm * tk * a.dtype.itemsize + tk * tn * b.dtype.itemsize) + 2 * tm * tn * out_bytes
        need += tm * tn * 4 * (2 if nk > 1 else 1)
        if need <= MM_VMEM_BUDGET:
            break
    n_te, n_ce, n_out = len(tile_extras), len(col_extras), len(out_dtypes)
    dims = (((0 if ta else 1,), (1 if tb else 0,)), ((), ()))

    def body(*refs):
        a_ref, b_ref = refs[0], refs[1]
        te_refs = refs[2:2 + n_te]
        ce_refs = refs[2 + n_te:2 + n_te + n_ce]
        first_out = 2 + n_te + n_ce + len(after)
        out_refs = refs[first_out:first_out + n_out]

        def finish(acc):
            if epilogue is None:
                outs = (acc,)
            else:
                outs = epilogue(acc, *[r[...] for r in te_refs], *[r[...] for r in ce_refs])
            for o_ref, o in zip(out_refs, outs):
                o_ref[...] = o.astype(o_ref.dtype)

        def product():
            return lax.dot_general(a_ref[...].astype(MXU_DTYPE), b_ref[...].astype(MXU_DTYPE), dims,
                                   preferred_element_type=F32)

        if nk == 1:
            finish(product())
        else:
            acc_ref = refs[-1]
            kk = pl.program_id(2)

            @pl.when(kk == 0)
            def _():
                acc_ref[...] = jnp.zeros_like(acc_ref)

            acc_ref[...] += product()

            @pl.when(kk == nk - 1)
            def _():
                finish(acc_ref[...])

    a_spec = pl.BlockSpec((tk, tm), lambda i, j, kk: (kk, i)) if ta else pl.BlockSpec((tm, tk), lambda i, j, kk: (i, kk))
    if b_dev and tb:
        b_spec = pl.BlockSpec((None, tn, tk), lambda i, j, kk: (kk, j, 0))
    elif b_dev:
        b_spec = pl.BlockSpec((None, tk, tn), lambda i, j, kk: (j, kk, 0))
    elif tb:
        b_spec = pl.BlockSpec((tn, tk), lambda i, j, kk: (j, kk))
    else:
        b_spec = pl.BlockSpec((tk, tn), lambda i, j, kk: (kk, j))
    tile_spec = pl.BlockSpec((tm, tn), lambda i, j, kk: (i, j))
    col_spec = pl.BlockSpec((1, tn), lambda i, j, kk: (0, j))
    if out_dev:
        out_spec = pl.BlockSpec((None, tm, tn), lambda i, j, kk: (j, i, 0))
        out_shape = [jax.ShapeDtypeStruct((N_DEV, m, tn), dt) for dt in out_dtypes]
    else:
        out_spec = tile_spec
        out_shape = [jax.ShapeDtypeStruct((m, n), dt) for dt in out_dtypes]
    outs = pl.pallas_call(
        body,
        name=name,
        grid=(m // tm, n // tn, nk),
        in_specs=[a_spec, b_spec] + [tile_spec] * n_te + [col_spec] * n_ce + [ANY] * len(after),
        out_specs=[out_spec] * n_out,
        out_shape=out_shape,
        scratch_shapes=[pltpu.VMEM((tm, tn), F32)] if nk > 1 else [],
        compiler_params=_params(("parallel", "parallel", "arbitrary")),
    )(a, b, *tile_extras, *col_extras, *after)
    return outs[0] if n_out == 1 else tuple(outs)


def _rowwise(fn, rows, bcast, out_rows, out_accs, *, name, tb_cap=256, after=()):
    t = rows[0].shape[0]
    tb = _row_tile(t, tb_cap)
    n_r, n_b, n_o, n_a = len(rows), len(bcast), len(out_rows), len(out_accs)

    def body(*refs):
        r_refs = refs[:n_r]
        b_refs = refs[n_r:n_r + n_b]
        first_out = n_r + n_b + len(after)
        o_refs = refs[first_out:first_out + n_o]
        a_refs = refs[first_out + n_o:]
        outs, accs = fn([r[...] for r in r_refs], [r[...] for r in b_refs])
        for o_ref, o in zip(o_refs, outs):
            o_ref[...] = o.astype(o_ref.dtype)
        if n_a:
            i = pl.program_id(0)

            @pl.when(i == 0)
            def _():
                for a_ref, acc in zip(a_refs, accs):
                    a_ref[...] = acc

            @pl.when(i > 0)
            def _():
                for a_ref, acc in zip(a_refs, accs):
                    a_ref[...] += acc

    in_specs = [pl.BlockSpec((tb, r.shape[1]), lambda i: (i, 0)) for r in rows]
    in_specs += [pl.BlockSpec(b.shape, lambda i: (0, 0)) for b in bcast]
    in_specs += [ANY] * len(after)
    out_specs = [pl.BlockSpec((tb, c), lambda i: (i, 0)) for c, _ in out_rows]
    out_specs += [pl.BlockSpec(s, lambda i: (0, 0)) for s in out_accs]
    out_shape = [jax.ShapeDtypeStruct((t, c), dt) for c, dt in out_rows]
    out_shape += [jax.ShapeDtypeStruct(s, F32) for s in out_accs]
    outs = pl.pallas_call(
        body,
        name=name,
        grid=(t // tb,),
        in_specs=in_specs,
        out_specs=out_specs,
        out_shape=out_shape,
        compiler_params=_params(("arbitrary",)),
    )(*rows, *bcast, *after)
    return tuple(outs)


def _colsum(v):
    return jnp.sum(v, axis=0, keepdims=True)


def _sigmoid(v):
    return 1.0 / (1.0 + jnp.exp(-v))


def _ln_fwd(x_res, y_sub, g, b, *, name, after=()):
    d = x_res.shape[1]

    def fn(rows, bc):
        z = ALPHA * rows[0] + rows[1]
        mu = jnp.mean(z, axis=-1, keepdims=True)
        zc = z - mu
        var = jnp.mean(zc * zc, axis=-1, keepdims=True)
        rstd = lax.rsqrt(var + LN_EPS)
        xhat = zc * rstd
        out = xhat * bc[0] + bc[1]
        return (out, out, xhat, rstd), ()

    return _rowwise(fn, [x_res, y_sub], [g, b], [(d, F32), (d, MXU_DTYPE), (d, F32), (1, F32)], [], name=name,
                    after=after, tb_cap=512)


def _ln_fwd_loss(x_res, y_sub, g, b, target, *, name, after=()):
    d = x_res.shape[1]

    def fn(rows, bc):
        z = ALPHA * rows[0] + rows[1]
        mu = jnp.mean(z, axis=-1, keepdims=True)
        zc = z - mu
        var = jnp.mean(zc * zc, axis=-1, keepdims=True)
        rstd = lax.rsqrt(var + LN_EPS)
        xhat = zc * rstd
        err = xhat * bc[0] + bc[1] - rows[2]
        return (xhat, rstd, err * (1.0 / d)), (_colsum(err * err),)

    return _rowwise(fn, [x_res, y_sub, target], [g, b], [(d, F32), (1, F32), (d, F32)], [(1, d)], name=name,
                    after=after, tb_cap=512)


def _ln_bwd(terms, xhat, rstd, g, *, name, after=()):
    d = xhat.shape[1]
    coefs = [c for c, _ in terms]

    def fn(rows, bc):
        xh, rs = rows[0], rows[1]
        dout = None
        for c, v in zip(coefs, rows[2:]):
            tv = v if c == 1.0 else c * v
            dout = tv if dout is None else dout + tv
        dxh = dout * bc[0]
        m1 = jnp.mean(dxh, axis=-1, keepdims=True)
        m2 = jnp.mean(dxh * xh, axis=-1, keepdims=True)
        dz = rs * (dxh - m1 - xh * m2)
        return (dz, dz), (_colsum(dout * xh), _colsum(dout), _colsum(dz))

    return _rowwise(fn, [xhat, rstd] + [v for _, v in terms], [g], [(d, F32), (d, MXU_DTYPE)],
                    [(1, d), (1, d), (1, d)], name=name, after=after, tb_cap=512)


def _rms_fwd(x, g, *, name):
    c = x.shape[1]

    def fn(rows, bc):
        v = rows[0]
        r = lax.rsqrt(jnp.mean(v * v, axis=-1, keepdims=True) + RMS_EPS)
        return (v * r * bc[0],), ()

    return _rowwise(fn, [x], [g], [(c, MXU_DTYPE)], [], name=name, tb_cap=NARROW_ROWS)[0]


def _rms_bwd(x, g, dy, *, name):
    c = x.shape[1]

    def fn(rows, bc):
        v, dyv = rows
        r = lax.rsqrt(jnp.mean(v * v, axis=-1, keepdims=True) + RMS_EPS)
        xh = v * r
        dxh = dyv * bc[0]
        dx = r * (dxh - xh * jnp.mean(dxh * xh, axis=-1, keepdims=True))
        return (dx,), (_colsum(dyv * xh),)

    return _rowwise(fn, [x, dy], [g], [(c, F32)], [(1, c)], name=name, tb_cap=NARROW_ROWS)


def _conv_chunk(t):
    return _row_tile(t, 256)


ATTN_SCALE = (QK_NOPE + QK_ROPE) ** -0.5
_NT = (((1,), (1,)), ((), ()))
_TN = (((0,), (0,)), ((), ()))
_NN = (((1,), (0,)), ((), ()))


def _dot(a, b, dims):
    return lax.dot_general(a, b, dims, preferred_element_type=F32)


def _block_scores(q_cat, kn_ref, kp_ref, lo, hi, diagonal):
    k_cat = jnp.concatenate([kn_ref[lo:hi, :], kp_ref[lo:hi, :]], axis=1)
    s = _dot(q_cat, k_cat, _NT) * ATTN_SCALE
    if diagonal:
        shift = CHUNK.bit_length() - 1
        rows = lax.broadcasted_iota(jnp.int32, s.shape, 0)
        cols = lax.broadcasted_iota(jnp.int32, s.shape, 1)
        s = jnp.where(jnp.right_shift(cols, shift) <= jnp.right_shift(rows, shift), s, MASK_VALUE)
    return s, k_cat


def _attn_fwd(qn, qp, kv, kp, *, name):
    h, t, _ = qp.shape
    tq = _row_tile(t, 256)
    assert tq % CHUNK == 0

    def body(qn_ref, qp_ref, kn_ref, kp_ref, v_ref, o_ref, of_ref, lse_ref):
        for r0 in range(0, t, tq):
            r1 = r0 + tq
            q_cat = jnp.concatenate([qn_ref[r0:r1, :].astype(MXU_DTYPE), qp_ref[r0:r1, :]], axis=1)
            s_d, _ = _block_scores(q_cat, kn_ref, kp_ref, r0, r1, True)
            m = jnp.max(s_d, axis=-1, keepdims=True)
            if r0:
                s_b, _ = _block_scores(q_cat, kn_ref, kp_ref, 0, r0, False)
                m = jnp.maximum(m, jnp.max(s_b, axis=-1, keepdims=True))
            p_d = jnp.exp(s_d - m)
            l = jnp.sum(p_d, axis=-1, keepdims=True)
            acc = _dot(p_d.astype(MXU_DTYPE), v_ref[r0:r1, :], _NN)
            if r0:
                p_b = jnp.exp(s_b - m)
                l = l + jnp.sum(p_b, axis=-1, keepdims=True)
                acc = acc + _dot(p_b.astype(MXU_DTYPE), v_ref[0:r0, :], _NN)
            o = acc / l
            o_ref[r0:r1, :] = o.astype(o_ref.dtype)
            of_ref[r0:r1, :] = o
            lse_ref[r0:r1, :] = m + jnp.log(l)

    blk = lambda w: pl.BlockSpec((None, t, w), lambda hh: (hh, 0, 0))
    cols = pl.BlockSpec((t, QK_NOPE), lambda hh: (0, hh))
    keys = pl.BlockSpec((t, QK_NOPE), lambda hh: (0, 2 * hh))
    values = pl.BlockSpec((t, V_HEAD), lambda hh: (0, 2 * hh + 1))
    return pl.pallas_call(
        body,
        name=name,
        grid=(h,),
        in_specs=[cols, blk(QK_ROPE), keys, pl.BlockSpec((t, QK_ROPE), lambda hh: (0, 0)), values],
        out_specs=[cols, cols, blk(1)],
        out_shape=[jax.ShapeDtypeStruct((t, h * V_HEAD), MXU_DTYPE), jax.ShapeDtypeStruct((t, h * V_HEAD), F32),
                   jax.ShapeDtypeStruct((h, t, 1), F32)],
        compiler_params=_params(("parallel",)),
    )(qn, qp, kv, kp, kv)


def _attn_bwd(qn, qp, kv, kp, do, o, lse, *, name):
    h, t, _ = qp.shape
    tq = _row_tile(t, 256)

    def body(qn_ref, qp_ref, kn_ref, kp_ref, v_ref, do_ref, o_ref, lse_ref, dqn_ref, dqp_ref, dkv_ref, dkp_ref):
        dkv_ref[...] = jnp.zeros_like(dkv_ref)

        @pl.when(pl.program_id(0) == 0)
        def _():
            dkp_ref[...] = jnp.zeros_like(dkp_ref)

        for r0 in range(0, t, tq):
            r1 = r0 + tq
            q_cat = jnp.concatenate([qn_ref[r0:r1, :].astype(MXU_DTYPE), qp_ref[r0:r1, :]], axis=1)
            dov, lse = do_ref[r0:r1, :], lse_ref[r0:r1, :]
            delta = jnp.sum(dov.astype(F32) * o_ref[r0:r1, :], axis=-1, keepdims=True)

            def piece(lo, hi, diagonal):
                s, k_cat = _block_scores(q_cat, kn_ref, kp_ref, lo, hi, diagonal)
                p = jnp.exp(s - lse)
                dp = _dot(dov, v_ref[lo:hi, :], _NT)
                ds_m = (p * (dp - delta) * ATTN_SCALE).astype(MXU_DTYPE)
                dk_cat = _dot(ds_m, q_cat, _TN)
                dkv_ref[lo:hi, :QK_NOPE] += dk_cat[:, :QK_NOPE]
                dkp_ref[lo:hi, :] += dk_cat[:, QK_NOPE:]
                dkv_ref[lo:hi, QK_NOPE:] += _dot(p.astype(MXU_DTYPE), dov, _TN)
                return _dot(ds_m, k_cat, _NN)

            dq_cat = piece(r0, r1, True)
            if r0:
                dq_cat = dq_cat + piece(0, r0, False)
            dqn_ref[r0:r1, :] = dq_cat[:, :QK_NOPE]
            dqp_ref[r0:r1, :] = dq_cat[:, QK_NOPE:]

    blk = lambda w: pl.BlockSpec((None, t, w), lambda hh: (hh, 0, 0))
    kpblk = pl.BlockSpec((t, QK_ROPE), lambda hh: (0, 0))
    cols = pl.BlockSpec((t, QK_NOPE), lambda hh: (0, hh))
    keys = pl.BlockSpec((t, QK_NOPE), lambda hh: (0, 2 * hh))
    values = pl.BlockSpec((t, V_HEAD), lambda hh: (0, 2 * hh + 1))
    both = pl.BlockSpec((t, QK_NOPE + V_HEAD), lambda hh: (0, hh))
    return pl.pallas_call(
        body,
        name=name,
        grid=(h,),
        in_specs=[cols, blk(QK_ROPE), keys, kpblk, values, cols, cols, blk(1)],
        out_specs=[cols, blk(QK_ROPE), both, kpblk],
        out_shape=[jax.ShapeDtypeStruct((t, h * QK_NOPE), F32), jax.ShapeDtypeStruct((h, t, QK_ROPE), F32),
                   jax.ShapeDtypeStruct((t, h * (QK_NOPE + V_HEAD)), F32), jax.ShapeDtypeStruct((t, QK_ROPE), F32)],
        compiler_params=_params(("arbitrary",)),
    )(qn, qp, kv, kp, kv, do, o, lse)


def _rope(x1, x2, cos, sin, *, name, inverse=False):
    w = x1.shape[1]
    sign = -1.0 if inverse else 1.0

    def fn(rows, bc):
        a, b, c, s = rows
        s = sign * s
        return (a * c - b * s, a * s + b * c), ()

    return _rowwise(fn, [x1, x2, cos, sin], [], [(w, F32), (w, F32)], [], name=name, tb_cap=NARROW_ROWS)


def _rope_in_place(x, cos_pairs, sin_signed, *, name, inverse=False):
    w = x.shape[1]
    half = ROPE_HALF
    sign = -1.0 if inverse else 1.0

    def fn(rows, bc):
        v, c, s = rows
        lane = lax.broadcasted_iota(jnp.int32, v.shape, 1)
        partner = jnp.where((lane & half) == 0, pltpu.roll(v, w - half, 1), pltpu.roll(v, half, 1))
        return (v * c + partner * (sign * s),), ()

    return _rowwise(fn, [x, cos_pairs, sin_signed], [], [(w, F32)], [], name=name, tb_cap=NARROW_ROWS)[0]


def _mlp_up(x_m, w1, tag, after=()):
    def epi(acc):
        r = jnp.maximum(acc, 0.0)
        return (r * r,)

    return _mm(x_m, w1, name=f"{tag}_up", out_dtypes=(MXU_DTYPE,), epilogue=epi, after=after)


def _mlp_fwd(x_m, w1, w2, tag, after=(), act=None):
    if act is None:
        act, after = _mlp_up(x_m, w1, tag, after=after), ()
    y = _mm(act, w2, name=f"{tag}_down", after=after)
    return y, (x_m, act)


def _plus_residual(acc, res):
    return (acc + ALPHA * res,)


def _mlp_bwd(dy_m, saved, w1, w2, tag, residual, after=()):
    x_m, act = saved
    dw2 = _mm(act, dy_m, ta=True, name=f"{tag}_dw2", out_dtypes=(MXU_DTYPE,))

    def epi(acc, av):
        return (acc * (2.0 * jnp.sqrt(av.astype(F32))),)

    dh = _mm(dy_m, w2, tb=True, name=f"{tag}_dact", out_dtypes=(MXU_DTYPE,), epilogue=epi, tile_extras=(act,),
             after=after)
    dw1 = _mm(x_m, dh, ta=True, out_dev=True, name=f"{tag}_dw1", out_dtypes=(MXU_DTYPE,))
    dx = _mm(dh, w1, tb=True, name=f"{tag}_dx", epilogue=_plus_residual, tile_extras=(residual,))
    return [dx], {"ff_w1": dw1, "ff_w2": dw2}


def _sc_gate_specs(t, d):
    per = d // LANES
    return [pl.BlockSpec((t, LANES), lambda j, k=k: (0, k * per + j)) for k in range(3)]


def _sc_gate_fwd(u, conv_w, *, name):
    t, d = u.shape[0], u.shape[1] // 3
    kw = conv_w.shape[0]
    ch = _conv_chunk(t)

    def body(b_ref, c_ref, h_ref, w_ref, r_ref, pad_ref):
        pad_ref[0:CONV_PAD, :] = jnp.zeros((CONV_PAD, LANES), F32)
        pad_ref[CONV_PAD:CONV_PAD + t, :] = c_ref[...] * h_ref[...]
        for t0 in range(0, t, ch):
            q = jnp.zeros((ch, LANES), F32)
            for k in range(kw):
                s = kw - 1 - k
                q = q + w_ref[k:k + 1, :] * pad_ref[CONV_PAD - s + t0:CONV_PAD - s + t0 + ch, :]
            r_ref[t0:t0 + ch, :] = (b_ref[t0:t0 + ch, :] * q).astype(r_ref.dtype)

    blk = pl.BlockSpec((t, LANES), lambda j: (0, j))
    return pl.pallas_call(
        body,
        name=name,
        grid=(d // LANES,),
        in_specs=_sc_gate_specs(t, d) + [pl.BlockSpec((kw, LANES), lambda j: (0, j))],
        out_specs=blk,
        out_shape=jax.ShapeDtypeStruct((t, d), MXU_DTYPE),
        scratch_shapes=[pltpu.VMEM((CONV_PAD + t, LANES), F32)],
        compiler_params=_params(("parallel",)),
    )(u, u, u, conv_w)


def _sc_gate_bwd(dr, u, conv_w, *, name):
    t, d = u.shape[0], u.shape[1] // 3
    kw = conv_w.shape[0]
    ch = _conv_chunk(t)

    def body(dr_ref, b_ref, c_ref, h_ref, w_ref, db_ref, dc_ref, dh_ref, dw_ref, ppad_ref, dpad_ref):
        ppad_ref[0:CONV_PAD, :] = jnp.zeros((CONV_PAD, LANES), F32)
        ppad_ref[CONV_PAD:CONV_PAD + t, :] = c_ref[...] * h_ref[...]
        dpad_ref[0:t, :] = dr_ref[...] * b_ref[...]
        dpad_ref[t:t + CONV_PAD, :] = jnp.zeros((CONV_PAD, LANES), F32)
        for t0 in range(0, t, ch):
            q = jnp.zeros((ch, LANES), F32)
            dp = jnp.zeros((ch, LANES), F32)
            for k in range(kw):
                s = kw - 1 - k
                q = q + w_ref[k:k + 1, :] * ppad_ref[CONV_PAD - s + t0:CONV_PAD - s + t0 + ch, :]
                dp = dp + w_ref[k:k + 1, :] * dpad_ref[t0 + s:t0 + s + ch, :]
            db_ref[t0:t0 + ch, :] = (dr_ref[t0:t0 + ch, :] * q).astype(db_ref.dtype)
            dc_ref[t0:t0 + ch, :] = (dp * h_ref[t0:t0 + ch, :]).astype(dc_ref.dtype)
            dh_ref[t0:t0 + ch, :] = (dp * c_ref[t0:t0 + ch, :]).astype(dh_ref.dtype)
        for k in range(kw):
            s = kw - 1 - k
            acc = jnp.zeros((ch, LANES), F32)
            for t0 in range(0, t, ch):
                acc = acc + dpad_ref[t0:t0 + ch, :] * ppad_ref[CONV_PAD - s + t0:CONV_PAD - s + t0 + ch, :]
            dw_ref[k:k + 1, :] = _colsum(acc)

    blk = pl.BlockSpec((t, LANES), lambda j: (0, j))
    wblk = pl.BlockSpec((kw, LANES), lambda j: (0, j))
    return pl.pallas_call(
        body,
        name=name,
        grid=(d // LANES,),
        in_specs=[blk] + _sc_gate_specs(t, d) + [wblk],
        out_specs=[blk, blk, blk, wblk],
        out_shape=[jax.ShapeDtypeStruct((t, d), MXU_DTYPE)] * 3 + [jax.ShapeDtypeStruct((kw, d), F32)],
        scratch_shapes=[pltpu.VMEM((CONV_PAD + t, LANES), F32), pltpu.VMEM((CONV_PAD + t, LANES), F32)],
        compiler_params=_params(("parallel",)),
    )(dr, u, u, u, conv_w)


def _sc_fwd(x_m, w_in, conv_w, w_out, tag, after=()):
    u = _mm(x_m, w_in, name=f"{tag}_in", after=after)
    r = _sc_gate_fwd(u, conv_w, name=f"{tag}_gate")
    y = _mm(r, w_out, name=f"{tag}_out")
    return y, (x_m, u, r)


def _sc_bwd(dy_m, saved, w_in, conv_w, w_out, tag, residual, after=()):
    d = D_MODEL
    x_m, u, r = saved
    dw_out = _mm(r, dy_m, ta=True, name=f"{tag}_dwout", out_dtypes=(MXU_DTYPE,))
    dr = _mm(dy_m, w_out, tb=True, name=f"{tag}_dr", after=after)
    db, dc, dh, dconv = _sc_gate_bwd(dr, u, conv_w, name=f"{tag}_dgate")
    du = jnp.concatenate([db, dc, dh], axis=1)
    dw_in = _mm(x_m, du, ta=True, out_dev=True, name=f"{tag}_dwin", out_dtypes=(MXU_DTYPE,))
    dx = _mm(du, w_in, tb=True, name=f"{tag}_dx", epilogue=_plus_residual, tile_extras=(residual,))
    return [dx], {"sc_w_in": dw_in, "sc_conv_w": dconv, "sc_w_out": dw_out}


def _cf_glu_specs(t, d):
    per = d // LANES
    return [pl.BlockSpec((t, LANES), lambda j, k=k: (0, k * per + j)) for k in range(2)]


def _cf_glu_conv_fwd(u, w, bias, *, name):
    t, d = u.shape[0], u.shape[1] // 2
    kw = w.shape[0]
    ch = _conv_chunk(t)

    def body(a_ref, g_ref, w_ref, b_ref, o_ref, pad_ref):
        pad_ref[0:CONV_PAD, :] = jnp.zeros((CONV_PAD, LANES), F32)
        pad_ref[CONV_PAD:CONV_PAD + t, :] = a_ref[...] * _sigmoid(g_ref[...])
        for t0 in range(0, t, ch):
            acc = jnp.broadcast_to(b_ref[...], (ch, LANES))
            for k in range(kw):
                s = kw - 1 - k
                acc = acc + w_ref[k:k + 1, :] * pad_ref[CONV_PAD - s + t0:CONV_PAD - s + t0 + ch, :]
            o_ref[t0:t0 + ch, :] = acc

    return pl.pallas_call(
        body,
        name=name,
        grid=(d // LANES,),
        in_specs=_cf_glu_specs(t, d) + [pl.BlockSpec((kw, LANES), lambda j: (0, j)),
                                        pl.BlockSpec((1, LANES), lambda j: (0, j))],
        out_specs=pl.BlockSpec((t, LANES), lambda j: (0, j)),
        out_shape=jax.ShapeDtypeStruct((t, d), F32),
        scratch_shapes=[pltpu.VMEM((CONV_PAD + t, LANES), F32)],
        compiler_params=_params(("parallel",)),
    )(u, u, w, bias)


def _cf_glu_conv_bwd(dy, u, w, *, name):
    t, d = u.shape[0], u.shape[1] // 2
    kw = w.shape[0]
    ch = _conv_chunk(t)

    def body(dy_ref, a_ref, g_ref, w_ref, da_ref, dg_ref, sa_ref, sg_ref, dw_ref, xpad_ref, dpad_ref):
        xpad_ref[0:CONV_PAD, :] = jnp.zeros((CONV_PAD, LANES), F32)
        xpad_ref[CONV_PAD:CONV_PAD + t, :] = a_ref[...] * _sigmoid(g_ref[...])
        dpad_ref[0:t, :] = dy_ref[...]
        dpad_ref[t:t + CONV_PAD, :] = jnp.zeros((CONV_PAD, LANES), F32)
        sum_a = jnp.zeros((1, LANES), F32)
        sum_g = jnp.zeros((1, LANES), F32)
        for t0 in range(0, t, ch):
            dh1 = jnp.zeros((ch, LANES), F32)
            for k in range(kw):
                s = kw - 1 - k
                dh1 = dh1 + w_ref[k:k + 1, :] * dpad_ref[t0 + s:t0 + s + ch, :]
            sg = _sigmoid(g_ref[t0:t0 + ch, :])
            da = dh1 * sg
            dgate = da * a_ref[t0:t0 + ch, :] * (1.0 - sg)
            da_ref[t0:t0 + ch, :] = da.astype(da_ref.dtype)
            dg_ref[t0:t0 + ch, :] = dgate.astype(dg_ref.dtype)
            sum_a = sum_a + _colsum(da)
            sum_g = sum_g + _colsum(dgate)
        sa_ref[...] = sum_a
        sg_ref[...] = sum_g
        for k in range(kw):
            s = kw - 1 - k
            acc = jnp.zeros((ch, LANES), F32)
            for t0 in range(0, t, ch):
                acc = acc + dy_ref[t0:t0 + ch, :] * xpad_ref[CONV_PAD - s + t0:CONV_PAD - s + t0 + ch, :]
            dw_ref[k:k + 1, :] = _colsum(acc)

    blk = pl.BlockSpec((t, LANES), lambda j: (0, j))
    row = pl.BlockSpec((1, LANES), lambda j: (0, j))
    wblk = pl.BlockSpec((kw, LANES), lambda j: (0, j))
    return pl.pallas_call(
        body,
        name=name,
        grid=(d // LANES,),
        in_specs=[blk] + _cf_glu_specs(t, d) + [wblk],
        out_specs=[blk, blk, row, row, wblk],
        out_shape=[jax.ShapeDtypeStruct((t, d), MXU_DTYPE)] * 2 + [jax.ShapeDtypeStruct((1, d), F32)] * 2
        + [jax.ShapeDtypeStruct((kw, d), F32)],
        scratch_shapes=[pltpu.VMEM((CONV_PAD + t, LANES), F32), pltpu.VMEM((CONV_PAD + t, LANES), F32)],
        compiler_params=_params(("parallel",)),
    )(dy, u, u, w)


def _cf_fwd(x_m, w, tag, after=()):
    d = D_MODEL

    def bias_epi(acc, bias):
        return (acc + bias,)

    u = _mm(x_m, w["cf_w_pw1"], name=f"{tag}_pw1", epilogue=bias_epi, col_extras=(w["cf_b_pw1"],),
            after=after)
    h2 = _cf_glu_conv_fwd(u, w["cf_dw_w"], w["cf_dw_b"], name=f"{tag}_conv")

    def norm_fn(rows, bc):
        hv = rows[0]
        mu = jnp.mean(hv, axis=-1, keepdims=True)
        hc = hv - mu
        rstd = lax.rsqrt(jnp.mean(hc * hc, axis=-1, keepdims=True) + LN_EPS)
        h3 = hc * rstd * bc[0] + bc[1]
        return (h3 * _sigmoid(h3),), ()

    h4 = _rowwise(norm_fn, [h2], [w["cf_norm_g"], w["cf_norm_b"]], [(d, MXU_DTYPE)], [], name=f"{tag}_norm")[0]
    y = _mm(h4, w["cf_w_pw2"], name=f"{tag}_pw2", epilogue=bias_epi, col_extras=(w["cf_b_pw2"],))
    return y, (x_m, u, h2, h4)


def _cf_bwd(dy_m, dy_colsum, saved, w, tag, residual, after=()):
    d = D_MODEL
    x_m, u, h2, h4 = saved
    dw_pw2 = _mm(h4, dy_m, ta=True, name=f"{tag}_dwpw2", out_dtypes=(MXU_DTYPE,))
    dh4 = _mm(dy_m, w["cf_w_pw2"], tb=True, name=f"{tag}_dh4", after=after)

    def dnorm_fn(rows, bc):
        dh4v, hv = rows
        g, b = bc
        mu = jnp.mean(hv, axis=-1, keepdims=True)
        hc = hv - mu
        rstd = lax.rsqrt(jnp.mean(hc * hc, axis=-1, keepdims=True) + LN_EPS)
        xh = hc * rstd
        h3 = xh * g + b
        sg = _sigmoid(h3)
        dh3 = dh4v * (sg * (1.0 + h3 * (1.0 - sg)))
        dxh = dh3 * g
        dh2 = rstd * (dxh - jnp.mean(dxh, axis=-1, keepdims=True) - xh * jnp.mean(dxh * xh, axis=-1, keepdims=True))
        return (dh2,), (_colsum(dh3 * xh), _colsum(dh3), _colsum(dh2))

    dh2, dnorm_g, dnorm_b, ddw_b = _rowwise(dnorm_fn, [dh4, h2], [w["cf_norm_g"], w["cf_norm_b"]], [(d, F32)],
                                            [(1, d), (1, d), (1, d)], name=f"{tag}_dnorm")
    da, dgate, sum_a, sum_g, ddw_w = _cf_glu_conv_bwd(dh2, u, w["cf_dw_w"], name=f"{tag}_dconv")
    du = jnp.concatenate([da, dgate], axis=1)
    db_pw1 = jnp.concatenate([sum_a, sum_g], axis=1)
    dw_pw1 = _mm(x_m, du, ta=True, out_dev=True, name=f"{tag}_dwpw1", out_dtypes=(MXU_DTYPE,))
    dx = _mm(du, w["cf_w_pw1"], tb=True, name=f"{tag}_dx", epilogue=_plus_residual, tile_extras=(residual,))
    grads = {"cf_w_pw1": dw_pw1, "cf_b_pw1": db_pw1, "cf_dw_w": ddw_w, "cf_dw_b": ddw_b, "cf_norm_g": dnorm_g,
             "cf_norm_b": dnorm_b, "cf_w_pw2": dw_pw2, "cf_b_pw2": dy_colsum}
    return [dx], grads


def _rope_tables(t):
    pos = jnp.arange(t, dtype=F32)
    inv_freq = ROPE_THETA ** (-jnp.arange(0, QK_ROPE, 2, dtype=F32) / QK_ROPE)
    ang = pos[:, None] * inv_freq[None, :]
    return jnp.cos(ang), jnp.sin(ang)


ROPE_HALF = QK_ROPE // 2
Q_NOPE_COLS = MLA_HEADS * QK_NOPE


def _interleave_heads(w_uk, w_uv):
    c = w_uk.shape[0]
    return jnp.concatenate([w_uk.reshape(c, MLA_HEADS, QK_NOPE), w_uv.reshape(c, MLA_HEADS, V_HEAD)],
                           axis=2).reshape(c, -1)


def _mla_fwd(x_m, w, tag, after=()):
    t = x_m.shape[0]
    cos, sin = _rope_tables(t)
    cos_h = jnp.tile(jnp.concatenate([cos, cos], axis=1), (1, MLA_HEADS))
    sin_h = jnp.tile(jnp.concatenate([-sin, sin], axis=1), (1, MLA_HEADS))
    cq_pre = _mm(x_m, w["mla_w_dq"], name=f"{tag}_dq", after=after)
    cq = _rms_fwd(cq_pre, w["mla_g_q"], name=f"{tag}_qnorm")
    q = _mm(cq, w["mla_w_uq"], name=f"{tag}_uq")
    qp = _rope_in_place(q[:, Q_NOPE_COLS:], cos_h, sin_h, name=f"{tag}_qrope")
    qp_h = qp.reshape(t, MLA_HEADS, QK_ROPE).transpose(1, 0, 2).astype(MXU_DTYPE)
    ckv_full = _mm(x_m, w["mla_w_dkv"], name=f"{tag}_dkv")
    ckv_pre = ckv_full[:, :KV_LORA]
    ckv = _rms_fwd(ckv_pre, w["mla_g_kv"], name=f"{tag}_kvnorm")
    k1, k2 = _rope(ckv_full[:, KV_LORA:KV_LORA + ROPE_HALF], ckv_full[:, KV_LORA + ROPE_HALF:], cos, sin,
                   name=f"{tag}_krope")
    kp = jnp.concatenate([k1, k2], axis=-1).astype(MXU_DTYPE)
    kv = _mm(ckv, _interleave_heads(w["mla_w_uk"], w["mla_w_uv"]), name=f"{tag}_ukv", out_dtypes=(MXU_DTYPE,))
    o, o_f, lse = _attn_fwd(q, qp_h, kv, kp, name=f"{tag}_attn")
    y = _mm(o, w["mla_w_o"], name=f"{tag}_o")
    return y, (x_m, cq_pre, cq, ckv_pre, ckv, q, qp_h, kv, kp, lse, o, o_f, (cos, sin, cos_h, sin_h))


def _mla_bwd(dy_m, saved, w, tag, residual, after=()):
    x_m, cq_pre, cq, ckv_pre, ckv, q, qp_h, kv, kp, lse, o, o_f, (cos, sin, cos_h, sin_h) = saved
    t = x_m.shape[0]
    half = ROPE_HALF
    dw_o = _mm(o, dy_m, ta=True, name=f"{tag}_dwo", out_dtypes=(MXU_DTYPE,))
    do = _mm(dy_m, w["mla_w_o"], tb=True, name=f"{tag}_do", out_dtypes=(MXU_DTYPE,), after=after)
    dqn, dqp_h, dkv, dkp = _attn_bwd(q, qp_h, kv, kp, do, o_f, lse, name=f"{tag}_dattn")
    dqp = _rope_in_place(dqp_h.transpose(1, 0, 2).reshape(t, -1), cos_h, sin_h, name=f"{tag}_dqrope", inverse=True)
    dq = jnp.concatenate([dqn, dqp], axis=-1).astype(MXU_DTYPE)
    dw_uq = _mm(cq, dq, ta=True, name=f"{tag}_dwuq", out_dtypes=(MXU_DTYPE,))
    dcq = _mm(dq, w["mla_w_uq"], tb=True, name=f"{tag}_dcq")
    dcq_pre, dg_q = _rms_bwd(cq_pre, w["mla_g_q"], dcq, name=f"{tag}_dqnorm")
    dcq_pre_m = dcq_pre.astype(MXU_DTYPE)
    dw_dq = _mm(x_m, dcq_pre_m, ta=True, name=f"{tag}_dwdq", out_dtypes=(MXU_DTYPE,))
    dx_q = _mm(dcq_pre_m, w["mla_w_dq"], tb=True, name=f"{tag}_dxq", epilogue=_plus_residual, tile_extras=(residual,))
    dw_kv = _mm(ckv, dkv, ta=True, name=f"{tag}_dwukv", out_dtypes=(MXU_DTYPE,))
    dw_kv = dw_kv.reshape(KV_LORA, MLA_HEADS, QK_NOPE + V_HEAD)
    dw_uk, dw_uv = dw_kv[:, :, :QK_NOPE].reshape(KV_LORA, -1), dw_kv[:, :, QK_NOPE:].reshape(KV_LORA, -1)
    dckv = _mm(dkv, _interleave_heads(w["mla_w_uk"], w["mla_w_uv"]), tb=True, name=f"{tag}_dckv")
    dckv_pre, dg_kv = _rms_bwd(ckv_pre, w["mla_g_kv"], dckv, name=f"{tag}_dkvnorm")
    dk1, dk2 = _rope(dkp[:, :half], dkp[:, half:], cos, sin, name=f"{tag}_dkrope", inverse=True)
    dckv_full = jnp.concatenate([dckv_pre, dk1, dk2], axis=-1).astype(MXU_DTYPE)
    dw_dkv = _mm(x_m, dckv_full, ta=True, name=f"{tag}_dwdkv", out_dtypes=(MXU_DTYPE,))
    dx_kv = _mm(dckv_full, w["mla_w_dkv"], tb=True, name=f"{tag}_dxkv", epilogue=lambda acc, other: (acc + other,),
                tile_extras=(dx_q,))
    grads = {"mla_w_dq": dw_dq, "mla_g_q": dg_q, "mla_w_uq": dw_uq, "mla_w_dkv": dw_dkv, "mla_g_kv": dg_kv,
             "mla_w_uk": dw_uk, "mla_w_uv": dw_uv, "mla_w_o": dw_o}
    return [dx_kv], grads


def _cast_input(x):
    cast = lambda rows, bc: ((rows[0],), ())
    return _rowwise(cast, [x], [], [(D_MODEL, MXU_DTYPE)], [], name="x_cast")[0]


def _hooked(value, hook):
    token = None if hook is None else hook(value)
    return () if token is None else (token,)


def _mix_fwd(i, cur, cur_m, wl, hook=None, after=()):
    mixer = i % N_MIXERS
    if mixer == 0:
        y, s_mix = _sc_fwd(cur_m, wl["sc_w_in"], wl["sc_conv_w"], wl["sc_w_out"], f"l{i}_sc", after=after)
    elif mixer == 1:
        y, s_mix = _mla_fwd(cur_m, wl, f"l{i}_mla", after=after)
    else:
        y, s_mix = _cf_fwd(cur_m, wl, f"l{i}_cf", after=after)
    cur, cur_m, xh_mix, rs_mix = _ln_fwd(cur, y, wl["ln_mix_g"], wl["ln_mix_b"], name=f"l{i}_lnmix",
                                         after=_hooked(y, hook))
    return cur, cur_m, (s_mix, xh_mix, rs_mix)


def _ff_fwd(i, cur, cur_m, wl, hook=None, after=(), loss_target=None, act=None):
    y, s_ff = _mlp_fwd(cur_m, wl["ff_w1"], wl["ff_w2"], f"l{i}_ff", after=after, act=act)
    if loss_target is not None:
        xh_ff, rs_ff, dy, sq = _ln_fwd_loss(cur, y, wl["ln_ff_g"], wl["ln_ff_b"], loss_target, name=f"l{i}_lnff_loss",
                                            after=_hooked(y, hook))
        return (0.5 / D_MODEL) * jnp.sum(sq), dy, (s_ff, xh_ff, rs_ff)
    cur, cur_m, xh_ff, rs_ff = _ln_fwd(cur, y, wl["ln_ff_g"], wl["ln_ff_b"], name=f"l{i}_lnff", after=_hooked(y, hook))
    return cur, cur_m, (s_ff, xh_ff, rs_ff)


def _layer_fwd(i, cur, cur_m, wl):
    cur, cur_m, s_mix = _mix_fwd(i, cur, cur_m, wl)
    cur, cur_m, s_ff = _ff_fwd(i, cur, cur_m, wl)
    return cur, cur_m, (*s_mix, *s_ff)


def _loss_and_grad(out, target):
    d = D_MODEL

    def loss_fn(rows, bc):
        err = rows[0] - rows[1]
        return (err * (1.0 / d),), (_colsum(err * err),)

    dy, sq = _rowwise(loss_fn, [out, target], [], [(d, F32)], [(1, d)], name="loss")
    return (0.5 / d) * jnp.sum(sq), dy


def _ff_bwd(i, terms, saved, wl, hook=None, after=()):
    s_ff, xh_ff, rs_ff = saved
    grads = {}
    dz, dz_m, grads["ln_ff_g"], grads["ln_ff_b"], _ = _ln_bwd(terms, xh_ff, rs_ff, wl["ln_ff_g"], name=f"l{i}_dlnff",
                                                              after=after)
    dxs, g_ff = _mlp_bwd(dz_m, s_ff, wl["ff_w1"], wl["ff_w2"], f"l{i}_ff", dz, after=_hooked(dz, hook))
    grads.update(g_ff)
    return [(1.0, v) for v in dxs], grads


def _mix_bwd(i, terms, saved, wl, hook=None, after=()):
    mixer = i % N_MIXERS
    s_mix, xh_mix, rs_mix = saved
    grads = {}
    dz, dz_m, grads["ln_mix_g"], grads["ln_mix_b"], dz_sum = _ln_bwd(terms, xh_mix, rs_mix, wl["ln_mix_g"],
                                                                     name=f"l{i}_dlnmix", after=after)
    behind = _hooked(dz, hook)
    if mixer == 0:
        dxs, g_mix = _sc_bwd(dz_m, s_mix, wl["sc_w_in"], wl["sc_conv_w"], wl["sc_w_out"], f"l{i}_sc", dz, after=behind)
    elif mixer == 1:
        dxs, g_mix = _mla_bwd(dz_m, s_mix, wl, f"l{i}_mla", dz, after=behind)
    else:
        dxs, g_mix = _cf_bwd(dz_m, dz_sum, s_mix, wl, f"l{i}_cf", dz, after=behind)
    grads.update(g_mix)
    return [(1.0, v) for v in dxs], grads


def _layer_bwd(i, terms, saved, wl):
    terms, g_ff = _ff_bwd(i, terms, saved[3:], wl)
    terms, g_mix = _mix_bwd(i, terms, saved[:3], wl)
    return terms, {**g_ff, **g_mix}


def _sum_terms(terms, *, name):
    coefs = [c for c, _ in terms]

    def sum_fn(rows, bc):
        acc = None
        for c, v in zip(coefs, rows):
            tv = v if c == 1.0 else c * v
            acc = tv if acc is None else acc + tv
        return (acc,), ()

    return _rowwise(sum_fn, [v for _, v in terms], [], [(D_MODEL, F32)], [], name=name)[0]


def _local_step(x, target, layers):
    cur, cur_m = x, _cast_input(x)
    saved = []
    for i in range(DEPTH):
        cur, cur_m, s = _layer_fwd(i, cur, cur_m, layers[i])
        saved.append(s)
    loss, dy = _loss_and_grad(cur, target)
    terms = [(1.0, dy)]
    grads = [None] * DEPTH
    for i in reversed(range(DEPTH)):
        terms, grads[i] = _layer_bwd(i, terms, saved[i], layers[i])
    return loss, _sum_terms(terms, name="grad_x"), grads


MESH_ID = pl.DeviceIdType.MESH
ANY = pl.BlockSpec(memory_space=pl.ANY)


def _block_id(px, py, pc):
    return 4 * px + 2 * py + pc


COPIES_PER_GATHER = 7


def _all_gather(shards, *, name):
    n = len(shards)

    def body(*refs):
        x_refs, out_refs = refs[:n], refs[n:2 * n]
        send_sems, recv_sems, local_sems = refs[2 * n:]
        x, y, core = lax.axis_index("x"), lax.axis_index("y"), lax.axis_index("c")
        me, sibling = (x, y, core), (x, y, 1 - core)
        chips = [(1 - x, y), (x, 1 - y), (1 - x, 1 - y)]

        def copy(t, k, block, to, from_input=False):
            dst = out_refs[t].at[_block_id(*block)]
            return pltpu.make_async_remote_copy(
                src_ref=x_refs[t] if from_input else dst, dst_ref=dst,
                send_sem=send_sems.at[t * COPIES_PER_GATHER + k], recv_sem=recv_sems.at[t * COPIES_PER_GATHER + k],
                device_id=to, device_id_type=MESH_ID)

        started = []
        mine = [pltpu.make_async_copy(x_refs[t], out_refs[t].at[_block_id(*me)], local_sems.at[t]) for t in range(n)]
        for t in range(n):
            mine[t].start()
            first = [copy(t, 0, me, sibling, from_input=True)]
            first += [copy(t, 1 + j, me, (*chip, core), from_input=True) for j, chip in enumerate(chips)]
            for cp in first:
                cp.start()
            started += first
        for t in range(n):
            for j, chip in enumerate(chips):
                copy(t, 1 + j, (*chip, core), me).wait_recv()
                passed = copy(t, 4 + j, (*chip, core), sibling)
                passed.start()
                started.append(passed)
        for t in range(n):
            copy(t, 0, sibling, me).wait_recv()
            for j, chip in enumerate(chips):
                copy(t, 4 + j, (*chip, 1 - core), me).wait_recv()
        for cp in started:
            cp.wait_send()
        for cp in mine:
            cp.wait()

    n_sems = n * COPIES_PER_GATHER
    outs = pl.pallas_call(
        body,
        name=name,
        out_shape=[jax.ShapeDtypeStruct((N_DEV, *s.shape), s.dtype) for s in shards],
        in_specs=[ANY] * n,
        out_specs=[ANY] * n,
        scratch_shapes=[pltpu.SemaphoreType.DMA((n_sems,)), pltpu.SemaphoreType.DMA((n_sems,)),
                        pltpu.SemaphoreType.DMA((n,))],
    )(*shards)
    return list(outs)


def _swap_with_sibling(parts, *, name):
    n = len(parts)

    def body(*refs):
        p_refs, got_refs = refs[:n], refs[n:2 * n]
        send_sems, recv_sems = refs[2 * n:]
        x, y, core = lax.axis_index("x"), lax.axis_index("y"), lax.axis_index("c")
        chips = [(x, y), (1 - x, y), (x, 1 - y), (1 - x, 1 - y)]
        copies = []
        for t in range(n):
            for k, (px, py) in enumerate(chips):
                copies.append(pltpu.make_async_remote_copy(
                    src_ref=p_refs[t].at[_block_id(px, py, 1 - core)], dst_ref=got_refs[t].at[k],
                    send_sem=send_sems.at[4 * t + k], recv_sem=recv_sems.at[4 * t + k], device_id=(x, y, 1 - core),
                    device_id_type=MESH_ID))
        for cp in copies:
            cp.start()
        for cp in copies:
            cp.wait()

    outs = pl.pallas_call(
        body,
        name=name,
        out_shape=[jax.ShapeDtypeStruct((4, *p.shape[1:]), p.dtype) for p in parts],
        in_specs=[ANY] * n,
        out_specs=[ANY] * n,
        scratch_shapes=[pltpu.SemaphoreType.DMA((4 * n,)), pltpu.SemaphoreType.DMA((4 * n,))],
    )(*parts)
    return list(outs)


def _swap_between_chips(parts, *, name):
    n = len(parts)

    def body(*refs):
        p_refs, got_refs = refs[:n], refs[n:2 * n]
        send_sems, recv_sems = refs[2 * n:]
        x, y, core = lax.axis_index("x"), lax.axis_index("y"), lax.axis_index("c")
        chips = [(1 - x, y), (x, 1 - y), (1 - x, 1 - y)]
        copies = []
        for t in range(n):
            for j, (px, py) in enumerate(chips):
                copies.append(pltpu.make_async_remote_copy(
                    src_ref=p_refs[t].at[j], dst_ref=got_refs[t].at[j], send_sem=send_sems.at[3 * t + j],
                    recv_sem=recv_sems.at[3 * t + j], device_id=(px, py, core), device_id_type=MESH_ID))
        for cp in copies:
            cp.start()
        for cp in copies:
            cp.wait()

    outs = pl.pallas_call(
        body,
        name=name,
        out_shape=[jax.ShapeDtypeStruct(p.shape, p.dtype) for p in parts],
        in_specs=[ANY] * n,
        out_specs=[ANY] * n,
        scratch_shapes=[pltpu.SemaphoreType.DMA((3 * n,)), pltpu.SemaphoreType.DMA((3 * n,))],
    )(*parts)
    return list(outs)


def _chip_partials_group(parts, got, *, name):
    n = len(parts)
    rows = [p.shape[1] for p in parts]
    steps = 4 if all(r % (4 * BF16_ROWS) == 0 for r in rows) else (2 if all(r % (2 * BF16_ROWS) == 0 for r in rows) else 1)
    x, y, core = lax.axis_index("x"), lax.axis_index("y"), lax.axis_index("c")
    chips = [(x, y), (1 - x, y), (x, 1 - y), (1 - x, 1 - y)]
    ids = jnp.stack([_block_id(px, py, core) for px, py in chips]).astype(jnp.int32)

    def body(ids_ref, *refs):
        ins, outs = refs[:5 * n], refs[5 * n:]
        for t in range(n):
            p0, p1, p2, p3, g_ref = ins[5 * t:5 * t + 5]
            own_ref, out_ref = outs[2 * t], outs[2 * t + 1]
            own_ref[...] = p0[...].astype(F32) + g_ref[0].astype(F32)
            for j, p in enumerate((p1, p2, p3)):
                out_ref[j] = (p[...].astype(F32) + g_ref[j + 1].astype(F32)).astype(out_ref.dtype)

    in_specs, out_specs, out_shape, operands = [], [], [], []
    for p, g in zip(parts, got):
        _, r, c = p.shape
        tb = r // steps
        in_specs += [pl.BlockSpec((None, tb, c), lambda i, ids_ref, k=k: (ids_ref[k], i, 0)) for k in range(4)]
        in_specs.append(pl.BlockSpec((4, tb, c), lambda i, ids_ref: (0, i, 0)))
        out_specs += [pl.BlockSpec((tb, c), lambda i, ids_ref: (i, 0)), pl.BlockSpec((3, tb, c), lambda i, ids_ref: (0, i, 0))]
        out_shape += [jax.ShapeDtypeStruct((r, c), F32), jax.ShapeDtypeStruct((3, r, c), p.dtype)]
        operands += [p, p, p, p, g]
    outs = pl.pallas_call(
        body,
        name=name,
        grid_spec=pltpu.PrefetchScalarGridSpec(num_scalar_prefetch=1, grid=(steps,), in_specs=in_specs,
                                               out_specs=out_specs),
        out_shape=out_shape,
        compiler_params=_params(("parallel",)),
    )(ids, *operands)
    return [(outs[2 * t], outs[2 * t + 1]) for t in range(n)]


def _sum_blocks(first, blocks, *, name):
    n, r, c = blocks.shape
    tb = _row_tile(r, 512)

    def body(*refs):
        b_ref, o_ref = refs[-2], refs[-1]
        acc = refs[0][...] if first is not None else b_ref[0].astype(F32)
        for j in range(0 if first is not None else 1, n):
            acc = acc + b_ref[j].astype(F32)
        o_ref[...] = acc

    row_spec = pl.BlockSpec((tb, c), lambda i: (i, 0))
    return pl.pallas_call(
        body,
        name=name,
        grid=(r // tb,),
        in_specs=([row_spec] if first is not None else []) + [pl.BlockSpec((n, tb, c), lambda i: (0, i, 0))],
        out_specs=row_spec,
        out_shape=jax.ShapeDtypeStruct((r, c), F32),
        compiler_params=_params(("parallel",)),
    )(*([first] if first is not None else []), blocks)


HBM = pl.BlockSpec(memory_space=pltpu.HBM)
SEM = pl.BlockSpec(memory_space=pltpu.SEMAPHORE)
DATAFLOW = pltpu.SideEffectType.DATAFLOW_SIDE_EFFECTING


def _split_call(name, arrays, sems, n_new, body, after=None):
    n_a, n_s = len(arrays), len(sems)
    after = [] if after is None else (list(after) if isinstance(after, (list, tuple)) else [after])
    n_after = len(after)

    def kernel_body(*refs):
        new = refs[n_a + n_s + n_after:n_a + n_s + n_after + 2] if n_new else (None, None)
        body(refs[:n_a], refs[n_a:n_a + n_s], *new)
        refs[-1][...] = jnp.zeros_like(refs[-1])

    n_sem_out = 2 if n_new else 0
    out_shape = [pltpu.SemaphoreType.DMA((n_new,))] * n_sem_out
    out_shape += [pltpu.HBM(a.shape, a.dtype) for a in arrays]
    out_shape += [jax.ShapeDtypeStruct((SUBLANES, LANES), F32)]
    outs = pl.pallas_call(
        kernel_body,
        name=name,
        out_shape=out_shape,
        in_specs=[HBM] * n_a + [SEM] * n_s + [ANY] * n_after,
        out_specs=[SEM] * n_sem_out + [HBM] * n_a + [pl.BlockSpec(memory_space=pltpu.VMEM)],
        input_output_aliases={i: n_sem_out + i for i in range(n_a)},
        compiler_params=pltpu.CompilerParams(has_side_effects=DATAFLOW),
    )(*[pltpu.with_memory_space_constraint(a, pltpu.HBM) for a in arrays], *sems, *after)
    return tuple(outs[:n_sem_out]), list(outs[n_sem_out:n_sem_out + n_a]), outs[-1]


def _remote(src, dst, send, recv, k, to):
    return pltpu.make_async_remote_copy(src_ref=src, dst_ref=dst, send_sem=send.at[k], recv_sem=recv.at[k],
                                        device_id=to, device_id_type=MESH_ID)


def _land_block(land_ref, block, by_cols, width):
    if by_cols:
        return land_ref.at[:, pl.ds(pl.multiple_of(block * width, LANES), width)]
    return land_ref.at[block]


def _place_in_window(stacked, layer, *, name):
    _, r, c = stacked.shape
    tb = _row_tile(r, 512)
    me = _block_id(lax.axis_index("x"), lax.axis_index("y"), lax.axis_index("c")).astype(jnp.int32).reshape(1)

    def body(me_ref, s_ref, o_ref, token_ref):
        o_ref[...] = s_ref[...].astype(o_ref.dtype)
        token_ref[...] = jnp.zeros_like(token_ref)

    return pl.pallas_call(
        body,
        name=name,
        grid_spec=pltpu.PrefetchScalarGridSpec(
            num_scalar_prefetch=1, grid=(r // tb,),
            in_specs=[pl.BlockSpec((None, tb, c), lambda i, me_ref: (layer, i, 0))],
            out_specs=[pl.BlockSpec((tb, c), lambda i, me_ref: (i, me_ref[0])),
                       pl.BlockSpec((SUBLANES, LANES), lambda i, me_ref: (0, 0))]),
        out_shape=[jax.ShapeDtypeStruct((r, N_DEV * c), MXU_DTYPE), jax.ShapeDtypeStruct((SUBLANES, LANES), F32)],
        compiler_params=_params(("arbitrary",)),
    )(me, stacked)


def _gather_prepare(shards, by_cols, *, name):
    me = _block_id(lax.axis_index("x"), lax.axis_index("y"), lax.axis_index("c"))
    lands, early, widths = [], [], []
    for i, (s, cols) in enumerate(zip(shards, by_cols)):
        if cols:
            land, token = _place_in_window(*s, name=f"{name}_place{i}")
            lands.append(land)
            early.append(token)
            widths.append(s[0].shape[-1])
        else:
            lands.append(lax.dynamic_update_index_in_dim(lax.empty((N_DEV, *s.shape), s.dtype), s, me, 0))
            early.append(s)
            widths.append(s.shape[-1])
    return lands, widths, list(by_cols), early


def _gather_start(shards, *, name, after=None, by_cols=None, prepared=None):
    if prepared is None:
        prepared = _gather_prepare(shards, by_cols if by_cols is not None else [False] * len(shards), name=name)
    lands, widths, by_cols, _ = prepared
    n = len(lands)

    def body(refs, _, send, recv):
        x, y, core = lax.axis_index("x"), lax.axis_index("y"), lax.axis_index("c")
        targets = [(x, y, 1 - core), (1 - x, y, core), (x, 1 - y, core), (1 - x, 1 - y, core)]
        for t in range(n):
            mine = _land_block(refs[t], _block_id(x, y, core), by_cols[t], widths[t])
            for k, to in enumerate(targets):
                _remote(mine, mine, send, recv, 4 * t + k, to).start()

    sems, thru, token = _split_call(name, lands, [], 4 * n, body, after=after)
    return {"n": n, "sems1": sems, "arrays": thru, "token": token, "by_cols": by_cols, "widths": widths}


def _gather_pass_on(st, *, name, after):
    n = st["n"]

    def body(refs, sems, send, recv):
        send1, recv1 = sems
        x, y, core = lax.axis_index("x"), lax.axis_index("y"), lax.axis_index("c")
        chips = [(1 - x, y), (x, 1 - y), (1 - x, 1 - y)]
        for t in range(n):
            for j, (px, py) in enumerate(chips):
                block = _land_block(refs[t], _block_id(px, py, core), st["by_cols"][t], st["widths"][t])
                _remote(block, block, send1, recv1, 4 * t + 1 + j, (x, y, 1 - core)).wait_recv()
                _remote(block, block, send, recv, 3 * t + j, (x, y, 1 - core)).start()

    sems2, thru, token = _split_call(name, st["arrays"], list(st["sems1"]), 3 * n, body, after=after)
    return {**st, "sems2": sems2, "arrays": thru, "token": token}


def _gather_finish(st, *, name, after, then_start=None):
    n = st["n"]
    n2 = 0 if then_start is None else len(then_start[0])

    def body(refs, sems, new_send=None, new_recv=None):
        send1, recv1, send2, recv2 = sems
        x, y, core = lax.axis_index("x"), lax.axis_index("y"), lax.axis_index("c")
        sibling = (x, y, 1 - core)
        chips = [(1 - x, y), (x, 1 - y), (1 - x, 1 - y)]
        for t in range(n):
            place = lambda px, py, pc, t=t: _land_block(refs[t], _block_id(px, py, pc), st["by_cols"][t],
                                                        st["widths"][t])
            mine = place(x, y, core)
            for k in range(4):
                _remote(mine, mine, send1, recv1, 4 * t + k, sibling).wait_send()
            _remote(mine, place(x, y, 1 - core), send1, recv1, 4 * t, sibling).wait_recv()
            for j, (px, py) in enumerate(chips):
                _remote(place(px, py, core), place(px, py, core), send2, recv2, 3 * t + j, sibling).wait_send()
                _remote(mine, place(px, py, 1 - core), send2, recv2, 3 * t + j, sibling).wait_recv()
        if n2:
            _, widths, by_cols, _ = then_start
            targets = [sibling, (1 - x, y, core), (x, 1 - y, core), (1 - x, 1 - y, core)]
            for t in range(n2):
                mine = _land_block(refs[n + t], _block_id(x, y, core), by_cols[t], widths[t])
                for k, to in enumerate(targets):
                    _remote(mine, mine, new_send, new_recv, 4 * t + k, to).start()

    arrays = st["arrays"] + (list(then_start[0]) if n2 else [])
    sems, thru, token = _split_call(name, arrays, [*st["sems1"], *st["sems2"]], 4 * n2, body, after=after)
    if not n2:
        return thru
    _, widths, by_cols, _ = then_start
    return thru[:n], {"n": n2, "sems1": sems, "arrays": thru[n:], "token": token, "by_cols": by_cols, "widths": widths}


def _swap_start(sources, lands, plan, per_array, *, name):
    n = len(sources)

    def body(refs, _, send, recv):
        x, y, core = lax.axis_index("x"), lax.axis_index("y"), lax.axis_index("c")
        for t in range(n):
            for k in range(per_array):
                block, to = plan(x, y, core, k)
                _remote(refs[t].at[block], refs[n + t].at[k], send, recv, per_array * t + k, to).start()

    sems, thru, token = _split_call(name, sources + lands, [], per_array * n, body)
    return {"n": n, "sems": sems, "arrays": thru, "token": token, "plan": plan, "per_array": per_array}


def _swap_finish(states, *, name, after):
    many = isinstance(states, (list, tuple))
    states = list(states) if many else [states]

    def body(refs, sems, *_):
        x, y, core = lax.axis_index("x"), lax.axis_index("y"), lax.axis_index("c")
        first = 0
        for j, st in enumerate(states):
            n, plan, per_array = st["n"], st["plan"], st["per_array"]
            send, recv = sems[2 * j], sems[2 * j + 1]
            for t in range(n):
                for k in range(per_array):
                    block, to = plan(x, y, core, k)
                    cp = _remote(refs[first + t].at[block], refs[first + n + t].at[k], send, recv, per_array * t + k, to)
                    cp.wait_send()
                    cp.wait_recv()
            first += 2 * n

    arrays = [a for st in states for a in st["arrays"]]
    _, thru, _ = _split_call(name, arrays, [s for st in states for s in st["sems"]], 0, body, after=after)
    out, first = [], 0
    for st in states:
        n = st["n"]
        out.append((thru[first:first + n], thru[first + n:first + 2 * n]))
        first += 2 * n
    return out if many else out[0]


def _to_sibling(x, y, core, k):
    px, py = [(x, y), (1 - x, y), (x, 1 - y), (1 - x, 1 - y)][k]
    return _block_id(px, py, 1 - core), (x, y, 1 - core)


def _to_chip(x, y, core, k):
    px, py = [(1 - x, y), (x, 1 - y), (1 - x, 1 - y)][k]
    return k, (px, py, core)


PARAMS = {
    "sc_w_in": ((2, 1024, 3072), 2), "sc_conv_w": ((2, 3, 1024), 2), "sc_w_out": ((2, 1024, 1024), 1),
    "mla_w_dq": ((1, 1024, 384), 1), "mla_g_q": ((1, 384), None), "mla_w_uq": ((1, 384, 1536), 2),
    "mla_w_dkv": ((1, 1024, 320), 1), "mla_g_kv": ((1, 256), None), "mla_w_uk": ((1, 256, 8, 128), 1),
    "mla_w_uv": ((1, 256, 8, 128), 1), "mla_w_o": ((1, 1024, 1024), 1), "cf_w_pw1": ((1, 1024, 2048), 2),
    "cf_b_pw1": ((1, 2048), 1), "cf_dw_w": ((1, 31, 1024), 2), "cf_dw_b": ((1, 1024), 1), "cf_norm_g": ((1, 1024), 1),
    "cf_norm_b": ((1, 1024), 1), "cf_w_pw2": ((1, 1024, 1024), 1), "cf_b_pw2": ((1, 1024), 1),
    "ff_w1": ((4, 1024, 4096), 2), "ff_w2": ((4, 4096, 1024), 1), "ln_mix_g": ((4, 1024), None),
    "ln_mix_b": ((4, 1024), None), "ln_ff_g": ((4, 1024), None), "ln_ff_b": ((4, 1024), None),
}
NAMES = list(PARAMS)
BIG = ["sc_w_in", "sc_w_out", "mla_w_dq", "mla_w_uq", "mla_w_dkv", "mla_w_uk", "mla_w_uv", "mla_w_o", "cf_w_pw1",
       "cf_w_pw2", "ff_w1", "ff_w2"]
SMALL = [n for n in NAMES if n not in BIG]
SMALL_SHARDED = [n for n in SMALL if PARAMS[n][1] is not None]
SMALL_WIDTH = LANES
DEV_BLOCKED = ("sc_w_in", "cf_w_pw1", "ff_w1")
MIXER_BIG = {0: ["sc_w_in", "sc_w_out"], 1: ["mla_w_dq", "mla_w_uq", "mla_w_dkv", "mla_w_uk", "mla_w_uv", "mla_w_o"],
             2: ["cf_w_pw1", "cf_w_pw2"]}


def _layer_big(i):
    return [(n, i // N_MIXERS) for n in MIXER_BIG[i % N_MIXERS]] + [("ff_w1", i), ("ff_w2", i)]


def _shard_2d(name, a):
    return a.reshape(a.shape[0], -1)


def _shard_for_gather(name, stacked, layer):
    if name in DEV_BLOCKED:
        return stacked, layer
    return _shard_2d(name, stacked[layer]).astype(MXU_DTYPE)


def _as_operand(name, g):
    if name in DEV_BLOCKED:
        return g
    if name == "mla_w_uq":
        by_head = g.transpose(1, 0, 2)
        pieces = [by_head[:, :, :QK_NOPE], by_head[:, :, QK_NOPE:]]
        return jnp.concatenate([p.reshape(g.shape[1], -1) for p in pieces], axis=1)
    return g.reshape(-1, g.shape[2])


def _as_parts(name, grad):
    if name in DEV_BLOCKED:
        return grad
    if name == "mla_w_uq":
        rows = grad.shape[0]
        pieces = [grad[:, :Q_NOPE_COLS].reshape(rows, MLA_HEADS, QK_NOPE),
                  grad[:, Q_NOPE_COLS:].reshape(rows, MLA_HEADS, QK_ROPE)]
        return jnp.concatenate(pieces, axis=2).transpose(1, 0, 2)
    return grad.reshape(N_DEV, -1, grad.shape[1])


def _shard_shape(name):
    shape, ax = PARAMS[name]
    if ax is None:
        return shape
    return tuple(s // N_DEV if i == ax else s for i, s in enumerate(shape))


def _rows_of(shape, width, align):
    n = 1
    for s in shape:
        n *= s
    rows = -(-n // width)
    return -(-rows // align) * align


def _pack_rows(arrays, width, align, dtype, lead=0):
    segs = []
    for a in arrays:
        batch = a.shape[:lead]
        flat = a.astype(dtype).reshape(*batch, -1)
        rows = _rows_of(a.shape[lead:], width, align)
        flat = jnp.pad(flat, [(0, 0)] * lead + [(0, rows * width - flat.shape[-1])])
        segs.append(flat.reshape(*batch, rows, width))
    return jnp.concatenate(segs, axis=lead)


def _unpack_rows(buf, shapes, width, align):
    out, off = [], 0
    lead = buf.shape[:-2]
    for shape in shapes:
        n = 1
        for s in shape:
            n *= s
        rows = _rows_of(shape, width, align)
        seg = buf[..., off:off + rows, :].reshape(*lead, rows * width)[..., :n]
        out.append(seg.reshape(*lead, *shape))
        off += rows
    return out


def _merge_shards(stacked, ax):
    moved = jnp.moveaxis(stacked, 0, ax)
    shape = moved.shape
    return moved.reshape(*shape[:ax], shape[ax] * shape[ax + 1], *shape[ax + 2:])


def _split_shards(full, ax):
    shape = full.shape
    split = full.reshape(*shape[:ax], N_DEV, shape[ax] // N_DEV, *shape[ax + 1:])
    return jnp.moveaxis(split, ax, 0)


def _adam_math(wv, gv, mv, vv):
    m_new = ADAM_B1 * mv + (1.0 - ADAM_B1) * gv
    v_new = ADAM_B2 * vv + (1.0 - ADAM_B2) * (gv * gv)
    m_hat = m_new / (1.0 - ADAM_B1 ** ADAM_STEP)
    v_hat = v_new / (1.0 - ADAM_B2 ** ADAM_STEP)
    delta = -ADAM_LR * (m_hat / (jnp.sqrt(v_hat) + ADAM_EPS) + ADAM_WD * wv)
    return delta, m_new, v_new


def _adamw(w, g, m, v, *, name):
    shape = w.shape
    c = shape[-1]

    def fn(rows, bc):
        return _adam_math(*rows), ()

    flat = [a.reshape(-1, c) for a in (w, g, m, v)]
    outs = _rowwise(fn, flat, [], [(c, F32)] * 3, [], name=name, tb_cap=512)
    return tuple(o.reshape(shape) for o in outs)


def _adamw_reduced(w, m, v, layer_grads, *, name, after=()):
    n_l, r, c = w.shape
    tb = _row_tile(r, 256 if c > 512 else 512)
    nb = r // tb

    def body(*refs):
        w_ref, m_ref, v_ref = refs[:3]
        g_refs = refs[3:3 + 2 * n_l]
        g_out, d_out, m_out, v_out = refs[3 + 2 * n_l + len(after):]
        layer = pl.program_id(0)
        for l in range(n_l):
            @pl.when(layer == l)
            def _(l=l):
                own, arr = g_refs[2 * l], g_refs[2 * l + 1]
                g = own[...] + arr[0].astype(F32)
                g = g + arr[1].astype(F32)
                g = g + arr[2].astype(F32)
                delta, m_new, v_new = _adam_math(w_ref[...], g, m_ref[...], v_ref[...])
                g_out[...] = g
                d_out[...] = delta
                m_out[...] = m_new
                v_out[...] = v_new

    stacked = pl.BlockSpec((tb, c), lambda l, i: (l * nb + i, 0))
    in_specs = [stacked] * 3
    operands = [a.reshape(n_l * r, c) for a in (w, m, v)]
    for l, (own, arr) in enumerate(layer_grads):
        in_specs.append(pl.BlockSpec((tb, c), lambda ll, i, l=l: (jnp.where(ll == l, i, 0), 0)))
        in_specs.append(pl.BlockSpec((3, tb, c), lambda ll, i, l=l: (0, jnp.where(ll == l, i, 0), 0)))
        operands += [own, arr]
    in_specs += [ANY] * len(after)
    operands += list(after)
    outs = pl.pallas_call(
        body,
        name=name,
        grid=(n_l, nb),
        in_specs=in_specs,
        out_specs=[stacked] * 4,
        out_shape=[jax.ShapeDtypeStruct((n_l * r, c), F32)] * 4,
        compiler_params=_params(("arbitrary", "arbitrary")),
    )(*operands)
    return tuple(o.reshape(n_l, r, c) for o in outs)


def _adamw_reduced_group(ws, ms, vs, grads, *, name, after=()):
    n = len(ws)

    def body(*refs):
        ins, outs = refs[:5 * n], refs[5 * n + len(after):]
        for i in range(n):
            w_ref, m_ref, v_ref, own, arr = ins[5 * i:5 * i + 5]
            g = own[...] + arr[0].astype(F32)
            g = g + arr[1].astype(F32)
            g = g + arr[2].astype(F32)
            delta, m_new, v_new = _adam_math(w_ref[...], g, m_ref[...], v_ref[...])
            for o_ref, val in zip(outs[4 * i:4 * i + 4], (g, delta, m_new, v_new)):
                o_ref[...] = val

    whole = lambda a: pl.BlockSpec(a.shape, lambda i, nd=a.ndim: (0,) * nd)
    operands, in_specs, out_specs, out_shape = [], [], [], []
    for w, m, v, (own, arr) in zip(ws, ms, vs, grads):
        operands += [w, m, v, own, arr]
        in_specs += [whole(a) for a in (w, m, v, own, arr)]
        out_specs += [whole(w)] * 4
        out_shape += [jax.ShapeDtypeStruct(w.shape, F32)] * 4
    outs = pl.pallas_call(
        body,
        name=name,
        grid=(1,),
        in_specs=in_specs + [ANY] * len(after),
        out_specs=out_specs,
        out_shape=out_shape,
        compiler_params=_params(("arbitrary",)),
    )(*operands, *after)
    return [tuple(outs[4 * i:4 * i + 4]) for i in range(n)]


def _train_step(x, target, weights, m_state, v_state):
    me = _block_id(lax.axis_index("x"), lax.axis_index("y"), lax.axis_index("c"))

    groups = [(kind, i) for i in range(DEPTH) for kind in ("mix", "ff")]
    forward = [groups[0], ("ff_up", 0), ("ff_down", 0)] + groups[2:]

    def group_names(g):
        kind, i = g
        if kind == "mix":
            return [(n, i // N_MIXERS) for n in MIXER_BIG[i % N_MIXERS]]
        return [(n, i) for n in {"ff": ["ff_w1", "ff_w2"], "ff_up": ["ff_w1"], "ff_down": ["ff_w2"]}[kind]]

    def shards_of(g):
        return [_shard_for_gather(n, weights[n], j) for n, j in group_names(g)]

    whole = lambda g: [n in DEV_BLOCKED for n, _ in group_names(g)]

    small_shapes = [_shard_shape(n) for n in SMALL_SHARDED]
    gather = {0: _gather_start([_pack_rows([weights[n] for n in SMALL_SHARDED], SMALL_WIDTH, SUBLANES, F32)]
                               + shards_of(groups[0]), name="gather_g0_start", by_cols=[False] + whole(groups[0]))}
    cur, cur_m = x[0], x[0]
    prepared = {k: _gather_prepare(shards_of(forward[k]), whole(forward[k]), name=f"gather_g{k}_start")
                for k in range(1, len(forward))}
    early = [a for k in prepared for a in prepared[k][3]]
    gather[0] = _gather_pass_on(gather[0], name="gather_g0_pass", after=early)
    first = _gather_finish(gather[0], name="gather_g0_finish", after=gather[0]["token"])
    small_all, first_matrices = first[0], first[1:]
    small_full = {n: weights[n] for n in SMALL if PARAMS[n][1] is None}
    for n, stacked in zip(SMALL_SHARDED, _unpack_rows(small_all, small_shapes, SMALL_WIDTH, SUBLANES)):
        small_full[n] = _merge_shards(stacked, PARAMS[n][1])

    def small_weights(i):
        wl = {}
        for n in SMALL:
            if n.startswith("ln_"):
                wl[n] = small_full[n][i:i + 1]
            elif n.startswith(("sc_", "mla_", "cf_")[i % N_MIXERS]):
                v = small_full[n][i // N_MIXERS]
                wl[n] = v if v.ndim == 2 else v.reshape(1, -1)
        return wl

    layers = [small_weights(i) for i in range(DEPTH)]

    def install(g, gathered):
        for (n, _), a in zip(group_names(g), gathered):
            layers[g[1]][n] = _as_operand(n, a)

    install(forward[0], first_matrices)
    gather[1] = _gather_start(None, name="gather_g1_start", after=gather[0]["token"], prepared=prepared[1])
    gather[2] = _gather_start(None, name="gather_g2_start", after=gather[1]["token"], prepared=prepared[2])
    saved = {}
    hidden = None
    for k, g in enumerate(forward):
        kind, i = g
        tokens = [gather[1]["token"]] if k == 0 else []
        if k + 2 < len(forward):
            tokens.append(gather[k + 2]["token"])

        def pass_on(y, k=k):
            if k + 1 == len(forward):
                return None
            gather[k + 1] = _gather_pass_on(gather[k + 1], name=f"gather_g{k + 1}_pass", after=y)
            return gather[k + 1]["token"]

        if k + 1 == len(forward):
            loss_local, dy, saved[g] = _ff_fwd(i, cur, cur_m, layers[i], after=tokens, loss_target=target[0])
            break
        if kind == "ff_up":
            hidden = _mlp_up(cur_m, layers[i]["ff_w1"], f"l{i}_ff", after=tokens)
            behind, done = _hooked(hidden, pass_on), hidden
        elif kind == "ff_down":
            cur, cur_m, saved[("ff", i)] = _ff_fwd(i, cur, cur_m, layers[i], hook=pass_on, after=[*tokens, *behind],
                                                   act=hidden)
            done = cur
        else:
            phase = _mix_fwd if kind == "mix" else _ff_fwd
            cur, cur_m, saved[g] = phase(i, cur, cur_m, layers[i], hook=pass_on, after=tokens)
            done = cur
        if k + 3 < len(forward):
            gathered, gather[k + 3] = _gather_finish(gather[k + 1], name=f"gather_g{k + 1}_finish", after=done,
                                                     then_start=prepared[k + 3])
        else:
            gathered = _gather_finish(gather[k + 1], name=f"gather_g{k + 1}_finish", after=done)
        install(forward[k + 1], gathered)

    loss = lax.psum(loss_local, MESH_AXES)

    reduced = {n: [None] * PARAMS[n][0][0] for n in BIG}
    small_grads = {n: [None] * PARAMS[n][0][0] for n in SMALL}
    reduce = {}

    def to_chips(p, after):
        st = reduce[p]
        parts, got = _swap_finish(st["sibling"], name=f"reduce_p{p}_sibling_finish", after=after)
        sums = _chip_partials_group(parts, got, name=f"reduce_p{p}_chipsum")
        st["own"] = [own for own, _ in sums]
        others = [o for _, o in sums]
        st["chips"] = _swap_start(others, [lax.empty(o.shape, o.dtype) for o in others], _to_chip, 3,
                                  name=f"reduce_p{p}_chips_start")
        return st["chips"]["token"]

    def arrived(phases, after):
        done = _swap_finish([reduce[p]["chips"] for p in phases], name=f"reduce_p{phases[0]}_chips_finish", after=after)
        for p, (_, got) in zip(phases, done):
            for (n, j), own, arr in zip(reduce[p]["names"], reduce[p]["own"], got):
                reduced[n][j] = (own, arr)

    terms = [(1.0, dy)]
    back = list(reversed(groups))
    for p, g in enumerate(back):
        kind, i = g

        def after_first_kernel(dz, p=p):
            return to_chips(p - 1, dz) if p >= 1 else None

        phase = _mix_bwd if kind == "mix" else _ff_bwd
        tokens = [reduce[p - 1]["sibling"]["token"]] if p >= 1 else []
        terms, grads = phase(i, terms, saved[g], layers[i], hook=after_first_kernel, after=tokens)
        for n in SMALL:
            if n in grads:
                small_grads[n][i if n.startswith("ln_") else i // N_MIXERS] = grads[n]
        names = group_names(g)
        parts = [_as_parts(n, grads[n]) for n, _ in names]
        reduce[p] = {"names": names, "sibling": _swap_start(
            parts, [lax.empty((4, *a.shape[1:]), a.dtype) for a in parts], _to_sibling, 4,
            name=f"reduce_p{p}_sibling_start")}
    (_, grad_x), = terms
    last = len(back) - 1
    small_mine = _pack_rows([jnp.stack(small_grads[n]).reshape(PARAMS[n][0]) for n in SMALL], SMALL_WIDTH, SUBLANES, F32)
    small_gather = _gather_start([small_mine], name="gather_small_grads_start")
    tail = to_chips(last, small_gather["token"])
    arrived(list(range(last)), tail)

    grad, delta, new_m, new_v = {}, {}, {}, {}

    def adamw_matrix(n, after):
        shape = weights[n].shape
        view = lambda a: a.reshape(shape[0], shape[1], -1)
        outs = _adamw_reduced(view(weights[n]), view(m_state[n]), view(v_state[n]), reduced[n], name=f"adamw_{n}",
                              after=after)
        grad[n], delta[n], new_m[n], new_v[n] = [o.reshape(shape) for o in outs]
        return outs[0]

    done = tail
    single = [n for n in BIG if PARAMS[n][0][0] == 1 and n != "cf_w_pw1"]
    flat = lambda a: a.reshape(a.shape[1], -1)
    outs = _adamw_reduced_group([flat(weights[n]) for n in single], [flat(m_state[n]) for n in single],
                                [flat(v_state[n]) for n in single], [reduced[n][0] for n in single],
                                name="adamw_single_layer", after=[done])
    for n, quad in zip(single, outs):
        grad[n], delta[n], new_m[n], new_v[n] = [o.reshape(weights[n].shape) for o in quad]
    done = outs[-1][0]
    for n in BIG:
        if n not in MIXER_BIG[0] and n not in single:
            done = adamw_matrix(n, [done])
    small_gather = _gather_pass_on(small_gather, name="gather_small_grads_pass", after=done)
    small_all = _gather_finish(small_gather, name="gather_small_grads_finish", after=small_gather["token"])[0]
    small_sum = _sum_blocks(None, small_all, name="sum_small_grads")
    for n, g in zip(SMALL, _unpack_rows(small_sum, [PARAMS[n][0] for n in SMALL], SMALL_WIDTH, SUBLANES)):
        ax = PARAMS[n][1]
        grad[n] = g if ax is None else lax.dynamic_index_in_dim(_split_shards(g, ax), me, axis=0, keepdims=False)
    small_shards = [_shard_shape(n) for n in SMALL]
    packed = [_pack_rows([src[n] for n in SMALL], SMALL_WIDTH, SUBLANES, F32) for src in (weights, grad, m_state, v_state)]
    small_out = _adamw(*packed, name="adamw_small")
    for outs, dst in zip(small_out, (delta, new_m, new_v)):
        for n, a in zip(SMALL, _unpack_rows(outs, small_shards, SMALL_WIDTH, SUBLANES)):
            dst[n] = a
    arrived([last], small_sum)
    for n in MIXER_BIG[0]:
        adamw_matrix(n, [])
    return (loss, grad_x[None], *[grad[n] for n in NAMES], *[delta[n] for n in NAMES],
            *[new_m[n] for n in NAMES], *[new_v[n] for n in NAMES])


def kernel(x, sc_w_in, sc_conv_w, sc_w_out, mla_w_dq, mla_g_q, mla_w_uq, mla_w_dkv, mla_g_kv, mla_w_uk, mla_w_uv, mla_w_o, cf_w_pw1, cf_b_pw1, cf_dw_w, cf_dw_b, cf_norm_g, cf_norm_b, cf_w_pw2, cf_b_pw2, ff_w1, ff_w2, ln_mix_g, ln_mix_b, ln_ff_g, ln_ff_b, loss_target, m_sc_w_in, m_sc_conv_w, m_sc_w_out, m_mla_w_dq, m_mla_g_q, m_mla_w_uq, m_mla_w_dkv, m_mla_g_kv, m_mla_w_uk, m_mla_w_uv, m_mla_w_o, m_cf_w_pw1, m_cf_b_pw1, m_cf_dw_w, m_cf_dw_b, m_cf_norm_g, m_cf_norm_b, m_cf_w_pw2, m_cf_b_pw2, m_ff_w1, m_ff_w2, m_ln_mix_g, m_ln_mix_b, m_ln_ff_g, m_ln_ff_b, v_sc_w_in, v_sc_conv_w, v_sc_w_out, v_mla_w_dq, v_mla_g_q, v_mla_w_uq, v_mla_w_dkv, v_mla_g_kv, v_mla_w_uk, v_mla_w_uv, v_mla_w_o, v_cf_w_pw1, v_cf_b_pw1, v_cf_dw_w, v_cf_dw_b, v_cf_norm_g, v_cf_norm_b, v_cf_w_pw2, v_cf_b_pw2, v_ff_w1, v_ff_w2, v_ln_mix_g, v_ln_mix_b, v_ln_ff_g, v_ln_ff_b):
    w_list = (sc_w_in, sc_conv_w, sc_w_out, mla_w_dq, mla_g_q, mla_w_uq, mla_w_dkv, mla_g_kv, mla_w_uk, mla_w_uv, mla_w_o,
              cf_w_pw1, cf_b_pw1, cf_dw_w, cf_dw_b, cf_norm_g, cf_norm_b, cf_w_pw2, cf_b_pw2, ff_w1, ff_w2, ln_mix_g,
              ln_mix_b, ln_ff_g, ln_ff_b)
    m_list = (m_sc_w_in, m_sc_conv_w, m_sc_w_out, m_mla_w_dq, m_mla_g_q, m_mla_w_uq, m_mla_w_dkv, m_mla_g_kv, m_mla_w_uk,
              m_mla_w_uv, m_mla_w_o, m_cf_w_pw1, m_cf_b_pw1, m_cf_dw_w, m_cf_dw_b, m_cf_norm_g, m_cf_norm_b, m_cf_w_pw2,
              m_cf_b_pw2, m_ff_w1, m_ff_w2, m_ln_mix_g, m_ln_mix_b, m_ln_ff_g, m_ln_ff_b)
    v_list = (v_sc_w_in, v_sc_conv_w, v_sc_w_out, v_mla_w_dq, v_mla_g_q, v_mla_w_uq, v_mla_w_dkv, v_mla_g_kv, v_mla_w_uk,
              v_mla_w_uv, v_mla_w_o, v_cf_w_pw1, v_cf_b_pw1, v_cf_dw_w, v_cf_dw_b, v_cf_norm_g, v_cf_norm_b, v_cf_w_pw2,
              v_cf_b_pw2, v_ff_w1, v_ff_w2, v_ln_mix_g, v_ln_mix_b, v_ln_ff_g, v_ln_ff_b)
    return _train_step(x, loss_target, dict(zip(NAMES, w_list)), dict(zip(NAMES, m_list)), dict(zip(NAMES, v_list)))
```

```python
import jax
import jax.numpy as jnp
from jax import lax
from jax.experimental import pallas as pl
from jax.experimental.pallas import tpu as pltpu

F32 = jnp.float32
MXU_DTYPE = jnp.bfloat16

N_DEV = 8
D_MODEL = 1024
DEPTH = 4
N_MIXERS = 3
CHUNK = 64
ALPHA = (2.0 * DEPTH) ** 0.25
LN_EPS = 1e-5
RMS_EPS = 1e-6
MLA_HEADS = 8
QK_NOPE = 128
QK_ROPE = 64
V_HEAD = 128
Q_LORA = 384
KV_LORA = 256
ROPE_THETA = 10000.0
ADAM_LR = 0.001
ADAM_B1 = 0.9
ADAM_B2 = 0.999
ADAM_EPS = 1e-08
ADAM_WD = 0.01
ADAM_STEP = 10

LANES = 128
SUBLANES = 8
BF16_ROWS = 16
VMEM_LIMIT = 56 * 1024 * 1024
MM_VMEM_BUDGET = 36 * 1024 * 1024
NARROW_ROWS = 1024
WIDE_ROWS = 1024
CONV_PAD = 32
MASK_VALUE = -1e30

MESH_AXES = ("x", "y", "c")


def _params(semantics):
    return pltpu.CompilerParams(dimension_semantics=semantics, vmem_limit_bytes=VMEM_LIMIT)


def _tile(n, cap):
    if n <= cap:
        return n
    t = cap - cap % LANES
    while t >= LANES:
        if n % t == 0:
            return t
        t -= LANES
    raise ValueError(f"no tile for {n} under {cap}")


def _row_tile(rows, cap):
    if rows <= cap:
        return rows
    t = cap - cap % SUBLANES
    while t >= SUBLANES:
        if rows % t == 0:
            return t
        t -= SUBLANES
    raise ValueError(f"no row tile for {rows} under {cap}")


def _mm(a, b, *, name, ta=False, tb=False, b_dev=False, out_dev=False, out_dtypes=(F32,), epilogue=None,
        tile_extras=(), col_extras=(), after=(), tm_cap=2048, tn_cap=1024, tk_cap=4096):
    m, k = (a.shape[1], a.shape[0]) if ta else a.shape
    if b_dev:
        nd, b_rows, ns = b.shape
        n, kb = (b_rows, nd * ns) if tb else (nd * ns, b_rows)
    else:
        n, kb = (b.shape[0], b.shape[1]) if tb else (b.shape[1], b.shape[0])
    assert k == kb, (a.shape, b.shape, ta, tb)
    if ta or tile_extras or (b_dev and tb):
        tm_cap = min(tm_cap, 1024)
    tk = ns if (b_dev and tb) else _tile(k, tk_cap)
    nk = k // tk
    fixed_tn = ns if (b_dev and not tb) else (n // N_DEV if out_dev else None)
    if out_dev:
        assert not tile_extras and not col_extras and n % N_DEV == 0
    out_bytes = sum(jnp.dtype(dt).itemsize for dt in out_dtypes) + sum(e.dtype.itemsize for e in tile_extras)
    tm, tn = None, None
    for cand_m, cand_n in ((tm_cap, tn_cap), (tm_cap, tn_cap // 2), (tm_cap // 2, tn_cap), (tm_cap // 2, tn_cap // 2),
                           (tm_cap // 4, tn_cap // 2), (tm_cap // 4, tn_cap // 4)):
        tm, tn = _tile(m, cand_m), (fixed_tn if fixed_tn is not None else _tile(n, cand_n))
        need = 2 * (tm * tk * a.dtype.itemsize + tk * tn * b.dtype.itemsize) + 2 * tm * tn * out_bytes
        need += tm * tn * 4 * (2 if nk > 1 else 1)
        if need <= MM_VMEM_BUDGET:
            break
    n_te, n_ce, n_out = len(tile_extras), len(col_extras), len(out_dtypes)
    dims = (((0 if ta else 1,), (1 if tb else 0,)), ((), ()))

    def body(*refs):
        a_ref, b_ref = refs[0], refs[1]
        te_refs = refs[2:2 + n_te]
        ce_refs = refs[2 + n_te:2 + n_te + n_ce]
        first_out = 2 + n_te + n_ce + len(after)
        out_refs = refs[first_out:first_out + n_out]

        def finish(acc):
            if epilogue is None:
                outs = (acc,)
            else:
                outs = epilogue(acc, *[r[...] for r in te_refs], *[r[...] for r in ce_refs])
            for o_ref, o in zip(out_refs, outs):
                o_ref[...] = o.astype(o_ref.dtype)

        def product():
            return lax.dot_general(a_ref[...].astype(MXU_DTYPE), b_ref[...].astype(MXU_DTYPE), dims,
                                   preferred_element_type=F32)

        if nk == 1:
            finish(product())
        else:
            acc_ref = refs[-1]
            kk = pl.program_id(2)

            @pl.when(kk == 0)
            def _():
                acc_ref[...] = jnp.zeros_like(acc_ref)

            acc_ref[...] += product()

            @pl.when(kk == nk - 1)
            def _():
                finish(acc_ref[...])

    a_spec = pl.BlockSpec((tk, tm), lambda i, j, kk: (kk, i)) if ta else pl.BlockSpec((tm, tk), lambda i, j, kk: (i, kk))
    if b_dev and tb:
        b_spec = pl.BlockSpec((None, tn, tk), lambda i, j, kk: (kk, j, 0))
    elif b_dev:
        b_spec = pl.BlockSpec((None, tk, tn), lambda i, j, kk: (j, kk, 0))
    elif tb:
        b_spec = pl.BlockSpec((tn, tk), lambda i, j, kk: (j, kk))
    else:
        b_spec = pl.BlockSpec((tk, tn), lambda i, j, kk: (kk, j))
    tile_spec = pl.BlockSpec((tm, tn), lambda i, j, kk: (i, j))
    col_spec = pl.BlockSpec((1, tn), lambda i, j, kk: (0, j))
    if out_dev:
        out_spec = pl.BlockSpec((None, tm, tn), lambda i, j, kk: (j, i, 0))
        out_shape = [jax.ShapeDtypeStruct((N_DEV, m, tn), dt) for dt in out_dtypes]
    else:
        out_spec = tile_spec
        out_shape = [jax.ShapeDtypeStruct((m, n), dt) for dt in out_dtypes]
    outs = pl.pallas_call(
        body,
        name=name,
        grid=(m // tm, n // tn, nk),
        in_specs=[a_spec, b_spec] + [tile_spec] * n_te + [col_spec] * n_ce + [ANY] * len(after),
        out_specs=[out_spec] * n_out,
        out_shape=out_shape,
        scratch_shapes=[pltpu.VMEM((tm, tn), F32)] if nk > 1 else [],
        compiler_params=_params(("parallel", "parallel", "arbitrary")),
    )(a, b, *tile_extras, *col_extras, *after)
    return outs[0] if n_out == 1 else tuple(outs)


def _rowwise(fn, rows, bcast, out_rows, out_accs, *, name, tb_cap=256, after=()):
    t = rows[0].shape[0]
    tb = _row_tile(t, tb_cap)
    n_r, n_b, n_o, n_a = len(rows), len(bcast), len(out_rows), len(out_accs)

    def body(*refs):
        r_refs = refs[:n_r]
        b_refs = refs[n_r:n_r + n_b]
        first_out = n_r + n_b + len(after)
        o_refs = refs[first_out:first_out + n_o]
        a_refs = refs[first_out + n_o:]
        outs, accs = fn([r[...] for r in r_refs], [r[...] for r in b_refs])
        for o_ref, o in zip(o_refs, outs):
            o_ref[...] = o.astype(o_ref.dtype)
        if n_a:
            i = pl.program_id(0)

            @pl.when(i == 0)
            def _():
                for a_ref, acc in zip(a_refs, accs):
                    a_ref[...] = acc

            @pl.when(i > 0)
            def _():
                for a_ref, acc in zip(a_refs, accs):
                    a_ref[...] += acc

    in_specs = [pl.BlockSpec((tb, r.shape[1]), lambda i: (i, 0)) for r in rows]
    in_specs += [pl.BlockSpec(b.shape, lambda i: (0, 0)) for b in bcast]
    in_specs += [ANY] * len(after)
    out_specs = [pl.BlockSpec((tb, c), lambda i: (i, 0)) for c, _ in out_rows]
    out_specs += [pl.BlockSpec(s, lambda i: (0, 0)) for s in out_accs]
    out_shape = [jax.ShapeDtypeStruct((t, c), dt) for c, dt in out_rows]
    out_shape += [jax.ShapeDtypeStruct(s, F32) for s in out_accs]
    outs = pl.pallas_call(
        body,
        name=name,
        grid=(t // tb,),
        in_specs=in_specs,
        out_specs=out_specs,
        out_shape=out_shape,
        compiler_params=_params(("arbitrary",)),
    )(*rows, *bcast, *after)
    return tuple(outs)


def _colsum(v):
    return jnp.sum(v, axis=0, keepdims=True)


def _sigmoid(v):
    return 1.0 / (1.0 + jnp.exp(-v))


def _ln_fwd(x_res, y_sub, g, b, *, name, after=()):
    d = x_res.shape[1]

    def fn(rows, bc):
        z = ALPHA * rows[0] + rows[1]
        mu = jnp.mean(z, axis=-1, keepdims=True)
        zc = z - mu
        var = jnp.mean(zc * zc, axis=-1, keepdims=True)
        rstd = lax.rsqrt(var + LN_EPS)
        xhat = zc * rstd
        out = xhat * bc[0] + bc[1]
        return (out, out, xhat, rstd), ()

    return _rowwise(fn, [x_res, y_sub], [g, b], [(d, F32), (d, MXU_DTYPE), (d, F32), (1, F32)], [], name=name,
                    after=after, tb_cap=WIDE_ROWS)


def _ln_fwd_loss(x_res, y_sub, g, b, target, *, name, after=()):
    d = x_res.shape[1]

    def fn(rows, bc):
        z = ALPHA * rows[0] + rows[1]
        mu = jnp.mean(z, axis=-1, keepdims=True)
        zc = z - mu
        var = jnp.mean(zc * zc, axis=-1, keepdims=True)
        rstd = lax.rsqrt(var + LN_EPS)
        xhat = zc * rstd
        err = xhat * bc[0] + bc[1] - rows[2]
        return (xhat, rstd, err * (1.0 / d)), (_colsum(err * err),)

    return _rowwise(fn, [x_res, y_sub, target], [g, b], [(d, F32), (1, F32), (d, F32)], [(1, d)], name=name,
                    after=after, tb_cap=WIDE_ROWS)


def _ln_bwd(terms, xhat, rstd, g, *, name, after=()):
    d = xhat.shape[1]
    coefs = [c for c, _ in terms]

    def fn(rows, bc):
        xh, rs = rows[0], rows[1]
        dout = None
        for c, v in zip(coefs, rows[2:]):
            tv = v if c == 1.0 else c * v
            dout = tv if dout is None else dout + tv
        dxh = dout * bc[0]
        m1 = jnp.mean(dxh, axis=-1, keepdims=True)
        m2 = jnp.mean(dxh * xh, axis=-1, keepdims=True)
        dz = rs * (dxh - m1 - xh * m2)
        return (dz, dz), (_colsum(dout * xh), _colsum(dout), _colsum(dz))

    return _rowwise(fn, [xhat, rstd] + [v for _, v in terms], [g], [(d, F32), (d, MXU_DTYPE)],
                    [(1, d), (1, d), (1, d)], name=name, after=after, tb_cap=WIDE_ROWS)


def _rms_fwd(x, g, *, name):
    c = x.shape[1]

    def fn(rows, bc):
        v = rows[0]
        r = lax.rsqrt(jnp.mean(v * v, axis=-1, keepdims=True) + RMS_EPS)
        return (v * r * bc[0],), ()

    return _rowwise(fn, [x], [g], [(c, MXU_DTYPE)], [], name=name, tb_cap=NARROW_ROWS)[0]


def _rms_bwd(x, g, dy, *, name):
    c = x.shape[1]

    def fn(rows, bc):
        v, dyv = rows
        r = lax.rsqrt(jnp.mean(v * v, axis=-1, keepdims=True) + RMS_EPS)
        xh = v * r
        dxh = dyv * bc[0]
        dx = r * (dxh - xh * jnp.mean(dxh * xh, axis=-1, keepdims=True))
        return (dx,), (_colsum(dyv * xh),)

    return _rowwise(fn, [x, dy], [g], [(c, F32)], [(1, c)], name=name, tb_cap=NARROW_ROWS)


def _conv_chunk(t):
    return _row_tile(t, 256)


ATTN_SCALE = (QK_NOPE + QK_ROPE) ** -0.5
_NT = (((1,), (1,)), ((), ()))
_TN = (((0,), (0,)), ((), ()))
_NN = (((1,), (0,)), ((), ()))


def _dot(a, b, dims):
    return lax.dot_general(a, b, dims, preferred_element_type=F32)


def _block_scores(q_cat, kn_ref, kp_ref, lo, hi, diagonal):
    k_cat = jnp.concatenate([kn_ref[lo:hi, :], kp_ref[lo:hi, :]], axis=1)
    s = _dot(q_cat, k_cat, _NT) * ATTN_SCALE
    if diagonal:
        shift = CHUNK.bit_length() - 1
        rows = lax.broadcasted_iota(jnp.int32, s.shape, 0)
        cols = lax.broadcasted_iota(jnp.int32, s.shape, 1)
        s = jnp.where(jnp.right_shift(cols, shift) <= jnp.right_shift(rows, shift), s, MASK_VALUE)
    return s, k_cat


def _attn_fwd(qn, qp, kv, kp, *, name):
    h, t, _ = qp.shape
    tq = _row_tile(t, 256)
    assert tq % CHUNK == 0

    def body(qn_ref, qp_ref, kn_ref, kp_ref, v_ref, o_ref, of_ref, lse_ref):
        for r0 in range(0, t, tq):
            r1 = r0 + tq
            q_cat = jnp.concatenate([qn_ref[r0:r1, :].astype(MXU_DTYPE), qp_ref[r0:r1, :]], axis=1)
            s_d, _ = _block_scores(q_cat, kn_ref, kp_ref, r0, r1, True)
            m = jnp.max(s_d, axis=-1, keepdims=True)
            if r0:
                s_b, _ = _block_scores(q_cat, kn_ref, kp_ref, 0, r0, False)
                m = jnp.maximum(m, jnp.max(s_b, axis=-1, keepdims=True))
            p_d = jnp.exp(s_d - m)
            l = jnp.sum(p_d, axis=-1, keepdims=True)
            acc = _dot(p_d.astype(MXU_DTYPE), v_ref[r0:r1, :], _NN)
            if r0:
                p_b = jnp.exp(s_b - m)
                l = l + jnp.sum(p_b, axis=-1, keepdims=True)
                acc = acc + _dot(p_b.astype(MXU_DTYPE), v_ref[0:r0, :], _NN)
            o = acc / l
            o_ref[r0:r1, :] = o.astype(o_ref.dtype)
            of_ref[r0:r1, :] = o
            lse_ref[r0:r1, :] = m + jnp.log(l)

    blk = lambda w: pl.BlockSpec((None, t, w), lambda hh: (hh, 0, 0))
    cols = pl.BlockSpec((t, QK_NOPE), lambda hh: (0, hh))
    keys = pl.BlockSpec((t, QK_NOPE), lambda hh: (0, 2 * hh))
    values = pl.BlockSpec((t, V_HEAD), lambda hh: (0, 2 * hh + 1))
    return pl.pallas_call(
        body,
        name=name,
        grid=(h,),
        in_specs=[cols, blk(QK_ROPE), keys, pl.BlockSpec((t, QK_ROPE), lambda hh: (0, 0)), values],
        out_specs=[cols, cols, blk(1)],
        out_shape=[jax.ShapeDtypeStruct((t, h * V_HEAD), MXU_DTYPE), jax.ShapeDtypeStruct((t, h * V_HEAD), F32),
                   jax.ShapeDtypeStruct((h, t, 1), F32)],
        compiler_params=_params(("parallel",)),
    )(qn, qp, kv, kp, kv)


def _attn_bwd(qn, qp, kv, kp, do, o, lse, *, name):
    h, t, _ = qp.shape
    tq = _row_tile(t, 256)

    def body(qn_ref, qp_ref, kn_ref, kp_ref, v_ref, do_ref, o_ref, lse_ref, dqn_ref, dqp_ref, dkv_ref, dkp_ref):
        dkv_ref[...] = jnp.zeros_like(dkv_ref)

        @pl.when(pl.program_id(0) == 0)
        def _():
            dkp_ref[...] = jnp.zeros_like(dkp_ref)

        for r0 in range(0, t, tq):
            r1 = r0 + tq
            q_cat = jnp.concatenate([qn_ref[r0:r1, :].astype(MXU_DTYPE), qp_ref[r0:r1, :]], axis=1)
            dov, lse = do_ref[r0:r1, :], lse_ref[r0:r1, :]
            delta = jnp.sum(dov.astype(F32) * o_ref[r0:r1, :], axis=-1, keepdims=True)

            def piece(lo, hi, diagonal):
                s, k_cat = _block_scores(q_cat, kn_ref, kp_ref, lo, hi, diagonal)
                p = jnp.exp(s - lse)
                dp = _dot(dov, v_ref[lo:hi, :], _NT)
                ds_m = (p * (dp - delta) * ATTN_SCALE).astype(MXU_DTYPE)
                dk_cat = _dot(ds_m, q_cat, _TN)
                dkv_ref[lo:hi, :QK_NOPE] += dk_cat[:, :QK_NOPE]
                dkp_ref[lo:hi, :] += dk_cat[:, QK_NOPE:]
                dkv_ref[lo:hi, QK_NOPE:] += _dot(p.astype(MXU_DTYPE), dov, _TN)
                return _dot(ds_m, k_cat, _NN)

            dq_cat = piece(r0, r1, True)
            if r0:
                dq_cat = dq_cat + piece(0, r0, False)
            dqn_ref[r0:r1, :] = dq_cat[:, :QK_NOPE]
            dqp_ref[r0:r1, :] = dq_cat[:, QK_NOPE:]

    blk = lambda w: pl.BlockSpec((None, t, w), lambda hh: (hh, 0, 0))
    kpblk = pl.BlockSpec((t, QK_ROPE), lambda hh: (0, 0))
    cols = pl.BlockSpec((t, QK_NOPE), lambda hh: (0, hh))
    keys = pl.BlockSpec((t, QK_NOPE), lambda hh: (0, 2 * hh))
    values = pl.BlockSpec((t, V_HEAD), lambda hh: (0, 2 * hh + 1))
    both = pl.BlockSpec((t, QK_NOPE + V_HEAD), lambda hh: (0, hh))
    return pl.pallas_call(
        body,
        name=name,
        grid=(h,),
        in_specs=[cols, blk(QK_ROPE), keys, kpblk, values, cols, cols, blk(1)],
        out_specs=[cols, blk(QK_ROPE), both, kpblk],
        out_shape=[jax.ShapeDtypeStruct((t, h * QK_NOPE), F32), jax.ShapeDtypeStruct((h, t, QK_ROPE), F32),
                   jax.ShapeDtypeStruct((t, h * (QK_NOPE + V_HEAD)), F32), jax.ShapeDtypeStruct((t, QK_ROPE), F32)],
        compiler_params=_params(("arbitrary",)),
    )(qn, qp, kv, kp, kv, do, o, lse)


def _rope(x1, x2, cos, sin, *, name, inverse=False):
    w = x1.shape[1]
    sign = -1.0 if inverse else 1.0

    def fn(rows, bc):
        a, b, c, s = rows
        s = sign * s
        return (a * c - b * s, a * s + b * c), ()

    return _rowwise(fn, [x1, x2, cos, sin], [], [(w, F32), (w, F32)], [], name=name, tb_cap=NARROW_ROWS)


def _rope_in_place(x, cos_pairs, sin_signed, *, name, inverse=False):
    w = x.shape[1]
    half = ROPE_HALF
    sign = -1.0 if inverse else 1.0

    def fn(rows, bc):
        v, c, s = rows
        lane = lax.broadcasted_iota(jnp.int32, v.shape, 1)
        partner = jnp.where((lane & half) == 0, pltpu.roll(v, w - half, 1), pltpu.roll(v, half, 1))
        return (v * c + partner * (sign * s),), ()

    return _rowwise(fn, [x, cos_pairs, sin_signed], [], [(w, F32)], [], name=name, tb_cap=NARROW_ROWS)[0]


def _mlp_up(x_m, w1, tag, after=()):
    def epi(acc):
        r = jnp.maximum(acc, 0.0)
        return (r * r,)

    return _mm(x_m, w1, name=f"{tag}_up", out_dtypes=(MXU_DTYPE,), epilogue=epi, after=after)


def _mlp_fwd(x_m, w1, w2, tag, after=(), act=None):
    if act is None:
        act, after = _mlp_up(x_m, w1, tag, after=after), ()
    y = _mm(act, w2, name=f"{tag}_down", after=after)
    return y, (x_m, act)


def _plus_residual(acc, res):
    return (acc + ALPHA * res,)


def _mlp_bwd(dy_m, saved, w1, w2, tag, residual, after=()):
    x_m, act = saved
    dw2 = _mm(act, dy_m, ta=True, name=f"{tag}_dw2", out_dtypes=(MXU_DTYPE,))

    def epi(acc, av):
        return (acc * (2.0 * jnp.sqrt(av.astype(F32))),)

    dh = _mm(dy_m, w2, tb=True, name=f"{tag}_dact", out_dtypes=(MXU_DTYPE,), epilogue=epi, tile_extras=(act,),
             after=after)
    dw1 = _mm(x_m, dh, ta=True, out_dev=True, name=f"{tag}_dw1", out_dtypes=(MXU_DTYPE,))
    dx = _mm(dh, w1, tb=True, name=f"{tag}_dx", epilogue=_plus_residual, tile_extras=(residual,))
    return [dx], {"ff_w1": dw1, "ff_w2": dw2}


def _sc_gate_specs(t, d):
    per = d // LANES
    return [pl.BlockSpec((t, LANES), lambda j, k=k: (0, k * per + j)) for k in range(3)]


def _sc_gate_fwd(u, conv_w, *, name):
    t, d = u.shape[0], u.shape[1] // 3
    kw = conv_w.shape[0]
    ch = _conv_chunk(t)

    def body(b_ref, c_ref, h_ref, w_ref, r_ref, pad_ref):
        pad_ref[0:CONV_PAD, :] = jnp.zeros((CONV_PAD, LANES), F32)
        pad_ref[CONV_PAD:CONV_PAD + t, :] = c_ref[...] * h_ref[...]
        for t0 in range(0, t, ch):
            q = jnp.zeros((ch, LANES), F32)
            for k in range(kw):
                s = kw - 1 - k
                q = q + w_ref[k:k + 1, :] * pad_ref[CONV_PAD - s + t0:CONV_PAD - s + t0 + ch, :]
            r_ref[t0:t0 + ch, :] = (b_ref[t0:t0 + ch, :] * q).astype(r_ref.dtype)

    blk = pl.BlockSpec((t, LANES), lambda j: (0, j))
    return pl.pallas_call(
        body,
        name=name,
        grid=(d // LANES,),
        in_specs=_sc_gate_specs(t, d) + [pl.BlockSpec((kw, LANES), lambda j: (0, j))],
        out_specs=blk,
        out_shape=jax.ShapeDtypeStruct((t, d), MXU_DTYPE),
        scratch_shapes=[pltpu.VMEM((CONV_PAD + t, LANES), F32)],
        compiler_params=_params(("parallel",)),
    )(u, u, u, conv_w)


def _sc_gate_bwd(dr, u, conv_w, *, name):
    t, d = u.shape[0], u.shape[1] // 3
    kw = conv_w.shape[0]
    ch = _conv_chunk(t)

    def body(dr_ref, b_ref, c_ref, h_ref, w_ref, db_ref, dc_ref, dh_ref, dw_ref, ppad_ref, dpad_ref):
        ppad_ref[0:CONV_PAD, :] = jnp.zeros((CONV_PAD, LANES), F32)
        ppad_ref[CONV_PAD:CONV_PAD + t, :] = c_ref[...] * h_ref[...]
        dpad_ref[0:t, :] = dr_ref[...] * b_ref[...]
        dpad_ref[t:t + CONV_PAD, :] = jnp.zeros((CONV_PAD, LANES), F32)
        for t0 in range(0, t, ch):
            q = jnp.zeros((ch, LANES), F32)
            dp = jnp.zeros((ch, LANES), F32)
            for k in range(kw):
                s = kw - 1 - k
                q = q + w_ref[k:k + 1, :] * ppad_ref[CONV_PAD - s + t0:CONV_PAD - s + t0 + ch, :]
                dp = dp + w_ref[k:k + 1, :] * dpad_ref[t0 + s:t0 + s + ch, :]
            db_ref[t0:t0 + ch, :] = (dr_ref[t0:t0 + ch, :] * q).astype(db_ref.dtype)
            dc_ref[t0:t0 + ch, :] = (dp * h_ref[t0:t0 + ch, :]).astype(dc_ref.dtype)
            dh_ref[t0:t0 + ch, :] = (dp * c_ref[t0:t0 + ch, :]).astype(dh_ref.dtype)
        for k in range(kw):
            s = kw - 1 - k
            acc = jnp.zeros((ch, LANES), F32)
            for t0 in range(0, t, ch):
                acc = acc + dpad_ref[t0:t0 + ch, :] * ppad_ref[CONV_PAD - s + t0:CONV_PAD - s + t0 + ch, :]
            dw_ref[k:k + 1, :] = _colsum(acc)

    blk = pl.BlockSpec((t, LANES), lambda j: (0, j))
    wblk = pl.BlockSpec((kw, LANES), lambda j: (0, j))
    return pl.pallas_call(
        body,
        name=name,
        grid=(d // LANES,),
        in_specs=[blk] + _sc_gate_specs(t, d) + [wblk],
        out_specs=[blk, blk, blk, wblk],
        out_shape=[jax.ShapeDtypeStruct((t, d), MXU_DTYPE)] * 3 + [jax.ShapeDtypeStruct((kw, d), F32)],
        scratch_shapes=[pltpu.VMEM((CONV_PAD + t, LANES), F32), pltpu.VMEM((CONV_PAD + t, LANES), F32)],
        compiler_params=_params(("parallel",)),
    )(dr, u, u, u, conv_w)


def _sc_fwd(x_m, w_in, conv_w, w_out, tag, after=()):
    u = _mm(x_m, w_in, name=f"{tag}_in", after=after)
    r = _sc_gate_fwd(u, conv_w, name=f"{tag}_gate")
    y = _mm(r, w_out, name=f"{tag}_out")
    return y, (x_m, u, r)


def _sc_bwd(dy_m, saved, w_in, conv_w, w_out, tag, residual, after=()):
    d = D_MODEL
    x_m, u, r = saved
    dw_out = _mm(r, dy_m, ta=True, name=f"{tag}_dwout", out_dtypes=(MXU_DTYPE,))
    dr = _mm(dy_m, w_out, tb=True, name=f"{tag}_dr", after=after)
    db, dc, dh, dconv = _sc_gate_bwd(dr, u, conv_w, name=f"{tag}_dgate")
    du = jnp.concatenate([db, dc, dh], axis=1)
    dw_in = _mm(x_m, du, ta=True, out_dev=True, name=f"{tag}_dwin", out_dtypes=(MXU_DTYPE,))
    dx = _mm(du, w_in, tb=True, name=f"{tag}_dx", epilogue=_plus_residual, tile_extras=(residual,))
    return [dx], {"sc_w_in": dw_in, "sc_conv_w": dconv, "sc_w_out": dw_out}


def _cf_glu_specs(t, d):
    per = d // LANES
    return [pl.BlockSpec((t, LANES), lambda j, k=k: (0, k * per + j)) for k in range(2)]


def _cf_glu_conv_fwd(u, w, bias, *, name):
    t, d = u.shape[0], u.shape[1] // 2
    kw = w.shape[0]
    ch = _conv_chunk(t)

    def body(a_ref, g_ref, w_ref, b_ref, o_ref, pad_ref):
        pad_ref[0:CONV_PAD, :] = jnp.zeros((CONV_PAD, LANES), F32)
        pad_ref[CONV_PAD:CONV_PAD + t, :] = a_ref[...] * _sigmoid(g_ref[...])
        for t0 in range(0, t, ch):
            acc = jnp.broadcast_to(b_ref[...], (ch, LANES))
            for k in range(kw):
                s = kw - 1 - k
                acc = acc + w_ref[k:k + 1, :] * pad_ref[CONV_PAD - s + t0:CONV_PAD - s + t0 + ch, :]
            o_ref[t0:t0 + ch, :] = acc

    return pl.pallas_call(
        body,
        name=name,
        grid=(d // LANES,),
        in_specs=_cf_glu_specs(t, d) + [pl.BlockSpec((kw, LANES), lambda j: (0, j)),
                                        pl.BlockSpec((1, LANES), lambda j: (0, j))],
        out_specs=pl.BlockSpec((t, LANES), lambda j: (0, j)),
        out_shape=jax.ShapeDtypeStruct((t, d), F32),
        scratch_shapes=[pltpu.VMEM((CONV_PAD + t, LANES), F32)],
        compiler_params=_params(("parallel",)),
    )(u, u, w, bias)


def _cf_glu_conv_bwd(dy, u, w, *, name):
    t, d = u.shape[0], u.shape[1] // 2
    kw = w.shape[0]
    ch = _conv_chunk(t)

    def body(dy_ref, a_ref, g_ref, w_ref, da_ref, dg_ref, sa_ref, sg_ref, dw_ref, xpad_ref, dpad_ref):
        xpad_ref[0:CONV_PAD, :] = jnp.zeros((CONV_PAD, LANES), F32)
        xpad_ref[CONV_PAD:CONV_PAD + t, :] = a_ref[...] * _sigmoid(g_ref[...])
        dpad_ref[0:t, :] = dy_ref[...]
        dpad_ref[t:t + CONV_PAD, :] = jnp.zeros((CONV_PAD, LANES), F32)
        sum_a = jnp.zeros((1, LANES), F32)
        sum_g = jnp.zeros((1, LANES), F32)
        for t0 in range(0, t, ch):
            dh1 = jnp.zeros((ch, LANES), F32)
            for k in range(kw):
                s = kw - 1 - k
                dh1 = dh1 + w_ref[k:k + 1, :] * dpad_ref[t0 + s:t0 + s + ch, :]
            sg = _sigmoid(g_ref[t0:t0 + ch, :])
            da = dh1 * sg
            dgate = da * a_ref[t0:t0 + ch, :] * (1.0 - sg)
            da_ref[t0:t0 + ch, :] = da.astype(da_ref.dtype)
            dg_ref[t0:t0 + ch, :] = dgate.astype(dg_ref.dtype)
            sum_a = sum_a + _colsum(da)
            sum_g = sum_g + _colsum(dgate)
        sa_ref[...] = sum_a
        sg_ref[...] = sum_g
        for k in range(kw):
            s = kw - 1 - k
            acc = jnp.zeros((ch, LANES), F32)
            for t0 in range(0, t, ch):
                acc = acc + dy_ref[t0:t0 + ch, :] * xpad_ref[CONV_PAD - s + t0:CONV_PAD - s + t0 + ch, :]
            dw_ref[k:k + 1, :] = _colsum(acc)

    blk = pl.BlockSpec((t, LANES), lambda j: (0, j))
    row = pl.BlockSpec((1, LANES), lambda j: (0, j))
    wblk = pl.BlockSpec((kw, LANES), lambda j: (0, j))
    return pl.pallas_call(
        body,
        name=name,
        grid=(d // LANES,),
        in_specs=[blk] + _cf_glu_specs(t, d) + [wblk],
        out_specs=[blk, blk, row, row, wblk],
        out_shape=[jax.ShapeDtypeStruct((t, d), MXU_DTYPE)] * 2 + [jax.ShapeDtypeStruct((1, d), F32)] * 2
        + [jax.ShapeDtypeStruct((kw, d), F32)],
        scratch_shapes=[pltpu.VMEM((CONV_PAD + t, LANES), F32), pltpu.VMEM((CONV_PAD + t, LANES), F32)],
        compiler_params=_params(("parallel",)),
    )(dy, u, u, w)


def _cf_fwd(x_m, w, tag, after=()):
    d = D_MODEL

    def bias_epi(acc, bias):
        return (acc + bias,)

    u = _mm(x_m, w["cf_w_pw1"], name=f"{tag}_pw1", epilogue=bias_epi, col_extras=(w["cf_b_pw1"],),
            after=after)
    h2 = _cf_glu_conv_fwd(u, w["cf_dw_w"], w["cf_dw_b"], name=f"{tag}_conv")

    def norm_fn(rows, bc):
        hv = rows[0]
        mu = jnp.mean(hv, axis=-1, keepdims=True)
        hc = hv - mu
        rstd = lax.rsqrt(jnp.mean(hc * hc, axis=-1, keepdims=True) + LN_EPS)
        h3 = hc * rstd * bc[0] + bc[1]
        return (h3 * _sigmoid(h3),), ()

    h4 = _rowwise(norm_fn, [h2], [w["cf_norm_g"], w["cf_norm_b"]], [(d, MXU_DTYPE)], [], name=f"{tag}_norm",
                  tb_cap=WIDE_ROWS)[0]
    y = _mm(h4, w["cf_w_pw2"], name=f"{tag}_pw2", epilogue=bias_epi, col_extras=(w["cf_b_pw2"],))
    return y, (x_m, u, h2, h4)


def _cf_bwd(dy_m, dy_colsum, saved, w, tag, residual, after=()):
    d = D_MODEL
    x_m, u, h2, h4 = saved
    dw_pw2 = _mm(h4, dy_m, ta=True, name=f"{tag}_dwpw2", out_dtypes=(MXU_DTYPE,))
    dh4 = _mm(dy_m, w["cf_w_pw2"], tb=True, name=f"{tag}_dh4", after=after)

    def dnorm_fn(rows, bc):
        dh4v, hv = rows
        g, b = bc
        mu = jnp.mean(hv, axis=-1, keepdims=True)
        hc = hv - mu
        rstd = lax.rsqrt(jnp.mean(hc * hc, axis=-1, keepdims=True) + LN_EPS)
        xh = hc * rstd
        h3 = xh * g + b
        sg = _sigmoid(h3)
        dh3 = dh4v * (sg * (1.0 + h3 * (1.0 - sg)))
        dxh = dh3 * g
        dh2 = rstd * (dxh - jnp.mean(dxh, axis=-1, keepdims=True) - xh * jnp.mean(dxh * xh, axis=-1, keepdims=True))
        return (dh2,), (_colsum(dh3 * xh), _colsum(dh3), _colsum(dh2))

    dh2, dnorm_g, dnorm_b, ddw_b = _rowwise(dnorm_fn, [dh4, h2], [w["cf_norm_g"], w["cf_norm_b"]], [(d, F32)],
                                            [(1, d), (1, d), (1, d)], name=f"{tag}_dnorm", tb_cap=WIDE_ROWS)
    da, dgate, sum_a, sum_g, ddw_w = _cf_glu_conv_bwd(dh2, u, w["cf_dw_w"], name=f"{tag}_dconv")
    du = jnp.concatenate([da, dgate], axis=1)
    db_pw1 = jnp.concatenate([sum_a, sum_g], axis=1)
    dw_pw1 = _mm(x_m, du, ta=True, out_dev=True, name=f"{tag}_dwpw1", out_dtypes=(MXU_DTYPE,))
    dx = _mm(du, w["cf_w_pw1"], tb=True, name=f"{tag}_dx", epilogue=_plus_residual, tile_extras=(residual,))
    grads = {"cf_w_pw1": dw_pw1, "cf_b_pw1": db_pw1, "cf_dw_w": ddw_w, "cf_dw_b": ddw_b, "cf_norm_g": dnorm_g,
             "cf_norm_b": dnorm_b, "cf_w_pw2": dw_pw2, "cf_b_pw2": dy_colsum}
    return [dx], grads


def _rope_tables(t):
    pos = jnp.arange(t, dtype=F32)
    inv_freq = ROPE_THETA ** (-jnp.arange(0, QK_ROPE, 2, dtype=F32) / QK_ROPE)
    ang = pos[:, None] * inv_freq[None, :]
    return jnp.cos(ang), jnp.sin(ang)


ROPE_HALF = QK_ROPE // 2
Q_NOPE_COLS = MLA_HEADS * QK_NOPE


def _interleave_heads(w_uk, w_uv):
    c = w_uk.shape[0]
    return jnp.concatenate([w_uk.reshape(c, MLA_HEADS, QK_NOPE), w_uv.reshape(c, MLA_HEADS, V_HEAD)],
                           axis=2).reshape(c, -1)


def _mla_fwd(x_m, w, tag, after=()):
    t = x_m.shape[0]
    cos, sin = _rope_tables(t)
    cos_h = jnp.tile(jnp.concatenate([cos, cos], axis=1), (1, MLA_HEADS))
    sin_h = jnp.tile(jnp.concatenate([-sin, sin], axis=1), (1, MLA_HEADS))
    cq_pre = _mm(x_m, w["mla_w_dq"], name=f"{tag}_dq", after=after)
    cq = _rms_fwd(cq_pre, w["mla_g_q"], name=f"{tag}_qnorm")
    q = _mm(cq, w["mla_w_uq"], name=f"{tag}_uq")
    qp = _rope_in_place(q[:, Q_NOPE_COLS:], cos_h, sin_h, name=f"{tag}_qrope")
    qp_h = qp.reshape(t, MLA_HEADS, QK_ROPE).transpose(1, 0, 2).astype(MXU_DTYPE)
    ckv_full = _mm(x_m, w["mla_w_dkv"], name=f"{tag}_dkv")
    ckv_pre = ckv_full[:, :KV_LORA]
    ckv = _rms_fwd(ckv_pre, w["mla_g_kv"], name=f"{tag}_kvnorm")
    k1, k2 = _rope(ckv_full[:, KV_LORA:KV_LORA + ROPE_HALF], ckv_full[:, KV_LORA + ROPE_HALF:], cos, sin,
                   name=f"{tag}_krope")
    kp = jnp.concatenate([k1, k2], axis=-1).astype(MXU_DTYPE)
    kv = _mm(ckv, _interleave_heads(w["mla_w_uk"], w["mla_w_uv"]), name=f"{tag}_ukv", out_dtypes=(MXU_DTYPE,))
    o, o_f, lse = _attn_fwd(q, qp_h, kv, kp, name=f"{tag}_attn")
    y = _mm(o, w["mla_w_o"], name=f"{tag}_o")
    return y, (x_m, cq_pre, cq, ckv_pre, ckv, q, qp_h, kv, kp, lse, o, o_f, (cos, sin, cos_h, sin_h))


def _mla_bwd(dy_m, saved, w, tag, residual, after=()):
    x_m, cq_pre, cq, ckv_pre, ckv, q, qp_h, kv, kp, lse, o, o_f, (cos, sin, cos_h, sin_h) = saved
    t = x_m.shape[0]
    half = ROPE_HALF
    dw_o = _mm(o, dy_m, ta=True, name=f"{tag}_dwo", out_dtypes=(MXU_DTYPE,))
    do = _mm(dy_m, w["mla_w_o"], tb=True, name=f"{tag}_do", out_dtypes=(MXU_DTYPE,), after=after)
    dqn, dqp_h, dkv, dkp = _attn_bwd(q, qp_h, kv, kp, do, o_f, lse, name=f"{tag}_dattn")
    dqp = _rope_in_place(dqp_h.transpose(1, 0, 2).reshape(t, -1), cos_h, sin_h, name=f"{tag}_dqrope", inverse=True)
    dq = jnp.concatenate([dqn, dqp], axis=-1).astype(MXU_DTYPE)
    dw_uq = _mm(cq, dq, ta=True, name=f"{tag}_dwuq", out_dtypes=(MXU_DTYPE,))
    dcq = _mm(dq, w["mla_w_uq"], tb=True, name=f"{tag}_dcq")
    dcq_pre, dg_q = _rms_bwd(cq_pre, w["mla_g_q"], dcq, name=f"{tag}_dqnorm")
    dcq_pre_m = dcq_pre.astype(MXU_DTYPE)
    dw_dq = _mm(x_m, dcq_pre_m, ta=True, name=f"{tag}_dwdq", out_dtypes=(MXU_DTYPE,))
    dx_q = _mm(dcq_pre_m, w["mla_w_dq"], tb=True, name=f"{tag}_dxq", epilogue=_plus_residual, tile_extras=(residual,))
    dw_kv = _mm(ckv, dkv, ta=True, name=f"{tag}_dwukv", out_dtypes=(MXU_DTYPE,))
    dw_kv = dw_kv.reshape(KV_LORA, MLA_HEADS, QK_NOPE + V_HEAD)
    dw_uk, dw_uv = dw_kv[:, :, :QK_NOPE].reshape(KV_LORA, -1), dw_kv[:, :, QK_NOPE:].reshape(KV_LORA, -1)
    dckv = _mm(dkv, _interleave_heads(w["mla_w_uk"], w["mla_w_uv"]), tb=True, name=f"{tag}_dckv")
    dckv_pre, dg_kv = _rms_bwd(ckv_pre, w["mla_g_kv"], dckv, name=f"{tag}_dkvnorm")
    dk1, dk2 = _rope(dkp[:, :half], dkp[:, half:], cos, sin, name=f"{tag}_dkrope", inverse=True)
    dckv_full = jnp.concatenate([dckv_pre, dk1, dk2], axis=-1).astype(MXU_DTYPE)
    dw_dkv = _mm(x_m, dckv_full, ta=True, name=f"{tag}_dwdkv", out_dtypes=(MXU_DTYPE,))
    dx_kv = _mm(dckv_full, w["mla_w_dkv"], tb=True, name=f"{tag}_dxkv", epilogue=lambda acc, other: (acc + other,),
                tile_extras=(dx_q,))
    grads = {"mla_w_dq": dw_dq, "mla_g_q": dg_q, "mla_w_uq": dw_uq, "mla_w_dkv": dw_dkv, "mla_g_kv": dg_kv,
             "mla_w_uk": dw_uk, "mla_w_uv": dw_uv, "mla_w_o": dw_o}
    return [dx_kv], grads


def _cast_input(x):
    cast = lambda rows, bc: ((rows[0],), ())
    return _rowwise(cast, [x], [], [(D_MODEL, MXU_DTYPE)], [], name="x_cast")[0]


def _hooked(value, hook):
    token = None if hook is None else hook(value)
    return () if token is None else (token,)


def _mix_fwd(i, cur, cur_m, wl, hook=None, after=()):
    mixer = i % N_MIXERS
    if mixer == 0:
        y, s_mix = _sc_fwd(cur_m, wl["sc_w_in"], wl["sc_conv_w"], wl["sc_w_out"], f"l{i}_sc", after=after)
    elif mixer == 1:
        y, s_mix = _mla_fwd(cur_m, wl, f"l{i}_mla", after=after)
    else:
        y, s_mix = _cf_fwd(cur_m, wl, f"l{i}_cf", after=after)
    cur, cur_m, xh_mix, rs_mix = _ln_fwd(cur, y, wl["ln_mix_g"], wl["ln_mix_b"], name=f"l{i}_lnmix",
                                         after=_hooked(y, hook))
    return cur, cur_m, (s_mix, xh_mix, rs_mix)


def _ff_fwd(i, cur, cur_m, wl, hook=None, after=(), loss_target=None, act=None):
    y, s_ff = _mlp_fwd(cur_m, wl["ff_w1"], wl["ff_w2"], f"l{i}_ff", after=after, act=act)
    if loss_target is not None:
        xh_ff, rs_ff, dy, sq = _ln_fwd_loss(cur, y, wl["ln_ff_g"], wl["ln_ff_b"], loss_target, name=f"l{i}_lnff_loss",
                                            after=_hooked(y, hook))
        return (0.5 / D_MODEL) * jnp.sum(sq), dy, (s_ff, xh_ff, rs_ff)
    cur, cur_m, xh_ff, rs_ff = _ln_fwd(cur, y, wl["ln_ff_g"], wl["ln_ff_b"], name=f"l{i}_lnff", after=_hooked(y, hook))
    return cur, cur_m, (s_ff, xh_ff, rs_ff)


def _layer_fwd(i, cur, cur_m, wl):
    cur, cur_m, s_mix = _mix_fwd(i, cur, cur_m, wl)
    cur, cur_m, s_ff = _ff_fwd(i, cur, cur_m, wl)
    return cur, cur_m, (*s_mix, *s_ff)


def _loss_and_grad(out, target):
    d = D_MODEL

    def loss_fn(rows, bc):
        err = rows[0] - rows[1]
        return (err * (1.0 / d),), (_colsum(err * err),)

    dy, sq = _rowwise(loss_fn, [out, target], [], [(d, F32)], [(1, d)], name="loss")
    return (0.5 / d) * jnp.sum(sq), dy


def _ff_bwd(i, terms, saved, wl, hook=None, after=()):
    s_ff, xh_ff, rs_ff = saved
    grads = {}
    dz, dz_m, grads["ln_ff_g"], grads["ln_ff_b"], _ = _ln_bwd(terms, xh_ff, rs_ff, wl["ln_ff_g"], name=f"l{i}_dlnff",
                                                              after=after)
    dxs, g_ff = _mlp_bwd(dz_m, s_ff, wl["ff_w1"], wl["ff_w2"], f"l{i}_ff", dz, after=_hooked(dz, hook))
    grads.update(g_ff)
    return [(1.0, v) for v in dxs], grads


def _mix_bwd(i, terms, saved, wl, hook=None, after=()):
    mixer = i % N_MIXERS
    s_mix, xh_mix, rs_mix = saved
    grads = {}
    dz, dz_m, grads["ln_mix_g"], grads["ln_mix_b"], dz_sum = _ln_bwd(terms, xh_mix, rs_mix, wl["ln_mix_g"],
                                                                     name=f"l{i}_dlnmix", after=after)
    behind = _hooked(dz, hook)
    if mixer == 0:
        dxs, g_mix = _sc_bwd(dz_m, s_mix, wl["sc_w_in"], wl["sc_conv_w"], wl["sc_w_out"], f"l{i}_sc", dz, after=behind)
    elif mixer == 1:
        dxs, g_mix = _mla_bwd(dz_m, s_mix, wl, f"l{i}_mla", dz, after=behind)
    else:
        dxs, g_mix = _cf_bwd(dz_m, dz_sum, s_mix, wl, f"l{i}_cf", dz, after=behind)
    grads.update(g_mix)
    return [(1.0, v) for v in dxs], grads


def _layer_bwd(i, terms, saved, wl):
    terms, g_ff = _ff_bwd(i, terms, saved[3:], wl)
    terms, g_mix = _mix_bwd(i, terms, saved[:3], wl)
    return terms, {**g_ff, **g_mix}


def _sum_terms(terms, *, name):
    coefs = [c for c, _ in terms]

    def sum_fn(rows, bc):
        acc = None
        for c, v in zip(coefs, rows):
            tv = v if c == 1.0 else c * v
            acc = tv if acc is None else acc + tv
        return (acc,), ()

    return _rowwise(sum_fn, [v for _, v in terms], [], [(D_MODEL, F32)], [], name=name)[0]


def _local_step(x, target, layers):
    cur, cur_m = x, _cast_input(x)
    saved = []
    for i in range(DEPTH):
        cur, cur_m, s = _layer_fwd(i, cur, cur_m, layers[i])
        saved.append(s)
    loss, dy = _loss_and_grad(cur, target)
    terms = [(1.0, dy)]
    grads = [None] * DEPTH
    for i in reversed(range(DEPTH)):
        terms, grads[i] = _layer_bwd(i, terms, saved[i], layers[i])
    return loss, _sum_terms(terms, name="grad_x"), grads


MESH_ID = pl.DeviceIdType.MESH
ANY = pl.BlockSpec(memory_space=pl.ANY)


def _block_id(px, py, pc):
    return 4 * px + 2 * py + pc


COPIES_PER_GATHER = 7


def _all_gather(shards, *, name):
    n = len(shards)

    def body(*refs):
        x_refs, out_refs = refs[:n], refs[n:2 * n]
        send_sems, recv_sems, local_sems = refs[2 * n:]
        x, y, core = lax.axis_index("x"), lax.axis_index("y"), lax.axis_index("c")
        me, sibling = (x, y, core), (x, y, 1 - core)
        chips = [(1 - x, y), (x, 1 - y), (1 - x, 1 - y)]

        def copy(t, k, block, to, from_input=False):
            dst = out_refs[t].at[_block_id(*block)]
            return pltpu.make_async_remote_copy(
                src_ref=x_refs[t] if from_input else dst, dst_ref=dst,
                send_sem=send_sems.at[t * COPIES_PER_GATHER + k], recv_sem=recv_sems.at[t * COPIES_PER_GATHER + k],
                device_id=to, device_id_type=MESH_ID)

        started = []
        mine = [pltpu.make_async_copy(x_refs[t], out_refs[t].at[_block_id(*me)], local_sems.at[t]) for t in range(n)]
        for t in range(n):
            mine[t].start()
            first = [copy(t, 0, me, sibling, from_input=True)]
            first += [copy(t, 1 + j, me, (*chip, core), from_input=True) for j, chip in enumerate(chips)]
            for cp in first:
                cp.start()
            started += first
        for t in range(n):
            for j, chip in enumerate(chips):
                copy(t, 1 + j, (*chip, core), me).wait_recv()
                passed = copy(t, 4 + j, (*chip, core), sibling)
                passed.start()
                started.append(passed)
        for t in range(n):
            copy(t, 0, sibling, me).wait_recv()
            for j, chip in enumerate(chips):
                copy(t, 4 + j, (*chip, 1 - core), me).wait_recv()
        for cp in started:
            cp.wait_send()
        for cp in mine:
            cp.wait()

    n_sems = n * COPIES_PER_GATHER
    outs = pl.pallas_call(
        body,
        name=name,
        out_shape=[jax.ShapeDtypeStruct((N_DEV, *s.shape), s.dtype) for s in shards],
        in_specs=[ANY] * n,
        out_specs=[ANY] * n,
        scratch_shapes=[pltpu.SemaphoreType.DMA((n_sems,)), pltpu.SemaphoreType.DMA((n_sems,)),
                        pltpu.SemaphoreType.DMA((n,))],
    )(*shards)
    return list(outs)


def _swap_with_sibling(parts, *, name):
    n = len(parts)

    def body(*refs):
        p_refs, got_refs = refs[:n], refs[n:2 * n]
        send_sems, recv_sems = refs[2 * n:]
        x, y, core = lax.axis_index("x"), lax.axis_index("y"), lax.axis_index("c")
        chips = [(x, y), (1 - x, y), (x, 1 - y), (1 - x, 1 - y)]
        copies = []
        for t in range(n):
            for k, (px, py) in enumerate(chips):
                copies.append(pltpu.make_async_remote_copy(
                    src_ref=p_refs[t].at[_block_id(px, py, 1 - core)], dst_ref=got_refs[t].at[k],
                    send_sem=send_sems.at[4 * t + k], recv_sem=recv_sems.at[4 * t + k], device_id=(x, y, 1 - core),
                    device_id_type=MESH_ID))
        for cp in copies:
            cp.start()
        for cp in copies:
            cp.wait()

    outs = pl.pallas_call(
        body,
        name=name,
        out_shape=[jax.ShapeDtypeStruct((4, *p.shape[1:]), p.dtype) for p in parts],
        in_specs=[ANY] * n,
        out_specs=[ANY] * n,
        scratch_shapes=[pltpu.SemaphoreType.DMA((4 * n,)), pltpu.SemaphoreType.DMA((4 * n,))],
    )(*parts)
    return list(outs)


def _swap_between_chips(parts, *, name):
    n = len(parts)

    def body(*refs):
        p_refs, got_refs = refs[:n], refs[n:2 * n]
        send_sems, recv_sems = refs[2 * n:]
        x, y, core = lax.axis_index("x"), lax.axis_index("y"), lax.axis_index("c")
        chips = [(1 - x, y), (x, 1 - y), (1 - x, 1 - y)]
        copies = []
        for t in range(n):
            for j, (px, py) in enumerate(chips):
                copies.append(pltpu.make_async_remote_copy(
                    src_ref=p_refs[t].at[j], dst_ref=got_refs[t].at[j], send_sem=send_sems.at[3 * t + j],
                    recv_sem=recv_sems.at[3 * t + j], device_id=(px, py, core), device_id_type=MESH_ID))
        for cp in copies:
            cp.start()
        for cp in copies:
            cp.wait()

    outs = pl.pallas_call(
        body,
        name=name,
        out_shape=[jax.ShapeDtypeStruct(p.shape, p.dtype) for p in parts],
        in_specs=[ANY] * n,
        out_specs=[ANY] * n,
        scratch_shapes=[pltpu.SemaphoreType.DMA((3 * n,)), pltpu.SemaphoreType.DMA((3 * n,))],
    )(*parts)
    return list(outs)


def _chip_partials_group(parts, got, *, name):
    n = len(parts)
    rows = [p.shape[1] for p in parts]
    steps = 4 if all(r % (4 * BF16_ROWS) == 0 for r in rows) else (2 if all(r % (2 * BF16_ROWS) == 0 for r in rows) else 1)
    x, y, core = lax.axis_index("x"), lax.axis_index("y"), lax.axis_index("c")
    chips = [(x, y), (1 - x, y), (x, 1 - y), (1 - x, 1 - y)]
    ids = jnp.stack([_block_id(px, py, core) for px, py in chips]).astype(jnp.int32)

    def body(ids_ref, *refs):
        ins, outs = refs[:5 * n], refs[5 * n:]
        for t in range(n):
            p0, p1, p2, p3, g_ref = ins[5 * t:5 * t + 5]
            own_ref, out_ref = outs[2 * t], outs[2 * t + 1]
            own_ref[...] = p0[...].astype(F32) + g_ref[0].astype(F32)
            for j, p in enumerate((p1, p2, p3)):
                out_ref[j] = (p[...].astype(F32) + g_ref[j + 1].astype(F32)).astype(out_ref.dtype)

    in_specs, out_specs, out_shape, operands = [], [], [], []
    for p, g in zip(parts, got):
        _, r, c = p.shape
        tb = r // steps
        in_specs += [pl.BlockSpec((None, tb, c), lambda i, ids_ref, k=k: (ids_ref[k], i, 0)) for k in range(4)]
        in_specs.append(pl.BlockSpec((4, tb, c), lambda i, ids_ref: (0, i, 0)))
        out_specs += [pl.BlockSpec((tb, c), lambda i, ids_ref: (i, 0)), pl.BlockSpec((3, tb, c), lambda i, ids_ref: (0, i, 0))]
        out_shape += [jax.ShapeDtypeStruct((r, c), F32), jax.ShapeDtypeStruct((3, r, c), p.dtype)]
        operands += [p, p, p, p, g]
    outs = pl.pallas_call(
        body,
        name=name,
        grid_spec=pltpu.PrefetchScalarGridSpec(num_scalar_prefetch=1, grid=(steps,), in_specs=in_specs,
                                               out_specs=out_specs),
        out_shape=out_shape,
        compiler_params=_params(("parallel",)),
    )(ids, *operands)
    return [(outs[2 * t], outs[2 * t + 1]) for t in range(n)]


def _sum_blocks(first, blocks, *, name):
    n, r, c = blocks.shape
    tb = _row_tile(r, 512)

    def body(*refs):
        b_ref, o_ref = refs[-2], refs[-1]
        acc = refs[0][...] if first is not None else b_ref[0].astype(F32)
        for j in range(0 if first is not None else 1, n):
            acc = acc + b_ref[j].astype(F32)
        o_ref[...] = acc

    row_spec = pl.BlockSpec((tb, c), lambda i: (i, 0))
    return pl.pallas_call(
        body,
        name=name,
        grid=(r // tb,),
        in_specs=([row_spec] if first is not None else []) + [pl.BlockSpec((n, tb, c), lambda i: (0, i, 0))],
        out_specs=row_spec,
        out_shape=jax.ShapeDtypeStruct((r, c), F32),
        compiler_params=_params(("parallel",)),
    )(*([first] if first is not None else []), blocks)


HBM = pl.BlockSpec(memory_space=pltpu.HBM)
SEM = pl.BlockSpec(memory_space=pltpu.SEMAPHORE)
DATAFLOW = pltpu.SideEffectType.DATAFLOW_SIDE_EFFECTING


def _split_call(name, arrays, sems, n_new, body, after=None):
    n_a, n_s = len(arrays), len(sems)
    after = [] if after is None else (list(after) if isinstance(after, (list, tuple)) else [after])
    n_after = len(after)

    def kernel_body(*refs):
        new = refs[n_a + n_s + n_after:n_a + n_s + n_after + 2] if n_new else (None, None)
        body(refs[:n_a], refs[n_a:n_a + n_s], *new)
        refs[-1][...] = jnp.zeros_like(refs[-1])

    n_sem_out = 2 if n_new else 0
    out_shape = [pltpu.SemaphoreType.DMA((n_new,))] * n_sem_out
    out_shape += [pltpu.HBM(a.shape, a.dtype) for a in arrays]
    out_shape += [jax.ShapeDtypeStruct((SUBLANES, LANES), F32)]
    outs = pl.pallas_call(
        kernel_body,
        name=name,
        out_shape=out_shape,
        in_specs=[HBM] * n_a + [SEM] * n_s + [ANY] * n_after,
        out_specs=[SEM] * n_sem_out + [HBM] * n_a + [pl.BlockSpec(memory_space=pltpu.VMEM)],
        input_output_aliases={i: n_sem_out + i for i in range(n_a)},
        compiler_params=pltpu.CompilerParams(has_side_effects=DATAFLOW),
    )(*[pltpu.with_memory_space_constraint(a, pltpu.HBM) for a in arrays], *sems, *after)
    return tuple(outs[:n_sem_out]), list(outs[n_sem_out:n_sem_out + n_a]), outs[-1]


def _remote(src, dst, send, recv, k, to):
    return pltpu.make_async_remote_copy(src_ref=src, dst_ref=dst, send_sem=send.at[k], recv_sem=recv.at[k],
                                        device_id=to, device_id_type=MESH_ID)


def _land_block(land_ref, block, by_cols, width):
    if by_cols:
        return land_ref.at[:, pl.ds(pl.multiple_of(block * width, LANES), width)]
    return land_ref.at[block]


def _place_in_window(stacked, layer, *, name):
    _, r, c = stacked.shape
    tb = _row_tile(r, 512)
    me = _block_id(lax.axis_index("x"), lax.axis_index("y"), lax.axis_index("c")).astype(jnp.int32).reshape(1)

    def body(me_ref, s_ref, o_ref, token_ref):
        o_ref[...] = s_ref[...].astype(o_ref.dtype)
        token_ref[...] = jnp.zeros_like(token_ref)

    return pl.pallas_call(
        body,
        name=name,
        grid_spec=pltpu.PrefetchScalarGridSpec(
            num_scalar_prefetch=1, grid=(r // tb,),
            in_specs=[pl.BlockSpec((None, tb, c), lambda i, me_ref: (layer, i, 0))],
            out_specs=[pl.BlockSpec((tb, c), lambda i, me_ref: (i, me_ref[0])),
                       pl.BlockSpec((SUBLANES, LANES), lambda i, me_ref: (0, 0))]),
        out_shape=[jax.ShapeDtypeStruct((r, N_DEV * c), MXU_DTYPE), jax.ShapeDtypeStruct((SUBLANES, LANES), F32)],
        compiler_params=_params(("arbitrary",)),
    )(me, stacked)


def _gather_prepare(shards, by_cols, *, name):
    me = _block_id(lax.axis_index("x"), lax.axis_index("y"), lax.axis_index("c"))
    lands, early, widths = [], [], []
    for i, (s, cols) in enumerate(zip(shards, by_cols)):
        if cols:
            land, token = _place_in_window(*s, name=f"{name}_place{i}")
            lands.append(land)
            early.append(token)
            widths.append(s[0].shape[-1])
        else:
            lands.append(lax.dynamic_update_index_in_dim(lax.empty((N_DEV, *s.shape), s.dtype), s, me, 0))
            early.append(s)
            widths.append(s.shape[-1])
    return lands, widths, list(by_cols), early


def _gather_start(shards, *, name, after=None, by_cols=None, prepared=None):
    if prepared is None:
        prepared = _gather_prepare(shards, by_cols if by_cols is not None else [False] * len(shards), name=name)
    lands, widths, by_cols, _ = prepared
    n = len(lands)

    def body(refs, _, send, recv):
        x, y, core = lax.axis_index("x"), lax.axis_index("y"), lax.axis_index("c")
        targets = [(x, y, 1 - core), (1 - x, y, core), (x, 1 - y, core), (1 - x, 1 - y, core)]
        for t in range(n):
            mine = _land_block(refs[t], _block_id(x, y, core), by_cols[t], widths[t])
            for k, to in enumerate(targets):
                _remote(mine, mine, send, recv, 4 * t + k, to).start()

    sems, thru, token = _split_call(name, lands, [], 4 * n, body, after=after)
    return {"n": n, "sems1": sems, "arrays": thru, "token": token, "by_cols": by_cols, "widths": widths}


def _gather_pass_on(st, *, name, after):
    n = st["n"]

    def body(refs, sems, send, recv):
        send1, recv1 = sems
        x, y, core = lax.axis_index("x"), lax.axis_index("y"), lax.axis_index("c")
        chips = [(1 - x, y), (x, 1 - y), (1 - x, 1 - y)]
        for t in range(n):
            for j, (px, py) in enumerate(chips):
                block = _land_block(refs[t], _block_id(px, py, core), st["by_cols"][t], st["widths"][t])
                _remote(block, block, send1, recv1, 4 * t + 1 + j, (x, y, 1 - core)).wait_recv()
                _remote(block, block, send, recv, 3 * t + j, (x, y, 1 - core)).start()

    sems2, thru, token = _split_call(name, st["arrays"], list(st["sems1"]), 3 * n, body, after=after)
    return {**st, "sems2": sems2, "arrays": thru, "token": token}


def _gather_finish(st, *, name, after, then_start=None):
    n = st["n"]
    n2 = 0 if then_start is None else len(then_start[0])

    def body(refs, sems, new_send=None, new_recv=None):
        send1, recv1, send2, recv2 = sems
        x, y, core = lax.axis_index("x"), lax.axis_index("y"), lax.axis_index("c")
        sibling = (x, y, 1 - core)
        chips = [(1 - x, y), (x, 1 - y), (1 - x, 1 - y)]
        for t in range(n):
            place = lambda px, py, pc, t=t: _land_block(refs[t], _block_id(px, py, pc), st["by_cols"][t],
                                                        st["widths"][t])
            mine = place(x, y, core)
            for k in range(4):
                _remote(mine, mine, send1, recv1, 4 * t + k, sibling).wait_send()
            _remote(mine, place(x, y, 1 - core), send1, recv1, 4 * t, sibling).wait_recv()
            for j, (px, py) in enumerate(chips):
                _remote(place(px, py, core), place(px, py, core), send2, recv2, 3 * t + j, sibling).wait_send()
                _remote(mine, place(px, py, 1 - core), send2, recv2, 3 * t + j, sibling).wait_recv()
        if n2:
            _, widths, by_cols, _ = then_start
            targets = [sibling, (1 - x, y, core), (x, 1 - y, core), (1 - x, 1 - y, core)]
            for t in range(n2):
                mine = _land_block(refs[n + t], _block_id(x, y, core), by_cols[t], widths[t])
                for k, to in enumerate(targets):
                    _remote(mine, mine, new_send, new_recv, 4 * t + k, to).start()

    arrays = st["arrays"] + (list(then_start[0]) if n2 else [])
    sems, thru, token = _split_call(name, arrays, [*st["sems1"], *st["sems2"]], 4 * n2, body, after=after)
    if not n2:
        return thru
    _, widths, by_cols, _ = then_start
    return thru[:n], {"n": n2, "sems1": sems, "arrays": thru[n:], "token": token, "by_cols": by_cols, "widths": widths}


def _swap_start(sources, lands, plan, per_array, *, name):
    n = len(sources)

    def body(refs, _, send, recv):
        x, y, core = lax.axis_index("x"), lax.axis_index("y"), lax.axis_index("c")
        for t in range(n):
            for k in range(per_array):
                block, to = plan(x, y, core, k)
                _remote(refs[t].at[block], refs[n + t].at[k], send, recv, per_array * t + k, to).start()

    sems, thru, token = _split_call(name, sources + lands, [], per_array * n, body)
    return {"n": n, "sems": sems, "arrays": thru, "token": token, "plan": plan, "per_array": per_array}


def _swap_finish(states, *, name, after):
    many = isinstance(states, (list, tuple))
    states = list(states) if many else [states]

    def body(refs, sems, *_):
        x, y, core = lax.axis_index("x"), lax.axis_index("y"), lax.axis_index("c")
        first = 0
        for j, st in enumerate(states):
            n, plan, per_array = st["n"], st["plan"], st["per_array"]
            send, recv = sems[2 * j], sems[2 * j + 1]
            for t in range(n):
                for k in range(per_array):
                    block, to = plan(x, y, core, k)
                    cp = _remote(refs[first + t].at[block], refs[first + n + t].at[k], send, recv, per_array * t + k, to)
                    cp.wait_send()
                    cp.wait_recv()
            first += 2 * n

    arrays = [a for st in states for a in st["arrays"]]
    _, thru, _ = _split_call(name, arrays, [s for st in states for s in st["sems"]], 0, body, after=after)
    out, first = [], 0
    for st in states:
        n = st["n"]
        out.append((thru[first:first + n], thru[first + n:first + 2 * n]))
        first += 2 * n
    return out if many else out[0]


def _to_sibling(x, y, core, k):
    px, py = [(x, y), (1 - x, y), (x, 1 - y), (1 - x, 1 - y)][k]
    return _block_id(px, py, 1 - core), (x, y, 1 - core)


def _to_chip(x, y, core, k):
    px, py = [(1 - x, y), (x, 1 - y), (1 - x, 1 - y)][k]
    return k, (px, py, core)


PARAMS = {
    "sc_w_in": ((2, 1024, 3072), 2), "sc_conv_w": ((2, 3, 1024), 2), "sc_w_out": ((2, 1024, 1024), 1),
    "mla_w_dq": ((1, 1024, 384), 1), "mla_g_q": ((1, 384), None), "mla_w_uq": ((1, 384, 1536), 2),
    "mla_w_dkv": ((1, 1024, 320), 1), "mla_g_kv": ((1, 256), None), "mla_w_uk": ((1, 256, 8, 128), 1),
    "mla_w_uv": ((1, 256, 8, 128), 1), "mla_w_o": ((1, 1024, 1024), 1), "cf_w_pw1": ((1, 1024, 2048), 2),
    "cf_b_pw1": ((1, 2048), 1), "cf_dw_w": ((1, 31, 1024), 2), "cf_dw_b": ((1, 1024), 1), "cf_norm_g": ((1, 1024), 1),
    "cf_norm_b": ((1, 1024), 1), "cf_w_pw2": ((1, 1024, 1024), 1), "cf_b_pw2": ((1, 1024), 1),
    "ff_w1": ((4, 1024, 4096), 2), "ff_w2": ((4, 4096, 1024), 1), "ln_mix_g": ((4, 1024), None),
    "ln_mix_b": ((4, 1024), None), "ln_ff_g": ((4, 1024), None), "ln_ff_b": ((4, 1024), None),
}
NAMES = list(PARAMS)
BIG = ["sc_w_in", "sc_w_out", "mla_w_dq", "mla_w_uq", "mla_w_dkv", "mla_w_uk", "mla_w_uv", "mla_w_o", "cf_w_pw1",
       "cf_w_pw2", "ff_w1", "ff_w2"]
SMALL = [n for n in NAMES if n not in BIG]
SMALL_SHARDED = [n for n in SMALL if PARAMS[n][1] is not None]
SMALL_WIDTH = LANES
DEV_BLOCKED = ("sc_w_in", "cf_w_pw1", "ff_w1")
MIXER_BIG = {0: ["sc_w_in", "sc_w_out"], 1: ["mla_w_dq", "mla_w_uq", "mla_w_dkv", "mla_w_uk", "mla_w_uv", "mla_w_o"],
             2: ["cf_w_pw1", "cf_w_pw2"]}


def _layer_big(i):
    return [(n, i // N_MIXERS) for n in MIXER_BIG[i % N_MIXERS]] + [("ff_w1", i), ("ff_w2", i)]


def _shard_2d(name, a):
    return a.reshape(a.shape[0], -1)


def _shard_for_gather(name, stacked, layer):
    if name in DEV_BLOCKED:
        return stacked, layer
    return _shard_2d(name, stacked[layer]).astype(MXU_DTYPE)


def _as_operand(name, g):
    if name in DEV_BLOCKED:
        return g
    if name == "mla_w_uq":
        by_head = g.transpose(1, 0, 2)
        pieces = [by_head[:, :, :QK_NOPE], by_head[:, :, QK_NOPE:]]
        return jnp.concatenate([p.reshape(g.shape[1], -1) for p in pieces], axis=1)
    return g.reshape(-1, g.shape[2])


def _as_parts(name, grad):
    if name in DEV_BLOCKED:
        return grad
    if name == "mla_w_uq":
        rows = grad.shape[0]
        pieces = [grad[:, :Q_NOPE_COLS].reshape(rows, MLA_HEADS, QK_NOPE),
                  grad[:, Q_NOPE_COLS:].reshape(rows, MLA_HEADS, QK_ROPE)]
        return jnp.concatenate(pieces, axis=2).transpose(1, 0, 2)
    return grad.reshape(N_DEV, -1, grad.shape[1])


def _shard_shape(name):
    shape, ax = PARAMS[name]
    if ax is None:
        return shape
    return tuple(s // N_DEV if i == ax else s for i, s in enumerate(shape))


def _rows_of(shape, width, align):
    n = 1
    for s in shape:
        n *= s
    rows = -(-n // width)
    return -(-rows // align) * align


def _pack_rows(arrays, width, align, dtype, lead=0):
    segs = []
    for a in arrays:
        batch = a.shape[:lead]
        flat = a.astype(dtype).reshape(*batch, -1)
        rows = _rows_of(a.shape[lead:], width, align)
        flat = jnp.pad(flat, [(0, 0)] * lead + [(0, rows * width - flat.shape[-1])])
        segs.append(flat.reshape(*batch, rows, width))
    return jnp.concatenate(segs, axis=lead)


def _unpack_rows(buf, shapes, width, align):
    out, off = [], 0
    lead = buf.shape[:-2]
    for shape in shapes:
        n = 1
        for s in shape:
            n *= s
        rows = _rows_of(shape, width, align)
        seg = buf[..., off:off + rows, :].reshape(*lead, rows * width)[..., :n]
        out.append(seg.reshape(*lead, *shape))
        off += rows
    return out


def _merge_shards(stacked, ax):
    moved = jnp.moveaxis(stacked, 0, ax)
    shape = moved.shape
    return moved.reshape(*shape[:ax], shape[ax] * shape[ax + 1], *shape[ax + 2:])


def _split_shards(full, ax):
    shape = full.shape
    split = full.reshape(*shape[:ax], N_DEV, shape[ax] // N_DEV, *shape[ax + 1:])
    return jnp.moveaxis(split, ax, 0)


def _adam_math(wv, gv, mv, vv):
    m_new = ADAM_B1 * mv + (1.0 - ADAM_B1) * gv
    v_new = ADAM_B2 * vv + (1.0 - ADAM_B2) * (gv * gv)
    m_hat = m_new / (1.0 - ADAM_B1 ** ADAM_STEP)
    v_hat = v_new / (1.0 - ADAM_B2 ** ADAM_STEP)
    delta = -ADAM_LR * (m_hat / (jnp.sqrt(v_hat) + ADAM_EPS) + ADAM_WD * wv)
    return delta, m_new, v_new


def _adamw(w, g, m, v, *, name):
    shape = w.shape
    c = shape[-1]

    def fn(rows, bc):
        return _adam_math(*rows), ()

    flat = [a.reshape(-1, c) for a in (w, g, m, v)]
    outs = _rowwise(fn, flat, [], [(c, F32)] * 3, [], name=name, tb_cap=512)
    return tuple(o.reshape(shape) for o in outs)


def _adamw_reduced(w, m, v, layer_grads, *, name, after=()):
    n_l, r, c = w.shape
    tb = _row_tile(r, 256 if c > 512 else 512)
    nb = r // tb

    def body(*refs):
        w_ref, m_ref, v_ref = refs[:3]
        g_refs = refs[3:3 + 2 * n_l]
        g_out, d_out, m_out, v_out = refs[3 + 2 * n_l + len(after):]
        layer = pl.program_id(0)
        for l in range(n_l):
            @pl.when(layer == l)
            def _(l=l):
                own, arr = g_refs[2 * l], g_refs[2 * l + 1]
                g = own[...] + arr[0].astype(F32)
                g = g + arr[1].astype(F32)
                g = g + arr[2].astype(F32)
                delta, m_new, v_new = _adam_math(w_ref[...], g, m_ref[...], v_ref[...])
                g_out[...] = g
                d_out[...] = delta
                m_out[...] = m_new
                v_out[...] = v_new

    stacked = pl.BlockSpec((tb, c), lambda l, i: (l * nb + i, 0))
    in_specs = [stacked] * 3
    operands = [a.reshape(n_l * r, c) for a in (w, m, v)]
    for l, (own, arr) in enumerate(layer_grads):
        in_specs.append(pl.BlockSpec((tb, c), lambda ll, i, l=l: (jnp.where(ll == l, i, 0), 0)))
        in_specs.append(pl.BlockSpec((3, tb, c), lambda ll, i, l=l: (0, jnp.where(ll == l, i, 0), 0)))
        operands += [own, arr]
    in_specs += [ANY] * len(after)
    operands += list(after)
    outs = pl.pallas_call(
        body,
        name=name,
        grid=(n_l, nb),
        in_specs=in_specs,
        out_specs=[stacked] * 4,
        out_shape=[jax.ShapeDtypeStruct((n_l * r, c), F32)] * 4,
        compiler_params=_params(("arbitrary", "arbitrary")),
    )(*operands)
    return tuple(o.reshape(n_l, r, c) for o in outs)


def _adamw_reduced_group(ws, ms, vs, grads, *, name, after=()):
    n = len(ws)

    def body(*refs):
        ins, outs = refs[:5 * n], refs[5 * n + len(after):]
        for i in range(n):
            w_ref, m_ref, v_ref, own, arr = ins[5 * i:5 * i + 5]
            g = own[...] + arr[0].astype(F32)
            g = g + arr[1].astype(F32)
            g = g + arr[2].astype(F32)
            delta, m_new, v_new = _adam_math(w_ref[...], g, m_ref[...], v_ref[...])
            for o_ref, val in zip(outs[4 * i:4 * i + 4], (g, delta, m_new, v_new)):
                o_ref[...] = val

    whole = lambda a: pl.BlockSpec(a.shape, lambda i, nd=a.ndim: (0,) * nd)
    operands, in_specs, out_specs, out_shape = [], [], [], []
    for w, m, v, (own, arr) in zip(ws, ms, vs, grads):
        operands += [w, m, v, own, arr]
        in_specs += [whole(a) for a in (w, m, v, own, arr)]
        out_specs += [whole(w)] * 4
        out_shape += [jax.ShapeDtypeStruct(w.shape, F32)] * 4
    outs = pl.pallas_call(
        body,
        name=name,
        grid=(1,),
        in_specs=in_specs + [ANY] * len(after),
        out_specs=out_specs,
        out_shape=out_shape,
        compiler_params=_params(("arbitrary",)),
    )(*operands, *after)
    return [tuple(outs[4 * i:4 * i + 4]) for i in range(n)]


def _train_step(x, target, weights, m_state, v_state):
    me = _block_id(lax.axis_index("x"), lax.axis_index("y"), lax.axis_index("c"))

    groups = [(kind, i) for i in range(DEPTH) for kind in ("mix", "ff")]
    forward = [groups[0], ("ff_up", 0), ("ff_down", 0)] + groups[2:]

    def group_names(g):
        kind, i = g
        if kind == "mix":
            return [(n, i // N_MIXERS) for n in MIXER_BIG[i % N_MIXERS]]
        return [(n, i) for n in {"ff": ["ff_w1", "ff_w2"], "ff_up": ["ff_w1"], "ff_down": ["ff_w2"]}[kind]]

    def shards_of(g):
        return [_shard_for_gather(n, weights[n], j) for n, j in group_names(g)]

    whole = lambda g: [n in DEV_BLOCKED for n, _ in group_names(g)]

    small_shapes = [_shard_shape(n) for n in SMALL_SHARDED]
    gather = {0: _gather_start([_pack_rows([weights[n] for n in SMALL_SHARDED], SMALL_WIDTH, SUBLANES, F32)]
                               + shards_of(groups[0]), name="gather_g0_start", by_cols=[False] + whole(groups[0]))}
    cur, cur_m = x[0], x[0]
    prepared = {k: _gather_prepare(shards_of(forward[k]), whole(forward[k]), name=f"gather_g{k}_start")
                for k in range(1, len(forward))}
    early = [a for k in prepared for a in prepared[k][3]]
    gather[0] = _gather_pass_on(gather[0], name="gather_g0_pass", after=early)
    first = _gather_finish(gather[0], name="gather_g0_finish", after=gather[0]["token"])
    small_all, first_matrices = first[0], first[1:]
    small_full = {n: weights[n] for n in SMALL if PARAMS[n][1] is None}
    for n, stacked in zip(SMALL_SHARDED, _unpack_rows(small_all, small_shapes, SMALL_WIDTH, SUBLANES)):
        small_full[n] = _merge_shards(stacked, PARAMS[n][1])

    def small_weights(i):
        wl = {}
        for n in SMALL:
            if n.startswith("ln_"):
                wl[n] = small_full[n][i:i + 1]
            elif n.startswith(("sc_", "mla_", "cf_")[i % N_MIXERS]):
                v = small_full[n][i // N_MIXERS]
                wl[n] = v if v.ndim == 2 else v.reshape(1, -1)
        return wl

    layers = [small_weights(i) for i in range(DEPTH)]

    def install(g, gathered):
        for (n, _), a in zip(group_names(g), gathered):
            layers[g[1]][n] = _as_operand(n, a)

    install(forward[0], first_matrices)
    gather[1] = _gather_start(None, name="gather_g1_start", after=gather[0]["token"], prepared=prepared[1])
    gather[2] = _gather_start(None, name="gather_g2_start", after=gather[1]["token"], prepared=prepared[2])
    saved = {}
    hidden = None
    for k, g in enumerate(forward):
        kind, i = g
        tokens = [gather[1]["token"]] if k == 0 else []
        if k + 2 < len(forward):
            tokens.append(gather[k + 2]["token"])

        def pass_on(y, k=k):
            if k + 1 == len(forward):
                return None
            gather[k + 1] = _gather_pass_on(gather[k + 1], name=f"gather_g{k + 1}_pass", after=y)
            return gather[k + 1]["token"]

        if k + 1 == len(forward):
            loss_local, dy, saved[g] = _ff_fwd(i, cur, cur_m, layers[i], after=tokens, loss_target=target[0])
            break
        if kind == "ff_up":
            hidden = _mlp_up(cur_m, layers[i]["ff_w1"], f"l{i}_ff", after=tokens)
            behind, done = _hooked(hidden, pass_on), hidden
        elif kind == "ff_down":
            cur, cur_m, saved[("ff", i)] = _ff_fwd(i, cur, cur_m, layers[i], hook=pass_on, after=[*tokens, *behind],
                                                   act=hidden)
            done = cur
        else:
            phase = _mix_fwd if kind == "mix" else _ff_fwd
            cur, cur_m, saved[g] = phase(i, cur, cur_m, layers[i], hook=pass_on, after=tokens)
            done = cur
        if k + 3 < len(forward):
            gathered, gather[k + 3] = _gather_finish(gather[k + 1], name=f"gather_g{k + 1}_finish", after=done,
                                                     then_start=prepared[k + 3])
        else:
            gathered = _gather_finish(gather[k + 1], name=f"gather_g{k + 1}_finish", after=done)
        install(forward[k + 1], gathered)

    loss = lax.psum(loss_local, MESH_AXES)

    reduced = {n: [None] * PARAMS[n][0][0] for n in BIG}
    small_grads = {n: [None] * PARAMS[n][0][0] for n in SMALL}
    reduce = {}

    def to_chips(p, after):
        st = reduce[p]
        parts, got = _swap_finish(st["sibling"], name=f"reduce_p{p}_sibling_finish", after=after)
        sums = _chip_partials_group(parts, got, name=f"reduce_p{p}_chipsum")
        st["own"] = [own for own, _ in sums]
        others = [o for _, o in sums]
        st["chips"] = _swap_start(others, [lax.empty(o.shape, o.dtype) for o in others], _to_chip, 3,
                                  name=f"reduce_p{p}_chips_start")
        return st["chips"]["token"]

    def arrived(phases, after):
        done = _swap_finish([reduce[p]["chips"] for p in phases], name=f"reduce_p{phases[0]}_chips_finish", after=after)
        for p, (_, got) in zip(phases, done):
            for (n, j), own, arr in zip(reduce[p]["names"], reduce[p]["own"], got):
                reduced[n][j] = (own, arr)

    terms = [(1.0, dy)]
    back = list(reversed(groups))
    for p, g in enumerate(back):
        kind, i = g

        def after_first_kernel(dz, p=p):
            return to_chips(p - 1, dz) if p >= 1 else None

        phase = _mix_bwd if kind == "mix" else _ff_bwd
        tokens = [reduce[p - 1]["sibling"]["token"]] if p >= 1 else []
        terms, grads = phase(i, terms, saved[g], layers[i], hook=after_first_kernel, after=tokens)
        for n in SMALL:
            if n in grads:
                small_grads[n][i if n.startswith("ln_") else i // N_MIXERS] = grads[n]
        names = group_names(g)
        parts = [_as_parts(n, grads[n]) for n, _ in names]
        reduce[p] = {"names": names, "sibling": _swap_start(
            parts, [lax.empty((4, *a.shape[1:]), a.dtype) for a in parts], _to_sibling, 4,
            name=f"reduce_p{p}_sibling_start")}
    (_, grad_x), = terms
    last = len(back) - 1
    small_mine = _pack_rows([jnp.stack(small_grads[n]).reshape(PARAMS[n][0]) for n in SMALL], SMALL_WIDTH, SUBLANES, F32)
    small_gather = _gather_start([small_mine], name="gather_small_grads_start")
    tail = to_chips(last, small_gather["token"])
    arrived(list(range(last)), tail)

    grad, delta, new_m, new_v = {}, {}, {}, {}

    def adamw_matrix(n, after):
        shape = weights[n].shape
        view = lambda a: a.reshape(shape[0], shape[1], -1)
        outs = _adamw_reduced(view(weights[n]), view(m_state[n]), view(v_state[n]), reduced[n], name=f"adamw_{n}",
                              after=after)
        grad[n], delta[n], new_m[n], new_v[n] = [o.reshape(shape) for o in outs]
        return outs[0]

    done = tail
    single = [n for n in BIG if PARAMS[n][0][0] == 1 and n != "cf_w_pw1"]
    flat = lambda a: a.reshape(a.shape[1], -1)
    outs = _adamw_reduced_group([flat(weights[n]) for n in single], [flat(m_state[n]) for n in single],
                                [flat(v_state[n]) for n in single], [reduced[n][0] for n in single],
                                name="adamw_single_layer", after=[done])
    for n, quad in zip(single, outs):
        grad[n], delta[n], new_m[n], new_v[n] = [o.reshape(weights[n].shape) for o in quad]
    done = outs[-1][0]
    for n in BIG:
        if n not in MIXER_BIG[0] and n not in single:
            done = adamw_matrix(n, [done])
    small_gather = _gather_pass_on(small_gather, name="gather_small_grads_pass", after=done)
    small_all = _gather_finish(small_gather, name="gather_small_grads_finish", after=small_gather["token"])[0]
    small_sum = _sum_blocks(None, small_all, name="sum_small_grads")
    for n, g in zip(SMALL, _unpack_rows(small_sum, [PARAMS[n][0] for n in SMALL], SMALL_WIDTH, SUBLANES)):
        ax = PARAMS[n][1]
        grad[n] = g if ax is None else lax.dynamic_index_in_dim(_split_shards(g, ax), me, axis=0, keepdims=False)
    small_shards = [_shard_shape(n) for n in SMALL]
    packed = [_pack_rows([src[n] for n in SMALL], SMALL_WIDTH, SUBLANES, F32) for src in (weights, grad, m_state, v_state)]
    small_out = _adamw(*packed, name="adamw_small")
    for outs, dst in zip(small_out, (delta, new_m, new_v)):
        for n, a in zip(SMALL, _unpack_rows(outs, small_shards, SMALL_WIDTH, SUBLANES)):
            dst[n] = a
    arrived([last], small_sum)
    for n in MIXER_BIG[0]:
        adamw_matrix(n, [])
    return (loss, grad_x[None], *[grad[n] for n in NAMES], *[delta[n] for n in NAMES],
            *[new_m[n] for n in NAMES], *[new_v[n] for n in NAMES])


def kernel(x, sc_w_in, sc_conv_w, sc_w_out, mla_w_dq, mla_g_q, mla_w_uq, mla_w_dkv, mla_g_kv, mla_w_uk, mla_w_uv, mla_w_o, cf_w_pw1, cf_b_pw1, cf_dw_w, cf_dw_b, cf_norm_g, cf_norm_b, cf_w_pw2, cf_b_pw2, ff_w1, ff_w2, ln_mix_g, ln_mix_b, ln_ff_g, ln_ff_b, loss_target, m_sc_w_in, m_sc_conv_w, m_sc_w_out, m_mla_w_dq, m_mla_g_q, m_mla_w_uq, m_mla_w_dkv, m_mla_g_kv, m_mla_w_uk, m_mla_w_uv, m_mla_w_o, m_cf_w_pw1, m_cf_b_pw1, m_cf_dw_w, m_cf_dw_b, m_cf_norm_g, m_cf_norm_b, m_cf_w_pw2, m_cf_b_pw2, m_ff_w1, m_ff_w2, m_ln_mix_g, m_ln_mix_b, m_ln_ff_g, m_ln_ff_b, v_sc_w_in, v_sc_conv_w, v_sc_w_out, v_mla_w_dq, v_mla_g_q, v_mla_w_uq, v_mla_w_dkv, v_mla_g_kv, v_mla_w_uk, v_mla_w_uv, v_mla_w_o, v_cf_w_pw1, v_cf_b_pw1, v_cf_dw_w, v_cf_dw_b, v_cf_norm_g, v_cf_norm_b, v_cf_w_pw2, v_cf_b_pw2, v_ff_w1, v_ff_w2, v_ln_mix_g, v_ln_mix_b, v_ln_ff_g, v_ln_ff_b):
    w_list = (sc_w_in, sc_conv_w, sc_w_out, mla_w_dq, mla_g_q, mla_w_uq, mla_w_dkv, mla_g_kv, mla_w_uk, mla_w_uv, mla_w_o,
              cf_w_pw1, cf_b_pw1, cf_dw_w, cf_dw_b, cf_norm_g, cf_norm_b, cf_w_pw2, cf_b_pw2, ff_w1, ff_w2, ln_mix_g,
              ln_mix_b, ln_ff_g, ln_ff_b)
    m_list = (m_sc_w_in, m_sc_conv_w, m_sc_w_out, m_mla_w_dq, m_mla_g_q, m_mla_w_uq, m_mla_w_dkv, m_mla_g_kv, m_mla_w_uk,
              m_mla_w_uv, m_mla_w_o, m_cf_w_pw1, m_cf_b_pw1, m_cf_dw_w, m_cf_dw_b, m_cf_norm_g, m_cf_norm_b, m_cf_w_pw2,
              m_cf_b_pw2, m_ff_w1, m_ff_w2, m_ln_mix_g, m_ln_mix_b, m_ln_ff_g, m_ln_ff_b)
    v_list = (v_sc_w_in, v_sc_conv_w, v_sc_w_out, v_mla_w_dq, v_mla_g_q, v_mla_w_uq, v_mla_w_dkv, v_mla_g_kv, v_mla_w_uk,
              v_mla_w_uv, v_mla_w_o, v_cf_w_pw1, v_cf_b_pw1, v_cf_dw_w, v_cf_dw_b, v_cf_norm_g, v_cf_norm_b, v_cf_w_pw2,
              v_cf_b_pw2, v_ff_w1, v_ff_w2, v_ln_mix_g, v_ln_mix_b, v_ln_ff_g, v_ln_ff_b)
    return _train_step(x, loss_target, dict(zip(NAMES, w_list)), dict(zip(NAMES, m_list)), dict(zip(NAMES, v_list)))
```

```python
import jax
import jax.numpy as jnp
from jax import lax
from jax.experimental import pallas as pl
from jax.experimental.pallas import tpu as pltpu

F32 = jnp.float32
MXU_DTYPE = jnp.bfloat16

N_DEV = 8
D_MODEL = 1024
DEPTH = 4
N_MIXERS = 3
CHUNK = 64
ALPHA = (2.0 * DEPTH) ** 0.25
LN_EPS = 1e-5
RMS_EPS = 1e-6
MLA_HEADS = 8
QK_NOPE = 128
QK_ROPE = 64
V_HEAD = 128
Q_LORA = 384
KV_LORA = 256
ROPE_THETA = 10000.0
ADAM_LR = 0.001
ADAM_B1 = 0.9
ADAM_B2 = 0.999
ADAM_EPS = 1e-08
ADAM_WD = 0.01
ADAM_STEP = 10

LANES = 128
SUBLANES = 8
BF16_ROWS = 16
VMEM_LIMIT = 56 * 1024 * 1024
MM_VMEM_BUDGET = 36 * 1024 * 1024
NARROW_ROWS = 1024
WIDE_ROWS = 1024
CONV_PAD = 32
MASK_VALUE = -1e30

MESH_AXES = ("x", "y", "c")


def _params(semantics):
    return pltpu.CompilerParams(dimension_semantics=semantics, vmem_limit_bytes=VMEM_LIMIT)


def _tile(n, cap):
    if n <= cap:
        return n
    t = cap - cap % LANES
    while t >= LANES:
        if n % t == 0:
            return t
        t -= LANES
    raise ValueError(f"no tile for {n} under {cap}")


def _row_tile(rows, cap):
    if rows <= cap:
        return rows
    t = cap - cap % SUBLANES
    while t >= SUBLANES:
        if rows % t == 0:
            return t
        t -= SUBLANES
    raise ValueError(f"no row tile for {rows} under {cap}")


def _mm(a, b, *, name, ta=False, tb=False, b_dev=False, out_dev=False, out_dtypes=(F32,), epilogue=None,
        tile_extras=(), col_extras=(), after=(), tm_cap=2048, tn_cap=1024, tk_cap=4096):
    m, k = (a.shape[1], a.shape[0]) if ta else a.shape
    if b_dev:
        nd, b_rows, ns = b.shape
        n, kb = (b_rows, nd * ns) if tb else (nd * ns, b_rows)
    else:
        n, kb = (b.shape[0], b.shape[1]) if tb else (b.shape[1], b.shape[0])
    assert k == kb, (a.shape, b.shape, ta, tb)
    if ta or tile_extras or (b_dev and tb):
        tm_cap = min(tm_cap, 1024)
    tk = ns if (b_dev and tb) else _tile(k, tk_cap)
    nk = k // tk
    fixed_tn = ns if (b_dev and not tb) else (n // N_DEV if out_dev else None)
    if out_dev:
        assert not tile_extras and not col_extras and n % N_DEV == 0
    out_bytes = sum(jnp.dtype(dt).itemsize for dt in out_dtypes) + sum(e.dtype.itemsize for e in tile_extras)
    tm, tn = None, None
    for cand_m, cand_n in ((tm_cap, tn_cap), (tm_cap, tn_cap // 2), (tm_cap // 2, tn_cap), (tm_cap // 2, tn_cap // 2),
                           (tm_cap // 4, tn_cap // 2), (tm_cap // 4, tn_cap // 4)):
        tm, tn = _tile(m, cand_m), (fixed_tn if fixed_tn is not None else _tile(n, cand_n))
        need = 2 * (tm * tk * a.dtype.itemsize + tk * tn * b.dtype.itemsize) + 2 * tm * tn * out_bytes
        need += tm * tn * 4 * (2 if nk > 1 else 1)
        if need <= MM_VMEM_BUDGET:
            break
    n_te, n_ce, n_out = len(tile_extras), len(col_extras), len(out_dtypes)
    dims = (((0 if ta else 1,), (1 if tb else 0,)), ((), ()))

    def body(*refs):
        a_ref, b_ref = refs[0], refs[1]
        te_refs = refs[2:2 + n_te]
        ce_refs = refs[2 + n_te:2 + n_te + n_ce]
        first_out = 2 + n_te + n_ce + len(after)
        out_refs = refs[first_out:first_out + n_out]

        def finish(acc):
            if epilogue is None:
                outs = (acc,)
            else:
                outs = epilogue(acc, *[r[...] for r in te_refs], *[r[...] for r in ce_refs])
            for o_ref, o in zip(out_refs, outs):
                o_ref[...] = o.astype(o_ref.dtype)

        def product():
            return lax.dot_general(a_ref[...].astype(MXU_DTYPE), b_ref[...].astype(MXU_DTYPE), dims,
                                   preferred_element_type=F32)

        if nk == 1:
            finish(product())
        else:
            acc_ref = refs[-1]
            kk = pl.program_id(2)

            @pl.when(kk == 0)
            def _():
                acc_ref[...] = jnp.zeros_like(acc_ref)

            acc_ref[...] += product()

            @pl.when(kk == nk - 1)
            def _():
                finish(acc_ref[...])

    a_spec = pl.BlockSpec((tk, tm), lambda i, j, kk: (kk, i)) if ta else pl.BlockSpec((tm, tk), lambda i, j, kk: (i, kk))
    if b_dev and tb:
        b_spec = pl.BlockSpec((None, tn, tk), lambda i, j, kk: (kk, j, 0))
    elif b_dev:
        b_spec = pl.BlockSpec((None, tk, tn), lambda i, j, kk: (j, kk, 0))
    elif tb:
        b_spec = pl.BlockSpec((tn, tk), lambda i, j, kk: (j, kk))
    else:
        b_spec = pl.BlockSpec((tk, tn), lambda i, j, kk: (kk, j))
    tile_spec = pl.BlockSpec((tm, tn), lambda i, j, kk: (i, j))
    col_spec = pl.BlockSpec((1, tn), lambda i, j, kk: (0, j))
    if out_dev:
        out_spec = pl.BlockSpec((None, tm, tn), lambda i, j, kk: (j, i, 0))
        out_shape = [jax.ShapeDtypeStruct((N_DEV, m, tn), dt) for dt in out_dtypes]
    else:
        out_spec = tile_spec
        out_shape = [jax.ShapeDtypeStruct((m, n), dt) for dt in out_dtypes]
    outs = pl.pallas_call(
        body,
        name=name,
        grid=(m // tm, n // tn, nk),
        in_specs=[a_spec, b_spec] + [tile_spec] * n_te + [col_spec] * n_ce + [ANY] * len(after),
        out_specs=[out_spec] * n_out,
        out_shape=out_shape,
        scratch_shapes=[pltpu.VMEM((tm, tn), F32)] if nk > 1 else [],
        compiler_params=_params(("parallel", "parallel", "arbitrary")),
    )(a, b, *tile_extras, *col_extras, *after)
    return outs[0] if n_out == 1 else tuple(outs)


def _rowwise(fn, rows, bcast, out_rows, out_accs, *, name, tb_cap=256, after=()):
    t = rows[0].shape[0]
    tb = _row_tile(t, tb_cap)
    n_r, n_b, n_o, n_a = len(rows), len(bcast), len(out_rows), len(out_accs)

    def body(*refs):
        r_refs = refs[:n_r]
        b_refs = refs[n_r:n_r + n_b]
        first_out = n_r + n_b + len(after)
        o_refs = refs[first_out:first_out + n_o]
        a_refs = refs[first_out + n_o:]
        outs, accs = fn([r[...] for r in r_refs], [r[...] for r in b_refs])
        for o_ref, o in zip(o_refs, outs):
            o_ref[...] = o.astype(o_ref.dtype)
        if n_a:
            i = pl.program_id(0)

            @pl.when(i == 0)
            def _():
                for a_ref, acc in zip(a_refs, accs):
                    a_ref[...] = acc

            @pl.when(i > 0)
            def _():
                for a_ref, acc in zip(a_refs, accs):
                    a_ref[...] += acc

    in_specs = [pl.BlockSpec((tb, r.shape[1]), lambda i: (i, 0)) for r in rows]
    in_specs += [pl.BlockSpec(b.shape, lambda i: (0, 0)) for b in bcast]
    in_specs += [ANY] * len(after)
    out_specs = [pl.BlockSpec((tb, c), lambda i: (i, 0)) for c, _ in out_rows]
    out_specs += [pl.BlockSpec(s, lambda i: (0, 0)) for s in out_accs]
    out_shape = [jax.ShapeDtypeStruct((t, c), dt) for c, dt in out_rows]
    out_shape += [jax.ShapeDtypeStruct(s, F32) for s in out_accs]
    outs = pl.pallas_call(
        body,
        name=name,
        grid=(t // tb,),
        in_specs=in_specs,
        out_specs=out_specs,
        out_shape=out_shape,
        compiler_params=_params(("arbitrary",)),
    )(*rows, *bcast, *after)
    return tuple(outs)


def _colsum(v):
    return jnp.sum(v, axis=0, keepdims=True)


def _sigmoid(v):
    return 1.0 / (1.0 + jnp.exp(-v))


def _ln_fwd(x_res, y_sub, g, b, *, name, after=()):
    d = x_res.shape[1]

    def fn(rows, bc):
        z = ALPHA * rows[0] + rows[1]
        mu = jnp.mean(z, axis=-1, keepdims=True)
        zc = z - mu
        var = jnp.mean(zc * zc, axis=-1, keepdims=True)
        rstd = lax.rsqrt(var + LN_EPS)
        xhat = zc * rstd
        out = xhat * bc[0] + bc[1]
        return (out, out, xhat, rstd), ()

    return _rowwise(fn, [x_res, y_sub], [g, b], [(d, F32), (d, MXU_DTYPE), (d, F32), (1, F32)], [], name=name,
                    after=after, tb_cap=WIDE_ROWS)


def _ln_fwd_loss(x_res, y_sub, g, b, target, *, name, after=()):
    d = x_res.shape[1]

    def fn(rows, bc):
        z = ALPHA * rows[0] + rows[1]
        mu = jnp.mean(z, axis=-1, keepdims=True)
        zc = z - mu
        var = jnp.mean(zc * zc, axis=-1, keepdims=True)
        rstd = lax.rsqrt(var + LN_EPS)
        xhat = zc * rstd
        err = xhat * bc[0] + bc[1] - rows[2]
        return (xhat, rstd, err * (1.0 / d)), (_colsum(err * err),)

    return _rowwise(fn, [x_res, y_sub, target], [g, b], [(d, F32), (1, F32), (d, F32)], [(1, d)], name=name,
                    after=after, tb_cap=WIDE_ROWS)


def _ln_bwd(terms, xhat, rstd, g, *, name, after=()):
    d = xhat.shape[1]
    coefs = [c for c, _ in terms]

    def fn(rows, bc):
        xh, rs = rows[0], rows[1]
        dout = None
        for c, v in zip(coefs, rows[2:]):
            tv = v if c == 1.0 else c * v
            dout = tv if dout is None else dout + tv
        dxh = dout * bc[0]
        m1 = jnp.mean(dxh, axis=-1, keepdims=True)
        m2 = jnp.mean(dxh * xh, axis=-1, keepdims=True)
        dz = rs * (dxh - m1 - xh * m2)
        return (dz, dz), (_colsum(dout * xh), _colsum(dout), _colsum(dz))

    return _rowwise(fn, [xhat, rstd] + [v for _, v in terms], [g], [(d, F32), (d, MXU_DTYPE)],
                    [(1, d), (1, d), (1, d)], name=name, after=after, tb_cap=WIDE_ROWS)


def _rms_fwd(x, g, *, name):
    c = x.shape[1]

    def fn(rows, bc):
        v = rows[0]
        r = lax.rsqrt(jnp.mean(v * v, axis=-1, keepdims=True) + RMS_EPS)
        return (v * r * bc[0],), ()

    return _rowwise(fn, [x], [g], [(c, MXU_DTYPE)], [], name=name, tb_cap=NARROW_ROWS)[0]


def _rms_bwd(x, g, dy, *, name):
    c = x.shape[1]

    def fn(rows, bc):
        v, dyv = rows
        r = lax.rsqrt(jnp.mean(v * v, axis=-1, keepdims=True) + RMS_EPS)
        xh = v * r
        dxh = dyv * bc[0]
        dx = r * (dxh - xh * jnp.mean(dxh * xh, axis=-1, keepdims=True))
        return (dx,), (_colsum(dyv * xh),)

    return _rowwise(fn, [x, dy], [g], [(c, F32)], [(1, c)], name=name, tb_cap=NARROW_ROWS)


def _conv_chunk(t):
    return _row_tile(t, 256)


ATTN_SCALE = (QK_NOPE + QK_ROPE) ** -0.5
_NT = (((1,), (1,)), ((), ()))
_TN = (((0,), (0,)), ((), ()))
_NN = (((1,), (0,)), ((), ()))


def _dot(a, b, dims):
    return lax.dot_general(a, b, dims, preferred_element_type=F32)


def _block_scores(q_cat, kn_ref, kp_ref, lo, hi, diagonal):
    k_cat = jnp.concatenate([kn_ref[lo:hi, :], kp_ref[lo:hi, :]], axis=1)
    s = _dot(q_cat, k_cat, _NT) * ATTN_SCALE
    if diagonal:
        shift = CHUNK.bit_length() - 1
        rows = lax.broadcasted_iota(jnp.int32, s.shape, 0)
        cols = lax.broadcasted_iota(jnp.int32, s.shape, 1)
        s = jnp.where(jnp.right_shift(cols, shift) <= jnp.right_shift(rows, shift), s, MASK_VALUE)
    return s, k_cat


def _attn_fwd(qn, qp, kv, kp, *, name):
    h, t, _ = qp.shape
    tq = _row_tile(t, 256)
    assert tq % CHUNK == 0

    def body(qn_ref, qp_ref, kn_ref, kp_ref, v_ref, o_ref, of_ref, lse_ref):
        for r0 in range(0, t, tq):
            r1 = r0 + tq
            q_cat = jnp.concatenate([qn_ref[r0:r1, :].astype(MXU_DTYPE), qp_ref[r0:r1, :]], axis=1)
            s_d, _ = _block_scores(q_cat, kn_ref, kp_ref, r0, r1, True)
            m = jnp.max(s_d, axis=-1, keepdims=True)
            if r0:
                s_b, _ = _block_scores(q_cat, kn_ref, kp_ref, 0, r0, False)
                m = jnp.maximum(m, jnp.max(s_b, axis=-1, keepdims=True))
            p_d = jnp.exp(s_d - m)
            l = jnp.sum(p_d, axis=-1, keepdims=True)
            acc = _dot(p_d.astype(MXU_DTYPE), v_ref[r0:r1, :], _NN)
            if r0:
                p_b = jnp.exp(s_b - m)
                l = l + jnp.sum(p_b, axis=-1, keepdims=True)
                acc = acc + _dot(p_b.astype(MXU_DTYPE), v_ref[0:r0, :], _NN)
            o = acc / l
            o_ref[r0:r1, :] = o.astype(o_ref.dtype)
            of_ref[r0:r1, :] = o
            lse_ref[r0:r1, :] = m + jnp.log(l)

    blk = lambda w: pl.BlockSpec((None, t, w), lambda hh: (hh, 0, 0))
    cols = pl.BlockSpec((t, QK_NOPE), lambda hh: (0, hh))
    keys = pl.BlockSpec((t, QK_NOPE), lambda hh: (0, 2 * hh))
    values = pl.BlockSpec((t, V_HEAD), lambda hh: (0, 2 * hh + 1))
    return pl.pallas_call(
        body,
        name=name,
        grid=(h,),
        in_specs=[cols, blk(QK_ROPE), keys, pl.BlockSpec((t, QK_ROPE), lambda hh: (0, 0)), values],
        out_specs=[cols, cols, blk(1)],
        out_shape=[jax.ShapeDtypeStruct((t, h * V_HEAD), MXU_DTYPE), jax.ShapeDtypeStruct((t, h * V_HEAD), F32),
                   jax.ShapeDtypeStruct((h, t, 1), F32)],
        compiler_params=_params(("parallel",)),
    )(qn, qp, kv, kp, kv)


def _attn_bwd(qn, qp, kv, kp, do, o, lse, *, name):
    h, t, _ = qp.shape
    tq = _row_tile(t, 256)

    def body(qn_ref, qp_ref, kn_ref, kp_ref, v_ref, do_ref, o_ref, lse_ref, dqn_ref, dqp_ref, dkv_ref, dkp_ref):
        dkv_ref[...] = jnp.zeros_like(dkv_ref)

        @pl.when(pl.program_id(0) == 0)
        def _():
            dkp_ref[...] = jnp.zeros_like(dkp_ref)

        for r0 in range(0, t, tq):
            r1 = r0 + tq
            q_cat = jnp.concatenate([qn_ref[r0:r1, :].astype(MXU_DTYPE), qp_ref[r0:r1, :]], axis=1)
            dov, lse = do_ref[r0:r1, :], lse_ref[r0:r1, :]
            delta = jnp.sum(dov.astype(F32) * o_ref[r0:r1, :], axis=-1, keepdims=True)

            def piece(lo, hi, diagonal):
                s, k_cat = _block_scores(q_cat, kn_ref, kp_ref, lo, hi, diagonal)
                p = jnp.exp(s - lse)
                dp = _dot(dov, v_ref[lo:hi, :], _NT)
                ds_m = (p * (dp - delta) * ATTN_SCALE).astype(MXU_DTYPE)
                dk_cat = _dot(ds_m, q_cat, _TN)
                dkv_ref[lo:hi, :QK_NOPE] += dk_cat[:, :QK_NOPE]
                dkp_ref[lo:hi, :] += dk_cat[:, QK_NOPE:]
                dkv_ref[lo:hi, QK_NOPE:] += _dot(p.astype(MXU_DTYPE), dov, _TN)
                return _dot(ds_m, k_cat, _NN)

            dq_cat = piece(r0, r1, True)
            if r0:
                dq_cat = dq_cat + piece(0, r0, False)
            dqn_ref[r0:r1, :] = dq_cat[:, :QK_NOPE]
            dqp_ref[r0:r1, :] = dq_cat[:, QK_NOPE:]

    blk = lambda w: pl.BlockSpec((None, t, w), lambda hh: (hh, 0, 0))
    kpblk = pl.BlockSpec((t, QK_ROPE), lambda hh: (0, 0))
    cols = pl.BlockSpec((t, QK_NOPE), lambda hh: (0, hh))
    keys = pl.BlockSpec((t, QK_NOPE), lambda hh: (0, 2 * hh))
    values = pl.BlockSpec((t, V_HEAD), lambda hh: (0, 2 * hh + 1))
    both = pl.BlockSpec((t, QK_NOPE + V_HEAD), lambda hh: (0, hh))
    return pl.pallas_call(
        body,
        name=name,
        grid=(h,),
        in_specs=[cols, blk(QK_ROPE), keys, kpblk, values, cols, cols, blk(1)],
        out_specs=[cols, blk(QK_ROPE), both, kpblk],
        out_shape=[jax.ShapeDtypeStruct((t, h * QK_NOPE), F32), jax.ShapeDtypeStruct((h, t, QK_ROPE), F32),
                   jax.ShapeDtypeStruct((t, h * (QK_NOPE + V_HEAD)), F32), jax.ShapeDtypeStruct((t, QK_ROPE), F32)],
        compiler_params=_params(("arbitrary",)),
    )(qn, qp, kv, kp, kv, do, o, lse)


def _rope(x1, x2, cos, sin, *, name, inverse=False):
    w = x1.shape[1]
    sign = -1.0 if inverse else 1.0

    def fn(rows, bc):
        a, b, c, s = rows
        s = sign * s
        return (a * c - b * s, a * s + b * c), ()

    return _rowwise(fn, [x1, x2, cos, sin], [], [(w, F32), (w, F32)], [], name=name, tb_cap=NARROW_ROWS)


def _rope_in_place(x, cos_pairs, sin_signed, *, name, inverse=False):
    w = x.shape[1]
    half = ROPE_HALF
    sign = -1.0 if inverse else 1.0

    def fn(rows, bc):
        v, c, s = rows
        lane = lax.broadcasted_iota(jnp.int32, v.shape, 1)
        partner = jnp.where((lane & half) == 0, pltpu.roll(v, w - half, 1), pltpu.roll(v, half, 1))
        return (v * c + partner * (sign * s),), ()

    return _rowwise(fn, [x, cos_pairs, sin_signed], [], [(w, F32)], [], name=name, tb_cap=NARROW_ROWS)[0]


def _mlp_up(x_m, w1, tag, after=()):
    def epi(acc):
        r = jnp.maximum(acc, 0.0)
        return (r * r,)

    return _mm(x_m, w1, name=f"{tag}_up", out_dtypes=(MXU_DTYPE,), epilogue=epi, after=after)


def _mlp_fwd(x_m, w1, w2, tag, after=(), act=None):
    if act is None:
        act, after = _mlp_up(x_m, w1, tag, after=after), ()
    y = _mm(act, w2, name=f"{tag}_down", after=after)
    return y, (x_m, act)


def _plus_residual(acc, res):
    return (acc + ALPHA * res,)


def _mlp_bwd(dy_m, saved, w1, w2, tag, residual, after=()):
    x_m, act = saved
    dw2 = _mm(act, dy_m, ta=True, name=f"{tag}_dw2", out_dtypes=(MXU_DTYPE,))

    def epi(acc, av):
        return (acc * (2.0 * jnp.sqrt(av.astype(F32))),)

    dh = _mm(dy_m, w2, tb=True, name=f"{tag}_dact", out_dtypes=(MXU_DTYPE,), epilogue=epi, tile_extras=(act,),
             after=after)
    dw1 = _mm(x_m, dh, ta=True, out_dev=True, name=f"{tag}_dw1", out_dtypes=(MXU_DTYPE,))
    dx = _mm(dh, w1, tb=True, name=f"{tag}_dx", epilogue=_plus_residual, tile_extras=(residual,))
    return [dx], {"ff_w1": dw1, "ff_w2": dw2}


def _sc_gate_specs(t, d):
    per = d // LANES
    return [pl.BlockSpec((t, LANES), lambda j, k=k: (0, k * per + j)) for k in range(3)]


def _sc_gate_fwd(u, conv_w, *, name):
    t, d = u.shape[0], u.shape[1] // 3
    kw = conv_w.shape[0]
    ch = _conv_chunk(t)

    def body(b_ref, c_ref, h_ref, w_ref, r_ref, pad_ref):
        pad_ref[0:CONV_PAD, :] = jnp.zeros((CONV_PAD, LANES), F32)
        pad_ref[CONV_PAD:CONV_PAD + t, :] = c_ref[...] * h_ref[...]
        for t0 in range(0, t, ch):
            q = jnp.zeros((ch, LANES), F32)
            for k in range(kw):
                s = kw - 1 - k
                q = q + w_ref[k:k + 1, :] * pad_ref[CONV_PAD - s + t0:CONV_PAD - s + t0 + ch, :]
            r_ref[t0:t0 + ch, :] = (b_ref[t0:t0 + ch, :] * q).astype(r_ref.dtype)

    blk = pl.BlockSpec((t, LANES), lambda j: (0, j))
    return pl.pallas_call(
        body,
        name=name,
        grid=(d // LANES,),
        in_specs=_sc_gate_specs(t, d) + [pl.BlockSpec((kw, LANES), lambda j: (0, j))],
        out_specs=blk,
        out_shape=jax.ShapeDtypeStruct((t, d), MXU_DTYPE),
        scratch_shapes=[pltpu.VMEM((CONV_PAD + t, LANES), F32)],
        compiler_params=_params(("parallel",)),
    )(u, u, u, conv_w)


def _sc_gate_bwd(dr, u, conv_w, *, name):
    t, d = u.shape[0], u.shape[1] // 3
    kw = conv_w.shape[0]
    ch = _conv_chunk(t)

    def body(dr_ref, b_ref, c_ref, h_ref, w_ref, db_ref, dc_ref, dh_ref, dw_ref, ppad_ref, dpad_ref):
        ppad_ref[0:CONV_PAD, :] = jnp.zeros((CONV_PAD, LANES), F32)
        ppad_ref[CONV_PAD:CONV_PAD + t, :] = c_ref[...] * h_ref[...]
        dpad_ref[0:t, :] = dr_ref[...] * b_ref[...]
        dpad_ref[t:t + CONV_PAD, :] = jnp.zeros((CONV_PAD, LANES), F32)
        for t0 in range(0, t, ch):
            q = jnp.zeros((ch, LANES), F32)
            dp = jnp.zeros((ch, LANES), F32)
            for k in range(kw):
                s = kw - 1 - k
                q = q + w_ref[k:k + 1, :] * ppad_ref[CONV_PAD - s + t0:CONV_PAD - s + t0 + ch, :]
                dp = dp + w_ref[k:k + 1, :] * dpad_ref[t0 + s:t0 + s + ch, :]
            db_ref[t0:t0 + ch, :] = (dr_ref[t0:t0 + ch, :] * q).astype(db_ref.dtype)
            dc_ref[t0:t0 + ch, :] = (dp * h_ref[t0:t0 + ch, :]).astype(dc_ref.dtype)
            dh_ref[t0:t0 + ch, :] = (dp * c_ref[t0:t0 + ch, :]).astype(dh_ref.dtype)
        for k in range(kw):
            s = kw - 1 - k
            acc = jnp.zeros((ch, LANES), F32)
            for t0 in range(0, t, ch):
                acc = acc + dpad_ref[t0:t0 + ch, :] * ppad_ref[CONV_PAD - s + t0:CONV_PAD - s + t0 + ch, :]
            dw_ref[k:k + 1, :] = _colsum(acc)

    blk = pl.BlockSpec((t, LANES), lambda j: (0, j))
    wblk = pl.BlockSpec((kw, LANES), lambda j: (0, j))
    return pl.pallas_call(
        body,
        name=name,
        grid=(d // LANES,),
        in_specs=[blk] + _sc_gate_specs(t, d) + [wblk],
        out_specs=[blk, blk, blk, wblk],
        out_shape=[jax.ShapeDtypeStruct((t, d), MXU_DTYPE)] * 3 + [jax.ShapeDtypeStruct((kw, d), F32)],
        scratch_shapes=[pltpu.VMEM((CONV_PAD + t, LANES), F32), pltpu.VMEM((CONV_PAD + t, LANES), F32)],
        compiler_params=_params(("parallel",)),
    )(dr, u, u, u, conv_w)


def _sc_fwd(x_m, w_in, conv_w, w_out, tag, after=()):
    u = _mm(x_m, w_in, name=f"{tag}_in", after=after)
    r = _sc_gate_fwd(u, conv_w, name=f"{tag}_gate")
    y = _mm(r, w_out, name=f"{tag}_out")
    return y, (x_m, u, r)


def _sc_bwd(dy_m, saved, w_in, conv_w, w_out, tag, residual, after=()):
    d = D_MODEL
    x_m, u, r = saved
    dw_out = _mm(r, dy_m, ta=True, name=f"{tag}_dwout", out_dtypes=(MXU_DTYPE,))
    dr = _mm(dy_m, w_out, tb=True, name=f"{tag}_dr", after=after)
    db, dc, dh, dconv = _sc_gate_bwd(dr, u, conv_w, name=f"{tag}_dgate")
    du = jnp.concatenate([db, dc, dh], axis=1)
    dw_in = _mm(x_m, du, ta=True, out_dev=True, name=f"{tag}_dwin", out_dtypes=(MXU_DTYPE,))
    dx = _mm(du, w_in, tb=True, name=f"{tag}_dx", epilogue=_plus_residual, tile_extras=(residual,))
    return [dx], {"sc_w_in": dw_in, "sc_conv_w": dconv, "sc_w_out": dw_out}


def _cf_glu_specs(t, d):
    per = d // LANES
    return [pl.BlockSpec((t, LANES), lambda j, k=k: (0, k * per + j)) for k in range(2)]


def _cf_glu_conv_fwd(u, w, bias, *, name):
    t, d = u.shape[0], u.shape[1] // 2
    kw = w.shape[0]
    ch = _conv_chunk(t)

    def body(a_ref, g_ref, w_ref, b_ref, o_ref, pad_ref):
        pad_ref[0:CONV_PAD, :] = jnp.zeros((CONV_PAD, LANES), F32)
        pad_ref[CONV_PAD:CONV_PAD + t, :] = a_ref[...] * _sigmoid(g_ref[...])
        for t0 in range(0, t, ch):
            acc = jnp.broadcast_to(b_ref[...], (ch, LANES))
            for k in range(kw):
                s = kw - 1 - k
                acc = acc + w_ref[k:k + 1, :] * pad_ref[CONV_PAD - s + t0:CONV_PAD - s + t0 + ch, :]
            o_ref[t0:t0 + ch, :] = acc

    return pl.pallas_call(
        body,
        name=name,
        grid=(d // LANES,),
        in_specs=_cf_glu_specs(t, d) + [pl.BlockSpec((kw, LANES), lambda j: (0, j)),
                                        pl.BlockSpec((1, LANES), lambda j: (0, j))],
        out_specs=pl.BlockSpec((t, LANES), lambda j: (0, j)),
        out_shape=jax.ShapeDtypeStruct((t, d), F32),
        scratch_shapes=[pltpu.VMEM((CONV_PAD + t, LANES), F32)],
        compiler_params=_params(("parallel",)),
    )(u, u, w, bias)


def _cf_glu_conv_bwd(dy, u, w, *, name):
    t, d = u.shape[0], u.shape[1] // 2
    kw = w.shape[0]
    ch = _conv_chunk(t)

    def body(dy_ref, a_ref, g_ref, w_ref, da_ref, dg_ref, sa_ref, sg_ref, dw_ref, xpad_ref, dpad_ref):
        xpad_ref[0:CONV_PAD, :] = jnp.zeros((CONV_PAD, LANES), F32)
        xpad_ref[CONV_PAD:CONV_PAD + t, :] = a_ref[...] * _sigmoid(g_ref[...])
        dpad_ref[0:t, :] = dy_ref[...]
        dpad_ref[t:t + CONV_PAD, :] = jnp.zeros((CONV_PAD, LANES), F32)
        sum_a = jnp.zeros((1, LANES), F32)
        sum_g = jnp.zeros((1, LANES), F32)
        for t0 in range(0, t, ch):
            dh1 = jnp.zeros((ch, LANES), F32)
            for k in range(kw):
                s = kw - 1 - k
                dh1 = dh1 + w_ref[k:k + 1, :] * dpad_ref[t0 + s:t0 + s + ch, :]
            sg = _sigmoid(g_ref[t0:t0 + ch, :])
            da = dh1 * sg
            dgate = da * a_ref[t0:t0 + ch, :] * (1.0 - sg)
            da_ref[t0:t0 + ch, :] = da.astype(da_ref.dtype)
            dg_ref[t0:t0 + ch, :] = dgate.astype(dg_ref.dtype)
            sum_a = sum_a + _colsum(da)
            sum_g = sum_g + _colsum(dgate)
        sa_ref[...] = sum_a
        sg_ref[...] = sum_g
        for k in range(kw):
            s = kw - 1 - k
            acc = jnp.zeros((ch, LANES), F32)
            for t0 in range(0, t, ch):
                acc = acc + dy_ref[t0:t0 + ch, :] * xpad_ref[CONV_PAD - s + t0:CONV_PAD - s + t0 + ch, :]
            dw_ref[k:k + 1, :] = _colsum(acc)

    blk = pl.BlockSpec((t, LANES), lambda j: (0, j))
    row = pl.BlockSpec((1, LANES), lambda j: (0, j))
    wblk = pl.BlockSpec((kw, LANES), lambda j: (0, j))
    return pl.pallas_call(
        body,
        name=name,
        grid=(d // LANES,),
        in_specs=[blk] + _cf_glu_specs(t, d) + [wblk],
        out_specs=[blk, blk, row, row, wblk],
        out_shape=[jax.ShapeDtypeStruct((t, d), MXU_DTYPE)] * 2 + [jax.ShapeDtypeStruct((1, d), F32)] * 2
        + [jax.ShapeDtypeStruct((kw, d), F32)],
        scratch_shapes=[pltpu.VMEM((CONV_PAD + t, LANES), F32), pltpu.VMEM((CONV_PAD + t, LANES), F32)],
        compiler_params=_params(("parallel",)),
    )(dy, u, u, w)


def _cf_fwd(x_m, w, tag, after=()):
    d = D_MODEL

    def bias_epi(acc, bias):
        return (acc + bias,)

    u = _mm(x_m, w["cf_w_pw1"], name=f"{tag}_pw1", epilogue=bias_epi, col_extras=(w["cf_b_pw1"],),
            after=after)
    h2 = _cf_glu_conv_fwd(u, w["cf_dw_w"], w["cf_dw_b"], name=f"{tag}_conv")

    def norm_fn(rows, bc):
        hv = rows[0]
        mu = jnp.mean(hv, axis=-1, keepdims=True)
        hc = hv - mu
        rstd = lax.rsqrt(jnp.mean(hc * hc, axis=-1, keepdims=True) + LN_EPS)
        h3 = hc * rstd * bc[0] + bc[1]
        return (h3 * _sigmoid(h3),), ()

    h4 = _rowwise(norm_fn, [h2], [w["cf_norm_g"], w["cf_norm_b"]], [(d, MXU_DTYPE)], [], name=f"{tag}_norm",
                  tb_cap=WIDE_ROWS)[0]
    y = _mm(h4, w["cf_w_pw2"], name=f"{tag}_pw2", epilogue=bias_epi, col_extras=(w["cf_b_pw2"],))
    return y, (x_m, u, h2, h4)


def _cf_bwd(dy_m, dy_colsum, saved, w, tag, residual, after=()):
    d = D_MODEL
    x_m, u, h2, h4 = saved
    dw_pw2 = _mm(h4, dy_m, ta=True, name=f"{tag}_dwpw2", out_dtypes=(MXU_DTYPE,))
    dh4 = _mm(dy_m, w["cf_w_pw2"], tb=True, name=f"{tag}_dh4", after=after)

    def dnorm_fn(rows, bc):
        dh4v, hv = rows
        g, b = bc
        mu = jnp.mean(hv, axis=-1, keepdims=True)
        hc = hv - mu
        rstd = lax.rsqrt(jnp.mean(hc * hc, axis=-1, keepdims=True) + LN_EPS)
        xh = hc * rstd
        h3 = xh * g + b
        sg = _sigmoid(h3)
        dh3 = dh4v * (sg * (1.0 + h3 * (1.0 - sg)))
        dxh = dh3 * g
        dh2 = rstd * (dxh - jnp.mean(dxh, axis=-1, keepdims=True) - xh * jnp.mean(dxh * xh, axis=-1, keepdims=True))
        return (dh2,), (_colsum(dh3 * xh), _colsum(dh3), _colsum(dh2))

    dh2, dnorm_g, dnorm_b, ddw_b = _rowwise(dnorm_fn, [dh4, h2], [w["cf_norm_g"], w["cf_norm_b"]], [(d, F32)],
                                            [(1, d), (1, d), (1, d)], name=f"{tag}_dnorm", tb_cap=WIDE_ROWS)
    da, dgate, sum_a, sum_g, ddw_w = _cf_glu_conv_bwd(dh2, u, w["cf_dw_w"], name=f"{tag}_dconv")
    du = jnp.concatenate([da, dgate], axis=1)
    db_pw1 = jnp.concatenate([sum_a, sum_g], axis=1)
    dw_pw1 = _mm(x_m, du, ta=True, out_dev=True, name=f"{tag}_dwpw1", out_dtypes=(MXU_DTYPE,))
    dx = _mm(du, w["cf_w_pw1"], tb=True, name=f"{tag}_dx", epilogue=_plus_residual, tile_extras=(residual,))
    grads = {"cf_w_pw1": dw_pw1, "cf_b_pw1": db_pw1, "cf_dw_w": ddw_w, "cf_dw_b": ddw_b, "cf_norm_g": dnorm_g,
             "cf_norm_b": dnorm_b, "cf_w_pw2": dw_pw2, "cf_b_pw2": dy_colsum}
    return [dx], grads


def _rope_tables(t):
    pos = jnp.arange(t, dtype=F32)
    inv_freq = ROPE_THETA ** (-jnp.arange(0, QK_ROPE, 2, dtype=F32) / QK_ROPE)
    ang = pos[:, None] * inv_freq[None, :]
    return jnp.cos(ang), jnp.sin(ang)


ROPE_HALF = QK_ROPE // 2
Q_NOPE_COLS = MLA_HEADS * QK_NOPE


def _interleave_heads(w_uk, w_uv):
    c = w_uk.shape[0]
    return jnp.concatenate([w_uk.reshape(c, MLA_HEADS, QK_NOPE), w_uv.reshape(c, MLA_HEADS, V_HEAD)],
                           axis=2).reshape(c, -1)


def _mla_fwd(x_m, w, tag, after=()):
    t = x_m.shape[0]
    cos, sin = _rope_tables(t)
    cos_h = jnp.tile(jnp.concatenate([cos, cos], axis=1), (1, MLA_HEADS))
    sin_h = jnp.tile(jnp.concatenate([-sin, sin], axis=1), (1, MLA_HEADS))
    cq_pre = _mm(x_m, w["mla_w_dq"], name=f"{tag}_dq", after=after)
    cq = _rms_fwd(cq_pre, w["mla_g_q"], name=f"{tag}_qnorm")
    q = _mm(cq, w["mla_w_uq"], name=f"{tag}_uq")
    qp = _rope_in_place(q[:, Q_NOPE_COLS:], cos_h, sin_h, name=f"{tag}_qrope")
    qp_h = qp.reshape(t, MLA_HEADS, QK_ROPE).transpose(1, 0, 2).astype(MXU_DTYPE)
    ckv_full = _mm(x_m, w["mla_w_dkv"], name=f"{tag}_dkv")
    ckv_pre = ckv_full[:, :KV_LORA]
    ckv = _rms_fwd(ckv_pre, w["mla_g_kv"], name=f"{tag}_kvnorm")
    k1, k2 = _rope(ckv_full[:, KV_LORA:KV_LORA + ROPE_HALF], ckv_full[:, KV_LORA + ROPE_HALF:], cos, sin,
                   name=f"{tag}_krope")
    kp = jnp.concatenate([k1, k2], axis=-1).astype(MXU_DTYPE)
    kv = _mm(ckv, _interleave_heads(w["mla_w_uk"], w["mla_w_uv"]), name=f"{tag}_ukv", out_dtypes=(MXU_DTYPE,))
    o, o_f, lse = _attn_fwd(q, qp_h, kv, kp, name=f"{tag}_attn")
    y = _mm(o, w["mla_w_o"], name=f"{tag}_o")
    return y, (x_m, cq_pre, cq, ckv_pre, ckv, q, qp_h, kv, kp, lse, o, o_f, (cos, sin, cos_h, sin_h))


def _mla_bwd(dy_m, saved, w, tag, residual, after=()):
    x_m, cq_pre, cq, ckv_pre, ckv, q, qp_h, kv, kp, lse, o, o_f, (cos, sin, cos_h, sin_h) = saved
    t = x_m.shape[0]
    half = ROPE_HALF
    dw_o = _mm(o, dy_m, ta=True, name=f"{tag}_dwo", out_dtypes=(MXU_DTYPE,))
    do = _mm(dy_m, w["mla_w_o"], tb=True, name=f"{tag}_do", out_dtypes=(MXU_DTYPE,), after=after)
    dqn, dqp_h, dkv, dkp = _attn_bwd(q, qp_h, kv, kp, do, o_f, lse, name=f"{tag}_dattn")
    dqp = _rope_in_place(dqp_h.transpose(1, 0, 2).reshape(t, -1), cos_h, sin_h, name=f"{tag}_dqrope", inverse=True)
    dq = jnp.concatenate([dqn, dqp], axis=-1).astype(MXU_DTYPE)
    dw_uq = _mm(cq, dq, ta=True, name=f"{tag}_dwuq", out_dtypes=(MXU_DTYPE,))
    dcq = _mm(dq, w["mla_w_uq"], tb=True, name=f"{tag}_dcq")
    dcq_pre, dg_q = _rms_bwd(cq_pre, w["mla_g_q"], dcq, name=f"{tag}_dqnorm")
    dcq_pre_m = dcq_pre.astype(MXU_DTYPE)
    dw_dq = _mm(x_m, dcq_pre_m, ta=True, name=f"{tag}_dwdq", out_dtypes=(MXU_DTYPE,))
    dx_q = _mm(dcq_pre_m, w["mla_w_dq"], tb=True, name=f"{tag}_dxq", epilogue=_plus_residual, tile_extras=(residual,))
    dw_kv = _mm(ckv, dkv, ta=True, name=f"{tag}_dwukv", out_dtypes=(MXU_DTYPE,))
    dw_kv = dw_kv.reshape(KV_LORA, MLA_HEADS, QK_NOPE + V_HEAD)
    dw_uk, dw_uv = dw_kv[:, :, :QK_NOPE].reshape(KV_LORA, -1), dw_kv[:, :, QK_NOPE:].reshape(KV_LORA, -1)
    dckv = _mm(dkv, _interleave_heads(w["mla_w_uk"], w["mla_w_uv"]), tb=True, name=f"{tag}_dckv")
    dckv_pre, dg_kv = _rms_bwd(ckv_pre, w["mla_g_kv"], dckv, name=f"{tag}_dkvnorm")
    dk1, dk2 = _rope(dkp[:, :half], dkp[:, half:], cos, sin, name=f"{tag}_dkrope", inverse=True)
    dckv_full = jnp.concatenate([dckv_pre, dk1, dk2], axis=-1).astype(MXU_DTYPE)
    dw_dkv = _mm(x_m, dckv_full, ta=True, name=f"{tag}_dwdkv", out_dtypes=(MXU_DTYPE,))
    dx_kv = _mm(dckv_full, w["mla_w_dkv"], tb=True, name=f"{tag}_dxkv", epilogue=lambda acc, other: (acc + other,),
                tile_extras=(dx_q,))
    grads = {"mla_w_dq": dw_dq, "mla_g_q": dg_q, "mla_w_uq": dw_uq, "mla_w_dkv": dw_dkv, "mla_g_kv": dg_kv,
             "mla_w_uk": dw_uk, "mla_w_uv": dw_uv, "mla_w_o": dw_o}
    return [dx_kv], grads


def _cast_input(x):
    cast = lambda rows, bc: ((rows[0],), ())
    return _rowwise(cast, [x], [], [(D_MODEL, MXU_DTYPE)], [], name="x_cast")[0]


def _hooked(value, hook):
    token = None if hook is None else hook(value)
    return () if token is None else (token,)


def _mix_fwd(i, cur, cur_m, wl, hook=None, after=()):
    mixer = i % N_MIXERS
    if mixer == 0:
        y, s_mix = _sc_fwd(cur_m, wl["sc_w_in"], wl["sc_conv_w"], wl["sc_w_out"], f"l{i}_sc", after=after)
    elif mixer == 1:
        y, s_mix = _mla_fwd(cur_m, wl, f"l{i}_mla", after=after)
    else:
        y, s_mix = _cf_fwd(cur_m, wl, f"l{i}_cf", after=after)
    cur, cur_m, xh_mix, rs_mix = _ln_fwd(cur, y, wl["ln_mix_g"], wl["ln_mix_b"], name=f"l{i}_lnmix",
                                         after=_hooked(y, hook))
    return cur, cur_m, (s_mix, xh_mix, rs_mix)


def _ff_fwd(i, cur, cur_m, wl, hook=None, after=(), loss_target=None, act=None):
    y, s_ff = _mlp_fwd(cur_m, wl["ff_w1"], wl["ff_w2"], f"l{i}_ff", after=after, act=act)
    if loss_target is not None:
        xh_ff, rs_ff, dy, sq = _ln_fwd_loss(cur, y, wl["ln_ff_g"], wl["ln_ff_b"], loss_target, name=f"l{i}_lnff_loss",
                                            after=_hooked(y, hook))
        return (0.5 / D_MODEL) * jnp.sum(sq), dy, (s_ff, xh_ff, rs_ff)
    cur, cur_m, xh_ff, rs_ff = _ln_fwd(cur, y, wl["ln_ff_g"], wl["ln_ff_b"], name=f"l{i}_lnff", after=_hooked(y, hook))
    return cur, cur_m, (s_ff, xh_ff, rs_ff)


def _layer_fwd(i, cur, cur_m, wl):
    cur, cur_m, s_mix = _mix_fwd(i, cur, cur_m, wl)
    cur, cur_m, s_ff = _ff_fwd(i, cur, cur_m, wl)
    return cur, cur_m, (*s_mix, *s_ff)


def _loss_and_grad(out, target):
    d = D_MODEL

    def loss_fn(rows, bc):
        err = rows[0] - rows[1]
        return (err * (1.0 / d),), (_colsum(err * err),)

    dy, sq = _rowwise(loss_fn, [out, target], [], [(d, F32)], [(1, d)], name="loss")
    return (0.5 / d) * jnp.sum(sq), dy


def _ff_bwd(i, terms, saved, wl, hook=None, after=()):
    s_ff, xh_ff, rs_ff = saved
    grads = {}
    dz, dz_m, grads["ln_ff_g"], grads["ln_ff_b"], _ = _ln_bwd(terms, xh_ff, rs_ff, wl["ln_ff_g"], name=f"l{i}_dlnff",
                                                              after=after)
    dxs, g_ff = _mlp_bwd(dz_m, s_ff, wl["ff_w1"], wl["ff_w2"], f"l{i}_ff", dz, after=_hooked(dz, hook))
    grads.update(g_ff)
    return [(1.0, v) for v in dxs], grads


def _mix_bwd(i, terms, saved, wl, hook=None, after=()):
    mixer = i % N_MIXERS
    s_mix, xh_mix, rs_mix = saved
    grads = {}
    dz, dz_m, grads["ln_mix_g"], grads["ln_mix_b"], dz_sum = _ln_bwd(terms, xh_mix, rs_mix, wl["ln_mix_g"],
                                                                     name=f"l{i}_dlnmix", after=after)
    behind = _hooked(dz, hook)
    if mixer == 0:
        dxs, g_mix = _sc_bwd(dz_m, s_mix, wl["sc_w_in"], wl["sc_conv_w"], wl["sc_w_out"], f"l{i}_sc", dz, after=behind)
    elif mixer == 1:
        dxs, g_mix = _mla_bwd(dz_m, s_mix, wl, f"l{i}_mla", dz, after=behind)
    else:
        dxs, g_mix = _cf_bwd(dz_m, dz_sum, s_mix, wl, f"l{i}_cf", dz, after=behind)
    grads.update(g_mix)
    return [(1.0, v) for v in dxs], grads


def _layer_bwd(i, terms, saved, wl):
    terms, g_ff = _ff_bwd(i, terms, saved[3:], wl)
    terms, g_mix = _mix_bwd(i, terms, saved[:3], wl)
    return terms, {**g_ff, **g_mix}


def _sum_terms(terms, *, name):
    coefs = [c for c, _ in terms]

    def sum_fn(rows, bc):
        acc = None
        for c, v in zip(coefs, rows):
            tv = v if c == 1.0 else c * v
            acc = tv if acc is None else acc + tv
        return (acc,), ()

    return _rowwise(sum_fn, [v for _, v in terms], [], [(D_MODEL, F32)], [], name=name)[0]


def _local_step(x, target, layers):
    cur, cur_m = x, _cast_input(x)
    saved = []
    for i in range(DEPTH):
        cur, cur_m, s = _layer_fwd(i, cur, cur_m, layers[i])
        saved.append(s)
    loss, dy = _loss_and_grad(cur, target)
    terms = [(1.0, dy)]
    grads = [None] * DEPTH
    for i in reversed(range(DEPTH)):
        terms, grads[i] = _layer_bwd(i, terms, saved[i], layers[i])
    return loss, _sum_terms(terms, name="grad_x"), grads


MESH_ID = pl.DeviceIdType.MESH
ANY = pl.BlockSpec(memory_space=pl.ANY)


def _block_id(px, py, pc):
    return 4 * px + 2 * py + pc


COPIES_PER_GATHER = 7


def _all_gather(shards, *, name):
    n = len(shards)

    def body(*refs):
        x_refs, out_refs = refs[:n], refs[n:2 * n]
        send_sems, recv_sems, local_sems = refs[2 * n:]
        x, y, core = lax.axis_index("x"), lax.axis_index("y"), lax.axis_index("c")
        me, sibling = (x, y, core), (x, y, 1 - core)
        chips = [(1 - x, y), (x, 1 - y), (1 - x, 1 - y)]

        def copy(t, k, block, to, from_input=False):
            dst = out_refs[t].at[_block_id(*block)]
            return pltpu.make_async_remote_copy(
                src_ref=x_refs[t] if from_input else dst, dst_ref=dst,
                send_sem=send_sems.at[t * COPIES_PER_GATHER + k], recv_sem=recv_sems.at[t * COPIES_PER_GATHER + k],
                device_id=to, device_id_type=MESH_ID)

        started = []
        mine = [pltpu.make_async_copy(x_refs[t], out_refs[t].at[_block_id(*me)], local_sems.at[t]) for t in range(n)]
        for t in range(n):
            mine[t].start()
            first = [copy(t, 0, me, sibling, from_input=True)]
            first += [copy(t, 1 + j, me, (*chip, core), from_input=True) for j, chip in enumerate(chips)]
            for cp in first:
                cp.start()
            started += first
        for t in range(n):
            for j, chip in enumerate(chips):
                copy(t, 1 + j, (*chip, core), me).wait_recv()
                passed = copy(t, 4 + j, (*chip, core), sibling)
                passed.start()
                started.append(passed)
        for t in range(n):
            copy(t, 0, sibling, me).wait_recv()
            for j, chip in enumerate(chips):
                copy(t, 4 + j, (*chip, 1 - core), me).wait_recv()
        for cp in started:
            cp.wait_send()
        for cp in mine:
            cp.wait()

    n_sems = n * COPIES_PER_GATHER
    outs = pl.pallas_call(
        body,
        name=name,
        out_shape=[jax.ShapeDtypeStruct((N_DEV, *s.shape), s.dtype) for s in shards],
        in_specs=[ANY] * n,
        out_specs=[ANY] * n,
        scratch_shapes=[pltpu.SemaphoreType.DMA((n_sems,)), pltpu.SemaphoreType.DMA((n_sems,)),
                        pltpu.SemaphoreType.DMA((n,))],
    )(*shards)
    return list(outs)


def _swap_with_sibling(parts, *, name):
    n = len(parts)

    def body(*refs):
        p_refs, got_refs = refs[:n], refs[n:2 * n]
        send_sems, recv_sems = refs[2 * n:]
        x, y, core = lax.axis_index("x"), lax.axis_index("y"), lax.axis_index("c")
        chips = [(x, y), (1 - x, y), (x, 1 - y), (1 - x, 1 - y)]
        copies = []
        for t in range(n):
            for k, (px, py) in enumerate(chips):
                copies.append(pltpu.make_async_remote_copy(
                    src_ref=p_refs[t].at[_block_id(px, py, 1 - core)], dst_ref=got_refs[t].at[k],
                    send_sem=send_sems.at[4 * t + k], recv_sem=recv_sems.at[4 * t + k], device_id=(x, y, 1 - core),
                    device_id_type=MESH_ID))
        for cp in copies:
            cp.start()
        for cp in copies:
            cp.wait()

    outs = pl.pallas_call(
        body,
        name=name,
        out_shape=[jax.ShapeDtypeStruct((4, *p.shape[1:]), p.dtype) for p in parts],
        in_specs=[ANY] * n,
        out_specs=[ANY] * n,
        scratch_shapes=[pltpu.SemaphoreType.DMA((4 * n,)), pltpu.SemaphoreType.DMA((4 * n,))],
    )(*parts)
    return list(outs)


def _swap_between_chips(parts, *, name):
    n = len(parts)

    def body(*refs):
        p_refs, got_refs = refs[:n], refs[n:2 * n]
        send_sems, recv_sems = refs[2 * n:]
        x, y, core = lax.axis_index("x"), lax.axis_index("y"), lax.axis_index("c")
        chips = [(1 - x, y), (x, 1 - y), (1 - x, 1 - y)]
        copies = []
        for t in range(n):
            for j, (px, py) in enumerate(chips):
                copies.append(pltpu.make_async_remote_copy(
                    src_ref=p_refs[t].at[j], dst_ref=got_refs[t].at[j], send_sem=send_sems.at[3 * t + j],
                    recv_sem=recv_sems.at[3 * t + j], device_id=(px, py, core), device_id_type=MESH_ID))
        for cp in copies:
            cp.start()
        for cp in copies:
            cp.wait()

    outs = pl.pallas_call(
        body,
        name=name,
        out_shape=[jax.ShapeDtypeStruct(p.shape, p.dtype) for p in parts],
        in_specs=[ANY] * n,
        out_specs=[ANY] * n,
        scratch_shapes=[pltpu.SemaphoreType.DMA((3 * n,)), pltpu.SemaphoreType.DMA((3 * n,))],
    )(*parts)
    return list(outs)


def _chip_partials_group(parts, got, *, name):
    n = len(parts)
    rows = [p.shape[1] for p in parts]
    steps = 4 if all(r % (4 * BF16_ROWS) == 0 for r in rows) else (2 if all(r % (2 * BF16_ROWS) == 0 for r in rows) else 1)
    x, y, core = lax.axis_index("x"), lax.axis_index("y"), lax.axis_index("c")
    chips = [(x, y), (1 - x, y), (x, 1 - y), (1 - x, 1 - y)]
    ids = jnp.stack([_block_id(px, py, core) for px, py in chips]).astype(jnp.int32)

    def body(ids_ref, *refs):
        ins, outs = refs[:5 * n], refs[5 * n:]
        for t in range(n):
            p0, p1, p2, p3, g_ref = ins[5 * t:5 * t + 5]
            own_ref, out_ref = outs[2 * t], outs[2 * t + 1]
            own_ref[...] = p0[...].astype(F32) + g_ref[0].astype(F32)
            for j, p in enumerate((p1, p2, p3)):
                out_ref[j] = (p[...].astype(F32) + g_ref[j + 1].astype(F32)).astype(out_ref.dtype)

    in_specs, out_specs, out_shape, operands = [], [], [], []
    for p, g in zip(parts, got):
        _, r, c = p.shape
        tb = r // steps
        in_specs += [pl.BlockSpec((None, tb, c), lambda i, ids_ref, k=k: (ids_ref[k], i, 0)) for k in range(4)]
        in_specs.append(pl.BlockSpec((4, tb, c), lambda i, ids_ref: (0, i, 0)))
        out_specs += [pl.BlockSpec((tb, c), lambda i, ids_ref: (i, 0)), pl.BlockSpec((3, tb, c), lambda i, ids_ref: (0, i, 0))]
        out_shape += [jax.ShapeDtypeStruct((r, c), F32), jax.ShapeDtypeStruct((3, r, c), p.dtype)]
        operands += [p, p, p, p, g]
    outs = pl.pallas_call(
        body,
        name=name,
        grid_spec=pltpu.PrefetchScalarGridSpec(num_scalar_prefetch=1, grid=(steps,), in_specs=in_specs,
                                               out_specs=out_specs),
        out_shape=out_shape,
        compiler_params=_params(("parallel",)),
    )(ids, *operands)
    return [(outs[2 * t], outs[2 * t + 1]) for t in range(n)]


def _sum_blocks(first, blocks, *, name):
    n, r, c = blocks.shape
    tb = _row_tile(r, 512)

    def body(*refs):
        b_ref, o_ref = refs[-2], refs[-1]
        acc = refs[0][...] if first is not None else b_ref[0].astype(F32)
        for j in range(0 if first is not None else 1, n):
            acc = acc + b_ref[j].astype(F32)
        o_ref[...] = acc

    row_spec = pl.BlockSpec((tb, c), lambda i: (i, 0))
    return pl.pallas_call(
        body,
        name=name,
        grid=(r // tb,),
        in_specs=([row_spec] if first is not None else []) + [pl.BlockSpec((n, tb, c), lambda i: (0, i, 0))],
        out_specs=row_spec,
        out_shape=jax.ShapeDtypeStruct((r, c), F32),
        compiler_params=_params(("parallel",)),
    )(*([first] if first is not None else []), blocks)


HBM = pl.BlockSpec(memory_space=pltpu.HBM)
SEM = pl.BlockSpec(memory_space=pltpu.SEMAPHORE)
DATAFLOW = pltpu.SideEffectType.DATAFLOW_SIDE_EFFECTING


def _split_call(name, arrays, sems, n_new, body, after=None):
    n_a, n_s = len(arrays), len(sems)
    after = [] if after is None else (list(after) if isinstance(after, (list, tuple)) else [after])
    n_after = len(after)

    def kernel_body(*refs):
        new = refs[n_a + n_s + n_after:n_a + n_s + n_after + 2] if n_new else (None, None)
        body(refs[:n_a], refs[n_a:n_a + n_s], *new)
        refs[-1][...] = jnp.zeros_like(refs[-1])

    n_sem_out = 2 if n_new else 0
    out_shape = [pltpu.SemaphoreType.DMA((n_new,))] * n_sem_out
    out_shape += [pltpu.HBM(a.shape, a.dtype) for a in arrays]
    out_shape += [jax.ShapeDtypeStruct((SUBLANES, LANES), F32)]
    outs = pl.pallas_call(
        kernel_body,
        name=name,
        out_shape=out_shape,
        in_specs=[HBM] * n_a + [SEM] * n_s + [ANY] * n_after,
        out_specs=[SEM] * n_sem_out + [HBM] * n_a + [pl.BlockSpec(memory_space=pltpu.VMEM)],
        input_output_aliases={i: n_sem_out + i for i in range(n_a)},
        compiler_params=pltpu.CompilerParams(has_side_effects=DATAFLOW),
    )(*[pltpu.with_memory_space_constraint(a, pltpu.HBM) for a in arrays], *sems, *after)
    return tuple(outs[:n_sem_out]), list(outs[n_sem_out:n_sem_out + n_a]), outs[-1]


def _remote(src, dst, send, recv, k, to):
    return pltpu.make_async_remote_copy(src_ref=src, dst_ref=dst, send_sem=send.at[k], recv_sem=recv.at[k],
                                        device_id=to, device_id_type=MESH_ID)


def _land_block(land_ref, block, by_cols, width):
    if by_cols:
        return land_ref.at[:, pl.ds(pl.multiple_of(block * width, LANES), width)]
    return land_ref.at[block]


def _place_in_window(stacked, layer, *, name):
    _, r, c = stacked.shape
    tb = _row_tile(r, 512)
    me = _block_id(lax.axis_index("x"), lax.axis_index("y"), lax.axis_index("c")).astype(jnp.int32).reshape(1)

    def body(me_ref, s_ref, o_ref, token_ref):
        o_ref[...] = s_ref[...].astype(o_ref.dtype)
        token_ref[...] = jnp.zeros_like(token_ref)

    return pl.pallas_call(
        body,
        name=name,
        grid_spec=pltpu.PrefetchScalarGridSpec(
            num_scalar_prefetch=1, grid=(r // tb,),
            in_specs=[pl.BlockSpec((None, tb, c), lambda i, me_ref: (layer, i, 0))],
            out_specs=[pl.BlockSpec((tb, c), lambda i, me_ref: (i, me_ref[0])),
                       pl.BlockSpec((SUBLANES, LANES), lambda i, me_ref: (0, 0))]),
        out_shape=[jax.ShapeDtypeStruct((r, N_DEV * c), MXU_DTYPE), jax.ShapeDtypeStruct((SUBLANES, LANES), F32)],
        compiler_params=_params(("arbitrary",)),
    )(me, stacked)


def _gather_prepare(shards, by_cols, *, name):
    me = _block_id(lax.axis_index("x"), lax.axis_index("y"), lax.axis_index("c"))
    lands, early, widths = [], [], []
    for i, (s, cols) in enumerate(zip(shards, by_cols)):
        if cols:
            land, token = _place_in_window(*s, name=f"{name}_place{i}")
            lands.append(land)
            early.append(token)
            widths.append(s[0].shape[-1])
        else:
            lands.append(lax.dynamic_update_index_in_dim(lax.empty((N_DEV, *s.shape), s.dtype), s, me, 0))
            early.append(s)
            widths.append(s.shape[-1])
    return lands, widths, list(by_cols), early


def _gather_start(shards, *, name, after=None, by_cols=None, prepared=None):
    if prepared is None:
        prepared = _gather_prepare(shards, by_cols if by_cols is not None else [False] * len(shards), name=name)
    lands, widths, by_cols, _ = prepared
    n = len(lands)

    def body(refs, _, send, recv):
        x, y, core = lax.axis_index("x"), lax.axis_index("y"), lax.axis_index("c")
        targets = [(x, y, 1 - core), (1 - x, y, core), (x, 1 - y, core), (1 - x, 1 - y, core)]
        for t in range(n):
            mine = _land_block(refs[t], _block_id(x, y, core), by_cols[t], widths[t])
            for k, to in enumerate(targets):
                _remote(mine, mine, send, recv, 4 * t + k, to).start()

    sems, thru, token = _split_call(name, lands, [], 4 * n, body, after=after)
    return {"n": n, "sems1": sems, "arrays": thru, "token": token, "by_cols": by_cols, "widths": widths}


def _gather_pass_on(st, *, name, after):
    n = st["n"]

    def body(refs, sems, send, recv):
        send1, recv1 = sems
        x, y, core = lax.axis_index("x"), lax.axis_index("y"), lax.axis_index("c")
        chips = [(1 - x, y), (x, 1 - y), (1 - x, 1 - y)]
        for t in range(n):
            for j, (px, py) in enumerate(chips):
                block = _land_block(refs[t], _block_id(px, py, core), st["by_cols"][t], st["widths"][t])
                _remote(block, block, send1, recv1, 4 * t + 1 + j, (x, y, 1 - core)).wait_recv()
                _remote(block, block, send, recv, 3 * t + j, (x, y, 1 - core)).start()

    sems2, thru, token = _split_call(name, st["arrays"], list(st["sems1"]), 3 * n, body, after=after)
    return {**st, "sems2": sems2, "arrays": thru, "token": token}


def _gather_finish(st, *, name, after, then_start=None):
    n = st["n"]
    n2 = 0 if then_start is None else len(then_start[0])

    def body(refs, sems, new_send=None, new_recv=None):
        send1, recv1, send2, recv2 = sems
        x, y, core = lax.axis_index("x"), lax.axis_index("y"), lax.axis_index("c")
        sibling = (x, y, 1 - core)
        chips = [(1 - x, y), (x, 1 - y), (1 - x, 1 - y)]
        for t in range(n):
            place = lambda px, py, pc, t=t: _land_block(refs[t], _block_id(px, py, pc), st["by_cols"][t],
                                                        st["widths"][t])
            mine = place(x, y, core)
            for k in range(4):
                _remote(mine, mine, send1, recv1, 4 * t + k, sibling).wait_send()
            _remote(mine, place(x, y, 1 - core), send1, recv1, 4 * t, sibling).wait_recv()
            for j, (px, py) in enumerate(chips):
                _remote(place(px, py, core), place(px, py, core), send2, recv2, 3 * t + j, sibling).wait_send()
                _remote(mine, place(px, py, 1 - core), send2, recv2, 3 * t + j, sibling).wait_recv()
        if n2:
            _, widths, by_cols, _ = then_start
            targets = [sibling, (1 - x, y, core), (x, 1 - y, core), (1 - x, 1 - y, core)]
            for t in range(n2):
                mine = _land_block(refs[n + t], _block_id(x, y, core), by_cols[t], widths[t])
                for k, to in enumerate(targets):
                    _remote(mine, mine, new_send, new_recv, 4 * t + k, to).start()

    arrays = st["arrays"] + (list(then_start[0]) if n2 else [])
    sems, thru, token = _split_call(name, arrays, [*st["sems1"], *st["sems2"]], 4 * n2, body, after=after)
    if not n2:
        return thru
    _, widths, by_cols, _ = then_start
    return thru[:n], {"n": n2, "sems1": sems, "arrays": thru[n:], "token": token, "by_cols": by_cols, "widths": widths}


def _swap_start(sources, lands, plan, per_array, *, name):
    n = len(sources)

    def body(refs, _, send, recv):
        x, y, core = lax.axis_index("x"), lax.axis_index("y"), lax.axis_index("c")
        for t in range(n):
            for k in range(per_array):
                block, to = plan(x, y, core, k)
                _remote(refs[t].at[block], refs[n + t].at[k], send, recv, per_array * t + k, to).start()

    sems, thru, token = _split_call(name, sources + lands, [], per_array * n, body)
    return {"n": n, "sems": sems, "arrays": thru, "token": token, "plan": plan, "per_array": per_array}


def _swap_finish(states, *, name, after):
    many = isinstance(states, (list, tuple))
    states = list(states) if many else [states]

    def body(refs, sems, *_):
        x, y, core = lax.axis_index("x"), lax.axis_index("y"), lax.axis_index("c")
        first = 0
        for j, st in enumerate(states):
            n, plan, per_array = st["n"], st["plan"], st["per_array"]
            send, recv = sems[2 * j], sems[2 * j + 1]
            for t in range(n):
                for k in range(per_array):
                    block, to = plan(x, y, core, k)
                    cp = _remote(refs[first + t].at[block], refs[first + n + t].at[k], send, recv, per_array * t + k, to)
                    cp.wait_send()
                    cp.wait_recv()
            first += 2 * n

    arrays = [a for st in states for a in st["arrays"]]
    _, thru, _ = _split_call(name, arrays, [s for st in states for s in st["sems"]], 0, body, after=after)
    out, first = [], 0
    for st in states:
        n = st["n"]
        out.append((thru[first:first + n], thru[first + n:first + 2 * n]))
        first += 2 * n
    return out if many else out[0]


def _to_sibling(x, y, core, k):
    px, py = [(x, y), (1 - x, y), (x, 1 - y), (1 - x, 1 - y)][k]
    return _block_id(px, py, 1 - core), (x, y, 1 - core)


def _to_chip(x, y, core, k):
    px, py = [(1 - x, y), (x, 1 - y), (1 - x, 1 - y)][k]
    return k, (px, py, core)


PARAMS = {
    "sc_w_in": ((2, 1024, 3072), 2), "sc_conv_w": ((2, 3, 1024), 2), "sc_w_out": ((2, 1024, 1024), 1),
    "mla_w_dq": ((1, 1024, 384), 1), "mla_g_q": ((1, 384), None), "mla_w_uq": ((1, 384, 1536), 2),
    "mla_w_dkv": ((1, 1024, 320), 1), "mla_g_kv": ((1, 256), None), "mla_w_uk": ((1, 256, 8, 128), 1),
    "mla_w_uv": ((1, 256, 8, 128), 1), "mla_w_o": ((1, 1024, 1024), 1), "cf_w_pw1": ((1, 1024, 2048), 2),
    "cf_b_pw1": ((1, 2048), 1), "cf_dw_w": ((1, 31, 1024), 2), "cf_dw_b": ((1, 1024), 1), "cf_norm_g": ((1, 1024), 1),
    "cf_norm_b": ((1, 1024), 1), "cf_w_pw2": ((1, 1024, 1024), 1), "cf_b_pw2": ((1, 1024), 1),
    "ff_w1": ((4, 1024, 4096), 2), "ff_w2": ((4, 4096, 1024), 1), "ln_mix_g": ((4, 1024), None),
    "ln_mix_b": ((4, 1024), None), "ln_ff_g": ((4, 1024), None), "ln_ff_b": ((4, 1024), None),
}
NAMES = list(PARAMS)
BIG = ["sc_w_in", "sc_w_out", "mla_w_dq", "mla_w_uq", "mla_w_dkv", "mla_w_uk", "mla_w_uv", "mla_w_o", "cf_w_pw1",
       "cf_w_pw2", "ff_w1", "ff_w2"]
SMALL = [n for n in NAMES if n not in BIG]
SMALL_SHARDED = [n for n in SMALL if PARAMS[n][1] is not None]
SMALL_WIDTH = LANES
DEV_BLOCKED = ("sc_w_in", "cf_w_pw1", "ff_w1")
MIXER_BIG = {0: ["sc_w_in", "sc_w_out"], 1: ["mla_w_dq", "mla_w_uq", "mla_w_dkv", "mla_w_uk", "mla_w_uv", "mla_w_o"],
             2: ["cf_w_pw1", "cf_w_pw2"]}


def _layer_big(i):
    return [(n, i // N_MIXERS) for n in MIXER_BIG[i % N_MIXERS]] + [("ff_w1", i), ("ff_w2", i)]


def _shard_2d(name, a):
    return a.reshape(a.shape[0], -1)


def _shard_for_gather(name, stacked, layer):
    if name in DEV_BLOCKED:
        return stacked, layer
    return _shard_2d(name, stacked[layer]).astype(MXU_DTYPE)


def _as_operand(name, g):
    if name in DEV_BLOCKED:
        return g
    if name == "mla_w_uq":
        by_head = g.transpose(1, 0, 2)
        pieces = [by_head[:, :, :QK_NOPE], by_head[:, :, QK_NOPE:]]
        return jnp.concatenate([p.reshape(g.shape[1], -1) for p in pieces], axis=1)
    return g.reshape(-1, g.shape[2])


def _as_parts(name, grad):
    if name in DEV_BLOCKED:
        return grad
    if name == "mla_w_uq":
        rows = grad.shape[0]
        pieces = [grad[:, :Q_NOPE_COLS].reshape(rows, MLA_HEADS, QK_NOPE),
                  grad[:, Q_NOPE_COLS:].reshape(rows, MLA_HEADS, QK_ROPE)]
        return jnp.concatenate(pieces, axis=2).transpose(1, 0, 2)
    return grad.reshape(N_DEV, -1, grad.shape[1])


def _shard_shape(name):
    shape, ax = PARAMS[name]
    if ax is None:
        return shape
    return tuple(s // N_DEV if i == ax else s for i, s in enumerate(shape))


def _rows_of(shape, width, align):
    n = 1
    for s in shape:
        n *= s
    rows = -(-n // width)
    return -(-rows // align) * align


def _pack_rows(arrays, width, align, dtype, lead=0):
    segs = []
    for a in arrays:
        batch = a.shape[:lead]
        flat = a.astype(dtype).reshape(*batch, -1)
        rows = _rows_of(a.shape[lead:], width, align)
        flat = jnp.pad(flat, [(0, 0)] * lead + [(0, rows * width - flat.shape[-1])])
        segs.append(flat.reshape(*batch, rows, width))
    return jnp.concatenate(segs, axis=lead)


def _unpack_rows(buf, shapes, width, align):
    out, off = [], 0
    lead = buf.shape[:-2]
    for shape in shapes:
        n = 1
        for s in shape:
            n *= s
        rows = _rows_of(shape, width, align)
        seg = buf[..., off:off + rows, :].reshape(*lead, rows * width)[..., :n]
        out.append(seg.reshape(*lead, *shape))
        off += rows
    return out


def _merge_shards(stacked, ax):
    moved = jnp.moveaxis(stacked, 0, ax)
    shape = moved.shape
    return moved.reshape(*shape[:ax], shape[ax] * shape[ax + 1], *shape[ax + 2:])


def _split_shards(full, ax):
    shape = full.shape
    split = full.reshape(*shape[:ax], N_DEV, shape[ax] // N_DEV, *shape[ax + 1:])
    return jnp.moveaxis(split, ax, 0)


def _adam_math(wv, gv, mv, vv):
    m_new = ADAM_B1 * mv + (1.0 - ADAM_B1) * gv
    v_new = ADAM_B2 * vv + (1.0 - ADAM_B2) * (gv * gv)
    m_hat = m_new / (1.0 - ADAM_B1 ** ADAM_STEP)
    v_hat = v_new / (1.0 - ADAM_B2 ** ADAM_STEP)
    delta = -ADAM_LR * (m_hat / (jnp.sqrt(v_hat) + ADAM_EPS) + ADAM_WD * wv)
    return delta, m_new, v_new


def _adamw(w, g, m, v, *, name):
    shape = w.shape
    c = shape[-1]

    def fn(rows, bc):
        return _adam_math(*rows), ()

    flat = [a.reshape(-1, c) for a in (w, g, m, v)]
    outs = _rowwise(fn, flat, [], [(c, F32)] * 3, [], name=name, tb_cap=512)
    return tuple(o.reshape(shape) for o in outs)


def _adamw_reduced(w, m, v, layer_grads, *, name, after=()):
    n_l, r, c = w.shape
    tb = _row_tile(r, 256 if c > 512 else 512)
    nb = r // tb

    def body(*refs):
        w_ref, m_ref, v_ref = refs[:3]
        g_refs = refs[3:3 + 2 * n_l]
        g_out, d_out, m_out, v_out = refs[3 + 2 * n_l + len(after):]
        layer = pl.program_id(0)
        for l in range(n_l):
            @pl.when(layer == l)
            def _(l=l):
                own, arr = g_refs[2 * l], g_refs[2 * l + 1]
                g = own[...] + arr[0].astype(F32)
                g = g + arr[1].astype(F32)
                g = g + arr[2].astype(F32)
                delta, m_new, v_new = _adam_math(w_ref[...], g, m_ref[...], v_ref[...])
                g_out[...] = g
                d_out[...] = delta
                m_out[...] = m_new
                v_out[...] = v_new

    stacked = pl.BlockSpec((tb, c), lambda l, i: (l * nb + i, 0))
    in_specs = [stacked] * 3
    operands = [a.reshape(n_l * r, c) for a in (w, m, v)]
    for l, (own, arr) in enumerate(layer_grads):
        in_specs.append(pl.BlockSpec((tb, c), lambda ll, i, l=l: (jnp.where(ll == l, i, 0), 0)))
        in_specs.append(pl.BlockSpec((3, tb, c), lambda ll, i, l=l: (0, jnp.where(ll == l, i, 0), 0)))
        operands += [own, arr]
    in_specs += [ANY] * len(after)
    operands += list(after)
    outs = pl.pallas_call(
        body,
        name=name,
        grid=(n_l, nb),
        in_specs=in_specs,
        out_specs=[stacked] * 4,
        out_shape=[jax.ShapeDtypeStruct((n_l * r, c), F32)] * 4,
        compiler_params=_params(("arbitrary", "arbitrary")),
    )(*operands)
    return tuple(o.reshape(n_l, r, c) for o in outs)


def _adamw_reduced_group(ws, ms, vs, grads, *, name, after=()):
    n = len(ws)

    def body(*refs):
        ins, outs = refs[:5 * n], refs[5 * n + len(after):]
        for i in range(n):
            w_ref, m_ref, v_ref, own, arr = ins[5 * i:5 * i + 5]
            g = own[...] + arr[0].astype(F32)
            g = g + arr[1].astype(F32)
            g = g + arr[2].astype(F32)
            delta, m_new, v_new = _adam_math(w_ref[...], g, m_ref[...], v_ref[...])
            for o_ref, val in zip(outs[4 * i:4 * i + 4], (g, delta, m_new, v_new)):
                o_ref[...] = val

    whole = lambda a: pl.BlockSpec(a.shape, lambda i, nd=a.ndim: (0,) * nd)
    operands, in_specs, out_specs, out_shape = [], [], [], []
    for w, m, v, (own, arr) in zip(ws, ms, vs, grads):
        operands += [w, m, v, own, arr]
        in_specs += [whole(a) for a in (w, m, v, own, arr)]
        out_specs += [whole(w)] * 4
        out_shape += [jax.ShapeDtypeStruct(w.shape, F32)] * 4
    outs = pl.pallas_call(
        body,
        name=name,
        grid=(1,),
        in_specs=in_specs + [ANY] * len(after),
        out_specs=out_specs,
        out_shape=out_shape,
        compiler_params=_params(("arbitrary",)),
    )(*operands, *after)
    return [tuple(outs[4 * i:4 * i + 4]) for i in range(n)]


def _train_step(x, target, weights, m_state, v_state):
    me = _block_id(lax.axis_index("x"), lax.axis_index("y"), lax.axis_index("c"))

    groups = [(kind, i) for i in range(DEPTH) for kind in ("mix", "ff")]
    forward = [groups[0], ("ff_up", 0), ("ff_down", 0)] + groups[2:]

    def group_names(g):
        kind, i = g
        if kind == "mix":
            return [(n, i // N_MIXERS) for n in MIXER_BIG[i % N_MIXERS]]
        return [(n, i) for n in {"ff": ["ff_w1", "ff_w2"], "ff_up": ["ff_w1"], "ff_down": ["ff_w2"]}[kind]]

    def shards_of(g):
        return [_shard_for_gather(n, weights[n], j) for n, j in group_names(g)]

    whole = lambda g: [n in DEV_BLOCKED for n, _ in group_names(g)]

    small_shapes = [_shard_shape(n) for n in SMALL_SHARDED]
    gather = {0: _gather_start([_pack_rows([weights[n] for n in SMALL_SHARDED], SMALL_WIDTH, SUBLANES, F32)]
                               + shards_of(groups[0]), name="gather_g0_start", by_cols=[False] + whole(groups[0]))}
    cur, cur_m = x[0], x[0]
    prepared = {k: _gather_prepare(shards_of(forward[k]), whole(forward[k]), name=f"gather_g{k}_start")
                for k in range(1, len(forward))}
    early = [a for k in prepared for a in prepared[k][3]]
    gather[0] = _gather_pass_on(gather[0], name="gather_g0_pass", after=early)
    first = _gather_finish(gather[0], name="gather_g0_finish", after=gather[0]["token"])
    small_all, first_matrices = first[0], first[1:]
    small_full = {n: weights[n] for n in SMALL if PARAMS[n][1] is None}
    for n, stacked in zip(SMALL_SHARDED, _unpack_rows(small_all, small_shapes, SMALL_WIDTH, SUBLANES)):
        small_full[n] = _merge_shards(stacked, PARAMS[n][1])

    def small_weights(i):
        wl = {}
        for n in SMALL:
            if n.startswith("ln_"):
                wl[n] = small_full[n][i:i + 1]
            elif n.startswith(("sc_", "mla_", "cf_")[i % N_MIXERS]):
                v = small_full[n][i // N_MIXERS]
                wl[n] = v if v.ndim == 2 else v.reshape(1, -1)
        return wl

    layers = [small_weights(i) for i in range(DEPTH)]

    def install(g, gathered):
        for (n, _), a in zip(group_names(g), gathered):
            layers[g[1]][n] = _as_operand(n, a)

    install(forward[0], first_matrices)
    gather[1] = _gather_start(None, name="gather_g1_start", after=gather[0]["token"], prepared=prepared[1])
    gather[2] = _gather_start(None, name="gather_g2_start", after=gather[1]["token"], prepared=prepared[2])
    saved = {}
    hidden = None
    for k, g in enumerate(forward):
        kind, i = g
        tokens = [gather[1]["token"]] if k == 0 else []
        if k + 2 < len(forward):
            tokens.append(gather[k + 2]["token"])

        def pass_on(y, k=k):
            if k + 1 == len(forward):
                return None
            gather[k + 1] = _gather_pass_on(gather[k + 1], name=f"gather_g{k + 1}_pass", after=y)
            return gather[k + 1]["token"]

        if k + 1 == len(forward):
            loss_local, dy, saved[g] = _ff_fwd(i, cur, cur_m, layers[i], after=tokens, loss_target=target[0])
            break
        if kind == "ff_up":
            hidden = _mlp_up(cur_m, layers[i]["ff_w1"], f"l{i}_ff", after=tokens)
            behind, done = _hooked(hidden, pass_on), hidden
        elif kind == "ff_down":
            cur, cur_m, saved[("ff", i)] = _ff_fwd(i, cur, cur_m, layers[i], hook=pass_on, after=[*tokens, *behind],
                                                   act=hidden)
            done = cur
        else:
            phase = _mix_fwd if kind == "mix" else _ff_fwd
            cur, cur_m, saved[g] = phase(i, cur, cur_m, layers[i], hook=pass_on, after=tokens)
            done = cur
        if k + 3 < len(forward):
            gathered, gather[k + 3] = _gather_finish(gather[k + 1], name=f"gather_g{k + 1}_finish", after=done,
                                                     then_start=prepared[k + 3])
        else:
            gathered = _gather_finish(gather[k + 1], name=f"gather_g{k + 1}_finish", after=done)
        install(forward[k + 1], gathered)

    reduced = {n: [None] * PARAMS[n][0][0] for n in BIG}
    small_grads = {n: [None] * PARAMS[n][0][0] for n in SMALL}
    reduce = {}

    def to_chips(p, after):
        st = reduce[p]
        parts, got = _swap_finish(st["sibling"], name=f"reduce_p{p}_sibling_finish", after=after)
        sums = _chip_partials_group(parts, got, name=f"reduce_p{p}_chipsum")
        st["own"] = [own for own, _ in sums]
        others = [o for _, o in sums]
        st["chips"] = _swap_start(others, [lax.empty(o.shape, o.dtype) for o in others], _to_chip, 3,
                                  name=f"reduce_p{p}_chips_start")
        return st["chips"]["token"]

    def arrived(phases, after):
        done = _swap_finish([reduce[p]["chips"] for p in phases], name=f"reduce_p{phases[0]}_chips_finish", after=after)
        for p, (_, got) in zip(phases, done):
            for (n, j), own, arr in zip(reduce[p]["names"], reduce[p]["own"], got):
                reduced[n][j] = (own, arr)

    terms = [(1.0, dy)]
    back = list(reversed(groups))
    for p, g in enumerate(back):
        kind, i = g

        def after_first_kernel(dz, p=p):
            return to_chips(p - 1, dz) if p >= 1 else None

        phase = _mix_bwd if kind == "mix" else _ff_bwd
        tokens = [reduce[p - 1]["sibling"]["token"]] if p >= 1 else []
        terms, grads = phase(i, terms, saved[g], layers[i], hook=after_first_kernel, after=tokens)
        for n in SMALL:
            if n in grads:
                small_grads[n][i if n.startswith("ln_") else i // N_MIXERS] = grads[n]
        names = group_names(g)
        parts = [_as_parts(n, grads[n]) for n, _ in names]
        reduce[p] = {"names": names, "sibling": _swap_start(
            parts, [lax.empty((4, *a.shape[1:]), a.dtype) for a in parts], _to_sibling, 4,
            name=f"reduce_p{p}_sibling_start")}
    (_, grad_x), = terms
    last = len(back) - 1
    small_mine = _pack_rows([jnp.stack(small_grads[n]).reshape(PARAMS[n][0]) for n in SMALL] + [loss_local.reshape(1)],
                            SMALL_WIDTH, SUBLANES, F32)
    small_gather = _gather_start([small_mine], name="gather_small_grads_start")
    tail = to_chips(last, small_gather["token"])
    arrived(list(range(last)), tail)

    grad, delta, new_m, new_v = {}, {}, {}, {}

    def adamw_matrix(n, after):
        shape = weights[n].shape
        view = lambda a: a.reshape(shape[0], shape[1], -1)
        outs = _adamw_reduced(view(weights[n]), view(m_state[n]), view(v_state[n]), reduced[n], name=f"adamw_{n}",
                              after=after)
        grad[n], delta[n], new_m[n], new_v[n] = [o.reshape(shape) for o in outs]
        return outs[0]

    done = tail
    single = [n for n in BIG if PARAMS[n][0][0] == 1 and n != "cf_w_pw1"]
    flat = lambda a: a.reshape(a.shape[1], -1)
    outs = _adamw_reduced_group([flat(weights[n]) for n in single], [flat(m_state[n]) for n in single],
                                [flat(v_state[n]) for n in single], [reduced[n][0] for n in single],
                                name="adamw_single_layer", after=[done])
    for n, quad in zip(single, outs):
        grad[n], delta[n], new_m[n], new_v[n] = [o.reshape(weights[n].shape) for o in quad]
    done = outs[-1][0]
    for n in BIG:
        if n not in MIXER_BIG[0] and n not in single:
            done = adamw_matrix(n, [done])
    small_gather = _gather_pass_on(small_gather, name="gather_small_grads_pass", after=done)
    small_all = _gather_finish(small_gather, name="gather_small_grads_finish", after=small_gather["token"])[0]
    small_sum = _sum_blocks(None, small_all, name="sum_small_grads")
    *small_full, loss_all = _unpack_rows(small_sum, [PARAMS[n][0] for n in SMALL] + [(1,)], SMALL_WIDTH, SUBLANES)
    loss = loss_all[0]
    for n, g in zip(SMALL, small_full):
        ax = PARAMS[n][1]
        grad[n] = g if ax is None else lax.dynamic_index_in_dim(_split_shards(g, ax), me, axis=0, keepdims=False)
    small_shards = [_shard_shape(n) for n in SMALL]
    packed = [_pack_rows([src[n] for n in SMALL], SMALL_WIDTH, SUBLANES, F32) for src in (weights, grad, m_state, v_state)]
    small_out = _adamw(*packed, name="adamw_small")
    for outs, dst in zip(small_out, (delta, new_m, new_v)):
        for n, a in zip(SMALL, _unpack_rows(outs, small_shards, SMALL_WIDTH, SUBLANES)):
            dst[n] = a
    arrived([last], small_sum)
    for n in MIXER_BIG[0]:
        adamw_matrix(n, [])
    return (loss, grad_x[None], *[grad[n] for n in NAMES], *[delta[n] for n in NAMES],
            *[new_m[n] for n in NAMES], *[new_v[n] for n in NAMES])


def kernel(x, sc_w_in, sc_conv_w, sc_w_out, mla_w_dq, mla_g_q, mla_w_uq, mla_w_dkv, mla_g_kv, mla_w_uk, mla_w_uv, mla_w_o, cf_w_pw1, cf_b_pw1, cf_dw_w, cf_dw_b, cf_norm_g, cf_norm_b, cf_w_pw2, cf_b_pw2, ff_w1, ff_w2, ln_mix_g, ln_mix_b, ln_ff_g, ln_ff_b, loss_target, m_sc_w_in, m_sc_conv_w, m_sc_w_out, m_mla_w_dq, m_mla_g_q, m_mla_w_uq, m_mla_w_dkv, m_mla_g_kv, m_mla_w_uk, m_mla_w_uv, m_mla_w_o, m_cf_w_pw1, m_cf_b_pw1, m_cf_dw_w, m_cf_dw_b, m_cf_norm_g, m_cf_norm_b, m_cf_w_pw2, m_cf_b_pw2, m_ff_w1, m_ff_w2, m_ln_mix_g, m_ln_mix_b, m_ln_ff_g, m_ln_ff_b, v_sc_w_in, v_sc_conv_w, v_sc_w_out, v_mla_w_dq, v_mla_g_q, v_mla_w_uq, v_mla_w_dkv, v_mla_g_kv, v_mla_w_uk, v_mla_w_uv, v_mla_w_o, v_cf_w_pw1, v_cf_b_pw1, v_cf_dw_w, v_cf_dw_b, v_cf_norm_g, v_cf_norm_b, v_cf_w_pw2, v_cf_b_pw2, v_ff_w1, v_ff_w2, v_ln_mix_g, v_ln_mix_b, v_ln_ff_g, v_ln_ff_b):
    w_list = (sc_w_in, sc_conv_w, sc_w_out, mla_w_dq, mla_g_q, mla_w_uq, mla_w_dkv, mla_g_kv, mla_w_uk, mla_w_uv, mla_w_o,
              cf_w_pw1, cf_b_pw1, cf_dw_w, cf_dw_b, cf_norm_g, cf_norm_b, cf_w_pw2, cf_b_pw2, ff_w1, ff_w2, ln_mix_g,
              ln_mix_b, ln_ff_g, ln_ff_b)
    m_list = (m_sc_w_in, m_sc_conv_w, m_sc_w_out, m_mla_w_dq, m_mla_g_q, m_mla_w_uq, m_mla_w_dkv, m_mla_g_kv, m_mla_w_uk,
              m_mla_w_uv, m_mla_w_o, m_cf_w_pw1, m_cf_b_pw1, m_cf_dw_w, m_cf_dw_b, m_cf_norm_g, m_cf_norm_b, m_cf_w_pw2,
              m_cf_b_pw2, m_ff_w1, m_ff_w2, m_ln_mix_g, m_ln_mix_b, m_ln_ff_g, m_ln_ff_b)
    v_list = (v_sc_w_in, v_sc_conv_w, v_sc_w_out, v_mla_w_dq, v_mla_g_q, v_mla_w_uq, v_mla_w_dkv, v_mla_g_kv, v_mla_w_uk,
              v_mla_w_uv, v_mla_w_o, v_cf_w_pw1, v_cf_b_pw1, v_cf_dw_w, v_cf_dw_b, v_cf_norm_g, v_cf_norm_b, v_cf_w_pw2,
              v_cf_b_pw2, v_ff_w1, v_ff_w2, v_ln_mix_g, v_ln_mix_b, v_ln_ff_g, v_ln_ff_b)
    return _train_step(x, loss_target, dict(zip(NAMES, w_list)), dict(zip(NAMES, m_list)), dict(zip(NAMES, v_list)))
```
